```python
import jax, jax.numpy as jnp
from jax import lax
import numpy as np

D_MODEL = 1024
BATCH = 32
SEQ = 2048
DEPTH = 1

D_PLE = 256
D_MIX = 2 * D_MODEL
GM_WIDTH = D_MIX // 2
GM_HEADS = 8
GM_HEAD_DIM = GM_WIDTH // GM_HEADS
GM_CHUNK = 128
SSD_WIDTH = D_MIX - GM_WIDTH
SSD_HEAD_DIM = 64
SSD_HEADS = SSD_WIDTH // SSD_HEAD_DIM
SSD_GROUPS = 2
SSD_HEADS_PER_GROUP = SSD_HEADS // SSD_GROUPS
SSD_STATE = 128
SSD_CONV = 4
SSD_CHUNK = 128
D_FF = 4 * D_MODEL
EPS = 1e-6
GM_COLS = 2 * GM_WIDTH
SSD_CONV_CH = SSD_WIDTH + 2 * SSD_GROUPS * SSD_STATE
SSD_COLS = SSD_WIDTH + SSD_CONV_CH + SSD_HEADS
D_IN_PROJ = GM_COLS + SSD_COLS

kernel_name = "hybrid_gmlp_ssd_parallel_heads"


def rmsnorm(x, g):
    xf = x.astype(jnp.float32)
    y = xf * lax.rsqrt(jnp.mean(xf * xf, axis=-1, keepdims=True) + EPS)
    return (y * g.astype(jnp.float32)).astype(x.dtype)


def gmlp_chunk_mixer(uv, v_norm_g, ws, bs, out_norm_g):
    b, s, _ = uv.shape
    nc = s // GM_CHUNK
    uv = jax.nn.gelu(uv, approximate=False)
    u = uv[..., :GM_WIDTH].reshape(b, nc, GM_CHUNK, GM_HEADS, GM_HEAD_DIM)
    v = uv[..., GM_WIDTH:].reshape(b, s, GM_HEADS, GM_HEAD_DIM)
    v = rmsnorm(v, v_norm_g.reshape(GM_HEADS, GM_HEAD_DIM))
    v = v.reshape(b, nc, GM_CHUNK, GM_HEADS, GM_HEAD_DIM)
    causal = jnp.tril(jnp.ones((GM_CHUNK, GM_CHUNK), dtype=bool))
    w = jnp.where(causal[None], ws, jnp.zeros_like(ws))
    mixed = jnp.einsum('hts,bcshp->bcthp', w, v) + bs.T[None, None, :, :, None]
    y = (u * mixed).reshape(b, s, GM_WIDTH)
    return rmsnorm(y, out_norm_g)


def causal_depthwise_conv(x, w, bias):
    out = lax.conv_general_dilated(
        x, w[:, None, :], window_strides=(1,), padding=[(SSD_CONV - 1, 0)],
        dimension_numbers=('NWC', 'WIO', 'NWC'), feature_group_count=x.shape[-1])
    return out + bias


def ssd_chunked_scan(x, dt, A, B, C):
    b, s, G, R, P = x.shape
    N = B.shape[-1]
    nc, L = s // SSD_CHUNK, SSD_CHUNK
    x = x.reshape(b, nc, L, G, R, P)
    dt = dt.reshape(b, nc, L, G, R)
    B = B.reshape(b, nc, L, G, N)
    C = C.reshape(b, nc, L, G, N)
    cs = jnp.cumsum(dt * A, axis=2)
    x_dt = x * dt[..., None].astype(x.dtype)
    seg = cs[:, :, :, None] - cs[:, :, None, :]
    causal = jnp.tril(jnp.ones((L, L), dtype=bool))[:, :, None, None]
    lmat = jnp.exp(jnp.where(causal, seg, -jnp.inf)).astype(x.dtype)
    cb = jnp.einsum('bclgn,bcsgn->bclsg', C, B)
    y_diag = jnp.einsum('bclsgr,bcsgrp->bclgrp', cb[..., None] * lmat, x_dt)
    decay_states = jnp.exp(cs[:, :, -1:] - cs).astype(x.dtype)
    states = jnp.einsum('bclgn,bclgr,bclgrp->bcgrpn', B, decay_states, x_dt)
    chunk_decay = jnp.exp(cs[:, :, -1]).astype(x.dtype)

    def step(carry, inp):
        st, dec = inp
        return carry * dec[..., None, None] + st, carry

    init = jnp.zeros((b, G, R, P, N), dtype=states.dtype)
    _, prev = lax.scan(step, init, (jnp.moveaxis(states, 1, 0), jnp.moveaxis(chunk_decay, 1, 0)))
    prev = jnp.moveaxis(prev, 0, 1)
    y_off = jnp.einsum('bclgn,bcgrpn,bclgr->bclgrp', C, prev, jnp.exp(cs).astype(x.dtype))
    return (y_diag + y_off).reshape(b, s, G, R, P)


def ssd_mixer(zxbcdt, conv_w, conv_b, dt_bias, a_log, d_skip, norm_g):
    b, s, _ = zxbcdt.shape
    z = zxbcdt[..., :SSD_WIDTH]
    xbc = zxbcdt[..., SSD_WIDTH:SSD_WIDTH + SSD_CONV_CH]
    dt_raw = zxbcdt[..., SSD_WIDTH + SSD_CONV_CH:]
    xbc = jax.nn.silu(causal_depthwise_conv(xbc, conv_w, conv_b))
    xs = xbc[..., :SSD_WIDTH].reshape(b, s, SSD_GROUPS, SSD_HEADS_PER_GROUP, SSD_HEAD_DIM)
    Bm = xbc[..., SSD_WIDTH:SSD_WIDTH + SSD_GROUPS * SSD_STATE].reshape(b, s, SSD_GROUPS, SSD_STATE)
    Cm = xbc[..., SSD_WIDTH + SSD_GROUPS * SSD_STATE:].reshape(b, s, SSD_GROUPS, SSD_STATE)
    dt = jax.nn.softplus(dt_raw.astype(jnp.float32) + dt_bias.astype(jnp.float32))
    dt = dt.reshape(b, s, SSD_GROUPS, SSD_HEADS_PER_GROUP)
    A = -jnp.exp(a_log.astype(jnp.float32)).reshape(SSD_GROUPS, SSD_HEADS_PER_GROUP)
    y = ssd_chunked_scan(xs, dt, A, Bm, Cm)
    y = y + d_skip.reshape(SSD_GROUPS, SSD_HEADS_PER_GROUP)[:, :, None] * xs
    y = y.reshape(b, s, SSD_WIDTH) * jax.nn.silu(z)
    gw = SSD_WIDTH // SSD_GROUPS
    y = rmsnorm(y.reshape(b, s, SSD_GROUPS, gw), norm_g.reshape(SSD_GROUPS, gw))
    return y.reshape(b, s, SSD_WIDTH)


def _fwd_setup_inputs(seed: int = 0) -> dict:
    key = jax.random.key(seed)
    ks = jax.random.split(key, 24)
    f32 = jnp.float32
    nrm = lambda k, shape, scale: jax.random.normal(k, shape, f32) * scale
    gain = lambda k, shape: 1.0 + 0.05 * jax.random.normal(k, shape, f32)
    dt_init = jnp.exp(jax.random.uniform(ks[14], (DEPTH, SSD_HEADS), f32,
                                         np.log(1e-3).astype(np.float32), np.log(1e-1).astype(np.float32)))
    return {
        "x": jax.random.normal(ks[0], (BATCH, SEQ, D_MODEL), f32),
        "p": jax.random.normal(ks[1], (DEPTH, BATCH, SEQ, D_PLE), f32),
        "norm_mix_g": gain(ks[2], (DEPTH, D_MODEL)),
        "w_in": nrm(ks[3], (DEPTH, D_MODEL, D_IN_PROJ), D_MODEL ** -0.5),
        "gm_v_norm_g": gain(ks[4], (DEPTH, GM_WIDTH)),
        "gm_ws": nrm(ks[5], (DEPTH, GM_HEADS, GM_CHUNK, GM_CHUNK), GM_CHUNK ** -0.5),
        "gm_bs": gain(ks[6], (DEPTH, GM_HEADS, GM_CHUNK)),
        "gm_out_norm_g": gain(ks[7], (DEPTH, GM_WIDTH)),
        "ssd_conv_w": nrm(ks[8], (DEPTH, SSD_CONV, SSD_CONV_CH), SSD_CONV ** -0.5),
        "ssd_conv_b": nrm(ks[9], (DEPTH, SSD_CONV_CH), 0.02),
        "ssd_dt_bias": dt_init + jnp.log(-jnp.expm1(-dt_init)),
        "ssd_a_log": jnp.log(jax.random.uniform(ks[10], (DEPTH, SSD_HEADS), f32, 1.0, 16.0)),
        "ssd_d": gain(ks[11], (DEPTH, SSD_HEADS)),
        "ssd_norm_g": gain(ks[12], (DEPTH, SSD_WIDTH)),
        "w_out": nrm(ks[13], (DEPTH, D_MIX, D_MODEL), D_MIX ** -0.5),
        "norm_mlp_g": gain(ks[15], (DEPTH, D_MODEL)),
        "w_ff1": nrm(ks[16], (DEPTH, D_MODEL, D_FF), D_MODEL ** -0.5),
        "w_ff2": nrm(ks[17], (DEPTH, D_FF, D_MODEL), D_FF ** -0.5),
        "ple_norm_g": gain(ks[18], (DEPTH, D_MODEL)),
        "w_ple_gate": nrm(ks[19], (DEPTH, D_MODEL, D_MODEL), D_MODEL ** -0.5),
        "w_ple_proj": nrm(ks[20], (DEPTH, D_PLE, D_MODEL), D_PLE ** -0.5),
        "final_norm_g": gain(ks[21], (D_MODEL,)),
    }


def _fwd_reference(x, p, norm_mix_g, w_in, gm_v_norm_g, gm_ws, gm_bs, gm_out_norm_g,
              ssd_conv_w, ssd_conv_b, ssd_dt_bias, ssd_a_log, ssd_d, ssd_norm_g,
              w_out, norm_mlp_g, w_ff1, w_ff2, ple_norm_g, w_ple_gate, w_ple_proj,
              final_norm_g):
    h = x
    for i in range(DEPTH):
        proj = rmsnorm(h, norm_mix_g[i]) @ w_in[i]
        y_a = gmlp_chunk_mixer(proj[..., :GM_COLS], gm_v_norm_g[i], gm_ws[i], gm_bs[i],
                               gm_out_norm_g[i])
        y_b = ssd_mixer(proj[..., GM_COLS:], ssd_conv_w[i], ssd_conv_b[i], ssd_dt_bias[i],
                        ssd_a_log[i], ssd_d[i], ssd_norm_g[i])
        h = h + jnp.concatenate([y_a, y_b], axis=-1) @ w_out[i]
        hid = jax.nn.relu(rmsnorm(h, norm_mlp_g[i]) @ w_ff1[i])
        h = h + (hid * hid) @ w_ff2[i]
        gate = jax.nn.sigmoid(rmsnorm(h, ple_norm_g[i]) @ w_ple_gate[i])
        h = h + gate * (p[i] @ w_ple_proj[i])
    return rmsnorm(h, final_norm_g)


import jax as _jax
import jax.numpy as _jnp

TWIN_FORMAT = 'train_step'
FWD_PARAMS = ['x', 'p', 'norm_mix_g', 'w_in', 'gm_v_norm_g', 'gm_ws', 'gm_bs', 'gm_out_norm_g', 'ssd_conv_w', 'ssd_conv_b', 'ssd_dt_bias', 'ssd_a_log', 'ssd_d', 'ssd_norm_g', 'w_out', 'norm_mlp_g', 'w_ff1', 'w_ff2', 'ple_norm_g', 'w_ple_gate', 'w_ple_proj', 'final_norm_g']
TWIN_WEIGHTS = ['norm_mix_g', 'w_in', 'gm_v_norm_g', 'gm_ws', 'gm_bs', 'gm_out_norm_g', 'ssd_conv_w', 'ssd_conv_b', 'ssd_dt_bias', 'ssd_a_log', 'ssd_d', 'ssd_norm_g', 'w_out', 'norm_mlp_g', 'w_ff1', 'w_ff2', 'ple_norm_g', 'w_ple_gate', 'w_ple_proj', 'final_norm_g']
TWIN_DIFF_INPUT = 'x'
TWIN_INPUTS = ['x', 'p', 'norm_mix_g', 'w_in', 'gm_v_norm_g', 'gm_ws', 'gm_bs', 'gm_out_norm_g', 'ssd_conv_w', 'ssd_conv_b', 'ssd_dt_bias', 'ssd_a_log', 'ssd_d', 'ssd_norm_g', 'w_out', 'norm_mlp_g', 'w_ff1', 'w_ff2', 'ple_norm_g', 'w_ple_gate', 'w_ple_proj', 'final_norm_g', 'loss_target', 'm_norm_mix_g', 'm_w_in', 'm_gm_v_norm_g', 'm_gm_ws', 'm_gm_bs', 'm_gm_out_norm_g', 'm_ssd_conv_w', 'm_ssd_conv_b', 'm_ssd_dt_bias', 'm_ssd_a_log', 'm_ssd_d', 'm_ssd_norm_g', 'm_w_out', 'm_norm_mlp_g', 'm_w_ff1', 'm_w_ff2', 'm_ple_norm_g', 'm_w_ple_gate', 'm_w_ple_proj', 'm_final_norm_g', 'v_norm_mix_g', 'v_w_in', 'v_gm_v_norm_g', 'v_gm_ws', 'v_gm_bs', 'v_gm_out_norm_g', 'v_ssd_conv_w', 'v_ssd_conv_b', 'v_ssd_dt_bias', 'v_ssd_a_log', 'v_ssd_d', 'v_ssd_norm_g', 'v_w_out', 'v_norm_mlp_g', 'v_w_ff1', 'v_w_ff2', 'v_ple_norm_g', 'v_w_ple_gate', 'v_w_ple_proj', 'v_final_norm_g']
TWIN_OUTPUTS = ['loss', 'grad_x', 'grad_norm_mix_g', 'grad_w_in', 'grad_gm_v_norm_g', 'grad_gm_ws', 'grad_gm_bs', 'grad_gm_out_norm_g', 'grad_ssd_conv_w', 'grad_ssd_conv_b', 'grad_ssd_dt_bias', 'grad_ssd_a_log', 'grad_ssd_d', 'grad_ssd_norm_g', 'grad_w_out', 'grad_norm_mlp_g', 'grad_w_ff1', 'grad_w_ff2', 'grad_ple_norm_g', 'grad_w_ple_gate', 'grad_w_ple_proj', 'grad_final_norm_g', 'delta_norm_mix_g', 'delta_w_in', 'delta_gm_v_norm_g', 'delta_gm_ws', 'delta_gm_bs', 'delta_gm_out_norm_g', 'delta_ssd_conv_w', 'delta_ssd_conv_b', 'delta_ssd_dt_bias', 'delta_ssd_a_log', 'delta_ssd_d', 'delta_ssd_norm_g', 'delta_w_out', 'delta_norm_mlp_g', 'delta_w_ff1', 'delta_w_ff2', 'delta_ple_norm_g', 'delta_w_ple_gate', 'delta_w_ple_proj', 'delta_final_norm_g', 'new_m_norm_mix_g', 'new_m_w_in', 'new_m_gm_v_norm_g', 'new_m_gm_ws', 'new_m_gm_bs', 'new_m_gm_out_norm_g', 'new_m_ssd_conv_w', 'new_m_ssd_conv_b', 'new_m_ssd_dt_bias', 'new_m_ssd_a_log', 'new_m_ssd_d', 'new_m_ssd_norm_g', 'new_m_w_out', 'new_m_norm_mlp_g', 'new_m_w_ff1', 'new_m_w_ff2', 'new_m_ple_norm_g', 'new_m_w_ple_gate', 'new_m_w_ple_proj', 'new_m_final_norm_g', 'new_v_norm_mix_g', 'new_v_w_in', 'new_v_gm_v_norm_g', 'new_v_gm_ws', 'new_v_gm_bs', 'new_v_gm_out_norm_g', 'new_v_ssd_conv_w', 'new_v_ssd_conv_b', 'new_v_ssd_dt_bias', 'new_v_ssd_a_log', 'new_v_ssd_d', 'new_v_ssd_norm_g', 'new_v_w_out', 'new_v_norm_mlp_g', 'new_v_w_ff1', 'new_v_w_ff2', 'new_v_ple_norm_g', 'new_v_w_ple_gate', 'new_v_w_ple_proj', 'new_v_final_norm_g']
TWIN_LEAF_KINDS = {'loss': 'loss', 'grad_x': 'grad_x', 'grad_norm_mix_g': 'grad_w', 'grad_w_in': 'grad_w', 'grad_gm_v_norm_g': 'grad_w', 'grad_gm_ws': 'grad_w', 'grad_gm_bs': 'grad_w', 'grad_gm_out_norm_g': 'grad_w', 'grad_ssd_conv_w': 'grad_w', 'grad_ssd_conv_b': 'grad_w', 'grad_ssd_dt_bias': 'grad_w', 'grad_ssd_a_log': 'grad_w', 'grad_ssd_d': 'grad_w', 'grad_ssd_norm_g': 'grad_w', 'grad_w_out': 'grad_w', 'grad_norm_mlp_g': 'grad_w', 'grad_w_ff1': 'grad_w', 'grad_w_ff2': 'grad_w', 'grad_ple_norm_g': 'grad_w', 'grad_w_ple_gate': 'grad_w', 'grad_w_ple_proj': 'grad_w', 'grad_final_norm_g': 'grad_w', 'delta_norm_mix_g': 'delta_w', 'delta_w_in': 'delta_w', 'delta_gm_v_norm_g': 'delta_w', 'delta_gm_ws': 'delta_w', 'delta_gm_bs': 'delta_w', 'delta_gm_out_norm_g': 'delta_w', 'delta_ssd_conv_w': 'delta_w', 'delta_ssd_conv_b': 'delta_w', 'delta_ssd_dt_bias': 'delta_w', 'delta_ssd_a_log': 'delta_w', 'delta_ssd_d': 'delta_w', 'delta_ssd_norm_g': 'delta_w', 'delta_w_out': 'delta_w', 'delta_norm_mlp_g': 'delta_w', 'delta_w_ff1': 'delta_w', 'delta_w_ff2': 'delta_w', 'delta_ple_norm_g': 'delta_w', 'delta_w_ple_gate': 'delta_w', 'delta_w_ple_proj': 'delta_w', 'delta_final_norm_g': 'delta_w', 'new_m_norm_mix_g': 'new_m', 'new_m_w_in': 'new_m', 'new_m_gm_v_norm_g': 'new_m', 'new_m_gm_ws': 'new_m', 'new_m_gm_bs': 'new_m', 'new_m_gm_out_norm_g': 'new_m', 'new_m_ssd_conv_w': 'new_m', 'new_m_ssd_conv_b': 'new_m', 'new_m_ssd_dt_bias': 'new_m', 'new_m_ssd_a_log': 'new_m', 'new_m_ssd_d': 'new_m', 'new_m_ssd_norm_g': 'new_m', 'new_m_w_out': 'new_m', 'new_m_norm_mlp_g': 'new_m', 'new_m_w_ff1': 'new_m', 'new_m_w_ff2': 'new_m', 'new_m_ple_norm_g': 'new_m', 'new_m_w_ple_gate': 'new_m', 'new_m_w_ple_proj': 'new_m', 'new_m_final_norm_g': 'new_m', 'new_v_norm_mix_g': 'new_v', 'new_v_w_in': 'new_v', 'new_v_gm_v_norm_g': 'new_v', 'new_v_gm_ws': 'new_v', 'new_v_gm_bs': 'new_v', 'new_v_gm_out_norm_g': 'new_v', 'new_v_ssd_conv_w': 'new_v', 'new_v_ssd_conv_b': 'new_v', 'new_v_ssd_dt_bias': 'new_v', 'new_v_ssd_a_log': 'new_v', 'new_v_ssd_d': 'new_v', 'new_v_ssd_norm_g': 'new_v', 'new_v_w_out': 'new_v', 'new_v_norm_mlp_g': 'new_v', 'new_v_w_ff1': 'new_v', 'new_v_w_ff2': 'new_v', 'new_v_ple_norm_g': 'new_v', 'new_v_w_ple_gate': 'new_v', 'new_v_w_ple_proj': 'new_v', 'new_v_final_norm_g': 'new_v'}


def _forward(args):
    return _fwd_reference(*[args[k] for k in FWD_PARAMS])


def _output_shape():
    out = _jax.eval_shape(lambda: _forward(_fwd_setup_inputs(0)))
    return out.shape, out.dtype

N_MICROBATCH = 1
ADAM_LR = 0.001
ADAM_B1 = 0.9
ADAM_B2 = 0.999
ADAM_EPS = 1e-08
ADAM_WD = 0.01
ADAM_STEP = 10
PER_EXAMPLE_BATCH_AXIS = {'x': 0, 'p': 1, 'loss_target': 0}
SHARED_INPUTS = []
_WEIGHT_DTYPES = {'norm_mix_g': _jnp.float32, 'w_in': _jnp.float32, 'gm_v_norm_g': _jnp.float32, 'gm_ws': _jnp.float32, 'gm_bs': _jnp.float32, 'gm_out_norm_g': _jnp.float32, 'ssd_conv_w': _jnp.float32, 'ssd_conv_b': _jnp.float32, 'ssd_dt_bias': _jnp.float32, 'ssd_a_log': _jnp.float32, 'ssd_d': _jnp.float32, 'ssd_norm_g': _jnp.float32, 'w_out': _jnp.float32, 'norm_mlp_g': _jnp.float32, 'w_ff1': _jnp.float32, 'w_ff2': _jnp.float32, 'ple_norm_g': _jnp.float32, 'w_ple_gate': _jnp.float32, 'w_ple_proj': _jnp.float32, 'final_norm_g': _jnp.float32}
MOMENT_SCALE = {'norm_mix_g': 2.595193e-01, 'w_in': 1.177806e-01, 'gm_v_norm_g': 7.371659e-02, 'gm_ws': 7.314801e-02, 'gm_bs': 1.117701e-01, 'gm_out_norm_g': 1.838137e-01, 'ssd_conv_w': 1.219578e-01, 'ssd_conv_b': 2.187494e-01, 'ssd_dt_bias': 4.337086e-01, 'ssd_a_log': 9.021708e-01, 'ssd_d': 1.114834e+00, 'ssd_norm_g': 1.683366e-01, 'w_out': 2.352041e-01, 'norm_mlp_g': 1.943661e-01, 'w_ff1': 1.004227e-01, 'w_ff2': 3.114785e-01, 'ple_norm_g': 3.427369e-02, 'w_ple_gate': 3.693527e-02, 'w_ple_proj': 7.098804e-02, 'final_norm_g': 6.465822e+01}


def _to_microbatches(a, axis):
    t = _jnp.moveaxis(a, axis, 0)
    t = t.reshape((N_MICROBATCH, t.shape[0] // N_MICROBATCH) + t.shape[1:])
    return _jnp.moveaxis(t, 1, axis + 1)


def setup_inputs(seed: int = 0) -> dict:
    inp = _fwd_setup_inputs(seed)
    key = _jax.random.fold_in(_jax.random.key(seed), 7919)
    shape, _ = _output_shape()
    out = dict(inp)
    out["loss_target"] = _jax.random.normal(_jax.random.fold_in(key, 0), shape, _jnp.float32)
    for i, name in enumerate(TWIN_WEIGHTS):
        w = inp[name].astype(_jnp.float32)
        if MOMENT_SCALE is None:
            s = _jnp.sqrt(_jnp.mean(_jnp.square(w)) + 1e-30)
        else:
            s = MOMENT_SCALE[name]
        km, kv = _jax.random.split(_jax.random.fold_in(key, i + 1))
        out[name] = w
        out["m_" + name] = s * _jax.random.normal(km, w.shape, _jnp.float32)
        out["v_" + name] = (s * s) * _jax.random.uniform(kv, w.shape, _jnp.float32, 0.5, 1.5)
    if N_MICROBATCH > 1:
        for name, axis in PER_EXAMPLE_BATCH_AXIS.items():
            out[name] = _to_microbatches(out[name], axis)
    return {'x': out['x'], 'p': out['p'], 'norm_mix_g': out['norm_mix_g'], 'w_in': out['w_in'], 'gm_v_norm_g': out['gm_v_norm_g'], 'gm_ws': out['gm_ws'], 'gm_bs': out['gm_bs'], 'gm_out_norm_g': out['gm_out_norm_g'], 'ssd_conv_w': out['ssd_conv_w'], 'ssd_conv_b': out['ssd_conv_b'], 'ssd_dt_bias': out['ssd_dt_bias'], 'ssd_a_log': out['ssd_a_log'], 'ssd_d': out['ssd_d'], 'ssd_norm_g': out['ssd_norm_g'], 'w_out': out['w_out'], 'norm_mlp_g': out['norm_mlp_g'], 'w_ff1': out['w_ff1'], 'w_ff2': out['w_ff2'], 'ple_norm_g': out['ple_norm_g'], 'w_ple_gate': out['w_ple_gate'], 'w_ple_proj': out['w_ple_proj'], 'final_norm_g': out['final_norm_g'], 'loss_target': out['loss_target'], 'm_norm_mix_g': out['m_norm_mix_g'], 'm_w_in': out['m_w_in'], 'm_gm_v_norm_g': out['m_gm_v_norm_g'], 'm_gm_ws': out['m_gm_ws'], 'm_gm_bs': out['m_gm_bs'], 'm_gm_out_norm_g': out['m_gm_out_norm_g'], 'm_ssd_conv_w': out['m_ssd_conv_w'], 'm_ssd_conv_b': out['m_ssd_conv_b'], 'm_ssd_dt_bias': out['m_ssd_dt_bias'], 'm_ssd_a_log': out['m_ssd_a_log'], 'm_ssd_d': out['m_ssd_d'], 'm_ssd_norm_g': out['m_ssd_norm_g'], 'm_w_out': out['m_w_out'], 'm_norm_mlp_g': out['m_norm_mlp_g'], 'm_w_ff1': out['m_w_ff1'], 'm_w_ff2': out['m_w_ff2'], 'm_ple_norm_g': out['m_ple_norm_g'], 'm_w_ple_gate': out['m_w_ple_gate'], 'm_w_ple_proj': out['m_w_ple_proj'], 'm_final_norm_g': out['m_final_norm_g'], 'v_norm_mix_g': out['v_norm_mix_g'], 'v_w_in': out['v_w_in'], 'v_gm_v_norm_g': out['v_gm_v_norm_g'], 'v_gm_ws': out['v_gm_ws'], 'v_gm_bs': out['v_gm_bs'], 'v_gm_out_norm_g': out['v_gm_out_norm_g'], 'v_ssd_conv_w': out['v_ssd_conv_w'], 'v_ssd_conv_b': out['v_ssd_conv_b'], 'v_ssd_dt_bias': out['v_ssd_dt_bias'], 'v_ssd_a_log': out['v_ssd_a_log'], 'v_ssd_d': out['v_ssd_d'], 'v_ssd_norm_g': out['v_ssd_norm_g'], 'v_w_out': out['v_w_out'], 'v_norm_mlp_g': out['v_norm_mlp_g'], 'v_w_ff1': out['v_w_ff1'], 'v_w_ff2': out['v_w_ff2'], 'v_ple_norm_g': out['v_ple_norm_g'], 'v_w_ple_gate': out['v_w_ple_gate'], 'v_w_ple_proj': out['v_w_ple_proj'], 'v_final_norm_g': out['v_final_norm_g']}


def _loss(weights, diff, rest, loss_target):
    with _jax.named_scope("forward"):
        args = {**rest, TWIN_DIFF_INPUT: diff, **{k: w.astype(_WEIGHT_DTYPES[k]) for k, w in weights.items()}}
        y = _forward(args)
    with _jax.named_scope("loss_head"):
        err = _jnp.square(y.astype(_jnp.float32) - loss_target)
        return 0.5 * _jnp.sum(_jnp.mean(err, axis=-1)) if err.ndim else 0.5 * err


def _adamw(w, g, m, v):
    m = ADAM_B1 * m + (1.0 - ADAM_B1) * g
    v = ADAM_B2 * v + (1.0 - ADAM_B2) * _jnp.square(g)
    m_hat = m / (1.0 - ADAM_B1 ** ADAM_STEP)
    v_hat = v / (1.0 - ADAM_B2 ** ADAM_STEP)
    delta = -ADAM_LR * (m_hat / (_jnp.sqrt(v_hat) + ADAM_EPS) + ADAM_WD * w)
    return delta, m, v


def reference(x, p, norm_mix_g, w_in, gm_v_norm_g, gm_ws, gm_bs, gm_out_norm_g, ssd_conv_w, ssd_conv_b, ssd_dt_bias, ssd_a_log, ssd_d, ssd_norm_g, w_out, norm_mlp_g, w_ff1, w_ff2, ple_norm_g, w_ple_gate, w_ple_proj, final_norm_g, loss_target, m_norm_mix_g, m_w_in, m_gm_v_norm_g, m_gm_ws, m_gm_bs, m_gm_out_norm_g, m_ssd_conv_w, m_ssd_conv_b, m_ssd_dt_bias, m_ssd_a_log, m_ssd_d, m_ssd_norm_g, m_w_out, m_norm_mlp_g, m_w_ff1, m_w_ff2, m_ple_norm_g, m_w_ple_gate, m_w_ple_proj, m_final_norm_g, v_norm_mix_g, v_w_in, v_gm_v_norm_g, v_gm_ws, v_gm_bs, v_gm_out_norm_g, v_ssd_conv_w, v_ssd_conv_b, v_ssd_dt_bias, v_ssd_a_log, v_ssd_d, v_ssd_norm_g, v_w_out, v_norm_mlp_g, v_w_ff1, v_w_ff2, v_ple_norm_g, v_w_ple_gate, v_w_ple_proj, v_final_norm_g):
    given = dict(x=x, p=p, norm_mix_g=norm_mix_g, w_in=w_in, gm_v_norm_g=gm_v_norm_g, gm_ws=gm_ws, gm_bs=gm_bs, gm_out_norm_g=gm_out_norm_g, ssd_conv_w=ssd_conv_w, ssd_conv_b=ssd_conv_b, ssd_dt_bias=ssd_dt_bias, ssd_a_log=ssd_a_log, ssd_d=ssd_d, ssd_norm_g=ssd_norm_g, w_out=w_out, norm_mlp_g=norm_mlp_g, w_ff1=w_ff1, w_ff2=w_ff2, ple_norm_g=ple_norm_g, w_ple_gate=w_ple_gate, w_ple_proj=w_ple_proj, final_norm_g=final_norm_g, loss_target=loss_target, m_norm_mix_g=m_norm_mix_g, m_w_in=m_w_in, m_gm_v_norm_g=m_gm_v_norm_g, m_gm_ws=m_gm_ws, m_gm_bs=m_gm_bs, m_gm_out_norm_g=m_gm_out_norm_g, m_ssd_conv_w=m_ssd_conv_w, m_ssd_conv_b=m_ssd_conv_b, m_ssd_dt_bias=m_ssd_dt_bias, m_ssd_a_log=m_ssd_a_log, m_ssd_d=m_ssd_d, m_ssd_norm_g=m_ssd_norm_g, m_w_out=m_w_out, m_norm_mlp_g=m_norm_mlp_g, m_w_ff1=m_w_ff1, m_w_ff2=m_w_ff2, m_ple_norm_g=m_ple_norm_g, m_w_ple_gate=m_w_ple_gate, m_w_ple_proj=m_w_ple_proj, m_final_norm_g=m_final_norm_g, v_norm_mix_g=v_norm_mix_g, v_w_in=v_w_in, v_gm_v_norm_g=v_gm_v_norm_g, v_gm_ws=v_gm_ws, v_gm_bs=v_gm_bs, v_gm_out_norm_g=v_gm_out_norm_g, v_ssd_conv_w=v_ssd_conv_w, v_ssd_conv_b=v_ssd_conv_b, v_ssd_dt_bias=v_ssd_dt_bias, v_ssd_a_log=v_ssd_a_log, v_ssd_d=v_ssd_d, v_ssd_norm_g=v_ssd_norm_g, v_w_out=v_w_out, v_norm_mlp_g=v_norm_mlp_g, v_w_ff1=v_w_ff1, v_w_ff2=v_w_ff2, v_ple_norm_g=v_ple_norm_g, v_w_ple_gate=v_w_ple_gate, v_w_ple_proj=v_w_ple_proj, v_final_norm_g=v_final_norm_g)
    weights = {n: given[n] for n in TWIN_WEIGHTS}
    shared = {n: given[n] for n in SHARED_INPUTS}
    per_example = {n: given[n] for n in ['x', 'p']}
    grad_fn = _jax.value_and_grad(_loss, argnums=(0, 1))

    def one_microbatch(ex, loss_target):
        ex = dict(ex)
        diff = ex.pop(TWIN_DIFF_INPUT)
        return grad_fn(weights, diff, {**shared, **ex}, loss_target)

    if N_MICROBATCH == 1:
        loss, (grad_w, grad_x) = one_microbatch(per_example, given["loss_target"])
    else:
        def body(carry, xs):
            loss_sum, grad_sum = carry
            l_k, (gw_k, gx_k) = one_microbatch(xs[0], xs[1])
            with _jax.named_scope("update"):
                return (loss_sum + l_k, _jax.tree.map(_jnp.add, grad_sum, gw_k)), gx_k

        init = (_jnp.zeros((), _jnp.float32), _jax.tree.map(_jnp.zeros_like, weights))
        (loss, grad_w), grad_x = _jax.lax.scan(body, init, (per_example, given["loss_target"]))
    with _jax.named_scope("update"):
        delta_w, new_m, new_v = {}, {}, {}
        for n in TWIN_WEIGHTS:
            delta_w[n], new_m[n], new_v[n] = _adamw(weights[n], grad_w[n], given["m_" + n], given["v_" + n])
    return (loss, grad_x, *[grad_w[n] for n in TWIN_WEIGHTS], *[delta_w[n] for n in TWIN_WEIGHTS],
            *[new_m[n] for n in TWIN_WEIGHTS], *[new_v[n] for n in TWIN_WEIGHTS])
```

```python
import functools

import jax
import jax.numpy as jnp
from jax import lax
from jax.experimental import pallas as pl
from jax.experimental.pallas import tpu as pltpu

F32 = jnp.float32
BF16 = jnp.bfloat16
MXU = jnp.bfloat16

D = 1024
CH = 128
GM_HEADS = 8
SSD_HEADS = 16
SSD_P = 64
CONV_CH = 1536
N_MAIN = 4608
N_INPROJ = 4624
DFF = 4096
DPLE = 256
EPS = 1e-6
NEG = -1e30

LR, B1, B2, ADAM_EPS, WD, STEP = 0.001, 0.9, 0.999, 1e-08, 0.01, 10

VMEM_LIMIT = 56 * 1024 * 1024
PACK_ROWS = 32768
HALF_ROWS = PACK_ROWS // 2
MESH = pl.DeviceIdType.MESH

INV_SQRT2 = 0.7071067811865476
INV_SQRT_2PI = 0.3989422804014327


def _cp(n_axes=1):
    return pltpu.CompilerParams(dimension_semantics=("arbitrary",) * n_axes, vmem_limit_bytes=VMEM_LIMIT)


def _dot(a, b):
    return jnp.dot(a, b, preferred_element_type=F32)


def _dot_nt(a, b):
    return lax.dot_general(a, b, (((1,), (1,)), ((), ())), preferred_element_type=F32)


def _dot_tn(a, b):
    return lax.dot_general(a, b, (((0,), (0,)), ((), ())), preferred_element_type=F32)


def _dot_hi(a, b):
    return jnp.dot(a, b, preferred_element_type=F32, precision=lax.Precision.HIGHEST)


def _rows(tm, n, j=0):
    return pl.BlockSpec((tm, n), lambda i: (i, j))


def _const(shape):
    nd = len(shape)
    return pl.BlockSpec(shape, lambda *_: (0,) * nd)


def _sds(shape, dtype):
    return jax.ShapeDtypeStruct(shape, dtype)


def _rms(x):
    r = lax.rsqrt(jnp.mean(x * x, axis=-1, keepdims=True) + EPS)
    return x * r, r


def _rms_bwd(dy, xhat, r, g):
    dyg = dy * g
    return r * (dyg - xhat * jnp.mean(dyg * xhat, axis=-1, keepdims=True))


def _sigmoid(x):
    return 1.0 / (1.0 + jnp.exp(-x))


def _gelu(x):
    cdf = 0.5 * (1.0 + lax.erf(x * INV_SQRT2))
    pdf = jnp.exp(-0.5 * x * x) * INV_SQRT_2PI
    return x * cdf, cdf + x * pdf


def _softplus(x):
    e = jnp.exp(-jnp.abs(x))
    u = 1.0 + e
    log1p = jnp.where(u == 1.0, e, jnp.log(u) * e / (u - 1.0))
    return jnp.maximum(x, 0.0) + log1p


def _inproj(x, g, wm, wdt, tm):
    T = x.shape[0]

    def body(x_ref, g_ref, wm_ref, wdt_ref, proj_ref, dt_ref, xn_ref):
        xh, _ = _rms(x_ref[...])
        xn = (xh * g_ref[...]).astype(MXU)
        xn_ref[...] = xn
        for n in range(N_MAIN // 512):
            proj_ref[:, n * 512:(n + 1) * 512] = _dot(xn, wm_ref[:, n * 512:(n + 1) * 512])
        dt_ref[...] = _dot(xn, wdt_ref[...])

    return pl.pallas_call(
        body, grid=(T // tm,), name="inproj",
        in_specs=[_rows(tm, D), _const((1, D)), _const((D, N_MAIN)), _const((D, 128))],
        out_specs=[_rows(tm, N_MAIN), _rows(tm, 128), _rows(tm, D)],
        out_shape=[_sds((T, N_MAIN), F32), _sds((T, 128), F32), _sds((T, D), MXU)],
        compiler_params=_cp(),
    )(x, g, wm, wdt)


def _gmlp_fwd_vals(u, v, gv, ws_ref, bst, gout):
    ug, dug = _gelu(u)
    vg, dvg = _gelu(v)
    row = lax.broadcasted_iota(jnp.int32, (CH, CH), 0)
    col = lax.broadcasted_iota(jnp.int32, (CH, CH), 1)
    tril = row >= col
    ys, heads = [], []
    for h in range(GM_HEADS):
        sl = slice(h * 128, (h + 1) * 128)
        vhat, rv = _rms(vg[:, sl])
        vn = (vhat * gv[:, sl]).astype(MXU)
        wt = jnp.where(tril, ws_ref[h], 0.0)
        mixed = _dot(wt.astype(MXU), vn) + bst[:, h:h + 1]
        ys.append(ug[:, sl] * mixed)
        heads.append((vhat, rv, vn, wt, mixed))
    y = jnp.concatenate(ys, axis=1)
    yhat, ry = _rms(y)
    return dict(ug=ug, dug=dug, dvg=dvg, heads=heads, yhat=yhat, ry=ry, tril=tril, out=yhat * gout)


def _gmlp_fwd(proj, gv, ws, bst, gout):
    T = proj.shape[0]

    def body(u_ref, v_ref, gv_ref, ws_ref, bst_ref, gout_ref, ya_ref):
        f = _gmlp_fwd_vals(u_ref[...], v_ref[...], gv_ref[...], ws_ref, bst_ref[...], gout_ref[...])
        ya_ref[...] = f["out"].astype(MXU)

    return pl.pallas_call(
        body, grid=(T // CH,), name="gmlp_fwd",
        in_specs=[_rows(CH, 1024, 0), _rows(CH, 1024, 1), _const((1, 1024)), _const((GM_HEADS, CH, CH)),
                  _const((CH, 128)), _const((1, 1024))],
        out_specs=_rows(CH, 1024),
        out_shape=_sds((T, 1024), MXU),
        compiler_params=_cp(),
    )(proj, proj, gv, ws, bst, gout)


def _shift_down(cur, halo, j):
    sh = pltpu.roll(cur, j, 0)
    row8 = lax.broadcasted_iota(jnp.int32, (8, cur.shape[1]), 0)
    top = jnp.where(row8 < j, pltpu.roll(halo, j, 0), sh[0:8])
    return jnp.concatenate([top, sh[8:]], axis=0)


def _shift_up(cur, halo, j):
    n = cur.shape[0]
    sh = pltpu.roll(cur, n - j, 0)
    row8 = lax.broadcasted_iota(jnp.int32, (8, cur.shape[1]), 0)
    bot = jnp.where(row8 + j >= 8, pltpu.roll(halo, 8 - j, 0), sh[n - 8:n])
    return jnp.concatenate([sh[0:n - 8], bot], axis=0)


def _ssd_fwd_vals(z, xbc, halo, dtraw, convw, convb, dtb, alog, dskip, ng, ex, ltri, s_prev):
    shifts = [xbc] + [_shift_down(xbc, halo, j) for j in (1, 2, 3)]
    conv = convb + convw[3:4] * shifts[0] + convw[2:3] * shifts[1] + convw[1:2] * shifts[2] + convw[0:1] * shifts[3]
    sig_c = _sigmoid(conv)
    xa = conv * sig_c
    xs = xa[:, :1024]
    bm = [xa[:, 1024:1152], xa[:, 1152:1280]]
    cm = [xa[:, 1280:1408], xa[:, 1408:1536]]
    dtpre = dtraw + dtb
    dt = _softplus(dtpre)
    a_neg = -jnp.exp(alog)
    cs = _dot_hi(ltri, dt * a_neg)
    cst = cs.T
    last = cs[CH - 1:CH]
    ecs = jnp.exp(cs)
    dec = jnp.exp(last - cs)
    dte = _dot_hi(dt, ex)
    ecse = _dot_hi(ecs, ex)
    dece = _dot_hi(dec, ex)
    cde = ecse[CH - 1:CH]
    de = _dot_hi(jnp.broadcast_to(dskip, (8, 128)), ex)[0:1]
    xdt = xs * dte
    row = lax.broadcasted_iota(jnp.int32, (CH, CH), 0)
    col = lax.broadcasted_iota(jnp.int32, (CH, CH), 1)
    tril = row >= col
    lo = col < SSD_P
    bmb = [b.astype(MXU) for b in bm]
    cmb = [c.astype(MXU) for c in cm]
    mg = [_dot_nt(cmb[g], bmb[g]) for g in range(2)]
    yd, lms, whs = [], [], []
    for q in range(8):
        g = q // 4
        xq = xdt[:, q * 128:(q + 1) * 128]
        acc = None
        for hh in range(2):
            h = 2 * q + hh
            seg = cs[:, h:h + 1] - cst[h:h + 1, :]
            lm = jnp.exp(jnp.where(tril, seg, NEG))
            wh = (mg[g] * lm).astype(MXU)
            xm = jnp.where(lo if hh == 0 else ~lo, xq, 0.0).astype(MXU)
            part = _dot(wh, xm)
            acc = part if acc is None else acc + part
            lms.append(lm)
            whs.append(wh)
        yd.append(acc)
    yd = jnp.concatenate(yd, axis=1)
    sb = s_prev.astype(MXU)
    yo = jnp.concatenate([_dot(cmb[g], sb[:, g * 512:(g + 1) * 512]) for g in range(2)], axis=1) * ecse
    xdec = (xdt * dece).astype(MXU)
    states = jnp.concatenate([_dot_tn(bmb[g], xdec[:, g * 512:(g + 1) * 512]) for g in range(2)], axis=1)
    s_next = s_prev * cde + states
    ypre = yd + yo + de * xs
    sig_z = _sigmoid(z)
    yg = ypre * z * sig_z
    outs, yhat, rr = [], [], []
    for g in range(2):
        sl = slice(g * 512, (g + 1) * 512)
        yh, r = _rms(yg[:, sl])
        yhat.append(yh)
        rr.append(r)
        outs.append(yh * ng[:, sl])
    return dict(shifts=shifts, conv=conv, sig_c=sig_c, xs=xs, bmb=bmb, cmb=cmb, dtpre=dtpre, dt=dt, a_neg=a_neg,
                cs=cs, last=last, ecs=ecs, dec=dec, dte=dte, ecse=ecse, dece=dece, cde=cde, de=de, xdt=xdt,
                mg=mg, lms=lms, whs=whs, lo=lo, yo=yo, sb=sb, xdec=xdec, s_next=s_next, ypre=ypre, sig_z=sig_z,
                yhat=yhat, rr=rr, out=jnp.concatenate(outs, axis=1))


def _ssd_specs(nch, rev):
    def tok(b, c):
        return b * nch + ((nch - 1 - c) if rev else c)

    return tok, [
        pl.BlockSpec((CH, 1024), lambda b, c: (tok(b, c), 2)),
        pl.BlockSpec((CH, CONV_CH), lambda b, c: (tok(b, c), 2)),
        pl.BlockSpec((8, CONV_CH), lambda b, c: (jnp.maximum(tok(b, c) * (CH // 8) - 1, 0), 2)),
        pl.BlockSpec((CH, 128), lambda b, c: (tok(b, c), 0)),
        _const((8, CONV_CH)), _const((1, CONV_CH)), _const((1, 128)), _const((1, 128)), _const((1, 128)),
        _const((1, 1024)), _const((128, 1024)), _const((CH, CH)),
    ]


def _ssd_fwd(proj, dtraw, convw, convb, dtb, alog, dskip, ng, ex, ltri, nb):
    T = proj.shape[0]
    nch = T // CH // nb
    tok, in_specs = _ssd_specs(nch, rev=False)

    def body(z_ref, xbc_ref, halo_ref, dt_ref, cw_ref, cb_ref, dtb_ref, al_ref, ds_ref, ng_ref, ex_ref, lt_ref,
             yb_ref, sall_ref, s_ref):
        c = pl.program_id(1)

        @pl.when(c == 0)
        def _():
            s_ref[...] = jnp.zeros_like(s_ref)

        halo = jnp.where(c == 0, 0.0, halo_ref[...])
        s_prev = s_ref[...]
        sall_ref[0] = s_prev
        f = _ssd_fwd_vals(z_ref[...], xbc_ref[...], halo, dt_ref[...], cw_ref[...], cb_ref[...], dtb_ref[...],
                          al_ref[...], ds_ref[...], ng_ref[...], ex_ref[...], lt_ref[...], s_prev)
        s_ref[...] = f["s_next"]
        yb_ref[...] = f["out"].astype(MXU)

    return pl.pallas_call(
        body, grid=(nb, nch), name="ssd_fwd",
        in_specs=in_specs,
        out_specs=[pl.BlockSpec((CH, 1024), lambda b, c: (tok(b, c), 0)),
                   pl.BlockSpec((1, 128, 1024), lambda b, c: (tok(b, c), 0, 0))],
        out_shape=[_sds((T, 1024), MXU), _sds((T // CH, 128, 1024), F32)],
        scratch_shapes=[pltpu.VMEM((128, 1024), F32)],
        compiler_params=_cp(2),
    )(proj, proj, proj, dtraw, convw, convb, dtb, alog, dskip, ng, ex, ltri)


def _outproj(ya, yb, wo, x, g, tm):
    T = x.shape[0]

    def body(ya_ref, yb_ref, wo_ref, x_ref, g_ref, h1_ref, hn_ref):
        h1 = x_ref[...] + _dot(ya_ref[...], wo_ref[0:1024, :]) + _dot(yb_ref[...], wo_ref[1024:2048, :])
        h1_ref[...] = h1
        hn_ref[...] = (_rms(h1)[0] * g_ref[...]).astype(MXU)

    return pl.pallas_call(
        body, grid=(T // tm,), name="outproj",
        in_specs=[_rows(tm, 1024), _rows(tm, 1024), _const((2048, D)), _rows(tm, D), _const((1, D))],
        out_specs=[_rows(tm, D), _rows(tm, D)],
        out_shape=[_sds((T, D), F32), _sds((T, D), MXU)],
        compiler_params=_cp(),
    )(ya, yb, wo, x, g)


def _ff1(hn, w1, tm):
    T = hn.shape[0]

    def body(hn_ref, w1_ref, hid_ref):
        hn_v = hn_ref[...]
        for n in range(DFF // 1024):
            sl = slice(n * 1024, (n + 1) * 1024)
            hid_ref[:, sl] = jnp.maximum(_dot(hn_v, w1_ref[:, sl]), 0.0).astype(MXU)

    return pl.pallas_call(
        body, grid=(T // tm,), name="ff1",
        in_specs=[_rows(tm, D), _const((D, DFF))],
        out_specs=_rows(tm, DFF),
        out_shape=_sds((T, DFF), MXU),
        compiler_params=_cp(),
    )(hn, w1)


def _sq(hid):
    h = hid.astype(F32)
    return (h * h).astype(MXU)


def _ff2(hid, w2, h1, g, tm):
    T = h1.shape[0]

    def body(hid_ref, w2_ref, h1_ref, g_ref, h2_ref, hp_ref):
        h2 = h1_ref[...] + _dot(_sq(hid_ref[...]), w2_ref[...])
        h2_ref[...] = h2
        hp_ref[...] = (_rms(h2)[0] * g_ref[...]).astype(MXU)

    return pl.pallas_call(
        body, grid=(T // tm,), name="ff2",
        in_specs=[_rows(tm, DFF), _const((DFF, D)), _rows(tm, D), _const((1, D))],
        out_specs=[_rows(tm, D), _rows(tm, D)],
        out_shape=[_sds((T, D), F32), _sds((T, D), MXU)],
        compiler_params=_cp(),
    )(hid, w2, h1, g)


def _tail(h2, hp, p, tgt, wg, wp, gf, tm):
    T = h2.shape[0]

    def body(h2_ref, hp_ref, p_ref, t_ref, wg_ref, wp_ref, gf_ref, dh3_ref, dgl_ref, dpe_ref, loss_ref, dgf_ref):
        @pl.when(pl.program_id(0) == 0)
        def _():
            loss_ref[...] = jnp.zeros_like(loss_ref)
            dgf_ref[...] = jnp.zeros_like(dgf_ref)

        gate = _sigmoid(_dot(hp_ref[...], wg_ref[...]))
        pe = _dot(p_ref[...].astype(MXU), wp_ref[...])
        h3 = h2_ref[...] + gate * pe
        hh, r = _rms(h3)
        gf = gf_ref[...]
        diff = hh * gf - t_ref[...]
        loss_ref[...] += 0.5 * jnp.sum(jnp.mean(diff * diff, axis=-1, keepdims=True))
        dout = diff * (1.0 / D)
        dgf_ref[...] += jnp.sum(dout * hh, axis=0, keepdims=True)
        dh3 = _rms_bwd(dout, hh, r, gf)
        dh3_ref[...] = dh3
        dgl_ref[...] = (dh3 * pe * gate * (1.0 - gate)).astype(MXU)
        dpe_ref[...] = (dh3 * gate).astype(MXU)

    return pl.pallas_call(
        body, grid=(T // tm,), name="tail",
        in_specs=[_rows(tm, D), _rows(tm, D), _rows(tm, DPLE), _rows(tm, D), _const((D, D)), _const((DPLE, D)),
                  _const((1, D))],
        out_specs=[_rows(tm, D), _rows(tm, D), _rows(tm, D), _const((8, 128)), _const((1, D))],
        out_shape=[_sds((T, D), F32), _sds((T, D), MXU), _sds((T, D), MXU), _sds((8, 128), F32), _sds((1, D), F32)],
        compiler_params=_cp(),
    )(h2, hp, p, tgt, wg, wp, gf)


def _ple_bwd(dgl, wg, dh3, h2, g, tm):
    T = h2.shape[0]

    def body(dgl_ref, wg_ref, dh3_ref, h2_ref, g_ref, dh2_ref, dh2b_ref, dg_ref):
        @pl.when(pl.program_id(0) == 0)
        def _():
            dg_ref[...] = jnp.zeros_like(dg_ref)

        dhp = _dot_nt(dgl_ref[...], wg_ref[...])
        hh, r = _rms(h2_ref[...])
        dg_ref[...] += jnp.sum(dhp * hh, axis=0, keepdims=True)
        dh2 = dh3_ref[...] + _rms_bwd(dhp, hh, r, g_ref[...])
        dh2_ref[...] = dh2
        dh2b_ref[...] = dh2.astype(MXU)

    return pl.pallas_call(
        body, grid=(T // tm,), name="ple_bwd",
        in_specs=[_rows(tm, D), _const((D, D)), _rows(tm, D), _rows(tm, D), _const((1, D))],
        out_specs=[_rows(tm, D), _rows(tm, D), _const((1, D))],
        out_shape=[_sds((T, D), F32), _sds((T, D), MXU), _sds((1, D), F32)],
        compiler_params=_cp(),
    )(dgl, wg, dh3, h2, g)


def _ff2_bwd(dh2b, w2, hid, tm):
    T = hid.shape[0]

    def body(dh2b_ref, w2_ref, hid_ref, dpre_ref):
        d = dh2b_ref[...]
        for n in range(DFF // 1024):
            sl = slice(n * 1024, (n + 1) * 1024)
            da = _dot_nt(d, w2_ref[sl, :])
            dpre_ref[:, sl] = (2.0 * da * hid_ref[:, sl].astype(F32)).astype(MXU)

    return pl.pallas_call(
        body, grid=(T // tm,), name="ff2_bwd",
        in_specs=[_rows(tm, D), _const((DFF, D)), _rows(tm, DFF)],
        out_specs=_rows(tm, DFF),
        out_shape=_sds((T, DFF), MXU),
        compiler_params=_cp(),
    )(dh2b, w2, hid)


def _ff1_bwd(dpre, w1, dh2, h1, g, tm):
    T = h1.shape[0]

    def body(dpre_ref, w1_ref, dh2_ref, h1_ref, g_ref, dh1_ref, dh1b_ref, dg_ref):
        @pl.when(pl.program_id(0) == 0)
        def _():
            dg_ref[...] = jnp.zeros_like(dg_ref)

        dhn = _dot_nt(dpre_ref[...], w1_ref[...])
        hh, r = _rms(h1_ref[...])
        dg_ref[...] += jnp.sum(dhn * hh, axis=0, keepdims=True)
        dh1 = dh2_ref[...] + _rms_bwd(dhn, hh, r, g_ref[...])
        dh1_ref[...] = dh1
        dh1b_ref[...] = dh1.astype(MXU)

    return pl.pallas_call(
        body, grid=(T // tm,), name="ff1_bwd",
        in_specs=[_rows(tm, DFF), _const((D, DFF)), _rows(tm, D), _rows(tm, D), _const((1, D))],
        out_specs=[_rows(tm, D), _rows(tm, D), _const((1, D))],
        out_shape=[_sds((T, D), F32), _sds((T, D), MXU), _sds((1, D), F32)],
        compiler_params=_cp(),
    )(dpre, w1, dh2, h1, g)


def _outproj_bwd(dh1b, wo, tm):
    T = dh1b.shape[0]

    def body(d_ref, wo_ref, dcat_ref):
        d = d_ref[...]
        dcat_ref[:, 0:1024] = _dot_nt(d, wo_ref[0:1024, :])
        dcat_ref[:, 1024:2048] = _dot_nt(d, wo_ref[1024:2048, :])

    return pl.pallas_call(
        body, grid=(T // tm,), name="outproj_bwd",
        in_specs=[_rows(tm, D), _const((2048, D))],
        out_specs=_rows(tm, 2048),
        out_shape=_sds((T, 2048), F32),
        compiler_params=_cp(),
    )(dh1b, wo)


def _gmlp_bwd(proj, dcat, gv, ws, bst, gout):
    T = proj.shape[0]

    def body(u_ref, v_ref, dya_ref, gv_ref, ws_ref, bst_ref, gout_ref, duv_ref, dgv_ref, dws_ref, dbst_ref, dgo_ref):
        @pl.when(pl.program_id(0) == 0)
        def _():
            dgv_ref[...] = jnp.zeros_like(dgv_ref)
            dws_ref[...] = jnp.zeros_like(dws_ref)
            dbst_ref[...] = jnp.zeros_like(dbst_ref)
            dgo_ref[...] = jnp.zeros_like(dgo_ref)

        gv = gv_ref[...]
        f = _gmlp_fwd_vals(u_ref[...], v_ref[...], gv, ws_ref, bst_ref[...], gout_ref[...])
        dya = dya_ref[...]
        dgo_ref[...] += jnp.sum(dya * f["yhat"], axis=0, keepdims=True)
        dy = _rms_bwd(dya, f["yhat"], f["ry"], gout_ref[...])
        lane = lax.broadcasted_iota(jnp.int32, (CH, 128), 1)
        dbs = jnp.zeros((CH, 128), F32)
        dug, dvg, dgvs = [], [], []
        for h in range(GM_HEADS):
            sl = slice(h * 128, (h + 1) * 128)
            vhat, rv, vn, wt, mixed = f["heads"][h]
            dyh = dy[:, sl]
            dug.append(dyh * mixed)
            dmixed = dyh * f["ug"][:, sl]
            dmb = dmixed.astype(MXU)
            dws_ref[h] += jnp.where(f["tril"], _dot_nt(dmb, vn), 0.0)
            dbs = dbs + jnp.where(lane == h, jnp.sum(dmixed, axis=1, keepdims=True), 0.0)
            dvn = _dot_tn(wt.astype(MXU), dmb)
            dgvs.append(jnp.sum(dvn * vhat, axis=0, keepdims=True))
            dvg.append(_rms_bwd(dvn, vhat, rv, gv[:, sl]))
        dbst_ref[...] += dbs
        dgv_ref[...] += jnp.concatenate(dgvs, axis=1)
        duv_ref[:, 0:1024] = (jnp.concatenate(dug, axis=1) * f["dug"]).astype(MXU)
        duv_ref[:, 1024:2048] = (jnp.concatenate(dvg, axis=1) * f["dvg"]).astype(MXU)

    return pl.pallas_call(
        body, grid=(T // CH,), name="gmlp_bwd",
        in_specs=[_rows(CH, 1024, 0), _rows(CH, 1024, 1), _rows(CH, 1024, 0), _const((1, 1024)),
                  _const((GM_HEADS, CH, CH)), _const((CH, 128)), _const((1, 1024))],
        out_specs=[_rows(CH, 2048), _const((1, 1024)), _const((GM_HEADS, CH, CH)), _const((CH, 128)),
                   _const((1, 1024))],
        out_shape=[_sds((T, 2048), MXU), _sds((1, 1024), F32), _sds((GM_HEADS, CH, CH), F32), _sds((CH, 128), F32),
                   _sds((1, 1024), F32)],
        compiler_params=_cp(),
    )(proj, proj, dcat, gv, ws, bst, gout)


def _ssd_bwd(proj, dtraw, sall, dcat, convw, convb, dtb, alog, dskip, ng, ex, ltri, ext, nb):
    T = proj.shape[0]
    nch = T // CH // nb
    tok, in_specs = _ssd_specs(nch, rev=True)
    in_specs = in_specs + [
        _const((1024, 128)),
        pl.BlockSpec((1, 128, 1024), lambda b, c: (tok(b, c), 0, 0)),
        pl.BlockSpec((CH, 1024), lambda b, c: (tok(b, c), 1)),
    ]

    def body(z_ref, xbc_ref, halo_ref, dt_ref, cw_ref, cb_ref, dtb_ref, al_ref, ds_ref, ng_ref, ex_ref, lt_ref,
             ext_ref, sall_ref, dyb_ref,
             dssd_ref, ddt_ref, dcw_ref, dcb_ref, ddtb_ref, dal_ref, dds_ref, dng_ref,
             dst_ref, dnext_ref):
        b = pl.program_id(0)
        c = pl.program_id(1)

        @pl.when((b == 0) & (c == 0))
        def _():
            for r in (dcw_ref, dcb_ref, ddtb_ref, dal_ref, dds_ref, dng_ref):
                r[...] = jnp.zeros_like(r)

        @pl.when(c == 0)
        def _():
            dst_ref[...] = jnp.zeros_like(dst_ref)
            dnext_ref[...] = jnp.zeros_like(dnext_ref)

        first_chunk = c == nch - 1
        halo = jnp.where(first_chunk, 0.0, halo_ref[...])
        z = z_ref[...]
        ex = ex_ref[...]
        ext = ext_ref[...]
        cw = cw_ref[...]
        ng = ng_ref[...]
        s_prev = sall_ref[0]
        f = _ssd_fwd_vals(z, xbc_ref[...], halo, dt_ref[...], cw, cb_ref[...], dtb_ref[...], al_ref[...],
                          ds_ref[...], ng, ex, lt_ref[...], s_prev)
        xs, xdt, cs, dec, dt = f["xs"], f["xdt"], f["cs"], f["dec"], f["dt"]
        dyb = dyb_ref[...]
        dyg, dngs = [], []
        for g in range(2):
            sl = slice(g * 512, (g + 1) * 512)
            dngs.append(jnp.sum(dyb[:, sl] * f["yhat"][g], axis=0, keepdims=True))
            dyg.append(_rms_bwd(dyb[:, sl], f["yhat"][g], f["rr"][g], ng[:, sl]))
        dng_ref[...] += jnp.concatenate(dngs, axis=1)
        dyg = jnp.concatenate(dyg, axis=1)
        sig_z = f["sig_z"]
        silu_z = z * sig_z
        dy = dyg * silu_z
        dz = dyg * f["ypre"] * sig_z * (1.0 + z * (1.0 - sig_z))
        dds_ref[...] += _dot_hi(jnp.broadcast_to(jnp.sum(dy * xs, axis=0, keepdims=True), (8, 1024)), ext)[0:1]
        dxs = dy * f["de"]
        dye = dy * f["ecse"]
        dyeb = dye.astype(MXU)
        dcs = _dot_hi(dy * f["yo"], ext)
        dst = dst_ref[...]
        dstb = dst.astype(MXU)
        bmb, cmb, sb, xdec = f["bmb"], f["cmb"], f["sb"], f["xdec"]
        u = jnp.concatenate([_dot(bmb[g], dstb[:, g * 512:(g + 1) * 512]) for g in range(2)], axis=1)
        dxdt = [u[:, q * 128:(q + 1) * 128] * f["dece"][:, q * 128:(q + 1) * 128] for q in range(8)]
        t = _dot_hi(u * xdt, ext) * dec
        row = lax.broadcasted_iota(jnp.int32, (CH, 128), 0)
        lane = lax.broadcasted_iota(jnp.int32, (CH, 128), 1)
        dcd = _dot_hi(jnp.broadcast_to(jnp.sum(dst * s_prev, axis=0, keepdims=True), (8, 1024)), ext)[0:1]
        cd = jnp.exp(f["last"])
        dcs = dcs - t + jnp.where(row == CH - 1, jnp.sum(t, axis=0, keepdims=True) + dcd * cd, 0.0)
        dcst = jnp.zeros((128, CH), F32)
        lo = f["lo"]
        dbm, dcm, ds_prev = [], [], []
        for g in range(2):
            sl = slice(g * 512, (g + 1) * 512)
            dmg = jnp.zeros((CH, CH), F32)
            for q in range(4 * g, 4 * g + 4):
                dyq = dy[:, q * 128:(q + 1) * 128]
                xq = xdt[:, q * 128:(q + 1) * 128].astype(MXU)
                for hh in range(2):
                    h = 2 * q + hh
                    m = lo if hh == 0 else ~lo
                    dym = jnp.where(m, dyq, 0.0).astype(MXU)
                    gh = _dot_nt(dym, xq)
                    gl = gh * f["lms"][h]
                    dmg = dmg + gl
                    qh = gl * f["mg"][g]
                    dcs = dcs + jnp.where(lane == h, jnp.sum(qh, axis=1, keepdims=True), 0.0)
                    dcst = dcst - jnp.where(row == h, jnp.sum(qh, axis=0, keepdims=True), 0.0)
                    dxdt[q] = dxdt[q] + _dot_tn(f["whs"][h], dym)
            dmgb = dmg.astype(MXU)
            dcm.append(_dot(dmgb, bmb[g]) + _dot_nt(dyeb[:, sl], sb[:, sl]))
            dbm.append(_dot_tn(dmgb, cmb[g]) + _dot_nt(xdec[:, sl], dstb[:, sl]))
            ds_prev.append(_dot_tn(cmb[g], dyeb[:, sl]))
        dst_ref[...] = jnp.concatenate(ds_prev, axis=1) + dst * f["cde"]
        dcs = dcs + dcst.T
        da = _dot_hi(lt_ref[...].T, dcs)
        dxdt = jnp.concatenate(dxdt, axis=1)
        a_neg = f["a_neg"]
        ddt = da * a_neg + _dot_hi(dxdt * xs, ext)
        dal_ref[...] += jnp.sum(da * dt, axis=0, keepdims=True) * a_neg
        dxs = dxs + dxdt * f["dte"]
        ddtraw = jnp.where(lane < SSD_HEADS, ddt * _sigmoid(f["dtpre"]), 0.0)
        ddtb_ref[...] += jnp.sum(ddtraw, axis=0, keepdims=True)
        ddt_ref[...] = ddtraw.astype(MXU)
        dxa = jnp.concatenate([dxs, dbm[0], dbm[1], dcm[0], dcm[1]], axis=1)
        sig_c = f["sig_c"]
        dconv = dxa * sig_c * (1.0 + f["conv"] * (1.0 - sig_c))
        dcb_ref[...] += jnp.sum(dconv, axis=0, keepdims=True)
        for k in range(4):
            dcw_ref[k:k + 1, :] += jnp.sum(dconv * f["shifts"][3 - k], axis=0, keepdims=True)
        dnext = dnext_ref[...]
        dxbc = cw[3:4] * dconv
        for j in (1, 2, 3):
            dxbc = dxbc + cw[3 - j:4 - j] * _shift_up(dconv, dnext, j)
        dnext_ref[...] = dconv[0:8]
        dssd_ref[:, 0:1024] = dz.astype(MXU)
        dssd_ref[:, 1024:2560] = dxbc.astype(MXU)

    return pl.pallas_call(
        body, grid=(nb, nch), name="ssd_bwd",
        in_specs=in_specs,
        out_specs=[pl.BlockSpec((CH, 2560), lambda b, c: (tok(b, c), 0)),
                   pl.BlockSpec((CH, 128), lambda b, c: (tok(b, c), 0)),
                   _const((8, CONV_CH)), _const((1, CONV_CH)), _const((1, 128)), _const((1, 128)), _const((1, 128)),
                   _const((1, 1024))],
        out_shape=[_sds((T, 2560), MXU), _sds((T, 128), MXU), _sds((8, CONV_CH), F32), _sds((1, CONV_CH), F32),
                   _sds((1, 128), F32), _sds((1, 128), F32), _sds((1, 128), F32), _sds((1, 1024), F32)],
        scratch_shapes=[pltpu.VMEM((128, 1024), F32), pltpu.VMEM((8, CONV_CH), F32)],
        compiler_params=_cp(2),
    )(proj, proj, proj, dtraw, convw, convb, dtb, alog, dskip, ng, ex, ltri, ext, sall, dcat)


def _inproj_bwd(duv, dssd, ddt, wm, wdt, dh1, x, g, tm):
    T = x.shape[0]

    def body(duv_ref, dssd_ref, ddt_ref, wm_ref, wdt_ref, dh1_ref, x_ref, g_ref, dx_ref, dg_ref):
        @pl.when(pl.program_id(0) == 0)
        def _():
            dg_ref[...] = jnp.zeros_like(dg_ref)

        dxn = (_dot_nt(duv_ref[...], wm_ref[:, 0:2048]) + _dot_nt(dssd_ref[...], wm_ref[:, 2048:N_MAIN])
               + _dot_nt(ddt_ref[...], wdt_ref[...]))
        xh, r = _rms(x_ref[...])
        dg_ref[...] += jnp.sum(dxn * xh, axis=0, keepdims=True)
        dx_ref[...] = dh1_ref[...] + _rms_bwd(dxn, xh, r, g_ref[...])

    return pl.pallas_call(
        body, grid=(T // tm,), name="inproj_bwd",
        in_specs=[_rows(tm, 2048), _rows(tm, 2560), _rows(tm, 128), _const((D, N_MAIN)), _const((D, 128)),
                  _rows(tm, D), _rows(tm, D), _const((1, D))],
        out_specs=[_rows(tm, D), _const((1, D))],
        out_shape=[_sds((T, D), F32), _sds((1, D), F32)],
        compiler_params=_cp(),
    )(duv, dssd, ddt, wm, wdt, dh1, x, g)


def _matmul_tn(a, b, name, a_fn=None):
    T, M = a.shape
    N = b.shape[1]
    tm = min(M, 1024)
    tn = 1280 if N == 2560 else min(N, 1024)
    tk = min(T, 512)

    def body(a_ref, b_ref, o_ref):
        @pl.when(pl.program_id(2) == 0)
        def _():
            o_ref[...] = jnp.zeros_like(o_ref)

        av = a_ref[...]
        if a_fn is not None:
            av = a_fn(av)
        o_ref[...] += _dot_tn(av, b_ref[...])

    return pl.pallas_call(
        body, grid=(M // tm, N // tn, T // tk), name=name,
        in_specs=[pl.BlockSpec((tk, tm), lambda i, j, k: (k, i)), pl.BlockSpec((tk, tn), lambda i, j, k: (k, j))],
        out_specs=pl.BlockSpec((tm, tn), lambda i, j, k: (i, j)),
        out_shape=_sds((M, N), F32),
        compiler_params=_cp(3),
    )(a, b)


def _adamw_vals(w, g, m, v):
    m = B1 * m + (1.0 - B1) * g
    v = B2 * v + (1.0 - B2) * (g * g)
    m_hat = m / (1.0 - B1 ** STEP)
    v_hat = v / (1.0 - B2 ** STEP)
    return -LR * (m_hat / (jnp.sqrt(v_hat) + ADAM_EPS) + WD * w), m, v


def _adamw(w, g, m, v, name):
    R, C = w.shape
    tr = 256 if R % 256 == 0 else R

    def body(w_ref, g_ref, m_ref, v_ref, d_ref, mo_ref, vo_ref):
        d_ref[...], mo_ref[...], vo_ref[...] = _adamw_vals(w_ref[...], g_ref[...], m_ref[...], v_ref[...])

    spec = _rows(tr, C)
    return pl.pallas_call(
        body, grid=(R // tr,), name=name,
        in_specs=[spec] * 4, out_specs=[spec] * 3, out_shape=[_sds((R, C), F32)] * 3,
        compiler_params=_cp(),
    )(w, g, m, v)


def _sum_slots(big, small):
    nd, rows, _ = big.shape
    rs = small.shape[1]
    tr = min(rows, 2048)

    def body(big_ref, small_ref, tot_ref, stot_ref):
        acc = big_ref[0].astype(F32)
        for d in range(1, nd):
            acc = acc + big_ref[d].astype(F32)
        tot_ref[...] = acc

        @pl.when(pl.program_id(0) == 0)
        def _():
            s = small_ref[0]
            for d in range(1, nd):
                s = s + small_ref[d]
            stot_ref[...] = s

    return pl.pallas_call(
        body, grid=(rows // tr,), name="sum_slots",
        in_specs=[pl.BlockSpec((nd, tr, 128), lambda i: (0, i, 0)), _const((nd, rs, 128))],
        out_specs=[_rows(tr, 128), _const((rs, 128))],
        out_shape=[_sds((rows, 128), F32), _sds((rs, 128), F32)],
        compiler_params=_cp(),
    )(big, small)


_ANY = pl.BlockSpec(memory_space=pl.ANY)


def _weight_gather(pack):
    def body(pack_ref, out_ref, send_sems, recv_sems, fsend_sems, frecv_sems, local_sem):
        x, y, c = lax.axis_index("x"), lax.axis_index("y"), lax.axis_index("c")
        me = 2 * x + y
        half = pl.ds(pl.multiple_of(c * HALF_ROWS, HALF_ROWS), HALF_ROWS)
        other = pl.ds(pl.multiple_of((1 - c) * HALF_ROWS, HALF_ROWS), HALF_ROWS)
        flips = [(1, 0), (0, 1), (1, 1)]
        mine = pltpu.make_async_copy(pack_ref, out_ref.at[me], local_sem)
        mine.start()
        sends = []
        for k, (fx, fy) in enumerate(flips):
            cp = pltpu.make_async_remote_copy(
                src_ref=pack_ref.at[half], dst_ref=out_ref.at[me, half],
                send_sem=send_sems.at[k], recv_sem=recv_sems.at[k],
                device_id=(x ^ fx, y ^ fy, c), device_id_type=MESH)
            cp.start()
            sends.append(cp)
        forwards = []
        for k, (fx, fy) in enumerate(flips):
            src = 2 * (x ^ fx) + (y ^ fy)
            landed = out_ref.at[src, half]
            pltpu.make_async_remote_copy(
                src_ref=landed, dst_ref=landed, send_sem=send_sems.at[k], recv_sem=recv_sems.at[k],
                device_id=(x ^ fx, y ^ fy, c), device_id_type=MESH).wait_recv()
            fw = pltpu.make_async_remote_copy(
                src_ref=landed, dst_ref=landed, send_sem=fsend_sems.at[k], recv_sem=frecv_sems.at[k],
                device_id=(x, y, 1 - c), device_id_type=MESH)
            fw.start()
            forwards.append(fw)
        for k, (fx, fy) in enumerate(flips):
            src = 2 * (x ^ fx) + (y ^ fy)
            theirs = out_ref.at[src, other]
            pltpu.make_async_remote_copy(
                src_ref=theirs, dst_ref=theirs, send_sem=fsend_sems.at[k], recv_sem=frecv_sems.at[k],
                device_id=(x, y, 1 - c), device_id_type=MESH).wait_recv()
        for cp in sends + forwards:
            cp.wait_send()
        mine.wait()

    return pl.pallas_call(
        body, name="weight_gather",
        in_specs=[_ANY], out_specs=_ANY,
        out_shape=_sds((4, PACK_ROWS, 128), BF16),
        scratch_shapes=[pltpu.SemaphoreType.DMA((3,)), pltpu.SemaphoreType.DMA((3,)), pltpu.SemaphoreType.DMA((3,)),
                        pltpu.SemaphoreType.DMA((3,)), pltpu.SemaphoreType.DMA],
    )(pack)


_FLIPS7 = [(fx, fy, fc) for fx in (0, 1) for fy in (0, 1) for fc in (0, 1)][1:]


def _grad_exchange(gpack, small):
    rs = small.shape[0]

    def body(g_ref, s_ref, big_ref, sm_ref, send_sems, recv_sems, ssend_sems, srecv_sems, local_sems):
        x, y, c = lax.axis_index("x"), lax.axis_index("y"), lax.axis_index("c")
        slot = 4 * x + 2 * y + c

        def piece(px, py, pc):
            return g_ref.at[2 * px + py, pl.ds(pl.multiple_of(pc * HALF_ROWS, HALF_ROWS), HALF_ROWS)]

        own = pltpu.make_async_copy(piece(x, y, c), big_ref.at[slot], local_sems.at[0])
        own_small = pltpu.make_async_copy(s_ref, sm_ref.at[slot], local_sems.at[1])
        own.start()
        own_small.start()
        copies = []
        for k, (fx, fy, fc) in enumerate(_FLIPS7):
            peer = (x ^ fx, y ^ fy, c ^ fc)
            cp = pltpu.make_async_remote_copy(
                src_ref=piece(*peer), dst_ref=big_ref.at[slot], send_sem=send_sems.at[k], recv_sem=recv_sems.at[k],
                device_id=peer, device_id_type=MESH)
            cs = pltpu.make_async_remote_copy(
                src_ref=s_ref, dst_ref=sm_ref.at[slot], send_sem=ssend_sems.at[k], recv_sem=srecv_sems.at[k],
                device_id=peer, device_id_type=MESH)
            cp.start()
            cs.start()
            copies += [cp, cs]
        for k, (fx, fy, fc) in enumerate(_FLIPS7):
            peer = (x ^ fx, y ^ fy, c ^ fc)
            pslot = 4 * peer[0] + 2 * peer[1] + peer[2]
            pltpu.make_async_remote_copy(
                src_ref=big_ref.at[pslot], dst_ref=big_ref.at[pslot], send_sem=send_sems.at[k],
                recv_sem=recv_sems.at[k], device_id=peer, device_id_type=MESH).wait_recv()
            pltpu.make_async_remote_copy(
                src_ref=sm_ref.at[pslot], dst_ref=sm_ref.at[pslot], send_sem=ssend_sems.at[k],
                recv_sem=srecv_sems.at[k], device_id=peer, device_id_type=MESH).wait_recv()
        for cp in copies:
            cp.wait_send()
        own.wait()
        own_small.wait()

    return pl.pallas_call(
        body, name="grad_exchange",
        in_specs=[_ANY, _ANY], out_specs=[_ANY, _ANY],
        out_shape=[_sds((8, HALF_ROWS, 128), BF16), _sds((8, rs, 128), F32)],
        scratch_shapes=[pltpu.SemaphoreType.DMA((7,)), pltpu.SemaphoreType.DMA((7,)), pltpu.SemaphoreType.DMA((7,)),
                        pltpu.SemaphoreType.DMA((7,)), pltpu.SemaphoreType.DMA((2,))],
    )(gpack, small)


def _sibling_exchange(tot):
    def body(t_ref, out_ref, send_sem, recv_sem, local_sem):
        x, y, c = lax.axis_index("x"), lax.axis_index("y"), lax.axis_index("c")
        mine = pltpu.make_async_copy(t_ref, out_ref.at[c], local_sem)
        mine.start()
        cp = pltpu.make_async_remote_copy(
            src_ref=t_ref, dst_ref=out_ref.at[c], send_sem=send_sem, recv_sem=recv_sem,
            device_id=(x, y, 1 - c), device_id_type=MESH)
        cp.start()
        pltpu.make_async_remote_copy(
            src_ref=out_ref.at[1 - c], dst_ref=out_ref.at[1 - c], send_sem=send_sem, recv_sem=recv_sem,
            device_id=(x, y, 1 - c), device_id_type=MESH).wait_recv()
        cp.wait_send()
        mine.wait()

    return pl.pallas_call(
        body, name="sibling_exchange",
        in_specs=[_ANY], out_specs=_ANY,
        out_shape=_sds((2, HALF_ROWS, 128), F32),
        scratch_shapes=[pltpu.SemaphoreType.DMA, pltpu.SemaphoreType.DMA, pltpu.SemaphoreType.DMA],
    )(tot)


_BIG = [("w_in", (1024, 1156)), ("w_out", (512, 1024)), ("w_ff1", (1024, 1024)), ("w_ff2", (1024, 1024)),
        ("w_ple_gate", (256, 1024)), ("w_ple_proj", (256, 256))]
_SMALL = [("norm_mix_g", (1, 1024)), ("gm_v_norm_g", (1, 1024)), ("gm_ws", (1, 8, 128, 128)), ("gm_bs", (1, 8, 128)),
          ("gm_out_norm_g", (1, 1024)), ("ssd_conv_w", (1, 4, 1536)), ("ssd_conv_b", (1, 1536)),
          ("ssd_dt_bias", (1, 16)), ("ssd_a_log", (1, 16)), ("ssd_d", (1, 16)), ("ssd_norm_g", (1, 1024)),
          ("norm_mlp_g", (1, 1024)), ("ple_norm_g", (1, 1024)), ("final_norm_g", (1024,))]


def _rows128(a):
    flat = a.reshape(-1)
    rows = -(-flat.shape[0] // 1024) * 8
    return jnp.pad(flat, (0, rows * 128 - flat.shape[0])).reshape(rows, 128)


def _pack_rows(parts, rows):
    flat = jnp.concatenate([p.reshape(-1) for p in parts])
    return jnp.pad(flat, (0, rows * 128 - flat.shape[0])).reshape(rows, 128)


def _unpack(flat, shapes):
    out, o = [], 0
    for shp in shapes:
        n = 1
        for s in shp:
            n *= s
        out.append(flat[o:o + n].reshape(shp))
        o += n
    return out


def _pad_lanes(v, n=128):
    v = v.reshape(1, -1)
    return jnp.pad(v, ((0, 0), (0, n - v.shape[1])))


def _local_step(x, p, tgt, sm, wfull, nb, tm):
    T = x.shape[0]
    wm, wdt, wo, w1, w2, wg, wp = (wfull[k] for k in ("wm", "wdt", "wo", "w1", "w2", "wg", "wp"))
    g_mix, gv, gout = sm["norm_mix_g"].reshape(1, D), sm["gm_v_norm_g"].reshape(1, D), sm["gm_out_norm_g"].reshape(1, D)
    ws = sm["gm_ws"].reshape(GM_HEADS, CH, CH)
    bst = jnp.pad(sm["gm_bs"].reshape(GM_HEADS, CH).T, ((0, 0), (0, 128 - GM_HEADS)))
    convw = jnp.pad(wfull["conv_w"], ((0, 4), (0, 0)))
    convb = sm["ssd_conv_b"].reshape(1, CONV_CH)
    dtb, alog, dskip = _pad_lanes(sm["ssd_dt_bias"]), _pad_lanes(sm["ssd_a_log"]), _pad_lanes(sm["ssd_d"])
    ng, g_mlp, g_ple = sm["ssd_norm_g"].reshape(1, D), sm["norm_mlp_g"].reshape(1, D), sm["ple_norm_g"].reshape(1, D)
    gf = sm["final_norm_g"].reshape(1, D)
    head_of_lane = lax.broadcasted_iota(jnp.int32, (128, 1024), 1) // SSD_P
    ex = (lax.broadcasted_iota(jnp.int32, (128, 1024), 0) == head_of_lane).astype(F32)
    ext = ex.T
    ltri = (lax.broadcasted_iota(jnp.int32, (CH, CH), 0) >= lax.broadcasted_iota(jnp.int32, (CH, CH), 1)).astype(F32)

    proj, dtraw, xn = _inproj(x, g_mix, wm, wdt, tm)
    ya = _gmlp_fwd(proj, gv, ws, bst, gout)
    yb, sall = _ssd_fwd(proj, dtraw, convw, convb, dtb, alog, dskip, ng, ex, ltri, nb)
    h1, hn = _outproj(ya, yb, wo, x, g_mlp, tm)
    hid = _ff1(hn, w1, tm)
    h2, hp = _ff2(hid, w2, h1, g_ple, tm)
    dh3, dgl, dpe, loss, d_gf = _tail(h2, hp, p, tgt, wg, wp, gf, tm)

    dh2, dh2b, d_gple = _ple_bwd(dgl, wg, dh3, h2, g_ple, tm)
    dpre = _ff2_bwd(dh2b, w2, hid, tm)
    dh1, dh1b, d_gmlp = _ff1_bwd(dpre, w1, dh2, h1, g_mlp, tm)
    dcat = _outproj_bwd(dh1b, wo, tm)
    duv, d_gv, d_ws, d_bst, d_gout = _gmlp_bwd(proj, dcat, gv, ws, bst, gout)
    dssd, ddt, d_cw, d_cb, d_dtb, d_al, d_ds, d_ng = _ssd_bwd(
        proj, dtraw, sall, dcat, convw, convb, dtb, alog, dskip, ng, ex, ltri, ext, nb)
    dx, d_gmix = _inproj_bwd(duv, dssd, ddt, wm, wdt, dh1, x, g_mix, tm)

    d_wp = _matmul_tn(p, dpe, "dw_ple_proj", a_fn=lambda a: a.astype(MXU))
    d_wg = _matmul_tn(hp, dgl, "dw_ple_gate")
    d_w2 = _matmul_tn(hid, dh2b, "dw_ff2", a_fn=_sq)
    d_w1 = _matmul_tn(hn, dpre, "dw_ff1")
    d_wo = jnp.concatenate([_matmul_tn(ya, dh1b, "dw_out_a"), _matmul_tn(yb, dh1b, "dw_out_b")], axis=0)
    d_win = jnp.concatenate([_matmul_tn(xn, duv, "dw_in_uv"), _matmul_tn(xn, dssd, "dw_in_ssd"),
                             _matmul_tn(xn, ddt, "dw_in_dt")[:, :16]], axis=1)
    grads = {
        "w_in": d_win, "w_out": d_wo, "w_ff1": d_w1, "w_ff2": d_w2, "w_ple_gate": d_wg, "w_ple_proj": d_wp,
        "norm_mix_g": d_gmix, "gm_v_norm_g": d_gv, "gm_ws": d_ws, "gm_bs": d_bst[:, :GM_HEADS].T,
        "gm_out_norm_g": d_gout, "ssd_conv_w": d_cw[0:4], "ssd_conv_b": d_cb, "ssd_dt_bias": d_dtb[:, :16],
        "ssd_a_log": d_al[:, :16], "ssd_d": d_ds[:, :16], "ssd_norm_g": d_ng, "norm_mlp_g": d_gmlp,
        "ple_norm_g": d_gple, "final_norm_g": d_gf,
    }
    return dx, loss, grads


def kernel(x, p, norm_mix_g, w_in, gm_v_norm_g, gm_ws, gm_bs, gm_out_norm_g, ssd_conv_w, ssd_conv_b, ssd_dt_bias, ssd_a_log, ssd_d, ssd_norm_g, w_out, norm_mlp_g, w_ff1, w_ff2, ple_norm_g, w_ple_gate, w_ple_proj, final_norm_g, loss_target, m_norm_mix_g, m_w_in, m_gm_v_norm_g, m_gm_ws, m_gm_bs, m_gm_out_norm_g, m_ssd_conv_w, m_ssd_conv_b, m_ssd_dt_bias, m_ssd_a_log, m_ssd_d, m_ssd_norm_g, m_w_out, m_norm_mlp_g, m_w_ff1, m_w_ff2, m_ple_norm_g, m_w_ple_gate, m_w_ple_proj, m_final_norm_g, v_norm_mix_g, v_w_in, v_gm_v_norm_g, v_gm_ws, v_gm_bs, v_gm_out_norm_g, v_ssd_conv_w, v_ssd_conv_b, v_ssd_dt_bias, v_ssd_a_log, v_ssd_d, v_ssd_norm_g, v_w_out, v_norm_mlp_g, v_w_ff1, v_w_ff2, v_ple_norm_g, v_w_ple_gate, v_w_ple_proj, v_final_norm_g):
    a = dict(locals())
    order = ["norm_mix_g", "w_in", "gm_v_norm_g", "gm_ws", "gm_bs", "gm_out_norm_g", "ssd_conv_w", "ssd_conv_b",
             "ssd_dt_bias", "ssd_a_log", "ssd_d", "ssd_norm_g", "w_out", "norm_mlp_g", "w_ff1", "w_ff2", "ple_norm_g",
             "w_ple_gate", "w_ple_proj", "final_norm_g"]
    chip = 2 * lax.axis_index("x") + lax.axis_index("y")
    nb, S = x.shape[0], x.shape[1]
    T = nb * S

    big_local = [a[n].reshape(shp) for n, shp in _BIG]
    conv_bits = lax.bitcast_convert_type(ssd_conv_w.reshape(4, 384), BF16)
    pack = _pack_rows([w.astype(BF16) for w in big_local] + [conv_bits], PACK_ROWS)
    gathered = _weight_gather(pack).reshape(4, PACK_ROWS * 128)
    per_chip = [_unpack(gathered[k], [shp for _, shp in _BIG] + [(4, 384, 2)]) for k in range(4)]
    cat = lambda i, axis: jnp.concatenate([per_chip[k][i] for k in range(4)], axis=axis)
    w_in_full = cat(0, 1)
    wfull = {
        "wm": w_in_full[:, :N_MAIN], "wdt": jnp.pad(w_in_full[:, N_MAIN:], ((0, 0), (0, 128 - 16))),
        "wo": cat(1, 0), "w1": cat(2, 1), "w2": cat(3, 0), "wg": cat(4, 0), "wp": cat(5, 1),
        "conv_w": jnp.concatenate([lax.bitcast_convert_type(per_chip[k][6], F32) for k in range(4)], axis=1),
    }
    sm = {n: a[n] for n, _ in _SMALL if n != "ssd_conv_w"}

    dx, loss, grads = _local_step(x.reshape(T, D), p.reshape(T, DPLE), loss_target.reshape(T, D), sm, wfull, nb, 256)

    gb = grads
    slabs = []
    for k in range(4):
        slabs.append(_pack_rows([
            gb["w_in"][:, 1156 * k:1156 * (k + 1)], gb["w_out"][512 * k:512 * (k + 1)],
            gb["w_ff1"][:, 1024 * k:1024 * (k + 1)], gb["w_ff2"][1024 * k:1024 * (k + 1)],
            gb["w_ple_gate"][256 * k:256 * (k + 1)], gb["w_ple_proj"][:, 256 * k:256 * (k + 1)]], PACK_ROWS).astype(BF16))
    small_parts = [_rows128(gb[n]) for n, _ in _SMALL] + [_rows128(loss[0:1, 0:1])]
    small_rows = [s.shape[0] for s in small_parts]
    slots_big, slots_small = _grad_exchange(jnp.stack(slabs), jnp.concatenate(small_parts, axis=0))
    tot_half, small_tot = _sum_slots(slots_big, slots_small)
    tot = _sibling_exchange(tot_half).reshape(-1)

    g_out = dict(zip([n for n, _ in _BIG], _unpack(tot, [shp for _, shp in _BIG])))
    o = 0
    for (n, shp), r in zip(_SMALL + [("loss", ())], small_rows):
        cnt = 1
        for s in shp:
            cnt *= s
        g_out[n] = small_tot[o:o + r].reshape(-1)[:cnt].reshape(shp)
        o += r
    g_out["ssd_conv_w"] = lax.dynamic_slice(g_out["ssd_conv_w"], (0, 0, chip * 384), (1, 4, 384))

    delta, new_m, new_v = {}, {}, {}
    for n, shp in _BIG:
        d_, m_, v_ = _adamw(a[n].reshape(shp), g_out[n], a["m_" + n].reshape(shp), a["v_" + n].reshape(shp), "adamw_" + n)
        delta[n], new_m[n], new_v[n] = d_.reshape(a[n].shape), m_.reshape(a[n].shape), v_.reshape(a[n].shape)
        g_out[n] = g_out[n].reshape(a[n].shape)
    small_names = [n for n, _ in _SMALL]
    packs = [jnp.concatenate([_rows128(src(n)) for n in small_names], axis=0)
             for src in (lambda n: a[n], lambda n: g_out[n], lambda n: a["m_" + n], lambda n: a["v_" + n])]
    outs = _adamw(*packs, "adamw_small")
    o = 0
    for n in small_names:
        r = _rows128(a[n]).shape[0]
        cnt = a[n].size
        for dst, src in zip((delta, new_m, new_v), outs):
            dst[n] = src[o:o + r].reshape(-1)[:cnt].reshape(a[n].shape)
        o += r
    return (g_out["loss"], dx.reshape(x.shape), *[g_out[n] for n in order], *[delta[n] for n in order],
            *[new_m[n] for n in order], *[new_v[n] for n in order])
```

```python
import jax
import jax.numpy as jnp
from jax import lax
from jax.experimental import pallas as pl
from jax.experimental.pallas import tpu as pltpu

F32 = jnp.float32
BF16 = jnp.bfloat16
MXU = jnp.bfloat16
GRAD = jnp.bfloat16

D = 1024
CH = 128
GM_HEADS = 8
SSD_HEADS = 16
SSD_P = 64
CONV_CH = 1536
N_MAIN = 4608
DFF = 4096
DPLE = 256
EPS = 1e-6
NEG = -1e30

LR, B1, B2, ADAM_EPS, WD, STEP = 0.001, 0.9, 0.999, 1e-08, 0.01, 10

VMEM_LIMIT = 56 * 1024 * 1024
MESH = pl.DeviceIdType.MESH

INV_SQRT2 = 0.7071067811865476
INV_SQRT_2PI = 0.3989422804014327


def _cp(n_axes=1):
    return pltpu.CompilerParams(dimension_semantics=("arbitrary",) * n_axes, vmem_limit_bytes=VMEM_LIMIT)


def _dot(a, b):
    return jnp.dot(a, b, preferred_element_type=F32)


def _dot_nt(a, b):
    return lax.dot_general(a, b, (((1,), (1,)), ((), ())), preferred_element_type=F32)


def _dot_tn(a, b):
    return lax.dot_general(a, b, (((0,), (0,)), ((), ())), preferred_element_type=F32)


def _dot_hi(a, b):
    return jnp.dot(a, b, preferred_element_type=F32, precision=lax.Precision.HIGHEST)


def _rows(tm, n, j=0):
    return pl.BlockSpec((tm, n), lambda i: (i, j))


def _const(shape):
    nd = len(shape)
    return pl.BlockSpec(shape, lambda *_: (0,) * nd)


def _sds(shape, dtype):
    return jax.ShapeDtypeStruct(shape, dtype)


def _rms(x):
    r = lax.rsqrt(jnp.mean(x * x, axis=-1, keepdims=True) + EPS)
    return x * r, r


def _rms_bwd(dy, xhat, r, g):
    dyg = dy * g
    return r * (dyg - xhat * jnp.mean(dyg * xhat, axis=-1, keepdims=True))


def _sigmoid(x):
    return 1.0 / (1.0 + jnp.exp(-x))


def _gelu(x):
    cdf = 0.5 * (1.0 + lax.erf(x * INV_SQRT2))
    pdf = jnp.exp(-0.5 * x * x) * INV_SQRT_2PI
    return x * cdf, cdf + x * pdf


def _softplus(x):
    e = jnp.exp(-jnp.abs(x))
    u = 1.0 + e
    log1p = jnp.where(u == 1.0, e, jnp.log(u) * e / (u - 1.0))
    return jnp.maximum(x, 0.0) + log1p


def _inproj(x, g, wm, wdt, tm):
    T = x.shape[0]

    def body(x_ref, g_ref, wm_ref, wdt_ref, proj_ref, dt_ref, xn_ref):
        xh, _ = _rms(x_ref[...])
        xn = (xh * g_ref[...]).astype(MXU)
        xn_ref[...] = xn
        for n in range(N_MAIN // 512):
            proj_ref[:, n * 512:(n + 1) * 512] = _dot(xn, wm_ref[:, n * 512:(n + 1) * 512])
        dt_ref[...] = _dot(xn, wdt_ref[...])

    return pl.pallas_call(
        body, grid=(T // tm,), name="inproj",
        in_specs=[_rows(tm, D), _const((1, D)), _const((D, N_MAIN)), _const((D, 128))],
        out_specs=[_rows(tm, N_MAIN), _rows(tm, 128), _rows(tm, D)],
        out_shape=[_sds((T, N_MAIN), F32), _sds((T, 128), F32), _sds((T, D), MXU)],
        compiler_params=_cp(),
    )(x, g, wm, wdt)


def _gmlp_fwd_vals(u, v, gv, ws_ref, bst, gout):
    ug, dug = _gelu(u)
    vg, dvg = _gelu(v)
    row = lax.broadcasted_iota(jnp.int32, (CH, CH), 0)
    col = lax.broadcasted_iota(jnp.int32, (CH, CH), 1)
    tril = row >= col
    ys, heads = [], []
    for h in range(GM_HEADS):
        sl = slice(h * 128, (h + 1) * 128)
        vhat, rv = _rms(vg[:, sl])
        vn = (vhat * gv[:, sl]).astype(MXU)
        wt = jnp.where(tril, ws_ref[h], 0.0)
        mixed = _dot(wt.astype(MXU), vn) + bst[:, h:h + 1]
        ys.append(ug[:, sl] * mixed)
        heads.append((vhat, rv, vn, wt, mixed))
    y = jnp.concatenate(ys, axis=1)
    yhat, ry = _rms(y)
    return dict(ug=ug, dug=dug, dvg=dvg, heads=heads, yhat=yhat, ry=ry, tril=tril, out=yhat * gout)


def _gmlp_fwd(proj, gv, ws, bst, gout):
    T = proj.shape[0]

    def body(u_ref, v_ref, gv_ref, ws_ref, bst_ref, gout_ref, ya_ref):
        f = _gmlp_fwd_vals(u_ref[...], v_ref[...], gv_ref[...], ws_ref, bst_ref[...], gout_ref[...])
        ya_ref[...] = f["out"].astype(MXU)

    return pl.pallas_call(
        body, grid=(T // CH,), name="gmlp_fwd",
        in_specs=[_rows(CH, 1024, 0), _rows(CH, 1024, 1), _const((1, 1024)), _const((GM_HEADS, CH, CH)),
                  _const((CH, 128)), _const((1, 1024))],
        out_specs=_rows(CH, 1024, 0),
        out_shape=_sds((T, 2048), MXU),
        compiler_params=_cp(),
    )(proj, proj, gv, ws, bst, gout)


def _shift_down(cur, halo, j):
    sh = pltpu.roll(cur, j, 0)
    row8 = lax.broadcasted_iota(jnp.int32, (8, cur.shape[1]), 0)
    top = jnp.where(row8 < j, pltpu.roll(halo, j, 0), sh[0:8])
    return jnp.concatenate([top, sh[8:]], axis=0)


def _shift_up(cur, halo, j):
    n = cur.shape[0]
    sh = pltpu.roll(cur, n - j, 0)
    row8 = lax.broadcasted_iota(jnp.int32, (8, cur.shape[1]), 0)
    bot = jnp.where(row8 + j >= 8, pltpu.roll(halo, 8 - j, 0), sh[n - 8:n])
    return jnp.concatenate([sh[0:n - 8], bot], axis=0)


def _ssd_fwd_vals(z, xbc, halo, dtraw, convw, convb, dtb, alog, dskip, ng, ex, ltri, s_prev):
    shifts = [xbc] + [_shift_down(xbc, halo, j) for j in (1, 2, 3)]
    conv = convb + convw[3:4] * shifts[0] + convw[2:3] * shifts[1] + convw[1:2] * shifts[2] + convw[0:1] * shifts[3]
    sig_c = _sigmoid(conv)
    xa = conv * sig_c
    xs = xa[:, :1024]
    bm = [xa[:, 1024:1152], xa[:, 1152:1280]]
    cm = [xa[:, 1280:1408], xa[:, 1408:1536]]
    dtpre = dtraw + dtb
    dt = _softplus(dtpre)
    a_neg = -jnp.exp(alog)
    cs = _dot_hi(ltri, dt * a_neg)
    cst = cs.T
    last = cs[CH - 1:CH]
    ecs = jnp.exp(cs)
    dec = jnp.exp(last - cs)
    dte = _dot_hi(dt, ex)
    ecse = _dot_hi(ecs, ex)
    dece = _dot_hi(dec, ex)
    cde = ecse[CH - 1:CH]
    de = _dot_hi(jnp.broadcast_to(dskip, (8, 128)), ex)[0:1]
    xdt = xs * dte
    row = lax.broadcasted_iota(jnp.int32, (CH, CH), 0)
    col = lax.broadcasted_iota(jnp.int32, (CH, CH), 1)
    tril = row >= col
    lo = col < SSD_P
    bmb = [b.astype(MXU) for b in bm]
    cmb = [c.astype(MXU) for c in cm]
    mg = [_dot_nt(cmb[g], bmb[g]) for g in range(2)]
    yd, lms, whs = [], [], []
    for q in range(8):
        g = q // 4
        xq = xdt[:, q * 128:(q + 1) * 128]
        acc = None
        for hh in range(2):
            h = 2 * q + hh
            seg = cs[:, h:h + 1] - cst[h:h + 1, :]
            lm = jnp.exp(jnp.where(tril, seg, NEG))
            wh = (mg[g] * lm).astype(MXU)
            xm = jnp.where(lo if hh == 0 else ~lo, xq, 0.0).astype(MXU)
            part = _dot(wh, xm)
            acc = part if acc is None else acc + part
            lms.append(lm)
            whs.append(wh)
        yd.append(acc)
    yd = jnp.concatenate(yd, axis=1)
    sb = s_prev.astype(MXU)
    yo = jnp.concatenate([_dot(cmb[g], sb[:, g * 512:(g + 1) * 512]) for g in range(2)], axis=1) * ecse
    xdec = (xdt * dece).astype(MXU)
    states = jnp.concatenate([_dot_tn(bmb[g], xdec[:, g * 512:(g + 1) * 512]) for g in range(2)], axis=1)
    s_next = s_prev * cde + states
    ypre = yd + yo + de * xs
    sig_z = _sigmoid(z)
    yg = ypre * z * sig_z
    outs, yhat, rr = [], [], []
    for g in range(2):
        sl = slice(g * 512, (g + 1) * 512)
        yh, r = _rms(yg[:, sl])
        yhat.append(yh)
        rr.append(r)
        outs.append(yh * ng[:, sl])
    return dict(shifts=shifts, conv=conv, sig_c=sig_c, xs=xs, bmb=bmb, cmb=cmb, dtpre=dtpre, dt=dt, a_neg=a_neg,
                cs=cs, last=last, ecs=ecs, dec=dec, dte=dte, ecse=ecse, dece=dece, cde=cde, de=de, xdt=xdt,
                mg=mg, lms=lms, whs=whs, lo=lo, yo=yo, sb=sb, xdec=xdec, s_next=s_next, ypre=ypre, sig_z=sig_z,
                yhat=yhat, rr=rr, out=jnp.concatenate(outs, axis=1))


def _ssd_specs(nch, rev):
    def tok(b, c):
        return b * nch + ((nch - 1 - c) if rev else c)

    return tok, [
        pl.BlockSpec((CH, 1024), lambda b, c: (tok(b, c), 2)),
        pl.BlockSpec((CH, CONV_CH), lambda b, c: (tok(b, c), 2)),
        pl.BlockSpec((8, CONV_CH), lambda b, c: (jnp.maximum(tok(b, c) * (CH // 8) - 1, 0), 2)),
        pl.BlockSpec((CH, 128), lambda b, c: (tok(b, c), 0)),
        _const((8, CONV_CH)), _const((1, CONV_CH)), _const((1, 128)), _const((1, 128)), _const((1, 128)),
        _const((1, 1024)), _const((128, 1024)), _const((CH, CH)),
    ]


def _ssd_fwd(proj, dtraw, cat, convw, convb, dtb, alog, dskip, ng, ex, ltri, nb):
    T = proj.shape[0]
    nch = T // CH // nb
    tok, in_specs = _ssd_specs(nch, rev=False)

    def body(z_ref, xbc_ref, halo_ref, dt_ref, cw_ref, cb_ref, dtb_ref, al_ref, ds_ref, ng_ref, ex_ref, lt_ref,
             cat_in_ref, yb_ref, sall_ref, s_ref):
        del cat_in_ref
        c = pl.program_id(1)

        @pl.when(c == 0)
        def _():
            s_ref[...] = jnp.zeros_like(s_ref)

        halo = jnp.where(c == 0, 0.0, halo_ref[...])
        s_prev = s_ref[...]
        sall_ref[0] = s_prev
        f = _ssd_fwd_vals(z_ref[...], xbc_ref[...], halo, dt_ref[...], cw_ref[...], cb_ref[...], dtb_ref[...],
                          al_ref[...], ds_ref[...], ng_ref[...], ex_ref[...], lt_ref[...], s_prev)
        s_ref[...] = f["s_next"]
        yb_ref[...] = f["out"].astype(MXU)

    return pl.pallas_call(
        body, grid=(nb, nch), name="ssd_fwd",
        in_specs=in_specs + [_ANY],
        out_specs=[pl.BlockSpec((CH, 1024), lambda b, c: (tok(b, c), 1)),
                   pl.BlockSpec((1, 128, 1024), lambda b, c: (tok(b, c), 0, 0))],
        out_shape=[_sds((T, 2048), MXU), _sds((T // CH, 128, 1024), F32)],
        scratch_shapes=[pltpu.VMEM((128, 1024), F32)],
        input_output_aliases={12: 0},
        compiler_params=_cp(2),
    )(proj, proj, proj, dtraw, convw, convb, dtb, alog, dskip, ng, ex, ltri, cat)


def _outproj(cat, wo, x, g, tm):
    T = x.shape[0]

    def body(cat_ref, wo_ref, x_ref, g_ref, h1_ref, hn_ref):
        h1 = x_ref[...] + _dot(cat_ref[...], wo_ref[...])
        h1_ref[...] = h1
        hn_ref[...] = (_rms(h1)[0] * g_ref[...]).astype(MXU)

    return pl.pallas_call(
        body, grid=(T // tm,), name="outproj",
        in_specs=[_rows(tm, 2048), _const((2048, D)), _rows(tm, D), _const((1, D))],
        out_specs=[_rows(tm, D), _rows(tm, D)],
        out_shape=[_sds((T, D), F32), _sds((T, D), MXU)],
        compiler_params=_cp(),
    )(cat, wo, x, g)


def _ff1(hn, w1, tm):
    T = hn.shape[0]

    def body(hn_ref, w1_ref, hid_ref):
        hn_v = hn_ref[...]
        for n in range(4):
            hid_ref[:, n * 1024:(n + 1) * 1024] = jnp.maximum(_dot(hn_v, w1_ref[n]), 0.0).astype(MXU)

    return pl.pallas_call(
        body, grid=(T // tm,), name="ff1",
        in_specs=[_rows(tm, D), _const((4, D, 1024))],
        out_specs=_rows(tm, DFF),
        out_shape=_sds((T, DFF), MXU),
        compiler_params=_cp(),
    )(hn, w1)


def _sq(hid):
    h = hid.astype(F32)
    return (h * h).astype(MXU)


def _ff2(hid, w2, h1, g, tm):
    T = h1.shape[0]

    def body(hid_ref, w2_ref, h1_ref, g_ref, h2_ref, hp_ref):
        h2 = h1_ref[...] + _dot(_sq(hid_ref[...]), w2_ref[...])
        h2_ref[...] = h2
        hp_ref[...] = (_rms(h2)[0] * g_ref[...]).astype(MXU)

    return pl.pallas_call(
        body, grid=(T // tm,), name="ff2",
        in_specs=[_rows(tm, DFF), _const((DFF, D)), _rows(tm, D), _const((1, D))],
        out_specs=[_rows(tm, D), _rows(tm, D)],
        out_shape=[_sds((T, D), F32), _sds((T, D), MXU)],
        compiler_params=_cp(),
    )(hid, w2, h1, g)


def _tail(h2, hp, p, tgt, wg, wp, gf, tm):
    T = h2.shape[0]

    def body(h2_ref, hp_ref, p_ref, t_ref, wg_ref, wp_ref, gf_ref, dh3_ref, dgl_ref, dpe_ref, loss_ref, dgf_ref):
        @pl.when(pl.program_id(0) == 0)
        def _():
            loss_ref[...] = jnp.zeros_like(loss_ref)
            dgf_ref[...] = jnp.zeros_like(dgf_ref)

        gate = _sigmoid(_dot(hp_ref[...], wg_ref[...]))
        pb = p_ref[...].astype(MXU)
        pe = jnp.concatenate([_dot(pb, wp_ref[k]) for k in range(4)], axis=1)
        h3 = h2_ref[...] + gate * pe
        hh, r = _rms(h3)
        gf = gf_ref[...]
        diff = hh * gf - t_ref[...]
        loss_ref[...] += 0.5 * jnp.sum(jnp.mean(diff * diff, axis=-1, keepdims=True))
        dout = diff * (1.0 / D)
        dgf_ref[...] += jnp.sum(dout * hh, axis=0, keepdims=True)
        dh3 = _rms_bwd(dout, hh, r, gf)
        dh3_ref[...] = dh3
        dgl_ref[...] = (dh3 * pe * gate * (1.0 - gate)).astype(MXU)
        dpe_ref[...] = (dh3 * gate).astype(MXU)

    return pl.pallas_call(
        body, grid=(T // tm,), name="tail",
        in_specs=[_rows(tm, D), _rows(tm, D), _rows(tm, DPLE), _rows(tm, D), _const((D, D)), _const((4, DPLE, 256)),
                  _const((1, D))],
        out_specs=[_rows(tm, D), _rows(tm, D), _rows(tm, D), _const((8, 128)), _const((1, D))],
        out_shape=[_sds((T, D), F32), _sds((T, D), MXU), _sds((T, D), MXU), _sds((8, 128), F32), _sds((1, D), F32)],
        compiler_params=_cp(),
    )(h2, hp, p, tgt, wg, wp, gf)


def _ple_bwd(dgl, wg, dh3, h2, g, tm):
    T = h2.shape[0]

    def body(dgl_ref, wg_ref, dh3_ref, h2_ref, g_ref, dh2_ref, dh2b_ref, dg_ref):
        @pl.when(pl.program_id(0) == 0)
        def _():
            dg_ref[...] = jnp.zeros_like(dg_ref)

        dhp = _dot_nt(dgl_ref[...], wg_ref[...])
        hh, r = _rms(h2_ref[...])
        dg_ref[...] += jnp.sum(dhp * hh, axis=0, keepdims=True)
        dh2 = dh3_ref[...] + _rms_bwd(dhp, hh, r, g_ref[...])
        dh2_ref[...] = dh2
        dh2b_ref[...] = dh2.astype(MXU)

    return pl.pallas_call(
        body, grid=(T // tm,), name="ple_bwd",
        in_specs=[_rows(tm, D), _const((D, D)), _rows(tm, D), _rows(tm, D), _const((1, D))],
        out_specs=[_rows(tm, D), _rows(tm, D), _const((1, D))],
        out_shape=[_sds((T, D), F32), _sds((T, D), MXU), _sds((1, D), F32)],
        compiler_params=_cp(),
    )(dgl, wg, dh3, h2, g)


def _ff2_bwd(dh2b, w2, hid, tm):
    T = hid.shape[0]

    def body(dh2b_ref, w2_ref, hid_ref, dpre_ref):
        d = dh2b_ref[...]
        for n in range(DFF // 1024):
            sl = slice(n * 1024, (n + 1) * 1024)
            da = _dot_nt(d, w2_ref[sl, :])
            dpre_ref[:, sl] = (2.0 * da * hid_ref[:, sl].astype(F32)).astype(MXU)

    return pl.pallas_call(
        body, grid=(T // tm,), name="ff2_bwd",
        in_specs=[_rows(tm, D), _const((DFF, D)), _rows(tm, DFF)],
        out_specs=_rows(tm, DFF),
        out_shape=_sds((T, DFF), MXU),
        compiler_params=_cp(),
    )(dh2b, w2, hid)


def _ff1_bwd(dpre, w1, dh2, h1, g, tm):
    T = h1.shape[0]

    def body(dpre_ref, w1_ref, dh2_ref, h1_ref, g_ref, dh1_ref, dh1b_ref, dg_ref):
        @pl.when(pl.program_id(0) == 0)
        def _():
            dg_ref[...] = jnp.zeros_like(dg_ref)

        dhn = _dot_nt(dpre_ref[:, 0:1024], w1_ref[0])
        for k in range(1, 4):
            dhn = dhn + _dot_nt(dpre_ref[:, k * 1024:(k + 1) * 1024], w1_ref[k])
        hh, r = _rms(h1_ref[...])
        dg_ref[...] += jnp.sum(dhn * hh, axis=0, keepdims=True)
        dh1 = dh2_ref[...] + _rms_bwd(dhn, hh, r, g_ref[...])
        dh1_ref[...] = dh1
        dh1b_ref[...] = dh1.astype(MXU)

    return pl.pallas_call(
        body, grid=(T // tm,), name="ff1_bwd",
        in_specs=[_rows(tm, DFF), _const((4, D, 1024)), _rows(tm, D), _rows(tm, D), _const((1, D))],
        out_specs=[_rows(tm, D), _rows(tm, D), _const((1, D))],
        out_shape=[_sds((T, D), F32), _sds((T, D), MXU), _sds((1, D), F32)],
        compiler_params=_cp(),
    )(dpre, w1, dh2, h1, g)


def _outproj_bwd(dh1b, wo, tm):
    T = dh1b.shape[0]

    def body(d_ref, wo_ref, dcat_ref):
        d = d_ref[...]
        dcat_ref[:, 0:1024] = _dot_nt(d, wo_ref[0:1024, :])
        dcat_ref[:, 1024:2048] = _dot_nt(d, wo_ref[1024:2048, :])

    return pl.pallas_call(
        body, grid=(T // tm,), name="outproj_bwd",
        in_specs=[_rows(tm, D), _const((2048, D))],
        out_specs=_rows(tm, 2048),
        out_shape=_sds((T, 2048), F32),
        compiler_params=_cp(),
    )(dh1b, wo)


def _gmlp_bwd(proj, dcat, gv, ws, bst, gout):
    T = proj.shape[0]

    def body(u_ref, v_ref, dya_ref, gv_ref, ws_ref, bst_ref, gout_ref, duv_ref, dgv_ref, dws_ref, dbst_ref, dgo_ref):
        @pl.when(pl.program_id(0) == 0)
        def _():
            dgv_ref[...] = jnp.zeros_like(dgv_ref)
            dws_ref[...] = jnp.zeros_like(dws_ref)
            dbst_ref[...] = jnp.zeros_like(dbst_ref)
            dgo_ref[...] = jnp.zeros_like(dgo_ref)

        gv = gv_ref[...]
        f = _gmlp_fwd_vals(u_ref[...], v_ref[...], gv, ws_ref, bst_ref[...], gout_ref[...])
        dya = dya_ref[...]
        dgo_ref[...] += jnp.sum(dya * f["yhat"], axis=0, keepdims=True)
        dy = _rms_bwd(dya, f["yhat"], f["ry"], gout_ref[...])
        lane = lax.broadcasted_iota(jnp.int32, (CH, 128), 1)
        dbs = jnp.zeros((CH, 128), F32)
        dug, dvg, dgvs = [], [], []
        for h in range(GM_HEADS):
            sl = slice(h * 128, (h + 1) * 128)
            vhat, rv, vn, wt, mixed = f["heads"][h]
            dyh = dy[:, sl]
            dug.append(dyh * mixed)
            dmixed = dyh * f["ug"][:, sl]
            dmb = dmixed.astype(MXU)
            dws_ref[h] += jnp.where(f["tril"], _dot_nt(dmb, vn), 0.0)
            dbs = dbs + jnp.where(lane == h, jnp.sum(dmixed, axis=1, keepdims=True), 0.0)
            dvn = _dot_tn(wt.astype(MXU), dmb)
            dgvs.append(jnp.sum(dvn * vhat, axis=0, keepdims=True))
            dvg.append(_rms_bwd(dvn, vhat, rv, gv[:, sl]))
        dbst_ref[...] += dbs
        dgv_ref[...] += jnp.concatenate(dgvs, axis=1)
        duv_ref[:, 0:1024] = (jnp.concatenate(dug, axis=1) * f["dug"]).astype(MXU)
        duv_ref[:, 1024:2048] = (jnp.concatenate(dvg, axis=1) * f["dvg"]).astype(MXU)

    return pl.pallas_call(
        body, grid=(T // CH,), name="gmlp_bwd",
        in_specs=[_rows(CH, 1024, 0), _rows(CH, 1024, 1), _rows(CH, 1024, 0), _const((1, 1024)),
                  _const((GM_HEADS, CH, CH)), _const((CH, 128)), _const((1, 1024))],
        out_specs=[_rows(CH, 2048), _const((1, 1024)), _const((GM_HEADS, CH, CH)), _const((CH, 128)),
                   _const((1, 1024))],
        out_shape=[_sds((T, 2048), MXU), _sds((1, 1024), F32), _sds((GM_HEADS, CH, CH), F32), _sds((CH, 128), F32),
                   _sds((1, 1024), F32)],
        compiler_params=_cp(),
    )(proj, proj, dcat, gv, ws, bst, gout)


def _ssd_bwd(proj, dtraw, sall, dcat, convw, convb, dtb, alog, dskip, ng, ex, ltri, ext, nb):
    T = proj.shape[0]
    nch = T // CH // nb
    tok, in_specs = _ssd_specs(nch, rev=True)
    in_specs = in_specs + [
        _const((1024, 128)),
        pl.BlockSpec((1, 128, 1024), lambda b, c: (tok(b, c), 0, 0)),
        pl.BlockSpec((CH, 1024), lambda b, c: (tok(b, c), 1)),
    ]

    def body(z_ref, xbc_ref, halo_ref, dt_ref, cw_ref, cb_ref, dtb_ref, al_ref, ds_ref, ng_ref, ex_ref, lt_ref,
             ext_ref, sall_ref, dyb_ref,
             dssd_ref, ddt_ref, dcw_ref, dcb_ref, ddtb_ref, dal_ref, dds_ref, dng_ref,
             dst_ref, dnext_ref):
        b = pl.program_id(0)
        c = pl.program_id(1)

        @pl.when((b == 0) & (c == 0))
        def _():
            for r in (dcw_ref, dcb_ref, ddtb_ref, dal_ref, dds_ref, dng_ref):
                r[...] = jnp.zeros_like(r)

        @pl.when(c == 0)
        def _():
            dst_ref[...] = jnp.zeros_like(dst_ref)
            dnext_ref[...] = jnp.zeros_like(dnext_ref)

        first_chunk = c == nch - 1
        halo = jnp.where(first_chunk, 0.0, halo_ref[...])
        z = z_ref[...]
        ex = ex_ref[...]
        ext = ext_ref[...]
        cw = cw_ref[...]
        ng = ng_ref[...]
        s_prev = sall_ref[0]
        f = _ssd_fwd_vals(z, xbc_ref[...], halo, dt_ref[...], cw, cb_ref[...], dtb_ref[...], al_ref[...],
                          ds_ref[...], ng, ex, lt_ref[...], s_prev)
        xs, xdt, cs, dec, dt = f["xs"], f["xdt"], f["cs"], f["dec"], f["dt"]
        dyb = dyb_ref[...]
        dyg, dngs = [], []
        for g in range(2):
            sl = slice(g * 512, (g + 1) * 512)
            dngs.append(jnp.sum(dyb[:, sl] * f["yhat"][g], axis=0, keepdims=True))
            dyg.append(_rms_bwd(dyb[:, sl], f["yhat"][g], f["rr"][g], ng[:, sl]))
        dng_ref[...] += jnp.concatenate(dngs, axis=1)
        dyg = jnp.concatenate(dyg, axis=1)
        sig_z = f["sig_z"]
        silu_z = z * sig_z
        dy = dyg * silu_z
        dz = dyg * f["ypre"] * sig_z * (1.0 + z * (1.0 - sig_z))
        dds_ref[...] += _dot_hi(jnp.broadcast_to(jnp.sum(dy * xs, axis=0, keepdims=True), (8, 1024)), ext)[0:1]
        dxs = dy * f["de"]
        dye = dy * f["ecse"]
        dyeb = dye.astype(MXU)
        dcs = _dot_hi(dy * f["yo"], ext)
        dst = dst_ref[...]
        dstb = dst.astype(MXU)
        bmb, cmb, sb, xdec = f["bmb"], f["cmb"], f["sb"], f["xdec"]
        u = jnp.concatenate([_dot(bmb[g], dstb[:, g * 512:(g + 1) * 512]) for g in range(2)], axis=1)
        dxdt = [u[:, q * 128:(q + 1) * 128] * f["dece"][:, q * 128:(q + 1) * 128] for q in range(8)]
        t = _dot_hi(u * xdt, ext) * dec
        row = lax.broadcasted_iota(jnp.int32, (CH, 128), 0)
        lane = lax.broadcasted_iota(jnp.int32, (CH, 128), 1)
        dcd = _dot_hi(jnp.broadcast_to(jnp.sum(dst * s_prev, axis=0, keepdims=True), (8, 1024)), ext)[0:1]
        cd = jnp.exp(f["last"])
        dcs = dcs - t + jnp.where(row == CH - 1, jnp.sum(t, axis=0, keepdims=True) + dcd * cd, 0.0)
        dcst = jnp.zeros((128, CH), F32)
        lo = f["lo"]
        dbm, dcm, ds_prev = [], [], []
        for g in range(2):
            sl = slice(g * 512, (g + 1) * 512)
            dmg = jnp.zeros((CH, CH), F32)
            for q in range(4 * g, 4 * g + 4):
                dyq = dy[:, q * 128:(q + 1) * 128]
                xq = xdt[:, q * 128:(q + 1) * 128].astype(MXU)
                for hh in range(2):
                    h = 2 * q + hh
                    m = lo if hh == 0 else ~lo
                    dym = jnp.where(m, dyq, 0.0).astype(MXU)
                    gh = _dot_nt(dym, xq)
                    gl = gh * f["lms"][h]
                    dmg = dmg + gl
                    qh = gl * f["mg"][g]
                    dcs = dcs + jnp.where(lane == h, jnp.sum(qh, axis=1, keepdims=True), 0.0)
                    dcst = dcst - jnp.where(row == h, jnp.sum(qh, axis=0, keepdims=True), 0.0)
                    dxdt[q] = dxdt[q] + _dot_tn(f["whs"][h], dym)
            dmgb = dmg.astype(MXU)
            dcm.append(_dot(dmgb, bmb[g]) + _dot_nt(dyeb[:, sl], sb[:, sl]))
            dbm.append(_dot_tn(dmgb, cmb[g]) + _dot_nt(xdec[:, sl], dstb[:, sl]))
            ds_prev.append(_dot_tn(cmb[g], dyeb[:, sl]))
        dst_ref[...] = jnp.concatenate(ds_prev, axis=1) + dst * f["cde"]
        dcs = dcs + dcst.T
        da = _dot_hi(lt_ref[...].T, dcs)
        dxdt = jnp.concatenate(dxdt, axis=1)
        a_neg = f["a_neg"]
        ddt = da * a_neg + _dot_hi(dxdt * xs, ext)
        dal_ref[...] += jnp.sum(da * dt, axis=0, keepdims=True) * a_neg
        dxs = dxs + dxdt * f["dte"]
        ddtraw = jnp.where(lane < SSD_HEADS, ddt * _sigmoid(f["dtpre"]), 0.0)
        ddtb_ref[...] += jnp.sum(ddtraw, axis=0, keepdims=True)
        ddt_ref[...] = ddtraw.astype(MXU)
        dxa = jnp.concatenate([dxs, dbm[0], dbm[1], dcm[0], dcm[1]], axis=1)
        sig_c = f["sig_c"]
        dconv = dxa * sig_c * (1.0 + f["conv"] * (1.0 - sig_c))
        dcb_ref[...] += jnp.sum(dconv, axis=0, keepdims=True)
        for k in range(4):
            dcw_ref[k:k + 1, :] += jnp.sum(dconv * f["shifts"][3 - k], axis=0, keepdims=True)
        dnext = dnext_ref[...]
        dxbc = cw[3:4] * dconv
        for j in (1, 2, 3):
            dxbc = dxbc + cw[3 - j:4 - j] * _shift_up(dconv, dnext, j)
        dnext_ref[...] = dconv[0:8]
        dssd_ref[:, 0:1024] = dz.astype(MXU)
        dssd_ref[:, 1024:2560] = dxbc.astype(MXU)

    return pl.pallas_call(
        body, grid=(nb, nch), name="ssd_bwd",
        in_specs=in_specs,
        out_specs=[pl.BlockSpec((CH, 2560), lambda b, c: (tok(b, c), 0)),
                   pl.BlockSpec((CH, 128), lambda b, c: (tok(b, c), 0)),
                   _const((8, CONV_CH)), _const((1, CONV_CH)), _const((1, 128)), _const((1, 128)), _const((1, 128)),
                   _const((1, 1024))],
        out_shape=[_sds((T, 2560), MXU), _sds((T, 128), MXU), _sds((8, CONV_CH), F32), _sds((1, CONV_CH), F32),
                   _sds((1, 128), F32), _sds((1, 128), F32), _sds((1, 128), F32), _sds((1, 1024), F32)],
        scratch_shapes=[pltpu.VMEM((128, 1024), F32), pltpu.VMEM((8, CONV_CH), F32)],
        compiler_params=_cp(2),
    )(proj, proj, proj, dtraw, convw, convb, dtb, alog, dskip, ng, ex, ltri, ext, sall, dcat)


def _inproj_bwd(duv, dssd, ddt, wm, wdt, dh1, x, g, tm):
    T = x.shape[0]

    def body(duv_ref, dssd_ref, ddt_ref, wm_ref, wdt_ref, dh1_ref, x_ref, g_ref, dx_ref, dg_ref):
        @pl.when(pl.program_id(0) == 0)
        def _():
            dg_ref[...] = jnp.zeros_like(dg_ref)

        dxn = (_dot_nt(duv_ref[...], wm_ref[:, 0:2048]) + _dot_nt(dssd_ref[...], wm_ref[:, 2048:N_MAIN])
               + _dot_nt(ddt_ref[...], wdt_ref[...]))
        xh, r = _rms(x_ref[...])
        dg_ref[...] += jnp.sum(dxn * xh, axis=0, keepdims=True)
        dx_ref[...] = dh1_ref[...] + _rms_bwd(dxn, xh, r, g_ref[...])

    return pl.pallas_call(
        body, grid=(T // tm,), name="inproj_bwd",
        in_specs=[_rows(tm, 2048), _rows(tm, 2560), _rows(tm, 128), _const((D, N_MAIN)), _const((D, 128)),
                  _rows(tm, D), _rows(tm, D), _const((1, D))],
        out_specs=[_rows(tm, D), _const((1, D))],
        out_shape=[_sds((T, D), F32), _sds((1, D), F32)],
        compiler_params=_cp(),
    )(duv, dssd, ddt, wm, wdt, dh1, x, g)


def _matmul_tn(a, b, name, a_fn=None):
    T, M = a.shape
    N = b.shape[1]
    tm = min(M, 1024)
    tn = 1280 if N == 2560 else min(N, 1024)
    tk = min(T, 512)

    def body(a_ref, b_ref, o_ref, acc_ref):
        k = pl.program_id(2)

        @pl.when(k == 0)
        def _():
            acc_ref[...] = jnp.zeros_like(acc_ref)

        av = a_ref[...]
        if a_fn is not None:
            av = a_fn(av)
        acc_ref[...] += _dot_tn(av, b_ref[...])

        @pl.when(k == T // tk - 1)
        def _():
            o_ref[...] = acc_ref[...].astype(o_ref.dtype)

    return pl.pallas_call(
        body, grid=(M // tm, N // tn, T // tk), name=name,
        in_specs=[pl.BlockSpec((tk, tm), lambda i, j, k: (k, i)), pl.BlockSpec((tk, tn), lambda i, j, k: (k, j))],
        out_specs=pl.BlockSpec((tm, tn), lambda i, j, k: (i, j)),
        out_shape=_sds((M, N), GRAD),
        scratch_shapes=[pltpu.VMEM((tm, tn), F32)],
        compiler_params=_cp(3),
    )(a, b)


def _adamw_vals(w, g, m, v):
    m = B1 * m + (1.0 - B1) * g
    v = B2 * v + (1.0 - B2) * (g * g)
    m_hat = m / (1.0 - B1 ** STEP)
    v_hat = v / (1.0 - B2 ** STEP)
    return -LR * (m_hat / (jnp.sqrt(v_hat) + ADAM_EPS) + WD * w), m, v


def _adamw(w, g, m, v, name):
    R, C = w.shape
    tr = 256 if R % 256 == 0 else R

    def body(w_ref, g_ref, m_ref, v_ref, d_ref, mo_ref, vo_ref):
        d_ref[...], mo_ref[...], vo_ref[...] = _adamw_vals(w_ref[...], g_ref[...], m_ref[...], v_ref[...])

    spec = _rows(tr, C)
    return pl.pallas_call(
        body, grid=(R // tr,), name=name,
        in_specs=[spec] * 4, out_specs=[spec] * 3, out_shape=[_sds((R, C), F32)] * 3,
        compiler_params=_cp(),
    )(w, g, m, v)


def _adamw_halves(w, own, other, m, v, name):
    R, C = w.shape
    half = R // 2
    tr = min(half, 256)
    nth = half // tr

    def body(w_ref, own_ref, oth_ref, m_ref, v_ref, g_ref, d_ref, mo_ref, vo_ref):
        mine = (pl.program_id(0) // nth) == lax.axis_index("c")
        g = jnp.where(mine, own_ref[...], oth_ref[...])
        g_ref[...] = g
        d_ref[...], mo_ref[...], vo_ref[...] = _adamw_vals(w_ref[...], g, m_ref[...], v_ref[...])

    full = _rows(tr, C)
    part = pl.BlockSpec((tr, C), lambda i: (i % nth, 0))
    return pl.pallas_call(
        body, grid=(R // tr,), name=name,
        in_specs=[full, part, part, full, full], out_specs=[full] * 4, out_shape=[_sds((R, C), F32)] * 4,
        compiler_params=_cp(),
    )(w, own, other, m, v)


def _sum_slots(slots, name):
    nd, rows, C = slots.shape
    tr = 256 if rows % 256 == 0 else rows

    def body(s_ref, o_ref):
        acc = s_ref[0].astype(F32)
        for d in range(1, nd):
            acc = acc + s_ref[d].astype(F32)
        o_ref[...] = acc

    return pl.pallas_call(
        body, grid=(rows // tr,), name=name,
        in_specs=[pl.BlockSpec((nd, tr, C), lambda i: (0, i, 0))],
        out_specs=_rows(tr, C),
        out_shape=_sds((rows, C), F32),
        compiler_params=_cp(),
    )(slots)


_ANY = pl.BlockSpec(memory_space=pl.ANY)
_CHIP_FLIPS = [(1, 0), (0, 1), (1, 1)]
_DEVICE_FLIPS = [(fx, fy, fc) for fx in (0, 1) for fy in (0, 1) for fc in (0, 1)][1:]


def _half(h, rows):
    return pl.ds(pl.multiple_of(h * rows, rows), rows)


def _weight_gather(shards, conv):
    n = len(shards)

    def body(*refs):
        ins, conv_ref, outs, conv_out = refs[:n], refs[n], refs[n + 1:2 * n + 1], refs[2 * n + 1]
        send_sems, recv_sems, fsend_sems, frecv_sems, csend_sems, crecv_sems, local_sems = refs[2 * n + 2:]
        x, y, c = lax.axis_index("x"), lax.axis_index("y"), lax.axis_index("c")
        me = 2 * x + y
        halves = [_half(c, r.shape[0] // 2) for r in ins]
        others = [_half(1 - c, r.shape[0] // 2) for r in ins]

        def remote(src, dst, ssem, rsem, to):
            return pltpu.make_async_remote_copy(src_ref=src, dst_ref=dst, send_sem=ssem, recv_sem=rsem,
                                                device_id=to, device_id_type=MESH)

        local = [pltpu.make_async_copy(ins[i], outs[i].at[me], local_sems.at[i]) for i in range(n)]
        local.append(pltpu.make_async_copy(conv_ref, conv_out.at[me], local_sems.at[n]))
        for cp in local:
            cp.start()
        sends = []
        for k, (fx, fy) in enumerate(_CHIP_FLIPS):
            peer = (x ^ fx, y ^ fy, c)
            for i in range(n):
                sends.append(remote(ins[i].at[halves[i]], outs[i].at[me, halves[i]],
                                    send_sems.at[k * n + i], recv_sems.at[k * n + i], peer))
            sends.append(remote(conv_ref, conv_out.at[me], csend_sems.at[k], crecv_sems.at[k], peer))
        for cp in sends:
            cp.start()
        sibling = (x, y, 1 - c)
        forwards = []
        for k, (fx, fy) in enumerate(_CHIP_FLIPS):
            peer = (x ^ fx, y ^ fy, c)
            src = 2 * (x ^ fx) + (y ^ fy)
            for i in range(n):
                landed = outs[i].at[src, halves[i]]
                remote(landed, landed, send_sems.at[k * n + i], recv_sems.at[k * n + i], peer).wait_recv()
                fw = remote(landed, landed, fsend_sems.at[k * n + i], frecv_sems.at[k * n + i], sibling)
                fw.start()
                forwards.append(fw)
            remote(conv_out.at[src], conv_out.at[src], csend_sems.at[k], crecv_sems.at[k], peer).wait_recv()
        for k, (fx, fy) in enumerate(_CHIP_FLIPS):
            src = 2 * (x ^ fx) + (y ^ fy)
            for i in range(n):
                theirs = outs[i].at[src, others[i]]
                remote(theirs, theirs, fsend_sems.at[k * n + i], frecv_sems.at[k * n + i], sibling).wait_recv()
        for cp in sends + forwards:
            cp.wait_send()
        for cp in local:
            cp.wait()

    dma = pltpu.SemaphoreType.DMA
    return pl.pallas_call(
        body, name="weight_gather",
        in_specs=[_ANY] * (n + 1), out_specs=[_ANY] * (n + 1),
        out_shape=[_sds((4,) + s.shape, s.dtype) for s in shards] + [_sds((4,) + conv.shape, conv.dtype)],
        scratch_shapes=[dma((3 * n,)), dma((3 * n,)), dma((3 * n,)), dma((3 * n,)), dma((3,)), dma((3,)),
                        dma((n + 1,))],
    )(*shards, conv)


def _piece(ref, kind, R, C, k, h):
    if kind == "slab":
        return ref.at[k, _half(h, R // 2), :]
    if kind == "rows":
        return ref.at[pl.ds(pl.multiple_of(k * R + h * (R // 2), R // 2), R // 2), :]
    return ref.at[_half(h, R // 2), pl.ds(pl.multiple_of(k * C, C), C)]


def _grad_exchange(grads, small):
    n = len(grads)
    rs = small.shape[0]

    def body(*refs):
        ins, s_ref, outs, sm_ref = refs[:n], refs[n], refs[n + 1:2 * n + 1], refs[2 * n + 1]
        send_sems, recv_sems, ssend_sems, srecv_sems, local_sems = refs[2 * n + 2:]
        x, y, c = lax.axis_index("x"), lax.axis_index("y"), lax.axis_index("c")
        slot = 4 * x + 2 * y + c

        def piece(i, px, py, pc):
            _, kind, (R, C) = grads[i]
            return _piece(ins[i], kind, R, C, 2 * px + py, pc)

        local = [pltpu.make_async_copy(piece(i, x, y, c), outs[i].at[slot], local_sems.at[i]) for i in range(n)]
        local.append(pltpu.make_async_copy(s_ref, sm_ref.at[slot], local_sems.at[n]))
        for cp in local:
            cp.start()
        copies = []
        for k, (fx, fy, fc) in enumerate(_DEVICE_FLIPS):
            peer = (x ^ fx, y ^ fy, c ^ fc)
            for i in range(n):
                copies.append(pltpu.make_async_remote_copy(
                    src_ref=piece(i, *peer), dst_ref=outs[i].at[slot], send_sem=send_sems.at[k * n + i],
                    recv_sem=recv_sems.at[k * n + i], device_id=peer, device_id_type=MESH))
            copies.append(pltpu.make_async_remote_copy(
                src_ref=s_ref, dst_ref=sm_ref.at[slot], send_sem=ssend_sems.at[k], recv_sem=srecv_sems.at[k],
                device_id=peer, device_id_type=MESH))
        for cp in copies:
            cp.start()
        for k, (fx, fy, fc) in enumerate(_DEVICE_FLIPS):
            peer = (x ^ fx, y ^ fy, c ^ fc)
            pslot = 4 * peer[0] + 2 * peer[1] + peer[2]
            for i in range(n):
                pltpu.make_async_remote_copy(
                    src_ref=outs[i].at[pslot], dst_ref=outs[i].at[pslot], send_sem=send_sems.at[k * n + i],
                    recv_sem=recv_sems.at[k * n + i], device_id=peer, device_id_type=MESH).wait_recv()
            pltpu.make_async_remote_copy(
                src_ref=sm_ref.at[pslot], dst_ref=sm_ref.at[pslot], send_sem=ssend_sems.at[k],
                recv_sem=srecv_sems.at[k], device_id=peer, device_id_type=MESH).wait_recv()
        for cp in copies:
            cp.wait_send()
        for cp in local:
            cp.wait()

    dma = pltpu.SemaphoreType.DMA
    return pl.pallas_call(
        body, name="grad_exchange",
        in_specs=[_ANY] * (n + 1), out_specs=[_ANY] * (n + 1),
        out_shape=[_sds((8, R // 2, C), g.dtype) for g, _, (R, C) in grads] + [_sds((8, rs, 128), F32)],
        scratch_shapes=[dma((7 * n,)), dma((7 * n,)), dma((7,)), dma((7,)), dma((n + 1,))],
    )(*[g for g, _, _ in grads], small)


def _sibling_exchange(halves):
    n = len(halves)

    def body(*refs):
        ins, outs, send_sems, recv_sems = refs[:n], refs[n:2 * n], refs[2 * n], refs[2 * n + 1]
        sibling = (lax.axis_index("x"), lax.axis_index("y"), 1 - lax.axis_index("c"))
        copies = [pltpu.make_async_remote_copy(src_ref=ins[i], dst_ref=outs[i], send_sem=send_sems.at[i],
                                               recv_sem=recv_sems.at[i], device_id=sibling, device_id_type=MESH)
                  for i in range(n)]
        for cp in copies:
            cp.start()
        for cp in copies:
            cp.wait()

    dma = pltpu.SemaphoreType.DMA
    return pl.pallas_call(
        body, name="sibling_exchange",
        in_specs=[_ANY] * n, out_specs=[_ANY] * n,
        out_shape=[_sds(h.shape, h.dtype) for h in halves],
        scratch_shapes=[dma((n,)), dma((n,))],
    )(*halves)


_BIG = [("w_in", (1024, 1156), "slab"), ("w_out", (512, 1024), "rows"), ("w_ff1", (1024, 1024), "cols"),
        ("w_ff2", (1024, 1024), "rows"), ("w_ple_gate", (256, 1024), "rows"), ("w_ple_proj", (256, 256), "cols")]
_SMALL = [("norm_mix_g", (1, 1024)), ("gm_v_norm_g", (1, 1024)), ("gm_ws", (1, 8, 128, 128)), ("gm_bs", (1, 8, 128)),
          ("gm_out_norm_g", (1, 1024)), ("ssd_conv_w", (1, 4, 1536)), ("ssd_conv_b", (1, 1536)),
          ("ssd_dt_bias", (1, 16)), ("ssd_a_log", (1, 16)), ("ssd_d", (1, 16)), ("ssd_norm_g", (1, 1024)),
          ("norm_mlp_g", (1, 1024)), ("ple_norm_g", (1, 1024)), ("final_norm_g", (1024,))]


def _rows128(a):
    flat = a.reshape(-1)
    rows = -(-flat.shape[0] // 1024) * 8
    return jnp.pad(flat, (0, rows * 128 - flat.shape[0])).reshape(rows, 128)


def _pad_lanes(v, n=128):
    v = v.reshape(1, -1)
    return jnp.pad(v, ((0, 0), (0, n - v.shape[1])))


def _local_step(x, p, tgt, sm, wfull, nb, tm):
    wm, wdt, wo, w1, w2, wg, wp = (wfull[k] for k in ("wm", "wdt", "wo", "w1", "w2", "wg", "wp"))
    g_mix, gv, gout = sm["norm_mix_g"].reshape(1, D), sm["gm_v_norm_g"].reshape(1, D), sm["gm_out_norm_g"].reshape(1, D)
    ws = sm["gm_ws"].reshape(GM_HEADS, CH, CH)
    bst = jnp.pad(sm["gm_bs"].reshape(GM_HEADS, CH).T, ((0, 0), (0, 128 - GM_HEADS)))
    convw = jnp.pad(wfull["conv_w"], ((0, 4), (0, 0)))
    convb = sm["ssd_conv_b"].reshape(1, CONV_CH)
    dtb, alog, dskip = _pad_lanes(sm["ssd_dt_bias"]), _pad_lanes(sm["ssd_a_log"]), _pad_lanes(sm["ssd_d"])
    ng, g_mlp, g_ple = sm["ssd_norm_g"].reshape(1, D), sm["norm_mlp_g"].reshape(1, D), sm["ple_norm_g"].reshape(1, D)
    gf = sm["final_norm_g"].reshape(1, D)
    head_of_lane = lax.broadcasted_iota(jnp.int32, (128, 1024), 1) // SSD_P
    ex = (lax.broadcasted_iota(jnp.int32, (128, 1024), 0) == head_of_lane).astype(F32)
    ext = ex.T
    ltri = (lax.broadcasted_iota(jnp.int32, (CH, CH), 0) >= lax.broadcasted_iota(jnp.int32, (CH, CH), 1)).astype(F32)

    proj, dtraw, xn = _inproj(x, g_mix, wm, wdt, tm)
    cat = _gmlp_fwd(proj, gv, ws, bst, gout)
    cat, sall = _ssd_fwd(proj, dtraw, cat, convw, convb, dtb, alog, dskip, ng, ex, ltri, nb)
    h1, hn = _outproj(cat, wo, x, g_mlp, tm)
    hid = _ff1(hn, w1, tm)
    h2, hp = _ff2(hid, w2, h1, g_ple, tm)
    dh3, dgl, dpe, loss, d_gf = _tail(h2, hp, p, tgt, wg, wp, gf, tm)

    dh2, dh2b, d_gple = _ple_bwd(dgl, wg, dh3, h2, g_ple, tm)
    dpre = _ff2_bwd(dh2b, w2, hid, tm)
    dh1, dh1b, d_gmlp = _ff1_bwd(dpre, w1, dh2, h1, g_mlp, tm)
    dcat = _outproj_bwd(dh1b, wo, tm)
    duv, d_gv, d_ws, d_bst, d_gout = _gmlp_bwd(proj, dcat, gv, ws, bst, gout)
    dssd, ddt, d_cw, d_cb, d_dtb, d_al, d_ds, d_ng = _ssd_bwd(
        proj, dtraw, sall, dcat, convw, convb, dtb, alog, dskip, ng, ex, ltri, ext, nb)
    dx, d_gmix = _inproj_bwd(duv, dssd, ddt, wm, wdt, dh1, x, g_mix, tm)

    d_win = jnp.concatenate([_matmul_tn(xn, duv, "dw_in_uv"), _matmul_tn(xn, dssd, "dw_in_ssd"),
                             _matmul_tn(xn, ddt, "dw_in_dt")[:, :16]], axis=1)
    grads = {
        "w_in": d_win, "w_out": _matmul_tn(cat, dh1b, "dw_out"), "w_ff1": _matmul_tn(hn, dpre, "dw_ff1"),
        "w_ff2": _matmul_tn(hid, dh2b, "dw_ff2", a_fn=_sq), "w_ple_gate": _matmul_tn(hp, dgl, "dw_ple_gate"),
        "w_ple_proj": _matmul_tn(p, dpe, "dw_ple_proj", a_fn=lambda a: a.astype(MXU)),
        "norm_mix_g": d_gmix, "gm_v_norm_g": d_gv, "gm_ws": d_ws, "gm_bs": d_bst[:, :GM_HEADS].T,
        "gm_out_norm_g": d_gout, "ssd_conv_w": d_cw[0:4], "ssd_conv_b": d_cb, "ssd_dt_bias": d_dtb[:, :16],
        "ssd_a_log": d_al[:, :16], "ssd_d": d_ds[:, :16], "ssd_norm_g": d_ng, "norm_mlp_g": d_gmlp,
        "ple_norm_g": d_gple, "final_norm_g": d_gf,
    }
    return dx, loss, grads


def kernel(x, p, norm_mix_g, w_in, gm_v_norm_g, gm_ws, gm_bs, gm_out_norm_g, ssd_conv_w, ssd_conv_b, ssd_dt_bias, ssd_a_log, ssd_d, ssd_norm_g, w_out, norm_mlp_g, w_ff1, w_ff2, ple_norm_g, w_ple_gate, w_ple_proj, final_norm_g, loss_target, m_norm_mix_g, m_w_in, m_gm_v_norm_g, m_gm_ws, m_gm_bs, m_gm_out_norm_g, m_ssd_conv_w, m_ssd_conv_b, m_ssd_dt_bias, m_ssd_a_log, m_ssd_d, m_ssd_norm_g, m_w_out, m_norm_mlp_g, m_w_ff1, m_w_ff2, m_ple_norm_g, m_w_ple_gate, m_w_ple_proj, m_final_norm_g, v_norm_mix_g, v_w_in, v_gm_v_norm_g, v_gm_ws, v_gm_bs, v_gm_out_norm_g, v_ssd_conv_w, v_ssd_conv_b, v_ssd_dt_bias, v_ssd_a_log, v_ssd_d, v_ssd_norm_g, v_w_out, v_norm_mlp_g, v_w_ff1, v_w_ff2, v_ple_norm_g, v_w_ple_gate, v_w_ple_proj, v_final_norm_g):
    a = dict(locals())
    order = ["norm_mix_g", "w_in", "gm_v_norm_g", "gm_ws", "gm_bs", "gm_out_norm_g", "ssd_conv_w", "ssd_conv_b",
             "ssd_dt_bias", "ssd_a_log", "ssd_d", "ssd_norm_g", "w_out", "norm_mlp_g", "w_ff1", "w_ff2", "ple_norm_g",
             "w_ple_gate", "w_ple_proj", "final_norm_g"]
    chip = 2 * lax.axis_index("x") + lax.axis_index("y")
    nb, S = x.shape[0], x.shape[1]
    T = nb * S

    shards = [a[n].reshape(shp).astype(BF16) for n, shp, _ in _BIG]
    g_win, g_wo, g_w1, g_w2, g_wg, g_wp, g_cw = _weight_gather(shards, ssd_conv_w.reshape(4, 384))
    w_in_full = jnp.concatenate([g_win[k] for k in range(4)], axis=1)
    wfull = {
        "wm": w_in_full[:, :N_MAIN], "wdt": jnp.pad(w_in_full[:, N_MAIN:], ((0, 0), (0, 128 - 16))),
        "wo": g_wo.reshape(2048, D), "w1": g_w1, "w2": g_w2.reshape(DFF, D), "wg": g_wg.reshape(D, D), "wp": g_wp,
        "conv_w": jnp.concatenate([g_cw[k] for k in range(4)], axis=1),
    }
    sm = {n: a[n] for n, _ in _SMALL if n != "ssd_conv_w"}

    dx, loss, grads = _local_step(x.reshape(T, D), p.reshape(T, DPLE), loss_target.reshape(T, D), sm, wfull, nb, 256)

    grads["w_in"] = jnp.stack([grads["w_in"][:, 1156 * k:1156 * (k + 1)] for k in range(4)])
    small_parts = [_rows128(grads[n]) for n, _ in _SMALL] + [_rows128(loss[0:1, 0:1])]
    small_rows = [s.shape[0] for s in small_parts]
    *slots, small_slots = _grad_exchange([(grads[n], kind, shp) for n, shp, kind in _BIG],
                                         jnp.concatenate(small_parts, axis=0))
    own = [_sum_slots(s, "sum_" + n) for s, (n, _, _) in zip(slots, _BIG)]
    small_tot = _sum_slots(small_slots, "sum_small")
    other = _sibling_exchange(own)

    g_out, delta, new_m, new_v = {}, {}, {}, {}
    for i, (n, shp, _) in enumerate(_BIG):
        res = _adamw_halves(a[n].reshape(shp), own[i], other[i], a["m_" + n].reshape(shp), a["v_" + n].reshape(shp),
                            "adamw_" + n)
        g_out[n], delta[n], new_m[n], new_v[n] = (r.reshape(a[n].shape) for r in res)
    o = 0
    for (n, shp), r in zip(_SMALL + [("loss", ())], small_rows):
        cnt = 1
        for s in shp:
            cnt *= s
        g_out[n] = small_tot[o:o + r].reshape(-1)[:cnt].reshape(shp)
        o += r
    g_out["ssd_conv_w"] = lax.dynamic_slice(g_out["ssd_conv_w"], (0, 0, chip * 384), (1, 4, 384))
    small_names = [n for n, _ in _SMALL]
    packs = [jnp.concatenate([_rows128(src(n)) for n in small_names], axis=0)
             for src in (lambda n: a[n], lambda n: g_out[n], lambda n: a["m_" + n], lambda n: a["v_" + n])]
    outs = _adamw(*packs, "adamw_small")
    o = 0
    for n in small_names:
        r = _rows128(a[n]).shape[0]
        cnt = a[n].size
        for dst, src in zip((delta, new_m, new_v), outs):
            dst[n] = src[o:o + r].reshape(-1)[:cnt].reshape(a[n].shape)
        o += r
    return (g_out["loss"], dx.reshape(x.shape), *[g_out[n] for n in order], *[delta[n] for n in order],
            *[new_m[n] for n in order], *[new_v[n] for n in order])
```

```python
import jax
import jax.numpy as jnp
from jax import lax
from jax.experimental import pallas as pl
from jax.experimental.pallas import tpu as pltpu

F32 = jnp.float32
BF16 = jnp.bfloat16
MXU = jnp.bfloat16
GRAD = jnp.bfloat16

D = 1024
CH = 128
GM_HEADS = 8
SSD_HEADS = 16
SSD_P = 64
CONV_CH = 1536
N_MAIN = 4608
DFF = 4096
DPLE = 256
EPS = 1e-6
NEG = -1e30

LR, B1, B2, ADAM_EPS, WD, STEP = 0.001, 0.9, 0.999, 1e-08, 0.01, 10

VMEM_LIMIT = 56 * 1024 * 1024
MESH = pl.DeviceIdType.MESH

INV_SQRT2 = 0.7071067811865476
INV_SQRT_2PI = 0.3989422804014327


def _cp(n_axes=1):
    return pltpu.CompilerParams(dimension_semantics=("arbitrary",) * n_axes, vmem_limit_bytes=VMEM_LIMIT)


def _dot(a, b):
    return jnp.dot(a, b, preferred_element_type=F32)


def _dot_nt(a, b):
    return lax.dot_general(a, b, (((1,), (1,)), ((), ())), preferred_element_type=F32)


def _dot_tn(a, b):
    return lax.dot_general(a, b, (((0,), (0,)), ((), ())), preferred_element_type=F32)


def _dot_hi(a, b):
    return jnp.dot(a, b, preferred_element_type=F32, precision=lax.Precision.HIGHEST)


def _rows(tm, n, j=0):
    return pl.BlockSpec((tm, n), lambda i: (i, j))


def _const(shape):
    nd = len(shape)
    return pl.BlockSpec(shape, lambda *_: (0,) * nd)


def _sds(shape, dtype):
    return jax.ShapeDtypeStruct(shape, dtype)


def _rms(x):
    r = lax.rsqrt(jnp.mean(x * x, axis=-1, keepdims=True) + EPS)
    return x * r, r


def _rms_bwd(dy, xhat, r, g):
    dyg = dy * g
    return r * (dyg - xhat * jnp.mean(dyg * xhat, axis=-1, keepdims=True))


def _sigmoid(x):
    return 1.0 / (1.0 + jnp.exp(-x))


def _gelu(x):
    cdf = 0.5 * (1.0 + lax.erf(x * INV_SQRT2))
    pdf = jnp.exp(-0.5 * x * x) * INV_SQRT_2PI
    return x * cdf, cdf + x * pdf


def _softplus(x):
    e = jnp.exp(-jnp.abs(x))
    u = 1.0 + e
    log1p = jnp.where(u == 1.0, e, jnp.log(u) * e / (u - 1.0))
    return jnp.maximum(x, 0.0) + log1p


def _after(n_in, fn):
    def body(*refs):
        return fn(*refs[:n_in], *refs[n_in + 1:])

    return body


def _inproj(x, g, wm, wdt, tm, after):
    T = x.shape[0]

    def body(x_ref, g_ref, wm_ref, wdt_ref, proj_ref, dt_ref, xn_ref):
        xh, _ = _rms(x_ref[...])
        xn = (xh * g_ref[...]).astype(MXU)
        xn_ref[...] = xn
        for n in range(N_MAIN // 512):
            proj_ref[:, n * 512:(n + 1) * 512] = _dot(xn, wm_ref[:, n * 512:(n + 1) * 512])
        dt_ref[...] = _dot(xn, wdt_ref[...])

    return pl.pallas_call(
        _after(4, body), grid=(T // tm,), name="inproj",
        in_specs=[_rows(tm, D), _const((1, D)), _const((D, N_MAIN)), _const((D, 128)), _ANY],
        out_specs=[_rows(tm, N_MAIN), _rows(tm, 128), _rows(tm, D)],
        out_shape=[_sds((T, N_MAIN), F32), _sds((T, 128), F32), _sds((T, D), MXU)],
        compiler_params=_cp(),
    )(x, g, wm, wdt, after)


def _gmlp_fwd_vals(u, v, gv, ws_ref, bst, gout):
    ug, dug = _gelu(u)
    vg, dvg = _gelu(v)
    row = lax.broadcasted_iota(jnp.int32, (CH, CH), 0)
    col = lax.broadcasted_iota(jnp.int32, (CH, CH), 1)
    tril = row >= col
    ys, heads = [], []
    for h in range(GM_HEADS):
        sl = slice(h * 128, (h + 1) * 128)
        vhat, rv = _rms(vg[:, sl])
        vn = (vhat * gv[:, sl]).astype(MXU)
        wt = jnp.where(tril, ws_ref[h], 0.0)
        mixed = _dot(wt.astype(MXU), vn) + bst[:, h:h + 1]
        ys.append(ug[:, sl] * mixed)
        heads.append((vhat, rv, vn, wt, mixed))
    y = jnp.concatenate(ys, axis=1)
    yhat, ry = _rms(y)
    return dict(ug=ug, dug=dug, dvg=dvg, heads=heads, yhat=yhat, ry=ry, tril=tril, out=yhat * gout)


def _gmlp_fwd(proj, gv, ws, bst, gout):
    T = proj.shape[0]

    def body(u_ref, v_ref, gv_ref, ws_ref, bst_ref, gout_ref, ya_ref):
        f = _gmlp_fwd_vals(u_ref[...], v_ref[...], gv_ref[...], ws_ref, bst_ref[...], gout_ref[...])
        ya_ref[...] = f["out"].astype(MXU)

    return pl.pallas_call(
        body, grid=(T // CH,), name="gmlp_fwd",
        in_specs=[_rows(CH, 1024, 0), _rows(CH, 1024, 1), _const((1, 1024)), _const((GM_HEADS, CH, CH)),
                  _const((CH, 128)), _const((1, 1024))],
        out_specs=_rows(CH, 1024, 0),
        out_shape=_sds((T, 2048), MXU),
        compiler_params=_cp(),
    )(proj, proj, gv, ws, bst, gout)


def _shift_down(cur, halo, j):
    sh = pltpu.roll(cur, j, 0)
    row8 = lax.broadcasted_iota(jnp.int32, (8, cur.shape[1]), 0)
    top = jnp.where(row8 < j, pltpu.roll(halo, j, 0), sh[0:8])
    return jnp.concatenate([top, sh[8:]], axis=0)


def _shift_up(cur, halo, j):
    n = cur.shape[0]
    sh = pltpu.roll(cur, n - j, 0)
    row8 = lax.broadcasted_iota(jnp.int32, (8, cur.shape[1]), 0)
    bot = jnp.where(row8 + j >= 8, pltpu.roll(halo, 8 - j, 0), sh[n - 8:n])
    return jnp.concatenate([sh[0:n - 8], bot], axis=0)


def _ssd_fwd_vals(z, xbc, halo, dtraw, convw, convb, dtb, alog, dskip, ng, ex, ltri, s_prev):
    shifts = [xbc] + [_shift_down(xbc, halo, j) for j in (1, 2, 3)]
    conv = convb + convw[3:4] * shifts[0] + convw[2:3] * shifts[1] + convw[1:2] * shifts[2] + convw[0:1] * shifts[3]
    sig_c = _sigmoid(conv)
    xa = conv * sig_c
    xs = xa[:, :1024]
    bm = [xa[:, 1024:1152], xa[:, 1152:1280]]
    cm = [xa[:, 1280:1408], xa[:, 1408:1536]]
    dtpre = dtraw + dtb
    dt = _softplus(dtpre)
    a_neg = -jnp.exp(alog)
    cs = _dot_hi(ltri, dt * a_neg)
    cst = cs.T
    last = cs[CH - 1:CH]
    ecs = jnp.exp(cs)
    dec = jnp.exp(last - cs)
    dte = _dot_hi(dt, ex)
    ecse = _dot_hi(ecs, ex)
    dece = _dot_hi(dec, ex)
    cde = ecse[CH - 1:CH]
    de = _dot_hi(jnp.broadcast_to(dskip, (8, 128)), ex)[0:1]
    xdt = xs * dte
    row = lax.broadcasted_iota(jnp.int32, (CH, CH), 0)
    col = lax.broadcasted_iota(jnp.int32, (CH, CH), 1)
    tril = row >= col
    lo = col < SSD_P
    bmb = [b.astype(MXU) for b in bm]
    cmb = [c.astype(MXU) for c in cm]
    mg = [_dot_nt(cmb[g], bmb[g]) for g in range(2)]
    yd, lms, whs = [], [], []
    for q in range(8):
        g = q // 4
        xq = xdt[:, q * 128:(q + 1) * 128]
        acc = None
        for hh in range(2):
            h = 2 * q + hh
            seg = cs[:, h:h + 1] - cst[h:h + 1, :]
            lm = jnp.exp(jnp.where(tril, seg, NEG))
            wh = (mg[g] * lm).astype(MXU)
            xm = jnp.where(lo if hh == 0 else ~lo, xq, 0.0).astype(MXU)
            part = _dot(wh, xm)
            acc = part if acc is None else acc + part
            lms.append(lm)
            whs.append(wh)
        yd.append(acc)
    yd = jnp.concatenate(yd, axis=1)
    sb = s_prev.astype(MXU)
    yo = jnp.concatenate([_dot(cmb[g], sb[:, g * 512:(g + 1) * 512]) for g in range(2)], axis=1) * ecse
    xdec = (xdt * dece).astype(MXU)
    states = jnp.concatenate([_dot_tn(bmb[g], xdec[:, g * 512:(g + 1) * 512]) for g in range(2)], axis=1)
    s_next = s_prev * cde + states
    ypre = yd + yo + de * xs
    sig_z = _sigmoid(z)
    yg = ypre * z * sig_z
    outs, yhat, rr = [], [], []
    for g in range(2):
        sl = slice(g * 512, (g + 1) * 512)
        yh, r = _rms(yg[:, sl])
        yhat.append(yh)
        rr.append(r)
        outs.append(yh * ng[:, sl])
    return dict(shifts=shifts, conv=conv, sig_c=sig_c, xs=xs, bmb=bmb, cmb=cmb, dtpre=dtpre, dt=dt, a_neg=a_neg,
                cs=cs, last=last, ecs=ecs, dec=dec, dte=dte, ecse=ecse, dece=dece, cde=cde, de=de, xdt=xdt,
                mg=mg, lms=lms, whs=whs, lo=lo, yo=yo, sb=sb, xdec=xdec, s_next=s_next, ypre=ypre, sig_z=sig_z,
                yhat=yhat, rr=rr, out=jnp.concatenate(outs, axis=1))


def _ssd_specs(nch, rev):
    def tok(b, c):
        return b * nch + ((nch - 1 - c) if rev else c)

    return tok, [
        pl.BlockSpec((CH, 1024), lambda b, c: (tok(b, c), 2)),
        pl.BlockSpec((CH, CONV_CH), lambda b, c: (tok(b, c), 2)),
        pl.BlockSpec((8, CONV_CH), lambda b, c: (jnp.maximum(tok(b, c) * (CH // 8) - 1, 0), 2)),
        pl.BlockSpec((CH, 128), lambda b, c: (tok(b, c), 0)),
        _const((8, CONV_CH)), _const((1, CONV_CH)), _const((1, 128)), _const((1, 128)), _const((1, 128)),
        _const((1, 1024)), _const((128, 1024)), _const((CH, CH)),
    ]


def _ssd_fwd(proj, dtraw, cat, convw, convb, dtb, alog, dskip, ng, ex, ltri, nb):
    T = proj.shape[0]
    nch = T // CH // nb
    tok, in_specs = _ssd_specs(nch, rev=False)

    def body(z_ref, xbc_ref, halo_ref, dt_ref, cw_ref, cb_ref, dtb_ref, al_ref, ds_ref, ng_ref, ex_ref, lt_ref,
             cat_in_ref, yb_ref, sall_ref, s_ref):
        del cat_in_ref
        c = pl.program_id(1)

        @pl.when(c == 0)
        def _():
            s_ref[...] = jnp.zeros_like(s_ref)

        halo = jnp.where(c == 0, 0.0, halo_ref[...])
        s_prev = s_ref[...]
        sall_ref[0] = s_prev
        f = _ssd_fwd_vals(z_ref[...], xbc_ref[...], halo, dt_ref[...], cw_ref[...], cb_ref[...], dtb_ref[...],
                          al_ref[...], ds_ref[...], ng_ref[...], ex_ref[...], lt_ref[...], s_prev)
        s_ref[...] = f["s_next"]
        yb_ref[...] = f["out"].astype(MXU)

    return pl.pallas_call(
        body, grid=(nb, nch), name="ssd_fwd",
        in_specs=in_specs + [_ANY],
        out_specs=[pl.BlockSpec((CH, 1024), lambda b, c: (tok(b, c), 1)),
                   pl.BlockSpec((1, 128, 1024), lambda b, c: (tok(b, c), 0, 0))],
        out_shape=[_sds((T, 2048), MXU), _sds((T // CH, 128, 1024), F32)],
        scratch_shapes=[pltpu.VMEM((128, 1024), F32)],
        input_output_aliases={12: 0},
        compiler_params=_cp(2),
    )(proj, proj, proj, dtraw, convw, convb, dtb, alog, dskip, ng, ex, ltri, cat)


def _outproj(cat, wo, x, g, tm):
    T = x.shape[0]

    def body(cat_ref, wo_ref, x_ref, g_ref, h1_ref, hn_ref):
        h1 = x_ref[...] + _dot(cat_ref[...], wo_ref[...])
        h1_ref[...] = h1
        hn_ref[...] = (_rms(h1)[0] * g_ref[...]).astype(MXU)

    return pl.pallas_call(
        body, grid=(T // tm,), name="outproj",
        in_specs=[_rows(tm, 2048), _const((2048, D)), _rows(tm, D), _const((1, D))],
        out_specs=[_rows(tm, D), _rows(tm, D)],
        out_shape=[_sds((T, D), F32), _sds((T, D), MXU)],
        compiler_params=_cp(),
    )(cat, wo, x, g)


def _ff1(hn, w1, tm):
    T = hn.shape[0]

    def body(hn_ref, w1_ref, hid_ref):
        hn_v = hn_ref[...]
        for n in range(4):
            hid_ref[:, n * 1024:(n + 1) * 1024] = jnp.maximum(_dot(hn_v, w1_ref[n]), 0.0).astype(MXU)

    return pl.pallas_call(
        body, grid=(T // tm,), name="ff1",
        in_specs=[_rows(tm, D), _const((4, D, 1024))],
        out_specs=_rows(tm, DFF),
        out_shape=_sds((T, DFF), MXU),
        compiler_params=_cp(),
    )(hn, w1)


def _sq(hid):
    h = hid.astype(F32)
    return (h * h).astype(MXU)


def _ff2(hid, w2, h1, g, tm):
    T = h1.shape[0]

    def body(hid_ref, w2_ref, h1_ref, g_ref, h2_ref, hp_ref):
        h2 = h1_ref[...] + _dot(_sq(hid_ref[...]), w2_ref[...])
        h2_ref[...] = h2
        hp_ref[...] = (_rms(h2)[0] * g_ref[...]).astype(MXU)

    return pl.pallas_call(
        body, grid=(T // tm,), name="ff2",
        in_specs=[_rows(tm, DFF), _const((DFF, D)), _rows(tm, D), _const((1, D))],
        out_specs=[_rows(tm, D), _rows(tm, D)],
        out_shape=[_sds((T, D), F32), _sds((T, D), MXU)],
        compiler_params=_cp(),
    )(hid, w2, h1, g)


def _tail(h2, hp, p, tgt, wg, wp, gf, tm):
    T = h2.shape[0]

    def body(h2_ref, hp_ref, p_ref, t_ref, wg_ref, wp_ref, gf_ref, dh3_ref, dgl_ref, dpe_ref, loss_ref, dgf_ref):
        @pl.when(pl.program_id(0) == 0)
        def _():
            loss_ref[...] = jnp.zeros_like(loss_ref)
            dgf_ref[...] = jnp.zeros_like(dgf_ref)

        gate = _sigmoid(_dot(hp_ref[...], wg_ref[...]))
        pb = p_ref[...].astype(MXU)
        pe = jnp.concatenate([_dot(pb, wp_ref[k]) for k in range(4)], axis=1)
        h3 = h2_ref[...] + gate * pe
        hh, r = _rms(h3)
        gf = gf_ref[...]
        diff = hh * gf - t_ref[...]
        loss_ref[...] += 0.5 * jnp.sum(jnp.mean(diff * diff, axis=-1, keepdims=True))
        dout = diff * (1.0 / D)
        dgf_ref[...] += jnp.sum(dout * hh, axis=0, keepdims=True)
        dh3 = _rms_bwd(dout, hh, r, gf)
        dh3_ref[...] = dh3
        dgl_ref[...] = (dh3 * pe * gate * (1.0 - gate)).astype(MXU)
        dpe_ref[...] = (dh3 * gate).astype(MXU)

    return pl.pallas_call(
        body, grid=(T // tm,), name="tail",
        in_specs=[_rows(tm, D), _rows(tm, D), _rows(tm, DPLE), _rows(tm, D), _const((D, D)), _const((4, DPLE, 256)),
                  _const((1, D))],
        out_specs=[_rows(tm, D), _rows(tm, D), _rows(tm, D), _const((8, 128)), _const((1, D))],
        out_shape=[_sds((T, D), F32), _sds((T, D), MXU), _sds((T, D), MXU), _sds((8, 128), F32), _sds((1, D), F32)],
        compiler_params=_cp(),
    )(h2, hp, p, tgt, wg, wp, gf)


def _ple_bwd(dgl, wg, dh3, h2, g, tm):
    T = h2.shape[0]

    def body(dgl_ref, wg_ref, dh3_ref, h2_ref, g_ref, dh2_ref, dh2b_ref, dg_ref):
        @pl.when(pl.program_id(0) == 0)
        def _():
            dg_ref[...] = jnp.zeros_like(dg_ref)

        dhp = _dot_nt(dgl_ref[...], wg_ref[...])
        hh, r = _rms(h2_ref[...])
        dg_ref[...] += jnp.sum(dhp * hh, axis=0, keepdims=True)
        dh2 = dh3_ref[...] + _rms_bwd(dhp, hh, r, g_ref[...])
        dh2_ref[...] = dh2
        dh2b_ref[...] = dh2.astype(MXU)

    return pl.pallas_call(
        body, grid=(T // tm,), name="ple_bwd",
        in_specs=[_rows(tm, D), _const((D, D)), _rows(tm, D), _rows(tm, D), _const((1, D))],
        out_specs=[_rows(tm, D), _rows(tm, D), _const((1, D))],
        out_shape=[_sds((T, D), F32), _sds((T, D), MXU), _sds((1, D), F32)],
        compiler_params=_cp(),
    )(dgl, wg, dh3, h2, g)


def _ff2_bwd(dh2b, w2, hid, tm):
    T = hid.shape[0]

    def body(dh2b_ref, w2_ref, hid_ref, dpre_ref):
        d = dh2b_ref[...]
        for n in range(DFF // 1024):
            sl = slice(n * 1024, (n + 1) * 1024)
            da = _dot_nt(d, w2_ref[sl, :])
            dpre_ref[:, sl] = (2.0 * da * hid_ref[:, sl].astype(F32)).astype(MXU)

    return pl.pallas_call(
        body, grid=(T // tm,), name="ff2_bwd",
        in_specs=[_rows(tm, D), _const((DFF, D)), _rows(tm, DFF)],
        out_specs=_rows(tm, DFF),
        out_shape=_sds((T, DFF), MXU),
        compiler_params=_cp(),
    )(dh2b, w2, hid)


def _ff1_bwd(dpre, w1, dh2, h1, g, tm, after):
    T = h1.shape[0]

    def body(dpre_ref, w1_ref, dh2_ref, h1_ref, g_ref, dh1_ref, dh1b_ref, dg_ref):
        @pl.when(pl.program_id(0) == 0)
        def _():
            dg_ref[...] = jnp.zeros_like(dg_ref)

        dhn = _dot_nt(dpre_ref[:, 0:1024], w1_ref[0])
        for k in range(1, 4):
            dhn = dhn + _dot_nt(dpre_ref[:, k * 1024:(k + 1) * 1024], w1_ref[k])
        hh, r = _rms(h1_ref[...])
        dg_ref[...] += jnp.sum(dhn * hh, axis=0, keepdims=True)
        dh1 = dh2_ref[...] + _rms_bwd(dhn, hh, r, g_ref[...])
        dh1_ref[...] = dh1
        dh1b_ref[...] = dh1.astype(MXU)

    return pl.pallas_call(
        _after(5, body), grid=(T // tm,), name="ff1_bwd",
        in_specs=[_rows(tm, DFF), _const((4, D, 1024)), _rows(tm, D), _rows(tm, D), _const((1, D)), _ANY],
        out_specs=[_rows(tm, D), _rows(tm, D), _const((1, D))],
        out_shape=[_sds((T, D), F32), _sds((T, D), MXU), _sds((1, D), F32)],
        compiler_params=_cp(),
    )(dpre, w1, dh2, h1, g, after)


def _outproj_bwd(dh1b, wo, tm):
    T = dh1b.shape[0]

    def body(d_ref, wo_ref, dcat_ref):
        d = d_ref[...]
        dcat_ref[:, 0:1024] = _dot_nt(d, wo_ref[0:1024, :])
        dcat_ref[:, 1024:2048] = _dot_nt(d, wo_ref[1024:2048, :])

    return pl.pallas_call(
        body, grid=(T // tm,), name="outproj_bwd",
        in_specs=[_rows(tm, D), _const((2048, D))],
        out_specs=_rows(tm, 2048),
        out_shape=_sds((T, 2048), F32),
        compiler_params=_cp(),
    )(dh1b, wo)


def _gmlp_bwd(proj, dcat, gv, ws, bst, gout):
    T = proj.shape[0]

    def body(u_ref, v_ref, dya_ref, gv_ref, ws_ref, bst_ref, gout_ref, duv_ref, dgv_ref, dws_ref, dbst_ref, dgo_ref):
        @pl.when(pl.program_id(0) == 0)
        def _():
            dgv_ref[...] = jnp.zeros_like(dgv_ref)
            dws_ref[...] = jnp.zeros_like(dws_ref)
            dbst_ref[...] = jnp.zeros_like(dbst_ref)
            dgo_ref[...] = jnp.zeros_like(dgo_ref)

        gv = gv_ref[...]
        f = _gmlp_fwd_vals(u_ref[...], v_ref[...], gv, ws_ref, bst_ref[...], gout_ref[...])
        dya = dya_ref[...]
        dgo_ref[...] += jnp.sum(dya * f["yhat"], axis=0, keepdims=True)
        dy = _rms_bwd(dya, f["yhat"], f["ry"], gout_ref[...])
        lane = lax.broadcasted_iota(jnp.int32, (CH, 128), 1)
        dbs = jnp.zeros((CH, 128), F32)
        dug, dvg, dgvs = [], [], []
        for h in range(GM_HEADS):
            sl = slice(h * 128, (h + 1) * 128)
            vhat, rv, vn, wt, mixed = f["heads"][h]
            dyh = dy[:, sl]
            dug.append(dyh * mixed)
            dmixed = dyh * f["ug"][:, sl]
            dmb = dmixed.astype(MXU)
            dws_ref[h] += jnp.where(f["tril"], _dot_nt(dmb, vn), 0.0)
            dbs = dbs + jnp.where(lane == h, jnp.sum(dmixed, axis=1, keepdims=True), 0.0)
            dvn = _dot_tn(wt.astype(MXU), dmb)
            dgvs.append(jnp.sum(dvn * vhat, axis=0, keepdims=True))
            dvg.append(_rms_bwd(dvn, vhat, rv, gv[:, sl]))
        dbst_ref[...] += dbs
        dgv_ref[...] += jnp.concatenate(dgvs, axis=1)
        duv_ref[:, 0:1024] = (jnp.concatenate(dug, axis=1) * f["dug"]).astype(MXU)
        duv_ref[:, 1024:2048] = (jnp.concatenate(dvg, axis=1) * f["dvg"]).astype(MXU)

    return pl.pallas_call(
        body, grid=(T // CH,), name="gmlp_bwd",
        in_specs=[_rows(CH, 1024, 0), _rows(CH, 1024, 1), _rows(CH, 1024, 0), _const((1, 1024)),
                  _const((GM_HEADS, CH, CH)), _const((CH, 128)), _const((1, 1024))],
        out_specs=[_rows(CH, 2048), _const((1, 1024)), _const((GM_HEADS, CH, CH)), _const((CH, 128)),
                   _const((1, 1024))],
        out_shape=[_sds((T, 2048), MXU), _sds((1, 1024), F32), _sds((GM_HEADS, CH, CH), F32), _sds((CH, 128), F32),
                   _sds((1, 1024), F32)],
        compiler_params=_cp(),
    )(proj, proj, dcat, gv, ws, bst, gout)


def _ssd_bwd(proj, dtraw, sall, dcat, convw, convb, dtb, alog, dskip, ng, ex, ltri, ext, nb, after):
    T = proj.shape[0]
    nch = T // CH // nb
    tok, in_specs = _ssd_specs(nch, rev=True)
    in_specs = in_specs + [
        _const((1024, 128)),
        pl.BlockSpec((1, 128, 1024), lambda b, c: (tok(b, c), 0, 0)),
        pl.BlockSpec((CH, 1024), lambda b, c: (tok(b, c), 1)),
        _ANY,
    ]

    def body(z_ref, xbc_ref, halo_ref, dt_ref, cw_ref, cb_ref, dtb_ref, al_ref, ds_ref, ng_ref, ex_ref, lt_ref,
             ext_ref, sall_ref, dyb_ref,
             dssd_ref, ddt_ref, dcw_ref, dcb_ref, ddtb_ref, dal_ref, dds_ref, dng_ref,
             dst_ref, dnext_ref):
        b = pl.program_id(0)
        c = pl.program_id(1)

        @pl.when((b == 0) & (c == 0))
        def _():
            for r in (dcw_ref, dcb_ref, ddtb_ref, dal_ref, dds_ref, dng_ref):
                r[...] = jnp.zeros_like(r)

        @pl.when(c == 0)
        def _():
            dst_ref[...] = jnp.zeros_like(dst_ref)
            dnext_ref[...] = jnp.zeros_like(dnext_ref)

        first_chunk = c == nch - 1
        halo = jnp.where(first_chunk, 0.0, halo_ref[...])
        z = z_ref[...]
        ex = ex_ref[...]
        ext = ext_ref[...]
        cw = cw_ref[...]
        ng = ng_ref[...]
        s_prev = sall_ref[0]
        f = _ssd_fwd_vals(z, xbc_ref[...], halo, dt_ref[...], cw, cb_ref[...], dtb_ref[...], al_ref[...],
                          ds_ref[...], ng, ex, lt_ref[...], s_prev)
        xs, xdt, cs, dec, dt = f["xs"], f["xdt"], f["cs"], f["dec"], f["dt"]
        dyb = dyb_ref[...]
        dyg, dngs = [], []
        for g in range(2):
            sl = slice(g * 512, (g + 1) * 512)
            dngs.append(jnp.sum(dyb[:, sl] * f["yhat"][g], axis=0, keepdims=True))
            dyg.append(_rms_bwd(dyb[:, sl], f["yhat"][g], f["rr"][g], ng[:, sl]))
        dng_ref[...] += jnp.concatenate(dngs, axis=1)
        dyg = jnp.concatenate(dyg, axis=1)
        sig_z = f["sig_z"]
        silu_z = z * sig_z
        dy = dyg * silu_z
        dz = dyg * f["ypre"] * sig_z * (1.0 + z * (1.0 - sig_z))
        dds_ref[...] += _dot_hi(jnp.broadcast_to(jnp.sum(dy * xs, axis=0, keepdims=True), (8, 1024)), ext)[0:1]
        dxs = dy * f["de"]
        dye = dy * f["ecse"]
        dyeb = dye.astype(MXU)
        dcs = _dot_hi(dy * f["yo"], ext)
        dst = dst_ref[...]
        dstb = dst.astype(MXU)
        bmb, cmb, sb, xdec = f["bmb"], f["cmb"], f["sb"], f["xdec"]
        u = jnp.concatenate([_dot(bmb[g], dstb[:, g * 512:(g + 1) * 512]) for g in range(2)], axis=1)
        dxdt = [u[:, q * 128:(q + 1) * 128] * f["dece"][:, q * 128:(q + 1) * 128] for q in range(8)]
        t = _dot_hi(u * xdt, ext) * dec
        row = lax.broadcasted_iota(jnp.int32, (CH, 128), 0)
        lane = lax.broadcasted_iota(jnp.int32, (CH, 128), 1)
        dcd = _dot_hi(jnp.broadcast_to(jnp.sum(dst * s_prev, axis=0, keepdims=True), (8, 1024)), ext)[0:1]
        cd = jnp.exp(f["last"])
        dcs = dcs - t + jnp.where(row == CH - 1, jnp.sum(t, axis=0, keepdims=True) + dcd * cd, 0.0)
        dcst = jnp.zeros((128, CH), F32)
        lo = f["lo"]
        dbm, dcm, ds_prev = [], [], []
        for g in range(2):
            sl = slice(g * 512, (g + 1) * 512)
            dmg = jnp.zeros((CH, CH), F32)
            for q in range(4 * g, 4 * g + 4):
                dyq = dy[:, q * 128:(q + 1) * 128]
                xq = xdt[:, q * 128:(q + 1) * 128].astype(MXU)
                for hh in range(2):
                    h = 2 * q + hh
                    m = lo if hh == 0 else ~lo
                    dym = jnp.where(m, dyq, 0.0).astype(MXU)
                    gh = _dot_nt(dym, xq)
                    gl = gh * f["lms"][h]
                    dmg = dmg + gl
                    qh = gl * f["mg"][g]
                    dcs = dcs + jnp.where(lane == h, jnp.sum(qh, axis=1, keepdims=True), 0.0)
                    dcst = dcst - jnp.where(row == h, jnp.sum(qh, axis=0, keepdims=True), 0.0)
                    dxdt[q] = dxdt[q] + _dot_tn(f["whs"][h], dym)
            dmgb = dmg.astype(MXU)
            dcm.append(_dot(dmgb, bmb[g]) + _dot_nt(dyeb[:, sl], sb[:, sl]))
            dbm.append(_dot_tn(dmgb, cmb[g]) + _dot_nt(xdec[:, sl], dstb[:, sl]))
            ds_prev.append(_dot_tn(cmb[g], dyeb[:, sl]))
        dst_ref[...] = jnp.concatenate(ds_prev, axis=1) + dst * f["cde"]
        dcs = dcs + dcst.T
        da = _dot_hi(lt_ref[...].T, dcs)
        dxdt = jnp.concatenate(dxdt, axis=1)
        a_neg = f["a_neg"]
        ddt = da * a_neg + _dot_hi(dxdt * xs, ext)
        dal_ref[...] += jnp.sum(da * dt, axis=0, keepdims=True) * a_neg
        dxs = dxs + dxdt * f["dte"]
        ddtraw = jnp.where(lane < SSD_HEADS, ddt * _sigmoid(f["dtpre"]), 0.0)
        ddtb_ref[...] += jnp.sum(ddtraw, axis=0, keepdims=True)
        ddt_ref[...] = ddtraw.astype(MXU)
        dxa = jnp.concatenate([dxs, dbm[0], dbm[1], dcm[0], dcm[1]], axis=1)
        sig_c = f["sig_c"]
        dconv = dxa * sig_c * (1.0 + f["conv"] * (1.0 - sig_c))
        dcb_ref[...] += jnp.sum(dconv, axis=0, keepdims=True)
        for k in range(4):
            dcw_ref[k:k + 1, :] += jnp.sum(dconv * f["shifts"][3 - k], axis=0, keepdims=True)
        dnext = dnext_ref[...]
        dxbc = cw[3:4] * dconv
        for j in (1, 2, 3):
            dxbc = dxbc + cw[3 - j:4 - j] * _shift_up(dconv, dnext, j)
        dnext_ref[...] = dconv[0:8]
        dssd_ref[:, 0:1024] = dz.astype(MXU)
        dssd_ref[:, 1024:2560] = dxbc.astype(MXU)

    return pl.pallas_call(
        _after(15, body), grid=(nb, nch), name="ssd_bwd",
        in_specs=in_specs,
        out_specs=[pl.BlockSpec((CH, 2560), lambda b, c: (tok(b, c), 0)),
                   pl.BlockSpec((CH, 128), lambda b, c: (tok(b, c), 0)),
                   _const((8, CONV_CH)), _const((1, CONV_CH)), _const((1, 128)), _const((1, 128)), _const((1, 128)),
                   _const((1, 1024))],
        out_shape=[_sds((T, 2560), MXU), _sds((T, 128), MXU), _sds((8, CONV_CH), F32), _sds((1, CONV_CH), F32),
                   _sds((1, 128), F32), _sds((1, 128), F32), _sds((1, 128), F32), _sds((1, 1024), F32)],
        scratch_shapes=[pltpu.VMEM((128, 1024), F32), pltpu.VMEM((8, CONV_CH), F32)],
        compiler_params=_cp(2),
    )(proj, proj, proj, dtraw, convw, convb, dtb, alog, dskip, ng, ex, ltri, ext, sall, dcat, after)


def _inproj_bwd(duv, dssd, ddt, wm, wdt, dh1, x, g, tm, after):
    T = x.shape[0]

    def body(duv_ref, dssd_ref, ddt_ref, wm_ref, wdt_ref, dh1_ref, x_ref, g_ref, dx_ref, dg_ref):
        @pl.when(pl.program_id(0) == 0)
        def _():
            dg_ref[...] = jnp.zeros_like(dg_ref)

        dxn = (_dot_nt(duv_ref[...], wm_ref[:, 0:2048]) + _dot_nt(dssd_ref[...], wm_ref[:, 2048:N_MAIN])
               + _dot_nt(ddt_ref[...], wdt_ref[...]))
        xh, r = _rms(x_ref[...])
        dg_ref[...] += jnp.sum(dxn * xh, axis=0, keepdims=True)
        dx_ref[...] = dh1_ref[...] + _rms_bwd(dxn, xh, r, g_ref[...])

    return pl.pallas_call(
        _after(8, body), grid=(T // tm,), name="inproj_bwd",
        in_specs=[_rows(tm, 2048), _rows(tm, 2560), _rows(tm, 128), _const((D, N_MAIN)), _const((D, 128)),
                  _rows(tm, D), _rows(tm, D), _const((1, D)), _ANY],
        out_specs=[_rows(tm, D), _const((1, D))],
        out_shape=[_sds((T, D), F32), _sds((1, D), F32)],
        compiler_params=_cp(),
    )(duv, dssd, ddt, wm, wdt, dh1, x, g, after)


def _matmul_tn(a, b, name, a_fn=None):
    T, M = a.shape
    N = b.shape[1]
    tm = min(M, 1024)
    tn = 1280 if N == 2560 else min(N, 1024)
    tk = min(T, 512)

    def body(a_ref, b_ref, o_ref, acc_ref):
        k = pl.program_id(2)

        @pl.when(k == 0)
        def _():
            acc_ref[...] = jnp.zeros_like(acc_ref)

        av = a_ref[...]
        if a_fn is not None:
            av = a_fn(av)
        acc_ref[...] += _dot_tn(av, b_ref[...])

        @pl.when(k == T // tk - 1)
        def _():
            o_ref[...] = acc_ref[...].astype(o_ref.dtype)

    return pl.pallas_call(
        body, grid=(M // tm, N // tn, T // tk), name=name,
        in_specs=[pl.BlockSpec((tk, tm), lambda i, j, k: (k, i)), pl.BlockSpec((tk, tn), lambda i, j, k: (k, j))],
        out_specs=pl.BlockSpec((tm, tn), lambda i, j, k: (i, j)),
        out_shape=_sds((M, N), GRAD),
        scratch_shapes=[pltpu.VMEM((tm, tn), F32)],
        compiler_params=_cp(3),
    )(a, b)


def _adamw_vals(w, g, m, v):
    m = B1 * m + (1.0 - B1) * g
    v = B2 * v + (1.0 - B2) * (g * g)
    m_hat = m / (1.0 - B1 ** STEP)
    v_hat = v / (1.0 - B2 ** STEP)
    return -LR * (m_hat / (jnp.sqrt(v_hat) + ADAM_EPS) + WD * w), m, v


def _adamw(w, g, m, v, name):
    R, C = w.shape
    tr = 256 if R % 256 == 0 else R

    def body(w_ref, g_ref, m_ref, v_ref, d_ref, mo_ref, vo_ref):
        d_ref[...], mo_ref[...], vo_ref[...] = _adamw_vals(w_ref[...], g_ref[...], m_ref[...], v_ref[...])

    spec = _rows(tr, C)
    return pl.pallas_call(
        body, grid=(R // tr,), name=name,
        in_specs=[spec] * 4, out_specs=[spec] * 3, out_shape=[_sds((R, C), F32)] * 3,
        compiler_params=_cp(),
    )(w, g, m, v)


def _adamw_halves(w, own, other, m, v, name):
    R, C = w.shape
    half = R // 2
    tr = min(half, 256)
    nth = half // tr

    def body(w_ref, own_ref, oth_ref, m_ref, v_ref, g_ref, d_ref, mo_ref, vo_ref):
        mine = (pl.program_id(0) // nth) == lax.axis_index("c")
        g = jnp.where(mine, own_ref[...], oth_ref[...])
        g_ref[...] = g
        d_ref[...], mo_ref[...], vo_ref[...] = _adamw_vals(w_ref[...], g, m_ref[...], v_ref[...])

    full = _rows(tr, C)
    part = pl.BlockSpec((tr, C), lambda i: (i % nth, 0))
    return pl.pallas_call(
        body, grid=(R // tr,), name=name,
        in_specs=[full, part, part, full, full], out_specs=[full] * 4, out_shape=[_sds((R, C), F32)] * 4,
        compiler_params=_cp(),
    )(w, own, other, m, v)


def _sum_small(slots, name):
    nd, rows, C = slots.shape

    def body(s_ref, o_ref):
        acc = s_ref[0]
        for d in range(1, nd):
            acc = acc + s_ref[d]
        o_ref[...] = acc

    return pl.pallas_call(
        body, grid=(1,), name=name,
        in_specs=[_const((nd, rows, C))], out_specs=_const((rows, C)), out_shape=_sds((rows, C), F32),
        compiler_params=_cp(),
    )(slots)


def _sum_slots(slots, src, kind, shp, kh, name):
    R, C = shp
    rh = R // 2
    tr = min(rh, 256)
    nth = rh // tr
    if kind == "slab":
        src_spec = pl.BlockSpec((1, tr, C), lambda i, kh: (kh[0], kh[1] * nth + i, 0))
    elif kind == "rows":
        src_spec = pl.BlockSpec((tr, C), lambda i, kh: (kh[0] * (R // tr) + kh[1] * nth + i, 0))
    else:
        src_spec = pl.BlockSpec((tr, C), lambda i, kh: (kh[1] * nth + i, kh[0]))

    def body(kh_ref, s_ref, own_ref, o_ref):
        me = 2 * kh_ref[0] + kh_ref[1]
        acc = (own_ref[0] if kind == "slab" else own_ref[...]).astype(F32)
        for k in range(1, 8):
            acc = acc + s_ref[me ^ k].astype(F32)
        o_ref[...] = acc

    return pl.pallas_call(
        body, name=name,
        grid_spec=pltpu.PrefetchScalarGridSpec(
            num_scalar_prefetch=1, grid=(nth,),
            in_specs=[pl.BlockSpec((8, tr, C), lambda i, kh: (0, i, 0)), src_spec],
            out_specs=pl.BlockSpec((tr, C), lambda i, kh: (i, 0))),
        out_shape=_sds((rh, C), F32),
        compiler_params=_cp(),
    )(kh, slots, src)


def _cast_into_slot(w, kh, name):
    R, C = w.shape
    tr = 256

    def body(kh_ref, w_ref, o_ref):
        o_ref[0] = w_ref[...].astype(BF16)

    return pl.pallas_call(
        body, name=name,
        grid_spec=pltpu.PrefetchScalarGridSpec(
            num_scalar_prefetch=1, grid=(R // tr,),
            in_specs=[pl.BlockSpec((tr, C), lambda i, kh: (i, 0))],
            out_specs=pl.BlockSpec((1, tr, C), lambda i, kh: (kh[0], i, 0))),
        out_shape=_sds((4, R, C), BF16),
        compiler_params=_cp(),
    )(kh, w)


_ANY = pl.BlockSpec(memory_space=pl.ANY)
_CHIP_FLIPS = [(1, 0), (0, 1), (1, 1)]
_DEVICE_FLIPS = [(fx, fy, fc) for fx in (0, 1) for fy in (0, 1) for fc in (0, 1)][1:]


def _half(h, rows):
    return pl.ds(pl.multiple_of(h * rows, rows), rows)


def _remote(src, dst, ssem, rsem, to):
    return pltpu.make_async_remote_copy(src_ref=src, dst_ref=dst, send_sem=ssem, recv_sem=rsem,
                                        device_id=to, device_id_type=MESH)


def _weight_gather(bufs, conv):
    n = len(bufs)

    def body(*refs):
        conv_ref, outs, conv_out = refs[n], refs[n + 1:2 * n + 1], refs[2 * n + 1]
        send_sems, recv_sems, fsend_sems, frecv_sems, csend_sems, crecv_sems, local_sem = refs[2 * n + 2:]
        x, y, c = lax.axis_index("x"), lax.axis_index("y"), lax.axis_index("c")
        me = 2 * x + y
        halves = [_half(c, r.shape[1] // 2) for r in outs]
        others = [_half(1 - c, r.shape[1] // 2) for r in outs]
        remote = _remote
        local = [pltpu.make_async_copy(conv_ref, conv_out.at[me], local_sem)]
        for cp in local:
            cp.start()
        sends = []
        for k, (fx, fy) in enumerate(_CHIP_FLIPS):
            peer = (x ^ fx, y ^ fy, c)
            for i in range(n):
                mine = outs[i].at[me, halves[i]]
                sends.append(remote(mine, mine, send_sems.at[k * n + i], recv_sems.at[k * n + i], peer))
            sends.append(remote(conv_ref, conv_out.at[me], csend_sems.at[k], crecv_sems.at[k], peer))
        for cp in sends:
            cp.start()
        sibling = (x, y, 1 - c)
        forwards = []
        for k, (fx, fy) in enumerate(_CHIP_FLIPS):
            peer = (x ^ fx, y ^ fy, c)
            src = 2 * (x ^ fx) + (y ^ fy)
            for i in range(n):
                landed = outs[i].at[src, halves[i]]
                remote(landed, landed, send_sems.at[k * n + i], recv_sems.at[k * n + i], peer).wait_recv()
                fw = remote(landed, landed, fsend_sems.at[k * n + i], frecv_sems.at[k * n + i], sibling)
                fw.start()
                forwards.append(fw)
            remote(conv_out.at[src], conv_out.at[src], csend_sems.at[k], crecv_sems.at[k], peer).wait_recv()
        for k, (fx, fy) in enumerate(_CHIP_FLIPS):
            src = 2 * (x ^ fx) + (y ^ fy)
            for i in range(n):
                theirs = outs[i].at[src, others[i]]
                remote(theirs, theirs, fsend_sems.at[k * n + i], frecv_sems.at[k * n + i], sibling).wait_recv()
        for cp in sends + forwards:
            cp.wait_send()
        for cp in local:
            cp.wait()

    dma = pltpu.SemaphoreType.DMA
    return pl.pallas_call(
        body, name="weight_gather",
        in_specs=[_ANY] * (n + 1), out_specs=[_ANY] * (n + 1),
        out_shape=[_sds(b.shape, b.dtype) for b in bufs] + [_sds((4,) + conv.shape, conv.dtype)],
        input_output_aliases={i: i for i in range(n)},
        scratch_shapes=[dma((3 * n,)), dma((3 * n,)), dma((3 * n,)), dma((3 * n,)), dma((3,)), dma((3,)), dma],
    )(*bufs, conv)


def _piece(ref, kind, R, C, k, h):
    if kind == "slab":
        return ref.at[k, _half(h, R // 2), :]
    if kind == "rows":
        return ref.at[pl.ds(pl.multiple_of(k * R + h * (R // 2), R // 2), R // 2), :]
    return ref.at[_half(h, R // 2), pl.ds(pl.multiple_of(k * C, C), C)]


def _small_exchange(small):
    rs = small.shape[0]

    def body(s_ref, out_ref, send_sems, recv_sems, local_sem):
        x, y, c = lax.axis_index("x"), lax.axis_index("y"), lax.axis_index("c")
        slot = 4 * x + 2 * y + c
        own = pltpu.make_async_copy(s_ref, out_ref.at[slot], local_sem)
        own.start()
        copies = []
        for k, (fx, fy, fc) in enumerate(_DEVICE_FLIPS):
            copies.append(_remote(s_ref, out_ref.at[slot], send_sems.at[k], recv_sems.at[k], (x ^ fx, y ^ fy, c ^ fc)))
        for cp in copies:
            cp.start()
        for k, (fx, fy, fc) in enumerate(_DEVICE_FLIPS):
            theirs = out_ref.at[slot ^ (k + 1)]
            _remote(theirs, theirs, send_sems.at[k], recv_sems.at[k], (x ^ fx, y ^ fy, c ^ fc)).wait_recv()
        for cp in copies:
            cp.wait_send()
        own.wait()

    dma = pltpu.SemaphoreType.DMA
    return pl.pallas_call(
        body, name="small_exchange",
        in_specs=[_ANY], out_specs=_ANY, out_shape=_sds((8, rs, 128), F32),
        scratch_shapes=[dma((7,)), dma((7,)), dma],
    )(small)


_HBM = pl.BlockSpec(memory_space=pltpu.HBM)
_SEM = pl.BlockSpec(memory_space=pltpu.SEMAPHORE)


def _split_start(name, arrays, n_copies, plan, after=None):
    n = len(arrays)
    extra = [] if after is None else [after]

    def body(*refs):
        m = n + len(extra)
        arrs, send_sems, recv_sems, token = refs[:n], refs[m], refs[m + 1], refs[-1]
        for j, (src, dst, peer) in enumerate(plan(arrs)):
            _remote(src, dst, send_sems.at[j], recv_sems.at[j], peer).start()
        token[...] = jnp.zeros_like(token)

    dma = pltpu.SemaphoreType.DMA
    res = pl.pallas_call(
        body, name=name,
        out_shape=(dma((n_copies,)), dma((n_copies,)), *[pltpu.HBM(a.shape, a.dtype) for a in arrays],
                   _sds((8, 128), F32)),
        in_specs=[_HBM] * n + [_ANY] * len(extra),
        out_specs=(_SEM, _SEM, *[_HBM] * n, pl.BlockSpec(memory_space=pltpu.VMEM)),
        input_output_aliases={i: 2 + i for i in range(n)},
        compiler_params=pltpu.CompilerParams(has_side_effects=pltpu.SideEffectType.DATAFLOW_SIDE_EFFECTING),
    )(*[pltpu.with_memory_space_constraint(a, pltpu.HBM) for a in arrays], *extra)
    return res[0], res[1], list(res[2:2 + n]), res[-1]


def _split_wait(name, arrays, send_sems, recv_sems, plan, after):
    n = len(arrays)

    def body(*refs):
        arrs, ssems, rsems = refs[:n], refs[n], refs[n + 1]
        for j, (src, dst, peer) in enumerate(plan(arrs)):
            cp = _remote(src, dst, ssems.at[j], rsems.at[j], peer)
            cp.wait_send()
            cp.wait_recv()

    return list(pl.pallas_call(
        body, name=name,
        out_shape=tuple(pltpu.HBM(a.shape, a.dtype) for a in arrays),
        in_specs=[_HBM] * n + [_SEM, _SEM, _ANY],
        out_specs=tuple([_HBM] * n),
        input_output_aliases={i: i for i in range(n)},
        compiler_params=pltpu.CompilerParams(has_side_effects=pltpu.SideEffectType.DATAFLOW_SIDE_EFFECTING),
    )(*arrays, send_sems, recv_sems, after))


def _gather_plan(n):
    def plan(bufs):
        x, y, c = lax.axis_index("x"), lax.axis_index("y"), lax.axis_index("c")
        me = 2 * x + y
        return [(bufs[i].at[me], bufs[i].at[me], (x ^ fx, y ^ fy, c)) for fx, fy in _CHIP_FLIPS for i in range(n)]

    return plan


def _reduce_plan(specs, n_small):
    n = len(specs)

    def plan(arrs):
        x, y, c = lax.axis_index("x"), lax.axis_index("y"), lax.axis_index("c")
        slot = 4 * x + 2 * y + c
        out = []
        for fx, fy, fc in _DEVICE_FLIPS:
            peer = (x ^ fx, y ^ fy, c ^ fc)
            for i, (kind, (R, C)) in enumerate(specs):
                out.append((_piece(arrs[i], kind, R, C, 2 * peer[0] + peer[1], peer[2]), arrs[n + i].at[slot], peer))
            for s in range(n_small):
                out.append((arrs[2 * n + 2 * s], arrs[2 * n + 2 * s + 1].at[slot], peer))
        return out

    return plan


def _sibling_exchange(halves):
    n = len(halves)

    def body(*refs):
        ins, outs, send_sems, recv_sems = refs[:n], refs[n:2 * n], refs[2 * n], refs[2 * n + 1]
        sibling = (lax.axis_index("x"), lax.axis_index("y"), 1 - lax.axis_index("c"))
        copies = [pltpu.make_async_remote_copy(src_ref=ins[i], dst_ref=outs[i], send_sem=send_sems.at[i],
                                               recv_sem=recv_sems.at[i], device_id=sibling, device_id_type=MESH)
                  for i in range(n)]
        for cp in copies:
            cp.start()
        for cp in copies:
            cp.wait()

    dma = pltpu.SemaphoreType.DMA
    return pl.pallas_call(
        body, name="sibling_exchange",
        in_specs=[_ANY] * n, out_specs=[_ANY] * n,
        out_shape=[_sds(h.shape, h.dtype) for h in halves],
        scratch_shapes=[dma((n,)), dma((n,))],
    )(*halves)


_BIG = [("w_in", (1024, 1156), "slab"), ("w_out", (512, 1024), "rows"), ("w_ff1", (1024, 1024), "cols"),
        ("w_ff2", (1024, 1024), "rows"), ("w_ple_gate", (256, 1024), "rows"), ("w_ple_proj", (256, 256), "cols")]
_SMALL = [("norm_mix_g", (1, 1024)), ("gm_v_norm_g", (1, 1024)), ("gm_ws", (1, 8, 128, 128)), ("gm_bs", (1, 8, 128)),
          ("gm_out_norm_g", (1, 1024)), ("ssd_conv_w", (1, 4, 1536)), ("ssd_conv_b", (1, 1536)),
          ("ssd_dt_bias", (1, 16)), ("ssd_a_log", (1, 16)), ("ssd_d", (1, 16)), ("ssd_norm_g", (1, 1024)),
          ("norm_mlp_g", (1, 1024)), ("ple_norm_g", (1, 1024)), ("final_norm_g", (1024,))]


def _rows128(a):
    flat = a.reshape(-1)
    rows = -(-flat.shape[0] // 1024) * 8
    return jnp.pad(flat, (0, rows * 128 - flat.shape[0])).reshape(rows, 128)


def _pad_lanes(v, n=128):
    v = v.reshape(1, -1)
    return jnp.pad(v, ((0, 0), (0, n - v.shape[1])))


_SMALL_SHAPES = dict(_SMALL + [("loss", ())])
_BIG_SPECS = {n: (kind, shp) for n, shp, kind in _BIG}


class _Comm:
    def __init__(self, a, kh):
        self.a, self.kh = a, kh
        self.bufs = {n: _cast_into_slot(a[n].reshape(shp), kh, "cast_" + n) for n, shp, _ in _BIG}
        self.sent = []
        self.small_tot = {}

    def w_in(self):
        (g_win,), g_cw = self._gather_now()
        rest = [self.bufs[n] for n, _, _ in _BIG[1:]]
        plan = _gather_plan(len(rest))
        ssem, rsem, thru, token = _split_start("gather_start", rest, 3 * len(rest), plan, after=g_cw)
        self.gather = (plan, ssem, rsem, thru)
        w_in_full = jnp.concatenate([g_win[k] for k in range(4)], axis=1)
        wm, wdt = w_in_full[:, :N_MAIN], jnp.pad(w_in_full[:, N_MAIN:], ((0, 0), (0, 128 - 16)))
        return wm, wdt, jnp.concatenate([g_cw[k] for k in range(4)], axis=1), token

    def _gather_now(self):
        *bufs, g_cw = _weight_gather([self.bufs["w_in"]], self.a["ssd_conv_w"].reshape(4, 384))
        return bufs, g_cw

    def rest(self, after):
        plan, ssem, rsem, thru = self.gather
        g_wo, g_w1, g_w2, g_wg, g_wp = _split_wait("gather_wait", thru, ssem, rsem, plan, after)
        return g_wo.reshape(2048, D), g_w1, g_w2.reshape(DFF, D), g_wg.reshape(D, D), g_wp

    def send(self, tag, grads):
        big = [n for n, _, _ in _BIG if n in grads]
        small = [n for n in _SMALL_SHAPES if n in grads]
        parts = [_rows128(grads[n]) for n in small]
        rows = [s.shape[0] for s in parts]
        if not big:
            self._unpack(_sum_small(_small_exchange(jnp.concatenate(parts, axis=0)), "sum_small_" + tag), small, rows)
            return None
        srcs = [jnp.stack([grads[n][:, 1156 * k:1156 * (k + 1)] for k in range(4)]) if n == "w_in" else grads[n]
                for n in big]
        lands = [lax.empty((8, _BIG_SPECS[n][1][0] // 2, _BIG_SPECS[n][1][1]), GRAD) for n in big]
        extra = []
        if small:
            pack = jnp.concatenate(parts, axis=0)
            extra = [pack, jnp.broadcast_to(pack, (8,) + pack.shape)]
        plan = _reduce_plan([_BIG_SPECS[n] for n in big], len(extra) // 2)
        n_copies = 7 * (len(big) + len(extra) // 2)
        ssem, rsem, thru, token = _split_start("reduce_start_" + tag, srcs + lands + extra, n_copies, plan)
        self.sent.append((tag, big, small, rows, plan, ssem, rsem, thru))
        return token

    def _unpack(self, tot, names, rows):
        o = 0
        for n, r in zip(names, rows):
            shp = _SMALL_SHAPES[n]
            cnt = 1
            for s in shp:
                cnt *= s
            self.small_tot[n] = tot[o:o + r].reshape(-1)[:cnt].reshape(shp)
            o += r

    def finish(self, after):
        own = {}
        for tag, big, small, rows, plan, ssem, rsem, thru in self.sent:
            arrs = _split_wait("reduce_wait_" + tag, thru, ssem, rsem, plan, after)
            nb_ = len(big)
            for i, n in enumerate(big):
                kind, shp = _BIG_SPECS[n]
                own[n] = _sum_slots(arrs[nb_ + i], arrs[i], kind, shp, self.kh, "sum_" + n)
                after = own[n]
            if small:
                self._unpack(_sum_small(arrs[2 * nb_ + 1], "sum_small_" + tag), small, rows)
        return [own[n] for n, _, _ in _BIG], dict(self.small_tot)


def _local_step(x, p, tgt, sm, comm, nb, tm):
    wm, wdt, conv_w, token = comm.w_in()
    g_mix, gv, gout = sm["norm_mix_g"].reshape(1, D), sm["gm_v_norm_g"].reshape(1, D), sm["gm_out_norm_g"].reshape(1, D)
    ws = sm["gm_ws"].reshape(GM_HEADS, CH, CH)
    bst = jnp.pad(sm["gm_bs"].reshape(GM_HEADS, CH).T, ((0, 0), (0, 128 - GM_HEADS)))
    convw = jnp.pad(conv_w, ((0, 4), (0, 0)))
    convb = sm["ssd_conv_b"].reshape(1, CONV_CH)
    dtb, alog, dskip = _pad_lanes(sm["ssd_dt_bias"]), _pad_lanes(sm["ssd_a_log"]), _pad_lanes(sm["ssd_d"])
    ng, g_mlp, g_ple = sm["ssd_norm_g"].reshape(1, D), sm["norm_mlp_g"].reshape(1, D), sm["ple_norm_g"].reshape(1, D)
    gf = sm["final_norm_g"].reshape(1, D)
    head_of_lane = lax.broadcasted_iota(jnp.int32, (128, 1024), 1) // SSD_P
    ex = (lax.broadcasted_iota(jnp.int32, (128, 1024), 0) == head_of_lane).astype(F32)
    ext = ex.T
    ltri = (lax.broadcasted_iota(jnp.int32, (CH, CH), 0) >= lax.broadcasted_iota(jnp.int32, (CH, CH), 1)).astype(F32)

    proj, dtraw, xn = _inproj(x, g_mix, wm, wdt, tm, token)
    cat = _gmlp_fwd(proj, gv, ws, bst, gout)
    cat, sall = _ssd_fwd(proj, dtraw, cat, convw, convb, dtb, alog, dskip, ng, ex, ltri, nb)
    wo, w1, w2, wg, wp = comm.rest(cat)
    h1, hn = _outproj(cat, wo, x, g_mlp, tm)
    hid = _ff1(hn, w1, tm)
    h2, hp = _ff2(hid, w2, h1, g_ple, tm)
    dh3, dgl, dpe, loss, d_gf = _tail(h2, hp, p, tgt, wg, wp, gf, tm)

    d_wp = _matmul_tn(p, dpe, "dw_ple_proj", a_fn=lambda a: a.astype(MXU))
    d_wg = _matmul_tn(hp, dgl, "dw_ple_gate")
    dh2, dh2b, d_gple = _ple_bwd(dgl, wg, dh3, h2, g_ple, tm)
    d_w2 = _matmul_tn(hid, dh2b, "dw_ff2", a_fn=_sq)
    dpre = _ff2_bwd(dh2b, w2, hid, tm)
    d_w1 = _matmul_tn(hn, dpre, "dw_ff1")
    token = comm.send("a", {"w_ple_proj": d_wp, "w_ple_gate": d_wg, "w_ff2": d_w2, "w_ff1": d_w1})
    dh1, dh1b, d_gmlp = _ff1_bwd(dpre, w1, dh2, h1, g_mlp, tm, token)
    dcat = _outproj_bwd(dh1b, wo, tm)
    d_wo = _matmul_tn(cat, dh1b, "dw_out")
    duv, d_gv, d_ws, d_bst, d_gout = _gmlp_bwd(proj, dcat, gv, ws, bst, gout)
    token = comm.send("b", {
        "w_out": d_wo, "loss": loss[0:1, 0:1], "final_norm_g": d_gf, "ple_norm_g": d_gple, "norm_mlp_g": d_gmlp,
        "gm_v_norm_g": d_gv, "gm_ws": d_ws, "gm_bs": d_bst[:, :GM_HEADS].T, "gm_out_norm_g": d_gout})
    dssd, ddt, d_cw, d_cb, d_dtb, d_al, d_ds, d_ng = _ssd_bwd(
        proj, dtraw, sall, dcat, convw, convb, dtb, alog, dskip, ng, ex, ltri, ext, nb, token)
    d_win = jnp.concatenate([_matmul_tn(xn, duv, "dw_in_uv"), _matmul_tn(xn, dssd, "dw_in_ssd"),
                             _matmul_tn(xn, ddt, "dw_in_dt")[:, :16]], axis=1)
    token = comm.send("c", {"w_in": d_win})
    dx, d_gmix = _inproj_bwd(duv, dssd, ddt, wm, wdt, dh1, x, g_mix, tm, token)
    comm.send("d", {"norm_mix_g": d_gmix, "ssd_conv_w": d_cw[0:4], "ssd_conv_b": d_cb, "ssd_dt_bias": d_dtb[:, :16],
                    "ssd_a_log": d_al[:, :16], "ssd_d": d_ds[:, :16], "ssd_norm_g": d_ng})
    return dx


def kernel(x, p, norm_mix_g, w_in, gm_v_norm_g, gm_ws, gm_bs, gm_out_norm_g, ssd_conv_w, ssd_conv_b, ssd_dt_bias, ssd_a_log, ssd_d, ssd_norm_g, w_out, norm_mlp_g, w_ff1, w_ff2, ple_norm_g, w_ple_gate, w_ple_proj, final_norm_g, loss_target, m_norm_mix_g, m_w_in, m_gm_v_norm_g, m_gm_ws, m_gm_bs, m_gm_out_norm_g, m_ssd_conv_w, m_ssd_conv_b, m_ssd_dt_bias, m_ssd_a_log, m_ssd_d, m_ssd_norm_g, m_w_out, m_norm_mlp_g, m_w_ff1, m_w_ff2, m_ple_norm_g, m_w_ple_gate, m_w_ple_proj, m_final_norm_g, v_norm_mix_g, v_w_in, v_gm_v_norm_g, v_gm_ws, v_gm_bs, v_gm_out_norm_g, v_ssd_conv_w, v_ssd_conv_b, v_ssd_dt_bias, v_ssd_a_log, v_ssd_d, v_ssd_norm_g, v_w_out, v_norm_mlp_g, v_w_ff1, v_w_ff2, v_ple_norm_g, v_w_ple_gate, v_w_ple_proj, v_final_norm_g):
    a = dict(locals())
    order = ["norm_mix_g", "w_in", "gm_v_norm_g", "gm_ws", "gm_bs", "gm_out_norm_g", "ssd_conv_w", "ssd_conv_b",
             "ssd_dt_bias", "ssd_a_log", "ssd_d", "ssd_norm_g", "w_out", "norm_mlp_g", "w_ff1", "w_ff2", "ple_norm_g",
             "w_ple_gate", "w_ple_proj", "final_norm_g"]
    chip = 2 * lax.axis_index("x") + lax.axis_index("y")
    nb, S = x.shape[0], x.shape[1]
    T = nb * S
    sm = {n: a[n] for n, _ in _SMALL if n != "ssd_conv_w"}
    comm = _Comm(a, jnp.stack([chip, lax.axis_index("c")]).astype(jnp.int32))
    dx = _local_step(x.reshape(T, D), p.reshape(T, DPLE), loss_target.reshape(T, D), sm, comm, nb, 256)
    own, g_out = comm.finish(dx)
    other = _sibling_exchange(own)

    delta, new_m, new_v = {}, {}, {}
    for i, (n, shp, _) in enumerate(_BIG):
        res = _adamw_halves(a[n].reshape(shp), own[i], other[i], a["m_" + n].reshape(shp), a["v_" + n].reshape(shp),
                            "adamw_" + n)
        g_out[n], delta[n], new_m[n], new_v[n] = (r.reshape(a[n].shape) for r in res)
    g_out["ssd_conv_w"] = lax.dynamic_slice(g_out["ssd_conv_w"], (0, 0, chip * 384), (1, 4, 384))
    small_names = [n for n, _ in _SMALL]
    packs = [jnp.concatenate([_rows128(src(n)) for n in small_names], axis=0)
             for src in (lambda n: a[n], lambda n: g_out[n], lambda n: a["m_" + n], lambda n: a["v_" + n])]
    outs = _adamw(*packs, "adamw_small")
    o = 0
    for n in small_names:
        r = _rows128(a[n]).shape[0]
        cnt = a[n].size
        for dst, src in zip((delta, new_m, new_v), outs):
            dst[n] = src[o:o + r].reshape(-1)[:cnt].reshape(a[n].shape)
        o += r
    return (g_out["loss"], dx.reshape(x.shape), *[g_out[n] for n in order], *[delta[n] for n in order],
            *[new_m[n] for n in order], *[new_v[n] for n in order])
```

```python
import jax
import jax.numpy as jnp
from jax import lax
from jax.experimental import pallas as pl
from jax.experimental.pallas import tpu as pltpu

F32 = jnp.float32
BF16 = jnp.bfloat16
MXU = jnp.bfloat16
GRAD = jnp.bfloat16

D = 1024
CH = 128
GM_HEADS = 8
SSD_HEADS = 16
SSD_P = 64
CONV_CH = 1536
N_MAIN = 4608
DFF = 4096
DPLE = 256
EPS = 1e-6
NEG = -1e30

LR, B1, B2, ADAM_EPS, WD, STEP = 0.001, 0.9, 0.999, 1e-08, 0.01, 10

VMEM_LIMIT = 56 * 1024 * 1024
MESH = pl.DeviceIdType.MESH

INV_SQRT2 = 0.7071067811865476
INV_SQRT_2PI = 0.3989422804014327


def _cp(n_axes=1):
    return pltpu.CompilerParams(dimension_semantics=("arbitrary",) * n_axes, vmem_limit_bytes=VMEM_LIMIT)


def _dot(a, b):
    return jnp.dot(a, b, preferred_element_type=F32)


def _dot_nt(a, b):
    return lax.dot_general(a, b, (((1,), (1,)), ((), ())), preferred_element_type=F32)


def _dot_tn(a, b):
    return lax.dot_general(a, b, (((0,), (0,)), ((), ())), preferred_element_type=F32)


def _dot_hi(a, b):
    return jnp.dot(a, b, preferred_element_type=F32, precision=lax.Precision.HIGHEST)


def _dot_01(a, sel):
    hi = a.astype(BF16)
    lo = (a - hi.astype(F32)).astype(BF16)
    n = a.shape[0]
    r = _dot(jnp.concatenate([hi, lo], axis=0), sel)
    return r[0:n] + r[n:2 * n]


def _rows(tm, n, j=0):
    return pl.BlockSpec((tm, n), lambda i: (i, j))


def _const(shape):
    nd = len(shape)
    return pl.BlockSpec(shape, lambda *_: (0,) * nd)


def _sds(shape, dtype):
    return jax.ShapeDtypeStruct(shape, dtype)


def _rms(x):
    r = lax.rsqrt(jnp.mean(x * x, axis=-1, keepdims=True) + EPS)
    return x * r, r


def _rms_bwd(dy, xhat, r, g):
    dyg = dy * g
    return r * (dyg - xhat * jnp.mean(dyg * xhat, axis=-1, keepdims=True))


def _sigmoid(x):
    return 1.0 / (1.0 + jnp.exp(-x))


def _gelu(x):
    cdf = 0.5 * (1.0 + lax.erf(x * INV_SQRT2))
    pdf = jnp.exp(-0.5 * x * x) * INV_SQRT_2PI
    return x * cdf, cdf + x * pdf


def _softplus(x):
    e = jnp.exp(-jnp.abs(x))
    u = 1.0 + e
    log1p = jnp.where(u == 1.0, e, jnp.log(u) * e / (u - 1.0))
    return jnp.maximum(x, 0.0) + log1p


def _after(n_in, fn):
    def body(*refs):
        return fn(*refs[:n_in], *refs[n_in + 1:])

    return body


def _inproj(x, g, wm, wdt, tm, after):
    T = x.shape[0]

    def body(x_ref, g_ref, wm_ref, wdt_ref, proj_ref, dt_ref, xn_ref):
        xh, _ = _rms(x_ref[...])
        xn = (xh * g_ref[...]).astype(MXU)
        xn_ref[...] = xn
        for n in range(N_MAIN // 512):
            proj_ref[:, n * 512:(n + 1) * 512] = _dot(xn, wm_ref[:, n * 512:(n + 1) * 512])
        dt_ref[...] = _dot(xn, wdt_ref[...])

    return pl.pallas_call(
        _after(4, body), grid=(T // tm,), name="inproj",
        in_specs=[_rows(tm, D), _const((1, D)), _const((D, N_MAIN)), _const((D, 128)), _ANY],
        out_specs=[_rows(tm, N_MAIN), _rows(tm, 128), _rows(tm, D)],
        out_shape=[_sds((T, N_MAIN), F32), _sds((T, 128), F32), _sds((T, D), MXU)],
        compiler_params=_cp(),
    )(x, g, wm, wdt, after)


def _gmlp_fwd_vals(u, v, gv, ws_ref, bst, gout):
    ug, dug = _gelu(u)
    vg, dvg = _gelu(v)
    row = lax.broadcasted_iota(jnp.int32, (CH, CH), 0)
    col = lax.broadcasted_iota(jnp.int32, (CH, CH), 1)
    tril = row >= col
    ys, heads = [], []
    for h in range(GM_HEADS):
        sl = slice(h * 128, (h + 1) * 128)
        vhat, rv = _rms(vg[:, sl])
        vn = (vhat * gv[:, sl]).astype(MXU)
        wt = jnp.where(tril, ws_ref[h], 0.0)
        mixed = _dot(wt.astype(MXU), vn) + bst[:, h:h + 1]
        ys.append(ug[:, sl] * mixed)
        heads.append((vhat, rv, vn, wt, mixed))
    y = jnp.concatenate(ys, axis=1)
    yhat, ry = _rms(y)
    return dict(ug=ug, dug=dug, dvg=dvg, heads=heads, yhat=yhat, ry=ry, tril=tril, out=yhat * gout)


def _gmlp_fwd(proj, gv, ws, bst, gout):
    T = proj.shape[0]

    nck = 4 if T % (4 * CH) == 0 else 1
    tb = nck * CH

    def body(u_ref, v_ref, gv_ref, ws_ref, bst_ref, gout_ref, ya_ref):
        for k in range(nck):
            sl = slice(k * CH, (k + 1) * CH)
            f = _gmlp_fwd_vals(u_ref[sl, :], v_ref[sl, :], gv_ref[...], ws_ref, bst_ref[...], gout_ref[...])
            ya_ref[sl, :] = f["out"].astype(MXU)

    return pl.pallas_call(
        body, grid=(T // tb,), name="gmlp_fwd",
        in_specs=[_rows(tb, 1024, 0), _rows(tb, 1024, 1), _const((1, 1024)), _const((GM_HEADS, CH, CH)),
                  _const((CH, 128)), _const((1, 1024))],
        out_specs=_rows(tb, 1024, 0),
        out_shape=_sds((T, 2048), MXU),
        compiler_params=_cp(),
    )(proj, proj, gv, ws, bst, gout)


def _shifts_down(cur, halo):
    row8 = lax.broadcasted_iota(jnp.int32, (8, cur.shape[1]), 0)
    out = [cur]
    for j in (1, 2, 3):
        sh = pltpu.roll(cur, j, 0)
        top = jnp.where(row8 < j, pltpu.roll(halo, j, 0), sh[0:8])
        out.append(jnp.concatenate([top, sh[8:]], axis=0))
    return out


def _shifts_up(cur, halo):
    row8 = lax.broadcasted_iota(jnp.int32, (8, cur.shape[1]), 0)
    out = []
    for j in (1, 2, 3):
        sh = pltpu.roll(cur, CH - j, 0)
        bot = jnp.where(row8 + j >= 8, pltpu.roll(halo, 8 - j, 0), sh[CH - 8:CH])
        out.append(jnp.concatenate([sh[0:CH - 8], bot], axis=0))
    return out


def _ssd_fwd_vals(z, xbc, halo, dtraw, convw, convb, dtb, alog, dskip, ng, ex, ltri, s_prev):
    shifts = _shifts_down(xbc, halo)
    conv = convb + convw[3:4] * shifts[0] + convw[2:3] * shifts[1] + convw[1:2] * shifts[2] + convw[0:1] * shifts[3]
    sig_c = _sigmoid(conv)
    xa = conv * sig_c
    xs = xa[:, :1024]
    bm = [xa[:, 1024:1152], xa[:, 1152:1280]]
    cm = [xa[:, 1280:1408], xa[:, 1408:1536]]
    dtpre = dtraw + dtb
    dt = _softplus(dtpre)
    a_neg = -jnp.exp(alog)
    cs = _dot_hi(ltri, dt * a_neg)
    cst = cs.T
    last = cs[CH - 1:CH]
    ecs = jnp.exp(cs)
    dec = jnp.exp(last - cs)
    spread = _dot_01(jnp.concatenate([dt, ecs, dec], axis=0), ex)
    dte, ecse, dece = spread[0:CH], spread[CH:2 * CH], spread[2 * CH:3 * CH]
    cde = ecse[CH - 1:CH]
    de = dskip
    xdt = xs * dte
    row = lax.broadcasted_iota(jnp.int32, (CH, CH), 0)
    col = lax.broadcasted_iota(jnp.int32, (CH, CH), 1)
    tril = row >= col
    lo = col < SSD_P
    bmb = [b.astype(MXU) for b in bm]
    cmb = [c.astype(MXU) for c in cm]
    mg = [_dot_nt(cmb[g], bmb[g]) for g in range(2)]
    yd, lms, whs = [], [], []
    for q in range(8):
        g = q // 4
        xq = xdt[:, q * 128:(q + 1) * 128]
        acc = None
        for hh in range(2):
            h = 2 * q + hh
            seg = cs[:, h:h + 1] - cst[h:h + 1, :]
            lm = jnp.exp(jnp.where(tril, seg, NEG))
            wh = (mg[g] * lm).astype(MXU)
            xm = jnp.where(lo if hh == 0 else ~lo, xq, 0.0).astype(MXU)
            part = _dot(wh, xm)
            acc = part if acc is None else acc + part
            lms.append(lm)
            whs.append(wh)
        yd.append(acc)
    yd = jnp.concatenate(yd, axis=1)
    sb = s_prev.astype(MXU)
    yo = jnp.concatenate([_dot(cmb[g], sb[:, g * 512:(g + 1) * 512]) for g in range(2)], axis=1) * ecse
    xdec = (xdt * dece).astype(MXU)
    states = jnp.concatenate([_dot_tn(bmb[g], xdec[:, g * 512:(g + 1) * 512]) for g in range(2)], axis=1)
    s_next = s_prev * cde + states
    ypre = yd + yo + de * xs
    sig_z = _sigmoid(z)
    yg = ypre * z * sig_z
    outs, yhat, rr = [], [], []
    for g in range(2):
        sl = slice(g * 512, (g + 1) * 512)
        yh, r = _rms(yg[:, sl])
        yhat.append(yh)
        rr.append(r)
        outs.append(yh * ng[:, sl])
    return dict(shifts=shifts, conv=conv, sig_c=sig_c, xs=xs, bmb=bmb, cmb=cmb, dtpre=dtpre, dt=dt, a_neg=a_neg,
                cs=cs, last=last, ecs=ecs, dec=dec, dte=dte, ecse=ecse, dece=dece, cde=cde, de=de, xdt=xdt,
                mg=mg, lms=lms, whs=whs, lo=lo, yo=yo, sb=sb, xdec=xdec, s_next=s_next, ypre=ypre, sig_z=sig_z,
                yhat=yhat, rr=rr, out=jnp.concatenate(outs, axis=1))


def _ssd_specs(nch, rev):
    def tok(b, c):
        return b * nch + ((nch - 1 - c) if rev else c)

    return tok, [
        pl.BlockSpec((CH, 1024), lambda b, c: (tok(b, c), 2)),
        pl.BlockSpec((CH, CONV_CH), lambda b, c: (tok(b, c), 2)),
        pl.BlockSpec((8, CONV_CH), lambda b, c: (jnp.maximum(tok(b, c) * (CH // 8) - 1, 0), 2)),
        pl.BlockSpec((CH, 128), lambda b, c: (tok(b, c), 0)),
        _const((8, CONV_CH)), _const((1, CONV_CH)), _const((1, 128)), _const((1, 128)), _const((1, 1024)),
        _const((1, 1024)), _const((128, 1024)), _const((CH, CH)),
    ]


def _ssd_fwd(proj, dtraw, cat, convw, convb, dtb, alog, dskip, ng, ex, ltri, nb):
    T = proj.shape[0]
    nch = T // CH // nb
    tok, in_specs = _ssd_specs(nch, rev=False)

    def body(z_ref, xbc_ref, halo_ref, dt_ref, cw_ref, cb_ref, dtb_ref, al_ref, ds_ref, ng_ref, ex_ref, lt_ref,
             cat_in_ref, yb_ref, sall_ref, s_ref):
        del cat_in_ref
        c = pl.program_id(1)

        @pl.when(c == 0)
        def _():
            s_ref[...] = jnp.zeros_like(s_ref)

        halo = jnp.where(c == 0, 0.0, halo_ref[...])
        s_prev = s_ref[...]
        sall_ref[0] = s_prev
        f = _ssd_fwd_vals(z_ref[...], xbc_ref[...], halo, dt_ref[...], cw_ref[...], cb_ref[...], dtb_ref[...],
                          al_ref[...], ds_ref[...], ng_ref[...], ex_ref[...], lt_ref[...], s_prev)
        s_ref[...] = f["s_next"]
        yb_ref[...] = f["out"].astype(MXU)

    return pl.pallas_call(
        body, grid=(nb, nch), name="ssd_fwd",
        in_specs=in_specs + [_ANY],
        out_specs=[pl.BlockSpec((CH, 1024), lambda b, c: (tok(b, c), 1)),
                   pl.BlockSpec((1, 128, 1024), lambda b, c: (tok(b, c), 0, 0))],
        out_shape=[_sds((T, 2048), MXU), _sds((T // CH, 128, 1024), F32)],
        scratch_shapes=[pltpu.VMEM((128, 1024), F32)],
        input_output_aliases={12: 0},
        compiler_params=_cp(2),
    )(proj, proj, proj, dtraw, convw, convb, dtb, alog, dskip, ng, ex, ltri, cat)


def _outproj(cat, wo, x, g, tm):
    T = x.shape[0]

    def body(cat_ref, wo_ref, x_ref, g_ref, h1_ref, hn_ref):
        h1 = x_ref[...] + _dot(cat_ref[...], wo_ref[...])
        h1_ref[...] = h1
        hn_ref[...] = (_rms(h1)[0] * g_ref[...]).astype(MXU)

    return pl.pallas_call(
        body, grid=(T // tm,), name="outproj",
        in_specs=[_rows(tm, 2048), _const((2048, D)), _rows(tm, D), _const((1, D))],
        out_specs=[_rows(tm, D), _rows(tm, D)],
        out_shape=[_sds((T, D), F32), _sds((T, D), MXU)],
        compiler_params=_cp(),
    )(cat, wo, x, g)


def _ff1(hn, w1, tm):
    T = hn.shape[0]

    def body(hn_ref, w1_ref, hid_ref):
        hn_v = hn_ref[...]
        for n in range(4):
            hid_ref[:, n * 1024:(n + 1) * 1024] = jnp.maximum(_dot(hn_v, w1_ref[n]), 0.0).astype(MXU)

    return pl.pallas_call(
        body, grid=(T // tm,), name="ff1",
        in_specs=[_rows(tm, D), _const((4, D, 1024))],
        out_specs=_rows(tm, DFF),
        out_shape=_sds((T, DFF), MXU),
        compiler_params=_cp(),
    )(hn, w1)


def _sq(hid):
    h = hid.astype(F32)
    return (h * h).astype(MXU)


def _ff2(hid, w2, h1, g, tm):
    T = h1.shape[0]

    def body(hid_ref, w2_ref, h1_ref, g_ref, h2_ref, hp_ref):
        h2 = h1_ref[...] + _dot(_sq(hid_ref[...]), w2_ref[...])
        h2_ref[...] = h2
        hp_ref[...] = (_rms(h2)[0] * g_ref[...]).astype(MXU)

    return pl.pallas_call(
        body, grid=(T // tm,), name="ff2",
        in_specs=[_rows(tm, DFF), _const((DFF, D)), _rows(tm, D), _const((1, D))],
        out_specs=[_rows(tm, D), _rows(tm, D)],
        out_shape=[_sds((T, D), F32), _sds((T, D), MXU)],
        compiler_params=_cp(),
    )(hid, w2, h1, g)


def _tail(h2, hp, p, tgt, wg, wp, gf, tm):
    T = h2.shape[0]

    def body(h2_ref, hp_ref, p_ref, t_ref, wg_ref, wp_ref, gf_ref, dh3_ref, dgl_ref, dpe_ref, loss_ref, dgf_ref):
        @pl.when(pl.program_id(0) == 0)
        def _():
            loss_ref[...] = jnp.zeros_like(loss_ref)
            dgf_ref[...] = jnp.zeros_like(dgf_ref)

        gate = _sigmoid(_dot(hp_ref[...], wg_ref[...]))
        pb = p_ref[...].astype(MXU)
        pe = jnp.concatenate([_dot(pb, wp_ref[k]) for k in range(4)], axis=1)
        h3 = h2_ref[...] + gate * pe
        hh, r = _rms(h3)
        gf = gf_ref[...]
        diff = hh * gf - t_ref[...]
        loss_ref[...] += 0.5 * jnp.sum(jnp.mean(diff * diff, axis=-1, keepdims=True))
        dout = diff * (1.0 / D)
        dgf_ref[...] += jnp.sum(dout * hh, axis=0, keepdims=True)
        dh3 = _rms_bwd(dout, hh, r, gf)
        dh3_ref[...] = dh3
        dgl_ref[...] = (dh3 * pe * gate * (1.0 - gate)).astype(MXU)
        dpe_ref[...] = (dh3 * gate).astype(MXU)

    return pl.pallas_call(
        body, grid=(T // tm,), name="tail",
        in_specs=[_rows(tm, D), _rows(tm, D), _rows(tm, DPLE), _rows(tm, D), _const((D, D)), _const((4, DPLE, 256)),
                  _const((1, D))],
        out_specs=[_rows(tm, D), _rows(tm, D), _rows(tm, D), _const((8, 128)), _const((1, D))],
        out_shape=[_sds((T, D), F32), _sds((T, D), MXU), _sds((T, D), MXU), _sds((8, 128), F32), _sds((1, D), F32)],
        compiler_params=_cp(),
    )(h2, hp, p, tgt, wg, wp, gf)


def _ple_bwd(dgl, wg, dh3, h2, g, tm):
    T = h2.shape[0]

    def body(dgl_ref, wg_ref, dh3_ref, h2_ref, g_ref, dh2_ref, dh2b_ref, dg_ref):
        @pl.when(pl.program_id(0) == 0)
        def _():
            dg_ref[...] = jnp.zeros_like(dg_ref)

        dhp = _dot_nt(dgl_ref[...], wg_ref[...])
        hh, r = _rms(h2_ref[...])
        dg_ref[...] += jnp.sum(dhp * hh, axis=0, keepdims=True)
        dh2 = dh3_ref[...] + _rms_bwd(dhp, hh, r, g_ref[...])
        dh2_ref[...] = dh2
        dh2b_ref[...] = dh2.astype(MXU)

    return pl.pallas_call(
        body, grid=(T // tm,), name="ple_bwd",
        in_specs=[_rows(tm, D), _const((D, D)), _rows(tm, D), _rows(tm, D), _const((1, D))],
        out_specs=[_rows(tm, D), _rows(tm, D), _const((1, D))],
        out_shape=[_sds((T, D), F32), _sds((T, D), MXU), _sds((1, D), F32)],
        compiler_params=_cp(),
    )(dgl, wg, dh3, h2, g)


def _ff2_bwd(dh2b, w2, hid, tm):
    T = hid.shape[0]

    def body(dh2b_ref, w2_ref, hid_ref, dpre_ref):
        d = dh2b_ref[...]
        for n in range(DFF // 1024):
            sl = slice(n * 1024, (n + 1) * 1024)
            da = _dot_nt(d, w2_ref[sl, :])
            dpre_ref[:, sl] = (2.0 * da * hid_ref[:, sl].astype(F32)).astype(MXU)

    return pl.pallas_call(
        body, grid=(T // tm,), name="ff2_bwd",
        in_specs=[_rows(tm, D), _const((DFF, D)), _rows(tm, DFF)],
        out_specs=_rows(tm, DFF),
        out_shape=_sds((T, DFF), MXU),
        compiler_params=_cp(),
    )(dh2b, w2, hid)


def _ff1_bwd(dpre, w1, dh2, h1, g, tm, after):
    T = h1.shape[0]

    def body(dpre_ref, w1_ref, dh2_ref, h1_ref, g_ref, dh1_ref, dh1b_ref, dg_ref):
        @pl.when(pl.program_id(0) == 0)
        def _():
            dg_ref[...] = jnp.zeros_like(dg_ref)

        dhn = _dot_nt(dpre_ref[:, 0:1024], w1_ref[0])
        for k in range(1, 4):
            dhn = dhn + _dot_nt(dpre_ref[:, k * 1024:(k + 1) * 1024], w1_ref[k])
        hh, r = _rms(h1_ref[...])
        dg_ref[...] += jnp.sum(dhn * hh, axis=0, keepdims=True)
        dh1 = dh2_ref[...] + _rms_bwd(dhn, hh, r, g_ref[...])
        dh1_ref[...] = dh1
        dh1b_ref[...] = dh1.astype(MXU)

    return pl.pallas_call(
        _after(5, body), grid=(T // tm,), name="ff1_bwd",
        in_specs=[_rows(tm, DFF), _const((4, D, 1024)), _rows(tm, D), _rows(tm, D), _const((1, D)), _ANY],
        out_specs=[_rows(tm, D), _rows(tm, D), _const((1, D))],
        out_shape=[_sds((T, D), F32), _sds((T, D), MXU), _sds((1, D), F32)],
        compiler_params=_cp(),
    )(dpre, w1, dh2, h1, g, after)


def _outproj_bwd(dh1b, wo, tm):
    T = dh1b.shape[0]

    def body(d_ref, wo_ref, dcat_ref):
        d = d_ref[...]
        dcat_ref[:, 0:1024] = _dot_nt(d, wo_ref[0:1024, :])
        dcat_ref[:, 1024:2048] = _dot_nt(d, wo_ref[1024:2048, :])

    return pl.pallas_call(
        body, grid=(T // tm,), name="outproj_bwd",
        in_specs=[_rows(tm, D), _const((2048, D))],
        out_specs=_rows(tm, 2048),
        out_shape=_sds((T, 2048), F32),
        compiler_params=_cp(),
    )(dh1b, wo)


def _gmlp_bwd(proj, dcat, gv, ws, bst, gout):
    T = proj.shape[0]
    nck = 2 if T % (2 * CH) == 0 else 1
    tb = nck * CH

    def body(u_ref, v_ref, dya_ref, gv_ref, ws_ref, bst_ref, gout_ref, duv_ref, dgv_ref, dws_ref, dbst_ref, dgo_ref):
        @pl.when(pl.program_id(0) == 0)
        def _():
            dgv_ref[...] = jnp.zeros_like(dgv_ref)
            dws_ref[...] = jnp.zeros_like(dws_ref)
            dbst_ref[...] = jnp.zeros_like(dbst_ref)
            dgo_ref[...] = jnp.zeros_like(dgo_ref)

        for k in range(nck):
            chunk(slice(k * CH, (k + 1) * CH), u_ref, v_ref, dya_ref, gv_ref, ws_ref, bst_ref, gout_ref, duv_ref,
                  dgv_ref, dws_ref, dbst_ref, dgo_ref)

    def chunk(rows, u_ref, v_ref, dya_ref, gv_ref, ws_ref, bst_ref, gout_ref, duv_ref, dgv_ref, dws_ref, dbst_ref,
              dgo_ref):
        gv = gv_ref[...]
        f = _gmlp_fwd_vals(u_ref[rows, :], v_ref[rows, :], gv, ws_ref, bst_ref[...], gout_ref[...])
        dya = dya_ref[rows, :]
        dgo_ref[...] += jnp.sum(dya * f["yhat"], axis=0, keepdims=True)
        dy = _rms_bwd(dya, f["yhat"], f["ry"], gout_ref[...])
        lane = lax.broadcasted_iota(jnp.int32, (CH, 128), 1)
        dbs = jnp.zeros((CH, 128), F32)
        dug, dvg, dgvs = [], [], []
        for h in range(GM_HEADS):
            sl = slice(h * 128, (h + 1) * 128)
            vhat, rv, vn, wt, mixed = f["heads"][h]
            dyh = dy[:, sl]
            dug.append(dyh * mixed)
            dmixed = dyh * f["ug"][:, sl]
            dmb = dmixed.astype(MXU)
            dws_ref[h] += jnp.where(f["tril"], _dot_nt(dmb, vn), 0.0)
            dbs = dbs + jnp.where(lane == h, jnp.sum(dmixed, axis=1, keepdims=True), 0.0)
            dvn = _dot_tn(wt.astype(MXU), dmb)
            dgvs.append(jnp.sum(dvn * vhat, axis=0, keepdims=True))
            dvg.append(_rms_bwd(dvn, vhat, rv, gv[:, sl]))
        dbst_ref[...] += dbs
        dgv_ref[...] += jnp.concatenate(dgvs, axis=1)
        duv_ref[rows, 0:1024] = (jnp.concatenate(dug, axis=1) * f["dug"]).astype(MXU)
        duv_ref[rows, 1024:2048] = (jnp.concatenate(dvg, axis=1) * f["dvg"]).astype(MXU)

    return pl.pallas_call(
        body, grid=(T // tb,), name="gmlp_bwd",
        in_specs=[_rows(tb, 1024, 0), _rows(tb, 1024, 1), _rows(tb, 1024, 0), _const((1, 1024)),
                  _const((GM_HEADS, CH, CH)), _const((CH, 128)), _const((1, 1024))],
        out_specs=[_rows(tb, 2048), _const((1, 1024)), _const((GM_HEADS, CH, CH)), _const((CH, 128)),
                   _const((1, 1024))],
        out_shape=[_sds((T, 2048), MXU), _sds((1, 1024), F32), _sds((GM_HEADS, CH, CH), F32), _sds((CH, 128), F32),
                   _sds((1, 1024), F32)],
        compiler_params=_cp(),
    )(proj, proj, dcat, gv, ws, bst, gout)


def _ssd_bwd(proj, dtraw, sall, dcat, convw, convb, dtb, alog, dskip, ng, ex, ltri, ext, nb, after):
    T = proj.shape[0]
    nch = T // CH // nb
    tok, in_specs = _ssd_specs(nch, rev=True)
    in_specs = in_specs + [
        _const((1024, 128)),
        pl.BlockSpec((1, 128, 1024), lambda b, c: (tok(b, c), 0, 0)),
        pl.BlockSpec((CH, 1024), lambda b, c: (tok(b, c), 1)),
        _ANY,
    ]

    def body(z_ref, xbc_ref, halo_ref, dt_ref, cw_ref, cb_ref, dtb_ref, al_ref, ds_ref, ng_ref, ex_ref, lt_ref,
             ext_ref, sall_ref, dyb_ref,
             dssd_ref, ddt_ref, dcw_ref, dcb_ref, ddtb_ref, dal_ref, dds_ref, dng_ref,
             dst_ref, dnext_ref, ddse_ref):
        b = pl.program_id(0)
        c = pl.program_id(1)

        @pl.when((b == 0) & (c == 0))
        def _():
            for r in (dcw_ref, dcb_ref, ddtb_ref, dal_ref, dds_ref, dng_ref, ddse_ref):
                r[...] = jnp.zeros_like(r)

        @pl.when(c == 0)
        def _():
            dst_ref[...] = jnp.zeros_like(dst_ref)
            dnext_ref[...] = jnp.zeros_like(dnext_ref)

        first_chunk = c == nch - 1
        halo = jnp.where(first_chunk, 0.0, halo_ref[...])
        z = z_ref[...]
        ex = ex_ref[...]
        ext = ext_ref[...]
        cw = cw_ref[...]
        ng = ng_ref[...]
        s_prev = sall_ref[0]
        f = _ssd_fwd_vals(z, xbc_ref[...], halo, dt_ref[...], cw, cb_ref[...], dtb_ref[...], al_ref[...],
                          ds_ref[...], ng, ex, lt_ref[...], s_prev)
        xs, xdt, cs, dec, dt = f["xs"], f["xdt"], f["cs"], f["dec"], f["dt"]
        dyb = dyb_ref[...]
        dyg, dngs = [], []
        for g in range(2):
            sl = slice(g * 512, (g + 1) * 512)
            dngs.append(jnp.sum(dyb[:, sl] * f["yhat"][g], axis=0, keepdims=True))
            dyg.append(_rms_bwd(dyb[:, sl], f["yhat"][g], f["rr"][g], ng[:, sl]))
        dng_ref[...] += jnp.concatenate(dngs, axis=1)
        dyg = jnp.concatenate(dyg, axis=1)
        sig_z = f["sig_z"]
        silu_z = z * sig_z
        dy = dyg * silu_z
        dz = dyg * f["ypre"] * sig_z * (1.0 + z * (1.0 - sig_z))
        ddse_ref[...] += jnp.sum(dy * xs, axis=0, keepdims=True)

        @pl.when((b == nb - 1) & (c == nch - 1))
        def _():
            dds_ref[...] = _dot_01(jnp.broadcast_to(ddse_ref[...], (8, 1024)), ext)[0:1]

        dxs = dy * f["de"]
        dye = dy * f["ecse"]
        dyeb = dye.astype(MXU)
        dst = dst_ref[...]
        dstb = dst.astype(MXU)
        bmb, cmb, sb, xdec = f["bmb"], f["cmb"], f["sb"], f["xdec"]
        u = jnp.concatenate([_dot(bmb[g], dstb[:, g * 512:(g + 1) * 512]) for g in range(2)], axis=1)
        dxdt = [u[:, q * 128:(q + 1) * 128] * f["dece"][:, q * 128:(q + 1) * 128] for q in range(8)]
        per_head = _dot_01(jnp.concatenate(
            [dy * f["yo"], u * xdt, jnp.broadcast_to(jnp.sum(dst * s_prev, axis=0, keepdims=True), (8, 1024))],
            axis=0), ext)
        dcs = per_head[0:CH]
        t = per_head[CH:2 * CH] * dec
        dcd = per_head[2 * CH:2 * CH + 1]
        row = lax.broadcasted_iota(jnp.int32, (CH, 128), 0)
        lane = lax.broadcasted_iota(jnp.int32, (CH, 128), 1)
        cd = jnp.exp(f["last"])
        dcs = dcs - t + jnp.where(row == CH - 1, jnp.sum(t, axis=0, keepdims=True) + dcd * cd, 0.0)
        dcst = jnp.zeros((128, CH), F32)
        lo = f["lo"]
        dbm, dcm, ds_prev = [], [], []
        for g in range(2):
            sl = slice(g * 512, (g + 1) * 512)
            dmg = jnp.zeros((CH, CH), F32)
            for q in range(4 * g, 4 * g + 4):
                dyq = dy[:, q * 128:(q + 1) * 128]
                xq = xdt[:, q * 128:(q + 1) * 128].astype(MXU)
                for hh in range(2):
                    h = 2 * q + hh
                    m = lo if hh == 0 else ~lo
                    dym = jnp.where(m, dyq, 0.0).astype(MXU)
                    gh = _dot_nt(dym, xq)
                    gl = gh * f["lms"][h]
                    dmg = dmg + gl
                    qh = gl * f["mg"][g]
                    dcs = dcs + jnp.where(lane == h, jnp.sum(qh, axis=1, keepdims=True), 0.0)
                    dcst = dcst - jnp.where(row == h, jnp.sum(qh, axis=0, keepdims=True), 0.0)
                    dxdt[q] = dxdt[q] + _dot_tn(f["whs"][h], dym)
            dmgb = dmg.astype(MXU)
            dcm.append(_dot(dmgb, bmb[g]) + _dot_nt(dyeb[:, sl], sb[:, sl]))
            dbm.append(_dot_tn(dmgb, cmb[g]) + _dot_nt(xdec[:, sl], dstb[:, sl]))
            ds_prev.append(_dot_tn(cmb[g], dyeb[:, sl]))
        dst_ref[...] = jnp.concatenate(ds_prev, axis=1) + dst * f["cde"]
        dcs = dcs + dcst.T
        da = _dot_hi(lt_ref[...].T, dcs)
        dxdt = jnp.concatenate(dxdt, axis=1)
        a_neg = f["a_neg"]
        ddt = da * a_neg + _dot_01(dxdt * xs, ext)
        dal_ref[...] += jnp.sum(da * dt, axis=0, keepdims=True) * a_neg
        dxs = dxs + dxdt * f["dte"]
        ddtraw = jnp.where(lane < SSD_HEADS, ddt * _sigmoid(f["dtpre"]), 0.0)
        ddtb_ref[...] += jnp.sum(ddtraw, axis=0, keepdims=True)
        ddt_ref[...] = ddtraw.astype(MXU)
        dxa = jnp.concatenate([dxs, dbm[0], dbm[1], dcm[0], dcm[1]], axis=1)
        sig_c = f["sig_c"]
        dconv = dxa * sig_c * (1.0 + f["conv"] * (1.0 - sig_c))
        dcb_ref[...] += jnp.sum(dconv, axis=0, keepdims=True)
        for k in range(4):
            dcw_ref[k:k + 1, :] += jnp.sum(dconv * f["shifts"][3 - k], axis=0, keepdims=True)
        dxbc = cw[3:4] * dconv
        for j, up in zip((1, 2, 3), _shifts_up(dconv, dnext_ref[...])):
            dxbc = dxbc + cw[3 - j:4 - j] * up
        dnext_ref[...] = dconv[0:8]
        dssd_ref[:, 0:1024] = dz.astype(MXU)
        dssd_ref[:, 1024:2560] = dxbc.astype(MXU)

    return pl.pallas_call(
        _after(15, body), grid=(nb, nch), name="ssd_bwd",
        in_specs=in_specs,
        out_specs=[pl.BlockSpec((CH, 2560), lambda b, c: (tok(b, c), 0)),
                   pl.BlockSpec((CH, 128), lambda b, c: (tok(b, c), 0)),
                   _const((8, CONV_CH)), _const((1, CONV_CH)), _const((1, 128)), _const((1, 128)), _const((1, 128)),
                   _const((1, 1024))],
        out_shape=[_sds((T, 2560), MXU), _sds((T, 128), MXU), _sds((8, CONV_CH), F32), _sds((1, CONV_CH), F32),
                   _sds((1, 128), F32), _sds((1, 128), F32), _sds((1, 128), F32), _sds((1, 1024), F32)],
        scratch_shapes=[pltpu.VMEM((128, 1024), F32), pltpu.VMEM((8, CONV_CH), F32), pltpu.VMEM((1, 1024), F32)],
        compiler_params=_cp(2),
    )(proj, proj, proj, dtraw, convw, convb, dtb, alog, dskip, ng, ex, ltri, ext, sall, dcat, after)


def _inproj_bwd(duv, dssd, ddt, wm, wdt, dh1, x, g, tm, after):
    T = x.shape[0]

    def body(duv_ref, dssd_ref, ddt_ref, wm_ref, wdt_ref, dh1_ref, x_ref, g_ref, dx_ref, dg_ref):
        @pl.when(pl.program_id(0) == 0)
        def _():
            dg_ref[...] = jnp.zeros_like(dg_ref)

        dxn = (_dot_nt(duv_ref[...], wm_ref[:, 0:2048]) + _dot_nt(dssd_ref[...], wm_ref[:, 2048:N_MAIN])
               + _dot_nt(ddt_ref[...], wdt_ref[...]))
        xh, r = _rms(x_ref[...])
        dg_ref[...] += jnp.sum(dxn * xh, axis=0, keepdims=True)
        dx_ref[...] = dh1_ref[...] + _rms_bwd(dxn, xh, r, g_ref[...])

    return pl.pallas_call(
        _after(8, body), grid=(T // tm,), name="inproj_bwd",
        in_specs=[_rows(tm, 2048), _rows(tm, 2560), _rows(tm, 128), _const((D, N_MAIN)), _const((D, 128)),
                  _rows(tm, D), _rows(tm, D), _const((1, D)), _ANY],
        out_specs=[_rows(tm, D), _const((1, D))],
        out_shape=[_sds((T, D), F32), _sds((1, D), F32)],
        compiler_params=_cp(),
    )(duv, dssd, ddt, wm, wdt, dh1, x, g, after)


def _matmul_tn(a, b, name, a_fn=None):
    T, M = a.shape
    N = b.shape[1]
    tm = min(M, 1024)
    tn = 1280 if N == 2560 else min(N, 1024)
    tk = min(T, 512)

    def body(a_ref, b_ref, o_ref, acc_ref):
        k = pl.program_id(2)

        @pl.when(k == 0)
        def _():
            acc_ref[...] = jnp.zeros_like(acc_ref)

        av = a_ref[...]
        if a_fn is not None:
            av = a_fn(av)
        acc_ref[...] += _dot_tn(av, b_ref[...])

        @pl.when(k == T // tk - 1)
        def _():
            o_ref[...] = acc_ref[...].astype(o_ref.dtype)

    return pl.pallas_call(
        body, grid=(M // tm, N // tn, T // tk), name=name,
        in_specs=[pl.BlockSpec((tk, tm), lambda i, j, k: (k, i)), pl.BlockSpec((tk, tn), lambda i, j, k: (k, j))],
        out_specs=pl.BlockSpec((tm, tn), lambda i, j, k: (i, j)),
        out_shape=_sds((M, N), GRAD),
        scratch_shapes=[pltpu.VMEM((tm, tn), F32)],
        compiler_params=_cp(3),
    )(a, b)


def _adamw_vals(w, g, m, v):
    m = B1 * m + (1.0 - B1) * g
    v = B2 * v + (1.0 - B2) * (g * g)
    m_hat = m / (1.0 - B1 ** STEP)
    v_hat = v / (1.0 - B2 ** STEP)
    return -LR * (m_hat / (jnp.sqrt(v_hat) + ADAM_EPS) + WD * w), m, v


def _adamw(w, g, m, v, name):
    R, C = w.shape
    tr = 256 if R % 256 == 0 else R

    def body(w_ref, g_ref, m_ref, v_ref, d_ref, mo_ref, vo_ref):
        d_ref[...], mo_ref[...], vo_ref[...] = _adamw_vals(w_ref[...], g_ref[...], m_ref[...], v_ref[...])

    spec = _rows(tr, C)
    return pl.pallas_call(
        body, grid=(R // tr,), name=name,
        in_specs=[spec] * 4, out_specs=[spec] * 3, out_shape=[_sds((R, C), F32)] * 3,
        compiler_params=_cp(),
    )(w, g, m, v)


def _adamw_halves(w, own, other, m, v, name):
    R, C = w.shape
    half = R // 2
    tr = min(half, 256)
    nth = half // tr

    def body(w_ref, own_ref, oth_ref, m_ref, v_ref, g_ref, d_ref, mo_ref, vo_ref):
        mine = (pl.program_id(0) // nth) == lax.axis_index("c")
        g = jnp.where(mine, own_ref[...], oth_ref[...])
        g_ref[...] = g
        d_ref[...], mo_ref[...], vo_ref[...] = _adamw_vals(w_ref[...], g, m_ref[...], v_ref[...])

    full = _rows(tr, C)
    part = pl.BlockSpec((tr, C), lambda i: (i % nth, 0))
    return pl.pallas_call(
        body, grid=(R // tr,), name=name,
        in_specs=[full, part, part, full, full], out_specs=[full] * 4, out_shape=[_sds((R, C), F32)] * 4,
        compiler_params=_cp(),
    )(w, own, other, m, v)


def _sum_small(slots, name):
    nd, rows, C = slots.shape

    def body(s_ref, o_ref):
        acc = s_ref[0]
        for d in range(1, nd):
            acc = acc + s_ref[d]
        o_ref[...] = acc

    return pl.pallas_call(
        body, grid=(1,), name=name,
        in_specs=[_const((nd, rows, C))], out_specs=_const((rows, C)), out_shape=_sds((rows, C), F32),
        compiler_params=_cp(),
    )(slots)


def _sum_slots(slots, src, kind, shp, kh, name):
    R, C = shp
    rh = R // 2
    tr = min(rh, 256)
    nth = rh // tr
    if kind == "slab":
        src_spec = pl.BlockSpec((1, tr, C), lambda i, kh: (kh[0], kh[1] * nth + i, 0))
    elif kind == "rows":
        src_spec = pl.BlockSpec((tr, C), lambda i, kh: (kh[0] * (R // tr) + kh[1] * nth + i, 0))
    else:
        src_spec = pl.BlockSpec((tr, C), lambda i, kh: (kh[1] * nth + i, kh[0]))

    def body(kh_ref, s_ref, own_ref, o_ref):
        me = 2 * kh_ref[0] + kh_ref[1]
        acc = (own_ref[0] if kind == "slab" else own_ref[...]).astype(F32)
        for k in range(1, 8):
            acc = acc + s_ref[me ^ k].astype(F32)
        o_ref[...] = acc

    return pl.pallas_call(
        body, name=name,
        grid_spec=pltpu.PrefetchScalarGridSpec(
            num_scalar_prefetch=1, grid=(nth,),
            in_specs=[pl.BlockSpec((8, tr, C), lambda i, kh: (0, i, 0)), src_spec],
            out_specs=pl.BlockSpec((tr, C), lambda i, kh: (i, 0))),
        out_shape=_sds((rh, C), F32),
        compiler_params=_cp(),
    )(kh, slots, src)


def _cast_into_slot(w, kh, name):
    R, C = w.shape
    tr = 256

    def body(kh_ref, w_ref, o_ref):
        o_ref[0] = w_ref[...].astype(BF16)

    return pl.pallas_call(
        body, name=name,
        grid_spec=pltpu.PrefetchScalarGridSpec(
            num_scalar_prefetch=1, grid=(R // tr,),
            in_specs=[pl.BlockSpec((tr, C), lambda i, kh: (i, 0))],
            out_specs=pl.BlockSpec((1, tr, C), lambda i, kh: (kh[0], i, 0))),
        out_shape=_sds((4, R, C), BF16),
        compiler_params=_cp(),
    )(kh, w)


_ANY = pl.BlockSpec(memory_space=pl.ANY)
_CHIP_FLIPS = [(1, 0), (0, 1), (1, 1)]
_DEVICE_FLIPS = [(fx, fy, fc) for fx in (0, 1) for fy in (0, 1) for fc in (0, 1)][1:]


def _half(h, rows):
    return pl.ds(pl.multiple_of(h * rows, rows), rows)


def _remote(src, dst, ssem, rsem, to):
    return pltpu.make_async_remote_copy(src_ref=src, dst_ref=dst, send_sem=ssem, recv_sem=rsem,
                                        device_id=to, device_id_type=MESH)


def _weight_gather(bufs, conv):
    n = len(bufs)

    def body(*refs):
        conv_ref, outs, conv_out = refs[n], refs[n + 1:2 * n + 1], refs[2 * n + 1]
        send_sems, recv_sems, fsend_sems, frecv_sems, csend_sems, crecv_sems, local_sem = refs[2 * n + 2:]
        x, y, c = lax.axis_index("x"), lax.axis_index("y"), lax.axis_index("c")
        me = 2 * x + y
        halves = [_half(c, r.shape[1] // 2) for r in outs]
        others = [_half(1 - c, r.shape[1] // 2) for r in outs]
        remote = _remote
        local = [pltpu.make_async_copy(conv_ref, conv_out.at[me], local_sem)]
        for cp in local:
            cp.start()
        sends = []
        for k, (fx, fy) in enumerate(_CHIP_FLIPS):
            peer = (x ^ fx, y ^ fy, c)
            for i in range(n):
                mine = outs[i].at[me, halves[i]]
                sends.append(remote(mine, mine, send_sems.at[k * n + i], recv_sems.at[k * n + i], peer))
            sends.append(remote(conv_ref, conv_out.at[me], csend_sems.at[k], crecv_sems.at[k], peer))
        for cp in sends:
            cp.start()
        sibling = (x, y, 1 - c)
        forwards = []
        for k, (fx, fy) in enumerate(_CHIP_FLIPS):
            peer = (x ^ fx, y ^ fy, c)
            src = 2 * (x ^ fx) + (y ^ fy)
            for i in range(n):
                landed = outs[i].at[src, halves[i]]
                remote(landed, landed, send_sems.at[k * n + i], recv_sems.at[k * n + i], peer).wait_recv()
                fw = remote(landed, landed, fsend_sems.at[k * n + i], frecv_sems.at[k * n + i], sibling)
                fw.start()
                forwards.append(fw)
            remote(conv_out.at[src], conv_out.at[src], csend_sems.at[k], crecv_sems.at[k], peer).wait_recv()
        for k, (fx, fy) in enumerate(_CHIP_FLIPS):
            src = 2 * (x ^ fx) + (y ^ fy)
            for i in range(n):
                theirs = outs[i].at[src, others[i]]
                remote(theirs, theirs, fsend_sems.at[k * n + i], frecv_sems.at[k * n + i], sibling).wait_recv()
        for cp in sends + forwards:
            cp.wait_send()
        for cp in local:
            cp.wait()

    dma = pltpu.SemaphoreType.DMA
    return pl.pallas_call(
        body, name="weight_gather",
        in_specs=[_ANY] * (n + 1), out_specs=[_ANY] * (n + 1),
        out_shape=[_sds(b.shape, b.dtype) for b in bufs] + [_sds((4,) + conv.shape, conv.dtype)],
        input_output_aliases={i: i for i in range(n)},
        scratch_shapes=[dma((3 * n,)), dma((3 * n,)), dma((3 * n,)), dma((3 * n,)), dma((3,)), dma((3,)), dma],
    )(*bufs, conv)


def _piece(ref, kind, R, C, k, h):
    if kind == "slab":
        return ref.at[k, _half(h, R // 2), :]
    if kind == "rows":
        return ref.at[pl.ds(pl.multiple_of(k * R + h * (R // 2), R // 2), R // 2), :]
    return ref.at[_half(h, R // 2), pl.ds(pl.multiple_of(k * C, C), C)]


def _small_exchange(small):
    rs = small.shape[0]

    def body(s_ref, out_ref, send_sems, recv_sems, local_sem):
        x, y, c = lax.axis_index("x"), lax.axis_index("y"), lax.axis_index("c")
        slot = 4 * x + 2 * y + c
        own = pltpu.make_async_copy(s_ref, out_ref.at[slot], local_sem)
        own.start()
        copies = []
        for k, (fx, fy, fc) in enumerate(_DEVICE_FLIPS):
            copies.append(_remote(s_ref, out_ref.at[slot], send_sems.at[k], recv_sems.at[k], (x ^ fx, y ^ fy, c ^ fc)))
        for cp in copies:
            cp.start()
        for k, (fx, fy, fc) in enumerate(_DEVICE_FLIPS):
            theirs = out_ref.at[slot ^ (k + 1)]
            _remote(theirs, theirs, send_sems.at[k], recv_sems.at[k], (x ^ fx, y ^ fy, c ^ fc)).wait_recv()
        for cp in copies:
            cp.wait_send()
        own.wait()

    dma = pltpu.SemaphoreType.DMA
    return pl.pallas_call(
        body, name="small_exchange",
        in_specs=[_ANY], out_specs=_ANY, out_shape=_sds((8, rs, 128), F32),
        scratch_shapes=[dma((7,)), dma((7,)), dma],
    )(small)


_HBM = pl.BlockSpec(memory_space=pltpu.HBM)
_SEM = pl.BlockSpec(memory_space=pltpu.SEMAPHORE)


def _split_start(name, arrays, n_copies, plan, after=None):
    n = len(arrays)
    extra = [] if after is None else [after]

    def body(*refs):
        m = n + len(extra)
        arrs, send_sems, recv_sems, token = refs[:n], refs[m], refs[m + 1], refs[-1]
        for j, (src, dst, peer) in enumerate(plan(arrs)):
            _remote(src, dst, send_sems.at[j], recv_sems.at[j], peer).start()
        token[...] = jnp.zeros_like(token)

    dma = pltpu.SemaphoreType.DMA
    res = pl.pallas_call(
        body, name=name,
        out_shape=(dma((n_copies,)), dma((n_copies,)), *[pltpu.HBM(a.shape, a.dtype) for a in arrays],
                   _sds((8, 128), F32)),
        in_specs=[_HBM] * n + [_ANY] * len(extra),
        out_specs=(_SEM, _SEM, *[_HBM] * n, pl.BlockSpec(memory_space=pltpu.VMEM)),
        input_output_aliases={i: 2 + i for i in range(n)},
        compiler_params=pltpu.CompilerParams(has_side_effects=pltpu.SideEffectType.DATAFLOW_SIDE_EFFECTING),
    )(*[pltpu.with_memory_space_constraint(a, pltpu.HBM) for a in arrays], *extra)
    return res[0], res[1], list(res[2:2 + n]), res[-1]


def _split_wait(name, arrays, send_sems, recv_sems, plan, after):
    n = len(arrays)

    def body(*refs):
        arrs, ssems, rsems = refs[:n], refs[n], refs[n + 1]
        for j, (src, dst, peer) in enumerate(plan(arrs)):
            cp = _remote(src, dst, ssems.at[j], rsems.at[j], peer)
            cp.wait_send()
            cp.wait_recv()

    return list(pl.pallas_call(
        body, name=name,
        out_shape=tuple(pltpu.HBM(a.shape, a.dtype) for a in arrays),
        in_specs=[_HBM] * n + [_SEM, _SEM, _ANY],
        out_specs=tuple([_HBM] * n),
        input_output_aliases={i: i for i in range(n)},
        compiler_params=pltpu.CompilerParams(has_side_effects=pltpu.SideEffectType.DATAFLOW_SIDE_EFFECTING),
    )(*arrays, send_sems, recv_sems, after))


def _gather_plan(n):
    def plan(bufs):
        x, y, c = lax.axis_index("x"), lax.axis_index("y"), lax.axis_index("c")
        me = 2 * x + y
        return [(bufs[i].at[me], bufs[i].at[me], (x ^ fx, y ^ fy, c)) for fx, fy in _CHIP_FLIPS for i in range(n)]

    return plan


def _reduce_plan(specs, n_small):
    n = len(specs)

    def plan(arrs):
        x, y, c = lax.axis_index("x"), lax.axis_index("y"), lax.axis_index("c")
        slot = 4 * x + 2 * y + c
        out = []
        for fx, fy, fc in _DEVICE_FLIPS:
            peer = (x ^ fx, y ^ fy, c ^ fc)
            for i, (kind, (R, C)) in enumerate(specs):
                out.append((_piece(arrs[i], kind, R, C, 2 * peer[0] + peer[1], peer[2]), arrs[n + i].at[slot], peer))
            for s in range(n_small):
                out.append((arrs[2 * n + 2 * s], arrs[2 * n + 2 * s + 1].at[slot], peer))
        return out

    return plan


def _sibling_exchange(halves):
    n = len(halves)

    def body(*refs):
        ins, outs, send_sems, recv_sems = refs[:n], refs[n:2 * n], refs[2 * n], refs[2 * n + 1]
        sibling = (lax.axis_index("x"), lax.axis_index("y"), 1 - lax.axis_index("c"))
        copies = [pltpu.make_async_remote_copy(src_ref=ins[i], dst_ref=outs[i], send_sem=send_sems.at[i],
                                               recv_sem=recv_sems.at[i], device_id=sibling, device_id_type=MESH)
                  for i in range(n)]
        for cp in copies:
            cp.start()
        for cp in copies:
            cp.wait()

    dma = pltpu.SemaphoreType.DMA
    return pl.pallas_call(
        body, name="sibling_exchange",
        in_specs=[_ANY] * n, out_specs=[_ANY] * n,
        out_shape=[_sds(h.shape, h.dtype) for h in halves],
        scratch_shapes=[dma((n,)), dma((n,))],
    )(*halves)


_BIG = [("w_in", (1024, 1156), "slab"), ("w_out", (512, 1024), "rows"), ("w_ff1", (1024, 1024), "cols"),
        ("w_ff2", (1024, 1024), "rows"), ("w_ple_gate", (256, 1024), "rows"), ("w_ple_proj", (256, 256), "cols")]
_SMALL = [("norm_mix_g", (1, 1024)), ("gm_v_norm_g", (1, 1024)), ("gm_ws", (1, 8, 128, 128)), ("gm_bs", (1, 8, 128)),
          ("gm_out_norm_g", (1, 1024)), ("ssd_conv_w", (1, 4, 1536)), ("ssd_conv_b", (1, 1536)),
          ("ssd_dt_bias", (1, 16)), ("ssd_a_log", (1, 16)), ("ssd_d", (1, 16)), ("ssd_norm_g", (1, 1024)),
          ("norm_mlp_g", (1, 1024)), ("ple_norm_g", (1, 1024)), ("final_norm_g", (1024,))]


def _rows128(a):
    flat = a.reshape(-1)
    rows = -(-flat.shape[0] // 1024) * 8
    return jnp.pad(flat, (0, rows * 128 - flat.shape[0])).reshape(rows, 128)


def _pad_lanes(v, n=128):
    v = v.reshape(1, -1)
    return jnp.pad(v, ((0, 0), (0, n - v.shape[1])))


_SMALL_SHAPES = dict(_SMALL + [("loss", ())])
_BIG_SPECS = {n: (kind, shp) for n, shp, kind in _BIG}


class _Comm:
    def __init__(self, a, kh):
        self.a, self.kh = a, kh
        self.bufs = {n: _cast_into_slot(a[n].reshape(shp), kh, "cast_" + n) for n, shp, _ in _BIG}
        self.sent = []
        self.small_tot = {}

    def w_in(self):
        (g_win,), g_cw = self._gather_now()
        rest = [self.bufs[n] for n, _, _ in _BIG[1:]]
        plan = _gather_plan(len(rest))
        ssem, rsem, thru, token = _split_start("gather_start", rest, 3 * len(rest), plan, after=g_cw)
        self.gather = (plan, ssem, rsem, thru)
        w_in_full = jnp.concatenate([g_win[k] for k in range(4)], axis=1)
        wm, wdt = w_in_full[:, :N_MAIN], jnp.pad(w_in_full[:, N_MAIN:], ((0, 0), (0, 128 - 16)))
        return wm, wdt, jnp.concatenate([g_cw[k] for k in range(4)], axis=1), token

    def _gather_now(self):
        *bufs, g_cw = _weight_gather([self.bufs["w_in"]], self.a["ssd_conv_w"].reshape(4, 384))
        return bufs, g_cw

    def rest(self, after):
        plan, ssem, rsem, thru = self.gather
        g_wo, g_w1, g_w2, g_wg, g_wp = _split_wait("gather_wait", thru, ssem, rsem, plan, after)
        return g_wo.reshape(2048, D), g_w1, g_w2.reshape(DFF, D), g_wg.reshape(D, D), g_wp

    def send(self, tag, grads):
        big = [n for n, _, _ in _BIG if n in grads]
        small = [n for n in _SMALL_SHAPES if n in grads]
        parts = [_rows128(grads[n]) for n in small]
        rows = [s.shape[0] for s in parts]
        if not big:
            self._unpack(_sum_small(_small_exchange(jnp.concatenate(parts, axis=0)), "sum_small_" + tag), small, rows)
            return None
        srcs = [jnp.stack([grads[n][:, 1156 * k:1156 * (k + 1)] for k in range(4)]) if n == "w_in" else grads[n]
                for n in big]
        lands = [lax.empty((8, _BIG_SPECS[n][1][0] // 2, _BIG_SPECS[n][1][1]), GRAD) for n in big]
        extra = []
        if small:
            pack = jnp.concatenate(parts, axis=0)
            extra = [pack, jnp.broadcast_to(pack, (8,) + pack.shape)]
        plan = _reduce_plan([_BIG_SPECS[n] for n in big], len(extra) // 2)
        n_copies = 7 * (len(big) + len(extra) // 2)
        ssem, rsem, thru, token = _split_start("reduce_start_" + tag, srcs + lands + extra, n_copies, plan)
        self.sent.append((tag, big, small, rows, plan, ssem, rsem, thru))
        return token

    def _unpack(self, tot, names, rows):
        o = 0
        for n, r in zip(names, rows):
            shp = _SMALL_SHAPES[n]
            cnt = 1
            for s in shp:
                cnt *= s
            self.small_tot[n] = tot[o:o + r].reshape(-1)[:cnt].reshape(shp)
            o += r

    def finish(self, after):
        own = {}
        for tag, big, small, rows, plan, ssem, rsem, thru in self.sent:
            arrs = _split_wait("reduce_wait_" + tag, thru, ssem, rsem, plan, after)
            nb_ = len(big)
            for i, n in enumerate(big):
                kind, shp = _BIG_SPECS[n]
                own[n] = _sum_slots(arrs[nb_ + i], arrs[i], kind, shp, self.kh, "sum_" + n)
                after = own[n]
            if small:
                self._unpack(_sum_small(arrs[2 * nb_ + 1], "sum_small_" + tag), small, rows)
        return [own[n] for n, _, _ in _BIG], dict(self.small_tot)


def _local_step(x, p, tgt, sm, comm, nb, tm):
    wm, wdt, conv_w, token = comm.w_in()
    g_mix, gv, gout = sm["norm_mix_g"].reshape(1, D), sm["gm_v_norm_g"].reshape(1, D), sm["gm_out_norm_g"].reshape(1, D)
    ws = sm["gm_ws"].reshape(GM_HEADS, CH, CH)
    bst = jnp.pad(sm["gm_bs"].reshape(GM_HEADS, CH).T, ((0, 0), (0, 128 - GM_HEADS)))
    convw = jnp.pad(conv_w, ((0, 4), (0, 0)))
    convb = sm["ssd_conv_b"].reshape(1, CONV_CH)
    dtb, alog = _pad_lanes(sm["ssd_dt_bias"]), _pad_lanes(sm["ssd_a_log"])
    dskip = jnp.repeat(sm["ssd_d"].reshape(SSD_HEADS), SSD_P).reshape(1, 1024)
    ng, g_mlp, g_ple = sm["ssd_norm_g"].reshape(1, D), sm["norm_mlp_g"].reshape(1, D), sm["ple_norm_g"].reshape(1, D)
    gf = sm["final_norm_g"].reshape(1, D)
    head_of_lane = lax.broadcasted_iota(jnp.int32, (128, 1024), 1) // SSD_P
    ex = (lax.broadcasted_iota(jnp.int32, (128, 1024), 0) == head_of_lane).astype(BF16)
    ext = ex.T
    ltri = (lax.broadcasted_iota(jnp.int32, (CH, CH), 0) >= lax.broadcasted_iota(jnp.int32, (CH, CH), 1)).astype(F32)

    proj, dtraw, xn = _inproj(x, g_mix, wm, wdt, tm, token)
    cat = _gmlp_fwd(proj, gv, ws, bst, gout)
    cat, sall = _ssd_fwd(proj, dtraw, cat, convw, convb, dtb, alog, dskip, ng, ex, ltri, nb)
    wo, w1, w2, wg, wp = comm.rest(cat)
    h1, hn = _outproj(cat, wo, x, g_mlp, tm)
    hid = _ff1(hn, w1, tm)
    h2, hp = _ff2(hid, w2, h1, g_ple, tm)
    dh3, dgl, dpe, loss, d_gf = _tail(h2, hp, p, tgt, wg, wp, gf, tm)

    d_wp = _matmul_tn(p, dpe, "dw_ple_proj", a_fn=lambda a: a.astype(MXU))
    d_wg = _matmul_tn(hp, dgl, "dw_ple_gate")
    dh2, dh2b, d_gple = _ple_bwd(dgl, wg, dh3, h2, g_ple, tm)
    d_w2 = _matmul_tn(hid, dh2b, "dw_ff2", a_fn=_sq)
    dpre = _ff2_bwd(dh2b, w2, hid, tm)
    d_w1 = _matmul_tn(hn, dpre, "dw_ff1")
    token = comm.send("a", {"w_ple_proj": d_wp, "w_ple_gate": d_wg, "w_ff2": d_w2, "w_ff1": d_w1})
    dh1, dh1b, d_gmlp = _ff1_bwd(dpre, w1, dh2, h1, g_mlp, tm, token)
    dcat = _outproj_bwd(dh1b, wo, tm)
    d_wo = _matmul_tn(cat, dh1b, "dw_out")
    duv, d_gv, d_ws, d_bst, d_gout = _gmlp_bwd(proj, dcat, gv, ws, bst, gout)
    token = comm.send("b", {
        "w_out": d_wo, "loss": loss[0:1, 0:1], "final_norm_g": d_gf, "ple_norm_g": d_gple, "norm_mlp_g": d_gmlp,
        "gm_v_norm_g": d_gv, "gm_ws": d_ws, "gm_bs": d_bst[:, :GM_HEADS].T, "gm_out_norm_g": d_gout})
    dssd, ddt, d_cw, d_cb, d_dtb, d_al, d_ds, d_ng = _ssd_bwd(
        proj, dtraw, sall, dcat, convw, convb, dtb, alog, dskip, ng, ex, ltri, ext, nb, token)
    d_win = jnp.concatenate([_matmul_tn(xn, duv, "dw_in_uv"), _matmul_tn(xn, dssd, "dw_in_ssd"),
                             _matmul_tn(xn, ddt, "dw_in_dt")[:, :16]], axis=1)
    token = comm.send("c", {"w_in": d_win})
    dx, d_gmix = _inproj_bwd(duv, dssd, ddt, wm, wdt, dh1, x, g_mix, tm, token)
    comm.send("d", {"norm_mix_g": d_gmix, "ssd_conv_w": d_cw[0:4], "ssd_conv_b": d_cb, "ssd_dt_bias": d_dtb[:, :16],
                    "ssd_a_log": d_al[:, :16], "ssd_d": d_ds[:, :16], "ssd_norm_g": d_ng})
    return dx


def kernel(x, p, norm_mix_g, w_in, gm_v_norm_g, gm_ws, gm_bs, gm_out_norm_g, ssd_conv_w, ssd_conv_b, ssd_dt_bias, ssd_a_log, ssd_d, ssd_norm_g, w_out, norm_mlp_g, w_ff1, w_ff2, ple_norm_g, w_ple_gate, w_ple_proj, final_norm_g, loss_target, m_norm_mix_g, m_w_in, m_gm_v_norm_g, m_gm_ws, m_gm_bs, m_gm_out_norm_g, m_ssd_conv_w, m_ssd_conv_b, m_ssd_dt_bias, m_ssd_a_log, m_ssd_d, m_ssd_norm_g, m_w_out, m_norm_mlp_g, m_w_ff1, m_w_ff2, m_ple_norm_g, m_w_ple_gate, m_w_ple_proj, m_final_norm_g, v_norm_mix_g, v_w_in, v_gm_v_norm_g, v_gm_ws, v_gm_bs, v_gm_out_norm_g, v_ssd_conv_w, v_ssd_conv_b, v_ssd_dt_bias, v_ssd_a_log, v_ssd_d, v_ssd_norm_g, v_w_out, v_norm_mlp_g, v_w_ff1, v_w_ff2, v_ple_norm_g, v_w_ple_gate, v_w_ple_proj, v_final_norm_g):
    a = dict(locals())
    order = ["norm_mix_g", "w_in", "gm_v_norm_g", "gm_ws", "gm_bs", "gm_out_norm_g", "ssd_conv_w", "ssd_conv_b",
             "ssd_dt_bias", "ssd_a_log", "ssd_d", "ssd_norm_g", "w_out", "norm_mlp_g", "w_ff1", "w_ff2", "ple_norm_g",
             "w_ple_gate", "w_ple_proj", "final_norm_g"]
    chip = 2 * lax.axis_index("x") + lax.axis_index("y")
    nb, S = x.shape[0], x.shape[1]
    T = nb * S
    sm = {n: a[n] for n, _ in _SMALL if n != "ssd_conv_w"}
    comm = _Comm(a, jnp.stack([chip, lax.axis_index("c")]).astype(jnp.int32))
    dx = _local_step(x.reshape(T, D), p.reshape(T, DPLE), loss_target.reshape(T, D), sm, comm, nb, 512)
    own, g_out = comm.finish(dx)
    other = _sibling_exchange(own)

    delta, new_m, new_v = {}, {}, {}
    for i, (n, shp, _) in enumerate(_BIG):
        res = _adamw_halves(a[n].reshape(shp), own[i], other[i], a["m_" + n].reshape(shp), a["v_" + n].reshape(shp),
                            "adamw_" + n)
        g_out[n], delta[n], new_m[n], new_v[n] = (r.reshape(a[n].shape) for r in res)
    g_out["ssd_conv_w"] = lax.dynamic_slice(g_out["ssd_conv_w"], (0, 0, chip * 384), (1, 4, 384))
    small_names = [n for n, _ in _SMALL]
    packs = [jnp.concatenate([_rows128(src(n)) for n in small_names], axis=0)
             for src in (lambda n: a[n], lambda n: g_out[n], lambda n: a["m_" + n], lambda n: a["v_" + n])]
    outs = _adamw(*packs, "adamw_small")
    o = 0
    for n in small_names:
        r = _rows128(a[n]).shape[0]
        cnt = a[n].size
        for dst, src in zip((delta, new_m, new_v), outs):
            dst[n] = src[o:o + r].reshape(-1)[:cnt].reshape(a[n].shape)
        o += r
    return (g_out["loss"], dx.reshape(x.shape), *[g_out[n] for n in order], *[delta[n] for n in order],
            *[new_m[n] for n in order], *[new_v[n] for n in order])
```

```python
import jax
import jax.numpy as jnp
from jax import lax
from jax.experimental import pallas as pl
from jax.experimental.pallas import tpu as pltpu

F32 = jnp.float32
BF16 = jnp.bfloat16
MXU = jnp.bfloat16
GRAD = jnp.bfloat16

D = 1024
CH = 128
GM_HEADS = 8
SSD_HEADS = 16
SSD_P = 64
CONV_CH = 1536
N_MAIN = 4608
DFF = 4096
DPLE = 256
EPS = 1e-6
NEG = -1e30

LR, B1, B2, ADAM_EPS, WD, STEP = 0.001, 0.9, 0.999, 1e-08, 0.01, 10

VMEM_LIMIT = 56 * 1024 * 1024
MESH = pl.DeviceIdType.MESH

INV_SQRT2 = 0.7071067811865476
INV_SQRT_2PI = 0.3989422804014327


def _cp(n_axes=1):
    return pltpu.CompilerParams(dimension_semantics=("arbitrary",) * n_axes, vmem_limit_bytes=VMEM_LIMIT)


def _dot(a, b):
    return jnp.dot(a, b, preferred_element_type=F32)


def _dot_nt(a, b):
    return lax.dot_general(a, b, (((1,), (1,)), ((), ())), preferred_element_type=F32)


def _dot_tn(a, b):
    return lax.dot_general(a, b, (((0,), (0,)), ((), ())), preferred_element_type=F32)


def _dot_hi(a, b):
    return jnp.dot(a, b, preferred_element_type=F32, precision=lax.Precision.HIGHEST)


def _dot_01(a, sel):
    hi = a.astype(BF16)
    lo = (a - hi.astype(F32)).astype(BF16)
    n = a.shape[0]
    r = _dot(jnp.concatenate([hi, lo], axis=0), sel)
    return r[0:n] + r[n:2 * n]


def _rows(tm, n, j=0):
    return pl.BlockSpec((tm, n), lambda i: (i, j))


def _const(shape):
    nd = len(shape)
    return pl.BlockSpec(shape, lambda *_: (0,) * nd)


def _sds(shape, dtype):
    return jax.ShapeDtypeStruct(shape, dtype)


def _rms(x):
    r = lax.rsqrt(jnp.mean(x * x, axis=-1, keepdims=True) + EPS)
    return x * r, r


def _rms_bwd(dy, xhat, r, g):
    dyg = dy * g
    return r * (dyg - xhat * jnp.mean(dyg * xhat, axis=-1, keepdims=True))


def _sigmoid(x):
    return 1.0 / (1.0 + jnp.exp(-x))


def _gelu(x):
    cdf = 0.5 * (1.0 + lax.erf(x * INV_SQRT2))
    pdf = jnp.exp(-0.5 * x * x) * INV_SQRT_2PI
    return x * cdf, cdf + x * pdf


def _softplus(x):
    e = jnp.exp(-jnp.abs(x))
    u = 1.0 + e
    log1p = jnp.where(u == 1.0, e, jnp.log(u) * e / (u - 1.0))
    return jnp.maximum(x, 0.0) + log1p


def _after(n_in, fn):
    def body(*refs):
        return fn(*refs[:n_in], *refs[n_in + 1:])

    return body


def _inproj(x, g, wm, wdt, tm, after):
    T = x.shape[0]

    def body(x_ref, g_ref, wm_ref, wdt_ref, proj_ref, dt_ref, xn_ref):
        xh, _ = _rms(x_ref[...])
        xn = (xh * g_ref[...]).astype(MXU)
        xn_ref[...] = xn
        for n in range(N_MAIN // 512):
            proj_ref[:, n * 512:(n + 1) * 512] = _dot(xn, wm_ref[:, n * 512:(n + 1) * 512])
        dt_ref[...] = _dot(xn, wdt_ref[...])

    return pl.pallas_call(
        _after(4, body), grid=(T // tm,), name="inproj",
        in_specs=[_rows(tm, D), _const((1, D)), _const((D, N_MAIN)), _const((D, 128)), _ANY],
        out_specs=[_rows(tm, N_MAIN), _rows(tm, 128), _rows(tm, D)],
        out_shape=[_sds((T, N_MAIN), F32), _sds((T, 128), F32), _sds((T, D), MXU)],
        compiler_params=_cp(),
    )(x, g, wm, wdt, after)


def _gmlp_fwd_vals(u, v, gv, ws_ref, bst, gout):
    ug, dug = _gelu(u)
    vg, dvg = _gelu(v)
    row = lax.broadcasted_iota(jnp.int32, (CH, CH), 0)
    col = lax.broadcasted_iota(jnp.int32, (CH, CH), 1)
    tril = row >= col
    ys, heads = [], []
    for h in range(GM_HEADS):
        sl = slice(h * 128, (h + 1) * 128)
        vhat, rv = _rms(vg[:, sl])
        vn = (vhat * gv[:, sl]).astype(MXU)
        wt = jnp.where(tril, ws_ref[h], 0.0)
        mixed = _dot(wt.astype(MXU), vn) + bst[:, h:h + 1]
        ys.append(ug[:, sl] * mixed)
        heads.append((vhat, rv, vn, wt, mixed))
    y = jnp.concatenate(ys, axis=1)
    yhat, ry = _rms(y)
    return dict(ug=ug, dug=dug, dvg=dvg, heads=heads, yhat=yhat, ry=ry, tril=tril, out=yhat * gout)


def _gmlp_fwd(proj, gv, ws, bst, gout):
    T = proj.shape[0]

    nck = 4 if T % (4 * CH) == 0 else 1
    tb = nck * CH

    def body(u_ref, v_ref, gv_ref, ws_ref, bst_ref, gout_ref, ya_ref):
        for k in range(nck):
            sl = slice(k * CH, (k + 1) * CH)
            f = _gmlp_fwd_vals(u_ref[sl, :], v_ref[sl, :], gv_ref[...], ws_ref, bst_ref[...], gout_ref[...])
            ya_ref[sl, :] = f["out"].astype(MXU)

    return pl.pallas_call(
        body, grid=(T // tb,), name="gmlp_fwd",
        in_specs=[_rows(tb, 1024, 0), _rows(tb, 1024, 1), _const((1, 1024)), _const((GM_HEADS, CH, CH)),
                  _const((CH, 128)), _const((1, 1024))],
        out_specs=_rows(tb, 1024, 0),
        out_shape=_sds((T, 2048), MXU),
        compiler_params=_cp(),
    )(proj, proj, gv, ws, bst, gout)


def _shifts_down(cur, halo):
    row8 = lax.broadcasted_iota(jnp.int32, (8, cur.shape[1]), 0)
    out = [cur]
    for j in (1, 2, 3):
        sh = pltpu.roll(cur, j, 0)
        top = jnp.where(row8 < j, pltpu.roll(halo, j, 0), sh[0:8])
        out.append(jnp.concatenate([top, sh[8:]], axis=0))
    return out


def _shifts_up(cur, halo):
    row8 = lax.broadcasted_iota(jnp.int32, (8, cur.shape[1]), 0)
    out = []
    for j in (1, 2, 3):
        sh = pltpu.roll(cur, CH - j, 0)
        bot = jnp.where(row8 + j >= 8, pltpu.roll(halo, 8 - j, 0), sh[CH - 8:CH])
        out.append(jnp.concatenate([sh[0:CH - 8], bot], axis=0))
    return out


def _ssd_fwd_vals(z, xbc, halo, dtraw, convw, convb, dtb, alog, dskip, ng, ex, ltri, s_prev):
    shifts = _shifts_down(xbc, halo)
    conv = convb + convw[3:4] * shifts[0] + convw[2:3] * shifts[1] + convw[1:2] * shifts[2] + convw[0:1] * shifts[3]
    sig_c = _sigmoid(conv)
    xa = conv * sig_c
    xs = xa[:, :1024]
    bm = [xa[:, 1024:1152], xa[:, 1152:1280]]
    cm = [xa[:, 1280:1408], xa[:, 1408:1536]]
    dtpre = dtraw + dtb
    dt = _softplus(dtpre)
    a_neg = -jnp.exp(alog)
    cs = _dot_hi(ltri, dt * a_neg)
    cst = cs.T
    last = cs[CH - 1:CH]
    ecs = jnp.exp(cs)
    dec = jnp.exp(last - cs)
    spread = _dot_01(jnp.concatenate([dt, ecs, dec], axis=0), ex)
    dte, ecse, dece = spread[0:CH], spread[CH:2 * CH], spread[2 * CH:3 * CH]
    cde = ecse[CH - 1:CH]
    de = dskip
    xdt = xs * dte
    row = lax.broadcasted_iota(jnp.int32, (CH, CH), 0)
    col = lax.broadcasted_iota(jnp.int32, (CH, CH), 1)
    tril = row >= col
    lo = col < SSD_P
    bmb = [b.astype(MXU) for b in bm]
    cmb = [c.astype(MXU) for c in cm]
    mg = [_dot_nt(cmb[g], bmb[g]) for g in range(2)]
    yd, lms, whs = [], [], []
    for q in range(8):
        g = q // 4
        xq = xdt[:, q * 128:(q + 1) * 128]
        acc = None
        for hh in range(2):
            h = 2 * q + hh
            seg = cs[:, h:h + 1] - cst[h:h + 1, :]
            lm = jnp.exp(jnp.where(tril, seg, NEG))
            wh = (mg[g] * lm).astype(MXU)
            xm = jnp.where(lo if hh == 0 else ~lo, xq, 0.0).astype(MXU)
            part = _dot(wh, xm)
            acc = part if acc is None else acc + part
            lms.append(lm)
            whs.append(wh)
        yd.append(acc)
    yd = jnp.concatenate(yd, axis=1)
    sb = s_prev.astype(MXU)
    yo = jnp.concatenate([_dot(cmb[g], sb[:, g * 512:(g + 1) * 512]) for g in range(2)], axis=1) * ecse
    xdec = (xdt * dece).astype(MXU)
    states = jnp.concatenate([_dot_tn(bmb[g], xdec[:, g * 512:(g + 1) * 512]) for g in range(2)], axis=1)
    s_next = s_prev * cde + states
    ypre = yd + yo + de * xs
    sig_z = _sigmoid(z)
    yg = ypre * z * sig_z
    outs, yhat, rr = [], [], []
    for g in range(2):
        sl = slice(g * 512, (g + 1) * 512)
        yh, r = _rms(yg[:, sl])
        yhat.append(yh)
        rr.append(r)
        outs.append(yh * ng[:, sl])
    return dict(shifts=shifts, conv=conv, sig_c=sig_c, xs=xs, bmb=bmb, cmb=cmb, dtpre=dtpre, dt=dt, a_neg=a_neg,
                cs=cs, last=last, ecs=ecs, dec=dec, dte=dte, ecse=ecse, dece=dece, cde=cde, de=de, xdt=xdt,
                mg=mg, lms=lms, whs=whs, lo=lo, yo=yo, sb=sb, xdec=xdec, s_next=s_next, ypre=ypre, sig_z=sig_z,
                yhat=yhat, rr=rr, out=jnp.concatenate(outs, axis=1))


def _ssd_specs(nch, rev):
    def tok(b, c):
        return b * nch + ((nch - 1 - c) if rev else c)

    return tok, [
        pl.BlockSpec((CH, 1024), lambda b, c: (tok(b, c), 2)),
        pl.BlockSpec((CH, CONV_CH), lambda b, c: (tok(b, c), 2)),
        pl.BlockSpec((8, CONV_CH), lambda b, c: (jnp.maximum(tok(b, c) * (CH // 8) - 1, 0), 2)),
        pl.BlockSpec((CH, 128), lambda b, c: (tok(b, c), 0)),
        _const((8, CONV_CH)), _const((1, CONV_CH)), _const((1, 128)), _const((1, 128)), _const((1, 1024)),
        _const((1, 1024)), _const((128, 1024)), _const((CH, CH)),
    ]


def _ssd_fwd(proj, dtraw, cat, convw, convb, dtb, alog, dskip, ng, ex, ltri, nb):
    T = proj.shape[0]
    nch = T // CH // nb
    tok, in_specs = _ssd_specs(nch, rev=False)

    def body(z_ref, xbc_ref, halo_ref, dt_ref, cw_ref, cb_ref, dtb_ref, al_ref, ds_ref, ng_ref, ex_ref, lt_ref,
             cat_in_ref, yb_ref, sall_ref, s_ref):
        del cat_in_ref
        c = pl.program_id(1)

        @pl.when(c == 0)
        def _():
            s_ref[...] = jnp.zeros_like(s_ref)

        halo = jnp.where(c == 0, 0.0, halo_ref[...])
        s_prev = s_ref[...]
        sall_ref[0] = s_prev
        f = _ssd_fwd_vals(z_ref[...], xbc_ref[...], halo, dt_ref[...], cw_ref[...], cb_ref[...], dtb_ref[...],
                          al_ref[...], ds_ref[...], ng_ref[...], ex_ref[...], lt_ref[...], s_prev)
        s_ref[...] = f["s_next"]
        yb_ref[...] = f["out"].astype(MXU)

    return pl.pallas_call(
        body, grid=(nb, nch), name="ssd_fwd",
        in_specs=in_specs + [_ANY],
        out_specs=[pl.BlockSpec((CH, 1024), lambda b, c: (tok(b, c), 1)),
                   pl.BlockSpec((1, 128, 1024), lambda b, c: (tok(b, c), 0, 0))],
        out_shape=[_sds((T, 2048), MXU), _sds((T // CH, 128, 1024), F32)],
        scratch_shapes=[pltpu.VMEM((128, 1024), F32)],
        input_output_aliases={12: 0},
        compiler_params=_cp(2),
    )(proj, proj, proj, dtraw, convw, convb, dtb, alog, dskip, ng, ex, ltri, cat)


def _outproj(cat, wo, x, g, tm):
    T = x.shape[0]

    def body(cat_ref, wo_ref, x_ref, g_ref, h1_ref, hn_ref):
        h1 = x_ref[...] + _dot(cat_ref[...], wo_ref[...])
        h1_ref[...] = h1
        hn_ref[...] = (_rms(h1)[0] * g_ref[...]).astype(MXU)

    return pl.pallas_call(
        body, grid=(T // tm,), name="outproj",
        in_specs=[_rows(tm, 2048), _const((2048, D)), _rows(tm, D), _const((1, D))],
        out_specs=[_rows(tm, D), _rows(tm, D)],
        out_shape=[_sds((T, D), F32), _sds((T, D), MXU)],
        compiler_params=_cp(),
    )(cat, wo, x, g)


def _ff1(hn, w1, tm):
    T = hn.shape[0]

    def body(hn_ref, w1_ref, hid_ref):
        hn_v = hn_ref[...]
        for n in range(4):
            hid_ref[:, n * 1024:(n + 1) * 1024] = jnp.maximum(_dot(hn_v, w1_ref[n]), 0.0).astype(MXU)

    return pl.pallas_call(
        body, grid=(T // tm,), name="ff1",
        in_specs=[_rows(tm, D), _const((4, D, 1024))],
        out_specs=_rows(tm, DFF),
        out_shape=_sds((T, DFF), MXU),
        compiler_params=_cp(),
    )(hn, w1)


def _sq(hid):
    h = hid.astype(F32)
    return (h * h).astype(MXU)


def _ff2_tail(hid, w2, h1, g_ple, p, tgt, wg, wp, gf, tm):
    T = h1.shape[0]

    def body(hid_ref, w2_ref, h1_ref, g_ref, p_ref, t_ref, wg_ref, wp_ref, gf_ref,
             hp_ref, dgl_ref, dpe_ref, dh2_ref, dh2b_ref, loss_ref, dgf_ref, dg_ref):
        @pl.when(pl.program_id(0) == 0)
        def _():
            loss_ref[...] = jnp.zeros_like(loss_ref)
            dgf_ref[...] = jnp.zeros_like(dgf_ref)
            dg_ref[...] = jnp.zeros_like(dg_ref)

        h2 = h1_ref[...] + _dot(_sq(hid_ref[...]), w2_ref[...])
        h2h, r2 = _rms(h2)
        g_ple = g_ref[...]
        hp = (h2h * g_ple).astype(MXU)
        hp_ref[...] = hp
        gate = _sigmoid(_dot(hp, wg_ref[...]))
        pb = p_ref[...].astype(MXU)
        pe = jnp.concatenate([_dot(pb, wp_ref[k]) for k in range(4)], axis=1)
        h3 = h2 + gate * pe
        hh, r = _rms(h3)
        gf = gf_ref[...]
        diff = hh * gf - t_ref[...]
        loss_ref[...] += 0.5 * jnp.sum(jnp.mean(diff * diff, axis=-1, keepdims=True))
        dout = diff * (1.0 / D)
        dgf_ref[...] += jnp.sum(dout * hh, axis=0, keepdims=True)
        dh3 = _rms_bwd(dout, hh, r, gf)
        dgl = (dh3 * pe * gate * (1.0 - gate)).astype(MXU)
        dgl_ref[...] = dgl
        dpe_ref[...] = (dh3 * gate).astype(MXU)
        dhp = _dot_nt(dgl, wg_ref[...])
        dg_ref[...] += jnp.sum(dhp * h2h, axis=0, keepdims=True)
        dh2 = dh3 + _rms_bwd(dhp, h2h, r2, g_ple)
        dh2_ref[...] = dh2
        dh2b_ref[...] = dh2.astype(MXU)

    return pl.pallas_call(
        body, grid=(T // tm,), name="ff2_tail",
        in_specs=[_rows(tm, DFF), _const((DFF, D)), _rows(tm, D), _const((1, D)), _rows(tm, DPLE), _rows(tm, D),
                  _const((D, D)), _const((4, DPLE, 256)), _const((1, D))],
        out_specs=[_rows(tm, D), _rows(tm, D), _rows(tm, D), _rows(tm, D), _rows(tm, D), _const((8, 128)),
                   _const((1, D)), _const((1, D))],
        out_shape=[_sds((T, D), MXU), _sds((T, D), MXU), _sds((T, D), MXU), _sds((T, D), F32), _sds((T, D), MXU),
                   _sds((8, 128), F32), _sds((1, D), F32), _sds((1, D), F32)],
        compiler_params=_cp(),
    )(hid, w2, h1, g_ple, p, tgt, wg, wp, gf)


def _ff2_bwd(dh2b, w2, hid, tm):
    T = hid.shape[0]

    def body(dh2b_ref, w2_ref, hid_ref, dpre_ref):
        d = dh2b_ref[...]
        for n in range(DFF // 1024):
            sl = slice(n * 1024, (n + 1) * 1024)
            da = _dot_nt(d, w2_ref[sl, :])
            dpre_ref[:, sl] = (2.0 * da * hid_ref[:, sl].astype(F32)).astype(MXU)

    return pl.pallas_call(
        body, grid=(T // tm,), name="ff2_bwd",
        in_specs=[_rows(tm, D), _const((DFF, D)), _rows(tm, DFF)],
        out_specs=_rows(tm, DFF),
        out_shape=_sds((T, DFF), MXU),
        compiler_params=_cp(),
    )(dh2b, w2, hid)


def _ff1_bwd(dpre, w1, dh2, h1, g, tm, after):
    T = h1.shape[0]

    def body(dpre_ref, w1_ref, dh2_ref, h1_ref, g_ref, dh1_ref, dh1b_ref, dg_ref):
        @pl.when(pl.program_id(0) == 0)
        def _():
            dg_ref[...] = jnp.zeros_like(dg_ref)

        dhn = _dot_nt(dpre_ref[:, 0:1024], w1_ref[0])
        for k in range(1, 4):
            dhn = dhn + _dot_nt(dpre_ref[:, k * 1024:(k + 1) * 1024], w1_ref[k])
        hh, r = _rms(h1_ref[...])
        dg_ref[...] += jnp.sum(dhn * hh, axis=0, keepdims=True)
        dh1 = dh2_ref[...] + _rms_bwd(dhn, hh, r, g_ref[...])
        dh1_ref[...] = dh1
        dh1b_ref[...] = dh1.astype(MXU)

    return pl.pallas_call(
        _after(5, body), grid=(T // tm,), name="ff1_bwd",
        in_specs=[_rows(tm, DFF), _const((4, D, 1024)), _rows(tm, D), _rows(tm, D), _const((1, D)), _ANY],
        out_specs=[_rows(tm, D), _rows(tm, D), _const((1, D))],
        out_shape=[_sds((T, D), F32), _sds((T, D), MXU), _sds((1, D), F32)],
        compiler_params=_cp(),
    )(dpre, w1, dh2, h1, g, after)


def _outproj_bwd(dh1b, wo, tm):
    T = dh1b.shape[0]

    def body(d_ref, wo_ref, dcat_ref):
        d = d_ref[...]
        dcat_ref[:, 0:1024] = _dot_nt(d, wo_ref[0:1024, :])
        dcat_ref[:, 1024:2048] = _dot_nt(d, wo_ref[1024:2048, :])

    return pl.pallas_call(
        body, grid=(T // tm,), name="outproj_bwd",
        in_specs=[_rows(tm, D), _const((2048, D))],
        out_specs=_rows(tm, 2048),
        out_shape=_sds((T, 2048), F32),
        compiler_params=_cp(),
    )(dh1b, wo)


def _gmlp_bwd(proj, dcat, gv, ws, bst, gout):
    T = proj.shape[0]
    nck = 2 if T % (2 * CH) == 0 else 1
    tb = nck * CH

    def body(u_ref, v_ref, dya_ref, gv_ref, ws_ref, bst_ref, gout_ref, duv_ref, dgv_ref, dws_ref, dbst_ref, dgo_ref):
        @pl.when(pl.program_id(0) == 0)
        def _():
            dgv_ref[...] = jnp.zeros_like(dgv_ref)
            dws_ref[...] = jnp.zeros_like(dws_ref)
            dbst_ref[...] = jnp.zeros_like(dbst_ref)
            dgo_ref[...] = jnp.zeros_like(dgo_ref)

        for k in range(nck):
            chunk(slice(k * CH, (k + 1) * CH), u_ref, v_ref, dya_ref, gv_ref, ws_ref, bst_ref, gout_ref, duv_ref,
                  dgv_ref, dws_ref, dbst_ref, dgo_ref)

    def chunk(rows, u_ref, v_ref, dya_ref, gv_ref, ws_ref, bst_ref, gout_ref, duv_ref, dgv_ref, dws_ref, dbst_ref,
              dgo_ref):
        gv = gv_ref[...]
        f = _gmlp_fwd_vals(u_ref[rows, :], v_ref[rows, :], gv, ws_ref, bst_ref[...], gout_ref[...])
        dya = dya_ref[rows, :]
        dgo_ref[...] += jnp.sum(dya * f["yhat"], axis=0, keepdims=True)
        dy = _rms_bwd(dya, f["yhat"], f["ry"], gout_ref[...])
        lane = lax.broadcasted_iota(jnp.int32, (CH, 128), 1)
        dbs = jnp.zeros((CH, 128), F32)
        dug, dvg, dgvs = [], [], []
        for h in range(GM_HEADS):
            sl = slice(h * 128, (h + 1) * 128)
            vhat, rv, vn, wt, mixed = f["heads"][h]
            dyh = dy[:, sl]
            dug.append(dyh * mixed)
            dmixed = dyh * f["ug"][:, sl]
            dmb = dmixed.astype(MXU)
            dws_ref[h] += jnp.where(f["tril"], _dot_nt(dmb, vn), 0.0)
            dbs = dbs + jnp.where(lane == h, jnp.sum(dmixed, axis=1, keepdims=True), 0.0)
            dvn = _dot_tn(wt.astype(MXU), dmb)
            dgvs.append(jnp.sum(dvn * vhat, axis=0, keepdims=True))
            dvg.append(_rms_bwd(dvn, vhat, rv, gv[:, sl]))
        dbst_ref[...] += dbs
        dgv_ref[...] += jnp.concatenate(dgvs, axis=1)
        duv_ref[rows, 0:1024] = (jnp.concatenate(dug, axis=1) * f["dug"]).astype(MXU)
        duv_ref[rows, 1024:2048] = (jnp.concatenate(dvg, axis=1) * f["dvg"]).astype(MXU)

    return pl.pallas_call(
        body, grid=(T // tb,), name="gmlp_bwd",
        in_specs=[_rows(tb, 1024, 0), _rows(tb, 1024, 1), _rows(tb, 1024, 0), _const((1, 1024)),
                  _const((GM_HEADS, CH, CH)), _const((CH, 128)), _const((1, 1024))],
        out_specs=[_rows(tb, 2048), _const((1, 1024)), _const((GM_HEADS, CH, CH)), _const((CH, 128)),
                   _const((1, 1024))],
        out_shape=[_sds((T, 2048), MXU), _sds((1, 1024), F32), _sds((GM_HEADS, CH, CH), F32), _sds((CH, 128), F32),
                   _sds((1, 1024), F32)],
        compiler_params=_cp(),
    )(proj, proj, dcat, gv, ws, bst, gout)


def _ssd_bwd(proj, dtraw, sall, dcat, convw, convb, dtb, alog, dskip, ng, ex, ltri, ext, nb, after):
    T = proj.shape[0]
    nch = T // CH // nb
    tok, in_specs = _ssd_specs(nch, rev=True)
    in_specs = in_specs + [
        _const((1024, 128)),
        pl.BlockSpec((1, 128, 1024), lambda b, c: (tok(b, c), 0, 0)),
        pl.BlockSpec((CH, 1024), lambda b, c: (tok(b, c), 1)),
        _ANY,
    ]

    def body(z_ref, xbc_ref, halo_ref, dt_ref, cw_ref, cb_ref, dtb_ref, al_ref, ds_ref, ng_ref, ex_ref, lt_ref,
             ext_ref, sall_ref, dyb_ref,
             dssd_ref, ddt_ref, dcw_ref, dcb_ref, ddtb_ref, dal_ref, dds_ref, dng_ref,
             dst_ref, dnext_ref, ddse_ref):
        b = pl.program_id(0)
        c = pl.program_id(1)

        @pl.when((b == 0) & (c == 0))
        def _():
            for r in (dcw_ref, dcb_ref, ddtb_ref, dal_ref, dds_ref, dng_ref, ddse_ref):
                r[...] = jnp.zeros_like(r)

        @pl.when(c == 0)
        def _():
            dst_ref[...] = jnp.zeros_like(dst_ref)
            dnext_ref[...] = jnp.zeros_like(dnext_ref)

        first_chunk = c == nch - 1
        halo = jnp.where(first_chunk, 0.0, halo_ref[...])
        z = z_ref[...]
        ex = ex_ref[...]
        ext = ext_ref[...]
        cw = cw_ref[...]
        ng = ng_ref[...]
        s_prev = sall_ref[0]
        f = _ssd_fwd_vals(z, xbc_ref[...], halo, dt_ref[...], cw, cb_ref[...], dtb_ref[...], al_ref[...],
                          ds_ref[...], ng, ex, lt_ref[...], s_prev)
        xs, xdt, cs, dec, dt = f["xs"], f["xdt"], f["cs"], f["dec"], f["dt"]
        dyb = dyb_ref[...]
        dyg, dngs = [], []
        for g in range(2):
            sl = slice(g * 512, (g + 1) * 512)
            dngs.append(jnp.sum(dyb[:, sl] * f["yhat"][g], axis=0, keepdims=True))
            dyg.append(_rms_bwd(dyb[:, sl], f["yhat"][g], f["rr"][g], ng[:, sl]))
        dng_ref[...] += jnp.concatenate(dngs, axis=1)
        dyg = jnp.concatenate(dyg, axis=1)
        sig_z = f["sig_z"]
        silu_z = z * sig_z
        dy = dyg * silu_z
        dz = dyg * f["ypre"] * sig_z * (1.0 + z * (1.0 - sig_z))
        ddse_ref[...] += jnp.sum(dy * xs, axis=0, keepdims=True)

        @pl.when((b == nb - 1) & (c == nch - 1))
        def _():
            dds_ref[...] = _dot_01(jnp.broadcast_to(ddse_ref[...], (8, 1024)), ext)[0:1]

        dxs = dy * f["de"]
        dye = dy * f["ecse"]
        dyeb = dye.astype(MXU)
        dst = dst_ref[...]
        dstb = dst.astype(MXU)
        bmb, cmb, sb, xdec = f["bmb"], f["cmb"], f["sb"], f["xdec"]
        u = jnp.concatenate([_dot(bmb[g], dstb[:, g * 512:(g + 1) * 512]) for g in range(2)], axis=1)
        dxdt = [u[:, q * 128:(q + 1) * 128] * f["dece"][:, q * 128:(q + 1) * 128] for q in range(8)]
        per_head = _dot_01(jnp.concatenate(
            [dy * f["yo"], u * xdt, jnp.broadcast_to(jnp.sum(dst * s_prev, axis=0, keepdims=True), (8, 1024))],
            axis=0), ext)
        dcs = per_head[0:CH]
        t = per_head[CH:2 * CH] * dec
        dcd = per_head[2 * CH:2 * CH + 1]
        row = lax.broadcasted_iota(jnp.int32, (CH, 128), 0)
        lane = lax.broadcasted_iota(jnp.int32, (CH, 128), 1)
        cd = jnp.exp(f["last"])
        dcs = dcs - t + jnp.where(row == CH - 1, jnp.sum(t, axis=0, keepdims=True) + dcd * cd, 0.0)
        dcst = jnp.zeros((128, CH), F32)
        lo = f["lo"]
        dbm, dcm, ds_prev = [], [], []
        for g in range(2):
            sl = slice(g * 512, (g + 1) * 512)
            dmg = jnp.zeros((CH, CH), F32)
            for q in range(4 * g, 4 * g + 4):
                dyq = dy[:, q * 128:(q + 1) * 128]
                xq = xdt[:, q * 128:(q + 1) * 128].astype(MXU)
                for hh in range(2):
                    h = 2 * q + hh
                    m = lo if hh == 0 else ~lo
                    dym = jnp.where(m, dyq, 0.0).astype(MXU)
                    gh = _dot_nt(dym, xq)
                    gl = gh * f["lms"][h]
                    dmg = dmg + gl
                    qh = gl * f["mg"][g]
                    dcs = dcs + jnp.where(lane == h, jnp.sum(qh, axis=1, keepdims=True), 0.0)
                    dcst = dcst - jnp.where(row == h, jnp.sum(qh, axis=0, keepdims=True), 0.0)
                    dxdt[q] = dxdt[q] + _dot_tn(f["whs"][h], dym)
            dmgb = dmg.astype(MXU)
            dcm.append(_dot(dmgb, bmb[g]) + _dot_nt(dyeb[:, sl], sb[:, sl]))
            dbm.append(_dot_tn(dmgb, cmb[g]) + _dot_nt(xdec[:, sl], dstb[:, sl]))
            ds_prev.append(_dot_tn(cmb[g], dyeb[:, sl]))
        dst_ref[...] = jnp.concatenate(ds_prev, axis=1) + dst * f["cde"]
        dcs = dcs + dcst.T
        da = _dot_hi(lt_ref[...].T, dcs)
        dxdt = jnp.concatenate(dxdt, axis=1)
        a_neg = f["a_neg"]
        ddt = da * a_neg + _dot_01(dxdt * xs, ext)
        dal_ref[...] += jnp.sum(da * dt, axis=0, keepdims=True) * a_neg
        dxs = dxs + dxdt * f["dte"]
        ddtraw = jnp.where(lane < SSD_HEADS, ddt * _sigmoid(f["dtpre"]), 0.0)
        ddtb_ref[...] += jnp.sum(ddtraw, axis=0, keepdims=True)
        ddt_ref[...] = ddtraw.astype(MXU)
        dxa = jnp.concatenate([dxs, dbm[0], dbm[1], dcm[0], dcm[1]], axis=1)
        sig_c = f["sig_c"]
        dconv = dxa * sig_c * (1.0 + f["conv"] * (1.0 - sig_c))
        dcb_ref[...] += jnp.sum(dconv, axis=0, keepdims=True)
        for k in range(4):
            dcw_ref[k:k + 1, :] += jnp.sum(dconv * f["shifts"][3 - k], axis=0, keepdims=True)
        dxbc = cw[3:4] * dconv
        for j, up in zip((1, 2, 3), _shifts_up(dconv, dnext_ref[...])):
            dxbc = dxbc + cw[3 - j:4 - j] * up
        dnext_ref[...] = dconv[0:8]
        dssd_ref[:, 0:1024] = dz.astype(MXU)
        dssd_ref[:, 1024:2560] = dxbc.astype(MXU)

    return pl.pallas_call(
        _after(15, body), grid=(nb, nch), name="ssd_bwd",
        in_specs=in_specs,
        out_specs=[pl.BlockSpec((CH, 2560), lambda b, c: (tok(b, c), 0)),
                   pl.BlockSpec((CH, 128), lambda b, c: (tok(b, c), 0)),
                   _const((8, CONV_CH)), _const((1, CONV_CH)), _const((1, 128)), _const((1, 128)), _const((1, 128)),
                   _const((1, 1024))],
        out_shape=[_sds((T, 2560), MXU), _sds((T, 128), MXU), _sds((8, CONV_CH), F32), _sds((1, CONV_CH), F32),
                   _sds((1, 128), F32), _sds((1, 128), F32), _sds((1, 128), F32), _sds((1, 1024), F32)],
        scratch_shapes=[pltpu.VMEM((128, 1024), F32), pltpu.VMEM((8, CONV_CH), F32), pltpu.VMEM((1, 1024), F32)],
        compiler_params=_cp(2),
    )(proj, proj, proj, dtraw, convw, convb, dtb, alog, dskip, ng, ex, ltri, ext, sall, dcat, after)


def _inproj_bwd(duv, dssd, ddt, wm, wdt, dh1, x, g, tm, after):
    T = x.shape[0]

    def body(duv_ref, dssd_ref, ddt_ref, wm_ref, wdt_ref, dh1_ref, x_ref, g_ref, dx_ref, dg_ref):
        @pl.when(pl.program_id(0) == 0)
        def _():
            dg_ref[...] = jnp.zeros_like(dg_ref)

        dxn = (_dot_nt(duv_ref[...], wm_ref[:, 0:2048]) + _dot_nt(dssd_ref[...], wm_ref[:, 2048:N_MAIN])
               + _dot_nt(ddt_ref[...], wdt_ref[...]))
        xh, r = _rms(x_ref[...])
        dg_ref[...] += jnp.sum(dxn * xh, axis=0, keepdims=True)
        dx_ref[...] = dh1_ref[...] + _rms_bwd(dxn, xh, r, g_ref[...])

    return pl.pallas_call(
        _after(8, body), grid=(T // tm,), name="inproj_bwd",
        in_specs=[_rows(tm, 2048), _rows(tm, 2560), _rows(tm, 128), _const((D, N_MAIN)), _const((D, 128)),
                  _rows(tm, D), _rows(tm, D), _const((1, D)), _ANY],
        out_specs=[_rows(tm, D), _const((1, D))],
        out_shape=[_sds((T, D), F32), _sds((1, D), F32)],
        compiler_params=_cp(),
    )(duv, dssd, ddt, wm, wdt, dh1, x, g, after)


def _matmul_tn(a, b, name, a_fn=None):
    T, M = a.shape
    N = b.shape[1]
    tm = min(M, 1024)
    tn = 1280 if N == 2560 else min(N, 1024)
    tk = min(T, 2048)

    def body(a_ref, b_ref, o_ref, acc_ref):
        k = pl.program_id(2)

        @pl.when(k == 0)
        def _():
            acc_ref[...] = jnp.zeros_like(acc_ref)

        av = a_ref[...]
        if a_fn is not None:
            av = a_fn(av)
        acc_ref[...] += _dot_tn(av, b_ref[...])

        @pl.when(k == T // tk - 1)
        def _():
            o_ref[...] = acc_ref[...].astype(o_ref.dtype)

    return pl.pallas_call(
        body, grid=(M // tm, N // tn, T // tk), name=name,
        in_specs=[pl.BlockSpec((tk, tm), lambda i, j, k: (k, i)), pl.BlockSpec((tk, tn), lambda i, j, k: (k, j))],
        out_specs=pl.BlockSpec((tm, tn), lambda i, j, k: (i, j)),
        out_shape=_sds((M, N), GRAD),
        scratch_shapes=[pltpu.VMEM((tm, tn), F32)],
        compiler_params=_cp(3),
    )(a, b)


def _adamw_vals(w, g, m, v):
    m = B1 * m + (1.0 - B1) * g
    v = B2 * v + (1.0 - B2) * (g * g)
    m_hat = m / (1.0 - B1 ** STEP)
    v_hat = v / (1.0 - B2 ** STEP)
    return -LR * (m_hat / (jnp.sqrt(v_hat) + ADAM_EPS) + WD * w), m, v


def _adamw(w, g, m, v, name):
    R, C = w.shape
    tr = 256 if R % 256 == 0 else R

    def body(w_ref, g_ref, m_ref, v_ref, d_ref, mo_ref, vo_ref):
        d_ref[...], mo_ref[...], vo_ref[...] = _adamw_vals(w_ref[...], g_ref[...], m_ref[...], v_ref[...])

    spec = _rows(tr, C)
    return pl.pallas_call(
        body, grid=(R // tr,), name=name,
        in_specs=[spec] * 4, out_specs=[spec] * 3, out_shape=[_sds((R, C), F32)] * 3,
        compiler_params=_cp(),
    )(w, g, m, v)


def _adamw_halves(w, own, other, m, v, name):
    R, C = w.shape
    half = R // 2
    tr = min(half, 256)
    nth = half // tr

    def body(w_ref, own_ref, oth_ref, m_ref, v_ref, g_ref, d_ref, mo_ref, vo_ref):
        mine = (pl.program_id(0) // nth) == lax.axis_index("c")
        g = jnp.where(mine, own_ref[...], oth_ref[...])
        g_ref[...] = g
        d_ref[...], mo_ref[...], vo_ref[...] = _adamw_vals(w_ref[...], g, m_ref[...], v_ref[...])

    full = _rows(tr, C)
    part = pl.BlockSpec((tr, C), lambda i: (i % nth, 0))
    return pl.pallas_call(
        body, grid=(R // tr,), name=name,
        in_specs=[full, part, part, full, full], out_specs=[full] * 4, out_shape=[_sds((R, C), F32)] * 4,
        compiler_params=_cp(),
    )(w, own, other, m, v)


def _sum_small(slots, name):
    nd, rows, C = slots.shape

    def body(s_ref, o_ref):
        acc = s_ref[0]
        for d in range(1, nd):
            acc = acc + s_ref[d]
        o_ref[...] = acc

    return pl.pallas_call(
        body, grid=(1,), name=name,
        in_specs=[_const((nd, rows, C))], out_specs=_const((rows, C)), out_shape=_sds((rows, C), F32),
        compiler_params=_cp(),
    )(slots)


def _sum_slots(slots, src, kind, shp, kh, name):
    R, C = shp
    rh = R // 2
    tr = min(rh, 256)
    nth = rh // tr
    if kind == "slab":
        src_spec = pl.BlockSpec((1, tr, C), lambda i, kh: (kh[0], kh[1] * nth + i, 0))
    elif kind == "rows":
        src_spec = pl.BlockSpec((tr, C), lambda i, kh: (kh[0] * (R // tr) + kh[1] * nth + i, 0))
    else:
        src_spec = pl.BlockSpec((tr, C), lambda i, kh: (kh[1] * nth + i, kh[0]))

    def body(kh_ref, s_ref, own_ref, o_ref):
        me = 2 * kh_ref[0] + kh_ref[1]
        acc = (own_ref[0] if kind == "slab" else own_ref[...]).astype(F32)
        for k in range(1, 8):
            acc = acc + s_ref[me ^ k].astype(F32)
        o_ref[...] = acc

    return pl.pallas_call(
        body, name=name,
        grid_spec=pltpu.PrefetchScalarGridSpec(
            num_scalar_prefetch=1, grid=(nth,),
            in_specs=[pl.BlockSpec((8, tr, C), lambda i, kh: (0, i, 0)), src_spec],
            out_specs=pl.BlockSpec((tr, C), lambda i, kh: (i, 0))),
        out_shape=_sds((rh, C), F32),
        compiler_params=_cp(),
    )(kh, slots, src)


def _cast_into_slot(w, kh, name):
    R, C = w.shape
    tr = 256

    def body(kh_ref, w_ref, o_ref):
        o_ref[0] = w_ref[...].astype(BF16)

    return pl.pallas_call(
        body, name=name,
        grid_spec=pltpu.PrefetchScalarGridSpec(
            num_scalar_prefetch=1, grid=(R // tr,),
            in_specs=[pl.BlockSpec((tr, C), lambda i, kh: (i, 0))],
            out_specs=pl.BlockSpec((1, tr, C), lambda i, kh: (kh[0], i, 0))),
        out_shape=_sds((4, R, C), BF16),
        compiler_params=_cp(),
    )(kh, w)


_ANY = pl.BlockSpec(memory_space=pl.ANY)
_CHIP_FLIPS = [(1, 0), (0, 1), (1, 1)]
_DEVICE_FLIPS = [(fx, fy, fc) for fx in (0, 1) for fy in (0, 1) for fc in (0, 1)][1:]


def _half(h, rows):
    return pl.ds(pl.multiple_of(h * rows, rows), rows)


def _remote(src, dst, ssem, rsem, to):
    return pltpu.make_async_remote_copy(src_ref=src, dst_ref=dst, send_sem=ssem, recv_sem=rsem,
                                        device_id=to, device_id_type=MESH)


def _weight_gather(bufs, conv):
    n = len(bufs)

    def body(*refs):
        conv_ref, outs, conv_out = refs[n], refs[n + 1:2 * n + 1], refs[2 * n + 1]
        send_sems, recv_sems, fsend_sems, frecv_sems, csend_sems, crecv_sems, local_sem = refs[2 * n + 2:]
        x, y, c = lax.axis_index("x"), lax.axis_index("y"), lax.axis_index("c")
        me = 2 * x + y
        halves = [_half(c, r.shape[1] // 2) for r in outs]
        others = [_half(1 - c, r.shape[1] // 2) for r in outs]
        remote = _remote
        local = [pltpu.make_async_copy(conv_ref, conv_out.at[me], local_sem)]
        for cp in local:
            cp.start()
        sends = []
        for k, (fx, fy) in enumerate(_CHIP_FLIPS):
            peer = (x ^ fx, y ^ fy, c)
            for i in range(n):
                mine = outs[i].at[me, halves[i]]
                sends.append(remote(mine, mine, send_sems.at[k * n + i], recv_sems.at[k * n + i], peer))
            sends.append(remote(conv_ref, conv_out.at[me], csend_sems.at[k], crecv_sems.at[k], peer))
        for cp in sends:
            cp.start()
        sibling = (x, y, 1 - c)
        forwards = []
        for k, (fx, fy) in enumerate(_CHIP_FLIPS):
            peer = (x ^ fx, y ^ fy, c)
            src = 2 * (x ^ fx) + (y ^ fy)
            for i in range(n):
                landed = outs[i].at[src, halves[i]]
                remote(landed, landed, send_sems.at[k * n + i], recv_sems.at[k * n + i], peer).wait_recv()
                fw = remote(landed, landed, fsend_sems.at[k * n + i], frecv_sems.at[k * n + i], sibling)
                fw.start()
                forwards.append(fw)
            remote(conv_out.at[src], conv_out.at[src], csend_sems.at[k], crecv_sems.at[k], peer).wait_recv()
        for k, (fx, fy) in enumerate(_CHIP_FLIPS):
            src = 2 * (x ^ fx) + (y ^ fy)
            for i in range(n):
                theirs = outs[i].at[src, others[i]]
                remote(theirs, theirs, fsend_sems.at[k * n + i], frecv_sems.at[k * n + i], sibling).wait_recv()
        for cp in sends + forwards:
            cp.wait_send()
        for cp in local:
            cp.wait()

    dma = pltpu.SemaphoreType.DMA
    return pl.pallas_call(
        body, name="weight_gather",
        in_specs=[_ANY] * (n + 1), out_specs=[_ANY] * (n + 1),
        out_shape=[_sds(b.shape, b.dtype) for b in bufs] + [_sds((4,) + conv.shape, conv.dtype)],
        input_output_aliases={i: i for i in range(n)},
        scratch_shapes=[dma((3 * n,)), dma((3 * n,)), dma((3 * n,)), dma((3 * n,)), dma((3,)), dma((3,)), dma],
    )(*bufs, conv)


def _piece(ref, kind, R, C, k, h):
    if kind == "slab":
        return ref.at[k, _half(h, R // 2), :]
    if kind == "rows":
        return ref.at[pl.ds(pl.multiple_of(k * R + h * (R // 2), R // 2), R // 2), :]
    return ref.at[_half(h, R // 2), pl.ds(pl.multiple_of(k * C, C), C)]


def _small_exchange(small):
    rs = small.shape[0]

    def body(s_ref, out_ref, send_sems, recv_sems, local_sem):
        x, y, c = lax.axis_index("x"), lax.axis_index("y"), lax.axis_index("c")
        slot = 4 * x + 2 * y + c
        own = pltpu.make_async_copy(s_ref, out_ref.at[slot], local_sem)
        own.start()
        copies = []
        for k, (fx, fy, fc) in enumerate(_DEVICE_FLIPS):
            copies.append(_remote(s_ref, out_ref.at[slot], send_sems.at[k], recv_sems.at[k], (x ^ fx, y ^ fy, c ^ fc)))
        for cp in copies:
            cp.start()
        for k, (fx, fy, fc) in enumerate(_DEVICE_FLIPS):
            theirs = out_ref.at[slot ^ (k + 1)]
            _remote(theirs, theirs, send_sems.at[k], recv_sems.at[k], (x ^ fx, y ^ fy, c ^ fc)).wait_recv()
        for cp in copies:
            cp.wait_send()
        own.wait()

    dma = pltpu.SemaphoreType.DMA
    return pl.pallas_call(
        body, name="small_exchange",
        in_specs=[_ANY], out_specs=_ANY, out_shape=_sds((8, rs, 128), F32),
        scratch_shapes=[dma((7,)), dma((7,)), dma],
    )(small)


_HBM = pl.BlockSpec(memory_space=pltpu.HBM)
_SEM = pl.BlockSpec(memory_space=pltpu.SEMAPHORE)


def _split_start(name, arrays, n_copies, plan, after=None):
    n = len(arrays)
    extra = [] if after is None else [after]

    def body(*refs):
        m = n + len(extra)
        arrs, send_sems, recv_sems, token = refs[:n], refs[m], refs[m + 1], refs[-1]
        for j, (src, dst, peer) in enumerate(plan(arrs)):
            _remote(src, dst, send_sems.at[j], recv_sems.at[j], peer).start()
        token[...] = jnp.zeros_like(token)

    dma = pltpu.SemaphoreType.DMA
    res = pl.pallas_call(
        body, name=name,
        out_shape=(dma((n_copies,)), dma((n_copies,)), *[pltpu.HBM(a.shape, a.dtype) for a in arrays],
                   _sds((8, 128), F32)),
        in_specs=[_HBM] * n + [_ANY] * len(extra),
        out_specs=(_SEM, _SEM, *[_HBM] * n, pl.BlockSpec(memory_space=pltpu.VMEM)),
        input_output_aliases={i: 2 + i for i in range(n)},
        compiler_params=pltpu.CompilerParams(has_side_effects=pltpu.SideEffectType.DATAFLOW_SIDE_EFFECTING),
    )(*[pltpu.with_memory_space_constraint(a, pltpu.HBM) for a in arrays], *extra)
    return res[0], res[1], list(res[2:2 + n]), res[-1]


def _split_wait(name, arrays, send_sems, recv_sems, plan, after):
    n = len(arrays)

    def body(*refs):
        arrs, ssems, rsems = refs[:n], refs[n], refs[n + 1]
        for j, (src, dst, peer) in enumerate(plan(arrs)):
            cp = _remote(src, dst, ssems.at[j], rsems.at[j], peer)
            cp.wait_send()
            cp.wait_recv()

    return list(pl.pallas_call(
        body, name=name,
        out_shape=tuple(pltpu.HBM(a.shape, a.dtype) for a in arrays),
        in_specs=[_HBM] * n + [_SEM, _SEM, _ANY],
        out_specs=tuple([_HBM] * n),
        input_output_aliases={i: i for i in range(n)},
        compiler_params=pltpu.CompilerParams(has_side_effects=pltpu.SideEffectType.DATAFLOW_SIDE_EFFECTING),
    )(*arrays, send_sems, recv_sems, after))


def _gather_plan(n):
    def plan(bufs):
        x, y, c = lax.axis_index("x"), lax.axis_index("y"), lax.axis_index("c")
        me = 2 * x + y
        return [(bufs[i].at[me], bufs[i].at[me], (x ^ fx, y ^ fy, c)) for fx, fy in _CHIP_FLIPS for i in range(n)]

    return plan


def _reduce_plan(specs, n_small):
    n = len(specs)

    def plan(arrs):
        x, y, c = lax.axis_index("x"), lax.axis_index("y"), lax.axis_index("c")
        slot = 4 * x + 2 * y + c
        out = []
        for fx, fy, fc in _DEVICE_FLIPS:
            peer = (x ^ fx, y ^ fy, c ^ fc)
            for i, (kind, (R, C)) in enumerate(specs):
                out.append((_piece(arrs[i], kind, R, C, 2 * peer[0] + peer[1], peer[2]), arrs[n + i].at[slot], peer))
            for s in range(n_small):
                out.append((arrs[2 * n + 2 * s], arrs[2 * n + 2 * s + 1].at[slot], peer))
        return out

    return plan


def _sibling_exchange(halves):
    n = len(halves)

    def body(*refs):
        ins, outs, send_sems, recv_sems = refs[:n], refs[n:2 * n], refs[2 * n], refs[2 * n + 1]
        sibling = (lax.axis_index("x"), lax.axis_index("y"), 1 - lax.axis_index("c"))
        copies = [pltpu.make_async_remote_copy(src_ref=ins[i], dst_ref=outs[i], send_sem=send_sems.at[i],
                                               recv_sem=recv_sems.at[i], device_id=sibling, device_id_type=MESH)
                  for i in range(n)]
        for cp in copies:
            cp.start()
        for cp in copies:
            cp.wait()

    dma = pltpu.SemaphoreType.DMA
    return pl.pallas_call(
        body, name="sibling_exchange",
        in_specs=[_ANY] * n, out_specs=[_ANY] * n,
        out_shape=[_sds(h.shape, h.dtype) for h in halves],
        scratch_shapes=[dma((n,)), dma((n,))],
    )(*halves)


_BIG = [("w_in", (1024, 1156), "slab"), ("w_out", (512, 1024), "rows"), ("w_ff1", (1024, 1024), "cols"),
        ("w_ff2", (1024, 1024), "rows"), ("w_ple_gate", (256, 1024), "rows"), ("w_ple_proj", (256, 256), "cols")]
_SMALL = [("norm_mix_g", (1, 1024)), ("gm_v_norm_g", (1, 1024)), ("gm_ws", (1, 8, 128, 128)), ("gm_bs", (1, 8, 128)),
          ("gm_out_norm_g", (1, 1024)), ("ssd_conv_w", (1, 4, 1536)), ("ssd_conv_b", (1, 1536)),
          ("ssd_dt_bias", (1, 16)), ("ssd_a_log", (1, 16)), ("ssd_d", (1, 16)), ("ssd_norm_g", (1, 1024)),
          ("norm_mlp_g", (1, 1024)), ("ple_norm_g", (1, 1024)), ("final_norm_g", (1024,))]


def _rows128(a):
    flat = a.reshape(-1)
    rows = -(-flat.shape[0] // 1024) * 8
    return jnp.pad(flat, (0, rows * 128 - flat.shape[0])).reshape(rows, 128)


def _pad_lanes(v, n=128):
    v = v.reshape(1, -1)
    return jnp.pad(v, ((0, 0), (0, n - v.shape[1])))


_SMALL_SHAPES = dict(_SMALL + [("loss", ())])
_BIG_SPECS = {n: (kind, shp) for n, shp, kind in _BIG}


class _Comm:
    def __init__(self, a, kh):
        self.a, self.kh = a, kh
        self.bufs = {n: _cast_into_slot(a[n].reshape(shp), kh, "cast_" + n) for n, shp, _ in _BIG}
        self.sent = []
        self.small_tot = {}

    def w_in(self):
        (g_win,), g_cw = self._gather_now()
        rest = [self.bufs[n] for n, _, _ in _BIG[1:]]
        plan = _gather_plan(len(rest))
        ssem, rsem, thru, token = _split_start("gather_start", rest, 3 * len(rest), plan, after=g_cw)
        self.gather = (plan, ssem, rsem, thru)
        w_in_full = jnp.concatenate([g_win[k] for k in range(4)], axis=1)
        wm, wdt = w_in_full[:, :N_MAIN], jnp.pad(w_in_full[:, N_MAIN:], ((0, 0), (0, 128 - 16)))
        return wm, wdt, jnp.concatenate([g_cw[k] for k in range(4)], axis=1), token

    def _gather_now(self):
        *bufs, g_cw = _weight_gather([self.bufs["w_in"]], self.a["ssd_conv_w"].reshape(4, 384))
        return bufs, g_cw

    def rest(self, after):
        plan, ssem, rsem, thru = self.gather
        g_wo, g_w1, g_w2, g_wg, g_wp = _split_wait("gather_wait", thru, ssem, rsem, plan, after)
        return g_wo.reshape(2048, D), g_w1, g_w2.reshape(DFF, D), g_wg.reshape(D, D), g_wp

    def send(self, tag, grads):
        big = [n for n, _, _ in _BIG if n in grads]
        small = [n for n in _SMALL_SHAPES if n in grads]
        parts = [_rows128(grads[n]) for n in small]
        rows = [s.shape[0] for s in parts]
        if not big:
            self._unpack(_sum_small(_small_exchange(jnp.concatenate(parts, axis=0)), "sum_small_" + tag), small, rows)
            return None
        srcs = [jnp.stack([grads[n][:, 1156 * k:1156 * (k + 1)] for k in range(4)]) if n == "w_in" else grads[n]
                for n in big]
        lands = [lax.empty((8, _BIG_SPECS[n][1][0] // 2, _BIG_SPECS[n][1][1]), GRAD) for n in big]
        extra = []
        if small:
            pack = jnp.concatenate(parts, axis=0)
            extra = [pack, jnp.broadcast_to(pack, (8,) + pack.shape)]
        plan = _reduce_plan([_BIG_SPECS[n] for n in big], len(extra) // 2)
        n_copies = 7 * (len(big) + len(extra) // 2)
        ssem, rsem, thru, token = _split_start("reduce_start_" + tag, srcs + lands + extra, n_copies, plan)
        self.sent.append((tag, big, small, rows, plan, ssem, rsem, thru))
        return token

    def _unpack(self, tot, names, rows):
        o = 0
        for n, r in zip(names, rows):
            shp = _SMALL_SHAPES[n]
            cnt = 1
            for s in shp:
                cnt *= s
            self.small_tot[n] = tot[o:o + r].reshape(-1)[:cnt].reshape(shp)
            o += r

    def finish(self, after):
        own = {}
        for tag, big, small, rows, plan, ssem, rsem, thru in self.sent:
            arrs = _split_wait("reduce_wait_" + tag, thru, ssem, rsem, plan, after)
            nb_ = len(big)
            for i, n in enumerate(big):
                kind, shp = _BIG_SPECS[n]
                own[n] = _sum_slots(arrs[nb_ + i], arrs[i], kind, shp, self.kh, "sum_" + n)
                after = own[n]
            if small:
                self._unpack(_sum_small(arrs[2 * nb_ + 1], "sum_small_" + tag), small, rows)
        return [own[n] for n, _, _ in _BIG], dict(self.small_tot)


def _local_step(x, p, tgt, sm, comm, nb, tm):
    wm, wdt, conv_w, token = comm.w_in()
    g_mix, gv, gout = sm["norm_mix_g"].reshape(1, D), sm["gm_v_norm_g"].reshape(1, D), sm["gm_out_norm_g"].reshape(1, D)
    ws = sm["gm_ws"].reshape(GM_HEADS, CH, CH)
    bst = jnp.pad(sm["gm_bs"].reshape(GM_HEADS, CH).T, ((0, 0), (0, 128 - GM_HEADS)))
    convw = jnp.pad(conv_w, ((0, 4), (0, 0)))
    convb = sm["ssd_conv_b"].reshape(1, CONV_CH)
    dtb, alog = _pad_lanes(sm["ssd_dt_bias"]), _pad_lanes(sm["ssd_a_log"])
    dskip = jnp.repeat(sm["ssd_d"].reshape(SSD_HEADS), SSD_P).reshape(1, 1024)
    ng, g_mlp, g_ple = sm["ssd_norm_g"].reshape(1, D), sm["norm_mlp_g"].reshape(1, D), sm["ple_norm_g"].reshape(1, D)
    gf = sm["final_norm_g"].reshape(1, D)
    head_of_lane = lax.broadcasted_iota(jnp.int32, (128, 1024), 1) // SSD_P
    ex = (lax.broadcasted_iota(jnp.int32, (128, 1024), 0) == head_of_lane).astype(BF16)
    ext = ex.T
    ltri = (lax.broadcasted_iota(jnp.int32, (CH, CH), 0) >= lax.broadcasted_iota(jnp.int32, (CH, CH), 1)).astype(F32)

    proj, dtraw, xn = _inproj(x, g_mix, wm, wdt, tm, token)
    cat = _gmlp_fwd(proj, gv, ws, bst, gout)
    cat, sall = _ssd_fwd(proj, dtraw, cat, convw, convb, dtb, alog, dskip, ng, ex, ltri, nb)
    wo, w1, w2, wg, wp = comm.rest(cat)
    h1, hn = _outproj(cat, wo, x, g_mlp, tm)
    hid = _ff1(hn, w1, tm)
    hp, dgl, dpe, dh2, dh2b, loss, d_gf, d_gple = _ff2_tail(hid, w2, h1, g_ple, p, tgt, wg, wp, gf, tm // 2)

    d_wp = _matmul_tn(p, dpe, "dw_ple_proj", a_fn=lambda a: a.astype(MXU))
    d_wg = _matmul_tn(hp, dgl, "dw_ple_gate")
    d_w2 = _matmul_tn(hid, dh2b, "dw_ff2", a_fn=_sq)
    dpre = _ff2_bwd(dh2b, w2, hid, tm)
    d_w1 = _matmul_tn(hn, dpre, "dw_ff1")
    token = comm.send("a", {"w_ple_proj": d_wp, "w_ple_gate": d_wg, "w_ff2": d_w2, "w_ff1": d_w1})
    dh1, dh1b, d_gmlp = _ff1_bwd(dpre, w1, dh2, h1, g_mlp, tm, token)
    dcat = _outproj_bwd(dh1b, wo, tm)
    d_wo = _matmul_tn(cat, dh1b, "dw_out")
    duv, d_gv, d_ws, d_bst, d_gout = _gmlp_bwd(proj, dcat, gv, ws, bst, gout)
    token = comm.send("b", {
        "w_out": d_wo, "loss": loss[0:1, 0:1], "final_norm_g": d_gf, "ple_norm_g": d_gple, "norm_mlp_g": d_gmlp,
        "gm_v_norm_g": d_gv, "gm_ws": d_ws, "gm_bs": d_bst[:, :GM_HEADS].T, "gm_out_norm_g": d_gout})
    dssd, ddt, d_cw, d_cb, d_dtb, d_al, d_ds, d_ng = _ssd_bwd(
        proj, dtraw, sall, dcat, convw, convb, dtb, alog, dskip, ng, ex, ltri, ext, nb, token)
    d_win = jnp.concatenate([_matmul_tn(xn, duv, "dw_in_uv"), _matmul_tn(xn, dssd, "dw_in_ssd"),
                             _matmul_tn(xn, ddt, "dw_in_dt")[:, :16]], axis=1)
    token = comm.send("c", {"w_in": d_win})
    dx, d_gmix = _inproj_bwd(duv, dssd, ddt, wm, wdt, dh1, x, g_mix, tm, token)
    comm.send("d", {"norm_mix_g": d_gmix, "ssd_conv_w": d_cw[0:4], "ssd_conv_b": d_cb, "ssd_dt_bias": d_dtb[:, :16],
                    "ssd_a_log": d_al[:, :16], "ssd_d": d_ds[:, :16], "ssd_norm_g": d_ng})
    return dx


def kernel(x, p, norm_mix_g, w_in, gm_v_norm_g, gm_ws, gm_bs, gm_out_norm_g, ssd_conv_w, ssd_conv_b, ssd_dt_bias, ssd_a_log, ssd_d, ssd_norm_g, w_out, norm_mlp_g, w_ff1, w_ff2, ple_norm_g, w_ple_gate, w_ple_proj, final_norm_g, loss_target, m_norm_mix_g, m_w_in, m_gm_v_norm_g, m_gm_ws, m_gm_bs, m_gm_out_norm_g, m_ssd_conv_w, m_ssd_conv_b, m_ssd_dt_bias, m_ssd_a_log, m_ssd_d, m_ssd_norm_g, m_w_out, m_norm_mlp_g, m_w_ff1, m_w_ff2, m_ple_norm_g, m_w_ple_gate, m_w_ple_proj, m_final_norm_g, v_norm_mix_g, v_w_in, v_gm_v_norm_g, v_gm_ws, v_gm_bs, v_gm_out_norm_g, v_ssd_conv_w, v_ssd_conv_b, v_ssd_dt_bias, v_ssd_a_log, v_ssd_d, v_ssd_norm_g, v_w_out, v_norm_mlp_g, v_w_ff1, v_w_ff2, v_ple_norm_g, v_w_ple_gate, v_w_ple_proj, v_final_norm_g):
    a = dict(locals())
    order = ["norm_mix_g", "w_in", "gm_v_norm_g", "gm_ws", "gm_bs", "gm_out_norm_g", "ssd_conv_w", "ssd_conv_b",
             "ssd_dt_bias", "ssd_a_log", "ssd_d", "ssd_norm_g", "w_out", "norm_mlp_g", "w_ff1", "w_ff2", "ple_norm_g",
             "w_ple_gate", "w_ple_proj", "final_norm_g"]
    chip = 2 * lax.axis_index("x") + lax.axis_index("y")
    nb, S = x.shape[0], x.shape[1]
    T = nb * S
    sm = {n: a[n] for n, _ in _SMALL if n != "ssd_conv_w"}
    comm = _Comm(a, jnp.stack([chip, lax.axis_index("c")]).astype(jnp.int32))
    dx = _local_step(x.reshape(T, D), p.reshape(T, DPLE), loss_target.reshape(T, D), sm, comm, nb, 512)
    own, g_out = comm.finish(dx)
    other = _sibling_exchange(own)

    delta, new_m, new_v = {}, {}, {}
    for i, (n, shp, _) in enumerate(_BIG):
        res = _adamw_halves(a[n].reshape(shp), own[i], other[i], a["m_" + n].reshape(shp), a["v_" + n].reshape(shp),
                            "adamw_" + n)
        g_out[n], delta[n], new_m[n], new_v[n] = (r.reshape(a[n].shape) for r in res)
    g_out["ssd_conv_w"] = lax.dynamic_slice(g_out["ssd_conv_w"], (0, 0, chip * 384), (1, 4, 384))
    small_names = [n for n, _ in _SMALL]
    packs = [jnp.concatenate([_rows128(src(n)) for n in small_names], axis=0)
             for src in (lambda n: a[n], lambda n: g_out[n], lambda n: a["m_" + n], lambda n: a["v_" + n])]
    outs = _adamw(*packs, "adamw_small")
    o = 0
    for n in small_names:
        r = _rows128(a[n]).shape[0]
        cnt = a[n].size
        for dst, src in zip((delta, new_m, new_v), outs):
            dst[n] = src[o:o + r].reshape(-1)[:cnt].reshape(a[n].shape)
        o += r
    return (g_out["loss"], dx.reshape(x.shape), *[g_out[n] for n in order], *[delta[n] for n in order],
            *[new_m[n] for n in order], *[new_v[n] for n in order])
```

```python
import jax
import jax.numpy as jnp
from jax import lax
from jax.experimental import pallas as pl
from jax.experimental.pallas import tpu as pltpu

F32 = jnp.float32
BF16 = jnp.bfloat16
MXU = jnp.bfloat16
GRAD = jnp.bfloat16

D = 1024
CH = 128
GM_HEADS = 8
SSD_HEADS = 16
SSD_P = 64
CONV_CH = 1536
N_MAIN = 4608
DFF = 4096
DPLE = 256
EPS = 1e-6
NEG = -1e30

LR, B1, B2, ADAM_EPS, WD, STEP = 0.001, 0.9, 0.999, 1e-08, 0.01, 10

VMEM_LIMIT = 56 * 1024 * 1024
MESH = pl.DeviceIdType.MESH

INV_SQRT2 = 0.7071067811865476
INV_SQRT_2PI = 0.3989422804014327


def _cp(n_axes=1):
    return pltpu.CompilerParams(dimension_semantics=("arbitrary",) * n_axes, vmem_limit_bytes=VMEM_LIMIT)


def _dot(a, b):
    return jnp.dot(a, b, preferred_element_type=F32)


def _dot_nt(a, b):
    return lax.dot_general(a, b, (((1,), (1,)), ((), ())), preferred_element_type=F32)


def _dot_tn(a, b):
    return lax.dot_general(a, b, (((0,), (0,)), ((), ())), preferred_element_type=F32)


def _dot_hi(a, b):
    return jnp.dot(a, b, preferred_element_type=F32, precision=lax.Precision.HIGHEST)


def _dot_01(a, sel):
    hi = a.astype(BF16)
    lo = (a - hi.astype(F32)).astype(BF16)
    n = a.shape[0]
    r = _dot(jnp.concatenate([hi, lo], axis=0), sel)
    return r[0:n] + r[n:2 * n]


def _rows(tm, n, j=0):
    return pl.BlockSpec((tm, n), lambda i: (i, j))


def _const(shape):
    nd = len(shape)
    return pl.BlockSpec(shape, lambda *_: (0,) * nd)


def _sds(shape, dtype):
    return jax.ShapeDtypeStruct(shape, dtype)


def _rms(x):
    r = lax.rsqrt(jnp.mean(x * x, axis=-1, keepdims=True) + EPS)
    return x * r, r


def _rms_bwd(dy, xhat, r, g):
    dyg = dy * g
    return r * (dyg - xhat * jnp.mean(dyg * xhat, axis=-1, keepdims=True))


def _sigmoid(x):
    return 1.0 / (1.0 + jnp.exp(-x))


def _gelu(x):
    cdf = 0.5 * (1.0 + lax.erf(x * INV_SQRT2))
    pdf = jnp.exp(-0.5 * x * x) * INV_SQRT_2PI
    return x * cdf, cdf + x * pdf


def _softplus(x):
    e = jnp.exp(-jnp.abs(x))
    u = 1.0 + e
    log1p = jnp.where(u == 1.0, e, jnp.log(u) * e / (u - 1.0))
    return jnp.maximum(x, 0.0) + log1p


def _after(n_in, fn):
    def body(*refs):
        return fn(*refs[:n_in], *refs[n_in + 1:])

    return body


def _inproj_gmlp(x, g, wm, wdt, gv, ws, bst, gout, tm, after):
    T = x.shape[0]

    def body(x_ref, g_ref, wm_ref, wdt_ref, gv_ref, ws_ref, bst_ref, gout_ref,
             z_ref, xbc_ref, dt_ref, xn_ref, ya_ref, uv_ref):
        xh, _ = _rms(x_ref[...])
        xn = (xh * g_ref[...]).astype(MXU)
        xn_ref[...] = xn
        for n in range(4):
            uv_ref[:, n * 512:(n + 1) * 512] = _dot(xn, wm_ref[:, n * 512:(n + 1) * 512])
        for n in range(2):
            z_ref[:, n * 512:(n + 1) * 512] = _dot(xn, wm_ref[:, 2048 + n * 512:2048 + (n + 1) * 512])
        for n in range(3):
            xbc_ref[:, n * 512:(n + 1) * 512] = _dot(xn, wm_ref[:, 3072 + n * 512:3072 + (n + 1) * 512])
        dt_ref[...] = _dot(xn, wdt_ref[...])
        for k in range(tm // CH):
            rows = slice(k * CH, (k + 1) * CH)
            f = _gmlp_fwd_vals(uv_ref[rows, 0:1024], uv_ref[rows, 1024:2048], gv_ref[...], ws_ref, bst_ref[...],
                               gout_ref[...])
            ya_ref[rows, :] = f["out"].astype(MXU)

    return pl.pallas_call(
        _after(8, body), grid=(T // tm,), name="inproj_gmlp",
        in_specs=[_rows(tm, D), _const((1, D)), _const((D, N_MAIN)), _const((D, 128)), _const((1, 1024)),
                  _const((GM_HEADS, CH, CH)), _const((CH, 128)), _const((1, 1024)), _ANY],
        out_specs=[_rows(tm, 1024), _rows(tm, CONV_CH), _rows(tm, 128), _rows(tm, D), _rows(tm, 1024, 0),
                   _rows(tm, 2048)],
        out_shape=[_sds((T, 1024), F32), _sds((T, CONV_CH), F32), _sds((T, 128), F32), _sds((T, D), MXU),
                   _sds((T, 2048), MXU), _sds((T, 2048), F32)],
        compiler_params=_cp(),
    )(x, g, wm, wdt, gv, ws, bst, gout, after)


def _gmlp_fwd_vals(u, v, gv, ws_ref, bst, gout):
    ug, dug = _gelu(u)
    vg, dvg = _gelu(v)
    row = lax.broadcasted_iota(jnp.int32, (CH, CH), 0)
    col = lax.broadcasted_iota(jnp.int32, (CH, CH), 1)
    tril = row >= col
    ys, heads = [], []
    for h in range(GM_HEADS):
        sl = slice(h * 128, (h + 1) * 128)
        vhat, rv = _rms(vg[:, sl])
        vn = (vhat * gv[:, sl]).astype(MXU)
        wt = jnp.where(tril, ws_ref[h], 0.0)
        mixed = _dot(wt.astype(MXU), vn) + bst[:, h:h + 1]
        ys.append(ug[:, sl] * mixed)
        heads.append((vhat, rv, vn, wt, mixed))
    y = jnp.concatenate(ys, axis=1)
    yhat, ry = _rms(y)
    return dict(ug=ug, dug=dug, dvg=dvg, heads=heads, yhat=yhat, ry=ry, tril=tril, out=yhat * gout)


def _shifts_down(cur, halo):
    row8 = lax.broadcasted_iota(jnp.int32, (8, cur.shape[1]), 0)
    out = [cur]
    for j in (1, 2, 3):
        sh = pltpu.roll(cur, j, 0)
        top = jnp.where(row8 < j, pltpu.roll(halo, j, 0), sh[0:8])
        out.append(jnp.concatenate([top, sh[8:]], axis=0))
    return out


def _shifts_up(cur, halo):
    row8 = lax.broadcasted_iota(jnp.int32, (8, cur.shape[1]), 0)
    out = []
    for j in (1, 2, 3):
        sh = pltpu.roll(cur, CH - j, 0)
        bot = jnp.where(row8 + j >= 8, pltpu.roll(halo, 8 - j, 0), sh[CH - 8:CH])
        out.append(jnp.concatenate([sh[0:CH - 8], bot], axis=0))
    return out


def _ssd_fwd_vals(z, xbc, halo, dtraw, convw, convb, dtb, alog, dskip, ng, ex, ltri, s_prev):
    shifts = _shifts_down(xbc, halo)
    conv = convb + convw[3:4] * shifts[0] + convw[2:3] * shifts[1] + convw[1:2] * shifts[2] + convw[0:1] * shifts[3]
    sig_c = _sigmoid(conv)
    xa = conv * sig_c
    xs = xa[:, :1024]
    bm = [xa[:, 1024:1152], xa[:, 1152:1280]]
    cm = [xa[:, 1280:1408], xa[:, 1408:1536]]
    dtpre = dtraw + dtb
    dt = _softplus(dtpre)
    a_neg = -jnp.exp(alog)
    cs = _dot_hi(ltri, dt * a_neg)
    cst = cs.T
    last = cs[CH - 1:CH]
    ecs = jnp.exp(cs)
    dec = jnp.exp(last - cs)
    spread = _dot_01(jnp.concatenate([dt, ecs, dec], axis=0), ex)
    dte, ecse, dece = spread[0:CH], spread[CH:2 * CH], spread[2 * CH:3 * CH]
    cde = ecse[CH - 1:CH]
    de = dskip
    xdt = xs * dte
    row = lax.broadcasted_iota(jnp.int32, (CH, CH), 0)
    col = lax.broadcasted_iota(jnp.int32, (CH, CH), 1)
    tril = row >= col
    lo = col < SSD_P
    bmb = [b.astype(MXU) for b in bm]
    cmb = [c.astype(MXU) for c in cm]
    mg = [_dot_nt(cmb[g], bmb[g]) for g in range(2)]
    yd, lms, whs = [], [], []
    for q in range(8):
        g = q // 4
        xq = xdt[:, q * 128:(q + 1) * 128]
        acc = None
        for hh in range(2):
            h = 2 * q + hh
            seg = cs[:, h:h + 1] - cst[h:h + 1, :]
            lm = jnp.exp(jnp.where(tril, seg, NEG))
            wh = (mg[g] * lm).astype(MXU)
            xm = jnp.where(lo if hh == 0 else ~lo, xq, 0.0).astype(MXU)
            part = _dot(wh, xm)
            acc = part if acc is None else acc + part
            lms.append(lm)
            whs.append(wh)
        yd.append(acc)
    yd = jnp.concatenate(yd, axis=1)
    sb = s_prev.astype(MXU)
    yo = jnp.concatenate([_dot(cmb[g], sb[:, g * 512:(g + 1) * 512]) for g in range(2)], axis=1) * ecse
    xdec = (xdt * dece).astype(MXU)
    states = jnp.concatenate([_dot_tn(bmb[g], xdec[:, g * 512:(g + 1) * 512]) for g in range(2)], axis=1)
    s_next = s_prev * cde + states
    ypre = yd + yo + de * xs
    sig_z = _sigmoid(z)
    yg = ypre * z * sig_z
    outs, yhat, rr = [], [], []
    for g in range(2):
        sl = slice(g * 512, (g + 1) * 512)
        yh, r = _rms(yg[:, sl])
        yhat.append(yh)
        rr.append(r)
        outs.append(yh * ng[:, sl])
    return dict(shifts=shifts, conv=conv, sig_c=sig_c, xs=xs, bmb=bmb, cmb=cmb, dtpre=dtpre, dt=dt, a_neg=a_neg,
                cs=cs, last=last, ecs=ecs, dec=dec, dte=dte, ecse=ecse, dece=dece, cde=cde, de=de, xdt=xdt,
                mg=mg, lms=lms, whs=whs, lo=lo, yo=yo, sb=sb, xdec=xdec, s_next=s_next, ypre=ypre, sig_z=sig_z,
                yhat=yhat, rr=rr, out=jnp.concatenate(outs, axis=1))


def _ssd_specs(nch, rev):
    def tok(b, c):
        return b * nch + ((nch - 1 - c) if rev else c)

    return tok, [
        pl.BlockSpec((CH, 1024), lambda b, c: (tok(b, c), 0)),
        pl.BlockSpec((CH, CONV_CH), lambda b, c: (tok(b, c), 0)),
        pl.BlockSpec((8, CONV_CH), lambda b, c: (jnp.maximum(tok(b, c) * (CH // 8) - 1, 0), 0)),
        pl.BlockSpec((CH, 128), lambda b, c: (tok(b, c), 0)),
        _const((8, CONV_CH)), _const((1, CONV_CH)), _const((1, 128)), _const((1, 128)), _const((1, 1024)),
        _const((1, 1024)), _const((128, 1024)), _const((CH, CH)),
    ]


def _ssd_fwd(pz, pxbc, dtraw, cat, convw, convb, dtb, alog, dskip, ng, ex, ltri, nb):
    T = pz.shape[0]
    nch = T // CH // nb
    tok, in_specs = _ssd_specs(nch, rev=False)

    def body(z_ref, xbc_ref, halo_ref, dt_ref, cw_ref, cb_ref, dtb_ref, al_ref, ds_ref, ng_ref, ex_ref, lt_ref,
             cat_in_ref, yb_ref, sall_ref, s_ref):
        del cat_in_ref
        c = pl.program_id(1)

        @pl.when(c == 0)
        def _():
            s_ref[...] = jnp.zeros_like(s_ref)

        halo = jnp.where(c == 0, 0.0, halo_ref[...])
        s_prev = s_ref[...]
        sall_ref[0] = s_prev
        f = _ssd_fwd_vals(z_ref[...], xbc_ref[...], halo, dt_ref[...], cw_ref[...], cb_ref[...], dtb_ref[...],
                          al_ref[...], ds_ref[...], ng_ref[...], ex_ref[...], lt_ref[...], s_prev)
        s_ref[...] = f["s_next"]
        yb_ref[...] = f["out"].astype(MXU)

    return pl.pallas_call(
        body, grid=(nb, nch), name="ssd_fwd",
        in_specs=in_specs + [_ANY],
        out_specs=[pl.BlockSpec((CH, 1024), lambda b, c: (tok(b, c), 1)),
                   pl.BlockSpec((1, 128, 1024), lambda b, c: (tok(b, c), 0, 0))],
        out_shape=[_sds((T, 2048), MXU), _sds((T // CH, 128, 1024), F32)],
        scratch_shapes=[pltpu.VMEM((128, 1024), F32)],
        input_output_aliases={12: 0},
        compiler_params=_cp(2),
    )(pz, pxbc, pxbc, dtraw, convw, convb, dtb, alog, dskip, ng, ex, ltri, cat)


def _outproj(cat, wo, x, g, tm):
    T = x.shape[0]

    def body(cat_ref, wo_ref, x_ref, g_ref, h1_ref, hn_ref):
        h1 = x_ref[...] + _dot(cat_ref[...], wo_ref[...])
        h1_ref[...] = h1
        hn_ref[...] = (_rms(h1)[0] * g_ref[...]).astype(MXU)

    return pl.pallas_call(
        body, grid=(T // tm,), name="outproj",
        in_specs=[_rows(tm, 2048), _const((2048, D)), _rows(tm, D), _const((1, D))],
        out_specs=[_rows(tm, D), _rows(tm, D)],
        out_shape=[_sds((T, D), F32), _sds((T, D), MXU)],
        compiler_params=_cp(),
    )(cat, wo, x, g)


def _ff1(hn, w1, tm):
    T = hn.shape[0]

    def body(hn_ref, w1_ref, hid_ref):
        hn_v = hn_ref[...]
        for n in range(4):
            hid_ref[:, n * 1024:(n + 1) * 1024] = jnp.maximum(_dot(hn_v, w1_ref[n]), 0.0).astype(MXU)

    return pl.pallas_call(
        body, grid=(T // tm,), name="ff1",
        in_specs=[_rows(tm, D), _const((4, D, 1024))],
        out_specs=_rows(tm, DFF),
        out_shape=_sds((T, DFF), MXU),
        compiler_params=_cp(),
    )(hn, w1)


def _sq(hid):
    h = hid.astype(F32)
    return (h * h).astype(MXU)


def _ff2_tail(hid, w2, h1, g_ple, p, tgt, wg, wp, gf, tm):
    T = h1.shape[0]

    def body(hid_ref, w2_ref, h1_ref, g_ref, p_ref, t_ref, wg_ref, wp_ref, gf_ref,
             hp_ref, dgl_ref, dpe_ref, dh2_ref, dh2b_ref, loss_ref, dgf_ref, dg_ref):
        @pl.when(pl.program_id(0) == 0)
        def _():
            loss_ref[...] = jnp.zeros_like(loss_ref)
            dgf_ref[...] = jnp.zeros_like(dgf_ref)
            dg_ref[...] = jnp.zeros_like(dg_ref)

        h2 = h1_ref[...] + _dot(_sq(hid_ref[...]), w2_ref[...])
        h2h, r2 = _rms(h2)
        g_ple = g_ref[...]
        hp = (h2h * g_ple).astype(MXU)
        hp_ref[...] = hp
        gate = _sigmoid(_dot(hp, wg_ref[...]))
        pb = p_ref[...].astype(MXU)
        pe = jnp.concatenate([_dot(pb, wp_ref[k]) for k in range(4)], axis=1)
        h3 = h2 + gate * pe
        hh, r = _rms(h3)
        gf = gf_ref[...]
        diff = hh * gf - t_ref[...]
        loss_ref[...] += 0.5 * jnp.sum(jnp.mean(diff * diff, axis=-1, keepdims=True))
        dout = diff * (1.0 / D)
        dgf_ref[...] += jnp.sum(dout * hh, axis=0, keepdims=True)
        dh3 = _rms_bwd(dout, hh, r, gf)
        dgl = (dh3 * pe * gate * (1.0 - gate)).astype(MXU)
        dgl_ref[...] = dgl
        dpe_ref[...] = (dh3 * gate).astype(MXU)
        dhp = _dot_nt(dgl, wg_ref[...])
        dg_ref[...] += jnp.sum(dhp * h2h, axis=0, keepdims=True)
        dh2 = dh3 + _rms_bwd(dhp, h2h, r2, g_ple)
        dh2_ref[...] = dh2
        dh2b_ref[...] = dh2.astype(MXU)

    return pl.pallas_call(
        body, grid=(T // tm,), name="ff2_tail",
        in_specs=[_rows(tm, DFF), _const((DFF, D)), _rows(tm, D), _const((1, D)), _rows(tm, DPLE), _rows(tm, D),
                  _const((D, D)), _const((4, DPLE, 256)), _const((1, D))],
        out_specs=[_rows(tm, D), _rows(tm, D), _rows(tm, D), _rows(tm, D), _rows(tm, D), _const((8, 128)),
                   _const((1, D)), _const((1, D))],
        out_shape=[_sds((T, D), MXU), _sds((T, D), MXU), _sds((T, D), MXU), _sds((T, D), F32), _sds((T, D), MXU),
                   _sds((8, 128), F32), _sds((1, D), F32), _sds((1, D), F32)],
        compiler_params=_cp(),
    )(hid, w2, h1, g_ple, p, tgt, wg, wp, gf)


def _ff2_bwd(dh2b, w2, hid, tm):
    T = hid.shape[0]

    def body(dh2b_ref, w2_ref, hid_ref, dpre_ref):
        d = dh2b_ref[...]
        for n in range(DFF // 1024):
            sl = slice(n * 1024, (n + 1) * 1024)
            da = _dot_nt(d, w2_ref[sl, :])
            dpre_ref[:, sl] = (2.0 * da * hid_ref[:, sl].astype(F32)).astype(MXU)

    return pl.pallas_call(
        body, grid=(T // tm,), name="ff2_bwd",
        in_specs=[_rows(tm, D), _const((DFF, D)), _rows(tm, DFF)],
        out_specs=_rows(tm, DFF),
        out_shape=_sds((T, DFF), MXU),
        compiler_params=_cp(),
    )(dh2b, w2, hid)


def _ff1_bwd(dpre, w1, dh2, h1, g, tm, after):
    T = h1.shape[0]

    def body(dpre_ref, w1_ref, dh2_ref, h1_ref, g_ref, dh1_ref, dh1b_ref, dg_ref):
        @pl.when(pl.program_id(0) == 0)
        def _():
            dg_ref[...] = jnp.zeros_like(dg_ref)

        dhn = _dot_nt(dpre_ref[:, 0:1024], w1_ref[0])
        for k in range(1, 4):
            dhn = dhn + _dot_nt(dpre_ref[:, k * 1024:(k + 1) * 1024], w1_ref[k])
        hh, r = _rms(h1_ref[...])
        dg_ref[...] += jnp.sum(dhn * hh, axis=0, keepdims=True)
        dh1 = dh2_ref[...] + _rms_bwd(dhn, hh, r, g_ref[...])
        dh1_ref[...] = dh1
        dh1b_ref[...] = dh1.astype(MXU)

    return pl.pallas_call(
        _after(5, body), grid=(T // tm,), name="ff1_bwd",
        in_specs=[_rows(tm, DFF), _const((4, D, 1024)), _rows(tm, D), _rows(tm, D), _const((1, D)), _ANY],
        out_specs=[_rows(tm, D), _rows(tm, D), _const((1, D))],
        out_shape=[_sds((T, D), F32), _sds((T, D), MXU), _sds((1, D), F32)],
        compiler_params=_cp(),
    )(dpre, w1, dh2, h1, g, after)


def _outproj_bwd(dh1b, wo, tm):
    T = dh1b.shape[0]

    def body(d_ref, wo_ref, dcat_ref):
        d = d_ref[...]
        dcat_ref[:, 0:1024] = _dot_nt(d, wo_ref[0:1024, :])
        dcat_ref[:, 1024:2048] = _dot_nt(d, wo_ref[1024:2048, :])

    return pl.pallas_call(
        body, grid=(T // tm,), name="outproj_bwd",
        in_specs=[_rows(tm, D), _const((2048, D))],
        out_specs=_rows(tm, 2048),
        out_shape=_sds((T, 2048), F32),
        compiler_params=_cp(),
    )(dh1b, wo)


def _gmlp_bwd(uv, dcat, gv, ws, bst, gout):
    T = uv.shape[0]
    nck = 2 if T % (2 * CH) == 0 else 1
    tb = nck * CH

    def body(uv_ref, dya_ref, gv_ref, ws_ref, bst_ref, gout_ref, duv_ref, dgv_ref, dws_ref, dbst_ref, dgo_ref):
        @pl.when(pl.program_id(0) == 0)
        def _():
            dgv_ref[...] = jnp.zeros_like(dgv_ref)
            dws_ref[...] = jnp.zeros_like(dws_ref)
            dbst_ref[...] = jnp.zeros_like(dbst_ref)
            dgo_ref[...] = jnp.zeros_like(dgo_ref)

        for k in range(nck):
            chunk(slice(k * CH, (k + 1) * CH), uv_ref, dya_ref, gv_ref, ws_ref, bst_ref, gout_ref, duv_ref,
                  dgv_ref, dws_ref, dbst_ref, dgo_ref)

    def chunk(rows, uv_ref, dya_ref, gv_ref, ws_ref, bst_ref, gout_ref, duv_ref, dgv_ref, dws_ref, dbst_ref,
              dgo_ref):
        gv = gv_ref[...]
        f = _gmlp_fwd_vals(uv_ref[rows, 0:1024], uv_ref[rows, 1024:2048], gv, ws_ref, bst_ref[...], gout_ref[...])
        dya = dya_ref[rows, :]
        dgo_ref[...] += jnp.sum(dya * f["yhat"], axis=0, keepdims=True)
        dy = _rms_bwd(dya, f["yhat"], f["ry"], gout_ref[...])
        lane = lax.broadcasted_iota(jnp.int32, (CH, 128), 1)
        dbs = jnp.zeros((CH, 128), F32)
        dug, dvg, dgvs = [], [], []
        for h in range(GM_HEADS):
            sl = slice(h * 128, (h + 1) * 128)
            vhat, rv, vn, wt, mixed = f["heads"][h]
            dyh = dy[:, sl]
            dug.append(dyh * mixed)
            dmixed = dyh * f["ug"][:, sl]
            dmb = dmixed.astype(MXU)
            dws_ref[h] += jnp.where(f["tril"], _dot_nt(dmb, vn), 0.0)
            dbs = dbs + jnp.where(lane == h, jnp.sum(dmixed, axis=1, keepdims=True), 0.0)
            dvn = _dot_tn(wt.astype(MXU), dmb)
            dgvs.append(jnp.sum(dvn * vhat, axis=0, keepdims=True))
            dvg.append(_rms_bwd(dvn, vhat, rv, gv[:, sl]))
        dbst_ref[...] += dbs
        dgv_ref[...] += jnp.concatenate(dgvs, axis=1)
        duv_ref[rows, 0:1024] = (jnp.concatenate(dug, axis=1) * f["dug"]).astype(MXU)
        duv_ref[rows, 1024:2048] = (jnp.concatenate(dvg, axis=1) * f["dvg"]).astype(MXU)

    return pl.pallas_call(
        body, grid=(T // tb,), name="gmlp_bwd",
        in_specs=[_rows(tb, 2048), _rows(tb, 1024, 0), _const((1, 1024)),
                  _const((GM_HEADS, CH, CH)), _const((CH, 128)), _const((1, 1024))],
        out_specs=[_rows(tb, 2048), _const((1, 1024)), _const((GM_HEADS, CH, CH)), _const((CH, 128)),
                   _const((1, 1024))],
        out_shape=[_sds((T, 2048), MXU), _sds((1, 1024), F32), _sds((GM_HEADS, CH, CH), F32), _sds((CH, 128), F32),
                   _sds((1, 1024), F32)],
        compiler_params=_cp(),
    )(uv, dcat, gv, ws, bst, gout)


def _ssd_bwd(pz, pxbc, dtraw, sall, dcat, convw, convb, dtb, alog, dskip, ng, ex, ltri, ext, nb, after):
    T = pz.shape[0]
    nch = T // CH // nb
    tok, in_specs = _ssd_specs(nch, rev=True)
    in_specs = in_specs + [
        _const((1024, 128)),
        pl.BlockSpec((1, 128, 1024), lambda b, c: (tok(b, c), 0, 0)),
        pl.BlockSpec((CH, 1024), lambda b, c: (tok(b, c), 1)),
        _ANY,
    ]

    def body(z_ref, xbc_ref, halo_ref, dt_ref, cw_ref, cb_ref, dtb_ref, al_ref, ds_ref, ng_ref, ex_ref, lt_ref,
             ext_ref, sall_ref, dyb_ref,
             dssd_ref, ddt_ref, dcw_ref, dcb_ref, ddtb_ref, dal_ref, dds_ref, dng_ref,
             dst_ref, dnext_ref, ddse_ref):
        b = pl.program_id(0)
        c = pl.program_id(1)

        @pl.when((b == 0) & (c == 0))
        def _():
            for r in (dcw_ref, dcb_ref, ddtb_ref, dal_ref, dds_ref, dng_ref, ddse_ref):
                r[...] = jnp.zeros_like(r)

        @pl.when(c == 0)
        def _():
            dst_ref[...] = jnp.zeros_like(dst_ref)
            dnext_ref[...] = jnp.zeros_like(dnext_ref)

        first_chunk = c == nch - 1
        halo = jnp.where(first_chunk, 0.0, halo_ref[...])
        z = z_ref[...]
        ex = ex_ref[...]
        ext = ext_ref[...]
        cw = cw_ref[...]
        ng = ng_ref[...]
        s_prev = sall_ref[0]
        f = _ssd_fwd_vals(z, xbc_ref[...], halo, dt_ref[...], cw, cb_ref[...], dtb_ref[...], al_ref[...],
                          ds_ref[...], ng, ex, lt_ref[...], s_prev)
        xs, xdt, cs, dec, dt = f["xs"], f["xdt"], f["cs"], f["dec"], f["dt"]
        dyb = dyb_ref[...]
        dyg, dngs = [], []
        for g in range(2):
            sl = slice(g * 512, (g + 1) * 512)
            dngs.append(jnp.sum(dyb[:, sl] * f["yhat"][g], axis=0, keepdims=True))
            dyg.append(_rms_bwd(dyb[:, sl], f["yhat"][g], f["rr"][g], ng[:, sl]))
        dng_ref[...] += jnp.concatenate(dngs, axis=1)
        dyg = jnp.concatenate(dyg, axis=1)
        sig_z = f["sig_z"]
        silu_z = z * sig_z
        dy = dyg * silu_z
        dz = dyg * f["ypre"] * sig_z * (1.0 + z * (1.0 - sig_z))
        ddse_ref[...] += jnp.sum(dy * xs, axis=0, keepdims=True)

        @pl.when((b == nb - 1) & (c == nch - 1))
        def _():
            dds_ref[...] = _dot_01(jnp.broadcast_to(ddse_ref[...], (8, 1024)), ext)[0:1]

        dxs = dy * f["de"]
        dye = dy * f["ecse"]
        dyeb = dye.astype(MXU)
        dst = dst_ref[...]
        dstb = dst.astype(MXU)
        bmb, cmb, sb, xdec = f["bmb"], f["cmb"], f["sb"], f["xdec"]
        u = jnp.concatenate([_dot(bmb[g], dstb[:, g * 512:(g + 1) * 512]) for g in range(2)], axis=1)
        dxdt = [u[:, q * 128:(q + 1) * 128] * f["dece"][:, q * 128:(q + 1) * 128] for q in range(8)]
        per_head = _dot_01(jnp.concatenate(
            [dy * f["yo"], u * xdt, jnp.broadcast_to(jnp.sum(dst * s_prev, axis=0, keepdims=True), (8, 1024))],
            axis=0), ext)
        dcs = per_head[0:CH]
        t = per_head[CH:2 * CH] * dec
        dcd = per_head[2 * CH:2 * CH + 1]
        row = lax.broadcasted_iota(jnp.int32, (CH, 128), 0)
        lane = lax.broadcasted_iota(jnp.int32, (CH, 128), 1)
        cd = jnp.exp(f["last"])
        dcs = dcs - t + jnp.where(row == CH - 1, jnp.sum(t, axis=0, keepdims=True) + dcd * cd, 0.0)
        dcst = jnp.zeros((128, CH), F32)
        lo = f["lo"]
        dbm, dcm, ds_prev = [], [], []
        for g in range(2):
            sl = slice(g * 512, (g + 1) * 512)
            dmg = jnp.zeros((CH, CH), F32)
            for q in range(4 * g, 4 * g + 4):
                dyq = dy[:, q * 128:(q + 1) * 128]
                xq = xdt[:, q * 128:(q + 1) * 128].astype(MXU)
                for hh in range(2):
                    h = 2 * q + hh
                    m = lo if hh == 0 else ~lo
                    dym = jnp.where(m, dyq, 0.0).astype(MXU)
                    gh = _dot_nt(dym, xq)
                    gl = gh * f["lms"][h]
                    dmg = dmg + gl
                    qh = gl * f["mg"][g]
                    dcs = dcs + jnp.where(lane == h, jnp.sum(qh, axis=1, keepdims=True), 0.0)
                    dcst = dcst - jnp.where(row == h, jnp.sum(qh, axis=0, keepdims=True), 0.0)
                    dxdt[q] = dxdt[q] + _dot_tn(f["whs"][h], dym)
            dmgb = dmg.astype(MXU)
            dcm.append(_dot(dmgb, bmb[g]) + _dot_nt(dyeb[:, sl], sb[:, sl]))
            dbm.append(_dot_tn(dmgb, cmb[g]) + _dot_nt(xdec[:, sl], dstb[:, sl]))
            ds_prev.append(_dot_tn(cmb[g], dyeb[:, sl]))
        dst_ref[...] = jnp.concatenate(ds_prev, axis=1) + dst * f["cde"]
        dcs = dcs + dcst.T
        da = _dot_hi(lt_ref[...].T, dcs)
        dxdt = jnp.concatenate(dxdt, axis=1)
        a_neg = f["a_neg"]
        ddt = da * a_neg + _dot_01(dxdt * xs, ext)
        dal_ref[...] += jnp.sum(da * dt, axis=0, keepdims=True) * a_neg
        dxs = dxs + dxdt * f["dte"]
        ddtraw = jnp.where(lane < SSD_HEADS, ddt * _sigmoid(f["dtpre"]), 0.0)
        ddtb_ref[...] += jnp.sum(ddtraw, axis=0, keepdims=True)
        ddt_ref[...] = ddtraw.astype(MXU)
        dxa = jnp.concatenate([dxs, dbm[0], dbm[1], dcm[0], dcm[1]], axis=1)
        sig_c = f["sig_c"]
        dconv = dxa * sig_c * (1.0 + f["conv"] * (1.0 - sig_c))
        dcb_ref[...] += jnp.sum(dconv, axis=0, keepdims=True)
        for k in range(4):
            dcw_ref[k:k + 1, :] += jnp.sum(dconv * f["shifts"][3 - k], axis=0, keepdims=True)
        dxbc = cw[3:4] * dconv
        for j, up in zip((1, 2, 3), _shifts_up(dconv, dnext_ref[...])):
            dxbc = dxbc + cw[3 - j:4 - j] * up
        dnext_ref[...] = dconv[0:8]
        dssd_ref[:, 0:1024] = dz.astype(MXU)
        dssd_ref[:, 1024:2560] = dxbc.astype(MXU)

    return pl.pallas_call(
        _after(15, body), grid=(nb, nch), name="ssd_bwd",
        in_specs=in_specs,
        out_specs=[pl.BlockSpec((CH, 2560), lambda b, c: (tok(b, c), 0)),
                   pl.BlockSpec((CH, 128), lambda b, c: (tok(b, c), 0)),
                   _const((8, CONV_CH)), _const((1, CONV_CH)), _const((1, 128)), _const((1, 128)), _const((1, 128)),
                   _const((1, 1024))],
        out_shape=[_sds((T, 2560), MXU), _sds((T, 128), MXU), _sds((8, CONV_CH), F32), _sds((1, CONV_CH), F32),
                   _sds((1, 128), F32), _sds((1, 128), F32), _sds((1, 128), F32), _sds((1, 1024), F32)],
        scratch_shapes=[pltpu.VMEM((128, 1024), F32), pltpu.VMEM((8, CONV_CH), F32), pltpu.VMEM((1, 1024), F32)],
        compiler_params=_cp(2),
    )(pz, pxbc, pxbc, dtraw, convw, convb, dtb, alog, dskip, ng, ex, ltri, ext, sall, dcat, after)


def _inproj_bwd(duv, dssd, ddt, wm, wdt, dh1, x, g, tm, after):
    T = x.shape[0]

    def body(duv_ref, dssd_ref, ddt_ref, wm_ref, wdt_ref, dh1_ref, x_ref, g_ref, dx_ref, dg_ref):
        @pl.when(pl.program_id(0) == 0)
        def _():
            dg_ref[...] = jnp.zeros_like(dg_ref)

        dxn = (_dot_nt(duv_ref[...], wm_ref[:, 0:2048]) + _dot_nt(dssd_ref[...], wm_ref[:, 2048:N_MAIN])
               + _dot_nt(ddt_ref[...], wdt_ref[...]))
        xh, r = _rms(x_ref[...])
        dg_ref[...] += jnp.sum(dxn * xh, axis=0, keepdims=True)
        dx_ref[...] = dh1_ref[...] + _rms_bwd(dxn, xh, r, g_ref[...])

    return pl.pallas_call(
        _after(8, body), grid=(T // tm,), name="inproj_bwd",
        in_specs=[_rows(tm, 2048), _rows(tm, 2560), _rows(tm, 128), _const((D, N_MAIN)), _const((D, 128)),
                  _rows(tm, D), _rows(tm, D), _const((1, D)), _ANY],
        out_specs=[_rows(tm, D), _const((1, D))],
        out_shape=[_sds((T, D), F32), _sds((1, D), F32)],
        compiler_params=_cp(),
    )(duv, dssd, ddt, wm, wdt, dh1, x, g, after)


def _matmul_tn(a, b, name, a_fn=None):
    T, M = a.shape
    N = b.shape[1]
    tm = min(M, 1024)
    tn = 1280 if N == 2560 else min(N, 1024)
    tk = min(T, 2048)

    def body(a_ref, b_ref, o_ref, acc_ref):
        k = pl.program_id(2)

        @pl.when(k == 0)
        def _():
            acc_ref[...] = jnp.zeros_like(acc_ref)

        av = a_ref[...]
        if a_fn is not None:
            av = a_fn(av)
        acc_ref[...] += _dot_tn(av, b_ref[...])

        @pl.when(k == T // tk - 1)
        def _():
            o_ref[...] = acc_ref[...].astype(o_ref.dtype)

    return pl.pallas_call(
        body, grid=(M // tm, N // tn, T // tk), name=name,
        in_specs=[pl.BlockSpec((tk, tm), lambda i, j, k: (k, i)), pl.BlockSpec((tk, tn), lambda i, j, k: (k, j))],
        out_specs=pl.BlockSpec((tm, tn), lambda i, j, k: (i, j)),
        out_shape=_sds((M, N), GRAD),
        scratch_shapes=[pltpu.VMEM((tm, tn), F32)],
        compiler_params=_cp(3),
    )(a, b)


def _adamw_vals(w, g, m, v):
    m = B1 * m + (1.0 - B1) * g
    v = B2 * v + (1.0 - B2) * (g * g)
    m_hat = m / (1.0 - B1 ** STEP)
    v_hat = v / (1.0 - B2 ** STEP)
    return -LR * (m_hat / (jnp.sqrt(v_hat) + ADAM_EPS) + WD * w), m, v


def _adamw(w, g, m, v, name):
    R, C = w.shape
    tr = 256 if R % 256 == 0 else R

    def body(w_ref, g_ref, m_ref, v_ref, d_ref, mo_ref, vo_ref):
        d_ref[...], mo_ref[...], vo_ref[...] = _adamw_vals(w_ref[...], g_ref[...], m_ref[...], v_ref[...])

    spec = _rows(tr, C)
    return pl.pallas_call(
        body, grid=(R // tr,), name=name,
        in_specs=[spec] * 4, out_specs=[spec] * 3, out_shape=[_sds((R, C), F32)] * 3,
        compiler_params=_cp(),
    )(w, g, m, v)


def _adamw_halves(w, own, other, m, v, name):
    R, C = w.shape
    half = R // 2
    tr = min(half, 256)
    nth = half // tr

    def body(w_ref, own_ref, oth_ref, m_ref, v_ref, g_ref, d_ref, mo_ref, vo_ref):
        mine = (pl.program_id(0) // nth) == lax.axis_index("c")
        g = jnp.where(mine, own_ref[...], oth_ref[...])
        g_ref[...] = g
        d_ref[...], mo_ref[...], vo_ref[...] = _adamw_vals(w_ref[...], g, m_ref[...], v_ref[...])

    full = _rows(tr, C)
    part = pl.BlockSpec((tr, C), lambda i: (i % nth, 0))
    return pl.pallas_call(
        body, grid=(R // tr,), name=name,
        in_specs=[full, part, part, full, full], out_specs=[full] * 4, out_shape=[_sds((R, C), F32)] * 4,
        compiler_params=_cp(),
    )(w, own, other, m, v)


def _sum_small(slots, name):
    nd, rows, C = slots.shape

    def body(s_ref, o_ref):
        acc = s_ref[0]
        for d in range(1, nd):
            acc = acc + s_ref[d]
        o_ref[...] = acc

    return pl.pallas_call(
        body, grid=(1,), name=name,
        in_specs=[_const((nd, rows, C))], out_specs=_const((rows, C)), out_shape=_sds((rows, C), F32),
        compiler_params=_cp(),
    )(slots)


def _sum_slots(slots, src, kind, shp, kh, name):
    R, C = shp
    rh = R // 2
    tr = min(rh, 256)
    nth = rh // tr
    if kind == "slab":
        src_spec = pl.BlockSpec((1, tr, C), lambda i, kh: (kh[0], kh[1] * nth + i, 0))
    elif kind == "rows":
        src_spec = pl.BlockSpec((tr, C), lambda i, kh: (kh[0] * (R // tr) + kh[1] * nth + i, 0))
    else:
        src_spec = pl.BlockSpec((tr, C), lambda i, kh: (kh[1] * nth + i, kh[0]))

    def body(kh_ref, s_ref, own_ref, o_ref):
        me = 2 * kh_ref[0] + kh_ref[1]
        acc = (own_ref[0] if kind == "slab" else own_ref[...]).astype(F32)
        for k in range(1, 8):
            acc = acc + s_ref[me ^ k].astype(F32)
        o_ref[...] = acc

    return pl.pallas_call(
        body, name=name,
        grid_spec=pltpu.PrefetchScalarGridSpec(
            num_scalar_prefetch=1, grid=(nth,),
            in_specs=[pl.BlockSpec((8, tr, C), lambda i, kh: (0, i, 0)), src_spec],
            out_specs=pl.BlockSpec((tr, C), lambda i, kh: (i, 0))),
        out_shape=_sds((rh, C), F32),
        compiler_params=_cp(),
    )(kh, slots, src)


def _cast_into_slot(w, kh, name):
    R, C = w.shape
    tr = 256

    def body(kh_ref, w_ref, o_ref):
        o_ref[0] = w_ref[...].astype(BF16)

    return pl.pallas_call(
        body, name=name,
        grid_spec=pltpu.PrefetchScalarGridSpec(
            num_scalar_prefetch=1, grid=(R // tr,),
            in_specs=[pl.BlockSpec((tr, C), lambda i, kh: (i, 0))],
            out_specs=pl.BlockSpec((1, tr, C), lambda i, kh: (kh[0], i, 0))),
        out_shape=_sds((4, R, C), BF16),
        compiler_params=_cp(),
    )(kh, w)


_ANY = pl.BlockSpec(memory_space=pl.ANY)
_CHIP_FLIPS = [(1, 0), (0, 1), (1, 1)]
_DEVICE_FLIPS = [(fx, fy, fc) for fx in (0, 1) for fy in (0, 1) for fc in (0, 1)][1:]


def _half(h, rows):
    return pl.ds(pl.multiple_of(h * rows, rows), rows)


def _remote(src, dst, ssem, rsem, to):
    return pltpu.make_async_remote_copy(src_ref=src, dst_ref=dst, send_sem=ssem, recv_sem=rsem,
                                        device_id=to, device_id_type=MESH)


def _weight_gather(bufs, conv):
    n = len(bufs)

    def body(*refs):
        conv_ref, outs, conv_out = refs[n], refs[n + 1:2 * n + 1], refs[2 * n + 1]
        send_sems, recv_sems, fsend_sems, frecv_sems, csend_sems, crecv_sems, local_sem = refs[2 * n + 2:]
        x, y, c = lax.axis_index("x"), lax.axis_index("y"), lax.axis_index("c")
        me = 2 * x + y
        halves = [_half(c, r.shape[1] // 2) for r in outs]
        others = [_half(1 - c, r.shape[1] // 2) for r in outs]
        remote = _remote
        local = [pltpu.make_async_copy(conv_ref, conv_out.at[me], local_sem)]
        for cp in local:
            cp.start()
        sends = []
        for k, (fx, fy) in enumerate(_CHIP_FLIPS):
            peer = (x ^ fx, y ^ fy, c)
            for i in range(n):
                mine = outs[i].at[me, halves[i]]
                sends.append(remote(mine, mine, send_sems.at[k * n + i], recv_sems.at[k * n + i], peer))
            sends.append(remote(conv_ref, conv_out.at[me], csend_sems.at[k], crecv_sems.at[k], peer))
        for cp in sends:
            cp.start()
        sibling = (x, y, 1 - c)
        forwards = []
        for k, (fx, fy) in enumerate(_CHIP_FLIPS):
            peer = (x ^ fx, y ^ fy, c)
            src = 2 * (x ^ fx) + (y ^ fy)
            for i in range(n):
                landed = outs[i].at[src, halves[i]]
                remote(landed, landed, send_sems.at[k * n + i], recv_sems.at[k * n + i], peer).wait_recv()
                fw = remote(landed, landed, fsend_sems.at[k * n + i], frecv_sems.at[k * n + i], sibling)
                fw.start()
                forwards.append(fw)
            remote(conv_out.at[src], conv_out.at[src], csend_sems.at[k], crecv_sems.at[k], peer).wait_recv()
        for k, (fx, fy) in enumerate(_CHIP_FLIPS):
            src = 2 * (x ^ fx) + (y ^ fy)
            for i in range(n):
                theirs = outs[i].at[src, others[i]]
                remote(theirs, theirs, fsend_sems.at[k * n + i], frecv_sems.at[k * n + i], sibling).wait_recv()
        for cp in sends + forwards:
            cp.wait_send()
        for cp in local:
            cp.wait()

    dma = pltpu.SemaphoreType.DMA
    return pl.pallas_call(
        body, name="weight_gather",
        in_specs=[_ANY] * (n + 1), out_specs=[_ANY] * (n + 1),
        out_shape=[_sds(b.shape, b.dtype) for b in bufs] + [_sds((4,) + conv.shape, conv.dtype)],
        input_output_aliases={i: i for i in range(n)},
        scratch_shapes=[dma((3 * n,)), dma((3 * n,)), dma((3 * n,)), dma((3 * n,)), dma((3,)), dma((3,)), dma],
    )(*bufs, conv)


def _piece(ref, kind, R, C, k, h):
    if kind == "slab":
        return ref.at[k, _half(h, R // 2), :]
    if kind == "rows":
        return ref.at[pl.ds(pl.multiple_of(k * R + h * (R // 2), R // 2), R // 2), :]
    return ref.at[_half(h, R // 2), pl.ds(pl.multiple_of(k * C, C), C)]


def _small_exchange(small):
    rs = small.shape[0]

    def body(s_ref, out_ref, send_sems, recv_sems, local_sem):
        x, y, c = lax.axis_index("x"), lax.axis_index("y"), lax.axis_index("c")
        slot = 4 * x + 2 * y + c
        own = pltpu.make_async_copy(s_ref, out_ref.at[slot], local_sem)
        own.start()
        copies = []
        for k, (fx, fy, fc) in enumerate(_DEVICE_FLIPS):
            copies.append(_remote(s_ref, out_ref.at[slot], send_sems.at[k], recv_sems.at[k], (x ^ fx, y ^ fy, c ^ fc)))
        for cp in copies:
            cp.start()
        for k, (fx, fy, fc) in enumerate(_DEVICE_FLIPS):
            theirs = out_ref.at[slot ^ (k + 1)]
            _remote(theirs, theirs, send_sems.at[k], recv_sems.at[k], (x ^ fx, y ^ fy, c ^ fc)).wait_recv()
        for cp in copies:
            cp.wait_send()
        own.wait()

    dma = pltpu.SemaphoreType.DMA
    return pl.pallas_call(
        body, name="small_exchange",
        in_specs=[_ANY], out_specs=_ANY, out_shape=_sds((8, rs, 128), F32),
        scratch_shapes=[dma((7,)), dma((7,)), dma],
    )(small)


_HBM = pl.BlockSpec(memory_space=pltpu.HBM)
_SEM = pl.BlockSpec(memory_space=pltpu.SEMAPHORE)


def _split_start(name, arrays, n_copies, plan, after=None):
    n = len(arrays)
    extra = [] if after is None else [after]

    def body(*refs):
        m = n + len(extra)
        arrs, send_sems, recv_sems, token = refs[:n], refs[m], refs[m + 1], refs[-1]
        for j, (src, dst, peer) in enumerate(plan(arrs)):
            _remote(src, dst, send_sems.at[j], recv_sems.at[j], peer).start()
        token[...] = jnp.zeros_like(token)

    dma = pltpu.SemaphoreType.DMA
    res = pl.pallas_call(
        body, name=name,
        out_shape=(dma((n_copies,)), dma((n_copies,)), *[pltpu.HBM(a.shape, a.dtype) for a in arrays],
                   _sds((8, 128), F32)),
        in_specs=[_HBM] * n + [_ANY] * len(extra),
        out_specs=(_SEM, _SEM, *[_HBM] * n, pl.BlockSpec(memory_space=pltpu.VMEM)),
        input_output_aliases={i: 2 + i for i in range(n)},
        compiler_params=pltpu.CompilerParams(has_side_effects=pltpu.SideEffectType.DATAFLOW_SIDE_EFFECTING),
    )(*[pltpu.with_memory_space_constraint(a, pltpu.HBM) for a in arrays], *extra)
    return res[0], res[1], list(res[2:2 + n]), res[-1]


def _split_wait(name, arrays, send_sems, recv_sems, plan, after):
    n = len(arrays)

    def body(*refs):
        arrs, ssems, rsems = refs[:n], refs[n], refs[n + 1]
        for j, (src, dst, peer) in enumerate(plan(arrs)):
            cp = _remote(src, dst, ssems.at[j], rsems.at[j], peer)
            cp.wait_send()
            cp.wait_recv()

    return list(pl.pallas_call(
        body, name=name,
        out_shape=tuple(pltpu.HBM(a.shape, a.dtype) for a in arrays),
        in_specs=[_HBM] * n + [_SEM, _SEM, _ANY],
        out_specs=tuple([_HBM] * n),
        input_output_aliases={i: i for i in range(n)},
        compiler_params=pltpu.CompilerParams(has_side_effects=pltpu.SideEffectType.DATAFLOW_SIDE_EFFECTING),
    )(*arrays, send_sems, recv_sems, after))


def _gather_plan(n):
    def plan(bufs):
        x, y, c = lax.axis_index("x"), lax.axis_index("y"), lax.axis_index("c")
        me = 2 * x + y
        return [(bufs[i].at[me], bufs[i].at[me], (x ^ fx, y ^ fy, c)) for fx, fy in _CHIP_FLIPS for i in range(n)]

    return plan


def _reduce_plan(specs, n_small):
    n = len(specs)

    def plan(arrs):
        x, y, c = lax.axis_index("x"), lax.axis_index("y"), lax.axis_index("c")
        slot = 4 * x + 2 * y + c
        out = []
        for fx, fy, fc in _DEVICE_FLIPS:
            peer = (x ^ fx, y ^ fy, c ^ fc)
            for i, (kind, (R, C)) in enumerate(specs):
                out.append((_piece(arrs[i], kind, R, C, 2 * peer[0] + peer[1], peer[2]), arrs[n + i].at[slot], peer))
            for s in range(n_small):
                out.append((arrs[2 * n + 2 * s], arrs[2 * n + 2 * s + 1].at[slot], peer))
        return out

    return plan


def _sibling_exchange(halves):
    n = len(halves)

    def body(*refs):
        ins, outs, send_sems, recv_sems = refs[:n], refs[n:2 * n], refs[2 * n], refs[2 * n + 1]
        sibling = (lax.axis_index("x"), lax.axis_index("y"), 1 - lax.axis_index("c"))
        copies = [pltpu.make_async_remote_copy(src_ref=ins[i], dst_ref=outs[i], send_sem=send_sems.at[i],
                                               recv_sem=recv_sems.at[i], device_id=sibling, device_id_type=MESH)
                  for i in range(n)]
        for cp in copies:
            cp.start()
        for cp in copies:
            cp.wait()

    dma = pltpu.SemaphoreType.DMA
    return pl.pallas_call(
        body, name="sibling_exchange",
        in_specs=[_ANY] * n, out_specs=[_ANY] * n,
        out_shape=[_sds(h.shape, h.dtype) for h in halves],
        scratch_shapes=[dma((n,)), dma((n,))],
    )(*halves)


_BIG = [("w_in", (1024, 1156), "slab"), ("w_out", (512, 1024), "rows"), ("w_ff1", (1024, 1024), "cols"),
        ("w_ff2", (1024, 1024), "rows"), ("w_ple_gate", (256, 1024), "rows"), ("w_ple_proj", (256, 256), "cols")]
_SMALL = [("norm_mix_g", (1, 1024)), ("gm_v_norm_g", (1, 1024)), ("gm_ws", (1, 8, 128, 128)), ("gm_bs", (1, 8, 128)),
          ("gm_out_norm_g", (1, 1024)), ("ssd_conv_w", (1, 4, 1536)), ("ssd_conv_b", (1, 1536)),
          ("ssd_dt_bias", (1, 16)), ("ssd_a_log", (1, 16)), ("ssd_d", (1, 16)), ("ssd_norm_g", (1, 1024)),
          ("norm_mlp_g", (1, 1024)), ("ple_norm_g", (1, 1024)), ("final_norm_g", (1024,))]


def _rows128(a):
    flat = a.reshape(-1)
    rows = -(-flat.shape[0] // 1024) * 8
    return jnp.pad(flat, (0, rows * 128 - flat.shape[0])).reshape(rows, 128)


def _pad_lanes(v, n=128):
    v = v.reshape(1, -1)
    return jnp.pad(v, ((0, 0), (0, n - v.shape[1])))


_SMALL_SHAPES = dict(_SMALL + [("loss", ())])
_BIG_SPECS = {n: (kind, shp) for n, shp, kind in _BIG}


class _Comm:
    def __init__(self, a, kh):
        self.a, self.kh = a, kh
        self.bufs = {n: _cast_into_slot(a[n].reshape(shp), kh, "cast_" + n) for n, shp, _ in _BIG}
        self.sent = []
        self.small_tot = {}

    def w_in(self):
        (g_win,), g_cw = self._gather_now()
        rest = [self.bufs[n] for n, _, _ in _BIG[1:]]
        plan = _gather_plan(len(rest))
        ssem, rsem, thru, token = _split_start("gather_start", rest, 3 * len(rest), plan, after=g_cw)
        self.gather = (plan, ssem, rsem, thru)
        w_in_full = jnp.concatenate([g_win[k] for k in range(4)], axis=1)
        wm, wdt = w_in_full[:, :N_MAIN], jnp.pad(w_in_full[:, N_MAIN:], ((0, 0), (0, 128 - 16)))
        return wm, wdt, jnp.concatenate([g_cw[k] for k in range(4)], axis=1), token

    def _gather_now(self):
        *bufs, g_cw = _weight_gather([self.bufs["w_in"]], self.a["ssd_conv_w"].reshape(4, 384))
        return bufs, g_cw

    def rest(self, after):
        plan, ssem, rsem, thru = self.gather
        g_wo, g_w1, g_w2, g_wg, g_wp = _split_wait("gather_wait", thru, ssem, rsem, plan, after)
        return g_wo.reshape(2048, D), g_w1, g_w2.reshape(DFF, D), g_wg.reshape(D, D), g_wp

    def send(self, tag, grads):
        big = [n for n, _, _ in _BIG if n in grads]
        small = [n for n in _SMALL_SHAPES if n in grads]
        parts = [_rows128(grads[n]) for n in small]
        rows = [s.shape[0] for s in parts]
        if not big:
            self._unpack(_sum_small(_small_exchange(jnp.concatenate(parts, axis=0)), "sum_small_" + tag), small, rows)
            return None
        srcs = [jnp.stack([grads[n][:, 1156 * k:1156 * (k + 1)] for k in range(4)]) if n == "w_in" else grads[n]
                for n in big]
        lands = [lax.empty((8, _BIG_SPECS[n][1][0] // 2, _BIG_SPECS[n][1][1]), GRAD) for n in big]
        extra = []
        if small:
            pack = jnp.concatenate(parts, axis=0)
            extra = [pack, jnp.broadcast_to(pack, (8,) + pack.shape)]
        plan = _reduce_plan([_BIG_SPECS[n] for n in big], len(extra) // 2)
        n_copies = 7 * (len(big) + len(extra) // 2)
        ssem, rsem, thru, token = _split_start("reduce_start_" + tag, srcs + lands + extra, n_copies, plan)
        self.sent.append((tag, big, small, rows, plan, ssem, rsem, thru))
        return token

    def _unpack(self, tot, names, rows):
        o = 0
        for n, r in zip(names, rows):
            shp = _SMALL_SHAPES[n]
            cnt = 1
            for s in shp:
                cnt *= s
            self.small_tot[n] = tot[o:o + r].reshape(-1)[:cnt].reshape(shp)
            o += r

    def finish(self, after):
        own = {}
        for tag, big, small, rows, plan, ssem, rsem, thru in self.sent:
            arrs = _split_wait("reduce_wait_" + tag, thru, ssem, rsem, plan, after)
            nb_ = len(big)
            for i, n in enumerate(big):
                kind, shp = _BIG_SPECS[n]
                own[n] = _sum_slots(arrs[nb_ + i], arrs[i], kind, shp, self.kh, "sum_" + n)
                after = own[n]
            if small:
                self._unpack(_sum_small(arrs[2 * nb_ + 1], "sum_small_" + tag), small, rows)
        return [own[n] for n, _, _ in _BIG], dict(self.small_tot)


def _local_step(x, p, tgt, sm, comm, nb, tm):
    wm, wdt, conv_w, token = comm.w_in()
    g_mix, gv, gout = sm["norm_mix_g"].reshape(1, D), sm["gm_v_norm_g"].reshape(1, D), sm["gm_out_norm_g"].reshape(1, D)
    ws = sm["gm_ws"].reshape(GM_HEADS, CH, CH)
    bst = jnp.pad(sm["gm_bs"].reshape(GM_HEADS, CH).T, ((0, 0), (0, 128 - GM_HEADS)))
    convw = jnp.pad(conv_w, ((0, 4), (0, 0)))
    convb = sm["ssd_conv_b"].reshape(1, CONV_CH)
    dtb, alog = _pad_lanes(sm["ssd_dt_bias"]), _pad_lanes(sm["ssd_a_log"])
    dskip = jnp.repeat(sm["ssd_d"].reshape(SSD_HEADS), SSD_P).reshape(1, 1024)
    ng, g_mlp, g_ple = sm["ssd_norm_g"].reshape(1, D), sm["norm_mlp_g"].reshape(1, D), sm["ple_norm_g"].reshape(1, D)
    gf = sm["final_norm_g"].reshape(1, D)
    head_of_lane = lax.broadcasted_iota(jnp.int32, (128, 1024), 1) // SSD_P
    ex = (lax.broadcasted_iota(jnp.int32, (128, 1024), 0) == head_of_lane).astype(BF16)
    ext = ex.T
    ltri = (lax.broadcasted_iota(jnp.int32, (CH, CH), 0) >= lax.broadcasted_iota(jnp.int32, (CH, CH), 1)).astype(F32)

    pz, pxbc, dtraw, xn, cat, uv = _inproj_gmlp(x, g_mix, wm, wdt, gv, ws, bst, gout, tm // 2, token)
    cat, sall = _ssd_fwd(pz, pxbc, dtraw, cat, convw, convb, dtb, alog, dskip, ng, ex, ltri, nb)
    wo, w1, w2, wg, wp = comm.rest(cat)
    h1, hn = _outproj(cat, wo, x, g_mlp, tm)
    hid = _ff1(hn, w1, tm)
    hp, dgl, dpe, dh2, dh2b, loss, d_gf, d_gple = _ff2_tail(hid, w2, h1, g_ple, p, tgt, wg, wp, gf, tm // 2)

    d_wp = _matmul_tn(p, dpe, "dw_ple_proj", a_fn=lambda a: a.astype(MXU))
    d_wg = _matmul_tn(hp, dgl, "dw_ple_gate")
    d_w2 = _matmul_tn(hid, dh2b, "dw_ff2", a_fn=_sq)
    dpre = _ff2_bwd(dh2b, w2, hid, tm)
    d_w1 = _matmul_tn(hn, dpre, "dw_ff1")
    token = comm.send("a", {"w_ple_proj": d_wp, "w_ple_gate": d_wg, "w_ff2": d_w2, "w_ff1": d_w1})
    dh1, dh1b, d_gmlp = _ff1_bwd(dpre, w1, dh2, h1, g_mlp, tm, token)
    dcat = _outproj_bwd(dh1b, wo, tm)
    d_wo = _matmul_tn(cat, dh1b, "dw_out")
    duv, d_gv, d_ws, d_bst, d_gout = _gmlp_bwd(uv, dcat, gv, ws, bst, gout)
    token = comm.send("b", {
        "w_out": d_wo, "loss": loss[0:1, 0:1], "final_norm_g": d_gf, "ple_norm_g": d_gple, "norm_mlp_g": d_gmlp,
        "gm_v_norm_g": d_gv, "gm_ws": d_ws, "gm_bs": d_bst[:, :GM_HEADS].T, "gm_out_norm_g": d_gout})
    dssd, ddt, d_cw, d_cb, d_dtb, d_al, d_ds, d_ng = _ssd_bwd(
        pz, pxbc, dtraw, sall, dcat, convw, convb, dtb, alog, dskip, ng, ex, ltri, ext, nb, token)
    d_win = jnp.concatenate([_matmul_tn(xn, duv, "dw_in_uv"), _matmul_tn(xn, dssd, "dw_in_ssd"),
                             _matmul_tn(xn, ddt, "dw_in_dt")[:, :16]], axis=1)
    token = comm.send("c", {"w_in": d_win})
    dx, d_gmix = _inproj_bwd(duv, dssd, ddt, wm, wdt, dh1, x, g_mix, tm, token)
    comm.send("d", {"norm_mix_g": d_gmix, "ssd_conv_w": d_cw[0:4], "ssd_conv_b": d_cb, "ssd_dt_bias": d_dtb[:, :16],
                    "ssd_a_log": d_al[:, :16], "ssd_d": d_ds[:, :16], "ssd_norm_g": d_ng})
    return dx


def kernel(x, p, norm_mix_g, w_in, gm_v_norm_g, gm_ws, gm_bs, gm_out_norm_g, ssd_conv_w, ssd_conv_b, ssd_dt_bias, ssd_a_log, ssd_d, ssd_norm_g, w_out, norm_mlp_g, w_ff1, w_ff2, ple_norm_g, w_ple_gate, w_ple_proj, final_norm_g, loss_target, m_norm_mix_g, m_w_in, m_gm_v_norm_g, m_gm_ws, m_gm_bs, m_gm_out_norm_g, m_ssd_conv_w, m_ssd_conv_b, m_ssd_dt_bias, m_ssd_a_log, m_ssd_d, m_ssd_norm_g, m_w_out, m_norm_mlp_g, m_w_ff1, m_w_ff2, m_ple_norm_g, m_w_ple_gate, m_w_ple_proj, m_final_norm_g, v_norm_mix_g, v_w_in, v_gm_v_norm_g, v_gm_ws, v_gm_bs, v_gm_out_norm_g, v_ssd_conv_w, v_ssd_conv_b, v_ssd_dt_bias, v_ssd_a_log, v_ssd_d, v_ssd_norm_g, v_w_out, v_norm_mlp_g, v_w_ff1, v_w_ff2, v_ple_norm_g, v_w_ple_gate, v_w_ple_proj, v_final_norm_g):
    a = dict(locals())
    order = ["norm_mix_g", "w_in", "gm_v_norm_g", "gm_ws", "gm_bs", "gm_out_norm_g", "ssd_conv_w", "ssd_conv_b",
             "ssd_dt_bias", "ssd_a_log", "ssd_d", "ssd_norm_g", "w_out", "norm_mlp_g", "w_ff1", "w_ff2", "ple_norm_g",
             "w_ple_gate", "w_ple_proj", "final_norm_g"]
    chip = 2 * lax.axis_index("x") + lax.axis_index("y")
    nb, S = x.shape[0], x.shape[1]
    T = nb * S
    sm = {n: a[n] for n, _ in _SMALL if n != "ssd_conv_w"}
    comm = _Comm(a, jnp.stack([chip, lax.axis_index("c")]).astype(jnp.int32))
    dx = _local_step(x.reshape(T, D), p.reshape(T, DPLE), loss_target.reshape(T, D), sm, comm, nb, 512)
    own, g_out = comm.finish(dx)
    other = _sibling_exchange(own)

    delta, new_m, new_v = {}, {}, {}
    for i, (n, shp, _) in enumerate(_BIG):
        res = _adamw_halves(a[n].reshape(shp), own[i], other[i], a["m_" + n].reshape(shp), a["v_" + n].reshape(shp),
                            "adamw_" + n)
        g_out[n], delta[n], new_m[n], new_v[n] = (r.reshape(a[n].shape) for r in res)
    g_out["ssd_conv_w"] = lax.dynamic_slice(g_out["ssd_conv_w"], (0, 0, chip * 384), (1, 4, 384))
    small_names = [n for n, _ in _SMALL]
    packs = [jnp.concatenate([_rows128(src(n)) for n in small_names], axis=0)
             for src in (lambda n: a[n], lambda n: g_out[n], lambda n: a["m_" + n], lambda n: a["v_" + n])]
    outs = _adamw(*packs, "adamw_small")
    o = 0
    for n in small_names:
        r = _rows128(a[n]).shape[0]
        cnt = a[n].size
        for dst, src in zip((delta, new_m, new_v), outs):
            dst[n] = src[o:o + r].reshape(-1)[:cnt].reshape(a[n].shape)
        o += r
    return (g_out["loss"], dx.reshape(x.shape), *[g_out[n] for n in order], *[delta[n] for n in order],
            *[new_m[n] for n in order], *[new_v[n] for n in order])
```

```python
import jax
import jax.numpy as jnp
from jax import lax
from jax.experimental import pallas as pl
from jax.experimental.pallas import tpu as pltpu

F32 = jnp.float32
BF16 = jnp.bfloat16
MXU = jnp.bfloat16
GRAD = jnp.bfloat16

D = 1024
CH = 128
GM_HEADS = 8
SSD_HEADS = 16
SSD_P = 64
CONV_CH = 1536
N_MAIN = 4608
DFF = 4096
DPLE = 256
EPS = 1e-6
NEG = -1e30

LR, B1, B2, ADAM_EPS, WD, STEP = 0.001, 0.9, 0.999, 1e-08, 0.01, 10

VMEM_LIMIT = 56 * 1024 * 1024
MESH = pl.DeviceIdType.MESH

INV_SQRT2 = 0.7071067811865476
INV_SQRT_2PI = 0.3989422804014327


def _cp(n_axes=1):
    return pltpu.CompilerParams(dimension_semantics=("arbitrary",) * n_axes, vmem_limit_bytes=VMEM_LIMIT)


def _dot(a, b):
    return jnp.dot(a, b, preferred_element_type=F32)


def _dot_nt(a, b):
    return lax.dot_general(a, b, (((1,), (1,)), ((), ())), preferred_element_type=F32)


def _dot_tn(a, b):
    return lax.dot_general(a, b, (((0,), (0,)), ((), ())), preferred_element_type=F32)


def _dot_hi(a, b):
    return jnp.dot(a, b, preferred_element_type=F32, precision=lax.Precision.HIGHEST)


def _dot_01(a, sel):
    hi = a.astype(BF16)
    lo = (a - hi.astype(F32)).astype(BF16)
    n = a.shape[0]
    r = _dot(jnp.concatenate([hi, lo], axis=0), sel)
    return r[0:n] + r[n:2 * n]


def _rows(tm, n, j=0):
    return pl.BlockSpec((tm, n), lambda i: (i, j))


def _const(shape):
    nd = len(shape)
    return pl.BlockSpec(shape, lambda *_: (0,) * nd)


def _sds(shape, dtype):
    return jax.ShapeDtypeStruct(shape, dtype)


def _rms(x):
    r = lax.rsqrt(jnp.mean(x * x, axis=-1, keepdims=True) + EPS)
    return x * r, r


def _rms_bwd(dy, xhat, r, g):
    dyg = dy * g
    return r * (dyg - xhat * jnp.mean(dyg * xhat, axis=-1, keepdims=True))


def _sigmoid(x):
    return 1.0 / (1.0 + jnp.exp(-x))


def _gelu(x):
    cdf = 0.5 * (1.0 + lax.erf(x * INV_SQRT2))
    pdf = jnp.exp(-0.5 * x * x) * INV_SQRT_2PI
    return x * cdf, cdf + x * pdf


def _softplus(x):
    e = jnp.exp(-jnp.abs(x))
    u = 1.0 + e
    log1p = jnp.where(u == 1.0, e, jnp.log(u) * e / (u - 1.0))
    return jnp.maximum(x, 0.0) + log1p


def _after(n_in, fn):
    def body(*refs):
        return fn(*refs[:n_in], *refs[n_in + 1:])

    return body


def _inproj_gmlp(x, g, wm, wdt, gv, ws, bst, gout, tm, after):
    T = x.shape[0]

    def body(x_ref, g_ref, wm_ref, wdt_ref, gv_ref, ws_ref, bst_ref, gout_ref,
             z_ref, xbc_ref, dt_ref, xn_ref, ya_ref, uv_ref):
        xh, _ = _rms(x_ref[...])
        xn = (xh * g_ref[...]).astype(MXU)
        xn_ref[...] = xn
        for n in range(4):
            uv_ref[:, n * 512:(n + 1) * 512] = _dot(xn, wm_ref[:, n * 512:(n + 1) * 512])
        for n in range(2):
            z_ref[:, n * 512:(n + 1) * 512] = _dot(xn, wm_ref[:, 2048 + n * 512:2048 + (n + 1) * 512])
        for n in range(3):
            xbc_ref[:, n * 512:(n + 1) * 512] = _dot(xn, wm_ref[:, 3072 + n * 512:3072 + (n + 1) * 512])
        dt_ref[...] = _dot(xn, wdt_ref[...])
        for k in range(tm // CH):
            rows = slice(k * CH, (k + 1) * CH)
            f = _gmlp_fwd_vals(uv_ref[rows, 0:1024], uv_ref[rows, 1024:2048], gv_ref[...], ws_ref, bst_ref[...],
                               gout_ref[...])
            ya_ref[rows, :] = f["out"].astype(MXU)

    return pl.pallas_call(
        _after(8, body), grid=(T // tm,), name="inproj_gmlp",
        in_specs=[_rows(tm, D), _const((1, D)), _const((D, N_MAIN)), _const((D, 128)), _const((1, 1024)),
                  _const((GM_HEADS, CH, CH)), _const((CH, 128)), _const((1, 1024)), _ANY],
        out_specs=[_rows(tm, 1024), _rows(tm, CONV_CH), _rows(tm, 128), _rows(tm, D), _rows(tm, 1024, 0),
                   _rows(tm, 2048)],
        out_shape=[_sds((T, 1024), F32), _sds((T, CONV_CH), F32), _sds((T, 128), F32), _sds((T, D), MXU),
                   _sds((T, 2048), MXU), _sds((T, 2048), F32)],
        compiler_params=_cp(),
    )(x, g, wm, wdt, gv, ws, bst, gout, after)


def _gmlp_fwd_vals(u, v, gv, ws_ref, bst, gout):
    ug, dug = _gelu(u)
    vg, dvg = _gelu(v)
    row = lax.broadcasted_iota(jnp.int32, (CH, CH), 0)
    col = lax.broadcasted_iota(jnp.int32, (CH, CH), 1)
    tril = row >= col
    ys, heads = [], []
    for h in range(GM_HEADS):
        sl = slice(h * 128, (h + 1) * 128)
        vhat, rv = _rms(vg[:, sl])
        vn = (vhat * gv[:, sl]).astype(MXU)
        wt = jnp.where(tril, ws_ref[h], 0.0)
        mixed = _dot(wt.astype(MXU), vn) + bst[:, h:h + 1]
        ys.append(ug[:, sl] * mixed)
        heads.append((vhat, rv, vn, wt, mixed))
    y = jnp.concatenate(ys, axis=1)
    yhat, ry = _rms(y)
    return dict(ug=ug, dug=dug, dvg=dvg, heads=heads, yhat=yhat, ry=ry, tril=tril, out=yhat * gout)


def _shifts_down(cur, halo):
    row8 = lax.broadcasted_iota(jnp.int32, (8, cur.shape[1]), 0)
    out = [cur]
    for j in (1, 2, 3):
        sh = pltpu.roll(cur, j, 0)
        top = jnp.where(row8 < j, pltpu.roll(halo, j, 0), sh[0:8])
        out.append(jnp.concatenate([top, sh[8:]], axis=0))
    return out


def _shifts_up(cur, halo):
    row8 = lax.broadcasted_iota(jnp.int32, (8, cur.shape[1]), 0)
    out = []
    for j in (1, 2, 3):
        sh = pltpu.roll(cur, CH - j, 0)
        bot = jnp.where(row8 + j >= 8, pltpu.roll(halo, 8 - j, 0), sh[CH - 8:CH])
        out.append(jnp.concatenate([sh[0:CH - 8], bot], axis=0))
    return out


def _ssd_fwd_vals(z, xbc, halo, dtraw, convw, convb, dtb, alog, dskip, ng, ex, ltri, s_prev):
    shifts = _shifts_down(xbc, halo)
    conv = convb + convw[3:4] * shifts[0] + convw[2:3] * shifts[1] + convw[1:2] * shifts[2] + convw[0:1] * shifts[3]
    sig_c = _sigmoid(conv)
    xa = conv * sig_c
    xs = xa[:, :1024]
    bm = [xa[:, 1024:1152], xa[:, 1152:1280]]
    cm = [xa[:, 1280:1408], xa[:, 1408:1536]]
    dtpre = dtraw + dtb
    dt = _softplus(dtpre)
    a_neg = -jnp.exp(alog)
    cs = _dot_hi(ltri, dt * a_neg)
    cst = cs.T
    last = cs[CH - 1:CH]
    ecs = jnp.exp(cs)
    dec = jnp.exp(last - cs)
    spread = _dot_01(jnp.concatenate([dt, ecs, dec], axis=0), ex)
    dte, ecse, dece = spread[0:CH], spread[CH:2 * CH], spread[2 * CH:3 * CH]
    cde = ecse[CH - 1:CH]
    de = dskip
    xdt = xs * dte
    row = lax.broadcasted_iota(jnp.int32, (CH, CH), 0)
    col = lax.broadcasted_iota(jnp.int32, (CH, CH), 1)
    tril = row >= col
    lo = col < SSD_P
    bmb = [b.astype(MXU) for b in bm]
    cmb = [c.astype(MXU) for c in cm]
    mg = [_dot_nt(cmb[g], bmb[g]) for g in range(2)]
    yd, lms, whs = [], [], []
    for q in range(8):
        g = q // 4
        xq = xdt[:, q * 128:(q + 1) * 128]
        acc = None
        for hh in range(2):
            h = 2 * q + hh
            seg = cs[:, h:h + 1] - cst[h:h + 1, :]
            lm = jnp.exp(jnp.where(tril, seg, NEG))
            wh = (mg[g] * lm).astype(MXU)
            xm = jnp.where(lo if hh == 0 else ~lo, xq, 0.0).astype(MXU)
            part = _dot(wh, xm)
            acc = part if acc is None else acc + part
            lms.append(lm)
            whs.append(wh)
        yd.append(acc)
    yd = jnp.concatenate(yd, axis=1)
    sb = s_prev.astype(MXU)
    yo = jnp.concatenate([_dot(cmb[g], sb[:, g * 512:(g + 1) * 512]) for g in range(2)], axis=1) * ecse
    xdec = (xdt * dece).astype(MXU)
    states = jnp.concatenate([_dot_tn(bmb[g], xdec[:, g * 512:(g + 1) * 512]) for g in range(2)], axis=1)
    s_next = s_prev * cde + states
    ypre = yd + yo + de * xs
    sig_z = _sigmoid(z)
    yg = ypre * z * sig_z
    outs, yhat, rr = [], [], []
    for g in range(2):
        sl = slice(g * 512, (g + 1) * 512)
        yh, r = _rms(yg[:, sl])
        yhat.append(yh)
        rr.append(r)
        outs.append(yh * ng[:, sl])
    return dict(shifts=shifts, conv=conv, sig_c=sig_c, xs=xs, bmb=bmb, cmb=cmb, dtpre=dtpre, dt=dt, a_neg=a_neg,
                cs=cs, last=last, ecs=ecs, dec=dec, dte=dte, ecse=ecse, dece=dece, cde=cde, de=de, xdt=xdt,
                mg=mg, lms=lms, whs=whs, lo=lo, yo=yo, sb=sb, xdec=xdec, s_next=s_next, ypre=ypre, sig_z=sig_z,
                yhat=yhat, rr=rr, out=jnp.concatenate(outs, axis=1))


def _ssd_specs(nch, rev):
    def tok(b, c):
        return b * nch + ((nch - 1 - c) if rev else c)

    return tok, [
        pl.BlockSpec((CH, 1024), lambda b, c: (tok(b, c), 0)),
        pl.BlockSpec((CH, CONV_CH), lambda b, c: (tok(b, c), 0)),
        pl.BlockSpec((8, CONV_CH), lambda b, c: (jnp.maximum(tok(b, c) * (CH // 8) - 1, 0), 0)),
        pl.BlockSpec((CH, 128), lambda b, c: (tok(b, c), 0)),
        _const((8, CONV_CH)), _const((1, CONV_CH)), _const((1, 128)), _const((1, 128)), _const((1, 1024)),
        _const((1, 1024)), _const((128, 1024)), _const((CH, CH)),
    ]


def _ssd_fwd(pz, pxbc, dtraw, cat, convw, convb, dtb, alog, dskip, ng, ex, ltri, nb):
    T = pz.shape[0]
    nch = T // CH // nb
    tok, in_specs = _ssd_specs(nch, rev=False)

    def body(z_ref, xbc_ref, halo_ref, dt_ref, cw_ref, cb_ref, dtb_ref, al_ref, ds_ref, ng_ref, ex_ref, lt_ref,
             cat_in_ref, yb_ref, sall_ref, s_ref):
        del cat_in_ref
        c = pl.program_id(1)

        @pl.when(c == 0)
        def _():
            s_ref[...] = jnp.zeros_like(s_ref)

        halo = jnp.where(c == 0, 0.0, halo_ref[...])
        s_prev = s_ref[...]
        sall_ref[0] = s_prev
        f = _ssd_fwd_vals(z_ref[...], xbc_ref[...], halo, dt_ref[...], cw_ref[...], cb_ref[...], dtb_ref[...],
                          al_ref[...], ds_ref[...], ng_ref[...], ex_ref[...], lt_ref[...], s_prev)
        s_ref[...] = f["s_next"]
        yb_ref[...] = f["out"].astype(MXU)

    return pl.pallas_call(
        body, grid=(nb, nch), name="ssd_fwd",
        in_specs=in_specs + [_ANY],
        out_specs=[pl.BlockSpec((CH, 1024), lambda b, c: (tok(b, c), 1)),
                   pl.BlockSpec((1, 128, 1024), lambda b, c: (tok(b, c), 0, 0))],
        out_shape=[_sds((T, 2048), MXU), _sds((T // CH, 128, 1024), F32)],
        scratch_shapes=[pltpu.VMEM((128, 1024), F32)],
        input_output_aliases={12: 0},
        compiler_params=_cp(2),
    )(pz, pxbc, pxbc, dtraw, convw, convb, dtb, alog, dskip, ng, ex, ltri, cat)


def _outproj(cat, wo, x, g, tm):
    T = x.shape[0]

    def body(cat_ref, wo_ref, x_ref, g_ref, h1_ref, hn_ref):
        h1 = x_ref[...] + _dot(cat_ref[...], wo_ref[...])
        h1_ref[...] = h1
        hn_ref[...] = (_rms(h1)[0] * g_ref[...]).astype(MXU)

    return pl.pallas_call(
        body, grid=(T // tm,), name="outproj",
        in_specs=[_rows(tm, 2048), _const((2048, D)), _rows(tm, D), _const((1, D))],
        out_specs=[_rows(tm, D), _rows(tm, D)],
        out_shape=[_sds((T, D), F32), _sds((T, D), MXU)],
        compiler_params=_cp(),
    )(cat, wo, x, g)


def _ff1(hn, w1, tm):
    T = hn.shape[0]

    def body(hn_ref, w1_ref, hid_ref):
        hn_v = hn_ref[...]
        for n in range(4):
            hid_ref[:, n * 1024:(n + 1) * 1024] = jnp.maximum(_dot(hn_v, w1_ref[n]), 0.0).astype(MXU)

    return pl.pallas_call(
        body, grid=(T // tm,), name="ff1",
        in_specs=[_rows(tm, D), _const((4, D, 1024))],
        out_specs=_rows(tm, DFF),
        out_shape=_sds((T, DFF), MXU),
        compiler_params=_cp(),
    )(hn, w1)


def _sq(hid):
    h = hid.astype(F32)
    return (h * h).astype(MXU)


def _ff2_tail(hid, w2, h1, g_ple, p, tgt, wg, wp, gf, tm):
    T = h1.shape[0]

    def body(hid_ref, w2_ref, h1_ref, g_ref, p_ref, t_ref, wg_ref, wp_ref, gf_ref,
             hp_ref, dgl_ref, dpe_ref, dh2_ref, dh2b_ref, loss_ref, dgf_ref, dg_ref):
        @pl.when(pl.program_id(0) == 0)
        def _():
            loss_ref[...] = jnp.zeros_like(loss_ref)
            dgf_ref[...] = jnp.zeros_like(dgf_ref)
            dg_ref[...] = jnp.zeros_like(dg_ref)

        h2 = h1_ref[...] + _dot(_sq(hid_ref[...]), w2_ref[...])
        h2h, r2 = _rms(h2)
        g_ple = g_ref[...]
        hp = (h2h * g_ple).astype(MXU)
        hp_ref[...] = hp
        gate = _sigmoid(_dot(hp, wg_ref[...]))
        pb = p_ref[...].astype(MXU)
        pe = jnp.concatenate([_dot(pb, wp_ref[k]) for k in range(4)], axis=1)
        h3 = h2 + gate * pe
        hh, r = _rms(h3)
        gf = gf_ref[...]
        diff = hh * gf - t_ref[...]
        loss_ref[...] += 0.5 * jnp.sum(jnp.mean(diff * diff, axis=-1, keepdims=True))
        dout = diff * (1.0 / D)
        dgf_ref[...] += jnp.sum(dout * hh, axis=0, keepdims=True)
        dh3 = _rms_bwd(dout, hh, r, gf)
        dgl = (dh3 * pe * gate * (1.0 - gate)).astype(MXU)
        dgl_ref[...] = dgl
        dpe_ref[...] = (dh3 * gate).astype(MXU)
        dhp = _dot_nt(dgl, wg_ref[...])
        dg_ref[...] += jnp.sum(dhp * h2h, axis=0, keepdims=True)
        dh2 = dh3 + _rms_bwd(dhp, h2h, r2, g_ple)
        dh2_ref[...] = dh2
        dh2b_ref[...] = dh2.astype(MXU)

    return pl.pallas_call(
        body, grid=(T // tm,), name="ff2_tail",
        in_specs=[_rows(tm, DFF), _const((DFF, D)), _rows(tm, D), _const((1, D)), _rows(tm, DPLE), _rows(tm, D),
                  _const((D, D)), _const((4, DPLE, 256)), _const((1, D))],
        out_specs=[_rows(tm, D), _rows(tm, D), _rows(tm, D), _rows(tm, D), _rows(tm, D), _const((8, 128)),
                   _const((1, D)), _const((1, D))],
        out_shape=[_sds((T, D), MXU), _sds((T, D), MXU), _sds((T, D), MXU), _sds((T, D), F32), _sds((T, D), MXU),
                   _sds((8, 128), F32), _sds((1, D), F32), _sds((1, D), F32)],
        compiler_params=_cp(),
    )(hid, w2, h1, g_ple, p, tgt, wg, wp, gf)


def _ff2_bwd(dh2b, w2, hid, tm):
    T = hid.shape[0]

    def body(dh2b_ref, w2_ref, hid_ref, dpre_ref):
        d = dh2b_ref[...]
        for n in range(DFF // 1024):
            sl = slice(n * 1024, (n + 1) * 1024)
            da = _dot_nt(d, w2_ref[sl, :])
            dpre_ref[:, sl] = (2.0 * da * hid_ref[:, sl].astype(F32)).astype(MXU)

    return pl.pallas_call(
        body, grid=(T // tm,), name="ff2_bwd",
        in_specs=[_rows(tm, D), _const((DFF, D)), _rows(tm, DFF)],
        out_specs=_rows(tm, DFF),
        out_shape=_sds((T, DFF), MXU),
        compiler_params=_cp(),
    )(dh2b, w2, hid)


def _ff1_bwd(dpre, w1, dh2, h1, g, tm, after):
    T = h1.shape[0]

    def body(dpre_ref, w1_ref, dh2_ref, h1_ref, g_ref, dh1_ref, dh1b_ref, dg_ref):
        @pl.when(pl.program_id(0) == 0)
        def _():
            dg_ref[...] = jnp.zeros_like(dg_ref)

        dhn = _dot_nt(dpre_ref[:, 0:1024], w1_ref[0])
        for k in range(1, 4):
            dhn = dhn + _dot_nt(dpre_ref[:, k * 1024:(k + 1) * 1024], w1_ref[k])
        hh, r = _rms(h1_ref[...])
        dg_ref[...] += jnp.sum(dhn * hh, axis=0, keepdims=True)
        dh1 = dh2_ref[...] + _rms_bwd(dhn, hh, r, g_ref[...])
        dh1_ref[...] = dh1
        dh1b_ref[...] = dh1.astype(MXU)

    return pl.pallas_call(
        _after(5, body), grid=(T // tm,), name="ff1_bwd",
        in_specs=[_rows(tm, DFF), _const((4, D, 1024)), _rows(tm, D), _rows(tm, D), _const((1, D)), _ANY],
        out_specs=[_rows(tm, D), _rows(tm, D), _const((1, D))],
        out_shape=[_sds((T, D), F32), _sds((T, D), MXU), _sds((1, D), F32)],
        compiler_params=_cp(),
    )(dpre, w1, dh2, h1, g, after)


def _outproj_bwd(dh1b, wo, tm):
    T = dh1b.shape[0]

    def body(d_ref, wo_ref, dcat_ref):
        d = d_ref[...]
        dcat_ref[:, 0:1024] = _dot_nt(d, wo_ref[0:1024, :])
        dcat_ref[:, 1024:2048] = _dot_nt(d, wo_ref[1024:2048, :])

    return pl.pallas_call(
        body, grid=(T // tm,), name="outproj_bwd",
        in_specs=[_rows(tm, D), _const((2048, D))],
        out_specs=_rows(tm, 2048),
        out_shape=_sds((T, 2048), F32),
        compiler_params=_cp(),
    )(dh1b, wo)


def _gmlp_bwd(uv, dcat, gv, ws, bst, gout, wm):
    T = uv.shape[0]
    nck = 2 if T % (2 * CH) == 0 else 1
    tb = nck * CH

    def body(uv_ref, dya_ref, gv_ref, ws_ref, bst_ref, gout_ref, wuv_ref, duv_ref, dgv_ref, dws_ref, dbst_ref,
             dgo_ref, dxn_ref):
        @pl.when(pl.program_id(0) == 0)
        def _():
            dgv_ref[...] = jnp.zeros_like(dgv_ref)
            dws_ref[...] = jnp.zeros_like(dws_ref)
            dbst_ref[...] = jnp.zeros_like(dbst_ref)
            dgo_ref[...] = jnp.zeros_like(dgo_ref)

        for k in range(nck):
            chunk(slice(k * CH, (k + 1) * CH), uv_ref, dya_ref, gv_ref, ws_ref, bst_ref, gout_ref, duv_ref,
                  dgv_ref, dws_ref, dbst_ref, dgo_ref)
        dxn_ref[...] = _dot_nt(duv_ref[...], wuv_ref[...])

    def chunk(rows, uv_ref, dya_ref, gv_ref, ws_ref, bst_ref, gout_ref, duv_ref, dgv_ref, dws_ref, dbst_ref,
              dgo_ref):
        gv = gv_ref[...]
        f = _gmlp_fwd_vals(uv_ref[rows, 0:1024], uv_ref[rows, 1024:2048], gv, ws_ref, bst_ref[...], gout_ref[...])
        dya = dya_ref[rows, :]
        dgo_ref[...] += jnp.sum(dya * f["yhat"], axis=0, keepdims=True)
        dy = _rms_bwd(dya, f["yhat"], f["ry"], gout_ref[...])
        lane = lax.broadcasted_iota(jnp.int32, (CH, 128), 1)
        dbs = jnp.zeros((CH, 128), F32)
        dug, dvg, dgvs = [], [], []
        for h in range(GM_HEADS):
            sl = slice(h * 128, (h + 1) * 128)
            vhat, rv, vn, wt, mixed = f["heads"][h]
            dyh = dy[:, sl]
            dug.append(dyh * mixed)
            dmixed = dyh * f["ug"][:, sl]
            dmb = dmixed.astype(MXU)
            dws_ref[h] += jnp.where(f["tril"], _dot_nt(dmb, vn), 0.0)
            dbs = dbs + jnp.where(lane == h, jnp.sum(dmixed, axis=1, keepdims=True), 0.0)
            dvn = _dot_tn(wt.astype(MXU), dmb)
            dgvs.append(jnp.sum(dvn * vhat, axis=0, keepdims=True))
            dvg.append(_rms_bwd(dvn, vhat, rv, gv[:, sl]))
        dbst_ref[...] += dbs
        dgv_ref[...] += jnp.concatenate(dgvs, axis=1)
        duv_ref[rows, 0:1024] = (jnp.concatenate(dug, axis=1) * f["dug"]).astype(MXU)
        duv_ref[rows, 1024:2048] = (jnp.concatenate(dvg, axis=1) * f["dvg"]).astype(MXU)

    return pl.pallas_call(
        body, grid=(T // tb,), name="gmlp_bwd",
        in_specs=[_rows(tb, 2048), _rows(tb, 1024, 0), _const((1, 1024)),
                  _const((GM_HEADS, CH, CH)), _const((CH, 128)), _const((1, 1024)), _const((D, 2048))],
        out_specs=[_rows(tb, 2048), _const((1, 1024)), _const((GM_HEADS, CH, CH)), _const((CH, 128)),
                   _const((1, 1024)), _rows(tb, D)],
        out_shape=[_sds((T, 2048), MXU), _sds((1, 1024), F32), _sds((GM_HEADS, CH, CH), F32), _sds((CH, 128), F32),
                   _sds((1, 1024), F32), _sds((T, D), F32)],
        compiler_params=_cp(),
    )(uv, dcat, gv, ws, bst, gout, wm)


def _ssd_bwd(pz, pxbc, dtraw, sall, dcat, convw, convb, dtb, alog, dskip, ng, ex, ltri, ext, nb, after):
    T = pz.shape[0]
    nch = T // CH // nb
    tok, in_specs = _ssd_specs(nch, rev=True)
    in_specs = in_specs + [
        _const((1024, 128)),
        pl.BlockSpec((1, 128, 1024), lambda b, c: (tok(b, c), 0, 0)),
        pl.BlockSpec((CH, 1024), lambda b, c: (tok(b, c), 1)),
        _ANY,
    ]

    def body(z_ref, xbc_ref, halo_ref, dt_ref, cw_ref, cb_ref, dtb_ref, al_ref, ds_ref, ng_ref, ex_ref, lt_ref,
             ext_ref, sall_ref, dyb_ref,
             dssd_ref, ddt_ref, dcw_ref, dcb_ref, ddtb_ref, dal_ref, dds_ref, dng_ref,
             dst_ref, dnext_ref, ddse_ref):
        b = pl.program_id(0)
        c = pl.program_id(1)

        @pl.when((b == 0) & (c == 0))
        def _():
            for r in (dcw_ref, dcb_ref, ddtb_ref, dal_ref, dds_ref, dng_ref, ddse_ref):
                r[...] = jnp.zeros_like(r)

        @pl.when(c == 0)
        def _():
            dst_ref[...] = jnp.zeros_like(dst_ref)
            dnext_ref[...] = jnp.zeros_like(dnext_ref)

        first_chunk = c == nch - 1
        halo = jnp.where(first_chunk, 0.0, halo_ref[...])
        z = z_ref[...]
        ex = ex_ref[...]
        ext = ext_ref[...]
        cw = cw_ref[...]
        ng = ng_ref[...]
        s_prev = sall_ref[0]
        f = _ssd_fwd_vals(z, xbc_ref[...], halo, dt_ref[...], cw, cb_ref[...], dtb_ref[...], al_ref[...],
                          ds_ref[...], ng, ex, lt_ref[...], s_prev)
        xs, xdt, cs, dec, dt = f["xs"], f["xdt"], f["cs"], f["dec"], f["dt"]
        dyb = dyb_ref[...]
        dyg, dngs = [], []
        for g in range(2):
            sl = slice(g * 512, (g + 1) * 512)
            dngs.append(jnp.sum(dyb[:, sl] * f["yhat"][g], axis=0, keepdims=True))
            dyg.append(_rms_bwd(dyb[:, sl], f["yhat"][g], f["rr"][g], ng[:, sl]))
        dng_ref[...] += jnp.concatenate(dngs, axis=1)
        dyg = jnp.concatenate(dyg, axis=1)
        sig_z = f["sig_z"]
        silu_z = z * sig_z
        dy = dyg * silu_z
        dz = dyg * f["ypre"] * sig_z * (1.0 + z * (1.0 - sig_z))
        ddse_ref[...] += jnp.sum(dy * xs, axis=0, keepdims=True)

        @pl.when((b == nb - 1) & (c == nch - 1))
        def _():
            dds_ref[...] = _dot_01(jnp.broadcast_to(ddse_ref[...], (8, 1024)), ext)[0:1]

        dxs = dy * f["de"]
        dye = dy * f["ecse"]
        dyeb = dye.astype(MXU)
        dst = dst_ref[...]
        dstb = dst.astype(MXU)
        bmb, cmb, sb, xdec = f["bmb"], f["cmb"], f["sb"], f["xdec"]
        u = jnp.concatenate([_dot(bmb[g], dstb[:, g * 512:(g + 1) * 512]) for g in range(2)], axis=1)
        dxdt = [u[:, q * 128:(q + 1) * 128] * f["dece"][:, q * 128:(q + 1) * 128] for q in range(8)]
        per_head = _dot_01(jnp.concatenate(
            [dy * f["yo"], u * xdt, jnp.broadcast_to(jnp.sum(dst * s_prev, axis=0, keepdims=True), (8, 1024))],
            axis=0), ext)
        dcs = per_head[0:CH]
        t = per_head[CH:2 * CH] * dec
        dcd = per_head[2 * CH:2 * CH + 1]
        row = lax.broadcasted_iota(jnp.int32, (CH, 128), 0)
        lane = lax.broadcasted_iota(jnp.int32, (CH, 128), 1)
        cd = jnp.exp(f["last"])
        dcs = dcs - t + jnp.where(row == CH - 1, jnp.sum(t, axis=0, keepdims=True) + dcd * cd, 0.0)
        dcst = jnp.zeros((128, CH), F32)
        lo = f["lo"]
        dbm, dcm, ds_prev = [], [], []
        for g in range(2):
            sl = slice(g * 512, (g + 1) * 512)
            dmg = jnp.zeros((CH, CH), F32)
            for q in range(4 * g, 4 * g + 4):
                dyq = dy[:, q * 128:(q + 1) * 128]
                xq = xdt[:, q * 128:(q + 1) * 128].astype(MXU)
                for hh in range(2):
                    h = 2 * q + hh
                    m = lo if hh == 0 else ~lo
                    dym = jnp.where(m, dyq, 0.0).astype(MXU)
                    gh = _dot_nt(dym, xq)
                    gl = gh * f["lms"][h]
                    dmg = dmg + gl
                    qh = gl * f["mg"][g]
                    dcs = dcs + jnp.where(lane == h, jnp.sum(qh, axis=1, keepdims=True), 0.0)
                    dcst = dcst - jnp.where(row == h, jnp.sum(qh, axis=0, keepdims=True), 0.0)
                    dxdt[q] = dxdt[q] + _dot_tn(f["whs"][h], dym)
            dmgb = dmg.astype(MXU)
            dcm.append(_dot(dmgb, bmb[g]) + _dot_nt(dyeb[:, sl], sb[:, sl]))
            dbm.append(_dot_tn(dmgb, cmb[g]) + _dot_nt(xdec[:, sl], dstb[:, sl]))
            ds_prev.append(_dot_tn(cmb[g], dyeb[:, sl]))
        dst_ref[...] = jnp.concatenate(ds_prev, axis=1) + dst * f["cde"]
        dcs = dcs + dcst.T
        da = _dot_hi(lt_ref[...].T, dcs)
        dxdt = jnp.concatenate(dxdt, axis=1)
        a_neg = f["a_neg"]
        ddt = da * a_neg + _dot_01(dxdt * xs, ext)
        dal_ref[...] += jnp.sum(da * dt, axis=0, keepdims=True) * a_neg
        dxs = dxs + dxdt * f["dte"]
        ddtraw = jnp.where(lane < SSD_HEADS, ddt * _sigmoid(f["dtpre"]), 0.0)
        ddtb_ref[...] += jnp.sum(ddtraw, axis=0, keepdims=True)
        ddt_ref[...] = ddtraw.astype(MXU)
        dxa = jnp.concatenate([dxs, dbm[0], dbm[1], dcm[0], dcm[1]], axis=1)
        sig_c = f["sig_c"]
        dconv = dxa * sig_c * (1.0 + f["conv"] * (1.0 - sig_c))
        dcb_ref[...] += jnp.sum(dconv, axis=0, keepdims=True)
        for k in range(4):
            dcw_ref[k:k + 1, :] += jnp.sum(dconv * f["shifts"][3 - k], axis=0, keepdims=True)
        dxbc = cw[3:4] * dconv
        for j, up in zip((1, 2, 3), _shifts_up(dconv, dnext_ref[...])):
            dxbc = dxbc + cw[3 - j:4 - j] * up
        dnext_ref[...] = dconv[0:8]
        dssd_ref[:, 0:1024] = dz.astype(MXU)
        dssd_ref[:, 1024:2560] = dxbc.astype(MXU)

    return pl.pallas_call(
        _after(15, body), grid=(nb, nch), name="ssd_bwd",
        in_specs=in_specs,
        out_specs=[pl.BlockSpec((CH, 2560), lambda b, c: (tok(b, c), 0)),
                   pl.BlockSpec((CH, 128), lambda b, c: (tok(b, c), 0)),
                   _const((8, CONV_CH)), _const((1, CONV_CH)), _const((1, 128)), _const((1, 128)), _const((1, 128)),
                   _const((1, 1024))],
        out_shape=[_sds((T, 2560), MXU), _sds((T, 128), MXU), _sds((8, CONV_CH), F32), _sds((1, CONV_CH), F32),
                   _sds((1, 128), F32), _sds((1, 128), F32), _sds((1, 128), F32), _sds((1, 1024), F32)],
        scratch_shapes=[pltpu.VMEM((128, 1024), F32), pltpu.VMEM((8, CONV_CH), F32), pltpu.VMEM((1, 1024), F32)],
        compiler_params=_cp(2),
    )(pz, pxbc, pxbc, dtraw, convw, convb, dtb, alog, dskip, ng, ex, ltri, ext, sall, dcat, after)


def _inproj_bwd(dxn_uv, dssd, ddt, wm, wdt, dh1, x, g, tm, after):
    T = x.shape[0]

    def body(dxnuv_ref, dssd_ref, ddt_ref, wm_ref, wdt_ref, dh1_ref, x_ref, g_ref, dx_ref, dg_ref):
        @pl.when(pl.program_id(0) == 0)
        def _():
            dg_ref[...] = jnp.zeros_like(dg_ref)

        dxn = (dxnuv_ref[...] + _dot_nt(dssd_ref[...], wm_ref[:, 2048:N_MAIN])
               + _dot_nt(ddt_ref[...], wdt_ref[...]))
        xh, r = _rms(x_ref[...])
        dg_ref[...] += jnp.sum(dxn * xh, axis=0, keepdims=True)
        dx_ref[...] = dh1_ref[...] + _rms_bwd(dxn, xh, r, g_ref[...])

    return pl.pallas_call(
        _after(8, body), grid=(T // tm,), name="inproj_bwd",
        in_specs=[_rows(tm, D), _rows(tm, 2560), _rows(tm, 128), _const((D, N_MAIN)), _const((D, 128)),
                  _rows(tm, D), _rows(tm, D), _const((1, D)), _ANY],
        out_specs=[_rows(tm, D), _const((1, D))],
        out_shape=[_sds((T, D), F32), _sds((1, D), F32)],
        compiler_params=_cp(),
    )(dxn_uv, dssd, ddt, wm, wdt, dh1, x, g, after)


def _matmul_tn(a, b, name, a_fn=None):
    T, M = a.shape
    N = b.shape[1]
    tm = min(M, 1024)
    tn = 1280 if N == 2560 else min(N, 1024)
    tk = min(T, 2048)

    def body(a_ref, b_ref, o_ref, acc_ref):
        k = pl.program_id(2)

        @pl.when(k == 0)
        def _():
            acc_ref[...] = jnp.zeros_like(acc_ref)

        av = a_ref[...]
        if a_fn is not None:
            av = a_fn(av)
        acc_ref[...] += _dot_tn(av, b_ref[...])

        @pl.when(k == T // tk - 1)
        def _():
            o_ref[...] = acc_ref[...].astype(o_ref.dtype)

    return pl.pallas_call(
        body, grid=(M // tm, N // tn, T // tk), name=name,
        in_specs=[pl.BlockSpec((tk, tm), lambda i, j, k: (k, i)), pl.BlockSpec((tk, tn), lambda i, j, k: (k, j))],
        out_specs=pl.BlockSpec((tm, tn), lambda i, j, k: (i, j)),
        out_shape=_sds((M, N), GRAD),
        scratch_shapes=[pltpu.VMEM((tm, tn), F32)],
        compiler_params=_cp(3),
    )(a, b)


def _adamw_vals(w, g, m, v):
    m = B1 * m + (1.0 - B1) * g
    v = B2 * v + (1.0 - B2) * (g * g)
    m_hat = m / (1.0 - B1 ** STEP)
    v_hat = v / (1.0 - B2 ** STEP)
    return -LR * (m_hat / (jnp.sqrt(v_hat) + ADAM_EPS) + WD * w), m, v


def _adamw(w, g, m, v, name):
    R, C = w.shape
    tr = 256 if R % 256 == 0 else R

    def body(w_ref, g_ref, m_ref, v_ref, d_ref, mo_ref, vo_ref):
        d_ref[...], mo_ref[...], vo_ref[...] = _adamw_vals(w_ref[...], g_ref[...], m_ref[...], v_ref[...])

    spec = _rows(tr, C)
    return pl.pallas_call(
        body, grid=(R // tr,), name=name,
        in_specs=[spec] * 4, out_specs=[spec] * 3, out_shape=[_sds((R, C), F32)] * 3,
        compiler_params=_cp(),
    )(w, g, m, v)


def _adamw_halves(w, own, other, m, v, name):
    R, C = w.shape
    half = R // 2
    tr = min(half, 256)
    nth = half // tr

    def body(w_ref, own_ref, oth_ref, m_ref, v_ref, g_ref, d_ref, mo_ref, vo_ref):
        mine = (pl.program_id(0) // nth) == lax.axis_index("c")
        g = jnp.where(mine, own_ref[...], oth_ref[...])
        g_ref[...] = g
        d_ref[...], mo_ref[...], vo_ref[...] = _adamw_vals(w_ref[...], g, m_ref[...], v_ref[...])

    full = _rows(tr, C)
    part = pl.BlockSpec((tr, C), lambda i: (i % nth, 0))
    return pl.pallas_call(
        body, grid=(R // tr,), name=name,
        in_specs=[full, part, part, full, full], out_specs=[full] * 4, out_shape=[_sds((R, C), F32)] * 4,
        compiler_params=_cp(),
    )(w, own, other, m, v)


def _sum_small(slots, name):
    nd, rows, C = slots.shape

    def body(s_ref, o_ref):
        acc = s_ref[0]
        for d in range(1, nd):
            acc = acc + s_ref[d]
        o_ref[...] = acc

    return pl.pallas_call(
        body, grid=(1,), name=name,
        in_specs=[_const((nd, rows, C))], out_specs=_const((rows, C)), out_shape=_sds((rows, C), F32),
        compiler_params=_cp(),
    )(slots)


def _sum_slots(slots, src, kind, shp, kh, name):
    R, C = shp
    rh = R // 2
    tr = min(rh, 256)
    nth = rh // tr
    if kind == "slab":
        src_spec = pl.BlockSpec((1, tr, C), lambda i, kh: (kh[0], kh[1] * nth + i, 0))
    elif kind == "rows":
        src_spec = pl.BlockSpec((tr, C), lambda i, kh: (kh[0] * (R // tr) + kh[1] * nth + i, 0))
    else:
        src_spec = pl.BlockSpec((tr, C), lambda i, kh: (kh[1] * nth + i, kh[0]))

    def body(kh_ref, s_ref, own_ref, o_ref):
        me = 2 * kh_ref[0] + kh_ref[1]
        acc = (own_ref[0] if kind == "slab" else own_ref[...]).astype(F32)
        for k in range(1, 8):
            acc = acc + s_ref[me ^ k].astype(F32)
        o_ref[...] = acc

    return pl.pallas_call(
        body, name=name,
        grid_spec=pltpu.PrefetchScalarGridSpec(
            num_scalar_prefetch=1, grid=(nth,),
            in_specs=[pl.BlockSpec((8, tr, C), lambda i, kh: (0, i, 0)), src_spec],
            out_specs=pl.BlockSpec((tr, C), lambda i, kh: (i, 0))),
        out_shape=_sds((rh, C), F32),
        compiler_params=_cp(),
    )(kh, slots, src)


def _assemble_w_in(slabs):
    tr = 256

    def body(s_ref, wm_ref, wdt_ref):
        full = jnp.concatenate([s_ref[k] for k in range(4)], axis=1)
        wm_ref[...] = full[:, :N_MAIN]
        wdt_ref[...] = jnp.concatenate([full[:, N_MAIN:], jnp.zeros((tr, 128 - 16), full.dtype)], axis=1)

    return pl.pallas_call(
        body, grid=(D // tr,), name="assemble_w_in",
        in_specs=[pl.BlockSpec((4, tr, 1156), lambda i: (0, i, 0))],
        out_specs=[_rows(tr, N_MAIN), _rows(tr, 128)],
        out_shape=[_sds((D, N_MAIN), slabs.dtype), _sds((D, 128), slabs.dtype)],
        compiler_params=_cp(),
    )(slabs)


def _split_dw_in(d_uv, d_ssd, d_dt):
    tr = 256

    def body(uv_ref, ssd_ref, dt_ref, o_ref):
        full = jnp.concatenate([uv_ref[...], ssd_ref[...], dt_ref[:, 0:16]], axis=1)
        for k in range(4):
            o_ref[k] = full[:, 1156 * k:1156 * (k + 1)]

    return pl.pallas_call(
        body, grid=(D // tr,), name="split_dw_in",
        in_specs=[_rows(tr, 2048), _rows(tr, 2560), _rows(tr, 128)],
        out_specs=pl.BlockSpec((4, tr, 1156), lambda i: (0, i, 0)),
        out_shape=_sds((4, D, 1156), d_uv.dtype),
        compiler_params=_cp(),
    )(d_uv, d_ssd, d_dt)


def _cast_into_slot(w, kh, name):
    R, C = w.shape
    tr = 256

    def body(kh_ref, w_ref, o_ref):
        o_ref[0] = w_ref[...].astype(BF16)

    return pl.pallas_call(
        body, name=name,
        grid_spec=pltpu.PrefetchScalarGridSpec(
            num_scalar_prefetch=1, grid=(R // tr,),
            in_specs=[pl.BlockSpec((tr, C), lambda i, kh: (i, 0))],
            out_specs=pl.BlockSpec((1, tr, C), lambda i, kh: (kh[0], i, 0))),
        out_shape=_sds((4, R, C), BF16),
        compiler_params=_cp(),
    )(kh, w)


_ANY = pl.BlockSpec(memory_space=pl.ANY)
_CHIP_FLIPS = [(1, 0), (0, 1), (1, 1)]
_DEVICE_FLIPS = [(fx, fy, fc) for fx in (0, 1) for fy in (0, 1) for fc in (0, 1)][1:]


def _half(h, rows):
    return pl.ds(pl.multiple_of(h * rows, rows), rows)


def _remote(src, dst, ssem, rsem, to):
    return pltpu.make_async_remote_copy(src_ref=src, dst_ref=dst, send_sem=ssem, recv_sem=rsem,
                                        device_id=to, device_id_type=MESH)


def _weight_gather(bufs, conv):
    n = len(bufs)

    def body(*refs):
        conv_ref, outs, conv_out = refs[n], refs[n + 1:2 * n + 1], refs[2 * n + 1]
        send_sems, recv_sems, fsend_sems, frecv_sems, csend_sems, crecv_sems, local_sem = refs[2 * n + 2:]
        x, y, c = lax.axis_index("x"), lax.axis_index("y"), lax.axis_index("c")
        me = 2 * x + y
        halves = [_half(c, r.shape[1] // 2) for r in outs]
        others = [_half(1 - c, r.shape[1] // 2) for r in outs]
        remote = _remote
        local = [pltpu.make_async_copy(conv_ref, conv_out.at[me], local_sem)]
        for cp in local:
            cp.start()
        sends = []
        for k, (fx, fy) in enumerate(_CHIP_FLIPS):
            peer = (x ^ fx, y ^ fy, c)
            for i in range(n):
                mine = outs[i].at[me, halves[i]]
                sends.append(remote(mine, mine, send_sems.at[k * n + i], recv_sems.at[k * n + i], peer))
            sends.append(remote(conv_ref, conv_out.at[me], csend_sems.at[k], crecv_sems.at[k], peer))
        for cp in sends:
            cp.start()
        sibling = (x, y, 1 - c)
        forwards = []
        for k, (fx, fy) in enumerate(_CHIP_FLIPS):
            peer = (x ^ fx, y ^ fy, c)
            src = 2 * (x ^ fx) + (y ^ fy)
            for i in range(n):
                landed = outs[i].at[src, halves[i]]
                remote(landed, landed, send_sems.at[k * n + i], recv_sems.at[k * n + i], peer).wait_recv()
                fw = remote(landed, landed, fsend_sems.at[k * n + i], frecv_sems.at[k * n + i], sibling)
                fw.start()
                forwards.append(fw)
            remote(conv_out.at[src], conv_out.at[src], csend_sems.at[k], crecv_sems.at[k], peer).wait_recv()
        for k, (fx, fy) in enumerate(_CHIP_FLIPS):
            src = 2 * (x ^ fx) + (y ^ fy)
            for i in range(n):
                theirs = outs[i].at[src, others[i]]
                remote(theirs, theirs, fsend_sems.at[k * n + i], frecv_sems.at[k * n + i], sibling).wait_recv()
        for cp in sends + forwards:
            cp.wait_send()
        for cp in local:
            cp.wait()

    dma = pltpu.SemaphoreType.DMA
    return pl.pallas_call(
        body, name="weight_gather",
        in_specs=[_ANY] * (n + 1), out_specs=[_ANY] * (n + 1),
        out_shape=[_sds(b.shape, b.dtype) for b in bufs] + [_sds((4,) + conv.shape, conv.dtype)],
        input_output_aliases={i: i for i in range(n)},
        scratch_shapes=[dma((3 * n,)), dma((3 * n,)), dma((3 * n,)), dma((3 * n,)), dma((3,)), dma((3,)), dma],
    )(*bufs, conv)


def _piece(ref, kind, R, C, k, h):
    if kind == "slab":
        return ref.at[k, _half(h, R // 2), :]
    if kind == "rows":
        return ref.at[pl.ds(pl.multiple_of(k * R + h * (R // 2), R // 2), R // 2), :]
    return ref.at[_half(h, R // 2), pl.ds(pl.multiple_of(k * C, C), C)]


def _small_exchange(small):
    rs = small.shape[0]

    def body(s_ref, out_ref, send_sems, recv_sems, local_sem):
        x, y, c = lax.axis_index("x"), lax.axis_index("y"), lax.axis_index("c")
        slot = 4 * x + 2 * y + c
        own = pltpu.make_async_copy(s_ref, out_ref.at[slot], local_sem)
        own.start()
        copies = []
        for k, (fx, fy, fc) in enumerate(_DEVICE_FLIPS):
            copies.append(_remote(s_ref, out_ref.at[slot], send_sems.at[k], recv_sems.at[k], (x ^ fx, y ^ fy, c ^ fc)))
        for cp in copies:
            cp.start()
        for k, (fx, fy, fc) in enumerate(_DEVICE_FLIPS):
            theirs = out_ref.at[slot ^ (k + 1)]
            _remote(theirs, theirs, send_sems.at[k], recv_sems.at[k], (x ^ fx, y ^ fy, c ^ fc)).wait_recv()
        for cp in copies:
            cp.wait_send()
        own.wait()

    dma = pltpu.SemaphoreType.DMA
    return pl.pallas_call(
        body, name="small_exchange",
        in_specs=[_ANY], out_specs=_ANY, out_shape=_sds((8, rs, 128), F32),
        scratch_shapes=[dma((7,)), dma((7,)), dma],
    )(small)


_HBM = pl.BlockSpec(memory_space=pltpu.HBM)
_SEM = pl.BlockSpec(memory_space=pltpu.SEMAPHORE)


def _split_start(name, arrays, n_copies, plan, after=None):
    n = len(arrays)
    extra = [] if after is None else [after]

    def body(*refs):
        m = n + len(extra)
        arrs, send_sems, recv_sems, token = refs[:n], refs[m], refs[m + 1], refs[-1]
        for j, (src, dst, peer) in enumerate(plan(arrs)):
            _remote(src, dst, send_sems.at[j], recv_sems.at[j], peer).start()
        token[...] = jnp.zeros_like(token)

    dma = pltpu.SemaphoreType.DMA
    res = pl.pallas_call(
        body, name=name,
        out_shape=(dma((n_copies,)), dma((n_copies,)), *[pltpu.HBM(a.shape, a.dtype) for a in arrays],
                   _sds((8, 128), F32)),
        in_specs=[_HBM] * n + [_ANY] * len(extra),
        out_specs=(_SEM, _SEM, *[_HBM] * n, pl.BlockSpec(memory_space=pltpu.VMEM)),
        input_output_aliases={i: 2 + i for i in range(n)},
        compiler_params=pltpu.CompilerParams(has_side_effects=pltpu.SideEffectType.DATAFLOW_SIDE_EFFECTING),
    )(*[pltpu.with_memory_space_constraint(a, pltpu.HBM) for a in arrays], *extra)
    return res[0], res[1], list(res[2:2 + n]), res[-1]


def _split_wait(name, arrays, send_sems, recv_sems, plan, after):
    n = len(arrays)

    def body(*refs):
        arrs, ssems, rsems = refs[:n], refs[n], refs[n + 1]
        for j, (src, dst, peer) in enumerate(plan(arrs)):
            cp = _remote(src, dst, ssems.at[j], rsems.at[j], peer)
            cp.wait_send()
            cp.wait_recv()

    return list(pl.pallas_call(
        body, name=name,
        out_shape=tuple(pltpu.HBM(a.shape, a.dtype) for a in arrays),
        in_specs=[_HBM] * n + [_SEM, _SEM, _ANY],
        out_specs=tuple([_HBM] * n),
        input_output_aliases={i: i for i in range(n)},
        compiler_params=pltpu.CompilerParams(has_side_effects=pltpu.SideEffectType.DATAFLOW_SIDE_EFFECTING),
    )(*arrays, send_sems, recv_sems, after))


def _gather_plan(n):
    def plan(bufs):
        x, y, c = lax.axis_index("x"), lax.axis_index("y"), lax.axis_index("c")
        me = 2 * x + y
        return [(bufs[i].at[me], bufs[i].at[me], (x ^ fx, y ^ fy, c)) for fx, fy in _CHIP_FLIPS for i in range(n)]

    return plan


def _reduce_plan(specs, n_small):
    n = len(specs)

    def plan(arrs):
        x, y, c = lax.axis_index("x"), lax.axis_index("y"), lax.axis_index("c")
        slot = 4 * x + 2 * y + c
        out = []
        for fx, fy, fc in _DEVICE_FLIPS:
            peer = (x ^ fx, y ^ fy, c ^ fc)
            for i, (kind, (R, C)) in enumerate(specs):
                out.append((_piece(arrs[i], kind, R, C, 2 * peer[0] + peer[1], peer[2]), arrs[n + i].at[slot], peer))
            for s in range(n_small):
                out.append((arrs[2 * n + 2 * s], arrs[2 * n + 2 * s + 1].at[slot], peer))
        return out

    return plan


def _sibling_exchange(halves):
    n = len(halves)

    def body(*refs):
        ins, outs, send_sems, recv_sems = refs[:n], refs[n:2 * n], refs[2 * n], refs[2 * n + 1]
        sibling = (lax.axis_index("x"), lax.axis_index("y"), 1 - lax.axis_index("c"))
        copies = [pltpu.make_async_remote_copy(src_ref=ins[i], dst_ref=outs[i], send_sem=send_sems.at[i],
                                               recv_sem=recv_sems.at[i], device_id=sibling, device_id_type=MESH)
                  for i in range(n)]
        for cp in copies:
            cp.start()
        for cp in copies:
            cp.wait()

    dma = pltpu.SemaphoreType.DMA
    return pl.pallas_call(
        body, name="sibling_exchange",
        in_specs=[_ANY] * n, out_specs=[_ANY] * n,
        out_shape=[_sds(h.shape, h.dtype) for h in halves],
        scratch_shapes=[dma((n,)), dma((n,))],
    )(*halves)


_BIG = [("w_in", (1024, 1156), "slab"), ("w_out", (512, 1024), "rows"), ("w_ff1", (1024, 1024), "cols"),
        ("w_ff2", (1024, 1024), "rows"), ("w_ple_gate", (256, 1024), "rows"), ("w_ple_proj", (256, 256), "cols")]
_SMALL = [("norm_mix_g", (1, 1024)), ("gm_v_norm_g", (1, 1024)), ("gm_ws", (1, 8, 128, 128)), ("gm_bs", (1, 8, 128)),
          ("gm_out_norm_g", (1, 1024)), ("ssd_conv_w", (1, 4, 1536)), ("ssd_conv_b", (1, 1536)),
          ("ssd_dt_bias", (1, 16)), ("ssd_a_log", (1, 16)), ("ssd_d", (1, 16)), ("ssd_norm_g", (1, 1024)),
          ("norm_mlp_g", (1, 1024)), ("ple_norm_g", (1, 1024)), ("final_norm_g", (1024,))]


def _rows128(a):
    flat = a.reshape(-1)
    rows = -(-flat.shape[0] // 1024) * 8
    return jnp.pad(flat, (0, rows * 128 - flat.shape[0])).reshape(rows, 128)


def _pad_lanes(v, n=128):
    v = v.reshape(1, -1)
    return jnp.pad(v, ((0, 0), (0, n - v.shape[1])))


_SMALL_SHAPES = dict(_SMALL + [("loss", ())])
_BIG_SPECS = {n: (kind, shp) for n, shp, kind in _BIG}


class _Comm:
    def __init__(self, a, kh):
        self.a, self.kh = a, kh
        self.bufs = {n: _cast_into_slot(a[n].reshape(shp), kh, "cast_" + n) for n, shp, _ in _BIG}
        self.sent = []
        self.small_tot = {}

    def w_in(self):
        (g_win,), g_cw = self._gather_now()
        rest = [self.bufs[n] for n, _, _ in _BIG[1:]]
        plan = _gather_plan(len(rest))
        ssem, rsem, thru, token = _split_start("gather_start", rest, 3 * len(rest), plan, after=g_cw)
        self.gather = (plan, ssem, rsem, thru)
        wm, wdt = _assemble_w_in(g_win)
        return wm, wdt, jnp.concatenate([g_cw[k] for k in range(4)], axis=1), token

    def _gather_now(self):
        *bufs, g_cw = _weight_gather([self.bufs["w_in"]], self.a["ssd_conv_w"].reshape(4, 384))
        return bufs, g_cw

    def rest(self, after):
        plan, ssem, rsem, thru = self.gather
        g_wo, g_w1, g_w2, g_wg, g_wp = _split_wait("gather_wait", thru, ssem, rsem, plan, after)
        return g_wo.reshape(2048, D), g_w1, g_w2.reshape(DFF, D), g_wg.reshape(D, D), g_wp

    def send(self, tag, grads):
        big = [n for n, _, _ in _BIG if n in grads]
        small = [n for n in _SMALL_SHAPES if n in grads]
        parts = [_rows128(grads[n]) for n in small]
        rows = [s.shape[0] for s in parts]
        if not big:
            self._unpack(_sum_small(_small_exchange(jnp.concatenate(parts, axis=0)), "sum_small_" + tag), small, rows)
            return None
        srcs = [grads[n] for n in big]
        lands = [lax.empty((8, _BIG_SPECS[n][1][0] // 2, _BIG_SPECS[n][1][1]), GRAD) for n in big]
        extra = []
        if small:
            pack = jnp.concatenate(parts, axis=0)
            extra = [pack, jnp.broadcast_to(pack, (8,) + pack.shape)]
        plan = _reduce_plan([_BIG_SPECS[n] for n in big], len(extra) // 2)
        n_copies = 7 * (len(big) + len(extra) // 2)
        ssem, rsem, thru, token = _split_start("reduce_start_" + tag, srcs + lands + extra, n_copies, plan)
        self.sent.append((tag, big, small, rows, plan, ssem, rsem, thru))
        return token

    def _unpack(self, tot, names, rows):
        o = 0
        for n, r in zip(names, rows):
            shp = _SMALL_SHAPES[n]
            cnt = 1
            for s in shp:
                cnt *= s
            self.small_tot[n] = tot[o:o + r].reshape(-1)[:cnt].reshape(shp)
            o += r

    def finish(self, after):
        own = {}
        for tag, big, small, rows, plan, ssem, rsem, thru in self.sent:
            arrs = _split_wait("reduce_wait_" + tag, thru, ssem, rsem, plan, after)
            nb_ = len(big)
            for i, n in enumerate(big):
                kind, shp = _BIG_SPECS[n]
                own[n] = _sum_slots(arrs[nb_ + i], arrs[i], kind, shp, self.kh, "sum_" + n)
                after = own[n]
            if small:
                self._unpack(_sum_small(arrs[2 * nb_ + 1], "sum_small_" + tag), small, rows)
        return [own[n] for n, _, _ in _BIG], dict(self.small_tot)


def _local_step(x, p, tgt, sm, comm, nb, tm):
    wm, wdt, conv_w, token = comm.w_in()
    g_mix, gv, gout = sm["norm_mix_g"].reshape(1, D), sm["gm_v_norm_g"].reshape(1, D), sm["gm_out_norm_g"].reshape(1, D)
    ws = sm["gm_ws"].reshape(GM_HEADS, CH, CH)
    bst = jnp.pad(sm["gm_bs"].reshape(GM_HEADS, CH).T, ((0, 0), (0, 128 - GM_HEADS)))
    convw = jnp.pad(conv_w, ((0, 4), (0, 0)))
    convb = sm["ssd_conv_b"].reshape(1, CONV_CH)
    dtb, alog = _pad_lanes(sm["ssd_dt_bias"]), _pad_lanes(sm["ssd_a_log"])
    dskip = jnp.repeat(sm["ssd_d"].reshape(SSD_HEADS), SSD_P).reshape(1, 1024)
    ng, g_mlp, g_ple = sm["ssd_norm_g"].reshape(1, D), sm["norm_mlp_g"].reshape(1, D), sm["ple_norm_g"].reshape(1, D)
    gf = sm["final_norm_g"].reshape(1, D)
    head_of_lane = lax.broadcasted_iota(jnp.int32, (128, 1024), 1) // SSD_P
    ex = (lax.broadcasted_iota(jnp.int32, (128, 1024), 0) == head_of_lane).astype(BF16)
    ext = ex.T
    ltri = (lax.broadcasted_iota(jnp.int32, (CH, CH), 0) >= lax.broadcasted_iota(jnp.int32, (CH, CH), 1)).astype(F32)

    pz, pxbc, dtraw, xn, cat, uv = _inproj_gmlp(x, g_mix, wm, wdt, gv, ws, bst, gout, tm // 2, token)
    cat, sall = _ssd_fwd(pz, pxbc, dtraw, cat, convw, convb, dtb, alog, dskip, ng, ex, ltri, nb)
    wo, w1, w2, wg, wp = comm.rest(cat)
    h1, hn = _outproj(cat, wo, x, g_mlp, tm)
    hid = _ff1(hn, w1, tm)
    hp, dgl, dpe, dh2, dh2b, loss, d_gf, d_gple = _ff2_tail(hid, w2, h1, g_ple, p, tgt, wg, wp, gf, tm // 2)

    d_wp = _matmul_tn(p, dpe, "dw_ple_proj", a_fn=lambda a: a.astype(MXU))
    d_wg = _matmul_tn(hp, dgl, "dw_ple_gate")
    d_w2 = _matmul_tn(hid, dh2b, "dw_ff2", a_fn=_sq)
    dpre = _ff2_bwd(dh2b, w2, hid, tm)
    d_w1 = _matmul_tn(hn, dpre, "dw_ff1")
    token = comm.send("a", {"w_ple_proj": d_wp, "w_ple_gate": d_wg, "w_ff2": d_w2, "w_ff1": d_w1})
    dh1, dh1b, d_gmlp = _ff1_bwd(dpre, w1, dh2, h1, g_mlp, tm, token)
    dcat = _outproj_bwd(dh1b, wo, tm)
    d_wo = _matmul_tn(cat, dh1b, "dw_out")
    duv, d_gv, d_ws, d_bst, d_gout, dxn_uv = _gmlp_bwd(uv, dcat, gv, ws, bst, gout, wm)
    token = comm.send("b", {
        "w_out": d_wo, "loss": loss[0:1, 0:1], "final_norm_g": d_gf, "ple_norm_g": d_gple, "norm_mlp_g": d_gmlp,
        "gm_v_norm_g": d_gv, "gm_ws": d_ws, "gm_bs": d_bst[:, :GM_HEADS].T, "gm_out_norm_g": d_gout})
    dssd, ddt, d_cw, d_cb, d_dtb, d_al, d_ds, d_ng = _ssd_bwd(
        pz, pxbc, dtraw, sall, dcat, convw, convb, dtb, alog, dskip, ng, ex, ltri, ext, nb, token)
    d_win = _split_dw_in(_matmul_tn(xn, duv, "dw_in_uv"), _matmul_tn(xn, dssd, "dw_in_ssd"),
                         _matmul_tn(xn, ddt, "dw_in_dt"))
    token = comm.send("c", {"w_in": d_win})
    dx, d_gmix = _inproj_bwd(dxn_uv, dssd, ddt, wm, wdt, dh1, x, g_mix, tm, token)
    comm.send("d", {"norm_mix_g": d_gmix, "ssd_conv_w": d_cw[0:4], "ssd_conv_b": d_cb, "ssd_dt_bias": d_dtb[:, :16],
                    "ssd_a_log": d_al[:, :16], "ssd_d": d_ds[:, :16], "ssd_norm_g": d_ng})
    return dx


def kernel(x, p, norm_mix_g, w_in, gm_v_norm_g, gm_ws, gm_bs, gm_out_norm_g, ssd_conv_w, ssd_conv_b, ssd_dt_bias, ssd_a_log, ssd_d, ssd_norm_g, w_out, norm_mlp_g, w_ff1, w_ff2, ple_norm_g, w_ple_gate, w_ple_proj, final_norm_g, loss_target, m_norm_mix_g, m_w_in, m_gm_v_norm_g, m_gm_ws, m_gm_bs, m_gm_out_norm_g, m_ssd_conv_w, m_ssd_conv_b, m_ssd_dt_bias, m_ssd_a_log, m_ssd_d, m_ssd_norm_g, m_w_out, m_norm_mlp_g, m_w_ff1, m_w_ff2, m_ple_norm_g, m_w_ple_gate, m_w_ple_proj, m_final_norm_g, v_norm_mix_g, v_w_in, v_gm_v_norm_g, v_gm_ws, v_gm_bs, v_gm_out_norm_g, v_ssd_conv_w, v_ssd_conv_b, v_ssd_dt_bias, v_ssd_a_log, v_ssd_d, v_ssd_norm_g, v_w_out, v_norm_mlp_g, v_w_ff1, v_w_ff2, v_ple_norm_g, v_w_ple_gate, v_w_ple_proj, v_final_norm_g):
    a = dict(locals())
    order = ["norm_mix_g", "w_in", "gm_v_norm_g", "gm_ws", "gm_bs", "gm_out_norm_g", "ssd_conv_w", "ssd_conv_b",
             "ssd_dt_bias", "ssd_a_log", "ssd_d", "ssd_norm_g", "w_out", "norm_mlp_g", "w_ff1", "w_ff2", "ple_norm_g",
             "w_ple_gate", "w_ple_proj", "final_norm_g"]
    chip = 2 * lax.axis_index("x") + lax.axis_index("y")
    nb, S = x.shape[0], x.shape[1]
    T = nb * S
    sm = {n: a[n] for n, _ in _SMALL if n != "ssd_conv_w"}
    comm = _Comm(a, jnp.stack([chip, lax.axis_index("c")]).astype(jnp.int32))
    dx = _local_step(x.reshape(T, D), p.reshape(T, DPLE), loss_target.reshape(T, D), sm, comm, nb, 512)
    own, g_out = comm.finish(dx)
    other = _sibling_exchange(own)

    delta, new_m, new_v = {}, {}, {}
    for i, (n, shp, _) in enumerate(_BIG):
        res = _adamw_halves(a[n].reshape(shp), own[i], other[i], a["m_" + n].reshape(shp), a["v_" + n].reshape(shp),
                            "adamw_" + n)
        g_out[n], delta[n], new_m[n], new_v[n] = (r.reshape(a[n].shape) for r in res)
    g_out["ssd_conv_w"] = lax.dynamic_slice(g_out["ssd_conv_w"], (0, 0, chip * 384), (1, 4, 384))
    small_names = [n for n, _ in _SMALL]
    packs = [jnp.concatenate([_rows128(src(n)) for n in small_names], axis=0)
             for src in (lambda n: a[n], lambda n: g_out[n], lambda n: a["m_" + n], lambda n: a["v_" + n])]
    outs = _adamw(*packs, "adamw_small")
    o = 0
    for n in small_names:
        r = _rows128(a[n]).shape[0]
        cnt = a[n].size
        for dst, src in zip((delta, new_m, new_v), outs):
            dst[n] = src[o:o + r].reshape(-1)[:cnt].reshape(a[n].shape)
        o += r
    return (g_out["loss"], dx.reshape(x.shape), *[g_out[n] for n in order], *[delta[n] for n in order],
            *[new_m[n] for n in order], *[new_v[n] for n in order])
```

```python
import jax
import jax.numpy as jnp
from jax import lax
from jax.experimental import pallas as pl
from jax.experimental.pallas import tpu as pltpu

F32 = jnp.float32
BF16 = jnp.bfloat16
MXU = jnp.bfloat16
GRAD = jnp.bfloat16

D = 1024
CH = 128
GM_HEADS = 8
SSD_HEADS = 16
SSD_P = 64
CONV_CH = 1536
N_MAIN = 4608
DFF = 4096
DPLE = 256
EPS = 1e-6
NEG = -1e30

LR, B1, B2, ADAM_EPS, WD, STEP = 0.001, 0.9, 0.999, 1e-08, 0.01, 10

VMEM_LIMIT = 56 * 1024 * 1024
MESH = pl.DeviceIdType.MESH

INV_SQRT2 = 0.7071067811865476
INV_SQRT_2PI = 0.3989422804014327


def _cp(n_axes=1):
    return pltpu.CompilerParams(dimension_semantics=("arbitrary",) * n_axes, vmem_limit_bytes=VMEM_LIMIT)


def _dot(a, b):
    return jnp.dot(a, b, preferred_element_type=F32)


def _dot_nt(a, b):
    return lax.dot_general(a, b, (((1,), (1,)), ((), ())), preferred_element_type=F32)


def _dot_tn(a, b):
    return lax.dot_general(a, b, (((0,), (0,)), ((), ())), preferred_element_type=F32)


def _dot_hi(a, b):
    return jnp.dot(a, b, preferred_element_type=F32, precision=lax.Precision.HIGHEST)


def _dot_01(a, sel):
    hi = a.astype(BF16)
    lo = (a - hi.astype(F32)).astype(BF16)
    n = a.shape[0]
    r = _dot(jnp.concatenate([hi, lo], axis=0), sel)
    return r[0:n] + r[n:2 * n]


def _rows(tm, n, j=0):
    return pl.BlockSpec((tm, n), lambda i: (i, j))


def _const(shape):
    nd = len(shape)
    return pl.BlockSpec(shape, lambda *_: (0,) * nd)


def _sds(shape, dtype):
    return jax.ShapeDtypeStruct(shape, dtype)


def _rms(x):
    r = lax.rsqrt(jnp.mean(x * x, axis=-1, keepdims=True) + EPS)
    return x * r, r


def _rms_bwd(dy, xhat, r, g):
    dyg = dy * g
    return r * (dyg - xhat * jnp.mean(dyg * xhat, axis=-1, keepdims=True))


def _sigmoid(x):
    return 1.0 / (1.0 + jnp.exp(-x))


def _gelu(x):
    cdf = 0.5 * (1.0 + lax.erf(x * INV_SQRT2))
    pdf = jnp.exp(-0.5 * x * x) * INV_SQRT_2PI
    return x * cdf, cdf + x * pdf


def _softplus(x):
    e = jnp.exp(-jnp.abs(x))
    u = 1.0 + e
    log1p = jnp.where(u == 1.0, e, jnp.log(u) * e / (u - 1.0))
    return jnp.maximum(x, 0.0) + log1p


def _after(n_in, fn):
    def body(*refs):
        return fn(*refs[:n_in], *refs[n_in + 1:])

    return body


def _inproj_gmlp(x, g, wm, wdt, gv, ws, bst, gout, tm, after):
    T = x.shape[0]

    def body(x_ref, g_ref, wm_ref, wdt_ref, gv_ref, ws_ref, bst_ref, gout_ref,
             z_ref, xbc_ref, dt_ref, xn_ref, ya_ref, uv_ref):
        xh, _ = _rms(x_ref[...])
        xn = (xh * g_ref[...]).astype(MXU)
        xn_ref[...] = xn
        for n in range(4):
            uv_ref[:, n * 512:(n + 1) * 512] = _dot(xn, wm_ref[:, n * 512:(n + 1) * 512])
        for n in range(2):
            z_ref[:, n * 512:(n + 1) * 512] = _dot(xn, wm_ref[:, 2048 + n * 512:2048 + (n + 1) * 512])
        for n in range(3):
            xbc_ref[:, n * 512:(n + 1) * 512] = _dot(xn, wm_ref[:, 3072 + n * 512:3072 + (n + 1) * 512])
        dt_ref[...] = _dot(xn, wdt_ref[...])
        for k in range(tm // CH):
            rows = slice(k * CH, (k + 1) * CH)
            f = _gmlp_fwd_vals(uv_ref[rows, 0:1024], uv_ref[rows, 1024:2048], gv_ref[...], ws_ref, bst_ref[...],
                               gout_ref[...])
            ya_ref[rows, :] = f["out"].astype(MXU)

    return pl.pallas_call(
        _after(8, body), grid=(T // tm,), name="inproj_gmlp",
        in_specs=[_rows(tm, D), _const((1, D)), _const((D, N_MAIN)), _const((D, 128)), _const((1, 1024)),
                  _const((GM_HEADS, CH, CH)), _const((CH, 128)), _const((1, 1024)), _ANY],
        out_specs=[_rows(tm, 1024), _rows(tm, CONV_CH), _rows(tm, 128), _rows(tm, D), _rows(tm, 1024, 0),
                   _rows(tm, 2048)],
        out_shape=[_sds((T, 1024), F32), _sds((T, CONV_CH), F32), _sds((T, 128), F32), _sds((T, D), MXU),
                   _sds((T, 2048), MXU), _sds((T, 2048), F32)],
        compiler_params=_cp(),
    )(x, g, wm, wdt, gv, ws, bst, gout, after)


def _gmlp_fwd_vals(u, v, gv, ws_ref, bst, gout):
    ug, dug = _gelu(u)
    vg, dvg = _gelu(v)
    row = lax.broadcasted_iota(jnp.int32, (CH, CH), 0)
    col = lax.broadcasted_iota(jnp.int32, (CH, CH), 1)
    tril = row >= col
    ys, heads = [], []
    for h in range(GM_HEADS):
        sl = slice(h * 128, (h + 1) * 128)
        vhat, rv = _rms(vg[:, sl])
        vn = (vhat * gv[:, sl]).astype(MXU)
        wt = jnp.where(tril, ws_ref[h], 0.0)
        mixed = _dot(wt.astype(MXU), vn) + bst[:, h:h + 1]
        ys.append(ug[:, sl] * mixed)
        heads.append((vhat, rv, vn, wt, mixed))
    y = jnp.concatenate(ys, axis=1)
    yhat, ry = _rms(y)
    return dict(ug=ug, dug=dug, dvg=dvg, heads=heads, yhat=yhat, ry=ry, tril=tril, out=yhat * gout)


def _shifts_down(cur, halo):
    row8 = lax.broadcasted_iota(jnp.int32, (8, cur.shape[1]), 0)
    out = [cur]
    for j in (1, 2, 3):
        sh = pltpu.roll(cur, j, 0)
        top = jnp.where(row8 < j, pltpu.roll(halo, j, 0), sh[0:8])
        out.append(jnp.concatenate([top, sh[8:]], axis=0))
    return out


def _shifts_up(cur, halo):
    row8 = lax.broadcasted_iota(jnp.int32, (8, cur.shape[1]), 0)
    out = []
    for j in (1, 2, 3):
        sh = pltpu.roll(cur, CH - j, 0)
        bot = jnp.where(row8 + j >= 8, pltpu.roll(halo, 8 - j, 0), sh[CH - 8:CH])
        out.append(jnp.concatenate([sh[0:CH - 8], bot], axis=0))
    return out


def _ssd_fwd_vals(z, xbc, halo, dtraw, convw, convb, dtb, alog, dskip, ng, ex, ltri, s_prev):
    shifts = _shifts_down(xbc, halo)
    conv = convb + convw[3:4] * shifts[0] + convw[2:3] * shifts[1] + convw[1:2] * shifts[2] + convw[0:1] * shifts[3]
    sig_c = _sigmoid(conv)
    xa = conv * sig_c
    xs = xa[:, :1024]
    bm = [xa[:, 1024:1152], xa[:, 1152:1280]]
    cm = [xa[:, 1280:1408], xa[:, 1408:1536]]
    dtpre = dtraw + dtb
    dt = _softplus(dtpre)
    a_neg = -jnp.exp(alog)
    cs = _dot_hi(ltri, dt * a_neg)
    cst = cs.T
    last = cs[CH - 1:CH]
    ecs = jnp.exp(cs)
    dec = jnp.exp(last - cs)
    spread = _dot_01(jnp.concatenate([dt, ecs, dec], axis=0), ex)
    dte, ecse, dece = spread[0:CH], spread[CH:2 * CH], spread[2 * CH:3 * CH]
    cde = ecse[CH - 1:CH]
    de = dskip
    xdt = xs * dte
    row = lax.broadcasted_iota(jnp.int32, (CH, CH), 0)
    col = lax.broadcasted_iota(jnp.int32, (CH, CH), 1)
    tril = row >= col
    lo = col < SSD_P
    bmb = [b.astype(MXU) for b in bm]
    cmb = [c.astype(MXU) for c in cm]
    mg = [_dot_nt(cmb[g], bmb[g]) for g in range(2)]
    yd, lms, whs = [], [], []
    for q in range(8):
        g = q // 4
        xq = xdt[:, q * 128:(q + 1) * 128]
        acc = None
        for hh in range(2):
            h = 2 * q + hh
            seg = cs[:, h:h + 1] - cst[h:h + 1, :]
            lm = jnp.exp(jnp.where(tril, seg, NEG))
            wh = (mg[g] * lm).astype(MXU)
            xm = jnp.where(lo if hh == 0 else ~lo, xq, 0.0).astype(MXU)
            part = _dot(wh, xm)
            acc = part if acc is None else acc + part
            lms.append(lm)
            whs.append(wh)
        yd.append(acc)
    yd = jnp.concatenate(yd, axis=1)
    sb = s_prev.astype(MXU)
    yo = jnp.concatenate([_dot(cmb[g], sb[:, g * 512:(g + 1) * 512]) for g in range(2)], axis=1) * ecse
    xdec = (xdt * dece).astype(MXU)
    states = jnp.concatenate([_dot_tn(bmb[g], xdec[:, g * 512:(g + 1) * 512]) for g in range(2)], axis=1)
    s_next = s_prev * cde + states
    ypre = yd + yo + de * xs
    sig_z = _sigmoid(z)
    yg = ypre * z * sig_z
    outs, yhat, rr = [], [], []
    for g in range(2):
        sl = slice(g * 512, (g + 1) * 512)
        yh, r = _rms(yg[:, sl])
        yhat.append(yh)
        rr.append(r)
        outs.append(yh * ng[:, sl])
    return dict(shifts=shifts, conv=conv, sig_c=sig_c, xs=xs, bmb=bmb, cmb=cmb, dtpre=dtpre, dt=dt, a_neg=a_neg,
                cs=cs, last=last, ecs=ecs, dec=dec, dte=dte, ecse=ecse, dece=dece, cde=cde, de=de, xdt=xdt,
                mg=mg, lms=lms, whs=whs, lo=lo, yo=yo, sb=sb, xdec=xdec, s_next=s_next, ypre=ypre, sig_z=sig_z,
                yhat=yhat, rr=rr, out=jnp.concatenate(outs, axis=1))


def _ssd_specs(nch, rev):
    def tok(b, c):
        return b * nch + ((nch - 1 - c) if rev else c)

    return tok, [
        pl.BlockSpec((CH, 1024), lambda b, c: (tok(b, c), 0)),
        pl.BlockSpec((CH, CONV_CH), lambda b, c: (tok(b, c), 0)),
        pl.BlockSpec((8, CONV_CH), lambda b, c: (jnp.maximum(tok(b, c) * (CH // 8) - 1, 0), 0)),
        pl.BlockSpec((CH, 128), lambda b, c: (tok(b, c), 0)),
        _const((8, CONV_CH)), _const((1, CONV_CH)), _const((1, 128)), _const((1, 128)), _const((1, 1024)),
        _const((1, 1024)), _const((128, 1024)), _const((CH, CH)),
    ]


def _ssd_fwd(pz, pxbc, dtraw, cat, convw, convb, dtb, alog, dskip, ng, ex, ltri, nb):
    T = pz.shape[0]
    nch = T // CH // nb
    tok, in_specs = _ssd_specs(nch, rev=False)

    def body(z_ref, xbc_ref, halo_ref, dt_ref, cw_ref, cb_ref, dtb_ref, al_ref, ds_ref, ng_ref, ex_ref, lt_ref,
             cat_in_ref, yb_ref, sall_ref, s_ref):
        del cat_in_ref
        c = pl.program_id(1)

        @pl.when(c == 0)
        def _():
            s_ref[...] = jnp.zeros_like(s_ref)

        halo = jnp.where(c == 0, 0.0, halo_ref[...])
        s_prev = s_ref[...]
        sall_ref[0] = s_prev
        f = _ssd_fwd_vals(z_ref[...], xbc_ref[...], halo, dt_ref[...], cw_ref[...], cb_ref[...], dtb_ref[...],
                          al_ref[...], ds_ref[...], ng_ref[...], ex_ref[...], lt_ref[...], s_prev)
        s_ref[...] = f["s_next"]
        yb_ref[...] = f["out"].astype(MXU)

    return pl.pallas_call(
        body, grid=(nb, nch), name="ssd_fwd",
        in_specs=in_specs + [_ANY],
        out_specs=[pl.BlockSpec((CH, 1024), lambda b, c: (tok(b, c), 1)),
                   pl.BlockSpec((1, 128, 1024), lambda b, c: (tok(b, c), 0, 0))],
        out_shape=[_sds((T, 2048), MXU), _sds((T // CH, 128, 1024), F32)],
        scratch_shapes=[pltpu.VMEM((128, 1024), F32)],
        input_output_aliases={12: 0},
        compiler_params=_cp(2),
    )(pz, pxbc, pxbc, dtraw, convw, convb, dtb, alog, dskip, ng, ex, ltri, cat)


def _outproj(cat, wo, x, g, tm):
    T = x.shape[0]

    def body(cat_ref, wo_ref, x_ref, g_ref, h1_ref, hn_ref):
        h1 = x_ref[...] + _dot(cat_ref[...], wo_ref[...])
        h1_ref[...] = h1
        hn_ref[...] = (_rms(h1)[0] * g_ref[...]).astype(MXU)

    return pl.pallas_call(
        body, grid=(T // tm,), name="outproj",
        in_specs=[_rows(tm, 2048), _const((2048, D)), _rows(tm, D), _const((1, D))],
        out_specs=[_rows(tm, D), _rows(tm, D)],
        out_shape=[_sds((T, D), F32), _sds((T, D), MXU)],
        compiler_params=_cp(),
    )(cat, wo, x, g)


def _ff1(hn, w1, tm):
    T = hn.shape[0]

    def body(hn_ref, w1_ref, hid_ref):
        hn_v = hn_ref[...]
        for n in range(4):
            hid_ref[:, n * 1024:(n + 1) * 1024] = jnp.maximum(_dot(hn_v, w1_ref[n]), 0.0).astype(MXU)

    return pl.pallas_call(
        body, grid=(T // tm,), name="ff1",
        in_specs=[_rows(tm, D), _const((4, D, 1024))],
        out_specs=_rows(tm, DFF),
        out_shape=_sds((T, DFF), MXU),
        compiler_params=_cp(),
    )(hn, w1)


def _sq(hid):
    h = hid.astype(F32)
    return (h * h).astype(MXU)


def _ff2_tail(hid, w2, h1, g_ple, p, tgt, wg, wp, gf, tm):
    T = h1.shape[0]

    def body(hid_ref, w2_ref, h1_ref, g_ref, p_ref, t_ref, wg_ref, wp_ref, gf_ref,
             hp_ref, dgl_ref, dpe_ref, dh2_ref, dh2b_ref, loss_ref, dgf_ref, dg_ref):
        @pl.when(pl.program_id(0) == 0)
        def _():
            loss_ref[...] = jnp.zeros_like(loss_ref)
            dgf_ref[...] = jnp.zeros_like(dgf_ref)
            dg_ref[...] = jnp.zeros_like(dg_ref)

        h2 = h1_ref[...] + _dot(_sq(hid_ref[...]), w2_ref[...])
        h2h, r2 = _rms(h2)
        g_ple = g_ref[...]
        hp = (h2h * g_ple).astype(MXU)
        hp_ref[...] = hp
        gate = _sigmoid(_dot(hp, wg_ref[...]))
        pb = p_ref[...].astype(MXU)
        pe = jnp.concatenate([_dot(pb, wp_ref[k]) for k in range(4)], axis=1)
        h3 = h2 + gate * pe
        hh, r = _rms(h3)
        gf = gf_ref[...]
        diff = hh * gf - t_ref[...]
        loss_ref[...] += 0.5 * jnp.sum(jnp.mean(diff * diff, axis=-1, keepdims=True))
        dout = diff * (1.0 / D)
        dgf_ref[...] += jnp.sum(dout * hh, axis=0, keepdims=True)
        dh3 = _rms_bwd(dout, hh, r, gf)
        dgl = (dh3 * pe * gate * (1.0 - gate)).astype(MXU)
        dgl_ref[...] = dgl
        dpe_ref[...] = (dh3 * gate).astype(MXU)
        dhp = _dot_nt(dgl, wg_ref[...])
        dg_ref[...] += jnp.sum(dhp * h2h, axis=0, keepdims=True)
        dh2 = dh3 + _rms_bwd(dhp, h2h, r2, g_ple)
        dh2_ref[...] = dh2
        dh2b_ref[...] = dh2.astype(MXU)

    return pl.pallas_call(
        body, grid=(T // tm,), name="ff2_tail",
        in_specs=[_rows(tm, DFF), _const((DFF, D)), _rows(tm, D), _const((1, D)), _rows(tm, DPLE), _rows(tm, D),
                  _const((D, D)), _const((4, DPLE, 256)), _const((1, D))],
        out_specs=[_rows(tm, D), _rows(tm, D), _rows(tm, D), _rows(tm, D), _rows(tm, D), _const((8, 128)),
                   _const((1, D)), _const((1, D))],
        out_shape=[_sds((T, D), MXU), _sds((T, D), MXU), _sds((T, D), MXU), _sds((T, D), F32), _sds((T, D), MXU),
                   _sds((8, 128), F32), _sds((1, D), F32), _sds((1, D), F32)],
        compiler_params=_cp(),
    )(hid, w2, h1, g_ple, p, tgt, wg, wp, gf)


def _ff2_bwd(dh2b, w2, hid, tm):
    T = hid.shape[0]

    def body(dh2b_ref, w2_ref, hid_ref, dpre_ref):
        d = dh2b_ref[...]
        for n in range(DFF // 1024):
            sl = slice(n * 1024, (n + 1) * 1024)
            da = _dot_nt(d, w2_ref[sl, :])
            dpre_ref[:, sl] = (2.0 * da * hid_ref[:, sl].astype(F32)).astype(MXU)

    return pl.pallas_call(
        body, grid=(T // tm,), name="ff2_bwd",
        in_specs=[_rows(tm, D), _const((DFF, D)), _rows(tm, DFF)],
        out_specs=_rows(tm, DFF),
        out_shape=_sds((T, DFF), MXU),
        compiler_params=_cp(),
    )(dh2b, w2, hid)


def _ff1_bwd(dpre, w1, dh2, h1, g, tm, after):
    T = h1.shape[0]

    def body(dpre_ref, w1_ref, dh2_ref, h1_ref, g_ref, dh1_ref, dh1b_ref, dg_ref):
        @pl.when(pl.program_id(0) == 0)
        def _():
            dg_ref[...] = jnp.zeros_like(dg_ref)

        dhn = _dot_nt(dpre_ref[:, 0:1024], w1_ref[0])
        for k in range(1, 4):
            dhn = dhn + _dot_nt(dpre_ref[:, k * 1024:(k + 1) * 1024], w1_ref[k])
        hh, r = _rms(h1_ref[...])
        dg_ref[...] += jnp.sum(dhn * hh, axis=0, keepdims=True)
        dh1 = dh2_ref[...] + _rms_bwd(dhn, hh, r, g_ref[...])
        dh1_ref[...] = dh1
        dh1b_ref[...] = dh1.astype(MXU)

    return pl.pallas_call(
        _after(5, body), grid=(T // tm,), name="ff1_bwd",
        in_specs=[_rows(tm, DFF), _const((4, D, 1024)), _rows(tm, D), _rows(tm, D), _const((1, D)), _ANY],
        out_specs=[_rows(tm, D), _rows(tm, D), _const((1, D))],
        out_shape=[_sds((T, D), F32), _sds((T, D), MXU), _sds((1, D), F32)],
        compiler_params=_cp(),
    )(dpre, w1, dh2, h1, g, after)


def _outproj_bwd(dh1b, wo, tm):
    T = dh1b.shape[0]

    def body(d_ref, wo_ref, dcat_ref):
        d = d_ref[...]
        dcat_ref[:, 0:1024] = _dot_nt(d, wo_ref[0:1024, :])
        dcat_ref[:, 1024:2048] = _dot_nt(d, wo_ref[1024:2048, :])

    return pl.pallas_call(
        body, grid=(T // tm,), name="outproj_bwd",
        in_specs=[_rows(tm, D), _const((2048, D))],
        out_specs=_rows(tm, 2048),
        out_shape=_sds((T, 2048), F32),
        compiler_params=_cp(),
    )(dh1b, wo)


def _gmlp_bwd(uv, dcat, gv, ws, bst, gout, wm):
    T = uv.shape[0]
    nck = 2 if T % (2 * CH) == 0 else 1
    tb = nck * CH

    def body(uv_ref, dya_ref, gv_ref, ws_ref, bst_ref, gout_ref, wuv_ref, duv_ref, dgv_ref, dws_ref, dbst_ref,
             dgo_ref, dxn_ref):
        @pl.when(pl.program_id(0) == 0)
        def _():
            dgv_ref[...] = jnp.zeros_like(dgv_ref)
            dws_ref[...] = jnp.zeros_like(dws_ref)
            dbst_ref[...] = jnp.zeros_like(dbst_ref)
            dgo_ref[...] = jnp.zeros_like(dgo_ref)

        for k in range(nck):
            chunk(slice(k * CH, (k + 1) * CH), uv_ref, dya_ref, gv_ref, ws_ref, bst_ref, gout_ref, duv_ref,
                  dgv_ref, dws_ref, dbst_ref, dgo_ref)
        dxn_ref[...] = _dot_nt(duv_ref[...], wuv_ref[...])

    def chunk(rows, uv_ref, dya_ref, gv_ref, ws_ref, bst_ref, gout_ref, duv_ref, dgv_ref, dws_ref, dbst_ref,
              dgo_ref):
        gv = gv_ref[...]
        f = _gmlp_fwd_vals(uv_ref[rows, 0:1024], uv_ref[rows, 1024:2048], gv, ws_ref, bst_ref[...], gout_ref[...])
        dya = dya_ref[rows, :]
        dgo_ref[...] += jnp.sum(dya * f["yhat"], axis=0, keepdims=True)
        dy = _rms_bwd(dya, f["yhat"], f["ry"], gout_ref[...])
        lane = lax.broadcasted_iota(jnp.int32, (CH, 128), 1)
        dbs = jnp.zeros((CH, 128), F32)
        dug, dvg, dgvs = [], [], []
        for h in range(GM_HEADS):
            sl = slice(h * 128, (h + 1) * 128)
            vhat, rv, vn, wt, mixed = f["heads"][h]
            dyh = dy[:, sl]
            dug.append(dyh * mixed)
            dmixed = dyh * f["ug"][:, sl]
            dmb = dmixed.astype(MXU)
            dws_ref[h] += jnp.where(f["tril"], _dot_nt(dmb, vn), 0.0)
            dbs = dbs + jnp.where(lane == h, jnp.sum(dmixed, axis=1, keepdims=True), 0.0)
            dvn = _dot_tn(wt.astype(MXU), dmb)
            dgvs.append(jnp.sum(dvn * vhat, axis=0, keepdims=True))
            dvg.append(_rms_bwd(dvn, vhat, rv, gv[:, sl]))
        dbst_ref[...] += dbs
        dgv_ref[...] += jnp.concatenate(dgvs, axis=1)
        duv_ref[rows, 0:1024] = (jnp.concatenate(dug, axis=1) * f["dug"]).astype(MXU)
        duv_ref[rows, 1024:2048] = (jnp.concatenate(dvg, axis=1) * f["dvg"]).astype(MXU)

    return pl.pallas_call(
        body, grid=(T // tb,), name="gmlp_bwd",
        in_specs=[_rows(tb, 2048), _rows(tb, 1024, 0), _const((1, 1024)),
                  _const((GM_HEADS, CH, CH)), _const((CH, 128)), _const((1, 1024)), _const((D, 2048))],
        out_specs=[_rows(tb, 2048), _const((1, 1024)), _const((GM_HEADS, CH, CH)), _const((CH, 128)),
                   _const((1, 1024)), _rows(tb, D)],
        out_shape=[_sds((T, 2048), MXU), _sds((1, 1024), F32), _sds((GM_HEADS, CH, CH), F32), _sds((CH, 128), F32),
                   _sds((1, 1024), F32), _sds((T, D), F32)],
        compiler_params=_cp(),
    )(uv, dcat, gv, ws, bst, gout, wm)


def _ssd_bwd(pz, pxbc, dtraw, sall, dcat, convw, convb, dtb, alog, dskip, ng, ex, ltri, ext, nb, after):
    T = pz.shape[0]
    nch = T // CH // nb
    tok, in_specs = _ssd_specs(nch, rev=True)
    in_specs = in_specs + [
        _const((1024, 128)),
        pl.BlockSpec((1, 128, 1024), lambda b, c: (tok(b, c), 0, 0)),
        pl.BlockSpec((CH, 1024), lambda b, c: (tok(b, c), 1)),
        _ANY,
    ]

    def body(z_ref, xbc_ref, halo_ref, dt_ref, cw_ref, cb_ref, dtb_ref, al_ref, ds_ref, ng_ref, ex_ref, lt_ref,
             ext_ref, sall_ref, dyb_ref,
             dssd_ref, ddt_ref, dcw_ref, dcb_ref, ddtb_ref, dal_ref, dds_ref, dng_ref,
             dst_ref, dnext_ref, ddse_ref):
        b = pl.program_id(0)
        c = pl.program_id(1)

        @pl.when((b == 0) & (c == 0))
        def _():
            for r in (dcw_ref, dcb_ref, ddtb_ref, dal_ref, dds_ref, dng_ref, ddse_ref):
                r[...] = jnp.zeros_like(r)

        @pl.when(c == 0)
        def _():
            dst_ref[...] = jnp.zeros_like(dst_ref)
            dnext_ref[...] = jnp.zeros_like(dnext_ref)

        first_chunk = c == nch - 1
        halo = jnp.where(first_chunk, 0.0, halo_ref[...])
        z = z_ref[...]
        ex = ex_ref[...]
        ext = ext_ref[...]
        cw = cw_ref[...]
        ng = ng_ref[...]
        s_prev = sall_ref[0]
        f = _ssd_fwd_vals(z, xbc_ref[...], halo, dt_ref[...], cw, cb_ref[...], dtb_ref[...], al_ref[...],
                          ds_ref[...], ng, ex, lt_ref[...], s_prev)
        xs, xdt, cs, dec, dt = f["xs"], f["xdt"], f["cs"], f["dec"], f["dt"]
        dyb = dyb_ref[...]
        dyg, dngs = [], []
        for g in range(2):
            sl = slice(g * 512, (g + 1) * 512)
            dngs.append(jnp.sum(dyb[:, sl] * f["yhat"][g], axis=0, keepdims=True))
            dyg.append(_rms_bwd(dyb[:, sl], f["yhat"][g], f["rr"][g], ng[:, sl]))
        dng_ref[...] += jnp.concatenate(dngs, axis=1)
        dyg = jnp.concatenate(dyg, axis=1)
        sig_z = f["sig_z"]
        silu_z = z * sig_z
        dy = dyg * silu_z
        dz = dyg * f["ypre"] * sig_z * (1.0 + z * (1.0 - sig_z))
        ddse_ref[...] += jnp.sum(dy * xs, axis=0, keepdims=True)

        @pl.when((b == nb - 1) & (c == nch - 1))
        def _():
            dds_ref[...] = _dot_01(jnp.broadcast_to(ddse_ref[...], (8, 1024)), ext)[0:1]

        dxs = dy * f["de"]
        dye = dy * f["ecse"]
        dyeb = dye.astype(MXU)
        dst = dst_ref[...]
        dstb = dst.astype(MXU)
        bmb, cmb, sb, xdec = f["bmb"], f["cmb"], f["sb"], f["xdec"]
        u = jnp.concatenate([_dot(bmb[g], dstb[:, g * 512:(g + 1) * 512]) for g in range(2)], axis=1)
        dxdt = [u[:, q * 128:(q + 1) * 128] * f["dece"][:, q * 128:(q + 1) * 128] for q in range(8)]
        per_head = _dot_01(jnp.concatenate(
            [dy * f["yo"], u * xdt, jnp.broadcast_to(jnp.sum(dst * s_prev, axis=0, keepdims=True), (8, 1024))],
            axis=0), ext)
        dcs = per_head[0:CH]
        t = per_head[CH:2 * CH] * dec
        dcd = per_head[2 * CH:2 * CH + 1]
        row = lax.broadcasted_iota(jnp.int32, (CH, 128), 0)
        lane = lax.broadcasted_iota(jnp.int32, (CH, 128), 1)
        cd = jnp.exp(f["last"])
        dcs = dcs - t + jnp.where(row == CH - 1, jnp.sum(t, axis=0, keepdims=True) + dcd * cd, 0.0)
        dcst = jnp.zeros((128, CH), F32)
        lo = f["lo"]
        dbm, dcm, ds_prev = [], [], []
        for g in range(2):
            sl = slice(g * 512, (g + 1) * 512)
            dmg = jnp.zeros((CH, CH), F32)
            for q in range(4 * g, 4 * g + 4):
                dyq = dy[:, q * 128:(q + 1) * 128]
                xq = xdt[:, q * 128:(q + 1) * 128].astype(MXU)
                for hh in range(2):
                    h = 2 * q + hh
                    m = lo if hh == 0 else ~lo
                    dym = jnp.where(m, dyq, 0.0).astype(MXU)
                    gh = _dot_nt(dym, xq)
                    gl = gh * f["lms"][h]
                    dmg = dmg + gl
                    qh = gl * f["mg"][g]
                    dcs = dcs + jnp.where(lane == h, jnp.sum(qh, axis=1, keepdims=True), 0.0)
                    dcst = dcst - jnp.where(row == h, jnp.sum(qh, axis=0, keepdims=True), 0.0)
                    dxdt[q] = dxdt[q] + _dot_tn(f["whs"][h], dym)
            dmgb = dmg.astype(MXU)
            dcm.append(_dot(dmgb, bmb[g]) + _dot_nt(dyeb[:, sl], sb[:, sl]))
            dbm.append(_dot_tn(dmgb, cmb[g]) + _dot_nt(xdec[:, sl], dstb[:, sl]))
            ds_prev.append(_dot_tn(cmb[g], dyeb[:, sl]))
        dst_ref[...] = jnp.concatenate(ds_prev, axis=1) + dst * f["cde"]
        dcs = dcs + dcst.T
        da = _dot_hi(lt_ref[...].T, dcs)
        dxdt = jnp.concatenate(dxdt, axis=1)
        a_neg = f["a_neg"]
        ddt = da * a_neg + _dot_01(dxdt * xs, ext)
        dal_ref[...] += jnp.sum(da * dt, axis=0, keepdims=True) * a_neg
        dxs = dxs + dxdt * f["dte"]
        ddtraw = jnp.where(lane < SSD_HEADS, ddt * _sigmoid(f["dtpre"]), 0.0)
        ddtb_ref[...] += jnp.sum(ddtraw, axis=0, keepdims=True)
        ddt_ref[...] = ddtraw.astype(MXU)
        dxa = jnp.concatenate([dxs, dbm[0], dbm[1], dcm[0], dcm[1]], axis=1)
        sig_c = f["sig_c"]
        dconv = dxa * sig_c * (1.0 + f["conv"] * (1.0 - sig_c))
        dcb_ref[...] += jnp.sum(dconv, axis=0, keepdims=True)
        for k in range(4):
            dcw_ref[k:k + 1, :] += jnp.sum(dconv * f["shifts"][3 - k], axis=0, keepdims=True)
        dxbc = cw[3:4] * dconv
        for j, up in zip((1, 2, 3), _shifts_up(dconv, dnext_ref[...])):
            dxbc = dxbc + cw[3 - j:4 - j] * up
        dnext_ref[...] = dconv[0:8]
        dssd_ref[:, 0:1024] = dz.astype(MXU)
        dssd_ref[:, 1024:2560] = dxbc.astype(MXU)

    return pl.pallas_call(
        _after(15, body), grid=(nb, nch), name="ssd_bwd",
        in_specs=in_specs,
        out_specs=[pl.BlockSpec((CH, 2560), lambda b, c: (tok(b, c), 0)),
                   pl.BlockSpec((CH, 128), lambda b, c: (tok(b, c), 0)),
                   _const((8, CONV_CH)), _const((1, CONV_CH)), _const((1, 128)), _const((1, 128)), _const((1, 128)),
                   _const((1, 1024))],
        out_shape=[_sds((T, 2560), MXU), _sds((T, 128), MXU), _sds((8, CONV_CH), F32), _sds((1, CONV_CH), F32),
                   _sds((1, 128), F32), _sds((1, 128), F32), _sds((1, 128), F32), _sds((1, 1024), F32)],
        scratch_shapes=[pltpu.VMEM((128, 1024), F32), pltpu.VMEM((8, CONV_CH), F32), pltpu.VMEM((1, 1024), F32)],
        compiler_params=_cp(2),
    )(pz, pxbc, pxbc, dtraw, convw, convb, dtb, alog, dskip, ng, ex, ltri, ext, sall, dcat, after)


def _inproj_bwd(dxn_uv, dssd, ddt, wm, wdt, dh1, x, g, tm, after):
    T = x.shape[0]

    def body(dxnuv_ref, dssd_ref, ddt_ref, wm_ref, wdt_ref, dh1_ref, x_ref, g_ref, dx_ref, dg_ref):
        @pl.when(pl.program_id(0) == 0)
        def _():
            dg_ref[...] = jnp.zeros_like(dg_ref)

        dxn = (dxnuv_ref[...] + _dot_nt(dssd_ref[...], wm_ref[:, 2048:N_MAIN])
               + _dot_nt(ddt_ref[...], wdt_ref[...]))
        xh, r = _rms(x_ref[...])
        dg_ref[...] += jnp.sum(dxn * xh, axis=0, keepdims=True)
        dx_ref[...] = dh1_ref[...] + _rms_bwd(dxn, xh, r, g_ref[...])

    return pl.pallas_call(
        _after(8, body), grid=(T // tm,), name="inproj_bwd",
        in_specs=[_rows(tm, D), _rows(tm, 2560), _rows(tm, 128), _const((D, N_MAIN)), _const((D, 128)),
                  _rows(tm, D), _rows(tm, D), _const((1, D)), _ANY],
        out_specs=[_rows(tm, D), _const((1, D))],
        out_shape=[_sds((T, D), F32), _sds((1, D), F32)],
        compiler_params=_cp(),
    )(dxn_uv, dssd, ddt, wm, wdt, dh1, x, g, after)


def _matmul_tn(a, b, name, a_fn=None):
    T, M = a.shape
    N = b.shape[1]
    tm = min(M, 1024)
    tn = 1280 if N == 2560 else min(N, 1024)
    tk = min(T, 2048)

    def body(a_ref, b_ref, o_ref, acc_ref):
        k = pl.program_id(2)

        @pl.when(k == 0)
        def _():
            acc_ref[...] = jnp.zeros_like(acc_ref)

        av = a_ref[...]
        if a_fn is not None:
            av = a_fn(av)
        acc_ref[...] += _dot_tn(av, b_ref[...])

        @pl.when(k == T // tk - 1)
        def _():
            o_ref[...] = acc_ref[...].astype(o_ref.dtype)

    return pl.pallas_call(
        body, grid=(M // tm, N // tn, T // tk), name=name,
        in_specs=[pl.BlockSpec((tk, tm), lambda i, j, k: (k, i)), pl.BlockSpec((tk, tn), lambda i, j, k: (k, j))],
        out_specs=pl.BlockSpec((tm, tn), lambda i, j, k: (i, j)),
        out_shape=_sds((M, N), GRAD),
        scratch_shapes=[pltpu.VMEM((tm, tn), F32)],
        compiler_params=_cp(3),
    )(a, b)


def _adamw_vals(w, g, m, v):
    m = B1 * m + (1.0 - B1) * g
    v = B2 * v + (1.0 - B2) * (g * g)
    m_hat = m / (1.0 - B1 ** STEP)
    v_hat = v / (1.0 - B2 ** STEP)
    return -LR * (m_hat / (jnp.sqrt(v_hat) + ADAM_EPS) + WD * w), m, v


def _adamw(w, g, m, v, name):
    R, C = w.shape
    tr = 256 if R % 256 == 0 else R

    def body(w_ref, g_ref, m_ref, v_ref, d_ref, mo_ref, vo_ref):
        d_ref[...], mo_ref[...], vo_ref[...] = _adamw_vals(w_ref[...], g_ref[...], m_ref[...], v_ref[...])

    spec = _rows(tr, C)
    return pl.pallas_call(
        body, grid=(R // tr,), name=name,
        in_specs=[spec] * 4, out_specs=[spec] * 3, out_shape=[_sds((R, C), F32)] * 3,
        compiler_params=_cp(),
    )(w, g, m, v)


def _adamw_halves(w, own, other, m, v, name):
    R, C = w.shape
    half = R // 2
    tr = min(half, 256)
    nth = half // tr

    def body(w_ref, own_ref, oth_ref, m_ref, v_ref, g_ref, d_ref, mo_ref, vo_ref):
        mine = (pl.program_id(0) // nth) == lax.axis_index("c")
        g = jnp.where(mine, own_ref[...], oth_ref[...])
        g_ref[...] = g
        d_ref[...], mo_ref[...], vo_ref[...] = _adamw_vals(w_ref[...], g, m_ref[...], v_ref[...])

    full = _rows(tr, C)
    part = pl.BlockSpec((tr, C), lambda i: (i % nth, 0))
    return pl.pallas_call(
        body, grid=(R // tr,), name=name,
        in_specs=[full, part, part, full, full], out_specs=[full] * 4, out_shape=[_sds((R, C), F32)] * 4,
        compiler_params=_cp(),
    )(w, own, other, m, v)


def _sum_small(slots, name):
    nd, rows, C = slots.shape

    def body(s_ref, o_ref):
        acc = s_ref[0]
        for d in range(1, nd):
            acc = acc + s_ref[d]
        o_ref[...] = acc

    return pl.pallas_call(
        body, grid=(1,), name=name,
        in_specs=[_const((nd, rows, C))], out_specs=_const((rows, C)), out_shape=_sds((rows, C), F32),
        compiler_params=_cp(),
    )(slots)


def _sum_slots(slots, src, kind, shp, kh, name):
    R, C = shp
    rh = R // 2
    tr = min(rh, 256)
    nth = rh // tr
    if kind == "slab":
        src_spec = pl.BlockSpec((1, tr, C), lambda i, kh: (kh[0], kh[1] * nth + i, 0))
    elif kind == "rows":
        src_spec = pl.BlockSpec((tr, C), lambda i, kh: (kh[0] * (R // tr) + kh[1] * nth + i, 0))
    else:
        src_spec = pl.BlockSpec((tr, C), lambda i, kh: (kh[1] * nth + i, kh[0]))

    def body(kh_ref, s_ref, own_ref, o_ref):
        me = 2 * kh_ref[0] + kh_ref[1]
        acc = (own_ref[0] if kind == "slab" else own_ref[...]).astype(F32)
        for k in range(1, 8):
            acc = acc + s_ref[me ^ k].astype(F32)
        o_ref[...] = acc

    return pl.pallas_call(
        body, name=name,
        grid_spec=pltpu.PrefetchScalarGridSpec(
            num_scalar_prefetch=1, grid=(nth,),
            in_specs=[pl.BlockSpec((8, tr, C), lambda i, kh: (0, i, 0)), src_spec],
            out_specs=pl.BlockSpec((tr, C), lambda i, kh: (i, 0))),
        out_shape=_sds((rh, C), F32),
        compiler_params=_cp(),
    )(kh, slots, src)


def _assemble_w_in(slabs):
    tr = 256

    def body(s_ref, wm_ref, wdt_ref):
        full = jnp.concatenate([s_ref[k] for k in range(4)], axis=1)
        wm_ref[...] = full[:, :N_MAIN]
        wdt_ref[...] = jnp.concatenate([full[:, N_MAIN:], jnp.zeros((tr, 128 - 16), full.dtype)], axis=1)

    return pl.pallas_call(
        body, grid=(D // tr,), name="assemble_w_in",
        in_specs=[pl.BlockSpec((4, tr, 1156), lambda i: (0, i, 0))],
        out_specs=[_rows(tr, N_MAIN), _rows(tr, 128)],
        out_shape=[_sds((D, N_MAIN), slabs.dtype), _sds((D, 128), slabs.dtype)],
        compiler_params=_cp(),
    )(slabs)


def _split_dw_in(d_uv, d_ssd, d_dt):
    tr = 256

    def body(uv_ref, ssd_ref, dt_ref, o_ref):
        full = jnp.concatenate([uv_ref[...], ssd_ref[...], dt_ref[:, 0:16]], axis=1)
        for k in range(4):
            o_ref[k] = full[:, 1156 * k:1156 * (k + 1)]

    return pl.pallas_call(
        body, grid=(D // tr,), name="split_dw_in",
        in_specs=[_rows(tr, 2048), _rows(tr, 2560), _rows(tr, 128)],
        out_specs=pl.BlockSpec((4, tr, 1156), lambda i: (0, i, 0)),
        out_shape=_sds((4, D, 1156), d_uv.dtype),
        compiler_params=_cp(),
    )(d_uv, d_ssd, d_dt)


def _cast_into_slot(w, kh, name):
    R, C = w.shape
    tr = 256

    def body(kh_ref, w_ref, o_ref):
        o_ref[0] = w_ref[...].astype(BF16)

    return pl.pallas_call(
        body, name=name,
        grid_spec=pltpu.PrefetchScalarGridSpec(
            num_scalar_prefetch=1, grid=(R // tr,),
            in_specs=[pl.BlockSpec((tr, C), lambda i, kh: (i, 0))],
            out_specs=pl.BlockSpec((1, tr, C), lambda i, kh: (kh[0], i, 0))),
        out_shape=_sds((4, R, C), BF16),
        compiler_params=_cp(),
    )(kh, w)


_ANY = pl.BlockSpec(memory_space=pl.ANY)
_CHIP_FLIPS = [(1, 0), (0, 1), (1, 1)]
_DEVICE_FLIPS = [(fx, fy, fc) for fx in (0, 1) for fy in (0, 1) for fc in (0, 1)][1:]


def _half(h, rows):
    return pl.ds(pl.multiple_of(h * rows, rows), rows)


def _remote(src, dst, ssem, rsem, to):
    return pltpu.make_async_remote_copy(src_ref=src, dst_ref=dst, send_sem=ssem, recv_sem=rsem,
                                        device_id=to, device_id_type=MESH)


def _weight_gather(bufs, conv):
    n = len(bufs)

    def body(*refs):
        conv_ref, outs, conv_out = refs[n], refs[n + 1:2 * n + 1], refs[2 * n + 1]
        send_sems, recv_sems, fsend_sems, frecv_sems, csend_sems, crecv_sems, local_sem = refs[2 * n + 2:]
        x, y, c = lax.axis_index("x"), lax.axis_index("y"), lax.axis_index("c")
        me = 2 * x + y
        halves = [_half(c, r.shape[1] // 2) for r in outs]
        others = [_half(1 - c, r.shape[1] // 2) for r in outs]
        remote = _remote
        local = [pltpu.make_async_copy(conv_ref, conv_out.at[me], local_sem)]
        for cp in local:
            cp.start()
        sends = []
        for k, (fx, fy) in enumerate(_CHIP_FLIPS):
            peer = (x ^ fx, y ^ fy, c)
            for i in range(n):
                mine = outs[i].at[me, halves[i]]
                sends.append(remote(mine, mine, send_sems.at[k * n + i], recv_sems.at[k * n + i], peer))
            sends.append(remote(conv_ref, conv_out.at[me], csend_sems.at[k], crecv_sems.at[k], peer))
        for cp in sends:
            cp.start()
        sibling = (x, y, 1 - c)
        forwards = []
        for k, (fx, fy) in enumerate(_CHIP_FLIPS):
            peer = (x ^ fx, y ^ fy, c)
            src = 2 * (x ^ fx) + (y ^ fy)
            for i in range(n):
                landed = outs[i].at[src, halves[i]]
                remote(landed, landed, send_sems.at[k * n + i], recv_sems.at[k * n + i], peer).wait_recv()
                fw = remote(landed, landed, fsend_sems.at[k * n + i], frecv_sems.at[k * n + i], sibling)
                fw.start()
                forwards.append(fw)
            remote(conv_out.at[src], conv_out.at[src], csend_sems.at[k], crecv_sems.at[k], peer).wait_recv()
        for k, (fx, fy) in enumerate(_CHIP_FLIPS):
            src = 2 * (x ^ fx) + (y ^ fy)
            for i in range(n):
                theirs = outs[i].at[src, others[i]]
                remote(theirs, theirs, fsend_sems.at[k * n + i], frecv_sems.at[k * n + i], sibling).wait_recv()
        for cp in sends + forwards:
            cp.wait_send()
        for cp in local:
            cp.wait()

    dma = pltpu.SemaphoreType.DMA
    return pl.pallas_call(
        body, name="weight_gather",
        in_specs=[_ANY] * (n + 1), out_specs=[_ANY] * (n + 1),
        out_shape=[_sds(b.shape, b.dtype) for b in bufs] + [_sds((4,) + conv.shape, conv.dtype)],
        input_output_aliases={i: i for i in range(n)},
        scratch_shapes=[dma((3 * n,)), dma((3 * n,)), dma((3 * n,)), dma((3 * n,)), dma((3,)), dma((3,)), dma],
    )(*bufs, conv)


def _piece(ref, kind, R, C, k, h):
    if kind == "slab":
        return ref.at[k, _half(h, R // 2), :]
    if kind == "rows":
        return ref.at[pl.ds(pl.multiple_of(k * R + h * (R // 2), R // 2), R // 2), :]
    return ref.at[_half(h, R // 2), pl.ds(pl.multiple_of(k * C, C), C)]


def _small_exchange(small, after):
    rs = small.shape[0]

    def body(s_ref, after_ref, out_ref, send_sems, recv_sems, local_sem):
        del after_ref
        x, y, c = lax.axis_index("x"), lax.axis_index("y"), lax.axis_index("c")
        slot = 4 * x + 2 * y + c
        own = pltpu.make_async_copy(s_ref, out_ref.at[slot], local_sem)
        own.start()
        copies = []
        for k, (fx, fy, fc) in enumerate(_DEVICE_FLIPS):
            copies.append(_remote(s_ref, out_ref.at[slot], send_sems.at[k], recv_sems.at[k], (x ^ fx, y ^ fy, c ^ fc)))
        for cp in copies:
            cp.start()
        for k, (fx, fy, fc) in enumerate(_DEVICE_FLIPS):
            theirs = out_ref.at[slot ^ (k + 1)]
            _remote(theirs, theirs, send_sems.at[k], recv_sems.at[k], (x ^ fx, y ^ fy, c ^ fc)).wait_recv()
        for cp in copies:
            cp.wait_send()
        own.wait()

    dma = pltpu.SemaphoreType.DMA
    return pl.pallas_call(
        body, name="small_exchange",
        in_specs=[_ANY, _ANY], out_specs=_ANY, out_shape=_sds((8, rs, 128), F32),
        scratch_shapes=[dma((7,)), dma((7,)), dma],
    )(small, after)


_HBM = pl.BlockSpec(memory_space=pltpu.HBM)
_SEM = pl.BlockSpec(memory_space=pltpu.SEMAPHORE)


def _split_start(name, arrays, n_copies, plan, after=None):
    n = len(arrays)
    extra = [] if after is None else [after]

    def body(*refs):
        m = n + len(extra)
        arrs, send_sems, recv_sems, token = refs[:n], refs[m], refs[m + 1], refs[-1]
        for j, (src, dst, peer) in enumerate(plan(arrs)):
            _remote(src, dst, send_sems.at[j], recv_sems.at[j], peer).start()
        token[...] = jnp.zeros_like(token)

    dma = pltpu.SemaphoreType.DMA
    res = pl.pallas_call(
        body, name=name,
        out_shape=(dma((n_copies,)), dma((n_copies,)), *[pltpu.HBM(a.shape, a.dtype) for a in arrays],
                   _sds((8, 128), F32)),
        in_specs=[_HBM] * n + [_ANY] * len(extra),
        out_specs=(_SEM, _SEM, *[_HBM] * n, pl.BlockSpec(memory_space=pltpu.VMEM)),
        input_output_aliases={i: 2 + i for i in range(n)},
        compiler_params=pltpu.CompilerParams(has_side_effects=pltpu.SideEffectType.DATAFLOW_SIDE_EFFECTING),
    )(*[pltpu.with_memory_space_constraint(a, pltpu.HBM) for a in arrays], *extra)
    return res[0], res[1], list(res[2:2 + n]), res[-1]


def _split_wait(name, arrays, send_sems, recv_sems, plan, after):
    n = len(arrays)

    def body(*refs):
        arrs, ssems, rsems = refs[:n], refs[n], refs[n + 1]
        for j, (src, dst, peer) in enumerate(plan(arrs)):
            cp = _remote(src, dst, ssems.at[j], rsems.at[j], peer)
            cp.wait_send()
            cp.wait_recv()

    return list(pl.pallas_call(
        body, name=name,
        out_shape=tuple(pltpu.HBM(a.shape, a.dtype) for a in arrays),
        in_specs=[_HBM] * n + [_SEM, _SEM, _ANY],
        out_specs=tuple([_HBM] * n),
        input_output_aliases={i: i for i in range(n)},
        compiler_params=pltpu.CompilerParams(has_side_effects=pltpu.SideEffectType.DATAFLOW_SIDE_EFFECTING),
    )(*arrays, send_sems, recv_sems, after))


def _gather_plan(n):
    def plan(bufs):
        x, y, c = lax.axis_index("x"), lax.axis_index("y"), lax.axis_index("c")
        me = 2 * x + y
        return [(bufs[i].at[me], bufs[i].at[me], (x ^ fx, y ^ fy, c)) for fx, fy in _CHIP_FLIPS for i in range(n)]

    return plan


def _reduce_plan(specs, n_small):
    n = len(specs)

    def plan(arrs):
        x, y, c = lax.axis_index("x"), lax.axis_index("y"), lax.axis_index("c")
        slot = 4 * x + 2 * y + c
        out = []
        for fx, fy, fc in _DEVICE_FLIPS:
            peer = (x ^ fx, y ^ fy, c ^ fc)
            for i, (kind, (R, C)) in enumerate(specs):
                out.append((_piece(arrs[i], kind, R, C, 2 * peer[0] + peer[1], peer[2]), arrs[n + i].at[slot], peer))
            for s in range(n_small):
                out.append((arrs[2 * n + 2 * s], arrs[2 * n + 2 * s + 1].at[slot], peer))
        return out

    return plan


def _sibling_exchange(halves):
    n = len(halves)

    def body(*refs):
        ins, outs, send_sems, recv_sems = refs[:n], refs[n:2 * n], refs[2 * n], refs[2 * n + 1]
        sibling = (lax.axis_index("x"), lax.axis_index("y"), 1 - lax.axis_index("c"))
        copies = [pltpu.make_async_remote_copy(src_ref=ins[i], dst_ref=outs[i], send_sem=send_sems.at[i],
                                               recv_sem=recv_sems.at[i], device_id=sibling, device_id_type=MESH)
                  for i in range(n)]
        for cp in copies:
            cp.start()
        for cp in copies:
            cp.wait()

    dma = pltpu.SemaphoreType.DMA
    return pl.pallas_call(
        body, name="sibling_exchange",
        in_specs=[_ANY] * n, out_specs=[_ANY] * n,
        out_shape=[_sds(h.shape, h.dtype) for h in halves],
        scratch_shapes=[dma((n,)), dma((n,))],
    )(*halves)


_BIG = [("w_in", (1024, 1156), "slab"), ("w_out", (512, 1024), "rows"), ("w_ff1", (1024, 1024), "cols"),
        ("w_ff2", (1024, 1024), "rows"), ("w_ple_gate", (256, 1024), "rows"), ("w_ple_proj", (256, 256), "cols")]
_SMALL = [("norm_mix_g", (1, 1024)), ("gm_v_norm_g", (1, 1024)), ("gm_ws", (1, 8, 128, 128)), ("gm_bs", (1, 8, 128)),
          ("gm_out_norm_g", (1, 1024)), ("ssd_conv_w", (1, 4, 1536)), ("ssd_conv_b", (1, 1536)),
          ("ssd_dt_bias", (1, 16)), ("ssd_a_log", (1, 16)), ("ssd_d", (1, 16)), ("ssd_norm_g", (1, 1024)),
          ("norm_mlp_g", (1, 1024)), ("ple_norm_g", (1, 1024)), ("final_norm_g", (1024,))]


def _rows128(a):
    flat = a.reshape(-1)
    rows = -(-flat.shape[0] // 1024) * 8
    return jnp.pad(flat, (0, rows * 128 - flat.shape[0])).reshape(rows, 128)


def _pad_lanes(v, n=128):
    v = v.reshape(1, -1)
    return jnp.pad(v, ((0, 0), (0, n - v.shape[1])))


_SMALL_SHAPES = dict(_SMALL + [("loss", ())])
_BIG_SPECS = {n: (kind, shp) for n, shp, kind in _BIG}


class _Comm:
    def __init__(self, a, kh):
        self.a, self.kh = a, kh
        self.bufs = {n: _cast_into_slot(a[n].reshape(shp), kh, "cast_" + n) for n, shp, _ in _BIG}
        self.sent = []
        self.small_tot = {}

    def w_in(self):
        (g_win,), g_cw = self._gather_now()
        rest = [self.bufs[n] for n, _, _ in _BIG[1:]]
        plan = _gather_plan(len(rest))
        ssem, rsem, thru, token = _split_start("gather_start", rest, 3 * len(rest), plan, after=g_cw)
        self.gather = (plan, ssem, rsem, thru)
        wm, wdt = _assemble_w_in(g_win)
        return wm, wdt, jnp.concatenate([g_cw[k] for k in range(4)], axis=1), token

    def _gather_now(self):
        *bufs, g_cw = _weight_gather([self.bufs["w_in"]], self.a["ssd_conv_w"].reshape(4, 384))
        return bufs, g_cw

    def rest(self, after):
        plan, ssem, rsem, thru = self.gather
        g_wo, g_w1, g_w2, g_wg, g_wp = _split_wait("gather_wait", thru, ssem, rsem, plan, after)
        return g_wo.reshape(2048, D), g_w1, g_w2.reshape(DFF, D), g_wg.reshape(D, D), g_wp

    def send(self, tag, grads):
        big = [n for n, _, _ in _BIG if n in grads]
        small = [n for n in _SMALL_SHAPES if n in grads]
        parts = [_rows128(grads[n]) for n in small]
        rows = [s.shape[0] for s in parts]
        if not big:
            self.last_small = (tag, small, rows, jnp.concatenate(parts, axis=0))
            return None
        srcs = [grads[n] for n in big]
        lands = [lax.empty((8, _BIG_SPECS[n][1][0] // 2, _BIG_SPECS[n][1][1]), GRAD) for n in big]
        extra = []
        if small:
            pack = jnp.concatenate(parts, axis=0)
            extra = [pack, jnp.broadcast_to(pack, (8,) + pack.shape)]
        plan = _reduce_plan([_BIG_SPECS[n] for n in big], len(extra) // 2)
        n_copies = 7 * (len(big) + len(extra) // 2)
        ssem, rsem, thru, token = _split_start("reduce_start_" + tag, srcs + lands + extra, n_copies, plan)
        self.sent.append((tag, big, small, rows, plan, ssem, rsem, thru))
        return token

    def _unpack(self, tot, names, rows):
        o = 0
        for n, r in zip(names, rows):
            shp = _SMALL_SHAPES[n]
            cnt = 1
            for s in shp:
                cnt *= s
            self.small_tot[n] = tot[o:o + r].reshape(-1)[:cnt].reshape(shp)
            o += r

    def finish(self, after):
        own = {}
        for tag, big, small, rows, plan, ssem, rsem, thru in self.sent:
            arrs = _split_wait("reduce_wait_" + tag, thru, ssem, rsem, plan, after)
            nb_ = len(big)
            for i, n in enumerate(big):
                kind, shp = _BIG_SPECS[n]
                own[n] = _sum_slots(arrs[nb_ + i], arrs[i], kind, shp, self.kh, "sum_" + n)
                after = own[n]
            if small:
                self._unpack(_sum_small(arrs[2 * nb_ + 1], "sum_small_" + tag), small, rows)
        tag, small, rows, pack = self.last_small
        self._unpack(_sum_small(_small_exchange(pack, after), "sum_small_" + tag), small, rows)
        return [own[n] for n, _, _ in _BIG], dict(self.small_tot)


def _local_step(x, p, tgt, sm, comm, nb, tm):
    wm, wdt, conv_w, token = comm.w_in()
    g_mix, gv, gout = sm["norm_mix_g"].reshape(1, D), sm["gm_v_norm_g"].reshape(1, D), sm["gm_out_norm_g"].reshape(1, D)
    ws = sm["gm_ws"].reshape(GM_HEADS, CH, CH)
    bst = jnp.pad(sm["gm_bs"].reshape(GM_HEADS, CH).T, ((0, 0), (0, 128 - GM_HEADS)))
    convw = jnp.pad(conv_w, ((0, 4), (0, 0)))
    convb = sm["ssd_conv_b"].reshape(1, CONV_CH)
    dtb, alog = _pad_lanes(sm["ssd_dt_bias"]), _pad_lanes(sm["ssd_a_log"])
    dskip = jnp.repeat(sm["ssd_d"].reshape(SSD_HEADS), SSD_P).reshape(1, 1024)
    ng, g_mlp, g_ple = sm["ssd_norm_g"].reshape(1, D), sm["norm_mlp_g"].reshape(1, D), sm["ple_norm_g"].reshape(1, D)
    gf = sm["final_norm_g"].reshape(1, D)
    head_of_lane = lax.broadcasted_iota(jnp.int32, (128, 1024), 1) // SSD_P
    ex = (lax.broadcasted_iota(jnp.int32, (128, 1024), 0) == head_of_lane).astype(BF16)
    ext = ex.T
    ltri = (lax.broadcasted_iota(jnp.int32, (CH, CH), 0) >= lax.broadcasted_iota(jnp.int32, (CH, CH), 1)).astype(F32)

    pz, pxbc, dtraw, xn, cat, uv = _inproj_gmlp(x, g_mix, wm, wdt, gv, ws, bst, gout, tm // 2, token)
    cat, sall = _ssd_fwd(pz, pxbc, dtraw, cat, convw, convb, dtb, alog, dskip, ng, ex, ltri, nb)
    wo, w1, w2, wg, wp = comm.rest(cat)
    h1, hn = _outproj(cat, wo, x, g_mlp, tm)
    hid = _ff1(hn, w1, tm)
    hp, dgl, dpe, dh2, dh2b, loss, d_gf, d_gple = _ff2_tail(hid, w2, h1, g_ple, p, tgt, wg, wp, gf, tm // 2)

    d_wp = _matmul_tn(p, dpe, "dw_ple_proj", a_fn=lambda a: a.astype(MXU))
    d_wg = _matmul_tn(hp, dgl, "dw_ple_gate")
    d_w2 = _matmul_tn(hid, dh2b, "dw_ff2", a_fn=_sq)
    dpre = _ff2_bwd(dh2b, w2, hid, tm)
    d_w1 = _matmul_tn(hn, dpre, "dw_ff1")
    token = comm.send("a", {"w_ple_proj": d_wp, "w_ple_gate": d_wg, "w_ff2": d_w2, "w_ff1": d_w1})
    dh1, dh1b, d_gmlp = _ff1_bwd(dpre, w1, dh2, h1, g_mlp, tm, token)
    dcat = _outproj_bwd(dh1b, wo, tm)
    d_wo = _matmul_tn(cat, dh1b, "dw_out")
    duv, d_gv, d_ws, d_bst, d_gout, dxn_uv = _gmlp_bwd(uv, dcat, gv, ws, bst, gout, wm)
    token = comm.send("b", {
        "w_out": d_wo, "loss": loss[0:1, 0:1], "final_norm_g": d_gf, "ple_norm_g": d_gple, "norm_mlp_g": d_gmlp,
        "gm_v_norm_g": d_gv, "gm_ws": d_ws, "gm_bs": d_bst[:, :GM_HEADS].T, "gm_out_norm_g": d_gout})
    dssd, ddt, d_cw, d_cb, d_dtb, d_al, d_ds, d_ng = _ssd_bwd(
        pz, pxbc, dtraw, sall, dcat, convw, convb, dtb, alog, dskip, ng, ex, ltri, ext, nb, token)
    d_win = _split_dw_in(_matmul_tn(xn, duv, "dw_in_uv"), _matmul_tn(xn, dssd, "dw_in_ssd"),
                         _matmul_tn(xn, ddt, "dw_in_dt"))
    token = comm.send("c", {"w_in": d_win})
    dx, d_gmix = _inproj_bwd(dxn_uv, dssd, ddt, wm, wdt, dh1, x, g_mix, tm, token)
    comm.send("d", {"norm_mix_g": d_gmix, "ssd_conv_w": d_cw[0:4], "ssd_conv_b": d_cb, "ssd_dt_bias": d_dtb[:, :16],
                    "ssd_a_log": d_al[:, :16], "ssd_d": d_ds[:, :16], "ssd_norm_g": d_ng})
    return dx


def kernel(x, p, norm_mix_g, w_in, gm_v_norm_g, gm_ws, gm_bs, gm_out_norm_g, ssd_conv_w, ssd_conv_b, ssd_dt_bias, ssd_a_log, ssd_d, ssd_norm_g, w_out, norm_mlp_g, w_ff1, w_ff2, ple_norm_g, w_ple_gate, w_ple_proj, final_norm_g, loss_target, m_norm_mix_g, m_w_in, m_gm_v_norm_g, m_gm_ws, m_gm_bs, m_gm_out_norm_g, m_ssd_conv_w, m_ssd_conv_b, m_ssd_dt_bias, m_ssd_a_log, m_ssd_d, m_ssd_norm_g, m_w_out, m_norm_mlp_g, m_w_ff1, m_w_ff2, m_ple_norm_g, m_w_ple_gate, m_w_ple_proj, m_final_norm_g, v_norm_mix_g, v_w_in, v_gm_v_norm_g, v_gm_ws, v_gm_bs, v_gm_out_norm_g, v_ssd_conv_w, v_ssd_conv_b, v_ssd_dt_bias, v_ssd_a_log, v_ssd_d, v_ssd_norm_g, v_w_out, v_norm_mlp_g, v_w_ff1, v_w_ff2, v_ple_norm_g, v_w_ple_gate, v_w_ple_proj, v_final_norm_g):
    a = dict(locals())
    order = ["norm_mix_g", "w_in", "gm_v_norm_g", "gm_ws", "gm_bs", "gm_out_norm_g", "ssd_conv_w", "ssd_conv_b",
             "ssd_dt_bias", "ssd_a_log", "ssd_d", "ssd_norm_g", "w_out", "norm_mlp_g", "w_ff1", "w_ff2", "ple_norm_g",
             "w_ple_gate", "w_ple_proj", "final_norm_g"]
    chip = 2 * lax.axis_index("x") + lax.axis_index("y")
    nb, S = x.shape[0], x.shape[1]
    T = nb * S
    sm = {n: a[n] for n, _ in _SMALL if n != "ssd_conv_w"}
    comm = _Comm(a, jnp.stack([chip, lax.axis_index("c")]).astype(jnp.int32))
    dx = _local_step(x.reshape(T, D), p.reshape(T, DPLE), loss_target.reshape(T, D), sm, comm, nb, 512)
    own, g_out = comm.finish(dx)
    other = _sibling_exchange(own)

    delta, new_m, new_v = {}, {}, {}
    for i, (n, shp, _) in enumerate(_BIG):
        res = _adamw_halves(a[n].reshape(shp), own[i], other[i], a["m_" + n].reshape(shp), a["v_" + n].reshape(shp),
                            "adamw_" + n)
        g_out[n], delta[n], new_m[n], new_v[n] = (r.reshape(a[n].shape) for r in res)
    g_out["ssd_conv_w"] = lax.dynamic_slice(g_out["ssd_conv_w"], (0, 0, chip * 384), (1, 4, 384))
    small_names = [n for n, _ in _SMALL]
    packs = [jnp.concatenate([_rows128(src(n)) for n in small_names], axis=0)
             for src in (lambda n: a[n], lambda n: g_out[n], lambda n: a["m_" + n], lambda n: a["v_" + n])]
    outs = _adamw(*packs, "adamw_small")
    o = 0
    for n in small_names:
        r = _rows128(a[n]).shape[0]
        cnt = a[n].size
        for dst, src in zip((delta, new_m, new_v), outs):
            dst[n] = src[o:o + r].reshape(-1)[:cnt].reshape(a[n].shape)
        o += r
    return (g_out["loss"], dx.reshape(x.shape), *[g_out[n] for n in order], *[delta[n] for n in order],
            *[new_m[n] for n in order], *[new_v[n] for n in order])
```

```python
import jax
import jax.numpy as jnp
from jax import lax
from jax.experimental import pallas as pl
from jax.experimental.pallas import tpu as pltpu

F32 = jnp.float32
BF16 = jnp.bfloat16
MXU = jnp.bfloat16
GRAD = jnp.bfloat16

D = 1024
CH = 128
GM_HEADS = 8
SSD_HEADS = 16
SSD_P = 64
CONV_CH = 1536
N_MAIN = 4608
DFF = 4096
DPLE = 256
EPS = 1e-6
NEG = -1e30

LR, B1, B2, ADAM_EPS, WD, STEP = 0.001, 0.9, 0.999, 1e-08, 0.01, 10

VMEM_LIMIT = 56 * 1024 * 1024
MESH = pl.DeviceIdType.MESH

INV_SQRT2 = 0.7071067811865476
INV_SQRT_2PI = 0.3989422804014327


def _cp(n_axes=1):
    return pltpu.CompilerParams(dimension_semantics=("arbitrary",) * n_axes, vmem_limit_bytes=VMEM_LIMIT)


def _dot(a, b):
    return jnp.dot(a, b, preferred_element_type=F32)


def _dot_nt(a, b):
    return lax.dot_general(a, b, (((1,), (1,)), ((), ())), preferred_element_type=F32)


def _dot_tn(a, b):
    return lax.dot_general(a, b, (((0,), (0,)), ((), ())), preferred_element_type=F32)


def _dot_hi(a, b):
    return jnp.dot(a, b, preferred_element_type=F32, precision=lax.Precision.HIGHEST)


def _dot_01(a, sel):
    hi = a.astype(BF16)
    lo = (a - hi.astype(F32)).astype(BF16)
    n = a.shape[0]
    r = _dot(jnp.concatenate([hi, lo], axis=0), sel)
    return r[0:n] + r[n:2 * n]


def _rows(tm, n, j=0):
    return pl.BlockSpec((tm, n), lambda i: (i, j))


def _const(shape):
    nd = len(shape)
    return pl.BlockSpec(shape, lambda *_: (0,) * nd)


def _sds(shape, dtype):
    return jax.ShapeDtypeStruct(shape, dtype)


def _rms(x):
    r = lax.rsqrt(jnp.mean(x * x, axis=-1, keepdims=True) + EPS)
    return x * r, r


def _rms_bwd(dy, xhat, r, g):
    dyg = dy * g
    return r * (dyg - xhat * jnp.mean(dyg * xhat, axis=-1, keepdims=True))


def _sigmoid(x):
    return 1.0 / (1.0 + jnp.exp(-x))


def _gelu(x):
    cdf = 0.5 * (1.0 + lax.erf(x * INV_SQRT2))
    pdf = jnp.exp(-0.5 * x * x) * INV_SQRT_2PI
    return x * cdf, cdf + x * pdf


def _softplus(x):
    e = jnp.exp(-jnp.abs(x))
    u = 1.0 + e
    log1p = jnp.where(u == 1.0, e, jnp.log(u) * e / (u - 1.0))
    return jnp.maximum(x, 0.0) + log1p


def _after(n_in, fn):
    def body(*refs):
        return fn(*refs[:n_in], *refs[n_in + 1:])

    return body


def _inproj_gmlp(x, g, wm, wdt, gv, ws, bst, gout, tm, after):
    T = x.shape[0]

    def body(x_ref, g_ref, wm_ref, wdt_ref, gv_ref, ws_ref, bst_ref, gout_ref,
             z_ref, xbc_ref, dt_ref, xn_ref, ya_ref, uv_ref):
        xh, _ = _rms(x_ref[...])
        xn = (xh * g_ref[...]).astype(MXU)
        xn_ref[...] = xn
        for n in range(4):
            uv_ref[:, n * 512:(n + 1) * 512] = _dot(xn, wm_ref[:, n * 512:(n + 1) * 512])
        for n in range(2):
            z_ref[:, n * 512:(n + 1) * 512] = _dot(xn, wm_ref[:, 2048 + n * 512:2048 + (n + 1) * 512])
        for n in range(3):
            xbc_ref[:, n * 512:(n + 1) * 512] = _dot(xn, wm_ref[:, 3072 + n * 512:3072 + (n + 1) * 512])
        dt_ref[...] = _dot(xn, wdt_ref[...])
        for k in range(tm // CH):
            rows = slice(k * CH, (k + 1) * CH)
            f = _gmlp_fwd_vals(uv_ref[rows, 0:1024], uv_ref[rows, 1024:2048], gv_ref[...], ws_ref, bst_ref[...],
                               gout_ref[...])
            ya_ref[rows, :] = f["out"].astype(MXU)

    return pl.pallas_call(
        _after(8, body), grid=(T // tm,), name="inproj_gmlp",
        in_specs=[_rows(tm, D), _const((1, D)), _const((D, N_MAIN)), _const((D, 128)), _const((1, 1024)),
                  _const((GM_HEADS, CH, CH)), _const((CH, 128)), _const((1, 1024)), _ANY],
        out_specs=[_rows(tm, 1024), _rows(tm, CONV_CH), _rows(tm, 128), _rows(tm, D), _rows(tm, 1024, 0),
                   _rows(tm, 2048)],
        out_shape=[_sds((T, 1024), F32), _sds((T, CONV_CH), F32), _sds((T, 128), F32), _sds((T, D), MXU),
                   _sds((T, 2048), MXU), _sds((T, 2048), F32)],
        compiler_params=_cp(),
    )(x, g, wm, wdt, gv, ws, bst, gout, after)


def _gmlp_fwd_vals(u, v, gv, ws_ref, bst, gout):
    ug, dug = _gelu(u)
    vg, dvg = _gelu(v)
    row = lax.broadcasted_iota(jnp.int32, (CH, CH), 0)
    col = lax.broadcasted_iota(jnp.int32, (CH, CH), 1)
    tril = row >= col
    ys, heads = [], []
    for h in range(GM_HEADS):
        sl = slice(h * 128, (h + 1) * 128)
        vhat, rv = _rms(vg[:, sl])
        vn = (vhat * gv[:, sl]).astype(MXU)
        wt = jnp.where(tril, ws_ref[h], 0.0)
        mixed = _dot(wt.astype(MXU), vn) + bst[:, h:h + 1]
        ys.append(ug[:, sl] * mixed)
        heads.append((vhat, rv, vn, wt, mixed))
    y = jnp.concatenate(ys, axis=1)
    yhat, ry = _rms(y)
    return dict(ug=ug, dug=dug, dvg=dvg, heads=heads, yhat=yhat, ry=ry, tril=tril, out=yhat * gout)


def _shifts_down(cur, halo):
    row8 = lax.broadcasted_iota(jnp.int32, (8, cur.shape[1]), 0)
    out = [cur]
    for j in (1, 2, 3):
        sh = pltpu.roll(cur, j, 0)
        top = jnp.where(row8 < j, pltpu.roll(halo, j, 0), sh[0:8])
        out.append(jnp.concatenate([top, sh[8:]], axis=0))
    return out


def _shifts_up(cur, halo):
    row8 = lax.broadcasted_iota(jnp.int32, (8, cur.shape[1]), 0)
    out = []
    for j in (1, 2, 3):
        sh = pltpu.roll(cur, CH - j, 0)
        bot = jnp.where(row8 + j >= 8, pltpu.roll(halo, 8 - j, 0), sh[CH - 8:CH])
        out.append(jnp.concatenate([sh[0:CH - 8], bot], axis=0))
    return out


def _ssd_fwd_vals(z, xbc, halo, dtraw, convw, convb, dtb, alog, dskip, ng, ex, ltri, s_prev):
    shifts = _shifts_down(xbc, halo)
    conv = convb + convw[3:4] * shifts[0] + convw[2:3] * shifts[1] + convw[1:2] * shifts[2] + convw[0:1] * shifts[3]
    sig_c = _sigmoid(conv)
    xa = conv * sig_c
    xs = xa[:, :1024]
    bm = [xa[:, 1024:1152], xa[:, 1152:1280]]
    cm = [xa[:, 1280:1408], xa[:, 1408:1536]]
    dtpre = dtraw + dtb
    dt = _softplus(dtpre)
    a_neg = -jnp.exp(alog)
    cs = _dot_hi(ltri, dt * a_neg)
    cst = cs.T
    last = cs[CH - 1:CH]
    ecs = jnp.exp(cs)
    dec = jnp.exp(last - cs)
    spread = _dot_01(jnp.concatenate([dt, ecs, dec], axis=0), ex)
    dte, ecse, dece = spread[0:CH], spread[CH:2 * CH], spread[2 * CH:3 * CH]
    cde = ecse[CH - 1:CH]
    de = dskip
    xdt = xs * dte
    row = lax.broadcasted_iota(jnp.int32, (CH, CH), 0)
    col = lax.broadcasted_iota(jnp.int32, (CH, CH), 1)
    tril = row >= col
    lo = col < SSD_P
    bmb = [b.astype(MXU) for b in bm]
    cmb = [c.astype(MXU) for c in cm]
    mg = [_dot_nt(cmb[g], bmb[g]) for g in range(2)]
    yd, lms, whs = [], [], []
    for q in range(8):
        g = q // 4
        xq = xdt[:, q * 128:(q + 1) * 128]
        acc = None
        for hh in range(2):
            h = 2 * q + hh
            seg = cs[:, h:h + 1] - cst[h:h + 1, :]
            lm = jnp.exp(jnp.where(tril, seg, NEG))
            wh = (mg[g] * lm).astype(MXU)
            xm = jnp.where(lo if hh == 0 else ~lo, xq, 0.0).astype(MXU)
            part = _dot(wh, xm)
            acc = part if acc is None else acc + part
            lms.append(lm)
            whs.append(wh)
        yd.append(acc)
    yd = jnp.concatenate(yd, axis=1)
    sb = s_prev.astype(MXU)
    yo = jnp.concatenate([_dot(cmb[g], sb[:, g * 512:(g + 1) * 512]) for g in range(2)], axis=1) * ecse
    xdec = (xdt * dece).astype(MXU)
    states = jnp.concatenate([_dot_tn(bmb[g], xdec[:, g * 512:(g + 1) * 512]) for g in range(2)], axis=1)
    s_next = s_prev * cde + states
    ypre = yd + yo + de * xs
    sig_z = _sigmoid(z)
    yg = ypre * z * sig_z
    outs, yhat, rr = [], [], []
    for g in range(2):
        sl = slice(g * 512, (g + 1) * 512)
        yh, r = _rms(yg[:, sl])
        yhat.append(yh)
        rr.append(r)
        outs.append(yh * ng[:, sl])
    return dict(shifts=shifts, conv=conv, sig_c=sig_c, xs=xs, bmb=bmb, cmb=cmb, dtpre=dtpre, dt=dt, a_neg=a_neg,
                cs=cs, last=last, ecs=ecs, dec=dec, dte=dte, ecse=ecse, dece=dece, cde=cde, de=de, xdt=xdt,
                mg=mg, lms=lms, whs=whs, lo=lo, yo=yo, sb=sb, xdec=xdec, s_next=s_next, ypre=ypre, sig_z=sig_z,
                yhat=yhat, rr=rr, out=jnp.concatenate(outs, axis=1))


def _ssd_specs(nch, rev):
    def tok(b, c):
        return b * nch + ((nch - 1 - c) if rev else c)

    return tok, [
        pl.BlockSpec((CH, 1024), lambda b, c: (tok(b, c), 0)),
        pl.BlockSpec((CH, CONV_CH), lambda b, c: (tok(b, c), 0)),
        pl.BlockSpec((8, CONV_CH), lambda b, c: (jnp.maximum(tok(b, c) * (CH // 8) - 1, 0), 0)),
        pl.BlockSpec((CH, 128), lambda b, c: (tok(b, c), 0)),
        _const((8, CONV_CH)), _const((1, CONV_CH)), _const((1, 128)), _const((1, 128)), _const((1, 1024)),
        _const((1, 1024)), _const((128, 1024)), _const((CH, CH)),
    ]


def _ssd_fwd(pz, pxbc, dtraw, cat, convw, convb, dtb, alog, dskip, ng, ex, ltri, nb):
    T = pz.shape[0]
    nch = T // CH // nb
    tok, in_specs = _ssd_specs(nch, rev=False)

    def body(z_ref, xbc_ref, halo_ref, dt_ref, cw_ref, cb_ref, dtb_ref, al_ref, ds_ref, ng_ref, ex_ref, lt_ref,
             cat_in_ref, yb_ref, sall_ref, s_ref):
        del cat_in_ref
        c = pl.program_id(1)

        @pl.when(c == 0)
        def _():
            s_ref[...] = jnp.zeros_like(s_ref)

        halo = jnp.where(c == 0, 0.0, halo_ref[...])
        s_prev = s_ref[...]
        sall_ref[0] = s_prev
        f = _ssd_fwd_vals(z_ref[...], xbc_ref[...], halo, dt_ref[...], cw_ref[...], cb_ref[...], dtb_ref[...],
                          al_ref[...], ds_ref[...], ng_ref[...], ex_ref[...], lt_ref[...], s_prev)
        s_ref[...] = f["s_next"]
        yb_ref[...] = f["out"].astype(MXU)

    return pl.pallas_call(
        body, grid=(nb, nch), name="ssd_fwd",
        in_specs=in_specs + [_ANY],
        out_specs=[pl.BlockSpec((CH, 1024), lambda b, c: (tok(b, c), 1)),
                   pl.BlockSpec((1, 128, 1024), lambda b, c: (tok(b, c), 0, 0))],
        out_shape=[_sds((T, 2048), MXU), _sds((T // CH, 128, 1024), F32)],
        scratch_shapes=[pltpu.VMEM((128, 1024), F32)],
        input_output_aliases={12: 0},
        compiler_params=_cp(2),
    )(pz, pxbc, pxbc, dtraw, convw, convb, dtb, alog, dskip, ng, ex, ltri, cat)


def _outproj(cat, wo, x, g, tm):
    T = x.shape[0]

    def body(cat_ref, wo_ref, x_ref, g_ref, h1_ref, hn_ref):
        h1 = x_ref[...] + _dot(cat_ref[...], wo_ref[...])
        h1_ref[...] = h1
        hn_ref[...] = (_rms(h1)[0] * g_ref[...]).astype(MXU)

    return pl.pallas_call(
        body, grid=(T // tm,), name="outproj",
        in_specs=[_rows(tm, 2048), _const((2048, D)), _rows(tm, D), _const((1, D))],
        out_specs=[_rows(tm, D), _rows(tm, D)],
        out_shape=[_sds((T, D), F32), _sds((T, D), MXU)],
        compiler_params=_cp(),
    )(cat, wo, x, g)


def _ff1(hn, w1, tm):
    T = hn.shape[0]

    def body(hn_ref, w1_ref, hid_ref):
        hn_v = hn_ref[...]
        for n in range(4):
            hid_ref[:, n * 1024:(n + 1) * 1024] = jnp.maximum(_dot(hn_v, w1_ref[n]), 0.0).astype(MXU)

    return pl.pallas_call(
        body, grid=(T // tm,), name="ff1",
        in_specs=[_rows(tm, D), _const((4, D, 1024))],
        out_specs=_rows(tm, DFF),
        out_shape=_sds((T, DFF), MXU),
        compiler_params=_cp(),
    )(hn, w1)


def _sq(hid):
    h = hid.astype(F32)
    return (h * h).astype(MXU)


def _ff2_tail(hid, w2, h1, g_ple, p, tgt, wg, wp, gf, tm):
    T = h1.shape[0]

    def body(hid_ref, w2_ref, h1_ref, g_ref, p_ref, t_ref, wg_ref, wp_ref, gf_ref,
             hp_ref, dgl_ref, dpe_ref, dh2_ref, dh2b_ref, loss_ref, dgf_ref, dg_ref):
        @pl.when(pl.program_id(0) == 0)
        def _():
            loss_ref[...] = jnp.zeros_like(loss_ref)
            dgf_ref[...] = jnp.zeros_like(dgf_ref)
            dg_ref[...] = jnp.zeros_like(dg_ref)

        h2 = h1_ref[...] + _dot(_sq(hid_ref[...]), w2_ref[...])
        h2h, r2 = _rms(h2)
        g_ple = g_ref[...]
        hp = (h2h * g_ple).astype(MXU)
        hp_ref[...] = hp
        gate = _sigmoid(_dot(hp, wg_ref[...]))
        pb = p_ref[...].astype(MXU)
        pe = jnp.concatenate([_dot(pb, wp_ref[k]) for k in range(4)], axis=1)
        h3 = h2 + gate * pe
        hh, r = _rms(h3)
        gf = gf_ref[...]
        diff = hh * gf - t_ref[...]
        loss_ref[...] += 0.5 * jnp.sum(jnp.mean(diff * diff, axis=-1, keepdims=True))
        dout = diff * (1.0 / D)
        dgf_ref[...] += jnp.sum(dout * hh, axis=0, keepdims=True)
        dh3 = _rms_bwd(dout, hh, r, gf)
        dgl = (dh3 * pe * gate * (1.0 - gate)).astype(MXU)
        dgl_ref[...] = dgl
        dpe_ref[...] = (dh3 * gate).astype(MXU)
        dhp = _dot_nt(dgl, wg_ref[...])
        dg_ref[...] += jnp.sum(dhp * h2h, axis=0, keepdims=True)
        dh2 = dh3 + _rms_bwd(dhp, h2h, r2, g_ple)
        dh2_ref[...] = dh2
        dh2b_ref[...] = dh2.astype(MXU)

    return pl.pallas_call(
        body, grid=(T // tm,), name="ff2_tail",
        in_specs=[_rows(tm, DFF), _const((DFF, D)), _rows(tm, D), _const((1, D)), _rows(tm, DPLE), _rows(tm, D),
                  _const((D, D)), _const((4, DPLE, 256)), _const((1, D))],
        out_specs=[_rows(tm, D), _rows(tm, D), _rows(tm, D), _rows(tm, D), _rows(tm, D), _const((8, 128)),
                   _const((1, D)), _const((1, D))],
        out_shape=[_sds((T, D), MXU), _sds((T, D), MXU), _sds((T, D), MXU), _sds((T, D), F32), _sds((T, D), MXU),
                   _sds((8, 128), F32), _sds((1, D), F32), _sds((1, D), F32)],
        compiler_params=_cp(),
    )(hid, w2, h1, g_ple, p, tgt, wg, wp, gf)


def _ff2_bwd(dh2b, w2, hid, tm):
    T = hid.shape[0]

    def body(dh2b_ref, w2_ref, hid_ref, dpre_ref):
        d = dh2b_ref[...]
        for n in range(DFF // 1024):
            sl = slice(n * 1024, (n + 1) * 1024)
            da = _dot_nt(d, w2_ref[sl, :])
            dpre_ref[:, sl] = (2.0 * da * hid_ref[:, sl].astype(F32)).astype(MXU)

    return pl.pallas_call(
        body, grid=(T // tm,), name="ff2_bwd",
        in_specs=[_rows(tm, D), _const((DFF, D)), _rows(tm, DFF)],
        out_specs=_rows(tm, DFF),
        out_shape=_sds((T, DFF), MXU),
        compiler_params=_cp(),
    )(dh2b, w2, hid)


def _ff1_bwd(dpre, w1, dh2, h1, g, tm, after):
    T = h1.shape[0]

    def body(dpre_ref, w1_ref, dh2_ref, h1_ref, g_ref, dh1_ref, dh1b_ref, dg_ref):
        @pl.when(pl.program_id(0) == 0)
        def _():
            dg_ref[...] = jnp.zeros_like(dg_ref)

        dhn = _dot_nt(dpre_ref[:, 0:1024], w1_ref[0])
        for k in range(1, 4):
            dhn = dhn + _dot_nt(dpre_ref[:, k * 1024:(k + 1) * 1024], w1_ref[k])
        hh, r = _rms(h1_ref[...])
        dg_ref[...] += jnp.sum(dhn * hh, axis=0, keepdims=True)
        dh1 = dh2_ref[...] + _rms_bwd(dhn, hh, r, g_ref[...])
        dh1_ref[...] = dh1
        dh1b_ref[...] = dh1.astype(MXU)

    return pl.pallas_call(
        _after(5, body), grid=(T // tm,), name="ff1_bwd",
        in_specs=[_rows(tm, DFF), _const((4, D, 1024)), _rows(tm, D), _rows(tm, D), _const((1, D)), _ANY],
        out_specs=[_rows(tm, D), _rows(tm, D), _const((1, D))],
        out_shape=[_sds((T, D), F32), _sds((T, D), MXU), _sds((1, D), F32)],
        compiler_params=_cp(),
    )(dpre, w1, dh2, h1, g, after)


def _outproj_bwd(dh1b, wo, tm):
    T = dh1b.shape[0]

    def body(d_ref, wo_ref, dcat_ref):
        d = d_ref[...]
        dcat_ref[:, 0:1024] = _dot_nt(d, wo_ref[0:1024, :])
        dcat_ref[:, 1024:2048] = _dot_nt(d, wo_ref[1024:2048, :])

    return pl.pallas_call(
        body, grid=(T // tm,), name="outproj_bwd",
        in_specs=[_rows(tm, D), _const((2048, D))],
        out_specs=_rows(tm, 2048),
        out_shape=_sds((T, 2048), F32),
        compiler_params=_cp(),
    )(dh1b, wo)


def _gmlp_bwd(uv, dcat, gv, ws, bst, gout, wm):
    T = uv.shape[0]
    nck = 2 if T % (2 * CH) == 0 else 1
    tb = nck * CH

    def body(uv_ref, dya_ref, gv_ref, ws_ref, bst_ref, gout_ref, wuv_ref, duv_ref, dgv_ref, dws_ref, dbst_ref,
             dgo_ref, dxn_ref):
        @pl.when(pl.program_id(0) == 0)
        def _():
            dgv_ref[...] = jnp.zeros_like(dgv_ref)
            dws_ref[...] = jnp.zeros_like(dws_ref)
            dbst_ref[...] = jnp.zeros_like(dbst_ref)
            dgo_ref[...] = jnp.zeros_like(dgo_ref)

        for k in range(nck):
            chunk(slice(k * CH, (k + 1) * CH), uv_ref, dya_ref, gv_ref, ws_ref, bst_ref, gout_ref, duv_ref,
                  dgv_ref, dws_ref, dbst_ref, dgo_ref)
        dxn_ref[...] = _dot_nt(duv_ref[...], wuv_ref[...])

    def chunk(rows, uv_ref, dya_ref, gv_ref, ws_ref, bst_ref, gout_ref, duv_ref, dgv_ref, dws_ref, dbst_ref,
              dgo_ref):
        gv = gv_ref[...]
        f = _gmlp_fwd_vals(uv_ref[rows, 0:1024], uv_ref[rows, 1024:2048], gv, ws_ref, bst_ref[...], gout_ref[...])
        dya = dya_ref[rows, :]
        dgo_ref[...] += jnp.sum(dya * f["yhat"], axis=0, keepdims=True)
        dy = _rms_bwd(dya, f["yhat"], f["ry"], gout_ref[...])
        lane = lax.broadcasted_iota(jnp.int32, (CH, 128), 1)
        dbs = jnp.zeros((CH, 128), F32)
        dug, dvg, dgvs = [], [], []
        for h in range(GM_HEADS):
            sl = slice(h * 128, (h + 1) * 128)
            vhat, rv, vn, wt, mixed = f["heads"][h]
            dyh = dy[:, sl]
            dug.append(dyh * mixed)
            dmixed = dyh * f["ug"][:, sl]
            dmb = dmixed.astype(MXU)
            dws_ref[h] += jnp.where(f["tril"], _dot_nt(dmb, vn), 0.0)
            dbs = dbs + jnp.where(lane == h, jnp.sum(dmixed, axis=1, keepdims=True), 0.0)
            dvn = _dot_tn(wt.astype(MXU), dmb)
            dgvs.append(jnp.sum(dvn * vhat, axis=0, keepdims=True))
            dvg.append(_rms_bwd(dvn, vhat, rv, gv[:, sl]))
        dbst_ref[...] += dbs
        dgv_ref[...] += jnp.concatenate(dgvs, axis=1)
        duv_ref[rows, 0:1024] = (jnp.concatenate(dug, axis=1) * f["dug"]).astype(MXU)
        duv_ref[rows, 1024:2048] = (jnp.concatenate(dvg, axis=1) * f["dvg"]).astype(MXU)

    return pl.pallas_call(
        body, grid=(T // tb,), name="gmlp_bwd",
        in_specs=[_rows(tb, 2048), _rows(tb, 1024, 0), _const((1, 1024)),
                  _const((GM_HEADS, CH, CH)), _const((CH, 128)), _const((1, 1024)), _const((D, 2048))],
        out_specs=[_rows(tb, 2048), _const((1, 1024)), _const((GM_HEADS, CH, CH)), _const((CH, 128)),
                   _const((1, 1024)), _rows(tb, D)],
        out_shape=[_sds((T, 2048), MXU), _sds((1, 1024), F32), _sds((GM_HEADS, CH, CH), F32), _sds((CH, 128), F32),
                   _sds((1, 1024), F32), _sds((T, D), F32)],
        compiler_params=_cp(),
    )(uv, dcat, gv, ws, bst, gout, wm)


def _ssd_bwd(pz, pxbc, dtraw, sall, dcat, convw, convb, dtb, alog, dskip, ng, ex, ltri, ext, nb, after):
    T = pz.shape[0]
    nch = T // CH // nb
    tok, in_specs = _ssd_specs(nch, rev=True)
    in_specs = in_specs + [
        _const((1024, 128)),
        pl.BlockSpec((1, 128, 1024), lambda b, c: (tok(b, c), 0, 0)),
        pl.BlockSpec((CH, 1024), lambda b, c: (tok(b, c), 1)),
        _ANY,
    ]

    def body(z_ref, xbc_ref, halo_ref, dt_ref, cw_ref, cb_ref, dtb_ref, al_ref, ds_ref, ng_ref, ex_ref, lt_ref,
             ext_ref, sall_ref, dyb_ref,
             dssd_ref, ddt_ref, dcw_ref, dcb_ref, ddtb_ref, dal_ref, dds_ref, dng_ref,
             dst_ref, dnext_ref, ddse_ref):
        b = pl.program_id(0)
        c = pl.program_id(1)

        @pl.when((b == 0) & (c == 0))
        def _():
            for r in (dcw_ref, dcb_ref, ddtb_ref, dal_ref, dds_ref, dng_ref, ddse_ref):
                r[...] = jnp.zeros_like(r)

        @pl.when(c == 0)
        def _():
            dst_ref[...] = jnp.zeros_like(dst_ref)
            dnext_ref[...] = jnp.zeros_like(dnext_ref)

        first_chunk = c == nch - 1
        halo = jnp.where(first_chunk, 0.0, halo_ref[...])
        z = z_ref[...]
        ex = ex_ref[...]
        ext = ext_ref[...]
        cw = cw_ref[...]
        ng = ng_ref[...]
        s_prev = sall_ref[0]
        f = _ssd_fwd_vals(z, xbc_ref[...], halo, dt_ref[...], cw, cb_ref[...], dtb_ref[...], al_ref[...],
                          ds_ref[...], ng, ex, lt_ref[...], s_prev)
        xs, xdt, cs, dec, dt = f["xs"], f["xdt"], f["cs"], f["dec"], f["dt"]
        dyb = dyb_ref[...]
        dyg, dngs = [], []
        for g in range(2):
            sl = slice(g * 512, (g + 1) * 512)
            dngs.append(jnp.sum(dyb[:, sl] * f["yhat"][g], axis=0, keepdims=True))
            dyg.append(_rms_bwd(dyb[:, sl], f["yhat"][g], f["rr"][g], ng[:, sl]))
        dng_ref[...] += jnp.concatenate(dngs, axis=1)
        dyg = jnp.concatenate(dyg, axis=1)
        sig_z = f["sig_z"]
        silu_z = z * sig_z
        dy = dyg * silu_z
        dz = dyg * f["ypre"] * sig_z * (1.0 + z * (1.0 - sig_z))
        ddse_ref[...] += jnp.sum(dy * xs, axis=0, keepdims=True)

        @pl.when((b == nb - 1) & (c == nch - 1))
        def _():
            dds_ref[...] = _dot_01(jnp.broadcast_to(ddse_ref[...], (8, 1024)), ext)[0:1]

        dxs = dy * f["de"]
        dye = dy * f["ecse"]
        dyeb = dye.astype(MXU)
        dst = dst_ref[...]
        dstb = dst.astype(MXU)
        bmb, cmb, sb, xdec = f["bmb"], f["cmb"], f["sb"], f["xdec"]
        u = jnp.concatenate([_dot(bmb[g], dstb[:, g * 512:(g + 1) * 512]) for g in range(2)], axis=1)
        dxdt = [u[:, q * 128:(q + 1) * 128] * f["dece"][:, q * 128:(q + 1) * 128] for q in range(8)]
        per_head = _dot_01(jnp.concatenate(
            [dy * f["yo"], u * xdt, jnp.broadcast_to(jnp.sum(dst * s_prev, axis=0, keepdims=True), (8, 1024))],
            axis=0), ext)
        dcs = per_head[0:CH]
        t = per_head[CH:2 * CH] * dec
        dcd = per_head[2 * CH:2 * CH + 1]
        row = lax.broadcasted_iota(jnp.int32, (CH, 128), 0)
        lane = lax.broadcasted_iota(jnp.int32, (CH, 128), 1)
        cd = jnp.exp(f["last"])
        dcs = dcs - t + jnp.where(row == CH - 1, jnp.sum(t, axis=0, keepdims=True) + dcd * cd, 0.0)
        dcst = jnp.zeros((128, CH), F32)
        lo = f["lo"]
        dbm, dcm, ds_prev = [], [], []
        for g in range(2):
            sl = slice(g * 512, (g + 1) * 512)
            dmg = jnp.zeros((CH, CH), F32)
            for q in range(4 * g, 4 * g + 4):
                dyq = dy[:, q * 128:(q + 1) * 128]
                xq = xdt[:, q * 128:(q + 1) * 128].astype(MXU)
                for hh in range(2):
                    h = 2 * q + hh
                    m = lo if hh == 0 else ~lo
                    dym = jnp.where(m, dyq, 0.0).astype(MXU)
                    gh = _dot_nt(dym, xq)
                    gl = gh * f["lms"][h]
                    dmg = dmg + gl
                    qh = gl * f["mg"][g]
                    dcs = dcs + jnp.where(lane == h, jnp.sum(qh, axis=1, keepdims=True), 0.0)
                    dcst = dcst - jnp.where(row == h, jnp.sum(qh, axis=0, keepdims=True), 0.0)
                    dxdt[q] = dxdt[q] + _dot_tn(f["whs"][h], dym)
            dmgb = dmg.astype(MXU)
            dcm.append(_dot(dmgb, bmb[g]) + _dot_nt(dyeb[:, sl], sb[:, sl]))
            dbm.append(_dot_tn(dmgb, cmb[g]) + _dot_nt(xdec[:, sl], dstb[:, sl]))
            ds_prev.append(_dot_tn(cmb[g], dyeb[:, sl]))
        dst_ref[...] = jnp.concatenate(ds_prev, axis=1) + dst * f["cde"]
        dcs = dcs + dcst.T
        da = _dot_hi(lt_ref[...].T, dcs)
        dxdt = jnp.concatenate(dxdt, axis=1)
        a_neg = f["a_neg"]
        ddt = da * a_neg + _dot_01(dxdt * xs, ext)
        dal_ref[...] += jnp.sum(da * dt, axis=0, keepdims=True) * a_neg
        dxs = dxs + dxdt * f["dte"]
        ddtraw = jnp.where(lane < SSD_HEADS, ddt * _sigmoid(f["dtpre"]), 0.0)
        ddtb_ref[...] += jnp.sum(ddtraw, axis=0, keepdims=True)
        ddt_ref[...] = ddtraw.astype(MXU)
        dxa = jnp.concatenate([dxs, dbm[0], dbm[1], dcm[0], dcm[1]], axis=1)
        sig_c = f["sig_c"]
        dconv = dxa * sig_c * (1.0 + f["conv"] * (1.0 - sig_c))
        dcb_ref[...] += jnp.sum(dconv, axis=0, keepdims=True)
        for k in range(4):
            dcw_ref[k:k + 1, :] += jnp.sum(dconv * f["shifts"][3 - k], axis=0, keepdims=True)
        dxbc = cw[3:4] * dconv
        for j, up in zip((1, 2, 3), _shifts_up(dconv, dnext_ref[...])):
            dxbc = dxbc + cw[3 - j:4 - j] * up
        dnext_ref[...] = dconv[0:8]
        dssd_ref[:, 0:1024] = dz.astype(MXU)
        dssd_ref[:, 1024:2560] = dxbc.astype(MXU)

    return pl.pallas_call(
        _after(15, body), grid=(nb, nch), name="ssd_bwd",
        in_specs=in_specs,
        out_specs=[pl.BlockSpec((CH, 2560), lambda b, c: (tok(b, c), 0)),
                   pl.BlockSpec((CH, 128), lambda b, c: (tok(b, c), 0)),
                   _const((8, CONV_CH)), _const((1, CONV_CH)), _const((1, 128)), _const((1, 128)), _const((1, 128)),
                   _const((1, 1024))],
        out_shape=[_sds((T, 2560), MXU), _sds((T, 128), MXU), _sds((8, CONV_CH), F32), _sds((1, CONV_CH), F32),
                   _sds((1, 128), F32), _sds((1, 128), F32), _sds((1, 128), F32), _sds((1, 1024), F32)],
        scratch_shapes=[pltpu.VMEM((128, 1024), F32), pltpu.VMEM((8, CONV_CH), F32), pltpu.VMEM((1, 1024), F32)],
        compiler_params=_cp(2),
    )(pz, pxbc, pxbc, dtraw, convw, convb, dtb, alog, dskip, ng, ex, ltri, ext, sall, dcat, after)


def _inproj_bwd(dxn_uv, dssd, ddt, wm, wdt, dh1, x, g, tm, after):
    T = x.shape[0]

    def body(dxnuv_ref, dssd_ref, ddt_ref, wm_ref, wdt_ref, dh1_ref, x_ref, g_ref, dx_ref, dg_ref):
        @pl.when(pl.program_id(0) == 0)
        def _():
            dg_ref[...] = jnp.zeros_like(dg_ref)

        dxn = (dxnuv_ref[...] + _dot_nt(dssd_ref[...], wm_ref[:, 2048:N_MAIN])
               + _dot_nt(ddt_ref[...], wdt_ref[...]))
        xh, r = _rms(x_ref[...])
        dg_ref[...] += jnp.sum(dxn * xh, axis=0, keepdims=True)
        dx_ref[...] = dh1_ref[...] + _rms_bwd(dxn, xh, r, g_ref[...])

    return pl.pallas_call(
        _after(8, body), grid=(T // tm,), name="inproj_bwd",
        in_specs=[_rows(tm, D), _rows(tm, 2560), _rows(tm, 128), _const((D, N_MAIN)), _const((D, 128)),
                  _rows(tm, D), _rows(tm, D), _const((1, D)), _ANY],
        out_specs=[_rows(tm, D), _const((1, D))],
        out_shape=[_sds((T, D), F32), _sds((1, D), F32)],
        compiler_params=_cp(),
    )(dxn_uv, dssd, ddt, wm, wdt, dh1, x, g, after)


def _matmul_tn(a, b, name, a_fn=None):
    T, M = a.shape
    N = b.shape[1]
    tm = min(M, 1024)
    tn = 1280 if N == 2560 else min(N, 1024)
    tk = min(T, 2048)

    def body(a_ref, b_ref, o_ref, acc_ref):
        k = pl.program_id(2)

        @pl.when(k == 0)
        def _():
            acc_ref[...] = jnp.zeros_like(acc_ref)

        av = a_ref[...]
        if a_fn is not None:
            av = a_fn(av)
        acc_ref[...] += _dot_tn(av, b_ref[...])

        @pl.when(k == T // tk - 1)
        def _():
            o_ref[...] = acc_ref[...].astype(o_ref.dtype)

    return pl.pallas_call(
        body, grid=(M // tm, N // tn, T // tk), name=name,
        in_specs=[pl.BlockSpec((tk, tm), lambda i, j, k: (k, i)), pl.BlockSpec((tk, tn), lambda i, j, k: (k, j))],
        out_specs=pl.BlockSpec((tm, tn), lambda i, j, k: (i, j)),
        out_shape=_sds((M, N), GRAD),
        scratch_shapes=[pltpu.VMEM((tm, tn), F32)],
        compiler_params=_cp(3),
    )(a, b)


def _adamw_vals(w, g, m, v):
    m = B1 * m + (1.0 - B1) * g
    v = B2 * v + (1.0 - B2) * (g * g)
    m_hat = m / (1.0 - B1 ** STEP)
    v_hat = v / (1.0 - B2 ** STEP)
    return -LR * (m_hat / (jnp.sqrt(v_hat) + ADAM_EPS) + WD * w), m, v


def _adamw(w, g, m, v, name):
    R, C = w.shape
    tr = 256 if R % 256 == 0 else R

    def body(w_ref, g_ref, m_ref, v_ref, d_ref, mo_ref, vo_ref):
        d_ref[...], mo_ref[...], vo_ref[...] = _adamw_vals(w_ref[...], g_ref[...], m_ref[...], v_ref[...])

    spec = _rows(tr, C)
    return pl.pallas_call(
        body, grid=(R // tr,), name=name,
        in_specs=[spec] * 4, out_specs=[spec] * 3, out_shape=[_sds((R, C), F32)] * 3,
        compiler_params=_cp(),
    )(w, g, m, v)


def _adamw_halves(w, own, other, m, v, name):
    R, C = w.shape
    half = R // 2
    tr = min(half, 256)
    nth = half // tr

    def body(w_ref, own_ref, oth_ref, m_ref, v_ref, g_ref, d_ref, mo_ref, vo_ref):
        mine = (pl.program_id(0) // nth) == lax.axis_index("c")
        g = jnp.where(mine, own_ref[...], oth_ref[...])
        g_ref[...] = g
        d_ref[...], mo_ref[...], vo_ref[...] = _adamw_vals(w_ref[...], g, m_ref[...], v_ref[...])

    full = _rows(tr, C)
    part = pl.BlockSpec((tr, C), lambda i: (i % nth, 0))
    return pl.pallas_call(
        body, grid=(R // tr,), name=name,
        in_specs=[full, part, part, full, full], out_specs=[full] * 4, out_shape=[_sds((R, C), F32)] * 4,
        compiler_params=_cp(),
    )(w, own, other, m, v)


def _sum_small(slots, name):
    nd, rows, C = slots.shape

    def body(s_ref, o_ref):
        acc = s_ref[0]
        for d in range(1, nd):
            acc = acc + s_ref[d]
        o_ref[...] = acc

    return pl.pallas_call(
        body, grid=(1,), name=name,
        in_specs=[_const((nd, rows, C))], out_specs=_const((rows, C)), out_shape=_sds((rows, C), F32),
        compiler_params=_cp(),
    )(slots)


def _sum_slots(slots, src, kind, shp, kh, name):
    R, C = shp
    rh = R // 2
    tr = min(rh, 256)
    nth = rh // tr
    if kind == "slab":
        src_spec = pl.BlockSpec((1, tr, C), lambda i, kh: (kh[0], kh[1] * nth + i, 0))
    elif kind == "rows":
        src_spec = pl.BlockSpec((tr, C), lambda i, kh: (kh[0] * (R // tr) + kh[1] * nth + i, 0))
    else:
        src_spec = pl.BlockSpec((tr, C), lambda i, kh: (kh[1] * nth + i, kh[0]))

    def body(kh_ref, s_ref, own_ref, o_ref):
        me = 2 * kh_ref[0] + kh_ref[1]
        acc = (own_ref[0] if kind == "slab" else own_ref[...]).astype(F32)
        for k in range(1, 8):
            acc = acc + s_ref[me ^ k].astype(F32)
        o_ref[...] = acc

    return pl.pallas_call(
        body, name=name,
        grid_spec=pltpu.PrefetchScalarGridSpec(
            num_scalar_prefetch=1, grid=(nth,),
            in_specs=[pl.BlockSpec((8, tr, C), lambda i, kh: (0, i, 0)), src_spec],
            out_specs=pl.BlockSpec((tr, C), lambda i, kh: (i, 0))),
        out_shape=_sds((rh, C), F32),
        compiler_params=_cp(),
    )(kh, slots, src)


def _assemble_w_in(slabs):
    tr = 256

    def body(s_ref, wm_ref, wdt_ref):
        full = jnp.concatenate([s_ref[k] for k in range(4)], axis=1)
        wm_ref[...] = full[:, :N_MAIN]
        wdt_ref[...] = jnp.concatenate([full[:, N_MAIN:], jnp.zeros((tr, 128 - 16), full.dtype)], axis=1)

    return pl.pallas_call(
        body, grid=(D // tr,), name="assemble_w_in",
        in_specs=[pl.BlockSpec((4, tr, 1156), lambda i: (0, i, 0))],
        out_specs=[_rows(tr, N_MAIN), _rows(tr, 128)],
        out_shape=[_sds((D, N_MAIN), slabs.dtype), _sds((D, 128), slabs.dtype)],
        compiler_params=_cp(),
    )(slabs)


def _split_dw_in(d_uv, d_ssd, d_dt):
    tr = 256

    def body(uv_ref, ssd_ref, dt_ref, o_ref):
        full = jnp.concatenate([uv_ref[...], ssd_ref[...], dt_ref[:, 0:16]], axis=1)
        for k in range(4):
            o_ref[k] = full[:, 1156 * k:1156 * (k + 1)]

    return pl.pallas_call(
        body, grid=(D // tr,), name="split_dw_in",
        in_specs=[_rows(tr, 2048), _rows(tr, 2560), _rows(tr, 128)],
        out_specs=pl.BlockSpec((4, tr, 1156), lambda i: (0, i, 0)),
        out_shape=_sds((4, D, 1156), d_uv.dtype),
        compiler_params=_cp(),
    )(d_uv, d_ssd, d_dt)


def _cast_into_slot(w, kh, name):
    R, C = w.shape
    tr = 256

    def body(kh_ref, w_ref, o_ref):
        o_ref[0] = w_ref[...].astype(BF16)

    return pl.pallas_call(
        body, name=name,
        grid_spec=pltpu.PrefetchScalarGridSpec(
            num_scalar_prefetch=1, grid=(R // tr,),
            in_specs=[pl.BlockSpec((tr, C), lambda i, kh: (i, 0))],
            out_specs=pl.BlockSpec((1, tr, C), lambda i, kh: (kh[0], i, 0))),
        out_shape=_sds((4, R, C), BF16),
        compiler_params=_cp(),
    )(kh, w)


_ANY = pl.BlockSpec(memory_space=pl.ANY)
_CHIP_FLIPS = [(1, 0), (0, 1), (1, 1)]
_DEVICE_FLIPS = [(fx, fy, fc) for fx in (0, 1) for fy in (0, 1) for fc in (0, 1)][1:]


def _half(h, rows):
    return pl.ds(pl.multiple_of(h * rows, rows), rows)


def _remote(src, dst, ssem, rsem, to):
    return pltpu.make_async_remote_copy(src_ref=src, dst_ref=dst, send_sem=ssem, recv_sem=rsem,
                                        device_id=to, device_id_type=MESH)


def _weight_gather(bufs, conv):
    n = len(bufs)

    def body(*refs):
        conv_ref, outs, conv_out = refs[n], refs[n + 1:2 * n + 1], refs[2 * n + 1]
        send_sems, recv_sems, fsend_sems, frecv_sems, csend_sems, crecv_sems, local_sem = refs[2 * n + 2:]
        x, y, c = lax.axis_index("x"), lax.axis_index("y"), lax.axis_index("c")
        me = 2 * x + y
        halves = [_half(c, r.shape[1] // 2) for r in outs]
        others = [_half(1 - c, r.shape[1] // 2) for r in outs]
        remote = _remote
        local = [pltpu.make_async_copy(conv_ref, conv_out.at[me], local_sem)]
        for cp in local:
            cp.start()
        sends = []
        for k, (fx, fy) in enumerate(_CHIP_FLIPS):
            peer = (x ^ fx, y ^ fy, c)
            for i in range(n):
                mine = outs[i].at[me, halves[i]]
                sends.append(remote(mine, mine, send_sems.at[k * n + i], recv_sems.at[k * n + i], peer))
            sends.append(remote(conv_ref, conv_out.at[me], csend_sems.at[k], crecv_sems.at[k], peer))
        for cp in sends:
            cp.start()
        sibling = (x, y, 1 - c)
        forwards = []
        for k, (fx, fy) in enumerate(_CHIP_FLIPS):
            peer = (x ^ fx, y ^ fy, c)
            src = 2 * (x ^ fx) + (y ^ fy)
            for i in range(n):
                landed = outs[i].at[src, halves[i]]
                remote(landed, landed, send_sems.at[k * n + i], recv_sems.at[k * n + i], peer).wait_recv()
                fw = remote(landed, landed, fsend_sems.at[k * n + i], frecv_sems.at[k * n + i], sibling)
                fw.start()
                forwards.append(fw)
            remote(conv_out.at[src], conv_out.at[src], csend_sems.at[k], crecv_sems.at[k], peer).wait_recv()
        for k, (fx, fy) in enumerate(_CHIP_FLIPS):
            src = 2 * (x ^ fx) + (y ^ fy)
            for i in range(n):
                theirs = outs[i].at[src, others[i]]
                remote(theirs, theirs, fsend_sems.at[k * n + i], frecv_sems.at[k * n + i], sibling).wait_recv()
        for cp in sends + forwards:
            cp.wait_send()
        for cp in local:
            cp.wait()

    dma = pltpu.SemaphoreType.DMA
    return pl.pallas_call(
        body, name="weight_gather",
        in_specs=[_ANY] * (n + 1), out_specs=[_ANY] * (n + 1),
        out_shape=[_sds(b.shape, b.dtype) for b in bufs] + [_sds((4,) + conv.shape, conv.dtype)],
        input_output_aliases={i: i for i in range(n)},
        scratch_shapes=[dma((3 * n,)), dma((3 * n,)), dma((3 * n,)), dma((3 * n,)), dma((3,)), dma((3,)), dma],
    )(*bufs, conv)


def _piece(ref, kind, R, C, k, h):
    if kind == "slab":
        return ref.at[k, _half(h, R // 2), :]
    if kind == "rows":
        return ref.at[pl.ds(pl.multiple_of(k * R + h * (R // 2), R // 2), R // 2), :]
    return ref.at[_half(h, R // 2), pl.ds(pl.multiple_of(k * C, C), C)]


def _small_exchange(small, after):
    rs = small.shape[0]

    def body(s_ref, after_ref, out_ref, send_sems, recv_sems, local_sem):
        del after_ref
        x, y, c = lax.axis_index("x"), lax.axis_index("y"), lax.axis_index("c")
        slot = 4 * x + 2 * y + c
        own = pltpu.make_async_copy(s_ref, out_ref.at[slot], local_sem)
        own.start()
        copies = []
        for k, (fx, fy, fc) in enumerate(_DEVICE_FLIPS):
            copies.append(_remote(s_ref, out_ref.at[slot], send_sems.at[k], recv_sems.at[k], (x ^ fx, y ^ fy, c ^ fc)))
        for cp in copies:
            cp.start()
        for k, (fx, fy, fc) in enumerate(_DEVICE_FLIPS):
            theirs = out_ref.at[slot ^ (k + 1)]
            _remote(theirs, theirs, send_sems.at[k], recv_sems.at[k], (x ^ fx, y ^ fy, c ^ fc)).wait_recv()
        for cp in copies:
            cp.wait_send()
        own.wait()

    dma = pltpu.SemaphoreType.DMA
    return pl.pallas_call(
        body, name="small_exchange",
        in_specs=[_ANY, _ANY], out_specs=_ANY, out_shape=_sds((8, rs, 128), F32),
        scratch_shapes=[dma((7,)), dma((7,)), dma],
    )(small, after)


_HBM = pl.BlockSpec(memory_space=pltpu.HBM)
_SEM = pl.BlockSpec(memory_space=pltpu.SEMAPHORE)


def _split_start(name, arrays, n_copies, plan, after=None):
    n = len(arrays)
    extra = [] if after is None else [after]

    def body(*refs):
        m = n + len(extra)
        arrs, send_sems, recv_sems, token = refs[:n], refs[m], refs[m + 1], refs[-1]
        for j, (src, dst, peer) in enumerate(plan(arrs)):
            _remote(src, dst, send_sems.at[j], recv_sems.at[j], peer).start()
        token[...] = jnp.zeros_like(token)

    dma = pltpu.SemaphoreType.DMA
    res = pl.pallas_call(
        body, name=name,
        out_shape=(dma((n_copies,)), dma((n_copies,)), *[pltpu.HBM(a.shape, a.dtype) for a in arrays],
                   _sds((8, 128), F32)),
        in_specs=[_HBM] * n + [_ANY] * len(extra),
        out_specs=(_SEM, _SEM, *[_HBM] * n, pl.BlockSpec(memory_space=pltpu.VMEM)),
        input_output_aliases={i: 2 + i for i in range(n)},
        compiler_params=pltpu.CompilerParams(has_side_effects=pltpu.SideEffectType.DATAFLOW_SIDE_EFFECTING),
    )(*[pltpu.with_memory_space_constraint(a, pltpu.HBM) for a in arrays], *extra)
    return res[0], res[1], list(res[2:2 + n]), res[-1]


def _split_wait(name, arrays, send_sems, recv_sems, plan, after):
    n = len(arrays)

    def body(*refs):
        arrs, ssems, rsems = refs[:n], refs[n], refs[n + 1]
        for j, (src, dst, peer) in enumerate(plan(arrs)):
            cp = _remote(src, dst, ssems.at[j], rsems.at[j], peer)
            cp.wait_send()
            cp.wait_recv()

    return list(pl.pallas_call(
        body, name=name,
        out_shape=tuple(pltpu.HBM(a.shape, a.dtype) for a in arrays),
        in_specs=[_HBM] * n + [_SEM, _SEM, _ANY],
        out_specs=tuple([_HBM] * n),
        input_output_aliases={i: i for i in range(n)},
        compiler_params=pltpu.CompilerParams(has_side_effects=pltpu.SideEffectType.DATAFLOW_SIDE_EFFECTING),
    )(*arrays, send_sems, recv_sems, after))


def _gather_plan(n):
    def plan(bufs):
        x, y, c = lax.axis_index("x"), lax.axis_index("y"), lax.axis_index("c")
        me = 2 * x + y
        return [(bufs[i].at[me], bufs[i].at[me], (x ^ fx, y ^ fy, c)) for fx, fy in _CHIP_FLIPS for i in range(n)]

    return plan


def _reduce_plan(specs, n_small):
    n = len(specs)

    def plan(arrs):
        x, y, c = lax.axis_index("x"), lax.axis_index("y"), lax.axis_index("c")
        slot = 4 * x + 2 * y + c
        out = []
        for fx, fy, fc in _DEVICE_FLIPS:
            peer = (x ^ fx, y ^ fy, c ^ fc)
            for i, (kind, (R, C)) in enumerate(specs):
                out.append((_piece(arrs[i], kind, R, C, 2 * peer[0] + peer[1], peer[2]), arrs[n + i].at[slot], peer))
            for s in range(n_small):
                out.append((arrs[2 * n + 2 * s], arrs[2 * n + 2 * s + 1].at[slot], peer))
        return out

    return plan


def _sibling_exchange(halves, name):
    n = len(halves)

    def body(*refs):
        ins, outs, send_sems, recv_sems = refs[:n], refs[n:2 * n], refs[2 * n], refs[2 * n + 1]
        sibling = (lax.axis_index("x"), lax.axis_index("y"), 1 - lax.axis_index("c"))
        copies = [pltpu.make_async_remote_copy(src_ref=ins[i], dst_ref=outs[i], send_sem=send_sems.at[i],
                                               recv_sem=recv_sems.at[i], device_id=sibling, device_id_type=MESH)
                  for i in range(n)]
        for cp in copies:
            cp.start()
        for cp in copies:
            cp.wait()

    dma = pltpu.SemaphoreType.DMA
    return pl.pallas_call(
        body, name=name,
        in_specs=[_ANY] * n, out_specs=[_ANY] * n,
        out_shape=[_sds(h.shape, h.dtype) for h in halves],
        scratch_shapes=[dma((n,)), dma((n,))],
    )(*halves)


_BIG = [("w_in", (1024, 1156), "slab"), ("w_out", (512, 1024), "rows"), ("w_ff1", (1024, 1024), "cols"),
        ("w_ff2", (1024, 1024), "rows"), ("w_ple_gate", (256, 1024), "rows"), ("w_ple_proj", (256, 256), "cols")]
_SMALL = [("norm_mix_g", (1, 1024)), ("gm_v_norm_g", (1, 1024)), ("gm_ws", (1, 8, 128, 128)), ("gm_bs", (1, 8, 128)),
          ("gm_out_norm_g", (1, 1024)), ("ssd_conv_w", (1, 4, 1536)), ("ssd_conv_b", (1, 1536)),
          ("ssd_dt_bias", (1, 16)), ("ssd_a_log", (1, 16)), ("ssd_d", (1, 16)), ("ssd_norm_g", (1, 1024)),
          ("norm_mlp_g", (1, 1024)), ("ple_norm_g", (1, 1024)), ("final_norm_g", (1024,))]


def _rows128(a):
    flat = a.reshape(-1)
    rows = -(-flat.shape[0] // 1024) * 8
    return jnp.pad(flat, (0, rows * 128 - flat.shape[0])).reshape(rows, 128)


def _pad_lanes(v, n=128):
    v = v.reshape(1, -1)
    return jnp.pad(v, ((0, 0), (0, n - v.shape[1])))


_SMALL_SHAPES = dict(_SMALL + [("loss", ())])
_BIG_SPECS = {n: (kind, shp) for n, shp, kind in _BIG}


class _Comm:
    def __init__(self, a, kh):
        self.a, self.kh = a, kh
        self.bufs = {n: _cast_into_slot(a[n].reshape(shp), kh, "cast_" + n) for n, shp, _ in _BIG}
        self.sent = []
        self.small_tot = {}

    def w_in(self):
        (g_win,), g_cw = self._gather_now()
        rest = [self.bufs[n] for n, _, _ in _BIG[1:]]
        plan = _gather_plan(len(rest))
        ssem, rsem, thru, token = _split_start("gather_start", rest, 3 * len(rest), plan, after=g_cw)
        self.gather = (plan, ssem, rsem, thru)
        wm, wdt = _assemble_w_in(g_win)
        return wm, wdt, jnp.concatenate([g_cw[k] for k in range(4)], axis=1), token

    def _gather_now(self):
        *bufs, g_cw = _weight_gather([self.bufs["w_in"]], self.a["ssd_conv_w"].reshape(4, 384))
        return bufs, g_cw

    def rest(self, after):
        plan, ssem, rsem, thru = self.gather
        g_wo, g_w1, g_w2, g_wg, g_wp = _split_wait("gather_wait", thru, ssem, rsem, plan, after)
        return g_wo.reshape(2048, D), g_w1, g_w2.reshape(DFF, D), g_wg.reshape(D, D), g_wp

    def send(self, tag, grads):
        big = [n for n, _, _ in _BIG if n in grads]
        small = [n for n in _SMALL_SHAPES if n in grads]
        parts = [_rows128(grads[n]) for n in small]
        rows = [s.shape[0] for s in parts]
        if not big:
            self.last_small = (tag, small, rows, jnp.concatenate(parts, axis=0))
            return None
        srcs = [grads[n] for n in big]
        lands = [lax.empty((8, _BIG_SPECS[n][1][0] // 2, _BIG_SPECS[n][1][1]), GRAD) for n in big]
        extra = []
        if small:
            pack = jnp.concatenate(parts, axis=0)
            extra = [pack, jnp.broadcast_to(pack, (8,) + pack.shape)]
        plan = _reduce_plan([_BIG_SPECS[n] for n in big], len(extra) // 2)
        n_copies = 7 * (len(big) + len(extra) // 2)
        ssem, rsem, thru, token = _split_start("reduce_start_" + tag, srcs + lands + extra, n_copies, plan)
        self.sent.append((tag, big, small, rows, plan, ssem, rsem, thru))
        return token

    def _unpack(self, tot, names, rows):
        o = 0
        for n, r in zip(names, rows):
            shp = _SMALL_SHAPES[n]
            cnt = 1
            for s in shp:
                cnt *= s
            self.small_tot[n] = tot[o:o + r].reshape(-1)[:cnt].reshape(shp)
            o += r

    def finish(self, after):
        a, results = self.a, {}

        def update(names, own, tag):
            other = _sibling_exchange([own[n] for n in names], "sibling_exchange_" + tag)
            for n, oth in zip(names, other):
                shp = _BIG_SPECS[n][1]
                results[n] = _adamw_halves(a[n].reshape(shp), own[n], oth, a["m_" + n].reshape(shp),
                                           a["v_" + n].reshape(shp), "adamw_" + n)
            return results[names[-1]][1]

        own, early = {}, []
        for tag, big, small, rows, plan, ssem, rsem, thru in self.sent:
            if tag == self.sent[-1][0]:
                after = update(early, own, "early")
            arrs = _split_wait("reduce_wait_" + tag, thru, ssem, rsem, plan, after)
            nb_ = len(big)
            for i, n in enumerate(big):
                kind, shp = _BIG_SPECS[n]
                own[n] = _sum_slots(arrs[nb_ + i], arrs[i], kind, shp, self.kh, "sum_" + n)
                after = own[n]
            early += big
            if small:
                self._unpack(_sum_small(arrs[2 * nb_ + 1], "sum_small_" + tag), small, rows)
        after = update(self.sent[-1][1], own, "late")
        tag, small, rows, pack = self.last_small
        self._unpack(_sum_small(_small_exchange(pack, after), "sum_small_" + tag), small, rows)
        return results, dict(self.small_tot)


def _local_step(x, p, tgt, sm, comm, nb, tm):
    wm, wdt, conv_w, token = comm.w_in()
    g_mix, gv, gout = sm["norm_mix_g"].reshape(1, D), sm["gm_v_norm_g"].reshape(1, D), sm["gm_out_norm_g"].reshape(1, D)
    ws = sm["gm_ws"].reshape(GM_HEADS, CH, CH)
    bst = jnp.pad(sm["gm_bs"].reshape(GM_HEADS, CH).T, ((0, 0), (0, 128 - GM_HEADS)))
    convw = jnp.pad(conv_w, ((0, 4), (0, 0)))
    convb = sm["ssd_conv_b"].reshape(1, CONV_CH)
    dtb, alog = _pad_lanes(sm["ssd_dt_bias"]), _pad_lanes(sm["ssd_a_log"])
    dskip = jnp.repeat(sm["ssd_d"].reshape(SSD_HEADS), SSD_P).reshape(1, 1024)
    ng, g_mlp, g_ple = sm["ssd_norm_g"].reshape(1, D), sm["norm_mlp_g"].reshape(1, D), sm["ple_norm_g"].reshape(1, D)
    gf = sm["final_norm_g"].reshape(1, D)
    head_of_lane = lax.broadcasted_iota(jnp.int32, (128, 1024), 1) // SSD_P
    ex = (lax.broadcasted_iota(jnp.int32, (128, 1024), 0) == head_of_lane).astype(BF16)
    ext = ex.T
    ltri = (lax.broadcasted_iota(jnp.int32, (CH, CH), 0) >= lax.broadcasted_iota(jnp.int32, (CH, CH), 1)).astype(F32)

    pz, pxbc, dtraw, xn, cat, uv = _inproj_gmlp(x, g_mix, wm, wdt, gv, ws, bst, gout, tm // 2, token)
    cat, sall = _ssd_fwd(pz, pxbc, dtraw, cat, convw, convb, dtb, alog, dskip, ng, ex, ltri, nb)
    wo, w1, w2, wg, wp = comm.rest(cat)
    h1, hn = _outproj(cat, wo, x, g_mlp, tm)
    hid = _ff1(hn, w1, tm)
    hp, dgl, dpe, dh2, dh2b, loss, d_gf, d_gple = _ff2_tail(hid, w2, h1, g_ple, p, tgt, wg, wp, gf, tm // 2)

    d_wp = _matmul_tn(p, dpe, "dw_ple_proj", a_fn=lambda a: a.astype(MXU))
    d_wg = _matmul_tn(hp, dgl, "dw_ple_gate")
    d_w2 = _matmul_tn(hid, dh2b, "dw_ff2", a_fn=_sq)
    dpre = _ff2_bwd(dh2b, w2, hid, tm)
    d_w1 = _matmul_tn(hn, dpre, "dw_ff1")
    token = comm.send("a", {"w_ple_proj": d_wp, "w_ple_gate": d_wg, "w_ff2": d_w2, "w_ff1": d_w1})
    dh1, dh1b, d_gmlp = _ff1_bwd(dpre, w1, dh2, h1, g_mlp, tm, token)
    dcat = _outproj_bwd(dh1b, wo, tm)
    d_wo = _matmul_tn(cat, dh1b, "dw_out")
    duv, d_gv, d_ws, d_bst, d_gout, dxn_uv = _gmlp_bwd(uv, dcat, gv, ws, bst, gout, wm)
    token = comm.send("b", {
        "w_out": d_wo, "loss": loss[0:1, 0:1], "final_norm_g": d_gf, "ple_norm_g": d_gple, "norm_mlp_g": d_gmlp,
        "gm_v_norm_g": d_gv, "gm_ws": d_ws, "gm_bs": d_bst[:, :GM_HEADS].T, "gm_out_norm_g": d_gout})
    dssd, ddt, d_cw, d_cb, d_dtb, d_al, d_ds, d_ng = _ssd_bwd(
        pz, pxbc, dtraw, sall, dcat, convw, convb, dtb, alog, dskip, ng, ex, ltri, ext, nb, token)
    d_win = _split_dw_in(_matmul_tn(xn, duv, "dw_in_uv"), _matmul_tn(xn, dssd, "dw_in_ssd"),
                         _matmul_tn(xn, ddt, "dw_in_dt"))
    token = comm.send("c", {"w_in": d_win})
    dx, d_gmix = _inproj_bwd(dxn_uv, dssd, ddt, wm, wdt, dh1, x, g_mix, tm, token)
    comm.send("d", {"norm_mix_g": d_gmix, "ssd_conv_w": d_cw[0:4], "ssd_conv_b": d_cb, "ssd_dt_bias": d_dtb[:, :16],
                    "ssd_a_log": d_al[:, :16], "ssd_d": d_ds[:, :16], "ssd_norm_g": d_ng})
    return dx


def kernel(x, p, norm_mix_g, w_in, gm_v_norm_g, gm_ws, gm_bs, gm_out_norm_g, ssd_conv_w, ssd_conv_b, ssd_dt_bias, ssd_a_log, ssd_d, ssd_norm_g, w_out, norm_mlp_g, w_ff1, w_ff2, ple_norm_g, w_ple_gate, w_ple_proj, final_norm_g, loss_target, m_norm_mix_g, m_w_in, m_gm_v_norm_g, m_gm_ws, m_gm_bs, m_gm_out_norm_g, m_ssd_conv_w, m_ssd_conv_b, m_ssd_dt_bias, m_ssd_a_log, m_ssd_d, m_ssd_norm_g, m_w_out, m_norm_mlp_g, m_w_ff1, m_w_ff2, m_ple_norm_g, m_w_ple_gate, m_w_ple_proj, m_final_norm_g, v_norm_mix_g, v_w_in, v_gm_v_norm_g, v_gm_ws, v_gm_bs, v_gm_out_norm_g, v_ssd_conv_w, v_ssd_conv_b, v_ssd_dt_bias, v_ssd_a_log, v_ssd_d, v_ssd_norm_g, v_w_out, v_norm_mlp_g, v_w_ff1, v_w_ff2, v_ple_norm_g, v_w_ple_gate, v_w_ple_proj, v_final_norm_g):
    a = dict(locals())
    order = ["norm_mix_g", "w_in", "gm_v_norm_g", "gm_ws", "gm_bs", "gm_out_norm_g", "ssd_conv_w", "ssd_conv_b",
             "ssd_dt_bias", "ssd_a_log", "ssd_d", "ssd_norm_g", "w_out", "norm_mlp_g", "w_ff1", "w_ff2", "ple_norm_g",
             "w_ple_gate", "w_ple_proj", "final_norm_g"]
    chip = 2 * lax.axis_index("x") + lax.axis_index("y")
    nb, S = x.shape[0], x.shape[1]
    T = nb * S
    sm = {n: a[n] for n, _ in _SMALL if n != "ssd_conv_w"}
    comm = _Comm(a, jnp.stack([chip, lax.axis_index("c")]).astype(jnp.int32))
    dx = _local_step(x.reshape(T, D), p.reshape(T, DPLE), loss_target.reshape(T, D), sm, comm, nb, 512)
    big, g_out = comm.finish(dx)
    delta, new_m, new_v = {}, {}, {}
    for n, _, _ in _BIG:
        g_out[n], delta[n], new_m[n], new_v[n] = (r.reshape(a[n].shape) for r in big[n])
    g_out["ssd_conv_w"] = lax.dynamic_slice(g_out["ssd_conv_w"], (0, 0, chip * 384), (1, 4, 384))
    small_names = [n for n, _ in _SMALL]
    packs = [jnp.concatenate([_rows128(src(n)) for n in small_names], axis=0)
             for src in (lambda n: a[n], lambda n: g_out[n], lambda n: a["m_" + n], lambda n: a["v_" + n])]
    outs = _adamw(*packs, "adamw_small")
    o = 0
    for n in small_names:
        r = _rows128(a[n]).shape[0]
        cnt = a[n].size
        for dst, src in zip((delta, new_m, new_v), outs):
            dst[n] = src[o:o + r].reshape(-1)[:cnt].reshape(a[n].shape)
        o += r
    return (g_out["loss"], dx.reshape(x.shape), *[g_out[n] for n in order], *[delta[n] for n in order],
            *[new_m[n] for n in order], *[new_v[n] for n in order])
```

```python
import jax
import jax.numpy as jnp
from jax import lax
from jax.experimental import pallas as pl
from jax.experimental.pallas import tpu as pltpu

F32 = jnp.float32
BF16 = jnp.bfloat16
MXU = jnp.bfloat16
GRAD = jnp.bfloat16

D = 1024
CH = 128
GM_HEADS = 8
SSD_HEADS = 16
SSD_P = 64
CONV_CH = 1536
N_MAIN = 4608
DFF = 4096
DPLE = 256
EPS = 1e-6
NEG = -1e30

LR, B1, B2, ADAM_EPS, WD, STEP = 0.001, 0.9, 0.999, 1e-08, 0.01, 10

VMEM_LIMIT = 56 * 1024 * 1024
_SEQS_PER_STEP = 2
MESH = pl.DeviceIdType.MESH

INV_SQRT2 = 0.7071067811865476
INV_SQRT_2PI = 0.3989422804014327


def _cp(n_axes=1):
    return pltpu.CompilerParams(dimension_semantics=("arbitrary",) * n_axes, vmem_limit_bytes=VMEM_LIMIT)


def _dot(a, b):
    return jnp.dot(a, b, preferred_element_type=F32)


def _dot_nt(a, b):
    return lax.dot_general(a, b, (((1,), (1,)), ((), ())), preferred_element_type=F32)


def _dot_tn(a, b):
    return lax.dot_general(a, b, (((0,), (0,)), ((), ())), preferred_element_type=F32)


def _dot_hi(a, b):
    return jnp.dot(a, b, preferred_element_type=F32, precision=lax.Precision.HIGHEST)


def _dot_01(a, sel):
    hi = a.astype(BF16)
    lo = (a - hi.astype(F32)).astype(BF16)
    n = a.shape[0]
    r = _dot(jnp.concatenate([hi, lo], axis=0), sel)
    return r[0:n] + r[n:2 * n]


def _rows(tm, n, j=0):
    return pl.BlockSpec((tm, n), lambda i: (i, j))


def _const(shape):
    nd = len(shape)
    return pl.BlockSpec(shape, lambda *_: (0,) * nd)


def _sds(shape, dtype):
    return jax.ShapeDtypeStruct(shape, dtype)


def _rms(x):
    r = lax.rsqrt(jnp.mean(x * x, axis=-1, keepdims=True) + EPS)
    return x * r, r


def _rms_bwd(dy, xhat, r, g):
    dyg = dy * g
    return r * (dyg - xhat * jnp.mean(dyg * xhat, axis=-1, keepdims=True))


def _sigmoid(x):
    return 1.0 / (1.0 + jnp.exp(-x))


def _gelu(x):
    cdf = 0.5 * (1.0 + lax.erf(x * INV_SQRT2))
    pdf = jnp.exp(-0.5 * x * x) * INV_SQRT_2PI
    return x * cdf, cdf + x * pdf


def _softplus(x):
    e = jnp.exp(-jnp.abs(x))
    u = 1.0 + e
    log1p = jnp.where(u == 1.0, e, jnp.log(u) * e / (u - 1.0))
    return jnp.maximum(x, 0.0) + log1p


def _after(n_in, fn):
    def body(*refs):
        return fn(*refs[:n_in], *refs[n_in + 1:])

    return body


def _inproj_gmlp(x, g, wm, wdt, gv, ws, bst, gout, tm, after):
    T = x.shape[0]

    def body(x_ref, g_ref, wm_ref, wdt_ref, gv_ref, ws_ref, bst_ref, gout_ref,
             z_ref, xbc_ref, dt_ref, xn_ref, ya_ref, uv_ref):
        xh, _ = _rms(x_ref[...])
        xn = (xh * g_ref[...]).astype(MXU)
        xn_ref[...] = xn
        for n in range(4):
            uv_ref[:, n * 512:(n + 1) * 512] = _dot(xn, wm_ref[:, n * 512:(n + 1) * 512])
        for n in range(2):
            z_ref[:, n * 512:(n + 1) * 512] = _dot(xn, wm_ref[:, 2048 + n * 512:2048 + (n + 1) * 512])
        for n in range(3):
            xbc_ref[:, n * 512:(n + 1) * 512] = _dot(xn, wm_ref[:, 3072 + n * 512:3072 + (n + 1) * 512])
        dt_ref[...] = _dot(xn, wdt_ref[...])
        for k in range(tm // CH):
            rows = slice(k * CH, (k + 1) * CH)
            f = _gmlp_fwd_vals(uv_ref[rows, 0:1024], uv_ref[rows, 1024:2048], gv_ref[...], ws_ref, bst_ref[...],
                               gout_ref[...])
            ya_ref[rows, :] = f["out"].astype(MXU)

    return pl.pallas_call(
        _after(8, body), grid=(T // tm,), name="inproj_gmlp",
        in_specs=[_rows(tm, D), _const((1, D)), _const((D, N_MAIN)), _const((D, 128)), _const((1, 1024)),
                  _const((GM_HEADS, CH, CH)), _const((CH, 128)), _const((1, 1024)), _ANY],
        out_specs=[_rows(tm, 1024), _rows(tm, CONV_CH), _rows(tm, 128), _rows(tm, D), _rows(tm, 1024, 0),
                   _rows(tm, 2048)],
        out_shape=[_sds((T, 1024), F32), _sds((T, CONV_CH), F32), _sds((T, 128), F32), _sds((T, D), MXU),
                   _sds((T, 2048), MXU), _sds((T, 2048), F32)],
        compiler_params=_cp(),
    )(x, g, wm, wdt, gv, ws, bst, gout, after)


def _gmlp_fwd_vals(u, v, gv, ws_ref, bst, gout):
    ug, dug = _gelu(u)
    vg, dvg = _gelu(v)
    row = lax.broadcasted_iota(jnp.int32, (CH, CH), 0)
    col = lax.broadcasted_iota(jnp.int32, (CH, CH), 1)
    tril = row >= col
    ys, heads = [], []
    for h in range(GM_HEADS):
        sl = slice(h * 128, (h + 1) * 128)
        vhat, rv = _rms(vg[:, sl])
        vn = (vhat * gv[:, sl]).astype(MXU)
        wt = jnp.where(tril, ws_ref[h], 0.0)
        mixed = _dot(wt.astype(MXU), vn) + bst[:, h:h + 1]
        ys.append(ug[:, sl] * mixed)
        heads.append((vhat, rv, vn, wt, mixed))
    y = jnp.concatenate(ys, axis=1)
    yhat, ry = _rms(y)
    return dict(ug=ug, dug=dug, dvg=dvg, heads=heads, yhat=yhat, ry=ry, tril=tril, out=yhat * gout)


def _shifts_down(cur, halo):
    row8 = lax.broadcasted_iota(jnp.int32, (8, cur.shape[1]), 0)
    out = [cur]
    for j in (1, 2, 3):
        sh = pltpu.roll(cur, j, 0)
        top = jnp.where(row8 < j, pltpu.roll(halo, j, 0), sh[0:8])
        out.append(jnp.concatenate([top, sh[8:]], axis=0))
    return out


def _shifts_up(cur, halo):
    row8 = lax.broadcasted_iota(jnp.int32, (8, cur.shape[1]), 0)
    out = []
    for j in (1, 2, 3):
        sh = pltpu.roll(cur, CH - j, 0)
        bot = jnp.where(row8 + j >= 8, pltpu.roll(halo, 8 - j, 0), sh[CH - 8:CH])
        out.append(jnp.concatenate([sh[0:CH - 8], bot], axis=0))
    return out


def _conv(xbc, halo, convw, convb):
    sh = _shifts_down(xbc, halo)
    return convb + convw[3:4] * sh[0] + convw[2:3] * sh[1] + convw[1:2] * sh[2] + convw[0:1] * sh[3]


def _ssd_fwd_vals(z, conv, dtraw, dtb, alog, dskip, ng, ex, ltri, s_prev):
    sig_c = _sigmoid(conv)
    xa = conv * sig_c
    xs = xa[:, :1024]
    bm = [xa[:, 1024:1152], xa[:, 1152:1280]]
    cm = [xa[:, 1280:1408], xa[:, 1408:1536]]
    dtpre = dtraw + dtb
    dt = _softplus(dtpre)
    a_neg = -jnp.exp(alog)
    cs = _dot_hi(ltri, dt * a_neg)
    cst = cs.T
    last = cs[CH - 1:CH]
    ecs = jnp.exp(cs)
    dec = jnp.exp(last - cs)
    spread = _dot_01(jnp.concatenate([dt, ecs, dec], axis=0), ex)
    dte, ecse, dece = spread[0:CH], spread[CH:2 * CH], spread[2 * CH:3 * CH]
    cde = ecse[CH - 1:CH]
    de = dskip
    xdt = xs * dte
    row = lax.broadcasted_iota(jnp.int32, (CH, CH), 0)
    col = lax.broadcasted_iota(jnp.int32, (CH, CH), 1)
    tril = row >= col
    lo = col < SSD_P
    bmb = [b.astype(MXU) for b in bm]
    cmb = [c.astype(MXU) for c in cm]
    mg = [_dot_nt(cmb[g], bmb[g]) for g in range(2)]
    yd, lms, whs = [], [], []
    for q in range(8):
        g = q // 4
        xq = xdt[:, q * 128:(q + 1) * 128]
        acc = None
        for hh in range(2):
            h = 2 * q + hh
            seg = cs[:, h:h + 1] - cst[h:h + 1, :]
            lm = jnp.exp(jnp.where(tril, seg, NEG))
            wh = (mg[g] * lm).astype(MXU)
            xm = jnp.where(lo if hh == 0 else ~lo, xq, 0.0).astype(MXU)
            part = _dot(wh, xm)
            acc = part if acc is None else acc + part
            lms.append(lm)
            whs.append(wh)
        yd.append(acc)
    yd = jnp.concatenate(yd, axis=1)
    sb = s_prev.astype(MXU)
    yo = jnp.concatenate([_dot(cmb[g], sb[:, g * 512:(g + 1) * 512]) for g in range(2)], axis=1) * ecse
    xdec = (xdt * dece).astype(MXU)
    states = jnp.concatenate([_dot_tn(bmb[g], xdec[:, g * 512:(g + 1) * 512]) for g in range(2)], axis=1)
    s_next = s_prev * cde + states
    ypre = yd + yo + de * xs
    sig_z = _sigmoid(z)
    yg = ypre * z * sig_z
    outs, yhat, rr = [], [], []
    for g in range(2):
        sl = slice(g * 512, (g + 1) * 512)
        yh, r = _rms(yg[:, sl])
        yhat.append(yh)
        rr.append(r)
        outs.append(yh * ng[:, sl])
    return dict(sig_c=sig_c, xa=xa, xs=xs, bmb=bmb, cmb=cmb, dtpre=dtpre, dt=dt, a_neg=a_neg,
                cs=cs, last=last, ecs=ecs, dec=dec, dte=dte, ecse=ecse, dece=dece, cde=cde, de=de, xdt=xdt,
                mg=mg, lms=lms, whs=whs, lo=lo, yo=yo, sb=sb, xdec=xdec, s_next=s_next, ypre=ypre, sig_z=sig_z,
                yhat=yhat, rr=rr, out=jnp.concatenate(outs, axis=1))


def _ssd_fwd(pz, pxbc, dtraw, cat, convw, convb, dtb, alog, dskip, ng, ex, ltri, nb):
    T = pz.shape[0]
    S = T // nb
    nch = S // CH
    ns = _SEQS_PER_STEP if nb % _SEQS_PER_STEP == 0 else 1

    def body(z_ref, xbc_ref, halo_ref, dt_ref, cw_ref, cb_ref, dtb_ref, al_ref, ds_ref, ng_ref, ex_ref, lt_ref,
             cat_in_ref, yb_ref, sall_ref, conv_ref, s_ref):
        del cat_in_ref
        c = pl.program_id(1)

        @pl.when(c == 0)
        def _():
            s_ref[...] = jnp.zeros_like(s_ref)

        for i in range(ns):
            halo = jnp.where(c == 0, 0.0, halo_ref[i])
            s_prev = s_ref[i]
            sall_ref[i, 0] = s_prev
            conv = _conv(xbc_ref[i], halo, cw_ref[...], cb_ref[...])
            conv_ref[i] = conv
            f = _ssd_fwd_vals(z_ref[i], conv, dt_ref[i], dtb_ref[...], al_ref[...], ds_ref[...], ng_ref[...],
                              ex_ref[...], lt_ref[...], s_prev)
            s_ref[i] = f["s_next"]
            yb_ref[i] = f["out"].astype(MXU)

    def seq(width, col=0):
        return pl.BlockSpec((ns, CH, width), lambda b, c: (b, c, col))

    cat, sall, conv = pl.pallas_call(
        body, grid=(nb // ns, nch), name="ssd_fwd",
        in_specs=[seq(1024), seq(CONV_CH),
                  pl.BlockSpec((ns, 8, CONV_CH), lambda b, c: (b, jnp.maximum(c * (CH // 8) - 1, 0), 0)),
                  seq(128),
                  _const((8, CONV_CH)), _const((1, CONV_CH)), _const((1, 128)), _const((1, 128)), _const((1, 1024)),
                  _const((1, 1024)), _const((128, 1024)), _const((CH, CH)), _ANY],
        out_specs=[seq(1024, 1), pl.BlockSpec((ns, 1, 128, 1024), lambda b, c: (b, c, 0, 0)), seq(CONV_CH)],
        out_shape=[_sds((nb, S, 2048), MXU), _sds((nb, nch, 128, 1024), F32), _sds((nb, S, CONV_CH), F32)],
        scratch_shapes=[pltpu.VMEM((ns, 128, 1024), F32)],
        input_output_aliases={12: 0},
        compiler_params=_cp(2),
    )(pz.reshape(nb, S, 1024), pxbc.reshape(nb, S, CONV_CH), pxbc.reshape(nb, S, CONV_CH), dtraw.reshape(nb, S, 128),
      convw, convb, dtb, alog, dskip, ng, ex, ltri, cat.reshape(nb, S, 2048))
    return cat.reshape(T, 2048), sall, conv.reshape(T, CONV_CH)


def _outproj(cat, wo, x, g, tm):
    T = x.shape[0]

    def body(cat_ref, wo_ref, x_ref, g_ref, h1_ref, hn_ref):
        h1 = x_ref[...] + _dot(cat_ref[...], wo_ref[...])
        h1_ref[...] = h1
        hn_ref[...] = (_rms(h1)[0] * g_ref[...]).astype(MXU)

    return pl.pallas_call(
        body, grid=(T // tm,), name="outproj",
        in_specs=[_rows(tm, 2048), _const((2048, D)), _rows(tm, D), _const((1, D))],
        out_specs=[_rows(tm, D), _rows(tm, D)],
        out_shape=[_sds((T, D), F32), _sds((T, D), MXU)],
        compiler_params=_cp(),
    )(cat, wo, x, g)


def _ff1(hn, w1, tm):
    T = hn.shape[0]

    def body(hn_ref, w1_ref, hid_ref):
        hn_v = hn_ref[...]
        for n in range(4):
            hid_ref[:, n * 1024:(n + 1) * 1024] = jnp.maximum(_dot(hn_v, w1_ref[n]), 0.0).astype(MXU)

    return pl.pallas_call(
        body, grid=(T // tm,), name="ff1",
        in_specs=[_rows(tm, D), _const((4, D, 1024))],
        out_specs=_rows(tm, DFF),
        out_shape=_sds((T, DFF), MXU),
        compiler_params=_cp(),
    )(hn, w1)


def _sq(hid):
    h = hid.astype(F32)
    return (h * h).astype(MXU)


def _ff2_tail(hid, w2, h1, g_ple, p, tgt, wg, wp, gf, tm):
    T = h1.shape[0]

    def body(hid_ref, w2_ref, h1_ref, g_ref, p_ref, t_ref, wg_ref, wp_ref, gf_ref,
             hp_ref, dgl_ref, dpe_ref, dh2_ref, dh2b_ref, loss_ref, dgf_ref, dg_ref):
        @pl.when(pl.program_id(0) == 0)
        def _():
            loss_ref[...] = jnp.zeros_like(loss_ref)
            dgf_ref[...] = jnp.zeros_like(dgf_ref)
            dg_ref[...] = jnp.zeros_like(dg_ref)

        h2 = h1_ref[...] + _dot(_sq(hid_ref[...]), w2_ref[...])
        h2h, r2 = _rms(h2)
        g_ple = g_ref[...]
        hp = (h2h * g_ple).astype(MXU)
        hp_ref[...] = hp
        gate = _sigmoid(_dot(hp, wg_ref[...]))
        pb = p_ref[...].astype(MXU)
        pe = jnp.concatenate([_dot(pb, wp_ref[k]) for k in range(4)], axis=1)
        h3 = h2 + gate * pe
        hh, r = _rms(h3)
        gf = gf_ref[...]
        diff = hh * gf - t_ref[...]
        loss_ref[...] += 0.5 * jnp.sum(jnp.mean(diff * diff, axis=-1, keepdims=True))
        dout = diff * (1.0 / D)
        dgf_ref[...] += jnp.sum(dout * hh, axis=0, keepdims=True)
        dh3 = _rms_bwd(dout, hh, r, gf)
        dgl = (dh3 * pe * gate * (1.0 - gate)).astype(MXU)
        dgl_ref[...] = dgl
        dpe_ref[...] = (dh3 * gate).astype(MXU)
        dhp = _dot_nt(dgl, wg_ref[...])
        dg_ref[...] += jnp.sum(dhp * h2h, axis=0, keepdims=True)
        dh2 = dh3 + _rms_bwd(dhp, h2h, r2, g_ple)
        dh2_ref[...] = dh2
        dh2b_ref[...] = dh2.astype(MXU)

    return pl.pallas_call(
        body, grid=(T // tm,), name="ff2_tail",
        in_specs=[_rows(tm, DFF), _const((DFF, D)), _rows(tm, D), _const((1, D)), _rows(tm, DPLE), _rows(tm, D),
                  _const((D, D)), _const((4, DPLE, 256)), _const((1, D))],
        out_specs=[_rows(tm, D), _rows(tm, D), _rows(tm, D), _rows(tm, D), _rows(tm, D), _const((8, 128)),
                   _const((1, D)), _const((1, D))],
        out_shape=[_sds((T, D), MXU), _sds((T, D), MXU), _sds((T, D), MXU), _sds((T, D), F32), _sds((T, D), MXU),
                   _sds((8, 128), F32), _sds((1, D), F32), _sds((1, D), F32)],
        compiler_params=_cp(),
    )(hid, w2, h1, g_ple, p, tgt, wg, wp, gf)


def _ff2_bwd(dh2b, w2, hid, tm):
    T = hid.shape[0]

    def body(dh2b_ref, w2_ref, hid_ref, dpre_ref):
        d = dh2b_ref[...]
        for n in range(DFF // 1024):
            sl = slice(n * 1024, (n + 1) * 1024)
            da = _dot_nt(d, w2_ref[sl, :])
            dpre_ref[:, sl] = (2.0 * da * hid_ref[:, sl].astype(F32)).astype(MXU)

    return pl.pallas_call(
        body, grid=(T // tm,), name="ff2_bwd",
        in_specs=[_rows(tm, D), _const((DFF, D)), _rows(tm, DFF)],
        out_specs=_rows(tm, DFF),
        out_shape=_sds((T, DFF), MXU),
        compiler_params=_cp(),
    )(dh2b, w2, hid)


def _ff1_bwd(dpre, w1, dh2, h1, g, tm, after):
    T = h1.shape[0]

    def body(dpre_ref, w1_ref, dh2_ref, h1_ref, g_ref, dh1_ref, dh1b_ref, dg_ref):
        @pl.when(pl.program_id(0) == 0)
        def _():
            dg_ref[...] = jnp.zeros_like(dg_ref)

        dhn = _dot_nt(dpre_ref[:, 0:1024], w1_ref[0])
        for k in range(1, 4):
            dhn = dhn + _dot_nt(dpre_ref[:, k * 1024:(k + 1) * 1024], w1_ref[k])
        hh, r = _rms(h1_ref[...])
        dg_ref[...] += jnp.sum(dhn * hh, axis=0, keepdims=True)
        dh1 = dh2_ref[...] + _rms_bwd(dhn, hh, r, g_ref[...])
        dh1_ref[...] = dh1
        dh1b_ref[...] = dh1.astype(MXU)

    return pl.pallas_call(
        _after(5, body), grid=(T // tm,), name="ff1_bwd",
        in_specs=[_rows(tm, DFF), _const((4, D, 1024)), _rows(tm, D), _rows(tm, D), _const((1, D)), _ANY],
        out_specs=[_rows(tm, D), _rows(tm, D), _const((1, D))],
        out_shape=[_sds((T, D), F32), _sds((T, D), MXU), _sds((1, D), F32)],
        compiler_params=_cp(),
    )(dpre, w1, dh2, h1, g, after)


def _outproj_bwd(dh1b, wo, tm):
    T = dh1b.shape[0]

    def body(d_ref, wo_ref, dcat_ref):
        d = d_ref[...]
        dcat_ref[:, 0:1024] = _dot_nt(d, wo_ref[0:1024, :])
        dcat_ref[:, 1024:2048] = _dot_nt(d, wo_ref[1024:2048, :])

    return pl.pallas_call(
        body, grid=(T // tm,), name="outproj_bwd",
        in_specs=[_rows(tm, D), _const((2048, D))],
        out_specs=_rows(tm, 2048),
        out_shape=_sds((T, 2048), F32),
        compiler_params=_cp(),
    )(dh1b, wo)


def _gmlp_bwd(uv, dcat, gv, ws, bst, gout, wm):
    T = uv.shape[0]
    nck = 2 if T % (2 * CH) == 0 else 1
    tb = nck * CH

    def body(uv_ref, dya_ref, gv_ref, ws_ref, bst_ref, gout_ref, wuv_ref, duv_ref, dgv_ref, dws_ref, dbst_ref,
             dgo_ref, dxn_ref):
        @pl.when(pl.program_id(0) == 0)
        def _():
            dgv_ref[...] = jnp.zeros_like(dgv_ref)
            dws_ref[...] = jnp.zeros_like(dws_ref)
            dbst_ref[...] = jnp.zeros_like(dbst_ref)
            dgo_ref[...] = jnp.zeros_like(dgo_ref)

        for k in range(nck):
            chunk(slice(k * CH, (k + 1) * CH), uv_ref, dya_ref, gv_ref, ws_ref, bst_ref, gout_ref, duv_ref,
                  dgv_ref, dws_ref, dbst_ref, dgo_ref)
        dxn_ref[...] = _dot_nt(duv_ref[...], wuv_ref[...])

    def chunk(rows, uv_ref, dya_ref, gv_ref, ws_ref, bst_ref, gout_ref, duv_ref, dgv_ref, dws_ref, dbst_ref,
              dgo_ref):
        gv = gv_ref[...]
        f = _gmlp_fwd_vals(uv_ref[rows, 0:1024], uv_ref[rows, 1024:2048], gv, ws_ref, bst_ref[...], gout_ref[...])
        dya = dya_ref[rows, :]
        dgo_ref[...] += jnp.sum(dya * f["yhat"], axis=0, keepdims=True)
        dy = _rms_bwd(dya, f["yhat"], f["ry"], gout_ref[...])
        lane = lax.broadcasted_iota(jnp.int32, (CH, 128), 1)
        dbs = jnp.zeros((CH, 128), F32)
        dug, dvg, dgvs = [], [], []
        for h in range(GM_HEADS):
            sl = slice(h * 128, (h + 1) * 128)
            vhat, rv, vn, wt, mixed = f["heads"][h]
            dyh = dy[:, sl]
            dug.append(dyh * mixed)
            dmixed = dyh * f["ug"][:, sl]
            dmb = dmixed.astype(MXU)
            dws_ref[h] += jnp.where(f["tril"], _dot_nt(dmb, vn), 0.0)
            dbs = dbs + jnp.where(lane == h, jnp.sum(dmixed, axis=1, keepdims=True), 0.0)
            dvn = _dot_tn(wt.astype(MXU), dmb)
            dgvs.append(jnp.sum(dvn * vhat, axis=0, keepdims=True))
            dvg.append(_rms_bwd(dvn, vhat, rv, gv[:, sl]))
        dbst_ref[...] += dbs
        dgv_ref[...] += jnp.concatenate(dgvs, axis=1)
        duv_ref[rows, 0:1024] = (jnp.concatenate(dug, axis=1) * f["dug"]).astype(MXU)
        duv_ref[rows, 1024:2048] = (jnp.concatenate(dvg, axis=1) * f["dvg"]).astype(MXU)

    return pl.pallas_call(
        body, grid=(T // tb,), name="gmlp_bwd",
        in_specs=[_rows(tb, 2048), _rows(tb, 1024, 0), _const((1, 1024)),
                  _const((GM_HEADS, CH, CH)), _const((CH, 128)), _const((1, 1024)), _const((D, 2048))],
        out_specs=[_rows(tb, 2048), _const((1, 1024)), _const((GM_HEADS, CH, CH)), _const((CH, 128)),
                   _const((1, 1024)), _rows(tb, D)],
        out_shape=[_sds((T, 2048), MXU), _sds((1, 1024), F32), _sds((GM_HEADS, CH, CH), F32), _sds((CH, 128), F32),
                   _sds((1, 1024), F32), _sds((T, D), F32)],
        compiler_params=_cp(),
    )(uv, dcat, gv, ws, bst, gout, wm)


def _ssd_bwd(pz, pxbc, conv, dtraw, sall, dcat, convw, dtb, alog, dskip, ng, ex, ltri, ext, nb, after):
    T = pz.shape[0]
    S = T // nb
    nch = S // CH
    ns = _SEQS_PER_STEP if nb % _SEQS_PER_STEP == 0 else 1

    def seq(width, col=0):
        return pl.BlockSpec((ns, CH, width), lambda b, c: (b, nch - 1 - c, col))

    in_specs = [
        seq(1024), seq(CONV_CH), seq(CONV_CH), seq(128),
        _const((8, CONV_CH)), _const((1, 128)), _const((1, 128)), _const((1, 1024)),
        _const((1, 1024)), _const((128, 1024)), _const((CH, CH)),
        _const((1024, 128)),
        pl.BlockSpec((ns, 1, 128, 1024), lambda b, c: (b, nch - 1 - c, 0, 0)),
        seq(1024, 1),
        _ANY,
    ]

    def body(z_ref, xbc_ref, conv_ref, dt_ref, cw_ref, dtb_ref, al_ref, ds_ref, ng_ref, ex_ref, lt_ref,
             ext_ref, sall_ref, dyb_ref,
             dssd_ref, ddt_ref, dcw_ref, dcb_ref, ddtb_ref, dal_ref, dds_ref, dng_ref,
             dst_ref, dnext_ref, ddse_ref):
        b = pl.program_id(0)
        c = pl.program_id(1)

        @pl.when((b == 0) & (c == 0))
        def _():
            for r in (dcw_ref, dcb_ref, ddtb_ref, dal_ref, dds_ref, dng_ref, ddse_ref):
                r[...] = jnp.zeros_like(r)

        @pl.when(c == 0)
        def _():
            dst_ref[...] = jnp.zeros_like(dst_ref)
            dnext_ref[...] = jnp.zeros_like(dnext_ref)

        ex = ex_ref[...]
        ext = ext_ref[...]
        cw = cw_ref[...]
        ng = ng_ref[...]
        for i in range(ns):
            one_chunk(i, ex, ext, cw, ng, z_ref, xbc_ref, conv_ref, dt_ref, dtb_ref, al_ref, ds_ref, lt_ref, sall_ref,
                      dyb_ref, dssd_ref, ddt_ref, dcw_ref, dcb_ref, ddtb_ref, dal_ref, dng_ref, dst_ref, dnext_ref,
                      ddse_ref)

        @pl.when((b == nb // ns - 1) & (c == nch - 1))
        def _():
            dds_ref[...] = _dot_01(jnp.broadcast_to(ddse_ref[...], (8, 1024)), ext)[0:1]

    def one_chunk(i, ex, ext, cw, ng, z_ref, xbc_ref, conv_ref, dt_ref, dtb_ref, al_ref, ds_ref, lt_ref, sall_ref,
                  dyb_ref, dssd_ref, ddt_ref, dcw_ref, dcb_ref, ddtb_ref, dal_ref, dng_ref, dst_ref, dnext_ref,
                  ddse_ref):
        z = z_ref[i]
        s_prev = sall_ref[i, 0]
        conv = conv_ref[i]
        f = _ssd_fwd_vals(z, conv, dt_ref[i], dtb_ref[...], al_ref[...], ds_ref[...], ng, ex, lt_ref[...], s_prev)
        xs, xdt, cs, dec, dt = f["xs"], f["xdt"], f["cs"], f["dec"], f["dt"]
        dyb = dyb_ref[i]
        dyg, dngs = [], []
        for g in range(2):
            sl = slice(g * 512, (g + 1) * 512)
            dngs.append(jnp.sum(dyb[:, sl] * f["yhat"][g], axis=0, keepdims=True))
            dyg.append(_rms_bwd(dyb[:, sl], f["yhat"][g], f["rr"][g], ng[:, sl]))
        dng_ref[...] += jnp.concatenate(dngs, axis=1)
        dyg = jnp.concatenate(dyg, axis=1)
        sig_z = f["sig_z"]
        silu_z = z * sig_z
        dy = dyg * silu_z
        dz = dyg * f["ypre"] * (sig_z + silu_z * (1.0 - sig_z))
        ddse_ref[...] += jnp.sum(dy * xs, axis=0, keepdims=True)
        dxs = dy * f["de"]
        dye = dy * f["ecse"]
        dyeb = dye.astype(MXU)
        dst = dst_ref[i]
        dstb = dst.astype(MXU)
        bmb, cmb, sb, xdec = f["bmb"], f["cmb"], f["sb"], f["xdec"]
        u = jnp.concatenate([_dot(bmb[g], dstb[:, g * 512:(g + 1) * 512]) for g in range(2)], axis=1)
        dxdt = [u[:, q * 128:(q + 1) * 128] * f["dece"][:, q * 128:(q + 1) * 128] for q in range(8)]
        per_head = _dot_01(jnp.concatenate(
            [dy * f["yo"], u * xdt, jnp.broadcast_to(jnp.sum(dst * s_prev, axis=0, keepdims=True), (8, 1024))],
            axis=0), ext)
        dcs = per_head[0:CH]
        t = per_head[CH:2 * CH] * dec
        dcd = per_head[2 * CH:2 * CH + 1]
        row = lax.broadcasted_iota(jnp.int32, (CH, 128), 0)
        lane = lax.broadcasted_iota(jnp.int32, (CH, 128), 1)
        cd = jnp.exp(f["last"])
        dcs = dcs - t + jnp.where(row == CH - 1, jnp.sum(t, axis=0, keepdims=True) + dcd * cd, 0.0)
        dcst = jnp.zeros((128, CH), F32)
        lo = f["lo"]
        dbm, dcm, ds_prev = [], [], []
        for g in range(2):
            sl = slice(g * 512, (g + 1) * 512)
            dmg = jnp.zeros((CH, CH), F32)
            for q in range(4 * g, 4 * g + 4):
                dyq = dy[:, q * 128:(q + 1) * 128]
                xq = xdt[:, q * 128:(q + 1) * 128].astype(MXU)
                for hh in range(2):
                    h = 2 * q + hh
                    m = lo if hh == 0 else ~lo
                    dym = jnp.where(m, dyq, 0.0).astype(MXU)
                    gh = _dot_nt(dym, xq)
                    gl = gh * f["lms"][h]
                    dmg = dmg + gl
                    qh = gl * f["mg"][g]
                    dcs = dcs + jnp.where(lane == h, jnp.sum(qh, axis=1, keepdims=True), 0.0)
                    dcst = dcst - jnp.where(row == h, jnp.sum(qh, axis=0, keepdims=True), 0.0)
                    dxdt[q] = dxdt[q] + _dot_tn(f["whs"][h], dym)
            dmgb = dmg.astype(MXU)
            dcm.append(_dot(dmgb, bmb[g]) + _dot_nt(dyeb[:, sl], sb[:, sl]))
            dbm.append(_dot_tn(dmgb, cmb[g]) + _dot_nt(xdec[:, sl], dstb[:, sl]))
            ds_prev.append(_dot_tn(cmb[g], dyeb[:, sl]))
        dst_ref[i] = jnp.concatenate(ds_prev, axis=1) + dst * f["cde"]
        dcs = dcs + dcst.T
        da = _dot_hi(lt_ref[...].T, dcs)
        dxdt = jnp.concatenate(dxdt, axis=1)
        a_neg = f["a_neg"]
        ddt = da * a_neg + _dot_01(dxdt * xs, ext)
        dal_ref[...] += jnp.sum(da * dt, axis=0, keepdims=True) * a_neg
        dxs = dxs + dxdt * f["dte"]
        ddtraw = jnp.where(lane < SSD_HEADS, ddt * _sigmoid(f["dtpre"]), 0.0)
        ddtb_ref[...] += jnp.sum(ddtraw, axis=0, keepdims=True)
        ddt_ref[i] = ddtraw.astype(MXU)
        dxa = jnp.concatenate([dxs, dbm[0], dbm[1], dcm[0], dcm[1]], axis=1)
        sig_c = f["sig_c"]
        dconv = dxa * (sig_c + f["xa"] * (1.0 - sig_c))
        dcb_ref[...] += jnp.sum(dconv, axis=0, keepdims=True)
        xbc = xbc_ref[i]
        dcw_ref[3:4, :] += jnp.sum(dconv * xbc, axis=0, keepdims=True)
        dxbc = cw[3:4] * dconv
        for j, up in zip((1, 2, 3), _shifts_up(dconv, dnext_ref[i])):
            dcw_ref[3 - j:4 - j, :] += jnp.sum(up * xbc, axis=0, keepdims=True)
            dxbc = dxbc + cw[3 - j:4 - j] * up
        dnext_ref[i] = dconv[0:8]
        dssd_ref[i, :, 0:1024] = dz.astype(MXU)
        dssd_ref[i, :, 1024:2560] = dxbc.astype(MXU)

    dssd, ddt, *small = pl.pallas_call(
        _after(14, body), grid=(nb // ns, nch), name="ssd_bwd",
        in_specs=in_specs,
        out_specs=[seq(2560), seq(128),
                   _const((8, CONV_CH)), _const((1, CONV_CH)), _const((1, 128)), _const((1, 128)), _const((1, 128)),
                   _const((1, 1024))],
        out_shape=[_sds((nb, S, 2560), MXU), _sds((nb, S, 128), MXU), _sds((8, CONV_CH), F32),
                   _sds((1, CONV_CH), F32), _sds((1, 128), F32), _sds((1, 128), F32), _sds((1, 128), F32),
                   _sds((1, 1024), F32)],
        scratch_shapes=[pltpu.VMEM((ns, 128, 1024), F32), pltpu.VMEM((ns, 8, CONV_CH), F32),
                        pltpu.VMEM((1, 1024), F32)],
        compiler_params=_cp(2),
    )(pz.reshape(nb, S, 1024), pxbc.reshape(nb, S, CONV_CH), conv.reshape(nb, S, CONV_CH), dtraw.reshape(nb, S, 128),
      convw, dtb, alog, dskip, ng, ex, ltri, ext, sall, dcat.reshape(nb, S, 2048), after)
    return (dssd.reshape(T, 2560), ddt.reshape(T, 128), *small)


def _inproj_bwd(dxn_uv, dssd, ddt, wm, wdt, dh1, x, g, tm, after):
    T = x.shape[0]

    def body(dxnuv_ref, dssd_ref, ddt_ref, wm_ref, wdt_ref, dh1_ref, x_ref, g_ref, dx_ref, dg_ref):
        @pl.when(pl.program_id(0) == 0)
        def _():
            dg_ref[...] = jnp.zeros_like(dg_ref)

        dxn = (dxnuv_ref[...] + _dot_nt(dssd_ref[...], wm_ref[:, 2048:N_MAIN])
               + _dot_nt(ddt_ref[...], wdt_ref[...]))
        xh, r = _rms(x_ref[...])
        dg_ref[...] += jnp.sum(dxn * xh, axis=0, keepdims=True)
        dx_ref[...] = dh1_ref[...] + _rms_bwd(dxn, xh, r, g_ref[...])

    return pl.pallas_call(
        _after(8, body), grid=(T // tm,), name="inproj_bwd",
        in_specs=[_rows(tm, D), _rows(tm, 2560), _rows(tm, 128), _const((D, N_MAIN)), _const((D, 128)),
                  _rows(tm, D), _rows(tm, D), _const((1, D)), _ANY],
        out_specs=[_rows(tm, D), _const((1, D))],
        out_shape=[_sds((T, D), F32), _sds((1, D), F32)],
        compiler_params=_cp(),
    )(dxn_uv, dssd, ddt, wm, wdt, dh1, x, g, after)


def _matmul_tn(a, b, name, a_fn=None):
    T, M = a.shape
    N = b.shape[1]
    tm = min(M, 1024)
    tn = 1280 if N == 2560 else min(N, 1024)
    tk = min(T, 2048)

    def body(a_ref, b_ref, o_ref, acc_ref):
        k = pl.program_id(2)

        @pl.when(k == 0)
        def _():
            acc_ref[...] = jnp.zeros_like(acc_ref)

        av = a_ref[...]
        if a_fn is not None:
            av = a_fn(av)
        acc_ref[...] += _dot_tn(av, b_ref[...])

        @pl.when(k == T // tk - 1)
        def _():
            o_ref[...] = acc_ref[...].astype(o_ref.dtype)

    return pl.pallas_call(
        body, grid=(M // tm, N // tn, T // tk), name=name,
        in_specs=[pl.BlockSpec((tk, tm), lambda i, j, k: (k, i)), pl.BlockSpec((tk, tn), lambda i, j, k: (k, j))],
        out_specs=pl.BlockSpec((tm, tn), lambda i, j, k: (i, j)),
        out_shape=_sds((M, N), GRAD),
        scratch_shapes=[pltpu.VMEM((tm, tn), F32)],
        compiler_params=_cp(3),
    )(a, b)


def _adamw_vals(w, g, m, v):
    m = B1 * m + (1.0 - B1) * g
    v = B2 * v + (1.0 - B2) * (g * g)
    m_hat = m / (1.0 - B1 ** STEP)
    v_hat = v / (1.0 - B2 ** STEP)
    return -LR * (m_hat / (jnp.sqrt(v_hat) + ADAM_EPS) + WD * w), m, v


def _adamw(w, g, m, v, name):
    R, C = w.shape
    tr = 256 if R % 256 == 0 else R

    def body(w_ref, g_ref, m_ref, v_ref, d_ref, mo_ref, vo_ref):
        d_ref[...], mo_ref[...], vo_ref[...] = _adamw_vals(w_ref[...], g_ref[...], m_ref[...], v_ref[...])

    spec = _rows(tr, C)
    return pl.pallas_call(
        body, grid=(R // tr,), name=name,
        in_specs=[spec] * 4, out_specs=[spec] * 3, out_shape=[_sds((R, C), F32)] * 3,
        compiler_params=_cp(),
    )(w, g, m, v)


def _adamw_halves(w, own, other, m, v, name):
    R, C = w.shape
    half = R // 2
    tr = min(half, 256)
    nth = half // tr

    def body(w_ref, own_ref, oth_ref, m_ref, v_ref, g_ref, d_ref, mo_ref, vo_ref):
        mine = (pl.program_id(0) // nth) == lax.axis_index("c")
        g = jnp.where(mine, own_ref[...], oth_ref[...])
        g_ref[...] = g
        d_ref[...], mo_ref[...], vo_ref[...] = _adamw_vals(w_ref[...], g, m_ref[...], v_ref[...])

    full = _rows(tr, C)
    part = pl.BlockSpec((tr, C), lambda i: (i % nth, 0))
    return pl.pallas_call(
        body, grid=(R // tr,), name=name,
        in_specs=[full, part, part, full, full], out_specs=[full] * 4, out_shape=[_sds((R, C), F32)] * 4,
        compiler_params=_cp(),
    )(w, own, other, m, v)


def _sum_small(slots, name):
    nd, rows, C = slots.shape

    def body(s_ref, o_ref):
        acc = s_ref[0]
        for d in range(1, nd):
            acc = acc + s_ref[d]
        o_ref[...] = acc

    return pl.pallas_call(
        body, grid=(1,), name=name,
        in_specs=[_const((nd, rows, C))], out_specs=_const((rows, C)), out_shape=_sds((rows, C), F32),
        compiler_params=_cp(),
    )(slots)


def _sum_slots(slots, src, kind, shp, kh, name):
    R, C = shp
    rh = R // 2
    tr = min(rh, 256)
    nth = rh // tr
    if kind == "slab":
        src_spec = pl.BlockSpec((1, tr, C), lambda i, kh: (kh[0], kh[1] * nth + i, 0))
    elif kind == "rows":
        src_spec = pl.BlockSpec((tr, C), lambda i, kh: (kh[0] * (R // tr) + kh[1] * nth + i, 0))
    else:
        src_spec = pl.BlockSpec((tr, C), lambda i, kh: (kh[1] * nth + i, kh[0]))

    def body(kh_ref, s_ref, own_ref, o_ref):
        me = 2 * kh_ref[0] + kh_ref[1]
        acc = (own_ref[0] if kind == "slab" else own_ref[...]).astype(F32)
        for k in range(1, 8):
            acc = acc + s_ref[me ^ k].astype(F32)
        o_ref[...] = acc

    return pl.pallas_call(
        body, name=name,
        grid_spec=pltpu.PrefetchScalarGridSpec(
            num_scalar_prefetch=1, grid=(nth,),
            in_specs=[pl.BlockSpec((8, tr, C), lambda i, kh: (0, i, 0)), src_spec],
            out_specs=pl.BlockSpec((tr, C), lambda i, kh: (i, 0))),
        out_shape=_sds((rh, C), F32),
        compiler_params=_cp(),
    )(kh, slots, src)


def _assemble_w_in(slabs):
    tr = 256

    def body(s_ref, wm_ref, wdt_ref):
        full = jnp.concatenate([s_ref[k] for k in range(4)], axis=1)
        wm_ref[...] = full[:, :N_MAIN]
        wdt_ref[...] = jnp.concatenate([full[:, N_MAIN:], jnp.zeros((tr, 128 - 16), full.dtype)], axis=1)

    return pl.pallas_call(
        body, grid=(D // tr,), name="assemble_w_in",
        in_specs=[pl.BlockSpec((4, tr, 1156), lambda i: (0, i, 0))],
        out_specs=[_rows(tr, N_MAIN), _rows(tr, 128)],
        out_shape=[_sds((D, N_MAIN), slabs.dtype), _sds((D, 128), slabs.dtype)],
        compiler_params=_cp(),
    )(slabs)


def _split_dw_in(d_uv, d_ssd, d_dt):
    tr = 256

    def body(uv_ref, ssd_ref, dt_ref, o_ref):
        full = jnp.concatenate([uv_ref[...], ssd_ref[...], dt_ref[:, 0:16]], axis=1)
        for k in range(4):
            o_ref[k] = full[:, 1156 * k:1156 * (k + 1)]

    return pl.pallas_call(
        body, grid=(D // tr,), name="split_dw_in",
        in_specs=[_rows(tr, 2048), _rows(tr, 2560), _rows(tr, 128)],
        out_specs=pl.BlockSpec((4, tr, 1156), lambda i: (0, i, 0)),
        out_shape=_sds((4, D, 1156), d_uv.dtype),
        compiler_params=_cp(),
    )(d_uv, d_ssd, d_dt)


def _cast_into_slot(w, kh, name):
    R, C = w.shape
    tr = 256

    def body(kh_ref, w_ref, o_ref):
        o_ref[0] = w_ref[...].astype(BF16)

    return pl.pallas_call(
        body, name=name,
        grid_spec=pltpu.PrefetchScalarGridSpec(
            num_scalar_prefetch=1, grid=(R // tr,),
            in_specs=[pl.BlockSpec((tr, C), lambda i, kh: (i, 0))],
            out_specs=pl.BlockSpec((1, tr, C), lambda i, kh: (kh[0], i, 0))),
        out_shape=_sds((4, R, C), BF16),
        compiler_params=_cp(),
    )(kh, w)


_ANY = pl.BlockSpec(memory_space=pl.ANY)
_CHIP_FLIPS = [(1, 0), (0, 1), (1, 1)]
_DEVICE_FLIPS = [(fx, fy, fc) for fx in (0, 1) for fy in (0, 1) for fc in (0, 1)][1:]


def _half(h, rows):
    return pl.ds(pl.multiple_of(h * rows, rows), rows)


def _remote(src, dst, ssem, rsem, to):
    return pltpu.make_async_remote_copy(src_ref=src, dst_ref=dst, send_sem=ssem, recv_sem=rsem,
                                        device_id=to, device_id_type=MESH)


def _weight_gather(bufs, conv):
    n = len(bufs)

    def body(*refs):
        conv_ref, outs, conv_out = refs[n], refs[n + 1:2 * n + 1], refs[2 * n + 1]
        send_sems, recv_sems, fsend_sems, frecv_sems, csend_sems, crecv_sems, local_sem = refs[2 * n + 2:]
        x, y, c = lax.axis_index("x"), lax.axis_index("y"), lax.axis_index("c")
        me = 2 * x + y
        halves = [_half(c, r.shape[1] // 2) for r in outs]
        others = [_half(1 - c, r.shape[1] // 2) for r in outs]
        remote = _remote
        local = [pltpu.make_async_copy(conv_ref, conv_out.at[me], local_sem)]
        for cp in local:
            cp.start()
        sends = []
        for k, (fx, fy) in enumerate(_CHIP_FLIPS):
            peer = (x ^ fx, y ^ fy, c)
            for i in range(n):
                mine = outs[i].at[me, halves[i]]
                sends.append(remote(mine, mine, send_sems.at[k * n + i], recv_sems.at[k * n + i], peer))
            sends.append(remote(conv_ref, conv_out.at[me], csend_sems.at[k], crecv_sems.at[k], peer))
        for cp in sends:
            cp.start()
        sibling = (x, y, 1 - c)
        forwards = []
        for k, (fx, fy) in enumerate(_CHIP_FLIPS):
            peer = (x ^ fx, y ^ fy, c)
            src = 2 * (x ^ fx) + (y ^ fy)
            for i in range(n):
                landed = outs[i].at[src, halves[i]]
                remote(landed, landed, send_sems.at[k * n + i], recv_sems.at[k * n + i], peer).wait_recv()
                fw = remote(landed, landed, fsend_sems.at[k * n + i], frecv_sems.at[k * n + i], sibling)
                fw.start()
                forwards.append(fw)
            remote(conv_out.at[src], conv_out.at[src], csend_sems.at[k], crecv_sems.at[k], peer).wait_recv()
        for k, (fx, fy) in enumerate(_CHIP_FLIPS):
            src = 2 * (x ^ fx) + (y ^ fy)
            for i in range(n):
                theirs = outs[i].at[src, others[i]]
                remote(theirs, theirs, fsend_sems.at[k * n + i], frecv_sems.at[k * n + i], sibling).wait_recv()
        for cp in sends + forwards:
            cp.wait_send()
        for cp in local:
            cp.wait()

    dma = pltpu.SemaphoreType.DMA
    return pl.pallas_call(
        body, name="weight_gather",
        in_specs=[_ANY] * (n + 1), out_specs=[_ANY] * (n + 1),
        out_shape=[_sds(b.shape, b.dtype) for b in bufs] + [_sds((4,) + conv.shape, conv.dtype)],
        input_output_aliases={i: i for i in range(n)},
        scratch_shapes=[dma((3 * n,)), dma((3 * n,)), dma((3 * n,)), dma((3 * n,)), dma((3,)), dma((3,)), dma],
    )(*bufs, conv)


def _piece(ref, kind, R, C, k, h):
    if kind == "slab":
        return ref.at[k, _half(h, R // 2), :]
    if kind == "rows":
        return ref.at[pl.ds(pl.multiple_of(k * R + h * (R // 2), R // 2), R // 2), :]
    return ref.at[_half(h, R // 2), pl.ds(pl.multiple_of(k * C, C), C)]


def _small_exchange(small, after):
    rs = small.shape[0]

    def body(s_ref, after_ref, out_ref, send_sems, recv_sems, local_sem):
        del after_ref
        x, y, c = lax.axis_index("x"), lax.axis_index("y"), lax.axis_index("c")
        slot = 4 * x + 2 * y + c
        own = pltpu.make_async_copy(s_ref, out_ref.at[slot], local_sem)
        own.start()
        copies = []
        for k, (fx, fy, fc) in enumerate(_DEVICE_FLIPS):
            copies.append(_remote(s_ref, out_ref.at[slot], send_sems.at[k], recv_sems.at[k], (x ^ fx, y ^ fy, c ^ fc)))
        for cp in copies:
            cp.start()
        for k, (fx, fy, fc) in enumerate(_DEVICE_FLIPS):
            theirs = out_ref.at[slot ^ (k + 1)]
            _remote(theirs, theirs, send_sems.at[k], recv_sems.at[k], (x ^ fx, y ^ fy, c ^ fc)).wait_recv()
        for cp in copies:
            cp.wait_send()
        own.wait()

    dma = pltpu.SemaphoreType.DMA
    return pl.pallas_call(
        body, name="small_exchange",
        in_specs=[_ANY, _ANY], out_specs=_ANY, out_shape=_sds((8, rs, 128), F32),
        scratch_shapes=[dma((7,)), dma((7,)), dma],
    )(small, after)


_HBM = pl.BlockSpec(memory_space=pltpu.HBM)
_SEM = pl.BlockSpec(memory_space=pltpu.SEMAPHORE)


def _split_start(name, arrays, n_copies, plan, after=None):
    n = len(arrays)
    extra = [] if after is None else [after]

    def body(*refs):
        m = n + len(extra)
        arrs, send_sems, recv_sems, token = refs[:n], refs[m], refs[m + 1], refs[-1]
        for j, (src, dst, peer) in enumerate(plan(arrs)):
            _remote(src, dst, send_sems.at[j], recv_sems.at[j], peer).start()
        token[...] = jnp.zeros_like(token)

    dma = pltpu.SemaphoreType.DMA
    res = pl.pallas_call(
        body, name=name,
        out_shape=(dma((n_copies,)), dma((n_copies,)), *[pltpu.HBM(a.shape, a.dtype) for a in arrays],
                   _sds((8, 128), F32)),
        in_specs=[_HBM] * n + [_ANY] * len(extra),
        out_specs=(_SEM, _SEM, *[_HBM] * n, pl.BlockSpec(memory_space=pltpu.VMEM)),
        input_output_aliases={i: 2 + i for i in range(n)},
        compiler_params=pltpu.CompilerParams(has_side_effects=pltpu.SideEffectType.DATAFLOW_SIDE_EFFECTING),
    )(*[pltpu.with_memory_space_constraint(a, pltpu.HBM) for a in arrays], *extra)
    return res[0], res[1], list(res[2:2 + n]), res[-1]


def _split_wait(name, arrays, send_sems, recv_sems, plan, after):
    n = len(arrays)

    def body(*refs):
        arrs, ssems, rsems = refs[:n], refs[n], refs[n + 1]
        for j, (src, dst, peer) in enumerate(plan(arrs)):
            cp = _remote(src, dst, ssems.at[j], rsems.at[j], peer)
            cp.wait_send()
            cp.wait_recv()

    return list(pl.pallas_call(
        body, name=name,
        out_shape=tuple(pltpu.HBM(a.shape, a.dtype) for a in arrays),
        in_specs=[_HBM] * n + [_SEM, _SEM, _ANY],
        out_specs=tuple([_HBM] * n),
        input_output_aliases={i: i for i in range(n)},
        compiler_params=pltpu.CompilerParams(has_side_effects=pltpu.SideEffectType.DATAFLOW_SIDE_EFFECTING),
    )(*arrays, send_sems, recv_sems, after))


def _gather_plan(n):
    def plan(bufs):
        x, y, c = lax.axis_index("x"), lax.axis_index("y"), lax.axis_index("c")
        me = 2 * x + y
        return [(bufs[i].at[me], bufs[i].at[me], (x ^ fx, y ^ fy, c)) for fx, fy in _CHIP_FLIPS for i in range(n)]

    return plan


def _reduce_plan(specs, n_small):
    n = len(specs)

    def plan(arrs):
        x, y, c = lax.axis_index("x"), lax.axis_index("y"), lax.axis_index("c")
        slot = 4 * x + 2 * y + c
        out = []
        for fx, fy, fc in _DEVICE_FLIPS:
            peer = (x ^ fx, y ^ fy, c ^ fc)
            for i, (kind, (R, C)) in enumerate(specs):
                out.append((_piece(arrs[i], kind, R, C, 2 * peer[0] + peer[1], peer[2]), arrs[n + i].at[slot], peer))
            for s in range(n_small):
                out.append((arrs[2 * n + 2 * s], arrs[2 * n + 2 * s + 1].at[slot], peer))
        return out

    return plan


def _sibling_exchange(halves, name):
    n = len(halves)

    def body(*refs):
        ins, outs, send_sems, recv_sems = refs[:n], refs[n:2 * n], refs[2 * n], refs[2 * n + 1]
        sibling = (lax.axis_index("x"), lax.axis_index("y"), 1 - lax.axis_index("c"))
        copies = [pltpu.make_async_remote_copy(src_ref=ins[i], dst_ref=outs[i], send_sem=send_sems.at[i],
                                               recv_sem=recv_sems.at[i], device_id=sibling, device_id_type=MESH)
                  for i in range(n)]
        for cp in copies:
            cp.start()
        for cp in copies:
            cp.wait()

    dma = pltpu.SemaphoreType.DMA
    return pl.pallas_call(
        body, name=name,
        in_specs=[_ANY] * n, out_specs=[_ANY] * n,
        out_shape=[_sds(h.shape, h.dtype) for h in halves],
        scratch_shapes=[dma((n,)), dma((n,))],
    )(*halves)


_BIG = [("w_in", (1024, 1156), "slab"), ("w_out", (512, 1024), "rows"), ("w_ff1", (1024, 1024), "cols"),
        ("w_ff2", (1024, 1024), "rows"), ("w_ple_gate", (256, 1024), "rows"), ("w_ple_proj", (256, 256), "cols")]
_SMALL = [("norm_mix_g", (1, 1024)), ("gm_v_norm_g", (1, 1024)), ("gm_ws", (1, 8, 128, 128)), ("gm_bs", (1, 8, 128)),
          ("gm_out_norm_g", (1, 1024)), ("ssd_conv_w", (1, 4, 1536)), ("ssd_conv_b", (1, 1536)),
          ("ssd_dt_bias", (1, 16)), ("ssd_a_log", (1, 16)), ("ssd_d", (1, 16)), ("ssd_norm_g", (1, 1024)),
          ("norm_mlp_g", (1, 1024)), ("ple_norm_g", (1, 1024)), ("final_norm_g", (1024,))]


def _rows128(a):
    flat = a.reshape(-1)
    rows = -(-flat.shape[0] // 1024) * 8
    return jnp.pad(flat, (0, rows * 128 - flat.shape[0])).reshape(rows, 128)


def _pad_lanes(v, n=128):
    v = v.reshape(1, -1)
    return jnp.pad(v, ((0, 0), (0, n - v.shape[1])))


_SMALL_SHAPES = dict(_SMALL + [("loss", ())])
_BIG_SPECS = {n: (kind, shp) for n, shp, kind in _BIG}


class _Comm:
    def __init__(self, a, kh):
        self.a, self.kh = a, kh
        self.bufs = {n: _cast_into_slot(a[n].reshape(shp), kh, "cast_" + n) for n, shp, _ in _BIG}
        self.sent = []
        self.small_tot = {}

    def w_in(self):
        (g_win,), g_cw = self._gather_now()
        rest = [self.bufs[n] for n, _, _ in _BIG[1:]]
        plan = _gather_plan(len(rest))
        ssem, rsem, thru, token = _split_start("gather_start", rest, 3 * len(rest), plan, after=g_cw)
        self.gather = (plan, ssem, rsem, thru)
        wm, wdt = _assemble_w_in(g_win)
        return wm, wdt, jnp.concatenate([g_cw[k] for k in range(4)], axis=1), token

    def _gather_now(self):
        *bufs, g_cw = _weight_gather([self.bufs["w_in"]], self.a["ssd_conv_w"].reshape(4, 384))
        return bufs, g_cw

    def rest(self, after):
        plan, ssem, rsem, thru = self.gather
        g_wo, g_w1, g_w2, g_wg, g_wp = _split_wait("gather_wait", thru, ssem, rsem, plan, after)
        return g_wo.reshape(2048, D), g_w1, g_w2.reshape(DFF, D), g_wg.reshape(D, D), g_wp

    def send(self, tag, grads):
        big = [n for n, _, _ in _BIG if n in grads]
        small = [n for n in _SMALL_SHAPES if n in grads]
        parts = [_rows128(grads[n]) for n in small]
        rows = [s.shape[0] for s in parts]
        if not big:
            self.last_small = (tag, small, rows, jnp.concatenate(parts, axis=0))
            return None
        srcs = [grads[n] for n in big]
        lands = [lax.empty((8, _BIG_SPECS[n][1][0] // 2, _BIG_SPECS[n][1][1]), GRAD) for n in big]
        extra = []
        if small:
            pack = jnp.concatenate(parts, axis=0)
            extra = [pack, jnp.broadcast_to(pack, (8,) + pack.shape)]
        plan = _reduce_plan([_BIG_SPECS[n] for n in big], len(extra) // 2)
        n_copies = 7 * (len(big) + len(extra) // 2)
        ssem, rsem, thru, token = _split_start("reduce_start_" + tag, srcs + lands + extra, n_copies, plan)
        self.sent.append((tag, big, small, rows, plan, ssem, rsem, thru))
        return token

    def _unpack(self, tot, names, rows):
        o = 0
        for n, r in zip(names, rows):
            shp = _SMALL_SHAPES[n]
            cnt = 1
            for s in shp:
                cnt *= s
            self.small_tot[n] = tot[o:o + r].reshape(-1)[:cnt].reshape(shp)
            o += r

    def finish(self, after):
        a, results = self.a, {}

        def update(names, own, tag):
            other = _sibling_exchange([own[n] for n in names], "sibling_exchange_" + tag)
            for n, oth in zip(names, other):
                shp = _BIG_SPECS[n][1]
                results[n] = _adamw_halves(a[n].reshape(shp), own[n], oth, a["m_" + n].reshape(shp),
                                           a["v_" + n].reshape(shp), "adamw_" + n)
            return results[names[-1]][1]

        own, early = {}, []
        for tag, big, small, rows, plan, ssem, rsem, thru in self.sent:
            if tag == self.sent[-1][0]:
                after = update(early, own, "early")
            arrs = _split_wait("reduce_wait_" + tag, thru, ssem, rsem, plan, after)
            nb_ = len(big)
            for i, n in enumerate(big):
                kind, shp = _BIG_SPECS[n]
                own[n] = _sum_slots(arrs[nb_ + i], arrs[i], kind, shp, self.kh, "sum_" + n)
                after = own[n]
            early += big
            if small:
                self._unpack(_sum_small(arrs[2 * nb_ + 1], "sum_small_" + tag), small, rows)
        after = update(self.sent[-1][1], own, "late")
        tag, small, rows, pack = self.last_small
        self._unpack(_sum_small(_small_exchange(pack, after), "sum_small_" + tag), small, rows)
        return results, dict(self.small_tot)


def _local_step(x, p, tgt, sm, comm, nb, tm):
    wm, wdt, conv_w, token = comm.w_in()
    g_mix, gv, gout = sm["norm_mix_g"].reshape(1, D), sm["gm_v_norm_g"].reshape(1, D), sm["gm_out_norm_g"].reshape(1, D)
    ws = sm["gm_ws"].reshape(GM_HEADS, CH, CH)
    bst = jnp.pad(sm["gm_bs"].reshape(GM_HEADS, CH).T, ((0, 0), (0, 128 - GM_HEADS)))
    convw = jnp.pad(conv_w, ((0, 4), (0, 0)))
    convb = sm["ssd_conv_b"].reshape(1, CONV_CH)
    dtb, alog = _pad_lanes(sm["ssd_dt_bias"]), _pad_lanes(sm["ssd_a_log"])
    dskip = jnp.repeat(sm["ssd_d"].reshape(SSD_HEADS), SSD_P).reshape(1, 1024)
    ng, g_mlp, g_ple = sm["ssd_norm_g"].reshape(1, D), sm["norm_mlp_g"].reshape(1, D), sm["ple_norm_g"].reshape(1, D)
    gf = sm["final_norm_g"].reshape(1, D)
    head_of_lane = lax.broadcasted_iota(jnp.int32, (128, 1024), 1) // SSD_P
    ex = (lax.broadcasted_iota(jnp.int32, (128, 1024), 0) == head_of_lane).astype(BF16)
    ext = ex.T
    ltri = (lax.broadcasted_iota(jnp.int32, (CH, CH), 0) >= lax.broadcasted_iota(jnp.int32, (CH, CH), 1)).astype(F32)

    pz, pxbc, dtraw, xn, cat, uv = _inproj_gmlp(x, g_mix, wm, wdt, gv, ws, bst, gout, tm // 2, token)
    cat, sall, conv = _ssd_fwd(pz, pxbc, dtraw, cat, convw, convb, dtb, alog, dskip, ng, ex, ltri, nb)
    wo, w1, w2, wg, wp = comm.rest(cat)
    h1, hn = _outproj(cat, wo, x, g_mlp, tm)
    hid = _ff1(hn, w1, tm)
    hp, dgl, dpe, dh2, dh2b, loss, d_gf, d_gple = _ff2_tail(hid, w2, h1, g_ple, p, tgt, wg, wp, gf, tm // 2)

    d_wp = _matmul_tn(p, dpe, "dw_ple_proj", a_fn=lambda a: a.astype(MXU))
    d_wg = _matmul_tn(hp, dgl, "dw_ple_gate")
    d_w2 = _matmul_tn(hid, dh2b, "dw_ff2", a_fn=_sq)
    dpre = _ff2_bwd(dh2b, w2, hid, tm)
    d_w1 = _matmul_tn(hn, dpre, "dw_ff1")
    token = comm.send("a", {"w_ple_proj": d_wp, "w_ple_gate": d_wg, "w_ff2": d_w2, "w_ff1": d_w1})
    dh1, dh1b, d_gmlp = _ff1_bwd(dpre, w1, dh2, h1, g_mlp, tm, token)
    dcat = _outproj_bwd(dh1b, wo, tm)
    d_wo = _matmul_tn(cat, dh1b, "dw_out")
    duv, d_gv, d_ws, d_bst, d_gout, dxn_uv = _gmlp_bwd(uv, dcat, gv, ws, bst, gout, wm)
    token = comm.send("b", {
        "w_out": d_wo, "loss": loss[0:1, 0:1], "final_norm_g": d_gf, "ple_norm_g": d_gple, "norm_mlp_g": d_gmlp,
        "gm_v_norm_g": d_gv, "gm_ws": d_ws, "gm_bs": d_bst[:, :GM_HEADS].T, "gm_out_norm_g": d_gout})
    dssd, ddt, d_cw, d_cb, d_dtb, d_al, d_ds, d_ng = _ssd_bwd(
        pz, pxbc, conv, dtraw, sall, dcat, convw, dtb, alog, dskip, ng, ex, ltri, ext, nb, token)
    d_win = _split_dw_in(_matmul_tn(xn, duv, "dw_in_uv"), _matmul_tn(xn, dssd, "dw_in_ssd"),
                         _matmul_tn(xn, ddt, "dw_in_dt"))
    token = comm.send("c", {"w_in": d_win})
    dx, d_gmix = _inproj_bwd(dxn_uv, dssd, ddt, wm, wdt, dh1, x, g_mix, tm, token)
    comm.send("d", {"norm_mix_g": d_gmix, "ssd_conv_w": d_cw[0:4], "ssd_conv_b": d_cb, "ssd_dt_bias": d_dtb[:, :16],
                    "ssd_a_log": d_al[:, :16], "ssd_d": d_ds[:, :16], "ssd_norm_g": d_ng})
    return dx


def kernel(x, p, norm_mix_g, w_in, gm_v_norm_g, gm_ws, gm_bs, gm_out_norm_g, ssd_conv_w, ssd_conv_b, ssd_dt_bias, ssd_a_log, ssd_d, ssd_norm_g, w_out, norm_mlp_g, w_ff1, w_ff2, ple_norm_g, w_ple_gate, w_ple_proj, final_norm_g, loss_target, m_norm_mix_g, m_w_in, m_gm_v_norm_g, m_gm_ws, m_gm_bs, m_gm_out_norm_g, m_ssd_conv_w, m_ssd_conv_b, m_ssd_dt_bias, m_ssd_a_log, m_ssd_d, m_ssd_norm_g, m_w_out, m_norm_mlp_g, m_w_ff1, m_w_ff2, m_ple_norm_g, m_w_ple_gate, m_w_ple_proj, m_final_norm_g, v_norm_mix_g, v_w_in, v_gm_v_norm_g, v_gm_ws, v_gm_bs, v_gm_out_norm_g, v_ssd_conv_w, v_ssd_conv_b, v_ssd_dt_bias, v_ssd_a_log, v_ssd_d, v_ssd_norm_g, v_w_out, v_norm_mlp_g, v_w_ff1, v_w_ff2, v_ple_norm_g, v_w_ple_gate, v_w_ple_proj, v_final_norm_g):
    a = dict(locals())
    order = ["norm_mix_g", "w_in", "gm_v_norm_g", "gm_ws", "gm_bs", "gm_out_norm_g", "ssd_conv_w", "ssd_conv_b",
             "ssd_dt_bias", "ssd_a_log", "ssd_d", "ssd_norm_g", "w_out", "norm_mlp_g", "w_ff1", "w_ff2", "ple_norm_g",
             "w_ple_gate", "w_ple_proj", "final_norm_g"]
    chip = 2 * lax.axis_index("x") + lax.axis_index("y")
    nb, S = x.shape[0], x.shape[1]
    T = nb * S
    sm = {n: a[n] for n, _ in _SMALL if n != "ssd_conv_w"}
    comm = _Comm(a, jnp.stack([chip, lax.axis_index("c")]).astype(jnp.int32))
    dx = _local_step(x.reshape(T, D), p.reshape(T, DPLE), loss_target.reshape(T, D), sm, comm, nb, 512)
    big, g_out = comm.finish(dx)
    delta, new_m, new_v = {}, {}, {}
    for n, _, _ in _BIG:
        g_out[n], delta[n], new_m[n], new_v[n] = (r.reshape(a[n].shape) for r in big[n])
    g_out["ssd_conv_w"] = lax.dynamic_slice(g_out["ssd_conv_w"], (0, 0, chip * 384), (1, 4, 384))
    small_names = [n for n, _ in _SMALL]
    packs = [jnp.concatenate([_rows128(src(n)) for n in small_names], axis=0)
             for src in (lambda n: a[n], lambda n: g_out[n], lambda n: a["m_" + n], lambda n: a["v_" + n])]
    outs = _adamw(*packs, "adamw_small")
    o = 0
    for n in small_names:
        r = _rows128(a[n]).shape[0]
        cnt = a[n].size
        for dst, src in zip((delta, new_m, new_v), outs):
            dst[n] = src[o:o + r].reshape(-1)[:cnt].reshape(a[n].shape)
        o += r
    return (g_out["loss"], dx.reshape(x.shape), *[g_out[n] for n in order], *[delta[n] for n in order],
            *[new_m[n] for n in order], *[new_v[n] for n in order])
```

```python
import jax
import jax.numpy as jnp
from jax import lax
from jax.experimental import pallas as pl
from jax.experimental.pallas import tpu as pltpu

F32 = jnp.float32
BF16 = jnp.bfloat16
MXU = jnp.bfloat16
GRAD = jnp.bfloat16

D = 1024
CH = 128
GM_HEADS = 8
SSD_HEADS = 16
SSD_P = 64
CONV_CH = 1536
N_MAIN = 4608
DFF = 4096
DPLE = 256
EPS = 1e-6
NEG = -1e30

LR, B1, B2, ADAM_EPS, WD, STEP = 0.001, 0.9, 0.999, 1e-08, 0.01, 10

VMEM_LIMIT = 56 * 1024 * 1024
_SEQS_PER_STEP = 2
MESH = pl.DeviceIdType.MESH

INV_SQRT2 = 0.7071067811865476
INV_SQRT_2PI = 0.3989422804014327


def _cp(n_axes=1):
    return pltpu.CompilerParams(dimension_semantics=("arbitrary",) * n_axes, vmem_limit_bytes=VMEM_LIMIT)


def _dot(a, b):
    return jnp.dot(a, b, preferred_element_type=F32)


def _dot_nt(a, b):
    return lax.dot_general(a, b, (((1,), (1,)), ((), ())), preferred_element_type=F32)


def _dot_tn(a, b):
    return lax.dot_general(a, b, (((0,), (0,)), ((), ())), preferred_element_type=F32)


def _dot_hi(a, b):
    return jnp.dot(a, b, preferred_element_type=F32, precision=lax.Precision.HIGHEST)


def _dot_01(a, sel):
    hi = a.astype(BF16)
    lo = (a - hi.astype(F32)).astype(BF16)
    n = a.shape[0]
    r = _dot(jnp.concatenate([hi, lo], axis=0), sel)
    return r[0:n] + r[n:2 * n]


def _rows(tm, n, j=0):
    return pl.BlockSpec((tm, n), lambda i: (i, j))


def _const(shape):
    nd = len(shape)
    return pl.BlockSpec(shape, lambda *_: (0,) * nd)


def _sds(shape, dtype):
    return jax.ShapeDtypeStruct(shape, dtype)


def _rms(x):
    r = lax.rsqrt(jnp.mean(x * x, axis=-1, keepdims=True) + EPS)
    return x * r, r


def _rms_bwd(dy, xhat, r, g):
    dyg = dy * g
    return r * (dyg - xhat * jnp.mean(dyg * xhat, axis=-1, keepdims=True))


def _sigmoid(x):
    return 1.0 / (1.0 + jnp.exp(-x))


def _gelu(x):
    cdf = 0.5 * (1.0 + lax.erf(x * INV_SQRT2))
    pdf = jnp.exp(-0.5 * x * x) * INV_SQRT_2PI
    return x * cdf, cdf + x * pdf


def _softplus(x):
    e = jnp.exp(-jnp.abs(x))
    u = 1.0 + e
    log1p = jnp.where(u == 1.0, e, jnp.log(u) * e / (u - 1.0))
    return jnp.maximum(x, 0.0) + log1p


def _after(n_in, fn):
    def body(*refs):
        return fn(*refs[:n_in], *refs[n_in + 1:])

    return body


def _inproj_gmlp(x, g, wm, wdt, gv, ws, bst, gout, tm, after):
    T = x.shape[0]

    def body(x_ref, g_ref, wm_ref, wdt_ref, gv_ref, ws_ref, bst_ref, gout_ref,
             z_ref, xbc_ref, dt_ref, xn_ref, ya_ref, uv_ref):
        xh, _ = _rms(x_ref[...])
        xn = (xh * g_ref[...]).astype(MXU)
        xn_ref[...] = xn
        for n in range(4):
            uv_ref[:, n * 512:(n + 1) * 512] = _dot(xn, wm_ref[:, n * 512:(n + 1) * 512])
        for n in range(2):
            z_ref[:, n * 512:(n + 1) * 512] = _dot(xn, wm_ref[:, 2048 + n * 512:2048 + (n + 1) * 512])
        for n in range(3):
            xbc_ref[:, n * 512:(n + 1) * 512] = _dot(xn, wm_ref[:, 3072 + n * 512:3072 + (n + 1) * 512])
        dt_ref[...] = _dot(xn, wdt_ref[...])
        for k in range(tm // CH):
            rows = slice(k * CH, (k + 1) * CH)
            f = _gmlp_fwd_vals(uv_ref[rows, 0:1024], uv_ref[rows, 1024:2048], gv_ref[...], ws_ref, bst_ref[...],
                               gout_ref[...])
            ya_ref[rows, :] = f["out"].astype(MXU)

    return pl.pallas_call(
        _after(8, body), grid=(T // tm,), name="inproj_gmlp",
        in_specs=[_rows(tm, D), _const((1, D)), _const((D, N_MAIN)), _const((D, 128)), _const((1, 1024)),
                  _const((GM_HEADS, CH, CH)), _const((CH, 128)), _const((1, 1024)), _ANY],
        out_specs=[_rows(tm, 1024), _rows(tm, CONV_CH), _rows(tm, 128), _rows(tm, D), _rows(tm, 1024, 0),
                   _rows(tm, 2048)],
        out_shape=[_sds((T, 1024), F32), _sds((T, CONV_CH), F32), _sds((T, 128), F32), _sds((T, D), MXU),
                   _sds((T, 2048), MXU), _sds((T, 2048), F32)],
        compiler_params=_cp(),
    )(x, g, wm, wdt, gv, ws, bst, gout, after)


def _gmlp_fwd_vals(u, v, gv, ws_ref, bst, gout):
    ug, dug = _gelu(u)
    vg, dvg = _gelu(v)
    row = lax.broadcasted_iota(jnp.int32, (CH, CH), 0)
    col = lax.broadcasted_iota(jnp.int32, (CH, CH), 1)
    tril = row >= col
    ys, heads = [], []
    for h in range(GM_HEADS):
        sl = slice(h * 128, (h + 1) * 128)
        vhat, rv = _rms(vg[:, sl])
        vn = (vhat * gv[:, sl]).astype(MXU)
        wt = jnp.where(tril, ws_ref[h], 0.0)
        mixed = _dot(wt.astype(MXU), vn) + bst[:, h:h + 1]
        ys.append(ug[:, sl] * mixed)
        heads.append((vhat, rv, vn, wt, mixed))
    y = jnp.concatenate(ys, axis=1)
    yhat, ry = _rms(y)
    return dict(ug=ug, dug=dug, dvg=dvg, heads=heads, yhat=yhat, ry=ry, tril=tril, out=yhat * gout)


def _shifts_down(cur, halo):
    row8 = lax.broadcasted_iota(jnp.int32, (8, cur.shape[1]), 0)
    out = [cur]
    for j in (1, 2, 3):
        sh = pltpu.roll(cur, j, 0)
        top = jnp.where(row8 < j, pltpu.roll(halo, j, 0), sh[0:8])
        out.append(jnp.concatenate([top, sh[8:]], axis=0))
    return out


def _shifts_up(cur, halo):
    row8 = lax.broadcasted_iota(jnp.int32, (8, cur.shape[1]), 0)
    out = []
    for j in (1, 2, 3):
        sh = pltpu.roll(cur, CH - j, 0)
        bot = jnp.where(row8 + j >= 8, pltpu.roll(halo, 8 - j, 0), sh[CH - 8:CH])
        out.append(jnp.concatenate([sh[0:CH - 8], bot], axis=0))
    return out


def _conv(xbc, halo, convw, convb):
    sh = _shifts_down(xbc, halo)
    return convb + convw[3:4] * sh[0] + convw[2:3] * sh[1] + convw[1:2] * sh[2] + convw[0:1] * sh[3]


def _ssd_fwd_vals(z, conv, dtraw, dtb, alog, dskip, ng, ex, ltri, s_prev):
    sig_c = _sigmoid(conv)
    xa = conv * sig_c
    xs = xa[:, :1024]
    bm = [xa[:, 1024:1152], xa[:, 1152:1280]]
    cm = [xa[:, 1280:1408], xa[:, 1408:1536]]
    dtpre = dtraw + dtb
    dt = _softplus(dtpre)
    a_neg = -jnp.exp(alog)
    cs = _dot_hi(ltri, dt * a_neg)
    cst = cs.T
    last = cs[CH - 1:CH]
    ecs = jnp.exp(cs)
    dec = jnp.exp(last - cs)
    spread = _dot_01(jnp.concatenate([dt, ecs, dec], axis=0), ex)
    dte, ecse, dece = spread[0:CH], spread[CH:2 * CH], spread[2 * CH:3 * CH]
    cde = ecse[CH - 1:CH]
    de = dskip
    xdt = xs * dte
    row = lax.broadcasted_iota(jnp.int32, (CH, CH), 0)
    col = lax.broadcasted_iota(jnp.int32, (CH, CH), 1)
    tril = row >= col
    lo = col < SSD_P
    bmb = [b.astype(MXU) for b in bm]
    cmb = [c.astype(MXU) for c in cm]
    mg = [_dot_nt(cmb[g], bmb[g]) for g in range(2)]
    yd, lms, whs = [], [], []
    for q in range(8):
        g = q // 4
        xq = xdt[:, q * 128:(q + 1) * 128]
        acc = None
        for hh in range(2):
            h = 2 * q + hh
            seg = cs[:, h:h + 1] - cst[h:h + 1, :]
            lm = jnp.exp(jnp.where(tril, seg, NEG))
            wh = (mg[g] * lm).astype(MXU)
            xm = jnp.where(lo if hh == 0 else ~lo, xq, 0.0).astype(MXU)
            part = _dot(wh, xm)
            acc = part if acc is None else acc + part
            lms.append(lm)
            whs.append(wh)
        yd.append(acc)
    yd = jnp.concatenate(yd, axis=1)
    sb = s_prev.astype(MXU)
    yo = jnp.concatenate([_dot(cmb[g], sb[:, g * 512:(g + 1) * 512]) for g in range(2)], axis=1) * ecse
    xdec = (xdt * dece).astype(MXU)
    states = jnp.concatenate([_dot_tn(bmb[g], xdec[:, g * 512:(g + 1) * 512]) for g in range(2)], axis=1)
    s_next = s_prev * cde + states
    ypre = yd + yo + de * xs
    sig_z = _sigmoid(z)
    yg = ypre * z * sig_z
    outs, yhat, rr = [], [], []
    for g in range(2):
        sl = slice(g * 512, (g + 1) * 512)
        yh, r = _rms(yg[:, sl])
        yhat.append(yh)
        rr.append(r)
        outs.append(yh * ng[:, sl])
    return dict(sig_c=sig_c, xa=xa, xs=xs, bmb=bmb, cmb=cmb, dtpre=dtpre, dt=dt, a_neg=a_neg,
                cs=cs, last=last, ecs=ecs, dec=dec, dte=dte, ecse=ecse, dece=dece, cde=cde, de=de, xdt=xdt,
                mg=mg, lms=lms, whs=whs, lo=lo, yo=yo, sb=sb, xdec=xdec, s_next=s_next, ypre=ypre, sig_z=sig_z,
                yhat=yhat, rr=rr, out=jnp.concatenate(outs, axis=1))


def _ssd_fwd(pz, pxbc, dtraw, cat, convw, convb, dtb, alog, dskip, ng, ex, ltri, nb):
    T = pz.shape[0]
    S = T // nb
    nch = S // CH
    ns = _SEQS_PER_STEP if nb % _SEQS_PER_STEP == 0 else 1

    def body(z_ref, xbc_ref, halo_ref, dt_ref, cw_ref, cb_ref, dtb_ref, al_ref, ds_ref, ng_ref, ex_ref, lt_ref,
             cat_in_ref, yb_ref, sall_ref, conv_ref, s_ref):
        del cat_in_ref
        c = pl.program_id(1)

        @pl.when(c == 0)
        def _():
            s_ref[...] = jnp.zeros_like(s_ref)

        for i in range(ns):
            halo = jnp.where(c == 0, 0.0, halo_ref[i])
            s_prev = s_ref[i]
            sall_ref[i, 0] = s_prev
            conv = _conv(xbc_ref[i], halo, cw_ref[...], cb_ref[...])
            conv_ref[i] = conv
            f = _ssd_fwd_vals(z_ref[i], conv, dt_ref[i], dtb_ref[...], al_ref[...], ds_ref[...], ng_ref[...],
                              ex_ref[...], lt_ref[...], s_prev)
            s_ref[i] = f["s_next"]
            yb_ref[i] = f["out"].astype(MXU)

    def seq(width, col=0):
        return pl.BlockSpec((ns, CH, width), lambda b, c: (b, c, col))

    cat, sall, conv = pl.pallas_call(
        body, grid=(nb // ns, nch), name="ssd_fwd",
        in_specs=[seq(1024), seq(CONV_CH),
                  pl.BlockSpec((ns, 8, CONV_CH), lambda b, c: (b, jnp.maximum(c * (CH // 8) - 1, 0), 0)),
                  seq(128),
                  _const((8, CONV_CH)), _const((1, CONV_CH)), _const((1, 128)), _const((1, 128)), _const((1, 1024)),
                  _const((1, 1024)), _const((128, 1024)), _const((CH, CH)), _ANY],
        out_specs=[seq(1024, 1), pl.BlockSpec((ns, 1, 128, 1024), lambda b, c: (b, c, 0, 0)), seq(CONV_CH)],
        out_shape=[_sds((nb, S, 2048), MXU), _sds((nb, nch, 128, 1024), F32), _sds((nb, S, CONV_CH), F32)],
        scratch_shapes=[pltpu.VMEM((ns, 128, 1024), F32)],
        input_output_aliases={12: 0},
        compiler_params=_cp(2),
    )(pz.reshape(nb, S, 1024), pxbc.reshape(nb, S, CONV_CH), pxbc.reshape(nb, S, CONV_CH), dtraw.reshape(nb, S, 128),
      convw, convb, dtb, alog, dskip, ng, ex, ltri, cat.reshape(nb, S, 2048))
    return cat.reshape(T, 2048), sall, conv.reshape(T, CONV_CH)


def _outproj(cat, wo, x, g, tm):
    T = x.shape[0]

    def body(cat_ref, wo_ref, x_ref, g_ref, h1_ref, hn_ref):
        h1 = x_ref[...] + _dot(cat_ref[...], wo_ref[...])
        h1_ref[...] = h1
        hn_ref[...] = (_rms(h1)[0] * g_ref[...]).astype(MXU)

    return pl.pallas_call(
        body, grid=(T // tm,), name="outproj",
        in_specs=[_rows(tm, 2048), _const((2048, D)), _rows(tm, D), _const((1, D))],
        out_specs=[_rows(tm, D), _rows(tm, D)],
        out_shape=[_sds((T, D), F32), _sds((T, D), MXU)],
        compiler_params=_cp(),
    )(cat, wo, x, g)


def _ff1(hn, w1, tm):
    T = hn.shape[0]

    def body(hn_ref, w1_ref, hid_ref):
        hn_v = hn_ref[...]
        for n in range(4):
            hid_ref[:, n * 1024:(n + 1) * 1024] = jnp.maximum(_dot(hn_v, w1_ref[n]), 0.0).astype(MXU)

    return pl.pallas_call(
        body, grid=(T // tm,), name="ff1",
        in_specs=[_rows(tm, D), _const((4, D, 1024))],
        out_specs=_rows(tm, DFF),
        out_shape=_sds((T, DFF), MXU),
        compiler_params=_cp(),
    )(hn, w1)


def _sq(hid):
    h = hid.astype(F32)
    return (h * h).astype(MXU)


def _ff2_tail(hid, w2, h1, g_ple, p, tgt, wg, wp, gf, tm):
    T = h1.shape[0]

    def body(hid_ref, w2_ref, h1_ref, g_ref, p_ref, t_ref, wg_ref, wp_ref, gf_ref,
             hp_ref, dgl_ref, dpe_ref, dh2_ref, dh2b_ref, loss_ref, dgf_ref, dg_ref):
        @pl.when(pl.program_id(0) == 0)
        def _():
            loss_ref[...] = jnp.zeros_like(loss_ref)
            dgf_ref[...] = jnp.zeros_like(dgf_ref)
            dg_ref[...] = jnp.zeros_like(dg_ref)

        h2 = h1_ref[...] + _dot(_sq(hid_ref[...]), w2_ref[...])
        h2h, r2 = _rms(h2)
        g_ple = g_ref[...]
        hp = (h2h * g_ple).astype(MXU)
        hp_ref[...] = hp
        gate = _sigmoid(_dot(hp, wg_ref[...]))
        pb = p_ref[...].astype(MXU)
        pe = jnp.concatenate([_dot(pb, wp_ref[k]) for k in range(4)], axis=1)
        h3 = h2 + gate * pe
        hh, r = _rms(h3)
        gf = gf_ref[...]
        diff = hh * gf - t_ref[...]
        loss_ref[...] += 0.5 * jnp.sum(jnp.mean(diff * diff, axis=-1, keepdims=True))
        dout = diff * (1.0 / D)
        dgf_ref[...] += jnp.sum(dout * hh, axis=0, keepdims=True)
        dh3 = _rms_bwd(dout, hh, r, gf)
        dgl = (dh3 * pe * gate * (1.0 - gate)).astype(MXU)
        dgl_ref[...] = dgl
        dpe_ref[...] = (dh3 * gate).astype(MXU)
        dhp = _dot_nt(dgl, wg_ref[...])
        dg_ref[...] += jnp.sum(dhp * h2h, axis=0, keepdims=True)
        dh2 = dh3 + _rms_bwd(dhp, h2h, r2, g_ple)
        dh2_ref[...] = dh2
        dh2b_ref[...] = dh2.astype(MXU)

    return pl.pallas_call(
        body, grid=(T // tm,), name="ff2_tail",
        in_specs=[_rows(tm, DFF), _const((DFF, D)), _rows(tm, D), _const((1, D)), _rows(tm, DPLE), _rows(tm, D),
                  _const((D, D)), _const((4, DPLE, 256)), _const((1, D))],
        out_specs=[_rows(tm, D), _rows(tm, D), _rows(tm, D), _rows(tm, D), _rows(tm, D), _const((8, 128)),
                   _const((1, D)), _const((1, D))],
        out_shape=[_sds((T, D), MXU), _sds((T, D), MXU), _sds((T, D), MXU), _sds((T, D), F32), _sds((T, D), MXU),
                   _sds((8, 128), F32), _sds((1, D), F32), _sds((1, D), F32)],
        compiler_params=_cp(),
    )(hid, w2, h1, g_ple, p, tgt, wg, wp, gf)


def _ff2_bwd(dh2b, w2, hid, tm):
    T = hid.shape[0]

    def body(dh2b_ref, w2_ref, hid_ref, dpre_ref):
        d = dh2b_ref[...]
        for n in range(DFF // 1024):
            sl = slice(n * 1024, (n + 1) * 1024)
            da = _dot_nt(d, w2_ref[sl, :])
            dpre_ref[:, sl] = (2.0 * da * hid_ref[:, sl].astype(F32)).astype(MXU)

    return pl.pallas_call(
        body, grid=(T // tm,), name="ff2_bwd",
        in_specs=[_rows(tm, D), _const((DFF, D)), _rows(tm, DFF)],
        out_specs=_rows(tm, DFF),
        out_shape=_sds((T, DFF), MXU),
        compiler_params=_cp(),
    )(dh2b, w2, hid)


def _ff1_bwd(dpre, w1, dh2, h1, g, tm, after):
    T = h1.shape[0]

    def body(dpre_ref, w1_ref, dh2_ref, h1_ref, g_ref, dh1_ref, dh1b_ref, dg_ref):
        @pl.when(pl.program_id(0) == 0)
        def _():
            dg_ref[...] = jnp.zeros_like(dg_ref)

        dhn = _dot_nt(dpre_ref[:, 0:1024], w1_ref[0])
        for k in range(1, 4):
            dhn = dhn + _dot_nt(dpre_ref[:, k * 1024:(k + 1) * 1024], w1_ref[k])
        hh, r = _rms(h1_ref[...])
        dg_ref[...] += jnp.sum(dhn * hh, axis=0, keepdims=True)
        dh1 = dh2_ref[...] + _rms_bwd(dhn, hh, r, g_ref[...])
        dh1_ref[...] = dh1
        dh1b_ref[...] = dh1.astype(MXU)

    return pl.pallas_call(
        _after(5, body), grid=(T // tm,), name="ff1_bwd",
        in_specs=[_rows(tm, DFF), _const((4, D, 1024)), _rows(tm, D), _rows(tm, D), _const((1, D)), _ANY],
        out_specs=[_rows(tm, D), _rows(tm, D), _const((1, D))],
        out_shape=[_sds((T, D), F32), _sds((T, D), MXU), _sds((1, D), F32)],
        compiler_params=_cp(),
    )(dpre, w1, dh2, h1, g, after)


def _outproj_bwd(dh1b, wo, tm):
    T = dh1b.shape[0]

    def body(d_ref, wo_ref, dcat_ref):
        d = d_ref[...]
        dcat_ref[:, 0:1024] = _dot_nt(d, wo_ref[0:1024, :])
        dcat_ref[:, 1024:2048] = _dot_nt(d, wo_ref[1024:2048, :])

    return pl.pallas_call(
        body, grid=(T // tm,), name="outproj_bwd",
        in_specs=[_rows(tm, D), _const((2048, D))],
        out_specs=_rows(tm, 2048),
        out_shape=_sds((T, 2048), F32),
        compiler_params=_cp(),
    )(dh1b, wo)


def _gmlp_bwd(uv, dcat, gv, ws, bst, gout, wm):
    T = uv.shape[0]
    nck = 2 if T % (2 * CH) == 0 else 1
    tb = nck * CH

    def body(uv_ref, dya_ref, gv_ref, ws_ref, bst_ref, gout_ref, wuv_ref, duv_ref, dgv_ref, dws_ref, dbst_ref,
             dgo_ref, dxn_ref):
        @pl.when(pl.program_id(0) == 0)
        def _():
            dgv_ref[...] = jnp.zeros_like(dgv_ref)
            dws_ref[...] = jnp.zeros_like(dws_ref)
            dbst_ref[...] = jnp.zeros_like(dbst_ref)
            dgo_ref[...] = jnp.zeros_like(dgo_ref)

        for k in range(nck):
            chunk(slice(k * CH, (k + 1) * CH), uv_ref, dya_ref, gv_ref, ws_ref, bst_ref, gout_ref, duv_ref,
                  dgv_ref, dws_ref, dbst_ref, dgo_ref)
        dxn_ref[...] = _dot_nt(duv_ref[...], wuv_ref[...])

    def chunk(rows, uv_ref, dya_ref, gv_ref, ws_ref, bst_ref, gout_ref, duv_ref, dgv_ref, dws_ref, dbst_ref,
              dgo_ref):
        gv = gv_ref[...]
        f = _gmlp_fwd_vals(uv_ref[rows, 0:1024], uv_ref[rows, 1024:2048], gv, ws_ref, bst_ref[...], gout_ref[...])
        dya = dya_ref[rows, :]
        dgo_ref[...] += jnp.sum(dya * f["yhat"], axis=0, keepdims=True)
        dy = _rms_bwd(dya, f["yhat"], f["ry"], gout_ref[...])
        lane = lax.broadcasted_iota(jnp.int32, (CH, 128), 1)
        dbs = jnp.zeros((CH, 128), F32)
        dug, dvg, dgvs = [], [], []
        for h in range(GM_HEADS):
            sl = slice(h * 128, (h + 1) * 128)
            vhat, rv, vn, wt, mixed = f["heads"][h]
            dyh = dy[:, sl]
            dug.append(dyh * mixed)
            dmixed = dyh * f["ug"][:, sl]
            dmb = dmixed.astype(MXU)
            dws_ref[h] += jnp.where(f["tril"], _dot_nt(dmb, vn), 0.0)
            dbs = dbs + jnp.where(lane == h, jnp.sum(dmixed, axis=1, keepdims=True), 0.0)
            dvn = _dot_tn(wt.astype(MXU), dmb)
            dgvs.append(jnp.sum(dvn * vhat, axis=0, keepdims=True))
            dvg.append(_rms_bwd(dvn, vhat, rv, gv[:, sl]))
        dbst_ref[...] += dbs
        dgv_ref[...] += jnp.concatenate(dgvs, axis=1)
        duv_ref[rows, 0:1024] = (jnp.concatenate(dug, axis=1) * f["dug"]).astype(MXU)
        duv_ref[rows, 1024:2048] = (jnp.concatenate(dvg, axis=1) * f["dvg"]).astype(MXU)

    return pl.pallas_call(
        body, grid=(T // tb,), name="gmlp_bwd",
        in_specs=[_rows(tb, 2048), _rows(tb, 1024, 0), _const((1, 1024)),
                  _const((GM_HEADS, CH, CH)), _const((CH, 128)), _const((1, 1024)), _const((D, 2048))],
        out_specs=[_rows(tb, 2048), _const((1, 1024)), _const((GM_HEADS, CH, CH)), _const((CH, 128)),
                   _const((1, 1024)), _rows(tb, D)],
        out_shape=[_sds((T, 2048), MXU), _sds((1, 1024), F32), _sds((GM_HEADS, CH, CH), F32), _sds((CH, 128), F32),
                   _sds((1, 1024), F32), _sds((T, D), F32)],
        compiler_params=_cp(),
    )(uv, dcat, gv, ws, bst, gout, wm)


def _ssd_bwd(pz, pxbc, conv, dtraw, sall, dcat, convw, dtb, alog, dskip, ng, ex, ltri, ext, nb, after):
    T = pz.shape[0]
    S = T // nb
    nch = S // CH
    ns = _SEQS_PER_STEP if nb % _SEQS_PER_STEP == 0 else 1

    def seq(width, col=0):
        return pl.BlockSpec((ns, CH, width), lambda b, c: (b, nch - 1 - c, col))

    in_specs = [
        seq(1024), seq(CONV_CH), seq(CONV_CH), seq(128),
        _const((8, CONV_CH)), _const((1, 128)), _const((1, 128)), _const((1, 1024)),
        _const((1, 1024)), _const((128, 1024)), _const((CH, CH)),
        _const((1024, 128)),
        pl.BlockSpec((ns, 1, 128, 1024), lambda b, c: (b, nch - 1 - c, 0, 0)),
        seq(1024, 1),
        _ANY,
    ]

    def body(z_ref, xbc_ref, conv_ref, dt_ref, cw_ref, dtb_ref, al_ref, ds_ref, ng_ref, ex_ref, lt_ref,
             ext_ref, sall_ref, dyb_ref,
             dssd_ref, ddt_ref, dcw_ref, dcb_ref, ddtb_ref, dal_ref, dds_ref, dng_ref,
             dst_ref, dnext_ref, ddse_ref):
        b = pl.program_id(0)
        c = pl.program_id(1)

        @pl.when((b == 0) & (c == 0))
        def _():
            for r in (dcw_ref, dcb_ref, ddtb_ref, dal_ref, dds_ref, dng_ref, ddse_ref):
                r[...] = jnp.zeros_like(r)

        @pl.when(c == 0)
        def _():
            dst_ref[...] = jnp.zeros_like(dst_ref)
            dnext_ref[...] = jnp.zeros_like(dnext_ref)

        ex = ex_ref[...]
        ext = ext_ref[...]
        cw = cw_ref[...]
        ng = ng_ref[...]
        for i in range(ns):
            one_chunk(i, ex, ext, cw, ng, z_ref, xbc_ref, conv_ref, dt_ref, dtb_ref, al_ref, ds_ref, lt_ref, sall_ref,
                      dyb_ref, dssd_ref, ddt_ref, dcw_ref, dcb_ref, ddtb_ref, dal_ref, dng_ref, dst_ref, dnext_ref,
                      ddse_ref)

        @pl.when((b == nb // ns - 1) & (c == nch - 1))
        def _():
            dds_ref[...] = _dot_01(jnp.broadcast_to(ddse_ref[...], (8, 1024)), ext)[0:1]

    def one_chunk(i, ex, ext, cw, ng, z_ref, xbc_ref, conv_ref, dt_ref, dtb_ref, al_ref, ds_ref, lt_ref, sall_ref,
                  dyb_ref, dssd_ref, ddt_ref, dcw_ref, dcb_ref, ddtb_ref, dal_ref, dng_ref, dst_ref, dnext_ref,
                  ddse_ref):
        z = z_ref[i]
        s_prev = sall_ref[i, 0]
        conv = conv_ref[i]
        f = _ssd_fwd_vals(z, conv, dt_ref[i], dtb_ref[...], al_ref[...], ds_ref[...], ng, ex, lt_ref[...], s_prev)
        xs, xdt, cs, dec, dt = f["xs"], f["xdt"], f["cs"], f["dec"], f["dt"]
        dyb = dyb_ref[i]
        dyg, dngs = [], []
        for g in range(2):
            sl = slice(g * 512, (g + 1) * 512)
            dngs.append(jnp.sum(dyb[:, sl] * f["yhat"][g], axis=0, keepdims=True))
            dyg.append(_rms_bwd(dyb[:, sl], f["yhat"][g], f["rr"][g], ng[:, sl]))
        dng_ref[...] += jnp.concatenate(dngs, axis=1)
        dyg = jnp.concatenate(dyg, axis=1)
        sig_z = f["sig_z"]
        silu_z = z * sig_z
        dy = dyg * silu_z
        dz = dyg * f["ypre"] * (sig_z + silu_z * (1.0 - sig_z))
        ddse_ref[...] += jnp.sum(dy * xs, axis=0, keepdims=True)
        dxs = dy * f["de"]
        dye = dy * f["ecse"]
        dyeb = dye.astype(MXU)
        dst = dst_ref[i]
        dstb = dst.astype(MXU)
        bmb, cmb, sb, xdec = f["bmb"], f["cmb"], f["sb"], f["xdec"]
        u = jnp.concatenate([_dot(bmb[g], dstb[:, g * 512:(g + 1) * 512]) for g in range(2)], axis=1)
        dxdt = [u[:, q * 128:(q + 1) * 128] * f["dece"][:, q * 128:(q + 1) * 128] for q in range(8)]
        per_head = _dot_01(jnp.concatenate(
            [dy * f["yo"], u * xdt, jnp.broadcast_to(jnp.sum(dst * s_prev, axis=0, keepdims=True), (8, 1024))],
            axis=0), ext)
        dcs = per_head[0:CH]
        t = per_head[CH:2 * CH] * dec
        dcd = per_head[2 * CH:2 * CH + 1]
        row = lax.broadcasted_iota(jnp.int32, (CH, 128), 0)
        lane = lax.broadcasted_iota(jnp.int32, (CH, 128), 1)
        cd = jnp.exp(f["last"])
        dcs = dcs - t + jnp.where(row == CH - 1, jnp.sum(t, axis=0, keepdims=True) + dcd * cd, 0.0)
        dcst = jnp.zeros((128, CH), F32)
        lo = f["lo"]
        dbm, dcm, ds_prev = [], [], []
        for g in range(2):
            sl = slice(g * 512, (g + 1) * 512)
            dmg = jnp.zeros((CH, CH), F32)
            for q in range(4 * g, 4 * g + 4):
                dyq = dy[:, q * 128:(q + 1) * 128]
                xq = xdt[:, q * 128:(q + 1) * 128].astype(MXU)
                for hh in range(2):
                    h = 2 * q + hh
                    m = lo if hh == 0 else ~lo
                    dym = jnp.where(m, dyq, 0.0).astype(MXU)
                    gh = _dot_nt(dym, xq)
                    gl = gh * f["lms"][h]
                    dmg = dmg + gl
                    qh = gl * f["mg"][g]
                    dcs = dcs + jnp.where(lane == h, jnp.sum(qh, axis=1, keepdims=True), 0.0)
                    dcst = dcst - jnp.where(row == h, jnp.sum(qh, axis=0, keepdims=True), 0.0)
                    dxdt[q] = dxdt[q] + _dot_tn(f["whs"][h], dym)
            dmgb = dmg.astype(MXU)
            dcm.append(_dot(dmgb, bmb[g]) + _dot_nt(dyeb[:, sl], sb[:, sl]))
            dbm.append(_dot_tn(dmgb, cmb[g]) + _dot_nt(xdec[:, sl], dstb[:, sl]))
            ds_prev.append(_dot_tn(cmb[g], dyeb[:, sl]))
        dst_ref[i] = jnp.concatenate(ds_prev, axis=1) + dst * f["cde"]
        dcs = dcs + dcst.T
        da = _dot_hi(lt_ref[...].T, dcs)
        dxdt = jnp.concatenate(dxdt, axis=1)
        a_neg = f["a_neg"]
        ddt = da * a_neg + _dot_01(dxdt * xs, ext)
        dal_ref[...] += jnp.sum(da * dt, axis=0, keepdims=True) * a_neg
        dxs = dxs + dxdt * f["dte"]
        ddtraw = jnp.where(lane < SSD_HEADS, ddt * _sigmoid(f["dtpre"]), 0.0)
        ddtb_ref[...] += jnp.sum(ddtraw, axis=0, keepdims=True)
        ddt_ref[i] = ddtraw.astype(MXU)
        dxa = jnp.concatenate([dxs, dbm[0], dbm[1], dcm[0], dcm[1]], axis=1)
        sig_c = f["sig_c"]
        dconv = dxa * (sig_c + f["xa"] * (1.0 - sig_c))
        dcb_ref[...] += jnp.sum(dconv, axis=0, keepdims=True)
        xbc = xbc_ref[i]
        dcw_ref[3:4, :] += jnp.sum(dconv * xbc, axis=0, keepdims=True)
        dxbc = cw[3:4] * dconv
        for j, up in zip((1, 2, 3), _shifts_up(dconv, dnext_ref[i])):
            dcw_ref[3 - j:4 - j, :] += jnp.sum(up * xbc, axis=0, keepdims=True)
            dxbc = dxbc + cw[3 - j:4 - j] * up
        dnext_ref[i] = dconv[0:8]
        dssd_ref[i, :, 0:1024] = dz.astype(MXU)
        dssd_ref[i, :, 1024:2560] = dxbc.astype(MXU)

    dssd, ddt, *small = pl.pallas_call(
        _after(14, body), grid=(nb // ns, nch), name="ssd_bwd",
        in_specs=in_specs,
        out_specs=[seq(2560), seq(128),
                   _const((8, CONV_CH)), _const((1, CONV_CH)), _const((1, 128)), _const((1, 128)), _const((1, 128)),
                   _const((1, 1024))],
        out_shape=[_sds((nb, S, 2560), MXU), _sds((nb, S, 128), MXU), _sds((8, CONV_CH), F32),
                   _sds((1, CONV_CH), F32), _sds((1, 128), F32), _sds((1, 128), F32), _sds((1, 128), F32),
                   _sds((1, 1024), F32)],
        scratch_shapes=[pltpu.VMEM((ns, 128, 1024), F32), pltpu.VMEM((ns, 8, CONV_CH), F32),
                        pltpu.VMEM((1, 1024), F32)],
        compiler_params=_cp(2),
    )(pz.reshape(nb, S, 1024), pxbc.reshape(nb, S, CONV_CH), conv.reshape(nb, S, CONV_CH), dtraw.reshape(nb, S, 128),
      convw, dtb, alog, dskip, ng, ex, ltri, ext, sall, dcat.reshape(nb, S, 2048), after)
    return (dssd.reshape(T, 2560), ddt.reshape(T, 128), *small)


def _inproj_bwd(dxn_uv, dssd, ddt, wm, wdt, dh1, x, g, tm, after):
    T = x.shape[0]

    def body(dxnuv_ref, dssd_ref, ddt_ref, wm_ref, wdt_ref, dh1_ref, x_ref, g_ref, dx_ref, dg_ref):
        @pl.when(pl.program_id(0) == 0)
        def _():
            dg_ref[...] = jnp.zeros_like(dg_ref)

        dxn = (dxnuv_ref[...] + _dot_nt(dssd_ref[...], wm_ref[:, 2048:N_MAIN])
               + _dot_nt(ddt_ref[...], wdt_ref[...]))
        xh, r = _rms(x_ref[...])
        dg_ref[...] += jnp.sum(dxn * xh, axis=0, keepdims=True)
        dx_ref[...] = dh1_ref[...] + _rms_bwd(dxn, xh, r, g_ref[...])

    return pl.pallas_call(
        _after(8, body), grid=(T // tm,), name="inproj_bwd",
        in_specs=[_rows(tm, D), _rows(tm, 2560), _rows(tm, 128), _const((D, N_MAIN)), _const((D, 128)),
                  _rows(tm, D), _rows(tm, D), _const((1, D)), _ANY],
        out_specs=[_rows(tm, D), _const((1, D))],
        out_shape=[_sds((T, D), F32), _sds((1, D), F32)],
        compiler_params=_cp(),
    )(dxn_uv, dssd, ddt, wm, wdt, dh1, x, g, after)


def _matmul_tn(a, b, name, a_fn=None):
    T, M = a.shape
    N = b.shape[1]
    tm = min(M, 1024)
    tn = 1280 if N == 2560 else min(N, 1024)
    tk = min(T, 2048)

    def body(a_ref, b_ref, o_ref, acc_ref):
        k = pl.program_id(2)

        @pl.when(k == 0)
        def _():
            acc_ref[...] = jnp.zeros_like(acc_ref)

        av = a_ref[...]
        if a_fn is not None:
            av = a_fn(av)
        acc_ref[...] += _dot_tn(av, b_ref[...])

        @pl.when(k == T // tk - 1)
        def _():
            o_ref[...] = acc_ref[...].astype(o_ref.dtype)

    return pl.pallas_call(
        body, grid=(M // tm, N // tn, T // tk), name=name,
        in_specs=[pl.BlockSpec((tk, tm), lambda i, j, k: (k, i)), pl.BlockSpec((tk, tn), lambda i, j, k: (k, j))],
        out_specs=pl.BlockSpec((tm, tn), lambda i, j, k: (i, j)),
        out_shape=_sds((M, N), GRAD),
        scratch_shapes=[pltpu.VMEM((tm, tn), F32)],
        compiler_params=_cp(3),
    )(a, b)


def _adamw_vals(w, g, m, v):
    m = B1 * m + (1.0 - B1) * g
    v = B2 * v + (1.0 - B2) * (g * g)
    m_hat = m / (1.0 - B1 ** STEP)
    v_hat = v / (1.0 - B2 ** STEP)
    return -LR * (m_hat / (jnp.sqrt(v_hat) + ADAM_EPS) + WD * w), m, v


def _adamw(w, g, m, v, name):
    R, C = w.shape
    tr = 256 if R % 256 == 0 else R

    def body(w_ref, g_ref, m_ref, v_ref, d_ref, mo_ref, vo_ref):
        d_ref[...], mo_ref[...], vo_ref[...] = _adamw_vals(w_ref[...], g_ref[...], m_ref[...], v_ref[...])

    spec = _rows(tr, C)
    return pl.pallas_call(
        body, grid=(R // tr,), name=name,
        in_specs=[spec] * 4, out_specs=[spec] * 3, out_shape=[_sds((R, C), F32)] * 3,
        compiler_params=_cp(),
    )(w, g, m, v)


def _adamw_halves(w, own, other, m, v, name):
    R, C = w.shape
    half = R // 2
    tr = min(half, 256)
    nth = half // tr

    def body(w_ref, own_ref, oth_ref, m_ref, v_ref, g_ref, d_ref, mo_ref, vo_ref):
        mine = (pl.program_id(0) // nth) == lax.axis_index("c")
        g = jnp.where(mine, own_ref[...], oth_ref[...])
        g_ref[...] = g
        d_ref[...], mo_ref[...], vo_ref[...] = _adamw_vals(w_ref[...], g, m_ref[...], v_ref[...])

    full = _rows(tr, C)
    part = pl.BlockSpec((tr, C), lambda i: (i % nth, 0))
    return pl.pallas_call(
        body, grid=(R // tr,), name=name,
        in_specs=[full, part, part, full, full], out_specs=[full] * 4, out_shape=[_sds((R, C), F32)] * 4,
        compiler_params=_cp(),
    )(w, own, other, m, v)


def _sum_small(slots, name):
    nd, rows, C = slots.shape

    def body(s_ref, o_ref):
        acc = s_ref[0]
        for d in range(1, nd):
            acc = acc + s_ref[d]
        o_ref[...] = acc

    return pl.pallas_call(
        body, grid=(1,), name=name,
        in_specs=[_const((nd, rows, C))], out_specs=_const((rows, C)), out_shape=_sds((rows, C), F32),
        compiler_params=_cp(),
    )(slots)


def _sum_slots(slots, src, kind, shp, kh, name):
    R, C = shp
    rh = R // 2
    tr = min(rh, 256)
    nth = rh // tr
    if kind == "slab":
        src_spec = pl.BlockSpec((1, tr, C), lambda i, kh: (kh[0], kh[1] * nth + i, 0))
    elif kind == "rows":
        src_spec = pl.BlockSpec((tr, C), lambda i, kh: (kh[0] * (R // tr) + kh[1] * nth + i, 0))
    else:
        src_spec = pl.BlockSpec((tr, C), lambda i, kh: (kh[1] * nth + i, kh[0]))

    def body(kh_ref, s_ref, own_ref, o_ref):
        me = 2 * kh_ref[0] + kh_ref[1]
        acc = (own_ref[0] if kind == "slab" else own_ref[...]).astype(F32)
        for k in range(1, 8):
            acc = acc + s_ref[me ^ k].astype(F32)
        o_ref[...] = acc

    return pl.pallas_call(
        body, name=name,
        grid_spec=pltpu.PrefetchScalarGridSpec(
            num_scalar_prefetch=1, grid=(nth,),
            in_specs=[pl.BlockSpec((8, tr, C), lambda i, kh: (0, i, 0)), src_spec],
            out_specs=pl.BlockSpec((tr, C), lambda i, kh: (i, 0))),
        out_shape=_sds((rh, C), F32),
        compiler_params=_cp(),
    )(kh, slots, src)


def _assemble_w_in(slabs):
    tr = 256

    def body(s_ref, wm_ref, wdt_ref):
        full = jnp.concatenate([s_ref[k] for k in range(4)], axis=1)
        wm_ref[...] = full[:, :N_MAIN]
        wdt_ref[...] = jnp.concatenate([full[:, N_MAIN:], jnp.zeros((tr, 128 - 16), full.dtype)], axis=1)

    return pl.pallas_call(
        body, grid=(D // tr,), name="assemble_w_in",
        in_specs=[pl.BlockSpec((4, tr, 1156), lambda i: (0, i, 0))],
        out_specs=[_rows(tr, N_MAIN), _rows(tr, 128)],
        out_shape=[_sds((D, N_MAIN), slabs.dtype), _sds((D, 128), slabs.dtype)],
        compiler_params=_cp(),
    )(slabs)


def _split_dw_in(d_uv, d_ssd, d_dt):
    tr = 256

    def body(uv_ref, ssd_ref, dt_ref, o_ref):
        full = jnp.concatenate([uv_ref[...], ssd_ref[...], dt_ref[:, 0:16]], axis=1)
        for k in range(4):
            o_ref[k] = full[:, 1156 * k:1156 * (k + 1)]

    return pl.pallas_call(
        body, grid=(D // tr,), name="split_dw_in",
        in_specs=[_rows(tr, 2048), _rows(tr, 2560), _rows(tr, 128)],
        out_specs=pl.BlockSpec((4, tr, 1156), lambda i: (0, i, 0)),
        out_shape=_sds((4, D, 1156), d_uv.dtype),
        compiler_params=_cp(),
    )(d_uv, d_ssd, d_dt)


def _cast_into_slot(w, kh, name):
    R, C = w.shape
    tr = 256

    def body(kh_ref, w_ref, o_ref):
        o_ref[0] = w_ref[...].astype(BF16)

    return pl.pallas_call(
        body, name=name,
        grid_spec=pltpu.PrefetchScalarGridSpec(
            num_scalar_prefetch=1, grid=(R // tr,),
            in_specs=[pl.BlockSpec((tr, C), lambda i, kh: (i, 0))],
            out_specs=pl.BlockSpec((1, tr, C), lambda i, kh: (kh[0], i, 0))),
        out_shape=_sds((4, R, C), BF16),
        compiler_params=_cp(),
    )(kh, w)


_ANY = pl.BlockSpec(memory_space=pl.ANY)
_CHIP_FLIPS = [(1, 0), (0, 1), (1, 1)]
_DEVICE_FLIPS = [(fx, fy, fc) for fx in (0, 1) for fy in (0, 1) for fc in (0, 1)][1:]


def _half(h, rows):
    return pl.ds(pl.multiple_of(h * rows, rows), rows)


def _remote(src, dst, ssem, rsem, to):
    return pltpu.make_async_remote_copy(src_ref=src, dst_ref=dst, send_sem=ssem, recv_sem=rsem,
                                        device_id=to, device_id_type=MESH)


def _weight_gather(bufs, conv):
    n = len(bufs)

    def body(*refs):
        conv_ref, outs, conv_out = refs[n], refs[n + 1:2 * n + 1], refs[2 * n + 1]
        send_sems, recv_sems, fsend_sems, frecv_sems, csend_sems, crecv_sems, local_sem = refs[2 * n + 2:]
        x, y, c = lax.axis_index("x"), lax.axis_index("y"), lax.axis_index("c")
        me = 2 * x + y
        halves = [_half(c, r.shape[1] // 2) for r in outs]
        others = [_half(1 - c, r.shape[1] // 2) for r in outs]
        remote = _remote
        local = [pltpu.make_async_copy(conv_ref, conv_out.at[me], local_sem)]
        for cp in local:
            cp.start()
        sends = []
        for k, (fx, fy) in enumerate(_CHIP_FLIPS):
            peer = (x ^ fx, y ^ fy, c)
            for i in range(n):
                mine = outs[i].at[me, halves[i]]
                sends.append(remote(mine, mine, send_sems.at[k * n + i], recv_sems.at[k * n + i], peer))
            sends.append(remote(conv_ref, conv_out.at[me], csend_sems.at[k], crecv_sems.at[k], peer))
        for cp in sends:
            cp.start()
        sibling = (x, y, 1 - c)
        forwards = []
        for k, (fx, fy) in enumerate(_CHIP_FLIPS):
            peer = (x ^ fx, y ^ fy, c)
            src = 2 * (x ^ fx) + (y ^ fy)
            for i in range(n):
                landed = outs[i].at[src, halves[i]]
                remote(landed, landed, send_sems.at[k * n + i], recv_sems.at[k * n + i], peer).wait_recv()
                fw = remote(landed, landed, fsend_sems.at[k * n + i], frecv_sems.at[k * n + i], sibling)
                fw.start()
                forwards.append(fw)
            remote(conv_out.at[src], conv_out.at[src], csend_sems.at[k], crecv_sems.at[k], peer).wait_recv()
        for k, (fx, fy) in enumerate(_CHIP_FLIPS):
            src = 2 * (x ^ fx) + (y ^ fy)
            for i in range(n):
                theirs = outs[i].at[src, others[i]]
                remote(theirs, theirs, fsend_sems.at[k * n + i], frecv_sems.at[k * n + i], sibling).wait_recv()
        for cp in sends + forwards:
            cp.wait_send()
        for cp in local:
            cp.wait()

    dma = pltpu.SemaphoreType.DMA
    return pl.pallas_call(
        body, name="weight_gather",
        in_specs=[_ANY] * (n + 1), out_specs=[_ANY] * (n + 1),
        out_shape=[_sds(b.shape, b.dtype) for b in bufs] + [_sds((4,) + conv.shape, conv.dtype)],
        input_output_aliases={i: i for i in range(n)},
        scratch_shapes=[dma((3 * n,)), dma((3 * n,)), dma((3 * n,)), dma((3 * n,)), dma((3,)), dma((3,)), dma],
    )(*bufs, conv)


def _piece(ref, kind, R, C, k, h):
    if kind == "slab":
        return ref.at[k, _half(h, R // 2), :]
    if kind == "rows":
        return ref.at[pl.ds(pl.multiple_of(k * R + h * (R // 2), R // 2), R // 2), :]
    return ref.at[_half(h, R // 2), pl.ds(pl.multiple_of(k * C, C), C)]


def _small_exchange(small, after):
    rs = small.shape[0]

    def body(s_ref, after_ref, out_ref, send_sems, recv_sems, local_sem):
        del after_ref
        x, y, c = lax.axis_index("x"), lax.axis_index("y"), lax.axis_index("c")
        slot = 4 * x + 2 * y + c
        own = pltpu.make_async_copy(s_ref, out_ref.at[slot], local_sem)
        own.start()
        copies = []
        for k, (fx, fy, fc) in enumerate(_DEVICE_FLIPS):
            copies.append(_remote(s_ref, out_ref.at[slot], send_sems.at[k], recv_sems.at[k], (x ^ fx, y ^ fy, c ^ fc)))
        for cp in copies:
            cp.start()
        for k, (fx, fy, fc) in enumerate(_DEVICE_FLIPS):
            theirs = out_ref.at[slot ^ (k + 1)]
            _remote(theirs, theirs, send_sems.at[k], recv_sems.at[k], (x ^ fx, y ^ fy, c ^ fc)).wait_recv()
        for cp in copies:
            cp.wait_send()
        own.wait()

    dma = pltpu.SemaphoreType.DMA
    return pl.pallas_call(
        body, name="small_exchange",
        in_specs=[_ANY, _ANY], out_specs=_ANY, out_shape=_sds((8, rs, 128), F32),
        scratch_shapes=[dma((7,)), dma((7,)), dma],
    )(small, after)


_HBM = pl.BlockSpec(memory_space=pltpu.HBM)
_SEM = pl.BlockSpec(memory_space=pltpu.SEMAPHORE)


def _split_start(name, arrays, n_copies, plan, after=None):
    n = len(arrays)
    extra = [] if after is None else [after]

    def body(*refs):
        m = n + len(extra)
        arrs, send_sems, recv_sems, token = refs[:n], refs[m], refs[m + 1], refs[-1]
        for j, (src, dst, peer) in enumerate(plan(arrs)):
            _remote(src, dst, send_sems.at[j], recv_sems.at[j], peer).start()
        token[...] = jnp.zeros_like(token)

    dma = pltpu.SemaphoreType.DMA
    res = pl.pallas_call(
        body, name=name,
        out_shape=(dma((n_copies,)), dma((n_copies,)), *[pltpu.HBM(a.shape, a.dtype) for a in arrays],
                   _sds((8, 128), F32)),
        in_specs=[_HBM] * n + [_ANY] * len(extra),
        out_specs=(_SEM, _SEM, *[_HBM] * n, pl.BlockSpec(memory_space=pltpu.VMEM)),
        input_output_aliases={i: 2 + i for i in range(n)},
        compiler_params=pltpu.CompilerParams(has_side_effects=pltpu.SideEffectType.DATAFLOW_SIDE_EFFECTING),
    )(*[pltpu.with_memory_space_constraint(a, pltpu.HBM) for a in arrays], *extra)
    return res[0], res[1], list(res[2:2 + n]), res[-1]


def _split_wait(name, arrays, send_sems, recv_sems, plan, after):
    n = len(arrays)

    def body(*refs):
        arrs, ssems, rsems = refs[:n], refs[n], refs[n + 1]
        for j, (src, dst, peer) in enumerate(plan(arrs)):
            cp = _remote(src, dst, ssems.at[j], rsems.at[j], peer)
            cp.wait_send()
            cp.wait_recv()

    return list(pl.pallas_call(
        body, name=name,
        out_shape=tuple(pltpu.HBM(a.shape, a.dtype) for a in arrays),
        in_specs=[_HBM] * n + [_SEM, _SEM, _ANY],
        out_specs=tuple([_HBM] * n),
        input_output_aliases={i: i for i in range(n)},
        compiler_params=pltpu.CompilerParams(has_side_effects=pltpu.SideEffectType.DATAFLOW_SIDE_EFFECTING),
    )(*arrays, send_sems, recv_sems, after))


def _gather_plan(n):
    def plan(bufs):
        x, y, c = lax.axis_index("x"), lax.axis_index("y"), lax.axis_index("c")
        me = 2 * x + y
        return [(bufs[i].at[me], bufs[i].at[me], (x ^ fx, y ^ fy, c)) for fx, fy in _CHIP_FLIPS for i in range(n)]

    return plan


def _reduce_plan(specs, n_small):
    n = len(specs)

    def plan(arrs):
        x, y, c = lax.axis_index("x"), lax.axis_index("y"), lax.axis_index("c")
        slot = 4 * x + 2 * y + c
        out = []
        for fx, fy, fc in _DEVICE_FLIPS:
            peer = (x ^ fx, y ^ fy, c ^ fc)
            for i, (kind, (R, C)) in enumerate(specs):
                out.append((_piece(arrs[i], kind, R, C, 2 * peer[0] + peer[1], peer[2]), arrs[n + i].at[slot], peer))
            for s in range(n_small):
                out.append((arrs[2 * n + 2 * s], arrs[2 * n + 2 * s + 1].at[slot], peer))
        return out

    return plan


def _sibling_exchange(halves, name):
    n = len(halves)

    def body(*refs):
        ins, outs, send_sems, recv_sems = refs[:n], refs[n:2 * n], refs[2 * n], refs[2 * n + 1]
        sibling = (lax.axis_index("x"), lax.axis_index("y"), 1 - lax.axis_index("c"))
        copies = [pltpu.make_async_remote_copy(src_ref=ins[i], dst_ref=outs[i], send_sem=send_sems.at[i],
                                               recv_sem=recv_sems.at[i], device_id=sibling, device_id_type=MESH)
                  for i in range(n)]
        for cp in copies:
            cp.start()
        for cp in copies:
            cp.wait()

    dma = pltpu.SemaphoreType.DMA
    return pl.pallas_call(
        body, name=name,
        in_specs=[_ANY] * n, out_specs=[_ANY] * n,
        out_shape=[_sds(h.shape, h.dtype) for h in halves],
        scratch_shapes=[dma((n,)), dma((n,))],
    )(*halves)


_BIG = [("w_in", (1024, 1156), "slab"), ("w_out", (512, 1024), "rows"), ("w_ff1", (1024, 1024), "cols"),
        ("w_ff2", (1024, 1024), "rows"), ("w_ple_gate", (256, 1024), "rows"), ("w_ple_proj", (256, 256), "cols")]
_SMALL = [("norm_mix_g", (1, 1024)), ("gm_v_norm_g", (1, 1024)), ("gm_ws", (1, 8, 128, 128)), ("gm_bs", (1, 8, 128)),
          ("gm_out_norm_g", (1, 1024)), ("ssd_conv_w", (1, 4, 1536)), ("ssd_conv_b", (1, 1536)),
          ("ssd_dt_bias", (1, 16)), ("ssd_a_log", (1, 16)), ("ssd_d", (1, 16)), ("ssd_norm_g", (1, 1024)),
          ("norm_mlp_g", (1, 1024)), ("ple_norm_g", (1, 1024)), ("final_norm_g", (1024,))]


def _rows128(a):
    flat = a.reshape(-1)
    rows = -(-flat.shape[0] // 1024) * 8
    return jnp.pad(flat, (0, rows * 128 - flat.shape[0])).reshape(rows, 128)


def _pad_lanes(v, n=128):
    v = v.reshape(1, -1)
    return jnp.pad(v, ((0, 0), (0, n - v.shape[1])))


_SMALL_SHAPES = dict(_SMALL + [("loss", ())])
_BIG_SPECS = {n: (kind, shp) for n, shp, kind in _BIG}


class _Comm:
    def __init__(self, a, kh):
        self.a, self.kh = a, kh
        self.bufs = {n: _cast_into_slot(a[n].reshape(shp), kh, "cast_" + n) for n, shp, _ in _BIG}
        self.sent = []
        self.small_tot = {}

    def w_in(self):
        (g_win,), g_cw = self._gather_now()
        rest = [self.bufs[n] for n, _, _ in _BIG[1:]]
        plan = _gather_plan(len(rest))
        ssem, rsem, thru, token = _split_start("gather_start", rest, 3 * len(rest), plan, after=g_cw)
        self.gather = (plan, ssem, rsem, thru)
        wm, wdt = _assemble_w_in(g_win)
        return wm, wdt, jnp.concatenate([g_cw[k] for k in range(4)], axis=1), token

    def _gather_now(self):
        *bufs, g_cw = _weight_gather([self.bufs["w_in"]], self.a["ssd_conv_w"].reshape(4, 384))
        return bufs, g_cw

    def rest(self, after):
        plan, ssem, rsem, thru = self.gather
        g_wo, g_w1, g_w2, g_wg, g_wp = _split_wait("gather_wait", thru, ssem, rsem, plan, after)
        return g_wo.reshape(2048, D), g_w1, g_w2.reshape(DFF, D), g_wg.reshape(D, D), g_wp

    def send(self, tag, grads):
        big = [n for n, _, _ in _BIG if n in grads]
        small = [n for n in _SMALL_SHAPES if n in grads]
        parts = [_rows128(grads[n]) for n in small]
        rows = [s.shape[0] for s in parts]
        if not big:
            self.last_small = (tag, small, rows, jnp.concatenate(parts, axis=0))
            return None
        srcs = [grads[n] for n in big]
        lands = [lax.empty((8, _BIG_SPECS[n][1][0] // 2, _BIG_SPECS[n][1][1]), GRAD) for n in big]
        extra = []
        if small:
            pack = jnp.concatenate(parts, axis=0)
            extra = [pack, jnp.broadcast_to(pack, (8,) + pack.shape)]
        plan = _reduce_plan([_BIG_SPECS[n] for n in big], len(extra) // 2)
        n_copies = 7 * (len(big) + len(extra) // 2)
        ssem, rsem, thru, token = _split_start("reduce_start_" + tag, srcs + lands + extra, n_copies, plan)
        self.sent.append((tag, big, small, rows, plan, ssem, rsem, thru))
        return token

    def _unpack(self, tot, names, rows):
        o = 0
        for n, r in zip(names, rows):
            shp = _SMALL_SHAPES[n]
            cnt = 1
            for s in shp:
                cnt *= s
            self.small_tot[n] = tot[o:o + r].reshape(-1)[:cnt].reshape(shp)
            o += r

    def finish(self, after):
        a, results = self.a, {}

        def update(names, own, tag):
            other = _sibling_exchange([own[n] for n in names], "sibling_exchange_" + tag)
            for n, oth in zip(names, other):
                shp = _BIG_SPECS[n][1]
                results[n] = _adamw_halves(a[n].reshape(shp), own[n], oth, a["m_" + n].reshape(shp),
                                           a["v_" + n].reshape(shp), "adamw_" + n)
            return results[names[-1]][1]

        own, early = {}, []
        for tag, big, small, rows, plan, ssem, rsem, thru in self.sent:
            if tag == self.sent[-1][0]:
                after = update(early, own, "early")
            arrs = _split_wait("reduce_wait_" + tag, thru, ssem, rsem, plan, after)
            nb_ = len(big)
            for i, n in enumerate(big):
                kind, shp = _BIG_SPECS[n]
                own[n] = _sum_slots(arrs[nb_ + i], arrs[i], kind, shp, self.kh, "sum_" + n)
                after = own[n]
            early += big
            if small:
                self._unpack(_sum_small(arrs[2 * nb_ + 1], "sum_small_" + tag), small, rows)
        after = update(self.sent[-1][1], own, "late")
        tag, small, rows, pack = self.last_small
        self._unpack(_sum_small(_small_exchange(pack, after), "sum_small_" + tag), small, rows)
        return results, dict(self.small_tot)


def _local_step(x, p, tgt, sm, comm, nb, tm):
    wm, wdt, conv_w, token = comm.w_in()
    g_mix, gv, gout = sm["norm_mix_g"].reshape(1, D), sm["gm_v_norm_g"].reshape(1, D), sm["gm_out_norm_g"].reshape(1, D)
    ws = sm["gm_ws"].reshape(GM_HEADS, CH, CH)
    bst = jnp.pad(sm["gm_bs"].reshape(GM_HEADS, CH).T, ((0, 0), (0, 128 - GM_HEADS)))
    convw = jnp.pad(conv_w, ((0, 4), (0, 0)))
    convb = sm["ssd_conv_b"].reshape(1, CONV_CH)
    dtb, alog = _pad_lanes(sm["ssd_dt_bias"]), _pad_lanes(sm["ssd_a_log"])
    dskip = jnp.repeat(sm["ssd_d"].reshape(SSD_HEADS), SSD_P).reshape(1, 1024)
    ng, g_mlp, g_ple = sm["ssd_norm_g"].reshape(1, D), sm["norm_mlp_g"].reshape(1, D), sm["ple_norm_g"].reshape(1, D)
    gf = sm["final_norm_g"].reshape(1, D)
    head_of_lane = lax.broadcasted_iota(jnp.int32, (128, 1024), 1) // SSD_P
    ex = (lax.broadcasted_iota(jnp.int32, (128, 1024), 0) == head_of_lane).astype(BF16)
    ext = ex.T
    ltri = (lax.broadcasted_iota(jnp.int32, (CH, CH), 0) >= lax.broadcasted_iota(jnp.int32, (CH, CH), 1)).astype(F32)

    pz, pxbc, dtraw, xn, cat, uv = _inproj_gmlp(x, g_mix, wm, wdt, gv, ws, bst, gout, tm, token)
    cat, sall, conv = _ssd_fwd(pz, pxbc, dtraw, cat, convw, convb, dtb, alog, dskip, ng, ex, ltri, nb)
    wo, w1, w2, wg, wp = comm.rest(cat)
    h1, hn = _outproj(cat, wo, x, g_mlp, tm)
    hid = _ff1(hn, w1, tm)
    hp, dgl, dpe, dh2, dh2b, loss, d_gf, d_gple = _ff2_tail(hid, w2, h1, g_ple, p, tgt, wg, wp, gf, tm)

    d_wp = _matmul_tn(p, dpe, "dw_ple_proj", a_fn=lambda a: a.astype(MXU))
    d_wg = _matmul_tn(hp, dgl, "dw_ple_gate")
    d_w2 = _matmul_tn(hid, dh2b, "dw_ff2", a_fn=_sq)
    dpre = _ff2_bwd(dh2b, w2, hid, tm)
    d_w1 = _matmul_tn(hn, dpre, "dw_ff1")
    token = comm.send("a", {"w_ple_proj": d_wp, "w_ple_gate": d_wg, "w_ff2": d_w2, "w_ff1": d_w1})
    dh1, dh1b, d_gmlp = _ff1_bwd(dpre, w1, dh2, h1, g_mlp, tm, token)
    dcat = _outproj_bwd(dh1b, wo, tm)
    d_wo = _matmul_tn(cat, dh1b, "dw_out")
    duv, d_gv, d_ws, d_bst, d_gout, dxn_uv = _gmlp_bwd(uv, dcat, gv, ws, bst, gout, wm)
    token = comm.send("b", {
        "w_out": d_wo, "loss": loss[0:1, 0:1], "final_norm_g": d_gf, "ple_norm_g": d_gple, "norm_mlp_g": d_gmlp,
        "gm_v_norm_g": d_gv, "gm_ws": d_ws, "gm_bs": d_bst[:, :GM_HEADS].T, "gm_out_norm_g": d_gout})
    dssd, ddt, d_cw, d_cb, d_dtb, d_al, d_ds, d_ng = _ssd_bwd(
        pz, pxbc, conv, dtraw, sall, dcat, convw, dtb, alog, dskip, ng, ex, ltri, ext, nb, token)
    d_win = _split_dw_in(_matmul_tn(xn, duv, "dw_in_uv"), _matmul_tn(xn, dssd, "dw_in_ssd"),
                         _matmul_tn(xn, ddt, "dw_in_dt"))
    token = comm.send("c", {"w_in": d_win})
    dx, d_gmix = _inproj_bwd(dxn_uv, dssd, ddt, wm, wdt, dh1, x, g_mix, tm, token)
    comm.send("d", {"norm_mix_g": d_gmix, "ssd_conv_w": d_cw[0:4], "ssd_conv_b": d_cb, "ssd_dt_bias": d_dtb[:, :16],
                    "ssd_a_log": d_al[:, :16], "ssd_d": d_ds[:, :16], "ssd_norm_g": d_ng})
    return dx


def kernel(x, p, norm_mix_g, w_in, gm_v_norm_g, gm_ws, gm_bs, gm_out_norm_g, ssd_conv_w, ssd_conv_b, ssd_dt_bias, ssd_a_log, ssd_d, ssd_norm_g, w_out, norm_mlp_g, w_ff1, w_ff2, ple_norm_g, w_ple_gate, w_ple_proj, final_norm_g, loss_target, m_norm_mix_g, m_w_in, m_gm_v_norm_g, m_gm_ws, m_gm_bs, m_gm_out_norm_g, m_ssd_conv_w, m_ssd_conv_b, m_ssd_dt_bias, m_ssd_a_log, m_ssd_d, m_ssd_norm_g, m_w_out, m_norm_mlp_g, m_w_ff1, m_w_ff2, m_ple_norm_g, m_w_ple_gate, m_w_ple_proj, m_final_norm_g, v_norm_mix_g, v_w_in, v_gm_v_norm_g, v_gm_ws, v_gm_bs, v_gm_out_norm_g, v_ssd_conv_w, v_ssd_conv_b, v_ssd_dt_bias, v_ssd_a_log, v_ssd_d, v_ssd_norm_g, v_w_out, v_norm_mlp_g, v_w_ff1, v_w_ff2, v_ple_norm_g, v_w_ple_gate, v_w_ple_proj, v_final_norm_g):
    a = dict(locals())
    order = ["norm_mix_g", "w_in", "gm_v_norm_g", "gm_ws", "gm_bs", "gm_out_norm_g", "ssd_conv_w", "ssd_conv_b",
             "ssd_dt_bias", "ssd_a_log", "ssd_d", "ssd_norm_g", "w_out", "norm_mlp_g", "w_ff1", "w_ff2", "ple_norm_g",
             "w_ple_gate", "w_ple_proj", "final_norm_g"]
    chip = 2 * lax.axis_index("x") + lax.axis_index("y")
    nb, S = x.shape[0], x.shape[1]
    T = nb * S
    sm = {n: a[n] for n, _ in _SMALL if n != "ssd_conv_w"}
    comm = _Comm(a, jnp.stack([chip, lax.axis_index("c")]).astype(jnp.int32))
    dx = _local_step(x.reshape(T, D), p.reshape(T, DPLE), loss_target.reshape(T, D), sm, comm, nb, 512)
    big, g_out = comm.finish(dx)
    delta, new_m, new_v = {}, {}, {}
    for n, _, _ in _BIG:
        g_out[n], delta[n], new_m[n], new_v[n] = (r.reshape(a[n].shape) for r in big[n])
    g_out["ssd_conv_w"] = lax.dynamic_slice(g_out["ssd_conv_w"], (0, 0, chip * 384), (1, 4, 384))
    small_names = [n for n, _ in _SMALL]
    packs = [jnp.concatenate([_rows128(src(n)) for n in small_names], axis=0)
             for src in (lambda n: a[n], lambda n: g_out[n], lambda n: a["m_" + n], lambda n: a["v_" + n])]
    outs = _adamw(*packs, "adamw_small")
    o = 0
    for n in small_names:
        r = _rows128(a[n]).shape[0]
        cnt = a[n].size
        for dst, src in zip((delta, new_m, new_v), outs):
            dst[n] = src[o:o + r].reshape(-1)[:cnt].reshape(a[n].shape)
        o += r
    return (g_out["loss"], dx.reshape(x.shape), *[g_out[n] for n in order], *[delta[n] for n in order],
            *[new_m[n] for n in order], *[new_v[n] for n in order])
```

```python
import jax
import jax.numpy as jnp
from jax import lax
from jax.experimental import pallas as pl
from jax.experimental.pallas import tpu as pltpu

F32 = jnp.float32
BF16 = jnp.bfloat16
MXU = jnp.bfloat16
GRAD = jnp.bfloat16

D = 1024
CH = 128
GM_HEADS = 8
SSD_HEADS = 16
SSD_P = 64
CONV_CH = 1536
N_MAIN = 4608
DFF = 4096
DPLE = 256
EPS = 1e-6
NEG = -1e30

LR, B1, B2, ADAM_EPS, WD, STEP = 0.001, 0.9, 0.999, 1e-08, 0.01, 10

VMEM_LIMIT = 56 * 1024 * 1024
_SEQS_PER_STEP = 2
MESH = pl.DeviceIdType.MESH

INV_SQRT2 = 0.7071067811865476
INV_SQRT_2PI = 0.3989422804014327


def _cp(n_axes=1):
    return pltpu.CompilerParams(dimension_semantics=("arbitrary",) * n_axes, vmem_limit_bytes=VMEM_LIMIT)


def _dot(a, b):
    return jnp.dot(a, b, preferred_element_type=F32)


def _dot_nt(a, b):
    return lax.dot_general(a, b, (((1,), (1,)), ((), ())), preferred_element_type=F32)


def _dot_tn(a, b):
    return lax.dot_general(a, b, (((0,), (0,)), ((), ())), preferred_element_type=F32)


def _dot_hi(a, b):
    return jnp.dot(a, b, preferred_element_type=F32, precision=lax.Precision.HIGHEST)


def _dot_01(a, sel):
    hi = a.astype(BF16)
    lo = (a - hi.astype(F32)).astype(BF16)
    n = a.shape[0]
    r = _dot(jnp.concatenate([hi, lo], axis=0), sel)
    return r[0:n] + r[n:2 * n]


def _rows(tm, n, j=0):
    return pl.BlockSpec((tm, n), lambda i: (i, j))


def _const(shape):
    nd = len(shape)
    return pl.BlockSpec(shape, lambda *_: (0,) * nd)


def _sds(shape, dtype):
    return jax.ShapeDtypeStruct(shape, dtype)


def _rms(x):
    r = lax.rsqrt(jnp.mean(x * x, axis=-1, keepdims=True) + EPS)
    return x * r, r


def _rms_bwd(dy, xhat, r, g):
    dyg = dy * g
    return r * (dyg - xhat * jnp.mean(dyg * xhat, axis=-1, keepdims=True))


def _sigmoid(x):
    return 1.0 / (1.0 + jnp.exp(-x))


def _gelu(x):
    cdf = 0.5 * (1.0 + lax.erf(x * INV_SQRT2))
    pdf = jnp.exp(-0.5 * x * x) * INV_SQRT_2PI
    return x * cdf, cdf + x * pdf


def _softplus(x):
    e = jnp.exp(-jnp.abs(x))
    u = 1.0 + e
    log1p = jnp.where(u == 1.0, e, jnp.log(u) * e / (u - 1.0))
    return jnp.maximum(x, 0.0) + log1p


def _after(n_in, fn):
    def body(*refs):
        return fn(*refs[:n_in], *refs[n_in + 1:])

    return body


def _inproj_gmlp(x, g, wm, wdt, gv, ws, bst, gout, tm, after):
    T = x.shape[0]

    def body(x_ref, g_ref, wm_ref, wdt_ref, gv_ref, ws_ref, bst_ref, gout_ref,
             z_ref, xbc_ref, dt_ref, xn_ref, ya_ref, uv_ref):
        xh, _ = _rms(x_ref[...])
        xn = (xh * g_ref[...]).astype(MXU)
        xn_ref[...] = xn
        for n in range(4):
            uv_ref[:, n * 512:(n + 1) * 512] = _dot(xn, wm_ref[:, n * 512:(n + 1) * 512])
        for n in range(2):
            z_ref[:, n * 512:(n + 1) * 512] = _dot(xn, wm_ref[:, 2048 + n * 512:2048 + (n + 1) * 512])
        for n in range(3):
            xbc_ref[:, n * 512:(n + 1) * 512] = _dot(xn, wm_ref[:, 3072 + n * 512:3072 + (n + 1) * 512])
        dt_ref[...] = _dot(xn, wdt_ref[...])
        for k in range(tm // CH):
            rows = slice(k * CH, (k + 1) * CH)
            f = _gmlp_fwd_vals(uv_ref[rows, 0:1024], uv_ref[rows, 1024:2048], gv_ref[...], ws_ref, bst_ref[...],
                               gout_ref[...])
            ya_ref[rows, :] = f["out"].astype(MXU)

    return pl.pallas_call(
        _after(8, body), grid=(T // tm,), name="inproj_gmlp",
        in_specs=[_rows(tm, D), _const((1, D)), _const((D, N_MAIN)), _const((D, 128)), _const((1, 1024)),
                  _const((GM_HEADS, CH, CH)), _const((CH, 128)), _const((1, 1024)), _ANY],
        out_specs=[_rows(tm, 1024), _rows(tm, CONV_CH), _rows(tm, 128), _rows(tm, D), _rows(tm, 1024, 0),
                   _rows(tm, 2048)],
        out_shape=[_sds((T, 1024), F32), _sds((T, CONV_CH), F32), _sds((T, 128), F32), _sds((T, D), MXU),
                   _sds((T, 2048), MXU), _sds((T, 2048), F32)],
        compiler_params=_cp(),
    )(x, g, wm, wdt, gv, ws, bst, gout, after)


def _gmlp_fwd_vals(u, v, gv, ws_ref, bst, gout):
    ug, dug = _gelu(u)
    vg, dvg = _gelu(v)
    row = lax.broadcasted_iota(jnp.int32, (CH, CH), 0)
    col = lax.broadcasted_iota(jnp.int32, (CH, CH), 1)
    tril = row >= col
    ys, heads = [], []
    for h in range(GM_HEADS):
        sl = slice(h * 128, (h + 1) * 128)
        vhat, rv = _rms(vg[:, sl])
        vn = (vhat * gv[:, sl]).astype(MXU)
        wt = jnp.where(tril, ws_ref[h], 0.0)
        mixed = _dot(wt.astype(MXU), vn) + bst[:, h:h + 1]
        ys.append(ug[:, sl] * mixed)
        heads.append((vhat, rv, vn, wt, mixed))
    y = jnp.concatenate(ys, axis=1)
    yhat, ry = _rms(y)
    return dict(ug=ug, dug=dug, dvg=dvg, heads=heads, yhat=yhat, ry=ry, tril=tril, out=yhat * gout)


def _shifts_down(cur, halo):
    row8 = lax.broadcasted_iota(jnp.int32, (8, cur.shape[1]), 0)
    out = [cur]
    for j in (1, 2, 3):
        sh = pltpu.roll(cur, j, 0)
        top = jnp.where(row8 < j, pltpu.roll(halo, j, 0), sh[0:8])
        out.append(jnp.concatenate([top, sh[8:]], axis=0))
    return out


def _shifts_up(cur, halo):
    row8 = lax.broadcasted_iota(jnp.int32, (8, cur.shape[1]), 0)
    out = []
    for j in (1, 2, 3):
        sh = pltpu.roll(cur, CH - j, 0)
        bot = jnp.where(row8 + j >= 8, pltpu.roll(halo, 8 - j, 0), sh[CH - 8:CH])
        out.append(jnp.concatenate([sh[0:CH - 8], bot], axis=0))
    return out


def _conv(xbc, halo, convw, convb):
    sh = _shifts_down(xbc, halo)
    return convb + convw[3:4] * sh[0] + convw[2:3] * sh[1] + convw[1:2] * sh[2] + convw[0:1] * sh[3]


def _ssd_fwd_vals(z, conv, dtraw, dtb, alog, dskip, ng, ex, ltri, s_prev):
    sig_c = _sigmoid(conv)
    xa = conv * sig_c
    xs = xa[:, :1024]
    bm = [xa[:, 1024:1152], xa[:, 1152:1280]]
    cm = [xa[:, 1280:1408], xa[:, 1408:1536]]
    dtpre = dtraw + dtb
    dt = _softplus(dtpre)
    a_neg = -jnp.exp(alog)
    cs = _dot_hi(ltri, dt * a_neg)
    cst = cs.T
    last = cs[CH - 1:CH]
    ecs = jnp.exp(cs)
    dec = jnp.exp(last - cs)
    spread = _dot_01(jnp.concatenate([dt, ecs, dec], axis=0), ex)
    dte, ecse, dece = spread[0:CH], spread[CH:2 * CH], spread[2 * CH:3 * CH]
    cde = ecse[CH - 1:CH]
    de = dskip
    xdt = xs * dte
    row = lax.broadcasted_iota(jnp.int32, (CH, CH), 0)
    col = lax.broadcasted_iota(jnp.int32, (CH, CH), 1)
    tril = row >= col
    lo = col < SSD_P
    bmb = [b.astype(MXU) for b in bm]
    cmb = [c.astype(MXU) for c in cm]
    mg = [_dot_nt(cmb[g], bmb[g]) for g in range(2)]
    yd, lms, whs = [], [], []
    for q in range(8):
        g = q // 4
        xq = xdt[:, q * 128:(q + 1) * 128]
        acc = None
        for hh in range(2):
            h = 2 * q + hh
            seg = cs[:, h:h + 1] - cst[h:h + 1, :]
            lm = jnp.exp(jnp.where(tril, seg, NEG))
            wh = (mg[g] * lm).astype(MXU)
            xm = jnp.where(lo if hh == 0 else ~lo, xq, 0.0).astype(MXU)
            part = _dot(wh, xm)
            acc = part if acc is None else acc + part
            lms.append(lm)
            whs.append(wh)
        yd.append(acc)
    yd = jnp.concatenate(yd, axis=1)
    sb = s_prev.astype(MXU)
    yo = jnp.concatenate([_dot(cmb[g], sb[:, g * 512:(g + 1) * 512]) for g in range(2)], axis=1) * ecse
    xdec = (xdt * dece).astype(MXU)
    states = jnp.concatenate([_dot_tn(bmb[g], xdec[:, g * 512:(g + 1) * 512]) for g in range(2)], axis=1)
    s_next = s_prev * cde + states
    ypre = yd + yo + de * xs
    sig_z = _sigmoid(z)
    yg = ypre * z * sig_z
    outs, yhat, rr = [], [], []
    for g in range(2):
        sl = slice(g * 512, (g + 1) * 512)
        yh, r = _rms(yg[:, sl])
        yhat.append(yh)
        rr.append(r)
        outs.append(yh * ng[:, sl])
    return dict(sig_c=sig_c, xa=xa, xs=xs, bmb=bmb, cmb=cmb, dtpre=dtpre, dt=dt, a_neg=a_neg,
                cs=cs, last=last, ecs=ecs, dec=dec, dte=dte, ecse=ecse, dece=dece, cde=cde, de=de, xdt=xdt,
                mg=mg, lms=lms, whs=whs, lo=lo, yo=yo, sb=sb, xdec=xdec, s_next=s_next, ypre=ypre, sig_z=sig_z,
                yhat=yhat, rr=rr, out=jnp.concatenate(outs, axis=1))


def _ssd_fwd(pz, pxbc, dtraw, cat, convw, convb, dtb, alog, dskip, ng, ex, ltri, nb):
    T = pz.shape[0]
    S = T // nb
    nch = S // CH
    ns = _SEQS_PER_STEP if nb % _SEQS_PER_STEP == 0 else 1

    def body(z_ref, xbc_ref, halo_ref, dt_ref, cw_ref, cb_ref, dtb_ref, al_ref, ds_ref, ng_ref, ex_ref, lt_ref,
             cat_in_ref, yb_ref, sall_ref, conv_ref, s_ref):
        del cat_in_ref
        c = pl.program_id(1)

        @pl.when(c == 0)
        def _():
            s_ref[...] = jnp.zeros_like(s_ref)

        for i in range(ns):
            halo = jnp.where(c == 0, 0.0, halo_ref[i])
            s_prev = s_ref[i]
            sall_ref[i, 0] = s_prev
            conv = _conv(xbc_ref[i], halo, cw_ref[...], cb_ref[...])
            conv_ref[i] = conv
            f = _ssd_fwd_vals(z_ref[i], conv, dt_ref[i], dtb_ref[...], al_ref[...], ds_ref[...], ng_ref[...],
                              ex_ref[...], lt_ref[...], s_prev)
            s_ref[i] = f["s_next"]
            yb_ref[i] = f["out"].astype(MXU)

    def seq(width, col=0):
        return pl.BlockSpec((ns, CH, width), lambda b, c: (b, c, col))

    cat, sall, conv = pl.pallas_call(
        body, grid=(nb // ns, nch), name="ssd_fwd",
        in_specs=[seq(1024), seq(CONV_CH),
                  pl.BlockSpec((ns, 8, CONV_CH), lambda b, c: (b, jnp.maximum(c * (CH // 8) - 1, 0), 0)),
                  seq(128),
                  _const((8, CONV_CH)), _const((1, CONV_CH)), _const((1, 128)), _const((1, 128)), _const((1, 1024)),
                  _const((1, 1024)), _const((128, 1024)), _const((CH, CH)), _ANY],
        out_specs=[seq(1024, 1), pl.BlockSpec((ns, 1, 128, 1024), lambda b, c: (b, c, 0, 0)), seq(CONV_CH)],
        out_shape=[_sds((nb, S, 2048), MXU), _sds((nb, nch, 128, 1024), F32), _sds((nb, S, CONV_CH), F32)],
        scratch_shapes=[pltpu.VMEM((ns, 128, 1024), F32)],
        input_output_aliases={12: 0},
        compiler_params=_cp(2),
    )(pz.reshape(nb, S, 1024), pxbc.reshape(nb, S, CONV_CH), pxbc.reshape(nb, S, CONV_CH), dtraw.reshape(nb, S, 128),
      convw, convb, dtb, alog, dskip, ng, ex, ltri, cat.reshape(nb, S, 2048))
    return cat.reshape(T, 2048), sall, conv.reshape(T, CONV_CH)


def _outproj(cat, wo, x, g, tm):
    T = x.shape[0]

    def body(cat_ref, wo_ref, x_ref, g_ref, h1_ref, hn_ref):
        h1 = x_ref[...] + _dot(cat_ref[...], wo_ref[...])
        h1_ref[...] = h1
        hn_ref[...] = (_rms(h1)[0] * g_ref[...]).astype(MXU)

    return pl.pallas_call(
        body, grid=(T // tm,), name="outproj",
        in_specs=[_rows(tm, 2048), _const((2048, D)), _rows(tm, D), _const((1, D))],
        out_specs=[_rows(tm, D), _rows(tm, D)],
        out_shape=[_sds((T, D), F32), _sds((T, D), MXU)],
        compiler_params=_cp(),
    )(cat, wo, x, g)


def _ff1(hn, w1, tm):
    T = hn.shape[0]

    def body(hn_ref, w1_ref, hid_ref):
        hn_v = hn_ref[...]
        for n in range(4):
            hid_ref[:, n * 1024:(n + 1) * 1024] = jnp.maximum(_dot(hn_v, w1_ref[n]), 0.0).astype(MXU)

    return pl.pallas_call(
        body, grid=(T // tm,), name="ff1",
        in_specs=[_rows(tm, D), _const((4, D, 1024))],
        out_specs=_rows(tm, DFF),
        out_shape=_sds((T, DFF), MXU),
        compiler_params=_cp(),
    )(hn, w1)


def _sq(hid):
    h = hid.astype(F32)
    return (h * h).astype(MXU)


def _ff2_tail(hid, w2, h1, g_ple, p, tgt, wg, wp, gf, tm):
    T = h1.shape[0]

    def body(hid_ref, w2_ref, h1_ref, g_ref, p_ref, t_ref, wg_ref, wp_ref, gf_ref,
             hp_ref, dgl_ref, dpe_ref, dh2_ref, dh2b_ref, loss_ref, dgf_ref, dg_ref):
        @pl.when(pl.program_id(0) == 0)
        def _():
            loss_ref[...] = jnp.zeros_like(loss_ref)
            dgf_ref[...] = jnp.zeros_like(dgf_ref)
            dg_ref[...] = jnp.zeros_like(dg_ref)

        h2 = h1_ref[...] + _dot(_sq(hid_ref[...]), w2_ref[...])
        h2h, r2 = _rms(h2)
        g_ple = g_ref[...]
        hp = (h2h * g_ple).astype(MXU)
        hp_ref[...] = hp
        gate = _sigmoid(_dot(hp, wg_ref[...]))
        pb = p_ref[...].astype(MXU)
        pe = jnp.concatenate([_dot(pb, wp_ref[k]) for k in range(4)], axis=1)
        h3 = h2 + gate * pe
        hh, r = _rms(h3)
        gf = gf_ref[...]
        diff = hh * gf - t_ref[...]
        loss_ref[...] += 0.5 * jnp.sum(jnp.mean(diff * diff, axis=-1, keepdims=True))
        dout = diff * (1.0 / D)
        dgf_ref[...] += jnp.sum(dout * hh, axis=0, keepdims=True)
        dh3 = _rms_bwd(dout, hh, r, gf)
        dgl = (dh3 * pe * gate * (1.0 - gate)).astype(MXU)
        dgl_ref[...] = dgl
        dpe_ref[...] = (dh3 * gate).astype(MXU)
        dhp = _dot_nt(dgl, wg_ref[...])
        dg_ref[...] += jnp.sum(dhp * h2h, axis=0, keepdims=True)
        dh2 = dh3 + _rms_bwd(dhp, h2h, r2, g_ple)
        dh2_ref[...] = dh2
        dh2b_ref[...] = dh2.astype(MXU)

    return pl.pallas_call(
        body, grid=(T // tm,), name="ff2_tail",
        in_specs=[_rows(tm, DFF), _const((DFF, D)), _rows(tm, D), _const((1, D)), _rows(tm, DPLE), _rows(tm, D),
                  _const((D, D)), _const((4, DPLE, 256)), _const((1, D))],
        out_specs=[_rows(tm, D), _rows(tm, D), _rows(tm, D), _rows(tm, D), _rows(tm, D), _const((8, 128)),
                   _const((1, D)), _const((1, D))],
        out_shape=[_sds((T, D), MXU), _sds((T, D), MXU), _sds((T, D), MXU), _sds((T, D), F32), _sds((T, D), MXU),
                   _sds((8, 128), F32), _sds((1, D), F32), _sds((1, D), F32)],
        compiler_params=_cp(),
    )(hid, w2, h1, g_ple, p, tgt, wg, wp, gf)


def _ff2_bwd(dh2b, w2, hid, tm):
    T = hid.shape[0]

    def body(dh2b_ref, w2_ref, hid_ref, dpre_ref):
        d = dh2b_ref[...]
        for n in range(DFF // 1024):
            sl = slice(n * 1024, (n + 1) * 1024)
            da = _dot_nt(d, w2_ref[sl, :])
            dpre_ref[:, sl] = (2.0 * da * hid_ref[:, sl].astype(F32)).astype(MXU)

    return pl.pallas_call(
        body, grid=(T // tm,), name="ff2_bwd",
        in_specs=[_rows(tm, D), _const((DFF, D)), _rows(tm, DFF)],
        out_specs=_rows(tm, DFF),
        out_shape=_sds((T, DFF), MXU),
        compiler_params=_cp(),
    )(dh2b, w2, hid)


def _ff1_bwd(dpre, w1, dh2, h1, g, tm, after):
    T = h1.shape[0]

    def body(dpre_ref, w1_ref, dh2_ref, h1_ref, g_ref, dh1_ref, dh1b_ref, dg_ref):
        @pl.when(pl.program_id(0) == 0)
        def _():
            dg_ref[...] = jnp.zeros_like(dg_ref)

        dhn = _dot_nt(dpre_ref[:, 0:1024], w1_ref[0])
        for k in range(1, 4):
            dhn = dhn + _dot_nt(dpre_ref[:, k * 1024:(k + 1) * 1024], w1_ref[k])
        hh, r = _rms(h1_ref[...])
        dg_ref[...] += jnp.sum(dhn * hh, axis=0, keepdims=True)
        dh1 = dh2_ref[...] + _rms_bwd(dhn, hh, r, g_ref[...])
        dh1_ref[...] = dh1
        dh1b_ref[...] = dh1.astype(MXU)

    return pl.pallas_call(
        _after(5, body), grid=(T // tm,), name="ff1_bwd",
        in_specs=[_rows(tm, DFF), _const((4, D, 1024)), _rows(tm, D), _rows(tm, D), _const((1, D)), _ANY],
        out_specs=[_rows(tm, D), _rows(tm, D), _const((1, D))],
        out_shape=[_sds((T, D), F32), _sds((T, D), MXU), _sds((1, D), F32)],
        compiler_params=_cp(),
    )(dpre, w1, dh2, h1, g, after)


def _outproj_bwd(dh1b, wo, tm):
    T = dh1b.shape[0]

    def body(d_ref, wo_ref, dcat_ref):
        d = d_ref[...]
        dcat_ref[:, 0:1024] = _dot_nt(d, wo_ref[0:1024, :])
        dcat_ref[:, 1024:2048] = _dot_nt(d, wo_ref[1024:2048, :])

    return pl.pallas_call(
        body, grid=(T // tm,), name="outproj_bwd",
        in_specs=[_rows(tm, D), _const((2048, D))],
        out_specs=_rows(tm, 2048),
        out_shape=_sds((T, 2048), F32),
        compiler_params=_cp(),
    )(dh1b, wo)


def _gmlp_bwd(uv, dcat, gv, ws, bst, gout, wm):
    T = uv.shape[0]
    nck = 2 if T % (2 * CH) == 0 else 1
    tb = nck * CH

    def body(uv_ref, dya_ref, gv_ref, ws_ref, bst_ref, gout_ref, wuv_ref, duv_ref, dgv_ref, dws_ref, dbst_ref,
             dgo_ref, dxn_ref):
        @pl.when(pl.program_id(0) == 0)
        def _():
            dgv_ref[...] = jnp.zeros_like(dgv_ref)
            dws_ref[...] = jnp.zeros_like(dws_ref)
            dbst_ref[...] = jnp.zeros_like(dbst_ref)
            dgo_ref[...] = jnp.zeros_like(dgo_ref)

        for k in range(nck):
            chunk(slice(k * CH, (k + 1) * CH), uv_ref, dya_ref, gv_ref, ws_ref, bst_ref, gout_ref, duv_ref,
                  dgv_ref, dws_ref, dbst_ref, dgo_ref)
        dxn_ref[...] = _dot_nt(duv_ref[...], wuv_ref[...])

    def chunk(rows, uv_ref, dya_ref, gv_ref, ws_ref, bst_ref, gout_ref, duv_ref, dgv_ref, dws_ref, dbst_ref,
              dgo_ref):
        gv = gv_ref[...]
        f = _gmlp_fwd_vals(uv_ref[rows, 0:1024], uv_ref[rows, 1024:2048], gv, ws_ref, bst_ref[...], gout_ref[...])
        dya = dya_ref[rows, :]
        dgo_ref[...] += jnp.sum(dya * f["yhat"], axis=0, keepdims=True)
        dy = _rms_bwd(dya, f["yhat"], f["ry"], gout_ref[...])
        lane = lax.broadcasted_iota(jnp.int32, (CH, 128), 1)
        dbs = jnp.zeros((CH, 128), F32)
        dug, dvg, dgvs = [], [], []
        for h in range(GM_HEADS):
            sl = slice(h * 128, (h + 1) * 128)
            vhat, rv, vn, wt, mixed = f["heads"][h]
            dyh = dy[:, sl]
            dug.append(dyh * mixed)
            dmixed = dyh * f["ug"][:, sl]
            dmb = dmixed.astype(MXU)
            dws_ref[h] += jnp.where(f["tril"], _dot_nt(dmb, vn), 0.0)
            dbs = dbs + jnp.where(lane == h, jnp.sum(dmixed, axis=1, keepdims=True), 0.0)
            dvn = _dot_tn(wt.astype(MXU), dmb)
            dgvs.append(jnp.sum(dvn * vhat, axis=0, keepdims=True))
            dvg.append(_rms_bwd(dvn, vhat, rv, gv[:, sl]))
        dbst_ref[...] += dbs
        dgv_ref[...] += jnp.concatenate(dgvs, axis=1)
        duv_ref[rows, 0:1024] = (jnp.concatenate(dug, axis=1) * f["dug"]).astype(MXU)
        duv_ref[rows, 1024:2048] = (jnp.concatenate(dvg, axis=1) * f["dvg"]).astype(MXU)

    return pl.pallas_call(
        body, grid=(T // tb,), name="gmlp_bwd",
        in_specs=[_rows(tb, 2048), _rows(tb, 1024, 0), _const((1, 1024)),
                  _const((GM_HEADS, CH, CH)), _const((CH, 128)), _const((1, 1024)), _const((D, 2048))],
        out_specs=[_rows(tb, 2048), _const((1, 1024)), _const((GM_HEADS, CH, CH)), _const((CH, 128)),
                   _const((1, 1024)), _rows(tb, D)],
        out_shape=[_sds((T, 2048), MXU), _sds((1, 1024), F32), _sds((GM_HEADS, CH, CH), F32), _sds((CH, 128), F32),
                   _sds((1, 1024), F32), _sds((T, D), F32)],
        compiler_params=_cp(),
    )(uv, dcat, gv, ws, bst, gout, wm)


def _ssd_bwd(pz, pxbc, conv, dtraw, sall, dcat, convw, dtb, alog, dskip, ng, ex, ltri, ext, nb, after):
    T = pz.shape[0]
    S = T // nb
    nch = S // CH
    ns = _SEQS_PER_STEP if nb % _SEQS_PER_STEP == 0 else 1

    def seq(width, col=0):
        return pl.BlockSpec((ns, CH, width), lambda b, c: (b, nch - 1 - c, col))

    in_specs = [
        seq(1024), seq(CONV_CH), seq(CONV_CH), seq(128),
        _const((8, CONV_CH)), _const((1, 128)), _const((1, 128)), _const((1, 1024)),
        _const((1, 1024)), _const((128, 1024)), _const((CH, CH)),
        _const((1024, 128)),
        pl.BlockSpec((ns, 1, 128, 1024), lambda b, c: (b, nch - 1 - c, 0, 0)),
        seq(1024, 1),
        _ANY,
    ]

    def body(z_ref, xbc_ref, conv_ref, dt_ref, cw_ref, dtb_ref, al_ref, ds_ref, ng_ref, ex_ref, lt_ref,
             ext_ref, sall_ref, dyb_ref,
             dssd_ref, ddt_ref, dcw_ref, dcb_ref, ddtb_ref, dal_ref, dds_ref, dng_ref,
             dst_ref, dnext_ref, ddse_ref):
        b = pl.program_id(0)
        c = pl.program_id(1)

        @pl.when((b == 0) & (c == 0))
        def _():
            for r in (dcw_ref, dcb_ref, ddtb_ref, dal_ref, dds_ref, dng_ref, ddse_ref):
                r[...] = jnp.zeros_like(r)

        @pl.when(c == 0)
        def _():
            dst_ref[...] = jnp.zeros_like(dst_ref)
            dnext_ref[...] = jnp.zeros_like(dnext_ref)

        ex = ex_ref[...]
        ext = ext_ref[...]
        cw = cw_ref[...]
        ng = ng_ref[...]
        for i in range(ns):
            one_chunk(i, ex, ext, cw, ng, z_ref, xbc_ref, conv_ref, dt_ref, dtb_ref, al_ref, ds_ref, lt_ref, sall_ref,
                      dyb_ref, dssd_ref, ddt_ref, dcw_ref, dcb_ref, ddtb_ref, dal_ref, dng_ref, dst_ref, dnext_ref,
                      ddse_ref)

        @pl.when((b == nb // ns - 1) & (c == nch - 1))
        def _():
            dds_ref[...] = _dot_01(jnp.broadcast_to(ddse_ref[...], (8, 1024)), ext)[0:1]

    def one_chunk(i, ex, ext, cw, ng, z_ref, xbc_ref, conv_ref, dt_ref, dtb_ref, al_ref, ds_ref, lt_ref, sall_ref,
                  dyb_ref, dssd_ref, ddt_ref, dcw_ref, dcb_ref, ddtb_ref, dal_ref, dng_ref, dst_ref, dnext_ref,
                  ddse_ref):
        z = z_ref[i]
        s_prev = sall_ref[i, 0]
        conv = conv_ref[i]
        f = _ssd_fwd_vals(z, conv, dt_ref[i], dtb_ref[...], al_ref[...], ds_ref[...], ng, ex, lt_ref[...], s_prev)
        xs, xdt, cs, dec, dt = f["xs"], f["xdt"], f["cs"], f["dec"], f["dt"]
        dyb = dyb_ref[i]
        dyg, dngs = [], []
        for g in range(2):
            sl = slice(g * 512, (g + 1) * 512)
            dngs.append(jnp.sum(dyb[:, sl] * f["yhat"][g], axis=0, keepdims=True))
            dyg.append(_rms_bwd(dyb[:, sl], f["yhat"][g], f["rr"][g], ng[:, sl]))
        dng_ref[...] += jnp.concatenate(dngs, axis=1)
        dyg = jnp.concatenate(dyg, axis=1)
        sig_z = f["sig_z"]
        silu_z = z * sig_z
        dy = dyg * silu_z
        dz = dyg * f["ypre"] * (sig_z + silu_z * (1.0 - sig_z))
        ddse_ref[...] += jnp.sum(dy * xs, axis=0, keepdims=True)
        dxs = dy * f["de"]
        dye = dy * f["ecse"]
        dyeb = dye.astype(MXU)
        dst = dst_ref[i]
        dstb = dst.astype(MXU)
        bmb, cmb, sb, xdec = f["bmb"], f["cmb"], f["sb"], f["xdec"]
        u = jnp.concatenate([_dot(bmb[g], dstb[:, g * 512:(g + 1) * 512]) for g in range(2)], axis=1)
        dxdt = [u[:, q * 128:(q + 1) * 128] * f["dece"][:, q * 128:(q + 1) * 128] for q in range(8)]
        per_head = _dot_01(jnp.concatenate(
            [dy * f["yo"], u * xdt, jnp.broadcast_to(jnp.sum(dst * s_prev, axis=0, keepdims=True), (8, 1024))],
            axis=0), ext)
        dcs = per_head[0:CH]
        t = per_head[CH:2 * CH] * dec
        dcd = per_head[2 * CH:2 * CH + 1]
        row = lax.broadcasted_iota(jnp.int32, (CH, 128), 0)
        lane = lax.broadcasted_iota(jnp.int32, (CH, 128), 1)
        cd = jnp.exp(f["last"])
        dcs = dcs - t + jnp.where(row == CH - 1, jnp.sum(t, axis=0, keepdims=True) + dcd * cd, 0.0)
        dcst = jnp.zeros((128, CH), F32)
        lo = f["lo"]
        dbm, dcm, ds_prev = [], [], []
        for g in range(2):
            sl = slice(g * 512, (g + 1) * 512)
            dmg = jnp.zeros((CH, CH), F32)
            for q in range(4 * g, 4 * g + 4):
                dyq = dy[:, q * 128:(q + 1) * 128]
                xq = xdt[:, q * 128:(q + 1) * 128].astype(MXU)
                for hh in range(2):
                    h = 2 * q + hh
                    m = lo if hh == 0 else ~lo
                    dym = jnp.where(m, dyq, 0.0).astype(MXU)
                    gh = _dot_nt(dym, xq)
                    gl = gh * f["lms"][h]
                    dmg = dmg + gl
                    qh = gl * f["mg"][g]
                    dcs = dcs + jnp.where(lane == h, jnp.sum(qh, axis=1, keepdims=True), 0.0)
                    dcst = dcst - jnp.where(row == h, jnp.sum(qh, axis=0, keepdims=True), 0.0)
                    dxdt[q] = dxdt[q] + _dot_tn(f["whs"][h], dym)
            dmgb = dmg.astype(MXU)
            dcm.append(_dot(dmgb, bmb[g]) + _dot_nt(dyeb[:, sl], sb[:, sl]))
            dbm.append(_dot_tn(dmgb, cmb[g]) + _dot_nt(xdec[:, sl], dstb[:, sl]))
            ds_prev.append(_dot_tn(cmb[g], dyeb[:, sl]))
        dst_ref[i] = jnp.concatenate(ds_prev, axis=1) + dst * f["cde"]
        dcs = dcs + dcst.T
        da = _dot_hi(lt_ref[...].T, dcs)
        dxdt = jnp.concatenate(dxdt, axis=1)
        a_neg = f["a_neg"]
        ddt = da * a_neg + _dot_01(dxdt * xs, ext)
        dal_ref[...] += jnp.sum(da * dt, axis=0, keepdims=True) * a_neg
        dxs = dxs + dxdt * f["dte"]
        ddtraw = jnp.where(lane < SSD_HEADS, ddt * _sigmoid(f["dtpre"]), 0.0)
        ddtb_ref[...] += jnp.sum(ddtraw, axis=0, keepdims=True)
        ddt_ref[i] = ddtraw.astype(MXU)
        dxa = jnp.concatenate([dxs, dbm[0], dbm[1], dcm[0], dcm[1]], axis=1)
        sig_c = f["sig_c"]
        dconv = dxa * (sig_c + f["xa"] * (1.0 - sig_c))
        dcb_ref[...] += jnp.sum(dconv, axis=0, keepdims=True)
        xbc = xbc_ref[i]
        dcw_ref[3:4, :] += jnp.sum(dconv * xbc, axis=0, keepdims=True)
        dxbc = cw[3:4] * dconv
        for j, up in zip((1, 2, 3), _shifts_up(dconv, dnext_ref[i])):
            dcw_ref[3 - j:4 - j, :] += jnp.sum(up * xbc, axis=0, keepdims=True)
            dxbc = dxbc + cw[3 - j:4 - j] * up
        dnext_ref[i] = dconv[0:8]
        dssd_ref[i, :, 0:1024] = dz.astype(MXU)
        dssd_ref[i, :, 1024:2560] = dxbc.astype(MXU)

    dssd, ddt, *small = pl.pallas_call(
        _after(14, body), grid=(nb // ns, nch), name="ssd_bwd",
        in_specs=in_specs,
        out_specs=[seq(2560), seq(128),
                   _const((8, CONV_CH)), _const((1, CONV_CH)), _const((1, 128)), _const((1, 128)), _const((1, 128)),
                   _const((1, 1024))],
        out_shape=[_sds((nb, S, 2560), MXU), _sds((nb, S, 128), MXU), _sds((8, CONV_CH), F32),
                   _sds((1, CONV_CH), F32), _sds((1, 128), F32), _sds((1, 128), F32), _sds((1, 128), F32),
                   _sds((1, 1024), F32)],
        scratch_shapes=[pltpu.VMEM((ns, 128, 1024), F32), pltpu.VMEM((ns, 8, CONV_CH), F32),
                        pltpu.VMEM((1, 1024), F32)],
        compiler_params=_cp(2),
    )(pz.reshape(nb, S, 1024), pxbc.reshape(nb, S, CONV_CH), conv.reshape(nb, S, CONV_CH), dtraw.reshape(nb, S, 128),
      convw, dtb, alog, dskip, ng, ex, ltri, ext, sall, dcat.reshape(nb, S, 2048), after)
    return (dssd.reshape(T, 2560), ddt.reshape(T, 128), *small)


def _inproj_bwd(dxn_uv, dssd, ddt, wm, wdt, dh1, x, g, tm, after):
    T = x.shape[0]

    def body(dxnuv_ref, dssd_ref, ddt_ref, wm_ref, wdt_ref, dh1_ref, x_ref, g_ref, dx_ref, dg_ref):
        @pl.when(pl.program_id(0) == 0)
        def _():
            dg_ref[...] = jnp.zeros_like(dg_ref)

        dxn = (dxnuv_ref[...] + _dot_nt(dssd_ref[...], wm_ref[:, 2048:N_MAIN])
               + _dot_nt(ddt_ref[...], wdt_ref[...]))
        xh, r = _rms(x_ref[...])
        dg_ref[...] += jnp.sum(dxn * xh, axis=0, keepdims=True)
        dx_ref[...] = dh1_ref[...] + _rms_bwd(dxn, xh, r, g_ref[...])

    return pl.pallas_call(
        _after(8, body), grid=(T // tm,), name="inproj_bwd",
        in_specs=[_rows(tm, D), _rows(tm, 2560), _rows(tm, 128), _const((D, N_MAIN)), _const((D, 128)),
                  _rows(tm, D), _rows(tm, D), _const((1, D)), _ANY],
        out_specs=[_rows(tm, D), _const((1, D))],
        out_shape=[_sds((T, D), F32), _sds((1, D), F32)],
        compiler_params=_cp(),
    )(dxn_uv, dssd, ddt, wm, wdt, dh1, x, g, after)


def _matmul_tn(a, b, name, a_fn=None):
    T, M = a.shape
    N = b.shape[1]
    tm = min(M, 1024)
    tn = 1280 if N == 2560 else min(N, 1024)
    tk = min(T, 2048)

    def body(a_ref, b_ref, o_ref, acc_ref):
        k = pl.program_id(2)

        @pl.when(k == 0)
        def _():
            acc_ref[...] = jnp.zeros_like(acc_ref)

        av = a_ref[...]
        if a_fn is not None:
            av = a_fn(av)
        acc_ref[...] += _dot_tn(av, b_ref[...])

        @pl.when(k == T // tk - 1)
        def _():
            o_ref[...] = acc_ref[...].astype(o_ref.dtype)

    return pl.pallas_call(
        body, grid=(M // tm, N // tn, T // tk), name=name,
        in_specs=[pl.BlockSpec((tk, tm), lambda i, j, k: (k, i)), pl.BlockSpec((tk, tn), lambda i, j, k: (k, j))],
        out_specs=pl.BlockSpec((tm, tn), lambda i, j, k: (i, j)),
        out_shape=_sds((M, N), GRAD),
        scratch_shapes=[pltpu.VMEM((tm, tn), F32)],
        compiler_params=_cp(3),
    )(a, b)


def _adamw_vals(w, g, m, v):
    m = B1 * m + (1.0 - B1) * g
    v = B2 * v + (1.0 - B2) * (g * g)
    m_hat = m / (1.0 - B1 ** STEP)
    v_hat = v / (1.0 - B2 ** STEP)
    return -LR * (m_hat / (jnp.sqrt(v_hat) + ADAM_EPS) + WD * w), m, v


def _adamw(w, g, m, v, name):
    R, C = w.shape
    tr = 256 if R % 256 == 0 else R

    def body(w_ref, g_ref, m_ref, v_ref, d_ref, mo_ref, vo_ref):
        d_ref[...], mo_ref[...], vo_ref[...] = _adamw_vals(w_ref[...], g_ref[...], m_ref[...], v_ref[...])

    spec = _rows(tr, C)
    return pl.pallas_call(
        body, grid=(R // tr,), name=name,
        in_specs=[spec] * 4, out_specs=[spec] * 3, out_shape=[_sds((R, C), F32)] * 3,
        compiler_params=_cp(),
    )(w, g, m, v)


def _adamw_halves(w, own, other, m, v, name):
    R, C = w.shape
    half = R // 2
    tr = min(half, 256)
    nth = half // tr

    def body(w_ref, own_ref, oth_ref, m_ref, v_ref, g_ref, d_ref, mo_ref, vo_ref):
        mine = (pl.program_id(0) // nth) == lax.axis_index("c")
        g = jnp.where(mine, own_ref[...], oth_ref[...])
        g_ref[...] = g
        d_ref[...], mo_ref[...], vo_ref[...] = _adamw_vals(w_ref[...], g, m_ref[...], v_ref[...])

    full = _rows(tr, C)
    part = pl.BlockSpec((tr, C), lambda i: (i % nth, 0))
    return pl.pallas_call(
        body, grid=(R // tr,), name=name,
        in_specs=[full, part, part, full, full], out_specs=[full] * 4, out_shape=[_sds((R, C), F32)] * 4,
        compiler_params=_cp(),
    )(w, own, other, m, v)


def _sum_small(slots, name):
    nd, rows, C = slots.shape

    def body(s_ref, o_ref):
        acc = s_ref[0]
        for d in range(1, nd):
            acc = acc + s_ref[d]
        o_ref[...] = acc

    return pl.pallas_call(
        body, grid=(1,), name=name,
        in_specs=[_const((nd, rows, C))], out_specs=_const((rows, C)), out_shape=_sds((rows, C), F32),
        compiler_params=_cp(),
    )(slots)


def _sum_slots(slots, src, kind, shp, kh, name):
    R, C = shp
    rh = R // 2
    tr = min(rh, 256)
    nth = rh // tr
    if kind == "slab":
        src_spec = pl.BlockSpec((1, tr, C), lambda i, kh: (kh[0], kh[1] * nth + i, 0))
    elif kind == "rows":
        src_spec = pl.BlockSpec((tr, C), lambda i, kh: (kh[0] * (R // tr) + kh[1] * nth + i, 0))
    else:
        src_spec = pl.BlockSpec((tr, C), lambda i, kh: (kh[1] * nth + i, kh[0]))

    def body(kh_ref, s_ref, own_ref, o_ref):
        me = 2 * kh_ref[0] + kh_ref[1]
        acc = (own_ref[0] if kind == "slab" else own_ref[...]).astype(F32)
        for k in range(1, 8):
            acc = acc + s_ref[me ^ k].astype(F32)
        o_ref[...] = acc

    return pl.pallas_call(
        body, name=name,
        grid_spec=pltpu.PrefetchScalarGridSpec(
            num_scalar_prefetch=1, grid=(nth,),
            in_specs=[pl.BlockSpec((8, tr, C), lambda i, kh: (0, i, 0)), src_spec],
            out_specs=pl.BlockSpec((tr, C), lambda i, kh: (i, 0))),
        out_shape=_sds((rh, C), F32),
        compiler_params=_cp(),
    )(kh, slots, src)


def _assemble_w_in(slabs):
    tr = 256

    def body(s_ref, wm_ref, wdt_ref):
        full = jnp.concatenate([s_ref[k] for k in range(4)], axis=1)
        wm_ref[...] = full[:, :N_MAIN]
        wdt_ref[...] = jnp.concatenate([full[:, N_MAIN:], jnp.zeros((tr, 128 - 16), full.dtype)], axis=1)

    return pl.pallas_call(
        body, grid=(D // tr,), name="assemble_w_in",
        in_specs=[pl.BlockSpec((4, tr, 1156), lambda i: (0, i, 0))],
        out_specs=[_rows(tr, N_MAIN), _rows(tr, 128)],
        out_shape=[_sds((D, N_MAIN), slabs.dtype), _sds((D, 128), slabs.dtype)],
        compiler_params=_cp(),
    )(slabs)


def _split_dw_in(d_uv, d_ssd, d_dt):
    tr = 256

    def body(uv_ref, ssd_ref, dt_ref, o_ref):
        full = jnp.concatenate([uv_ref[...], ssd_ref[...], dt_ref[:, 0:16]], axis=1)
        for k in range(4):
            o_ref[k] = full[:, 1156 * k:1156 * (k + 1)]

    return pl.pallas_call(
        body, grid=(D // tr,), name="split_dw_in",
        in_specs=[_rows(tr, 2048), _rows(tr, 2560), _rows(tr, 128)],
        out_specs=pl.BlockSpec((4, tr, 1156), lambda i: (0, i, 0)),
        out_shape=_sds((4, D, 1156), d_uv.dtype),
        compiler_params=_cp(),
    )(d_uv, d_ssd, d_dt)


def _cast_into_slot(w, kh, name):
    R, C = w.shape
    tr = 256

    def body(kh_ref, w_ref, o_ref):
        o_ref[0] = w_ref[...].astype(BF16)

    return pl.pallas_call(
        body, name=name,
        grid_spec=pltpu.PrefetchScalarGridSpec(
            num_scalar_prefetch=1, grid=(R // tr,),
            in_specs=[pl.BlockSpec((tr, C), lambda i, kh: (i, 0))],
            out_specs=pl.BlockSpec((1, tr, C), lambda i, kh: (kh[0], i, 0))),
        out_shape=_sds((4, R, C), BF16),
        compiler_params=_cp(),
    )(kh, w)


_ANY = pl.BlockSpec(memory_space=pl.ANY)
_CHIP_FLIPS = [(1, 0), (0, 1), (1, 1)]
_DEVICE_FLIPS = [(fx, fy, fc) for fx in (0, 1) for fy in (0, 1) for fc in (0, 1)][1:]


def _half(h, rows):
    return pl.ds(pl.multiple_of(h * rows, rows), rows)


def _remote(src, dst, ssem, rsem, to):
    return pltpu.make_async_remote_copy(src_ref=src, dst_ref=dst, send_sem=ssem, recv_sem=rsem,
                                        device_id=to, device_id_type=MESH)


def _weight_gather(bufs, conv):
    n = len(bufs)

    def body(*refs):
        conv_ref, outs, conv_out = refs[n], refs[n + 1:2 * n + 1], refs[2 * n + 1]
        send_sems, recv_sems, fsend_sems, frecv_sems, csend_sems, crecv_sems, local_sem = refs[2 * n + 2:]
        x, y, c = lax.axis_index("x"), lax.axis_index("y"), lax.axis_index("c")
        me = 2 * x + y
        halves = [_half(c, r.shape[1] // 2) for r in outs]
        others = [_half(1 - c, r.shape[1] // 2) for r in outs]
        remote = _remote
        local = [pltpu.make_async_copy(conv_ref, conv_out.at[me], local_sem)]
        for cp in local:
            cp.start()
        sends = []
        for k, (fx, fy) in enumerate(_CHIP_FLIPS):
            peer = (x ^ fx, y ^ fy, c)
            for i in range(n):
                mine = outs[i].at[me, halves[i]]
                sends.append(remote(mine, mine, send_sems.at[k * n + i], recv_sems.at[k * n + i], peer))
            sends.append(remote(conv_ref, conv_out.at[me], csend_sems.at[k], crecv_sems.at[k], peer))
        for cp in sends:
            cp.start()
        sibling = (x, y, 1 - c)
        forwards = []
        for k, (fx, fy) in enumerate(_CHIP_FLIPS):
            peer = (x ^ fx, y ^ fy, c)
            src = 2 * (x ^ fx) + (y ^ fy)
            for i in range(n):
                landed = outs[i].at[src, halves[i]]
                remote(landed, landed, send_sems.at[k * n + i], recv_sems.at[k * n + i], peer).wait_recv()
                fw = remote(landed, landed, fsend_sems.at[k * n + i], frecv_sems.at[k * n + i], sibling)
                fw.start()
                forwards.append(fw)
            remote(conv_out.at[src], conv_out.at[src], csend_sems.at[k], crecv_sems.at[k], peer).wait_recv()
        for k, (fx, fy) in enumerate(_CHIP_FLIPS):
            src = 2 * (x ^ fx) + (y ^ fy)
            for i in range(n):
                theirs = outs[i].at[src, others[i]]
                remote(theirs, theirs, fsend_sems.at[k * n + i], frecv_sems.at[k * n + i], sibling).wait_recv()
        for cp in sends + forwards:
            cp.wait_send()
        for cp in local:
            cp.wait()

    dma = pltpu.SemaphoreType.DMA
    return pl.pallas_call(
        body, name="weight_gather",
        in_specs=[_ANY] * (n + 1), out_specs=[_ANY] * (n + 1),
        out_shape=[_sds(b.shape, b.dtype) for b in bufs] + [_sds((4,) + conv.shape, conv.dtype)],
        input_output_aliases={i: i for i in range(n)},
        scratch_shapes=[dma((3 * n,)), dma((3 * n,)), dma((3 * n,)), dma((3 * n,)), dma((3,)), dma((3,)), dma],
    )(*bufs, conv)


def _piece(ref, kind, R, C, k, h):
    if kind == "slab":
        return ref.at[k, _half(h, R // 2), :]
    if kind == "rows":
        return ref.at[pl.ds(pl.multiple_of(k * R + h * (R // 2), R // 2), R // 2), :]
    return ref.at[_half(h, R // 2), pl.ds(pl.multiple_of(k * C, C), C)]


def _small_exchange(small, after):
    rs = small.shape[0]

    def body(s_ref, after_ref, out_ref, send_sems, recv_sems, local_sem):
        del after_ref
        x, y, c = lax.axis_index("x"), lax.axis_index("y"), lax.axis_index("c")
        slot = 4 * x + 2 * y + c
        own = pltpu.make_async_copy(s_ref, out_ref.at[slot], local_sem)
        own.start()
        copies = []
        for k, (fx, fy, fc) in enumerate(_DEVICE_FLIPS):
            copies.append(_remote(s_ref, out_ref.at[slot], send_sems.at[k], recv_sems.at[k], (x ^ fx, y ^ fy, c ^ fc)))
        for cp in copies:
            cp.start()
        for k, (fx, fy, fc) in enumerate(_DEVICE_FLIPS):
            theirs = out_ref.at[slot ^ (k + 1)]
            _remote(theirs, theirs, send_sems.at[k], recv_sems.at[k], (x ^ fx, y ^ fy, c ^ fc)).wait_recv()
        for cp in copies:
            cp.wait_send()
        own.wait()

    dma = pltpu.SemaphoreType.DMA
    return pl.pallas_call(
        body, name="small_exchange",
        in_specs=[_ANY, _ANY], out_specs=_ANY, out_shape=_sds((8, rs, 128), F32),
        scratch_shapes=[dma((7,)), dma((7,)), dma],
    )(small, after)


_HBM = pl.BlockSpec(memory_space=pltpu.HBM)
_SEM = pl.BlockSpec(memory_space=pltpu.SEMAPHORE)


def _split_start(name, arrays, n_copies, plan, after=None):
    n = len(arrays)
    extra = [] if after is None else [after]

    def body(*refs):
        m = n + len(extra)
        arrs, send_sems, recv_sems, token = refs[:n], refs[m], refs[m + 1], refs[-1]
        for j, (src, dst, peer) in enumerate(plan(arrs)):
            _remote(src, dst, send_sems.at[j], recv_sems.at[j], peer).start()
        token[...] = jnp.zeros_like(token)

    dma = pltpu.SemaphoreType.DMA
    res = pl.pallas_call(
        body, name=name,
        out_shape=(dma((n_copies,)), dma((n_copies,)), *[pltpu.HBM(a.shape, a.dtype) for a in arrays],
                   _sds((8, 128), F32)),
        in_specs=[_HBM] * n + [_ANY] * len(extra),
        out_specs=(_SEM, _SEM, *[_HBM] * n, pl.BlockSpec(memory_space=pltpu.VMEM)),
        input_output_aliases={i: 2 + i for i in range(n)},
        compiler_params=pltpu.CompilerParams(has_side_effects=pltpu.SideEffectType.DATAFLOW_SIDE_EFFECTING),
    )(*[pltpu.with_memory_space_constraint(a, pltpu.HBM) for a in arrays], *extra)
    return res[0], res[1], list(res[2:2 + n]), res[-1]


def _split_wait(name, arrays, send_sems, recv_sems, plan, after):
    n = len(arrays)

    def body(*refs):
        arrs, ssems, rsems = refs[:n], refs[n], refs[n + 1]
        for j, (src, dst, peer) in enumerate(plan(arrs)):
            cp = _remote(src, dst, ssems.at[j], rsems.at[j], peer)
            cp.wait_send()
            cp.wait_recv()

    return list(pl.pallas_call(
        body, name=name,
        out_shape=tuple(pltpu.HBM(a.shape, a.dtype) for a in arrays),
        in_specs=[_HBM] * n + [_SEM, _SEM, _ANY],
        out_specs=tuple([_HBM] * n),
        input_output_aliases={i: i for i in range(n)},
        compiler_params=pltpu.CompilerParams(has_side_effects=pltpu.SideEffectType.DATAFLOW_SIDE_EFFECTING),
    )(*arrays, send_sems, recv_sems, after))


def _gather_plan(n):
    def plan(bufs):
        x, y, c = lax.axis_index("x"), lax.axis_index("y"), lax.axis_index("c")
        me = 2 * x + y
        return [(bufs[i].at[me], bufs[i].at[me], (x ^ fx, y ^ fy, c)) for fx, fy in _CHIP_FLIPS for i in range(n)]

    return plan


def _reduce_plan(specs, n_small):
    n = len(specs)

    def plan(arrs):
        x, y, c = lax.axis_index("x"), lax.axis_index("y"), lax.axis_index("c")
        slot = 4 * x + 2 * y + c
        out = []
        for fx, fy, fc in _DEVICE_FLIPS:
            peer = (x ^ fx, y ^ fy, c ^ fc)
            for i, (kind, (R, C)) in enumerate(specs):
                out.append((_piece(arrs[i], kind, R, C, 2 * peer[0] + peer[1], peer[2]), arrs[n + i].at[slot], peer))
            for s in range(n_small):
                out.append((arrs[2 * n + 2 * s], arrs[2 * n + 2 * s + 1].at[slot], peer))
        return out

    return plan


def _sibling_exchange(halves, name):
    n = len(halves)

    def body(*refs):
        ins, outs, send_sems, recv_sems = refs[:n], refs[n:2 * n], refs[2 * n], refs[2 * n + 1]
        sibling = (lax.axis_index("x"), lax.axis_index("y"), 1 - lax.axis_index("c"))
        copies = [pltpu.make_async_remote_copy(src_ref=ins[i], dst_ref=outs[i], send_sem=send_sems.at[i],
                                               recv_sem=recv_sems.at[i], device_id=sibling, device_id_type=MESH)
                  for i in range(n)]
        for cp in copies:
            cp.start()
        for cp in copies:
            cp.wait()

    dma = pltpu.SemaphoreType.DMA
    return pl.pallas_call(
        body, name=name,
        in_specs=[_ANY] * n, out_specs=[_ANY] * n,
        out_shape=[_sds(h.shape, h.dtype) for h in halves],
        scratch_shapes=[dma((n,)), dma((n,))],
    )(*halves)


_BIG = [("w_in", (1024, 1156), "slab"), ("w_out", (512, 1024), "rows"), ("w_ff1", (1024, 1024), "cols"),
        ("w_ff2", (1024, 1024), "rows"), ("w_ple_gate", (256, 1024), "rows"), ("w_ple_proj", (256, 256), "cols")]
_SMALL = [("norm_mix_g", (1, 1024)), ("gm_v_norm_g", (1, 1024)), ("gm_ws", (1, 8, 128, 128)), ("gm_bs", (1, 8, 128)),
          ("gm_out_norm_g", (1, 1024)), ("ssd_conv_w", (1, 4, 1536)), ("ssd_conv_b", (1, 1536)),
          ("ssd_dt_bias", (1, 16)), ("ssd_a_log", (1, 16)), ("ssd_d", (1, 16)), ("ssd_norm_g", (1, 1024)),
          ("norm_mlp_g", (1, 1024)), ("ple_norm_g", (1, 1024)), ("final_norm_g", (1024,))]


def _rows128(a):
    flat = a.reshape(-1)
    rows = -(-flat.shape[0] // 1024) * 8
    return jnp.pad(flat, (0, rows * 128 - flat.shape[0])).reshape(rows, 128)


def _pad_lanes(v, n=128):
    v = v.reshape(1, -1)
    return jnp.pad(v, ((0, 0), (0, n - v.shape[1])))


_SMALL_SHAPES = dict(_SMALL + [("loss", ())])
_BIG_SPECS = {n: (kind, shp) for n, shp, kind in _BIG}


class _Comm:
    def __init__(self, a, kh):
        self.a, self.kh = a, kh
        self.bufs = {n: _cast_into_slot(a[n].reshape(shp), kh, "cast_" + n) for n, shp, _ in _BIG}
        self.sent = []
        self.small_tot = {}

    def w_in(self):
        (g_win,), g_cw = self._gather_now()
        rest = [self.bufs[n] for n, _, _ in _BIG[1:]]
        plan = _gather_plan(len(rest))
        ssem, rsem, thru, token = _split_start("gather_start", rest, 3 * len(rest), plan, after=g_cw)
        self.gather = (plan, ssem, rsem, thru)
        wm, wdt = _assemble_w_in(g_win)
        return wm, wdt, jnp.concatenate([g_cw[k] for k in range(4)], axis=1), token

    def _gather_now(self):
        *bufs, g_cw = _weight_gather([self.bufs["w_in"]], self.a["ssd_conv_w"].reshape(4, 384))
        return bufs, g_cw

    def rest(self, after):
        plan, ssem, rsem, thru = self.gather
        g_wo, g_w1, g_w2, g_wg, g_wp = _split_wait("gather_wait", thru, ssem, rsem, plan, after)
        return g_wo.reshape(2048, D), g_w1, g_w2.reshape(DFF, D), g_wg.reshape(D, D), g_wp

    def send(self, tag, grads):
        big = [n for n, _, _ in _BIG if n in grads]
        small = [n for n in _SMALL_SHAPES if n in grads]
        parts = [_rows128(grads[n]) for n in small]
        rows = [s.shape[0] for s in parts]
        if not big:
            self.last_small = (tag, small, rows, jnp.concatenate(parts, axis=0))
            return None
        srcs = [grads[n] for n in big]
        lands = [lax.empty((8, _BIG_SPECS[n][1][0] // 2, _BIG_SPECS[n][1][1]), GRAD) for n in big]
        extra = []
        if small:
            pack = jnp.concatenate(parts, axis=0)
            extra = [pack, jnp.broadcast_to(pack, (8,) + pack.shape)]
        plan = _reduce_plan([_BIG_SPECS[n] for n in big], len(extra) // 2)
        n_copies = 7 * (len(big) + len(extra) // 2)
        ssem, rsem, thru, token = _split_start("reduce_start_" + tag, srcs + lands + extra, n_copies, plan)
        self.sent.append((tag, big, small, rows, plan, ssem, rsem, thru))
        return token

    def _unpack(self, tot, names, rows):
        o = 0
        for n, r in zip(names, rows):
            shp = _SMALL_SHAPES[n]
            cnt = 1
            for s in shp:
                cnt *= s
            self.small_tot[n] = tot[o:o + r].reshape(-1)[:cnt].reshape(shp)
            o += r

    def finish(self, after):
        a, results = self.a, {}

        def update(names, own, tag):
            other = _sibling_exchange([own[n] for n in names], "sibling_exchange_" + tag)
            for n, oth in zip(names, other):
                shp = _BIG_SPECS[n][1]
                results[n] = _adamw_halves(a[n].reshape(shp), own[n], oth, a["m_" + n].reshape(shp),
                                           a["v_" + n].reshape(shp), "adamw_" + n)
            return results[names[-1]][1]

        own, early = {}, []
        for tag, big, small, rows, plan, ssem, rsem, thru in self.sent:
            if tag == self.sent[-1][0]:
                after = update(early, own, "early")
            arrs = _split_wait("reduce_wait_" + tag, thru, ssem, rsem, plan, after)
            nb_ = len(big)
            for i, n in enumerate(big):
                kind, shp = _BIG_SPECS[n]
                own[n] = _sum_slots(arrs[nb_ + i], arrs[i], kind, shp, self.kh, "sum_" + n)
                after = own[n]
            early += big
            if small:
                self._unpack(_sum_small(arrs[2 * nb_ + 1], "sum_small_" + tag), small, rows)
        after = update(self.sent[-1][1], own, "late")
        tag, small, rows, pack = self.last_small
        self._unpack(_sum_small(_small_exchange(pack, after), "sum_small_" + tag), small, rows)
        return results, dict(self.small_tot)


def _local_step(x, p, tgt, sm, comm, nb, tm):
    T = x.shape[0]
    wm, wdt, conv_w, token = comm.w_in()
    g_mix, gv, gout = sm["norm_mix_g"].reshape(1, D), sm["gm_v_norm_g"].reshape(1, D), sm["gm_out_norm_g"].reshape(1, D)
    ws = sm["gm_ws"].reshape(GM_HEADS, CH, CH)
    bst = jnp.pad(sm["gm_bs"].reshape(GM_HEADS, CH).T, ((0, 0), (0, 128 - GM_HEADS)))
    convw = jnp.pad(conv_w, ((0, 4), (0, 0)))
    convb = sm["ssd_conv_b"].reshape(1, CONV_CH)
    dtb, alog = _pad_lanes(sm["ssd_dt_bias"]), _pad_lanes(sm["ssd_a_log"])
    dskip = jnp.repeat(sm["ssd_d"].reshape(SSD_HEADS), SSD_P).reshape(1, 1024)
    ng, g_mlp, g_ple = sm["ssd_norm_g"].reshape(1, D), sm["norm_mlp_g"].reshape(1, D), sm["ple_norm_g"].reshape(1, D)
    gf = sm["final_norm_g"].reshape(1, D)
    head_of_lane = lax.broadcasted_iota(jnp.int32, (128, 1024), 1) // SSD_P
    ex = (lax.broadcasted_iota(jnp.int32, (128, 1024), 0) == head_of_lane).astype(BF16)
    ext = ex.T
    ltri = (lax.broadcasted_iota(jnp.int32, (CH, CH), 0) >= lax.broadcasted_iota(jnp.int32, (CH, CH), 1)).astype(F32)

    pz, pxbc, dtraw, xn, cat, uv = _inproj_gmlp(x, g_mix, wm, wdt, gv, ws, bst, gout, tm, token)
    cat, sall, conv = _ssd_fwd(pz, pxbc, dtraw, cat, convw, convb, dtb, alog, dskip, ng, ex, ltri, nb)
    wo, w1, w2, wg, wp = comm.rest(cat)
    h1, hn = _outproj(cat, wo, x, g_mlp, tm)
    hid = _ff1(hn, w1, min(T, 2 * tm))
    hp, dgl, dpe, dh2, dh2b, loss, d_gf, d_gple = _ff2_tail(hid, w2, h1, g_ple, p, tgt, wg, wp, gf, tm)

    d_wp = _matmul_tn(p, dpe, "dw_ple_proj", a_fn=lambda a: a.astype(MXU))
    d_wg = _matmul_tn(hp, dgl, "dw_ple_gate")
    d_w2 = _matmul_tn(hid, dh2b, "dw_ff2", a_fn=_sq)
    dpre = _ff2_bwd(dh2b, w2, hid, min(T, 2 * tm))
    d_w1 = _matmul_tn(hn, dpre, "dw_ff1")
    token = comm.send("a", {"w_ple_proj": d_wp, "w_ple_gate": d_wg, "w_ff2": d_w2, "w_ff1": d_w1})
    dh1, dh1b, d_gmlp = _ff1_bwd(dpre, w1, dh2, h1, g_mlp, tm, token)
    dcat = _outproj_bwd(dh1b, wo, min(T, 2 * tm))
    d_wo = _matmul_tn(cat, dh1b, "dw_out")
    duv, d_gv, d_ws, d_bst, d_gout, dxn_uv = _gmlp_bwd(uv, dcat, gv, ws, bst, gout, wm)
    token = comm.send("b", {
        "w_out": d_wo, "loss": loss[0:1, 0:1], "final_norm_g": d_gf, "ple_norm_g": d_gple, "norm_mlp_g": d_gmlp,
        "gm_v_norm_g": d_gv, "gm_ws": d_ws, "gm_bs": d_bst[:, :GM_HEADS].T, "gm_out_norm_g": d_gout})
    dssd, ddt, d_cw, d_cb, d_dtb, d_al, d_ds, d_ng = _ssd_bwd(
        pz, pxbc, conv, dtraw, sall, dcat, convw, dtb, alog, dskip, ng, ex, ltri, ext, nb, token)
    d_win = _split_dw_in(_matmul_tn(xn, duv, "dw_in_uv"), _matmul_tn(xn, dssd, "dw_in_ssd"),
                         _matmul_tn(xn, ddt, "dw_in_dt"))
    token = comm.send("c", {"w_in": d_win})
    dx, d_gmix = _inproj_bwd(dxn_uv, dssd, ddt, wm, wdt, dh1, x, g_mix, tm, token)
    comm.send("d", {"norm_mix_g": d_gmix, "ssd_conv_w": d_cw[0:4], "ssd_conv_b": d_cb, "ssd_dt_bias": d_dtb[:, :16],
                    "ssd_a_log": d_al[:, :16], "ssd_d": d_ds[:, :16], "ssd_norm_g": d_ng})
    return dx


def kernel(x, p, norm_mix_g, w_in, gm_v_norm_g, gm_ws, gm_bs, gm_out_norm_g, ssd_conv_w, ssd_conv_b, ssd_dt_bias, ssd_a_log, ssd_d, ssd_norm_g, w_out, norm_mlp_g, w_ff1, w_ff2, ple_norm_g, w_ple_gate, w_ple_proj, final_norm_g, loss_target, m_norm_mix_g, m_w_in, m_gm_v_norm_g, m_gm_ws, m_gm_bs, m_gm_out_norm_g, m_ssd_conv_w, m_ssd_conv_b, m_ssd_dt_bias, m_ssd_a_log, m_ssd_d, m_ssd_norm_g, m_w_out, m_norm_mlp_g, m_w_ff1, m_w_ff2, m_ple_norm_g, m_w_ple_gate, m_w_ple_proj, m_final_norm_g, v_norm_mix_g, v_w_in, v_gm_v_norm_g, v_gm_ws, v_gm_bs, v_gm_out_norm_g, v_ssd_conv_w, v_ssd_conv_b, v_ssd_dt_bias, v_ssd_a_log, v_ssd_d, v_ssd_norm_g, v_w_out, v_norm_mlp_g, v_w_ff1, v_w_ff2, v_ple_norm_g, v_w_ple_gate, v_w_ple_proj, v_final_norm_g):
    a = dict(locals())
    order = ["norm_mix_g", "w_in", "gm_v_norm_g", "gm_ws", "gm_bs", "gm_out_norm_g", "ssd_conv_w", "ssd_conv_b",
             "ssd_dt_bias", "ssd_a_log", "ssd_d", "ssd_norm_g", "w_out", "norm_mlp_g", "w_ff1", "w_ff2", "ple_norm_g",
             "w_ple_gate", "w_ple_proj", "final_norm_g"]
    chip = 2 * lax.axis_index("x") + lax.axis_index("y")
    nb, S = x.shape[0], x.shape[1]
    T = nb * S
    sm = {n: a[n] for n, _ in _SMALL if n != "ssd_conv_w"}
    comm = _Comm(a, jnp.stack([chip, lax.axis_index("c")]).astype(jnp.int32))
    dx = _local_step(x.reshape(T, D), p.reshape(T, DPLE), loss_target.reshape(T, D), sm, comm, nb, 512)
    big, g_out = comm.finish(dx)
    delta, new_m, new_v = {}, {}, {}
    for n, _, _ in _BIG:
        g_out[n], delta[n], new_m[n], new_v[n] = (r.reshape(a[n].shape) for r in big[n])
    g_out["ssd_conv_w"] = lax.dynamic_slice(g_out["ssd_conv_w"], (0, 0, chip * 384), (1, 4, 384))
    small_names = [n for n, _ in _SMALL]
    packs = [jnp.concatenate([_rows128(src(n)) for n in small_names], axis=0)
             for src in (lambda n: a[n], lambda n: g_out[n], lambda n: a["m_" + n], lambda n: a["v_" + n])]
    outs = _adamw(*packs, "adamw_small")
    o = 0
    for n in small_names:
        r = _rows128(a[n]).shape[0]
        cnt = a[n].size
        for dst, src in zip((delta, new_m, new_v), outs):
            dst[n] = src[o:o + r].reshape(-1)[:cnt].reshape(a[n].shape)
        o += r
    return (g_out["loss"], dx.reshape(x.shape), *[g_out[n] for n in order], *[delta[n] for n in order],
            *[new_m[n] for n in order], *[new_v[n] for n in order])
```

```python
import jax
import jax.numpy as jnp
from jax import lax
from jax.experimental import pallas as pl
from jax.experimental.pallas import tpu as pltpu

F32 = jnp.float32
BF16 = jnp.bfloat16
MXU = jnp.bfloat16
GRAD = jnp.bfloat16

D = 1024
CH = 128
GM_HEADS = 8
SSD_HEADS = 16
SSD_P = 64
CONV_CH = 1536
N_MAIN = 4608
DFF = 4096
DPLE = 256
EPS = 1e-6
NEG = -1e30

LR, B1, B2, ADAM_EPS, WD, STEP = 0.001, 0.9, 0.999, 1e-08, 0.01, 10

VMEM_LIMIT = 56 * 1024 * 1024
_SEQS_PER_STEP = 2
MESH = pl.DeviceIdType.MESH

INV_SQRT2 = 0.7071067811865476
INV_SQRT_2PI = 0.3989422804014327


def _cp(n_axes=1):
    return pltpu.CompilerParams(dimension_semantics=("arbitrary",) * n_axes, vmem_limit_bytes=VMEM_LIMIT)


def _dot(a, b):
    return jnp.dot(a, b, preferred_element_type=F32)


def _dot_nt(a, b):
    return lax.dot_general(a, b, (((1,), (1,)), ((), ())), preferred_element_type=F32)


def _dot_tn(a, b):
    return lax.dot_general(a, b, (((0,), (0,)), ((), ())), preferred_element_type=F32)


def _dot_hi(a, b):
    return jnp.dot(a, b, preferred_element_type=F32, precision=lax.Precision.HIGHEST)


def _dot_01(a, sel):
    hi = a.astype(BF16)
    lo = (a - hi.astype(F32)).astype(BF16)
    n = a.shape[0]
    r = _dot(jnp.concatenate([hi, lo], axis=0), sel)
    return r[0:n] + r[n:2 * n]


def _rows(tm, n, j=0):
    return pl.BlockSpec((tm, n), lambda i: (i, j))


def _const(shape):
    nd = len(shape)
    return pl.BlockSpec(shape, lambda *_: (0,) * nd)


def _sds(shape, dtype):
    return jax.ShapeDtypeStruct(shape, dtype)


def _rms(x):
    r = lax.rsqrt(jnp.mean(x * x, axis=-1, keepdims=True) + EPS)
    return x * r, r


def _rms_bwd(dy, xhat, r, g):
    dyg = dy * g
    return r * (dyg - xhat * jnp.mean(dyg * xhat, axis=-1, keepdims=True))


def _sigmoid(x):
    return 1.0 / (1.0 + jnp.exp(-x))


def _gelu(x):
    cdf = 0.5 * (1.0 + lax.erf(x * INV_SQRT2))
    pdf = jnp.exp(-0.5 * x * x) * INV_SQRT_2PI
    return x * cdf, cdf + x * pdf


def _softplus(x):
    e = jnp.exp(-jnp.abs(x))
    u = 1.0 + e
    log1p = jnp.where(u == 1.0, e, jnp.log(u) * e / (u - 1.0))
    return jnp.maximum(x, 0.0) + log1p


def _after(n_in, fn):
    def body(*refs):
        return fn(*refs[:n_in], *refs[n_in + 1:])

    return body


def _inproj_gmlp(x, g, wm, wdt, gv, ws, bst, gout, tm, after):
    T = x.shape[0]

    def body(x_ref, g_ref, wm_ref, wdt_ref, gv_ref, ws_ref, bst_ref, gout_ref,
             z_ref, xbc_ref, dt_ref, xn_ref, ya_ref, uv_ref):
        xh, _ = _rms(x_ref[...])
        xn = (xh * g_ref[...]).astype(MXU)
        xn_ref[...] = xn
        for n in range(4):
            uv_ref[:, n * 512:(n + 1) * 512] = _dot(xn, wm_ref[:, n * 512:(n + 1) * 512])
        for n in range(2):
            z_ref[:, n * 512:(n + 1) * 512] = _dot(xn, wm_ref[:, 2048 + n * 512:2048 + (n + 1) * 512])
        for n in range(3):
            xbc_ref[:, n * 512:(n + 1) * 512] = _dot(xn, wm_ref[:, 3072 + n * 512:3072 + (n + 1) * 512])
        dt_ref[...] = _dot(xn, wdt_ref[...])
        for k in range(tm // CH):
            rows = slice(k * CH, (k + 1) * CH)
            f = _gmlp_fwd_vals(uv_ref[rows, 0:1024], uv_ref[rows, 1024:2048], gv_ref[...], ws_ref, bst_ref[...],
                               gout_ref[...])
            ya_ref[rows, :] = f["out"].astype(MXU)

    return pl.pallas_call(
        _after(8, body), grid=(T // tm,), name="inproj_gmlp",
        in_specs=[_rows(tm, D), _const((1, D)), _const((D, N_MAIN)), _const((D, 128)), _const((1, 1024)),
                  _const((GM_HEADS, CH, CH)), _const((CH, 128)), _const((1, 1024)), _ANY],
        out_specs=[_rows(tm, 1024), _rows(tm, CONV_CH), _rows(tm, 128), _rows(tm, D), _rows(tm, 1024, 0),
                   _rows(tm, 2048)],
        out_shape=[_sds((T, 1024), F32), _sds((T, CONV_CH), F32), _sds((T, 128), F32), _sds((T, D), MXU),
                   _sds((T, 2048), MXU), _sds((T, 2048), F32)],
        compiler_params=_cp(),
    )(x, g, wm, wdt, gv, ws, bst, gout, after)


def _gmlp_fwd_vals(u, v, gv, ws_ref, bst, gout):
    ug, dug = _gelu(u)
    vg, dvg = _gelu(v)
    row = lax.broadcasted_iota(jnp.int32, (CH, CH), 0)
    col = lax.broadcasted_iota(jnp.int32, (CH, CH), 1)
    tril = row >= col
    ys, heads = [], []
    for h in range(GM_HEADS):
        sl = slice(h * 128, (h + 1) * 128)
        vhat, rv = _rms(vg[:, sl])
        vn = (vhat * gv[:, sl]).astype(MXU)
        wt = jnp.where(tril, ws_ref[h], 0.0)
        mixed = _dot(wt.astype(MXU), vn) + bst[:, h:h + 1]
        ys.append(ug[:, sl] * mixed)
        heads.append((vhat, rv, vn, wt, mixed))
    y = jnp.concatenate(ys, axis=1)
    yhat, ry = _rms(y)
    return dict(ug=ug, dug=dug, dvg=dvg, heads=heads, yhat=yhat, ry=ry, tril=tril, out=yhat * gout)


def _shifts_down(cur, halo):
    row8 = lax.broadcasted_iota(jnp.int32, (8, cur.shape[1]), 0)
    out = [cur]
    for j in (1, 2, 3):
        sh = pltpu.roll(cur, j, 0)
        top = jnp.where(row8 < j, pltpu.roll(halo, j, 0), sh[0:8])
        out.append(jnp.concatenate([top, sh[8:]], axis=0))
    return out


def _shifts_up(cur, halo):
    row8 = lax.broadcasted_iota(jnp.int32, (8, cur.shape[1]), 0)
    out = []
    for j in (1, 2, 3):
        sh = pltpu.roll(cur, CH - j, 0)
        bot = jnp.where(row8 + j >= 8, pltpu.roll(halo, 8 - j, 0), sh[CH - 8:CH])
        out.append(jnp.concatenate([sh[0:CH - 8], bot], axis=0))
    return out


def _conv(xbc, halo, convw, convb):
    sh = _shifts_down(xbc, halo)
    return convb + convw[3:4] * sh[0] + convw[2:3] * sh[1] + convw[1:2] * sh[2] + convw[0:1] * sh[3]


def _ssd_fwd_vals(z, conv, dtraw, dtb, alog, dskip, ng, ex, ltri, s_prev):
    sig_c = _sigmoid(conv)
    xa = conv * sig_c
    xs = xa[:, :1024]
    bm = [xa[:, 1024:1152], xa[:, 1152:1280]]
    cm = [xa[:, 1280:1408], xa[:, 1408:1536]]
    dtpre = dtraw + dtb
    dt = _softplus(dtpre)
    a_neg = -jnp.exp(alog)
    cs = _dot_hi(ltri, dt * a_neg)
    cst = cs.T
    last = cs[CH - 1:CH]
    ecs = jnp.exp(cs)
    dec = jnp.exp(last - cs)
    spread = _dot_01(jnp.concatenate([dt, ecs, dec], axis=0), ex)
    dte, ecse, dece = spread[0:CH], spread[CH:2 * CH], spread[2 * CH:3 * CH]
    cde = ecse[CH - 1:CH]
    de = dskip
    xdt = xs * dte
    row = lax.broadcasted_iota(jnp.int32, (CH, CH), 0)
    col = lax.broadcasted_iota(jnp.int32, (CH, CH), 1)
    tril = row >= col
    lo = col < SSD_P
    bmb = [b.astype(MXU) for b in bm]
    cmb = [c.astype(MXU) for c in cm]
    mg = [_dot_nt(cmb[g], bmb[g]) for g in range(2)]
    yd, lms, whs = [], [], []
    for q in range(8):
        g = q // 4
        xq = xdt[:, q * 128:(q + 1) * 128]
        acc = None
        for hh in range(2):
            h = 2 * q + hh
            seg = cs[:, h:h + 1] - cst[h:h + 1, :]
            lm = jnp.exp(jnp.where(tril, seg, NEG))
            wh = (mg[g] * lm).astype(MXU)
            xm = jnp.where(lo if hh == 0 else ~lo, xq, 0.0).astype(MXU)
            part = _dot(wh, xm)
            acc = part if acc is None else acc + part
            lms.append(lm)
            whs.append(wh)
        yd.append(acc)
    yd = jnp.concatenate(yd, axis=1)
    sb = s_prev.astype(MXU)
    yo = jnp.concatenate([_dot(cmb[g], sb[:, g * 512:(g + 1) * 512]) for g in range(2)], axis=1) * ecse
    xdec = (xdt * dece).astype(MXU)
    states = jnp.concatenate([_dot_tn(bmb[g], xdec[:, g * 512:(g + 1) * 512]) for g in range(2)], axis=1)
    s_next = s_prev * cde + states
    ypre = yd + yo + de * xs
    sig_z = _sigmoid(z)
    yg = ypre * z * sig_z
    outs, yhat, rr = [], [], []
    for g in range(2):
        sl = slice(g * 512, (g + 1) * 512)
        yh, r = _rms(yg[:, sl])
        yhat.append(yh)
        rr.append(r)
        outs.append(yh * ng[:, sl])
    return dict(sig_c=sig_c, xa=xa, xs=xs, bmb=bmb, cmb=cmb, dtpre=dtpre, dt=dt, a_neg=a_neg,
                cs=cs, last=last, ecs=ecs, dec=dec, dte=dte, ecse=ecse, dece=dece, cde=cde, de=de, xdt=xdt,
                mg=mg, lms=lms, whs=whs, lo=lo, yo=yo, sb=sb, xdec=xdec, s_next=s_next, ypre=ypre, sig_z=sig_z,
                yhat=yhat, rr=rr, out=jnp.concatenate(outs, axis=1))


def _ssd_fwd(pz, pxbc, dtraw, cat, convw, convb, dtb, alog, dskip, ng, ex, ltri, nb):
    T = pz.shape[0]
    S = T // nb
    nch = S // CH
    ns = _SEQS_PER_STEP if nb % _SEQS_PER_STEP == 0 else 1

    def body(z_ref, xbc_ref, halo_ref, dt_ref, cw_ref, cb_ref, dtb_ref, al_ref, ds_ref, ng_ref, ex_ref, lt_ref,
             cat_in_ref, yb_ref, sall_ref, conv_ref, s_ref):
        del cat_in_ref
        c = pl.program_id(1)

        @pl.when(c == 0)
        def _():
            s_ref[...] = jnp.zeros_like(s_ref)

        for i in range(ns):
            halo = jnp.where(c == 0, 0.0, halo_ref[i])
            s_prev = s_ref[i]
            sall_ref[i, 0] = s_prev
            conv = _conv(xbc_ref[i], halo, cw_ref[...], cb_ref[...])
            conv_ref[i] = conv
            f = _ssd_fwd_vals(z_ref[i], conv, dt_ref[i], dtb_ref[...], al_ref[...], ds_ref[...], ng_ref[...],
                              ex_ref[...], lt_ref[...], s_prev)
            s_ref[i] = f["s_next"]
            yb_ref[i] = f["out"].astype(MXU)

    def seq(width, col=0):
        return pl.BlockSpec((ns, CH, width), lambda b, c: (b, c, col))

    cat, sall, conv = pl.pallas_call(
        body, grid=(nb // ns, nch), name="ssd_fwd",
        in_specs=[seq(1024), seq(CONV_CH),
                  pl.BlockSpec((ns, 8, CONV_CH), lambda b, c: (b, jnp.maximum(c * (CH // 8) - 1, 0), 0)),
                  seq(128),
                  _const((8, CONV_CH)), _const((1, CONV_CH)), _const((1, 128)), _const((1, 128)), _const((1, 1024)),
                  _const((1, 1024)), _const((128, 1024)), _const((CH, CH)), _ANY],
        out_specs=[seq(1024, 1), pl.BlockSpec((ns, 1, 128, 1024), lambda b, c: (b, c, 0, 0)), seq(CONV_CH)],
        out_shape=[_sds((nb, S, 2048), MXU), _sds((nb, nch, 128, 1024), F32), _sds((nb, S, CONV_CH), F32)],
        scratch_shapes=[pltpu.VMEM((ns, 128, 1024), F32)],
        input_output_aliases={12: 0},
        compiler_params=_cp(2),
    )(pz.reshape(nb, S, 1024), pxbc.reshape(nb, S, CONV_CH), pxbc.reshape(nb, S, CONV_CH), dtraw.reshape(nb, S, 128),
      convw, convb, dtb, alog, dskip, ng, ex, ltri, cat.reshape(nb, S, 2048))
    return cat.reshape(T, 2048), sall, conv.reshape(T, CONV_CH)


def _outproj(cat, wo, x, g, tm):
    T = x.shape[0]

    def body(cat_ref, wo_ref, x_ref, g_ref, h1_ref, hn_ref):
        h1 = x_ref[...] + _dot(cat_ref[...], wo_ref[...])
        h1_ref[...] = h1
        hn_ref[...] = (_rms(h1)[0] * g_ref[...]).astype(MXU)

    return pl.pallas_call(
        body, grid=(T // tm,), name="outproj",
        in_specs=[_rows(tm, 2048), _const((2048, D)), _rows(tm, D), _const((1, D))],
        out_specs=[_rows(tm, D), _rows(tm, D)],
        out_shape=[_sds((T, D), F32), _sds((T, D), MXU)],
        compiler_params=_cp(),
    )(cat, wo, x, g)


def _ff1(hn, w1, tm):
    T = hn.shape[0]

    def body(hn_ref, w1_ref, hid_ref):
        hn_v = hn_ref[...]
        for n in range(4):
            hid_ref[:, n * 1024:(n + 1) * 1024] = jnp.maximum(_dot(hn_v, w1_ref[n]), 0.0).astype(MXU)

    return pl.pallas_call(
        body, grid=(T // tm,), name="ff1",
        in_specs=[_rows(tm, D), _const((4, D, 1024))],
        out_specs=_rows(tm, DFF),
        out_shape=_sds((T, DFF), MXU),
        compiler_params=_cp(),
    )(hn, w1)


def _sq(hid):
    h = hid.astype(F32)
    return (h * h).astype(MXU)


def _ff2_tail(hid, w2, h1, g_ple, p, tgt, wg, wp, gf, tm):
    T = h1.shape[0]

    def body(hid_ref, w2_ref, h1_ref, g_ref, p_ref, t_ref, wg_ref, wp_ref, gf_ref,
             hp_ref, dgl_ref, dpe_ref, dh2_ref, dh2b_ref, loss_ref, dgf_ref, dg_ref):
        @pl.when(pl.program_id(0) == 0)
        def _():
            loss_ref[...] = jnp.zeros_like(loss_ref)
            dgf_ref[...] = jnp.zeros_like(dgf_ref)
            dg_ref[...] = jnp.zeros_like(dg_ref)

        h2 = h1_ref[...] + _dot(_sq(hid_ref[...]), w2_ref[...])
        h2h, r2 = _rms(h2)
        g_ple = g_ref[...]
        hp = (h2h * g_ple).astype(MXU)
        hp_ref[...] = hp
        gate = _sigmoid(_dot(hp, wg_ref[...]))
        pb = p_ref[...].astype(MXU)
        pe = jnp.concatenate([_dot(pb, wp_ref[k]) for k in range(4)], axis=1)
        h3 = h2 + gate * pe
        hh, r = _rms(h3)
        gf = gf_ref[...]
        diff = hh * gf - t_ref[...]
        loss_ref[...] += 0.5 * jnp.sum(jnp.mean(diff * diff, axis=-1, keepdims=True))
        dout = diff * (1.0 / D)
        dgf_ref[...] += jnp.sum(dout * hh, axis=0, keepdims=True)
        dh3 = _rms_bwd(dout, hh, r, gf)
        dgl = (dh3 * pe * gate * (1.0 - gate)).astype(MXU)
        dgl_ref[...] = dgl
        dpe_ref[...] = (dh3 * gate).astype(MXU)
        dhp = _dot_nt(dgl, wg_ref[...])
        dg_ref[...] += jnp.sum(dhp * h2h, axis=0, keepdims=True)
        dh2 = dh3 + _rms_bwd(dhp, h2h, r2, g_ple)
        dh2_ref[...] = dh2
        dh2b_ref[...] = dh2.astype(MXU)

    return pl.pallas_call(
        body, grid=(T // tm,), name="ff2_tail",
        in_specs=[_rows(tm, DFF), _const((DFF, D)), _rows(tm, D), _const((1, D)), _rows(tm, DPLE), _rows(tm, D),
                  _const((D, D)), _const((4, DPLE, 256)), _const((1, D))],
        out_specs=[_rows(tm, D), _rows(tm, D), _rows(tm, D), _rows(tm, D), _rows(tm, D), _const((8, 128)),
                   _const((1, D)), _const((1, D))],
        out_shape=[_sds((T, D), MXU), _sds((T, D), MXU), _sds((T, D), MXU), _sds((T, D), F32), _sds((T, D), MXU),
                   _sds((8, 128), F32), _sds((1, D), F32), _sds((1, D), F32)],
        compiler_params=_cp(),
    )(hid, w2, h1, g_ple, p, tgt, wg, wp, gf)


def _ff2_bwd(dh2b, w2, hid, tm):
    T = hid.shape[0]

    def body(dh2b_ref, w2_ref, hid_ref, dpre_ref):
        d = dh2b_ref[...]
        for n in range(DFF // 1024):
            sl = slice(n * 1024, (n + 1) * 1024)
            da = _dot_nt(d, w2_ref[sl, :])
            dpre_ref[:, sl] = (2.0 * da * hid_ref[:, sl].astype(F32)).astype(MXU)

    return pl.pallas_call(
        body, grid=(T // tm,), name="ff2_bwd",
        in_specs=[_rows(tm, D), _const((DFF, D)), _rows(tm, DFF)],
        out_specs=_rows(tm, DFF),
        out_shape=_sds((T, DFF), MXU),
        compiler_params=_cp(),
    )(dh2b, w2, hid)


def _ff1_bwd(dpre, w1, dh2, h1, g, tm, after):
    T = h1.shape[0]

    def body(dpre_ref, w1_ref, dh2_ref, h1_ref, g_ref, dh1_ref, dh1b_ref, dg_ref):
        @pl.when(pl.program_id(0) == 0)
        def _():
            dg_ref[...] = jnp.zeros_like(dg_ref)

        dhn = _dot_nt(dpre_ref[:, 0:1024], w1_ref[0])
        for k in range(1, 4):
            dhn = dhn + _dot_nt(dpre_ref[:, k * 1024:(k + 1) * 1024], w1_ref[k])
        hh, r = _rms(h1_ref[...])
        dg_ref[...] += jnp.sum(dhn * hh, axis=0, keepdims=True)
        dh1 = dh2_ref[...] + _rms_bwd(dhn, hh, r, g_ref[...])
        dh1_ref[...] = dh1
        dh1b_ref[...] = dh1.astype(MXU)

    return pl.pallas_call(
        _after(5, body), grid=(T // tm,), name="ff1_bwd",
        in_specs=[_rows(tm, DFF), _const((4, D, 1024)), _rows(tm, D), _rows(tm, D), _const((1, D)), _ANY],
        out_specs=[_rows(tm, D), _rows(tm, D), _const((1, D))],
        out_shape=[_sds((T, D), F32), _sds((T, D), MXU), _sds((1, D), F32)],
        compiler_params=_cp(),
    )(dpre, w1, dh2, h1, g, after)


def _outproj_bwd(dh1b, wo, tm):
    T = dh1b.shape[0]

    def body(d_ref, wo_ref, dcat_ref):
        d = d_ref[...]
        dcat_ref[:, 0:1024] = _dot_nt(d, wo_ref[0:1024, :])
        dcat_ref[:, 1024:2048] = _dot_nt(d, wo_ref[1024:2048, :])

    return pl.pallas_call(
        body, grid=(T // tm,), name="outproj_bwd",
        in_specs=[_rows(tm, D), _const((2048, D))],
        out_specs=_rows(tm, 2048),
        out_shape=_sds((T, 2048), F32),
        compiler_params=_cp(),
    )(dh1b, wo)


def _gmlp_bwd(uv, dcat, gv, ws, bst, gout, wm):
    T = uv.shape[0]
    nck = 2 if T % (2 * CH) == 0 else 1
    tb = nck * CH

    def body(uv_ref, dya_ref, gv_ref, ws_ref, bst_ref, gout_ref, wuv_ref, duv_ref, dgv_ref, dws_ref, dbst_ref,
             dgo_ref, dxn_ref):
        @pl.when(pl.program_id(0) == 0)
        def _():
            dgv_ref[...] = jnp.zeros_like(dgv_ref)
            dws_ref[...] = jnp.zeros_like(dws_ref)
            dbst_ref[...] = jnp.zeros_like(dbst_ref)
            dgo_ref[...] = jnp.zeros_like(dgo_ref)

        for k in range(nck):
            chunk(slice(k * CH, (k + 1) * CH), uv_ref, dya_ref, gv_ref, ws_ref, bst_ref, gout_ref, duv_ref,
                  dgv_ref, dws_ref, dbst_ref, dgo_ref)
        dxn_ref[...] = _dot_nt(duv_ref[...], wuv_ref[...])

    def chunk(rows, uv_ref, dya_ref, gv_ref, ws_ref, bst_ref, gout_ref, duv_ref, dgv_ref, dws_ref, dbst_ref,
              dgo_ref):
        gv = gv_ref[...]
        f = _gmlp_fwd_vals(uv_ref[rows, 0:1024], uv_ref[rows, 1024:2048], gv, ws_ref, bst_ref[...], gout_ref[...])
        dya = dya_ref[rows, :]
        dgo_ref[...] += jnp.sum(dya * f["yhat"], axis=0, keepdims=True)
        dy = _rms_bwd(dya, f["yhat"], f["ry"], gout_ref[...])
        lane = lax.broadcasted_iota(jnp.int32, (CH, 128), 1)
        dbs = jnp.zeros((CH, 128), F32)
        dug, dvg, dgvs = [], [], []
        for h in range(GM_HEADS):
            sl = slice(h * 128, (h + 1) * 128)
            vhat, rv, vn, wt, mixed = f["heads"][h]
            dyh = dy[:, sl]
            dug.append(dyh * mixed)
            dmixed = dyh * f["ug"][:, sl]
            dmb = dmixed.astype(MXU)
            dws_ref[h] += jnp.where(f["tril"], _dot_nt(dmb, vn), 0.0)
            dbs = dbs + jnp.where(lane == h, jnp.sum(dmixed, axis=1, keepdims=True), 0.0)
            dvn = _dot_tn(wt.astype(MXU), dmb)
            dgvs.append(jnp.sum(dvn * vhat, axis=0, keepdims=True))
            dvg.append(_rms_bwd(dvn, vhat, rv, gv[:, sl]))
        dbst_ref[...] += dbs
        dgv_ref[...] += jnp.concatenate(dgvs, axis=1)
        duv_ref[rows, 0:1024] = (jnp.concatenate(dug, axis=1) * f["dug"]).astype(MXU)
        duv_ref[rows, 1024:2048] = (jnp.concatenate(dvg, axis=1) * f["dvg"]).astype(MXU)

    return pl.pallas_call(
        body, grid=(T // tb,), name="gmlp_bwd",
        in_specs=[_rows(tb, 2048), _rows(tb, 1024, 0), _const((1, 1024)),
                  _const((GM_HEADS, CH, CH)), _const((CH, 128)), _const((1, 1024)), _const((D, 2048))],
        out_specs=[_rows(tb, 2048), _const((1, 1024)), _const((GM_HEADS, CH, CH)), _const((CH, 128)),
                   _const((1, 1024)), _rows(tb, D)],
        out_shape=[_sds((T, 2048), MXU), _sds((1, 1024), F32), _sds((GM_HEADS, CH, CH), F32), _sds((CH, 128), F32),
                   _sds((1, 1024), F32), _sds((T, D), F32)],
        compiler_params=_cp(),
    )(uv, dcat, gv, ws, bst, gout, wm)


def _ssd_bwd(pz, pxbc, conv, dtraw, sall, dcat, convw, dtb, alog, dskip, ng, ex, ltri, ext, nb, after):
    T = pz.shape[0]
    S = T // nb
    nch = S // CH
    ns = _SEQS_PER_STEP if nb % _SEQS_PER_STEP == 0 else 1

    def seq(width, col=0):
        return pl.BlockSpec((ns, CH, width), lambda b, c: (b, nch - 1 - c, col))

    in_specs = [
        seq(1024), seq(CONV_CH), seq(CONV_CH), seq(128),
        _const((8, CONV_CH)), _const((1, 128)), _const((1, 128)), _const((1, 1024)),
        _const((1, 1024)), _const((128, 1024)), _const((CH, CH)),
        _const((1024, 128)),
        pl.BlockSpec((ns, 1, 128, 1024), lambda b, c: (b, nch - 1 - c, 0, 0)),
        seq(1024, 1),
        _ANY,
    ]

    def body(z_ref, xbc_ref, conv_ref, dt_ref, cw_ref, dtb_ref, al_ref, ds_ref, ng_ref, ex_ref, lt_ref,
             ext_ref, sall_ref, dyb_ref,
             dssd_ref, ddt_ref, dcw_ref, dcb_ref, ddtb_ref, dal_ref, dds_ref, dng_ref,
             dst_ref, dnext_ref, ddse_ref):
        b = pl.program_id(0)
        c = pl.program_id(1)

        @pl.when((b == 0) & (c == 0))
        def _():
            for r in (dcw_ref, dcb_ref, ddtb_ref, dal_ref, dds_ref, dng_ref, ddse_ref):
                r[...] = jnp.zeros_like(r)

        @pl.when(c == 0)
        def _():
            dst_ref[...] = jnp.zeros_like(dst_ref)
            dnext_ref[...] = jnp.zeros_like(dnext_ref)

        ex = ex_ref[...]
        ext = ext_ref[...]
        cw = cw_ref[...]
        ng = ng_ref[...]
        for i in range(ns):
            one_chunk(i, ex, ext, cw, ng, z_ref, xbc_ref, conv_ref, dt_ref, dtb_ref, al_ref, ds_ref, lt_ref, sall_ref,
                      dyb_ref, dssd_ref, ddt_ref, dcw_ref, dcb_ref, ddtb_ref, dal_ref, dng_ref, dst_ref, dnext_ref,
                      ddse_ref)

        @pl.when((b == nb // ns - 1) & (c == nch - 1))
        def _():
            dds_ref[...] = _dot_01(jnp.broadcast_to(ddse_ref[...], (8, 1024)), ext)[0:1]

    def one_chunk(i, ex, ext, cw, ng, z_ref, xbc_ref, conv_ref, dt_ref, dtb_ref, al_ref, ds_ref, lt_ref, sall_ref,
                  dyb_ref, dssd_ref, ddt_ref, dcw_ref, dcb_ref, ddtb_ref, dal_ref, dng_ref, dst_ref, dnext_ref,
                  ddse_ref):
        z = z_ref[i]
        s_prev = sall_ref[i, 0]
        conv = conv_ref[i]
        f = _ssd_fwd_vals(z, conv, dt_ref[i], dtb_ref[...], al_ref[...], ds_ref[...], ng, ex, lt_ref[...], s_prev)
        xs, xdt, cs, dec, dt = f["xs"], f["xdt"], f["cs"], f["dec"], f["dt"]
        dyb = dyb_ref[i]
        dyg, dngs = [], []
        for g in range(2):
            sl = slice(g * 512, (g + 1) * 512)
            dngs.append(jnp.sum(dyb[:, sl] * f["yhat"][g], axis=0, keepdims=True))
            dyg.append(_rms_bwd(dyb[:, sl], f["yhat"][g], f["rr"][g], ng[:, sl]))
        dng_ref[...] += jnp.concatenate(dngs, axis=1)
        dyg = jnp.concatenate(dyg, axis=1)
        sig_z = f["sig_z"]
        silu_z = z * sig_z
        dy = dyg * silu_z
        dz = dyg * f["ypre"] * (sig_z + silu_z * (1.0 - sig_z))
        ddse_ref[...] += jnp.sum(dy * xs, axis=0, keepdims=True)
        dxs = dy * f["de"]
        dye = dy * f["ecse"]
        dyeb = dye.astype(MXU)
        dst = dst_ref[i]
        dstb = dst.astype(MXU)
        bmb, cmb, sb, xdec = f["bmb"], f["cmb"], f["sb"], f["xdec"]
        u = jnp.concatenate([_dot(bmb[g], dstb[:, g * 512:(g + 1) * 512]) for g in range(2)], axis=1)
        dxdt = [u[:, q * 128:(q + 1) * 128] * f["dece"][:, q * 128:(q + 1) * 128] for q in range(8)]
        per_head = _dot_01(jnp.concatenate(
            [dy * f["yo"], u * xdt, jnp.broadcast_to(jnp.sum(dst * s_prev, axis=0, keepdims=True), (8, 1024))],
            axis=0), ext)
        dcs = per_head[0:CH]
        t = per_head[CH:2 * CH] * dec
        dcd = per_head[2 * CH:2 * CH + 1]
        row = lax.broadcasted_iota(jnp.int32, (CH, 128), 0)
        lane = lax.broadcasted_iota(jnp.int32, (CH, 128), 1)
        cd = jnp.exp(f["last"])
        dcs = dcs - t + jnp.where(row == CH - 1, jnp.sum(t, axis=0, keepdims=True) + dcd * cd, 0.0)
        dcst = jnp.zeros((128, CH), F32)
        lo = f["lo"]
        dbm, dcm, ds_prev = [], [], []
        for g in range(2):
            sl = slice(g * 512, (g + 1) * 512)
            dmg = jnp.zeros((CH, CH), F32)
            for q in range(4 * g, 4 * g + 4):
                dyq = dy[:, q * 128:(q + 1) * 128]
                xq = xdt[:, q * 128:(q + 1) * 128].astype(MXU)
                for hh in range(2):
                    h = 2 * q + hh
                    m = lo if hh == 0 else ~lo
                    dym = jnp.where(m, dyq, 0.0).astype(MXU)
                    gh = _dot_nt(dym, xq)
                    gl = gh * f["lms"][h]
                    dmg = dmg + gl
                    qh = gl * f["mg"][g]
                    dcs = dcs + jnp.where(lane == h, jnp.sum(qh, axis=1, keepdims=True), 0.0)
                    dcst = dcst - jnp.where(row == h, jnp.sum(qh, axis=0, keepdims=True), 0.0)
                    dxdt[q] = dxdt[q] + _dot_tn(f["whs"][h], dym)
            dmgb = dmg.astype(MXU)
            dcm.append(_dot(dmgb, bmb[g]) + _dot_nt(dyeb[:, sl], sb[:, sl]))
            dbm.append(_dot_tn(dmgb, cmb[g]) + _dot_nt(xdec[:, sl], dstb[:, sl]))
            ds_prev.append(_dot_tn(cmb[g], dyeb[:, sl]))
        dst_ref[i] = jnp.concatenate(ds_prev, axis=1) + dst * f["cde"]
        dcs = dcs + dcst.T
        da = _dot_hi(lt_ref[...].T, dcs)
        dxdt = jnp.concatenate(dxdt, axis=1)
        a_neg = f["a_neg"]
        ddt = da * a_neg + _dot_01(dxdt * xs, ext)
        dal_ref[...] += jnp.sum(da * dt, axis=0, keepdims=True) * a_neg
        dxs = dxs + dxdt * f["dte"]
        ddtraw = jnp.where(lane < SSD_HEADS, ddt * _sigmoid(f["dtpre"]), 0.0)
        ddtb_ref[...] += jnp.sum(ddtraw, axis=0, keepdims=True)
        ddt_ref[i] = ddtraw.astype(MXU)
        dxa = jnp.concatenate([dxs, dbm[0], dbm[1], dcm[0], dcm[1]], axis=1)
        sig_c = f["sig_c"]
        dconv = dxa * (sig_c + f["xa"] * (1.0 - sig_c))
        dcb_ref[...] += jnp.sum(dconv, axis=0, keepdims=True)
        xbc = xbc_ref[i]
        dcw_ref[3:4, :] += jnp.sum(dconv * xbc, axis=0, keepdims=True)
        dxbc = cw[3:4] * dconv
        for j, up in zip((1, 2, 3), _shifts_up(dconv, dnext_ref[i])):
            dcw_ref[3 - j:4 - j, :] += jnp.sum(up * xbc, axis=0, keepdims=True)
            dxbc = dxbc + cw[3 - j:4 - j] * up
        dnext_ref[i] = dconv[0:8]
        dssd_ref[i, :, 0:1024] = dz.astype(MXU)
        dssd_ref[i, :, 1024:2560] = dxbc.astype(MXU)

    dssd, ddt, *small = pl.pallas_call(
        _after(14, body), grid=(nb // ns, nch), name="ssd_bwd",
        in_specs=in_specs,
        out_specs=[seq(2560), seq(128),
                   _const((8, CONV_CH)), _const((1, CONV_CH)), _const((1, 128)), _const((1, 128)), _const((1, 128)),
                   _const((1, 1024))],
        out_shape=[_sds((nb, S, 2560), MXU), _sds((nb, S, 128), MXU), _sds((8, CONV_CH), F32),
                   _sds((1, CONV_CH), F32), _sds((1, 128), F32), _sds((1, 128), F32), _sds((1, 128), F32),
                   _sds((1, 1024), F32)],
        scratch_shapes=[pltpu.VMEM((ns, 128, 1024), F32), pltpu.VMEM((ns, 8, CONV_CH), F32),
                        pltpu.VMEM((1, 1024), F32)],
        compiler_params=_cp(2),
    )(pz.reshape(nb, S, 1024), pxbc.reshape(nb, S, CONV_CH), conv.reshape(nb, S, CONV_CH), dtraw.reshape(nb, S, 128),
      convw, dtb, alog, dskip, ng, ex, ltri, ext, sall, dcat.reshape(nb, S, 2048), after)
    return (dssd.reshape(T, 2560), ddt.reshape(T, 128), *small)


def _inproj_bwd(dxn_uv, dssd, ddt, wm, wdt, dh1, x, g, tm, after):
    T = x.shape[0]

    def body(dxnuv_ref, dssd_ref, ddt_ref, wm_ref, wdt_ref, dh1_ref, x_ref, g_ref, dx_ref, dg_ref):
        @pl.when(pl.program_id(0) == 0)
        def _():
            dg_ref[...] = jnp.zeros_like(dg_ref)

        dxn = (dxnuv_ref[...] + _dot_nt(dssd_ref[...], wm_ref[:, 2048:N_MAIN])
               + _dot_nt(ddt_ref[...], wdt_ref[...]))
        xh, r = _rms(x_ref[...])
        dg_ref[...] += jnp.sum(dxn * xh, axis=0, keepdims=True)
        dx_ref[...] = dh1_ref[...] + _rms_bwd(dxn, xh, r, g_ref[...])

    return pl.pallas_call(
        _after(8, body), grid=(T // tm,), name="inproj_bwd",
        in_specs=[_rows(tm, D), _rows(tm, 2560), _rows(tm, 128), _const((D, N_MAIN)), _const((D, 128)),
                  _rows(tm, D), _rows(tm, D), _const((1, D)), _ANY],
        out_specs=[_rows(tm, D), _const((1, D))],
        out_shape=[_sds((T, D), F32), _sds((1, D), F32)],
        compiler_params=_cp(),
    )(dxn_uv, dssd, ddt, wm, wdt, dh1, x, g, after)


def _matmul_tn(a, b, name, a_fn=None):
    T, M = a.shape
    N = b.shape[1]
    tm = min(M, 1024)
    tn = 1280 if N == 2560 else min(N, 1024)
    tk = min(T, 2048)

    def body(a_ref, b_ref, o_ref, acc_ref):
        k = pl.program_id(2)

        @pl.when(k == 0)
        def _():
            acc_ref[...] = jnp.zeros_like(acc_ref)

        av = a_ref[...]
        if a_fn is not None:
            av = a_fn(av)
        acc_ref[...] += _dot_tn(av, b_ref[...])

        @pl.when(k == T // tk - 1)
        def _():
            o_ref[...] = acc_ref[...].astype(o_ref.dtype)

    return pl.pallas_call(
        body, grid=(M // tm, N // tn, T // tk), name=name,
        in_specs=[pl.BlockSpec((tk, tm), lambda i, j, k: (k, i)), pl.BlockSpec((tk, tn), lambda i, j, k: (k, j))],
        out_specs=pl.BlockSpec((tm, tn), lambda i, j, k: (i, j)),
        out_shape=_sds((M, N), GRAD),
        scratch_shapes=[pltpu.VMEM((tm, tn), F32)],
        compiler_params=_cp(3),
    )(a, b)


def _adamw_vals(w, g, m, v):
    m = B1 * m + (1.0 - B1) * g
    v = B2 * v + (1.0 - B2) * (g * g)
    m_hat = m / (1.0 - B1 ** STEP)
    v_hat = v / (1.0 - B2 ** STEP)
    return -LR * (m_hat / (jnp.sqrt(v_hat) + ADAM_EPS) + WD * w), m, v


def _adamw(w, g, m, v, name):
    R, C = w.shape
    tr = 256 if R % 256 == 0 else R

    def body(w_ref, g_ref, m_ref, v_ref, d_ref, mo_ref, vo_ref):
        d_ref[...], mo_ref[...], vo_ref[...] = _adamw_vals(w_ref[...], g_ref[...], m_ref[...], v_ref[...])

    spec = _rows(tr, C)
    return pl.pallas_call(
        body, grid=(R // tr,), name=name,
        in_specs=[spec] * 4, out_specs=[spec] * 3, out_shape=[_sds((R, C), F32)] * 3,
        compiler_params=_cp(),
    )(w, g, m, v)


_PARTS = 4


def _adamw_halves(items, name):
    n = len(items)

    def body(*refs):
        mine = (pl.program_id(0) // _PARTS) == lax.axis_index("c")
        for k in range(n):
            w_ref, own_ref, oth_ref, m_ref, v_ref = refs[5 * k:5 * k + 5]
            g_ref, d_ref, mo_ref, vo_ref = refs[5 * n + 4 * k:5 * n + 4 * k + 4]
            g = jnp.where(mine, own_ref[...], oth_ref[...])
            g_ref[...] = g
            d_ref[...], mo_ref[...], vo_ref[...] = _adamw_vals(w_ref[...], g, m_ref[...], v_ref[...])

    in_specs, out_specs, out_shape = [], [], []
    for w, *_ in items:
        R, C = w.shape
        full = _rows(R // (2 * _PARTS), C)
        part = pl.BlockSpec((R // (2 * _PARTS), C), lambda i: (i % _PARTS, 0))
        in_specs += [full, part, part, full, full]
        out_specs += [full] * 4
        out_shape += [_sds((R, C), F32)] * 4
    res = pl.pallas_call(
        body, grid=(2 * _PARTS,), name=name, in_specs=in_specs, out_specs=out_specs, out_shape=out_shape,
        compiler_params=_cp(),
    )(*[a for item in items for a in item])
    return [tuple(res[4 * k:4 * k + 4]) for k in range(n)]


def _sum_small(slots, name):
    nd, rows, C = slots.shape

    def body(s_ref, o_ref):
        acc = s_ref[0]
        for d in range(1, nd):
            acc = acc + s_ref[d]
        o_ref[...] = acc

    return pl.pallas_call(
        body, grid=(1,), name=name,
        in_specs=[_const((nd, rows, C))], out_specs=_const((rows, C)), out_shape=_sds((rows, C), F32),
        compiler_params=_cp(),
    )(slots)


def _sum_slots(items, kh, name):
    n = len(items)
    in_specs, out_specs, out_shape = [], [], []
    for slots, src, kind, (R, C) in items:
        tr = R // (2 * _PARTS)
        if kind == "slab":
            src_spec = pl.BlockSpec((1, tr, C), lambda i, kh: (kh[0], kh[1] * _PARTS + i, 0))
        elif kind == "rows":
            src_spec = pl.BlockSpec((tr, C), lambda i, kh: (kh[0] * (2 * _PARTS) + kh[1] * _PARTS + i, 0))
        else:
            src_spec = pl.BlockSpec((tr, C), lambda i, kh: (kh[1] * _PARTS + i, kh[0]))
        in_specs += [pl.BlockSpec((8, tr, C), lambda i, kh: (0, i, 0)), src_spec]
        out_specs.append(pl.BlockSpec((tr, C), lambda i, kh: (i, 0)))
        out_shape.append(_sds((R // 2, C), F32))

    def body(kh_ref, *refs):
        me = 2 * kh_ref[0] + kh_ref[1]
        for k, (_, _, kind, _) in enumerate(items):
            s_ref, own_ref, o_ref = refs[2 * k], refs[2 * k + 1], refs[2 * n + k]
            acc = (own_ref[0] if kind == "slab" else own_ref[...]).astype(F32)
            for j in range(1, 8):
                acc = acc + s_ref[me ^ j].astype(F32)
            o_ref[...] = acc

    return pl.pallas_call(
        body, name=name,
        grid_spec=pltpu.PrefetchScalarGridSpec(
            num_scalar_prefetch=1, grid=(_PARTS,), in_specs=in_specs, out_specs=out_specs),
        out_shape=out_shape,
        compiler_params=_cp(),
    )(kh, *[a for slots, src, _, _ in items for a in (slots, src)])


def _assemble_w_in(slabs):
    tr = 256

    def body(s_ref, wm_ref, wdt_ref):
        full = jnp.concatenate([s_ref[k] for k in range(4)], axis=1)
        wm_ref[...] = full[:, :N_MAIN]
        wdt_ref[...] = jnp.concatenate([full[:, N_MAIN:], jnp.zeros((tr, 128 - 16), full.dtype)], axis=1)

    return pl.pallas_call(
        body, grid=(D // tr,), name="assemble_w_in",
        in_specs=[pl.BlockSpec((4, tr, 1156), lambda i: (0, i, 0))],
        out_specs=[_rows(tr, N_MAIN), _rows(tr, 128)],
        out_shape=[_sds((D, N_MAIN), slabs.dtype), _sds((D, 128), slabs.dtype)],
        compiler_params=_cp(),
    )(slabs)


def _split_dw_in(d_uv, d_ssd, d_dt):
    tr = 256

    def body(uv_ref, ssd_ref, dt_ref, o_ref):
        full = jnp.concatenate([uv_ref[...], ssd_ref[...], dt_ref[:, 0:16]], axis=1)
        for k in range(4):
            o_ref[k] = full[:, 1156 * k:1156 * (k + 1)]

    return pl.pallas_call(
        body, grid=(D // tr,), name="split_dw_in",
        in_specs=[_rows(tr, 2048), _rows(tr, 2560), _rows(tr, 128)],
        out_specs=pl.BlockSpec((4, tr, 1156), lambda i: (0, i, 0)),
        out_shape=_sds((4, D, 1156), d_uv.dtype),
        compiler_params=_cp(),
    )(d_uv, d_ssd, d_dt)


def _cast_into_slot(ws, kh, name):
    n = len(ws)

    def body(kh_ref, *refs):
        for k in range(n):
            refs[n + k][0] = refs[k][...].astype(BF16)

    return pl.pallas_call(
        body, name=name,
        grid_spec=pltpu.PrefetchScalarGridSpec(
            num_scalar_prefetch=1, grid=(_PARTS,),
            in_specs=[pl.BlockSpec((w.shape[0] // _PARTS, w.shape[1]), lambda i, kh: (i, 0)) for w in ws],
            out_specs=[pl.BlockSpec((1, w.shape[0] // _PARTS, w.shape[1]), lambda i, kh: (kh[0], i, 0))
                       for w in ws]),
        out_shape=[_sds((4,) + w.shape, BF16) for w in ws],
        compiler_params=_cp(),
    )(kh, *ws)


_ANY = pl.BlockSpec(memory_space=pl.ANY)
_CHIP_FLIPS = [(1, 0), (0, 1), (1, 1)]
_DEVICE_FLIPS = [(fx, fy, fc) for fx in (0, 1) for fy in (0, 1) for fc in (0, 1)][1:]


def _half(h, rows):
    return pl.ds(pl.multiple_of(h * rows, rows), rows)


def _remote(src, dst, ssem, rsem, to):
    return pltpu.make_async_remote_copy(src_ref=src, dst_ref=dst, send_sem=ssem, recv_sem=rsem,
                                        device_id=to, device_id_type=MESH)


def _weight_gather(bufs, conv):
    n = len(bufs)

    def body(*refs):
        conv_ref, outs, conv_out = refs[n], refs[n + 1:2 * n + 1], refs[2 * n + 1]
        send_sems, recv_sems, fsend_sems, frecv_sems, csend_sems, crecv_sems, local_sem = refs[2 * n + 2:]
        x, y, c = lax.axis_index("x"), lax.axis_index("y"), lax.axis_index("c")
        me = 2 * x + y
        halves = [_half(c, r.shape[1] // 2) for r in outs]
        others = [_half(1 - c, r.shape[1] // 2) for r in outs]
        remote = _remote
        local = [pltpu.make_async_copy(conv_ref, conv_out.at[me], local_sem)]
        for cp in local:
            cp.start()
        sends = []
        for k, (fx, fy) in enumerate(_CHIP_FLIPS):
            peer = (x ^ fx, y ^ fy, c)
            for i in range(n):
                mine = outs[i].at[me, halves[i]]
                sends.append(remote(mine, mine, send_sems.at[k * n + i], recv_sems.at[k * n + i], peer))
            sends.append(remote(conv_ref, conv_out.at[me], csend_sems.at[k], crecv_sems.at[k], peer))
        for cp in sends:
            cp.start()
        sibling = (x, y, 1 - c)
        forwards = []
        for k, (fx, fy) in enumerate(_CHIP_FLIPS):
            peer = (x ^ fx, y ^ fy, c)
            src = 2 * (x ^ fx) + (y ^ fy)
            for i in range(n):
                landed = outs[i].at[src, halves[i]]
                remote(landed, landed, send_sems.at[k * n + i], recv_sems.at[k * n + i], peer).wait_recv()
                fw = remote(landed, landed, fsend_sems.at[k * n + i], frecv_sems.at[k * n + i], sibling)
                fw.start()
                forwards.append(fw)
            remote(conv_out.at[src], conv_out.at[src], csend_sems.at[k], crecv_sems.at[k], peer).wait_recv()
        for k, (fx, fy) in enumerate(_CHIP_FLIPS):
            src = 2 * (x ^ fx) + (y ^ fy)
            for i in range(n):
                theirs = outs[i].at[src, others[i]]
                remote(theirs, theirs, fsend_sems.at[k * n + i], frecv_sems.at[k * n + i], sibling).wait_recv()
        for cp in sends + forwards:
            cp.wait_send()
        for cp in local:
            cp.wait()

    dma = pltpu.SemaphoreType.DMA
    return pl.pallas_call(
        body, name="weight_gather",
        in_specs=[_ANY] * (n + 1), out_specs=[_ANY] * (n + 1),
        out_shape=[_sds(b.shape, b.dtype) for b in bufs] + [_sds((4,) + conv.shape, conv.dtype)],
        input_output_aliases={i: i for i in range(n)},
        scratch_shapes=[dma((3 * n,)), dma((3 * n,)), dma((3 * n,)), dma((3 * n,)), dma((3,)), dma((3,)), dma],
    )(*bufs, conv)


def _piece(ref, kind, R, C, k, h):
    if kind == "slab":
        return ref.at[k, _half(h, R // 2), :]
    if kind == "rows":
        return ref.at[pl.ds(pl.multiple_of(k * R + h * (R // 2), R // 2), R // 2), :]
    return ref.at[_half(h, R // 2), pl.ds(pl.multiple_of(k * C, C), C)]


def _small_exchange(small, after):
    rs = small.shape[0]

    def body(s_ref, after_ref, out_ref, send_sems, recv_sems, local_sem):
        del after_ref
        x, y, c = lax.axis_index("x"), lax.axis_index("y"), lax.axis_index("c")
        slot = 4 * x + 2 * y + c
        own = pltpu.make_async_copy(s_ref, out_ref.at[slot], local_sem)
        own.start()
        copies = []
        for k, (fx, fy, fc) in enumerate(_DEVICE_FLIPS):
            copies.append(_remote(s_ref, out_ref.at[slot], send_sems.at[k], recv_sems.at[k], (x ^ fx, y ^ fy, c ^ fc)))
        for cp in copies:
            cp.start()
        for k, (fx, fy, fc) in enumerate(_DEVICE_FLIPS):
            theirs = out_ref.at[slot ^ (k + 1)]
            _remote(theirs, theirs, send_sems.at[k], recv_sems.at[k], (x ^ fx, y ^ fy, c ^ fc)).wait_recv()
        for cp in copies:
            cp.wait_send()
        own.wait()

    dma = pltpu.SemaphoreType.DMA
    return pl.pallas_call(
        body, name="small_exchange",
        in_specs=[_ANY, _ANY], out_specs=_ANY, out_shape=_sds((8, rs, 128), F32),
        scratch_shapes=[dma((7,)), dma((7,)), dma],
    )(small, after)


_HBM = pl.BlockSpec(memory_space=pltpu.HBM)
_SEM = pl.BlockSpec(memory_space=pltpu.SEMAPHORE)


def _split_start(name, arrays, n_copies, plan, after=None):
    n = len(arrays)
    extra = [] if after is None else [after]

    def body(*refs):
        m = n + len(extra)
        arrs, send_sems, recv_sems, token = refs[:n], refs[m], refs[m + 1], refs[-1]
        for j, (src, dst, peer) in enumerate(plan(arrs)):
            _remote(src, dst, send_sems.at[j], recv_sems.at[j], peer).start()
        token[...] = jnp.zeros_like(token)

    dma = pltpu.SemaphoreType.DMA
    res = pl.pallas_call(
        body, name=name,
        out_shape=(dma((n_copies,)), dma((n_copies,)), *[pltpu.HBM(a.shape, a.dtype) for a in arrays],
                   _sds((8, 128), F32)),
        in_specs=[_HBM] * n + [_ANY] * len(extra),
        out_specs=(_SEM, _SEM, *[_HBM] * n, pl.BlockSpec(memory_space=pltpu.VMEM)),
        input_output_aliases={i: 2 + i for i in range(n)},
        compiler_params=pltpu.CompilerParams(has_side_effects=pltpu.SideEffectType.DATAFLOW_SIDE_EFFECTING),
    )(*[pltpu.with_memory_space_constraint(a, pltpu.HBM) for a in arrays], *extra)
    return res[0], res[1], list(res[2:2 + n]), res[-1]


def _split_wait(name, arrays, send_sems, recv_sems, plan, after):
    n = len(arrays)

    def body(*refs):
        arrs, ssems, rsems = refs[:n], refs[n], refs[n + 1]
        for j, (src, dst, peer) in enumerate(plan(arrs)):
            cp = _remote(src, dst, ssems.at[j], rsems.at[j], peer)
            cp.wait_send()
            cp.wait_recv()

    return list(pl.pallas_call(
        body, name=name,
        out_shape=tuple(pltpu.HBM(a.shape, a.dtype) for a in arrays),
        in_specs=[_HBM] * n + [_SEM, _SEM, _ANY],
        out_specs=tuple([_HBM] * n),
        input_output_aliases={i: i for i in range(n)},
        compiler_params=pltpu.CompilerParams(has_side_effects=pltpu.SideEffectType.DATAFLOW_SIDE_EFFECTING),
    )(*arrays, send_sems, recv_sems, after))


def _gather_plan(n):
    def plan(bufs):
        x, y, c = lax.axis_index("x"), lax.axis_index("y"), lax.axis_index("c")
        me = 2 * x + y
        return [(bufs[i].at[me], bufs[i].at[me], (x ^ fx, y ^ fy, c)) for fx, fy in _CHIP_FLIPS for i in range(n)]

    return plan


def _reduce_plan(specs, n_small):
    n = len(specs)

    def plan(arrs):
        x, y, c = lax.axis_index("x"), lax.axis_index("y"), lax.axis_index("c")
        slot = 4 * x + 2 * y + c
        out = []
        for fx, fy, fc in _DEVICE_FLIPS:
            peer = (x ^ fx, y ^ fy, c ^ fc)
            for i, (kind, (R, C)) in enumerate(specs):
                out.append((_piece(arrs[i], kind, R, C, 2 * peer[0] + peer[1], peer[2]), arrs[n + i].at[slot], peer))
            for s in range(n_small):
                out.append((arrs[2 * n + 2 * s], arrs[2 * n + 2 * s + 1].at[slot], peer))
        return out

    return plan


def _sibling_exchange(halves, name):
    n = len(halves)

    def body(*refs):
        ins, outs, send_sems, recv_sems = refs[:n], refs[n:2 * n], refs[2 * n], refs[2 * n + 1]
        sibling = (lax.axis_index("x"), lax.axis_index("y"), 1 - lax.axis_index("c"))
        copies = [pltpu.make_async_remote_copy(src_ref=ins[i], dst_ref=outs[i], send_sem=send_sems.at[i],
                                               recv_sem=recv_sems.at[i], device_id=sibling, device_id_type=MESH)
                  for i in range(n)]
        for cp in copies:
            cp.start()
        for cp in copies:
            cp.wait()

    dma = pltpu.SemaphoreType.DMA
    return pl.pallas_call(
        body, name=name,
        in_specs=[_ANY] * n, out_specs=[_ANY] * n,
        out_shape=[_sds(h.shape, h.dtype) for h in halves],
        scratch_shapes=[dma((n,)), dma((n,))],
    )(*halves)


_BIG = [("w_in", (1024, 1156), "slab"), ("w_out", (512, 1024), "rows"), ("w_ff1", (1024, 1024), "cols"),
        ("w_ff2", (1024, 1024), "rows"), ("w_ple_gate", (256, 1024), "rows"), ("w_ple_proj", (256, 256), "cols")]
_SMALL = [("norm_mix_g", (1, 1024)), ("gm_v_norm_g", (1, 1024)), ("gm_ws", (1, 8, 128, 128)), ("gm_bs", (1, 8, 128)),
          ("gm_out_norm_g", (1, 1024)), ("ssd_conv_w", (1, 4, 1536)), ("ssd_conv_b", (1, 1536)),
          ("ssd_dt_bias", (1, 16)), ("ssd_a_log", (1, 16)), ("ssd_d", (1, 16)), ("ssd_norm_g", (1, 1024)),
          ("norm_mlp_g", (1, 1024)), ("ple_norm_g", (1, 1024)), ("final_norm_g", (1024,))]


def _rows128(a):
    flat = a.reshape(-1)
    rows = -(-flat.shape[0] // 1024) * 8
    return jnp.pad(flat, (0, rows * 128 - flat.shape[0])).reshape(rows, 128)


def _pad_lanes(v, n=128):
    v = v.reshape(1, -1)
    return jnp.pad(v, ((0, 0), (0, n - v.shape[1])))


_SMALL_SHAPES = dict(_SMALL + [("loss", ())])
_BIG_SPECS = {n: (kind, shp) for n, shp, kind in _BIG}


class _Comm:
    def __init__(self, a, kh):
        self.a, self.kh = a, kh
        first, rest = _BIG[:1], _BIG[1:]
        self.bufs = {}
        for group, name in ((first, "cast_w_in"), (rest, "cast_rest")):
            cast = _cast_into_slot([a[n].reshape(shp) for n, shp, _ in group], kh, name)
            self.bufs.update({n: c for (n, _, _), c in zip(group, cast)})
        self.sent = []
        self.small_tot = {}

    def w_in(self):
        (g_win,), g_cw = self._gather_now()
        rest = [self.bufs[n] for n, _, _ in _BIG[1:]]
        plan = _gather_plan(len(rest))
        ssem, rsem, thru, token = _split_start("gather_start", rest, 3 * len(rest), plan, after=g_cw)
        self.gather = (plan, ssem, rsem, thru)
        wm, wdt = _assemble_w_in(g_win)
        return wm, wdt, jnp.concatenate([g_cw[k] for k in range(4)], axis=1), token

    def _gather_now(self):
        *bufs, g_cw = _weight_gather([self.bufs["w_in"]], self.a["ssd_conv_w"].reshape(4, 384))
        return bufs, g_cw

    def rest(self, after):
        plan, ssem, rsem, thru = self.gather
        g_wo, g_w1, g_w2, g_wg, g_wp = _split_wait("gather_wait", thru, ssem, rsem, plan, after)
        return g_wo.reshape(2048, D), g_w1, g_w2.reshape(DFF, D), g_wg.reshape(D, D), g_wp

    def send(self, tag, grads):
        big = [n for n, _, _ in _BIG if n in grads]
        small = [n for n in _SMALL_SHAPES if n in grads]
        parts = [_rows128(grads[n]) for n in small]
        rows = [s.shape[0] for s in parts]
        if not big:
            self.last_small = (tag, small, rows, jnp.concatenate(parts, axis=0))
            return None
        srcs = [grads[n] for n in big]
        lands = [lax.empty((8, _BIG_SPECS[n][1][0] // 2, _BIG_SPECS[n][1][1]), GRAD) for n in big]
        extra = []
        if small:
            pack = jnp.concatenate(parts, axis=0)
            extra = [pack, jnp.broadcast_to(pack, (8,) + pack.shape)]
        plan = _reduce_plan([_BIG_SPECS[n] for n in big], len(extra) // 2)
        n_copies = 7 * (len(big) + len(extra) // 2)
        ssem, rsem, thru, token = _split_start("reduce_start_" + tag, srcs + lands + extra, n_copies, plan)
        self.sent.append((tag, big, small, rows, plan, ssem, rsem, thru))
        return token

    def _unpack(self, tot, names, rows):
        o = 0
        for n, r in zip(names, rows):
            shp = _SMALL_SHAPES[n]
            cnt = 1
            for s in shp:
                cnt *= s
            self.small_tot[n] = tot[o:o + r].reshape(-1)[:cnt].reshape(shp)
            o += r

    def finish(self, after):
        a, results = self.a, {}

        def update(names, own, tag):
            other = _sibling_exchange([own[n] for n in names], "sibling_exchange_" + tag)
            items = [(a[n].reshape(_BIG_SPECS[n][1]), own[n], oth, a["m_" + n].reshape(_BIG_SPECS[n][1]),
                      a["v_" + n].reshape(_BIG_SPECS[n][1])) for n, oth in zip(names, other)]
            results.update(zip(names, _adamw_halves(items, "adamw_" + tag)))
            return results[names[-1]][1]

        own, early = {}, []
        for tag, big, small, rows, plan, ssem, rsem, thru in self.sent:
            if tag == self.sent[-1][0]:
                after = update(early, own, "early")
            arrs = _split_wait("reduce_wait_" + tag, thru, ssem, rsem, plan, after)
            nb_ = len(big)
            sums = _sum_slots([(arrs[nb_ + i], arrs[i]) + _BIG_SPECS[n] for i, n in enumerate(big)], self.kh,
                              "sum_" + tag)
            own.update(zip(big, sums))
            after = sums[-1]
            early += big
            if small:
                self._unpack(_sum_small(arrs[2 * nb_ + 1], "sum_small_" + tag), small, rows)
        after = update(self.sent[-1][1], own, "late")
        tag, small, rows, pack = self.last_small
        self._unpack(_sum_small(_small_exchange(pack, after), "sum_small_" + tag), small, rows)
        return results, dict(self.small_tot)


def _local_step(x, p, tgt, sm, comm, nb, tm):
    T = x.shape[0]
    wm, wdt, conv_w, token = comm.w_in()
    g_mix, gv, gout = sm["norm_mix_g"].reshape(1, D), sm["gm_v_norm_g"].reshape(1, D), sm["gm_out_norm_g"].reshape(1, D)
    ws = sm["gm_ws"].reshape(GM_HEADS, CH, CH)
    bst = jnp.pad(sm["gm_bs"].reshape(GM_HEADS, CH).T, ((0, 0), (0, 128 - GM_HEADS)))
    convw = jnp.pad(conv_w, ((0, 4), (0, 0)))
    convb = sm["ssd_conv_b"].reshape(1, CONV_CH)
    dtb, alog = _pad_lanes(sm["ssd_dt_bias"]), _pad_lanes(sm["ssd_a_log"])
    dskip = jnp.repeat(sm["ssd_d"].reshape(SSD_HEADS), SSD_P).reshape(1, 1024)
    ng, g_mlp, g_ple = sm["ssd_norm_g"].reshape(1, D), sm["norm_mlp_g"].reshape(1, D), sm["ple_norm_g"].reshape(1, D)
    gf = sm["final_norm_g"].reshape(1, D)
    head_of_lane = lax.broadcasted_iota(jnp.int32, (128, 1024), 1) // SSD_P
    ex = (lax.broadcasted_iota(jnp.int32, (128, 1024), 0) == head_of_lane).astype(BF16)
    ext = ex.T
    ltri = (lax.broadcasted_iota(jnp.int32, (CH, CH), 0) >= lax.broadcasted_iota(jnp.int32, (CH, CH), 1)).astype(F32)

    pz, pxbc, dtraw, xn, cat, uv = _inproj_gmlp(x, g_mix, wm, wdt, gv, ws, bst, gout, tm, token)
    cat, sall, conv = _ssd_fwd(pz, pxbc, dtraw, cat, convw, convb, dtb, alog, dskip, ng, ex, ltri, nb)
    wo, w1, w2, wg, wp = comm.rest(cat)
    h1, hn = _outproj(cat, wo, x, g_mlp, tm)
    hid = _ff1(hn, w1, min(T, 2 * tm))
    hp, dgl, dpe, dh2, dh2b, loss, d_gf, d_gple = _ff2_tail(hid, w2, h1, g_ple, p, tgt, wg, wp, gf, tm)

    d_wp = _matmul_tn(p, dpe, "dw_ple_proj", a_fn=lambda a: a.astype(MXU))
    d_wg = _matmul_tn(hp, dgl, "dw_ple_gate")
    d_w2 = _matmul_tn(hid, dh2b, "dw_ff2", a_fn=_sq)
    dpre = _ff2_bwd(dh2b, w2, hid, min(T, 2 * tm))
    d_w1 = _matmul_tn(hn, dpre, "dw_ff1")
    token = comm.send("a", {"w_ple_proj": d_wp, "w_ple_gate": d_wg, "w_ff2": d_w2, "w_ff1": d_w1})
    dh1, dh1b, d_gmlp = _ff1_bwd(dpre, w1, dh2, h1, g_mlp, tm, token)
    dcat = _outproj_bwd(dh1b, wo, min(T, 2 * tm))
    d_wo = _matmul_tn(cat, dh1b, "dw_out")
    duv, d_gv, d_ws, d_bst, d_gout, dxn_uv = _gmlp_bwd(uv, dcat, gv, ws, bst, gout, wm)
    token = comm.send("b", {
        "w_out": d_wo, "loss": loss[0:1, 0:1], "final_norm_g": d_gf, "ple_norm_g": d_gple, "norm_mlp_g": d_gmlp,
        "gm_v_norm_g": d_gv, "gm_ws": d_ws, "gm_bs": d_bst[:, :GM_HEADS].T, "gm_out_norm_g": d_gout})
    dssd, ddt, d_cw, d_cb, d_dtb, d_al, d_ds, d_ng = _ssd_bwd(
        pz, pxbc, conv, dtraw, sall, dcat, convw, dtb, alog, dskip, ng, ex, ltri, ext, nb, token)
    d_win = _split_dw_in(_matmul_tn(xn, duv, "dw_in_uv"), _matmul_tn(xn, dssd, "dw_in_ssd"),
                         _matmul_tn(xn, ddt, "dw_in_dt"))
    token = comm.send("c", {"w_in": d_win})
    dx, d_gmix = _inproj_bwd(dxn_uv, dssd, ddt, wm, wdt, dh1, x, g_mix, tm, token)
    comm.send("d", {"norm_mix_g": d_gmix, "ssd_conv_w": d_cw[0:4], "ssd_conv_b": d_cb, "ssd_dt_bias": d_dtb[:, :16],
                    "ssd_a_log": d_al[:, :16], "ssd_d": d_ds[:, :16], "ssd_norm_g": d_ng})
    return dx


def kernel(x, p, norm_mix_g, w_in, gm_v_norm_g, gm_ws, gm_bs, gm_out_norm_g, ssd_conv_w, ssd_conv_b, ssd_dt_bias, ssd_a_log, ssd_d, ssd_norm_g, w_out, norm_mlp_g, w_ff1, w_ff2, ple_norm_g, w_ple_gate, w_ple_proj, final_norm_g, loss_target, m_norm_mix_g, m_w_in, m_gm_v_norm_g, m_gm_ws, m_gm_bs, m_gm_out_norm_g, m_ssd_conv_w, m_ssd_conv_b, m_ssd_dt_bias, m_ssd_a_log, m_ssd_d, m_ssd_norm_g, m_w_out, m_norm_mlp_g, m_w_ff1, m_w_ff2, m_ple_norm_g, m_w_ple_gate, m_w_ple_proj, m_final_norm_g, v_norm_mix_g, v_w_in, v_gm_v_norm_g, v_gm_ws, v_gm_bs, v_gm_out_norm_g, v_ssd_conv_w, v_ssd_conv_b, v_ssd_dt_bias, v_ssd_a_log, v_ssd_d, v_ssd_norm_g, v_w_out, v_norm_mlp_g, v_w_ff1, v_w_ff2, v_ple_norm_g, v_w_ple_gate, v_w_ple_proj, v_final_norm_g):
    a = dict(locals())
    order = ["norm_mix_g", "w_in", "gm_v_norm_g", "gm_ws", "gm_bs", "gm_out_norm_g", "ssd_conv_w", "ssd_conv_b",
             "ssd_dt_bias", "ssd_a_log", "ssd_d", "ssd_norm_g", "w_out", "norm_mlp_g", "w_ff1", "w_ff2", "ple_norm_g",
             "w_ple_gate", "w_ple_proj", "final_norm_g"]
    chip = 2 * lax.axis_index("x") + lax.axis_index("y")
    nb, S = x.shape[0], x.shape[1]
    T = nb * S
    sm = {n: a[n] for n, _ in _SMALL if n != "ssd_conv_w"}
    comm = _Comm(a, jnp.stack([chip, lax.axis_index("c")]).astype(jnp.int32))
    dx = _local_step(x.reshape(T, D), p.reshape(T, DPLE), loss_target.reshape(T, D), sm, comm, nb, 512)
    big, g_out = comm.finish(dx)
    delta, new_m, new_v = {}, {}, {}
    for n, _, _ in _BIG:
        g_out[n], delta[n], new_m[n], new_v[n] = (r.reshape(a[n].shape) for r in big[n])
    g_out["ssd_conv_w"] = lax.dynamic_slice(g_out["ssd_conv_w"], (0, 0, chip * 384), (1, 4, 384))
    small_names = [n for n, _ in _SMALL]
    packs = [jnp.concatenate([_rows128(src(n)) for n in small_names], axis=0)
             for src in (lambda n: a[n], lambda n: g_out[n], lambda n: a["m_" + n], lambda n: a["v_" + n])]
    outs = _adamw(*packs, "adamw_small")
    o = 0
    for n in small_names:
        r = _rows128(a[n]).shape[0]
        cnt = a[n].size
        for dst, src in zip((delta, new_m, new_v), outs):
            dst[n] = src[o:o + r].reshape(-1)[:cnt].reshape(a[n].shape)
        o += r
    return (g_out["loss"], dx.reshape(x.shape), *[g_out[n] for n in order], *[delta[n] for n in order],
            *[new_m[n] for n in order], *[new_v[n] for n in order])
```

```python
import jax
import jax.numpy as jnp
from jax import lax
from jax.experimental import pallas as pl
from jax.experimental.pallas import tpu as pltpu

F32 = jnp.float32
BF16 = jnp.bfloat16
MXU = jnp.bfloat16
GRAD = jnp.bfloat16

D = 1024
CH = 128
GM_HEADS = 8
SSD_HEADS = 16
SSD_P = 64
CONV_CH = 1536
N_MAIN = 4608
DFF = 4096
DPLE = 256
EPS = 1e-6
NEG = -1e30

LR, B1, B2, ADAM_EPS, WD, STEP = 0.001, 0.9, 0.999, 1e-08, 0.01, 10

VMEM_LIMIT = 56 * 1024 * 1024
_SEQS_PER_STEP = 2
MESH = pl.DeviceIdType.MESH

INV_SQRT2 = 0.7071067811865476
INV_SQRT_2PI = 0.3989422804014327


def _cp(n_axes=1):
    return pltpu.CompilerParams(dimension_semantics=("arbitrary",) * n_axes, vmem_limit_bytes=VMEM_LIMIT)


def _dot(a, b):
    return jnp.dot(a, b, preferred_element_type=F32)


def _dot_nt(a, b):
    return lax.dot_general(a, b, (((1,), (1,)), ((), ())), preferred_element_type=F32)


def _dot_tn(a, b):
    return lax.dot_general(a, b, (((0,), (0,)), ((), ())), preferred_element_type=F32)


def _dot_hi(a, b):
    return jnp.dot(a, b, preferred_element_type=F32, precision=lax.Precision.HIGHEST)


def _dot_01(a, sel):
    hi = a.astype(BF16)
    lo = (a - hi.astype(F32)).astype(BF16)
    n = a.shape[0]
    r = _dot(jnp.concatenate([hi, lo], axis=0), sel)
    return r[0:n] + r[n:2 * n]


def _rows(tm, n, j=0):
    return pl.BlockSpec((tm, n), lambda i: (i, j))


def _const(shape):
    nd = len(shape)
    return pl.BlockSpec(shape, lambda *_: (0,) * nd)


def _sds(shape, dtype):
    return jax.ShapeDtypeStruct(shape, dtype)


def _rms(x):
    r = lax.rsqrt(jnp.mean(x * x, axis=-1, keepdims=True) + EPS)
    return x * r, r


def _rms_bwd(dy, xhat, r, g):
    dyg = dy * g
    return r * (dyg - xhat * jnp.mean(dyg * xhat, axis=-1, keepdims=True))


def _sigmoid(x):
    return 1.0 / (1.0 + jnp.exp(-x))


def _gelu(x):
    cdf = 0.5 * (1.0 + lax.erf(x * INV_SQRT2))
    pdf = jnp.exp(-0.5 * x * x) * INV_SQRT_2PI
    return x * cdf, cdf + x * pdf


def _softplus(x):
    e = jnp.exp(-jnp.abs(x))
    u = 1.0 + e
    log1p = jnp.where(u == 1.0, e, jnp.log(u) * e / (u - 1.0))
    return jnp.maximum(x, 0.0) + log1p


def _after(n_in, fn):
    def body(*refs):
        return fn(*refs[:n_in], *refs[n_in + 1:])

    return body


def _inproj_gmlp(x, g, wm, wdt, gv, ws, bst, gout, tm, after):
    T = x.shape[0]

    def body(x_ref, g_ref, wm_ref, wdt_ref, gv_ref, ws_ref, bst_ref, gout_ref,
             z_ref, xbc_ref, dt_ref, xn_ref, ya_ref, uv_ref):
        xh, _ = _rms(x_ref[...])
        xn = (xh * g_ref[...]).astype(MXU)
        xn_ref[...] = xn
        for n in range(4):
            uv_ref[:, n * 512:(n + 1) * 512] = _dot(xn, wm_ref[:, n * 512:(n + 1) * 512])
        for n in range(2):
            z_ref[:, n * 512:(n + 1) * 512] = _dot(xn, wm_ref[:, 2048 + n * 512:2048 + (n + 1) * 512])
        for n in range(3):
            xbc_ref[:, n * 512:(n + 1) * 512] = _dot(xn, wm_ref[:, 3072 + n * 512:3072 + (n + 1) * 512])
        dt_ref[...] = _dot(xn, wdt_ref[...])
        for k in range(tm // CH):
            rows = slice(k * CH, (k + 1) * CH)
            f = _gmlp_fwd_vals(uv_ref[rows, 0:1024], uv_ref[rows, 1024:2048], gv_ref[...], ws_ref, bst_ref[...],
                               gout_ref[...])
            ya_ref[rows, :] = f["out"].astype(MXU)

    return pl.pallas_call(
        _after(8, body), grid=(T // tm,), name="inproj_gmlp",
        in_specs=[_rows(tm, D), _const((1, D)), _const((D, N_MAIN)), _const((D, 128)), _const((1, 1024)),
                  _const((GM_HEADS, CH, CH)), _const((CH, 128)), _const((1, 1024)), _ANY],
        out_specs=[_rows(tm, 1024), _rows(tm, CONV_CH), _rows(tm, 128), _rows(tm, D), _rows(tm, 1024, 0),
                   _rows(tm, 2048)],
        out_shape=[_sds((T, 1024), F32), _sds((T, CONV_CH), F32), _sds((T, 128), F32), _sds((T, D), MXU),
                   _sds((T, 2048), MXU), _sds((T, 2048), F32)],
        compiler_params=_cp(),
    )(x, g, wm, wdt, gv, ws, bst, gout, after)


def _gmlp_fwd_vals(u, v, gv, ws_ref, bst, gout):
    ug, dug = _gelu(u)
    vg, dvg = _gelu(v)
    row = lax.broadcasted_iota(jnp.int32, (CH, CH), 0)
    col = lax.broadcasted_iota(jnp.int32, (CH, CH), 1)
    tril = row >= col
    ys, heads = [], []
    for h in range(GM_HEADS):
        sl = slice(h * 128, (h + 1) * 128)
        vhat, rv = _rms(vg[:, sl])
        vn = (vhat * gv[:, sl]).astype(MXU)
        wt = jnp.where(tril, ws_ref[h], 0.0)
        mixed = _dot(wt.astype(MXU), vn) + bst[:, h:h + 1]
        ys.append(ug[:, sl] * mixed)
        heads.append((vhat, rv, vn, wt, mixed))
    y = jnp.concatenate(ys, axis=1)
    yhat, ry = _rms(y)
    return dict(ug=ug, dug=dug, dvg=dvg, heads=heads, yhat=yhat, ry=ry, tril=tril, out=yhat * gout)


def _shifts_down(cur, halo):
    row8 = lax.broadcasted_iota(jnp.int32, (8, cur.shape[1]), 0)
    out = [cur]
    for j in (1, 2, 3):
        sh = pltpu.roll(cur, j, 0)
        top = jnp.where(row8 < j, pltpu.roll(halo, j, 0), sh[0:8])
        out.append(jnp.concatenate([top, sh[8:]], axis=0))
    return out


def _shifts_up(cur, halo):
    row8 = lax.broadcasted_iota(jnp.int32, (8, cur.shape[1]), 0)
    out = []
    for j in (1, 2, 3):
        sh = pltpu.roll(cur, CH - j, 0)
        bot = jnp.where(row8 + j >= 8, pltpu.roll(halo, 8 - j, 0), sh[CH - 8:CH])
        out.append(jnp.concatenate([sh[0:CH - 8], bot], axis=0))
    return out


def _conv(xbc, halo, convw, convb):
    sh = _shifts_down(xbc, halo)
    return convb + convw[3:4] * sh[0] + convw[2:3] * sh[1] + convw[1:2] * sh[2] + convw[0:1] * sh[3]


def _ssd_fwd_vals(z, conv, dtraw, dtb, alog, dskip, ng, ex, ltri, s_prev):
    sig_c = _sigmoid(conv)
    xa = conv * sig_c
    xs = xa[:, :1024]
    bm = [xa[:, 1024:1152], xa[:, 1152:1280]]
    cm = [xa[:, 1280:1408], xa[:, 1408:1536]]
    dtpre = dtraw + dtb
    dt = _softplus(dtpre)
    a_neg = -jnp.exp(alog)
    cs = _dot_hi(ltri, dt * a_neg)
    cst = cs.T
    last = cs[CH - 1:CH]
    ecs = jnp.exp(cs)
    dec = jnp.exp(last - cs)
    spread = _dot_01(jnp.concatenate([dt, ecs, dec], axis=0), ex)
    dte, ecse, dece = spread[0:CH], spread[CH:2 * CH], spread[2 * CH:3 * CH]
    cde = ecse[CH - 1:CH]
    de = dskip
    xdt = xs * dte
    row = lax.broadcasted_iota(jnp.int32, (CH, CH), 0)
    col = lax.broadcasted_iota(jnp.int32, (CH, CH), 1)
    tril = row >= col
    lo = col < SSD_P
    bmb = [b.astype(MXU) for b in bm]
    cmb = [c.astype(MXU) for c in cm]
    mg = [_dot_nt(cmb[g], bmb[g]) for g in range(2)]
    yd, lms, whs = [], [], []
    for q in range(8):
        g = q // 4
        xq = xdt[:, q * 128:(q + 1) * 128]
        acc = None
        for hh in range(2):
            h = 2 * q + hh
            seg = cs[:, h:h + 1] - cst[h:h + 1, :]
            lm = jnp.exp(jnp.where(tril, seg, NEG))
            wh = (mg[g] * lm).astype(MXU)
            xm = jnp.where(lo if hh == 0 else ~lo, xq, 0.0).astype(MXU)
            part = _dot(wh, xm)
            acc = part if acc is None else acc + part
            lms.append(lm)
            whs.append(wh)
        yd.append(acc)
    yd = jnp.concatenate(yd, axis=1)
    sb = s_prev.astype(MXU)
    yo = jnp.concatenate([_dot(cmb[g], sb[:, g * 512:(g + 1) * 512]) for g in range(2)], axis=1) * ecse
    xdec = (xdt * dece).astype(MXU)
    states = jnp.concatenate([_dot_tn(bmb[g], xdec[:, g * 512:(g + 1) * 512]) for g in range(2)], axis=1)
    s_next = s_prev * cde + states
    ypre = yd + yo + de * xs
    sig_z = _sigmoid(z)
    yg = ypre * z * sig_z
    outs, yhat, rr = [], [], []
    for g in range(2):
        sl = slice(g * 512, (g + 1) * 512)
        yh, r = _rms(yg[:, sl])
        yhat.append(yh)
        rr.append(r)
        outs.append(yh * ng[:, sl])
    return dict(sig_c=sig_c, xa=xa, xs=xs, bmb=bmb, cmb=cmb, dtpre=dtpre, dt=dt, a_neg=a_neg,
                cs=cs, last=last, ecs=ecs, dec=dec, dte=dte, ecse=ecse, dece=dece, cde=cde, de=de, xdt=xdt,
                mg=mg, lms=lms, whs=whs, lo=lo, yo=yo, sb=sb, xdec=xdec, s_next=s_next, ypre=ypre, sig_z=sig_z,
                yhat=yhat, rr=rr, out=jnp.concatenate(outs, axis=1))


def _ssd_fwd(pz, pxbc, dtraw, cat, convw, convb, dtb, alog, dskip, ng, ex, ltri, nb):
    T = pz.shape[0]
    S = T // nb
    nch = S // CH
    ns = _SEQS_PER_STEP if nb % _SEQS_PER_STEP == 0 else 1

    def body(z_ref, xbc_ref, halo_ref, dt_ref, cw_ref, cb_ref, dtb_ref, al_ref, ds_ref, ng_ref, ex_ref, lt_ref,
             cat_in_ref, yb_ref, sall_ref, conv_ref, s_ref):
        del cat_in_ref
        c = pl.program_id(1)

        @pl.when(c == 0)
        def _():
            s_ref[...] = jnp.zeros_like(s_ref)

        for i in range(ns):
            halo = jnp.where(c == 0, 0.0, halo_ref[i])
            s_prev = s_ref[i]
            sall_ref[i, 0] = s_prev
            conv = _conv(xbc_ref[i], halo, cw_ref[...], cb_ref[...])
            conv_ref[i] = conv
            f = _ssd_fwd_vals(z_ref[i], conv, dt_ref[i], dtb_ref[...], al_ref[...], ds_ref[...], ng_ref[...],
                              ex_ref[...], lt_ref[...], s_prev)
            s_ref[i] = f["s_next"]
            yb_ref[i] = f["out"].astype(MXU)

    def seq(width, col=0):
        return pl.BlockSpec((ns, CH, width), lambda b, c: (b, c, col))

    cat, sall, conv = pl.pallas_call(
        body, grid=(nb // ns, nch), name="ssd_fwd",
        in_specs=[seq(1024), seq(CONV_CH),
                  pl.BlockSpec((ns, 8, CONV_CH), lambda b, c: (b, jnp.maximum(c * (CH // 8) - 1, 0), 0)),
                  seq(128),
                  _const((8, CONV_CH)), _const((1, CONV_CH)), _const((1, 128)), _const((1, 128)), _const((1, 1024)),
                  _const((1, 1024)), _const((128, 1024)), _const((CH, CH)), _ANY],
        out_specs=[seq(1024, 1), pl.BlockSpec((ns, 1, 128, 1024), lambda b, c: (b, c, 0, 0)), seq(CONV_CH)],
        out_shape=[_sds((nb, S, 2048), MXU), _sds((nb, nch, 128, 1024), F32), _sds((nb, S, CONV_CH), F32)],
        scratch_shapes=[pltpu.VMEM((ns, 128, 1024), F32)],
        input_output_aliases={12: 0},
        compiler_params=_cp(2),
    )(pz.reshape(nb, S, 1024), pxbc.reshape(nb, S, CONV_CH), pxbc.reshape(nb, S, CONV_CH), dtraw.reshape(nb, S, 128),
      convw, convb, dtb, alog, dskip, ng, ex, ltri, cat.reshape(nb, S, 2048))
    return cat.reshape(T, 2048), sall, conv.reshape(T, CONV_CH)


def _outproj(cat, wo, x, g, tm):
    T = x.shape[0]

    def body(cat_ref, wo_ref, x_ref, g_ref, h1_ref, hn_ref):
        h1 = x_ref[...] + _dot(cat_ref[...], wo_ref[...])
        h1_ref[...] = h1
        hn_ref[...] = (_rms(h1)[0] * g_ref[...]).astype(MXU)

    return pl.pallas_call(
        body, grid=(T // tm,), name="outproj",
        in_specs=[_rows(tm, 2048), _const((2048, D)), _rows(tm, D), _const((1, D))],
        out_specs=[_rows(tm, D), _rows(tm, D)],
        out_shape=[_sds((T, D), F32), _sds((T, D), MXU)],
        compiler_params=_cp(),
    )(cat, wo, x, g)


def _ff1(hn, w1, tm):
    T = hn.shape[0]

    def body(hn_ref, w1_ref, hid_ref):
        hn_v = hn_ref[...]
        for n in range(4):
            hid_ref[:, n * 1024:(n + 1) * 1024] = jnp.maximum(_dot(hn_v, w1_ref[n]), 0.0).astype(MXU)

    return pl.pallas_call(
        body, grid=(T // tm,), name="ff1",
        in_specs=[_rows(tm, D), _const((4, D, 1024))],
        out_specs=_rows(tm, DFF),
        out_shape=_sds((T, DFF), MXU),
        compiler_params=_cp(),
    )(hn, w1)


def _sq(hid):
    h = hid.astype(F32)
    return (h * h).astype(MXU)


def _ff2_tail(hid, w2, h1, g_ple, p, tgt, wg, wp, gf, tm):
    T = h1.shape[0]

    def body(hid_ref, w2_ref, h1_ref, g_ref, p_ref, t_ref, wg_ref, wp_ref, gf_ref,
             hp_ref, dgl_ref, dpe_ref, dh2_ref, dh2b_ref, loss_ref, dgf_ref, dg_ref):
        @pl.when(pl.program_id(0) == 0)
        def _():
            loss_ref[...] = jnp.zeros_like(loss_ref)
            dgf_ref[...] = jnp.zeros_like(dgf_ref)
            dg_ref[...] = jnp.zeros_like(dg_ref)

        h2 = h1_ref[...] + _dot(_sq(hid_ref[...]), w2_ref[...])
        h2h, r2 = _rms(h2)
        g_ple = g_ref[...]
        hp = (h2h * g_ple).astype(MXU)
        hp_ref[...] = hp
        gate = _sigmoid(_dot(hp, wg_ref[...]))
        pb = p_ref[...].astype(MXU)
        pe = jnp.concatenate([_dot(pb, wp_ref[k]) for k in range(4)], axis=1)
        h3 = h2 + gate * pe
        hh, r = _rms(h3)
        gf = gf_ref[...]
        diff = hh * gf - t_ref[...]
        loss_ref[...] += 0.5 * jnp.sum(jnp.mean(diff * diff, axis=-1, keepdims=True))
        dout = diff * (1.0 / D)
        dgf_ref[...] += jnp.sum(dout * hh, axis=0, keepdims=True)
        dh3 = _rms_bwd(dout, hh, r, gf)
        dgl = (dh3 * pe * gate * (1.0 - gate)).astype(MXU)
        dgl_ref[...] = dgl
        dpe_ref[...] = (dh3 * gate).astype(MXU)
        dhp = _dot_nt(dgl, wg_ref[...])
        dg_ref[...] += jnp.sum(dhp * h2h, axis=0, keepdims=True)
        dh2 = dh3 + _rms_bwd(dhp, h2h, r2, g_ple)
        dh2_ref[...] = dh2
        dh2b_ref[...] = dh2.astype(MXU)

    return pl.pallas_call(
        body, grid=(T // tm,), name="ff2_tail",
        in_specs=[_rows(tm, DFF), _const((DFF, D)), _rows(tm, D), _const((1, D)), _rows(tm, DPLE), _rows(tm, D),
                  _const((D, D)), _const((4, DPLE, 256)), _const((1, D))],
        out_specs=[_rows(tm, D), _rows(tm, D), _rows(tm, D), _rows(tm, D), _rows(tm, D), _const((8, 128)),
                   _const((1, D)), _const((1, D))],
        out_shape=[_sds((T, D), MXU), _sds((T, D), MXU), _sds((T, D), MXU), _sds((T, D), F32), _sds((T, D), MXU),
                   _sds((8, 128), F32), _sds((1, D), F32), _sds((1, D), F32)],
        compiler_params=_cp(),
    )(hid, w2, h1, g_ple, p, tgt, wg, wp, gf)


def _ff2_bwd(dh2b, w2, hid, tm):
    T = hid.shape[0]

    def body(dh2b_ref, w2_ref, hid_ref, dpre_ref):
        d = dh2b_ref[...]
        for n in range(DFF // 1024):
            sl = slice(n * 1024, (n + 1) * 1024)
            da = _dot_nt(d, w2_ref[sl, :])
            dpre_ref[:, sl] = (2.0 * da * hid_ref[:, sl].astype(F32)).astype(MXU)

    return pl.pallas_call(
        body, grid=(T // tm,), name="ff2_bwd",
        in_specs=[_rows(tm, D), _const((DFF, D)), _rows(tm, DFF)],
        out_specs=_rows(tm, DFF),
        out_shape=_sds((T, DFF), MXU),
        compiler_params=_cp(),
    )(dh2b, w2, hid)


def _ff1_bwd(dpre, w1, dh2, h1, g, tm, after):
    T = h1.shape[0]

    def body(dpre_ref, w1_ref, dh2_ref, h1_ref, g_ref, dh1_ref, dh1b_ref, dg_ref):
        @pl.when(pl.program_id(0) == 0)
        def _():
            dg_ref[...] = jnp.zeros_like(dg_ref)

        dhn = _dot_nt(dpre_ref[:, 0:1024], w1_ref[0])
        for k in range(1, 4):
            dhn = dhn + _dot_nt(dpre_ref[:, k * 1024:(k + 1) * 1024], w1_ref[k])
        hh, r = _rms(h1_ref[...])
        dg_ref[...] += jnp.sum(dhn * hh, axis=0, keepdims=True)
        dh1 = dh2_ref[...] + _rms_bwd(dhn, hh, r, g_ref[...])
        dh1_ref[...] = dh1
        dh1b_ref[...] = dh1.astype(MXU)

    return pl.pallas_call(
        _after(5, body), grid=(T // tm,), name="ff1_bwd",
        in_specs=[_rows(tm, DFF), _const((4, D, 1024)), _rows(tm, D), _rows(tm, D), _const((1, D)), _ANY],
        out_specs=[_rows(tm, D), _rows(tm, D), _const((1, D))],
        out_shape=[_sds((T, D), F32), _sds((T, D), MXU), _sds((1, D), F32)],
        compiler_params=_cp(),
    )(dpre, w1, dh2, h1, g, after)


def _outproj_bwd(dh1b, wo, tm):
    T = dh1b.shape[0]

    def body(d_ref, wo_ref, dcat_ref):
        d = d_ref[...]
        dcat_ref[:, 0:1024] = _dot_nt(d, wo_ref[0:1024, :])
        dcat_ref[:, 1024:2048] = _dot_nt(d, wo_ref[1024:2048, :])

    return pl.pallas_call(
        body, grid=(T // tm,), name="outproj_bwd",
        in_specs=[_rows(tm, D), _const((2048, D))],
        out_specs=_rows(tm, 2048),
        out_shape=_sds((T, 2048), F32),
        compiler_params=_cp(),
    )(dh1b, wo)


def _gmlp_bwd(uv, dcat, gv, ws, bst, gout, wm):
    T = uv.shape[0]
    nck = 2 if T % (2 * CH) == 0 else 1
    tb = nck * CH

    def body(uv_ref, dya_ref, gv_ref, ws_ref, bst_ref, gout_ref, wuv_ref, duv_ref, dgv_ref, dws_ref, dbst_ref,
             dgo_ref, dxn_ref):
        @pl.when(pl.program_id(0) == 0)
        def _():
            dgv_ref[...] = jnp.zeros_like(dgv_ref)
            dws_ref[...] = jnp.zeros_like(dws_ref)
            dbst_ref[...] = jnp.zeros_like(dbst_ref)
            dgo_ref[...] = jnp.zeros_like(dgo_ref)

        for k in range(nck):
            chunk(slice(k * CH, (k + 1) * CH), uv_ref, dya_ref, gv_ref, ws_ref, bst_ref, gout_ref, duv_ref,
                  dgv_ref, dws_ref, dbst_ref, dgo_ref)
        dxn_ref[...] = _dot_nt(duv_ref[...], wuv_ref[...])

    def chunk(rows, uv_ref, dya_ref, gv_ref, ws_ref, bst_ref, gout_ref, duv_ref, dgv_ref, dws_ref, dbst_ref,
              dgo_ref):
        gv = gv_ref[...]
        f = _gmlp_fwd_vals(uv_ref[rows, 0:1024], uv_ref[rows, 1024:2048], gv, ws_ref, bst_ref[...], gout_ref[...])
        dya = dya_ref[rows, :]
        dgo_ref[...] += jnp.sum(dya * f["yhat"], axis=0, keepdims=True)
        dy = _rms_bwd(dya, f["yhat"], f["ry"], gout_ref[...])
        lane = lax.broadcasted_iota(jnp.int32, (CH, 128), 1)
        dbs = jnp.zeros((CH, 128), F32)
        dug, dvg, dgvs = [], [], []
        for h in range(GM_HEADS):
            sl = slice(h * 128, (h + 1) * 128)
            vhat, rv, vn, wt, mixed = f["heads"][h]
            dyh = dy[:, sl]
            dug.append(dyh * mixed)
            dmixed = dyh * f["ug"][:, sl]
            dmb = dmixed.astype(MXU)
            dws_ref[h] += jnp.where(f["tril"], _dot_nt(dmb, vn), 0.0)
            dbs = dbs + jnp.where(lane == h, jnp.sum(dmixed, axis=1, keepdims=True), 0.0)
            dvn = _dot_tn(wt.astype(MXU), dmb)
            dgvs.append(jnp.sum(dvn * vhat, axis=0, keepdims=True))
            dvg.append(_rms_bwd(dvn, vhat, rv, gv[:, sl]))
        dbst_ref[...] += dbs
        dgv_ref[...] += jnp.concatenate(dgvs, axis=1)
        duv_ref[rows, 0:1024] = (jnp.concatenate(dug, axis=1) * f["dug"]).astype(MXU)
        duv_ref[rows, 1024:2048] = (jnp.concatenate(dvg, axis=1) * f["dvg"]).astype(MXU)

    return pl.pallas_call(
        body, grid=(T // tb,), name="gmlp_bwd",
        in_specs=[_rows(tb, 2048), _rows(tb, 1024, 0), _const((1, 1024)),
                  _const((GM_HEADS, CH, CH)), _const((CH, 128)), _const((1, 1024)), _const((D, 2048))],
        out_specs=[_rows(tb, 2048), _const((1, 1024)), _const((GM_HEADS, CH, CH)), _const((CH, 128)),
                   _const((1, 1024)), _rows(tb, D)],
        out_shape=[_sds((T, 2048), MXU), _sds((1, 1024), F32), _sds((GM_HEADS, CH, CH), F32), _sds((CH, 128), F32),
                   _sds((1, 1024), F32), _sds((T, D), F32)],
        compiler_params=_cp(),
    )(uv, dcat, gv, ws, bst, gout, wm)


def _ssd_bwd(pz, pxbc, conv, dtraw, sall, dcat, convw, dtb, alog, dskip, ng, ex, ltri, ext, nb, after):
    T = pz.shape[0]
    S = T // nb
    nch = S // CH
    ns = _SEQS_PER_STEP if nb % _SEQS_PER_STEP == 0 else 1

    def seq(width, col=0):
        return pl.BlockSpec((ns, CH, width), lambda b, c: (b, nch - 1 - c, col))

    in_specs = [
        seq(1024), seq(CONV_CH), seq(CONV_CH), seq(128),
        _const((8, CONV_CH)), _const((1, 128)), _const((1, 128)), _const((1, 1024)),
        _const((1, 1024)), _const((128, 1024)), _const((CH, CH)),
        _const((1024, 128)),
        pl.BlockSpec((ns, 1, 128, 1024), lambda b, c: (b, nch - 1 - c, 0, 0)),
        seq(1024, 1),
        _ANY,
    ]

    def body(z_ref, xbc_ref, conv_ref, dt_ref, cw_ref, dtb_ref, al_ref, ds_ref, ng_ref, ex_ref, lt_ref,
             ext_ref, sall_ref, dyb_ref,
             dssd_ref, ddt_ref, dcw_ref, dcb_ref, ddtb_ref, dal_ref, dds_ref, dng_ref,
             dst_ref, dnext_ref, ddse_ref):
        b = pl.program_id(0)
        c = pl.program_id(1)

        @pl.when((b == 0) & (c == 0))
        def _():
            for r in (dcw_ref, dcb_ref, ddtb_ref, dal_ref, dds_ref, dng_ref, ddse_ref):
                r[...] = jnp.zeros_like(r)

        @pl.when(c == 0)
        def _():
            dst_ref[...] = jnp.zeros_like(dst_ref)
            dnext_ref[...] = jnp.zeros_like(dnext_ref)

        ex = ex_ref[...]
        ext = ext_ref[...]
        cw = cw_ref[...]
        ng = ng_ref[...]
        for i in range(ns):
            one_chunk(i, ex, ext, cw, ng, z_ref, xbc_ref, conv_ref, dt_ref, dtb_ref, al_ref, ds_ref, lt_ref, sall_ref,
                      dyb_ref, dssd_ref, ddt_ref, dcw_ref, dcb_ref, ddtb_ref, dal_ref, dng_ref, dst_ref, dnext_ref,
                      ddse_ref)

        @pl.when((b == nb // ns - 1) & (c == nch - 1))
        def _():
            dds_ref[...] = _dot_01(jnp.broadcast_to(ddse_ref[...], (8, 1024)), ext)[0:1]

    def one_chunk(i, ex, ext, cw, ng, z_ref, xbc_ref, conv_ref, dt_ref, dtb_ref, al_ref, ds_ref, lt_ref, sall_ref,
                  dyb_ref, dssd_ref, ddt_ref, dcw_ref, dcb_ref, ddtb_ref, dal_ref, dng_ref, dst_ref, dnext_ref,
                  ddse_ref):
        z = z_ref[i]
        s_prev = sall_ref[i, 0]
        conv = conv_ref[i]
        f = _ssd_fwd_vals(z, conv, dt_ref[i], dtb_ref[...], al_ref[...], ds_ref[...], ng, ex, lt_ref[...], s_prev)
        xs, xdt, cs, dec, dt = f["xs"], f["xdt"], f["cs"], f["dec"], f["dt"]
        dyb = dyb_ref[i]
        dyg, dngs = [], []
        for g in range(2):
            sl = slice(g * 512, (g + 1) * 512)
            dngs.append(jnp.sum(dyb[:, sl] * f["yhat"][g], axis=0, keepdims=True))
            dyg.append(_rms_bwd(dyb[:, sl], f["yhat"][g], f["rr"][g], ng[:, sl]))
        dng_ref[...] += jnp.concatenate(dngs, axis=1)
        dyg = jnp.concatenate(dyg, axis=1)
        sig_z = f["sig_z"]
        silu_z = z * sig_z
        dy = dyg * silu_z
        dz = dyg * f["ypre"] * (sig_z + silu_z * (1.0 - sig_z))
        ddse_ref[...] += jnp.sum(dy * xs, axis=0, keepdims=True)
        dxs = dy * f["de"]
        dye = dy * f["ecse"]
        dyeb = dye.astype(MXU)
        dst = dst_ref[i]
        dstb = dst.astype(MXU)
        bmb, cmb, sb, xdec = f["bmb"], f["cmb"], f["sb"], f["xdec"]
        u = jnp.concatenate([_dot(bmb[g], dstb[:, g * 512:(g + 1) * 512]) for g in range(2)], axis=1)
        dxdt = [u[:, q * 128:(q + 1) * 128] * f["dece"][:, q * 128:(q + 1) * 128] for q in range(8)]
        per_head = _dot_01(jnp.concatenate(
            [dy * f["yo"], u * xdt, jnp.broadcast_to(jnp.sum(dst * s_prev, axis=0, keepdims=True), (8, 1024))],
            axis=0), ext)
        dcs = per_head[0:CH]
        t = per_head[CH:2 * CH] * dec
        dcd = per_head[2 * CH:2 * CH + 1]
        row = lax.broadcasted_iota(jnp.int32, (CH, 128), 0)
        lane = lax.broadcasted_iota(jnp.int32, (CH, 128), 1)
        cd = jnp.exp(f["last"])
        dcs = dcs - t + jnp.where(row == CH - 1, jnp.sum(t, axis=0, keepdims=True) + dcd * cd, 0.0)
        dcst = jnp.zeros((128, CH), F32)
        lo = f["lo"]
        dbm, dcm, ds_prev = [], [], []
        for g in range(2):
            sl = slice(g * 512, (g + 1) * 512)
            dmg = jnp.zeros((CH, CH), F32)
            for q in range(4 * g, 4 * g + 4):
                dyq = dy[:, q * 128:(q + 1) * 128]
                xq = xdt[:, q * 128:(q + 1) * 128].astype(MXU)
                for hh in range(2):
                    h = 2 * q + hh
                    m = lo if hh == 0 else ~lo
                    dym = jnp.where(m, dyq, 0.0).astype(MXU)
                    gh = _dot_nt(dym, xq)
                    gl = gh * f["lms"][h]
                    dmg = dmg + gl
                    qh = gl * f["mg"][g]
                    dcs = dcs + jnp.where(lane == h, jnp.sum(qh, axis=1, keepdims=True), 0.0)
                    dcst = dcst - jnp.where(row == h, jnp.sum(qh, axis=0, keepdims=True), 0.0)
                    dxdt[q] = dxdt[q] + _dot_tn(f["whs"][h], dym)
            dmgb = dmg.astype(MXU)
            dcm.append(_dot(dmgb, bmb[g]) + _dot_nt(dyeb[:, sl], sb[:, sl]))
            dbm.append(_dot_tn(dmgb, cmb[g]) + _dot_nt(xdec[:, sl], dstb[:, sl]))
            ds_prev.append(_dot_tn(cmb[g], dyeb[:, sl]))
        dst_ref[i] = jnp.concatenate(ds_prev, axis=1) + dst * f["cde"]
        dcs = dcs + dcst.T
        da = _dot_hi(lt_ref[...].T, dcs)
        dxdt = jnp.concatenate(dxdt, axis=1)
        a_neg = f["a_neg"]
        ddt = da * a_neg + _dot_01(dxdt * xs, ext)
        dal_ref[...] += jnp.sum(da * dt, axis=0, keepdims=True) * a_neg
        dxs = dxs + dxdt * f["dte"]
        ddtraw = jnp.where(lane < SSD_HEADS, ddt * _sigmoid(f["dtpre"]), 0.0)
        ddtb_ref[...] += jnp.sum(ddtraw, axis=0, keepdims=True)
        ddt_ref[i] = ddtraw.astype(MXU)
        dxa = jnp.concatenate([dxs, dbm[0], dbm[1], dcm[0], dcm[1]], axis=1)
        sig_c = f["sig_c"]
        dconv = dxa * (sig_c + f["xa"] * (1.0 - sig_c))
        dcb_ref[...] += jnp.sum(dconv, axis=0, keepdims=True)
        xbc = xbc_ref[i]
        dcw_ref[3:4, :] += jnp.sum(dconv * xbc, axis=0, keepdims=True)
        dxbc = cw[3:4] * dconv
        for j, up in zip((1, 2, 3), _shifts_up(dconv, dnext_ref[i])):
            dcw_ref[3 - j:4 - j, :] += jnp.sum(up * xbc, axis=0, keepdims=True)
            dxbc = dxbc + cw[3 - j:4 - j] * up
        dnext_ref[i] = dconv[0:8]
        dssd_ref[i, :, 0:1024] = dz.astype(MXU)
        dssd_ref[i, :, 1024:2560] = dxbc.astype(MXU)

    dssd, ddt, *small = pl.pallas_call(
        _after(14, body), grid=(nb // ns, nch), name="ssd_bwd",
        in_specs=in_specs,
        out_specs=[seq(2560), seq(128),
                   _const((8, CONV_CH)), _const((1, CONV_CH)), _const((1, 128)), _const((1, 128)), _const((1, 128)),
                   _const((1, 1024))],
        out_shape=[_sds((nb, S, 2560), MXU), _sds((nb, S, 128), MXU), _sds((8, CONV_CH), F32),
                   _sds((1, CONV_CH), F32), _sds((1, 128), F32), _sds((1, 128), F32), _sds((1, 128), F32),
                   _sds((1, 1024), F32)],
        scratch_shapes=[pltpu.VMEM((ns, 128, 1024), F32), pltpu.VMEM((ns, 8, CONV_CH), F32),
                        pltpu.VMEM((1, 1024), F32)],
        compiler_params=_cp(2),
    )(pz.reshape(nb, S, 1024), pxbc.reshape(nb, S, CONV_CH), conv.reshape(nb, S, CONV_CH), dtraw.reshape(nb, S, 128),
      convw, dtb, alog, dskip, ng, ex, ltri, ext, sall, dcat.reshape(nb, S, 2048), after)
    return (dssd.reshape(T, 2560), ddt.reshape(T, 128), *small)


def _inproj_bwd(dxn_uv, dssd, ddt, wm, wdt, dh1, x, g, tm, after):
    T = x.shape[0]

    def body(dxnuv_ref, dssd_ref, ddt_ref, wm_ref, wdt_ref, dh1_ref, x_ref, g_ref, dx_ref, dg_ref):
        @pl.when(pl.program_id(0) == 0)
        def _():
            dg_ref[...] = jnp.zeros_like(dg_ref)

        dxn = (dxnuv_ref[...] + _dot_nt(dssd_ref[...], wm_ref[:, 2048:N_MAIN])
               + _dot_nt(ddt_ref[...], wdt_ref[...]))
        xh, r = _rms(x_ref[...])
        dg_ref[...] += jnp.sum(dxn * xh, axis=0, keepdims=True)
        dx_ref[...] = dh1_ref[...] + _rms_bwd(dxn, xh, r, g_ref[...])

    return pl.pallas_call(
        _after(8, body), grid=(T // tm,), name="inproj_bwd",
        in_specs=[_rows(tm, D), _rows(tm, 2560), _rows(tm, 128), _const((D, N_MAIN)), _const((D, 128)),
                  _rows(tm, D), _rows(tm, D), _const((1, D)), _ANY],
        out_specs=[_rows(tm, D), _const((1, D))],
        out_shape=[_sds((T, D), F32), _sds((1, D), F32)],
        compiler_params=_cp(),
    )(dxn_uv, dssd, ddt, wm, wdt, dh1, x, g, after)


def _matmul_tn(a, b, name, a_fn=None):
    T, M = a.shape
    N = b.shape[1]
    tm = min(M, 1024)
    tn = 1280 if N == 2560 else min(N, 1024)
    tk = min(T, 2048)

    def body(a_ref, b_ref, o_ref, acc_ref):
        k = pl.program_id(2)

        @pl.when(k == 0)
        def _():
            acc_ref[...] = jnp.zeros_like(acc_ref)

        av = a_ref[...]
        if a_fn is not None:
            av = a_fn(av)
        acc_ref[...] += _dot_tn(av, b_ref[...])

        @pl.when(k == T // tk - 1)
        def _():
            o_ref[...] = acc_ref[...].astype(o_ref.dtype)

    return pl.pallas_call(
        body, grid=(M // tm, N // tn, T // tk), name=name,
        in_specs=[pl.BlockSpec((tk, tm), lambda i, j, k: (k, i)), pl.BlockSpec((tk, tn), lambda i, j, k: (k, j))],
        out_specs=pl.BlockSpec((tm, tn), lambda i, j, k: (i, j)),
        out_shape=_sds((M, N), GRAD),
        scratch_shapes=[pltpu.VMEM((tm, tn), F32)],
        compiler_params=_cp(3),
    )(a, b)


def _adamw_vals(w, g, m, v):
    m = B1 * m + (1.0 - B1) * g
    v = B2 * v + (1.0 - B2) * (g * g)
    m_hat = m / (1.0 - B1 ** STEP)
    v_hat = v / (1.0 - B2 ** STEP)
    return -LR * (m_hat / (jnp.sqrt(v_hat) + ADAM_EPS) + WD * w), m, v


def _adamw(w, g, m, v, name):
    R, C = w.shape
    tr = 256 if R % 256 == 0 else R

    def body(w_ref, g_ref, m_ref, v_ref, d_ref, mo_ref, vo_ref):
        d_ref[...], mo_ref[...], vo_ref[...] = _adamw_vals(w_ref[...], g_ref[...], m_ref[...], v_ref[...])

    spec = _rows(tr, C)
    return pl.pallas_call(
        body, grid=(R // tr,), name=name,
        in_specs=[spec] * 4, out_specs=[spec] * 3, out_shape=[_sds((R, C), F32)] * 3,
        compiler_params=_cp(),
    )(w, g, m, v)


_PARTS = 4


def _adamw_halves(items, name):
    n = len(items)

    def body(*refs):
        mine = (pl.program_id(0) // _PARTS) == lax.axis_index("c")
        for k in range(n):
            w_ref, own_ref, oth_ref, m_ref, v_ref = refs[5 * k:5 * k + 5]
            g_ref, d_ref, mo_ref, vo_ref = refs[5 * n + 4 * k:5 * n + 4 * k + 4]
            g = jnp.where(mine, own_ref[...], oth_ref[...])
            g_ref[...] = g
            d_ref[...], mo_ref[...], vo_ref[...] = _adamw_vals(w_ref[...], g, m_ref[...], v_ref[...])

    in_specs, out_specs, out_shape = [], [], []
    for w, *_ in items:
        R, C = w.shape
        full = _rows(R // (2 * _PARTS), C)
        part = pl.BlockSpec((R // (2 * _PARTS), C), lambda i: (i % _PARTS, 0))
        in_specs += [full, part, part, full, full]
        out_specs += [full] * 4
        out_shape += [_sds((R, C), F32)] * 4
    res = pl.pallas_call(
        body, grid=(2 * _PARTS,), name=name, in_specs=in_specs, out_specs=out_specs, out_shape=out_shape,
        compiler_params=_cp(),
    )(*[a for item in items for a in item])
    return [tuple(res[4 * k:4 * k + 4]) for k in range(n)]


_TJ = 128


def _adamw_transposed(w, own, other, m, v, name):
    C, _, R = w.shape

    def body(w_ref, own_ref, oth_ref, m_ref, v_ref, g_ref, d_ref, mo_ref, vo_ref):
        first = lax.axis_index("c") == 0
        g = jnp.concatenate([jnp.where(first, own_ref[...], oth_ref[...]),
                             jnp.where(first, oth_ref[...], own_ref[...])], axis=0).T
        d, mo, vo = _adamw_vals(w_ref[:, 0, :], g, m_ref[:, 0, :], v_ref[:, 0, :])
        for ref, val in ((g_ref, g), (d_ref, d), (mo_ref, mo), (vo_ref, vo)):
            ref[:, 0, :] = val

    cols = pl.BlockSpec((_TJ, 1, R), lambda j: (j, 0, 0))
    half = pl.BlockSpec((R // 2, _TJ), lambda j: (0, j))
    return pl.pallas_call(
        body, grid=(pl.cdiv(C, _TJ),), name=name,
        in_specs=[cols, half, half, cols, cols], out_specs=[cols] * 4, out_shape=[_sds((C, 1, R), F32)] * 4,
        compiler_params=_cp(),
    )(w, own, other, m, v)


def _sum_small(slots, name):
    nd, rows, C = slots.shape

    def body(s_ref, o_ref):
        acc = s_ref[0]
        for d in range(1, nd):
            acc = acc + s_ref[d]
        o_ref[...] = acc

    return pl.pallas_call(
        body, grid=(1,), name=name,
        in_specs=[_const((nd, rows, C))], out_specs=_const((rows, C)), out_shape=_sds((rows, C), F32),
        compiler_params=_cp(),
    )(slots)


def _sum_slots(items, kh, name):
    n = len(items)
    in_specs, out_specs, out_shape = [], [], []
    for slots, src, kind, (R, C) in items:
        tr = R // (2 * _PARTS)
        if kind == "slab":
            src_spec = pl.BlockSpec((1, tr, C), lambda i, kh: (kh[0], kh[1] * _PARTS + i, 0))
        elif kind == "rows":
            src_spec = pl.BlockSpec((tr, C), lambda i, kh: (kh[0] * (2 * _PARTS) + kh[1] * _PARTS + i, 0))
        else:
            src_spec = pl.BlockSpec((tr, C), lambda i, kh: (kh[1] * _PARTS + i, kh[0]))
        in_specs += [pl.BlockSpec((8, tr, C), lambda i, kh: (0, i, 0)), src_spec]
        out_specs.append(pl.BlockSpec((tr, C), lambda i, kh: (i, 0)))
        out_shape.append(_sds((R // 2, C), F32))

    def body(kh_ref, *refs):
        me = 2 * kh_ref[0] + kh_ref[1]
        for k, (_, _, kind, _) in enumerate(items):
            s_ref, own_ref, o_ref = refs[2 * k], refs[2 * k + 1], refs[2 * n + k]
            acc = (own_ref[0] if kind == "slab" else own_ref[...]).astype(F32)
            for j in range(1, 8):
                acc = acc + s_ref[me ^ j].astype(F32)
            o_ref[...] = acc

    return pl.pallas_call(
        body, name=name,
        grid_spec=pltpu.PrefetchScalarGridSpec(
            num_scalar_prefetch=1, grid=(_PARTS,), in_specs=in_specs, out_specs=out_specs),
        out_shape=out_shape,
        compiler_params=_cp(),
    )(kh, *[a for slots, src, _, _ in items for a in (slots, src)])


def _assemble_w_in(slabs):
    tr = 256

    def body(s_ref, wm_ref, wdt_ref):
        full = jnp.concatenate([s_ref[k] for k in range(4)], axis=1)
        wm_ref[...] = full[:, :N_MAIN]
        wdt_ref[...] = jnp.concatenate([full[:, N_MAIN:], jnp.zeros((tr, 128 - 16), full.dtype)], axis=1)

    return pl.pallas_call(
        body, grid=(D // tr,), name="assemble_w_in",
        in_specs=[pl.BlockSpec((4, tr, 1156), lambda i: (0, i, 0))],
        out_specs=[_rows(tr, N_MAIN), _rows(tr, 128)],
        out_shape=[_sds((D, N_MAIN), slabs.dtype), _sds((D, 128), slabs.dtype)],
        compiler_params=_cp(),
    )(slabs)


def _split_dw_in(d_uv, d_ssd, d_dt):
    tr = 256

    def body(uv_ref, ssd_ref, dt_ref, o_ref):
        full = jnp.concatenate([uv_ref[...], ssd_ref[...], dt_ref[:, 0:16]], axis=1)
        for k in range(4):
            o_ref[k] = full[:, 1156 * k:1156 * (k + 1)]

    return pl.pallas_call(
        body, grid=(D // tr,), name="split_dw_in",
        in_specs=[_rows(tr, 2048), _rows(tr, 2560), _rows(tr, 128)],
        out_specs=pl.BlockSpec((4, tr, 1156), lambda i: (0, i, 0)),
        out_shape=_sds((4, D, 1156), d_uv.dtype),
        compiler_params=_cp(),
    )(d_uv, d_ssd, d_dt)


def _cast_w_in(w, kh):
    C, _, R = w.shape

    def body(kh_ref, w_ref, o_ref):
        o_ref[0] = w_ref[:, 0, :].T.astype(BF16)

    return pl.pallas_call(
        body, name="cast_w_in",
        grid_spec=pltpu.PrefetchScalarGridSpec(
            num_scalar_prefetch=1, grid=(pl.cdiv(C, _TJ),),
            in_specs=[pl.BlockSpec((_TJ, 1, R), lambda j, kh: (j, 0, 0))],
            out_specs=pl.BlockSpec((1, R, _TJ), lambda j, kh: (kh[0], 0, j))),
        out_shape=_sds((4, R, C), BF16),
        compiler_params=_cp(),
    )(kh, w)


def _cast_into_slot(ws, kh, name):
    n = len(ws)

    def body(kh_ref, *refs):
        for k in range(n):
            refs[n + k][0] = refs[k][...].astype(BF16)

    return pl.pallas_call(
        body, name=name,
        grid_spec=pltpu.PrefetchScalarGridSpec(
            num_scalar_prefetch=1, grid=(_PARTS,),
            in_specs=[pl.BlockSpec((w.shape[0] // _PARTS, w.shape[1]), lambda i, kh: (i, 0)) for w in ws],
            out_specs=[pl.BlockSpec((1, w.shape[0] // _PARTS, w.shape[1]), lambda i, kh: (kh[0], i, 0))
                       for w in ws]),
        out_shape=[_sds((4,) + w.shape, BF16) for w in ws],
        compiler_params=_cp(),
    )(kh, *ws)


_ANY = pl.BlockSpec(memory_space=pl.ANY)
_CHIP_FLIPS = [(1, 0), (0, 1), (1, 1)]
_DEVICE_FLIPS = [(fx, fy, fc) for fx in (0, 1) for fy in (0, 1) for fc in (0, 1)][1:]


def _half(h, rows):
    return pl.ds(pl.multiple_of(h * rows, rows), rows)


def _remote(src, dst, ssem, rsem, to):
    return pltpu.make_async_remote_copy(src_ref=src, dst_ref=dst, send_sem=ssem, recv_sem=rsem,
                                        device_id=to, device_id_type=MESH)


def _weight_gather(bufs, conv):
    n = len(bufs)

    def body(*refs):
        conv_ref, outs, conv_out = refs[n], refs[n + 1:2 * n + 1], refs[2 * n + 1]
        send_sems, recv_sems, fsend_sems, frecv_sems, csend_sems, crecv_sems, local_sem = refs[2 * n + 2:]
        x, y, c = lax.axis_index("x"), lax.axis_index("y"), lax.axis_index("c")
        me = 2 * x + y
        halves = [_half(c, r.shape[1] // 2) for r in outs]
        others = [_half(1 - c, r.shape[1] // 2) for r in outs]
        remote = _remote
        local = [pltpu.make_async_copy(conv_ref, conv_out.at[me], local_sem)]
        for cp in local:
            cp.start()
        sends = []
        for k, (fx, fy) in enumerate(_CHIP_FLIPS):
            peer = (x ^ fx, y ^ fy, c)
            for i in range(n):
                mine = outs[i].at[me, halves[i]]
                sends.append(remote(mine, mine, send_sems.at[k * n + i], recv_sems.at[k * n + i], peer))
            sends.append(remote(conv_ref, conv_out.at[me], csend_sems.at[k], crecv_sems.at[k], peer))
        for cp in sends:
            cp.start()
        sibling = (x, y, 1 - c)
        forwards = []
        for k, (fx, fy) in enumerate(_CHIP_FLIPS):
            peer = (x ^ fx, y ^ fy, c)
            src = 2 * (x ^ fx) + (y ^ fy)
            for i in range(n):
                landed = outs[i].at[src, halves[i]]
                remote(landed, landed, send_sems.at[k * n + i], recv_sems.at[k * n + i], peer).wait_recv()
                fw = remote(landed, landed, fsend_sems.at[k * n + i], frecv_sems.at[k * n + i], sibling)
                fw.start()
                forwards.append(fw)
            remote(conv_out.at[src], conv_out.at[src], csend_sems.at[k], crecv_sems.at[k], peer).wait_recv()
        for k, (fx, fy) in enumerate(_CHIP_FLIPS):
            src = 2 * (x ^ fx) + (y ^ fy)
            for i in range(n):
                theirs = outs[i].at[src, others[i]]
                remote(theirs, theirs, fsend_sems.at[k * n + i], frecv_sems.at[k * n + i], sibling).wait_recv()
        for cp in sends + forwards:
            cp.wait_send()
        for cp in local:
            cp.wait()

    dma = pltpu.SemaphoreType.DMA
    return pl.pallas_call(
        body, name="weight_gather",
        in_specs=[_ANY] * (n + 1), out_specs=[_ANY] * (n + 1),
        out_shape=[_sds(b.shape, b.dtype) for b in bufs] + [_sds((4,) + conv.shape, conv.dtype)],
        input_output_aliases={i: i for i in range(n)},
        scratch_shapes=[dma((3 * n,)), dma((3 * n,)), dma((3 * n,)), dma((3 * n,)), dma((3,)), dma((3,)), dma],
    )(*bufs, conv)


def _piece(ref, kind, R, C, k, h):
    if kind == "slab":
        return ref.at[k, _half(h, R // 2), :]
    if kind == "rows":
        return ref.at[pl.ds(pl.multiple_of(k * R + h * (R // 2), R // 2), R // 2), :]
    return ref.at[_half(h, R // 2), pl.ds(pl.multiple_of(k * C, C), C)]


def _small_exchange(small, after):
    rs = small.shape[0]

    def body(s_ref, after_ref, out_ref, send_sems, recv_sems, local_sem):
        del after_ref
        x, y, c = lax.axis_index("x"), lax.axis_index("y"), lax.axis_index("c")
        slot = 4 * x + 2 * y + c
        own = pltpu.make_async_copy(s_ref, out_ref.at[slot], local_sem)
        own.start()
        copies = []
        for k, (fx, fy, fc) in enumerate(_DEVICE_FLIPS):
            copies.append(_remote(s_ref, out_ref.at[slot], send_sems.at[k], recv_sems.at[k], (x ^ fx, y ^ fy, c ^ fc)))
        for cp in copies:
            cp.start()
        for k, (fx, fy, fc) in enumerate(_DEVICE_FLIPS):
            theirs = out_ref.at[slot ^ (k + 1)]
            _remote(theirs, theirs, send_sems.at[k], recv_sems.at[k], (x ^ fx, y ^ fy, c ^ fc)).wait_recv()
        for cp in copies:
            cp.wait_send()
        own.wait()

    dma = pltpu.SemaphoreType.DMA
    return pl.pallas_call(
        body, name="small_exchange",
        in_specs=[_ANY, _ANY], out_specs=_ANY, out_shape=_sds((8, rs, 128), F32),
        scratch_shapes=[dma((7,)), dma((7,)), dma],
    )(small, after)


_HBM = pl.BlockSpec(memory_space=pltpu.HBM)
_SEM = pl.BlockSpec(memory_space=pltpu.SEMAPHORE)


def _split_start(name, arrays, n_copies, plan, after=None):
    n = len(arrays)
    extra = [] if after is None else [after]

    def body(*refs):
        m = n + len(extra)
        arrs, send_sems, recv_sems, token = refs[:n], refs[m], refs[m + 1], refs[-1]
        for j, (src, dst, peer) in enumerate(plan(arrs)):
            _remote(src, dst, send_sems.at[j], recv_sems.at[j], peer).start()
        token[...] = jnp.zeros_like(token)

    dma = pltpu.SemaphoreType.DMA
    res = pl.pallas_call(
        body, name=name,
        out_shape=(dma((n_copies,)), dma((n_copies,)), *[pltpu.HBM(a.shape, a.dtype) for a in arrays],
                   _sds((8, 128), F32)),
        in_specs=[_HBM] * n + [_ANY] * len(extra),
        out_specs=(_SEM, _SEM, *[_HBM] * n, pl.BlockSpec(memory_space=pltpu.VMEM)),
        input_output_aliases={i: 2 + i for i in range(n)},
        compiler_params=pltpu.CompilerParams(has_side_effects=pltpu.SideEffectType.DATAFLOW_SIDE_EFFECTING),
    )(*[pltpu.with_memory_space_constraint(a, pltpu.HBM) for a in arrays], *extra)
    return res[0], res[1], list(res[2:2 + n]), res[-1]


def _split_wait(name, arrays, send_sems, recv_sems, plan, after):
    n = len(arrays)

    def body(*refs):
        arrs, ssems, rsems = refs[:n], refs[n], refs[n + 1]
        for j, (src, dst, peer) in enumerate(plan(arrs)):
            cp = _remote(src, dst, ssems.at[j], rsems.at[j], peer)
            cp.wait_send()
            cp.wait_recv()

    return list(pl.pallas_call(
        body, name=name,
        out_shape=tuple(pltpu.HBM(a.shape, a.dtype) for a in arrays),
        in_specs=[_HBM] * n + [_SEM, _SEM, _ANY],
        out_specs=tuple([_HBM] * n),
        input_output_aliases={i: i for i in range(n)},
        compiler_params=pltpu.CompilerParams(has_side_effects=pltpu.SideEffectType.DATAFLOW_SIDE_EFFECTING),
    )(*arrays, send_sems, recv_sems, after))


def _gather_plan(n):
    def plan(bufs):
        x, y, c = lax.axis_index("x"), lax.axis_index("y"), lax.axis_index("c")
        me = 2 * x + y
        return [(bufs[i].at[me], bufs[i].at[me], (x ^ fx, y ^ fy, c)) for fx, fy in _CHIP_FLIPS for i in range(n)]

    return plan


def _reduce_plan(specs, n_small):
    n = len(specs)

    def plan(arrs):
        x, y, c = lax.axis_index("x"), lax.axis_index("y"), lax.axis_index("c")
        slot = 4 * x + 2 * y + c
        out = []
        for fx, fy, fc in _DEVICE_FLIPS:
            peer = (x ^ fx, y ^ fy, c ^ fc)
            for i, (kind, (R, C)) in enumerate(specs):
                out.append((_piece(arrs[i], kind, R, C, 2 * peer[0] + peer[1], peer[2]), arrs[n + i].at[slot], peer))
            for s in range(n_small):
                out.append((arrs[2 * n + 2 * s], arrs[2 * n + 2 * s + 1].at[slot], peer))
        return out

    return plan


def _sibling_exchange(halves, name):
    n = len(halves)

    def body(*refs):
        ins, outs, send_sems, recv_sems = refs[:n], refs[n:2 * n], refs[2 * n], refs[2 * n + 1]
        sibling = (lax.axis_index("x"), lax.axis_index("y"), 1 - lax.axis_index("c"))
        copies = [pltpu.make_async_remote_copy(src_ref=ins[i], dst_ref=outs[i], send_sem=send_sems.at[i],
                                               recv_sem=recv_sems.at[i], device_id=sibling, device_id_type=MESH)
                  for i in range(n)]
        for cp in copies:
            cp.start()
        for cp in copies:
            cp.wait()

    dma = pltpu.SemaphoreType.DMA
    return pl.pallas_call(
        body, name=name,
        in_specs=[_ANY] * n, out_specs=[_ANY] * n,
        out_shape=[_sds(h.shape, h.dtype) for h in halves],
        scratch_shapes=[dma((n,)), dma((n,))],
    )(*halves)


_BIG = [("w_in", (1024, 1156), "slab"), ("w_out", (512, 1024), "rows"), ("w_ff1", (1024, 1024), "cols"),
        ("w_ff2", (1024, 1024), "rows"), ("w_ple_gate", (256, 1024), "rows"), ("w_ple_proj", (256, 256), "cols")]
_SMALL = [("norm_mix_g", (1, 1024)), ("gm_v_norm_g", (1, 1024)), ("gm_ws", (1, 8, 128, 128)), ("gm_bs", (1, 8, 128)),
          ("gm_out_norm_g", (1, 1024)), ("ssd_conv_w", (1, 4, 1536)), ("ssd_conv_b", (1, 1536)),
          ("ssd_dt_bias", (1, 16)), ("ssd_a_log", (1, 16)), ("ssd_d", (1, 16)), ("ssd_norm_g", (1, 1024)),
          ("norm_mlp_g", (1, 1024)), ("ple_norm_g", (1, 1024)), ("final_norm_g", (1024,))]


def _rows128(a):
    flat = a.reshape(-1)
    rows = -(-flat.shape[0] // 1024) * 8
    return jnp.pad(flat, (0, rows * 128 - flat.shape[0])).reshape(rows, 128)


def _pad_lanes(v, n=128):
    v = v.reshape(1, -1)
    return jnp.pad(v, ((0, 0), (0, n - v.shape[1])))


_SMALL_SHAPES = dict(_SMALL + [("loss", ())])
_BIG_SPECS = {n: (kind, shp) for n, shp, kind in _BIG}


class _Comm:
    def __init__(self, a, kh):
        self.a, self.kh = a, kh
        rest = _BIG[1:]
        self.bufs = {"w_in": _cast_w_in(a["w_in"].transpose(2, 0, 1), kh)}
        cast = _cast_into_slot([a[n].reshape(shp) for n, shp, _ in rest], kh, "cast_rest")
        self.bufs.update({n: c for (n, _, _), c in zip(rest, cast)})
        self.sent = []
        self.small_tot = {}

    def w_in(self):
        (g_win,), g_cw = self._gather_now()
        rest = [self.bufs[n] for n, _, _ in _BIG[1:]]
        plan = _gather_plan(len(rest))
        ssem, rsem, thru, token = _split_start("gather_start", rest, 3 * len(rest), plan, after=g_cw)
        self.gather = (plan, ssem, rsem, thru)
        wm, wdt = _assemble_w_in(g_win)
        return wm, wdt, jnp.concatenate([g_cw[k] for k in range(4)], axis=1), token

    def _gather_now(self):
        *bufs, g_cw = _weight_gather([self.bufs["w_in"]], self.a["ssd_conv_w"].reshape(4, 384))
        return bufs, g_cw

    def rest(self, after):
        plan, ssem, rsem, thru = self.gather
        g_wo, g_w1, g_w2, g_wg, g_wp = _split_wait("gather_wait", thru, ssem, rsem, plan, after)
        return g_wo.reshape(2048, D), g_w1, g_w2.reshape(DFF, D), g_wg.reshape(D, D), g_wp

    def send(self, tag, grads):
        big = [n for n, _, _ in _BIG if n in grads]
        small = [n for n in _SMALL_SHAPES if n in grads]
        parts = [_rows128(grads[n]) for n in small]
        rows = [s.shape[0] for s in parts]
        if not big:
            self.last_small = (tag, small, rows, jnp.concatenate(parts, axis=0))
            return None
        srcs = [grads[n] for n in big]
        lands = [lax.empty((8, _BIG_SPECS[n][1][0] // 2, _BIG_SPECS[n][1][1]), GRAD) for n in big]
        extra = []
        if small:
            pack = jnp.concatenate(parts, axis=0)
            extra = [pack, jnp.broadcast_to(pack, (8,) + pack.shape)]
        plan = _reduce_plan([_BIG_SPECS[n] for n in big], len(extra) // 2)
        n_copies = 7 * (len(big) + len(extra) // 2)
        ssem, rsem, thru, token = _split_start("reduce_start_" + tag, srcs + lands + extra, n_copies, plan)
        self.sent.append((tag, big, small, rows, plan, ssem, rsem, thru))
        return token

    def _unpack(self, tot, names, rows):
        o = 0
        for n, r in zip(names, rows):
            shp = _SMALL_SHAPES[n]
            cnt = 1
            for s in shp:
                cnt *= s
            self.small_tot[n] = tot[o:o + r].reshape(-1)[:cnt].reshape(shp)
            o += r

    def finish(self, after):
        a, results = self.a, {}

        def update(names, own, tag):
            other = _sibling_exchange([own[n] for n in names], "sibling_exchange_" + tag)
            if names == ["w_in"]:
                w, m, v = (a[k].transpose(2, 0, 1) for k in ("w_in", "m_w_in", "v_w_in"))
                raw = _adamw_transposed(w, own["w_in"], other[0], m, v, "adamw_" + tag)
                results["w_in"] = tuple(r.transpose(1, 2, 0) for r in raw)
                return raw[1]
            items = [(a[n].reshape(_BIG_SPECS[n][1]), own[n], oth, a["m_" + n].reshape(_BIG_SPECS[n][1]),
                      a["v_" + n].reshape(_BIG_SPECS[n][1])) for n, oth in zip(names, other)]
            results.update(zip(names, _adamw_halves(items, "adamw_" + tag)))
            return results[names[-1]][1]

        own, early = {}, []
        for tag, big, small, rows, plan, ssem, rsem, thru in self.sent:
            if tag == self.sent[-1][0]:
                after = update(early, own, "early")
            arrs = _split_wait("reduce_wait_" + tag, thru, ssem, rsem, plan, after)
            nb_ = len(big)
            sums = _sum_slots([(arrs[nb_ + i], arrs[i]) + _BIG_SPECS[n] for i, n in enumerate(big)], self.kh,
                              "sum_" + tag)
            own.update(zip(big, sums))
            after = sums[-1]
            early += big
            if small:
                self._unpack(_sum_small(arrs[2 * nb_ + 1], "sum_small_" + tag), small, rows)
        after = update(self.sent[-1][1], own, "late")
        tag, small, rows, pack = self.last_small
        self._unpack(_sum_small(_small_exchange(pack, after), "sum_small_" + tag), small, rows)
        return results, dict(self.small_tot)


def _local_step(x, p, tgt, sm, comm, nb, tm):
    T = x.shape[0]
    wm, wdt, conv_w, token = comm.w_in()
    g_mix, gv, gout = sm["norm_mix_g"].reshape(1, D), sm["gm_v_norm_g"].reshape(1, D), sm["gm_out_norm_g"].reshape(1, D)
    ws = sm["gm_ws"].reshape(GM_HEADS, CH, CH)
    bst = jnp.pad(sm["gm_bs"].reshape(GM_HEADS, CH).T, ((0, 0), (0, 128 - GM_HEADS)))
    convw = jnp.pad(conv_w, ((0, 4), (0, 0)))
    convb = sm["ssd_conv_b"].reshape(1, CONV_CH)
    dtb, alog = _pad_lanes(sm["ssd_dt_bias"]), _pad_lanes(sm["ssd_a_log"])
    dskip = jnp.repeat(sm["ssd_d"].reshape(SSD_HEADS), SSD_P).reshape(1, 1024)
    ng, g_mlp, g_ple = sm["ssd_norm_g"].reshape(1, D), sm["norm_mlp_g"].reshape(1, D), sm["ple_norm_g"].reshape(1, D)
    gf = sm["final_norm_g"].reshape(1, D)
    head_of_lane = lax.broadcasted_iota(jnp.int32, (128, 1024), 1) // SSD_P
    ex = (lax.broadcasted_iota(jnp.int32, (128, 1024), 0) == head_of_lane).astype(BF16)
    ext = ex.T
    ltri = (lax.broadcasted_iota(jnp.int32, (CH, CH), 0) >= lax.broadcasted_iota(jnp.int32, (CH, CH), 1)).astype(F32)

    pz, pxbc, dtraw, xn, cat, uv = _inproj_gmlp(x, g_mix, wm, wdt, gv, ws, bst, gout, tm, token)
    cat, sall, conv = _ssd_fwd(pz, pxbc, dtraw, cat, convw, convb, dtb, alog, dskip, ng, ex, ltri, nb)
    wo, w1, w2, wg, wp = comm.rest(cat)
    h1, hn = _outproj(cat, wo, x, g_mlp, tm)
    hid = _ff1(hn, w1, min(T, 2 * tm))
    hp, dgl, dpe, dh2, dh2b, loss, d_gf, d_gple = _ff2_tail(hid, w2, h1, g_ple, p, tgt, wg, wp, gf, tm)

    d_wp = _matmul_tn(p, dpe, "dw_ple_proj", a_fn=lambda a: a.astype(MXU))
    d_wg = _matmul_tn(hp, dgl, "dw_ple_gate")
    d_w2 = _matmul_tn(hid, dh2b, "dw_ff2", a_fn=_sq)
    dpre = _ff2_bwd(dh2b, w2, hid, min(T, 2 * tm))
    d_w1 = _matmul_tn(hn, dpre, "dw_ff1")
    token = comm.send("a", {"w_ple_proj": d_wp, "w_ple_gate": d_wg, "w_ff2": d_w2, "w_ff1": d_w1})
    dh1, dh1b, d_gmlp = _ff1_bwd(dpre, w1, dh2, h1, g_mlp, tm, token)
    dcat = _outproj_bwd(dh1b, wo, min(T, 2 * tm))
    d_wo = _matmul_tn(cat, dh1b, "dw_out")
    duv, d_gv, d_ws, d_bst, d_gout, dxn_uv = _gmlp_bwd(uv, dcat, gv, ws, bst, gout, wm)
    token = comm.send("b", {
        "w_out": d_wo, "loss": loss[0:1, 0:1], "final_norm_g": d_gf, "ple_norm_g": d_gple, "norm_mlp_g": d_gmlp,
        "gm_v_norm_g": d_gv, "gm_ws": d_ws, "gm_bs": d_bst[:, :GM_HEADS].T, "gm_out_norm_g": d_gout})
    dssd, ddt, d_cw, d_cb, d_dtb, d_al, d_ds, d_ng = _ssd_bwd(
        pz, pxbc, conv, dtraw, sall, dcat, convw, dtb, alog, dskip, ng, ex, ltri, ext, nb, token)
    d_win = _split_dw_in(_matmul_tn(xn, duv, "dw_in_uv"), _matmul_tn(xn, dssd, "dw_in_ssd"),
                         _matmul_tn(xn, ddt, "dw_in_dt"))
    token = comm.send("c", {"w_in": d_win})
    dx, d_gmix = _inproj_bwd(dxn_uv, dssd, ddt, wm, wdt, dh1, x, g_mix, tm, token)
    comm.send("d", {"norm_mix_g": d_gmix, "ssd_conv_w": d_cw[0:4], "ssd_conv_b": d_cb, "ssd_dt_bias": d_dtb[:, :16],
                    "ssd_a_log": d_al[:, :16], "ssd_d": d_ds[:, :16], "ssd_norm_g": d_ng})
    return dx


def kernel(x, p, norm_mix_g, w_in, gm_v_norm_g, gm_ws, gm_bs, gm_out_norm_g, ssd_conv_w, ssd_conv_b, ssd_dt_bias, ssd_a_log, ssd_d, ssd_norm_g, w_out, norm_mlp_g, w_ff1, w_ff2, ple_norm_g, w_ple_gate, w_ple_proj, final_norm_g, loss_target, m_norm_mix_g, m_w_in, m_gm_v_norm_g, m_gm_ws, m_gm_bs, m_gm_out_norm_g, m_ssd_conv_w, m_ssd_conv_b, m_ssd_dt_bias, m_ssd_a_log, m_ssd_d, m_ssd_norm_g, m_w_out, m_norm_mlp_g, m_w_ff1, m_w_ff2, m_ple_norm_g, m_w_ple_gate, m_w_ple_proj, m_final_norm_g, v_norm_mix_g, v_w_in, v_gm_v_norm_g, v_gm_ws, v_gm_bs, v_gm_out_norm_g, v_ssd_conv_w, v_ssd_conv_b, v_ssd_dt_bias, v_ssd_a_log, v_ssd_d, v_ssd_norm_g, v_w_out, v_norm_mlp_g, v_w_ff1, v_w_ff2, v_ple_norm_g, v_w_ple_gate, v_w_ple_proj, v_final_norm_g):
    a = dict(locals())
    order = ["norm_mix_g", "w_in", "gm_v_norm_g", "gm_ws", "gm_bs", "gm_out_norm_g", "ssd_conv_w", "ssd_conv_b",
             "ssd_dt_bias", "ssd_a_log", "ssd_d", "ssd_norm_g", "w_out", "norm_mlp_g", "w_ff1", "w_ff2", "ple_norm_g",
             "w_ple_gate", "w_ple_proj", "final_norm_g"]
    chip = 2 * lax.axis_index("x") + lax.axis_index("y")
    nb, S = x.shape[0], x.shape[1]
    T = nb * S
    sm = {n: a[n] for n, _ in _SMALL if n != "ssd_conv_w"}
    comm = _Comm(a, jnp.stack([chip, lax.axis_index("c")]).astype(jnp.int32))
    dx = _local_step(x.reshape(T, D), p.reshape(T, DPLE), loss_target.reshape(T, D), sm, comm, nb, 512)
    big, g_out = comm.finish(dx)
    delta, new_m, new_v = {}, {}, {}
    for n, _, _ in _BIG:
        g_out[n], delta[n], new_m[n], new_v[n] = (r.reshape(a[n].shape) for r in big[n])
    g_out["ssd_conv_w"] = lax.dynamic_slice(g_out["ssd_conv_w"], (0, 0, chip * 384), (1, 4, 384))
    small_names = [n for n, _ in _SMALL]
    packs = [jnp.concatenate([_rows128(src(n)) for n in small_names], axis=0)
             for src in (lambda n: a[n], lambda n: g_out[n], lambda n: a["m_" + n], lambda n: a["v_" + n])]
    outs = _adamw(*packs, "adamw_small")
    o = 0
    for n in small_names:
        r = _rows128(a[n]).shape[0]
        cnt = a[n].size
        for dst, src in zip((delta, new_m, new_v), outs):
            dst[n] = src[o:o + r].reshape(-1)[:cnt].reshape(a[n].shape)
        o += r
    return (g_out["loss"], dx.reshape(x.shape), *[g_out[n] for n in order], *[delta[n] for n in order],
            *[new_m[n] for n in order], *[new_v[n] for n in order])
```

```python
import jax
import jax.numpy as jnp
from jax import lax
from jax.experimental import pallas as pl
from jax.experimental.pallas import tpu as pltpu

F32 = jnp.float32
BF16 = jnp.bfloat16
MXU = jnp.bfloat16
GRAD = jnp.bfloat16

D = 1024
CH = 128
GM_HEADS = 8
SSD_HEADS = 16
SSD_P = 64
CONV_CH = 1536
N_MAIN = 4608
DFF = 4096
DPLE = 256
EPS = 1e-6
NEG = -1e30

LR, B1, B2, ADAM_EPS, WD, STEP = 0.001, 0.9, 0.999, 1e-08, 0.01, 10

VMEM_LIMIT = 56 * 1024 * 1024
_SEQS_PER_STEP = 4
MESH = pl.DeviceIdType.MESH

INV_SQRT2 = 0.7071067811865476
INV_SQRT_2PI = 0.3989422804014327


def _cp(n_axes=1):
    return pltpu.CompilerParams(dimension_semantics=("arbitrary",) * n_axes, vmem_limit_bytes=VMEM_LIMIT)


def _dot(a, b):
    return jnp.dot(a, b, preferred_element_type=F32)


def _dot_nt(a, b):
    return lax.dot_general(a, b, (((1,), (1,)), ((), ())), preferred_element_type=F32)


def _dot_tn(a, b):
    return lax.dot_general(a, b, (((0,), (0,)), ((), ())), preferred_element_type=F32)


def _dot_hi(a, b):
    return jnp.dot(a, b, preferred_element_type=F32, precision=lax.Precision.HIGHEST)


def _dot_01(a, sel):
    hi = a.astype(BF16)
    lo = (a - hi.astype(F32)).astype(BF16)
    n = a.shape[0]
    r = _dot(jnp.concatenate([hi, lo], axis=0), sel)
    return r[0:n] + r[n:2 * n]


def _rows(tm, n, j=0):
    return pl.BlockSpec((tm, n), lambda i: (i, j))


def _const(shape):
    nd = len(shape)
    return pl.BlockSpec(shape, lambda *_: (0,) * nd)


def _sds(shape, dtype):
    return jax.ShapeDtypeStruct(shape, dtype)


def _rms(x):
    r = lax.rsqrt(jnp.mean(x * x, axis=-1, keepdims=True) + EPS)
    return x * r, r


def _rms_bwd(dy, xhat, r, g):
    dyg = dy * g
    return r * (dyg - xhat * jnp.mean(dyg * xhat, axis=-1, keepdims=True))


def _sigmoid(x):
    return 1.0 / (1.0 + jnp.exp(-x))


def _gelu(x):
    cdf = 0.5 * (1.0 + lax.erf(x * INV_SQRT2))
    pdf = jnp.exp(-0.5 * x * x) * INV_SQRT_2PI
    return x * cdf, cdf + x * pdf


def _softplus(x):
    e = jnp.exp(-jnp.abs(x))
    u = 1.0 + e
    log1p = jnp.where(u == 1.0, e, jnp.log(u) * e / (u - 1.0))
    return jnp.maximum(x, 0.0) + log1p


def _after(n_in, fn):
    def body(*refs):
        return fn(*refs[:n_in], *refs[n_in + 1:])

    return body


def _inproj_gmlp(x, g, wm, wdt, gv, ws, bst, gout, tm, after):
    T = x.shape[0]

    def body(x_ref, g_ref, wm_ref, wdt_ref, gv_ref, ws_ref, bst_ref, gout_ref,
             z_ref, xbc_ref, dt_ref, xn_ref, ya_ref, uv_ref):
        xh, _ = _rms(x_ref[...])
        xn = (xh * g_ref[...]).astype(MXU)
        xn_ref[...] = xn
        for n in range(4):
            uv_ref[:, n * 512:(n + 1) * 512] = _dot(xn, wm_ref[:, n * 512:(n + 1) * 512])
        for n in range(2):
            z_ref[:, n * 512:(n + 1) * 512] = _dot(xn, wm_ref[:, 2048 + n * 512:2048 + (n + 1) * 512])
        for n in range(3):
            xbc_ref[:, n * 512:(n + 1) * 512] = _dot(xn, wm_ref[:, 3072 + n * 512:3072 + (n + 1) * 512])
        dt_ref[...] = _dot(xn, wdt_ref[...])
        for k in range(tm // CH):
            rows = slice(k * CH, (k + 1) * CH)
            f = _gmlp_fwd_vals(uv_ref[rows, 0:1024], uv_ref[rows, 1024:2048], gv_ref[...], ws_ref, bst_ref[...],
                               gout_ref[...])
            ya_ref[rows, :] = f["out"].astype(MXU)

    return pl.pallas_call(
        _after(8, body), grid=(T // tm,), name="inproj_gmlp",
        in_specs=[_rows(tm, D), _const((1, D)), _const((D, N_MAIN)), _const((D, 128)), _const((1, 1024)),
                  _const((GM_HEADS, CH, CH)), _const((CH, 128)), _const((1, 1024)), _ANY],
        out_specs=[_rows(tm, 1024), _rows(tm, CONV_CH), _rows(tm, 128), _rows(tm, D), _rows(tm, 1024, 0),
                   _rows(tm, 2048)],
        out_shape=[_sds((T, 1024), F32), _sds((T, CONV_CH), F32), _sds((T, 128), F32), _sds((T, D), MXU),
                   _sds((T, 2048), MXU), _sds((T, 2048), F32)],
        compiler_params=_cp(),
    )(x, g, wm, wdt, gv, ws, bst, gout, after)


def _gmlp_fwd_vals(u, v, gv, ws_ref, bst, gout):
    ug, dug = _gelu(u)
    vg, dvg = _gelu(v)
    row = lax.broadcasted_iota(jnp.int32, (CH, CH), 0)
    col = lax.broadcasted_iota(jnp.int32, (CH, CH), 1)
    tril = row >= col
    ys, heads = [], []
    for h in range(GM_HEADS):
        sl = slice(h * 128, (h + 1) * 128)
        vhat, rv = _rms(vg[:, sl])
        vn = (vhat * gv[:, sl]).astype(MXU)
        wt = jnp.where(tril, ws_ref[h], 0.0)
        mixed = _dot(wt.astype(MXU), vn) + bst[:, h:h + 1]
        ys.append(ug[:, sl] * mixed)
        heads.append((vhat, rv, vn, wt, mixed))
    y = jnp.concatenate(ys, axis=1)
    yhat, ry = _rms(y)
    return dict(ug=ug, dug=dug, dvg=dvg, heads=heads, yhat=yhat, ry=ry, tril=tril, out=yhat * gout)


def _shifts_down(cur, halo):
    row8 = lax.broadcasted_iota(jnp.int32, (8, cur.shape[1]), 0)
    out = [cur]
    for j in (1, 2, 3):
        sh = pltpu.roll(cur, j, 0)
        top = jnp.where(row8 < j, pltpu.roll(halo, j, 0), sh[0:8])
        out.append(jnp.concatenate([top, sh[8:]], axis=0))
    return out


def _shifts_up(cur, halo):
    row8 = lax.broadcasted_iota(jnp.int32, (8, cur.shape[1]), 0)
    out = []
    for j in (1, 2, 3):
        sh = pltpu.roll(cur, CH - j, 0)
        bot = jnp.where(row8 + j >= 8, pltpu.roll(halo, 8 - j, 0), sh[CH - 8:CH])
        out.append(jnp.concatenate([sh[0:CH - 8], bot], axis=0))
    return out


def _conv(xbc, halo, convw, convb):
    sh = _shifts_down(xbc, halo)
    return convb + convw[3:4] * sh[0] + convw[2:3] * sh[1] + convw[1:2] * sh[2] + convw[0:1] * sh[3]


def _ssd_fwd_vals(z, conv, dtraw, dtb, alog, dskip, ng, ex, ltri, s_prev):
    sig_c = _sigmoid(conv)
    xa = conv * sig_c
    xs = xa[:, :1024]
    bm = [xa[:, 1024:1152], xa[:, 1152:1280]]
    cm = [xa[:, 1280:1408], xa[:, 1408:1536]]
    dtpre = dtraw + dtb
    dt = _softplus(dtpre)
    a_neg = -jnp.exp(alog)
    cs = _dot_hi(ltri, dt * a_neg)
    cst = cs.T
    last = cs[CH - 1:CH]
    ecs = jnp.exp(cs)
    dec = jnp.exp(last - cs)
    spread = _dot_01(jnp.concatenate([dt, ecs, dec], axis=0), ex)
    dte, ecse, dece = spread[0:CH], spread[CH:2 * CH], spread[2 * CH:3 * CH]
    cde = ecse[CH - 1:CH]
    de = dskip
    xdt = xs * dte
    row = lax.broadcasted_iota(jnp.int32, (CH, CH), 0)
    col = lax.broadcasted_iota(jnp.int32, (CH, CH), 1)
    tril = row >= col
    lo = col < SSD_P
    bmb = [b.astype(MXU) for b in bm]
    cmb = [c.astype(MXU) for c in cm]
    mg = [_dot_nt(cmb[g], bmb[g]) for g in range(2)]
    yd, lms, whs = [], [], []
    for q in range(8):
        g = q // 4
        xq = xdt[:, q * 128:(q + 1) * 128]
        acc = None
        for hh in range(2):
            h = 2 * q + hh
            seg = cs[:, h:h + 1] - cst[h:h + 1, :]
            lm = jnp.exp(jnp.where(tril, seg, NEG))
            wh = (mg[g] * lm).astype(MXU)
            xm = jnp.where(lo if hh == 0 else ~lo, xq, 0.0).astype(MXU)
            part = _dot(wh, xm)
            acc = part if acc is None else acc + part
            lms.append(lm)
            whs.append(wh)
        yd.append(acc)
    yd = jnp.concatenate(yd, axis=1)
    sb = s_prev.astype(MXU)
    yo = jnp.concatenate([_dot(cmb[g], sb[:, g * 512:(g + 1) * 512]) for g in range(2)], axis=1) * ecse
    xdec = (xdt * dece).astype(MXU)
    states = jnp.concatenate([_dot_tn(bmb[g], xdec[:, g * 512:(g + 1) * 512]) for g in range(2)], axis=1)
    s_next = s_prev * cde + states
    ypre = yd + yo + de * xs
    sig_z = _sigmoid(z)
    yg = ypre * z * sig_z
    outs, yhat, rr = [], [], []
    for g in range(2):
        sl = slice(g * 512, (g + 1) * 512)
        yh, r = _rms(yg[:, sl])
        yhat.append(yh)
        rr.append(r)
        outs.append(yh * ng[:, sl])
    return dict(sig_c=sig_c, xa=xa, xs=xs, bmb=bmb, cmb=cmb, dtpre=dtpre, dt=dt, a_neg=a_neg,
                cs=cs, last=last, ecs=ecs, dec=dec, dte=dte, ecse=ecse, dece=dece, cde=cde, de=de, xdt=xdt,
                mg=mg, lms=lms, whs=whs, lo=lo, yo=yo, sb=sb, xdec=xdec, s_next=s_next, ypre=ypre, sig_z=sig_z,
                yhat=yhat, rr=rr, out=jnp.concatenate(outs, axis=1))


def _ssd_fwd(pz, pxbc, dtraw, cat, convw, convb, dtb, alog, dskip, ng, ex, ltri, nb):
    T = pz.shape[0]
    S = T // nb
    nch = S // CH
    ns = _SEQS_PER_STEP if nb % _SEQS_PER_STEP == 0 else 1

    def body(z_ref, xbc_ref, halo_ref, dt_ref, cw_ref, cb_ref, dtb_ref, al_ref, ds_ref, ng_ref, ex_ref, lt_ref,
             cat_in_ref, yb_ref, sall_ref, conv_ref, s_ref):
        del cat_in_ref
        c = pl.program_id(1)

        @pl.when(c == 0)
        def _():
            s_ref[...] = jnp.zeros_like(s_ref)

        for i in range(ns):
            halo = jnp.where(c == 0, 0.0, halo_ref[i])
            s_prev = s_ref[i]
            sall_ref[i, 0] = s_prev
            conv = _conv(xbc_ref[i], halo, cw_ref[...], cb_ref[...])
            conv_ref[i] = conv
            f = _ssd_fwd_vals(z_ref[i], conv, dt_ref[i], dtb_ref[...], al_ref[...], ds_ref[...], ng_ref[...],
                              ex_ref[...], lt_ref[...], s_prev)
            s_ref[i] = f["s_next"]
            yb_ref[i] = f["out"].astype(MXU)

    def seq(width, col=0):
        return pl.BlockSpec((ns, CH, width), lambda b, c: (b, c, col))

    cat, sall, conv = pl.pallas_call(
        body, grid=(nb // ns, nch), name="ssd_fwd",
        in_specs=[seq(1024), seq(CONV_CH),
                  pl.BlockSpec((ns, 8, CONV_CH), lambda b, c: (b, jnp.maximum(c * (CH // 8) - 1, 0), 0)),
                  seq(128),
                  _const((8, CONV_CH)), _const((1, CONV_CH)), _const((1, 128)), _const((1, 128)), _const((1, 1024)),
                  _const((1, 1024)), _const((128, 1024)), _const((CH, CH)), _ANY],
        out_specs=[seq(1024, 1), pl.BlockSpec((ns, 1, 128, 1024), lambda b, c: (b, c, 0, 0)), seq(CONV_CH)],
        out_shape=[_sds((nb, S, 2048), MXU), _sds((nb, nch, 128, 1024), F32), _sds((nb, S, CONV_CH), F32)],
        scratch_shapes=[pltpu.VMEM((ns, 128, 1024), F32)],
        input_output_aliases={12: 0},
        compiler_params=_cp(2),
    )(pz.reshape(nb, S, 1024), pxbc.reshape(nb, S, CONV_CH), pxbc.reshape(nb, S, CONV_CH), dtraw.reshape(nb, S, 128),
      convw, convb, dtb, alog, dskip, ng, ex, ltri, cat.reshape(nb, S, 2048))
    return cat.reshape(T, 2048), sall, conv.reshape(T, CONV_CH)


def _outproj(cat, wo, x, g, tm):
    T = x.shape[0]

    def body(cat_ref, wo_ref, x_ref, g_ref, h1_ref, hn_ref):
        h1 = x_ref[...] + _dot(cat_ref[...], wo_ref[...])
        h1_ref[...] = h1
        hn_ref[...] = (_rms(h1)[0] * g_ref[...]).astype(MXU)

    return pl.pallas_call(
        body, grid=(T // tm,), name="outproj",
        in_specs=[_rows(tm, 2048), _const((2048, D)), _rows(tm, D), _const((1, D))],
        out_specs=[_rows(tm, D), _rows(tm, D)],
        out_shape=[_sds((T, D), F32), _sds((T, D), MXU)],
        compiler_params=_cp(),
    )(cat, wo, x, g)


def _ff1(hn, w1, tm):
    T = hn.shape[0]

    def body(hn_ref, w1_ref, hid_ref):
        hn_v = hn_ref[...]
        for n in range(4):
            hid_ref[:, n * 1024:(n + 1) * 1024] = jnp.maximum(_dot(hn_v, w1_ref[n]), 0.0).astype(MXU)

    return pl.pallas_call(
        body, grid=(T // tm,), name="ff1",
        in_specs=[_rows(tm, D), _const((4, D, 1024))],
        out_specs=_rows(tm, DFF),
        out_shape=_sds((T, DFF), MXU),
        compiler_params=_cp(),
    )(hn, w1)


def _sq(hid):
    h = hid.astype(F32)
    return (h * h).astype(MXU)


def _ff2_tail(hid, w2, h1, g_ple, p, tgt, wg, wp, gf, tm):
    T = h1.shape[0]

    def body(hid_ref, w2_ref, h1_ref, g_ref, p_ref, t_ref, wg_ref, wp_ref, gf_ref,
             hp_ref, dgl_ref, dpe_ref, dh2_ref, dh2b_ref, loss_ref, dgf_ref, dg_ref):
        @pl.when(pl.program_id(0) == 0)
        def _():
            loss_ref[...] = jnp.zeros_like(loss_ref)
            dgf_ref[...] = jnp.zeros_like(dgf_ref)
            dg_ref[...] = jnp.zeros_like(dg_ref)

        h2 = h1_ref[...] + _dot(_sq(hid_ref[...]), w2_ref[...])
        h2h, r2 = _rms(h2)
        g_ple = g_ref[...]
        hp = (h2h * g_ple).astype(MXU)
        hp_ref[...] = hp
        gate = _sigmoid(_dot(hp, wg_ref[...]))
        pb = p_ref[...].astype(MXU)
        pe = jnp.concatenate([_dot(pb, wp_ref[k]) for k in range(4)], axis=1)
        h3 = h2 + gate * pe
        hh, r = _rms(h3)
        gf = gf_ref[...]
        diff = hh * gf - t_ref[...]
        loss_ref[...] += 0.5 * jnp.sum(jnp.mean(diff * diff, axis=-1, keepdims=True))
        dout = diff * (1.0 / D)
        dgf_ref[...] += jnp.sum(dout * hh, axis=0, keepdims=True)
        dh3 = _rms_bwd(dout, hh, r, gf)
        dgl = (dh3 * pe * gate * (1.0 - gate)).astype(MXU)
        dgl_ref[...] = dgl
        dpe_ref[...] = (dh3 * gate).astype(MXU)
        dhp = _dot_nt(dgl, wg_ref[...])
        dg_ref[...] += jnp.sum(dhp * h2h, axis=0, keepdims=True)
        dh2 = dh3 + _rms_bwd(dhp, h2h, r2, g_ple)
        dh2_ref[...] = dh2
        dh2b_ref[...] = dh2.astype(MXU)

    return pl.pallas_call(
        body, grid=(T // tm,), name="ff2_tail",
        in_specs=[_rows(tm, DFF), _const((DFF, D)), _rows(tm, D), _const((1, D)), _rows(tm, DPLE), _rows(tm, D),
                  _const((D, D)), _const((4, DPLE, 256)), _const((1, D))],
        out_specs=[_rows(tm, D), _rows(tm, D), _rows(tm, D), _rows(tm, D), _rows(tm, D), _const((8, 128)),
                   _const((1, D)), _const((1, D))],
        out_shape=[_sds((T, D), MXU), _sds((T, D), MXU), _sds((T, D), MXU), _sds((T, D), F32), _sds((T, D), MXU),
                   _sds((8, 128), F32), _sds((1, D), F32), _sds((1, D), F32)],
        compiler_params=_cp(),
    )(hid, w2, h1, g_ple, p, tgt, wg, wp, gf)


def _ff2_bwd(dh2b, w2, hid, tm):
    T = hid.shape[0]

    def body(dh2b_ref, w2_ref, hid_ref, dpre_ref):
        d = dh2b_ref[...]
        for n in range(DFF // 1024):
            sl = slice(n * 1024, (n + 1) * 1024)
            da = _dot_nt(d, w2_ref[sl, :])
            dpre_ref[:, sl] = (2.0 * da * hid_ref[:, sl].astype(F32)).astype(MXU)

    return pl.pallas_call(
        body, grid=(T // tm,), name="ff2_bwd",
        in_specs=[_rows(tm, D), _const((DFF, D)), _rows(tm, DFF)],
        out_specs=_rows(tm, DFF),
        out_shape=_sds((T, DFF), MXU),
        compiler_params=_cp(),
    )(dh2b, w2, hid)


def _ff1_bwd(dpre, w1, dh2, h1, g, tm, after):
    T = h1.shape[0]

    def body(dpre_ref, w1_ref, dh2_ref, h1_ref, g_ref, dh1_ref, dh1b_ref, dg_ref):
        @pl.when(pl.program_id(0) == 0)
        def _():
            dg_ref[...] = jnp.zeros_like(dg_ref)

        dhn = _dot_nt(dpre_ref[:, 0:1024], w1_ref[0])
        for k in range(1, 4):
            dhn = dhn + _dot_nt(dpre_ref[:, k * 1024:(k + 1) * 1024], w1_ref[k])
        hh, r = _rms(h1_ref[...])
        dg_ref[...] += jnp.sum(dhn * hh, axis=0, keepdims=True)
        dh1 = dh2_ref[...] + _rms_bwd(dhn, hh, r, g_ref[...])
        dh1_ref[...] = dh1
        dh1b_ref[...] = dh1.astype(MXU)

    return pl.pallas_call(
        _after(5, body), grid=(T // tm,), name="ff1_bwd",
        in_specs=[_rows(tm, DFF), _const((4, D, 1024)), _rows(tm, D), _rows(tm, D), _const((1, D)), _ANY],
        out_specs=[_rows(tm, D), _rows(tm, D), _const((1, D))],
        out_shape=[_sds((T, D), F32), _sds((T, D), MXU), _sds((1, D), F32)],
        compiler_params=_cp(),
    )(dpre, w1, dh2, h1, g, after)


def _outproj_bwd(dh1b, wo, tm):
    T = dh1b.shape[0]

    def body(d_ref, wo_ref, dcat_ref):
        d = d_ref[...]
        dcat_ref[:, 0:1024] = _dot_nt(d, wo_ref[0:1024, :])
        dcat_ref[:, 1024:2048] = _dot_nt(d, wo_ref[1024:2048, :])

    return pl.pallas_call(
        body, grid=(T // tm,), name="outproj_bwd",
        in_specs=[_rows(tm, D), _const((2048, D))],
        out_specs=_rows(tm, 2048),
        out_shape=_sds((T, 2048), F32),
        compiler_params=_cp(),
    )(dh1b, wo)


def _gmlp_bwd(uv, dcat, gv, ws, bst, gout, wm):
    T = uv.shape[0]
    nck = 2 if T % (2 * CH) == 0 else 1
    tb = nck * CH

    def body(uv_ref, dya_ref, gv_ref, ws_ref, bst_ref, gout_ref, wuv_ref, duv_ref, dgv_ref, dws_ref, dbst_ref,
             dgo_ref, dxn_ref):
        @pl.when(pl.program_id(0) == 0)
        def _():
            dgv_ref[...] = jnp.zeros_like(dgv_ref)
            dws_ref[...] = jnp.zeros_like(dws_ref)
            dbst_ref[...] = jnp.zeros_like(dbst_ref)
            dgo_ref[...] = jnp.zeros_like(dgo_ref)

        for k in range(nck):
            chunk(slice(k * CH, (k + 1) * CH), uv_ref, dya_ref, gv_ref, ws_ref, bst_ref, gout_ref, duv_ref,
                  dgv_ref, dws_ref, dbst_ref, dgo_ref)
        dxn_ref[...] = _dot_nt(duv_ref[...], wuv_ref[...])

    def chunk(rows, uv_ref, dya_ref, gv_ref, ws_ref, bst_ref, gout_ref, duv_ref, dgv_ref, dws_ref, dbst_ref,
              dgo_ref):
        gv = gv_ref[...]
        f = _gmlp_fwd_vals(uv_ref[rows, 0:1024], uv_ref[rows, 1024:2048], gv, ws_ref, bst_ref[...], gout_ref[...])
        dya = dya_ref[rows, :]
        dgo_ref[...] += jnp.sum(dya * f["yhat"], axis=0, keepdims=True)
        dy = _rms_bwd(dya, f["yhat"], f["ry"], gout_ref[...])
        lane = lax.broadcasted_iota(jnp.int32, (CH, 128), 1)
        dbs = jnp.zeros((CH, 128), F32)
        dug, dvg, dgvs = [], [], []
        for h in range(GM_HEADS):
            sl = slice(h * 128, (h + 1) * 128)
            vhat, rv, vn, wt, mixed = f["heads"][h]
            dyh = dy[:, sl]
            dug.append(dyh * mixed)
            dmixed = dyh * f["ug"][:, sl]
            dmb = dmixed.astype(MXU)
            dws_ref[h] += jnp.where(f["tril"], _dot_nt(dmb, vn), 0.0)
            dbs = dbs + jnp.where(lane == h, jnp.sum(dmixed, axis=1, keepdims=True), 0.0)
            dvn = _dot_tn(wt.astype(MXU), dmb)
            dgvs.append(jnp.sum(dvn * vhat, axis=0, keepdims=True))
            dvg.append(_rms_bwd(dvn, vhat, rv, gv[:, sl]))
        dbst_ref[...] += dbs
        dgv_ref[...] += jnp.concatenate(dgvs, axis=1)
        duv_ref[rows, 0:1024] = (jnp.concatenate(dug, axis=1) * f["dug"]).astype(MXU)
        duv_ref[rows, 1024:2048] = (jnp.concatenate(dvg, axis=1) * f["dvg"]).astype(MXU)

    return pl.pallas_call(
        body, grid=(T // tb,), name="gmlp_bwd",
        in_specs=[_rows(tb, 2048), _rows(tb, 1024, 0), _const((1, 1024)),
                  _const((GM_HEADS, CH, CH)), _const((CH, 128)), _const((1, 1024)), _const((D, 2048))],
        out_specs=[_rows(tb, 2048), _const((1, 1024)), _const((GM_HEADS, CH, CH)), _const((CH, 128)),
                   _const((1, 1024)), _rows(tb, D)],
        out_shape=[_sds((T, 2048), MXU), _sds((1, 1024), F32), _sds((GM_HEADS, CH, CH), F32), _sds((CH, 128), F32),
                   _sds((1, 1024), F32), _sds((T, D), F32)],
        compiler_params=_cp(),
    )(uv, dcat, gv, ws, bst, gout, wm)


def _ssd_bwd(pz, pxbc, conv, dtraw, sall, dcat, convw, dtb, alog, dskip, ng, ex, ltri, ext, nb, after):
    T = pz.shape[0]
    S = T // nb
    nch = S // CH
    ns = _SEQS_PER_STEP if nb % _SEQS_PER_STEP == 0 else 1

    def seq(width, col=0):
        return pl.BlockSpec((ns, CH, width), lambda b, c: (b, nch - 1 - c, col))

    in_specs = [
        seq(1024), seq(CONV_CH), seq(CONV_CH), seq(128),
        _const((8, CONV_CH)), _const((1, 128)), _const((1, 128)), _const((1, 1024)),
        _const((1, 1024)), _const((128, 1024)), _const((CH, CH)),
        _const((1024, 128)),
        pl.BlockSpec((ns, 1, 128, 1024), lambda b, c: (b, nch - 1 - c, 0, 0)),
        seq(1024, 1),
        _ANY,
    ]

    def body(z_ref, xbc_ref, conv_ref, dt_ref, cw_ref, dtb_ref, al_ref, ds_ref, ng_ref, ex_ref, lt_ref,
             ext_ref, sall_ref, dyb_ref,
             dssd_ref, ddt_ref, dcw_ref, dcb_ref, ddtb_ref, dal_ref, dds_ref, dng_ref,
             dst_ref, dnext_ref, ddse_ref):
        b = pl.program_id(0)
        c = pl.program_id(1)

        @pl.when((b == 0) & (c == 0))
        def _():
            for r in (dcw_ref, dcb_ref, ddtb_ref, dal_ref, dds_ref, dng_ref, ddse_ref):
                r[...] = jnp.zeros_like(r)

        @pl.when(c == 0)
        def _():
            dst_ref[...] = jnp.zeros_like(dst_ref)
            dnext_ref[...] = jnp.zeros_like(dnext_ref)

        ex = ex_ref[...]
        ext = ext_ref[...]
        cw = cw_ref[...]
        ng = ng_ref[...]
        for i in range(ns):
            one_chunk(i, ex, ext, cw, ng, z_ref, xbc_ref, conv_ref, dt_ref, dtb_ref, al_ref, ds_ref, lt_ref, sall_ref,
                      dyb_ref, dssd_ref, ddt_ref, dcw_ref, dcb_ref, ddtb_ref, dal_ref, dng_ref, dst_ref, dnext_ref,
                      ddse_ref)

        @pl.when((b == nb // ns - 1) & (c == nch - 1))
        def _():
            dds_ref[...] = _dot_01(jnp.broadcast_to(ddse_ref[...], (8, 1024)), ext)[0:1]

    def one_chunk(i, ex, ext, cw, ng, z_ref, xbc_ref, conv_ref, dt_ref, dtb_ref, al_ref, ds_ref, lt_ref, sall_ref,
                  dyb_ref, dssd_ref, ddt_ref, dcw_ref, dcb_ref, ddtb_ref, dal_ref, dng_ref, dst_ref, dnext_ref,
                  ddse_ref):
        z = z_ref[i]
        s_prev = sall_ref[i, 0]
        conv = conv_ref[i]
        f = _ssd_fwd_vals(z, conv, dt_ref[i], dtb_ref[...], al_ref[...], ds_ref[...], ng, ex, lt_ref[...], s_prev)
        xs, xdt, cs, dec, dt = f["xs"], f["xdt"], f["cs"], f["dec"], f["dt"]
        dyb = dyb_ref[i]
        dyg, dngs = [], []
        for g in range(2):
            sl = slice(g * 512, (g + 1) * 512)
            dngs.append(jnp.sum(dyb[:, sl] * f["yhat"][g], axis=0, keepdims=True))
            dyg.append(_rms_bwd(dyb[:, sl], f["yhat"][g], f["rr"][g], ng[:, sl]))
        dng_ref[...] += jnp.concatenate(dngs, axis=1)
        dyg = jnp.concatenate(dyg, axis=1)
        sig_z = f["sig_z"]
        silu_z = z * sig_z
        dy = dyg * silu_z
        dz = dyg * f["ypre"] * (sig_z + silu_z * (1.0 - sig_z))
        ddse_ref[...] += jnp.sum(dy * xs, axis=0, keepdims=True)
        dxs = dy * f["de"]
        dye = dy * f["ecse"]
        dyeb = dye.astype(MXU)
        dst = dst_ref[i]
        dstb = dst.astype(MXU)
        bmb, cmb, sb, xdec = f["bmb"], f["cmb"], f["sb"], f["xdec"]
        u = jnp.concatenate([_dot(bmb[g], dstb[:, g * 512:(g + 1) * 512]) for g in range(2)], axis=1)
        dxdt = [u[:, q * 128:(q + 1) * 128] * f["dece"][:, q * 128:(q + 1) * 128] for q in range(8)]
        per_head = _dot_01(jnp.concatenate(
            [dy * f["yo"], u * xdt, jnp.broadcast_to(jnp.sum(dst * s_prev, axis=0, keepdims=True), (8, 1024))],
            axis=0), ext)
        dcs = per_head[0:CH]
        t = per_head[CH:2 * CH] * dec
        dcd = per_head[2 * CH:2 * CH + 1]
        row = lax.broadcasted_iota(jnp.int32, (CH, 128), 0)
        lane = lax.broadcasted_iota(jnp.int32, (CH, 128), 1)
        cd = jnp.exp(f["last"])
        dcs = dcs - t + jnp.where(row == CH - 1, jnp.sum(t, axis=0, keepdims=True) + dcd * cd, 0.0)
        dcst = jnp.zeros((128, CH), F32)
        lo = f["lo"]
        dbm, dcm, ds_prev = [], [], []
        for g in range(2):
            sl = slice(g * 512, (g + 1) * 512)
            dmg = jnp.zeros((CH, CH), F32)
            for q in range(4 * g, 4 * g + 4):
                dyq = dy[:, q * 128:(q + 1) * 128]
                xq = xdt[:, q * 128:(q + 1) * 128].astype(MXU)
                for hh in range(2):
                    h = 2 * q + hh
                    m = lo if hh == 0 else ~lo
                    dym = jnp.where(m, dyq, 0.0).astype(MXU)
                    gh = _dot_nt(dym, xq)
                    gl = gh * f["lms"][h]
                    dmg = dmg + gl
                    qh = gl * f["mg"][g]
                    dcs = dcs + jnp.where(lane == h, jnp.sum(qh, axis=1, keepdims=True), 0.0)
                    dcst = dcst - jnp.where(row == h, jnp.sum(qh, axis=0, keepdims=True), 0.0)
                    dxdt[q] = dxdt[q] + _dot_tn(f["whs"][h], dym)
            dmgb = dmg.astype(MXU)
            dcm.append(_dot(dmgb, bmb[g]) + _dot_nt(dyeb[:, sl], sb[:, sl]))
            dbm.append(_dot_tn(dmgb, cmb[g]) + _dot_nt(xdec[:, sl], dstb[:, sl]))
            ds_prev.append(_dot_tn(cmb[g], dyeb[:, sl]))
        dst_ref[i] = jnp.concatenate(ds_prev, axis=1) + dst * f["cde"]
        dcs = dcs + dcst.T
        da = _dot_hi(lt_ref[...].T, dcs)
        dxdt = jnp.concatenate(dxdt, axis=1)
        a_neg = f["a_neg"]
        ddt = da * a_neg + _dot_01(dxdt * xs, ext)
        dal_ref[...] += jnp.sum(da * dt, axis=0, keepdims=True) * a_neg
        dxs = dxs + dxdt * f["dte"]
        ddtraw = jnp.where(lane < SSD_HEADS, ddt * _sigmoid(f["dtpre"]), 0.0)
        ddtb_ref[...] += jnp.sum(ddtraw, axis=0, keepdims=True)
        ddt_ref[i] = ddtraw.astype(MXU)
        dxa = jnp.concatenate([dxs, dbm[0], dbm[1], dcm[0], dcm[1]], axis=1)
        sig_c = f["sig_c"]
        dconv = dxa * (sig_c + f["xa"] * (1.0 - sig_c))
        dcb_ref[...] += jnp.sum(dconv, axis=0, keepdims=True)
        xbc = xbc_ref[i]
        dcw_ref[3:4, :] += jnp.sum(dconv * xbc, axis=0, keepdims=True)
        dxbc = cw[3:4] * dconv
        for j, up in zip((1, 2, 3), _shifts_up(dconv, dnext_ref[i])):
            dcw_ref[3 - j:4 - j, :] += jnp.sum(up * xbc, axis=0, keepdims=True)
            dxbc = dxbc + cw[3 - j:4 - j] * up
        dnext_ref[i] = dconv[0:8]
        dssd_ref[i, :, 0:1024] = dz.astype(MXU)
        dssd_ref[i, :, 1024:2560] = dxbc.astype(MXU)

    dssd, ddt, *small = pl.pallas_call(
        _after(14, body), grid=(nb // ns, nch), name="ssd_bwd",
        in_specs=in_specs,
        out_specs=[seq(2560), seq(128),
                   _const((8, CONV_CH)), _const((1, CONV_CH)), _const((1, 128)), _const((1, 128)), _const((1, 128)),
                   _const((1, 1024))],
        out_shape=[_sds((nb, S, 2560), MXU), _sds((nb, S, 128), MXU), _sds((8, CONV_CH), F32),
                   _sds((1, CONV_CH), F32), _sds((1, 128), F32), _sds((1, 128), F32), _sds((1, 128), F32),
                   _sds((1, 1024), F32)],
        scratch_shapes=[pltpu.VMEM((ns, 128, 1024), F32), pltpu.VMEM((ns, 8, CONV_CH), F32),
                        pltpu.VMEM((1, 1024), F32)],
        compiler_params=_cp(2),
    )(pz.reshape(nb, S, 1024), pxbc.reshape(nb, S, CONV_CH), conv.reshape(nb, S, CONV_CH), dtraw.reshape(nb, S, 128),
      convw, dtb, alog, dskip, ng, ex, ltri, ext, sall, dcat.reshape(nb, S, 2048), after)
    return (dssd.reshape(T, 2560), ddt.reshape(T, 128), *small)


def _inproj_bwd(dxn_uv, dssd, ddt, wm, wdt, dh1, x, g, tm, after):
    T = x.shape[0]

    def body(dxnuv_ref, dssd_ref, ddt_ref, wm_ref, wdt_ref, dh1_ref, x_ref, g_ref, dx_ref, dg_ref):
        @pl.when(pl.program_id(0) == 0)
        def _():
            dg_ref[...] = jnp.zeros_like(dg_ref)

        dxn = (dxnuv_ref[...] + _dot_nt(dssd_ref[...], wm_ref[:, 2048:N_MAIN])
               + _dot_nt(ddt_ref[...], wdt_ref[...]))
        xh, r = _rms(x_ref[...])
        dg_ref[...] += jnp.sum(dxn * xh, axis=0, keepdims=True)
        dx_ref[...] = dh1_ref[...] + _rms_bwd(dxn, xh, r, g_ref[...])

    return pl.pallas_call(
        _after(8, body), grid=(T // tm,), name="inproj_bwd",
        in_specs=[_rows(tm, D), _rows(tm, 2560), _rows(tm, 128), _const((D, N_MAIN)), _const((D, 128)),
                  _rows(tm, D), _rows(tm, D), _const((1, D)), _ANY],
        out_specs=[_rows(tm, D), _const((1, D))],
        out_shape=[_sds((T, D), F32), _sds((1, D), F32)],
        compiler_params=_cp(),
    )(dxn_uv, dssd, ddt, wm, wdt, dh1, x, g, after)


def _matmul_tn(a, b, name, a_fn=None):
    T, M = a.shape
    N = b.shape[1]
    tm = min(M, 1024)
    tn = 1280 if N == 2560 else min(N, 1024)
    tk = min(T, 2048)

    def body(a_ref, b_ref, o_ref, acc_ref):
        k = pl.program_id(2)

        @pl.when(k == 0)
        def _():
            acc_ref[...] = jnp.zeros_like(acc_ref)

        av = a_ref[...]
        if a_fn is not None:
            av = a_fn(av)
        acc_ref[...] += _dot_tn(av, b_ref[...])

        @pl.when(k == T // tk - 1)
        def _():
            o_ref[...] = acc_ref[...].astype(o_ref.dtype)

    return pl.pallas_call(
        body, grid=(M // tm, N // tn, T // tk), name=name,
        in_specs=[pl.BlockSpec((tk, tm), lambda i, j, k: (k, i)), pl.BlockSpec((tk, tn), lambda i, j, k: (k, j))],
        out_specs=pl.BlockSpec((tm, tn), lambda i, j, k: (i, j)),
        out_shape=_sds((M, N), GRAD),
        scratch_shapes=[pltpu.VMEM((tm, tn), F32)],
        compiler_params=_cp(3),
    )(a, b)


def _adamw_vals(w, g, m, v):
    m = B1 * m + (1.0 - B1) * g
    v = B2 * v + (1.0 - B2) * (g * g)
    m_hat = m / (1.0 - B1 ** STEP)
    v_hat = v / (1.0 - B2 ** STEP)
    return -LR * (m_hat / (jnp.sqrt(v_hat) + ADAM_EPS) + WD * w), m, v


def _adamw(w, g, m, v, name):
    R, C = w.shape
    tr = 256 if R % 256 == 0 else R

    def body(w_ref, g_ref, m_ref, v_ref, d_ref, mo_ref, vo_ref):
        d_ref[...], mo_ref[...], vo_ref[...] = _adamw_vals(w_ref[...], g_ref[...], m_ref[...], v_ref[...])

    spec = _rows(tr, C)
    return pl.pallas_call(
        body, grid=(R // tr,), name=name,
        in_specs=[spec] * 4, out_specs=[spec] * 3, out_shape=[_sds((R, C), F32)] * 3,
        compiler_params=_cp(),
    )(w, g, m, v)


_PARTS = 4


def _adamw_halves(items, name):
    n = len(items)

    def body(*refs):
        mine = (pl.program_id(0) // _PARTS) == lax.axis_index("c")
        for k in range(n):
            w_ref, own_ref, oth_ref, m_ref, v_ref = refs[5 * k:5 * k + 5]
            g_ref, d_ref, mo_ref, vo_ref = refs[5 * n + 4 * k:5 * n + 4 * k + 4]
            g = jnp.where(mine, own_ref[...], oth_ref[...])
            g_ref[...] = g
            d_ref[...], mo_ref[...], vo_ref[...] = _adamw_vals(w_ref[...], g, m_ref[...], v_ref[...])

    in_specs, out_specs, out_shape = [], [], []
    for w, *_ in items:
        R, C = w.shape
        full = _rows(R // (2 * _PARTS), C)
        part = pl.BlockSpec((R // (2 * _PARTS), C), lambda i: (i % _PARTS, 0))
        in_specs += [full, part, part, full, full]
        out_specs += [full] * 4
        out_shape += [_sds((R, C), F32)] * 4
    res = pl.pallas_call(
        body, grid=(2 * _PARTS,), name=name, in_specs=in_specs, out_specs=out_specs, out_shape=out_shape,
        compiler_params=_cp(),
    )(*[a for item in items for a in item])
    return [tuple(res[4 * k:4 * k + 4]) for k in range(n)]


_TJ = 128


def _adamw_transposed(w, own, other, m, v, name):
    C, _, R = w.shape

    def body(w_ref, own_ref, oth_ref, m_ref, v_ref, g_ref, d_ref, mo_ref, vo_ref):
        first = lax.axis_index("c") == 0
        g = jnp.concatenate([jnp.where(first, own_ref[...], oth_ref[...]),
                             jnp.where(first, oth_ref[...], own_ref[...])], axis=0).T
        d, mo, vo = _adamw_vals(w_ref[:, 0, :], g, m_ref[:, 0, :], v_ref[:, 0, :])
        for ref, val in ((g_ref, g), (d_ref, d), (mo_ref, mo), (vo_ref, vo)):
            ref[:, 0, :] = val

    cols = pl.BlockSpec((_TJ, 1, R), lambda j: (j, 0, 0))
    half = pl.BlockSpec((R // 2, _TJ), lambda j: (0, j))
    return pl.pallas_call(
        body, grid=(pl.cdiv(C, _TJ),), name=name,
        in_specs=[cols, half, half, cols, cols], out_specs=[cols] * 4, out_shape=[_sds((C, 1, R), F32)] * 4,
        compiler_params=_cp(),
    )(w, own, other, m, v)


def _sum_small(slots, name):
    nd, rows, C = slots.shape

    def body(s_ref, o_ref):
        acc = s_ref[0]
        for d in range(1, nd):
            acc = acc + s_ref[d]
        o_ref[...] = acc

    return pl.pallas_call(
        body, grid=(1,), name=name,
        in_specs=[_const((nd, rows, C))], out_specs=_const((rows, C)), out_shape=_sds((rows, C), F32),
        compiler_params=_cp(),
    )(slots)


def _sum_slots(items, kh, name):
    n = len(items)
    in_specs, out_specs, out_shape = [], [], []
    for slots, src, kind, (R, C) in items:
        tr = R // (2 * _PARTS)
        if kind == "slab":
            src_spec = pl.BlockSpec((1, tr, C), lambda i, kh: (kh[0], kh[1] * _PARTS + i, 0))
        elif kind == "rows":
            src_spec = pl.BlockSpec((tr, C), lambda i, kh: (kh[0] * (2 * _PARTS) + kh[1] * _PARTS + i, 0))
        else:
            src_spec = pl.BlockSpec((tr, C), lambda i, kh: (kh[1] * _PARTS + i, kh[0]))
        in_specs += [pl.BlockSpec((8, tr, C), lambda i, kh: (0, i, 0)), src_spec]
        out_specs.append(pl.BlockSpec((tr, C), lambda i, kh: (i, 0)))
        out_shape.append(_sds((R // 2, C), F32))

    def body(kh_ref, *refs):
        me = 2 * kh_ref[0] + kh_ref[1]
        for k, (_, _, kind, _) in enumerate(items):
            s_ref, own_ref, o_ref = refs[2 * k], refs[2 * k + 1], refs[2 * n + k]
            acc = (own_ref[0] if kind == "slab" else own_ref[...]).astype(F32)
            for j in range(1, 8):
                acc = acc + s_ref[me ^ j].astype(F32)
            o_ref[...] = acc

    return pl.pallas_call(
        body, name=name,
        grid_spec=pltpu.PrefetchScalarGridSpec(
            num_scalar_prefetch=1, grid=(_PARTS,), in_specs=in_specs, out_specs=out_specs),
        out_shape=out_shape,
        compiler_params=_cp(),
    )(kh, *[a for slots, src, _, _ in items for a in (slots, src)])


def _assemble_w_in(slabs):
    tr = 256

    def body(s_ref, wm_ref, wdt_ref):
        full = jnp.concatenate([s_ref[k] for k in range(4)], axis=1)
        wm_ref[...] = full[:, :N_MAIN]
        wdt_ref[...] = jnp.concatenate([full[:, N_MAIN:], jnp.zeros((tr, 128 - 16), full.dtype)], axis=1)

    return pl.pallas_call(
        body, grid=(D // tr,), name="assemble_w_in",
        in_specs=[pl.BlockSpec((4, tr, 1156), lambda i: (0, i, 0))],
        out_specs=[_rows(tr, N_MAIN), _rows(tr, 128)],
        out_shape=[_sds((D, N_MAIN), slabs.dtype), _sds((D, 128), slabs.dtype)],
        compiler_params=_cp(),
    )(slabs)


def _split_dw_in(d_uv, d_ssd, d_dt):
    tr = 256

    def body(uv_ref, ssd_ref, dt_ref, o_ref):
        full = jnp.concatenate([uv_ref[...], ssd_ref[...], dt_ref[:, 0:16]], axis=1)
        for k in range(4):
            o_ref[k] = full[:, 1156 * k:1156 * (k + 1)]

    return pl.pallas_call(
        body, grid=(D // tr,), name="split_dw_in",
        in_specs=[_rows(tr, 2048), _rows(tr, 2560), _rows(tr, 128)],
        out_specs=pl.BlockSpec((4, tr, 1156), lambda i: (0, i, 0)),
        out_shape=_sds((4, D, 1156), d_uv.dtype),
        compiler_params=_cp(),
    )(d_uv, d_ssd, d_dt)


def _cast_w_in(w, kh):
    C, _, R = w.shape

    def body(kh_ref, w_ref, o_ref):
        o_ref[0] = w_ref[:, 0, :].T.astype(BF16)

    return pl.pallas_call(
        body, name="cast_w_in",
        grid_spec=pltpu.PrefetchScalarGridSpec(
            num_scalar_prefetch=1, grid=(pl.cdiv(C, _TJ),),
            in_specs=[pl.BlockSpec((_TJ, 1, R), lambda j, kh: (j, 0, 0))],
            out_specs=pl.BlockSpec((1, R, _TJ), lambda j, kh: (kh[0], 0, j))),
        out_shape=_sds((4, R, C), BF16),
        compiler_params=_cp(),
    )(kh, w)


def _cast_into_slot(ws, kh, name):
    n = len(ws)

    def body(kh_ref, *refs):
        for k in range(n):
            refs[n + k][0] = refs[k][...].astype(BF16)

    return pl.pallas_call(
        body, name=name,
        grid_spec=pltpu.PrefetchScalarGridSpec(
            num_scalar_prefetch=1, grid=(_PARTS,),
            in_specs=[pl.BlockSpec((w.shape[0] // _PARTS, w.shape[1]), lambda i, kh: (i, 0)) for w in ws],
            out_specs=[pl.BlockSpec((1, w.shape[0] // _PARTS, w.shape[1]), lambda i, kh: (kh[0], i, 0))
                       for w in ws]),
        out_shape=[_sds((4,) + w.shape, BF16) for w in ws],
        compiler_params=_cp(),
    )(kh, *ws)


_ANY = pl.BlockSpec(memory_space=pl.ANY)
_CHIP_FLIPS = [(1, 0), (0, 1), (1, 1)]
_DEVICE_FLIPS = [(fx, fy, fc) for fx in (0, 1) for fy in (0, 1) for fc in (0, 1)][1:]


def _half(h, rows):
    return pl.ds(pl.multiple_of(h * rows, rows), rows)


def _remote(src, dst, ssem, rsem, to):
    return pltpu.make_async_remote_copy(src_ref=src, dst_ref=dst, send_sem=ssem, recv_sem=rsem,
                                        device_id=to, device_id_type=MESH)


def _weight_gather(bufs, conv):
    n = len(bufs)

    def body(*refs):
        conv_ref, outs, conv_out = refs[n], refs[n + 1:2 * n + 1], refs[2 * n + 1]
        send_sems, recv_sems, fsend_sems, frecv_sems, csend_sems, crecv_sems, local_sem = refs[2 * n + 2:]
        x, y, c = lax.axis_index("x"), lax.axis_index("y"), lax.axis_index("c")
        me = 2 * x + y
        halves = [_half(c, r.shape[1] // 2) for r in outs]
        others = [_half(1 - c, r.shape[1] // 2) for r in outs]
        remote = _remote
        local = [pltpu.make_async_copy(conv_ref, conv_out.at[me], local_sem)]
        for cp in local:
            cp.start()
        sends = []
        for k, (fx, fy) in enumerate(_CHIP_FLIPS):
            peer = (x ^ fx, y ^ fy, c)
            for i in range(n):
                mine = outs[i].at[me, halves[i]]
                sends.append(remote(mine, mine, send_sems.at[k * n + i], recv_sems.at[k * n + i], peer))
            sends.append(remote(conv_ref, conv_out.at[me], csend_sems.at[k], crecv_sems.at[k], peer))
        for cp in sends:
            cp.start()
        sibling = (x, y, 1 - c)
        forwards = []
        for k, (fx, fy) in enumerate(_CHIP_FLIPS):
            peer = (x ^ fx, y ^ fy, c)
            src = 2 * (x ^ fx) + (y ^ fy)
            for i in range(n):
                landed = outs[i].at[src, halves[i]]
                remote(landed, landed, send_sems.at[k * n + i], recv_sems.at[k * n + i], peer).wait_recv()
                fw = remote(landed, landed, fsend_sems.at[k * n + i], frecv_sems.at[k * n + i], sibling)
                fw.start()
                forwards.append(fw)
            remote(conv_out.at[src], conv_out.at[src], csend_sems.at[k], crecv_sems.at[k], peer).wait_recv()
        for k, (fx, fy) in enumerate(_CHIP_FLIPS):
            src = 2 * (x ^ fx) + (y ^ fy)
            for i in range(n):
                theirs = outs[i].at[src, others[i]]
                remote(theirs, theirs, fsend_sems.at[k * n + i], frecv_sems.at[k * n + i], sibling).wait_recv()
        for cp in sends + forwards:
            cp.wait_send()
        for cp in local:
            cp.wait()

    dma = pltpu.SemaphoreType.DMA
    return pl.pallas_call(
        body, name="weight_gather",
        in_specs=[_ANY] * (n + 1), out_specs=[_ANY] * (n + 1),
        out_shape=[_sds(b.shape, b.dtype) for b in bufs] + [_sds((4,) + conv.shape, conv.dtype)],
        input_output_aliases={i: i for i in range(n)},
        scratch_shapes=[dma((3 * n,)), dma((3 * n,)), dma((3 * n,)), dma((3 * n,)), dma((3,)), dma((3,)), dma],
    )(*bufs, conv)


def _piece(ref, kind, R, C, k, h):
    if kind == "slab":
        return ref.at[k, _half(h, R // 2), :]
    if kind == "rows":
        return ref.at[pl.ds(pl.multiple_of(k * R + h * (R // 2), R // 2), R // 2), :]
    return ref.at[_half(h, R // 2), pl.ds(pl.multiple_of(k * C, C), C)]


def _small_exchange(small, after):
    rs = small.shape[0]

    def body(s_ref, after_ref, out_ref, send_sems, recv_sems, local_sem):
        del after_ref
        x, y, c = lax.axis_index("x"), lax.axis_index("y"), lax.axis_index("c")
        slot = 4 * x + 2 * y + c
        own = pltpu.make_async_copy(s_ref, out_ref.at[slot], local_sem)
        own.start()
        copies = []
        for k, (fx, fy, fc) in enumerate(_DEVICE_FLIPS):
            copies.append(_remote(s_ref, out_ref.at[slot], send_sems.at[k], recv_sems.at[k], (x ^ fx, y ^ fy, c ^ fc)))
        for cp in copies:
            cp.start()
        for k, (fx, fy, fc) in enumerate(_DEVICE_FLIPS):
            theirs = out_ref.at[slot ^ (k + 1)]
            _remote(theirs, theirs, send_sems.at[k], recv_sems.at[k], (x ^ fx, y ^ fy, c ^ fc)).wait_recv()
        for cp in copies:
            cp.wait_send()
        own.wait()

    dma = pltpu.SemaphoreType.DMA
    return pl.pallas_call(
        body, name="small_exchange",
        in_specs=[_ANY, _ANY], out_specs=_ANY, out_shape=_sds((8, rs, 128), F32),
        scratch_shapes=[dma((7,)), dma((7,)), dma],
    )(small, after)


_HBM = pl.BlockSpec(memory_space=pltpu.HBM)
_SEM = pl.BlockSpec(memory_space=pltpu.SEMAPHORE)


def _split_start(name, arrays, n_copies, plan, after=None):
    n = len(arrays)
    extra = [] if after is None else [after]

    def body(*refs):
        m = n + len(extra)
        arrs, send_sems, recv_sems, token = refs[:n], refs[m], refs[m + 1], refs[-1]
        for j, (src, dst, peer) in enumerate(plan(arrs)):
            _remote(src, dst, send_sems.at[j], recv_sems.at[j], peer).start()
        token[...] = jnp.zeros_like(token)

    dma = pltpu.SemaphoreType.DMA
    res = pl.pallas_call(
        body, name=name,
        out_shape=(dma((n_copies,)), dma((n_copies,)), *[pltpu.HBM(a.shape, a.dtype) for a in arrays],
                   _sds((8, 128), F32)),
        in_specs=[_HBM] * n + [_ANY] * len(extra),
        out_specs=(_SEM, _SEM, *[_HBM] * n, pl.BlockSpec(memory_space=pltpu.VMEM)),
        input_output_aliases={i: 2 + i for i in range(n)},
        compiler_params=pltpu.CompilerParams(has_side_effects=pltpu.SideEffectType.DATAFLOW_SIDE_EFFECTING),
    )(*[pltpu.with_memory_space_constraint(a, pltpu.HBM) for a in arrays], *extra)
    return res[0], res[1], list(res[2:2 + n]), res[-1]


def _split_wait(name, arrays, send_sems, recv_sems, plan, after):
    n = len(arrays)

    def body(*refs):
        arrs, ssems, rsems = refs[:n], refs[n], refs[n + 1]
        for j, (src, dst, peer) in enumerate(plan(arrs)):
            cp = _remote(src, dst, ssems.at[j], rsems.at[j], peer)
            cp.wait_send()
            cp.wait_recv()

    return list(pl.pallas_call(
        body, name=name,
        out_shape=tuple(pltpu.HBM(a.shape, a.dtype) for a in arrays),
        in_specs=[_HBM] * n + [_SEM, _SEM, _ANY],
        out_specs=tuple([_HBM] * n),
        input_output_aliases={i: i for i in range(n)},
        compiler_params=pltpu.CompilerParams(has_side_effects=pltpu.SideEffectType.DATAFLOW_SIDE_EFFECTING),
    )(*arrays, send_sems, recv_sems, after))


def _gather_plan(n):
    def plan(bufs):
        x, y, c = lax.axis_index("x"), lax.axis_index("y"), lax.axis_index("c")
        me = 2 * x + y
        return [(bufs[i].at[me], bufs[i].at[me], (x ^ fx, y ^ fy, c)) for fx, fy in _CHIP_FLIPS for i in range(n)]

    return plan


def _reduce_plan(specs, n_small):
    n = len(specs)

    def plan(arrs):
        x, y, c = lax.axis_index("x"), lax.axis_index("y"), lax.axis_index("c")
        slot = 4 * x + 2 * y + c
        out = []
        for fx, fy, fc in _DEVICE_FLIPS:
            peer = (x ^ fx, y ^ fy, c ^ fc)
            for i, (kind, (R, C)) in enumerate(specs):
                out.append((_piece(arrs[i], kind, R, C, 2 * peer[0] + peer[1], peer[2]), arrs[n + i].at[slot], peer))
            for s in range(n_small):
                out.append((arrs[2 * n + 2 * s], arrs[2 * n + 2 * s + 1].at[slot], peer))
        return out

    return plan


def _sibling_exchange(halves, name):
    n = len(halves)

    def body(*refs):
        ins, outs, send_sems, recv_sems = refs[:n], refs[n:2 * n], refs[2 * n], refs[2 * n + 1]
        sibling = (lax.axis_index("x"), lax.axis_index("y"), 1 - lax.axis_index("c"))
        copies = [pltpu.make_async_remote_copy(src_ref=ins[i], dst_ref=outs[i], send_sem=send_sems.at[i],
                                               recv_sem=recv_sems.at[i], device_id=sibling, device_id_type=MESH)
                  for i in range(n)]
        for cp in copies:
            cp.start()
        for cp in copies:
            cp.wait()

    dma = pltpu.SemaphoreType.DMA
    return pl.pallas_call(
        body, name=name,
        in_specs=[_ANY] * n, out_specs=[_ANY] * n,
        out_shape=[_sds(h.shape, h.dtype) for h in halves],
        scratch_shapes=[dma((n,)), dma((n,))],
    )(*halves)


_BIG = [("w_in", (1024, 1156), "slab"), ("w_out", (512, 1024), "rows"), ("w_ff1", (1024, 1024), "cols"),
        ("w_ff2", (1024, 1024), "rows"), ("w_ple_gate", (256, 1024), "rows"), ("w_ple_proj", (256, 256), "cols")]
_SMALL = [("norm_mix_g", (1, 1024)), ("gm_v_norm_g", (1, 1024)), ("gm_ws", (1, 8, 128, 128)), ("gm_bs", (1, 8, 128)),
          ("gm_out_norm_g", (1, 1024)), ("ssd_conv_w", (1, 4, 1536)), ("ssd_conv_b", (1, 1536)),
          ("ssd_dt_bias", (1, 16)), ("ssd_a_log", (1, 16)), ("ssd_d", (1, 16)), ("ssd_norm_g", (1, 1024)),
          ("norm_mlp_g", (1, 1024)), ("ple_norm_g", (1, 1024)), ("final_norm_g", (1024,))]


def _rows128(a):
    flat = a.reshape(-1)
    rows = -(-flat.shape[0] // 1024) * 8
    return jnp.pad(flat, (0, rows * 128 - flat.shape[0])).reshape(rows, 128)


def _pad_lanes(v, n=128):
    v = v.reshape(1, -1)
    return jnp.pad(v, ((0, 0), (0, n - v.shape[1])))


_SMALL_SHAPES = dict(_SMALL + [("loss", ())])
_BIG_SPECS = {n: (kind, shp) for n, shp, kind in _BIG}


class _Comm:
    def __init__(self, a, kh):
        self.a, self.kh = a, kh
        rest = _BIG[1:]
        self.bufs = {"w_in": _cast_w_in(a["w_in"].transpose(2, 0, 1), kh)}
        cast = _cast_into_slot([a[n].reshape(shp) for n, shp, _ in rest], kh, "cast_rest")
        self.bufs.update({n: c for (n, _, _), c in zip(rest, cast)})
        self.sent = []
        self.small_tot = {}

    def w_in(self):
        g_win, g_cw = _weight_gather([self.bufs["w_in"]], self.a["ssd_conv_w"].reshape(4, 384))
        token = g_cw
        self.gather = {}
        for tag, names in (("out", ["w_out"]), ("ff", ["w_ff1", "w_ff2", "w_ple_gate", "w_ple_proj"])):
            plan = _gather_plan(len(names))
            ssem, rsem, thru, token = _split_start("gather_start_" + tag, [self.bufs[n] for n in names],
                                                   3 * len(names), plan, after=token)
            self.gather[tag] = (plan, ssem, rsem, thru)
        wm, wdt = _assemble_w_in(g_win)
        return wm, wdt, jnp.concatenate([g_cw[k] for k in range(4)], axis=1), token

    def rest(self, tag, after):
        plan, ssem, rsem, thru = self.gather[tag]
        got = _split_wait("gather_wait_" + tag, thru, ssem, rsem, plan, after)
        if tag == "out":
            return got[0].reshape(2048, D)
        g_w1, g_w2, g_wg, g_wp = got
        return g_w1, g_w2.reshape(DFF, D), g_wg.reshape(D, D), g_wp

    def send(self, tag, grads):
        big = [n for n, _, _ in _BIG if n in grads]
        small = [n for n in _SMALL_SHAPES if n in grads]
        parts = [_rows128(grads[n]) for n in small]
        rows = [s.shape[0] for s in parts]
        if not big:
            self.last_small = (tag, small, rows, jnp.concatenate(parts, axis=0))
            return None
        srcs = [grads[n] for n in big]
        lands = [lax.empty((8, _BIG_SPECS[n][1][0] // 2, _BIG_SPECS[n][1][1]), GRAD) for n in big]
        extra = []
        if small:
            pack = jnp.concatenate(parts, axis=0)
            extra = [pack, jnp.broadcast_to(pack, (8,) + pack.shape)]
        plan = _reduce_plan([_BIG_SPECS[n] for n in big], len(extra) // 2)
        n_copies = 7 * (len(big) + len(extra) // 2)
        ssem, rsem, thru, token = _split_start("reduce_start_" + tag, srcs + lands + extra, n_copies, plan)
        self.sent.append((tag, big, small, rows, plan, ssem, rsem, thru))
        return token

    def _unpack(self, tot, names, rows):
        o = 0
        for n, r in zip(names, rows):
            shp = _SMALL_SHAPES[n]
            cnt = 1
            for s in shp:
                cnt *= s
            self.small_tot[n] = tot[o:o + r].reshape(-1)[:cnt].reshape(shp)
            o += r

    def finish(self, after):
        a, results = self.a, {}

        def update(names, own, tag):
            other = _sibling_exchange([own[n] for n in names], "sibling_exchange_" + tag)
            if names == ["w_in"]:
                w, m, v = (a[k].transpose(2, 0, 1) for k in ("w_in", "m_w_in", "v_w_in"))
                raw = _adamw_transposed(w, own["w_in"], other[0], m, v, "adamw_" + tag)
                results["w_in"] = tuple(r.transpose(1, 2, 0) for r in raw)
                return raw[1]
            items = [(a[n].reshape(_BIG_SPECS[n][1]), own[n], oth, a["m_" + n].reshape(_BIG_SPECS[n][1]),
                      a["v_" + n].reshape(_BIG_SPECS[n][1])) for n, oth in zip(names, other)]
            results.update(zip(names, _adamw_halves(items, "adamw_" + tag)))
            return results[names[-1]][1]

        own, early = {}, []
        for tag, big, small, rows, plan, ssem, rsem, thru in self.sent:
            if tag == self.sent[-1][0]:
                after = update(early, own, "early")
            arrs = _split_wait("reduce_wait_" + tag, thru, ssem, rsem, plan, after)
            nb_ = len(big)
            sums = _sum_slots([(arrs[nb_ + i], arrs[i]) + _BIG_SPECS[n] for i, n in enumerate(big)], self.kh,
                              "sum_" + tag)
            own.update(zip(big, sums))
            after = sums[-1]
            early += big
            if small:
                self._unpack(_sum_small(arrs[2 * nb_ + 1], "sum_small_" + tag), small, rows)
        after = update(self.sent[-1][1], own, "late")
        tag, small, rows, pack = self.last_small
        self._unpack(_sum_small(_small_exchange(pack, after), "sum_small_" + tag), small, rows)
        return results, dict(self.small_tot)


def _local_step(x, p, tgt, sm, comm, nb, tm):
    T = x.shape[0]
    wm, wdt, conv_w, token = comm.w_in()
    g_mix, gv, gout = sm["norm_mix_g"].reshape(1, D), sm["gm_v_norm_g"].reshape(1, D), sm["gm_out_norm_g"].reshape(1, D)
    ws = sm["gm_ws"].reshape(GM_HEADS, CH, CH)
    bst = jnp.pad(sm["gm_bs"].reshape(GM_HEADS, CH).T, ((0, 0), (0, 128 - GM_HEADS)))
    convw = jnp.pad(conv_w, ((0, 4), (0, 0)))
    convb = sm["ssd_conv_b"].reshape(1, CONV_CH)
    dtb, alog = _pad_lanes(sm["ssd_dt_bias"]), _pad_lanes(sm["ssd_a_log"])
    dskip = jnp.repeat(sm["ssd_d"].reshape(SSD_HEADS), SSD_P).reshape(1, 1024)
    ng, g_mlp, g_ple = sm["ssd_norm_g"].reshape(1, D), sm["norm_mlp_g"].reshape(1, D), sm["ple_norm_g"].reshape(1, D)
    gf = sm["final_norm_g"].reshape(1, D)
    head_of_lane = lax.broadcasted_iota(jnp.int32, (128, 1024), 1) // SSD_P
    ex = (lax.broadcasted_iota(jnp.int32, (128, 1024), 0) == head_of_lane).astype(BF16)
    ext = ex.T
    ltri = (lax.broadcasted_iota(jnp.int32, (CH, CH), 0) >= lax.broadcasted_iota(jnp.int32, (CH, CH), 1)).astype(F32)

    pz, pxbc, dtraw, xn, cat, uv = _inproj_gmlp(x, g_mix, wm, wdt, gv, ws, bst, gout, tm, token)
    cat, sall, conv = _ssd_fwd(pz, pxbc, dtraw, cat, convw, convb, dtb, alog, dskip, ng, ex, ltri, nb)
    wo = comm.rest("out", cat)
    h1, hn = _outproj(cat, wo, x, g_mlp, tm)
    w1, w2, wg, wp = comm.rest("ff", hn)
    hid = _ff1(hn, w1, min(T, 2 * tm))
    hp, dgl, dpe, dh2, dh2b, loss, d_gf, d_gple = _ff2_tail(hid, w2, h1, g_ple, p, tgt, wg, wp, gf, tm)

    d_wp = _matmul_tn(p, dpe, "dw_ple_proj", a_fn=lambda a: a.astype(MXU))
    d_wg = _matmul_tn(hp, dgl, "dw_ple_gate")
    d_w2 = _matmul_tn(hid, dh2b, "dw_ff2", a_fn=_sq)
    dpre = _ff2_bwd(dh2b, w2, hid, min(T, 2 * tm))
    d_w1 = _matmul_tn(hn, dpre, "dw_ff1")
    token = comm.send("a", {"w_ple_proj": d_wp, "w_ple_gate": d_wg, "w_ff2": d_w2, "w_ff1": d_w1})
    dh1, dh1b, d_gmlp = _ff1_bwd(dpre, w1, dh2, h1, g_mlp, tm, token)
    dcat = _outproj_bwd(dh1b, wo, min(T, 2 * tm))
    d_wo = _matmul_tn(cat, dh1b, "dw_out")
    duv, d_gv, d_ws, d_bst, d_gout, dxn_uv = _gmlp_bwd(uv, dcat, gv, ws, bst, gout, wm)
    token = comm.send("b", {
        "w_out": d_wo, "loss": loss[0:1, 0:1], "final_norm_g": d_gf, "ple_norm_g": d_gple, "norm_mlp_g": d_gmlp,
        "gm_v_norm_g": d_gv, "gm_ws": d_ws, "gm_bs": d_bst[:, :GM_HEADS].T, "gm_out_norm_g": d_gout})
    dssd, ddt, d_cw, d_cb, d_dtb, d_al, d_ds, d_ng = _ssd_bwd(
        pz, pxbc, conv, dtraw, sall, dcat, convw, dtb, alog, dskip, ng, ex, ltri, ext, nb, token)
    d_win = _split_dw_in(_matmul_tn(xn, duv, "dw_in_uv"), _matmul_tn(xn, dssd, "dw_in_ssd"),
                         _matmul_tn(xn, ddt, "dw_in_dt"))
    token = comm.send("c", {"w_in": d_win})
    dx, d_gmix = _inproj_bwd(dxn_uv, dssd, ddt, wm, wdt, dh1, x, g_mix, tm, token)
    comm.send("d", {"norm_mix_g": d_gmix, "ssd_conv_w": d_cw[0:4], "ssd_conv_b": d_cb, "ssd_dt_bias": d_dtb[:, :16],
                    "ssd_a_log": d_al[:, :16], "ssd_d": d_ds[:, :16], "ssd_norm_g": d_ng})
    return dx


def kernel(x, p, norm_mix_g, w_in, gm_v_norm_g, gm_ws, gm_bs, gm_out_norm_g, ssd_conv_w, ssd_conv_b, ssd_dt_bias, ssd_a_log, ssd_d, ssd_norm_g, w_out, norm_mlp_g, w_ff1, w_ff2, ple_norm_g, w_ple_gate, w_ple_proj, final_norm_g, loss_target, m_norm_mix_g, m_w_in, m_gm_v_norm_g, m_gm_ws, m_gm_bs, m_gm_out_norm_g, m_ssd_conv_w, m_ssd_conv_b, m_ssd_dt_bias, m_ssd_a_log, m_ssd_d, m_ssd_norm_g, m_w_out, m_norm_mlp_g, m_w_ff1, m_w_ff2, m_ple_norm_g, m_w_ple_gate, m_w_ple_proj, m_final_norm_g, v_norm_mix_g, v_w_in, v_gm_v_norm_g, v_gm_ws, v_gm_bs, v_gm_out_norm_g, v_ssd_conv_w, v_ssd_conv_b, v_ssd_dt_bias, v_ssd_a_log, v_ssd_d, v_ssd_norm_g, v_w_out, v_norm_mlp_g, v_w_ff1, v_w_ff2, v_ple_norm_g, v_w_ple_gate, v_w_ple_proj, v_final_norm_g):
    a = dict(locals())
    order = ["norm_mix_g", "w_in", "gm_v_norm_g", "gm_ws", "gm_bs", "gm_out_norm_g", "ssd_conv_w", "ssd_conv_b",
             "ssd_dt_bias", "ssd_a_log", "ssd_d", "ssd_norm_g", "w_out", "norm_mlp_g", "w_ff1", "w_ff2", "ple_norm_g",
             "w_ple_gate", "w_ple_proj", "final_norm_g"]
    chip = 2 * lax.axis_index("x") + lax.axis_index("y")
    nb, S = x.shape[0], x.shape[1]
    T = nb * S
    sm = {n: a[n] for n, _ in _SMALL if n != "ssd_conv_w"}
    comm = _Comm(a, jnp.stack([chip, lax.axis_index("c")]).astype(jnp.int32))
    dx = _local_step(x.reshape(T, D), p.reshape(T, DPLE), loss_target.reshape(T, D), sm, comm, nb, 512)
    big, g_out = comm.finish(dx)
    delta, new_m, new_v = {}, {}, {}
    for n, _, _ in _BIG:
        g_out[n], delta[n], new_m[n], new_v[n] = (r.reshape(a[n].shape) for r in big[n])
    g_out["ssd_conv_w"] = lax.dynamic_slice(g_out["ssd_conv_w"], (0, 0, chip * 384), (1, 4, 384))
    small_names = [n for n, _ in _SMALL]
    packs = [jnp.concatenate([_rows128(src(n)) for n in small_names], axis=0)
             for src in (lambda n: a[n], lambda n: g_out[n], lambda n: a["m_" + n], lambda n: a["v_" + n])]
    outs = _adamw(*packs, "adamw_small")
    o = 0
    for n in small_names:
        r = _rows128(a[n]).shape[0]
        cnt = a[n].size
        for dst, src in zip((delta, new_m, new_v), outs):
            dst[n] = src[o:o + r].reshape(-1)[:cnt].reshape(a[n].shape)
        o += r
    return (g_out["loss"], dx.reshape(x.shape), *[g_out[n] for n in order], *[delta[n] for n in order],
            *[new_m[n] for n in order], *[new_v[n] for n in order])
```

```python
import jax
import jax.numpy as jnp
from jax import lax
from jax.experimental import pallas as pl
from jax.experimental.pallas import tpu as pltpu

F32 = jnp.float32
BF16 = jnp.bfloat16
MXU = jnp.bfloat16
GRAD = jnp.bfloat16

D = 1024
CH = 128
GM_HEADS = 8
SSD_HEADS = 16
SSD_P = 64
CONV_CH = 1536
N_MAIN = 4608
DFF = 4096
DPLE = 256
EPS = 1e-6
NEG = -1e30

LR, B1, B2, ADAM_EPS, WD, STEP = 0.001, 0.9, 0.999, 1e-08, 0.01, 10

VMEM_LIMIT = 56 * 1024 * 1024
_SEQS_PER_STEP = 4
MESH = pl.DeviceIdType.MESH

INV_SQRT2 = 0.7071067811865476
INV_SQRT_2PI = 0.3989422804014327


def _cp(n_axes=1):
    return pltpu.CompilerParams(dimension_semantics=("arbitrary",) * n_axes, vmem_limit_bytes=VMEM_LIMIT)


def _dot(a, b):
    return jnp.dot(a, b, preferred_element_type=F32)


def _dot_nt(a, b):
    return lax.dot_general(a, b, (((1,), (1,)), ((), ())), preferred_element_type=F32)


def _dot_tn(a, b):
    return lax.dot_general(a, b, (((0,), (0,)), ((), ())), preferred_element_type=F32)


def _dot_hi(a, b):
    return jnp.dot(a, b, preferred_element_type=F32, precision=lax.Precision.HIGHEST)


def _dot_01(a, sel):
    hi = a.astype(BF16)
    lo = (a - hi.astype(F32)).astype(BF16)
    n = a.shape[0]
    r = _dot(jnp.concatenate([hi, lo], axis=0), sel)
    return r[0:n] + r[n:2 * n]


def _rows(tm, n, j=0):
    return pl.BlockSpec((tm, n), lambda i: (i, j))


def _const(shape):
    nd = len(shape)
    return pl.BlockSpec(shape, lambda *_: (0,) * nd)


def _sds(shape, dtype):
    return jax.ShapeDtypeStruct(shape, dtype)


def _rms(x):
    r = lax.rsqrt(jnp.mean(x * x, axis=-1, keepdims=True) + EPS)
    return x * r, r


def _rms_bwd(dy, xhat, r, g):
    dyg = dy * g
    return r * (dyg - xhat * jnp.mean(dyg * xhat, axis=-1, keepdims=True))


def _sigmoid(x):
    return 1.0 / (1.0 + jnp.exp(-x))


def _gelu(x):
    cdf = 0.5 * (1.0 + lax.erf(x * INV_SQRT2))
    pdf = jnp.exp(-0.5 * x * x) * INV_SQRT_2PI
    return x * cdf, cdf + x * pdf


def _softplus(x):
    e = jnp.exp(-jnp.abs(x))
    u = 1.0 + e
    log1p = jnp.where(u == 1.0, e, jnp.log(u) * e / (u - 1.0))
    return jnp.maximum(x, 0.0) + log1p


def _after(n_in, fn):
    def body(*refs):
        return fn(*refs[:n_in], *refs[n_in + 1:])

    return body


def _inproj_gmlp(x, g, wm, wdt, gv, ws, bst, gout, tm, after):
    T = x.shape[0]

    def body(x_ref, g_ref, wm_ref, wdt_ref, gv_ref, ws_ref, bst_ref, gout_ref,
             z_ref, xbc_ref, dt_ref, xn_ref, ya_ref, uv_ref):
        xh, _ = _rms(x_ref[...])
        xn = (xh * g_ref[...]).astype(MXU)
        xn_ref[...] = xn
        for n in range(4):
            uv_ref[:, n * 512:(n + 1) * 512] = _dot(xn, wm_ref[:, n * 512:(n + 1) * 512])
        for n in range(2):
            z_ref[:, n * 512:(n + 1) * 512] = _dot(xn, wm_ref[:, 2048 + n * 512:2048 + (n + 1) * 512])
        for n in range(3):
            xbc_ref[:, n * 512:(n + 1) * 512] = _dot(xn, wm_ref[:, 3072 + n * 512:3072 + (n + 1) * 512])
        dt_ref[...] = _dot(xn, wdt_ref[...])
        for k in range(tm // CH):
            rows = slice(k * CH, (k + 1) * CH)
            f = _gmlp_fwd_vals(uv_ref[rows, 0:1024], uv_ref[rows, 1024:2048], gv_ref[...], ws_ref, bst_ref[...],
                               gout_ref[...])
            ya_ref[rows, :] = f["out"].astype(MXU)

    return pl.pallas_call(
        _after(8, body), grid=(T // tm,), name="inproj_gmlp",
        in_specs=[_rows(tm, D), _const((1, D)), _const((D, N_MAIN)), _const((D, 128)), _const((1, 1024)),
                  _const((GM_HEADS, CH, CH)), _const((CH, 128)), _const((1, 1024)), _ANY],
        out_specs=[_rows(tm, 1024), _rows(tm, CONV_CH), _rows(tm, 128), _rows(tm, D), _rows(tm, 1024, 0),
                   _rows(tm, 2048)],
        out_shape=[_sds((T, 1024), F32), _sds((T, CONV_CH), F32), _sds((T, 128), F32), _sds((T, D), MXU),
                   _sds((T, 2048), MXU), _sds((T, 2048), F32)],
        compiler_params=_cp(),
    )(x, g, wm, wdt, gv, ws, bst, gout, after)


def _gmlp_fwd_vals(u, v, gv, ws_ref, bst, gout):
    ug, dug = _gelu(u)
    vg, dvg = _gelu(v)
    row = lax.broadcasted_iota(jnp.int32, (CH, CH), 0)
    col = lax.broadcasted_iota(jnp.int32, (CH, CH), 1)
    tril = row >= col
    ys, heads = [], []
    for h in range(GM_HEADS):
        sl = slice(h * 128, (h + 1) * 128)
        vhat, rv = _rms(vg[:, sl])
        vn = (vhat * gv[:, sl]).astype(MXU)
        wt = jnp.where(tril, ws_ref[h], 0.0)
        mixed = _dot(wt.astype(MXU), vn) + bst[:, h:h + 1]
        ys.append(ug[:, sl] * mixed)
        heads.append((vhat, rv, vn, wt, mixed))
    y = jnp.concatenate(ys, axis=1)
    yhat, ry = _rms(y)
    return dict(ug=ug, dug=dug, dvg=dvg, heads=heads, yhat=yhat, ry=ry, tril=tril, out=yhat * gout)


def _shifts_down(cur, halo):
    row8 = lax.broadcasted_iota(jnp.int32, (8, cur.shape[1]), 0)
    out = [cur]
    for j in (1, 2, 3):
        sh = pltpu.roll(cur, j, 0)
        top = jnp.where(row8 < j, pltpu.roll(halo, j, 0), sh[0:8])
        out.append(jnp.concatenate([top, sh[8:]], axis=0))
    return out


def _shifts_up(cur, halo):
    row8 = lax.broadcasted_iota(jnp.int32, (8, cur.shape[1]), 0)
    out = []
    for j in (1, 2, 3):
        sh = pltpu.roll(cur, CH - j, 0)
        bot = jnp.where(row8 + j >= 8, pltpu.roll(halo, 8 - j, 0), sh[CH - 8:CH])
        out.append(jnp.concatenate([sh[0:CH - 8], bot], axis=0))
    return out


def _conv(xbc, halo, convw, convb):
    sh = _shifts_down(xbc, halo)
    return convb + convw[3:4] * sh[0] + convw[2:3] * sh[1] + convw[1:2] * sh[2] + convw[0:1] * sh[3]


def _ssd_fwd_vals(z, conv, dtraw, dtb, alog, dskip, ng, ex, ltri, s_prev):
    sig_c = _sigmoid(conv)
    xa = conv * sig_c
    xs = xa[:, :1024]
    bm = [xa[:, 1024:1152], xa[:, 1152:1280]]
    cm = [xa[:, 1280:1408], xa[:, 1408:1536]]
    dtpre = dtraw + dtb
    dt = _softplus(dtpre)
    a_neg = -jnp.exp(alog)
    cs = _dot_hi(ltri, dt * a_neg)
    cst = cs.T
    last = cs[CH - 1:CH]
    ecs = jnp.exp(cs)
    dec = jnp.exp(last - cs)
    spread = _dot_01(jnp.concatenate([dt, ecs, dec], axis=0), ex)
    dte, ecse, dece = spread[0:CH], spread[CH:2 * CH], spread[2 * CH:3 * CH]
    cde = ecse[CH - 1:CH]
    de = dskip
    xdt = xs * dte
    row = lax.broadcasted_iota(jnp.int32, (CH, CH), 0)
    col = lax.broadcasted_iota(jnp.int32, (CH, CH), 1)
    tril = row >= col
    lo = col < SSD_P
    bmb = [b.astype(MXU) for b in bm]
    cmb = [c.astype(MXU) for c in cm]
    mg = [_dot_nt(cmb[g], bmb[g]) for g in range(2)]
    yd, lms, whs = [], [], []
    for q in range(8):
        g = q // 4
        xq = xdt[:, q * 128:(q + 1) * 128]
        acc = None
        for hh in range(2):
            h = 2 * q + hh
            seg = cs[:, h:h + 1] - cst[h:h + 1, :]
            lm = jnp.exp(jnp.where(tril, seg, NEG))
            wh = (mg[g] * lm).astype(MXU)
            xm = jnp.where(lo if hh == 0 else ~lo, xq, 0.0).astype(MXU)
            part = _dot(wh, xm)
            acc = part if acc is None else acc + part
            lms.append(lm)
            whs.append(wh)
        yd.append(acc)
    yd = jnp.concatenate(yd, axis=1)
    sb = s_prev.astype(MXU)
    yo = jnp.concatenate([_dot(cmb[g], sb[:, g * 512:(g + 1) * 512]) for g in range(2)], axis=1) * ecse
    xdec = (xdt * dece).astype(MXU)
    states = jnp.concatenate([_dot_tn(bmb[g], xdec[:, g * 512:(g + 1) * 512]) for g in range(2)], axis=1)
    s_next = s_prev * cde + states
    ypre = yd + yo + de * xs
    sig_z = _sigmoid(z)
    yg = ypre * z * sig_z
    outs, yhat, rr = [], [], []
    for g in range(2):
        sl = slice(g * 512, (g + 1) * 512)
        yh, r = _rms(yg[:, sl])
        yhat.append(yh)
        rr.append(r)
        outs.append(yh * ng[:, sl])
    return dict(sig_c=sig_c, xa=xa, xs=xs, bmb=bmb, cmb=cmb, dtpre=dtpre, dt=dt, a_neg=a_neg,
                cs=cs, last=last, ecs=ecs, dec=dec, dte=dte, ecse=ecse, dece=dece, cde=cde, de=de, xdt=xdt,
                mg=mg, lms=lms, whs=whs, lo=lo, yo=yo, sb=sb, xdec=xdec, s_next=s_next, ypre=ypre, sig_z=sig_z,
                yhat=yhat, rr=rr, out=jnp.concatenate(outs, axis=1))


def _ssd_fwd(pz, pxbc, dtraw, cat, convw, convb, dtb, alog, dskip, ng, ex, ltri, nb):
    T = pz.shape[0]
    S = T // nb
    nch = S // CH
    ns = _SEQS_PER_STEP if nb % _SEQS_PER_STEP == 0 else 1

    def body(z_ref, xbc_ref, halo_ref, dt_ref, cw_ref, cb_ref, dtb_ref, al_ref, ds_ref, ng_ref, ex_ref, lt_ref,
             cat_in_ref, yb_ref, sall_ref, conv_ref, s_ref):
        del cat_in_ref
        c = pl.program_id(1)

        @pl.when(c == 0)
        def _():
            s_ref[...] = jnp.zeros_like(s_ref)

        for i in range(ns):
            halo = jnp.where(c == 0, 0.0, halo_ref[i])
            s_prev = s_ref[i]
            sall_ref[i, 0] = s_prev
            conv = _conv(xbc_ref[i], halo, cw_ref[...], cb_ref[...])
            conv_ref[i] = conv
            f = _ssd_fwd_vals(z_ref[i], conv, dt_ref[i], dtb_ref[...], al_ref[...], ds_ref[...], ng_ref[...],
                              ex_ref[...], lt_ref[...], s_prev)
            s_ref[i] = f["s_next"]
            yb_ref[i] = f["out"].astype(MXU)

    def seq(width, col=0):
        return pl.BlockSpec((ns, CH, width), lambda b, c: (b, c, col))

    cat, sall, conv = pl.pallas_call(
        body, grid=(nb // ns, nch), name="ssd_fwd",
        in_specs=[seq(1024), seq(CONV_CH),
                  pl.BlockSpec((ns, 8, CONV_CH), lambda b, c: (b, jnp.maximum(c * (CH // 8) - 1, 0), 0)),
                  seq(128),
                  _const((8, CONV_CH)), _const((1, CONV_CH)), _const((1, 128)), _const((1, 128)), _const((1, 1024)),
                  _const((1, 1024)), _const((128, 1024)), _const((CH, CH)), _ANY],
        out_specs=[seq(1024, 1), pl.BlockSpec((ns, 1, 128, 1024), lambda b, c: (b, c, 0, 0)), seq(CONV_CH)],
        out_shape=[_sds((nb, S, 2048), MXU), _sds((nb, nch, 128, 1024), F32), _sds((nb, S, CONV_CH), F32)],
        scratch_shapes=[pltpu.VMEM((ns, 128, 1024), F32)],
        input_output_aliases={12: 0},
        compiler_params=_cp(2),
    )(pz.reshape(nb, S, 1024), pxbc.reshape(nb, S, CONV_CH), pxbc.reshape(nb, S, CONV_CH), dtraw.reshape(nb, S, 128),
      convw, convb, dtb, alog, dskip, ng, ex, ltri, cat.reshape(nb, S, 2048))
    return cat.reshape(T, 2048), sall, conv.reshape(T, CONV_CH)


def _outproj(cat, wo, x, g, tm):
    T = x.shape[0]

    def body(cat_ref, wo_ref, x_ref, g_ref, h1_ref, hn_ref):
        h1 = x_ref[...] + _dot(cat_ref[...], wo_ref[...])
        h1_ref[...] = h1
        hn_ref[...] = (_rms(h1)[0] * g_ref[...]).astype(MXU)

    return pl.pallas_call(
        body, grid=(T // tm,), name="outproj",
        in_specs=[_rows(tm, 2048), _const((2048, D)), _rows(tm, D), _const((1, D))],
        out_specs=[_rows(tm, D), _rows(tm, D)],
        out_shape=[_sds((T, D), F32), _sds((T, D), MXU)],
        compiler_params=_cp(),
    )(cat, wo, x, g)


def _ff1(hn, w1, tm):
    T = hn.shape[0]

    def body(hn_ref, w1_ref, hid_ref):
        hn_v = hn_ref[...]
        for n in range(4):
            hid_ref[:, n * 1024:(n + 1) * 1024] = jnp.maximum(_dot(hn_v, w1_ref[n]), 0.0).astype(MXU)

    return pl.pallas_call(
        body, grid=(T // tm,), name="ff1",
        in_specs=[_rows(tm, D), _const((4, D, 1024))],
        out_specs=_rows(tm, DFF),
        out_shape=_sds((T, DFF), MXU),
        compiler_params=_cp(),
    )(hn, w1)


def _sq(hid):
    h = hid.astype(F32)
    return (h * h).astype(MXU)


def _ff2_tail(hid, w2, h1, g_ple, p, tgt, wg, wp, gf, tm):
    T = h1.shape[0]

    def body(hid_ref, w2_ref, h1_ref, g_ref, p_ref, t_ref, wg_ref, wp_ref, gf_ref,
             hp_ref, dgl_ref, dpe_ref, dh2_ref, dh2b_ref, loss_ref, dgf_ref, dg_ref):
        @pl.when(pl.program_id(0) == 0)
        def _():
            loss_ref[...] = jnp.zeros_like(loss_ref)
            dgf_ref[...] = jnp.zeros_like(dgf_ref)
            dg_ref[...] = jnp.zeros_like(dg_ref)

        h2 = h1_ref[...] + _dot(_sq(hid_ref[...]), w2_ref[...])
        h2h, r2 = _rms(h2)
        g_ple = g_ref[...]
        hp = (h2h * g_ple).astype(MXU)
        hp_ref[...] = hp
        gate = _sigmoid(_dot(hp, wg_ref[...]))
        pb = p_ref[...].astype(MXU)
        pe = jnp.concatenate([_dot(pb, wp_ref[k]) for k in range(4)], axis=1)
        h3 = h2 + gate * pe
        hh, r = _rms(h3)
        gf = gf_ref[...]
        diff = hh * gf - t_ref[...]
        loss_ref[...] += 0.5 * jnp.sum(jnp.mean(diff * diff, axis=-1, keepdims=True))
        dout = diff * (1.0 / D)
        dgf_ref[...] += jnp.sum(dout * hh, axis=0, keepdims=True)
        dh3 = _rms_bwd(dout, hh, r, gf)
        dgl = (dh3 * pe * gate * (1.0 - gate)).astype(MXU)
        dgl_ref[...] = dgl
        dpe_ref[...] = (dh3 * gate).astype(MXU)
        dhp = _dot_nt(dgl, wg_ref[...])
        dg_ref[...] += jnp.sum(dhp * h2h, axis=0, keepdims=True)
        dh2 = dh3 + _rms_bwd(dhp, h2h, r2, g_ple)
        dh2_ref[...] = dh2
        dh2b_ref[...] = dh2.astype(MXU)

    return pl.pallas_call(
        body, grid=(T // tm,), name="ff2_tail",
        in_specs=[_rows(tm, DFF), _const((DFF, D)), _rows(tm, D), _const((1, D)), _rows(tm, DPLE), _rows(tm, D),
                  _const((D, D)), _const((4, DPLE, 256)), _const((1, D))],
        out_specs=[_rows(tm, D), _rows(tm, D), _rows(tm, D), _rows(tm, D), _rows(tm, D), _const((8, 128)),
                   _const((1, D)), _const((1, D))],
        out_shape=[_sds((T, D), MXU), _sds((T, D), MXU), _sds((T, D), MXU), _sds((T, D), F32), _sds((T, D), MXU),
                   _sds((8, 128), F32), _sds((1, D), F32), _sds((1, D), F32)],
        compiler_params=_cp(),
    )(hid, w2, h1, g_ple, p, tgt, wg, wp, gf)


def _ff2_bwd(dh2b, w2, hid, tm):
    T = hid.shape[0]

    def body(dh2b_ref, w2_ref, hid_ref, dpre_ref):
        d = dh2b_ref[...]
        for n in range(DFF // 1024):
            sl = slice(n * 1024, (n + 1) * 1024)
            da = _dot_nt(d, w2_ref[sl, :])
            dpre_ref[:, sl] = (2.0 * da * hid_ref[:, sl].astype(F32)).astype(MXU)

    return pl.pallas_call(
        body, grid=(T // tm,), name="ff2_bwd",
        in_specs=[_rows(tm, D), _const((DFF, D)), _rows(tm, DFF)],
        out_specs=_rows(tm, DFF),
        out_shape=_sds((T, DFF), MXU),
        compiler_params=_cp(),
    )(dh2b, w2, hid)


def _ff1_bwd(dpre, w1, dh2, h1, g, tm, after):
    T = h1.shape[0]

    def body(dpre_ref, w1_ref, dh2_ref, h1_ref, g_ref, dh1_ref, dh1b_ref, dg_ref):
        @pl.when(pl.program_id(0) == 0)
        def _():
            dg_ref[...] = jnp.zeros_like(dg_ref)

        dhn = _dot_nt(dpre_ref[:, 0:1024], w1_ref[0])
        for k in range(1, 4):
            dhn = dhn + _dot_nt(dpre_ref[:, k * 1024:(k + 1) * 1024], w1_ref[k])
        hh, r = _rms(h1_ref[...])
        dg_ref[...] += jnp.sum(dhn * hh, axis=0, keepdims=True)
        dh1 = dh2_ref[...] + _rms_bwd(dhn, hh, r, g_ref[...])
        dh1_ref[...] = dh1
        dh1b_ref[...] = dh1.astype(MXU)

    return pl.pallas_call(
        _after(5, body), grid=(T // tm,), name="ff1_bwd",
        in_specs=[_rows(tm, DFF), _const((4, D, 1024)), _rows(tm, D), _rows(tm, D), _const((1, D)), _ANY],
        out_specs=[_rows(tm, D), _rows(tm, D), _const((1, D))],
        out_shape=[_sds((T, D), F32), _sds((T, D), MXU), _sds((1, D), F32)],
        compiler_params=_cp(),
    )(dpre, w1, dh2, h1, g, after)


def _outproj_bwd(dh1b, wo, tm):
    T = dh1b.shape[0]

    def body(d_ref, wo_ref, dcat_ref):
        d = d_ref[...]
        dcat_ref[:, 0:1024] = _dot_nt(d, wo_ref[0:1024, :])
        dcat_ref[:, 1024:2048] = _dot_nt(d, wo_ref[1024:2048, :])

    return pl.pallas_call(
        body, grid=(T // tm,), name="outproj_bwd",
        in_specs=[_rows(tm, D), _const((2048, D))],
        out_specs=_rows(tm, 2048),
        out_shape=_sds((T, 2048), F32),
        compiler_params=_cp(),
    )(dh1b, wo)


def _gmlp_bwd(uv, dcat, gv, ws, bst, gout, wm):
    T = uv.shape[0]
    nck = 2 if T % (2 * CH) == 0 else 1
    tb = nck * CH

    def body(uv_ref, dya_ref, gv_ref, ws_ref, bst_ref, gout_ref, wuv_ref, duv_ref, dgv_ref, dws_ref, dbst_ref,
             dgo_ref, dxn_ref):
        @pl.when(pl.program_id(0) == 0)
        def _():
            dgv_ref[...] = jnp.zeros_like(dgv_ref)
            dws_ref[...] = jnp.zeros_like(dws_ref)
            dbst_ref[...] = jnp.zeros_like(dbst_ref)
            dgo_ref[...] = jnp.zeros_like(dgo_ref)

        for k in range(nck):
            chunk(slice(k * CH, (k + 1) * CH), uv_ref, dya_ref, gv_ref, ws_ref, bst_ref, gout_ref, duv_ref,
                  dgv_ref, dws_ref, dbst_ref, dgo_ref)
        dxn_ref[...] = _dot_nt(duv_ref[...], wuv_ref[...])

    def chunk(rows, uv_ref, dya_ref, gv_ref, ws_ref, bst_ref, gout_ref, duv_ref, dgv_ref, dws_ref, dbst_ref,
              dgo_ref):
        gv = gv_ref[...]
        f = _gmlp_fwd_vals(uv_ref[rows, 0:1024], uv_ref[rows, 1024:2048], gv, ws_ref, bst_ref[...], gout_ref[...])
        dya = dya_ref[rows, :]
        dgo_ref[...] += jnp.sum(dya * f["yhat"], axis=0, keepdims=True)
        dy = _rms_bwd(dya, f["yhat"], f["ry"], gout_ref[...])
        lane = lax.broadcasted_iota(jnp.int32, (CH, 128), 1)
        dbs = jnp.zeros((CH, 128), F32)
        dug, dvg, dgvs = [], [], []
        for h in range(GM_HEADS):
            sl = slice(h * 128, (h + 1) * 128)
            vhat, rv, vn, wt, mixed = f["heads"][h]
            dyh = dy[:, sl]
            dug.append(dyh * mixed)
            dmixed = dyh * f["ug"][:, sl]
            dmb = dmixed.astype(MXU)
            dws_ref[h] += jnp.where(f["tril"], _dot_nt(dmb, vn), 0.0)
            dbs = dbs + jnp.where(lane == h, jnp.sum(dmixed, axis=1, keepdims=True), 0.0)
            dvn = _dot_tn(wt.astype(MXU), dmb)
            dgvs.append(jnp.sum(dvn * vhat, axis=0, keepdims=True))
            dvg.append(_rms_bwd(dvn, vhat, rv, gv[:, sl]))
        dbst_ref[...] += dbs
        dgv_ref[...] += jnp.concatenate(dgvs, axis=1)
        duv_ref[rows, 0:1024] = (jnp.concatenate(dug, axis=1) * f["dug"]).astype(MXU)
        duv_ref[rows, 1024:2048] = (jnp.concatenate(dvg, axis=1) * f["dvg"]).astype(MXU)

    return pl.pallas_call(
        body, grid=(T // tb,), name="gmlp_bwd",
        in_specs=[_rows(tb, 2048), _rows(tb, 1024, 0), _const((1, 1024)),
                  _const((GM_HEADS, CH, CH)), _const((CH, 128)), _const((1, 1024)), _const((D, 2048))],
        out_specs=[_rows(tb, 2048), _const((1, 1024)), _const((GM_HEADS, CH, CH)), _const((CH, 128)),
                   _const((1, 1024)), _rows(tb, D)],
        out_shape=[_sds((T, 2048), MXU), _sds((1, 1024), F32), _sds((GM_HEADS, CH, CH), F32), _sds((CH, 128), F32),
                   _sds((1, 1024), F32), _sds((T, D), F32)],
        compiler_params=_cp(),
    )(uv, dcat, gv, ws, bst, gout, wm)


def _ssd_bwd(pz, pxbc, conv, dtraw, sall, dcat, convw, dtb, alog, dskip, ng, ex, ltri, ext, nb, after):
    T = pz.shape[0]
    S = T // nb
    nch = S // CH
    ns = _SEQS_PER_STEP if nb % _SEQS_PER_STEP == 0 else 1

    def seq(width, col=0):
        return pl.BlockSpec((ns, CH, width), lambda b, c: (b, nch - 1 - c, col))

    in_specs = [
        seq(1024), seq(CONV_CH), seq(CONV_CH), seq(128),
        _const((8, CONV_CH)), _const((1, 128)), _const((1, 128)), _const((1, 1024)),
        _const((1, 1024)), _const((128, 1024)), _const((CH, CH)),
        _const((1024, 128)),
        pl.BlockSpec((ns, 1, 128, 1024), lambda b, c: (b, nch - 1 - c, 0, 0)),
        seq(1024, 1),
        _ANY,
    ]

    def body(z_ref, xbc_ref, conv_ref, dt_ref, cw_ref, dtb_ref, al_ref, ds_ref, ng_ref, ex_ref, lt_ref,
             ext_ref, sall_ref, dyb_ref,
             dssd_ref, ddt_ref, dcw_ref, dcb_ref, ddtb_ref, dal_ref, dds_ref, dng_ref,
             dst_ref, dnext_ref, ddse_ref):
        b = pl.program_id(0)
        c = pl.program_id(1)

        @pl.when((b == 0) & (c == 0))
        def _():
            for r in (dcw_ref, dcb_ref, ddtb_ref, dal_ref, dds_ref, dng_ref, ddse_ref):
                r[...] = jnp.zeros_like(r)

        @pl.when(c == 0)
        def _():
            dst_ref[...] = jnp.zeros_like(dst_ref)
            dnext_ref[...] = jnp.zeros_like(dnext_ref)

        ex = ex_ref[...]
        ext = ext_ref[...]
        cw = cw_ref[...]
        ng = ng_ref[...]
        for i in range(ns):
            one_chunk(i, ex, ext, cw, ng, z_ref, xbc_ref, conv_ref, dt_ref, dtb_ref, al_ref, ds_ref, lt_ref, sall_ref,
                      dyb_ref, dssd_ref, ddt_ref, dcw_ref, dcb_ref, ddtb_ref, dal_ref, dng_ref, dst_ref, dnext_ref,
                      ddse_ref)

        @pl.when((b == nb // ns - 1) & (c == nch - 1))
        def _():
            dds_ref[...] = _dot_01(jnp.broadcast_to(ddse_ref[...], (8, 1024)), ext)[0:1]

    def one_chunk(i, ex, ext, cw, ng, z_ref, xbc_ref, conv_ref, dt_ref, dtb_ref, al_ref, ds_ref, lt_ref, sall_ref,
                  dyb_ref, dssd_ref, ddt_ref, dcw_ref, dcb_ref, ddtb_ref, dal_ref, dng_ref, dst_ref, dnext_ref,
                  ddse_ref):
        z = z_ref[i]
        s_prev = sall_ref[i, 0]
        conv = conv_ref[i]
        f = _ssd_fwd_vals(z, conv, dt_ref[i], dtb_ref[...], al_ref[...], ds_ref[...], ng, ex, lt_ref[...], s_prev)
        xs, xdt, cs, dec, dt = f["xs"], f["xdt"], f["cs"], f["dec"], f["dt"]
        dyb = dyb_ref[i]
        dyg, dngs = [], []
        for g in range(2):
            sl = slice(g * 512, (g + 1) * 512)
            dngs.append(jnp.sum(dyb[:, sl] * f["yhat"][g], axis=0, keepdims=True))
            dyg.append(_rms_bwd(dyb[:, sl], f["yhat"][g], f["rr"][g], ng[:, sl]))
        dng_ref[...] += jnp.concatenate(dngs, axis=1)
        dyg = jnp.concatenate(dyg, axis=1)
        sig_z = f["sig_z"]
        silu_z = z * sig_z
        dy = dyg * silu_z
        dz = dyg * f["ypre"] * (sig_z + silu_z * (1.0 - sig_z))
        ddse_ref[...] += jnp.sum(dy * xs, axis=0, keepdims=True)
        dxs = dy * f["de"]
        dye = dy * f["ecse"]
        dyeb = dye.astype(MXU)
        dst = dst_ref[i]
        dstb = dst.astype(MXU)
        bmb, cmb, sb, xdec = f["bmb"], f["cmb"], f["sb"], f["xdec"]
        u = jnp.concatenate([_dot(bmb[g], dstb[:, g * 512:(g + 1) * 512]) for g in range(2)], axis=1)
        dxdt = [u[:, q * 128:(q + 1) * 128] * f["dece"][:, q * 128:(q + 1) * 128] for q in range(8)]
        per_head = _dot_01(jnp.concatenate(
            [dy * f["yo"], u * xdt, jnp.broadcast_to(jnp.sum(dst * s_prev, axis=0, keepdims=True), (8, 1024))],
            axis=0), ext)
        dcs = per_head[0:CH]
        t = per_head[CH:2 * CH] * dec
        dcd = per_head[2 * CH:2 * CH + 1]
        row = lax.broadcasted_iota(jnp.int32, (CH, 128), 0)
        lane = lax.broadcasted_iota(jnp.int32, (CH, 128), 1)
        cd = jnp.exp(f["last"])
        dcs = dcs - t + jnp.where(row == CH - 1, jnp.sum(t, axis=0, keepdims=True) + dcd * cd, 0.0)
        dcst = jnp.zeros((128, CH), F32)
        lo = f["lo"]
        dbm, dcm, ds_prev = [], [], []
        for g in range(2):
            sl = slice(g * 512, (g + 1) * 512)
            dmg = jnp.zeros((CH, CH), F32)
            for q in range(4 * g, 4 * g + 4):
                dyq = dy[:, q * 128:(q + 1) * 128]
                xq = xdt[:, q * 128:(q + 1) * 128].astype(MXU)
                for hh in range(2):
                    h = 2 * q + hh
                    m = lo if hh == 0 else ~lo
                    dym = jnp.where(m, dyq, 0.0).astype(MXU)
                    gh = _dot_nt(dym, xq)
                    gl = gh * f["lms"][h]
                    dmg = dmg + gl
                    qh = gl * f["mg"][g]
                    dcs = dcs + jnp.where(lane == h, jnp.sum(qh, axis=1, keepdims=True), 0.0)
                    dcst = dcst - jnp.where(row == h, jnp.sum(qh, axis=0, keepdims=True), 0.0)
                    dxdt[q] = dxdt[q] + _dot_tn(f["whs"][h], dym)
            dmgb = dmg.astype(MXU)
            dcm.append(_dot(dmgb, bmb[g]) + _dot_nt(dyeb[:, sl], sb[:, sl]))
            dbm.append(_dot_tn(dmgb, cmb[g]) + _dot_nt(xdec[:, sl], dstb[:, sl]))
            ds_prev.append(_dot_tn(cmb[g], dyeb[:, sl]))
        dst_ref[i] = jnp.concatenate(ds_prev, axis=1) + dst * f["cde"]
        dcs = dcs + dcst.T
        da = _dot_hi(lt_ref[...].T, dcs)
        dxdt = jnp.concatenate(dxdt, axis=1)
        a_neg = f["a_neg"]
        ddt = da * a_neg + _dot_01(dxdt * xs, ext)
        dal_ref[...] += jnp.sum(da * dt, axis=0, keepdims=True) * a_neg
        dxs = dxs + dxdt * f["dte"]
        ddtraw = jnp.where(lane < SSD_HEADS, ddt * _sigmoid(f["dtpre"]), 0.0)
        ddtb_ref[...] += jnp.sum(ddtraw, axis=0, keepdims=True)
        ddt_ref[i] = ddtraw.astype(MXU)
        dxa = jnp.concatenate([dxs, dbm[0], dbm[1], dcm[0], dcm[1]], axis=1)
        sig_c = f["sig_c"]
        dconv = dxa * (sig_c + f["xa"] * (1.0 - sig_c))
        dcb_ref[...] += jnp.sum(dconv, axis=0, keepdims=True)
        xbc = xbc_ref[i]
        dcw_ref[3:4, :] += jnp.sum(dconv * xbc, axis=0, keepdims=True)
        dxbc = cw[3:4] * dconv
        for j, up in zip((1, 2, 3), _shifts_up(dconv, dnext_ref[i])):
            dcw_ref[3 - j:4 - j, :] += jnp.sum(up * xbc, axis=0, keepdims=True)
            dxbc = dxbc + cw[3 - j:4 - j] * up
        dnext_ref[i] = dconv[0:8]
        dssd_ref[i, :, 0:1024] = dz.astype(MXU)
        dssd_ref[i, :, 1024:2560] = dxbc.astype(MXU)

    dssd, ddt, *small = pl.pallas_call(
        _after(14, body), grid=(nb // ns, nch), name="ssd_bwd",
        in_specs=in_specs,
        out_specs=[seq(2560), seq(128),
                   _const((8, CONV_CH)), _const((1, CONV_CH)), _const((1, 128)), _const((1, 128)), _const((1, 128)),
                   _const((1, 1024))],
        out_shape=[_sds((nb, S, 2560), MXU), _sds((nb, S, 128), MXU), _sds((8, CONV_CH), F32),
                   _sds((1, CONV_CH), F32), _sds((1, 128), F32), _sds((1, 128), F32), _sds((1, 128), F32),
                   _sds((1, 1024), F32)],
        scratch_shapes=[pltpu.VMEM((ns, 128, 1024), F32), pltpu.VMEM((ns, 8, CONV_CH), F32),
                        pltpu.VMEM((1, 1024), F32)],
        compiler_params=_cp(2),
    )(pz.reshape(nb, S, 1024), pxbc.reshape(nb, S, CONV_CH), conv.reshape(nb, S, CONV_CH), dtraw.reshape(nb, S, 128),
      convw, dtb, alog, dskip, ng, ex, ltri, ext, sall, dcat.reshape(nb, S, 2048), after)
    return (dssd.reshape(T, 2560), ddt.reshape(T, 128), *small)


def _inproj_bwd(dxn_uv, dssd, ddt, wm, wdt, dh1, x, g, tm, after):
    T = x.shape[0]

    def body(dxnuv_ref, dssd_ref, ddt_ref, wm_ref, wdt_ref, dh1_ref, x_ref, g_ref, dx_ref, dg_ref):
        @pl.when(pl.program_id(0) == 0)
        def _():
            dg_ref[...] = jnp.zeros_like(dg_ref)

        dxn = (dxnuv_ref[...] + _dot_nt(dssd_ref[...], wm_ref[:, 2048:N_MAIN])
               + _dot_nt(ddt_ref[...], wdt_ref[...]))
        xh, r = _rms(x_ref[...])
        dg_ref[...] += jnp.sum(dxn * xh, axis=0, keepdims=True)
        dx_ref[...] = dh1_ref[...] + _rms_bwd(dxn, xh, r, g_ref[...])

    return pl.pallas_call(
        _after(8, body), grid=(T // tm,), name="inproj_bwd",
        in_specs=[_rows(tm, D), _rows(tm, 2560), _rows(tm, 128), _const((D, N_MAIN)), _const((D, 128)),
                  _rows(tm, D), _rows(tm, D), _const((1, D)), _ANY],
        out_specs=[_rows(tm, D), _const((1, D))],
        out_shape=[_sds((T, D), F32), _sds((1, D), F32)],
        compiler_params=_cp(),
    )(dxn_uv, dssd, ddt, wm, wdt, dh1, x, g, after)


def _matmul_tn(a, b, name, a_fn=None):
    T, M = a.shape
    N = b.shape[1]
    tm = min(M, 1024)
    tn = 1280 if N == 2560 else min(N, 2048)
    tk = min(T, 2048)

    def body(a_ref, b_ref, o_ref, acc_ref):
        k = pl.program_id(2)

        @pl.when(k == 0)
        def _():
            acc_ref[...] = jnp.zeros_like(acc_ref)

        av = a_ref[...]
        if a_fn is not None:
            av = a_fn(av)
        acc_ref[...] += _dot_tn(av, b_ref[...])

        @pl.when(k == T // tk - 1)
        def _():
            o_ref[...] = acc_ref[...].astype(o_ref.dtype)

    return pl.pallas_call(
        body, grid=(M // tm, N // tn, T // tk), name=name,
        in_specs=[pl.BlockSpec((tk, tm), lambda i, j, k: (k, i)), pl.BlockSpec((tk, tn), lambda i, j, k: (k, j))],
        out_specs=pl.BlockSpec((tm, tn), lambda i, j, k: (i, j)),
        out_shape=_sds((M, N), GRAD),
        scratch_shapes=[pltpu.VMEM((tm, tn), F32)],
        compiler_params=_cp(3),
    )(a, b)


def _adamw_vals(w, g, m, v):
    m = B1 * m + (1.0 - B1) * g
    v = B2 * v + (1.0 - B2) * (g * g)
    m_hat = m / (1.0 - B1 ** STEP)
    v_hat = v / (1.0 - B2 ** STEP)
    return -LR * (m_hat / (jnp.sqrt(v_hat) + ADAM_EPS) + WD * w), m, v


def _adamw(w, g, m, v, name):
    R, C = w.shape
    tr = 256 if R % 256 == 0 else R

    def body(w_ref, g_ref, m_ref, v_ref, d_ref, mo_ref, vo_ref):
        d_ref[...], mo_ref[...], vo_ref[...] = _adamw_vals(w_ref[...], g_ref[...], m_ref[...], v_ref[...])

    spec = _rows(tr, C)
    return pl.pallas_call(
        body, grid=(R // tr,), name=name,
        in_specs=[spec] * 4, out_specs=[spec] * 3, out_shape=[_sds((R, C), F32)] * 3,
        compiler_params=_cp(),
    )(w, g, m, v)


_PARTS = 4


def _adamw_halves(items, name):
    n = len(items)

    def body(*refs):
        mine = (pl.program_id(0) // _PARTS) == lax.axis_index("c")
        for k in range(n):
            w_ref, own_ref, oth_ref, m_ref, v_ref = refs[5 * k:5 * k + 5]
            g_ref, d_ref, mo_ref, vo_ref = refs[5 * n + 4 * k:5 * n + 4 * k + 4]
            g = jnp.where(mine, own_ref[...], oth_ref[...])
            g_ref[...] = g
            d_ref[...], mo_ref[...], vo_ref[...] = _adamw_vals(w_ref[...], g, m_ref[...], v_ref[...])

    in_specs, out_specs, out_shape = [], [], []
    for w, *_ in items:
        R, C = w.shape
        full = _rows(R // (2 * _PARTS), C)
        part = pl.BlockSpec((R // (2 * _PARTS), C), lambda i: (i % _PARTS, 0))
        in_specs += [full, part, part, full, full]
        out_specs += [full] * 4
        out_shape += [_sds((R, C), F32)] * 4
    res = pl.pallas_call(
        body, grid=(2 * _PARTS,), name=name, in_specs=in_specs, out_specs=out_specs, out_shape=out_shape,
        compiler_params=_cp(),
    )(*[a for item in items for a in item])
    return [tuple(res[4 * k:4 * k + 4]) for k in range(n)]


_TJ = 128


def _adamw_transposed(w, own, other, m, v, name):
    C, _, R = w.shape

    def body(w_ref, own_ref, oth_ref, m_ref, v_ref, g_ref, d_ref, mo_ref, vo_ref):
        first = lax.axis_index("c") == 0
        g = jnp.concatenate([jnp.where(first, own_ref[...], oth_ref[...]),
                             jnp.where(first, oth_ref[...], own_ref[...])], axis=0).T
        d, mo, vo = _adamw_vals(w_ref[:, 0, :], g, m_ref[:, 0, :], v_ref[:, 0, :])
        for ref, val in ((g_ref, g), (d_ref, d), (mo_ref, mo), (vo_ref, vo)):
            ref[:, 0, :] = val

    cols = pl.BlockSpec((_TJ, 1, R), lambda j: (j, 0, 0))
    half = pl.BlockSpec((R // 2, _TJ), lambda j: (0, j))
    return pl.pallas_call(
        body, grid=(pl.cdiv(C, _TJ),), name=name,
        in_specs=[cols, half, half, cols, cols], out_specs=[cols] * 4, out_shape=[_sds((C, 1, R), F32)] * 4,
        compiler_params=_cp(),
    )(w, own, other, m, v)


def _sum_small(slots, name):
    nd, rows, C = slots.shape

    def body(s_ref, o_ref):
        acc = s_ref[0]
        for d in range(1, nd):
            acc = acc + s_ref[d]
        o_ref[...] = acc

    return pl.pallas_call(
        body, grid=(1,), name=name,
        in_specs=[_const((nd, rows, C))], out_specs=_const((rows, C)), out_shape=_sds((rows, C), F32),
        compiler_params=_cp(),
    )(slots)


def _sum_slots(items, kh, name):
    n = len(items)
    in_specs, out_specs, out_shape = [], [], []
    for slots, src, kind, (R, C) in items:
        tr = R // (2 * _PARTS)
        if kind == "slab":
            src_spec = pl.BlockSpec((1, tr, C), lambda i, kh: (kh[0], kh[1] * _PARTS + i, 0))
        elif kind == "rows":
            src_spec = pl.BlockSpec((tr, C), lambda i, kh: (kh[0] * (2 * _PARTS) + kh[1] * _PARTS + i, 0))
        else:
            src_spec = pl.BlockSpec((tr, C), lambda i, kh: (kh[1] * _PARTS + i, kh[0]))
        in_specs += [pl.BlockSpec((8, tr, C), lambda i, kh: (0, i, 0)), src_spec]
        out_specs.append(pl.BlockSpec((tr, C), lambda i, kh: (i, 0)))
        out_shape.append(_sds((R // 2, C), F32))

    def body(kh_ref, *refs):
        me = 2 * kh_ref[0] + kh_ref[1]
        for k, (_, _, kind, _) in enumerate(items):
            s_ref, own_ref, o_ref = refs[2 * k], refs[2 * k + 1], refs[2 * n + k]
            acc = (own_ref[0] if kind == "slab" else own_ref[...]).astype(F32)
            for j in range(1, 8):
                acc = acc + s_ref[me ^ j].astype(F32)
            o_ref[...] = acc

    return pl.pallas_call(
        body, name=name,
        grid_spec=pltpu.PrefetchScalarGridSpec(
            num_scalar_prefetch=1, grid=(_PARTS,), in_specs=in_specs, out_specs=out_specs),
        out_shape=out_shape,
        compiler_params=_cp(),
    )(kh, *[a for slots, src, _, _ in items for a in (slots, src)])


def _assemble_w_in(slabs):
    tr = 256

    def body(s_ref, wm_ref, wdt_ref):
        full = jnp.concatenate([s_ref[k] for k in range(4)], axis=1)
        wm_ref[...] = full[:, :N_MAIN]
        wdt_ref[...] = jnp.concatenate([full[:, N_MAIN:], jnp.zeros((tr, 128 - 16), full.dtype)], axis=1)

    return pl.pallas_call(
        body, grid=(D // tr,), name="assemble_w_in",
        in_specs=[pl.BlockSpec((4, tr, 1156), lambda i: (0, i, 0))],
        out_specs=[_rows(tr, N_MAIN), _rows(tr, 128)],
        out_shape=[_sds((D, N_MAIN), slabs.dtype), _sds((D, 128), slabs.dtype)],
        compiler_params=_cp(),
    )(slabs)


def _split_dw_in(d_uv, d_ssd, d_dt):
    tr = 256

    def body(uv_ref, ssd_ref, dt_ref, o_ref):
        full = jnp.concatenate([uv_ref[...], ssd_ref[...], dt_ref[:, 0:16]], axis=1)
        for k in range(4):
            o_ref[k] = full[:, 1156 * k:1156 * (k + 1)]

    return pl.pallas_call(
        body, grid=(D // tr,), name="split_dw_in",
        in_specs=[_rows(tr, 2048), _rows(tr, 2560), _rows(tr, 128)],
        out_specs=pl.BlockSpec((4, tr, 1156), lambda i: (0, i, 0)),
        out_shape=_sds((4, D, 1156), d_uv.dtype),
        compiler_params=_cp(),
    )(d_uv, d_ssd, d_dt)


def _cast_w_in(w, kh):
    C, _, R = w.shape

    def body(kh_ref, w_ref, o_ref):
        o_ref[0] = w_ref[:, 0, :].T.astype(BF16)

    return pl.pallas_call(
        body, name="cast_w_in",
        grid_spec=pltpu.PrefetchScalarGridSpec(
            num_scalar_prefetch=1, grid=(pl.cdiv(C, _TJ),),
            in_specs=[pl.BlockSpec((_TJ, 1, R), lambda j, kh: (j, 0, 0))],
            out_specs=pl.BlockSpec((1, R, _TJ), lambda j, kh: (kh[0], 0, j))),
        out_shape=_sds((4, R, C), BF16),
        compiler_params=_cp(),
    )(kh, w)


def _cast_into_slot(ws, kh, name):
    n = len(ws)

    def body(kh_ref, *refs):
        for k in range(n):
            refs[n + k][0] = refs[k][...].astype(BF16)

    return pl.pallas_call(
        body, name=name,
        grid_spec=pltpu.PrefetchScalarGridSpec(
            num_scalar_prefetch=1, grid=(_PARTS,),
            in_specs=[pl.BlockSpec((w.shape[0] // _PARTS, w.shape[1]), lambda i, kh: (i, 0)) for w in ws],
            out_specs=[pl.BlockSpec((1, w.shape[0] // _PARTS, w.shape[1]), lambda i, kh: (kh[0], i, 0))
                       for w in ws]),
        out_shape=[_sds((4,) + w.shape, BF16) for w in ws],
        compiler_params=_cp(),
    )(kh, *ws)


_ANY = pl.BlockSpec(memory_space=pl.ANY)
_CHIP_FLIPS = [(1, 0), (0, 1), (1, 1)]
_DEVICE_FLIPS = [(fx, fy, fc) for fx in (0, 1) for fy in (0, 1) for fc in (0, 1)][1:]


def _half(h, rows):
    return pl.ds(pl.multiple_of(h * rows, rows), rows)


def _remote(src, dst, ssem, rsem, to):
    return pltpu.make_async_remote_copy(src_ref=src, dst_ref=dst, send_sem=ssem, recv_sem=rsem,
                                        device_id=to, device_id_type=MESH)


def _weight_gather(bufs, conv):
    n = len(bufs)

    def body(*refs):
        conv_ref, outs, conv_out = refs[n], refs[n + 1:2 * n + 1], refs[2 * n + 1]
        send_sems, recv_sems, fsend_sems, frecv_sems, csend_sems, crecv_sems, local_sem = refs[2 * n + 2:]
        x, y, c = lax.axis_index("x"), lax.axis_index("y"), lax.axis_index("c")
        me = 2 * x + y
        halves = [_half(c, r.shape[1] // 2) for r in outs]
        others = [_half(1 - c, r.shape[1] // 2) for r in outs]
        remote = _remote
        local = [pltpu.make_async_copy(conv_ref, conv_out.at[me], local_sem)]
        for cp in local:
            cp.start()
        sends = []
        for k, (fx, fy) in enumerate(_CHIP_FLIPS):
            peer = (x ^ fx, y ^ fy, c)
            for i in range(n):
                mine = outs[i].at[me, halves[i]]
                sends.append(remote(mine, mine, send_sems.at[k * n + i], recv_sems.at[k * n + i], peer))
            sends.append(remote(conv_ref, conv_out.at[me], csend_sems.at[k], crecv_sems.at[k], peer))
        for cp in sends:
            cp.start()
        sibling = (x, y, 1 - c)
        forwards = []
        for k, (fx, fy) in enumerate(_CHIP_FLIPS):
            peer = (x ^ fx, y ^ fy, c)
            src = 2 * (x ^ fx) + (y ^ fy)
            for i in range(n):
                landed = outs[i].at[src, halves[i]]
                remote(landed, landed, send_sems.at[k * n + i], recv_sems.at[k * n + i], peer).wait_recv()
                fw = remote(landed, landed, fsend_sems.at[k * n + i], frecv_sems.at[k * n + i], sibling)
                fw.start()
                forwards.append(fw)
            remote(conv_out.at[src], conv_out.at[src], csend_sems.at[k], crecv_sems.at[k], peer).wait_recv()
        for k, (fx, fy) in enumerate(_CHIP_FLIPS):
            src = 2 * (x ^ fx) + (y ^ fy)
            for i in range(n):
                theirs = outs[i].at[src, others[i]]
                remote(theirs, theirs, fsend_sems.at[k * n + i], frecv_sems.at[k * n + i], sibling).wait_recv()
        for cp in sends + forwards:
            cp.wait_send()
        for cp in local:
            cp.wait()

    dma = pltpu.SemaphoreType.DMA
    return pl.pallas_call(
        body, name="weight_gather",
        in_specs=[_ANY] * (n + 1), out_specs=[_ANY] * (n + 1),
        out_shape=[_sds(b.shape, b.dtype) for b in bufs] + [_sds((4,) + conv.shape, conv.dtype)],
        input_output_aliases={i: i for i in range(n)},
        scratch_shapes=[dma((3 * n,)), dma((3 * n,)), dma((3 * n,)), dma((3 * n,)), dma((3,)), dma((3,)), dma],
    )(*bufs, conv)


def _piece(ref, kind, R, C, k, h):
    if kind == "slab":
        return ref.at[k, _half(h, R // 2), :]
    if kind == "rows":
        return ref.at[pl.ds(pl.multiple_of(k * R + h * (R // 2), R // 2), R // 2), :]
    return ref.at[_half(h, R // 2), pl.ds(pl.multiple_of(k * C, C), C)]


def _small_exchange(small, after):
    rs = small.shape[0]

    def body(s_ref, after_ref, out_ref, send_sems, recv_sems, local_sem):
        del after_ref
        x, y, c = lax.axis_index("x"), lax.axis_index("y"), lax.axis_index("c")
        slot = 4 * x + 2 * y + c
        own = pltpu.make_async_copy(s_ref, out_ref.at[slot], local_sem)
        own.start()
        copies = []
        for k, (fx, fy, fc) in enumerate(_DEVICE_FLIPS):
            copies.append(_remote(s_ref, out_ref.at[slot], send_sems.at[k], recv_sems.at[k], (x ^ fx, y ^ fy, c ^ fc)))
        for cp in copies:
            cp.start()
        for k, (fx, fy, fc) in enumerate(_DEVICE_FLIPS):
            theirs = out_ref.at[slot ^ (k + 1)]
            _remote(theirs, theirs, send_sems.at[k], recv_sems.at[k], (x ^ fx, y ^ fy, c ^ fc)).wait_recv()
        for cp in copies:
            cp.wait_send()
        own.wait()

    dma = pltpu.SemaphoreType.DMA
    return pl.pallas_call(
        body, name="small_exchange",
        in_specs=[_ANY, _ANY], out_specs=_ANY, out_shape=_sds((8, rs, 128), F32),
        scratch_shapes=[dma((7,)), dma((7,)), dma],
    )(small, after)


_HBM = pl.BlockSpec(memory_space=pltpu.HBM)
_SEM = pl.BlockSpec(memory_space=pltpu.SEMAPHORE)


def _split_start(name, arrays, n_copies, plan, after=None):
    n = len(arrays)
    extra = [] if after is None else [after]

    def body(*refs):
        m = n + len(extra)
        arrs, send_sems, recv_sems, token = refs[:n], refs[m], refs[m + 1], refs[-1]
        for j, (src, dst, peer) in enumerate(plan(arrs)):
            _remote(src, dst, send_sems.at[j], recv_sems.at[j], peer).start()
        token[...] = jnp.zeros_like(token)

    dma = pltpu.SemaphoreType.DMA
    res = pl.pallas_call(
        body, name=name,
        out_shape=(dma((n_copies,)), dma((n_copies,)), *[pltpu.HBM(a.shape, a.dtype) for a in arrays],
                   _sds((8, 128), F32)),
        in_specs=[_HBM] * n + [_ANY] * len(extra),
        out_specs=(_SEM, _SEM, *[_HBM] * n, pl.BlockSpec(memory_space=pltpu.VMEM)),
        input_output_aliases={i: 2 + i for i in range(n)},
        compiler_params=pltpu.CompilerParams(has_side_effects=pltpu.SideEffectType.DATAFLOW_SIDE_EFFECTING),
    )(*[pltpu.with_memory_space_constraint(a, pltpu.HBM) for a in arrays], *extra)
    return res[0], res[1], list(res[2:2 + n]), res[-1]


def _split_wait(name, arrays, send_sems, recv_sems, plan, after):
    n = len(arrays)

    def body(*refs):
        arrs, ssems, rsems = refs[:n], refs[n], refs[n + 1]
        for j, (src, dst, peer) in enumerate(plan(arrs)):
            cp = _remote(src, dst, ssems.at[j], rsems.at[j], peer)
            cp.wait_send()
            cp.wait_recv()

    return list(pl.pallas_call(
        body, name=name,
        out_shape=tuple(pltpu.HBM(a.shape, a.dtype) for a in arrays),
        in_specs=[_HBM] * n + [_SEM, _SEM, _ANY],
        out_specs=tuple([_HBM] * n),
        input_output_aliases={i: i for i in range(n)},
        compiler_params=pltpu.CompilerParams(has_side_effects=pltpu.SideEffectType.DATAFLOW_SIDE_EFFECTING),
    )(*arrays, send_sems, recv_sems, after))


def _gather_plan(n):
    def plan(bufs):
        x, y, c = lax.axis_index("x"), lax.axis_index("y"), lax.axis_index("c")
        me = 2 * x + y
        return [(bufs[i].at[me], bufs[i].at[me], (x ^ fx, y ^ fy, c)) for fx, fy in _CHIP_FLIPS for i in range(n)]

    return plan


def _reduce_plan(specs, n_small):
    n = len(specs)

    def plan(arrs):
        x, y, c = lax.axis_index("x"), lax.axis_index("y"), lax.axis_index("c")
        slot = 4 * x + 2 * y + c
        out = []
        for fx, fy, fc in _DEVICE_FLIPS:
            peer = (x ^ fx, y ^ fy, c ^ fc)
            for i, (kind, (R, C)) in enumerate(specs):
                out.append((_piece(arrs[i], kind, R, C, 2 * peer[0] + peer[1], peer[2]), arrs[n + i].at[slot], peer))
            for s in range(n_small):
                out.append((arrs[2 * n + 2 * s], arrs[2 * n + 2 * s + 1].at[slot], peer))
        return out

    return plan


def _sibling_exchange(halves, name):
    n = len(halves)

    def body(*refs):
        ins, outs, send_sems, recv_sems = refs[:n], refs[n:2 * n], refs[2 * n], refs[2 * n + 1]
        sibling = (lax.axis_index("x"), lax.axis_index("y"), 1 - lax.axis_index("c"))
        copies = [pltpu.make_async_remote_copy(src_ref=ins[i], dst_ref=outs[i], send_sem=send_sems.at[i],
                                               recv_sem=recv_sems.at[i], device_id=sibling, device_id_type=MESH)
                  for i in range(n)]
        for cp in copies:
            cp.start()
        for cp in copies:
            cp.wait()

    dma = pltpu.SemaphoreType.DMA
    return pl.pallas_call(
        body, name=name,
        in_specs=[_ANY] * n, out_specs=[_ANY] * n,
        out_shape=[_sds(h.shape, h.dtype) for h in halves],
        scratch_shapes=[dma((n,)), dma((n,))],
    )(*halves)


_BIG = [("w_in", (1024, 1156), "slab"), ("w_out", (512, 1024), "rows"), ("w_ff1", (1024, 1024), "cols"),
        ("w_ff2", (1024, 1024), "rows"), ("w_ple_gate", (256, 1024), "rows"), ("w_ple_proj", (256, 256), "cols")]
_SMALL = [("norm_mix_g", (1, 1024)), ("gm_v_norm_g", (1, 1024)), ("gm_ws", (1, 8, 128, 128)), ("gm_bs", (1, 8, 128)),
          ("gm_out_norm_g", (1, 1024)), ("ssd_conv_w", (1, 4, 1536)), ("ssd_conv_b", (1, 1536)),
          ("ssd_dt_bias", (1, 16)), ("ssd_a_log", (1, 16)), ("ssd_d", (1, 16)), ("ssd_norm_g", (1, 1024)),
          ("norm_mlp_g", (1, 1024)), ("ple_norm_g", (1, 1024)), ("final_norm_g", (1024,))]


def _rows128(a):
    flat = a.reshape(-1)
    rows = -(-flat.shape[0] // 1024) * 8
    return jnp.pad(flat, (0, rows * 128 - flat.shape[0])).reshape(rows, 128)


def _pad_lanes(v, n=128):
    v = v.reshape(1, -1)
    return jnp.pad(v, ((0, 0), (0, n - v.shape[1])))


_SMALL_SHAPES = dict(_SMALL + [("loss", ())])
_BIG_SPECS = {n: (kind, shp) for n, shp, kind in _BIG}


class _Comm:
    def __init__(self, a, kh):
        self.a, self.kh = a, kh
        rest = _BIG[1:]
        self.bufs = {"w_in": _cast_w_in(a["w_in"].transpose(2, 0, 1), kh)}
        cast = _cast_into_slot([a[n].reshape(shp) for n, shp, _ in rest], kh, "cast_rest")
        self.bufs.update({n: c for (n, _, _), c in zip(rest, cast)})
        self.sent = []
        self.small_tot = {}

    def w_in(self):
        g_win, g_cw = _weight_gather([self.bufs["w_in"]], self.a["ssd_conv_w"].reshape(4, 384))
        token = g_cw
        self.gather = {}
        for tag, names in (("out", ["w_out"]), ("ff", ["w_ff1", "w_ff2", "w_ple_gate", "w_ple_proj"])):
            plan = _gather_plan(len(names))
            ssem, rsem, thru, token = _split_start("gather_start_" + tag, [self.bufs[n] for n in names],
                                                   3 * len(names), plan, after=token)
            self.gather[tag] = (plan, ssem, rsem, thru)
        wm, wdt = _assemble_w_in(g_win)
        return wm, wdt, jnp.concatenate([g_cw[k] for k in range(4)], axis=1), token

    def rest(self, tag, after):
        plan, ssem, rsem, thru = self.gather[tag]
        got = _split_wait("gather_wait_" + tag, thru, ssem, rsem, plan, after)
        if tag == "out":
            return got[0].reshape(2048, D)
        g_w1, g_w2, g_wg, g_wp = got
        return g_w1, g_w2.reshape(DFF, D), g_wg.reshape(D, D), g_wp

    def send(self, tag, grads):
        big = [n for n, _, _ in _BIG if n in grads]
        small = [n for n in _SMALL_SHAPES if n in grads]
        parts = [_rows128(grads[n]) for n in small]
        rows = [s.shape[0] for s in parts]
        if not big:
            self.last_small = (tag, small, rows, jnp.concatenate(parts, axis=0))
            return None
        srcs = [grads[n] for n in big]
        lands = [lax.empty((8, _BIG_SPECS[n][1][0] // 2, _BIG_SPECS[n][1][1]), GRAD) for n in big]
        extra = []
        if small:
            pack = jnp.concatenate(parts, axis=0)
            extra = [pack, jnp.broadcast_to(pack, (8,) + pack.shape)]
        plan = _reduce_plan([_BIG_SPECS[n] for n in big], len(extra) // 2)
        n_copies = 7 * (len(big) + len(extra) // 2)
        ssem, rsem, thru, token = _split_start("reduce_start_" + tag, srcs + lands + extra, n_copies, plan)
        self.sent.append((tag, big, small, rows, plan, ssem, rsem, thru))
        return token

    def _unpack(self, tot, names, rows):
        o = 0
        for n, r in zip(names, rows):
            shp = _SMALL_SHAPES[n]
            cnt = 1
            for s in shp:
                cnt *= s
            self.small_tot[n] = tot[o:o + r].reshape(-1)[:cnt].reshape(shp)
            o += r

    def finish(self, after):
        a, results = self.a, {}

        def update(names, own, tag):
            other = _sibling_exchange([own[n] for n in names], "sibling_exchange_" + tag)
            if names == ["w_in"]:
                w, m, v = (a[k].transpose(2, 0, 1) for k in ("w_in", "m_w_in", "v_w_in"))
                raw = _adamw_transposed(w, own["w_in"], other[0], m, v, "adamw_" + tag)
                results["w_in"] = tuple(r.transpose(1, 2, 0) for r in raw)
                return raw[1]
            items = [(a[n].reshape(_BIG_SPECS[n][1]), own[n], oth, a["m_" + n].reshape(_BIG_SPECS[n][1]),
                      a["v_" + n].reshape(_BIG_SPECS[n][1])) for n, oth in zip(names, other)]
            results.update(zip(names, _adamw_halves(items, "adamw_" + tag)))
            return results[names[-1]][1]

        own, early = {}, []
        for tag, big, small, rows, plan, ssem, rsem, thru in self.sent:
            if tag == self.sent[-1][0]:
                after = update(early, own, "early")
            arrs = _split_wait("reduce_wait_" + tag, thru, ssem, rsem, plan, after)
            nb_ = len(big)
            sums = _sum_slots([(arrs[nb_ + i], arrs[i]) + _BIG_SPECS[n] for i, n in enumerate(big)], self.kh,
                              "sum_" + tag)
            own.update(zip(big, sums))
            after = sums[-1]
            early += big
            if small:
                self._unpack(_sum_small(arrs[2 * nb_ + 1], "sum_small_" + tag), small, rows)
        after = update(self.sent[-1][1], own, "late")
        tag, small, rows, pack = self.last_small
        self._unpack(_sum_small(_small_exchange(pack, after), "sum_small_" + tag), small, rows)
        return results, dict(self.small_tot)


def _local_step(x, p, tgt, sm, comm, nb, tm):
    T = x.shape[0]
    wm, wdt, conv_w, token = comm.w_in()
    g_mix, gv, gout = sm["norm_mix_g"].reshape(1, D), sm["gm_v_norm_g"].reshape(1, D), sm["gm_out_norm_g"].reshape(1, D)
    ws = sm["gm_ws"].reshape(GM_HEADS, CH, CH)
    bst = jnp.pad(sm["gm_bs"].reshape(GM_HEADS, CH).T, ((0, 0), (0, 128 - GM_HEADS)))
    convw = jnp.pad(conv_w, ((0, 4), (0, 0)))
    convb = sm["ssd_conv_b"].reshape(1, CONV_CH)
    dtb, alog = _pad_lanes(sm["ssd_dt_bias"]), _pad_lanes(sm["ssd_a_log"])
    dskip = jnp.repeat(sm["ssd_d"].reshape(SSD_HEADS), SSD_P).reshape(1, 1024)
    ng, g_mlp, g_ple = sm["ssd_norm_g"].reshape(1, D), sm["norm_mlp_g"].reshape(1, D), sm["ple_norm_g"].reshape(1, D)
    gf = sm["final_norm_g"].reshape(1, D)
    head_of_lane = lax.broadcasted_iota(jnp.int32, (128, 1024), 1) // SSD_P
    ex = (lax.broadcasted_iota(jnp.int32, (128, 1024), 0) == head_of_lane).astype(BF16)
    ext = ex.T
    ltri = (lax.broadcasted_iota(jnp.int32, (CH, CH), 0) >= lax.broadcasted_iota(jnp.int32, (CH, CH), 1)).astype(F32)

    pz, pxbc, dtraw, xn, cat, uv = _inproj_gmlp(x, g_mix, wm, wdt, gv, ws, bst, gout, tm, token)
    cat, sall, conv = _ssd_fwd(pz, pxbc, dtraw, cat, convw, convb, dtb, alog, dskip, ng, ex, ltri, nb)
    wo = comm.rest("out", cat)
    h1, hn = _outproj(cat, wo, x, g_mlp, min(T, 2 * tm))
    w1, w2, wg, wp = comm.rest("ff", hn)
    hid = _ff1(hn, w1, min(T, 2 * tm))
    hp, dgl, dpe, dh2, dh2b, loss, d_gf, d_gple = _ff2_tail(hid, w2, h1, g_ple, p, tgt, wg, wp, gf, tm)

    d_wp = _matmul_tn(p, dpe, "dw_ple_proj", a_fn=lambda a: a.astype(MXU))
    d_wg = _matmul_tn(hp, dgl, "dw_ple_gate")
    d_w2 = _matmul_tn(hid, dh2b, "dw_ff2", a_fn=_sq)
    dpre = _ff2_bwd(dh2b, w2, hid, min(T, 2 * tm))
    d_w1 = _matmul_tn(hn, dpre, "dw_ff1")
    token = comm.send("a", {"w_ple_proj": d_wp, "w_ple_gate": d_wg, "w_ff2": d_w2, "w_ff1": d_w1})
    dh1, dh1b, d_gmlp = _ff1_bwd(dpre, w1, dh2, h1, g_mlp, tm, token)
    dcat = _outproj_bwd(dh1b, wo, min(T, 2 * tm))
    d_wo = _matmul_tn(cat, dh1b, "dw_out")
    duv, d_gv, d_ws, d_bst, d_gout, dxn_uv = _gmlp_bwd(uv, dcat, gv, ws, bst, gout, wm)
    token = comm.send("b", {
        "w_out": d_wo, "loss": loss[0:1, 0:1], "final_norm_g": d_gf, "ple_norm_g": d_gple, "norm_mlp_g": d_gmlp,
        "gm_v_norm_g": d_gv, "gm_ws": d_ws, "gm_bs": d_bst[:, :GM_HEADS].T, "gm_out_norm_g": d_gout})
    dssd, ddt, d_cw, d_cb, d_dtb, d_al, d_ds, d_ng = _ssd_bwd(
        pz, pxbc, conv, dtraw, sall, dcat, convw, dtb, alog, dskip, ng, ex, ltri, ext, nb, token)
    d_win = _split_dw_in(_matmul_tn(xn, duv, "dw_in_uv"), _matmul_tn(xn, dssd, "dw_in_ssd"),
                         _matmul_tn(xn, ddt, "dw_in_dt"))
    token = comm.send("c", {"w_in": d_win})
    dx, d_gmix = _inproj_bwd(dxn_uv, dssd, ddt, wm, wdt, dh1, x, g_mix, tm, token)
    comm.send("d", {"norm_mix_g": d_gmix, "ssd_conv_w": d_cw[0:4], "ssd_conv_b": d_cb, "ssd_dt_bias": d_dtb[:, :16],
                    "ssd_a_log": d_al[:, :16], "ssd_d": d_ds[:, :16], "ssd_norm_g": d_ng})
    return dx


def kernel(x, p, norm_mix_g, w_in, gm_v_norm_g, gm_ws, gm_bs, gm_out_norm_g, ssd_conv_w, ssd_conv_b, ssd_dt_bias, ssd_a_log, ssd_d, ssd_norm_g, w_out, norm_mlp_g, w_ff1, w_ff2, ple_norm_g, w_ple_gate, w_ple_proj, final_norm_g, loss_target, m_norm_mix_g, m_w_in, m_gm_v_norm_g, m_gm_ws, m_gm_bs, m_gm_out_norm_g, m_ssd_conv_w, m_ssd_conv_b, m_ssd_dt_bias, m_ssd_a_log, m_ssd_d, m_ssd_norm_g, m_w_out, m_norm_mlp_g, m_w_ff1, m_w_ff2, m_ple_norm_g, m_w_ple_gate, m_w_ple_proj, m_final_norm_g, v_norm_mix_g, v_w_in, v_gm_v_norm_g, v_gm_ws, v_gm_bs, v_gm_out_norm_g, v_ssd_conv_w, v_ssd_conv_b, v_ssd_dt_bias, v_ssd_a_log, v_ssd_d, v_ssd_norm_g, v_w_out, v_norm_mlp_g, v_w_ff1, v_w_ff2, v_ple_norm_g, v_w_ple_gate, v_w_ple_proj, v_final_norm_g):
    a = dict(locals())
    order = ["norm_mix_g", "w_in", "gm_v_norm_g", "gm_ws", "gm_bs", "gm_out_norm_g", "ssd_conv_w", "ssd_conv_b",
             "ssd_dt_bias", "ssd_a_log", "ssd_d", "ssd_norm_g", "w_out", "norm_mlp_g", "w_ff1", "w_ff2", "ple_norm_g",
             "w_ple_gate", "w_ple_proj", "final_norm_g"]
    chip = 2 * lax.axis_index("x") + lax.axis_index("y")
    nb, S = x.shape[0], x.shape[1]
    T = nb * S
    sm = {n: a[n] for n, _ in _SMALL if n != "ssd_conv_w"}
    comm = _Comm(a, jnp.stack([chip, lax.axis_index("c")]).astype(jnp.int32))
    dx = _local_step(x.reshape(T, D), p.reshape(T, DPLE), loss_target.reshape(T, D), sm, comm, nb, 512)
    big, g_out = comm.finish(dx)
    delta, new_m, new_v = {}, {}, {}
    for n, _, _ in _BIG:
        g_out[n], delta[n], new_m[n], new_v[n] = (r.reshape(a[n].shape) for r in big[n])
    g_out["ssd_conv_w"] = lax.dynamic_slice(g_out["ssd_conv_w"], (0, 0, chip * 384), (1, 4, 384))
    small_names = [n for n, _ in _SMALL]
    packs = [jnp.concatenate([_rows128(src(n)) for n in small_names], axis=0)
             for src in (lambda n: a[n], lambda n: g_out[n], lambda n: a["m_" + n], lambda n: a["v_" + n])]
    outs = _adamw(*packs, "adamw_small")
    o = 0
    for n in small_names:
        r = _rows128(a[n]).shape[0]
        cnt = a[n].size
        for dst, src in zip((delta, new_m, new_v), outs):
            dst[n] = src[o:o + r].reshape(-1)[:cnt].reshape(a[n].shape)
        o += r
    return (g_out["loss"], dx.reshape(x.shape), *[g_out[n] for n in order], *[delta[n] for n in order],
            *[new_m[n] for n in order], *[new_v[n] for n in order])
```

```python
import jax
import jax.numpy as jnp
from jax import lax
from jax.experimental import pallas as pl
from jax.experimental.pallas import tpu as pltpu

F32 = jnp.float32
BF16 = jnp.bfloat16
MXU = jnp.bfloat16
GRAD = jnp.bfloat16

D = 1024
CH = 128
GM_HEADS = 8
SSD_HEADS = 16
SSD_P = 64
CONV_CH = 1536
N_MAIN = 4608
DFF = 4096
DPLE = 256
EPS = 1e-6
NEG = -1e30

LR, B1, B2, ADAM_EPS, WD, STEP = 0.001, 0.9, 0.999, 1e-08, 0.01, 10

VMEM_LIMIT = 56 * 1024 * 1024
_SEQS_PER_STEP = 4
MESH = pl.DeviceIdType.MESH

INV_SQRT2 = 0.7071067811865476
INV_SQRT_2PI = 0.3989422804014327


def _cp(n_axes=1):
    return pltpu.CompilerParams(dimension_semantics=("arbitrary",) * n_axes, vmem_limit_bytes=VMEM_LIMIT)


def _dot(a, b):
    return jnp.dot(a, b, preferred_element_type=F32)


def _dot_nt(a, b):
    return lax.dot_general(a, b, (((1,), (1,)), ((), ())), preferred_element_type=F32)


def _dot_tn(a, b):
    return lax.dot_general(a, b, (((0,), (0,)), ((), ())), preferred_element_type=F32)


def _dot_hi(a, b):
    return jnp.dot(a, b, preferred_element_type=F32, precision=lax.Precision.HIGHEST)


def _dot_01(a, sel):
    hi = a.astype(BF16)
    lo = (a - hi.astype(F32)).astype(BF16)
    n = a.shape[0]
    r = _dot(jnp.concatenate([hi, lo], axis=0), sel)
    return r[0:n] + r[n:2 * n]


def _rows(tm, n, j=0):
    return pl.BlockSpec((tm, n), lambda i: (i, j))


def _const(shape):
    nd = len(shape)
    return pl.BlockSpec(shape, lambda *_: (0,) * nd)


def _sds(shape, dtype):
    return jax.ShapeDtypeStruct(shape, dtype)


def _rms(x):
    r = lax.rsqrt(jnp.mean(x * x, axis=-1, keepdims=True) + EPS)
    return x * r, r


def _rms_bwd(dy, xhat, r, g):
    dyg = dy * g
    return r * (dyg - xhat * jnp.mean(dyg * xhat, axis=-1, keepdims=True))


def _sigmoid(x):
    return 1.0 / (1.0 + jnp.exp(-x))


def _gelu(x):
    cdf = 0.5 * (1.0 + lax.erf(x * INV_SQRT2))
    pdf = jnp.exp(-0.5 * x * x) * INV_SQRT_2PI
    return x * cdf, cdf + x * pdf


def _softplus(x):
    e = jnp.exp(-jnp.abs(x))
    u = 1.0 + e
    log1p = jnp.where(u == 1.0, e, jnp.log(u) * e / (u - 1.0))
    return jnp.maximum(x, 0.0) + log1p


def _after(n_in, fn):
    def body(*refs):
        return fn(*refs[:n_in], *refs[n_in + 1:])

    return body


def _inproj_gmlp(x, g, wm, wdt, gv, ws, bst, gout, tm, after):
    T = x.shape[0]

    def body(x_ref, g_ref, wm_ref, wdt_ref, gv_ref, ws_ref, bst_ref, gout_ref,
             z_ref, xbc_ref, dt_ref, xn_ref, ya_ref, uv_ref):
        xh, _ = _rms(x_ref[...])
        xn = (xh * g_ref[...]).astype(MXU)
        xn_ref[...] = xn
        for n in range(4):
            uv_ref[:, n * 512:(n + 1) * 512] = _dot(xn, wm_ref[:, n * 512:(n + 1) * 512])
        for n in range(2):
            z_ref[:, n * 512:(n + 1) * 512] = _dot(xn, wm_ref[:, 2048 + n * 512:2048 + (n + 1) * 512])
        for n in range(3):
            xbc_ref[:, n * 512:(n + 1) * 512] = _dot(xn, wm_ref[:, 3072 + n * 512:3072 + (n + 1) * 512])
        dt_ref[...] = _dot(xn, wdt_ref[...])
        for k in range(tm // CH):
            rows = slice(k * CH, (k + 1) * CH)
            f = _gmlp_fwd_vals(uv_ref[rows, 0:1024], uv_ref[rows, 1024:2048], gv_ref[...], ws_ref, bst_ref[...],
                               gout_ref[...])
            ya_ref[rows, :] = f["out"].astype(MXU)

    return pl.pallas_call(
        _after(8, body), grid=(T // tm,), name="inproj_gmlp",
        in_specs=[_rows(tm, D), _const((1, D)), _const((D, N_MAIN)), _const((D, 128)), _const((1, 1024)),
                  _const((GM_HEADS, CH, CH)), _const((CH, 128)), _const((1, 1024)), _ANY],
        out_specs=[_rows(tm, 1024), _rows(tm, CONV_CH), _rows(tm, 128), _rows(tm, D), _rows(tm, 1024, 0),
                   _rows(tm, 2048)],
        out_shape=[_sds((T, 1024), F32), _sds((T, CONV_CH), F32), _sds((T, 128), F32), _sds((T, D), MXU),
                   _sds((T, 2048), MXU), _sds((T, 2048), F32)],
        compiler_params=_cp(),
    )(x, g, wm, wdt, gv, ws, bst, gout, after)


def _gmlp_fwd_vals(u, v, gv, ws_ref, bst, gout):
    ug, dug = _gelu(u)
    vg, dvg = _gelu(v)
    row = lax.broadcasted_iota(jnp.int32, (CH, CH), 0)
    col = lax.broadcasted_iota(jnp.int32, (CH, CH), 1)
    tril = row >= col
    ys, heads = [], []
    for h in range(GM_HEADS):
        sl = slice(h * 128, (h + 1) * 128)
        vhat, rv = _rms(vg[:, sl])
        vn = (vhat * gv[:, sl]).astype(MXU)
        wt = jnp.where(tril, ws_ref[h], 0.0)
        mixed = _dot(wt.astype(MXU), vn) + bst[:, h:h + 1]
        ys.append(ug[:, sl] * mixed)
        heads.append((vhat, rv, vn, wt, mixed))
    y = jnp.concatenate(ys, axis=1)
    yhat, ry = _rms(y)
    return dict(ug=ug, dug=dug, dvg=dvg, heads=heads, yhat=yhat, ry=ry, tril=tril, out=yhat * gout)


def _shifts_down(cur, halo):
    row8 = lax.broadcasted_iota(jnp.int32, (8, cur.shape[1]), 0)
    out = [cur]
    for j in (1, 2, 3):
        sh = pltpu.roll(cur, j, 0)
        top = jnp.where(row8 < j, pltpu.roll(halo, j, 0), sh[0:8])
        out.append(jnp.concatenate([top, sh[8:]], axis=0))
    return out


def _shifts_up(cur, halo):
    row8 = lax.broadcasted_iota(jnp.int32, (8, cur.shape[1]), 0)
    out = []
    for j in (1, 2, 3):
        sh = pltpu.roll(cur, CH - j, 0)
        bot = jnp.where(row8 + j >= 8, pltpu.roll(halo, 8 - j, 0), sh[CH - 8:CH])
        out.append(jnp.concatenate([sh[0:CH - 8], bot], axis=0))
    return out


def _conv(xbc, halo, convw, convb):
    sh = _shifts_down(xbc, halo)
    return convb + convw[3:4] * sh[0] + convw[2:3] * sh[1] + convw[1:2] * sh[2] + convw[0:1] * sh[3]


def _ssd_fwd_vals(z, conv, dtraw, dtb, alog, dskip, ng, ex, ltri, s_prev):
    sig_c = _sigmoid(conv)
    xa = conv * sig_c
    xs = xa[:, :1024]
    bm = [xa[:, 1024:1152], xa[:, 1152:1280]]
    cm = [xa[:, 1280:1408], xa[:, 1408:1536]]
    dtpre = dtraw + dtb
    dt = _softplus(dtpre)
    a_neg = -jnp.exp(alog)
    cs = _dot_hi(ltri, dt * a_neg)
    cst = cs.T
    last = cs[CH - 1:CH]
    ecs = jnp.exp(cs)
    dec = jnp.exp(last - cs)
    spread = _dot_01(jnp.concatenate([dt, ecs, dec], axis=0), ex)
    dte, ecse, dece = spread[0:CH], spread[CH:2 * CH], spread[2 * CH:3 * CH]
    cde = ecse[CH - 1:CH]
    de = dskip
    xdt = xs * dte
    row = lax.broadcasted_iota(jnp.int32, (CH, CH), 0)
    col = lax.broadcasted_iota(jnp.int32, (CH, CH), 1)
    tril = row >= col
    lo = col < SSD_P
    bmb = [b.astype(MXU) for b in bm]
    cmb = [c.astype(MXU) for c in cm]
    mg = [_dot_nt(cmb[g], bmb[g]) for g in range(2)]
    yd, lms, whs = [], [], []
    for q in range(8):
        g = q // 4
        xq = xdt[:, q * 128:(q + 1) * 128]
        acc = None
        for hh in range(2):
            h = 2 * q + hh
            seg = cs[:, h:h + 1] - cst[h:h + 1, :]
            lm = jnp.exp(jnp.where(tril, seg, NEG))
            wh = (mg[g] * lm).astype(MXU)
            xm = jnp.where(lo if hh == 0 else ~lo, xq, 0.0).astype(MXU)
            part = _dot(wh, xm)
            acc = part if acc is None else acc + part
            lms.append(lm)
            whs.append(wh)
        yd.append(acc)
    yd = jnp.concatenate(yd, axis=1)
    sb = s_prev.astype(MXU)
    yo = jnp.concatenate([_dot(cmb[g], sb[:, g * 512:(g + 1) * 512]) for g in range(2)], axis=1) * ecse
    xdec = (xdt * dece).astype(MXU)
    states = jnp.concatenate([_dot_tn(bmb[g], xdec[:, g * 512:(g + 1) * 512]) for g in range(2)], axis=1)
    s_next = s_prev * cde + states
    ypre = yd + yo + de * xs
    sig_z = _sigmoid(z)
    yg = ypre * z * sig_z
    outs, yhat, rr = [], [], []
    for g in range(2):
        sl = slice(g * 512, (g + 1) * 512)
        yh, r = _rms(yg[:, sl])
        yhat.append(yh)
        rr.append(r)
        outs.append(yh * ng[:, sl])
    return dict(sig_c=sig_c, xa=xa, xs=xs, bmb=bmb, cmb=cmb, dtpre=dtpre, dt=dt, a_neg=a_neg,
                cs=cs, last=last, ecs=ecs, dec=dec, dte=dte, ecse=ecse, dece=dece, cde=cde, de=de, xdt=xdt,
                mg=mg, lms=lms, whs=whs, lo=lo, yo=yo, sb=sb, xdec=xdec, s_next=s_next, ypre=ypre, sig_z=sig_z,
                yhat=yhat, rr=rr, out=jnp.concatenate(outs, axis=1))


def _ssd_fwd(pz, pxbc, dtraw, cat, convw, convb, dtb, alog, dskip, ng, ex, ltri, nb):
    T = pz.shape[0]
    S = T // nb
    nch = S // CH
    ns = _SEQS_PER_STEP if nb % _SEQS_PER_STEP == 0 else 1

    def body(z_ref, xbc_ref, halo_ref, dt_ref, cw_ref, cb_ref, dtb_ref, al_ref, ds_ref, ng_ref, ex_ref, lt_ref,
             cat_in_ref, yb_ref, sall_ref, conv_ref, s_ref):
        del cat_in_ref
        c = pl.program_id(1)

        @pl.when(c == 0)
        def _():
            s_ref[...] = jnp.zeros_like(s_ref)

        for i in range(ns):
            halo = jnp.where(c == 0, 0.0, halo_ref[i])
            s_prev = s_ref[i]
            sall_ref[i, 0] = s_prev
            conv = _conv(xbc_ref[i], halo, cw_ref[...], cb_ref[...])
            conv_ref[i] = conv
            f = _ssd_fwd_vals(z_ref[i], conv, dt_ref[i], dtb_ref[...], al_ref[...], ds_ref[...], ng_ref[...],
                              ex_ref[...], lt_ref[...], s_prev)
            s_ref[i] = f["s_next"]
            yb_ref[i] = f["out"].astype(MXU)

    def seq(width, col=0):
        return pl.BlockSpec((ns, CH, width), lambda b, c: (b, c, col))

    cat, sall, conv = pl.pallas_call(
        body, grid=(nb // ns, nch), name="ssd_fwd",
        in_specs=[seq(1024), seq(CONV_CH),
                  pl.BlockSpec((ns, 8, CONV_CH), lambda b, c: (b, jnp.maximum(c * (CH // 8) - 1, 0), 0)),
                  seq(128),
                  _const((8, CONV_CH)), _const((1, CONV_CH)), _const((1, 128)), _const((1, 128)), _const((1, 1024)),
                  _const((1, 1024)), _const((128, 1024)), _const((CH, CH)), _ANY],
        out_specs=[seq(1024, 1), pl.BlockSpec((ns, 1, 128, 1024), lambda b, c: (b, c, 0, 0)), seq(CONV_CH)],
        out_shape=[_sds((nb, S, 2048), MXU), _sds((nb, nch, 128, 1024), F32), _sds((nb, S, CONV_CH), F32)],
        scratch_shapes=[pltpu.VMEM((ns, 128, 1024), F32)],
        input_output_aliases={12: 0},
        compiler_params=_cp(2),
    )(pz.reshape(nb, S, 1024), pxbc.reshape(nb, S, CONV_CH), pxbc.reshape(nb, S, CONV_CH), dtraw.reshape(nb, S, 128),
      convw, convb, dtb, alog, dskip, ng, ex, ltri, cat.reshape(nb, S, 2048))
    return cat.reshape(T, 2048), sall, conv.reshape(T, CONV_CH)


def _outproj(cat, wo, x, g, tm):
    T = x.shape[0]

    def body(cat_ref, wo_ref, x_ref, g_ref, h1_ref, hn_ref):
        h1 = x_ref[...] + _dot(cat_ref[...], wo_ref[...])
        h1_ref[...] = h1
        hn_ref[...] = (_rms(h1)[0] * g_ref[...]).astype(MXU)

    return pl.pallas_call(
        body, grid=(T // tm,), name="outproj",
        in_specs=[_rows(tm, 2048), _const((2048, D)), _rows(tm, D), _const((1, D))],
        out_specs=[_rows(tm, D), _rows(tm, D)],
        out_shape=[_sds((T, D), F32), _sds((T, D), MXU)],
        compiler_params=_cp(),
    )(cat, wo, x, g)


def _ff1(hn, w1, tm):
    T = hn.shape[0]

    def body(hn_ref, w1_ref, hid_ref):
        hn_v = hn_ref[...]
        for n in range(4):
            hid_ref[:, n * 1024:(n + 1) * 1024] = jnp.maximum(_dot(hn_v, w1_ref[n]), 0.0).astype(MXU)

    return pl.pallas_call(
        body, grid=(T // tm,), name="ff1",
        in_specs=[_rows(tm, D), _const((4, D, 1024))],
        out_specs=_rows(tm, DFF),
        out_shape=_sds((T, DFF), MXU),
        compiler_params=_cp(),
    )(hn, w1)


def _sq(hid):
    h = hid.astype(F32)
    return (h * h).astype(MXU)


def _ff2_tail(hid, w2, h1, g_ple, p, tgt, wg, wp, gf, tm):
    T = h1.shape[0]

    def body(hid_ref, w2_ref, h1_ref, g_ref, p_ref, t_ref, wg_ref, wp_ref, gf_ref,
             hp_ref, dgl_ref, dpe_ref, dh2_ref, dh2b_ref, loss_ref, dgf_ref, dg_ref):
        @pl.when(pl.program_id(0) == 0)
        def _():
            loss_ref[...] = jnp.zeros_like(loss_ref)
            dgf_ref[...] = jnp.zeros_like(dgf_ref)
            dg_ref[...] = jnp.zeros_like(dg_ref)

        h2 = h1_ref[...] + _dot(_sq(hid_ref[...]), w2_ref[...])
        h2h, r2 = _rms(h2)
        g_ple = g_ref[...]
        hp = (h2h * g_ple).astype(MXU)
        hp_ref[...] = hp
        gate = _sigmoid(_dot(hp, wg_ref[...]))
        pb = p_ref[...].astype(MXU)
        pe = jnp.concatenate([_dot(pb, wp_ref[k]) for k in range(4)], axis=1)
        h3 = h2 + gate * pe
        hh, r = _rms(h3)
        gf = gf_ref[...]
        diff = hh * gf - t_ref[...]
        loss_ref[...] += 0.5 * jnp.sum(jnp.mean(diff * diff, axis=-1, keepdims=True))
        dout = diff * (1.0 / D)
        dgf_ref[...] += jnp.sum(dout * hh, axis=0, keepdims=True)
        dh3 = _rms_bwd(dout, hh, r, gf)
        dgl = (dh3 * pe * gate * (1.0 - gate)).astype(MXU)
        dgl_ref[...] = dgl
        dpe_ref[...] = (dh3 * gate).astype(MXU)
        dhp = _dot_nt(dgl, wg_ref[...])
        dg_ref[...] += jnp.sum(dhp * h2h, axis=0, keepdims=True)
        dh2 = dh3 + _rms_bwd(dhp, h2h, r2, g_ple)
        dh2_ref[...] = dh2
        dh2b_ref[...] = dh2.astype(MXU)

    return pl.pallas_call(
        body, grid=(T // tm,), name="ff2_tail",
        in_specs=[_rows(tm, DFF), _const((DFF, D)), _rows(tm, D), _const((1, D)), _rows(tm, DPLE), _rows(tm, D),
                  _const((D, D)), _const((4, DPLE, 256)), _const((1, D))],
        out_specs=[_rows(tm, D), _rows(tm, D), _rows(tm, D), _rows(tm, D), _rows(tm, D), _const((8, 128)),
                   _const((1, D)), _const((1, D))],
        out_shape=[_sds((T, D), MXU), _sds((T, D), MXU), _sds((T, D), MXU), _sds((T, D), F32), _sds((T, D), MXU),
                   _sds((8, 128), F32), _sds((1, D), F32), _sds((1, D), F32)],
        compiler_params=_cp(),
    )(hid, w2, h1, g_ple, p, tgt, wg, wp, gf)


def _ff2_bwd(dh2b, w2, hid, tm):
    T = hid.shape[0]

    def body(dh2b_ref, w2_ref, hid_ref, dpre_ref):
        d = dh2b_ref[...]
        for n in range(DFF // 1024):
            sl = slice(n * 1024, (n + 1) * 1024)
            da = _dot_nt(d, w2_ref[sl, :])
            dpre_ref[:, sl] = (2.0 * da * hid_ref[:, sl].astype(F32)).astype(MXU)

    return pl.pallas_call(
        body, grid=(T // tm,), name="ff2_bwd",
        in_specs=[_rows(tm, D), _const((DFF, D)), _rows(tm, DFF)],
        out_specs=_rows(tm, DFF),
        out_shape=_sds((T, DFF), MXU),
        compiler_params=_cp(),
    )(dh2b, w2, hid)


def _ff1_bwd(dpre, w1, dh2, h1, g, tm, after):
    T = h1.shape[0]

    def body(dpre_ref, w1_ref, dh2_ref, h1_ref, g_ref, dh1_ref, dh1b_ref, dg_ref):
        @pl.when(pl.program_id(0) == 0)
        def _():
            dg_ref[...] = jnp.zeros_like(dg_ref)

        dhn = _dot_nt(dpre_ref[:, 0:1024], w1_ref[0])
        for k in range(1, 4):
            dhn = dhn + _dot_nt(dpre_ref[:, k * 1024:(k + 1) * 1024], w1_ref[k])
        hh, r = _rms(h1_ref[...])
        dg_ref[...] += jnp.sum(dhn * hh, axis=0, keepdims=True)
        dh1 = dh2_ref[...] + _rms_bwd(dhn, hh, r, g_ref[...])
        dh1_ref[...] = dh1
        dh1b_ref[...] = dh1.astype(MXU)

    return pl.pallas_call(
        _after(5, body), grid=(T // tm,), name="ff1_bwd",
        in_specs=[_rows(tm, DFF), _const((4, D, 1024)), _rows(tm, D), _rows(tm, D), _const((1, D)), _ANY],
        out_specs=[_rows(tm, D), _rows(tm, D), _const((1, D))],
        out_shape=[_sds((T, D), F32), _sds((T, D), MXU), _sds((1, D), F32)],
        compiler_params=_cp(),
    )(dpre, w1, dh2, h1, g, after)


def _outproj_bwd(dh1b, wo, tm):
    T = dh1b.shape[0]

    def body(d_ref, wo_ref, dcat_ref):
        d = d_ref[...]
        dcat_ref[:, 0:1024] = _dot_nt(d, wo_ref[0:1024, :])
        dcat_ref[:, 1024:2048] = _dot_nt(d, wo_ref[1024:2048, :])

    return pl.pallas_call(
        body, grid=(T // tm,), name="outproj_bwd",
        in_specs=[_rows(tm, D), _const((2048, D))],
        out_specs=_rows(tm, 2048),
        out_shape=_sds((T, 2048), F32),
        compiler_params=_cp(),
    )(dh1b, wo)


def _gmlp_bwd(uv, dcat, gv, ws, bst, gout, wm):
    T = uv.shape[0]
    nck = 4 if T % (4 * CH) == 0 else 1
    tb = nck * CH

    def body(uv_ref, dya_ref, gv_ref, ws_ref, bst_ref, gout_ref, wuv_ref, duv_ref, dgv_ref, dws_ref, dbst_ref,
             dgo_ref, dxn_ref):
        @pl.when(pl.program_id(0) == 0)
        def _():
            dgv_ref[...] = jnp.zeros_like(dgv_ref)
            dws_ref[...] = jnp.zeros_like(dws_ref)
            dbst_ref[...] = jnp.zeros_like(dbst_ref)
            dgo_ref[...] = jnp.zeros_like(dgo_ref)

        for k in range(nck):
            chunk(slice(k * CH, (k + 1) * CH), uv_ref, dya_ref, gv_ref, ws_ref, bst_ref, gout_ref, duv_ref,
                  dgv_ref, dws_ref, dbst_ref, dgo_ref)
        dxn_ref[...] = _dot_nt(duv_ref[...], wuv_ref[...])

    def chunk(rows, uv_ref, dya_ref, gv_ref, ws_ref, bst_ref, gout_ref, duv_ref, dgv_ref, dws_ref, dbst_ref,
              dgo_ref):
        gv = gv_ref[...]
        f = _gmlp_fwd_vals(uv_ref[rows, 0:1024], uv_ref[rows, 1024:2048], gv, ws_ref, bst_ref[...], gout_ref[...])
        dya = dya_ref[rows, :]
        dgo_ref[...] += jnp.sum(dya * f["yhat"], axis=0, keepdims=True)
        dy = _rms_bwd(dya, f["yhat"], f["ry"], gout_ref[...])
        lane = lax.broadcasted_iota(jnp.int32, (CH, 128), 1)
        dbs = jnp.zeros((CH, 128), F32)
        dug, dvg, dgvs = [], [], []
        for h in range(GM_HEADS):
            sl = slice(h * 128, (h + 1) * 128)
            vhat, rv, vn, wt, mixed = f["heads"][h]
            dyh = dy[:, sl]
            dug.append(dyh * mixed)
            dmixed = dyh * f["ug"][:, sl]
            dmb = dmixed.astype(MXU)
            dws_ref[h] += jnp.where(f["tril"], _dot_nt(dmb, vn), 0.0)
            dbs = dbs + jnp.where(lane == h, jnp.sum(dmixed, axis=1, keepdims=True), 0.0)
            dvn = _dot_tn(wt.astype(MXU), dmb)
            dgvs.append(jnp.sum(dvn * vhat, axis=0, keepdims=True))
            dvg.append(_rms_bwd(dvn, vhat, rv, gv[:, sl]))
        dbst_ref[...] += dbs
        dgv_ref[...] += jnp.concatenate(dgvs, axis=1)
        duv_ref[rows, 0:1024] = (jnp.concatenate(dug, axis=1) * f["dug"]).astype(MXU)
        duv_ref[rows, 1024:2048] = (jnp.concatenate(dvg, axis=1) * f["dvg"]).astype(MXU)

    return pl.pallas_call(
        body, grid=(T // tb,), name="gmlp_bwd",
        in_specs=[_rows(tb, 2048), _rows(tb, 1024, 0), _const((1, 1024)),
                  _const((GM_HEADS, CH, CH)), _const((CH, 128)), _const((1, 1024)), _const((D, 2048))],
        out_specs=[_rows(tb, 2048), _const((1, 1024)), _const((GM_HEADS, CH, CH)), _const((CH, 128)),
                   _const((1, 1024)), _rows(tb, D)],
        out_shape=[_sds((T, 2048), MXU), _sds((1, 1024), F32), _sds((GM_HEADS, CH, CH), F32), _sds((CH, 128), F32),
                   _sds((1, 1024), F32), _sds((T, D), F32)],
        compiler_params=_cp(),
    )(uv, dcat, gv, ws, bst, gout, wm)


def _ssd_bwd(pz, pxbc, conv, dtraw, sall, dcat, convw, dtb, alog, dskip, ng, ex, ltri, ext, nb, after):
    T = pz.shape[0]
    S = T // nb
    nch = S // CH
    ns = _SEQS_PER_STEP if nb % _SEQS_PER_STEP == 0 else 1

    def seq(width, col=0):
        return pl.BlockSpec((ns, CH, width), lambda b, c: (b, nch - 1 - c, col))

    in_specs = [
        seq(1024), seq(CONV_CH), seq(CONV_CH), seq(128),
        _const((8, CONV_CH)), _const((1, 128)), _const((1, 128)), _const((1, 1024)),
        _const((1, 1024)), _const((128, 1024)), _const((CH, CH)),
        _const((1024, 128)),
        pl.BlockSpec((ns, 1, 128, 1024), lambda b, c: (b, nch - 1 - c, 0, 0)),
        seq(1024, 1),
        _ANY,
    ]

    def body(z_ref, xbc_ref, conv_ref, dt_ref, cw_ref, dtb_ref, al_ref, ds_ref, ng_ref, ex_ref, lt_ref,
             ext_ref, sall_ref, dyb_ref,
             dssd_ref, ddt_ref, dcw_ref, dcb_ref, ddtb_ref, dal_ref, dds_ref, dng_ref,
             dst_ref, dnext_ref, ddse_ref):
        b = pl.program_id(0)
        c = pl.program_id(1)

        @pl.when((b == 0) & (c == 0))
        def _():
            for r in (dcw_ref, dcb_ref, ddtb_ref, dal_ref, dds_ref, dng_ref, ddse_ref):
                r[...] = jnp.zeros_like(r)

        @pl.when(c == 0)
        def _():
            dst_ref[...] = jnp.zeros_like(dst_ref)
            dnext_ref[...] = jnp.zeros_like(dnext_ref)

        ex = ex_ref[...]
        ext = ext_ref[...]
        cw = cw_ref[...]
        ng = ng_ref[...]
        for i in range(ns):
            one_chunk(i, ex, ext, cw, ng, z_ref, xbc_ref, conv_ref, dt_ref, dtb_ref, al_ref, ds_ref, lt_ref, sall_ref,
                      dyb_ref, dssd_ref, ddt_ref, dcw_ref, dcb_ref, ddtb_ref, dal_ref, dng_ref, dst_ref, dnext_ref,
                      ddse_ref)

        @pl.when((b == nb // ns - 1) & (c == nch - 1))
        def _():
            dds_ref[...] = _dot_01(jnp.broadcast_to(ddse_ref[...], (8, 1024)), ext)[0:1]

    def one_chunk(i, ex, ext, cw, ng, z_ref, xbc_ref, conv_ref, dt_ref, dtb_ref, al_ref, ds_ref, lt_ref, sall_ref,
                  dyb_ref, dssd_ref, ddt_ref, dcw_ref, dcb_ref, ddtb_ref, dal_ref, dng_ref, dst_ref, dnext_ref,
                  ddse_ref):
        z = z_ref[i]
        s_prev = sall_ref[i, 0]
        conv = conv_ref[i]
        f = _ssd_fwd_vals(z, conv, dt_ref[i], dtb_ref[...], al_ref[...], ds_ref[...], ng, ex, lt_ref[...], s_prev)
        xs, xdt, cs, dec, dt = f["xs"], f["xdt"], f["cs"], f["dec"], f["dt"]
        dyb = dyb_ref[i]
        dyg, dngs = [], []
        for g in range(2):
            sl = slice(g * 512, (g + 1) * 512)
            dngs.append(jnp.sum(dyb[:, sl] * f["yhat"][g], axis=0, keepdims=True))
            dyg.append(_rms_bwd(dyb[:, sl], f["yhat"][g], f["rr"][g], ng[:, sl]))
        dng_ref[...] += jnp.concatenate(dngs, axis=1)
        dyg = jnp.concatenate(dyg, axis=1)
        sig_z = f["sig_z"]
        silu_z = z * sig_z
        dy = dyg * silu_z
        dz = dyg * f["ypre"] * (sig_z + silu_z * (1.0 - sig_z))
        ddse_ref[...] += jnp.sum(dy * xs, axis=0, keepdims=True)
        dxs = dy * f["de"]
        dye = dy * f["ecse"]
        dyeb = dye.astype(MXU)
        dst = dst_ref[i]
        dstb = dst.astype(MXU)
        bmb, cmb, sb, xdec = f["bmb"], f["cmb"], f["sb"], f["xdec"]
        u = jnp.concatenate([_dot(bmb[g], dstb[:, g * 512:(g + 1) * 512]) for g in range(2)], axis=1)
        dxdt = [u[:, q * 128:(q + 1) * 128] * f["dece"][:, q * 128:(q + 1) * 128] for q in range(8)]
        per_head = _dot_01(jnp.concatenate(
            [dy * f["yo"], u * xdt, jnp.broadcast_to(jnp.sum(dst * s_prev, axis=0, keepdims=True), (8, 1024))],
            axis=0), ext)
        dcs = per_head[0:CH]
        t = per_head[CH:2 * CH] * dec
        dcd = per_head[2 * CH:2 * CH + 1]
        row = lax.broadcasted_iota(jnp.int32, (CH, 128), 0)
        lane = lax.broadcasted_iota(jnp.int32, (CH, 128), 1)
        cd = jnp.exp(f["last"])
        dcs = dcs - t + jnp.where(row == CH - 1, jnp.sum(t, axis=0, keepdims=True) + dcd * cd, 0.0)
        dcst = jnp.zeros((128, CH), F32)
        lo = f["lo"]
        dbm, dcm, ds_prev = [], [], []
        for g in range(2):
            sl = slice(g * 512, (g + 1) * 512)
            dmg = jnp.zeros((CH, CH), F32)
            for q in range(4 * g, 4 * g + 4):
                dyq = dy[:, q * 128:(q + 1) * 128]
                xq = xdt[:, q * 128:(q + 1) * 128].astype(MXU)
                for hh in range(2):
                    h = 2 * q + hh
                    m = lo if hh == 0 else ~lo
                    dym = jnp.where(m, dyq, 0.0).astype(MXU)
                    gh = _dot_nt(dym, xq)
                    gl = gh * f["lms"][h]
                    dmg = dmg + gl
                    qh = gl * f["mg"][g]
                    dcs = dcs + jnp.where(lane == h, jnp.sum(qh, axis=1, keepdims=True), 0.0)
                    dcst = dcst - jnp.where(row == h, jnp.sum(qh, axis=0, keepdims=True), 0.0)
                    dxdt[q] = dxdt[q] + _dot_tn(f["whs"][h], dym)
            dmgb = dmg.astype(MXU)
            dcm.append(_dot(dmgb, bmb[g]) + _dot_nt(dyeb[:, sl], sb[:, sl]))
            dbm.append(_dot_tn(dmgb, cmb[g]) + _dot_nt(xdec[:, sl], dstb[:, sl]))
            ds_prev.append(_dot_tn(cmb[g], dyeb[:, sl]))
        dst_ref[i] = jnp.concatenate(ds_prev, axis=1) + dst * f["cde"]
        dcs = dcs + dcst.T
        da = _dot_hi(lt_ref[...].T, dcs)
        dxdt = jnp.concatenate(dxdt, axis=1)
        a_neg = f["a_neg"]
        ddt = da * a_neg + _dot_01(dxdt * xs, ext)
        dal_ref[...] += jnp.sum(da * dt, axis=0, keepdims=True) * a_neg
        dxs = dxs + dxdt * f["dte"]
        ddtraw = jnp.where(lane < SSD_HEADS, ddt * _sigmoid(f["dtpre"]), 0.0)
        ddtb_ref[...] += jnp.sum(ddtraw, axis=0, keepdims=True)
        ddt_ref[i] = ddtraw.astype(MXU)
        dxa = jnp.concatenate([dxs, dbm[0], dbm[1], dcm[0], dcm[1]], axis=1)
        sig_c = f["sig_c"]
        dconv = dxa * (sig_c + f["xa"] * (1.0 - sig_c))
        dcb_ref[...] += jnp.sum(dconv, axis=0, keepdims=True)
        xbc = xbc_ref[i]
        dcw_ref[3:4, :] += jnp.sum(dconv * xbc, axis=0, keepdims=True)
        dxbc = cw[3:4] * dconv
        for j, up in zip((1, 2, 3), _shifts_up(dconv, dnext_ref[i])):
            dcw_ref[3 - j:4 - j, :] += jnp.sum(up * xbc, axis=0, keepdims=True)
            dxbc = dxbc + cw[3 - j:4 - j] * up
        dnext_ref[i] = dconv[0:8]
        dssd_ref[i, :, 0:1024] = dz.astype(MXU)
        dssd_ref[i, :, 1024:2560] = dxbc.astype(MXU)

    dssd, ddt, *small = pl.pallas_call(
        _after(14, body), grid=(nb // ns, nch), name="ssd_bwd",
        in_specs=in_specs,
        out_specs=[seq(2560), seq(128),
                   _const((8, CONV_CH)), _const((1, CONV_CH)), _const((1, 128)), _const((1, 128)), _const((1, 128)),
                   _const((1, 1024))],
        out_shape=[_sds((nb, S, 2560), MXU), _sds((nb, S, 128), MXU), _sds((8, CONV_CH), F32),
                   _sds((1, CONV_CH), F32), _sds((1, 128), F32), _sds((1, 128), F32), _sds((1, 128), F32),
                   _sds((1, 1024), F32)],
        scratch_shapes=[pltpu.VMEM((ns, 128, 1024), F32), pltpu.VMEM((ns, 8, CONV_CH), F32),
                        pltpu.VMEM((1, 1024), F32)],
        compiler_params=_cp(2),
    )(pz.reshape(nb, S, 1024), pxbc.reshape(nb, S, CONV_CH), conv.reshape(nb, S, CONV_CH), dtraw.reshape(nb, S, 128),
      convw, dtb, alog, dskip, ng, ex, ltri, ext, sall, dcat.reshape(nb, S, 2048), after)
    return (dssd.reshape(T, 2560), ddt.reshape(T, 128), *small)


def _inproj_bwd(dxn_uv, dssd, ddt, wm, wdt, dh1, x, g, tm, after):
    T = x.shape[0]

    def body(dxnuv_ref, dssd_ref, ddt_ref, wm_ref, wdt_ref, dh1_ref, x_ref, g_ref, dx_ref, dg_ref):
        @pl.when(pl.program_id(0) == 0)
        def _():
            dg_ref[...] = jnp.zeros_like(dg_ref)

        dxn = (dxnuv_ref[...] + _dot_nt(dssd_ref[...], wm_ref[:, 2048:N_MAIN])
               + _dot_nt(ddt_ref[...], wdt_ref[...]))
        xh, r = _rms(x_ref[...])
        dg_ref[...] += jnp.sum(dxn * xh, axis=0, keepdims=True)
        dx_ref[...] = dh1_ref[...] + _rms_bwd(dxn, xh, r, g_ref[...])

    return pl.pallas_call(
        _after(8, body), grid=(T // tm,), name="inproj_bwd",
        in_specs=[_rows(tm, D), _rows(tm, 2560), _rows(tm, 128), _const((D, N_MAIN)), _const((D, 128)),
                  _rows(tm, D), _rows(tm, D), _const((1, D)), _ANY],
        out_specs=[_rows(tm, D), _const((1, D))],
        out_shape=[_sds((T, D), F32), _sds((1, D), F32)],
        compiler_params=_cp(),
    )(dxn_uv, dssd, ddt, wm, wdt, dh1, x, g, after)


def _matmul_tn(a, b, name, a_fn=None):
    T, M = a.shape
    N = b.shape[1]
    tm = min(M, 1024)
    tn = 1280 if N == 2560 else min(N, 1024)
    tk = min(T, 2048)

    def body(a_ref, b_ref, o_ref, acc_ref):
        k = pl.program_id(2)

        @pl.when(k == 0)
        def _():
            acc_ref[...] = jnp.zeros_like(acc_ref)

        av = a_ref[...]
        if a_fn is not None:
            av = a_fn(av)
        acc_ref[...] += _dot_tn(av, b_ref[...])

        @pl.when(k == T // tk - 1)
        def _():
            o_ref[...] = acc_ref[...].astype(o_ref.dtype)

    return pl.pallas_call(
        body, grid=(M // tm, N // tn, T // tk), name=name,
        in_specs=[pl.BlockSpec((tk, tm), lambda i, j, k: (k, i)), pl.BlockSpec((tk, tn), lambda i, j, k: (k, j))],
        out_specs=pl.BlockSpec((tm, tn), lambda i, j, k: (i, j)),
        out_shape=_sds((M, N), GRAD),
        scratch_shapes=[pltpu.VMEM((tm, tn), F32)],
        compiler_params=_cp(3),
    )(a, b)


def _adamw_vals(w, g, m, v):
    m = B1 * m + (1.0 - B1) * g
    v = B2 * v + (1.0 - B2) * (g * g)
    m_hat = m / (1.0 - B1 ** STEP)
    v_hat = v / (1.0 - B2 ** STEP)
    return -LR * (m_hat / (jnp.sqrt(v_hat) + ADAM_EPS) + WD * w), m, v


def _adamw(w, g, m, v, name):
    R, C = w.shape
    tr = 256 if R % 256 == 0 else R

    def body(w_ref, g_ref, m_ref, v_ref, d_ref, mo_ref, vo_ref):
        d_ref[...], mo_ref[...], vo_ref[...] = _adamw_vals(w_ref[...], g_ref[...], m_ref[...], v_ref[...])

    spec = _rows(tr, C)
    return pl.pallas_call(
        body, grid=(R // tr,), name=name,
        in_specs=[spec] * 4, out_specs=[spec] * 3, out_shape=[_sds((R, C), F32)] * 3,
        compiler_params=_cp(),
    )(w, g, m, v)


_PARTS = 4


def _adamw_halves(items, name):
    n = len(items)

    def body(*refs):
        mine = (pl.program_id(0) // _PARTS) == lax.axis_index("c")
        for k in range(n):
            w_ref, own_ref, oth_ref, m_ref, v_ref = refs[5 * k:5 * k + 5]
            g_ref, d_ref, mo_ref, vo_ref = refs[5 * n + 4 * k:5 * n + 4 * k + 4]
            g = jnp.where(mine, own_ref[...], oth_ref[...])
            g_ref[...] = g
            d_ref[...], mo_ref[...], vo_ref[...] = _adamw_vals(w_ref[...], g, m_ref[...], v_ref[...])

    in_specs, out_specs, out_shape = [], [], []
    for w, *_ in items:
        R, C = w.shape
        full = _rows(R // (2 * _PARTS), C)
        part = pl.BlockSpec((R // (2 * _PARTS), C), lambda i: (i % _PARTS, 0))
        in_specs += [full, part, part, full, full]
        out_specs += [full] * 4
        out_shape += [_sds((R, C), F32)] * 4
    res = pl.pallas_call(
        body, grid=(2 * _PARTS,), name=name, in_specs=in_specs, out_specs=out_specs, out_shape=out_shape,
        compiler_params=_cp(),
    )(*[a for item in items for a in item])
    return [tuple(res[4 * k:4 * k + 4]) for k in range(n)]


_TJ = 128


def _adamw_transposed(w, own, other, m, v, name):
    C, _, R = w.shape

    def body(w_ref, own_ref, oth_ref, m_ref, v_ref, g_ref, d_ref, mo_ref, vo_ref):
        first = lax.axis_index("c") == 0
        g = jnp.concatenate([jnp.where(first, own_ref[...], oth_ref[...]),
                             jnp.where(first, oth_ref[...], own_ref[...])], axis=0).T
        d, mo, vo = _adamw_vals(w_ref[:, 0, :], g, m_ref[:, 0, :], v_ref[:, 0, :])
        for ref, val in ((g_ref, g), (d_ref, d), (mo_ref, mo), (vo_ref, vo)):
            ref[:, 0, :] = val

    cols = pl.BlockSpec((_TJ, 1, R), lambda j: (j, 0, 0))
    half = pl.BlockSpec((R // 2, _TJ), lambda j: (0, j))
    return pl.pallas_call(
        body, grid=(pl.cdiv(C, _TJ),), name=name,
        in_specs=[cols, half, half, cols, cols], out_specs=[cols] * 4, out_shape=[_sds((C, 1, R), F32)] * 4,
        compiler_params=_cp(),
    )(w, own, other, m, v)


def _sum_small(slots, name):
    nd, rows, C = slots.shape

    def body(s_ref, o_ref):
        acc = s_ref[0]
        for d in range(1, nd):
            acc = acc + s_ref[d]
        o_ref[...] = acc

    return pl.pallas_call(
        body, grid=(1,), name=name,
        in_specs=[_const((nd, rows, C))], out_specs=_const((rows, C)), out_shape=_sds((rows, C), F32),
        compiler_params=_cp(),
    )(slots)


def _sum_slots(items, kh, name):
    n = len(items)
    in_specs, out_specs, out_shape = [], [], []
    for slots, src, kind, (R, C) in items:
        tr = R // (2 * _PARTS)
        if kind == "slab":
            src_spec = pl.BlockSpec((1, tr, C), lambda i, kh: (kh[0], kh[1] * _PARTS + i, 0))
        elif kind == "rows":
            src_spec = pl.BlockSpec((tr, C), lambda i, kh: (kh[0] * (2 * _PARTS) + kh[1] * _PARTS + i, 0))
        else:
            src_spec = pl.BlockSpec((tr, C), lambda i, kh: (kh[1] * _PARTS + i, kh[0]))
        in_specs += [pl.BlockSpec((8, tr, C), lambda i, kh: (0, i, 0)), src_spec]
        out_specs.append(pl.BlockSpec((tr, C), lambda i, kh: (i, 0)))
        out_shape.append(_sds((R // 2, C), F32))

    def body(kh_ref, *refs):
        me = 2 * kh_ref[0] + kh_ref[1]
        for k, (_, _, kind, _) in enumerate(items):
            s_ref, own_ref, o_ref = refs[2 * k], refs[2 * k + 1], refs[2 * n + k]
            acc = (own_ref[0] if kind == "slab" else own_ref[...]).astype(F32)
            for j in range(1, 8):
                acc = acc + s_ref[me ^ j].astype(F32)
            o_ref[...] = acc

    return pl.pallas_call(
        body, name=name,
        grid_spec=pltpu.PrefetchScalarGridSpec(
            num_scalar_prefetch=1, grid=(_PARTS,), in_specs=in_specs, out_specs=out_specs),
        out_shape=out_shape,
        compiler_params=_cp(),
    )(kh, *[a for slots, src, _, _ in items for a in (slots, src)])


def _assemble_w_in(slabs):
    tr = 256

    def body(s_ref, wm_ref, wdt_ref):
        full = jnp.concatenate([s_ref[k] for k in range(4)], axis=1)
        wm_ref[...] = full[:, :N_MAIN]
        wdt_ref[...] = jnp.concatenate([full[:, N_MAIN:], jnp.zeros((tr, 128 - 16), full.dtype)], axis=1)

    return pl.pallas_call(
        body, grid=(D // tr,), name="assemble_w_in",
        in_specs=[pl.BlockSpec((4, tr, 1156), lambda i: (0, i, 0))],
        out_specs=[_rows(tr, N_MAIN), _rows(tr, 128)],
        out_shape=[_sds((D, N_MAIN), slabs.dtype), _sds((D, 128), slabs.dtype)],
        compiler_params=_cp(),
    )(slabs)


def _split_dw_in(d_uv, d_ssd, d_dt):
    tr = 256

    def body(uv_ref, ssd_ref, dt_ref, o_ref):
        full = jnp.concatenate([uv_ref[...], ssd_ref[...], dt_ref[:, 0:16]], axis=1)
        for k in range(4):
            o_ref[k] = full[:, 1156 * k:1156 * (k + 1)]

    return pl.pallas_call(
        body, grid=(D // tr,), name="split_dw_in",
        in_specs=[_rows(tr, 2048), _rows(tr, 2560), _rows(tr, 128)],
        out_specs=pl.BlockSpec((4, tr, 1156), lambda i: (0, i, 0)),
        out_shape=_sds((4, D, 1156), d_uv.dtype),
        compiler_params=_cp(),
    )(d_uv, d_ssd, d_dt)


def _cast_w_in(w, kh):
    C, _, R = w.shape

    def body(kh_ref, w_ref, o_ref):
        o_ref[0] = w_ref[:, 0, :].T.astype(BF16)

    return pl.pallas_call(
        body, name="cast_w_in",
        grid_spec=pltpu.PrefetchScalarGridSpec(
            num_scalar_prefetch=1, grid=(pl.cdiv(C, _TJ),),
            in_specs=[pl.BlockSpec((_TJ, 1, R), lambda j, kh: (j, 0, 0))],
            out_specs=pl.BlockSpec((1, R, _TJ), lambda j, kh: (kh[0], 0, j))),
        out_shape=_sds((4, R, C), BF16),
        compiler_params=_cp(),
    )(kh, w)


def _cast_into_slot(ws, kh, name):
    n = len(ws)

    def body(kh_ref, *refs):
        for k in range(n):
            refs[n + k][0] = refs[k][...].astype(BF16)

    return pl.pallas_call(
        body, name=name,
        grid_spec=pltpu.PrefetchScalarGridSpec(
            num_scalar_prefetch=1, grid=(_PARTS,),
            in_specs=[pl.BlockSpec((w.shape[0] // _PARTS, w.shape[1]), lambda i, kh: (i, 0)) for w in ws],
            out_specs=[pl.BlockSpec((1, w.shape[0] // _PARTS, w.shape[1]), lambda i, kh: (kh[0], i, 0))
                       for w in ws]),
        out_shape=[_sds((4,) + w.shape, BF16) for w in ws],
        compiler_params=_cp(),
    )(kh, *ws)


_ANY = pl.BlockSpec(memory_space=pl.ANY)
_CHIP_FLIPS = [(1, 0), (0, 1), (1, 1)]
_DEVICE_FLIPS = [(fx, fy, fc) for fx in (0, 1) for fy in (0, 1) for fc in (0, 1)][1:]


def _half(h, rows):
    return pl.ds(pl.multiple_of(h * rows, rows), rows)


def _remote(src, dst, ssem, rsem, to):
    return pltpu.make_async_remote_copy(src_ref=src, dst_ref=dst, send_sem=ssem, recv_sem=rsem,
                                        device_id=to, device_id_type=MESH)


def _weight_gather(bufs, conv):
    n = len(bufs)

    def body(*refs):
        conv_ref, outs, conv_out = refs[n], refs[n + 1:2 * n + 1], refs[2 * n + 1]
        send_sems, recv_sems, fsend_sems, frecv_sems, csend_sems, crecv_sems, local_sem = refs[2 * n + 2:]
        x, y, c = lax.axis_index("x"), lax.axis_index("y"), lax.axis_index("c")
        me = 2 * x + y
        halves = [_half(c, r.shape[1] // 2) for r in outs]
        others = [_half(1 - c, r.shape[1] // 2) for r in outs]
        remote = _remote
        local = [pltpu.make_async_copy(conv_ref, conv_out.at[me], local_sem)]
        for cp in local:
            cp.start()
        sends = []
        for k, (fx, fy) in enumerate(_CHIP_FLIPS):
            peer = (x ^ fx, y ^ fy, c)
            for i in range(n):
                mine = outs[i].at[me, halves[i]]
                sends.append(remote(mine, mine, send_sems.at[k * n + i], recv_sems.at[k * n + i], peer))
            sends.append(remote(conv_ref, conv_out.at[me], csend_sems.at[k], crecv_sems.at[k], peer))
        for cp in sends:
            cp.start()
        sibling = (x, y, 1 - c)
        forwards = []
        for k, (fx, fy) in enumerate(_CHIP_FLIPS):
            peer = (x ^ fx, y ^ fy, c)
            src = 2 * (x ^ fx) + (y ^ fy)
            for i in range(n):
                landed = outs[i].at[src, halves[i]]
                remote(landed, landed, send_sems.at[k * n + i], recv_sems.at[k * n + i], peer).wait_recv()
                fw = remote(landed, landed, fsend_sems.at[k * n + i], frecv_sems.at[k * n + i], sibling)
                fw.start()
                forwards.append(fw)
            remote(conv_out.at[src], conv_out.at[src], csend_sems.at[k], crecv_sems.at[k], peer).wait_recv()
        for k, (fx, fy) in enumerate(_CHIP_FLIPS):
            src = 2 * (x ^ fx) + (y ^ fy)
            for i in range(n):
                theirs = outs[i].at[src, others[i]]
                remote(theirs, theirs, fsend_sems.at[k * n + i], frecv_sems.at[k * n + i], sibling).wait_recv()
        for cp in sends + forwards:
            cp.wait_send()
        for cp in local:
            cp.wait()

    dma = pltpu.SemaphoreType.DMA
    return pl.pallas_call(
        body, name="weight_gather",
        in_specs=[_ANY] * (n + 1), out_specs=[_ANY] * (n + 1),
        out_shape=[_sds(b.shape, b.dtype) for b in bufs] + [_sds((4,) + conv.shape, conv.dtype)],
        input_output_aliases={i: i for i in range(n)},
        scratch_shapes=[dma((3 * n,)), dma((3 * n,)), dma((3 * n,)), dma((3 * n,)), dma((3,)), dma((3,)), dma],
    )(*bufs, conv)


def _piece(ref, kind, R, C, k, h):
    if kind == "slab":
        return ref.at[k, _half(h, R // 2), :]
    if kind == "rows":
        return ref.at[pl.ds(pl.multiple_of(k * R + h * (R // 2), R // 2), R // 2), :]
    return ref.at[_half(h, R // 2), pl.ds(pl.multiple_of(k * C, C), C)]


_HBM = pl.BlockSpec(memory_space=pltpu.HBM)
_SEM = pl.BlockSpec(memory_space=pltpu.SEMAPHORE)


def _split_start(name, arrays, n_copies, plan, after=None):
    n = len(arrays)
    extra = [] if after is None else [after]

    def body(*refs):
        m = n + len(extra)
        arrs, send_sems, recv_sems, token = refs[:n], refs[m], refs[m + 1], refs[-1]
        for j, (src, dst, peer) in enumerate(plan(arrs)):
            _remote(src, dst, send_sems.at[j], recv_sems.at[j], peer).start()
        token[...] = jnp.zeros_like(token)

    dma = pltpu.SemaphoreType.DMA
    res = pl.pallas_call(
        body, name=name,
        out_shape=(dma((n_copies,)), dma((n_copies,)), *[pltpu.HBM(a.shape, a.dtype) for a in arrays],
                   _sds((8, 128), F32)),
        in_specs=[_HBM] * n + [_ANY] * len(extra),
        out_specs=(_SEM, _SEM, *[_HBM] * n, pl.BlockSpec(memory_space=pltpu.VMEM)),
        input_output_aliases={i: 2 + i for i in range(n)},
        compiler_params=pltpu.CompilerParams(has_side_effects=pltpu.SideEffectType.DATAFLOW_SIDE_EFFECTING),
    )(*[pltpu.with_memory_space_constraint(a, pltpu.HBM) for a in arrays], *extra)
    return res[0], res[1], list(res[2:2 + n]), res[-1]


def _split_wait(name, arrays, send_sems, recv_sems, plan, after):
    n = len(arrays)

    def body(*refs):
        arrs, ssems, rsems = refs[:n], refs[n], refs[n + 1]
        for j, (src, dst, peer) in enumerate(plan(arrs)):
            cp = _remote(src, dst, ssems.at[j], rsems.at[j], peer)
            cp.wait_send()
            cp.wait_recv()

    return list(pl.pallas_call(
        body, name=name,
        out_shape=tuple(pltpu.HBM(a.shape, a.dtype) for a in arrays),
        in_specs=[_HBM] * n + [_SEM, _SEM, _ANY],
        out_specs=tuple([_HBM] * n),
        input_output_aliases={i: i for i in range(n)},
        compiler_params=pltpu.CompilerParams(has_side_effects=pltpu.SideEffectType.DATAFLOW_SIDE_EFFECTING),
    )(*arrays, send_sems, recv_sems, after))


def _gather_plan(n):
    def plan(bufs):
        x, y, c = lax.axis_index("x"), lax.axis_index("y"), lax.axis_index("c")
        me = 2 * x + y
        return [(bufs[i].at[me], bufs[i].at[me], (x ^ fx, y ^ fy, c)) for fx, fy in _CHIP_FLIPS for i in range(n)]

    return plan


def _reduce_plan(specs, n_small):
    n = len(specs)

    def plan(arrs):
        x, y, c = lax.axis_index("x"), lax.axis_index("y"), lax.axis_index("c")
        slot = 4 * x + 2 * y + c
        out = []
        for fx, fy, fc in _DEVICE_FLIPS:
            peer = (x ^ fx, y ^ fy, c ^ fc)
            for i, (kind, (R, C)) in enumerate(specs):
                out.append((_piece(arrs[i], kind, R, C, 2 * peer[0] + peer[1], peer[2]), arrs[n + i].at[slot], peer))
            for s in range(n_small):
                out.append((arrs[2 * n + 2 * s], arrs[2 * n + 2 * s + 1].at[slot], peer))
        return out

    return plan


def _sibling_exchange(halves, name, small=None):
    n = len(halves)
    ns = 0 if small is None else 1

    def body(*refs):
        ins, outs = refs[:n], refs[n + ns:2 * n + ns]
        send_sems, recv_sems = refs[2 * (n + ns)], refs[2 * (n + ns) + 1]
        x, y, c = lax.axis_index("x"), lax.axis_index("y"), lax.axis_index("c")
        copies = [_remote(ins[i], outs[i], send_sems.at[i], recv_sems.at[i], (x, y, 1 - c)) for i in range(n)]
        waits = list(copies)
        if ns:
            s_ref, slots_ref, ssend_sems, srecv_sems, local_sem = refs[n], refs[2 * n + 1], *refs[2 * (n + ns) + 2:]
            slot = 4 * x + 2 * y + c
            own = pltpu.make_async_copy(s_ref, slots_ref.at[slot], local_sem)
            own.start()
            for k, (fx, fy, fc) in enumerate(_DEVICE_FLIPS):
                peer = (x ^ fx, y ^ fy, c ^ fc)
                copies.append(_remote(s_ref, slots_ref.at[slot], ssend_sems.at[k], srecv_sems.at[k], peer))
                theirs = slots_ref.at[slot ^ (k + 1)]
                waits.append(_remote(theirs, theirs, ssend_sems.at[k], srecv_sems.at[k], peer))
        for cp in copies:
            cp.start()
        for cp in waits:
            cp.wait()
        if ns:
            own.wait()

    dma = pltpu.SemaphoreType.DMA
    extra_in = [] if small is None else [small]
    extra_out = [] if small is None else [_sds((8,) + small.shape, F32)]
    return pl.pallas_call(
        body, name=name,
        in_specs=[_ANY] * (n + ns), out_specs=[_ANY] * (n + ns),
        out_shape=[_sds(h.shape, h.dtype) for h in halves] + extra_out,
        scratch_shapes=[dma((n,)), dma((n,))] + ([dma((7,)), dma((7,)), dma] if ns else []),
    )(*halves, *extra_in)


_BIG = [("w_in", (1024, 1156), "slab"), ("w_out", (512, 1024), "rows"), ("w_ff1", (1024, 1024), "cols"),
        ("w_ff2", (1024, 1024), "rows"), ("w_ple_gate", (256, 1024), "rows"), ("w_ple_proj", (256, 256), "cols")]
_SMALL = [("norm_mix_g", (1, 1024)), ("gm_v_norm_g", (1, 1024)), ("gm_ws", (1, 8, 128, 128)), ("gm_bs", (1, 8, 128)),
          ("gm_out_norm_g", (1, 1024)), ("ssd_conv_w", (1, 4, 1536)), ("ssd_conv_b", (1, 1536)),
          ("ssd_dt_bias", (1, 16)), ("ssd_a_log", (1, 16)), ("ssd_d", (1, 16)), ("ssd_norm_g", (1, 1024)),
          ("norm_mlp_g", (1, 1024)), ("ple_norm_g", (1, 1024)), ("final_norm_g", (1024,))]


def _rows128(a):
    flat = a.reshape(-1)
    rows = -(-flat.shape[0] // 1024) * 8
    return jnp.pad(flat, (0, rows * 128 - flat.shape[0])).reshape(rows, 128)


def _pad_lanes(v, n=128):
    v = v.reshape(1, -1)
    return jnp.pad(v, ((0, 0), (0, n - v.shape[1])))


_SMALL_SHAPES = dict(_SMALL + [("loss", ())])
_BIG_SPECS = {n: (kind, shp) for n, shp, kind in _BIG}


class _Comm:
    def __init__(self, a, kh):
        self.a, self.kh = a, kh
        rest = _BIG[1:]
        self.bufs = {"w_in": _cast_w_in(a["w_in"].transpose(2, 0, 1), kh)}
        cast = _cast_into_slot([a[n].reshape(shp) for n, shp, _ in rest], kh, "cast_rest")
        self.bufs.update({n: c for (n, _, _), c in zip(rest, cast)})
        self.sent = []
        self.small_tot = {}

    def w_in(self):
        g_win, g_cw = _weight_gather([self.bufs["w_in"]], self.a["ssd_conv_w"].reshape(4, 384))
        token = g_cw
        self.gather = {}
        for tag, names in (("out", ["w_out"]), ("ff", ["w_ff1", "w_ff2", "w_ple_gate", "w_ple_proj"])):
            plan = _gather_plan(len(names))
            ssem, rsem, thru, token = _split_start("gather_start_" + tag, [self.bufs[n] for n in names],
                                                   3 * len(names), plan, after=token)
            self.gather[tag] = (plan, ssem, rsem, thru)
        wm, wdt = _assemble_w_in(g_win)
        return wm, wdt, jnp.concatenate([g_cw[k] for k in range(4)], axis=1), token

    def rest(self, tag, after):
        plan, ssem, rsem, thru = self.gather[tag]
        got = _split_wait("gather_wait_" + tag, thru, ssem, rsem, plan, after)
        if tag == "out":
            return got[0].reshape(2048, D)
        g_w1, g_w2, g_wg, g_wp = got
        return g_w1, g_w2.reshape(DFF, D), g_wg.reshape(D, D), g_wp

    def send(self, tag, grads):
        big = [n for n, _, _ in _BIG if n in grads]
        small = [n for n in _SMALL_SHAPES if n in grads]
        parts = [_rows128(grads[n]) for n in small]
        rows = [s.shape[0] for s in parts]
        if not big:
            self.last_small = (tag, small, rows, jnp.concatenate(parts, axis=0))
            return None
        srcs = [grads[n] for n in big]
        lands = [lax.empty((8, _BIG_SPECS[n][1][0] // 2, _BIG_SPECS[n][1][1]), GRAD) for n in big]
        extra = []
        if small:
            pack = jnp.concatenate(parts, axis=0)
            extra = [pack, jnp.broadcast_to(pack, (8,) + pack.shape)]
        plan = _reduce_plan([_BIG_SPECS[n] for n in big], len(extra) // 2)
        n_copies = 7 * (len(big) + len(extra) // 2)
        ssem, rsem, thru, token = _split_start("reduce_start_" + tag, srcs + lands + extra, n_copies, plan)
        self.sent.append((tag, big, small, rows, plan, ssem, rsem, thru))
        return token

    def _unpack(self, tot, names, rows):
        o = 0
        for n, r in zip(names, rows):
            shp = _SMALL_SHAPES[n]
            cnt = 1
            for s in shp:
                cnt *= s
            self.small_tot[n] = tot[o:o + r].reshape(-1)[:cnt].reshape(shp)
            o += r

    def finish(self, after):
        a, results = self.a, {}

        def update(names, own, tag):
            if names == ["w_in"]:
                stag, small, rows, pack = self.last_small
                *other, slots = _sibling_exchange([own[n] for n in names], "sibling_exchange_" + tag, pack)
                self._unpack(_sum_small(slots, "sum_small_" + stag), small, rows)
                w, m, v = (a[k].transpose(2, 0, 1) for k in ("w_in", "m_w_in", "v_w_in"))
                raw = _adamw_transposed(w, own["w_in"], other[0], m, v, "adamw_" + tag)
                results["w_in"] = tuple(r.transpose(1, 2, 0) for r in raw)
                return raw[1]
            other = _sibling_exchange([own[n] for n in names], "sibling_exchange_" + tag)
            items = [(a[n].reshape(_BIG_SPECS[n][1]), own[n], oth, a["m_" + n].reshape(_BIG_SPECS[n][1]),
                      a["v_" + n].reshape(_BIG_SPECS[n][1])) for n, oth in zip(names, other)]
            results.update(zip(names, _adamw_halves(items, "adamw_" + tag)))
            return results[names[-1]][1]

        own, early = {}, []
        for tag, big, small, rows, plan, ssem, rsem, thru in self.sent:
            if tag == self.sent[-1][0]:
                after = update(early, own, "early")
            arrs = _split_wait("reduce_wait_" + tag, thru, ssem, rsem, plan, after)
            nb_ = len(big)
            sums = _sum_slots([(arrs[nb_ + i], arrs[i]) + _BIG_SPECS[n] for i, n in enumerate(big)], self.kh,
                              "sum_" + tag)
            own.update(zip(big, sums))
            after = sums[-1]
            early += big
            if small:
                self._unpack(_sum_small(arrs[2 * nb_ + 1], "sum_small_" + tag), small, rows)
        update(self.sent[-1][1], own, "late")
        return results, dict(self.small_tot)


def _local_step(x, p, tgt, sm, comm, nb, tm):
    T = x.shape[0]
    wm, wdt, conv_w, token = comm.w_in()
    g_mix, gv, gout = sm["norm_mix_g"].reshape(1, D), sm["gm_v_norm_g"].reshape(1, D), sm["gm_out_norm_g"].reshape(1, D)
    ws = sm["gm_ws"].reshape(GM_HEADS, CH, CH)
    bst = jnp.pad(sm["gm_bs"].reshape(GM_HEADS, CH).T, ((0, 0), (0, 128 - GM_HEADS)))
    convw = jnp.pad(conv_w, ((0, 4), (0, 0)))
    convb = sm["ssd_conv_b"].reshape(1, CONV_CH)
    dtb, alog = _pad_lanes(sm["ssd_dt_bias"]), _pad_lanes(sm["ssd_a_log"])
    dskip = jnp.repeat(sm["ssd_d"].reshape(SSD_HEADS), SSD_P).reshape(1, 1024)
    ng, g_mlp, g_ple = sm["ssd_norm_g"].reshape(1, D), sm["norm_mlp_g"].reshape(1, D), sm["ple_norm_g"].reshape(1, D)
    gf = sm["final_norm_g"].reshape(1, D)
    head_of_lane = lax.broadcasted_iota(jnp.int32, (128, 1024), 1) // SSD_P
    ex = (lax.broadcasted_iota(jnp.int32, (128, 1024), 0) == head_of_lane).astype(BF16)
    ext = ex.T
    ltri = (lax.broadcasted_iota(jnp.int32, (CH, CH), 0) >= lax.broadcasted_iota(jnp.int32, (CH, CH), 1)).astype(F32)

    pz, pxbc, dtraw, xn, cat, uv = _inproj_gmlp(x, g_mix, wm, wdt, gv, ws, bst, gout, tm, token)
    cat, sall, conv = _ssd_fwd(pz, pxbc, dtraw, cat, convw, convb, dtb, alog, dskip, ng, ex, ltri, nb)
    wo = comm.rest("out", cat)
    h1, hn = _outproj(cat, wo, x, g_mlp, tm)
    w1, w2, wg, wp = comm.rest("ff", hn)
    hid = _ff1(hn, w1, min(T, 2 * tm))
    hp, dgl, dpe, dh2, dh2b, loss, d_gf, d_gple = _ff2_tail(hid, w2, h1, g_ple, p, tgt, wg, wp, gf, tm)

    d_wp = _matmul_tn(p, dpe, "dw_ple_proj", a_fn=lambda a: a.astype(MXU))
    d_wg = _matmul_tn(hp, dgl, "dw_ple_gate")
    d_w2 = _matmul_tn(hid, dh2b, "dw_ff2", a_fn=_sq)
    dpre = _ff2_bwd(dh2b, w2, hid, min(T, 2 * tm))
    d_w1 = _matmul_tn(hn, dpre, "dw_ff1")
    token = comm.send("a", {"w_ple_proj": d_wp, "w_ple_gate": d_wg, "w_ff2": d_w2, "w_ff1": d_w1})
    dh1, dh1b, d_gmlp = _ff1_bwd(dpre, w1, dh2, h1, g_mlp, tm, token)
    dcat = _outproj_bwd(dh1b, wo, min(T, 2 * tm))
    d_wo = _matmul_tn(cat, dh1b, "dw_out")
    duv, d_gv, d_ws, d_bst, d_gout, dxn_uv = _gmlp_bwd(uv, dcat, gv, ws, bst, gout, wm)
    token = comm.send("b", {
        "w_out": d_wo, "loss": loss[0:1, 0:1], "final_norm_g": d_gf, "ple_norm_g": d_gple, "norm_mlp_g": d_gmlp,
        "gm_v_norm_g": d_gv, "gm_ws": d_ws, "gm_bs": d_bst[:, :GM_HEADS].T, "gm_out_norm_g": d_gout})
    dssd, ddt, d_cw, d_cb, d_dtb, d_al, d_ds, d_ng = _ssd_bwd(
        pz, pxbc, conv, dtraw, sall, dcat, convw, dtb, alog, dskip, ng, ex, ltri, ext, nb, token)
    d_win = _split_dw_in(_matmul_tn(xn, duv, "dw_in_uv"), _matmul_tn(xn, dssd, "dw_in_ssd"),
                         _matmul_tn(xn, ddt, "dw_in_dt"))
    token = comm.send("c", {"w_in": d_win})
    dx, d_gmix = _inproj_bwd(dxn_uv, dssd, ddt, wm, wdt, dh1, x, g_mix, tm, token)
    comm.send("d", {"norm_mix_g": d_gmix, "ssd_conv_w": d_cw[0:4], "ssd_conv_b": d_cb, "ssd_dt_bias": d_dtb[:, :16],
                    "ssd_a_log": d_al[:, :16], "ssd_d": d_ds[:, :16], "ssd_norm_g": d_ng})
    return dx


def kernel(x, p, norm_mix_g, w_in, gm_v_norm_g, gm_ws, gm_bs, gm_out_norm_g, ssd_conv_w, ssd_conv_b, ssd_dt_bias, ssd_a_log, ssd_d, ssd_norm_g, w_out, norm_mlp_g, w_ff1, w_ff2, ple_norm_g, w_ple_gate, w_ple_proj, final_norm_g, loss_target, m_norm_mix_g, m_w_in, m_gm_v_norm_g, m_gm_ws, m_gm_bs, m_gm_out_norm_g, m_ssd_conv_w, m_ssd_conv_b, m_ssd_dt_bias, m_ssd_a_log, m_ssd_d, m_ssd_norm_g, m_w_out, m_norm_mlp_g, m_w_ff1, m_w_ff2, m_ple_norm_g, m_w_ple_gate, m_w_ple_proj, m_final_norm_g, v_norm_mix_g, v_w_in, v_gm_v_norm_g, v_gm_ws, v_gm_bs, v_gm_out_norm_g, v_ssd_conv_w, v_ssd_conv_b, v_ssd_dt_bias, v_ssd_a_log, v_ssd_d, v_ssd_norm_g, v_w_out, v_norm_mlp_g, v_w_ff1, v_w_ff2, v_ple_norm_g, v_w_ple_gate, v_w_ple_proj, v_final_norm_g):
    a = dict(locals())
    order = ["norm_mix_g", "w_in", "gm_v_norm_g", "gm_ws", "gm_bs", "gm_out_norm_g", "ssd_conv_w", "ssd_conv_b",
             "ssd_dt_bias", "ssd_a_log", "ssd_d", "ssd_norm_g", "w_out", "norm_mlp_g", "w_ff1", "w_ff2", "ple_norm_g",
             "w_ple_gate", "w_ple_proj", "final_norm_g"]
    chip = 2 * lax.axis_index("x") + lax.axis_index("y")
    nb, S = x.shape[0], x.shape[1]
    T = nb * S
    sm = {n: a[n] for n, _ in _SMALL if n != "ssd_conv_w"}
    comm = _Comm(a, jnp.stack([chip, lax.axis_index("c")]).astype(jnp.int32))
    dx = _local_step(x.reshape(T, D), p.reshape(T, DPLE), loss_target.reshape(T, D), sm, comm, nb, 512)
    big, g_out = comm.finish(dx)
    delta, new_m, new_v = {}, {}, {}
    for n, _, _ in _BIG:
        g_out[n], delta[n], new_m[n], new_v[n] = (r.reshape(a[n].shape) for r in big[n])
    g_out["ssd_conv_w"] = lax.dynamic_slice(g_out["ssd_conv_w"], (0, 0, chip * 384), (1, 4, 384))
    small_names = [n for n, _ in _SMALL]
    packs = [jnp.concatenate([_rows128(src(n)) for n in small_names], axis=0)
             for src in (lambda n: a[n], lambda n: g_out[n], lambda n: a["m_" + n], lambda n: a["v_" + n])]
    outs = _adamw(*packs, "adamw_small")
    o = 0
    for n in small_names:
        r = _rows128(a[n]).shape[0]
        cnt = a[n].size
        for dst, src in zip((delta, new_m, new_v), outs):
            dst[n] = src[o:o + r].reshape(-1)[:cnt].reshape(a[n].shape)
        o += r
    return (g_out["loss"], dx.reshape(x.shape), *[g_out[n] for n in order], *[delta[n] for n in order],
            *[new_m[n] for n in order], *[new_v[n] for n in order])
```

```python
import jax
import jax.numpy as jnp
from jax import lax
from jax.experimental import pallas as pl
from jax.experimental.pallas import tpu as pltpu

F32 = jnp.float32
BF16 = jnp.bfloat16
MXU = jnp.bfloat16
GRAD = jnp.bfloat16

D = 1024
CH = 128
GM_HEADS = 8
SSD_HEADS = 16
SSD_P = 64
CONV_CH = 1536
N_MAIN = 4608
DFF = 4096
DPLE = 256
EPS = 1e-6
NEG = -1e30

LR, B1, B2, ADAM_EPS, WD, STEP = 0.001, 0.9, 0.999, 1e-08, 0.01, 10

VMEM_LIMIT = 56 * 1024 * 1024
_SEQS_PER_STEP = 4
MESH = pl.DeviceIdType.MESH

INV_SQRT2 = 0.7071067811865476
INV_SQRT_2PI = 0.3989422804014327


def _cp(n_axes=1):
    return pltpu.CompilerParams(dimension_semantics=("arbitrary",) * n_axes, vmem_limit_bytes=VMEM_LIMIT)


def _dot(a, b):
    return jnp.dot(a, b, preferred_element_type=F32)


def _dot_nt(a, b):
    return lax.dot_general(a, b, (((1,), (1,)), ((), ())), preferred_element_type=F32)


def _dot_tn(a, b):
    return lax.dot_general(a, b, (((0,), (0,)), ((), ())), preferred_element_type=F32)


def _dot_hi(a, b):
    return jnp.dot(a, b, preferred_element_type=F32, precision=lax.Precision.HIGHEST)


def _dot_01(a, sel):
    hi = a.astype(BF16)
    lo = (a - hi.astype(F32)).astype(BF16)
    n = a.shape[0]
    r = _dot(jnp.concatenate([hi, lo], axis=0), sel)
    return r[0:n] + r[n:2 * n]


def _rows(tm, n, j=0):
    return pl.BlockSpec((tm, n), lambda i: (i, j))


def _const(shape):
    nd = len(shape)
    return pl.BlockSpec(shape, lambda *_: (0,) * nd)


def _sds(shape, dtype):
    return jax.ShapeDtypeStruct(shape, dtype)


def _rms(x):
    r = lax.rsqrt(jnp.mean(x * x, axis=-1, keepdims=True) + EPS)
    return x * r, r


def _rms_bwd(dy, xhat, r, g):
    dyg = dy * g
    return r * (dyg - xhat * jnp.mean(dyg * xhat, axis=-1, keepdims=True))


def _sigmoid(x):
    return 1.0 / (1.0 + jnp.exp(-x))


def _gelu(x):
    cdf = 0.5 * (1.0 + lax.erf(x * INV_SQRT2))
    pdf = jnp.exp(-0.5 * x * x) * INV_SQRT_2PI
    return x * cdf, cdf + x * pdf


def _softplus(x):
    e = jnp.exp(-jnp.abs(x))
    u = 1.0 + e
    log1p = jnp.where(u == 1.0, e, jnp.log(u) * e / (u - 1.0))
    return jnp.maximum(x, 0.0) + log1p


def _after(n_in, fn):
    def body(*refs):
        return fn(*refs[:n_in], *refs[n_in + 1:])

    return body


def _inproj_gmlp(x, g, wm, wdt, gv, ws, bst, gout, tm, after):
    T = x.shape[0]

    def body(x_ref, g_ref, wm_ref, wdt_ref, gv_ref, ws_ref, bst_ref, gout_ref,
             z_ref, xbc_ref, dt_ref, xn_ref, ya_ref, uv_ref):
        xh, _ = _rms(x_ref[...])
        xn = (xh * g_ref[...]).astype(MXU)
        xn_ref[...] = xn
        for n in range(4):
            uv_ref[:, n * 512:(n + 1) * 512] = _dot(xn, wm_ref[:, n * 512:(n + 1) * 512])
        for n in range(2):
            z_ref[:, n * 512:(n + 1) * 512] = _dot(xn, wm_ref[:, 2048 + n * 512:2048 + (n + 1) * 512])
        for n in range(3):
            xbc_ref[:, n * 512:(n + 1) * 512] = _dot(xn, wm_ref[:, 3072 + n * 512:3072 + (n + 1) * 512])
        dt_ref[...] = _dot(xn, wdt_ref[...])
        for k in range(tm // CH):
            rows = slice(k * CH, (k + 1) * CH)
            f = _gmlp_fwd_vals(uv_ref[rows, 0:1024], uv_ref[rows, 1024:2048], gv_ref[...], ws_ref, bst_ref[...],
                               gout_ref[...])
            ya_ref[rows, :] = f["out"].astype(MXU)

    return pl.pallas_call(
        _after(8, body), grid=(T // tm,), name="inproj_gmlp",
        in_specs=[_rows(tm, D), _const((1, D)), _const((D, N_MAIN)), _const((D, 128)), _const((1, 1024)),
                  _const((GM_HEADS, CH, CH)), _const((CH, 128)), _const((1, 1024)), _ANY],
        out_specs=[_rows(tm, 1024), _rows(tm, CONV_CH), _rows(tm, 128), _rows(tm, D), _rows(tm, 1024, 0),
                   _rows(tm, 2048)],
        out_shape=[_sds((T, 1024), F32), _sds((T, CONV_CH), F32), _sds((T, 128), F32), _sds((T, D), MXU),
                   _sds((T, 2048), MXU), _sds((T, 2048), F32)],
        compiler_params=_cp(),
    )(x, g, wm, wdt, gv, ws, bst, gout, after)


def _gmlp_fwd_vals(u, v, gv, ws_ref, bst, gout):
    ug, dug = _gelu(u)
    vg, dvg = _gelu(v)
    row = lax.broadcasted_iota(jnp.int32, (CH, CH), 0)
    col = lax.broadcasted_iota(jnp.int32, (CH, CH), 1)
    tril = row >= col
    ys, heads = [], []
    for h in range(GM_HEADS):
        sl = slice(h * 128, (h + 1) * 128)
        vhat, rv = _rms(vg[:, sl])
        vn = (vhat * gv[:, sl]).astype(MXU)
        wt = jnp.where(tril, ws_ref[h], 0.0)
        mixed = _dot(wt.astype(MXU), vn) + bst[:, h:h + 1]
        ys.append(ug[:, sl] * mixed)
        heads.append((vhat, rv, vn, wt, mixed))
    y = jnp.concatenate(ys, axis=1)
    yhat, ry = _rms(y)
    return dict(ug=ug, dug=dug, dvg=dvg, heads=heads, yhat=yhat, ry=ry, tril=tril, out=yhat * gout)


def _shifts_down(cur, halo):
    row8 = lax.broadcasted_iota(jnp.int32, (8, cur.shape[1]), 0)
    out = [cur]
    for j in (1, 2, 3):
        sh = pltpu.roll(cur, j, 0)
        top = jnp.where(row8 < j, pltpu.roll(halo, j, 0), sh[0:8])
        out.append(jnp.concatenate([top, sh[8:]], axis=0))
    return out


def _shifts_up(cur, halo):
    row8 = lax.broadcasted_iota(jnp.int32, (8, cur.shape[1]), 0)
    out = []
    for j in (1, 2, 3):
        sh = pltpu.roll(cur, CH - j, 0)
        bot = jnp.where(row8 + j >= 8, pltpu.roll(halo, 8 - j, 0), sh[CH - 8:CH])
        out.append(jnp.concatenate([sh[0:CH - 8], bot], axis=0))
    return out


def _conv(xbc, halo, convw, convb):
    sh = _shifts_down(xbc, halo)
    return convb + convw[3:4] * sh[0] + convw[2:3] * sh[1] + convw[1:2] * sh[2] + convw[0:1] * sh[3]


def _ssd_fwd_vals(z, conv, dtraw, dtb, alog, dskip, ng, ex, ltri, s_prev):
    sig_c = _sigmoid(conv)
    xa = conv * sig_c
    xs = xa[:, :1024]
    bm = [xa[:, 1024:1152], xa[:, 1152:1280]]
    cm = [xa[:, 1280:1408], xa[:, 1408:1536]]
    dtpre = dtraw + dtb
    dt = _softplus(dtpre)
    a_neg = -jnp.exp(alog)
    cs = _dot_hi(ltri, dt * a_neg)
    cst = cs.T
    last = cs[CH - 1:CH]
    ecs = jnp.exp(cs)
    dec = jnp.exp(last - cs)
    spread = _dot_01(jnp.concatenate([dt, ecs, dec], axis=0), ex)
    dte, ecse, dece = spread[0:CH], spread[CH:2 * CH], spread[2 * CH:3 * CH]
    cde = ecse[CH - 1:CH]
    de = dskip
    xdt = xs * dte
    row = lax.broadcasted_iota(jnp.int32, (CH, CH), 0)
    col = lax.broadcasted_iota(jnp.int32, (CH, CH), 1)
    tril = row >= col
    lo = col < SSD_P
    bmb = [b.astype(MXU) for b in bm]
    cmb = [c.astype(MXU) for c in cm]
    mg = [_dot_nt(cmb[g], bmb[g]) for g in range(2)]
    yd, lms, whs = [], [], []
    for q in range(8):
        g = q // 4
        xq = xdt[:, q * 128:(q + 1) * 128]
        acc = None
        for hh in range(2):
            h = 2 * q + hh
            seg = cs[:, h:h + 1] - cst[h:h + 1, :]
            lm = jnp.exp(jnp.where(tril, seg, NEG))
            wh = (mg[g] * lm).astype(MXU)
            xm = jnp.where(lo if hh == 0 else ~lo, xq, 0.0).astype(MXU)
            part = _dot(wh, xm)
            acc = part if acc is None else acc + part
            lms.append(lm)
            whs.append(wh)
        yd.append(acc)
    yd = jnp.concatenate(yd, axis=1)
    sb = s_prev.astype(MXU)
    yo = jnp.concatenate([_dot(cmb[g], sb[:, g * 512:(g + 1) * 512]) for g in range(2)], axis=1) * ecse
    xdec = (xdt * dece).astype(MXU)
    states = jnp.concatenate([_dot_tn(bmb[g], xdec[:, g * 512:(g + 1) * 512]) for g in range(2)], axis=1)
    s_next = s_prev * cde + states
    ypre = yd + yo + de * xs
    sig_z = _sigmoid(z)
    yg = ypre * z * sig_z
    outs, yhat, rr = [], [], []
    for g in range(2):
        sl = slice(g * 512, (g + 1) * 512)
        yh, r = _rms(yg[:, sl])
        yhat.append(yh)
        rr.append(r)
        outs.append(yh * ng[:, sl])
    return dict(sig_c=sig_c, xa=xa, xs=xs, bmb=bmb, cmb=cmb, dtpre=dtpre, dt=dt, a_neg=a_neg,
                cs=cs, last=last, ecs=ecs, dec=dec, dte=dte, ecse=ecse, dece=dece, cde=cde, de=de, xdt=xdt,
                mg=mg, lms=lms, whs=whs, lo=lo, yo=yo, sb=sb, xdec=xdec, s_next=s_next, ypre=ypre, sig_z=sig_z,
                yhat=yhat, rr=rr, out=jnp.concatenate(outs, axis=1))


def _ssd_fwd(pz, pxbc, dtraw, cat, convw, convb, dtb, alog, dskip, ng, ex, ltri, nb):
    T = pz.shape[0]
    S = T // nb
    nch = S // CH
    ns = _SEQS_PER_STEP if nb % _SEQS_PER_STEP == 0 else 1

    def body(z_ref, xbc_ref, halo_ref, dt_ref, cw_ref, cb_ref, dtb_ref, al_ref, ds_ref, ng_ref, ex_ref, lt_ref,
             cat_in_ref, yb_ref, sall_ref, conv_ref, s_ref):
        del cat_in_ref
        c = pl.program_id(1)

        @pl.when(c == 0)
        def _():
            s_ref[...] = jnp.zeros_like(s_ref)

        for i in range(ns):
            halo = jnp.where(c == 0, 0.0, halo_ref[i])
            s_prev = s_ref[i]
            sall_ref[i, 0] = s_prev
            conv = _conv(xbc_ref[i], halo, cw_ref[...], cb_ref[...])
            conv_ref[i] = conv
            f = _ssd_fwd_vals(z_ref[i], conv, dt_ref[i], dtb_ref[...], al_ref[...], ds_ref[...], ng_ref[...],
                              ex_ref[...], lt_ref[...], s_prev)
            s_ref[i] = f["s_next"]
            yb_ref[i] = f["out"].astype(MXU)

    def seq(width, col=0):
        return pl.BlockSpec((ns, CH, width), lambda b, c: (b, c, col))

    cat, sall, conv = pl.pallas_call(
        body, grid=(nb // ns, nch), name="ssd_fwd",
        in_specs=[seq(1024), seq(CONV_CH),
                  pl.BlockSpec((ns, 8, CONV_CH), lambda b, c: (b, jnp.maximum(c * (CH // 8) - 1, 0), 0)),
                  seq(128),
                  _const((8, CONV_CH)), _const((1, CONV_CH)), _const((1, 128)), _const((1, 128)), _const((1, 1024)),
                  _const((1, 1024)), _const((128, 1024)), _const((CH, CH)), _ANY],
        out_specs=[seq(1024, 1), pl.BlockSpec((ns, 1, 128, 1024), lambda b, c: (b, c, 0, 0)), seq(CONV_CH)],
        out_shape=[_sds((nb, S, 2048), MXU), _sds((nb, nch, 128, 1024), F32), _sds((nb, S, CONV_CH), F32)],
        scratch_shapes=[pltpu.VMEM((ns, 128, 1024), F32)],
        input_output_aliases={12: 0},
        compiler_params=_cp(2),
    )(pz.reshape(nb, S, 1024), pxbc.reshape(nb, S, CONV_CH), pxbc.reshape(nb, S, CONV_CH), dtraw.reshape(nb, S, 128),
      convw, convb, dtb, alog, dskip, ng, ex, ltri, cat.reshape(nb, S, 2048))
    return cat.reshape(T, 2048), sall, conv.reshape(T, CONV_CH)


def _outproj(cat, wo, x, g, tm):
    T = x.shape[0]

    def body(cat_ref, wo_ref, x_ref, g_ref, h1_ref, hn_ref):
        h1 = x_ref[...] + _dot(cat_ref[...], wo_ref[...])
        h1_ref[...] = h1
        hn_ref[...] = (_rms(h1)[0] * g_ref[...]).astype(MXU)

    return pl.pallas_call(
        body, grid=(T // tm,), name="outproj",
        in_specs=[_rows(tm, 2048), _const((2048, D)), _rows(tm, D), _const((1, D))],
        out_specs=[_rows(tm, D), _rows(tm, D)],
        out_shape=[_sds((T, D), F32), _sds((T, D), MXU)],
        compiler_params=_cp(),
    )(cat, wo, x, g)


def _ff1(hn, w1, tm):
    T = hn.shape[0]

    def body(hn_ref, w1_ref, hid_ref):
        hn_v = hn_ref[...]
        for n in range(4):
            hid_ref[:, n * 1024:(n + 1) * 1024] = jnp.maximum(_dot(hn_v, w1_ref[n]), 0.0).astype(MXU)

    return pl.pallas_call(
        body, grid=(T // tm,), name="ff1",
        in_specs=[_rows(tm, D), _const((4, D, 1024))],
        out_specs=_rows(tm, DFF),
        out_shape=_sds((T, DFF), MXU),
        compiler_params=_cp(),
    )(hn, w1)


def _sq(hid):
    h = hid.astype(F32)
    return (h * h).astype(MXU)


def _ff2_tail(hid, w2, h1, g_ple, p, tgt, wg, wp, gf, tm):
    T = h1.shape[0]

    def body(hid_ref, w2_ref, h1_ref, g_ref, p_ref, t_ref, wg_ref, wp_ref, gf_ref,
             hp_ref, dgl_ref, dpe_ref, dh2_ref, dh2b_ref, loss_ref, dgf_ref, dg_ref):
        @pl.when(pl.program_id(0) == 0)
        def _():
            loss_ref[...] = jnp.zeros_like(loss_ref)
            dgf_ref[...] = jnp.zeros_like(dgf_ref)
            dg_ref[...] = jnp.zeros_like(dg_ref)

        h2 = h1_ref[...] + _dot(_sq(hid_ref[...]), w2_ref[...])
        h2h, r2 = _rms(h2)
        g_ple = g_ref[...]
        hp = (h2h * g_ple).astype(MXU)
        hp_ref[...] = hp
        gate = _sigmoid(_dot(hp, wg_ref[...]))
        pb = p_ref[...].astype(MXU)
        pe = jnp.concatenate([_dot(pb, wp_ref[k]) for k in range(4)], axis=1)
        h3 = h2 + gate * pe
        hh, r = _rms(h3)
        gf = gf_ref[...]
        diff = hh * gf - t_ref[...]
        loss_ref[...] += 0.5 * jnp.sum(jnp.mean(diff * diff, axis=-1, keepdims=True))
        dout = diff * (1.0 / D)
        dgf_ref[...] += jnp.sum(dout * hh, axis=0, keepdims=True)
        dh3 = _rms_bwd(dout, hh, r, gf)
        dgl = (dh3 * pe * gate * (1.0 - gate)).astype(MXU)
        dgl_ref[...] = dgl
        dpe_ref[...] = (dh3 * gate).astype(MXU)
        dhp = _dot_nt(dgl, wg_ref[...])
        dg_ref[...] += jnp.sum(dhp * h2h, axis=0, keepdims=True)
        dh2 = dh3 + _rms_bwd(dhp, h2h, r2, g_ple)
        dh2_ref[...] = dh2
        dh2b_ref[...] = dh2.astype(MXU)

    return pl.pallas_call(
        body, grid=(T // tm,), name="ff2_tail",
        in_specs=[_rows(tm, DFF), _const((DFF, D)), _rows(tm, D), _const((1, D)), _rows(tm, DPLE), _rows(tm, D),
                  _const((D, D)), _const((4, DPLE, 256)), _const((1, D))],
        out_specs=[_rows(tm, D), _rows(tm, D), _rows(tm, D), _rows(tm, D), _rows(tm, D), _const((8, 128)),
                   _const((1, D)), _const((1, D))],
        out_shape=[_sds((T, D), MXU), _sds((T, D), MXU), _sds((T, D), MXU), _sds((T, D), F32), _sds((T, D), MXU),
                   _sds((8, 128), F32), _sds((1, D), F32), _sds((1, D), F32)],
        compiler_params=_cp(),
    )(hid, w2, h1, g_ple, p, tgt, wg, wp, gf)


def _ff2_bwd(dh2b, w2, hid, tm):
    T = hid.shape[0]

    def body(dh2b_ref, w2_ref, hid_ref, dpre_ref):
        d = dh2b_ref[...]
        for n in range(DFF // 1024):
            sl = slice(n * 1024, (n + 1) * 1024)
            da = _dot_nt(d, w2_ref[sl, :])
            dpre_ref[:, sl] = (2.0 * da * hid_ref[:, sl].astype(F32)).astype(MXU)

    return pl.pallas_call(
        body, grid=(T // tm,), name="ff2_bwd",
        in_specs=[_rows(tm, D), _const((DFF, D)), _rows(tm, DFF)],
        out_specs=_rows(tm, DFF),
        out_shape=_sds((T, DFF), MXU),
        compiler_params=_cp(),
    )(dh2b, w2, hid)


def _ff1_bwd(dpre, w1, dh2, h1, g, tm, after):
    T = h1.shape[0]

    def body(dpre_ref, w1_ref, dh2_ref, h1_ref, g_ref, dh1_ref, dh1b_ref, dg_ref):
        @pl.when(pl.program_id(0) == 0)
        def _():
            dg_ref[...] = jnp.zeros_like(dg_ref)

        dhn = _dot_nt(dpre_ref[:, 0:1024], w1_ref[0])
        for k in range(1, 4):
            dhn = dhn + _dot_nt(dpre_ref[:, k * 1024:(k + 1) * 1024], w1_ref[k])
        hh, r = _rms(h1_ref[...])
        dg_ref[...] += jnp.sum(dhn * hh, axis=0, keepdims=True)
        dh1 = dh2_ref[...] + _rms_bwd(dhn, hh, r, g_ref[...])
        dh1_ref[...] = dh1
        dh1b_ref[...] = dh1.astype(MXU)

    return pl.pallas_call(
        _after(5, body), grid=(T // tm,), name="ff1_bwd",
        in_specs=[_rows(tm, DFF), _const((4, D, 1024)), _rows(tm, D), _rows(tm, D), _const((1, D)), _ANY],
        out_specs=[_rows(tm, D), _rows(tm, D), _const((1, D))],
        out_shape=[_sds((T, D), F32), _sds((T, D), MXU), _sds((1, D), F32)],
        compiler_params=_cp(),
    )(dpre, w1, dh2, h1, g, after)


def _outproj_bwd(dh1b, wo, tm):
    T = dh1b.shape[0]

    def body(d_ref, wo_ref, dcat_ref):
        d = d_ref[...]
        dcat_ref[:, 0:1024] = _dot_nt(d, wo_ref[0:1024, :])
        dcat_ref[:, 1024:2048] = _dot_nt(d, wo_ref[1024:2048, :])

    return pl.pallas_call(
        body, grid=(T // tm,), name="outproj_bwd",
        in_specs=[_rows(tm, D), _const((2048, D))],
        out_specs=_rows(tm, 2048),
        out_shape=_sds((T, 2048), F32),
        compiler_params=_cp(),
    )(dh1b, wo)


def _gmlp_bwd(uv, dcat, gv, ws, bst, gout, wm):
    T = uv.shape[0]
    nck = 4 if T % (4 * CH) == 0 else 1
    tb = nck * CH

    def body(uv_ref, dya_ref, gv_ref, ws_ref, bst_ref, gout_ref, wuv_ref, duv_ref, dgv_ref, dws_ref, dbst_ref,
             dgo_ref, dxn_ref):
        @pl.when(pl.program_id(0) == 0)
        def _():
            dgv_ref[...] = jnp.zeros_like(dgv_ref)
            dws_ref[...] = jnp.zeros_like(dws_ref)
            dbst_ref[...] = jnp.zeros_like(dbst_ref)
            dgo_ref[...] = jnp.zeros_like(dgo_ref)

        for k in range(nck):
            chunk(slice(k * CH, (k + 1) * CH), uv_ref, dya_ref, gv_ref, ws_ref, bst_ref, gout_ref, duv_ref,
                  dgv_ref, dws_ref, dbst_ref, dgo_ref)
        dxn_ref[...] = _dot_nt(duv_ref[...], wuv_ref[...])

    def chunk(rows, uv_ref, dya_ref, gv_ref, ws_ref, bst_ref, gout_ref, duv_ref, dgv_ref, dws_ref, dbst_ref,
              dgo_ref):
        gv = gv_ref[...]
        f = _gmlp_fwd_vals(uv_ref[rows, 0:1024], uv_ref[rows, 1024:2048], gv, ws_ref, bst_ref[...], gout_ref[...])
        dya = dya_ref[rows, :]
        dgo_ref[...] += jnp.sum(dya * f["yhat"], axis=0, keepdims=True)
        dy = _rms_bwd(dya, f["yhat"], f["ry"], gout_ref[...])
        lane = lax.broadcasted_iota(jnp.int32, (CH, 128), 1)
        dbs = jnp.zeros((CH, 128), F32)
        dug, dvg, dgvs = [], [], []
        for h in range(GM_HEADS):
            sl = slice(h * 128, (h + 1) * 128)
            vhat, rv, vn, wt, mixed = f["heads"][h]
            dyh = dy[:, sl]
            dug.append(dyh * mixed)
            dmixed = dyh * f["ug"][:, sl]
            dmb = dmixed.astype(MXU)
            dws_ref[h] += jnp.where(f["tril"], _dot_nt(dmb, vn), 0.0)
            dbs = dbs + jnp.where(lane == h, jnp.sum(dmixed, axis=1, keepdims=True), 0.0)
            dvn = _dot_tn(wt.astype(MXU), dmb)
            dgvs.append(jnp.sum(dvn * vhat, axis=0, keepdims=True))
            dvg.append(_rms_bwd(dvn, vhat, rv, gv[:, sl]))
        dbst_ref[...] += dbs
        dgv_ref[...] += jnp.concatenate(dgvs, axis=1)
        duv_ref[rows, 0:1024] = (jnp.concatenate(dug, axis=1) * f["dug"]).astype(MXU)
        duv_ref[rows, 1024:2048] = (jnp.concatenate(dvg, axis=1) * f["dvg"]).astype(MXU)

    return pl.pallas_call(
        body, grid=(T // tb,), name="gmlp_bwd",
        in_specs=[_rows(tb, 2048), _rows(tb, 1024, 0), _const((1, 1024)),
                  _const((GM_HEADS, CH, CH)), _const((CH, 128)), _const((1, 1024)), _const((D, 2048))],
        out_specs=[_rows(tb, 2048), _const((1, 1024)), _const((GM_HEADS, CH, CH)), _const((CH, 128)),
                   _const((1, 1024)), _rows(tb, D)],
        out_shape=[_sds((T, 2048), MXU), _sds((1, 1024), F32), _sds((GM_HEADS, CH, CH), F32), _sds((CH, 128), F32),
                   _sds((1, 1024), F32), _sds((T, D), F32)],
        compiler_params=_cp(),
    )(uv, dcat, gv, ws, bst, gout, wm)


def _ssd_bwd(pz, pxbc, conv, dtraw, sall, dcat, convw, dtb, alog, dskip, ng, ex, ltri, ext, nb, after):
    T = pz.shape[0]
    S = T // nb
    nch = S // CH
    ns = _SEQS_PER_STEP if nb % _SEQS_PER_STEP == 0 else 1

    def seq(width, col=0):
        return pl.BlockSpec((ns, CH, width), lambda b, c: (b, nch - 1 - c, col))

    in_specs = [
        seq(1024), seq(CONV_CH), seq(CONV_CH), seq(128),
        _const((8, CONV_CH)), _const((1, 128)), _const((1, 128)), _const((1, 1024)),
        _const((1, 1024)), _const((128, 1024)), _const((CH, CH)),
        _const((1024, 128)),
        pl.BlockSpec((ns, 1, 128, 1024), lambda b, c: (b, nch - 1 - c, 0, 0)),
        seq(1024, 1),
        _ANY,
    ]

    def body(z_ref, xbc_ref, conv_ref, dt_ref, cw_ref, dtb_ref, al_ref, ds_ref, ng_ref, ex_ref, lt_ref,
             ext_ref, sall_ref, dyb_ref,
             dssd_ref, ddt_ref, dcw_ref, dcb_ref, ddtb_ref, dal_ref, dds_ref, dng_ref,
             dst_ref, dnext_ref, ddse_ref):
        b = pl.program_id(0)
        c = pl.program_id(1)

        @pl.when((b == 0) & (c == 0))
        def _():
            for r in (dcw_ref, dcb_ref, ddtb_ref, dal_ref, dds_ref, dng_ref, ddse_ref):
                r[...] = jnp.zeros_like(r)

        @pl.when(c == 0)
        def _():
            dst_ref[...] = jnp.zeros_like(dst_ref)
            dnext_ref[...] = jnp.zeros_like(dnext_ref)

        ex = ex_ref[...]
        ext = ext_ref[...]
        cw = cw_ref[...]
        ng = ng_ref[...]
        for i in range(ns):
            one_chunk(i, ex, ext, cw, ng, z_ref, xbc_ref, conv_ref, dt_ref, dtb_ref, al_ref, ds_ref, lt_ref, sall_ref,
                      dyb_ref, dssd_ref, ddt_ref, dcw_ref, dcb_ref, ddtb_ref, dal_ref, dng_ref, dst_ref, dnext_ref,
                      ddse_ref)

        @pl.when((b == nb // ns - 1) & (c == nch - 1))
        def _():
            dds_ref[...] = _dot_01(jnp.broadcast_to(ddse_ref[...], (8, 1024)), ext)[0:1]

    def one_chunk(i, ex, ext, cw, ng, z_ref, xbc_ref, conv_ref, dt_ref, dtb_ref, al_ref, ds_ref, lt_ref, sall_ref,
                  dyb_ref, dssd_ref, ddt_ref, dcw_ref, dcb_ref, ddtb_ref, dal_ref, dng_ref, dst_ref, dnext_ref,
                  ddse_ref):
        z = z_ref[i]
        s_prev = sall_ref[i, 0]
        conv = conv_ref[i]
        f = _ssd_fwd_vals(z, conv, dt_ref[i], dtb_ref[...], al_ref[...], ds_ref[...], ng, ex, lt_ref[...], s_prev)
        xs, xdt, cs, dec, dt = f["xs"], f["xdt"], f["cs"], f["dec"], f["dt"]
        dyb = dyb_ref[i]
        dyg, dngs = [], []
        for g in range(2):
            sl = slice(g * 512, (g + 1) * 512)
            dngs.append(jnp.sum(dyb[:, sl] * f["yhat"][g], axis=0, keepdims=True))
            dyg.append(_rms_bwd(dyb[:, sl], f["yhat"][g], f["rr"][g], ng[:, sl]))
        dng_ref[...] += jnp.concatenate(dngs, axis=1)
        dyg = jnp.concatenate(dyg, axis=1)
        sig_z = f["sig_z"]
        silu_z = z * sig_z
        dy = dyg * silu_z
        dz = dyg * f["ypre"] * (sig_z + silu_z * (1.0 - sig_z))
        ddse_ref[...] += jnp.sum(dy * xs, axis=0, keepdims=True)
        dxs = dy * f["de"]
        dye = dy * f["ecse"]
        dyeb = dye.astype(MXU)
        dst = dst_ref[i]
        dstb = dst.astype(MXU)
        bmb, cmb, sb, xdec = f["bmb"], f["cmb"], f["sb"], f["xdec"]
        u = jnp.concatenate([_dot(bmb[g], dstb[:, g * 512:(g + 1) * 512]) for g in range(2)], axis=1)
        dxdt = [u[:, q * 128:(q + 1) * 128] * f["dece"][:, q * 128:(q + 1) * 128] for q in range(8)]
        per_head = _dot_01(jnp.concatenate(
            [dy * f["yo"], u * xdt, jnp.broadcast_to(jnp.sum(dst * s_prev, axis=0, keepdims=True), (8, 1024))],
            axis=0), ext)
        dcs = per_head[0:CH]
        t = per_head[CH:2 * CH] * dec
        dcd = per_head[2 * CH:2 * CH + 1]
        row = lax.broadcasted_iota(jnp.int32, (CH, 128), 0)
        lane = lax.broadcasted_iota(jnp.int32, (CH, 128), 1)
        cd = jnp.exp(f["last"])
        dcs = dcs - t + jnp.where(row == CH - 1, jnp.sum(t, axis=0, keepdims=True) + dcd * cd, 0.0)
        dcst = jnp.zeros((128, CH), F32)
        lo = f["lo"]
        dbm, dcm, ds_prev = [], [], []
        for g in range(2):
            sl = slice(g * 512, (g + 1) * 512)
            dmg = jnp.zeros((CH, CH), F32)
            for q in range(4 * g, 4 * g + 4):
                dyq = dy[:, q * 128:(q + 1) * 128]
                xq = xdt[:, q * 128:(q + 1) * 128].astype(MXU)
                for hh in range(2):
                    h = 2 * q + hh
                    m = lo if hh == 0 else ~lo
                    dym = jnp.where(m, dyq, 0.0).astype(MXU)
                    gh = _dot_nt(dym, xq)
                    gl = gh * f["lms"][h]
                    dmg = dmg + gl
                    qh = gl * f["mg"][g]
                    dcs = dcs + jnp.where(lane == h, jnp.sum(qh, axis=1, keepdims=True), 0.0)
                    dcst = dcst - jnp.where(row == h, jnp.sum(qh, axis=0, keepdims=True), 0.0)
                    dxdt[q] = dxdt[q] + _dot_tn(f["whs"][h], dym)
            dmgb = dmg.astype(MXU)
            dcm.append(_dot(dmgb, bmb[g]) + _dot_nt(dyeb[:, sl], sb[:, sl]))
            dbm.append(_dot_tn(dmgb, cmb[g]) + _dot_nt(xdec[:, sl], dstb[:, sl]))
            ds_prev.append(_dot_tn(cmb[g], dyeb[:, sl]))
        dst_ref[i] = jnp.concatenate(ds_prev, axis=1) + dst * f["cde"]
        dcs = dcs + dcst.T
        da = _dot_hi(lt_ref[...].T, dcs)
        dxdt = jnp.concatenate(dxdt, axis=1)
        a_neg = f["a_neg"]
        ddt = da * a_neg + _dot_01(dxdt * xs, ext)
        dal_ref[...] += jnp.sum(da * dt, axis=0, keepdims=True) * a_neg
        dxs = dxs + dxdt * f["dte"]
        ddtraw = jnp.where(lane < SSD_HEADS, ddt * _sigmoid(f["dtpre"]), 0.0)
        ddtb_ref[...] += jnp.sum(ddtraw, axis=0, keepdims=True)
        ddt_ref[i] = ddtraw.astype(MXU)
        dxa = jnp.concatenate([dxs, dbm[0], dbm[1], dcm[0], dcm[1]], axis=1)
        sig_c = f["sig_c"]
        dconv = dxa * (sig_c + f["xa"] * (1.0 - sig_c))
        dcb_ref[...] += jnp.sum(dconv, axis=0, keepdims=True)
        xbc = xbc_ref[i]
        dcw_ref[3:4, :] += jnp.sum(dconv * xbc, axis=0, keepdims=True)
        dxbc = cw[3:4] * dconv
        for j, up in zip((1, 2, 3), _shifts_up(dconv, dnext_ref[i])):
            dcw_ref[3 - j:4 - j, :] += jnp.sum(up * xbc, axis=0, keepdims=True)
            dxbc = dxbc + cw[3 - j:4 - j] * up
        dnext_ref[i] = dconv[0:8]
        dssd_ref[i, :, 0:1024] = dz.astype(MXU)
        dssd_ref[i, :, 1024:2560] = dxbc.astype(MXU)

    dssd, ddt, *small = pl.pallas_call(
        _after(14, body), grid=(nb // ns, nch), name="ssd_bwd",
        in_specs=in_specs,
        out_specs=[seq(2560), seq(128),
                   _const((8, CONV_CH)), _const((1, CONV_CH)), _const((1, 128)), _const((1, 128)), _const((1, 128)),
                   _const((1, 1024))],
        out_shape=[_sds((nb, S, 2560), MXU), _sds((nb, S, 128), MXU), _sds((8, CONV_CH), F32),
                   _sds((1, CONV_CH), F32), _sds((1, 128), F32), _sds((1, 128), F32), _sds((1, 128), F32),
                   _sds((1, 1024), F32)],
        scratch_shapes=[pltpu.VMEM((ns, 128, 1024), F32), pltpu.VMEM((ns, 8, CONV_CH), F32),
                        pltpu.VMEM((1, 1024), F32)],
        compiler_params=_cp(2),
    )(pz.reshape(nb, S, 1024), pxbc.reshape(nb, S, CONV_CH), conv.reshape(nb, S, CONV_CH), dtraw.reshape(nb, S, 128),
      convw, dtb, alog, dskip, ng, ex, ltri, ext, sall, dcat.reshape(nb, S, 2048), after)
    return (dssd.reshape(T, 2560), ddt.reshape(T, 128), *small)


def _inproj_bwd(dxn_uv, dssd, ddt, wm, wdt, dh1, x, g, tm, after):
    T = x.shape[0]

    def body(dxnuv_ref, dssd_ref, ddt_ref, wm_ref, wdt_ref, dh1_ref, x_ref, g_ref, dx_ref, dg_ref):
        @pl.when(pl.program_id(0) == 0)
        def _():
            dg_ref[...] = jnp.zeros_like(dg_ref)

        dxn = (dxnuv_ref[...] + _dot_nt(dssd_ref[...], wm_ref[:, 2048:N_MAIN])
               + _dot_nt(ddt_ref[...], wdt_ref[...]))
        xh, r = _rms(x_ref[...])
        dg_ref[...] += jnp.sum(dxn * xh, axis=0, keepdims=True)
        dx_ref[...] = dh1_ref[...] + _rms_bwd(dxn, xh, r, g_ref[...])

    return pl.pallas_call(
        _after(8, body), grid=(T // tm,), name="inproj_bwd",
        in_specs=[_rows(tm, D), _rows(tm, 2560), _rows(tm, 128), _const((D, N_MAIN)), _const((D, 128)),
                  _rows(tm, D), _rows(tm, D), _const((1, D)), _ANY],
        out_specs=[_rows(tm, D), _const((1, D))],
        out_shape=[_sds((T, D), F32), _sds((1, D), F32)],
        compiler_params=_cp(),
    )(dxn_uv, dssd, ddt, wm, wdt, dh1, x, g, after)


def _matmul_tn(a, b, name, a_fn=None):
    T, M = a.shape
    N = b.shape[1]
    tm = min(M, 1024)
    tn = 1280 if N == 2560 else min(N, 1024)
    tk = min(T, 2048)

    def body(a_ref, b_ref, o_ref, acc_ref):
        k = pl.program_id(2)

        @pl.when(k == 0)
        def _():
            acc_ref[...] = jnp.zeros_like(acc_ref)

        av = a_ref[...]
        if a_fn is not None:
            av = a_fn(av)
        acc_ref[...] += _dot_tn(av, b_ref[...])

        @pl.when(k == T // tk - 1)
        def _():
            o_ref[...] = acc_ref[...].astype(o_ref.dtype)

    return pl.pallas_call(
        body, grid=(M // tm, N // tn, T // tk), name=name,
        in_specs=[pl.BlockSpec((tk, tm), lambda i, j, k: (k, i)), pl.BlockSpec((tk, tn), lambda i, j, k: (k, j))],
        out_specs=pl.BlockSpec((tm, tn), lambda i, j, k: (i, j)),
        out_shape=_sds((M, N), GRAD),
        scratch_shapes=[pltpu.VMEM((tm, tn), F32)],
        compiler_params=_cp(3),
    )(a, b)


def _adamw_vals(w, g, m, v):
    m = B1 * m + (1.0 - B1) * g
    v = B2 * v + (1.0 - B2) * (g * g)
    m_hat = m / (1.0 - B1 ** STEP)
    v_hat = v / (1.0 - B2 ** STEP)
    return -LR * (m_hat / (jnp.sqrt(v_hat) + ADAM_EPS) + WD * w), m, v


_PARTS = 4


def _adamw_halves(items, name):
    n = len(items)

    def body(*refs):
        mine = (pl.program_id(0) // _PARTS) == lax.axis_index("c")
        for k in range(n):
            w_ref, own_ref, oth_ref, m_ref, v_ref = refs[5 * k:5 * k + 5]
            g_ref, d_ref, mo_ref, vo_ref = refs[5 * n + 4 * k:5 * n + 4 * k + 4]
            g = jnp.where(mine, own_ref[...], oth_ref[...])
            g_ref[...] = g
            d_ref[...], mo_ref[...], vo_ref[...] = _adamw_vals(w_ref[...], g, m_ref[...], v_ref[...])

    in_specs, out_specs, out_shape = [], [], []
    for w, *_ in items:
        R, C = w.shape
        full = _rows(R // (2 * _PARTS), C)
        part = pl.BlockSpec((R // (2 * _PARTS), C), lambda i: (i % _PARTS, 0))
        in_specs += [full, part, part, full, full]
        out_specs += [full] * 4
        out_shape += [_sds((R, C), F32)] * 4
    res = pl.pallas_call(
        body, grid=(2 * _PARTS,), name=name, in_specs=in_specs, out_specs=out_specs, out_shape=out_shape,
        compiler_params=_cp(),
    )(*[a for item in items for a in item])
    return [tuple(res[4 * k:4 * k + 4]) for k in range(n)]


_TJ = 128


def _adamw_transposed(w, own, other, m, v, name):
    C, _, R = w.shape

    def body(w_ref, own_ref, oth_ref, m_ref, v_ref, g_ref, d_ref, mo_ref, vo_ref):
        first = lax.axis_index("c") == 0
        g = jnp.concatenate([jnp.where(first, own_ref[...], oth_ref[...]),
                             jnp.where(first, oth_ref[...], own_ref[...])], axis=0).T
        d, mo, vo = _adamw_vals(w_ref[:, 0, :], g, m_ref[:, 0, :], v_ref[:, 0, :])
        for ref, val in ((g_ref, g), (d_ref, d), (mo_ref, mo), (vo_ref, vo)):
            ref[:, 0, :] = val

    cols = pl.BlockSpec((_TJ, 1, R), lambda j: (j, 0, 0))
    half = pl.BlockSpec((R // 2, _TJ), lambda j: (0, j))
    return pl.pallas_call(
        body, grid=(pl.cdiv(C, _TJ),), name=name,
        in_specs=[cols, half, half, cols, cols], out_specs=[cols] * 4, out_shape=[_sds((C, 1, R), F32)] * 4,
        compiler_params=_cp(),
    )(w, own, other, m, v)


def _lanes(rows):
    return jnp.concatenate([rows[i:i + 1, :] for i in range(rows.shape[0])], axis=1)


def _small_update(a, packs):
    names = [n for n, _ in _SMALL]
    where = {}
    for k, (pnames, rows, _) in enumerate(packs):
        o = 0
        for n, r in zip(pnames, rows):
            where[n] = (k, o, r)
            o += r
    view = {n: (1, 1024) for n in names}
    view.update(gm_ws=(1024, 128), gm_bs=(8, 128), ssd_conv_w=(4, 384), ssd_conv_b=(1, CONV_CH),
                ssd_dt_bias=(1, 16), ssd_a_log=(1, 16), ssd_d=(1, 16))
    npk = len(packs)

    def body(*refs):
        tots = []
        for k in range(npk):
            tot = refs[k][0]
            for d in range(1, 8):
                tot = tot + refs[k][d]
            tots.append(tot)
        ins, outs = refs[npk:npk + 3 * len(names)], refs[npk + 3 * len(names):]
        chip = 2 * lax.axis_index("x") + lax.axis_index("y")
        for i, n in enumerate(names):
            k, o, r = where[n]
            blk = tots[k][o:o + r, :]
            if n == "gm_ws":
                g = blk
            elif n == "gm_bs":
                g = blk[0:8]
            elif view[n] == (1, 16):
                g = blk[0:1, 0:16]
            elif n == "ssd_conv_w":
                taps = jnp.concatenate([_lanes(blk[12 * t:12 * t + 12]) for t in range(4)], axis=0)
                g = taps[:, 0:384]
                for c in range(1, 4):
                    g = jnp.where(chip == c, taps[:, 384 * c:384 * (c + 1)], g)
            else:
                g = _lanes(blk[0:view[n][1] // 128])
            d, mo, vo = _adamw_vals(ins[3 * i][...], g, ins[3 * i + 1][...], ins[3 * i + 2][...])
            for j, val in enumerate((g, d, mo, vo)):
                outs[4 * i + j][...] = val
        k, o, _ = where["loss"]
        outs[-1][...] = tots[k][o:o + 1, 0:1]

    ins = [a[pre + n].reshape(view[n]) for n in names for pre in ("", "m_", "v_")]
    res = pl.pallas_call(
        body, name="small_update",
        out_shape=[_sds(view[n], F32) for n in names for _ in range(4)] + [_sds((1, 1), F32)],
    )(*[slots for _, _, slots in packs], *ins)
    return {n: tuple(r.reshape(a[n].shape) for r in res[4 * i:4 * i + 4]) for i, n in enumerate(names)}, res[-1]


def _sum_slots(items, kh, name):
    n = len(items)
    in_specs, out_specs, out_shape = [], [], []
    for slots, src, kind, (R, C) in items:
        tr = R // (2 * _PARTS)
        if kind == "slab":
            src_spec = pl.BlockSpec((1, tr, C), lambda i, kh: (kh[0], kh[1] * _PARTS + i, 0))
        elif kind == "rows":
            src_spec = pl.BlockSpec((tr, C), lambda i, kh: (kh[0] * (2 * _PARTS) + kh[1] * _PARTS + i, 0))
        else:
            src_spec = pl.BlockSpec((tr, C), lambda i, kh: (kh[1] * _PARTS + i, kh[0]))
        in_specs += [pl.BlockSpec((8, tr, C), lambda i, kh: (0, i, 0)), src_spec]
        out_specs.append(pl.BlockSpec((tr, C), lambda i, kh: (i, 0)))
        out_shape.append(_sds((R // 2, C), F32))

    def body(kh_ref, *refs):
        me = 2 * kh_ref[0] + kh_ref[1]
        for k, (_, _, kind, _) in enumerate(items):
            s_ref, own_ref, o_ref = refs[2 * k], refs[2 * k + 1], refs[2 * n + k]
            acc = (own_ref[0] if kind == "slab" else own_ref[...]).astype(F32)
            for j in range(1, 8):
                acc = acc + s_ref[me ^ j].astype(F32)
            o_ref[...] = acc

    return pl.pallas_call(
        body, name=name,
        grid_spec=pltpu.PrefetchScalarGridSpec(
            num_scalar_prefetch=1, grid=(_PARTS,), in_specs=in_specs, out_specs=out_specs),
        out_shape=out_shape,
        compiler_params=_cp(),
    )(kh, *[a for slots, src, _, _ in items for a in (slots, src)])


def _assemble_w_in(slabs):
    tr = 256

    def body(s_ref, wm_ref, wdt_ref):
        full = jnp.concatenate([s_ref[k] for k in range(4)], axis=1)
        wm_ref[...] = full[:, :N_MAIN]
        wdt_ref[...] = jnp.concatenate([full[:, N_MAIN:], jnp.zeros((tr, 128 - 16), full.dtype)], axis=1)

    return pl.pallas_call(
        body, grid=(D // tr,), name="assemble_w_in",
        in_specs=[pl.BlockSpec((4, tr, 1156), lambda i: (0, i, 0))],
        out_specs=[_rows(tr, N_MAIN), _rows(tr, 128)],
        out_shape=[_sds((D, N_MAIN), slabs.dtype), _sds((D, 128), slabs.dtype)],
        compiler_params=_cp(),
    )(slabs)


def _split_dw_in(d_uv, d_ssd, d_dt):
    tr = 256

    def body(uv_ref, ssd_ref, dt_ref, o_ref):
        full = jnp.concatenate([uv_ref[...], ssd_ref[...], dt_ref[:, 0:16]], axis=1)
        for k in range(4):
            o_ref[k] = full[:, 1156 * k:1156 * (k + 1)]

    return pl.pallas_call(
        body, grid=(D // tr,), name="split_dw_in",
        in_specs=[_rows(tr, 2048), _rows(tr, 2560), _rows(tr, 128)],
        out_specs=pl.BlockSpec((4, tr, 1156), lambda i: (0, i, 0)),
        out_shape=_sds((4, D, 1156), d_uv.dtype),
        compiler_params=_cp(),
    )(d_uv, d_ssd, d_dt)


def _cast_w_in(w, kh):
    C, _, R = w.shape

    def body(kh_ref, w_ref, o_ref):
        o_ref[0] = w_ref[:, 0, :].T.astype(BF16)

    return pl.pallas_call(
        body, name="cast_w_in",
        grid_spec=pltpu.PrefetchScalarGridSpec(
            num_scalar_prefetch=1, grid=(pl.cdiv(C, _TJ),),
            in_specs=[pl.BlockSpec((_TJ, 1, R), lambda j, kh: (j, 0, 0))],
            out_specs=pl.BlockSpec((1, R, _TJ), lambda j, kh: (kh[0], 0, j))),
        out_shape=_sds((4, R, C), BF16),
        compiler_params=_cp(),
    )(kh, w)


def _cast_into_slot(ws, kh, name):
    n = len(ws)

    def body(kh_ref, *refs):
        for k in range(n):
            refs[n + k][0] = refs[k][...].astype(BF16)

    return pl.pallas_call(
        body, name=name,
        grid_spec=pltpu.PrefetchScalarGridSpec(
            num_scalar_prefetch=1, grid=(_PARTS,),
            in_specs=[pl.BlockSpec((w.shape[0] // _PARTS, w.shape[1]), lambda i, kh: (i, 0)) for w in ws],
            out_specs=[pl.BlockSpec((1, w.shape[0] // _PARTS, w.shape[1]), lambda i, kh: (kh[0], i, 0))
                       for w in ws]),
        out_shape=[_sds((4,) + w.shape, BF16) for w in ws],
        compiler_params=_cp(),
    )(kh, *ws)


_ANY = pl.BlockSpec(memory_space=pl.ANY)
_CHIP_FLIPS = [(1, 0), (0, 1), (1, 1)]
_DEVICE_FLIPS = [(fx, fy, fc) for fx in (0, 1) for fy in (0, 1) for fc in (0, 1)][1:]


def _half(h, rows):
    return pl.ds(pl.multiple_of(h * rows, rows), rows)


def _remote(src, dst, ssem, rsem, to):
    return pltpu.make_async_remote_copy(src_ref=src, dst_ref=dst, send_sem=ssem, recv_sem=rsem,
                                        device_id=to, device_id_type=MESH)


def _weight_gather(bufs, conv):
    n = len(bufs)

    def body(*refs):
        conv_ref, outs, conv_out = refs[n], refs[n + 1:2 * n + 1], refs[2 * n + 1]
        send_sems, recv_sems, fsend_sems, frecv_sems, csend_sems, crecv_sems, local_sem = refs[2 * n + 2:]
        x, y, c = lax.axis_index("x"), lax.axis_index("y"), lax.axis_index("c")
        me = 2 * x + y
        halves = [_half(c, r.shape[1] // 2) for r in outs]
        others = [_half(1 - c, r.shape[1] // 2) for r in outs]
        remote = _remote
        local = [pltpu.make_async_copy(conv_ref, conv_out.at[me], local_sem)]
        for cp in local:
            cp.start()
        sends = []
        for k, (fx, fy) in enumerate(_CHIP_FLIPS):
            peer = (x ^ fx, y ^ fy, c)
            for i in range(n):
                mine = outs[i].at[me, halves[i]]
                sends.append(remote(mine, mine, send_sems.at[k * n + i], recv_sems.at[k * n + i], peer))
            sends.append(remote(conv_ref, conv_out.at[me], csend_sems.at[k], crecv_sems.at[k], peer))
        for cp in sends:
            cp.start()
        sibling = (x, y, 1 - c)
        forwards = []
        for k, (fx, fy) in enumerate(_CHIP_FLIPS):
            peer = (x ^ fx, y ^ fy, c)
            src = 2 * (x ^ fx) + (y ^ fy)
            for i in range(n):
                landed = outs[i].at[src, halves[i]]
                remote(landed, landed, send_sems.at[k * n + i], recv_sems.at[k * n + i], peer).wait_recv()
                fw = remote(landed, landed, fsend_sems.at[k * n + i], frecv_sems.at[k * n + i], sibling)
                fw.start()
                forwards.append(fw)
            remote(conv_out.at[src], conv_out.at[src], csend_sems.at[k], crecv_sems.at[k], peer).wait_recv()
        for k, (fx, fy) in enumerate(_CHIP_FLIPS):
            src = 2 * (x ^ fx) + (y ^ fy)
            for i in range(n):
                theirs = outs[i].at[src, others[i]]
                remote(theirs, theirs, fsend_sems.at[k * n + i], frecv_sems.at[k * n + i], sibling).wait_recv()
        for cp in sends + forwards:
            cp.wait_send()
        for cp in local:
            cp.wait()

    dma = pltpu.SemaphoreType.DMA
    return pl.pallas_call(
        body, name="weight_gather",
        in_specs=[_ANY] * (n + 1), out_specs=[_ANY] * (n + 1),
        out_shape=[_sds(b.shape, b.dtype) for b in bufs] + [_sds((4,) + conv.shape, conv.dtype)],
        input_output_aliases={i: i for i in range(n)},
        scratch_shapes=[dma((3 * n,)), dma((3 * n,)), dma((3 * n,)), dma((3 * n,)), dma((3,)), dma((3,)), dma],
    )(*bufs, conv)


def _piece(ref, kind, R, C, k, h):
    if kind == "slab":
        return ref.at[k, _half(h, R // 2), :]
    if kind == "rows":
        return ref.at[pl.ds(pl.multiple_of(k * R + h * (R // 2), R // 2), R // 2), :]
    return ref.at[_half(h, R // 2), pl.ds(pl.multiple_of(k * C, C), C)]


_HBM = pl.BlockSpec(memory_space=pltpu.HBM)
_SEM = pl.BlockSpec(memory_space=pltpu.SEMAPHORE)


def _split_start(name, arrays, n_copies, plan, after=None):
    n = len(arrays)
    extra = [] if after is None else [after]

    def body(*refs):
        m = n + len(extra)
        arrs, send_sems, recv_sems, token = refs[:n], refs[m], refs[m + 1], refs[-1]
        for j, (src, dst, peer) in enumerate(plan(arrs)):
            _remote(src, dst, send_sems.at[j], recv_sems.at[j], peer).start()
        token[...] = jnp.zeros_like(token)

    dma = pltpu.SemaphoreType.DMA
    res = pl.pallas_call(
        body, name=name,
        out_shape=(dma((n_copies,)), dma((n_copies,)), *[pltpu.HBM(a.shape, a.dtype) for a in arrays],
                   _sds((8, 128), F32)),
        in_specs=[_HBM] * n + [_ANY] * len(extra),
        out_specs=(_SEM, _SEM, *[_HBM] * n, pl.BlockSpec(memory_space=pltpu.VMEM)),
        input_output_aliases={i: 2 + i for i in range(n)},
        compiler_params=pltpu.CompilerParams(has_side_effects=pltpu.SideEffectType.DATAFLOW_SIDE_EFFECTING),
    )(*[pltpu.with_memory_space_constraint(a, pltpu.HBM) for a in arrays], *extra)
    return res[0], res[1], list(res[2:2 + n]), res[-1]


def _split_wait(name, arrays, send_sems, recv_sems, plan, after):
    n = len(arrays)

    def body(*refs):
        arrs, ssems, rsems = refs[:n], refs[n], refs[n + 1]
        for j, (src, dst, peer) in enumerate(plan(arrs)):
            cp = _remote(src, dst, ssems.at[j], rsems.at[j], peer)
            cp.wait_send()
            cp.wait_recv()

    return list(pl.pallas_call(
        body, name=name,
        out_shape=tuple(pltpu.HBM(a.shape, a.dtype) for a in arrays),
        in_specs=[_HBM] * n + [_SEM, _SEM, _ANY],
        out_specs=tuple([_HBM] * n),
        input_output_aliases={i: i for i in range(n)},
        compiler_params=pltpu.CompilerParams(has_side_effects=pltpu.SideEffectType.DATAFLOW_SIDE_EFFECTING),
    )(*arrays, send_sems, recv_sems, after))


def _gather_plan(n):
    def plan(bufs):
        x, y, c = lax.axis_index("x"), lax.axis_index("y"), lax.axis_index("c")
        me = 2 * x + y
        return [(bufs[i].at[me], bufs[i].at[me], (x ^ fx, y ^ fy, c)) for fx, fy in _CHIP_FLIPS for i in range(n)]

    return plan


def _reduce_plan(specs, n_small):
    n = len(specs)

    def plan(arrs):
        x, y, c = lax.axis_index("x"), lax.axis_index("y"), lax.axis_index("c")
        slot = 4 * x + 2 * y + c
        out = []
        for fx, fy, fc in _DEVICE_FLIPS:
            peer = (x ^ fx, y ^ fy, c ^ fc)
            for i, (kind, (R, C)) in enumerate(specs):
                out.append((_piece(arrs[i], kind, R, C, 2 * peer[0] + peer[1], peer[2]), arrs[n + i].at[slot], peer))
            for s in range(n_small):
                out.append((arrs[2 * n + 2 * s], arrs[2 * n + 2 * s + 1].at[slot], peer))
        return out

    return plan


def _sibling_exchange(halves, name, small=None):
    n = len(halves)
    ns = 0 if small is None else 1

    def body(*refs):
        ins, outs = refs[:n], refs[n + ns:2 * n + ns]
        send_sems, recv_sems = refs[2 * (n + ns)], refs[2 * (n + ns) + 1]
        x, y, c = lax.axis_index("x"), lax.axis_index("y"), lax.axis_index("c")
        copies = [_remote(ins[i], outs[i], send_sems.at[i], recv_sems.at[i], (x, y, 1 - c)) for i in range(n)]
        waits = list(copies)
        if ns:
            s_ref, slots_ref, ssend_sems, srecv_sems, local_sem = refs[n], refs[2 * n + 1], *refs[2 * (n + ns) + 2:]
            slot = 4 * x + 2 * y + c
            own = pltpu.make_async_copy(s_ref, slots_ref.at[slot], local_sem)
            own.start()
            for k, (fx, fy, fc) in enumerate(_DEVICE_FLIPS):
                peer = (x ^ fx, y ^ fy, c ^ fc)
                copies.append(_remote(s_ref, slots_ref.at[slot], ssend_sems.at[k], srecv_sems.at[k], peer))
                theirs = slots_ref.at[slot ^ (k + 1)]
                waits.append(_remote(theirs, theirs, ssend_sems.at[k], srecv_sems.at[k], peer))
        for cp in copies:
            cp.start()
        for cp in waits:
            cp.wait()
        if ns:
            own.wait()

    dma = pltpu.SemaphoreType.DMA
    extra_in = [] if small is None else [small]
    extra_out = [] if small is None else [_sds((8,) + small.shape, F32)]
    return pl.pallas_call(
        body, name=name,
        in_specs=[_ANY] * (n + ns), out_specs=[_ANY] * (n + ns),
        out_shape=[_sds(h.shape, h.dtype) for h in halves] + extra_out,
        scratch_shapes=[dma((n,)), dma((n,))] + ([dma((7,)), dma((7,)), dma] if ns else []),
    )(*halves, *extra_in)


_BIG = [("w_in", (1024, 1156), "slab"), ("w_out", (512, 1024), "rows"), ("w_ff1", (1024, 1024), "cols"),
        ("w_ff2", (1024, 1024), "rows"), ("w_ple_gate", (256, 1024), "rows"), ("w_ple_proj", (256, 256), "cols")]
_SMALL = [("norm_mix_g", (1, 1024)), ("gm_v_norm_g", (1, 1024)), ("gm_ws", (1, 8, 128, 128)), ("gm_bs", (1, 8, 128)),
          ("gm_out_norm_g", (1, 1024)), ("ssd_conv_w", (1, 4, 1536)), ("ssd_conv_b", (1, 1536)),
          ("ssd_dt_bias", (1, 16)), ("ssd_a_log", (1, 16)), ("ssd_d", (1, 16)), ("ssd_norm_g", (1, 1024)),
          ("norm_mlp_g", (1, 1024)), ("ple_norm_g", (1, 1024)), ("final_norm_g", (1024,))]


def _rows128(a):
    flat = a.reshape(-1)
    rows = -(-flat.shape[0] // 1024) * 8
    return jnp.pad(flat, (0, rows * 128 - flat.shape[0])).reshape(rows, 128)


def _pad_lanes(v, n=128):
    v = v.reshape(1, -1)
    return jnp.pad(v, ((0, 0), (0, n - v.shape[1])))


_SMALL_SHAPES = dict(_SMALL + [("loss", ())])
_BIG_SPECS = {n: (kind, shp) for n, shp, kind in _BIG}


class _Comm:
    def __init__(self, a, kh):
        self.a, self.kh = a, kh
        rest = _BIG[1:]
        self.bufs = {"w_in": _cast_w_in(a["w_in"].transpose(2, 0, 1), kh)}
        cast = _cast_into_slot([a[n].reshape(shp) for n, shp, _ in rest], kh, "cast_rest")
        self.bufs.update({n: c for (n, _, _), c in zip(rest, cast)})
        self.sent = []
        self.small_packs = []

    def w_in(self):
        g_win, g_cw = _weight_gather([self.bufs["w_in"]], self.a["ssd_conv_w"].reshape(4, 384))
        token = g_cw
        self.gather = {}
        for tag, names in (("out", ["w_out"]), ("ff", ["w_ff1", "w_ff2", "w_ple_gate", "w_ple_proj"])):
            plan = _gather_plan(len(names))
            ssem, rsem, thru, token = _split_start("gather_start_" + tag, [self.bufs[n] for n in names],
                                                   3 * len(names), plan, after=token)
            self.gather[tag] = (plan, ssem, rsem, thru)
        wm, wdt = _assemble_w_in(g_win)
        return wm, wdt, jnp.concatenate([g_cw[k] for k in range(4)], axis=1), token

    def rest(self, tag, after):
        plan, ssem, rsem, thru = self.gather[tag]
        got = _split_wait("gather_wait_" + tag, thru, ssem, rsem, plan, after)
        if tag == "out":
            return got[0].reshape(2048, D)
        g_w1, g_w2, g_wg, g_wp = got
        return g_w1, g_w2.reshape(DFF, D), g_wg.reshape(D, D), g_wp

    def send(self, tag, grads):
        big = [n for n, _, _ in _BIG if n in grads]
        small = [n for n in _SMALL_SHAPES if n in grads]
        parts = [_rows128(grads[n]) for n in small]
        rows = [s.shape[0] for s in parts]
        if not big:
            self.last_small = (tag, small, rows, jnp.concatenate(parts, axis=0))
            return None
        srcs = [grads[n] for n in big]
        lands = [lax.empty((8, _BIG_SPECS[n][1][0] // 2, _BIG_SPECS[n][1][1]), GRAD) for n in big]
        extra = []
        if small:
            pack = jnp.concatenate(parts, axis=0)
            extra = [pack, jnp.broadcast_to(pack, (8,) + pack.shape)]
        plan = _reduce_plan([_BIG_SPECS[n] for n in big], len(extra) // 2)
        n_copies = 7 * (len(big) + len(extra) // 2)
        ssem, rsem, thru, token = _split_start("reduce_start_" + tag, srcs + lands + extra, n_copies, plan)
        self.sent.append((tag, big, small, rows, plan, ssem, rsem, thru))
        return token

    def finish(self, after):
        a, results = self.a, {}

        def update(names, own, tag):
            if names == ["w_in"]:
                stag, small, rows, pack = self.last_small
                *other, slots = _sibling_exchange([own[n] for n in names], "sibling_exchange_" + tag, pack)
                self.small_packs.append((small, rows, slots))
                w, m, v = (a[k].transpose(2, 0, 1) for k in ("w_in", "m_w_in", "v_w_in"))
                raw = _adamw_transposed(w, own["w_in"], other[0], m, v, "adamw_" + tag)
                results["w_in"] = tuple(r.transpose(1, 2, 0) for r in raw)
                return raw[1]
            other = _sibling_exchange([own[n] for n in names], "sibling_exchange_" + tag)
            items = [(a[n].reshape(_BIG_SPECS[n][1]), own[n], oth, a["m_" + n].reshape(_BIG_SPECS[n][1]),
                      a["v_" + n].reshape(_BIG_SPECS[n][1])) for n, oth in zip(names, other)]
            results.update(zip(names, _adamw_halves(items, "adamw_" + tag)))
            return results[names[-1]][1]

        own, early = {}, []
        for tag, big, small, rows, plan, ssem, rsem, thru in self.sent:
            if tag == self.sent[-1][0]:
                after = update(early, own, "early")
            arrs = _split_wait("reduce_wait_" + tag, thru, ssem, rsem, plan, after)
            nb_ = len(big)
            sums = _sum_slots([(arrs[nb_ + i], arrs[i]) + _BIG_SPECS[n] for i, n in enumerate(big)], self.kh,
                              "sum_" + tag)
            own.update(zip(big, sums))
            after = sums[-1]
            early += big
            if small:
                self.small_packs.append((small, rows, arrs[2 * nb_ + 1]))
        update(self.sent[-1][1], own, "late")
        return results, self.small_packs


def _local_step(x, p, tgt, sm, comm, nb, tm):
    T = x.shape[0]
    wm, wdt, conv_w, token = comm.w_in()
    g_mix, gv, gout = sm["norm_mix_g"].reshape(1, D), sm["gm_v_norm_g"].reshape(1, D), sm["gm_out_norm_g"].reshape(1, D)
    ws = sm["gm_ws"].reshape(GM_HEADS, CH, CH)
    bst = jnp.pad(sm["gm_bs"].reshape(GM_HEADS, CH).T, ((0, 0), (0, 128 - GM_HEADS)))
    convw = jnp.pad(conv_w, ((0, 4), (0, 0)))
    convb = sm["ssd_conv_b"].reshape(1, CONV_CH)
    dtb, alog = _pad_lanes(sm["ssd_dt_bias"]), _pad_lanes(sm["ssd_a_log"])
    dskip = jnp.repeat(sm["ssd_d"].reshape(SSD_HEADS), SSD_P).reshape(1, 1024)
    ng, g_mlp, g_ple = sm["ssd_norm_g"].reshape(1, D), sm["norm_mlp_g"].reshape(1, D), sm["ple_norm_g"].reshape(1, D)
    gf = sm["final_norm_g"].reshape(1, D)
    head_of_lane = lax.broadcasted_iota(jnp.int32, (128, 1024), 1) // SSD_P
    ex = (lax.broadcasted_iota(jnp.int32, (128, 1024), 0) == head_of_lane).astype(BF16)
    ext = ex.T
    ltri = (lax.broadcasted_iota(jnp.int32, (CH, CH), 0) >= lax.broadcasted_iota(jnp.int32, (CH, CH), 1)).astype(F32)

    pz, pxbc, dtraw, xn, cat, uv = _inproj_gmlp(x, g_mix, wm, wdt, gv, ws, bst, gout, tm, token)
    cat, sall, conv = _ssd_fwd(pz, pxbc, dtraw, cat, convw, convb, dtb, alog, dskip, ng, ex, ltri, nb)
    wo = comm.rest("out", cat)
    h1, hn = _outproj(cat, wo, x, g_mlp, tm)
    w1, w2, wg, wp = comm.rest("ff", hn)
    hid = _ff1(hn, w1, min(T, 2 * tm))
    hp, dgl, dpe, dh2, dh2b, loss, d_gf, d_gple = _ff2_tail(hid, w2, h1, g_ple, p, tgt, wg, wp, gf, tm)

    d_wp = _matmul_tn(p, dpe, "dw_ple_proj", a_fn=lambda a: a.astype(MXU))
    d_wg = _matmul_tn(hp, dgl, "dw_ple_gate")
    d_w2 = _matmul_tn(hid, dh2b, "dw_ff2", a_fn=_sq)
    dpre = _ff2_bwd(dh2b, w2, hid, min(T, 2 * tm))
    d_w1 = _matmul_tn(hn, dpre, "dw_ff1")
    token = comm.send("a", {"w_ple_proj": d_wp, "w_ple_gate": d_wg, "w_ff2": d_w2, "w_ff1": d_w1})
    dh1, dh1b, d_gmlp = _ff1_bwd(dpre, w1, dh2, h1, g_mlp, tm, token)
    dcat = _outproj_bwd(dh1b, wo, min(T, 2 * tm))
    d_wo = _matmul_tn(cat, dh1b, "dw_out")
    duv, d_gv, d_ws, d_bst, d_gout, dxn_uv = _gmlp_bwd(uv, dcat, gv, ws, bst, gout, wm)
    token = comm.send("b", {
        "w_out": d_wo, "loss": loss[0:1, 0:1], "final_norm_g": d_gf, "ple_norm_g": d_gple, "norm_mlp_g": d_gmlp,
        "gm_v_norm_g": d_gv, "gm_ws": d_ws, "gm_bs": d_bst[:, :GM_HEADS].T, "gm_out_norm_g": d_gout})
    dssd, ddt, d_cw, d_cb, d_dtb, d_al, d_ds, d_ng = _ssd_bwd(
        pz, pxbc, conv, dtraw, sall, dcat, convw, dtb, alog, dskip, ng, ex, ltri, ext, nb, token)
    d_win = _split_dw_in(_matmul_tn(xn, duv, "dw_in_uv"), _matmul_tn(xn, dssd, "dw_in_ssd"),
                         _matmul_tn(xn, ddt, "dw_in_dt"))
    token = comm.send("c", {"w_in": d_win})
    dx, d_gmix = _inproj_bwd(dxn_uv, dssd, ddt, wm, wdt, dh1, x, g_mix, tm, token)
    comm.send("d", {"norm_mix_g": d_gmix, "ssd_conv_w": d_cw[0:4], "ssd_conv_b": d_cb, "ssd_dt_bias": d_dtb[:, :16],
                    "ssd_a_log": d_al[:, :16], "ssd_d": d_ds[:, :16], "ssd_norm_g": d_ng})
    return dx


def kernel(x, p, norm_mix_g, w_in, gm_v_norm_g, gm_ws, gm_bs, gm_out_norm_g, ssd_conv_w, ssd_conv_b, ssd_dt_bias, ssd_a_log, ssd_d, ssd_norm_g, w_out, norm_mlp_g, w_ff1, w_ff2, ple_norm_g, w_ple_gate, w_ple_proj, final_norm_g, loss_target, m_norm_mix_g, m_w_in, m_gm_v_norm_g, m_gm_ws, m_gm_bs, m_gm_out_norm_g, m_ssd_conv_w, m_ssd_conv_b, m_ssd_dt_bias, m_ssd_a_log, m_ssd_d, m_ssd_norm_g, m_w_out, m_norm_mlp_g, m_w_ff1, m_w_ff2, m_ple_norm_g, m_w_ple_gate, m_w_ple_proj, m_final_norm_g, v_norm_mix_g, v_w_in, v_gm_v_norm_g, v_gm_ws, v_gm_bs, v_gm_out_norm_g, v_ssd_conv_w, v_ssd_conv_b, v_ssd_dt_bias, v_ssd_a_log, v_ssd_d, v_ssd_norm_g, v_w_out, v_norm_mlp_g, v_w_ff1, v_w_ff2, v_ple_norm_g, v_w_ple_gate, v_w_ple_proj, v_final_norm_g):
    a = dict(locals())
    order = ["norm_mix_g", "w_in", "gm_v_norm_g", "gm_ws", "gm_bs", "gm_out_norm_g", "ssd_conv_w", "ssd_conv_b",
             "ssd_dt_bias", "ssd_a_log", "ssd_d", "ssd_norm_g", "w_out", "norm_mlp_g", "w_ff1", "w_ff2", "ple_norm_g",
             "w_ple_gate", "w_ple_proj", "final_norm_g"]
    chip = 2 * lax.axis_index("x") + lax.axis_index("y")
    nb, S = x.shape[0], x.shape[1]
    T = nb * S
    sm = {n: a[n] for n, _ in _SMALL if n != "ssd_conv_w"}
    comm = _Comm(a, jnp.stack([chip, lax.axis_index("c")]).astype(jnp.int32))
    dx = _local_step(x.reshape(T, D), p.reshape(T, DPLE), loss_target.reshape(T, D), sm, comm, nb, 512)
    big, small_packs = comm.finish(dx)
    small, loss = _small_update(a, small_packs)
    g_out, delta, new_m, new_v = {}, {}, {}, {}
    for n in order:
        g_out[n], delta[n], new_m[n], new_v[n] = (r.reshape(a[n].shape) for r in (big[n] if n in big else small[n]))
    return (loss.reshape(()), dx.reshape(x.shape), *[g_out[n] for n in order], *[delta[n] for n in order],
            *[new_m[n] for n in order], *[new_v[n] for n in order])
```

```python
import jax
import jax.numpy as jnp
from jax import lax
from jax.experimental import pallas as pl
from jax.experimental.pallas import tpu as pltpu

F32 = jnp.float32
BF16 = jnp.bfloat16
MXU = jnp.bfloat16
GRAD = jnp.bfloat16

D = 1024
CH = 128
GM_HEADS = 8
SSD_HEADS = 16
SSD_P = 64
CONV_CH = 1536
N_MAIN = 4608
DFF = 4096
DPLE = 256
EPS = 1e-6
NEG = -1e30

LR, B1, B2, ADAM_EPS, WD, STEP = 0.001, 0.9, 0.999, 1e-08, 0.01, 10

VMEM_LIMIT = 56 * 1024 * 1024
_SEQS_PER_STEP = 4
MESH = pl.DeviceIdType.MESH

INV_SQRT2 = 0.7071067811865476
INV_SQRT_2PI = 0.3989422804014327


def _cp(n_axes=1):
    return pltpu.CompilerParams(dimension_semantics=("arbitrary",) * n_axes, vmem_limit_bytes=VMEM_LIMIT)


def _dot(a, b):
    return jnp.dot(a, b, preferred_element_type=F32)


def _dot_nt(a, b):
    return lax.dot_general(a, b, (((1,), (1,)), ((), ())), preferred_element_type=F32)


def _dot_tn(a, b):
    return lax.dot_general(a, b, (((0,), (0,)), ((), ())), preferred_element_type=F32)


def _dot_hi(a, b):
    return jnp.dot(a, b, preferred_element_type=F32, precision=lax.Precision.HIGHEST)


def _dot_01(a, sel):
    hi = a.astype(BF16)
    lo = (a - hi.astype(F32)).astype(BF16)
    n = a.shape[0]
    r = _dot(jnp.concatenate([hi, lo], axis=0), sel)
    return r[0:n] + r[n:2 * n]


def _rows(tm, n, j=0):
    return pl.BlockSpec((tm, n), lambda i: (i, j))


def _const(shape):
    nd = len(shape)
    return pl.BlockSpec(shape, lambda *_: (0,) * nd)


def _sds(shape, dtype):
    return jax.ShapeDtypeStruct(shape, dtype)


def _rms(x):
    r = lax.rsqrt(jnp.mean(x * x, axis=-1, keepdims=True) + EPS)
    return x * r, r


def _rms_bwd(dy, xhat, r, g):
    dyg = dy * g
    return r * (dyg - xhat * jnp.mean(dyg * xhat, axis=-1, keepdims=True))


def _sigmoid(x):
    return 1.0 / (1.0 + jnp.exp(-x))


def _gelu(x):
    cdf = 0.5 * (1.0 + lax.erf(x * INV_SQRT2))
    pdf = jnp.exp(-0.5 * x * x) * INV_SQRT_2PI
    return x * cdf, cdf + x * pdf


def _softplus(x):
    e = jnp.exp(-jnp.abs(x))
    u = 1.0 + e
    log1p = jnp.where(u == 1.0, e, jnp.log(u) * e / (u - 1.0))
    return jnp.maximum(x, 0.0) + log1p


def _after(n_in, fn):
    def body(*refs):
        return fn(*refs[:n_in], *refs[n_in + 1:])

    return body


def _inproj_gmlp(x, g, wm, wdt, gv, ws, bst, gout, tm, after):
    T = x.shape[0]

    def body(x_ref, g_ref, wm_ref, wdt_ref, gv_ref, ws_ref, bst_ref, gout_ref,
             z_ref, xbc_ref, dt_ref, xn_ref, ya_ref, uv_ref):
        xh, _ = _rms(x_ref[...])
        xn = (xh * g_ref[...]).astype(MXU)
        xn_ref[...] = xn
        for n in range(4):
            uv_ref[:, n * 512:(n + 1) * 512] = _dot(xn, wm_ref[:, n * 512:(n + 1) * 512])
        for n in range(2):
            z_ref[:, n * 512:(n + 1) * 512] = _dot(xn, wm_ref[:, 2048 + n * 512:2048 + (n + 1) * 512])
        for n in range(3):
            xbc_ref[:, n * 512:(n + 1) * 512] = _dot(xn, wm_ref[:, 3072 + n * 512:3072 + (n + 1) * 512])
        dt_ref[...] = _dot(xn, wdt_ref[...])
        for k in range(tm // CH):
            rows = slice(k * CH, (k + 1) * CH)
            f = _gmlp_fwd_vals(uv_ref[rows, 0:1024], uv_ref[rows, 1024:2048], gv_ref[...], ws_ref, bst_ref[...],
                               gout_ref[...])
            ya_ref[rows, :] = f["out"].astype(MXU)

    return pl.pallas_call(
        _after(8, body), grid=(T // tm,), name="inproj_gmlp",
        in_specs=[_rows(tm, D), _const((1, D)), _const((D, N_MAIN)), _const((D, 128)), _const((1, 1024)),
                  _const((GM_HEADS, CH, CH)), _const((CH, 128)), _const((1, 1024)), _ANY],
        out_specs=[_rows(tm, 1024), _rows(tm, CONV_CH), _rows(tm, 128), _rows(tm, D), _rows(tm, 1024, 0),
                   _rows(tm, 2048)],
        out_shape=[_sds((T, 1024), F32), _sds((T, CONV_CH), F32), _sds((T, 128), F32), _sds((T, D), MXU),
                   _sds((T, 2048), MXU), _sds((T, 2048), F32)],
        compiler_params=_cp(),
    )(x, g, wm, wdt, gv, ws, bst, gout, after)


def _gmlp_fwd_vals(u, v, gv, ws_ref, bst, gout):
    ug, dug = _gelu(u)
    vg, dvg = _gelu(v)
    row = lax.broadcasted_iota(jnp.int32, (CH, CH), 0)
    col = lax.broadcasted_iota(jnp.int32, (CH, CH), 1)
    tril = row >= col
    ys, heads = [], []
    for h in range(GM_HEADS):
        sl = slice(h * 128, (h + 1) * 128)
        vhat, rv = _rms(vg[:, sl])
        vn = (vhat * gv[:, sl]).astype(MXU)
        wt = jnp.where(tril, ws_ref[h], 0.0)
        mixed = _dot(wt.astype(MXU), vn) + bst[:, h:h + 1]
        ys.append(ug[:, sl] * mixed)
        heads.append((vhat, rv, vn, wt, mixed))
    y = jnp.concatenate(ys, axis=1)
    yhat, ry = _rms(y)
    return dict(ug=ug, dug=dug, dvg=dvg, heads=heads, yhat=yhat, ry=ry, tril=tril, out=yhat * gout)


def _shifts_down(cur, halo):
    row8 = lax.broadcasted_iota(jnp.int32, (8, cur.shape[1]), 0)
    out = [cur]
    for j in (1, 2, 3):
        sh = pltpu.roll(cur, j, 0)
        top = jnp.where(row8 < j, pltpu.roll(halo, j, 0), sh[0:8])
        out.append(jnp.concatenate([top, sh[8:]], axis=0))
    return out


def _shifts_up(cur, halo):
    row8 = lax.broadcasted_iota(jnp.int32, (8, cur.shape[1]), 0)
    out = []
    for j in (1, 2, 3):
        sh = pltpu.roll(cur, CH - j, 0)
        bot = jnp.where(row8 + j >= 8, pltpu.roll(halo, 8 - j, 0), sh[CH - 8:CH])
        out.append(jnp.concatenate([sh[0:CH - 8], bot], axis=0))
    return out


def _conv(xbc, halo, convw, convb):
    sh = _shifts_down(xbc, halo)
    return convb + convw[3:4] * sh[0] + convw[2:3] * sh[1] + convw[1:2] * sh[2] + convw[0:1] * sh[3]


def _ssd_fwd_vals(z, conv, dtraw, dtb, alog, dskip, ng, ex, ltri, s_prev):
    sig_c = _sigmoid(conv)
    xa = conv * sig_c
    xs = xa[:, :1024]
    bm = [xa[:, 1024:1152], xa[:, 1152:1280]]
    cm = [xa[:, 1280:1408], xa[:, 1408:1536]]
    dtpre = dtraw + dtb
    dt = _softplus(dtpre)
    a_neg = -jnp.exp(alog)
    cs = _dot_hi(ltri, dt * a_neg)
    cst = cs.T
    last = cs[CH - 1:CH]
    ecs = jnp.exp(cs)
    dec = jnp.exp(last - cs)
    spread = _dot_01(jnp.concatenate([dt, ecs, dec], axis=0), ex)
    dte, ecse, dece = spread[0:CH], spread[CH:2 * CH], spread[2 * CH:3 * CH]
    cde = ecse[CH - 1:CH]
    de = dskip
    xdt = xs * dte
    row = lax.broadcasted_iota(jnp.int32, (CH, CH), 0)
    col = lax.broadcasted_iota(jnp.int32, (CH, CH), 1)
    tril = row >= col
    lo = col < SSD_P
    bmb = [b.astype(MXU) for b in bm]
    cmb = [c.astype(MXU) for c in cm]
    mg = [_dot_nt(cmb[g], bmb[g]) for g in range(2)]
    yd, lms, whs = [], [], []
    for q in range(8):
        g = q // 4
        xq = xdt[:, q * 128:(q + 1) * 128]
        acc = None
        for hh in range(2):
            h = 2 * q + hh
            seg = cs[:, h:h + 1] - cst[h:h + 1, :]
            lm = jnp.exp(jnp.where(tril, seg, NEG))
            wh = (mg[g] * lm).astype(MXU)
            xm = jnp.where(lo if hh == 0 else ~lo, xq, 0.0).astype(MXU)
            part = _dot(wh, xm)
            acc = part if acc is None else acc + part
            lms.append(lm)
            whs.append(wh)
        yd.append(acc)
    yd = jnp.concatenate(yd, axis=1)
    sb = s_prev.astype(MXU)
    yo = jnp.concatenate([_dot(cmb[g], sb[:, g * 512:(g + 1) * 512]) for g in range(2)], axis=1) * ecse
    xdec = (xdt * dece).astype(MXU)
    states = jnp.concatenate([_dot_tn(bmb[g], xdec[:, g * 512:(g + 1) * 512]) for g in range(2)], axis=1)
    s_next = s_prev * cde + states
    ypre = yd + yo + de * xs
    sig_z = _sigmoid(z)
    yg = ypre * z * sig_z
    outs, yhat, rr = [], [], []
    for g in range(2):
        sl = slice(g * 512, (g + 1) * 512)
        yh, r = _rms(yg[:, sl])
        yhat.append(yh)
        rr.append(r)
        outs.append(yh * ng[:, sl])
    return dict(sig_c=sig_c, xa=xa, xs=xs, bmb=bmb, cmb=cmb, dtpre=dtpre, dt=dt, a_neg=a_neg,
                cs=cs, last=last, ecs=ecs, dec=dec, dte=dte, ecse=ecse, dece=dece, cde=cde, de=de, xdt=xdt,
                mg=mg, lms=lms, whs=whs, lo=lo, yo=yo, sb=sb, xdec=xdec, s_next=s_next, ypre=ypre, sig_z=sig_z,
                yhat=yhat, rr=rr, out=jnp.concatenate(outs, axis=1))


def _ssd_fwd(pz, pxbc, dtraw, cat, convw, convb, dtb, alog, dskip, ng, ex, ltri, nb):
    T = pz.shape[0]
    S = T // nb
    nch = S // CH
    ns = _SEQS_PER_STEP if nb % _SEQS_PER_STEP == 0 else 1

    def body(z_ref, xbc_ref, halo_ref, dt_ref, cw_ref, cb_ref, dtb_ref, al_ref, ds_ref, ng_ref, ex_ref, lt_ref,
             cat_in_ref, yb_ref, sall_ref, conv_ref, s_ref):
        del cat_in_ref
        c = pl.program_id(1)

        @pl.when(c == 0)
        def _():
            s_ref[...] = jnp.zeros_like(s_ref)

        for i in range(ns):
            halo = jnp.where(c == 0, 0.0, halo_ref[i])
            s_prev = s_ref[i]
            sall_ref[i, 0] = s_prev
            conv = _conv(xbc_ref[i], halo, cw_ref[...], cb_ref[...])
            conv_ref[i] = conv
            f = _ssd_fwd_vals(z_ref[i], conv, dt_ref[i], dtb_ref[...], al_ref[...], ds_ref[...], ng_ref[...],
                              ex_ref[...], lt_ref[...], s_prev)
            s_ref[i] = f["s_next"]
            yb_ref[i] = f["out"].astype(MXU)

    def seq(width, col=0):
        return pl.BlockSpec((ns, CH, width), lambda b, c: (b, c, col))

    cat, sall, conv = pl.pallas_call(
        body, grid=(nb // ns, nch), name="ssd_fwd",
        in_specs=[seq(1024), seq(CONV_CH),
                  pl.BlockSpec((ns, 8, CONV_CH), lambda b, c: (b, jnp.maximum(c * (CH // 8) - 1, 0), 0)),
                  seq(128),
                  _const((8, CONV_CH)), _const((1, CONV_CH)), _const((1, 128)), _const((1, 128)), _const((1, 1024)),
                  _const((1, 1024)), _const((128, 1024)), _const((CH, CH)), _ANY],
        out_specs=[seq(1024, 1), pl.BlockSpec((ns, 1, 128, 1024), lambda b, c: (b, c, 0, 0)), seq(CONV_CH)],
        out_shape=[_sds((nb, S, 2048), MXU), _sds((nb, nch, 128, 1024), F32), _sds((nb, S, CONV_CH), F32)],
        scratch_shapes=[pltpu.VMEM((ns, 128, 1024), F32)],
        input_output_aliases={12: 0},
        compiler_params=_cp(2),
    )(pz.reshape(nb, S, 1024), pxbc.reshape(nb, S, CONV_CH), pxbc.reshape(nb, S, CONV_CH), dtraw.reshape(nb, S, 128),
      convw, convb, dtb, alog, dskip, ng, ex, ltri, cat.reshape(nb, S, 2048))
    return cat.reshape(T, 2048), sall, conv.reshape(T, CONV_CH)


def _outproj(cat, wo, x, g, tm):
    T = x.shape[0]

    def body(cat_ref, wo_ref, x_ref, g_ref, h1_ref, hn_ref):
        h1 = x_ref[...] + _dot(cat_ref[...], wo_ref[...])
        h1_ref[...] = h1
        hn_ref[...] = (_rms(h1)[0] * g_ref[...]).astype(MXU)

    return pl.pallas_call(
        body, grid=(T // tm,), name="outproj",
        in_specs=[_rows(tm, 2048), _const((2048, D)), _rows(tm, D), _const((1, D))],
        out_specs=[_rows(tm, D), _rows(tm, D)],
        out_shape=[_sds((T, D), F32), _sds((T, D), MXU)],
        compiler_params=_cp(),
    )(cat, wo, x, g)


def _ff1(hn, w1, tm):
    T = hn.shape[0]

    def body(hn_ref, w1_ref, hid_ref):
        hn_v = hn_ref[...]
        for n in range(4):
            hid_ref[:, n * 1024:(n + 1) * 1024] = jnp.maximum(_dot(hn_v, w1_ref[n]), 0.0).astype(MXU)

    return pl.pallas_call(
        body, grid=(T // tm,), name="ff1",
        in_specs=[_rows(tm, D), _const((4, D, 1024))],
        out_specs=_rows(tm, DFF),
        out_shape=_sds((T, DFF), MXU),
        compiler_params=_cp(),
    )(hn, w1)


def _sq(hid):
    h = hid.astype(F32)
    return (h * h).astype(MXU)


def _ff2_tail(hid, w2, h1, g_ple, p, tgt, wg, wp, gf, tm):
    T = h1.shape[0]

    def body(hid_ref, w2_ref, h1_ref, g_ref, p_ref, t_ref, wg_ref, wp_ref, gf_ref,
             hp_ref, dgl_ref, dpe_ref, dh2_ref, dh2b_ref, loss_ref, dgf_ref, dg_ref):
        @pl.when(pl.program_id(0) == 0)
        def _():
            loss_ref[...] = jnp.zeros_like(loss_ref)
            dgf_ref[...] = jnp.zeros_like(dgf_ref)
            dg_ref[...] = jnp.zeros_like(dg_ref)

        h2 = h1_ref[...] + _dot(_sq(hid_ref[...]), w2_ref[...])
        h2h, r2 = _rms(h2)
        g_ple = g_ref[...]
        hp = (h2h * g_ple).astype(MXU)
        hp_ref[...] = hp
        gate = _sigmoid(_dot(hp, wg_ref[...]))
        pb = p_ref[...].astype(MXU)
        pe = jnp.concatenate([_dot(pb, wp_ref[k]) for k in range(4)], axis=1)
        h3 = h2 + gate * pe
        hh, r = _rms(h3)
        gf = gf_ref[...]
        diff = hh * gf - t_ref[...]
        loss_ref[...] += 0.5 * jnp.sum(jnp.mean(diff * diff, axis=-1, keepdims=True))
        dout = diff * (1.0 / D)
        dgf_ref[...] += jnp.sum(dout * hh, axis=0, keepdims=True)
        dh3 = _rms_bwd(dout, hh, r, gf)
        dgl = (dh3 * pe * gate * (1.0 - gate)).astype(MXU)
        dgl_ref[...] = dgl
        dpe_ref[...] = (dh3 * gate).astype(MXU)
        dhp = _dot_nt(dgl, wg_ref[...])
        dg_ref[...] += jnp.sum(dhp * h2h, axis=0, keepdims=True)
        dh2 = dh3 + _rms_bwd(dhp, h2h, r2, g_ple)
        dh2_ref[...] = dh2
        dh2b_ref[...] = dh2.astype(MXU)

    return pl.pallas_call(
        body, grid=(T // tm,), name="ff2_tail",
        in_specs=[_rows(tm, DFF), _const((DFF, D)), _rows(tm, D), _const((1, D)), _rows(tm, DPLE), _rows(tm, D),
                  _const((D, D)), _const((4, DPLE, 256)), _const((1, D))],
        out_specs=[_rows(tm, D), _rows(tm, D), _rows(tm, D), _rows(tm, D), _rows(tm, D), _const((8, 128)),
                   _const((1, D)), _const((1, D))],
        out_shape=[_sds((T, D), MXU), _sds((T, D), MXU), _sds((T, D), MXU), _sds((T, D), F32), _sds((T, D), MXU),
                   _sds((8, 128), F32), _sds((1, D), F32), _sds((1, D), F32)],
        compiler_params=_cp(),
    )(hid, w2, h1, g_ple, p, tgt, wg, wp, gf)


def _ff2_bwd(dh2b, w2, hid, tm):
    T = hid.shape[0]

    def body(dh2b_ref, w2_ref, hid_ref, dpre_ref):
        d = dh2b_ref[...]
        for n in range(DFF // 1024):
            sl = slice(n * 1024, (n + 1) * 1024)
            da = _dot_nt(d, w2_ref[sl, :])
            dpre_ref[:, sl] = (2.0 * da * hid_ref[:, sl].astype(F32)).astype(MXU)

    return pl.pallas_call(
        body, grid=(T // tm,), name="ff2_bwd",
        in_specs=[_rows(tm, D), _const((DFF, D)), _rows(tm, DFF)],
        out_specs=_rows(tm, DFF),
        out_shape=_sds((T, DFF), MXU),
        compiler_params=_cp(),
    )(dh2b, w2, hid)


def _ff1_bwd(dpre, w1, dh2, h1, g, tm, after):
    T = h1.shape[0]

    def body(dpre_ref, w1_ref, dh2_ref, h1_ref, g_ref, dh1_ref, dh1b_ref, dg_ref):
        @pl.when(pl.program_id(0) == 0)
        def _():
            dg_ref[...] = jnp.zeros_like(dg_ref)

        dhn = _dot_nt(dpre_ref[:, 0:1024], w1_ref[0])
        for k in range(1, 4):
            dhn = dhn + _dot_nt(dpre_ref[:, k * 1024:(k + 1) * 1024], w1_ref[k])
        hh, r = _rms(h1_ref[...])
        dg_ref[...] += jnp.sum(dhn * hh, axis=0, keepdims=True)
        dh1 = dh2_ref[...] + _rms_bwd(dhn, hh, r, g_ref[...])
        dh1_ref[...] = dh1
        dh1b_ref[...] = dh1.astype(MXU)

    return pl.pallas_call(
        _after(5, body), grid=(T // tm,), name="ff1_bwd",
        in_specs=[_rows(tm, DFF), _const((4, D, 1024)), _rows(tm, D), _rows(tm, D), _const((1, D)), _ANY],
        out_specs=[_rows(tm, D), _rows(tm, D), _const((1, D))],
        out_shape=[_sds((T, D), F32), _sds((T, D), MXU), _sds((1, D), F32)],
        compiler_params=_cp(),
    )(dpre, w1, dh2, h1, g, after)


def _outproj_bwd(dh1b, wo, tm):
    T = dh1b.shape[0]

    def body(d_ref, wo_ref, dcat_ref):
        d = d_ref[...]
        dcat_ref[:, 0:1024] = _dot_nt(d, wo_ref[0:1024, :])
        dcat_ref[:, 1024:2048] = _dot_nt(d, wo_ref[1024:2048, :])

    return pl.pallas_call(
        body, grid=(T // tm,), name="outproj_bwd",
        in_specs=[_rows(tm, D), _const((2048, D))],
        out_specs=_rows(tm, 2048),
        out_shape=_sds((T, 2048), F32),
        compiler_params=_cp(),
    )(dh1b, wo)


def _gmlp_bwd(uv, dcat, gv, ws, bst, gout, wm):
    T = uv.shape[0]
    nck = 4 if T % (4 * CH) == 0 else 1
    tb = nck * CH

    def body(uv_ref, dya_ref, gv_ref, ws_ref, bst_ref, gout_ref, wuv_ref, duv_ref, dgv_ref, dws_ref, dbst_ref,
             dgo_ref, dxn_ref):
        @pl.when(pl.program_id(0) == 0)
        def _():
            dgv_ref[...] = jnp.zeros_like(dgv_ref)
            dws_ref[...] = jnp.zeros_like(dws_ref)
            dbst_ref[...] = jnp.zeros_like(dbst_ref)
            dgo_ref[...] = jnp.zeros_like(dgo_ref)

        for k in range(nck):
            chunk(slice(k * CH, (k + 1) * CH), uv_ref, dya_ref, gv_ref, ws_ref, bst_ref, gout_ref, duv_ref,
                  dgv_ref, dws_ref, dbst_ref, dgo_ref)
        dxn_ref[...] = _dot_nt(duv_ref[...], wuv_ref[...])

    def chunk(rows, uv_ref, dya_ref, gv_ref, ws_ref, bst_ref, gout_ref, duv_ref, dgv_ref, dws_ref, dbst_ref,
              dgo_ref):
        gv = gv_ref[...]
        f = _gmlp_fwd_vals(uv_ref[rows, 0:1024], uv_ref[rows, 1024:2048], gv, ws_ref, bst_ref[...], gout_ref[...])
        dya = dya_ref[rows, :]
        dgo_ref[...] += jnp.sum(dya * f["yhat"], axis=0, keepdims=True)
        dy = _rms_bwd(dya, f["yhat"], f["ry"], gout_ref[...])
        lane = lax.broadcasted_iota(jnp.int32, (CH, 128), 1)
        dbs = jnp.zeros((CH, 128), F32)
        dug, dvg, dgvs = [], [], []
        for h in range(GM_HEADS):
            sl = slice(h * 128, (h + 1) * 128)
            vhat, rv, vn, wt, mixed = f["heads"][h]
            dyh = dy[:, sl]
            dug.append(dyh * mixed)
            dmixed = dyh * f["ug"][:, sl]
            dmb = dmixed.astype(MXU)
            dws_ref[h] += jnp.where(f["tril"], _dot_nt(dmb, vn), 0.0)
            dbs = dbs + jnp.where(lane == h, jnp.sum(dmixed, axis=1, keepdims=True), 0.0)
            dvn = _dot_tn(wt.astype(MXU), dmb)
            dgvs.append(jnp.sum(dvn * vhat, axis=0, keepdims=True))
            dvg.append(_rms_bwd(dvn, vhat, rv, gv[:, sl]))
        dbst_ref[...] += dbs
        dgv_ref[...] += jnp.concatenate(dgvs, axis=1)
        duv_ref[rows, 0:1024] = (jnp.concatenate(dug, axis=1) * f["dug"]).astype(MXU)
        duv_ref[rows, 1024:2048] = (jnp.concatenate(dvg, axis=1) * f["dvg"]).astype(MXU)

    return pl.pallas_call(
        body, grid=(T // tb,), name="gmlp_bwd",
        in_specs=[_rows(tb, 2048), _rows(tb, 1024, 0), _const((1, 1024)),
                  _const((GM_HEADS, CH, CH)), _const((CH, 128)), _const((1, 1024)), _const((D, 2048))],
        out_specs=[_rows(tb, 2048), _const((1, 1024)), _const((GM_HEADS, CH, CH)), _const((CH, 128)),
                   _const((1, 1024)), _rows(tb, D)],
        out_shape=[_sds((T, 2048), MXU), _sds((1, 1024), F32), _sds((GM_HEADS, CH, CH), F32), _sds((CH, 128), F32),
                   _sds((1, 1024), F32), _sds((T, D), F32)],
        compiler_params=_cp(),
    )(uv, dcat, gv, ws, bst, gout, wm)


def _ssd_bwd(pz, pxbc, conv, dtraw, sall, dcat, convw, dtb, alog, dskip, ng, ex, ltri, ext, nb, after):
    T = pz.shape[0]
    S = T // nb
    nch = S // CH
    ns = _SEQS_PER_STEP if nb % _SEQS_PER_STEP == 0 else 1

    def seq(width, col=0):
        return pl.BlockSpec((ns, CH, width), lambda b, c: (b, nch - 1 - c, col))

    in_specs = [
        seq(1024), seq(CONV_CH), seq(CONV_CH), seq(128),
        _const((8, CONV_CH)), _const((1, 128)), _const((1, 128)), _const((1, 1024)),
        _const((1, 1024)), _const((128, 1024)), _const((CH, CH)),
        _const((1024, 128)),
        pl.BlockSpec((ns, 1, 128, 1024), lambda b, c: (b, nch - 1 - c, 0, 0)),
        seq(1024, 1),
        _ANY,
    ]

    def body(z_ref, xbc_ref, conv_ref, dt_ref, cw_ref, dtb_ref, al_ref, ds_ref, ng_ref, ex_ref, lt_ref,
             ext_ref, sall_ref, dyb_ref,
             dssd_ref, ddt_ref, dcw_ref, dcb_ref, ddtb_ref, dal_ref, dds_ref, dng_ref,
             dst_ref, dnext_ref, ddse_ref):
        b = pl.program_id(0)
        c = pl.program_id(1)

        @pl.when((b == 0) & (c == 0))
        def _():
            for r in (dcw_ref, dcb_ref, ddtb_ref, dal_ref, dds_ref, dng_ref, ddse_ref):
                r[...] = jnp.zeros_like(r)

        @pl.when(c == 0)
        def _():
            dst_ref[...] = jnp.zeros_like(dst_ref)
            dnext_ref[...] = jnp.zeros_like(dnext_ref)

        ex = ex_ref[...]
        ext = ext_ref[...]
        cw = cw_ref[...]
        ng = ng_ref[...]
        for i in range(ns):
            one_chunk(i, ex, ext, cw, ng, z_ref, xbc_ref, conv_ref, dt_ref, dtb_ref, al_ref, ds_ref, lt_ref, sall_ref,
                      dyb_ref, dssd_ref, ddt_ref, dcw_ref, dcb_ref, ddtb_ref, dal_ref, dng_ref, dst_ref, dnext_ref,
                      ddse_ref)

        @pl.when((b == nb // ns - 1) & (c == nch - 1))
        def _():
            dds_ref[...] = _dot_01(jnp.broadcast_to(ddse_ref[...], (8, 1024)), ext)[0:1]

    def one_chunk(i, ex, ext, cw, ng, z_ref, xbc_ref, conv_ref, dt_ref, dtb_ref, al_ref, ds_ref, lt_ref, sall_ref,
                  dyb_ref, dssd_ref, ddt_ref, dcw_ref, dcb_ref, ddtb_ref, dal_ref, dng_ref, dst_ref, dnext_ref,
                  ddse_ref):
        z = z_ref[i]
        s_prev = sall_ref[i, 0]
        conv = conv_ref[i]
        f = _ssd_fwd_vals(z, conv, dt_ref[i], dtb_ref[...], al_ref[...], ds_ref[...], ng, ex, lt_ref[...], s_prev)
        xs, xdt, cs, dec, dt = f["xs"], f["xdt"], f["cs"], f["dec"], f["dt"]
        dyb = dyb_ref[i]
        dyg, dngs = [], []
        for g in range(2):
            sl = slice(g * 512, (g + 1) * 512)
            dngs.append(jnp.sum(dyb[:, sl] * f["yhat"][g], axis=0, keepdims=True))
            dyg.append(_rms_bwd(dyb[:, sl], f["yhat"][g], f["rr"][g], ng[:, sl]))
        dng_ref[...] += jnp.concatenate(dngs, axis=1)
        dyg = jnp.concatenate(dyg, axis=1)
        sig_z = f["sig_z"]
        silu_z = z * sig_z
        dy = dyg * silu_z
        dz = dyg * f["ypre"] * (sig_z + silu_z * (1.0 - sig_z))
        ddse_ref[...] += jnp.sum(dy * xs, axis=0, keepdims=True)
        dxs = dy * f["de"]
        dye = dy * f["ecse"]
        dyeb = dye.astype(MXU)
        dst = dst_ref[i]
        dstb = dst.astype(MXU)
        bmb, cmb, sb, xdec = f["bmb"], f["cmb"], f["sb"], f["xdec"]
        u = jnp.concatenate([_dot(bmb[g], dstb[:, g * 512:(g + 1) * 512]) for g in range(2)], axis=1)
        dxdt = [u[:, q * 128:(q + 1) * 128] * f["dece"][:, q * 128:(q + 1) * 128] for q in range(8)]
        per_head = _dot_01(jnp.concatenate(
            [dy * f["yo"], u * xdt, jnp.broadcast_to(jnp.sum(dst * s_prev, axis=0, keepdims=True), (8, 1024))],
            axis=0), ext)
        dcs = per_head[0:CH]
        t = per_head[CH:2 * CH] * dec
        dcd = per_head[2 * CH:2 * CH + 1]
        row = lax.broadcasted_iota(jnp.int32, (CH, 128), 0)
        lane = lax.broadcasted_iota(jnp.int32, (CH, 128), 1)
        cd = jnp.exp(f["last"])
        dcs = dcs - t + jnp.where(row == CH - 1, jnp.sum(t, axis=0, keepdims=True) + dcd * cd, 0.0)
        dcst = jnp.zeros((128, CH), F32)
        lo = f["lo"]
        dbm, dcm, ds_prev = [], [], []
        for g in range(2):
            sl = slice(g * 512, (g + 1) * 512)
            dmg = jnp.zeros((CH, CH), F32)
            for q in range(4 * g, 4 * g + 4):
                dyq = dy[:, q * 128:(q + 1) * 128]
                xq = xdt[:, q * 128:(q + 1) * 128].astype(MXU)
                for hh in range(2):
                    h = 2 * q + hh
                    m = lo if hh == 0 else ~lo
                    dym = jnp.where(m, dyq, 0.0).astype(MXU)
                    gh = _dot_nt(dym, xq)
                    gl = gh * f["lms"][h]
                    dmg = dmg + gl
                    qh = gl * f["mg"][g]
                    dcs = dcs + jnp.where(lane == h, jnp.sum(qh, axis=1, keepdims=True), 0.0)
                    dcst = dcst - jnp.where(row == h, jnp.sum(qh, axis=0, keepdims=True), 0.0)
                    dxdt[q] = dxdt[q] + _dot_tn(f["whs"][h], dym)
            dmgb = dmg.astype(MXU)
            dcm.append(_dot(dmgb, bmb[g]) + _dot_nt(dyeb[:, sl], sb[:, sl]))
            dbm.append(_dot_tn(dmgb, cmb[g]) + _dot_nt(xdec[:, sl], dstb[:, sl]))
            ds_prev.append(_dot_tn(cmb[g], dyeb[:, sl]))
        dst_ref[i] = jnp.concatenate(ds_prev, axis=1) + dst * f["cde"]
        dcs = dcs + dcst.T
        da = _dot_hi(lt_ref[...].T, dcs)
        dxdt = jnp.concatenate(dxdt, axis=1)
        a_neg = f["a_neg"]
        ddt = da * a_neg + _dot_01(dxdt * xs, ext)
        dal_ref[...] += jnp.sum(da * dt, axis=0, keepdims=True) * a_neg
        dxs = dxs + dxdt * f["dte"]
        ddtraw = jnp.where(lane < SSD_HEADS, ddt * _sigmoid(f["dtpre"]), 0.0)
        ddtb_ref[...] += jnp.sum(ddtraw, axis=0, keepdims=True)
        ddt_ref[i] = ddtraw.astype(MXU)
        dxa = jnp.concatenate([dxs, dbm[0], dbm[1], dcm[0], dcm[1]], axis=1)
        sig_c = f["sig_c"]
        dconv = dxa * (sig_c + f["xa"] * (1.0 - sig_c))
        dcb_ref[...] += jnp.sum(dconv, axis=0, keepdims=True)
        xbc = xbc_ref[i]
        dcw_ref[3:4, :] += jnp.sum(dconv * xbc, axis=0, keepdims=True)
        dxbc = cw[3:4] * dconv
        for j, up in zip((1, 2, 3), _shifts_up(dconv, dnext_ref[i])):
            dcw_ref[3 - j:4 - j, :] += jnp.sum(up * xbc, axis=0, keepdims=True)
            dxbc = dxbc + cw[3 - j:4 - j] * up
        dnext_ref[i] = dconv[0:8]
        dssd_ref[i, :, 0:1024] = dz.astype(MXU)
        dssd_ref[i, :, 1024:2560] = dxbc.astype(MXU)

    dssd, ddt, *small = pl.pallas_call(
        _after(14, body), grid=(nb // ns, nch), name="ssd_bwd",
        in_specs=in_specs,
        out_specs=[seq(2560), seq(128),
                   _const((8, CONV_CH)), _const((1, CONV_CH)), _const((1, 128)), _const((1, 128)), _const((1, 128)),
                   _const((1, 1024))],
        out_shape=[_sds((nb, S, 2560), MXU), _sds((nb, S, 128), MXU), _sds((8, CONV_CH), F32),
                   _sds((1, CONV_CH), F32), _sds((1, 128), F32), _sds((1, 128), F32), _sds((1, 128), F32),
                   _sds((1, 1024), F32)],
        scratch_shapes=[pltpu.VMEM((ns, 128, 1024), F32), pltpu.VMEM((ns, 8, CONV_CH), F32),
                        pltpu.VMEM((1, 1024), F32)],
        compiler_params=_cp(2),
    )(pz.reshape(nb, S, 1024), pxbc.reshape(nb, S, CONV_CH), conv.reshape(nb, S, CONV_CH), dtraw.reshape(nb, S, 128),
      convw, dtb, alog, dskip, ng, ex, ltri, ext, sall, dcat.reshape(nb, S, 2048), after)
    return (dssd.reshape(T, 2560), ddt.reshape(T, 128), *small)


def _inproj_bwd(dxn_uv, dssd, ddt, wm, wdt, dh1, x, g, tm, after):
    T = x.shape[0]

    def body(dxnuv_ref, dssd_ref, ddt_ref, wm_ref, wdt_ref, dh1_ref, x_ref, g_ref, dx_ref, dg_ref):
        @pl.when(pl.program_id(0) == 0)
        def _():
            dg_ref[...] = jnp.zeros_like(dg_ref)

        dxn = (dxnuv_ref[...] + _dot_nt(dssd_ref[...], wm_ref[:, 2048:N_MAIN])
               + _dot_nt(ddt_ref[...], wdt_ref[...]))
        xh, r = _rms(x_ref[...])
        dg_ref[...] += jnp.sum(dxn * xh, axis=0, keepdims=True)
        dx_ref[...] = dh1_ref[...] + _rms_bwd(dxn, xh, r, g_ref[...])

    return pl.pallas_call(
        _after(8, body), grid=(T // tm,), name="inproj_bwd",
        in_specs=[_rows(tm, D), _rows(tm, 2560), _rows(tm, 128), _const((D, N_MAIN)), _const((D, 128)),
                  _rows(tm, D), _rows(tm, D), _const((1, D)), _ANY],
        out_specs=[_rows(tm, D), _const((1, D))],
        out_shape=[_sds((T, D), F32), _sds((1, D), F32)],
        compiler_params=_cp(),
    )(dxn_uv, dssd, ddt, wm, wdt, dh1, x, g, after)


def _matmul_tn(a, b, name, a_fn=None):
    T, M = a.shape
    N = b.shape[1]
    tm = min(M, 1024)
    tn = 1280 if N == 2560 else min(N, 1024)
    tk = min(T, 4096)

    def body(a_ref, b_ref, o_ref, acc_ref):
        k = pl.program_id(2)

        @pl.when(k == 0)
        def _():
            acc_ref[...] = jnp.zeros_like(acc_ref)

        av = a_ref[...]
        if a_fn is not None:
            av = a_fn(av)
        acc_ref[...] += _dot_tn(av, b_ref[...])

        @pl.when(k == T // tk - 1)
        def _():
            o_ref[...] = acc_ref[...].astype(o_ref.dtype)

    return pl.pallas_call(
        body, grid=(M // tm, N // tn, T // tk), name=name,
        in_specs=[pl.BlockSpec((tk, tm), lambda i, j, k: (k, i)), pl.BlockSpec((tk, tn), lambda i, j, k: (k, j))],
        out_specs=pl.BlockSpec((tm, tn), lambda i, j, k: (i, j)),
        out_shape=_sds((M, N), GRAD),
        scratch_shapes=[pltpu.VMEM((tm, tn), F32)],
        compiler_params=_cp(3),
    )(a, b)


def _adamw_vals(w, g, m, v):
    m = B1 * m + (1.0 - B1) * g
    v = B2 * v + (1.0 - B2) * (g * g)
    m_hat = m / (1.0 - B1 ** STEP)
    v_hat = v / (1.0 - B2 ** STEP)
    return -LR * (m_hat / (jnp.sqrt(v_hat) + ADAM_EPS) + WD * w), m, v


_PARTS = 4


def _adamw_halves(items, name):
    n = len(items)

    def body(*refs):
        mine = (pl.program_id(0) // _PARTS) == lax.axis_index("c")
        for k in range(n):
            w_ref, own_ref, oth_ref, m_ref, v_ref = refs[5 * k:5 * k + 5]
            g_ref, d_ref, mo_ref, vo_ref = refs[5 * n + 4 * k:5 * n + 4 * k + 4]
            g = jnp.where(mine, own_ref[...], oth_ref[...])
            g_ref[...] = g
            d_ref[...], mo_ref[...], vo_ref[...] = _adamw_vals(w_ref[...], g, m_ref[...], v_ref[...])

    in_specs, out_specs, out_shape = [], [], []
    for w, *_ in items:
        R, C = w.shape
        full = _rows(R // (2 * _PARTS), C)
        part = pl.BlockSpec((R // (2 * _PARTS), C), lambda i: (i % _PARTS, 0))
        in_specs += [full, part, part, full, full]
        out_specs += [full] * 4
        out_shape += [_sds((R, C), F32)] * 4
    res = pl.pallas_call(
        body, grid=(2 * _PARTS,), name=name, in_specs=in_specs, out_specs=out_specs, out_shape=out_shape,
        compiler_params=_cp(),
    )(*[a for item in items for a in item])
    return [tuple(res[4 * k:4 * k + 4]) for k in range(n)]


_TJ = 128


def _adamw_transposed(w, own, other, m, v, name):
    C, _, R = w.shape

    def body(w_ref, own_ref, oth_ref, m_ref, v_ref, g_ref, d_ref, mo_ref, vo_ref):
        first = lax.axis_index("c") == 0
        g = jnp.concatenate([jnp.where(first, own_ref[...], oth_ref[...]),
                             jnp.where(first, oth_ref[...], own_ref[...])], axis=0).T
        d, mo, vo = _adamw_vals(w_ref[:, 0, :], g, m_ref[:, 0, :], v_ref[:, 0, :])
        for ref, val in ((g_ref, g), (d_ref, d), (mo_ref, mo), (vo_ref, vo)):
            ref[:, 0, :] = val

    cols = pl.BlockSpec((_TJ, 1, R), lambda j: (j, 0, 0))
    half = pl.BlockSpec((R // 2, _TJ), lambda j: (0, j))
    return pl.pallas_call(
        body, grid=(pl.cdiv(C, _TJ),), name=name,
        in_specs=[cols, half, half, cols, cols], out_specs=[cols] * 4, out_shape=[_sds((C, 1, R), F32)] * 4,
        compiler_params=_cp(),
    )(w, own, other, m, v)


def _lanes(rows):
    return jnp.concatenate([rows[i:i + 1, :] for i in range(rows.shape[0])], axis=1)


def _small_update(a, packs):
    names = [n for n, _ in _SMALL]
    where = {}
    for k, (pnames, rows, _) in enumerate(packs):
        o = 0
        for n, r in zip(pnames, rows):
            where[n] = (k, o, r)
            o += r
    view = {n: (1, 1024) for n in names}
    view.update(gm_ws=(1024, 128), gm_bs=(8, 128), ssd_conv_w=(4, 384), ssd_conv_b=(1, CONV_CH),
                ssd_dt_bias=(1, 16), ssd_a_log=(1, 16), ssd_d=(1, 16))
    npk = len(packs)

    def body(*refs):
        tots = []
        for k in range(npk):
            tot = refs[k][0]
            for d in range(1, 8):
                tot = tot + refs[k][d]
            tots.append(tot)
        ins, outs = refs[npk:npk + 3 * len(names)], refs[npk + 3 * len(names):]
        chip = 2 * lax.axis_index("x") + lax.axis_index("y")
        for i, n in enumerate(names):
            k, o, r = where[n]
            blk = tots[k][o:o + r, :]
            if n == "gm_ws":
                g = blk
            elif n == "gm_bs":
                g = blk[0:8]
            elif view[n] == (1, 16):
                g = blk[0:1, 0:16]
            elif n == "ssd_conv_w":
                taps = jnp.concatenate([_lanes(blk[12 * t:12 * t + 12]) for t in range(4)], axis=0)
                g = taps[:, 0:384]
                for c in range(1, 4):
                    g = jnp.where(chip == c, taps[:, 384 * c:384 * (c + 1)], g)
            else:
                g = _lanes(blk[0:view[n][1] // 128])
            d, mo, vo = _adamw_vals(ins[3 * i][...], g, ins[3 * i + 1][...], ins[3 * i + 2][...])
            for j, val in enumerate((g, d, mo, vo)):
                outs[4 * i + j][...] = val
        k, o, _ = where["loss"]
        outs[-1][...] = tots[k][o:o + 1, 0:1]

    ins = [a[pre + n].reshape(view[n]) for n in names for pre in ("", "m_", "v_")]
    res = pl.pallas_call(
        body, name="small_update",
        out_shape=[_sds(view[n], F32) for n in names for _ in range(4)] + [_sds((1, 1), F32)],
    )(*[slots for _, _, slots in packs], *ins)
    return {n: tuple(r.reshape(a[n].shape) for r in res[4 * i:4 * i + 4]) for i, n in enumerate(names)}, res[-1]


def _sum_slots(items, kh, name):
    n = len(items)
    in_specs, out_specs, out_shape = [], [], []
    for slots, src, kind, (R, C) in items:
        tr = R // (2 * _PARTS)
        if kind == "slab":
            src_spec = pl.BlockSpec((1, tr, C), lambda i, kh: (kh[0], kh[1] * _PARTS + i, 0))
        elif kind == "rows":
            src_spec = pl.BlockSpec((tr, C), lambda i, kh: (kh[0] * (2 * _PARTS) + kh[1] * _PARTS + i, 0))
        else:
            src_spec = pl.BlockSpec((tr, C), lambda i, kh: (kh[1] * _PARTS + i, kh[0]))
        in_specs += [pl.BlockSpec((8, tr, C), lambda i, kh: (0, i, 0)), src_spec]
        out_specs.append(pl.BlockSpec((tr, C), lambda i, kh: (i, 0)))
        out_shape.append(_sds((R // 2, C), F32))

    def body(kh_ref, *refs):
        me = 2 * kh_ref[0] + kh_ref[1]
        for k, (_, _, kind, _) in enumerate(items):
            s_ref, own_ref, o_ref = refs[2 * k], refs[2 * k + 1], refs[2 * n + k]
            acc = (own_ref[0] if kind == "slab" else own_ref[...]).astype(F32)
            for j in range(1, 8):
                acc = acc + s_ref[me ^ j].astype(F32)
            o_ref[...] = acc

    return pl.pallas_call(
        body, name=name,
        grid_spec=pltpu.PrefetchScalarGridSpec(
            num_scalar_prefetch=1, grid=(_PARTS,), in_specs=in_specs, out_specs=out_specs),
        out_shape=out_shape,
        compiler_params=_cp(),
    )(kh, *[a for slots, src, _, _ in items for a in (slots, src)])


def _assemble_w_in(slabs):
    tr = 256

    def body(s_ref, wm_ref, wdt_ref):
        full = jnp.concatenate([s_ref[k] for k in range(4)], axis=1)
        wm_ref[...] = full[:, :N_MAIN]
        wdt_ref[...] = jnp.concatenate([full[:, N_MAIN:], jnp.zeros((tr, 128 - 16), full.dtype)], axis=1)

    return pl.pallas_call(
        body, grid=(D // tr,), name="assemble_w_in",
        in_specs=[pl.BlockSpec((4, tr, 1156), lambda i: (0, i, 0))],
        out_specs=[_rows(tr, N_MAIN), _rows(tr, 128)],
        out_shape=[_sds((D, N_MAIN), slabs.dtype), _sds((D, 128), slabs.dtype)],
        compiler_params=_cp(),
    )(slabs)


def _split_dw_in(d_uv, d_ssd, d_dt):
    tr = 256

    def body(uv_ref, ssd_ref, dt_ref, o_ref):
        full = jnp.concatenate([uv_ref[...], ssd_ref[...], dt_ref[:, 0:16]], axis=1)
        for k in range(4):
            o_ref[k] = full[:, 1156 * k:1156 * (k + 1)]

    return pl.pallas_call(
        body, grid=(D // tr,), name="split_dw_in",
        in_specs=[_rows(tr, 2048), _rows(tr, 2560), _rows(tr, 128)],
        out_specs=pl.BlockSpec((4, tr, 1156), lambda i: (0, i, 0)),
        out_shape=_sds((4, D, 1156), d_uv.dtype),
        compiler_params=_cp(),
    )(d_uv, d_ssd, d_dt)


def _cast_w_in(w, kh):
    C, _, R = w.shape

    def body(kh_ref, w_ref, o_ref):
        o_ref[0] = w_ref[:, 0, :].T.astype(BF16)

    return pl.pallas_call(
        body, name="cast_w_in",
        grid_spec=pltpu.PrefetchScalarGridSpec(
            num_scalar_prefetch=1, grid=(pl.cdiv(C, _TJ),),
            in_specs=[pl.BlockSpec((_TJ, 1, R), lambda j, kh: (j, 0, 0))],
            out_specs=pl.BlockSpec((1, R, _TJ), lambda j, kh: (kh[0], 0, j))),
        out_shape=_sds((4, R, C), BF16),
        compiler_params=_cp(),
    )(kh, w)


def _cast_into_slot(ws, kh, name):
    n = len(ws)

    def body(kh_ref, *refs):
        for k in range(n):
            refs[n + k][0] = refs[k][...].astype(BF16)

    return pl.pallas_call(
        body, name=name,
        grid_spec=pltpu.PrefetchScalarGridSpec(
            num_scalar_prefetch=1, grid=(_PARTS,),
            in_specs=[pl.BlockSpec((w.shape[0] // _PARTS, w.shape[1]), lambda i, kh: (i, 0)) for w in ws],
            out_specs=[pl.BlockSpec((1, w.shape[0] // _PARTS, w.shape[1]), lambda i, kh: (kh[0], i, 0))
                       for w in ws]),
        out_shape=[_sds((4,) + w.shape, BF16) for w in ws],
        compiler_params=_cp(),
    )(kh, *ws)


_ANY = pl.BlockSpec(memory_space=pl.ANY)
_CHIP_FLIPS = [(1, 0), (0, 1), (1, 1)]
_DEVICE_FLIPS = [(fx, fy, fc) for fx in (0, 1) for fy in (0, 1) for fc in (0, 1)][1:]


def _half(h, rows):
    return pl.ds(pl.multiple_of(h * rows, rows), rows)


def _remote(src, dst, ssem, rsem, to):
    return pltpu.make_async_remote_copy(src_ref=src, dst_ref=dst, send_sem=ssem, recv_sem=rsem,
                                        device_id=to, device_id_type=MESH)


def _weight_gather(bufs, conv):
    n = len(bufs)

    def body(*refs):
        conv_ref, outs, conv_out = refs[n], refs[n + 1:2 * n + 1], refs[2 * n + 1]
        send_sems, recv_sems, fsend_sems, frecv_sems, csend_sems, crecv_sems, local_sem = refs[2 * n + 2:]
        x, y, c = lax.axis_index("x"), lax.axis_index("y"), lax.axis_index("c")
        me = 2 * x + y
        halves = [_half(c, r.shape[1] // 2) for r in outs]
        others = [_half(1 - c, r.shape[1] // 2) for r in outs]
        remote = _remote
        local = [pltpu.make_async_copy(conv_ref, conv_out.at[me], local_sem)]
        for cp in local:
            cp.start()
        sends = []
        for k, (fx, fy) in enumerate(_CHIP_FLIPS):
            peer = (x ^ fx, y ^ fy, c)
            for i in range(n):
                mine = outs[i].at[me, halves[i]]
                sends.append(remote(mine, mine, send_sems.at[k * n + i], recv_sems.at[k * n + i], peer))
            sends.append(remote(conv_ref, conv_out.at[me], csend_sems.at[k], crecv_sems.at[k], peer))
        for cp in sends:
            cp.start()
        sibling = (x, y, 1 - c)
        forwards = []
        for k, (fx, fy) in enumerate(_CHIP_FLIPS):
            peer = (x ^ fx, y ^ fy, c)
            src = 2 * (x ^ fx) + (y ^ fy)
            for i in range(n):
                landed = outs[i].at[src, halves[i]]
                remote(landed, landed, send_sems.at[k * n + i], recv_sems.at[k * n + i], peer).wait_recv()
                fw = remote(landed, landed, fsend_sems.at[k * n + i], frecv_sems.at[k * n + i], sibling)
                fw.start()
                forwards.append(fw)
            remote(conv_out.at[src], conv_out.at[src], csend_sems.at[k], crecv_sems.at[k], peer).wait_recv()
        for k, (fx, fy) in enumerate(_CHIP_FLIPS):
            src = 2 * (x ^ fx) + (y ^ fy)
            for i in range(n):
                theirs = outs[i].at[src, others[i]]
                remote(theirs, theirs, fsend_sems.at[k * n + i], frecv_sems.at[k * n + i], sibling).wait_recv()
        for cp in sends + forwards:
            cp.wait_send()
        for cp in local:
            cp.wait()

    dma = pltpu.SemaphoreType.DMA
    return pl.pallas_call(
        body, name="weight_gather",
        in_specs=[_ANY] * (n + 1), out_specs=[_ANY] * (n + 1),
        out_shape=[_sds(b.shape, b.dtype) for b in bufs] + [_sds((4,) + conv.shape, conv.dtype)],
        input_output_aliases={i: i for i in range(n)},
        scratch_shapes=[dma((3 * n,)), dma((3 * n,)), dma((3 * n,)), dma((3 * n,)), dma((3,)), dma((3,)), dma],
    )(*bufs, conv)


def _piece(ref, kind, R, C, k, h):
    if kind == "slab":
        return ref.at[k, _half(h, R // 2), :]
    if kind == "rows":
        return ref.at[pl.ds(pl.multiple_of(k * R + h * (R // 2), R // 2), R // 2), :]
    return ref.at[_half(h, R // 2), pl.ds(pl.multiple_of(k * C, C), C)]


_HBM = pl.BlockSpec(memory_space=pltpu.HBM)
_SEM = pl.BlockSpec(memory_space=pltpu.SEMAPHORE)


def _split_start(name, arrays, n_copies, plan, after=None):
    n = len(arrays)
    extra = [] if after is None else [after]

    def body(*refs):
        m = n + len(extra)
        arrs, send_sems, recv_sems, token = refs[:n], refs[m], refs[m + 1], refs[-1]
        for j, (src, dst, peer) in enumerate(plan(arrs)):
            _remote(src, dst, send_sems.at[j], recv_sems.at[j], peer).start()
        token[...] = jnp.zeros_like(token)

    dma = pltpu.SemaphoreType.DMA
    res = pl.pallas_call(
        body, name=name,
        out_shape=(dma((n_copies,)), dma((n_copies,)), *[pltpu.HBM(a.shape, a.dtype) for a in arrays],
                   _sds((8, 128), F32)),
        in_specs=[_HBM] * n + [_ANY] * len(extra),
        out_specs=(_SEM, _SEM, *[_HBM] * n, pl.BlockSpec(memory_space=pltpu.VMEM)),
        input_output_aliases={i: 2 + i for i in range(n)},
        compiler_params=pltpu.CompilerParams(has_side_effects=pltpu.SideEffectType.DATAFLOW_SIDE_EFFECTING),
    )(*[pltpu.with_memory_space_constraint(a, pltpu.HBM) for a in arrays], *extra)
    return res[0], res[1], list(res[2:2 + n]), res[-1]


def _split_wait(name, arrays, send_sems, recv_sems, plan, after):
    n = len(arrays)

    def body(*refs):
        arrs, ssems, rsems = refs[:n], refs[n], refs[n + 1]
        for j, (src, dst, peer) in enumerate(plan(arrs)):
            cp = _remote(src, dst, ssems.at[j], rsems.at[j], peer)
            cp.wait_send()
            cp.wait_recv()

    return list(pl.pallas_call(
        body, name=name,
        out_shape=tuple(pltpu.HBM(a.shape, a.dtype) for a in arrays),
        in_specs=[_HBM] * n + [_SEM, _SEM, _ANY],
        out_specs=tuple([_HBM] * n),
        input_output_aliases={i: i for i in range(n)},
        compiler_params=pltpu.CompilerParams(has_side_effects=pltpu.SideEffectType.DATAFLOW_SIDE_EFFECTING),
    )(*arrays, send_sems, recv_sems, after))


def _gather_plan(n):
    def plan(bufs):
        x, y, c = lax.axis_index("x"), lax.axis_index("y"), lax.axis_index("c")
        me = 2 * x + y
        return [(bufs[i].at[me], bufs[i].at[me], (x ^ fx, y ^ fy, c)) for fx, fy in _CHIP_FLIPS for i in range(n)]

    return plan


def _reduce_plan(specs, n_small):
    n = len(specs)

    def plan(arrs):
        x, y, c = lax.axis_index("x"), lax.axis_index("y"), lax.axis_index("c")
        slot = 4 * x + 2 * y + c
        out = []
        for fx, fy, fc in _DEVICE_FLIPS:
            peer = (x ^ fx, y ^ fy, c ^ fc)
            for i, (kind, (R, C)) in enumerate(specs):
                out.append((_piece(arrs[i], kind, R, C, 2 * peer[0] + peer[1], peer[2]), arrs[n + i].at[slot], peer))
            for s in range(n_small):
                out.append((arrs[2 * n + 2 * s], arrs[2 * n + 2 * s + 1].at[slot], peer))
        return out

    return plan


def _sibling_exchange(halves, name, small=None):
    n = len(halves)
    ns = 0 if small is None else 1

    def body(*refs):
        ins, outs = refs[:n], refs[n + ns:2 * n + ns]
        send_sems, recv_sems = refs[2 * (n + ns)], refs[2 * (n + ns) + 1]
        x, y, c = lax.axis_index("x"), lax.axis_index("y"), lax.axis_index("c")
        copies = [_remote(ins[i], outs[i], send_sems.at[i], recv_sems.at[i], (x, y, 1 - c)) for i in range(n)]
        waits = list(copies)
        if ns:
            s_ref, slots_ref, ssend_sems, srecv_sems, local_sem = refs[n], refs[2 * n + 1], *refs[2 * (n + ns) + 2:]
            slot = 4 * x + 2 * y + c
            own = pltpu.make_async_copy(s_ref, slots_ref.at[slot], local_sem)
            own.start()
            for k, (fx, fy, fc) in enumerate(_DEVICE_FLIPS):
                peer = (x ^ fx, y ^ fy, c ^ fc)
                copies.append(_remote(s_ref, slots_ref.at[slot], ssend_sems.at[k], srecv_sems.at[k], peer))
                theirs = slots_ref.at[slot ^ (k + 1)]
                waits.append(_remote(theirs, theirs, ssend_sems.at[k], srecv_sems.at[k], peer))
        for cp in copies:
            cp.start()
        for cp in waits:
            cp.wait()
        if ns:
            own.wait()

    dma = pltpu.SemaphoreType.DMA
    extra_in = [] if small is None else [small]
    extra_out = [] if small is None else [_sds((8,) + small.shape, F32)]
    return pl.pallas_call(
        body, name=name,
        in_specs=[_ANY] * (n + ns), out_specs=[_ANY] * (n + ns),
        out_shape=[_sds(h.shape, h.dtype) for h in halves] + extra_out,
        scratch_shapes=[dma((n,)), dma((n,))] + ([dma((7,)), dma((7,)), dma] if ns else []),
    )(*halves, *extra_in)


_BIG = [("w_in", (1024, 1156), "slab"), ("w_out", (512, 1024), "rows"), ("w_ff1", (1024, 1024), "cols"),
        ("w_ff2", (1024, 1024), "rows"), ("w_ple_gate", (256, 1024), "rows"), ("w_ple_proj", (256, 256), "cols")]
_SMALL = [("norm_mix_g", (1, 1024)), ("gm_v_norm_g", (1, 1024)), ("gm_ws", (1, 8, 128, 128)), ("gm_bs", (1, 8, 128)),
          ("gm_out_norm_g", (1, 1024)), ("ssd_conv_w", (1, 4, 1536)), ("ssd_conv_b", (1, 1536)),
          ("ssd_dt_bias", (1, 16)), ("ssd_a_log", (1, 16)), ("ssd_d", (1, 16)), ("ssd_norm_g", (1, 1024)),
          ("norm_mlp_g", (1, 1024)), ("ple_norm_g", (1, 1024)), ("final_norm_g", (1024,))]


def _rows128(a):
    flat = a.reshape(-1)
    rows = -(-flat.shape[0] // 1024) * 8
    return jnp.pad(flat, (0, rows * 128 - flat.shape[0])).reshape(rows, 128)


def _pad_lanes(v, n=128):
    v = v.reshape(1, -1)
    return jnp.pad(v, ((0, 0), (0, n - v.shape[1])))


_SMALL_SHAPES = dict(_SMALL + [("loss", ())])
_BIG_SPECS = {n: (kind, shp) for n, shp, kind in _BIG}


class _Comm:
    def __init__(self, a, kh):
        self.a, self.kh = a, kh
        rest = _BIG[1:]
        self.bufs = {"w_in": _cast_w_in(a["w_in"].transpose(2, 0, 1), kh)}
        cast = _cast_into_slot([a[n].reshape(shp) for n, shp, _ in rest], kh, "cast_rest")
        self.bufs.update({n: c for (n, _, _), c in zip(rest, cast)})
        self.sent = []
        self.small_packs = []

    def w_in(self):
        g_win, g_cw = _weight_gather([self.bufs["w_in"]], self.a["ssd_conv_w"].reshape(4, 384))
        token = g_cw
        self.gather = {}
        for tag, names in (("out", ["w_out"]), ("ff", ["w_ff1", "w_ff2", "w_ple_gate", "w_ple_proj"])):
            plan = _gather_plan(len(names))
            ssem, rsem, thru, token = _split_start("gather_start_" + tag, [self.bufs[n] for n in names],
                                                   3 * len(names), plan, after=token)
            self.gather[tag] = (plan, ssem, rsem, thru)
        wm, wdt = _assemble_w_in(g_win)
        return wm, wdt, jnp.concatenate([g_cw[k] for k in range(4)], axis=1), token

    def rest(self, tag, after):
        plan, ssem, rsem, thru = self.gather[tag]
        got = _split_wait("gather_wait_" + tag, thru, ssem, rsem, plan, after)
        if tag == "out":
            return got[0].reshape(2048, D)
        g_w1, g_w2, g_wg, g_wp = got
        return g_w1, g_w2.reshape(DFF, D), g_wg.reshape(D, D), g_wp

    def send(self, tag, grads):
        big = [n for n, _, _ in _BIG if n in grads]
        small = [n for n in _SMALL_SHAPES if n in grads]
        parts = [_rows128(grads[n]) for n in small]
        rows = [s.shape[0] for s in parts]
        if not big:
            self.last_small = (tag, small, rows, jnp.concatenate(parts, axis=0))
            return None
        srcs = [grads[n] for n in big]
        lands = [lax.empty((8, _BIG_SPECS[n][1][0] // 2, _BIG_SPECS[n][1][1]), GRAD) for n in big]
        extra = []
        if small:
            pack = jnp.concatenate(parts, axis=0)
            extra = [pack, jnp.broadcast_to(pack, (8,) + pack.shape)]
        plan = _reduce_plan([_BIG_SPECS[n] for n in big], len(extra) // 2)
        n_copies = 7 * (len(big) + len(extra) // 2)
        ssem, rsem, thru, token = _split_start("reduce_start_" + tag, srcs + lands + extra, n_copies, plan)
        self.sent.append((tag, big, small, rows, plan, ssem, rsem, thru))
        return token

    def finish(self, after):
        a, results = self.a, {}

        def update(names, own, tag):
            if names == ["w_in"]:
                stag, small, rows, pack = self.last_small
                *other, slots = _sibling_exchange([own[n] for n in names], "sibling_exchange_" + tag, pack)
                self.small_packs.append((small, rows, slots))
                w, m, v = (a[k].transpose(2, 0, 1) for k in ("w_in", "m_w_in", "v_w_in"))
                raw = _adamw_transposed(w, own["w_in"], other[0], m, v, "adamw_" + tag)
                results["w_in"] = tuple(r.transpose(1, 2, 0) for r in raw)
                return raw[1]
            other = _sibling_exchange([own[n] for n in names], "sibling_exchange_" + tag)
            items = [(a[n].reshape(_BIG_SPECS[n][1]), own[n], oth, a["m_" + n].reshape(_BIG_SPECS[n][1]),
                      a["v_" + n].reshape(_BIG_SPECS[n][1])) for n, oth in zip(names, other)]
            results.update(zip(names, _adamw_halves(items, "adamw_" + tag)))
            return results[names[-1]][1]

        own, early = {}, []
        for tag, big, small, rows, plan, ssem, rsem, thru in self.sent:
            if tag == self.sent[-1][0]:
                after = update(early, own, "early")
            arrs = _split_wait("reduce_wait_" + tag, thru, ssem, rsem, plan, after)
            nb_ = len(big)
            sums = _sum_slots([(arrs[nb_ + i], arrs[i]) + _BIG_SPECS[n] for i, n in enumerate(big)], self.kh,
                              "sum_" + tag)
            own.update(zip(big, sums))
            after = sums[-1]
            early += big
            if small:
                self.small_packs.append((small, rows, arrs[2 * nb_ + 1]))
        update(self.sent[-1][1], own, "late")
        return results, self.small_packs


def _local_step(x, p, tgt, sm, comm, nb, tm):
    T = x.shape[0]
    wm, wdt, conv_w, token = comm.w_in()
    g_mix, gv, gout = sm["norm_mix_g"].reshape(1, D), sm["gm_v_norm_g"].reshape(1, D), sm["gm_out_norm_g"].reshape(1, D)
    ws = sm["gm_ws"].reshape(GM_HEADS, CH, CH)
    bst = jnp.pad(sm["gm_bs"].reshape(GM_HEADS, CH).T, ((0, 0), (0, 128 - GM_HEADS)))
    convw = jnp.pad(conv_w, ((0, 4), (0, 0)))
    convb = sm["ssd_conv_b"].reshape(1, CONV_CH)
    dtb, alog = _pad_lanes(sm["ssd_dt_bias"]), _pad_lanes(sm["ssd_a_log"])
    dskip = jnp.repeat(sm["ssd_d"].reshape(SSD_HEADS), SSD_P).reshape(1, 1024)
    ng, g_mlp, g_ple = sm["ssd_norm_g"].reshape(1, D), sm["norm_mlp_g"].reshape(1, D), sm["ple_norm_g"].reshape(1, D)
    gf = sm["final_norm_g"].reshape(1, D)
    head_of_lane = lax.broadcasted_iota(jnp.int32, (128, 1024), 1) // SSD_P
    ex = (lax.broadcasted_iota(jnp.int32, (128, 1024), 0) == head_of_lane).astype(BF16)
    ext = ex.T
    ltri = (lax.broadcasted_iota(jnp.int32, (CH, CH), 0) >= lax.broadcasted_iota(jnp.int32, (CH, CH), 1)).astype(F32)

    pz, pxbc, dtraw, xn, cat, uv = _inproj_gmlp(x, g_mix, wm, wdt, gv, ws, bst, gout, tm, token)
    cat, sall, conv = _ssd_fwd(pz, pxbc, dtraw, cat, convw, convb, dtb, alog, dskip, ng, ex, ltri, nb)
    wo = comm.rest("out", cat)
    h1, hn = _outproj(cat, wo, x, g_mlp, tm)
    w1, w2, wg, wp = comm.rest("ff", hn)
    hid = _ff1(hn, w1, min(T, 2 * tm))
    hp, dgl, dpe, dh2, dh2b, loss, d_gf, d_gple = _ff2_tail(hid, w2, h1, g_ple, p, tgt, wg, wp, gf, tm)

    d_wp = _matmul_tn(p, dpe, "dw_ple_proj", a_fn=lambda a: a.astype(MXU))
    d_wg = _matmul_tn(hp, dgl, "dw_ple_gate")
    d_w2 = _matmul_tn(hid, dh2b, "dw_ff2", a_fn=_sq)
    dpre = _ff2_bwd(dh2b, w2, hid, min(T, 2 * tm))
    d_w1 = _matmul_tn(hn, dpre, "dw_ff1")
    token = comm.send("a", {"w_ple_proj": d_wp, "w_ple_gate": d_wg, "w_ff2": d_w2, "w_ff1": d_w1})
    dh1, dh1b, d_gmlp = _ff1_bwd(dpre, w1, dh2, h1, g_mlp, tm, token)
    dcat = _outproj_bwd(dh1b, wo, min(T, 2 * tm))
    d_wo = _matmul_tn(cat, dh1b, "dw_out")
    duv, d_gv, d_ws, d_bst, d_gout, dxn_uv = _gmlp_bwd(uv, dcat, gv, ws, bst, gout, wm)
    token = comm.send("b", {
        "w_out": d_wo, "loss": loss[0:1, 0:1], "final_norm_g": d_gf, "ple_norm_g": d_gple, "norm_mlp_g": d_gmlp,
        "gm_v_norm_g": d_gv, "gm_ws": d_ws, "gm_bs": d_bst[:, :GM_HEADS].T, "gm_out_norm_g": d_gout})
    dssd, ddt, d_cw, d_cb, d_dtb, d_al, d_ds, d_ng = _ssd_bwd(
        pz, pxbc, conv, dtraw, sall, dcat, convw, dtb, alog, dskip, ng, ex, ltri, ext, nb, token)
    d_win = _split_dw_in(_matmul_tn(xn, duv, "dw_in_uv"), _matmul_tn(xn, dssd, "dw_in_ssd"),
                         _matmul_tn(xn, ddt, "dw_in_dt"))
    token = comm.send("c", {"w_in": d_win})
    dx, d_gmix = _inproj_bwd(dxn_uv, dssd, ddt, wm, wdt, dh1, x, g_mix, tm, token)
    comm.send("d", {"norm_mix_g": d_gmix, "ssd_conv_w": d_cw[0:4], "ssd_conv_b": d_cb, "ssd_dt_bias": d_dtb[:, :16],
                    "ssd_a_log": d_al[:, :16], "ssd_d": d_ds[:, :16], "ssd_norm_g": d_ng})
    return dx


def kernel(x, p, norm_mix_g, w_in, gm_v_norm_g, gm_ws, gm_bs, gm_out_norm_g, ssd_conv_w, ssd_conv_b, ssd_dt_bias, ssd_a_log, ssd_d, ssd_norm_g, w_out, norm_mlp_g, w_ff1, w_ff2, ple_norm_g, w_ple_gate, w_ple_proj, final_norm_g, loss_target, m_norm_mix_g, m_w_in, m_gm_v_norm_g, m_gm_ws, m_gm_bs, m_gm_out_norm_g, m_ssd_conv_w, m_ssd_conv_b, m_ssd_dt_bias, m_ssd_a_log, m_ssd_d, m_ssd_norm_g, m_w_out, m_norm_mlp_g, m_w_ff1, m_w_ff2, m_ple_norm_g, m_w_ple_gate, m_w_ple_proj, m_final_norm_g, v_norm_mix_g, v_w_in, v_gm_v_norm_g, v_gm_ws, v_gm_bs, v_gm_out_norm_g, v_ssd_conv_w, v_ssd_conv_b, v_ssd_dt_bias, v_ssd_a_log, v_ssd_d, v_ssd_norm_g, v_w_out, v_norm_mlp_g, v_w_ff1, v_w_ff2, v_ple_norm_g, v_w_ple_gate, v_w_ple_proj, v_final_norm_g):
    a = dict(locals())
    order = ["norm_mix_g", "w_in", "gm_v_norm_g", "gm_ws", "gm_bs", "gm_out_norm_g", "ssd_conv_w", "ssd_conv_b",
             "ssd_dt_bias", "ssd_a_log", "ssd_d", "ssd_norm_g", "w_out", "norm_mlp_g", "w_ff1", "w_ff2", "ple_norm_g",
             "w_ple_gate", "w_ple_proj", "final_norm_g"]
    chip = 2 * lax.axis_index("x") + lax.axis_index("y")
    nb, S = x.shape[0], x.shape[1]
    T = nb * S
    sm = {n: a[n] for n, _ in _SMALL if n != "ssd_conv_w"}
    comm = _Comm(a, jnp.stack([chip, lax.axis_index("c")]).astype(jnp.int32))
    dx = _local_step(x.reshape(T, D), p.reshape(T, DPLE), loss_target.reshape(T, D), sm, comm, nb, 512)
    big, small_packs = comm.finish(dx)
    small, loss = _small_update(a, small_packs)
    g_out, delta, new_m, new_v = {}, {}, {}, {}
    for n in order:
        g_out[n], delta[n], new_m[n], new_v[n] = (r.reshape(a[n].shape) for r in (big[n] if n in big else small[n]))
    return (loss.reshape(()), dx.reshape(x.shape), *[g_out[n] for n in order], *[delta[n] for n in order],
            *[new_m[n] for n in order], *[new_v[n] for n in order])
```

```python
import jax
import jax.numpy as jnp
from jax import lax
from jax.experimental import pallas as pl
from jax.experimental.pallas import tpu as pltpu

F32 = jnp.float32
BF16 = jnp.bfloat16
MXU = jnp.bfloat16
GRAD = jnp.bfloat16

D = 1024
CH = 128
GM_HEADS = 8
SSD_HEADS = 16
SSD_P = 64
CONV_CH = 1536
N_MAIN = 4608
DFF = 4096
DPLE = 256
EPS = 1e-6
NEG = -1e30

LR, B1, B2, ADAM_EPS, WD, STEP = 0.001, 0.9, 0.999, 1e-08, 0.01, 10

VMEM_LIMIT = 56 * 1024 * 1024
_SEQS_PER_STEP = 4
MESH = pl.DeviceIdType.MESH

INV_SQRT2 = 0.7071067811865476
INV_SQRT_2PI = 0.3989422804014327


def _cp(n_axes=1):
    return pltpu.CompilerParams(dimension_semantics=("arbitrary",) * n_axes, vmem_limit_bytes=VMEM_LIMIT)


def _dot(a, b):
    return jnp.dot(a, b, preferred_element_type=F32)


def _dot_nt(a, b):
    return lax.dot_general(a, b, (((1,), (1,)), ((), ())), preferred_element_type=F32)


def _dot_tn(a, b):
    return lax.dot_general(a, b, (((0,), (0,)), ((), ())), preferred_element_type=F32)


def _dot_hi(a, b):
    return jnp.dot(a, b, preferred_element_type=F32, precision=lax.Precision.HIGHEST)


def _dot_01(a, sel):
    hi = a.astype(BF16)
    lo = (a - hi.astype(F32)).astype(BF16)
    n = a.shape[0]
    r = _dot(jnp.concatenate([hi, lo], axis=0), sel)
    return r[0:n] + r[n:2 * n]


def _rows(tm, n, j=0):
    return pl.BlockSpec((tm, n), lambda i: (i, j))


def _const(shape):
    nd = len(shape)
    return pl.BlockSpec(shape, lambda *_: (0,) * nd)


def _sds(shape, dtype):
    return jax.ShapeDtypeStruct(shape, dtype)


def _rms(x):
    r = lax.rsqrt(jnp.mean(x * x, axis=-1, keepdims=True) + EPS)
    return x * r, r


def _rms_bwd(dy, xhat, r, g):
    dyg = dy * g
    return r * (dyg - xhat * jnp.mean(dyg * xhat, axis=-1, keepdims=True))


def _sigmoid(x):
    return 1.0 / (1.0 + jnp.exp(-x))


def _gelu(x):
    cdf = 0.5 * (1.0 + lax.erf(x * INV_SQRT2))
    pdf = jnp.exp(-0.5 * x * x) * INV_SQRT_2PI
    return x * cdf, cdf + x * pdf


def _softplus(x):
    e = jnp.exp(-jnp.abs(x))
    u = 1.0 + e
    log1p = jnp.where(u == 1.0, e, jnp.log(u) * e / (u - 1.0))
    return jnp.maximum(x, 0.0) + log1p


def _after(n_in, fn):
    def body(*refs):
        return fn(*refs[:n_in], *refs[n_in + 1:])

    return body


def _inproj_gmlp(x, g, wm, wdt, gv, ws, bst, gout, tm, after):
    T = x.shape[0]

    def body(x_ref, g_ref, wm_ref, wdt_ref, gv_ref, ws_ref, bst_ref, gout_ref,
             z_ref, xbc_ref, dt_ref, xn_ref, ya_ref, uv_ref):
        xh, _ = _rms(x_ref[...])
        xn = (xh * g_ref[...]).astype(MXU)
        xn_ref[...] = xn
        for n in range(4):
            uv_ref[:, n * 512:(n + 1) * 512] = _dot(xn, wm_ref[:, n * 512:(n + 1) * 512])
        for n in range(2):
            z_ref[:, n * 512:(n + 1) * 512] = _dot(xn, wm_ref[:, 2048 + n * 512:2048 + (n + 1) * 512])
        for n in range(3):
            xbc_ref[:, n * 512:(n + 1) * 512] = _dot(xn, wm_ref[:, 3072 + n * 512:3072 + (n + 1) * 512])
        dt_ref[...] = _dot(xn, wdt_ref[...])
        for k in range(tm // CH):
            rows = slice(k * CH, (k + 1) * CH)
            f = _gmlp_fwd_vals(uv_ref[rows, 0:1024], uv_ref[rows, 1024:2048], gv_ref[...], ws_ref, bst_ref[...],
                               gout_ref[...])
            ya_ref[rows, :] = f["out"].astype(MXU)

    return pl.pallas_call(
        _after(8, body), grid=(T // tm,), name="inproj_gmlp",
        in_specs=[_rows(tm, D), _const((1, D)), _const((D, N_MAIN)), _const((D, 128)), _const((1, 1024)),
                  _const((GM_HEADS, CH, CH)), _const((CH, 128)), _const((1, 1024)), _ANY],
        out_specs=[_rows(tm, 1024), _rows(tm, CONV_CH), _rows(tm, 128), _rows(tm, D), _rows(tm, 1024, 0),
                   _rows(tm, 2048)],
        out_shape=[_sds((T, 1024), F32), _sds((T, CONV_CH), F32), _sds((T, 128), F32), _sds((T, D), MXU),
                   _sds((T, 2048), MXU), _sds((T, 2048), F32)],
        compiler_params=_cp(),
    )(x, g, wm, wdt, gv, ws, bst, gout, after)


def _gmlp_fwd_vals(u, v, gv, ws_ref, bst, gout):
    ug, dug = _gelu(u)
    vg, dvg = _gelu(v)
    row = lax.broadcasted_iota(jnp.int32, (CH, CH), 0)
    col = lax.broadcasted_iota(jnp.int32, (CH, CH), 1)
    tril = row >= col
    ys, heads = [], []
    for h in range(GM_HEADS):
        sl = slice(h * 128, (h + 1) * 128)
        vhat, rv = _rms(vg[:, sl])
        vn = (vhat * gv[:, sl]).astype(MXU)
        wt = jnp.where(tril, ws_ref[h], 0.0)
        mixed = _dot(wt.astype(MXU), vn) + bst[:, h:h + 1]
        ys.append(ug[:, sl] * mixed)
        heads.append((vhat, rv, vn, wt, mixed))
    y = jnp.concatenate(ys, axis=1)
    yhat, ry = _rms(y)
    return dict(ug=ug, dug=dug, dvg=dvg, heads=heads, yhat=yhat, ry=ry, tril=tril, out=yhat * gout)


def _shifts_down(cur, halo):
    row8 = lax.broadcasted_iota(jnp.int32, (8, cur.shape[1]), 0)
    out = [cur]
    for j in (1, 2, 3):
        sh = pltpu.roll(cur, j, 0)
        top = jnp.where(row8 < j, pltpu.roll(halo, j, 0), sh[0:8])
        out.append(jnp.concatenate([top, sh[8:]], axis=0))
    return out


def _shifts_up(cur, halo):
    row8 = lax.broadcasted_iota(jnp.int32, (8, cur.shape[1]), 0)
    out = []
    for j in (1, 2, 3):
        sh = pltpu.roll(cur, CH - j, 0)
        bot = jnp.where(row8 + j >= 8, pltpu.roll(halo, 8 - j, 0), sh[CH - 8:CH])
        out.append(jnp.concatenate([sh[0:CH - 8], bot], axis=0))
    return out


def _conv(xbc, halo, convw, convb):
    sh = _shifts_down(xbc, halo)
    return convb + convw[3:4] * sh[0] + convw[2:3] * sh[1] + convw[1:2] * sh[2] + convw[0:1] * sh[3]


def _ssd_fwd_vals(z, conv, dtraw, dtb, alog, dskip, ng, ex, ltri, s_prev):
    sig_c = _sigmoid(conv)
    xa = conv * sig_c
    xs = xa[:, :1024]
    bm = [xa[:, 1024:1152], xa[:, 1152:1280]]
    cm = [xa[:, 1280:1408], xa[:, 1408:1536]]
    dtpre = dtraw + dtb
    dt = _softplus(dtpre)
    a_neg = -jnp.exp(alog)
    cs = _dot_hi(ltri, dt * a_neg)
    cst = cs.T
    last = cs[CH - 1:CH]
    ecs = jnp.exp(cs)
    dec = jnp.exp(last - cs)
    spread = _dot_01(jnp.concatenate([dt, ecs, dec], axis=0), ex)
    dte, ecse, dece = spread[0:CH], spread[CH:2 * CH], spread[2 * CH:3 * CH]
    cde = ecse[CH - 1:CH]
    de = dskip
    xdt = xs * dte
    row = lax.broadcasted_iota(jnp.int32, (CH, CH), 0)
    col = lax.broadcasted_iota(jnp.int32, (CH, CH), 1)
    tril = row >= col
    lo = col < SSD_P
    bmb = [b.astype(MXU) for b in bm]
    cmb = [c.astype(MXU) for c in cm]
    mg = [_dot_nt(cmb[g], bmb[g]) for g in range(2)]
    yd, lms, whs = [], [], []
    for q in range(8):
        g = q // 4
        xq = xdt[:, q * 128:(q + 1) * 128]
        acc = None
        for hh in range(2):
            h = 2 * q + hh
            seg = cs[:, h:h + 1] - cst[h:h + 1, :]
            lm = jnp.exp(jnp.where(tril, seg, NEG))
            wh = (mg[g] * lm).astype(MXU)
            xm = jnp.where(lo if hh == 0 else ~lo, xq, 0.0).astype(MXU)
            part = _dot(wh, xm)
            acc = part if acc is None else acc + part
            lms.append(lm)
            whs.append(wh)
        yd.append(acc)
    yd = jnp.concatenate(yd, axis=1)
    sb = s_prev.astype(MXU)
    yo = jnp.concatenate([_dot(cmb[g], sb[:, g * 512:(g + 1) * 512]) for g in range(2)], axis=1) * ecse
    xdec = (xdt * dece).astype(MXU)
    states = jnp.concatenate([_dot_tn(bmb[g], xdec[:, g * 512:(g + 1) * 512]) for g in range(2)], axis=1)
    s_next = s_prev * cde + states
    ypre = yd + yo + de * xs
    sig_z = _sigmoid(z)
    yg = ypre * z * sig_z
    outs, yhat, rr = [], [], []
    for g in range(2):
        sl = slice(g * 512, (g + 1) * 512)
        yh, r = _rms(yg[:, sl])
        yhat.append(yh)
        rr.append(r)
        outs.append(yh * ng[:, sl])
    return dict(sig_c=sig_c, xa=xa, xs=xs, bmb=bmb, cmb=cmb, dtpre=dtpre, dt=dt, a_neg=a_neg,
                cs=cs, last=last, ecs=ecs, dec=dec, dte=dte, ecse=ecse, dece=dece, cde=cde, de=de, xdt=xdt,
                mg=mg, lms=lms, whs=whs, lo=lo, yo=yo, sb=sb, xdec=xdec, s_next=s_next, ypre=ypre, sig_z=sig_z,
                yhat=yhat, rr=rr, out=jnp.concatenate(outs, axis=1))


def _ssd_fwd(pz, pxbc, dtraw, cat, convw, convb, dtb, alog, dskip, ng, ex, ltri, nb):
    T = pz.shape[0]
    S = T // nb
    nch = S // CH
    ns = _SEQS_PER_STEP if nb % _SEQS_PER_STEP == 0 else 1

    def body(z_ref, xbc_ref, halo_ref, dt_ref, cw_ref, cb_ref, dtb_ref, al_ref, ds_ref, ng_ref, ex_ref, lt_ref,
             cat_in_ref, yb_ref, sall_ref, conv_ref, s_ref):
        del cat_in_ref
        c = pl.program_id(1)

        @pl.when(c == 0)
        def _():
            s_ref[...] = jnp.zeros_like(s_ref)

        for i in range(ns):
            halo = jnp.where(c == 0, 0.0, halo_ref[i])
            s_prev = s_ref[i]
            sall_ref[i, 0] = s_prev
            conv = _conv(xbc_ref[i], halo, cw_ref[...], cb_ref[...])
            conv_ref[i] = conv
            f = _ssd_fwd_vals(z_ref[i], conv, dt_ref[i], dtb_ref[...], al_ref[...], ds_ref[...], ng_ref[...],
                              ex_ref[...], lt_ref[...], s_prev)
            s_ref[i] = f["s_next"]
            yb_ref[i] = f["out"].astype(MXU)

    def seq(width, col=0):
        return pl.BlockSpec((ns, CH, width), lambda b, c: (b, c, col))

    cat, sall, conv = pl.pallas_call(
        body, grid=(nb // ns, nch), name="ssd_fwd",
        in_specs=[seq(1024), seq(CONV_CH),
                  pl.BlockSpec((ns, 8, CONV_CH), lambda b, c: (b, jnp.maximum(c * (CH // 8) - 1, 0), 0)),
                  seq(128),
                  _const((8, CONV_CH)), _const((1, CONV_CH)), _const((1, 128)), _const((1, 128)), _const((1, 1024)),
                  _const((1, 1024)), _const((128, 1024)), _const((CH, CH)), _ANY],
        out_specs=[seq(1024, 1), pl.BlockSpec((ns, 1, 128, 1024), lambda b, c: (b, c, 0, 0)), seq(CONV_CH)],
        out_shape=[_sds((nb, S, 2048), MXU), _sds((nb, nch, 128, 1024), F32), _sds((nb, S, CONV_CH), F32)],
        scratch_shapes=[pltpu.VMEM((ns, 128, 1024), F32)],
        input_output_aliases={12: 0},
        compiler_params=_cp(2),
    )(pz.reshape(nb, S, 1024), pxbc.reshape(nb, S, CONV_CH), pxbc.reshape(nb, S, CONV_CH), dtraw.reshape(nb, S, 128),
      convw, convb, dtb, alog, dskip, ng, ex, ltri, cat.reshape(nb, S, 2048))
    return cat.reshape(T, 2048), sall, conv.reshape(T, CONV_CH)


def _outproj_ff1(cat, wo, x, g, w1, tm):
    T = x.shape[0]

    def body(cat_ref, wo_ref, x_ref, g_ref, w1_ref, h1_ref, hn_ref, hid_ref):
        h1 = x_ref[...] + _dot(cat_ref[...], wo_ref[...])
        h1_ref[...] = h1
        hn = (_rms(h1)[0] * g_ref[...]).astype(MXU)
        hn_ref[...] = hn
        for n in range(4):
            hid_ref[:, n * 1024:(n + 1) * 1024] = jnp.maximum(_dot(hn, w1_ref[n]), 0.0).astype(MXU)

    return pl.pallas_call(
        body, grid=(T // tm,), name="outproj_ff1",
        in_specs=[_rows(tm, 2048), _const((2048, D)), _rows(tm, D), _const((1, D)), _const((4, D, 1024))],
        out_specs=[_rows(tm, D), _rows(tm, D), _rows(tm, DFF)],
        out_shape=[_sds((T, D), F32), _sds((T, D), MXU), _sds((T, DFF), MXU)],
        compiler_params=_cp(),
    )(cat, wo, x, g, w1)


def _sq(hid):
    h = hid.astype(F32)
    return (h * h).astype(MXU)


def _ff2_tail(hid, w2, h1, g_ple, p, tgt, wg, wp, gf, tm):
    T = h1.shape[0]

    def body(hid_ref, w2_ref, h1_ref, g_ref, p_ref, t_ref, wg_ref, wp_ref, gf_ref,
             hp_ref, dgl_ref, dpe_ref, dh2_ref, dh2b_ref, loss_ref, dgf_ref, dg_ref):
        @pl.when(pl.program_id(0) == 0)
        def _():
            loss_ref[...] = jnp.zeros_like(loss_ref)
            dgf_ref[...] = jnp.zeros_like(dgf_ref)
            dg_ref[...] = jnp.zeros_like(dg_ref)

        h2 = h1_ref[...] + _dot(_sq(hid_ref[...]), w2_ref[...])
        h2h, r2 = _rms(h2)
        g_ple = g_ref[...]
        hp = (h2h * g_ple).astype(MXU)
        hp_ref[...] = hp
        gate = _sigmoid(_dot(hp, wg_ref[...]))
        pb = p_ref[...].astype(MXU)
        pe = jnp.concatenate([_dot(pb, wp_ref[k]) for k in range(4)], axis=1)
        h3 = h2 + gate * pe
        hh, r = _rms(h3)
        gf = gf_ref[...]
        diff = hh * gf - t_ref[...]
        loss_ref[...] += 0.5 * jnp.sum(jnp.mean(diff * diff, axis=-1, keepdims=True))
        dout = diff * (1.0 / D)
        dgf_ref[...] += jnp.sum(dout * hh, axis=0, keepdims=True)
        dh3 = _rms_bwd(dout, hh, r, gf)
        dgl = (dh3 * pe * gate * (1.0 - gate)).astype(MXU)
        dgl_ref[...] = dgl
        dpe_ref[...] = (dh3 * gate).astype(MXU)
        dhp = _dot_nt(dgl, wg_ref[...])
        dg_ref[...] += jnp.sum(dhp * h2h, axis=0, keepdims=True)
        dh2 = dh3 + _rms_bwd(dhp, h2h, r2, g_ple)
        dh2_ref[...] = dh2
        dh2b_ref[...] = dh2.astype(MXU)

    return pl.pallas_call(
        body, grid=(T // tm,), name="ff2_tail",
        in_specs=[_rows(tm, DFF), _const((DFF, D)), _rows(tm, D), _const((1, D)), _rows(tm, DPLE), _rows(tm, D),
                  _const((D, D)), _const((4, DPLE, 256)), _const((1, D))],
        out_specs=[_rows(tm, D), _rows(tm, D), _rows(tm, D), _rows(tm, D), _rows(tm, D), _const((8, 128)),
                   _const((1, D)), _const((1, D))],
        out_shape=[_sds((T, D), MXU), _sds((T, D), MXU), _sds((T, D), MXU), _sds((T, D), F32), _sds((T, D), MXU),
                   _sds((8, 128), F32), _sds((1, D), F32), _sds((1, D), F32)],
        compiler_params=_cp(),
    )(hid, w2, h1, g_ple, p, tgt, wg, wp, gf)


def _ff2_bwd(dh2b, w2, hid, tm):
    T = hid.shape[0]

    def body(dh2b_ref, w2_ref, hid_ref, dpre_ref):
        d = dh2b_ref[...]
        for n in range(DFF // 1024):
            sl = slice(n * 1024, (n + 1) * 1024)
            da = _dot_nt(d, w2_ref[sl, :])
            dpre_ref[:, sl] = (2.0 * da * hid_ref[:, sl].astype(F32)).astype(MXU)

    return pl.pallas_call(
        body, grid=(T // tm,), name="ff2_bwd",
        in_specs=[_rows(tm, D), _const((DFF, D)), _rows(tm, DFF)],
        out_specs=_rows(tm, DFF),
        out_shape=_sds((T, DFF), MXU),
        compiler_params=_cp(),
    )(dh2b, w2, hid)


def _ff1_bwd(dpre, w1, dh2, h1, g, tm, after):
    T = h1.shape[0]

    def body(dpre_ref, w1_ref, dh2_ref, h1_ref, g_ref, dh1_ref, dh1b_ref, dg_ref):
        @pl.when(pl.program_id(0) == 0)
        def _():
            dg_ref[...] = jnp.zeros_like(dg_ref)

        dhn = _dot_nt(dpre_ref[:, 0:1024], w1_ref[0])
        for k in range(1, 4):
            dhn = dhn + _dot_nt(dpre_ref[:, k * 1024:(k + 1) * 1024], w1_ref[k])
        hh, r = _rms(h1_ref[...])
        dg_ref[...] += jnp.sum(dhn * hh, axis=0, keepdims=True)
        dh1 = dh2_ref[...] + _rms_bwd(dhn, hh, r, g_ref[...])
        dh1_ref[...] = dh1
        dh1b_ref[...] = dh1.astype(MXU)

    return pl.pallas_call(
        _after(5, body), grid=(T // tm,), name="ff1_bwd",
        in_specs=[_rows(tm, DFF), _const((4, D, 1024)), _rows(tm, D), _rows(tm, D), _const((1, D)), _ANY],
        out_specs=[_rows(tm, D), _rows(tm, D), _const((1, D))],
        out_shape=[_sds((T, D), F32), _sds((T, D), MXU), _sds((1, D), F32)],
        compiler_params=_cp(),
    )(dpre, w1, dh2, h1, g, after)


def _outproj_bwd(dh1b, wo, tm):
    T = dh1b.shape[0]

    def body(d_ref, wo_ref, dcat_ref):
        d = d_ref[...]
        dcat_ref[:, 0:1024] = _dot_nt(d, wo_ref[0:1024, :])
        dcat_ref[:, 1024:2048] = _dot_nt(d, wo_ref[1024:2048, :])

    return pl.pallas_call(
        body, grid=(T // tm,), name="outproj_bwd",
        in_specs=[_rows(tm, D), _const((2048, D))],
        out_specs=_rows(tm, 2048),
        out_shape=_sds((T, 2048), F32),
        compiler_params=_cp(),
    )(dh1b, wo)


def _gmlp_bwd(uv, dcat, gv, ws, bst, gout, wm):
    T = uv.shape[0]
    nck = 4 if T % (4 * CH) == 0 else 1
    tb = nck * CH

    def body(uv_ref, dya_ref, gv_ref, ws_ref, bst_ref, gout_ref, wuv_ref, duv_ref, dgv_ref, dws_ref, dbst_ref,
             dgo_ref, dxn_ref):
        @pl.when(pl.program_id(0) == 0)
        def _():
            dgv_ref[...] = jnp.zeros_like(dgv_ref)
            dws_ref[...] = jnp.zeros_like(dws_ref)
            dbst_ref[...] = jnp.zeros_like(dbst_ref)
            dgo_ref[...] = jnp.zeros_like(dgo_ref)

        for k in range(nck):
            chunk(slice(k * CH, (k + 1) * CH), uv_ref, dya_ref, gv_ref, ws_ref, bst_ref, gout_ref, duv_ref,
                  dgv_ref, dws_ref, dbst_ref, dgo_ref)
        dxn_ref[...] = _dot_nt(duv_ref[...], wuv_ref[...])

    def chunk(rows, uv_ref, dya_ref, gv_ref, ws_ref, bst_ref, gout_ref, duv_ref, dgv_ref, dws_ref, dbst_ref,
              dgo_ref):
        gv = gv_ref[...]
        f = _gmlp_fwd_vals(uv_ref[rows, 0:1024], uv_ref[rows, 1024:2048], gv, ws_ref, bst_ref[...], gout_ref[...])
        dya = dya_ref[rows, :]
        dgo_ref[...] += jnp.sum(dya * f["yhat"], axis=0, keepdims=True)
        dy = _rms_bwd(dya, f["yhat"], f["ry"], gout_ref[...])
        lane = lax.broadcasted_iota(jnp.int32, (CH, 128), 1)
        dbs = jnp.zeros((CH, 128), F32)
        dug, dvg, dgvs = [], [], []
        for h in range(GM_HEADS):
            sl = slice(h * 128, (h + 1) * 128)
            vhat, rv, vn, wt, mixed = f["heads"][h]
            dyh = dy[:, sl]
            dug.append(dyh * mixed)
            dmixed = dyh * f["ug"][:, sl]
            dmb = dmixed.astype(MXU)
            dws_ref[h] += jnp.where(f["tril"], _dot_nt(dmb, vn), 0.0)
            dbs = dbs + jnp.where(lane == h, jnp.sum(dmixed, axis=1, keepdims=True), 0.0)
            dvn = _dot_tn(wt.astype(MXU), dmb)
            dgvs.append(jnp.sum(dvn * vhat, axis=0, keepdims=True))
            dvg.append(_rms_bwd(dvn, vhat, rv, gv[:, sl]))
        dbst_ref[...] += dbs
        dgv_ref[...] += jnp.concatenate(dgvs, axis=1)
        duv_ref[rows, 0:1024] = (jnp.concatenate(dug, axis=1) * f["dug"]).astype(MXU)
        duv_ref[rows, 1024:2048] = (jnp.concatenate(dvg, axis=1) * f["dvg"]).astype(MXU)

    return pl.pallas_call(
        body, grid=(T // tb,), name="gmlp_bwd",
        in_specs=[_rows(tb, 2048), _rows(tb, 1024, 0), _const((1, 1024)),
                  _const((GM_HEADS, CH, CH)), _const((CH, 128)), _const((1, 1024)), _const((D, 2048))],
        out_specs=[_rows(tb, 2048), _const((1, 1024)), _const((GM_HEADS, CH, CH)), _const((CH, 128)),
                   _const((1, 1024)), _rows(tb, D)],
        out_shape=[_sds((T, 2048), MXU), _sds((1, 1024), F32), _sds((GM_HEADS, CH, CH), F32), _sds((CH, 128), F32),
                   _sds((1, 1024), F32), _sds((T, D), F32)],
        compiler_params=_cp(),
    )(uv, dcat, gv, ws, bst, gout, wm)


def _ssd_bwd(pz, pxbc, conv, dtraw, sall, dcat, convw, dtb, alog, dskip, ng, ex, ltri, ext, nb, after):
    T = pz.shape[0]
    S = T // nb
    nch = S // CH
    ns = _SEQS_PER_STEP if nb % _SEQS_PER_STEP == 0 else 1

    def seq(width, col=0):
        return pl.BlockSpec((ns, CH, width), lambda b, c: (b, nch - 1 - c, col))

    in_specs = [
        seq(1024), seq(CONV_CH), seq(CONV_CH), seq(128),
        _const((8, CONV_CH)), _const((1, 128)), _const((1, 128)), _const((1, 1024)),
        _const((1, 1024)), _const((128, 1024)), _const((CH, CH)),
        _const((1024, 128)),
        pl.BlockSpec((ns, 1, 128, 1024), lambda b, c: (b, nch - 1 - c, 0, 0)),
        seq(1024, 1),
        _ANY,
    ]

    def body(z_ref, xbc_ref, conv_ref, dt_ref, cw_ref, dtb_ref, al_ref, ds_ref, ng_ref, ex_ref, lt_ref,
             ext_ref, sall_ref, dyb_ref,
             dssd_ref, ddt_ref, dcw_ref, dcb_ref, ddtb_ref, dal_ref, dds_ref, dng_ref,
             dst_ref, dnext_ref, ddse_ref):
        b = pl.program_id(0)
        c = pl.program_id(1)

        @pl.when((b == 0) & (c == 0))
        def _():
            for r in (dcw_ref, dcb_ref, ddtb_ref, dal_ref, dds_ref, dng_ref, ddse_ref):
                r[...] = jnp.zeros_like(r)

        @pl.when(c == 0)
        def _():
            dst_ref[...] = jnp.zeros_like(dst_ref)
            dnext_ref[...] = jnp.zeros_like(dnext_ref)

        ex = ex_ref[...]
        ext = ext_ref[...]
        cw = cw_ref[...]
        ng = ng_ref[...]
        for i in range(ns):
            one_chunk(i, ex, ext, cw, ng, z_ref, xbc_ref, conv_ref, dt_ref, dtb_ref, al_ref, ds_ref, lt_ref, sall_ref,
                      dyb_ref, dssd_ref, ddt_ref, dcw_ref, dcb_ref, ddtb_ref, dal_ref, dng_ref, dst_ref, dnext_ref,
                      ddse_ref)

        @pl.when((b == nb // ns - 1) & (c == nch - 1))
        def _():
            dds_ref[...] = _dot_01(jnp.broadcast_to(ddse_ref[...], (8, 1024)), ext)[0:1]

    def one_chunk(i, ex, ext, cw, ng, z_ref, xbc_ref, conv_ref, dt_ref, dtb_ref, al_ref, ds_ref, lt_ref, sall_ref,
                  dyb_ref, dssd_ref, ddt_ref, dcw_ref, dcb_ref, ddtb_ref, dal_ref, dng_ref, dst_ref, dnext_ref,
                  ddse_ref):
        z = z_ref[i]
        s_prev = sall_ref[i, 0]
        conv = conv_ref[i]
        f = _ssd_fwd_vals(z, conv, dt_ref[i], dtb_ref[...], al_ref[...], ds_ref[...], ng, ex, lt_ref[...], s_prev)
        xs, xdt, cs, dec, dt = f["xs"], f["xdt"], f["cs"], f["dec"], f["dt"]
        dyb = dyb_ref[i]
        dyg, dngs = [], []
        for g in range(2):
            sl = slice(g * 512, (g + 1) * 512)
            dngs.append(jnp.sum(dyb[:, sl] * f["yhat"][g], axis=0, keepdims=True))
            dyg.append(_rms_bwd(dyb[:, sl], f["yhat"][g], f["rr"][g], ng[:, sl]))
        dng_ref[...] += jnp.concatenate(dngs, axis=1)
        dyg = jnp.concatenate(dyg, axis=1)
        sig_z = f["sig_z"]
        silu_z = z * sig_z
        dy = dyg * silu_z
        dz = dyg * f["ypre"] * (sig_z + silu_z * (1.0 - sig_z))
        ddse_ref[...] += jnp.sum(dy * xs, axis=0, keepdims=True)
        dxs = dy * f["de"]
        dye = dy * f["ecse"]
        dyeb = dye.astype(MXU)
        dst = dst_ref[i]
        dstb = dst.astype(MXU)
        bmb, cmb, sb, xdec = f["bmb"], f["cmb"], f["sb"], f["xdec"]
        u = jnp.concatenate([_dot(bmb[g], dstb[:, g * 512:(g + 1) * 512]) for g in range(2)], axis=1)
        dxdt = [u[:, q * 128:(q + 1) * 128] * f["dece"][:, q * 128:(q + 1) * 128] for q in range(8)]
        per_head = _dot_01(jnp.concatenate(
            [dy * f["yo"], u * xdt, jnp.broadcast_to(jnp.sum(dst * s_prev, axis=0, keepdims=True), (8, 1024))],
            axis=0), ext)
        dcs = per_head[0:CH]
        t = per_head[CH:2 * CH] * dec
        dcd = per_head[2 * CH:2 * CH + 1]
        row = lax.broadcasted_iota(jnp.int32, (CH, 128), 0)
        lane = lax.broadcasted_iota(jnp.int32, (CH, 128), 1)
        cd = jnp.exp(f["last"])
        dcs = dcs - t + jnp.where(row == CH - 1, jnp.sum(t, axis=0, keepdims=True) + dcd * cd, 0.0)
        dcst = jnp.zeros((128, CH), F32)
        lo = f["lo"]
        dbm, dcm, ds_prev = [], [], []
        for g in range(2):
            sl = slice(g * 512, (g + 1) * 512)
            dmg = jnp.zeros((CH, CH), F32)
            for q in range(4 * g, 4 * g + 4):
                dyq = dy[:, q * 128:(q + 1) * 128]
                xq = xdt[:, q * 128:(q + 1) * 128].astype(MXU)
                for hh in range(2):
                    h = 2 * q + hh
                    m = lo if hh == 0 else ~lo
                    dym = jnp.where(m, dyq, 0.0).astype(MXU)
                    gh = _dot_nt(dym, xq)
                    gl = gh * f["lms"][h]
                    dmg = dmg + gl
                    qh = gl * f["mg"][g]
                    dcs = dcs + jnp.where(lane == h, jnp.sum(qh, axis=1, keepdims=True), 0.0)
                    dcst = dcst - jnp.where(row == h, jnp.sum(qh, axis=0, keepdims=True), 0.0)
                    dxdt[q] = dxdt[q] + _dot_tn(f["whs"][h], dym)
            dmgb = dmg.astype(MXU)
            dcm.append(_dot(dmgb, bmb[g]) + _dot_nt(dyeb[:, sl], sb[:, sl]))
            dbm.append(_dot_tn(dmgb, cmb[g]) + _dot_nt(xdec[:, sl], dstb[:, sl]))
            ds_prev.append(_dot_tn(cmb[g], dyeb[:, sl]))
        dst_ref[i] = jnp.concatenate(ds_prev, axis=1) + dst * f["cde"]
        dcs = dcs + dcst.T
        da = _dot_hi(lt_ref[...].T, dcs)
        dxdt = jnp.concatenate(dxdt, axis=1)
        a_neg = f["a_neg"]
        ddt = da * a_neg + _dot_01(dxdt * xs, ext)
        dal_ref[...] += jnp.sum(da * dt, axis=0, keepdims=True) * a_neg
        dxs = dxs + dxdt * f["dte"]
        ddtraw = jnp.where(lane < SSD_HEADS, ddt * _sigmoid(f["dtpre"]), 0.0)
        ddtb_ref[...] += jnp.sum(ddtraw, axis=0, keepdims=True)
        ddt_ref[i] = ddtraw.astype(MXU)
        dxa = jnp.concatenate([dxs, dbm[0], dbm[1], dcm[0], dcm[1]], axis=1)
        sig_c = f["sig_c"]
        dconv = dxa * (sig_c + f["xa"] * (1.0 - sig_c))
        dcb_ref[...] += jnp.sum(dconv, axis=0, keepdims=True)
        xbc = xbc_ref[i]
        dcw_ref[3:4, :] += jnp.sum(dconv * xbc, axis=0, keepdims=True)
        dxbc = cw[3:4] * dconv
        for j, up in zip((1, 2, 3), _shifts_up(dconv, dnext_ref[i])):
            dcw_ref[3 - j:4 - j, :] += jnp.sum(up * xbc, axis=0, keepdims=True)
            dxbc = dxbc + cw[3 - j:4 - j] * up
        dnext_ref[i] = dconv[0:8]
        dssd_ref[i, :, 0:1024] = dz.astype(MXU)
        dssd_ref[i, :, 1024:2560] = dxbc.astype(MXU)

    dssd, ddt, *small = pl.pallas_call(
        _after(14, body), grid=(nb // ns, nch), name="ssd_bwd",
        in_specs=in_specs,
        out_specs=[seq(2560), seq(128),
                   _const((8, CONV_CH)), _const((1, CONV_CH)), _const((1, 128)), _const((1, 128)), _const((1, 128)),
                   _const((1, 1024))],
        out_shape=[_sds((nb, S, 2560), MXU), _sds((nb, S, 128), MXU), _sds((8, CONV_CH), F32),
                   _sds((1, CONV_CH), F32), _sds((1, 128), F32), _sds((1, 128), F32), _sds((1, 128), F32),
                   _sds((1, 1024), F32)],
        scratch_shapes=[pltpu.VMEM((ns, 128, 1024), F32), pltpu.VMEM((ns, 8, CONV_CH), F32),
                        pltpu.VMEM((1, 1024), F32)],
        compiler_params=_cp(2),
    )(pz.reshape(nb, S, 1024), pxbc.reshape(nb, S, CONV_CH), conv.reshape(nb, S, CONV_CH), dtraw.reshape(nb, S, 128),
      convw, dtb, alog, dskip, ng, ex, ltri, ext, sall, dcat.reshape(nb, S, 2048), after)
    return (dssd.reshape(T, 2560), ddt.reshape(T, 128), *small)


def _inproj_bwd(dxn_uv, dssd, ddt, wm, wdt, dh1, x, g, tm, after):
    T = x.shape[0]

    def body(dxnuv_ref, dssd_ref, ddt_ref, wm_ref, wdt_ref, dh1_ref, x_ref, g_ref, dx_ref, dg_ref):
        @pl.when(pl.program_id(0) == 0)
        def _():
            dg_ref[...] = jnp.zeros_like(dg_ref)

        dxn = (dxnuv_ref[...] + _dot_nt(dssd_ref[...], wm_ref[:, 2048:N_MAIN])
               + _dot_nt(ddt_ref[...], wdt_ref[...]))
        xh, r = _rms(x_ref[...])
        dg_ref[...] += jnp.sum(dxn * xh, axis=0, keepdims=True)
        dx_ref[...] = dh1_ref[...] + _rms_bwd(dxn, xh, r, g_ref[...])

    return pl.pallas_call(
        _after(8, body), grid=(T // tm,), name="inproj_bwd",
        in_specs=[_rows(tm, D), _rows(tm, 2560), _rows(tm, 128), _const((D, N_MAIN)), _const((D, 128)),
                  _rows(tm, D), _rows(tm, D), _const((1, D)), _ANY],
        out_specs=[_rows(tm, D), _const((1, D))],
        out_shape=[_sds((T, D), F32), _sds((1, D), F32)],
        compiler_params=_cp(),
    )(dxn_uv, dssd, ddt, wm, wdt, dh1, x, g, after)


def _matmul_tn(a, b, name, a_fn=None):
    T, M = a.shape
    N = b.shape[1]
    tm = min(M, 1024)
    tn = 1280 if N == 2560 else min(N, 1024)
    tk = min(T, 2048)

    def body(a_ref, b_ref, o_ref, acc_ref):
        k = pl.program_id(2)

        @pl.when(k == 0)
        def _():
            acc_ref[...] = jnp.zeros_like(acc_ref)

        av = a_ref[...]
        if a_fn is not None:
            av = a_fn(av)
        acc_ref[...] += _dot_tn(av, b_ref[...])

        @pl.when(k == T // tk - 1)
        def _():
            o_ref[...] = acc_ref[...].astype(o_ref.dtype)

    return pl.pallas_call(
        body, grid=(M // tm, N // tn, T // tk), name=name,
        in_specs=[pl.BlockSpec((tk, tm), lambda i, j, k: (k, i)), pl.BlockSpec((tk, tn), lambda i, j, k: (k, j))],
        out_specs=pl.BlockSpec((tm, tn), lambda i, j, k: (i, j)),
        out_shape=_sds((M, N), GRAD),
        scratch_shapes=[pltpu.VMEM((tm, tn), F32)],
        compiler_params=_cp(3),
    )(a, b)


def _adamw_vals(w, g, m, v):
    m = B1 * m + (1.0 - B1) * g
    v = B2 * v + (1.0 - B2) * (g * g)
    m_hat = m / (1.0 - B1 ** STEP)
    v_hat = v / (1.0 - B2 ** STEP)
    return -LR * (m_hat / (jnp.sqrt(v_hat) + ADAM_EPS) + WD * w), m, v


_PARTS = 4


def _adamw_halves(items, name):
    n = len(items)

    def body(*refs):
        mine = (pl.program_id(0) // _PARTS) == lax.axis_index("c")
        for k in range(n):
            w_ref, own_ref, oth_ref, m_ref, v_ref = refs[5 * k:5 * k + 5]
            g_ref, d_ref, mo_ref, vo_ref = refs[5 * n + 4 * k:5 * n + 4 * k + 4]
            g = jnp.where(mine, own_ref[...], oth_ref[...])
            g_ref[...] = g
            d_ref[...], mo_ref[...], vo_ref[...] = _adamw_vals(w_ref[...], g, m_ref[...], v_ref[...])

    in_specs, out_specs, out_shape = [], [], []
    for w, *_ in items:
        R, C = w.shape
        full = _rows(R // (2 * _PARTS), C)
        part = pl.BlockSpec((R // (2 * _PARTS), C), lambda i: (i % _PARTS, 0))
        in_specs += [full, part, part, full, full]
        out_specs += [full] * 4
        out_shape += [_sds((R, C), F32)] * 4
    res = pl.pallas_call(
        body, grid=(2 * _PARTS,), name=name, in_specs=in_specs, out_specs=out_specs, out_shape=out_shape,
        compiler_params=_cp(),
    )(*[a for item in items for a in item])
    return [tuple(res[4 * k:4 * k + 4]) for k in range(n)]


_TJ = 128


def _adamw_transposed(w, own, other, m, v, name):
    C, _, R = w.shape

    def body(w_ref, own_ref, oth_ref, m_ref, v_ref, g_ref, d_ref, mo_ref, vo_ref):
        first = lax.axis_index("c") == 0
        g = jnp.concatenate([jnp.where(first, own_ref[...], oth_ref[...]),
                             jnp.where(first, oth_ref[...], own_ref[...])], axis=0).T
        d, mo, vo = _adamw_vals(w_ref[:, 0, :], g, m_ref[:, 0, :], v_ref[:, 0, :])
        for ref, val in ((g_ref, g), (d_ref, d), (mo_ref, mo), (vo_ref, vo)):
            ref[:, 0, :] = val

    cols = pl.BlockSpec((_TJ, 1, R), lambda j: (j, 0, 0))
    half = pl.BlockSpec((R // 2, _TJ), lambda j: (0, j))
    return pl.pallas_call(
        body, grid=(pl.cdiv(C, _TJ),), name=name,
        in_specs=[cols, half, half, cols, cols], out_specs=[cols] * 4, out_shape=[_sds((C, 1, R), F32)] * 4,
        compiler_params=_cp(),
    )(w, own, other, m, v)


def _lanes(rows):
    return jnp.concatenate([rows[i:i + 1, :] for i in range(rows.shape[0])], axis=1)


def _small_update(a, packs):
    names = [n for n, _ in _SMALL]
    where = {}
    for k, (pnames, rows, _) in enumerate(packs):
        o = 0
        for n, r in zip(pnames, rows):
            where[n] = (k, o, r)
            o += r
    view = {n: (1, 1024) for n in names}
    view.update(gm_ws=(1024, 128), gm_bs=(8, 128), ssd_conv_w=(4, 384), ssd_conv_b=(1, CONV_CH),
                ssd_dt_bias=(1, 16), ssd_a_log=(1, 16), ssd_d=(1, 16))
    npk = len(packs)

    def body(*refs):
        tots = []
        for k in range(npk):
            tot = refs[k][0]
            for d in range(1, 8):
                tot = tot + refs[k][d]
            tots.append(tot)
        ins, outs = refs[npk:npk + 3 * len(names)], refs[npk + 3 * len(names):]
        chip = 2 * lax.axis_index("x") + lax.axis_index("y")
        for i, n in enumerate(names):
            k, o, r = where[n]
            blk = tots[k][o:o + r, :]
            if n == "gm_ws":
                g = blk
            elif n == "gm_bs":
                g = blk[0:8]
            elif view[n] == (1, 16):
                g = blk[0:1, 0:16]
            elif n == "ssd_conv_w":
                taps = jnp.concatenate([_lanes(blk[12 * t:12 * t + 12]) for t in range(4)], axis=0)
                g = taps[:, 0:384]
                for c in range(1, 4):
                    g = jnp.where(chip == c, taps[:, 384 * c:384 * (c + 1)], g)
            else:
                g = _lanes(blk[0:view[n][1] // 128])
            d, mo, vo = _adamw_vals(ins[3 * i][...], g, ins[3 * i + 1][...], ins[3 * i + 2][...])
            for j, val in enumerate((g, d, mo, vo)):
                outs[4 * i + j][...] = val
        k, o, _ = where["loss"]
        outs[-1][...] = tots[k][o:o + 1, 0:1]

    ins = [a[pre + n].reshape(view[n]) for n in names for pre in ("", "m_", "v_")]
    res = pl.pallas_call(
        body, name="small_update",
        out_shape=[_sds(view[n], F32) for n in names for _ in range(4)] + [_sds((1, 1), F32)],
    )(*[slots for _, _, slots in packs], *ins)
    return {n: tuple(r.reshape(a[n].shape) for r in res[4 * i:4 * i + 4]) for i, n in enumerate(names)}, res[-1]


def _sum_slots(items, kh, name):
    n = len(items)
    in_specs, out_specs, out_shape = [], [], []
    for slots, src, kind, (R, C) in items:
        tr = R // (2 * _PARTS)
        if kind == "slab":
            src_spec = pl.BlockSpec((1, tr, C), lambda i, kh: (kh[0], kh[1] * _PARTS + i, 0))
        elif kind == "rows":
            src_spec = pl.BlockSpec((tr, C), lambda i, kh: (kh[0] * (2 * _PARTS) + kh[1] * _PARTS + i, 0))
        else:
            src_spec = pl.BlockSpec((tr, C), lambda i, kh: (kh[1] * _PARTS + i, kh[0]))
        in_specs += [pl.BlockSpec((8, tr, C), lambda i, kh: (0, i, 0)), src_spec]
        out_specs.append(pl.BlockSpec((tr, C), lambda i, kh: (i, 0)))
        out_shape.append(_sds((R // 2, C), F32))

    def body(kh_ref, *refs):
        me = 2 * kh_ref[0] + kh_ref[1]
        for k, (_, _, kind, _) in enumerate(items):
            s_ref, own_ref, o_ref = refs[2 * k], refs[2 * k + 1], refs[2 * n + k]
            acc = (own_ref[0] if kind == "slab" else own_ref[...]).astype(F32)
            for j in range(1, 8):
                acc = acc + s_ref[me ^ j].astype(F32)
            o_ref[...] = acc

    return pl.pallas_call(
        body, name=name,
        grid_spec=pltpu.PrefetchScalarGridSpec(
            num_scalar_prefetch=1, grid=(_PARTS,), in_specs=in_specs, out_specs=out_specs),
        out_shape=out_shape,
        compiler_params=_cp(),
    )(kh, *[a for slots, src, _, _ in items for a in (slots, src)])


def _assemble_w_in(slabs):
    tr = 256

    def body(s_ref, wm_ref, wdt_ref):
        full = jnp.concatenate([s_ref[k] for k in range(4)], axis=1)
        wm_ref[...] = full[:, :N_MAIN]
        wdt_ref[...] = jnp.concatenate([full[:, N_MAIN:], jnp.zeros((tr, 128 - 16), full.dtype)], axis=1)

    return pl.pallas_call(
        body, grid=(D // tr,), name="assemble_w_in",
        in_specs=[pl.BlockSpec((4, tr, 1156), lambda i: (0, i, 0))],
        out_specs=[_rows(tr, N_MAIN), _rows(tr, 128)],
        out_shape=[_sds((D, N_MAIN), slabs.dtype), _sds((D, 128), slabs.dtype)],
        compiler_params=_cp(),
    )(slabs)


def _split_dw_in(d_uv, d_ssd, d_dt):
    tr = 256

    def body(uv_ref, ssd_ref, dt_ref, o_ref):
        full = jnp.concatenate([uv_ref[...], ssd_ref[...], dt_ref[:, 0:16]], axis=1)
        for k in range(4):
            o_ref[k] = full[:, 1156 * k:1156 * (k + 1)]

    return pl.pallas_call(
        body, grid=(D // tr,), name="split_dw_in",
        in_specs=[_rows(tr, 2048), _rows(tr, 2560), _rows(tr, 128)],
        out_specs=pl.BlockSpec((4, tr, 1156), lambda i: (0, i, 0)),
        out_shape=_sds((4, D, 1156), d_uv.dtype),
        compiler_params=_cp(),
    )(d_uv, d_ssd, d_dt)


def _cast_w_in(w, kh):
    C, _, R = w.shape

    def body(kh_ref, w_ref, o_ref):
        o_ref[0] = w_ref[:, 0, :].T.astype(BF16)

    return pl.pallas_call(
        body, name="cast_w_in",
        grid_spec=pltpu.PrefetchScalarGridSpec(
            num_scalar_prefetch=1, grid=(pl.cdiv(C, _TJ),),
            in_specs=[pl.BlockSpec((_TJ, 1, R), lambda j, kh: (j, 0, 0))],
            out_specs=pl.BlockSpec((1, R, _TJ), lambda j, kh: (kh[0], 0, j))),
        out_shape=_sds((4, R, C), BF16),
        compiler_params=_cp(),
    )(kh, w)


def _cast_into_slot(ws, kh, name):
    n = len(ws)

    def body(kh_ref, *refs):
        for k in range(n):
            refs[n + k][0] = refs[k][...].astype(BF16)

    return pl.pallas_call(
        body, name=name,
        grid_spec=pltpu.PrefetchScalarGridSpec(
            num_scalar_prefetch=1, grid=(_PARTS,),
            in_specs=[pl.BlockSpec((w.shape[0] // _PARTS, w.shape[1]), lambda i, kh: (i, 0)) for w in ws],
            out_specs=[pl.BlockSpec((1, w.shape[0] // _PARTS, w.shape[1]), lambda i, kh: (kh[0], i, 0))
                       for w in ws]),
        out_shape=[_sds((4,) + w.shape, BF16) for w in ws],
        compiler_params=_cp(),
    )(kh, *ws)


_ANY = pl.BlockSpec(memory_space=pl.ANY)
_CHIP_FLIPS = [(1, 0), (0, 1), (1, 1)]
_DEVICE_FLIPS = [(fx, fy, fc) for fx in (0, 1) for fy in (0, 1) for fc in (0, 1)][1:]


def _half(h, rows):
    return pl.ds(pl.multiple_of(h * rows, rows), rows)


def _remote(src, dst, ssem, rsem, to):
    return pltpu.make_async_remote_copy(src_ref=src, dst_ref=dst, send_sem=ssem, recv_sem=rsem,
                                        device_id=to, device_id_type=MESH)


def _weight_gather(bufs, conv):
    n = len(bufs)

    def body(*refs):
        conv_ref, outs, conv_out = refs[n], refs[n + 1:2 * n + 1], refs[2 * n + 1]
        send_sems, recv_sems, fsend_sems, frecv_sems, csend_sems, crecv_sems, local_sem = refs[2 * n + 2:]
        x, y, c = lax.axis_index("x"), lax.axis_index("y"), lax.axis_index("c")
        me = 2 * x + y
        halves = [_half(c, r.shape[1] // 2) for r in outs]
        others = [_half(1 - c, r.shape[1] // 2) for r in outs]
        remote = _remote
        local = [pltpu.make_async_copy(conv_ref, conv_out.at[me], local_sem)]
        for cp in local:
            cp.start()
        sends = []
        for k, (fx, fy) in enumerate(_CHIP_FLIPS):
            peer = (x ^ fx, y ^ fy, c)
            for i in range(n):
                mine = outs[i].at[me, halves[i]]
                sends.append(remote(mine, mine, send_sems.at[k * n + i], recv_sems.at[k * n + i], peer))
            sends.append(remote(conv_ref, conv_out.at[me], csend_sems.at[k], crecv_sems.at[k], peer))
        for cp in sends:
            cp.start()
        sibling = (x, y, 1 - c)
        forwards = []
        for k, (fx, fy) in enumerate(_CHIP_FLIPS):
            peer = (x ^ fx, y ^ fy, c)
            src = 2 * (x ^ fx) + (y ^ fy)
            for i in range(n):
                landed = outs[i].at[src, halves[i]]
                remote(landed, landed, send_sems.at[k * n + i], recv_sems.at[k * n + i], peer).wait_recv()
                fw = remote(landed, landed, fsend_sems.at[k * n + i], frecv_sems.at[k * n + i], sibling)
                fw.start()
                forwards.append(fw)
            remote(conv_out.at[src], conv_out.at[src], csend_sems.at[k], crecv_sems.at[k], peer).wait_recv()
        for k, (fx, fy) in enumerate(_CHIP_FLIPS):
            src = 2 * (x ^ fx) + (y ^ fy)
            for i in range(n):
                theirs = outs[i].at[src, others[i]]
                remote(theirs, theirs, fsend_sems.at[k * n + i], frecv_sems.at[k * n + i], sibling).wait_recv()
        for cp in sends + forwards:
            cp.wait_send()
        for cp in local:
            cp.wait()

    dma = pltpu.SemaphoreType.DMA
    return pl.pallas_call(
        body, name="weight_gather",
        in_specs=[_ANY] * (n + 1), out_specs=[_ANY] * (n + 1),
        out_shape=[_sds(b.shape, b.dtype) for b in bufs] + [_sds((4,) + conv.shape, conv.dtype)],
        input_output_aliases={i: i for i in range(n)},
        scratch_shapes=[dma((3 * n,)), dma((3 * n,)), dma((3 * n,)), dma((3 * n,)), dma((3,)), dma((3,)), dma],
    )(*bufs, conv)


def _piece(ref, kind, R, C, k, h):
    if kind == "slab":
        return ref.at[k, _half(h, R // 2), :]
    if kind == "rows":
        return ref.at[pl.ds(pl.multiple_of(k * R + h * (R // 2), R // 2), R // 2), :]
    return ref.at[_half(h, R // 2), pl.ds(pl.multiple_of(k * C, C), C)]


_HBM = pl.BlockSpec(memory_space=pltpu.HBM)
_SEM = pl.BlockSpec(memory_space=pltpu.SEMAPHORE)


def _split_start(name, arrays, n_copies, plan, after=None):
    n = len(arrays)
    extra = [] if after is None else [after]

    def body(*refs):
        m = n + len(extra)
        arrs, send_sems, recv_sems, token = refs[:n], refs[m], refs[m + 1], refs[-1]
        for j, (src, dst, peer) in enumerate(plan(arrs)):
            _remote(src, dst, send_sems.at[j], recv_sems.at[j], peer).start()
        token[...] = jnp.zeros_like(token)

    dma = pltpu.SemaphoreType.DMA
    res = pl.pallas_call(
        body, name=name,
        out_shape=(dma((n_copies,)), dma((n_copies,)), *[pltpu.HBM(a.shape, a.dtype) for a in arrays],
                   _sds((8, 128), F32)),
        in_specs=[_HBM] * n + [_ANY] * len(extra),
        out_specs=(_SEM, _SEM, *[_HBM] * n, pl.BlockSpec(memory_space=pltpu.VMEM)),
        input_output_aliases={i: 2 + i for i in range(n)},
        compiler_params=pltpu.CompilerParams(has_side_effects=pltpu.SideEffectType.DATAFLOW_SIDE_EFFECTING),
    )(*[pltpu.with_memory_space_constraint(a, pltpu.HBM) for a in arrays], *extra)
    return res[0], res[1], list(res[2:2 + n]), res[-1]


def _split_wait(name, arrays, send_sems, recv_sems, plan, after):
    n = len(arrays)

    def body(*refs):
        arrs, ssems, rsems = refs[:n], refs[n], refs[n + 1]
        for j, (src, dst, peer) in enumerate(plan(arrs)):
            cp = _remote(src, dst, ssems.at[j], rsems.at[j], peer)
            cp.wait_send()
            cp.wait_recv()

    return list(pl.pallas_call(
        body, name=name,
        out_shape=tuple(pltpu.HBM(a.shape, a.dtype) for a in arrays),
        in_specs=[_HBM] * n + [_SEM, _SEM, _ANY],
        out_specs=tuple([_HBM] * n),
        input_output_aliases={i: i for i in range(n)},
        compiler_params=pltpu.CompilerParams(has_side_effects=pltpu.SideEffectType.DATAFLOW_SIDE_EFFECTING),
    )(*arrays, send_sems, recv_sems, after))


def _gather_plan(n):
    def plan(bufs):
        x, y, c = lax.axis_index("x"), lax.axis_index("y"), lax.axis_index("c")
        me = 2 * x + y
        return [(bufs[i].at[me], bufs[i].at[me], (x ^ fx, y ^ fy, c)) for fx, fy in _CHIP_FLIPS for i in range(n)]

    return plan


def _reduce_plan(specs, n_small):
    n = len(specs)

    def plan(arrs):
        x, y, c = lax.axis_index("x"), lax.axis_index("y"), lax.axis_index("c")
        slot = 4 * x + 2 * y + c
        out = []
        for fx, fy, fc in _DEVICE_FLIPS:
            peer = (x ^ fx, y ^ fy, c ^ fc)
            for i, (kind, (R, C)) in enumerate(specs):
                out.append((_piece(arrs[i], kind, R, C, 2 * peer[0] + peer[1], peer[2]), arrs[n + i].at[slot], peer))
            for s in range(n_small):
                out.append((arrs[2 * n + 2 * s], arrs[2 * n + 2 * s + 1].at[slot], peer))
        return out

    return plan


def _sibling_exchange(halves, name, small=None):
    n = len(halves)
    ns = 0 if small is None else 1

    def body(*refs):
        ins, outs = refs[:n], refs[n + ns:2 * n + ns]
        send_sems, recv_sems = refs[2 * (n + ns)], refs[2 * (n + ns) + 1]
        x, y, c = lax.axis_index("x"), lax.axis_index("y"), lax.axis_index("c")
        copies = [_remote(ins[i], outs[i], send_sems.at[i], recv_sems.at[i], (x, y, 1 - c)) for i in range(n)]
        waits = list(copies)
        if ns:
            s_ref, slots_ref, ssend_sems, srecv_sems, local_sem = refs[n], refs[2 * n + 1], *refs[2 * (n + ns) + 2:]
            slot = 4 * x + 2 * y + c
            own = pltpu.make_async_copy(s_ref, slots_ref.at[slot], local_sem)
            own.start()
            for k, (fx, fy, fc) in enumerate(_DEVICE_FLIPS):
                peer = (x ^ fx, y ^ fy, c ^ fc)
                copies.append(_remote(s_ref, slots_ref.at[slot], ssend_sems.at[k], srecv_sems.at[k], peer))
                theirs = slots_ref.at[slot ^ (k + 1)]
                waits.append(_remote(theirs, theirs, ssend_sems.at[k], srecv_sems.at[k], peer))
        for cp in copies:
            cp.start()
        for cp in waits:
            cp.wait()
        if ns:
            own.wait()

    dma = pltpu.SemaphoreType.DMA
    extra_in = [] if small is None else [small]
    extra_out = [] if small is None else [_sds((8,) + small.shape, F32)]
    return pl.pallas_call(
        body, name=name,
        in_specs=[_ANY] * (n + ns), out_specs=[_ANY] * (n + ns),
        out_shape=[_sds(h.shape, h.dtype) for h in halves] + extra_out,
        scratch_shapes=[dma((n,)), dma((n,))] + ([dma((7,)), dma((7,)), dma] if ns else []),
    )(*halves, *extra_in)


_BIG = [("w_in", (1024, 1156), "slab"), ("w_out", (512, 1024), "rows"), ("w_ff1", (1024, 1024), "cols"),
        ("w_ff2", (1024, 1024), "rows"), ("w_ple_gate", (256, 1024), "rows"), ("w_ple_proj", (256, 256), "cols")]
_SMALL = [("norm_mix_g", (1, 1024)), ("gm_v_norm_g", (1, 1024)), ("gm_ws", (1, 8, 128, 128)), ("gm_bs", (1, 8, 128)),
          ("gm_out_norm_g", (1, 1024)), ("ssd_conv_w", (1, 4, 1536)), ("ssd_conv_b", (1, 1536)),
          ("ssd_dt_bias", (1, 16)), ("ssd_a_log", (1, 16)), ("ssd_d", (1, 16)), ("ssd_norm_g", (1, 1024)),
          ("norm_mlp_g", (1, 1024)), ("ple_norm_g", (1, 1024)), ("final_norm_g", (1024,))]


def _rows128(a):
    flat = a.reshape(-1)
    rows = -(-flat.shape[0] // 1024) * 8
    return jnp.pad(flat, (0, rows * 128 - flat.shape[0])).reshape(rows, 128)


def _pad_lanes(v, n=128):
    v = v.reshape(1, -1)
    return jnp.pad(v, ((0, 0), (0, n - v.shape[1])))


_SMALL_SHAPES = dict(_SMALL + [("loss", ())])
_BIG_SPECS = {n: (kind, shp) for n, shp, kind in _BIG}


class _Comm:
    def __init__(self, a, kh):
        self.a, self.kh = a, kh
        rest = _BIG[1:]
        self.bufs = {"w_in": _cast_w_in(a["w_in"].transpose(2, 0, 1), kh)}
        cast = _cast_into_slot([a[n].reshape(shp) for n, shp, _ in rest], kh, "cast_rest")
        self.bufs.update({n: c for (n, _, _), c in zip(rest, cast)})
        self.sent = []
        self.small_packs = []

    def w_in(self):
        g_win, g_cw = _weight_gather([self.bufs["w_in"]], self.a["ssd_conv_w"].reshape(4, 384))
        token = g_cw
        self.gather = {}
        for tag, names in (("out", ["w_out", "w_ff1"]), ("ff", ["w_ff2", "w_ple_gate", "w_ple_proj"])):
            plan = _gather_plan(len(names))
            ssem, rsem, thru, token = _split_start("gather_start_" + tag, [self.bufs[n] for n in names],
                                                   3 * len(names), plan, after=token)
            self.gather[tag] = (plan, ssem, rsem, thru)
        wm, wdt = _assemble_w_in(g_win)
        return wm, wdt, jnp.concatenate([g_cw[k] for k in range(4)], axis=1), token

    def rest(self, tag, after):
        plan, ssem, rsem, thru = self.gather[tag]
        got = _split_wait("gather_wait_" + tag, thru, ssem, rsem, plan, after)
        if tag == "out":
            return got[0].reshape(2048, D), got[1]
        g_w2, g_wg, g_wp = got
        return g_w2.reshape(DFF, D), g_wg.reshape(D, D), g_wp

    def send(self, tag, grads):
        big = [n for n, _, _ in _BIG if n in grads]
        small = [n for n in _SMALL_SHAPES if n in grads]
        parts = [_rows128(grads[n]) for n in small]
        rows = [s.shape[0] for s in parts]
        if not big:
            self.last_small = (tag, small, rows, jnp.concatenate(parts, axis=0))
            return None
        srcs = [grads[n] for n in big]
        lands = [lax.empty((8, _BIG_SPECS[n][1][0] // 2, _BIG_SPECS[n][1][1]), GRAD) for n in big]
        extra = []
        if small:
            pack = jnp.concatenate(parts, axis=0)
            extra = [pack, jnp.broadcast_to(pack, (8,) + pack.shape)]
        plan = _reduce_plan([_BIG_SPECS[n] for n in big], len(extra) // 2)
        n_copies = 7 * (len(big) + len(extra) // 2)
        ssem, rsem, thru, token = _split_start("reduce_start_" + tag, srcs + lands + extra, n_copies, plan)
        self.sent.append((tag, big, small, rows, plan, ssem, rsem, thru))
        return token

    def finish(self, after):
        a, results = self.a, {}

        def update(names, own, tag):
            if names == ["w_in"]:
                stag, small, rows, pack = self.last_small
                *other, slots = _sibling_exchange([own[n] for n in names], "sibling_exchange_" + tag, pack)
                self.small_packs.append((small, rows, slots))
                w, m, v = (a[k].transpose(2, 0, 1) for k in ("w_in", "m_w_in", "v_w_in"))
                raw = _adamw_transposed(w, own["w_in"], other[0], m, v, "adamw_" + tag)
                results["w_in"] = tuple(r.transpose(1, 2, 0) for r in raw)
                return raw[1]
            other = _sibling_exchange([own[n] for n in names], "sibling_exchange_" + tag)
            items = [(a[n].reshape(_BIG_SPECS[n][1]), own[n], oth, a["m_" + n].reshape(_BIG_SPECS[n][1]),
                      a["v_" + n].reshape(_BIG_SPECS[n][1])) for n, oth in zip(names, other)]
            results.update(zip(names, _adamw_halves(items, "adamw_" + tag)))
            return results[names[-1]][1]

        own, early = {}, []
        for tag, big, small, rows, plan, ssem, rsem, thru in self.sent:
            if tag == self.sent[-1][0]:
                after = update(early, own, "early")
            arrs = _split_wait("reduce_wait_" + tag, thru, ssem, rsem, plan, after)
            nb_ = len(big)
            sums = _sum_slots([(arrs[nb_ + i], arrs[i]) + _BIG_SPECS[n] for i, n in enumerate(big)], self.kh,
                              "sum_" + tag)
            own.update(zip(big, sums))
            after = sums[-1]
            early += big
            if small:
                self.small_packs.append((small, rows, arrs[2 * nb_ + 1]))
        update(self.sent[-1][1], own, "late")
        return results, self.small_packs


def _local_step(x, p, tgt, sm, comm, nb, tm):
    T = x.shape[0]
    wm, wdt, conv_w, token = comm.w_in()
    g_mix, gv, gout = sm["norm_mix_g"].reshape(1, D), sm["gm_v_norm_g"].reshape(1, D), sm["gm_out_norm_g"].reshape(1, D)
    ws = sm["gm_ws"].reshape(GM_HEADS, CH, CH)
    bst = jnp.pad(sm["gm_bs"].reshape(GM_HEADS, CH).T, ((0, 0), (0, 128 - GM_HEADS)))
    convw = jnp.pad(conv_w, ((0, 4), (0, 0)))
    convb = sm["ssd_conv_b"].reshape(1, CONV_CH)
    dtb, alog = _pad_lanes(sm["ssd_dt_bias"]), _pad_lanes(sm["ssd_a_log"])
    dskip = jnp.repeat(sm["ssd_d"].reshape(SSD_HEADS), SSD_P).reshape(1, 1024)
    ng, g_mlp, g_ple = sm["ssd_norm_g"].reshape(1, D), sm["norm_mlp_g"].reshape(1, D), sm["ple_norm_g"].reshape(1, D)
    gf = sm["final_norm_g"].reshape(1, D)
    head_of_lane = lax.broadcasted_iota(jnp.int32, (128, 1024), 1) // SSD_P
    ex = (lax.broadcasted_iota(jnp.int32, (128, 1024), 0) == head_of_lane).astype(BF16)
    ext = ex.T
    ltri = (lax.broadcasted_iota(jnp.int32, (CH, CH), 0) >= lax.broadcasted_iota(jnp.int32, (CH, CH), 1)).astype(F32)

    pz, pxbc, dtraw, xn, cat, uv = _inproj_gmlp(x, g_mix, wm, wdt, gv, ws, bst, gout, tm, token)
    cat, sall, conv = _ssd_fwd(pz, pxbc, dtraw, cat, convw, convb, dtb, alog, dskip, ng, ex, ltri, nb)
    wo, w1 = comm.rest("out", cat)
    h1, hn, hid = _outproj_ff1(cat, wo, x, g_mlp, w1, tm)
    w2, wg, wp = comm.rest("ff", hn)
    hp, dgl, dpe, dh2, dh2b, loss, d_gf, d_gple = _ff2_tail(hid, w2, h1, g_ple, p, tgt, wg, wp, gf, tm)

    d_wp = _matmul_tn(p, dpe, "dw_ple_proj", a_fn=lambda a: a.astype(MXU))
    d_wg = _matmul_tn(hp, dgl, "dw_ple_gate")
    d_w2 = _matmul_tn(hid, dh2b, "dw_ff2", a_fn=_sq)
    dpre = _ff2_bwd(dh2b, w2, hid, min(T, 2 * tm))
    d_w1 = _matmul_tn(hn, dpre, "dw_ff1")
    token = comm.send("a", {"w_ple_proj": d_wp, "w_ple_gate": d_wg, "w_ff2": d_w2, "w_ff1": d_w1})
    dh1, dh1b, d_gmlp = _ff1_bwd(dpre, w1, dh2, h1, g_mlp, tm, token)
    dcat = _outproj_bwd(dh1b, wo, min(T, 2 * tm))
    d_wo = _matmul_tn(cat, dh1b, "dw_out")
    duv, d_gv, d_ws, d_bst, d_gout, dxn_uv = _gmlp_bwd(uv, dcat, gv, ws, bst, gout, wm)
    token = comm.send("b", {
        "w_out": d_wo, "loss": loss[0:1, 0:1], "final_norm_g": d_gf, "ple_norm_g": d_gple, "norm_mlp_g": d_gmlp,
        "gm_v_norm_g": d_gv, "gm_ws": d_ws, "gm_bs": d_bst[:, :GM_HEADS].T, "gm_out_norm_g": d_gout})
    dssd, ddt, d_cw, d_cb, d_dtb, d_al, d_ds, d_ng = _ssd_bwd(
        pz, pxbc, conv, dtraw, sall, dcat, convw, dtb, alog, dskip, ng, ex, ltri, ext, nb, token)
    d_win = _split_dw_in(_matmul_tn(xn, duv, "dw_in_uv"), _matmul_tn(xn, dssd, "dw_in_ssd"),
                         _matmul_tn(xn, ddt, "dw_in_dt"))
    token = comm.send("c", {"w_in": d_win})
    dx, d_gmix = _inproj_bwd(dxn_uv, dssd, ddt, wm, wdt, dh1, x, g_mix, tm, token)
    comm.send("d", {"norm_mix_g": d_gmix, "ssd_conv_w": d_cw[0:4], "ssd_conv_b": d_cb, "ssd_dt_bias": d_dtb[:, :16],
                    "ssd_a_log": d_al[:, :16], "ssd_d": d_ds[:, :16], "ssd_norm_g": d_ng})
    return dx


def kernel(x, p, norm_mix_g, w_in, gm_v_norm_g, gm_ws, gm_bs, gm_out_norm_g, ssd_conv_w, ssd_conv_b, ssd_dt_bias, ssd_a_log, ssd_d, ssd_norm_g, w_out, norm_mlp_g, w_ff1, w_ff2, ple_norm_g, w_ple_gate, w_ple_proj, final_norm_g, loss_target, m_norm_mix_g, m_w_in, m_gm_v_norm_g, m_gm_ws, m_gm_bs, m_gm_out_norm_g, m_ssd_conv_w, m_ssd_conv_b, m_ssd_dt_bias, m_ssd_a_log, m_ssd_d, m_ssd_norm_g, m_w_out, m_norm_mlp_g, m_w_ff1, m_w_ff2, m_ple_norm_g, m_w_ple_gate, m_w_ple_proj, m_final_norm_g, v_norm_mix_g, v_w_in, v_gm_v_norm_g, v_gm_ws, v_gm_bs, v_gm_out_norm_g, v_ssd_conv_w, v_ssd_conv_b, v_ssd_dt_bias, v_ssd_a_log, v_ssd_d, v_ssd_norm_g, v_w_out, v_norm_mlp_g, v_w_ff1, v_w_ff2, v_ple_norm_g, v_w_ple_gate, v_w_ple_proj, v_final_norm_g):
    a = dict(locals())
    order = ["norm_mix_g", "w_in", "gm_v_norm_g", "gm_ws", "gm_bs", "gm_out_norm_g", "ssd_conv_w", "ssd_conv_b",
             "ssd_dt_bias", "ssd_a_log", "ssd_d", "ssd_norm_g", "w_out", "norm_mlp_g", "w_ff1", "w_ff2", "ple_norm_g",
             "w_ple_gate", "w_ple_proj", "final_norm_g"]
    chip = 2 * lax.axis_index("x") + lax.axis_index("y")
    nb, S = x.shape[0], x.shape[1]
    T = nb * S
    sm = {n: a[n] for n, _ in _SMALL if n != "ssd_conv_w"}
    comm = _Comm(a, jnp.stack([chip, lax.axis_index("c")]).astype(jnp.int32))
    dx = _local_step(x.reshape(T, D), p.reshape(T, DPLE), loss_target.reshape(T, D), sm, comm, nb, 512)
    big, small_packs = comm.finish(dx)
    small, loss = _small_update(a, small_packs)
    g_out, delta, new_m, new_v = {}, {}, {}, {}
    for n in order:
        g_out[n], delta[n], new_m[n], new_v[n] = (r.reshape(a[n].shape) for r in (big[n] if n in big else small[n]))
    return (loss.reshape(()), dx.reshape(x.shape), *[g_out[n] for n in order], *[delta[n] for n in order],
            *[new_m[n] for n in order], *[new_v[n] for n in order])
```

```python
import jax
import jax.numpy as jnp
from jax import lax
from jax.experimental import pallas as pl
from jax.experimental.pallas import tpu as pltpu

F32 = jnp.float32
BF16 = jnp.bfloat16
MXU = jnp.bfloat16
GRAD = jnp.bfloat16

D = 1024
CH = 128
GM_HEADS = 8
SSD_HEADS = 16
SSD_P = 64
CONV_CH = 1536
N_MAIN = 4608
DFF = 4096
DPLE = 256
EPS = 1e-6
NEG = -1e30

LR, B1, B2, ADAM_EPS, WD, STEP = 0.001, 0.9, 0.999, 1e-08, 0.01, 10

VMEM_LIMIT = 56 * 1024 * 1024
_SEQS_PER_STEP = 4
MESH = pl.DeviceIdType.MESH

INV_SQRT2 = 0.7071067811865476
INV_SQRT_2PI = 0.3989422804014327


def _cp(n_axes=1):
    return pltpu.CompilerParams(dimension_semantics=("arbitrary",) * n_axes, vmem_limit_bytes=VMEM_LIMIT)


def _dot(a, b):
    return jnp.dot(a, b, preferred_element_type=F32)


def _dot_nt(a, b):
    return lax.dot_general(a, b, (((1,), (1,)), ((), ())), preferred_element_type=F32)


def _dot_tn(a, b):
    return lax.dot_general(a, b, (((0,), (0,)), ((), ())), preferred_element_type=F32)


def _dot_hi(a, b):
    return jnp.dot(a, b, preferred_element_type=F32, precision=lax.Precision.HIGHEST)


def _dot_01(a, sel):
    hi = a.astype(BF16)
    lo = (a - hi.astype(F32)).astype(BF16)
    n = a.shape[0]
    r = _dot(jnp.concatenate([hi, lo], axis=0), sel)
    return r[0:n] + r[n:2 * n]


def _rows(tm, n, j=0):
    return pl.BlockSpec((tm, n), lambda i: (i, j))


def _const(shape):
    nd = len(shape)
    return pl.BlockSpec(shape, lambda *_: (0,) * nd)


def _sds(shape, dtype):
    return jax.ShapeDtypeStruct(shape, dtype)


def _rms(x):
    r = lax.rsqrt(jnp.mean(x * x, axis=-1, keepdims=True) + EPS)
    return x * r, r


def _rms_bwd(dy, xhat, r, g):
    dyg = dy * g
    return r * (dyg - xhat * jnp.mean(dyg * xhat, axis=-1, keepdims=True))


def _sigmoid(x):
    return 1.0 / (1.0 + jnp.exp(-x))


def _gelu(x):
    cdf = 0.5 * (1.0 + lax.erf(x * INV_SQRT2))
    pdf = jnp.exp(-0.5 * x * x) * INV_SQRT_2PI
    return x * cdf, cdf + x * pdf


def _softplus(x):
    e = jnp.exp(-jnp.abs(x))
    u = 1.0 + e
    log1p = jnp.where(u == 1.0, e, jnp.log(u) * e / (u - 1.0))
    return jnp.maximum(x, 0.0) + log1p


def _after(n_in, fn):
    def body(*refs):
        return fn(*refs[:n_in], *refs[n_in + 1:])

    return body


def _inproj_gmlp(x, g, wm, wdt, gv, ws, bst, gout, tm, after):
    T = x.shape[0]

    def body(x_ref, g_ref, wm_ref, wdt_ref, gv_ref, ws_ref, bst_ref, gout_ref,
             z_ref, xbc_ref, dt_ref, xn_ref, ya_ref, uv_ref):
        xh, _ = _rms(x_ref[...])
        xn = (xh * g_ref[...]).astype(MXU)
        xn_ref[...] = xn
        for n in range(4):
            uv_ref[:, n * 512:(n + 1) * 512] = _dot(xn, wm_ref[:, n * 512:(n + 1) * 512])
        for n in range(2):
            z_ref[:, n * 512:(n + 1) * 512] = _dot(xn, wm_ref[:, 2048 + n * 512:2048 + (n + 1) * 512])
        for n in range(3):
            xbc_ref[:, n * 512:(n + 1) * 512] = _dot(xn, wm_ref[:, 3072 + n * 512:3072 + (n + 1) * 512])
        dt_ref[...] = _dot(xn, wdt_ref[...])
        for k in range(tm // CH):
            rows = slice(k * CH, (k + 1) * CH)
            f = _gmlp_fwd_vals(uv_ref[rows, 0:1024], uv_ref[rows, 1024:2048], gv_ref[...], ws_ref, bst_ref[...],
                               gout_ref[...])
            ya_ref[rows, :] = f["out"].astype(MXU)

    return pl.pallas_call(
        _after(8, body), grid=(T // tm,), name="inproj_gmlp",
        in_specs=[_rows(tm, D), _const((1, D)), _const((D, N_MAIN)), _const((D, 128)), _const((1, 1024)),
                  _const((GM_HEADS, CH, CH)), _const((CH, 128)), _const((1, 1024)), _ANY],
        out_specs=[_rows(tm, 1024), _rows(tm, CONV_CH), _rows(tm, 128), _rows(tm, D), _rows(tm, 1024, 0),
                   _rows(tm, 2048)],
        out_shape=[_sds((T, 1024), F32), _sds((T, CONV_CH), F32), _sds((T, 128), F32), _sds((T, D), MXU),
                   _sds((T, 2048), MXU), _sds((T, 2048), F32)],
        compiler_params=_cp(),
    )(x, g, wm, wdt, gv, ws, bst, gout, after)


def _gmlp_fwd_vals(u, v, gv, ws_ref, bst, gout):
    ug, dug = _gelu(u)
    vg, dvg = _gelu(v)
    row = lax.broadcasted_iota(jnp.int32, (CH, CH), 0)
    col = lax.broadcasted_iota(jnp.int32, (CH, CH), 1)
    tril = row >= col
    ys, heads = [], []
    for h in range(GM_HEADS):
        sl = slice(h * 128, (h + 1) * 128)
        vhat, rv = _rms(vg[:, sl])
        vn = (vhat * gv[:, sl]).astype(MXU)
        wt = jnp.where(tril, ws_ref[h], 0.0)
        mixed = _dot(wt.astype(MXU), vn) + bst[:, h:h + 1]
        ys.append(ug[:, sl] * mixed)
        heads.append((vhat, rv, vn, wt, mixed))
    y = jnp.concatenate(ys, axis=1)
    yhat, ry = _rms(y)
    return dict(ug=ug, dug=dug, dvg=dvg, heads=heads, yhat=yhat, ry=ry, tril=tril, out=yhat * gout)


def _shifts_down(cur, halo):
    row8 = lax.broadcasted_iota(jnp.int32, (8, cur.shape[1]), 0)
    out = [cur]
    for j in (1, 2, 3):
        sh = pltpu.roll(cur, j, 0)
        top = jnp.where(row8 < j, pltpu.roll(halo, j, 0), sh[0:8])
        out.append(jnp.concatenate([top, sh[8:]], axis=0))
    return out


def _shifts_up(cur, halo):
    row8 = lax.broadcasted_iota(jnp.int32, (8, cur.shape[1]), 0)
    out = []
    for j in (1, 2, 3):
        sh = pltpu.roll(cur, CH - j, 0)
        bot = jnp.where(row8 + j >= 8, pltpu.roll(halo, 8 - j, 0), sh[CH - 8:CH])
        out.append(jnp.concatenate([sh[0:CH - 8], bot], axis=0))
    return out


def _conv(xbc, halo, convw, convb):
    sh = _shifts_down(xbc, halo)
    return convb + convw[3:4] * sh[0] + convw[2:3] * sh[1] + convw[1:2] * sh[2] + convw[0:1] * sh[3]


def _ssd_fwd_vals(z, conv, dtraw, dtb, alog, dskip, ng, ex, ltri, s_prev):
    sig_c = _sigmoid(conv)
    xa = conv * sig_c
    xs = xa[:, :1024]
    bm = [xa[:, 1024:1152], xa[:, 1152:1280]]
    cm = [xa[:, 1280:1408], xa[:, 1408:1536]]
    dtpre = dtraw + dtb
    dt = _softplus(dtpre)
    a_neg = -jnp.exp(alog)
    cs = _dot_hi(ltri, dt * a_neg)
    cst = cs.T
    last = cs[CH - 1:CH]
    ecs = jnp.exp(cs)
    dec = jnp.exp(last - cs)
    spread = _dot_01(jnp.concatenate([dt, ecs, dec], axis=0), ex)
    dte, ecse, dece = spread[0:CH], spread[CH:2 * CH], spread[2 * CH:3 * CH]
    cde = ecse[CH - 1:CH]
    de = dskip
    xdt = xs * dte
    row = lax.broadcasted_iota(jnp.int32, (CH, CH), 0)
    col = lax.broadcasted_iota(jnp.int32, (CH, CH), 1)
    tril = row >= col
    lo = col < SSD_P
    bmb = [b.astype(MXU) for b in bm]
    cmb = [c.astype(MXU) for c in cm]
    mg = [_dot_nt(cmb[g], bmb[g]) for g in range(2)]
    yd, lms, whs = [], [], []
    for q in range(8):
        g = q // 4
        xq = xdt[:, q * 128:(q + 1) * 128]
        acc = None
        for hh in range(2):
            h = 2 * q + hh
            seg = cs[:, h:h + 1] - cst[h:h + 1, :]
            lm = jnp.exp(jnp.where(tril, seg, NEG))
            wh = (mg[g] * lm).astype(MXU)
            xm = jnp.where(lo if hh == 0 else ~lo, xq, 0.0).astype(MXU)
            part = _dot(wh, xm)
            acc = part if acc is None else acc + part
            lms.append(lm)
            whs.append(wh)
        yd.append(acc)
    yd = jnp.concatenate(yd, axis=1)
    sb = s_prev.astype(MXU)
    yo = jnp.concatenate([_dot(cmb[g], sb[:, g * 512:(g + 1) * 512]) for g in range(2)], axis=1) * ecse
    xdec = (xdt * dece).astype(MXU)
    states = jnp.concatenate([_dot_tn(bmb[g], xdec[:, g * 512:(g + 1) * 512]) for g in range(2)], axis=1)
    s_next = s_prev * cde + states
    ypre = yd + yo + de * xs
    sig_z = _sigmoid(z)
    yg = ypre * z * sig_z
    outs, yhat, rr = [], [], []
    for g in range(2):
        sl = slice(g * 512, (g + 1) * 512)
        yh, r = _rms(yg[:, sl])
        yhat.append(yh)
        rr.append(r)
        outs.append(yh * ng[:, sl])
    return dict(sig_c=sig_c, xa=xa, xs=xs, bmb=bmb, cmb=cmb, dtpre=dtpre, dt=dt, a_neg=a_neg,
                cs=cs, last=last, ecs=ecs, dec=dec, dte=dte, ecse=ecse, dece=dece, cde=cde, de=de, xdt=xdt,
                mg=mg, lms=lms, whs=whs, lo=lo, yo=yo, sb=sb, xdec=xdec, s_next=s_next, ypre=ypre, sig_z=sig_z,
                yhat=yhat, rr=rr, out=jnp.concatenate(outs, axis=1))


def _ssd_fwd(pz, pxbc, dtraw, cat, convw, convb, dtb, alog, dskip, ng, ex, ltri, nb):
    T = pz.shape[0]
    S = T // nb
    nch = S // CH
    ns = _SEQS_PER_STEP if nb % _SEQS_PER_STEP == 0 else 1

    def body(z_ref, xbc_ref, halo_ref, dt_ref, cw_ref, cb_ref, dtb_ref, al_ref, ds_ref, ng_ref, ex_ref, lt_ref,
             cat_in_ref, yb_ref, sall_ref, conv_ref, s_ref):
        del cat_in_ref
        c = pl.program_id(1)

        @pl.when(c == 0)
        def _():
            s_ref[...] = jnp.zeros_like(s_ref)

        for i in range(ns):
            halo = jnp.where(c == 0, 0.0, halo_ref[i])
            s_prev = s_ref[i]
            sall_ref[i, 0] = s_prev
            conv = _conv(xbc_ref[i], halo, cw_ref[...], cb_ref[...])
            conv_ref[i] = conv
            f = _ssd_fwd_vals(z_ref[i], conv, dt_ref[i], dtb_ref[...], al_ref[...], ds_ref[...], ng_ref[...],
                              ex_ref[...], lt_ref[...], s_prev)
            s_ref[i] = f["s_next"]
            yb_ref[i] = f["out"].astype(MXU)

    def seq(width, col=0):
        return pl.BlockSpec((ns, CH, width), lambda b, c: (b, c, col))

    cat, sall, conv = pl.pallas_call(
        body, grid=(nb // ns, nch), name="ssd_fwd",
        in_specs=[seq(1024), seq(CONV_CH),
                  pl.BlockSpec((ns, 8, CONV_CH), lambda b, c: (b, jnp.maximum(c * (CH // 8) - 1, 0), 0)),
                  seq(128),
                  _const((8, CONV_CH)), _const((1, CONV_CH)), _const((1, 128)), _const((1, 128)), _const((1, 1024)),
                  _const((1, 1024)), _const((128, 1024)), _const((CH, CH)), _ANY],
        out_specs=[seq(1024, 1), pl.BlockSpec((ns, 1, 128, 1024), lambda b, c: (b, c, 0, 0)), seq(CONV_CH)],
        out_shape=[_sds((nb, S, 2048), MXU), _sds((nb, nch, 128, 1024), F32), _sds((nb, S, CONV_CH), F32)],
        scratch_shapes=[pltpu.VMEM((ns, 128, 1024), F32)],
        input_output_aliases={12: 0},
        compiler_params=_cp(2),
    )(pz.reshape(nb, S, 1024), pxbc.reshape(nb, S, CONV_CH), pxbc.reshape(nb, S, CONV_CH), dtraw.reshape(nb, S, 128),
      convw, convb, dtb, alog, dskip, ng, ex, ltri, cat.reshape(nb, S, 2048))
    return cat.reshape(T, 2048), sall, conv.reshape(T, CONV_CH)


def _outproj_ff1(cat, wo, x, g, w1, tm):
    T = x.shape[0]

    def body(cat_ref, wo_ref, x_ref, g_ref, w1_ref, h1_ref, hn_ref, hid_ref):
        h1 = x_ref[...] + _dot(cat_ref[...], wo_ref[...])
        h1_ref[...] = h1
        hn = (_rms(h1)[0] * g_ref[...]).astype(MXU)
        hn_ref[...] = hn
        for n in range(4):
            hid_ref[:, n * 1024:(n + 1) * 1024] = jnp.maximum(_dot(hn, w1_ref[n]), 0.0).astype(MXU)

    return pl.pallas_call(
        body, grid=(T // tm,), name="outproj_ff1",
        in_specs=[_rows(tm, 2048), _const((2048, D)), _rows(tm, D), _const((1, D)), _const((4, D, 1024))],
        out_specs=[_rows(tm, D), _rows(tm, D), _rows(tm, DFF)],
        out_shape=[_sds((T, D), F32), _sds((T, D), MXU), _sds((T, DFF), MXU)],
        compiler_params=_cp(),
    )(cat, wo, x, g, w1)


def _sq(hid):
    h = hid.astype(F32)
    return (h * h).astype(MXU)


def _ff2_tail(hid, w2, h1, g_ple, p, tgt, wg, wp, gf, tm):
    T = h1.shape[0]

    def body(hid_ref, w2_ref, h1_ref, g_ref, p_ref, t_ref, wg_ref, wp_ref, gf_ref,
             hp_ref, dgl_ref, dpe_ref, dh2_ref, dh2b_ref, loss_ref, dgf_ref, dg_ref):
        @pl.when(pl.program_id(0) == 0)
        def _():
            loss_ref[...] = jnp.zeros_like(loss_ref)
            dgf_ref[...] = jnp.zeros_like(dgf_ref)
            dg_ref[...] = jnp.zeros_like(dg_ref)

        h2 = h1_ref[...] + _dot(_sq(hid_ref[...]), w2_ref[...])
        h2h, r2 = _rms(h2)
        g_ple = g_ref[...]
        hp = (h2h * g_ple).astype(MXU)
        hp_ref[...] = hp
        gate = _sigmoid(_dot(hp, wg_ref[...]))
        pb = p_ref[...].astype(MXU)
        pe = jnp.concatenate([_dot(pb, wp_ref[k]) for k in range(4)], axis=1)
        h3 = h2 + gate * pe
        hh, r = _rms(h3)
        gf = gf_ref[...]
        diff = hh * gf - t_ref[...]
        loss_ref[...] += 0.5 * jnp.sum(jnp.mean(diff * diff, axis=-1, keepdims=True))
        dout = diff * (1.0 / D)
        dgf_ref[...] += jnp.sum(dout * hh, axis=0, keepdims=True)
        dh3 = _rms_bwd(dout, hh, r, gf)
        dgl = (dh3 * pe * gate * (1.0 - gate)).astype(MXU)
        dgl_ref[...] = dgl
        dpe_ref[...] = (dh3 * gate).astype(MXU)
        dhp = _dot_nt(dgl, wg_ref[...])
        dg_ref[...] += jnp.sum(dhp * h2h, axis=0, keepdims=True)
        dh2 = dh3 + _rms_bwd(dhp, h2h, r2, g_ple)
        dh2_ref[...] = dh2
        dh2b_ref[...] = dh2.astype(MXU)

    return pl.pallas_call(
        body, grid=(T // tm,), name="ff2_tail",
        in_specs=[_rows(tm, DFF), _const((DFF, D)), _rows(tm, D), _const((1, D)), _rows(tm, DPLE), _rows(tm, D),
                  _const((D, D)), _const((4, DPLE, 256)), _const((1, D))],
        out_specs=[_rows(tm, D), _rows(tm, D), _rows(tm, D), _rows(tm, D), _rows(tm, D), _const((8, 128)),
                   _const((1, D)), _const((1, D))],
        out_shape=[_sds((T, D), MXU), _sds((T, D), MXU), _sds((T, D), MXU), _sds((T, D), F32), _sds((T, D), MXU),
                   _sds((8, 128), F32), _sds((1, D), F32), _sds((1, D), F32)],
        compiler_params=_cp(),
    )(hid, w2, h1, g_ple, p, tgt, wg, wp, gf)


def _ff2_bwd(dh2b, w2, hid, tm):
    T = hid.shape[0]

    def body(dh2b_ref, w2_ref, hid_ref, dpre_ref):
        d = dh2b_ref[...]
        for n in range(DFF // 1024):
            sl = slice(n * 1024, (n + 1) * 1024)
            da = _dot_nt(d, w2_ref[sl, :])
            dpre_ref[:, sl] = (2.0 * da * hid_ref[:, sl].astype(F32)).astype(MXU)

    return pl.pallas_call(
        body, grid=(T // tm,), name="ff2_bwd",
        in_specs=[_rows(tm, D), _const((DFF, D)), _rows(tm, DFF)],
        out_specs=_rows(tm, DFF),
        out_shape=_sds((T, DFF), MXU),
        compiler_params=_cp(),
    )(dh2b, w2, hid)


def _ff1_bwd(dpre, w1, dh2, h1, g, tm):
    T = h1.shape[0]

    def body(dpre_ref, w1_ref, dh2_ref, h1_ref, g_ref, dh1_ref, dh1b_ref, dg_ref):
        @pl.when(pl.program_id(0) == 0)
        def _():
            dg_ref[...] = jnp.zeros_like(dg_ref)

        dhn = _dot_nt(dpre_ref[:, 0:1024], w1_ref[0])
        for k in range(1, 4):
            dhn = dhn + _dot_nt(dpre_ref[:, k * 1024:(k + 1) * 1024], w1_ref[k])
        hh, r = _rms(h1_ref[...])
        dg_ref[...] += jnp.sum(dhn * hh, axis=0, keepdims=True)
        dh1 = dh2_ref[...] + _rms_bwd(dhn, hh, r, g_ref[...])
        dh1_ref[...] = dh1
        dh1b_ref[...] = dh1.astype(MXU)

    return pl.pallas_call(
        body, grid=(T // tm,), name="ff1_bwd",
        in_specs=[_rows(tm, DFF), _const((4, D, 1024)), _rows(tm, D), _rows(tm, D), _const((1, D))],
        out_specs=[_rows(tm, D), _rows(tm, D), _const((1, D))],
        out_shape=[_sds((T, D), F32), _sds((T, D), MXU), _sds((1, D), F32)],
        compiler_params=_cp(),
    )(dpre, w1, dh2, h1, g)


def _outproj_bwd(dh1b, wo, tm):
    T = dh1b.shape[0]

    def body(d_ref, wo_ref, dcat_ref):
        d = d_ref[...]
        dcat_ref[:, 0:1024] = _dot_nt(d, wo_ref[0:1024, :])
        dcat_ref[:, 1024:2048] = _dot_nt(d, wo_ref[1024:2048, :])

    return pl.pallas_call(
        body, grid=(T // tm,), name="outproj_bwd",
        in_specs=[_rows(tm, D), _const((2048, D))],
        out_specs=_rows(tm, 2048),
        out_shape=_sds((T, 2048), F32),
        compiler_params=_cp(),
    )(dh1b, wo)


def _gmlp_bwd(uv, dcat, gv, ws, bst, gout, wm):
    T = uv.shape[0]
    nck = 4 if T % (4 * CH) == 0 else 1
    tb = nck * CH

    def body(uv_ref, dya_ref, gv_ref, ws_ref, bst_ref, gout_ref, wuv_ref, duv_ref, dgv_ref, dws_ref, dbst_ref,
             dgo_ref, dxn_ref):
        @pl.when(pl.program_id(0) == 0)
        def _():
            dgv_ref[...] = jnp.zeros_like(dgv_ref)
            dws_ref[...] = jnp.zeros_like(dws_ref)
            dbst_ref[...] = jnp.zeros_like(dbst_ref)
            dgo_ref[...] = jnp.zeros_like(dgo_ref)

        for k in range(nck):
            chunk(slice(k * CH, (k + 1) * CH), uv_ref, dya_ref, gv_ref, ws_ref, bst_ref, gout_ref, duv_ref,
                  dgv_ref, dws_ref, dbst_ref, dgo_ref)
        dxn_ref[...] = _dot_nt(duv_ref[...], wuv_ref[...])

    def chunk(rows, uv_ref, dya_ref, gv_ref, ws_ref, bst_ref, gout_ref, duv_ref, dgv_ref, dws_ref, dbst_ref,
              dgo_ref):
        gv = gv_ref[...]
        f = _gmlp_fwd_vals(uv_ref[rows, 0:1024], uv_ref[rows, 1024:2048], gv, ws_ref, bst_ref[...], gout_ref[...])
        dya = dya_ref[rows, :]
        dgo_ref[...] += jnp.sum(dya * f["yhat"], axis=0, keepdims=True)
        dy = _rms_bwd(dya, f["yhat"], f["ry"], gout_ref[...])
        lane = lax.broadcasted_iota(jnp.int32, (CH, 128), 1)
        dbs = jnp.zeros((CH, 128), F32)
        dug, dvg, dgvs = [], [], []
        for h in range(GM_HEADS):
            sl = slice(h * 128, (h + 1) * 128)
            vhat, rv, vn, wt, mixed = f["heads"][h]
            dyh = dy[:, sl]
            dug.append(dyh * mixed)
            dmixed = dyh * f["ug"][:, sl]
            dmb = dmixed.astype(MXU)
            dws_ref[h] += jnp.where(f["tril"], _dot_nt(dmb, vn), 0.0)
            dbs = dbs + jnp.where(lane == h, jnp.sum(dmixed, axis=1, keepdims=True), 0.0)
            dvn = _dot_tn(wt.astype(MXU), dmb)
            dgvs.append(jnp.sum(dvn * vhat, axis=0, keepdims=True))
            dvg.append(_rms_bwd(dvn, vhat, rv, gv[:, sl]))
        dbst_ref[...] += dbs
        dgv_ref[...] += jnp.concatenate(dgvs, axis=1)
        duv_ref[rows, 0:1024] = (jnp.concatenate(dug, axis=1) * f["dug"]).astype(MXU)
        duv_ref[rows, 1024:2048] = (jnp.concatenate(dvg, axis=1) * f["dvg"]).astype(MXU)

    return pl.pallas_call(
        body, grid=(T // tb,), name="gmlp_bwd",
        in_specs=[_rows(tb, 2048), _rows(tb, 1024, 0), _const((1, 1024)),
                  _const((GM_HEADS, CH, CH)), _const((CH, 128)), _const((1, 1024)), _const((D, 2048))],
        out_specs=[_rows(tb, 2048), _const((1, 1024)), _const((GM_HEADS, CH, CH)), _const((CH, 128)),
                   _const((1, 1024)), _rows(tb, D)],
        out_shape=[_sds((T, 2048), MXU), _sds((1, 1024), F32), _sds((GM_HEADS, CH, CH), F32), _sds((CH, 128), F32),
                   _sds((1, 1024), F32), _sds((T, D), F32)],
        compiler_params=_cp(),
    )(uv, dcat, gv, ws, bst, gout, wm)


def _ssd_bwd(pz, pxbc, conv, dtraw, sall, dcat, convw, dtb, alog, dskip, ng, ex, ltri, ext, nb, after):
    T = pz.shape[0]
    S = T // nb
    nch = S // CH
    ns = _SEQS_PER_STEP if nb % _SEQS_PER_STEP == 0 else 1

    def seq(width, col=0):
        return pl.BlockSpec((ns, CH, width), lambda b, c: (b, nch - 1 - c, col))

    in_specs = [
        seq(1024), seq(CONV_CH), seq(CONV_CH), seq(128),
        _const((8, CONV_CH)), _const((1, 128)), _const((1, 128)), _const((1, 1024)),
        _const((1, 1024)), _const((128, 1024)), _const((CH, CH)),
        _const((1024, 128)),
        pl.BlockSpec((ns, 1, 128, 1024), lambda b, c: (b, nch - 1 - c, 0, 0)),
        seq(1024, 1),
        _ANY,
    ]

    def body(z_ref, xbc_ref, conv_ref, dt_ref, cw_ref, dtb_ref, al_ref, ds_ref, ng_ref, ex_ref, lt_ref,
             ext_ref, sall_ref, dyb_ref,
             dssd_ref, ddt_ref, dcw_ref, dcb_ref, ddtb_ref, dal_ref, dds_ref, dng_ref,
             dst_ref, dnext_ref, ddse_ref):
        b = pl.program_id(0)
        c = pl.program_id(1)

        @pl.when((b == 0) & (c == 0))
        def _():
            for r in (dcw_ref, dcb_ref, ddtb_ref, dal_ref, dds_ref, dng_ref, ddse_ref):
                r[...] = jnp.zeros_like(r)

        @pl.when(c == 0)
        def _():
            dst_ref[...] = jnp.zeros_like(dst_ref)
            dnext_ref[...] = jnp.zeros_like(dnext_ref)

        ex = ex_ref[...]
        ext = ext_ref[...]
        cw = cw_ref[...]
        ng = ng_ref[...]
        for i in range(ns):
            one_chunk(i, ex, ext, cw, ng, z_ref, xbc_ref, conv_ref, dt_ref, dtb_ref, al_ref, ds_ref, lt_ref, sall_ref,
                      dyb_ref, dssd_ref, ddt_ref, dcw_ref, dcb_ref, ddtb_ref, dal_ref, dng_ref, dst_ref, dnext_ref,
                      ddse_ref)

        @pl.when((b == nb // ns - 1) & (c == nch - 1))
        def _():
            dds_ref[...] = _dot_01(jnp.broadcast_to(ddse_ref[...], (8, 1024)), ext)[0:1]

    def one_chunk(i, ex, ext, cw, ng, z_ref, xbc_ref, conv_ref, dt_ref, dtb_ref, al_ref, ds_ref, lt_ref, sall_ref,
                  dyb_ref, dssd_ref, ddt_ref, dcw_ref, dcb_ref, ddtb_ref, dal_ref, dng_ref, dst_ref, dnext_ref,
                  ddse_ref):
        z = z_ref[i]
        s_prev = sall_ref[i, 0]
        conv = conv_ref[i]
        f = _ssd_fwd_vals(z, conv, dt_ref[i], dtb_ref[...], al_ref[...], ds_ref[...], ng, ex, lt_ref[...], s_prev)
        xs, xdt, cs, dec, dt = f["xs"], f["xdt"], f["cs"], f["dec"], f["dt"]
        dyb = dyb_ref[i]
        dyg, dngs = [], []
        for g in range(2):
            sl = slice(g * 512, (g + 1) * 512)
            dngs.append(jnp.sum(dyb[:, sl] * f["yhat"][g], axis=0, keepdims=True))
            dyg.append(_rms_bwd(dyb[:, sl], f["yhat"][g], f["rr"][g], ng[:, sl]))
        dng_ref[...] += jnp.concatenate(dngs, axis=1)
        dyg = jnp.concatenate(dyg, axis=1)
        sig_z = f["sig_z"]
        silu_z = z * sig_z
        dy = dyg * silu_z
        dz = dyg * f["ypre"] * (sig_z + silu_z * (1.0 - sig_z))
        ddse_ref[...] += jnp.sum(dy * xs, axis=0, keepdims=True)
        dxs = dy * f["de"]
        dye = dy * f["ecse"]
        dyeb = dye.astype(MXU)
        dst = dst_ref[i]
        dstb = dst.astype(MXU)
        bmb, cmb, sb, xdec = f["bmb"], f["cmb"], f["sb"], f["xdec"]
        u = jnp.concatenate([_dot(bmb[g], dstb[:, g * 512:(g + 1) * 512]) for g in range(2)], axis=1)
        dxdt = [u[:, q * 128:(q + 1) * 128] * f["dece"][:, q * 128:(q + 1) * 128] for q in range(8)]
        per_head = _dot_01(jnp.concatenate(
            [dy * f["yo"], u * xdt, jnp.broadcast_to(jnp.sum(dst * s_prev, axis=0, keepdims=True), (8, 1024))],
            axis=0), ext)
        dcs = per_head[0:CH]
        t = per_head[CH:2 * CH] * dec
        dcd = per_head[2 * CH:2 * CH + 1]
        row = lax.broadcasted_iota(jnp.int32, (CH, 128), 0)
        lane = lax.broadcasted_iota(jnp.int32, (CH, 128), 1)
        cd = jnp.exp(f["last"])
        dcs = dcs - t + jnp.where(row == CH - 1, jnp.sum(t, axis=0, keepdims=True) + dcd * cd, 0.0)
        dcst = jnp.zeros((128, CH), F32)
        lo = f["lo"]
        dbm, dcm, ds_prev = [], [], []
        for g in range(2):
            sl = slice(g * 512, (g + 1) * 512)
            dmg = jnp.zeros((CH, CH), F32)
            for q in range(4 * g, 4 * g + 4):
                dyq = dy[:, q * 128:(q + 1) * 128]
                xq = xdt[:, q * 128:(q + 1) * 128].astype(MXU)
                for hh in range(2):
                    h = 2 * q + hh
                    m = lo if hh == 0 else ~lo
                    dym = jnp.where(m, dyq, 0.0).astype(MXU)
                    gh = _dot_nt(dym, xq)
                    gl = gh * f["lms"][h]
                    dmg = dmg + gl
                    qh = gl * f["mg"][g]
                    dcs = dcs + jnp.where(lane == h, jnp.sum(qh, axis=1, keepdims=True), 0.0)
                    dcst = dcst - jnp.where(row == h, jnp.sum(qh, axis=0, keepdims=True), 0.0)
                    dxdt[q] = dxdt[q] + _dot_tn(f["whs"][h], dym)
            dmgb = dmg.astype(MXU)
            dcm.append(_dot(dmgb, bmb[g]) + _dot_nt(dyeb[:, sl], sb[:, sl]))
            dbm.append(_dot_tn(dmgb, cmb[g]) + _dot_nt(xdec[:, sl], dstb[:, sl]))
            ds_prev.append(_dot_tn(cmb[g], dyeb[:, sl]))
        dst_ref[i] = jnp.concatenate(ds_prev, axis=1) + dst * f["cde"]
        dcs = dcs + dcst.T
        da = _dot_hi(lt_ref[...].T, dcs)
        dxdt = jnp.concatenate(dxdt, axis=1)
        a_neg = f["a_neg"]
        ddt = da * a_neg + _dot_01(dxdt * xs, ext)
        dal_ref[...] += jnp.sum(da * dt, axis=0, keepdims=True) * a_neg
        dxs = dxs + dxdt * f["dte"]
        ddtraw = jnp.where(lane < SSD_HEADS, ddt * _sigmoid(f["dtpre"]), 0.0)
        ddtb_ref[...] += jnp.sum(ddtraw, axis=0, keepdims=True)
        ddt_ref[i] = ddtraw.astype(MXU)
        dxa = jnp.concatenate([dxs, dbm[0], dbm[1], dcm[0], dcm[1]], axis=1)
        sig_c = f["sig_c"]
        dconv = dxa * (sig_c + f["xa"] * (1.0 - sig_c))
        dcb_ref[...] += jnp.sum(dconv, axis=0, keepdims=True)
        xbc = xbc_ref[i]
        dcw_ref[3:4, :] += jnp.sum(dconv * xbc, axis=0, keepdims=True)
        dxbc = cw[3:4] * dconv
        for j, up in zip((1, 2, 3), _shifts_up(dconv, dnext_ref[i])):
            dcw_ref[3 - j:4 - j, :] += jnp.sum(up * xbc, axis=0, keepdims=True)
            dxbc = dxbc + cw[3 - j:4 - j] * up
        dnext_ref[i] = dconv[0:8]
        dssd_ref[i, :, 0:1024] = dz.astype(MXU)
        dssd_ref[i, :, 1024:2560] = dxbc.astype(MXU)

    dssd, ddt, *small = pl.pallas_call(
        _after(14, body), grid=(nb // ns, nch), name="ssd_bwd",
        in_specs=in_specs,
        out_specs=[seq(2560), seq(128),
                   _const((8, CONV_CH)), _const((1, CONV_CH)), _const((1, 128)), _const((1, 128)), _const((1, 128)),
                   _const((1, 1024))],
        out_shape=[_sds((nb, S, 2560), MXU), _sds((nb, S, 128), MXU), _sds((8, CONV_CH), F32),
                   _sds((1, CONV_CH), F32), _sds((1, 128), F32), _sds((1, 128), F32), _sds((1, 128), F32),
                   _sds((1, 1024), F32)],
        scratch_shapes=[pltpu.VMEM((ns, 128, 1024), F32), pltpu.VMEM((ns, 8, CONV_CH), F32),
                        pltpu.VMEM((1, 1024), F32)],
        compiler_params=_cp(2),
    )(pz.reshape(nb, S, 1024), pxbc.reshape(nb, S, CONV_CH), conv.reshape(nb, S, CONV_CH), dtraw.reshape(nb, S, 128),
      convw, dtb, alog, dskip, ng, ex, ltri, ext, sall, dcat.reshape(nb, S, 2048), after)
    return (dssd.reshape(T, 2560), ddt.reshape(T, 128), *small)


def _inproj_bwd(dxn_uv, dssd, ddt, wm, wdt, dh1, x, g, tm, after):
    T = x.shape[0]

    def body(dxnuv_ref, dssd_ref, ddt_ref, wm_ref, wdt_ref, dh1_ref, x_ref, g_ref, dx_ref, dg_ref):
        @pl.when(pl.program_id(0) == 0)
        def _():
            dg_ref[...] = jnp.zeros_like(dg_ref)

        dxn = (dxnuv_ref[...] + _dot_nt(dssd_ref[...], wm_ref[:, 2048:N_MAIN])
               + _dot_nt(ddt_ref[...], wdt_ref[...]))
        xh, r = _rms(x_ref[...])
        dg_ref[...] += jnp.sum(dxn * xh, axis=0, keepdims=True)
        dx_ref[...] = dh1_ref[...] + _rms_bwd(dxn, xh, r, g_ref[...])

    return pl.pallas_call(
        _after(8, body), grid=(T // tm,), name="inproj_bwd",
        in_specs=[_rows(tm, D), _rows(tm, 2560), _rows(tm, 128), _const((D, N_MAIN)), _const((D, 128)),
                  _rows(tm, D), _rows(tm, D), _const((1, D)), _ANY],
        out_specs=[_rows(tm, D), _const((1, D))],
        out_shape=[_sds((T, D), F32), _sds((1, D), F32)],
        compiler_params=_cp(),
    )(dxn_uv, dssd, ddt, wm, wdt, dh1, x, g, after)


def _matmul_tn(a, b, name, a_fn=None):
    T, M = a.shape
    N = b.shape[1]
    tm = min(M, 1024)
    tn = 1280 if N == 2560 else min(N, 1024)
    tk = min(T, 2048)

    def body(a_ref, b_ref, o_ref, acc_ref):
        k = pl.program_id(2)

        @pl.when(k == 0)
        def _():
            acc_ref[...] = jnp.zeros_like(acc_ref)

        av = a_ref[...]
        if a_fn is not None:
            av = a_fn(av)
        acc_ref[...] += _dot_tn(av, b_ref[...])

        @pl.when(k == T // tk - 1)
        def _():
            o_ref[...] = acc_ref[...].astype(o_ref.dtype)

    return pl.pallas_call(
        body, grid=(M // tm, N // tn, T // tk), name=name,
        in_specs=[pl.BlockSpec((tk, tm), lambda i, j, k: (k, i)), pl.BlockSpec((tk, tn), lambda i, j, k: (k, j))],
        out_specs=pl.BlockSpec((tm, tn), lambda i, j, k: (i, j)),
        out_shape=_sds((M, N), GRAD),
        scratch_shapes=[pltpu.VMEM((tm, tn), F32)],
        compiler_params=_cp(3),
    )(a, b)


def _adamw_vals(w, g, m, v):
    m = B1 * m + (1.0 - B1) * g
    v = B2 * v + (1.0 - B2) * (g * g)
    m_hat = m / (1.0 - B1 ** STEP)
    v_hat = v / (1.0 - B2 ** STEP)
    return -LR * (m_hat / (jnp.sqrt(v_hat) + ADAM_EPS) + WD * w), m, v


_PARTS = 4


def _adamw_halves(items, name):
    n = len(items)

    def body(*refs):
        mine = (pl.program_id(0) // _PARTS) == lax.axis_index("c")
        for k in range(n):
            w_ref, own_ref, oth_ref, m_ref, v_ref = refs[5 * k:5 * k + 5]
            g_ref, d_ref, mo_ref, vo_ref = refs[5 * n + 4 * k:5 * n + 4 * k + 4]
            g = jnp.where(mine, own_ref[...], oth_ref[...])
            g_ref[...] = g
            d_ref[...], mo_ref[...], vo_ref[...] = _adamw_vals(w_ref[...], g, m_ref[...], v_ref[...])

    in_specs, out_specs, out_shape = [], [], []
    for w, *_ in items:
        R, C = w.shape
        full = _rows(R // (2 * _PARTS), C)
        part = pl.BlockSpec((R // (2 * _PARTS), C), lambda i: (i % _PARTS, 0))
        in_specs += [full, part, part, full, full]
        out_specs += [full] * 4
        out_shape += [_sds((R, C), F32)] * 4
    res = pl.pallas_call(
        body, grid=(2 * _PARTS,), name=name, in_specs=in_specs, out_specs=out_specs, out_shape=out_shape,
        compiler_params=_cp(),
    )(*[a for item in items for a in item])
    return [tuple(res[4 * k:4 * k + 4]) for k in range(n)]


_TJ = 128


def _adamw_transposed(w, own, other, m, v, name):
    C, _, R = w.shape

    def body(w_ref, own_ref, oth_ref, m_ref, v_ref, g_ref, d_ref, mo_ref, vo_ref):
        first = lax.axis_index("c") == 0
        g = jnp.concatenate([jnp.where(first, own_ref[...], oth_ref[...]),
                             jnp.where(first, oth_ref[...], own_ref[...])], axis=0).T
        d, mo, vo = _adamw_vals(w_ref[:, 0, :], g, m_ref[:, 0, :], v_ref[:, 0, :])
        for ref, val in ((g_ref, g), (d_ref, d), (mo_ref, mo), (vo_ref, vo)):
            ref[:, 0, :] = val

    cols = pl.BlockSpec((_TJ, 1, R), lambda j: (j, 0, 0))
    half = pl.BlockSpec((R // 2, _TJ), lambda j: (0, j))
    return pl.pallas_call(
        body, grid=(pl.cdiv(C, _TJ),), name=name,
        in_specs=[cols, half, half, cols, cols], out_specs=[cols] * 4, out_shape=[_sds((C, 1, R), F32)] * 4,
        compiler_params=_cp(),
    )(w, own, other, m, v)


def _lanes(rows):
    return jnp.concatenate([rows[i:i + 1, :] for i in range(rows.shape[0])], axis=1)


def _small_update(a, packs):
    names = [n for n, _ in _SMALL]
    where = {}
    for k, (pnames, rows, _) in enumerate(packs):
        o = 0
        for n, r in zip(pnames, rows):
            where[n] = (k, o, r)
            o += r
    view = {n: (1, 1024) for n in names}
    view.update(gm_ws=(1024, 128), gm_bs=(8, 128), ssd_conv_w=(4, 384), ssd_conv_b=(1, CONV_CH),
                ssd_dt_bias=(1, 16), ssd_a_log=(1, 16), ssd_d=(1, 16))
    npk = len(packs)

    def body(*refs):
        tots = []
        for k in range(npk):
            tot = refs[k][0]
            for d in range(1, 8):
                tot = tot + refs[k][d]
            tots.append(tot)
        ins, outs = refs[npk:npk + 3 * len(names)], refs[npk + 3 * len(names):]
        chip = 2 * lax.axis_index("x") + lax.axis_index("y")
        for i, n in enumerate(names):
            k, o, r = where[n]
            blk = tots[k][o:o + r, :]
            if n == "gm_ws":
                g = blk
            elif n == "gm_bs":
                g = blk[0:8]
            elif view[n] == (1, 16):
                g = blk[0:1, 0:16]
            elif n == "ssd_conv_w":
                taps = jnp.concatenate([_lanes(blk[12 * t:12 * t + 12]) for t in range(4)], axis=0)
                g = taps[:, 0:384]
                for c in range(1, 4):
                    g = jnp.where(chip == c, taps[:, 384 * c:384 * (c + 1)], g)
            else:
                g = _lanes(blk[0:view[n][1] // 128])
            d, mo, vo = _adamw_vals(ins[3 * i][...], g, ins[3 * i + 1][...], ins[3 * i + 2][...])
            for j, val in enumerate((g, d, mo, vo)):
                outs[4 * i + j][...] = val
        k, o, _ = where["loss"]
        outs[-1][...] = tots[k][o:o + 1, 0:1]

    ins = [a[pre + n].reshape(view[n]) for n in names for pre in ("", "m_", "v_")]
    res = pl.pallas_call(
        body, name="small_update",
        out_shape=[_sds(view[n], F32) for n in names for _ in range(4)] + [_sds((1, 1), F32)],
    )(*[slots for _, _, slots in packs], *ins)
    return {n: tuple(r.reshape(a[n].shape) for r in res[4 * i:4 * i + 4]) for i, n in enumerate(names)}, res[-1]


def _sum_slots(items, kh, name):
    n = len(items)
    in_specs, out_specs, out_shape = [], [], []
    for slots, src, kind, (R, C) in items:
        tr = R // (2 * _PARTS)
        if kind == "slab":
            src_spec = pl.BlockSpec((1, tr, C), lambda i, kh: (kh[0], kh[1] * _PARTS + i, 0))
        elif kind == "rows":
            src_spec = pl.BlockSpec((tr, C), lambda i, kh: (kh[0] * (2 * _PARTS) + kh[1] * _PARTS + i, 0))
        else:
            src_spec = pl.BlockSpec((tr, C), lambda i, kh: (kh[1] * _PARTS + i, kh[0]))
        in_specs += [pl.BlockSpec((8, tr, C), lambda i, kh: (0, i, 0)), src_spec]
        out_specs.append(pl.BlockSpec((tr, C), lambda i, kh: (i, 0)))
        out_shape.append(_sds((R // 2, C), F32))

    def body(kh_ref, *refs):
        me = 2 * kh_ref[0] + kh_ref[1]
        for k, (_, _, kind, _) in enumerate(items):
            s_ref, own_ref, o_ref = refs[2 * k], refs[2 * k + 1], refs[2 * n + k]
            acc = (own_ref[0] if kind == "slab" else own_ref[...]).astype(F32)
            for j in range(1, 8):
                acc = acc + s_ref[me ^ j].astype(F32)
            o_ref[...] = acc

    return pl.pallas_call(
        body, name=name,
        grid_spec=pltpu.PrefetchScalarGridSpec(
            num_scalar_prefetch=1, grid=(_PARTS,), in_specs=in_specs, out_specs=out_specs),
        out_shape=out_shape,
        compiler_params=_cp(),
    )(kh, *[a for slots, src, _, _ in items for a in (slots, src)])


def _assemble_w_in(slabs):
    tr = 256

    def body(s_ref, wm_ref, wdt_ref):
        full = jnp.concatenate([s_ref[k] for k in range(4)], axis=1)
        wm_ref[...] = full[:, :N_MAIN]
        wdt_ref[...] = jnp.concatenate([full[:, N_MAIN:], jnp.zeros((tr, 128 - 16), full.dtype)], axis=1)

    return pl.pallas_call(
        body, grid=(D // tr,), name="assemble_w_in",
        in_specs=[pl.BlockSpec((4, tr, 1156), lambda i: (0, i, 0))],
        out_specs=[_rows(tr, N_MAIN), _rows(tr, 128)],
        out_shape=[_sds((D, N_MAIN), slabs.dtype), _sds((D, 128), slabs.dtype)],
        compiler_params=_cp(),
    )(slabs)


def _split_dw_in(d_uv, d_ssd, d_dt):
    tr = 256

    def body(uv_ref, ssd_ref, dt_ref, o_ref):
        full = jnp.concatenate([uv_ref[...], ssd_ref[...], dt_ref[:, 0:16]], axis=1)
        for k in range(4):
            o_ref[k] = full[:, 1156 * k:1156 * (k + 1)]

    return pl.pallas_call(
        body, grid=(D // tr,), name="split_dw_in",
        in_specs=[_rows(tr, 2048), _rows(tr, 2560), _rows(tr, 128)],
        out_specs=pl.BlockSpec((4, tr, 1156), lambda i: (0, i, 0)),
        out_shape=_sds((4, D, 1156), d_uv.dtype),
        compiler_params=_cp(),
    )(d_uv, d_ssd, d_dt)


def _cast_w_in(w, kh):
    C, _, R = w.shape

    def body(kh_ref, w_ref, o_ref):
        o_ref[0] = w_ref[:, 0, :].T.astype(BF16)

    return pl.pallas_call(
        body, name="cast_w_in",
        grid_spec=pltpu.PrefetchScalarGridSpec(
            num_scalar_prefetch=1, grid=(pl.cdiv(C, _TJ),),
            in_specs=[pl.BlockSpec((_TJ, 1, R), lambda j, kh: (j, 0, 0))],
            out_specs=pl.BlockSpec((1, R, _TJ), lambda j, kh: (kh[0], 0, j))),
        out_shape=_sds((4, R, C), BF16),
        compiler_params=_cp(),
    )(kh, w)


def _cast_into_slot(ws, kh, name):
    n = len(ws)

    def body(kh_ref, *refs):
        for k in range(n):
            refs[n + k][0] = refs[k][...].astype(BF16)

    return pl.pallas_call(
        body, name=name,
        grid_spec=pltpu.PrefetchScalarGridSpec(
            num_scalar_prefetch=1, grid=(_PARTS,),
            in_specs=[pl.BlockSpec((w.shape[0] // _PARTS, w.shape[1]), lambda i, kh: (i, 0)) for w in ws],
            out_specs=[pl.BlockSpec((1, w.shape[0] // _PARTS, w.shape[1]), lambda i, kh: (kh[0], i, 0))
                       for w in ws]),
        out_shape=[_sds((4,) + w.shape, BF16) for w in ws],
        compiler_params=_cp(),
    )(kh, *ws)


_ANY = pl.BlockSpec(memory_space=pl.ANY)
_CHIP_FLIPS = [(1, 0), (0, 1), (1, 1)]
_DEVICE_FLIPS = [(fx, fy, fc) for fx in (0, 1) for fy in (0, 1) for fc in (0, 1)][1:]


def _half(h, rows):
    return pl.ds(pl.multiple_of(h * rows, rows), rows)


def _remote(src, dst, ssem, rsem, to):
    return pltpu.make_async_remote_copy(src_ref=src, dst_ref=dst, send_sem=ssem, recv_sem=rsem,
                                        device_id=to, device_id_type=MESH)


def _weight_gather(bufs, conv):
    n = len(bufs)

    def body(*refs):
        conv_ref, outs, conv_out = refs[n], refs[n + 1:2 * n + 1], refs[2 * n + 1]
        send_sems, recv_sems, fsend_sems, frecv_sems, csend_sems, crecv_sems, local_sem = refs[2 * n + 2:]
        x, y, c = lax.axis_index("x"), lax.axis_index("y"), lax.axis_index("c")
        me = 2 * x + y
        halves = [_half(c, r.shape[1] // 2) for r in outs]
        others = [_half(1 - c, r.shape[1] // 2) for r in outs]
        remote = _remote
        local = [pltpu.make_async_copy(conv_ref, conv_out.at[me], local_sem)]
        for cp in local:
            cp.start()
        sends = []
        for k, (fx, fy) in enumerate(_CHIP_FLIPS):
            peer = (x ^ fx, y ^ fy, c)
            for i in range(n):
                mine = outs[i].at[me, halves[i]]
                sends.append(remote(mine, mine, send_sems.at[k * n + i], recv_sems.at[k * n + i], peer))
            sends.append(remote(conv_ref, conv_out.at[me], csend_sems.at[k], crecv_sems.at[k], peer))
        for cp in sends:
            cp.start()
        sibling = (x, y, 1 - c)
        forwards = []
        for k, (fx, fy) in enumerate(_CHIP_FLIPS):
            peer = (x ^ fx, y ^ fy, c)
            src = 2 * (x ^ fx) + (y ^ fy)
            for i in range(n):
                landed = outs[i].at[src, halves[i]]
                remote(landed, landed, send_sems.at[k * n + i], recv_sems.at[k * n + i], peer).wait_recv()
                fw = remote(landed, landed, fsend_sems.at[k * n + i], frecv_sems.at[k * n + i], sibling)
                fw.start()
                forwards.append(fw)
            remote(conv_out.at[src], conv_out.at[src], csend_sems.at[k], crecv_sems.at[k], peer).wait_recv()
        for k, (fx, fy) in enumerate(_CHIP_FLIPS):
            src = 2 * (x ^ fx) + (y ^ fy)
            for i in range(n):
                theirs = outs[i].at[src, others[i]]
                remote(theirs, theirs, fsend_sems.at[k * n + i], frecv_sems.at[k * n + i], sibling).wait_recv()
        for cp in sends + forwards:
            cp.wait_send()
        for cp in local:
            cp.wait()

    dma = pltpu.SemaphoreType.DMA
    return pl.pallas_call(
        body, name="weight_gather",
        in_specs=[_ANY] * (n + 1), out_specs=[_ANY] * (n + 1),
        out_shape=[_sds(b.shape, b.dtype) for b in bufs] + [_sds((4,) + conv.shape, conv.dtype)],
        input_output_aliases={i: i for i in range(n)},
        scratch_shapes=[dma((3 * n,)), dma((3 * n,)), dma((3 * n,)), dma((3 * n,)), dma((3,)), dma((3,)), dma],
    )(*bufs, conv)


def _piece(ref, kind, R, C, k, h):
    if kind == "slab":
        return ref.at[k, _half(h, R // 2), :]
    if kind == "rows":
        return ref.at[pl.ds(pl.multiple_of(k * R + h * (R // 2), R // 2), R // 2), :]
    return ref.at[_half(h, R // 2), pl.ds(pl.multiple_of(k * C, C), C)]


_HBM = pl.BlockSpec(memory_space=pltpu.HBM)
_SEM = pl.BlockSpec(memory_space=pltpu.SEMAPHORE)


def _split_start(name, arrays, n_copies, plan, after=None):
    n = len(arrays)
    extra = [] if after is None else [after]

    def body(*refs):
        m = n + len(extra)
        arrs, send_sems, recv_sems, token = refs[:n], refs[m], refs[m + 1], refs[-1]
        for j, (src, dst, peer) in enumerate(plan(arrs)):
            _remote(src, dst, send_sems.at[j], recv_sems.at[j], peer).start()
        token[...] = jnp.zeros_like(token)

    dma = pltpu.SemaphoreType.DMA
    res = pl.pallas_call(
        body, name=name,
        out_shape=(dma((n_copies,)), dma((n_copies,)), *[pltpu.HBM(a.shape, a.dtype) for a in arrays],
                   _sds((8, 128), F32)),
        in_specs=[_HBM] * n + [_ANY] * len(extra),
        out_specs=(_SEM, _SEM, *[_HBM] * n, pl.BlockSpec(memory_space=pltpu.VMEM)),
        input_output_aliases={i: 2 + i for i in range(n)},
        compiler_params=pltpu.CompilerParams(has_side_effects=pltpu.SideEffectType.DATAFLOW_SIDE_EFFECTING),
    )(*[pltpu.with_memory_space_constraint(a, pltpu.HBM) for a in arrays], *extra)
    return res[0], res[1], list(res[2:2 + n]), res[-1]


def _split_wait(name, arrays, send_sems, recv_sems, plan, after):
    n = len(arrays)

    def body(*refs):
        arrs, ssems, rsems = refs[:n], refs[n], refs[n + 1]
        for j, (src, dst, peer) in enumerate(plan(arrs)):
            cp = _remote(src, dst, ssems.at[j], rsems.at[j], peer)
            cp.wait_send()
            cp.wait_recv()

    return list(pl.pallas_call(
        body, name=name,
        out_shape=tuple(pltpu.HBM(a.shape, a.dtype) for a in arrays),
        in_specs=[_HBM] * n + [_SEM, _SEM, _ANY],
        out_specs=tuple([_HBM] * n),
        input_output_aliases={i: i for i in range(n)},
        compiler_params=pltpu.CompilerParams(has_side_effects=pltpu.SideEffectType.DATAFLOW_SIDE_EFFECTING),
    )(*arrays, send_sems, recv_sems, after))


def _gather_plan(n):
    def plan(bufs):
        x, y, c = lax.axis_index("x"), lax.axis_index("y"), lax.axis_index("c")
        me = 2 * x + y
        return [(bufs[i].at[me], bufs[i].at[me], (x ^ fx, y ^ fy, c)) for fx, fy in _CHIP_FLIPS for i in range(n)]

    return plan


def _reduce_plan(specs, n_small):
    n = len(specs)

    def plan(arrs):
        x, y, c = lax.axis_index("x"), lax.axis_index("y"), lax.axis_index("c")
        slot = 4 * x + 2 * y + c
        out = []
        for fx, fy, fc in _DEVICE_FLIPS:
            peer = (x ^ fx, y ^ fy, c ^ fc)
            for i, (kind, (R, C)) in enumerate(specs):
                out.append((_piece(arrs[i], kind, R, C, 2 * peer[0] + peer[1], peer[2]), arrs[n + i].at[slot], peer))
            for s in range(n_small):
                out.append((arrs[2 * n + 2 * s], arrs[2 * n + 2 * s + 1].at[slot], peer))
        return out

    return plan


def _sibling_exchange(halves, name, small=None):
    n = len(halves)
    ns = 0 if small is None else 1

    def body(*refs):
        ins, outs = refs[:n], refs[n + ns:2 * n + ns]
        send_sems, recv_sems = refs[2 * (n + ns)], refs[2 * (n + ns) + 1]
        x, y, c = lax.axis_index("x"), lax.axis_index("y"), lax.axis_index("c")
        copies = [_remote(ins[i], outs[i], send_sems.at[i], recv_sems.at[i], (x, y, 1 - c)) for i in range(n)]
        waits = list(copies)
        if ns:
            s_ref, slots_ref, ssend_sems, srecv_sems, local_sem = refs[n], refs[2 * n + 1], *refs[2 * (n + ns) + 2:]
            slot = 4 * x + 2 * y + c
            own = pltpu.make_async_copy(s_ref, slots_ref.at[slot], local_sem)
            own.start()
            for k, (fx, fy, fc) in enumerate(_DEVICE_FLIPS):
                peer = (x ^ fx, y ^ fy, c ^ fc)
                copies.append(_remote(s_ref, slots_ref.at[slot], ssend_sems.at[k], srecv_sems.at[k], peer))
                theirs = slots_ref.at[slot ^ (k + 1)]
                waits.append(_remote(theirs, theirs, ssend_sems.at[k], srecv_sems.at[k], peer))
        for cp in copies:
            cp.start()
        for cp in waits:
            cp.wait()
        if ns:
            own.wait()

    dma = pltpu.SemaphoreType.DMA
    extra_in = [] if small is None else [small]
    extra_out = [] if small is None else [_sds((8,) + small.shape, F32)]
    return pl.pallas_call(
        body, name=name,
        in_specs=[_ANY] * (n + ns), out_specs=[_ANY] * (n + ns),
        out_shape=[_sds(h.shape, h.dtype) for h in halves] + extra_out,
        scratch_shapes=[dma((n,)), dma((n,))] + ([dma((7,)), dma((7,)), dma] if ns else []),
    )(*halves, *extra_in)


_BIG = [("w_in", (1024, 1156), "slab"), ("w_out", (512, 1024), "rows"), ("w_ff1", (1024, 1024), "cols"),
        ("w_ff2", (1024, 1024), "rows"), ("w_ple_gate", (256, 1024), "rows"), ("w_ple_proj", (256, 256), "cols")]
_SMALL = [("norm_mix_g", (1, 1024)), ("gm_v_norm_g", (1, 1024)), ("gm_ws", (1, 8, 128, 128)), ("gm_bs", (1, 8, 128)),
          ("gm_out_norm_g", (1, 1024)), ("ssd_conv_w", (1, 4, 1536)), ("ssd_conv_b", (1, 1536)),
          ("ssd_dt_bias", (1, 16)), ("ssd_a_log", (1, 16)), ("ssd_d", (1, 16)), ("ssd_norm_g", (1, 1024)),
          ("norm_mlp_g", (1, 1024)), ("ple_norm_g", (1, 1024)), ("final_norm_g", (1024,))]


def _rows128(a):
    flat = a.reshape(-1)
    rows = -(-flat.shape[0] // 1024) * 8
    return jnp.pad(flat, (0, rows * 128 - flat.shape[0])).reshape(rows, 128)


def _pad_lanes(v, n=128):
    v = v.reshape(1, -1)
    return jnp.pad(v, ((0, 0), (0, n - v.shape[1])))


_SMALL_SHAPES = dict(_SMALL + [("loss", ())])
_BIG_SPECS = {n: (kind, shp) for n, shp, kind in _BIG}


class _Comm:
    def __init__(self, a, kh):
        self.a, self.kh = a, kh
        rest = _BIG[1:]
        self.bufs = {"w_in": _cast_w_in(a["w_in"].transpose(2, 0, 1), kh)}
        cast = _cast_into_slot([a[n].reshape(shp) for n, shp, _ in rest], kh, "cast_rest")
        self.bufs.update({n: c for (n, _, _), c in zip(rest, cast)})
        self.sent = []
        self.small_packs = []

    def w_in(self):
        g_win, g_cw = _weight_gather([self.bufs["w_in"]], self.a["ssd_conv_w"].reshape(4, 384))
        token = g_cw
        self.gather = {}
        for tag, names in (("out", ["w_out", "w_ff1"]), ("ff", ["w_ff2", "w_ple_gate", "w_ple_proj"])):
            plan = _gather_plan(len(names))
            ssem, rsem, thru, token = _split_start("gather_start_" + tag, [self.bufs[n] for n in names],
                                                   3 * len(names), plan, after=token)
            self.gather[tag] = (plan, ssem, rsem, thru)
        wm, wdt = _assemble_w_in(g_win)
        return wm, wdt, jnp.concatenate([g_cw[k] for k in range(4)], axis=1), token

    def rest(self, tag, after):
        plan, ssem, rsem, thru = self.gather[tag]
        got = _split_wait("gather_wait_" + tag, thru, ssem, rsem, plan, after)
        if tag == "out":
            return got[0].reshape(2048, D), got[1]
        g_w2, g_wg, g_wp = got
        return g_w2.reshape(DFF, D), g_wg.reshape(D, D), g_wp

    def send(self, tag, grads):
        big = [n for n, _, _ in _BIG if n in grads]
        small = [n for n in _SMALL_SHAPES if n in grads]
        parts = [_rows128(grads[n]) for n in small]
        rows = [s.shape[0] for s in parts]
        if not big:
            self.last_small = (tag, small, rows, jnp.concatenate(parts, axis=0))
            return None
        srcs = [grads[n] for n in big]
        lands = [lax.empty((8, _BIG_SPECS[n][1][0] // 2, _BIG_SPECS[n][1][1]), GRAD) for n in big]
        extra = []
        if small:
            pack = jnp.concatenate(parts, axis=0)
            extra = [pack, jnp.broadcast_to(pack, (8,) + pack.shape)]
        plan = _reduce_plan([_BIG_SPECS[n] for n in big], len(extra) // 2)
        n_copies = 7 * (len(big) + len(extra) // 2)
        ssem, rsem, thru, token = _split_start("reduce_start_" + tag, srcs + lands + extra, n_copies, plan)
        self.sent.append((tag, big, small, rows, plan, ssem, rsem, thru))
        return token

    def finish(self, after):
        a, results = self.a, {}

        def update(names, own, tag):
            if names == ["w_in"]:
                stag, small, rows, pack = self.last_small
                *other, slots = _sibling_exchange([own[n] for n in names], "sibling_exchange_" + tag, pack)
                self.small_packs.append((small, rows, slots))
                w, m, v = (a[k].transpose(2, 0, 1) for k in ("w_in", "m_w_in", "v_w_in"))
                raw = _adamw_transposed(w, own["w_in"], other[0], m, v, "adamw_" + tag)
                results["w_in"] = tuple(r.transpose(1, 2, 0) for r in raw)
                return raw[1]
            other = _sibling_exchange([own[n] for n in names], "sibling_exchange_" + tag)
            items = [(a[n].reshape(_BIG_SPECS[n][1]), own[n], oth, a["m_" + n].reshape(_BIG_SPECS[n][1]),
                      a["v_" + n].reshape(_BIG_SPECS[n][1])) for n, oth in zip(names, other)]
            results.update(zip(names, _adamw_halves(items, "adamw_" + tag)))
            return results[names[-1]][1]

        own, early = {}, []
        for tag, big, small, rows, plan, ssem, rsem, thru in self.sent:
            if tag == self.sent[-1][0]:
                after = update(early, own, "early")
            arrs = _split_wait("reduce_wait_" + tag, thru, ssem, rsem, plan, after)
            nb_ = len(big)
            sums = _sum_slots([(arrs[nb_ + i], arrs[i]) + _BIG_SPECS[n] for i, n in enumerate(big)], self.kh,
                              "sum_" + tag)
            own.update(zip(big, sums))
            after = sums[-1]
            early += big
            if small:
                self.small_packs.append((small, rows, arrs[2 * nb_ + 1]))
        update(self.sent[-1][1], own, "late")
        return results, self.small_packs


def _local_step(x, p, tgt, sm, comm, nb, tm):
    T = x.shape[0]
    wm, wdt, conv_w, token = comm.w_in()
    g_mix, gv, gout = sm["norm_mix_g"].reshape(1, D), sm["gm_v_norm_g"].reshape(1, D), sm["gm_out_norm_g"].reshape(1, D)
    ws = sm["gm_ws"].reshape(GM_HEADS, CH, CH)
    bst = jnp.pad(sm["gm_bs"].reshape(GM_HEADS, CH).T, ((0, 0), (0, 128 - GM_HEADS)))
    convw = jnp.pad(conv_w, ((0, 4), (0, 0)))
    convb = sm["ssd_conv_b"].reshape(1, CONV_CH)
    dtb, alog = _pad_lanes(sm["ssd_dt_bias"]), _pad_lanes(sm["ssd_a_log"])
    dskip = jnp.repeat(sm["ssd_d"].reshape(SSD_HEADS), SSD_P).reshape(1, 1024)
    ng, g_mlp, g_ple = sm["ssd_norm_g"].reshape(1, D), sm["norm_mlp_g"].reshape(1, D), sm["ple_norm_g"].reshape(1, D)
    gf = sm["final_norm_g"].reshape(1, D)
    head_of_lane = lax.broadcasted_iota(jnp.int32, (128, 1024), 1) // SSD_P
    ex = (lax.broadcasted_iota(jnp.int32, (128, 1024), 0) == head_of_lane).astype(BF16)
    ext = ex.T
    ltri = (lax.broadcasted_iota(jnp.int32, (CH, CH), 0) >= lax.broadcasted_iota(jnp.int32, (CH, CH), 1)).astype(F32)

    pz, pxbc, dtraw, xn, cat, uv = _inproj_gmlp(x, g_mix, wm, wdt, gv, ws, bst, gout, tm, token)
    cat, sall, conv = _ssd_fwd(pz, pxbc, dtraw, cat, convw, convb, dtb, alog, dskip, ng, ex, ltri, nb)
    wo, w1 = comm.rest("out", cat)
    h1, hn, hid = _outproj_ff1(cat, wo, x, g_mlp, w1, tm)
    w2, wg, wp = comm.rest("ff", hn)
    hp, dgl, dpe, dh2, dh2b, loss, d_gf, d_gple = _ff2_tail(hid, w2, h1, g_ple, p, tgt, wg, wp, gf, tm)

    d_wp = _matmul_tn(p, dpe, "dw_ple_proj", a_fn=lambda a: a.astype(MXU))
    d_wg = _matmul_tn(hp, dgl, "dw_ple_gate")
    d_w2 = _matmul_tn(hid, dh2b, "dw_ff2", a_fn=_sq)
    dpre = _ff2_bwd(dh2b, w2, hid, min(T, 2 * tm))
    d_w1 = _matmul_tn(hn, dpre, "dw_ff1")
    dh1, dh1b, d_gmlp = _ff1_bwd(dpre, w1, dh2, h1, g_mlp, tm)
    dcat = _outproj_bwd(dh1b, wo, min(T, 2 * tm))
    d_wo = _matmul_tn(cat, dh1b, "dw_out")
    duv, d_gv, d_ws, d_bst, d_gout, dxn_uv = _gmlp_bwd(uv, dcat, gv, ws, bst, gout, wm)
    token = comm.send("early", {
        "w_ple_proj": d_wp, "w_ple_gate": d_wg, "w_ff2": d_w2, "w_ff1": d_w1, "w_out": d_wo, "loss": loss[0:1, 0:1], "final_norm_g": d_gf, "ple_norm_g": d_gple, "norm_mlp_g": d_gmlp,
        "gm_v_norm_g": d_gv, "gm_ws": d_ws, "gm_bs": d_bst[:, :GM_HEADS].T, "gm_out_norm_g": d_gout})
    dssd, ddt, d_cw, d_cb, d_dtb, d_al, d_ds, d_ng = _ssd_bwd(
        pz, pxbc, conv, dtraw, sall, dcat, convw, dtb, alog, dskip, ng, ex, ltri, ext, nb, token)
    d_win = _split_dw_in(_matmul_tn(xn, duv, "dw_in_uv"), _matmul_tn(xn, dssd, "dw_in_ssd"),
                         _matmul_tn(xn, ddt, "dw_in_dt"))
    token = comm.send("late", {"w_in": d_win})
    dx, d_gmix = _inproj_bwd(dxn_uv, dssd, ddt, wm, wdt, dh1, x, g_mix, tm, token)
    comm.send("d", {"norm_mix_g": d_gmix, "ssd_conv_w": d_cw[0:4], "ssd_conv_b": d_cb, "ssd_dt_bias": d_dtb[:, :16],
                    "ssd_a_log": d_al[:, :16], "ssd_d": d_ds[:, :16], "ssd_norm_g": d_ng})
    return dx


def kernel(x, p, norm_mix_g, w_in, gm_v_norm_g, gm_ws, gm_bs, gm_out_norm_g, ssd_conv_w, ssd_conv_b, ssd_dt_bias, ssd_a_log, ssd_d, ssd_norm_g, w_out, norm_mlp_g, w_ff1, w_ff2, ple_norm_g, w_ple_gate, w_ple_proj, final_norm_g, loss_target, m_norm_mix_g, m_w_in, m_gm_v_norm_g, m_gm_ws, m_gm_bs, m_gm_out_norm_g, m_ssd_conv_w, m_ssd_conv_b, m_ssd_dt_bias, m_ssd_a_log, m_ssd_d, m_ssd_norm_g, m_w_out, m_norm_mlp_g, m_w_ff1, m_w_ff2, m_ple_norm_g, m_w_ple_gate, m_w_ple_proj, m_final_norm_g, v_norm_mix_g, v_w_in, v_gm_v_norm_g, v_gm_ws, v_gm_bs, v_gm_out_norm_g, v_ssd_conv_w, v_ssd_conv_b, v_ssd_dt_bias, v_ssd_a_log, v_ssd_d, v_ssd_norm_g, v_w_out, v_norm_mlp_g, v_w_ff1, v_w_ff2, v_ple_norm_g, v_w_ple_gate, v_w_ple_proj, v_final_norm_g):
    a = dict(locals())
    order = ["norm_mix_g", "w_in", "gm_v_norm_g", "gm_ws", "gm_bs", "gm_out_norm_g", "ssd_conv_w", "ssd_conv_b",
             "ssd_dt_bias", "ssd_a_log", "ssd_d", "ssd_norm_g", "w_out", "norm_mlp_g", "w_ff1", "w_ff2", "ple_norm_g",
             "w_ple_gate", "w_ple_proj", "final_norm_g"]
    chip = 2 * lax.axis_index("x") + lax.axis_index("y")
    nb, S = x.shape[0], x.shape[1]
    T = nb * S
    sm = {n: a[n] for n, _ in _SMALL if n != "ssd_conv_w"}
    comm = _Comm(a, jnp.stack([chip, lax.axis_index("c")]).astype(jnp.int32))
    dx = _local_step(x.reshape(T, D), p.reshape(T, DPLE), loss_target.reshape(T, D), sm, comm, nb, 512)
    big, small_packs = comm.finish(dx)
    small, loss = _small_update(a, small_packs)
    g_out, delta, new_m, new_v = {}, {}, {}, {}
    for n in order:
        g_out[n], delta[n], new_m[n], new_v[n] = (r.reshape(a[n].shape) for r in (big[n] if n in big else small[n]))
    return (loss.reshape(()), dx.reshape(x.shape), *[g_out[n] for n in order], *[delta[n] for n in order],
            *[new_m[n] for n in order], *[new_v[n] for n in order])
```

```python
import jax
import jax.numpy as jnp
from jax import lax
from jax.experimental import pallas as pl
from jax.experimental.pallas import tpu as pltpu

F32 = jnp.float32
BF16 = jnp.bfloat16
MXU = jnp.bfloat16
GRAD = jnp.bfloat16

D = 1024
CH = 128
GM_HEADS = 8
SSD_HEADS = 16
SSD_P = 64
CONV_CH = 1536
N_MAIN = 4608
DFF = 4096
DPLE = 256
EPS = 1e-6
NEG = -1e30

LR, B1, B2, ADAM_EPS, WD, STEP = 0.001, 0.9, 0.999, 1e-08, 0.01, 10

VMEM_LIMIT = 56 * 1024 * 1024
_SEQS_PER_STEP = 4
MESH = pl.DeviceIdType.MESH

INV_SQRT2 = 0.7071067811865476
INV_SQRT_2PI = 0.3989422804014327


def _cp(n_axes=1):
    return pltpu.CompilerParams(dimension_semantics=("arbitrary",) * n_axes, vmem_limit_bytes=VMEM_LIMIT)


def _dot(a, b):
    return jnp.dot(a, b, preferred_element_type=F32)


def _dot_nt(a, b):
    return lax.dot_general(a, b, (((1,), (1,)), ((), ())), preferred_element_type=F32)


def _dot_tn(a, b):
    return lax.dot_general(a, b, (((0,), (0,)), ((), ())), preferred_element_type=F32)


def _dot_hi(a, b):
    return jnp.dot(a, b, preferred_element_type=F32, precision=lax.Precision.HIGHEST)


def _dot_01(a, sel):
    hi = a.astype(BF16)
    lo = (a - hi.astype(F32)).astype(BF16)
    n = a.shape[0]
    r = _dot(jnp.concatenate([hi, lo], axis=0), sel)
    return r[0:n] + r[n:2 * n]


def _rows(tm, n, j=0):
    return pl.BlockSpec((tm, n), lambda i: (i, j))


def _const(shape):
    nd = len(shape)
    return pl.BlockSpec(shape, lambda *_: (0,) * nd)


def _sds(shape, dtype):
    return jax.ShapeDtypeStruct(shape, dtype)


def _rms(x):
    r = lax.rsqrt(jnp.mean(x * x, axis=-1, keepdims=True) + EPS)
    return x * r, r


def _rms_bwd(dy, xhat, r, g):
    dyg = dy * g
    return r * (dyg - xhat * jnp.mean(dyg * xhat, axis=-1, keepdims=True))


def _sigmoid(x):
    return 1.0 / (1.0 + jnp.exp(-x))


def _gelu(x):
    cdf = 0.5 * (1.0 + lax.erf(x * INV_SQRT2))
    pdf = jnp.exp(-0.5 * x * x) * INV_SQRT_2PI
    return x * cdf, cdf + x * pdf


def _softplus(x):
    e = jnp.exp(-jnp.abs(x))
    u = 1.0 + e
    log1p = jnp.where(u == 1.0, e, jnp.log(u) * e / (u - 1.0))
    return jnp.maximum(x, 0.0) + log1p


def _after(n_in, fn):
    def body(*refs):
        return fn(*refs[:n_in], *refs[n_in + 1:])

    return body


def _inproj_gmlp(x, g, wm, wdt, gv, ws, bst, gout, tm, after):
    T = x.shape[0]

    def body(x_ref, g_ref, wm_ref, wdt_ref, gv_ref, ws_ref, bst_ref, gout_ref,
             z_ref, xbc_ref, dt_ref, xn_ref, ya_ref, uv_ref):
        xh, _ = _rms(x_ref[...])
        xn = (xh * g_ref[...]).astype(MXU)
        xn_ref[...] = xn
        for n in range(4):
            uv_ref[:, n * 512:(n + 1) * 512] = _dot(xn, wm_ref[:, n * 512:(n + 1) * 512])
        for n in range(2):
            z_ref[:, n * 512:(n + 1) * 512] = _dot(xn, wm_ref[:, 2048 + n * 512:2048 + (n + 1) * 512])
        for n in range(3):
            xbc_ref[:, n * 512:(n + 1) * 512] = _dot(xn, wm_ref[:, 3072 + n * 512:3072 + (n + 1) * 512])
        dt_ref[...] = _dot(xn, wdt_ref[...])
        for k in range(tm // CH):
            rows = slice(k * CH, (k + 1) * CH)
            f = _gmlp_fwd_vals(uv_ref[rows, 0:1024], uv_ref[rows, 1024:2048], gv_ref[...], ws_ref, bst_ref[...],
                               gout_ref[...])
            ya_ref[rows, :] = f["out"].astype(MXU)

    return pl.pallas_call(
        _after(8, body), grid=(T // tm,), name="inproj_gmlp",
        in_specs=[_rows(tm, D), _const((1, D)), _const((D, N_MAIN)), _const((D, 128)), _const((1, 1024)),
                  _const((GM_HEADS, CH, CH)), _const((CH, 128)), _const((1, 1024)), _ANY],
        out_specs=[_rows(tm, 1024), _rows(tm, CONV_CH), _rows(tm, 128), _rows(tm, D), _rows(tm, 1024, 0),
                   _rows(tm, 2048)],
        out_shape=[_sds((T, 1024), F32), _sds((T, CONV_CH), F32), _sds((T, 128), F32), _sds((T, D), MXU),
                   _sds((T, 2048), MXU), _sds((T, 2048), F32)],
        compiler_params=_cp(),
    )(x, g, wm, wdt, gv, ws, bst, gout, after)


def _gmlp_fwd_vals(u, v, gv, ws_ref, bst, gout):
    ug, dug = _gelu(u)
    vg, dvg = _gelu(v)
    row = lax.broadcasted_iota(jnp.int32, (CH, CH), 0)
    col = lax.broadcasted_iota(jnp.int32, (CH, CH), 1)
    tril = row >= col
    ys, heads = [], []
    for h in range(GM_HEADS):
        sl = slice(h * 128, (h + 1) * 128)
        vhat, rv = _rms(vg[:, sl])
        vn = (vhat * gv[:, sl]).astype(MXU)
        wt = jnp.where(tril, ws_ref[h], 0.0)
        mixed = _dot(wt.astype(MXU), vn) + bst[:, h:h + 1]
        ys.append(ug[:, sl] * mixed)
        heads.append((vhat, rv, vn, wt, mixed))
    y = jnp.concatenate(ys, axis=1)
    yhat, ry = _rms(y)
    return dict(ug=ug, dug=dug, dvg=dvg, heads=heads, yhat=yhat, ry=ry, tril=tril, out=yhat * gout)


def _shifts_down(cur, halo):
    row8 = lax.broadcasted_iota(jnp.int32, (8, cur.shape[1]), 0)
    out = [cur]
    for j in (1, 2, 3):
        sh = pltpu.roll(cur, j, 0)
        top = jnp.where(row8 < j, pltpu.roll(halo, j, 0), sh[0:8])
        out.append(jnp.concatenate([top, sh[8:]], axis=0))
    return out


def _shifts_up(cur, halo):
    row8 = lax.broadcasted_iota(jnp.int32, (8, cur.shape[1]), 0)
    out = []
    for j in (1, 2, 3):
        sh = pltpu.roll(cur, CH - j, 0)
        bot = jnp.where(row8 + j >= 8, pltpu.roll(halo, 8 - j, 0), sh[CH - 8:CH])
        out.append(jnp.concatenate([sh[0:CH - 8], bot], axis=0))
    return out


def _conv(xbc, halo, convw, convb):
    sh = _shifts_down(xbc, halo)
    return convb + convw[3:4] * sh[0] + convw[2:3] * sh[1] + convw[1:2] * sh[2] + convw[0:1] * sh[3]


def _ssd_fwd_vals(z, conv, dtraw, dtb, alog, dskip, ng, ex, ltri, s_prev):
    sig_c = _sigmoid(conv)
    xa = conv * sig_c
    xs = xa[:, :1024]
    bm = [xa[:, 1024:1152], xa[:, 1152:1280]]
    cm = [xa[:, 1280:1408], xa[:, 1408:1536]]
    dtpre = dtraw + dtb
    dt = _softplus(dtpre)
    a_neg = -jnp.exp(alog)
    cs = _dot_hi(ltri, dt * a_neg)
    cst = cs.T
    last = cs[CH - 1:CH]
    ecs = jnp.exp(cs)
    dec = jnp.exp(last - cs)
    spread = _dot_01(jnp.concatenate([dt, ecs, dec], axis=0), ex)
    dte, ecse, dece = spread[0:CH], spread[CH:2 * CH], spread[2 * CH:3 * CH]
    cde = ecse[CH - 1:CH]
    de = dskip
    xdt = xs * dte
    row = lax.broadcasted_iota(jnp.int32, (CH, CH), 0)
    col = lax.broadcasted_iota(jnp.int32, (CH, CH), 1)
    tril = row >= col
    lo = col < SSD_P
    bmb = [b.astype(MXU) for b in bm]
    cmb = [c.astype(MXU) for c in cm]
    mg = [_dot_nt(cmb[g], bmb[g]) for g in range(2)]
    yd, lms, whs = [], [], []
    for q in range(8):
        g = q // 4
        xq = xdt[:, q * 128:(q + 1) * 128]
        acc = None
        for hh in range(2):
            h = 2 * q + hh
            seg = cs[:, h:h + 1] - cst[h:h + 1, :]
            lm = jnp.exp(jnp.where(tril, seg, NEG))
            wh = (mg[g] * lm).astype(MXU)
            xm = jnp.where(lo if hh == 0 else ~lo, xq, 0.0).astype(MXU)
            part = _dot(wh, xm)
            acc = part if acc is None else acc + part
            lms.append(lm)
            whs.append(wh)
        yd.append(acc)
    yd = jnp.concatenate(yd, axis=1)
    sb = s_prev.astype(MXU)
    yo = jnp.concatenate([_dot(cmb[g], sb[:, g * 512:(g + 1) * 512]) for g in range(2)], axis=1) * ecse
    xdec = (xdt * dece).astype(MXU)
    states = jnp.concatenate([_dot_tn(bmb[g], xdec[:, g * 512:(g + 1) * 512]) for g in range(2)], axis=1)
    s_next = s_prev * cde + states
    ypre = yd + yo + de * xs
    sig_z = _sigmoid(z)
    yg = ypre * z * sig_z
    outs, yhat, rr = [], [], []
    for g in range(2):
        sl = slice(g * 512, (g + 1) * 512)
        yh, r = _rms(yg[:, sl])
        yhat.append(yh)
        rr.append(r)
        outs.append(yh * ng[:, sl])
    return dict(sig_c=sig_c, xa=xa, xs=xs, bmb=bmb, cmb=cmb, dtpre=dtpre, dt=dt, a_neg=a_neg,
                cs=cs, last=last, ecs=ecs, dec=dec, dte=dte, ecse=ecse, dece=dece, cde=cde, de=de, xdt=xdt,
                mg=mg, lms=lms, whs=whs, lo=lo, yo=yo, sb=sb, xdec=xdec, s_next=s_next, ypre=ypre, sig_z=sig_z,
                yhat=yhat, rr=rr, out=jnp.concatenate(outs, axis=1))


def _ssd_fwd(pz, pxbc, dtraw, cat, convw, convb, dtb, alog, dskip, ng, ex, ltri, nb):
    T = pz.shape[0]
    S = T // nb
    nch = S // CH
    ns = _SEQS_PER_STEP if nb % _SEQS_PER_STEP == 0 else 1

    def body(z_ref, xbc_ref, halo_ref, dt_ref, cw_ref, cb_ref, dtb_ref, al_ref, ds_ref, ng_ref, ex_ref, lt_ref,
             cat_in_ref, yb_ref, sall_ref, conv_ref, s_ref):
        del cat_in_ref
        c = pl.program_id(1)

        @pl.when(c == 0)
        def _():
            s_ref[...] = jnp.zeros_like(s_ref)

        for i in range(ns):
            halo = jnp.where(c == 0, 0.0, halo_ref[i])
            s_prev = s_ref[i]
            sall_ref[i, 0] = s_prev
            conv = _conv(xbc_ref[i], halo, cw_ref[...], cb_ref[...])
            conv_ref[i] = conv
            f = _ssd_fwd_vals(z_ref[i], conv, dt_ref[i], dtb_ref[...], al_ref[...], ds_ref[...], ng_ref[...],
                              ex_ref[...], lt_ref[...], s_prev)
            s_ref[i] = f["s_next"]
            yb_ref[i] = f["out"].astype(MXU)

    def seq(width, col=0):
        return pl.BlockSpec((ns, CH, width), lambda b, c: (b, c, col))

    cat, sall, conv = pl.pallas_call(
        body, grid=(nb // ns, nch), name="ssd_fwd",
        in_specs=[seq(1024), seq(CONV_CH),
                  pl.BlockSpec((ns, 8, CONV_CH), lambda b, c: (b, jnp.maximum(c * (CH // 8) - 1, 0), 0)),
                  seq(128),
                  _const((8, CONV_CH)), _const((1, CONV_CH)), _const((1, 128)), _const((1, 128)), _const((1, 1024)),
                  _const((1, 1024)), _const((128, 1024)), _const((CH, CH)), _ANY],
        out_specs=[seq(1024, 1), pl.BlockSpec((ns, 1, 128, 1024), lambda b, c: (b, c, 0, 0)), seq(CONV_CH)],
        out_shape=[_sds((nb, S, 2048), MXU), _sds((nb, nch, 128, 1024), F32), _sds((nb, S, CONV_CH), F32)],
        scratch_shapes=[pltpu.VMEM((ns, 128, 1024), F32)],
        input_output_aliases={12: 0},
        compiler_params=_cp(2),
    )(pz.reshape(nb, S, 1024), pxbc.reshape(nb, S, CONV_CH), pxbc.reshape(nb, S, CONV_CH), dtraw.reshape(nb, S, 128),
      convw, convb, dtb, alog, dskip, ng, ex, ltri, cat.reshape(nb, S, 2048))
    return cat.reshape(T, 2048), sall, conv.reshape(T, CONV_CH)


def _outproj_ff1(cat, wo, x, g, w1, tm):
    T = x.shape[0]

    def body(cat_ref, wo_ref, x_ref, g_ref, w1_ref, h1_ref, hn_ref, hid_ref):
        h1 = x_ref[...] + _dot(cat_ref[...], wo_ref[...])
        h1_ref[...] = h1
        hn = (_rms(h1)[0] * g_ref[...]).astype(MXU)
        hn_ref[...] = hn
        for n in range(4):
            hid_ref[:, n * 1024:(n + 1) * 1024] = jnp.maximum(_dot(hn, w1_ref[n]), 0.0).astype(MXU)

    return pl.pallas_call(
        body, grid=(T // tm,), name="outproj_ff1",
        in_specs=[_rows(tm, 2048), _const((2048, D)), _rows(tm, D), _const((1, D)), _const((4, D, 1024))],
        out_specs=[_rows(tm, D), _rows(tm, D), _rows(tm, DFF)],
        out_shape=[_sds((T, D), F32), _sds((T, D), MXU), _sds((T, DFF), MXU)],
        compiler_params=_cp(),
    )(cat, wo, x, g, w1)


def _sq(hid):
    h = hid.astype(F32)
    return (h * h).astype(MXU)


def _ff2_tail(hid, w2, h1, g_ple, p, tgt, wg, wp, gf, tm):
    T = h1.shape[0]

    def body(hid_ref, w2_ref, h1_ref, g_ref, p_ref, t_ref, wg_ref, wp_ref, gf_ref,
             hp_ref, dgl_ref, dpe_ref, dh2_ref, dh2b_ref, loss_ref, dgf_ref, dg_ref):
        @pl.when(pl.program_id(0) == 0)
        def _():
            loss_ref[...] = jnp.zeros_like(loss_ref)
            dgf_ref[...] = jnp.zeros_like(dgf_ref)
            dg_ref[...] = jnp.zeros_like(dg_ref)

        h2 = h1_ref[...] + _dot(_sq(hid_ref[...]), w2_ref[...])
        h2h, r2 = _rms(h2)
        g_ple = g_ref[...]
        hp = (h2h * g_ple).astype(MXU)
        hp_ref[...] = hp
        gate = _sigmoid(_dot(hp, wg_ref[...]))
        pb = p_ref[...].astype(MXU)
        pe = jnp.concatenate([_dot(pb, wp_ref[k]) for k in range(4)], axis=1)
        h3 = h2 + gate * pe
        hh, r = _rms(h3)
        gf = gf_ref[...]
        diff = hh * gf - t_ref[...]
        loss_ref[...] += 0.5 * jnp.sum(jnp.mean(diff * diff, axis=-1, keepdims=True))
        dout = diff * (1.0 / D)
        dgf_ref[...] += jnp.sum(dout * hh, axis=0, keepdims=True)
        dh3 = _rms_bwd(dout, hh, r, gf)
        dgl = (dh3 * pe * gate * (1.0 - gate)).astype(MXU)
        dgl_ref[...] = dgl
        dpe_ref[...] = (dh3 * gate).astype(MXU)
        dhp = _dot_nt(dgl, wg_ref[...])
        dg_ref[...] += jnp.sum(dhp * h2h, axis=0, keepdims=True)
        dh2 = dh3 + _rms_bwd(dhp, h2h, r2, g_ple)
        dh2_ref[...] = dh2
        dh2b_ref[...] = dh2.astype(MXU)

    return pl.pallas_call(
        body, grid=(T // tm,), name="ff2_tail",
        in_specs=[_rows(tm, DFF), _const((DFF, D)), _rows(tm, D), _const((1, D)), _rows(tm, DPLE), _rows(tm, D),
                  _const((D, D)), _const((4, DPLE, 256)), _const((1, D))],
        out_specs=[_rows(tm, D), _rows(tm, D), _rows(tm, D), _rows(tm, D), _rows(tm, D), _const((8, 128)),
                   _const((1, D)), _const((1, D))],
        out_shape=[_sds((T, D), MXU), _sds((T, D), MXU), _sds((T, D), MXU), _sds((T, D), F32), _sds((T, D), MXU),
                   _sds((8, 128), F32), _sds((1, D), F32), _sds((1, D), F32)],
        compiler_params=_cp(),
    )(hid, w2, h1, g_ple, p, tgt, wg, wp, gf)


def _ff2_bwd(dh2b, w2, hid, tm):
    T = hid.shape[0]

    def body(dh2b_ref, w2_ref, hid_ref, dpre_ref):
        d = dh2b_ref[...]
        for n in range(DFF // 1024):
            sl = slice(n * 1024, (n + 1) * 1024)
            da = _dot_nt(d, w2_ref[sl, :])
            dpre_ref[:, sl] = (2.0 * da * hid_ref[:, sl].astype(F32)).astype(MXU)

    return pl.pallas_call(
        body, grid=(T // tm,), name="ff2_bwd",
        in_specs=[_rows(tm, D), _const((DFF, D)), _rows(tm, DFF)],
        out_specs=_rows(tm, DFF),
        out_shape=_sds((T, DFF), MXU),
        compiler_params=_cp(),
    )(dh2b, w2, hid)


def _ff1_bwd(dpre, w1, dh2, h1, g, wo, tm):
    T = h1.shape[0]

    def body(dpre_ref, w1_ref, dh2_ref, h1_ref, g_ref, wo_ref, dh1_ref, dh1b_ref, dg_ref, dcat_ref):
        @pl.when(pl.program_id(0) == 0)
        def _():
            dg_ref[...] = jnp.zeros_like(dg_ref)

        dhn = _dot_nt(dpre_ref[:, 0:1024], w1_ref[0])
        for k in range(1, 4):
            dhn = dhn + _dot_nt(dpre_ref[:, k * 1024:(k + 1) * 1024], w1_ref[k])
        hh, r = _rms(h1_ref[...])
        dg_ref[...] += jnp.sum(dhn * hh, axis=0, keepdims=True)
        dh1 = dh2_ref[...] + _rms_bwd(dhn, hh, r, g_ref[...])
        dh1_ref[...] = dh1
        dh1b = dh1.astype(MXU)
        dh1b_ref[...] = dh1b
        dcat_ref[:, 0:1024] = _dot_nt(dh1b, wo_ref[0:1024, :])
        dcat_ref[:, 1024:2048] = _dot_nt(dh1b, wo_ref[1024:2048, :])

    return pl.pallas_call(
        body, grid=(T // tm,), name="ff1_bwd",
        in_specs=[_rows(tm, DFF), _const((4, D, 1024)), _rows(tm, D), _rows(tm, D), _const((1, D)),
                  _const((2048, D))],
        out_specs=[_rows(tm, D), _rows(tm, D), _const((1, D)), _rows(tm, 2048)],
        out_shape=[_sds((T, D), F32), _sds((T, D), MXU), _sds((1, D), F32), _sds((T, 2048), F32)],
        compiler_params=_cp(),
    )(dpre, w1, dh2, h1, g, wo)


def _gmlp_bwd(uv, dcat, gv, ws, bst, gout, wm):
    T = uv.shape[0]
    nck = 4 if T % (4 * CH) == 0 else 1
    tb = nck * CH

    def body(uv_ref, dya_ref, gv_ref, ws_ref, bst_ref, gout_ref, wuv_ref, duv_ref, dgv_ref, dws_ref, dbst_ref,
             dgo_ref, dxn_ref):
        @pl.when(pl.program_id(0) == 0)
        def _():
            dgv_ref[...] = jnp.zeros_like(dgv_ref)
            dws_ref[...] = jnp.zeros_like(dws_ref)
            dbst_ref[...] = jnp.zeros_like(dbst_ref)
            dgo_ref[...] = jnp.zeros_like(dgo_ref)

        for k in range(nck):
            chunk(slice(k * CH, (k + 1) * CH), uv_ref, dya_ref, gv_ref, ws_ref, bst_ref, gout_ref, duv_ref,
                  dgv_ref, dws_ref, dbst_ref, dgo_ref)
        dxn_ref[...] = _dot_nt(duv_ref[...], wuv_ref[...])

    def chunk(rows, uv_ref, dya_ref, gv_ref, ws_ref, bst_ref, gout_ref, duv_ref, dgv_ref, dws_ref, dbst_ref,
              dgo_ref):
        gv = gv_ref[...]
        f = _gmlp_fwd_vals(uv_ref[rows, 0:1024], uv_ref[rows, 1024:2048], gv, ws_ref, bst_ref[...], gout_ref[...])
        dya = dya_ref[rows, :]
        dgo_ref[...] += jnp.sum(dya * f["yhat"], axis=0, keepdims=True)
        dy = _rms_bwd(dya, f["yhat"], f["ry"], gout_ref[...])
        lane = lax.broadcasted_iota(jnp.int32, (CH, 128), 1)
        dbs = jnp.zeros((CH, 128), F32)
        dug, dvg, dgvs = [], [], []
        for h in range(GM_HEADS):
            sl = slice(h * 128, (h + 1) * 128)
            vhat, rv, vn, wt, mixed = f["heads"][h]
            dyh = dy[:, sl]
            dug.append(dyh * mixed)
            dmixed = dyh * f["ug"][:, sl]
            dmb = dmixed.astype(MXU)
            dws_ref[h] += jnp.where(f["tril"], _dot_nt(dmb, vn), 0.0)
            dbs = dbs + jnp.where(lane == h, jnp.sum(dmixed, axis=1, keepdims=True), 0.0)
            dvn = _dot_tn(wt.astype(MXU), dmb)
            dgvs.append(jnp.sum(dvn * vhat, axis=0, keepdims=True))
            dvg.append(_rms_bwd(dvn, vhat, rv, gv[:, sl]))
        dbst_ref[...] += dbs
        dgv_ref[...] += jnp.concatenate(dgvs, axis=1)
        duv_ref[rows, 0:1024] = (jnp.concatenate(dug, axis=1) * f["dug"]).astype(MXU)
        duv_ref[rows, 1024:2048] = (jnp.concatenate(dvg, axis=1) * f["dvg"]).astype(MXU)

    return pl.pallas_call(
        body, grid=(T // tb,), name="gmlp_bwd",
        in_specs=[_rows(tb, 2048), _rows(tb, 1024, 0), _const((1, 1024)),
                  _const((GM_HEADS, CH, CH)), _const((CH, 128)), _const((1, 1024)), _const((D, 2048))],
        out_specs=[_rows(tb, 2048), _const((1, 1024)), _const((GM_HEADS, CH, CH)), _const((CH, 128)),
                   _const((1, 1024)), _rows(tb, D)],
        out_shape=[_sds((T, 2048), MXU), _sds((1, 1024), F32), _sds((GM_HEADS, CH, CH), F32), _sds((CH, 128), F32),
                   _sds((1, 1024), F32), _sds((T, D), F32)],
        compiler_params=_cp(),
    )(uv, dcat, gv, ws, bst, gout, wm)


def _ssd_bwd(pz, pxbc, conv, dtraw, sall, dcat, convw, dtb, alog, dskip, ng, ex, ltri, ext, nb, after):
    T = pz.shape[0]
    S = T // nb
    nch = S // CH
    ns = _SEQS_PER_STEP if nb % _SEQS_PER_STEP == 0 else 1

    def seq(width, col=0):
        return pl.BlockSpec((ns, CH, width), lambda b, c: (b, nch - 1 - c, col))

    in_specs = [
        seq(1024), seq(CONV_CH), seq(CONV_CH), seq(128),
        _const((8, CONV_CH)), _const((1, 128)), _const((1, 128)), _const((1, 1024)),
        _const((1, 1024)), _const((128, 1024)), _const((CH, CH)),
        _const((1024, 128)),
        pl.BlockSpec((ns, 1, 128, 1024), lambda b, c: (b, nch - 1 - c, 0, 0)),
        seq(1024, 1),
        _ANY,
    ]

    def body(z_ref, xbc_ref, conv_ref, dt_ref, cw_ref, dtb_ref, al_ref, ds_ref, ng_ref, ex_ref, lt_ref,
             ext_ref, sall_ref, dyb_ref,
             dssd_ref, ddt_ref, dcw_ref, dcb_ref, ddtb_ref, dal_ref, dds_ref, dng_ref,
             dst_ref, dnext_ref, ddse_ref):
        b = pl.program_id(0)
        c = pl.program_id(1)

        @pl.when((b == 0) & (c == 0))
        def _():
            for r in (dcw_ref, dcb_ref, ddtb_ref, dal_ref, dds_ref, dng_ref, ddse_ref):
                r[...] = jnp.zeros_like(r)

        @pl.when(c == 0)
        def _():
            dst_ref[...] = jnp.zeros_like(dst_ref)
            dnext_ref[...] = jnp.zeros_like(dnext_ref)

        ex = ex_ref[...]
        ext = ext_ref[...]
        cw = cw_ref[...]
        ng = ng_ref[...]
        for i in range(ns):
            one_chunk(i, ex, ext, cw, ng, z_ref, xbc_ref, conv_ref, dt_ref, dtb_ref, al_ref, ds_ref, lt_ref, sall_ref,
                      dyb_ref, dssd_ref, ddt_ref, dcw_ref, dcb_ref, ddtb_ref, dal_ref, dng_ref, dst_ref, dnext_ref,
                      ddse_ref)

        @pl.when((b == nb // ns - 1) & (c == nch - 1))
        def _():
            dds_ref[...] = _dot_01(jnp.broadcast_to(ddse_ref[...], (8, 1024)), ext)[0:1]

    def one_chunk(i, ex, ext, cw, ng, z_ref, xbc_ref, conv_ref, dt_ref, dtb_ref, al_ref, ds_ref, lt_ref, sall_ref,
                  dyb_ref, dssd_ref, ddt_ref, dcw_ref, dcb_ref, ddtb_ref, dal_ref, dng_ref, dst_ref, dnext_ref,
                  ddse_ref):
        z = z_ref[i]
        s_prev = sall_ref[i, 0]
        conv = conv_ref[i]
        f = _ssd_fwd_vals(z, conv, dt_ref[i], dtb_ref[...], al_ref[...], ds_ref[...], ng, ex, lt_ref[...], s_prev)
        xs, xdt, cs, dec, dt = f["xs"], f["xdt"], f["cs"], f["dec"], f["dt"]
        dyb = dyb_ref[i]
        dyg, dngs = [], []
        for g in range(2):
            sl = slice(g * 512, (g + 1) * 512)
            dngs.append(jnp.sum(dyb[:, sl] * f["yhat"][g], axis=0, keepdims=True))
            dyg.append(_rms_bwd(dyb[:, sl], f["yhat"][g], f["rr"][g], ng[:, sl]))
        dng_ref[...] += jnp.concatenate(dngs, axis=1)
        dyg = jnp.concatenate(dyg, axis=1)
        sig_z = f["sig_z"]
        silu_z = z * sig_z
        dy = dyg * silu_z
        dz = dyg * f["ypre"] * (sig_z + silu_z * (1.0 - sig_z))
        ddse_ref[...] += jnp.sum(dy * xs, axis=0, keepdims=True)
        dxs = dy * f["de"]
        dye = dy * f["ecse"]
        dyeb = dye.astype(MXU)
        dst = dst_ref[i]
        dstb = dst.astype(MXU)
        bmb, cmb, sb, xdec = f["bmb"], f["cmb"], f["sb"], f["xdec"]
        u = jnp.concatenate([_dot(bmb[g], dstb[:, g * 512:(g + 1) * 512]) for g in range(2)], axis=1)
        dxdt = [u[:, q * 128:(q + 1) * 128] * f["dece"][:, q * 128:(q + 1) * 128] for q in range(8)]
        per_head = _dot_01(jnp.concatenate(
            [dy * f["yo"], u * xdt, jnp.broadcast_to(jnp.sum(dst * s_prev, axis=0, keepdims=True), (8, 1024))],
            axis=0), ext)
        dcs = per_head[0:CH]
        t = per_head[CH:2 * CH] * dec
        dcd = per_head[2 * CH:2 * CH + 1]
        row = lax.broadcasted_iota(jnp.int32, (CH, 128), 0)
        lane = lax.broadcasted_iota(jnp.int32, (CH, 128), 1)
        cd = jnp.exp(f["last"])
        dcs = dcs - t + jnp.where(row == CH - 1, jnp.sum(t, axis=0, keepdims=True) + dcd * cd, 0.0)
        dcst = jnp.zeros((128, CH), F32)
        lo = f["lo"]
        dbm, dcm, ds_prev = [], [], []
        for g in range(2):
            sl = slice(g * 512, (g + 1) * 512)
            dmg = jnp.zeros((CH, CH), F32)
            for q in range(4 * g, 4 * g + 4):
                dyq = dy[:, q * 128:(q + 1) * 128]
                xq = xdt[:, q * 128:(q + 1) * 128].astype(MXU)
                for hh in range(2):
                    h = 2 * q + hh
                    m = lo if hh == 0 else ~lo
                    dym = jnp.where(m, dyq, 0.0).astype(MXU)
                    gh = _dot_nt(dym, xq)
                    gl = gh * f["lms"][h]
                    dmg = dmg + gl
                    qh = gl * f["mg"][g]
                    dcs = dcs + jnp.where(lane == h, jnp.sum(qh, axis=1, keepdims=True), 0.0)
                    dcst = dcst - jnp.where(row == h, jnp.sum(qh, axis=0, keepdims=True), 0.0)
                    dxdt[q] = dxdt[q] + _dot_tn(f["whs"][h], dym)
            dmgb = dmg.astype(MXU)
            dcm.append(_dot(dmgb, bmb[g]) + _dot_nt(dyeb[:, sl], sb[:, sl]))
            dbm.append(_dot_tn(dmgb, cmb[g]) + _dot_nt(xdec[:, sl], dstb[:, sl]))
            ds_prev.append(_dot_tn(cmb[g], dyeb[:, sl]))
        dst_ref[i] = jnp.concatenate(ds_prev, axis=1) + dst * f["cde"]
        dcs = dcs + dcst.T
        da = _dot_hi(lt_ref[...].T, dcs)
        dxdt = jnp.concatenate(dxdt, axis=1)
        a_neg = f["a_neg"]
        ddt = da * a_neg + _dot_01(dxdt * xs, ext)
        dal_ref[...] += jnp.sum(da * dt, axis=0, keepdims=True) * a_neg
        dxs = dxs + dxdt * f["dte"]
        ddtraw = jnp.where(lane < SSD_HEADS, ddt * _sigmoid(f["dtpre"]), 0.0)
        ddtb_ref[...] += jnp.sum(ddtraw, axis=0, keepdims=True)
        ddt_ref[i] = ddtraw.astype(MXU)
        dxa = jnp.concatenate([dxs, dbm[0], dbm[1], dcm[0], dcm[1]], axis=1)
        sig_c = f["sig_c"]
        dconv = dxa * (sig_c + f["xa"] * (1.0 - sig_c))
        dcb_ref[...] += jnp.sum(dconv, axis=0, keepdims=True)
        xbc = xbc_ref[i]
        dcw_ref[3:4, :] += jnp.sum(dconv * xbc, axis=0, keepdims=True)
        dxbc = cw[3:4] * dconv
        for j, up in zip((1, 2, 3), _shifts_up(dconv, dnext_ref[i])):
            dcw_ref[3 - j:4 - j, :] += jnp.sum(up * xbc, axis=0, keepdims=True)
            dxbc = dxbc + cw[3 - j:4 - j] * up
        dnext_ref[i] = dconv[0:8]
        dssd_ref[i, :, 0:1024] = dz.astype(MXU)
        dssd_ref[i, :, 1024:2560] = dxbc.astype(MXU)

    dssd, ddt, *small = pl.pallas_call(
        _after(14, body), grid=(nb // ns, nch), name="ssd_bwd",
        in_specs=in_specs,
        out_specs=[seq(2560), seq(128),
                   _const((8, CONV_CH)), _const((1, CONV_CH)), _const((1, 128)), _const((1, 128)), _const((1, 128)),
                   _const((1, 1024))],
        out_shape=[_sds((nb, S, 2560), MXU), _sds((nb, S, 128), MXU), _sds((8, CONV_CH), F32),
                   _sds((1, CONV_CH), F32), _sds((1, 128), F32), _sds((1, 128), F32), _sds((1, 128), F32),
                   _sds((1, 1024), F32)],
        scratch_shapes=[pltpu.VMEM((ns, 128, 1024), F32), pltpu.VMEM((ns, 8, CONV_CH), F32),
                        pltpu.VMEM((1, 1024), F32)],
        compiler_params=_cp(2),
    )(pz.reshape(nb, S, 1024), pxbc.reshape(nb, S, CONV_CH), conv.reshape(nb, S, CONV_CH), dtraw.reshape(nb, S, 128),
      convw, dtb, alog, dskip, ng, ex, ltri, ext, sall, dcat.reshape(nb, S, 2048), after)
    return (dssd.reshape(T, 2560), ddt.reshape(T, 128), *small)


def _inproj_bwd(dxn_uv, dssd, ddt, wm, wdt, dh1, x, g, tm, after):
    T = x.shape[0]

    def body(dxnuv_ref, dssd_ref, ddt_ref, wm_ref, wdt_ref, dh1_ref, x_ref, g_ref, dx_ref, dg_ref):
        @pl.when(pl.program_id(0) == 0)
        def _():
            dg_ref[...] = jnp.zeros_like(dg_ref)

        dxn = (dxnuv_ref[...] + _dot_nt(dssd_ref[...], wm_ref[:, 2048:N_MAIN])
               + _dot_nt(ddt_ref[...], wdt_ref[...]))
        xh, r = _rms(x_ref[...])
        dg_ref[...] += jnp.sum(dxn * xh, axis=0, keepdims=True)
        dx_ref[...] = dh1_ref[...] + _rms_bwd(dxn, xh, r, g_ref[...])

    return pl.pallas_call(
        _after(8, body), grid=(T // tm,), name="inproj_bwd",
        in_specs=[_rows(tm, D), _rows(tm, 2560), _rows(tm, 128), _const((D, N_MAIN)), _const((D, 128)),
                  _rows(tm, D), _rows(tm, D), _const((1, D)), _ANY],
        out_specs=[_rows(tm, D), _const((1, D))],
        out_shape=[_sds((T, D), F32), _sds((1, D), F32)],
        compiler_params=_cp(),
    )(dxn_uv, dssd, ddt, wm, wdt, dh1, x, g, after)


def _matmul_tn(a, b, name, a_fn=None):
    T, M = a.shape
    N = b.shape[1]
    tm = min(M, 1024)
    tn = 1280 if N == 2560 else min(N, 1024)
    tk = min(T, 2048)

    def body(a_ref, b_ref, o_ref, acc_ref):
        k = pl.program_id(2)

        @pl.when(k == 0)
        def _():
            acc_ref[...] = jnp.zeros_like(acc_ref)

        av = a_ref[...]
        if a_fn is not None:
            av = a_fn(av)
        acc_ref[...] += _dot_tn(av, b_ref[...])

        @pl.when(k == T // tk - 1)
        def _():
            o_ref[...] = acc_ref[...].astype(o_ref.dtype)

    return pl.pallas_call(
        body, grid=(M // tm, N // tn, T // tk), name=name,
        in_specs=[pl.BlockSpec((tk, tm), lambda i, j, k: (k, i)), pl.BlockSpec((tk, tn), lambda i, j, k: (k, j))],
        out_specs=pl.BlockSpec((tm, tn), lambda i, j, k: (i, j)),
        out_shape=_sds((M, N), GRAD),
        scratch_shapes=[pltpu.VMEM((tm, tn), F32)],
        compiler_params=_cp(3),
    )(a, b)


def _adamw_vals(w, g, m, v):
    m = B1 * m + (1.0 - B1) * g
    v = B2 * v + (1.0 - B2) * (g * g)
    m_hat = m / (1.0 - B1 ** STEP)
    v_hat = v / (1.0 - B2 ** STEP)
    return -LR * (m_hat / (jnp.sqrt(v_hat) + ADAM_EPS) + WD * w), m, v


_PARTS = 4


def _adamw_halves(items, name):
    n = len(items)

    def body(*refs):
        mine = (pl.program_id(0) // _PARTS) == lax.axis_index("c")
        for k in range(n):
            w_ref, own_ref, oth_ref, m_ref, v_ref = refs[5 * k:5 * k + 5]
            g_ref, d_ref, mo_ref, vo_ref = refs[5 * n + 4 * k:5 * n + 4 * k + 4]
            g = jnp.where(mine, own_ref[...], oth_ref[...])
            g_ref[...] = g
            d_ref[...], mo_ref[...], vo_ref[...] = _adamw_vals(w_ref[...], g, m_ref[...], v_ref[...])

    in_specs, out_specs, out_shape = [], [], []
    for w, *_ in items:
        R, C = w.shape
        full = _rows(R // (2 * _PARTS), C)
        part = pl.BlockSpec((R // (2 * _PARTS), C), lambda i: (i % _PARTS, 0))
        in_specs += [full, part, part, full, full]
        out_specs += [full] * 4
        out_shape += [_sds((R, C), F32)] * 4
    res = pl.pallas_call(
        body, grid=(2 * _PARTS,), name=name, in_specs=in_specs, out_specs=out_specs, out_shape=out_shape,
        compiler_params=_cp(),
    )(*[a for item in items for a in item])
    return [tuple(res[4 * k:4 * k + 4]) for k in range(n)]


_TJ = 128


def _adamw_transposed(w, own, other, m, v, name):
    C, _, R = w.shape

    def body(w_ref, own_ref, oth_ref, m_ref, v_ref, g_ref, d_ref, mo_ref, vo_ref):
        first = lax.axis_index("c") == 0
        g = jnp.concatenate([jnp.where(first, own_ref[...], oth_ref[...]),
                             jnp.where(first, oth_ref[...], own_ref[...])], axis=0).T
        d, mo, vo = _adamw_vals(w_ref[:, 0, :], g, m_ref[:, 0, :], v_ref[:, 0, :])
        for ref, val in ((g_ref, g), (d_ref, d), (mo_ref, mo), (vo_ref, vo)):
            ref[:, 0, :] = val

    cols = pl.BlockSpec((_TJ, 1, R), lambda j: (j, 0, 0))
    half = pl.BlockSpec((R // 2, _TJ), lambda j: (0, j))
    return pl.pallas_call(
        body, grid=(pl.cdiv(C, _TJ),), name=name,
        in_specs=[cols, half, half, cols, cols], out_specs=[cols] * 4, out_shape=[_sds((C, 1, R), F32)] * 4,
        compiler_params=_cp(),
    )(w, own, other, m, v)


def _lanes(rows):
    return jnp.concatenate([rows[i:i + 1, :] for i in range(rows.shape[0])], axis=1)


def _small_update(a, packs):
    names = [n for n, _ in _SMALL]
    where = {}
    for k, (pnames, rows, _) in enumerate(packs):
        o = 0
        for n, r in zip(pnames, rows):
            where[n] = (k, o, r)
            o += r
    view = {n: (1, 1024) for n in names}
    view.update(gm_ws=(1024, 128), gm_bs=(8, 128), ssd_conv_w=(4, 384), ssd_conv_b=(1, CONV_CH),
                ssd_dt_bias=(1, 16), ssd_a_log=(1, 16), ssd_d=(1, 16))
    npk = len(packs)

    def body(*refs):
        tots = []
        for k in range(npk):
            tot = refs[k][0]
            for d in range(1, 8):
                tot = tot + refs[k][d]
            tots.append(tot)
        ins, outs = refs[npk:npk + 3 * len(names)], refs[npk + 3 * len(names):]
        chip = 2 * lax.axis_index("x") + lax.axis_index("y")
        for i, n in enumerate(names):
            k, o, r = where[n]
            blk = tots[k][o:o + r, :]
            if n == "gm_ws":
                g = blk
            elif n == "gm_bs":
                g = blk[0:8]
            elif view[n] == (1, 16):
                g = blk[0:1, 0:16]
            elif n == "ssd_conv_w":
                taps = jnp.concatenate([_lanes(blk[12 * t:12 * t + 12]) for t in range(4)], axis=0)
                g = taps[:, 0:384]
                for c in range(1, 4):
                    g = jnp.where(chip == c, taps[:, 384 * c:384 * (c + 1)], g)
            else:
                g = _lanes(blk[0:view[n][1] // 128])
            d, mo, vo = _adamw_vals(ins[3 * i][...], g, ins[3 * i + 1][...], ins[3 * i + 2][...])
            for j, val in enumerate((g, d, mo, vo)):
                outs[4 * i + j][...] = val
        k, o, _ = where["loss"]
        outs[-1][...] = tots[k][o:o + 1, 0:1]

    ins = [a[pre + n].reshape(view[n]) for n in names for pre in ("", "m_", "v_")]
    res = pl.pallas_call(
        body, name="small_update",
        out_shape=[_sds(view[n], F32) for n in names for _ in range(4)] + [_sds((1, 1), F32)],
    )(*[slots for _, _, slots in packs], *ins)
    return {n: tuple(r.reshape(a[n].shape) for r in res[4 * i:4 * i + 4]) for i, n in enumerate(names)}, res[-1]


def _sum_slots(items, kh, name):
    n = len(items)
    in_specs, out_specs, out_shape = [], [], []
    for slots, src, kind, (R, C) in items:
        tr = R // (2 * _PARTS)
        if kind == "slab":
            src_spec = pl.BlockSpec((1, tr, C), lambda i, kh: (kh[0], kh[1] * _PARTS + i, 0))
        elif kind == "rows":
            src_spec = pl.BlockSpec((tr, C), lambda i, kh: (kh[0] * (2 * _PARTS) + kh[1] * _PARTS + i, 0))
        else:
            src_spec = pl.BlockSpec((tr, C), lambda i, kh: (kh[1] * _PARTS + i, kh[0]))
        in_specs += [pl.BlockSpec((8, tr, C), lambda i, kh: (0, i, 0)), src_spec]
        out_specs.append(pl.BlockSpec((tr, C), lambda i, kh: (i, 0)))
        out_shape.append(_sds((R // 2, C), F32))

    def body(kh_ref, *refs):
        me = 2 * kh_ref[0] + kh_ref[1]
        for k, (_, _, kind, _) in enumerate(items):
            s_ref, own_ref, o_ref = refs[2 * k], refs[2 * k + 1], refs[2 * n + k]
            acc = (own_ref[0] if kind == "slab" else own_ref[...]).astype(F32)
            for j in range(1, 8):
                acc = acc + s_ref[me ^ j].astype(F32)
            o_ref[...] = acc

    return pl.pallas_call(
        body, name=name,
        grid_spec=pltpu.PrefetchScalarGridSpec(
            num_scalar_prefetch=1, grid=(_PARTS,), in_specs=in_specs, out_specs=out_specs),
        out_shape=out_shape,
        compiler_params=_cp(),
    )(kh, *[a for slots, src, _, _ in items for a in (slots, src)])


def _assemble_w_in(slabs):
    tr = 256

    def body(s_ref, wm_ref, wdt_ref):
        full = jnp.concatenate([s_ref[k] for k in range(4)], axis=1)
        wm_ref[...] = full[:, :N_MAIN]
        wdt_ref[...] = jnp.concatenate([full[:, N_MAIN:], jnp.zeros((tr, 128 - 16), full.dtype)], axis=1)

    return pl.pallas_call(
        body, grid=(D // tr,), name="assemble_w_in",
        in_specs=[pl.BlockSpec((4, tr, 1156), lambda i: (0, i, 0))],
        out_specs=[_rows(tr, N_MAIN), _rows(tr, 128)],
        out_shape=[_sds((D, N_MAIN), slabs.dtype), _sds((D, 128), slabs.dtype)],
        compiler_params=_cp(),
    )(slabs)


def _split_dw_in(d_uv, d_ssd, d_dt):
    tr = 256

    def body(uv_ref, ssd_ref, dt_ref, o_ref):
        full = jnp.concatenate([uv_ref[...], ssd_ref[...], dt_ref[:, 0:16]], axis=1)
        for k in range(4):
            o_ref[k] = full[:, 1156 * k:1156 * (k + 1)]

    return pl.pallas_call(
        body, grid=(D // tr,), name="split_dw_in",
        in_specs=[_rows(tr, 2048), _rows(tr, 2560), _rows(tr, 128)],
        out_specs=pl.BlockSpec((4, tr, 1156), lambda i: (0, i, 0)),
        out_shape=_sds((4, D, 1156), d_uv.dtype),
        compiler_params=_cp(),
    )(d_uv, d_ssd, d_dt)


def _cast_w_in(w, kh):
    C, _, R = w.shape

    def body(kh_ref, w_ref, o_ref):
        o_ref[0] = w_ref[:, 0, :].T.astype(BF16)

    return pl.pallas_call(
        body, name="cast_w_in",
        grid_spec=pltpu.PrefetchScalarGridSpec(
            num_scalar_prefetch=1, grid=(pl.cdiv(C, _TJ),),
            in_specs=[pl.BlockSpec((_TJ, 1, R), lambda j, kh: (j, 0, 0))],
            out_specs=pl.BlockSpec((1, R, _TJ), lambda j, kh: (kh[0], 0, j))),
        out_shape=_sds((4, R, C), BF16),
        compiler_params=_cp(),
    )(kh, w)


def _cast_into_slot(ws, kh, name):
    n = len(ws)

    def body(kh_ref, *refs):
        for k in range(n):
            refs[n + k][0] = refs[k][...].astype(BF16)

    return pl.pallas_call(
        body, name=name,
        grid_spec=pltpu.PrefetchScalarGridSpec(
            num_scalar_prefetch=1, grid=(_PARTS,),
            in_specs=[pl.BlockSpec((w.shape[0] // _PARTS, w.shape[1]), lambda i, kh: (i, 0)) for w in ws],
            out_specs=[pl.BlockSpec((1, w.shape[0] // _PARTS, w.shape[1]), lambda i, kh: (kh[0], i, 0))
                       for w in ws]),
        out_shape=[_sds((4,) + w.shape, BF16) for w in ws],
        compiler_params=_cp(),
    )(kh, *ws)


_ANY = pl.BlockSpec(memory_space=pl.ANY)
_CHIP_FLIPS = [(1, 0), (0, 1), (1, 1)]
_DEVICE_FLIPS = [(fx, fy, fc) for fx in (0, 1) for fy in (0, 1) for fc in (0, 1)][1:]


def _half(h, rows):
    return pl.ds(pl.multiple_of(h * rows, rows), rows)


def _remote(src, dst, ssem, rsem, to):
    return pltpu.make_async_remote_copy(src_ref=src, dst_ref=dst, send_sem=ssem, recv_sem=rsem,
                                        device_id=to, device_id_type=MESH)


def _weight_gather(bufs, conv):
    n = len(bufs)

    def body(*refs):
        conv_ref, outs, conv_out = refs[n], refs[n + 1:2 * n + 1], refs[2 * n + 1]
        send_sems, recv_sems, fsend_sems, frecv_sems, csend_sems, crecv_sems, local_sem = refs[2 * n + 2:]
        x, y, c = lax.axis_index("x"), lax.axis_index("y"), lax.axis_index("c")
        me = 2 * x + y
        halves = [_half(c, r.shape[1] // 2) for r in outs]
        others = [_half(1 - c, r.shape[1] // 2) for r in outs]
        remote = _remote
        local = [pltpu.make_async_copy(conv_ref, conv_out.at[me], local_sem)]
        for cp in local:
            cp.start()
        sends = []
        for k, (fx, fy) in enumerate(_CHIP_FLIPS):
            peer = (x ^ fx, y ^ fy, c)
            for i in range(n):
                mine = outs[i].at[me, halves[i]]
                sends.append(remote(mine, mine, send_sems.at[k * n + i], recv_sems.at[k * n + i], peer))
            sends.append(remote(conv_ref, conv_out.at[me], csend_sems.at[k], crecv_sems.at[k], peer))
        for cp in sends:
            cp.start()
        sibling = (x, y, 1 - c)
        forwards = []
        for k, (fx, fy) in enumerate(_CHIP_FLIPS):
            peer = (x ^ fx, y ^ fy, c)
            src = 2 * (x ^ fx) + (y ^ fy)
            for i in range(n):
                landed = outs[i].at[src, halves[i]]
                remote(landed, landed, send_sems.at[k * n + i], recv_sems.at[k * n + i], peer).wait_recv()
                fw = remote(landed, landed, fsend_sems.at[k * n + i], frecv_sems.at[k * n + i], sibling)
                fw.start()
                forwards.append(fw)
            remote(conv_out.at[src], conv_out.at[src], csend_sems.at[k], crecv_sems.at[k], peer).wait_recv()
        for k, (fx, fy) in enumerate(_CHIP_FLIPS):
            src = 2 * (x ^ fx) + (y ^ fy)
            for i in range(n):
                theirs = outs[i].at[src, others[i]]
                remote(theirs, theirs, fsend_sems.at[k * n + i], frecv_sems.at[k * n + i], sibling).wait_recv()
        for cp in sends + forwards:
            cp.wait_send()
        for cp in local:
            cp.wait()

    dma = pltpu.SemaphoreType.DMA
    return pl.pallas_call(
        body, name="weight_gather",
        in_specs=[_ANY] * (n + 1), out_specs=[_ANY] * (n + 1),
        out_shape=[_sds(b.shape, b.dtype) for b in bufs] + [_sds((4,) + conv.shape, conv.dtype)],
        input_output_aliases={i: i for i in range(n)},
        scratch_shapes=[dma((3 * n,)), dma((3 * n,)), dma((3 * n,)), dma((3 * n,)), dma((3,)), dma((3,)), dma],
    )(*bufs, conv)


def _piece(ref, kind, R, C, k, h):
    if kind == "slab":
        return ref.at[k, _half(h, R // 2), :]
    if kind == "rows":
        return ref.at[pl.ds(pl.multiple_of(k * R + h * (R // 2), R // 2), R // 2), :]
    return ref.at[_half(h, R // 2), pl.ds(pl.multiple_of(k * C, C), C)]


_HBM = pl.BlockSpec(memory_space=pltpu.HBM)
_SEM = pl.BlockSpec(memory_space=pltpu.SEMAPHORE)


def _split_start(name, arrays, n_copies, plan, after=None):
    n = len(arrays)
    extra = [] if after is None else [after]

    def body(*refs):
        m = n + len(extra)
        arrs, send_sems, recv_sems, token = refs[:n], refs[m], refs[m + 1], refs[-1]
        for j, (src, dst, peer) in enumerate(plan(arrs)):
            _remote(src, dst, send_sems.at[j], recv_sems.at[j], peer).start()
        token[...] = jnp.zeros_like(token)

    dma = pltpu.SemaphoreType.DMA
    res = pl.pallas_call(
        body, name=name,
        out_shape=(dma((n_copies,)), dma((n_copies,)), *[pltpu.HBM(a.shape, a.dtype) for a in arrays],
                   _sds((8, 128), F32)),
        in_specs=[_HBM] * n + [_ANY] * len(extra),
        out_specs=(_SEM, _SEM, *[_HBM] * n, pl.BlockSpec(memory_space=pltpu.VMEM)),
        input_output_aliases={i: 2 + i for i in range(n)},
        compiler_params=pltpu.CompilerParams(has_side_effects=pltpu.SideEffectType.DATAFLOW_SIDE_EFFECTING),
    )(*[pltpu.with_memory_space_constraint(a, pltpu.HBM) for a in arrays], *extra)
    return res[0], res[1], list(res[2:2 + n]), res[-1]


def _split_wait(name, arrays, send_sems, recv_sems, plan, after):
    n = len(arrays)

    def body(*refs):
        arrs, ssems, rsems = refs[:n], refs[n], refs[n + 1]
        for j, (src, dst, peer) in enumerate(plan(arrs)):
            cp = _remote(src, dst, ssems.at[j], rsems.at[j], peer)
            cp.wait_send()
            cp.wait_recv()

    return list(pl.pallas_call(
        body, name=name,
        out_shape=tuple(pltpu.HBM(a.shape, a.dtype) for a in arrays),
        in_specs=[_HBM] * n + [_SEM, _SEM, _ANY],
        out_specs=tuple([_HBM] * n),
        input_output_aliases={i: i for i in range(n)},
        compiler_params=pltpu.CompilerParams(has_side_effects=pltpu.SideEffectType.DATAFLOW_SIDE_EFFECTING),
    )(*arrays, send_sems, recv_sems, after))


def _gather_plan(n):
    def plan(bufs):
        x, y, c = lax.axis_index("x"), lax.axis_index("y"), lax.axis_index("c")
        me = 2 * x + y
        return [(bufs[i].at[me], bufs[i].at[me], (x ^ fx, y ^ fy, c)) for fx, fy in _CHIP_FLIPS for i in range(n)]

    return plan


def _reduce_plan(specs, n_small):
    n = len(specs)

    def plan(arrs):
        x, y, c = lax.axis_index("x"), lax.axis_index("y"), lax.axis_index("c")
        slot = 4 * x + 2 * y + c
        out = []
        for fx, fy, fc in _DEVICE_FLIPS:
            peer = (x ^ fx, y ^ fy, c ^ fc)
            for i, (kind, (R, C)) in enumerate(specs):
                out.append((_piece(arrs[i], kind, R, C, 2 * peer[0] + peer[1], peer[2]), arrs[n + i].at[slot], peer))
            for s in range(n_small):
                out.append((arrs[2 * n + 2 * s], arrs[2 * n + 2 * s + 1].at[slot], peer))
        return out

    return plan


def _sibling_exchange(halves, name, small=None):
    n = len(halves)
    ns = 0 if small is None else 1

    def body(*refs):
        ins, outs = refs[:n], refs[n + ns:2 * n + ns]
        send_sems, recv_sems = refs[2 * (n + ns)], refs[2 * (n + ns) + 1]
        x, y, c = lax.axis_index("x"), lax.axis_index("y"), lax.axis_index("c")
        copies = [_remote(ins[i], outs[i], send_sems.at[i], recv_sems.at[i], (x, y, 1 - c)) for i in range(n)]
        waits = list(copies)
        if ns:
            s_ref, slots_ref, ssend_sems, srecv_sems, local_sem = refs[n], refs[2 * n + 1], *refs[2 * (n + ns) + 2:]
            slot = 4 * x + 2 * y + c
            own = pltpu.make_async_copy(s_ref, slots_ref.at[slot], local_sem)
            own.start()
            for k, (fx, fy, fc) in enumerate(_DEVICE_FLIPS):
                peer = (x ^ fx, y ^ fy, c ^ fc)
                copies.append(_remote(s_ref, slots_ref.at[slot], ssend_sems.at[k], srecv_sems.at[k], peer))
                theirs = slots_ref.at[slot ^ (k + 1)]
                waits.append(_remote(theirs, theirs, ssend_sems.at[k], srecv_sems.at[k], peer))
        for cp in copies:
            cp.start()
        for cp in waits:
            cp.wait()
        if ns:
            own.wait()

    dma = pltpu.SemaphoreType.DMA
    extra_in = [] if small is None else [small]
    extra_out = [] if small is None else [_sds((8,) + small.shape, F32)]
    return pl.pallas_call(
        body, name=name,
        in_specs=[_ANY] * (n + ns), out_specs=[_ANY] * (n + ns),
        out_shape=[_sds(h.shape, h.dtype) for h in halves] + extra_out,
        scratch_shapes=[dma((n,)), dma((n,))] + ([dma((7,)), dma((7,)), dma] if ns else []),
    )(*halves, *extra_in)


_BIG = [("w_in", (1024, 1156), "slab"), ("w_out", (512, 1024), "rows"), ("w_ff1", (1024, 1024), "cols"),
        ("w_ff2", (1024, 1024), "rows"), ("w_ple_gate", (256, 1024), "rows"), ("w_ple_proj", (256, 256), "cols")]
_SMALL = [("norm_mix_g", (1, 1024)), ("gm_v_norm_g", (1, 1024)), ("gm_ws", (1, 8, 128, 128)), ("gm_bs", (1, 8, 128)),
          ("gm_out_norm_g", (1, 1024)), ("ssd_conv_w", (1, 4, 1536)), ("ssd_conv_b", (1, 1536)),
          ("ssd_dt_bias", (1, 16)), ("ssd_a_log", (1, 16)), ("ssd_d", (1, 16)), ("ssd_norm_g", (1, 1024)),
          ("norm_mlp_g", (1, 1024)), ("ple_norm_g", (1, 1024)), ("final_norm_g", (1024,))]


def _rows128(a):
    flat = a.reshape(-1)
    rows = -(-flat.shape[0] // 1024) * 8
    return jnp.pad(flat, (0, rows * 128 - flat.shape[0])).reshape(rows, 128)


def _pad_lanes(v, n=128):
    v = v.reshape(1, -1)
    return jnp.pad(v, ((0, 0), (0, n - v.shape[1])))


_SMALL_SHAPES = dict(_SMALL + [("loss", ())])
_BIG_SPECS = {n: (kind, shp) for n, shp, kind in _BIG}


class _Comm:
    def __init__(self, a, kh):
        self.a, self.kh = a, kh
        rest = _BIG[1:]
        self.bufs = {"w_in": _cast_w_in(a["w_in"].transpose(2, 0, 1), kh)}
        cast = _cast_into_slot([a[n].reshape(shp) for n, shp, _ in rest], kh, "cast_rest")
        self.bufs.update({n: c for (n, _, _), c in zip(rest, cast)})
        self.sent = []
        self.small_packs = []

    def w_in(self):
        g_win, g_cw = _weight_gather([self.bufs["w_in"]], self.a["ssd_conv_w"].reshape(4, 384))
        token = g_cw
        self.gather = {}
        for tag, names in (("out", ["w_out", "w_ff1"]), ("ff", ["w_ff2", "w_ple_gate", "w_ple_proj"])):
            plan = _gather_plan(len(names))
            ssem, rsem, thru, token = _split_start("gather_start_" + tag, [self.bufs[n] for n in names],
                                                   3 * len(names), plan, after=token)
            self.gather[tag] = (plan, ssem, rsem, thru)
        wm, wdt = _assemble_w_in(g_win)
        return wm, wdt, jnp.concatenate([g_cw[k] for k in range(4)], axis=1), token

    def rest(self, tag, after):
        plan, ssem, rsem, thru = self.gather[tag]
        got = _split_wait("gather_wait_" + tag, thru, ssem, rsem, plan, after)
        if tag == "out":
            return got[0].reshape(2048, D), got[1]
        g_w2, g_wg, g_wp = got
        return g_w2.reshape(DFF, D), g_wg.reshape(D, D), g_wp

    def send(self, tag, grads):
        big = [n for n, _, _ in _BIG if n in grads]
        small = [n for n in _SMALL_SHAPES if n in grads]
        parts = [_rows128(grads[n]) for n in small]
        rows = [s.shape[0] for s in parts]
        if not big:
            self.last_small = (tag, small, rows, jnp.concatenate(parts, axis=0))
            return None
        srcs = [grads[n] for n in big]
        lands = [lax.empty((8, _BIG_SPECS[n][1][0] // 2, _BIG_SPECS[n][1][1]), GRAD) for n in big]
        extra = []
        if small:
            pack = jnp.concatenate(parts, axis=0)
            extra = [pack, jnp.broadcast_to(pack, (8,) + pack.shape)]
        plan = _reduce_plan([_BIG_SPECS[n] for n in big], len(extra) // 2)
        n_copies = 7 * (len(big) + len(extra) // 2)
        ssem, rsem, thru, token = _split_start("reduce_start_" + tag, srcs + lands + extra, n_copies, plan)
        self.sent.append((tag, big, small, rows, plan, ssem, rsem, thru))
        return token

    def finish(self, after):
        a, results = self.a, {}

        def update(names, own, tag):
            if names == ["w_in"]:
                stag, small, rows, pack = self.last_small
                *other, slots = _sibling_exchange([own[n] for n in names], "sibling_exchange_" + tag, pack)
                self.small_packs.append((small, rows, slots))
                w, m, v = (a[k].transpose(2, 0, 1) for k in ("w_in", "m_w_in", "v_w_in"))
                raw = _adamw_transposed(w, own["w_in"], other[0], m, v, "adamw_" + tag)
                results["w_in"] = tuple(r.transpose(1, 2, 0) for r in raw)
                return raw[1]
            other = _sibling_exchange([own[n] for n in names], "sibling_exchange_" + tag)
            items = [(a[n].reshape(_BIG_SPECS[n][1]), own[n], oth, a["m_" + n].reshape(_BIG_SPECS[n][1]),
                      a["v_" + n].reshape(_BIG_SPECS[n][1])) for n, oth in zip(names, other)]
            results.update(zip(names, _adamw_halves(items, "adamw_" + tag)))
            return results[names[-1]][1]

        own, early = {}, []
        for tag, big, small, rows, plan, ssem, rsem, thru in self.sent:
            if tag == self.sent[-1][0]:
                after = update(early, own, "early")
            arrs = _split_wait("reduce_wait_" + tag, thru, ssem, rsem, plan, after)
            nb_ = len(big)
            sums = _sum_slots([(arrs[nb_ + i], arrs[i]) + _BIG_SPECS[n] for i, n in enumerate(big)], self.kh,
                              "sum_" + tag)
            own.update(zip(big, sums))
            after = sums[-1]
            early += big
            if small:
                self.small_packs.append((small, rows, arrs[2 * nb_ + 1]))
        update(self.sent[-1][1], own, "late")
        return results, self.small_packs


def _local_step(x, p, tgt, sm, comm, nb, tm):
    T = x.shape[0]
    wm, wdt, conv_w, token = comm.w_in()
    g_mix, gv, gout = sm["norm_mix_g"].reshape(1, D), sm["gm_v_norm_g"].reshape(1, D), sm["gm_out_norm_g"].reshape(1, D)
    ws = sm["gm_ws"].reshape(GM_HEADS, CH, CH)
    bst = jnp.pad(sm["gm_bs"].reshape(GM_HEADS, CH).T, ((0, 0), (0, 128 - GM_HEADS)))
    convw = jnp.pad(conv_w, ((0, 4), (0, 0)))
    convb = sm["ssd_conv_b"].reshape(1, CONV_CH)
    dtb, alog = _pad_lanes(sm["ssd_dt_bias"]), _pad_lanes(sm["ssd_a_log"])
    dskip = jnp.repeat(sm["ssd_d"].reshape(SSD_HEADS), SSD_P).reshape(1, 1024)
    ng, g_mlp, g_ple = sm["ssd_norm_g"].reshape(1, D), sm["norm_mlp_g"].reshape(1, D), sm["ple_norm_g"].reshape(1, D)
    gf = sm["final_norm_g"].reshape(1, D)
    head_of_lane = lax.broadcasted_iota(jnp.int32, (128, 1024), 1) // SSD_P
    ex = (lax.broadcasted_iota(jnp.int32, (128, 1024), 0) == head_of_lane).astype(BF16)
    ext = ex.T
    ltri = (lax.broadcasted_iota(jnp.int32, (CH, CH), 0) >= lax.broadcasted_iota(jnp.int32, (CH, CH), 1)).astype(F32)

    pz, pxbc, dtraw, xn, cat, uv = _inproj_gmlp(x, g_mix, wm, wdt, gv, ws, bst, gout, tm, token)
    cat, sall, conv = _ssd_fwd(pz, pxbc, dtraw, cat, convw, convb, dtb, alog, dskip, ng, ex, ltri, nb)
    wo, w1 = comm.rest("out", cat)
    h1, hn, hid = _outproj_ff1(cat, wo, x, g_mlp, w1, tm)
    w2, wg, wp = comm.rest("ff", hn)
    hp, dgl, dpe, dh2, dh2b, loss, d_gf, d_gple = _ff2_tail(hid, w2, h1, g_ple, p, tgt, wg, wp, gf, tm)

    d_wp = _matmul_tn(p, dpe, "dw_ple_proj", a_fn=lambda a: a.astype(MXU))
    d_wg = _matmul_tn(hp, dgl, "dw_ple_gate")
    d_w2 = _matmul_tn(hid, dh2b, "dw_ff2", a_fn=_sq)
    dpre = _ff2_bwd(dh2b, w2, hid, min(T, 2 * tm))
    d_w1 = _matmul_tn(hn, dpre, "dw_ff1")
    dh1, dh1b, d_gmlp, dcat = _ff1_bwd(dpre, w1, dh2, h1, g_mlp, wo, tm)
    d_wo = _matmul_tn(cat, dh1b, "dw_out")
    duv, d_gv, d_ws, d_bst, d_gout, dxn_uv = _gmlp_bwd(uv, dcat, gv, ws, bst, gout, wm)
    token = comm.send("early", {
        "w_ple_proj": d_wp, "w_ple_gate": d_wg, "w_ff2": d_w2, "w_ff1": d_w1, "w_out": d_wo, "loss": loss[0:1, 0:1], "final_norm_g": d_gf, "ple_norm_g": d_gple, "norm_mlp_g": d_gmlp,
        "gm_v_norm_g": d_gv, "gm_ws": d_ws, "gm_bs": d_bst[:, :GM_HEADS].T, "gm_out_norm_g": d_gout})
    dssd, ddt, d_cw, d_cb, d_dtb, d_al, d_ds, d_ng = _ssd_bwd(
        pz, pxbc, conv, dtraw, sall, dcat, convw, dtb, alog, dskip, ng, ex, ltri, ext, nb, token)
    d_win = _split_dw_in(_matmul_tn(xn, duv, "dw_in_uv"), _matmul_tn(xn, dssd, "dw_in_ssd"),
                         _matmul_tn(xn, ddt, "dw_in_dt"))
    token = comm.send("late", {"w_in": d_win})
    dx, d_gmix = _inproj_bwd(dxn_uv, dssd, ddt, wm, wdt, dh1, x, g_mix, tm, token)
    comm.send("d", {"norm_mix_g": d_gmix, "ssd_conv_w": d_cw[0:4], "ssd_conv_b": d_cb, "ssd_dt_bias": d_dtb[:, :16],
                    "ssd_a_log": d_al[:, :16], "ssd_d": d_ds[:, :16], "ssd_norm_g": d_ng})
    return dx


def kernel(x, p, norm_mix_g, w_in, gm_v_norm_g, gm_ws, gm_bs, gm_out_norm_g, ssd_conv_w, ssd_conv_b, ssd_dt_bias, ssd_a_log, ssd_d, ssd_norm_g, w_out, norm_mlp_g, w_ff1, w_ff2, ple_norm_g, w_ple_gate, w_ple_proj, final_norm_g, loss_target, m_norm_mix_g, m_w_in, m_gm_v_norm_g, m_gm_ws, m_gm_bs, m_gm_out_norm_g, m_ssd_conv_w, m_ssd_conv_b, m_ssd_dt_bias, m_ssd_a_log, m_ssd_d, m_ssd_norm_g, m_w_out, m_norm_mlp_g, m_w_ff1, m_w_ff2, m_ple_norm_g, m_w_ple_gate, m_w_ple_proj, m_final_norm_g, v_norm_mix_g, v_w_in, v_gm_v_norm_g, v_gm_ws, v_gm_bs, v_gm_out_norm_g, v_ssd_conv_w, v_ssd_conv_b, v_ssd_dt_bias, v_ssd_a_log, v_ssd_d, v_ssd_norm_g, v_w_out, v_norm_mlp_g, v_w_ff1, v_w_ff2, v_ple_norm_g, v_w_ple_gate, v_w_ple_proj, v_final_norm_g):
    a = dict(locals())
    order = ["norm_mix_g", "w_in", "gm_v_norm_g", "gm_ws", "gm_bs", "gm_out_norm_g", "ssd_conv_w", "ssd_conv_b",
             "ssd_dt_bias", "ssd_a_log", "ssd_d", "ssd_norm_g", "w_out", "norm_mlp_g", "w_ff1", "w_ff2", "ple_norm_g",
             "w_ple_gate", "w_ple_proj", "final_norm_g"]
    chip = 2 * lax.axis_index("x") + lax.axis_index("y")
    nb, S = x.shape[0], x.shape[1]
    T = nb * S
    sm = {n: a[n] for n, _ in _SMALL if n != "ssd_conv_w"}
    comm = _Comm(a, jnp.stack([chip, lax.axis_index("c")]).astype(jnp.int32))
    dx = _local_step(x.reshape(T, D), p.reshape(T, DPLE), loss_target.reshape(T, D), sm, comm, nb, 512)
    big, small_packs = comm.finish(dx)
    small, loss = _small_update(a, small_packs)
    g_out, delta, new_m, new_v = {}, {}, {}, {}
    for n in order:
        g_out[n], delta[n], new_m[n], new_v[n] = (r.reshape(a[n].shape) for r in (big[n] if n in big else small[n]))
    return (loss.reshape(()), dx.reshape(x.shape), *[g_out[n] for n in order], *[delta[n] for n in order],
            *[new_m[n] for n in order], *[new_v[n] for n in order])
```

```python
import jax
import jax.numpy as jnp
from jax import lax
from jax.experimental import pallas as pl
from jax.experimental.pallas import tpu as pltpu

F32 = jnp.float32
BF16 = jnp.bfloat16
MXU = jnp.bfloat16
GRAD = jnp.bfloat16

D = 1024
CH = 128
GM_HEADS = 8
SSD_HEADS = 16
SSD_P = 64
CONV_CH = 1536
N_MAIN = 4608
DFF = 4096
DPLE = 256
EPS = 1e-6
NEG = -1e30

LR, B1, B2, ADAM_EPS, WD, STEP = 0.001, 0.9, 0.999, 1e-08, 0.01, 10

VMEM_LIMIT = 56 * 1024 * 1024
_SEQS_PER_STEP = 4
MESH = pl.DeviceIdType.MESH

INV_SQRT2 = 0.7071067811865476
INV_SQRT_2PI = 0.3989422804014327


def _cp(n_axes=1):
    return pltpu.CompilerParams(dimension_semantics=("arbitrary",) * n_axes, vmem_limit_bytes=VMEM_LIMIT)


def _dot(a, b):
    return jnp.dot(a, b, preferred_element_type=F32)


def _dot_nt(a, b):
    return lax.dot_general(a, b, (((1,), (1,)), ((), ())), preferred_element_type=F32)


def _dot_tn(a, b):
    return lax.dot_general(a, b, (((0,), (0,)), ((), ())), preferred_element_type=F32)


def _dot_hi(a, b):
    return jnp.dot(a, b, preferred_element_type=F32, precision=lax.Precision.HIGHEST)


def _dot_01(a, sel):
    hi = a.astype(BF16)
    lo = (a - hi.astype(F32)).astype(BF16)
    n = a.shape[0]
    r = _dot(jnp.concatenate([hi, lo], axis=0), sel)
    return r[0:n] + r[n:2 * n]


def _rows(tm, n, j=0):
    return pl.BlockSpec((tm, n), lambda i: (i, j))


def _const(shape):
    nd = len(shape)
    return pl.BlockSpec(shape, lambda *_: (0,) * nd)


def _sds(shape, dtype):
    return jax.ShapeDtypeStruct(shape, dtype)


def _rms(x):
    r = lax.rsqrt(jnp.mean(x * x, axis=-1, keepdims=True) + EPS)
    return x * r, r


def _rms_bwd(dy, xhat, r, g):
    dyg = dy * g
    return r * (dyg - xhat * jnp.mean(dyg * xhat, axis=-1, keepdims=True))


def _sigmoid(x):
    return 1.0 / (1.0 + jnp.exp(-x))


def _gelu(x):
    cdf = 0.5 * (1.0 + lax.erf(x * INV_SQRT2))
    pdf = jnp.exp(-0.5 * x * x) * INV_SQRT_2PI
    return x * cdf, cdf + x * pdf


def _softplus(x):
    e = jnp.exp(-jnp.abs(x))
    u = 1.0 + e
    log1p = jnp.where(u == 1.0, e, jnp.log(u) * e / (u - 1.0))
    return jnp.maximum(x, 0.0) + log1p


def _after(n_in, fn):
    def body(*refs):
        return fn(*refs[:n_in], *refs[n_in + 1:])

    return body


def _inproj_gmlp(x, g, wm, wdt, gv, ws, bst, gout, tm, after):
    T = x.shape[0]

    def body(x_ref, g_ref, wm_ref, wdt_ref, gv_ref, ws_ref, bst_ref, gout_ref,
             z_ref, xbc_ref, dt_ref, xn_ref, ya_ref, uv_ref):
        xh, _ = _rms(x_ref[...])
        xn = (xh * g_ref[...]).astype(MXU)
        xn_ref[...] = xn
        for n in range(4):
            uv_ref[:, n * 512:(n + 1) * 512] = _dot(xn, wm_ref[:, n * 512:(n + 1) * 512])
        for n in range(2):
            z_ref[:, n * 512:(n + 1) * 512] = _dot(xn, wm_ref[:, 2048 + n * 512:2048 + (n + 1) * 512])
        for n in range(3):
            xbc_ref[:, n * 512:(n + 1) * 512] = _dot(xn, wm_ref[:, 3072 + n * 512:3072 + (n + 1) * 512])
        dt_ref[...] = _dot(xn, wdt_ref[...])
        for k in range(tm // CH):
            rows = slice(k * CH, (k + 1) * CH)
            f = _gmlp_fwd_vals(uv_ref[rows, 0:1024], uv_ref[rows, 1024:2048], gv_ref[...], ws_ref, bst_ref[...],
                               gout_ref[...])
            ya_ref[rows, :] = f["out"].astype(MXU)

    return pl.pallas_call(
        _after(8, body), grid=(T // tm,), name="inproj_gmlp",
        in_specs=[_rows(tm, D), _const((1, D)), _const((D, N_MAIN)), _const((D, 128)), _const((1, 1024)),
                  _const((GM_HEADS, CH, CH)), _const((CH, 128)), _const((1, 1024)), _ANY],
        out_specs=[_rows(tm, 1024), _rows(tm, CONV_CH), _rows(tm, 128), _rows(tm, D), _rows(tm, 1024, 0),
                   _rows(tm, 2048)],
        out_shape=[_sds((T, 1024), F32), _sds((T, CONV_CH), F32), _sds((T, 128), F32), _sds((T, D), MXU),
                   _sds((T, 2048), MXU), _sds((T, 2048), F32)],
        compiler_params=_cp(),
    )(x, g, wm, wdt, gv, ws, bst, gout, after)


def _gmlp_fwd_vals(u, v, gv, ws_ref, bst, gout):
    ug, dug = _gelu(u)
    vg, dvg = _gelu(v)
    row = lax.broadcasted_iota(jnp.int32, (CH, CH), 0)
    col = lax.broadcasted_iota(jnp.int32, (CH, CH), 1)
    tril = row >= col
    ys, heads = [], []
    for h in range(GM_HEADS):
        sl = slice(h * 128, (h + 1) * 128)
        vhat, rv = _rms(vg[:, sl])
        vn = (vhat * gv[:, sl]).astype(MXU)
        wt = jnp.where(tril, ws_ref[h], 0.0)
        mixed = _dot(wt.astype(MXU), vn) + bst[:, h:h + 1]
        ys.append(ug[:, sl] * mixed)
        heads.append((vhat, rv, vn, wt, mixed))
    y = jnp.concatenate(ys, axis=1)
    yhat, ry = _rms(y)
    return dict(ug=ug, dug=dug, dvg=dvg, heads=heads, yhat=yhat, ry=ry, tril=tril, out=yhat * gout)


def _shifts_down(cur, halo):
    row8 = lax.broadcasted_iota(jnp.int32, (8, cur.shape[1]), 0)
    out = [cur]
    for j in (1, 2, 3):
        sh = pltpu.roll(cur, j, 0)
        top = jnp.where(row8 < j, pltpu.roll(halo, j, 0), sh[0:8])
        out.append(jnp.concatenate([top, sh[8:]], axis=0))
    return out


def _shifts_up(cur, halo):
    row8 = lax.broadcasted_iota(jnp.int32, (8, cur.shape[1]), 0)
    out = []
    for j in (1, 2, 3):
        sh = pltpu.roll(cur, CH - j, 0)
        bot = jnp.where(row8 + j >= 8, pltpu.roll(halo, 8 - j, 0), sh[CH - 8:CH])
        out.append(jnp.concatenate([sh[0:CH - 8], bot], axis=0))
    return out


def _conv(xbc, halo, convw, convb):
    sh = _shifts_down(xbc, halo)
    return convb + convw[3:4] * sh[0] + convw[2:3] * sh[1] + convw[1:2] * sh[2] + convw[0:1] * sh[3]


def _ssd_fwd_vals(z, conv, dtraw, dtb, alog, dskip, ng, ex, ltri, s_prev):
    sig_c = _sigmoid(conv)
    xa = conv * sig_c
    xs = xa[:, :1024]
    bm = [xa[:, 1024:1152], xa[:, 1152:1280]]
    cm = [xa[:, 1280:1408], xa[:, 1408:1536]]
    dtpre = dtraw + dtb
    dt = _softplus(dtpre)
    a_neg = -jnp.exp(alog)
    cs = _dot_hi(ltri, dt * a_neg)
    cst = cs.T
    last = cs[CH - 1:CH]
    ecs = jnp.exp(cs)
    dec = jnp.exp(last - cs)
    spread = _dot_01(jnp.concatenate([dt, ecs, dec], axis=0), ex)
    dte, ecse, dece = spread[0:CH], spread[CH:2 * CH], spread[2 * CH:3 * CH]
    cde = ecse[CH - 1:CH]
    de = dskip
    xdt = xs * dte
    row = lax.broadcasted_iota(jnp.int32, (CH, CH), 0)
    col = lax.broadcasted_iota(jnp.int32, (CH, CH), 1)
    tril = row >= col
    lo = col < SSD_P
    bmb = [b.astype(MXU) for b in bm]
    cmb = [c.astype(MXU) for c in cm]
    mg = [_dot_nt(cmb[g], bmb[g]) for g in range(2)]
    yd, lms, whs = [], [], []
    for q in range(8):
        g = q // 4
        xq = xdt[:, q * 128:(q + 1) * 128]
        acc = None
        for hh in range(2):
            h = 2 * q + hh
            seg = cs[:, h:h + 1] - cst[h:h + 1, :]
            lm = jnp.exp(jnp.where(tril, seg, NEG))
            wh = (mg[g] * lm).astype(MXU)
            xm = jnp.where(lo if hh == 0 else ~lo, xq, 0.0).astype(MXU)
            part = _dot(wh, xm)
            acc = part if acc is None else acc + part
            lms.append(lm)
            whs.append(wh)
        yd.append(acc)
    yd = jnp.concatenate(yd, axis=1)
    sb = s_prev.astype(MXU)
    yo = jnp.concatenate([_dot(cmb[g], sb[:, g * 512:(g + 1) * 512]) for g in range(2)], axis=1) * ecse
    xdec = (xdt * dece).astype(MXU)
    states = jnp.concatenate([_dot_tn(bmb[g], xdec[:, g * 512:(g + 1) * 512]) for g in range(2)], axis=1)
    s_next = s_prev * cde + states
    ypre = yd + yo + de * xs
    sig_z = _sigmoid(z)
    yg = ypre * z * sig_z
    outs, yhat, rr = [], [], []
    for g in range(2):
        sl = slice(g * 512, (g + 1) * 512)
        yh, r = _rms(yg[:, sl])
        yhat.append(yh)
        rr.append(r)
        outs.append(yh * ng[:, sl])
    return dict(sig_c=sig_c, xa=xa, xs=xs, bmb=bmb, cmb=cmb, dtpre=dtpre, dt=dt, a_neg=a_neg,
                cs=cs, last=last, ecs=ecs, dec=dec, dte=dte, ecse=ecse, dece=dece, cde=cde, de=de, xdt=xdt,
                mg=mg, lms=lms, whs=whs, lo=lo, yo=yo, sb=sb, xdec=xdec, s_next=s_next, ypre=ypre, sig_z=sig_z,
                yhat=yhat, rr=rr, out=jnp.concatenate(outs, axis=1))


def _ssd_fwd(pz, pxbc, dtraw, cat, convw, convb, dtb, alog, dskip, ng, ex, ltri, nb):
    T = pz.shape[0]
    S = T // nb
    nch = S // CH
    ns = _SEQS_PER_STEP if nb % _SEQS_PER_STEP == 0 else 1

    def body(z_ref, xbc_ref, halo_ref, dt_ref, cw_ref, cb_ref, dtb_ref, al_ref, ds_ref, ng_ref, ex_ref, lt_ref,
             cat_in_ref, yb_ref, sall_ref, conv_ref, s_ref):
        del cat_in_ref
        c = pl.program_id(1)

        @pl.when(c == 0)
        def _():
            s_ref[...] = jnp.zeros_like(s_ref)

        for i in range(ns):
            halo = jnp.where(c == 0, 0.0, halo_ref[i])
            s_prev = s_ref[i]
            sall_ref[i, 0] = s_prev
            conv = _conv(xbc_ref[i], halo, cw_ref[...], cb_ref[...])
            conv_ref[i] = conv
            f = _ssd_fwd_vals(z_ref[i], conv, dt_ref[i], dtb_ref[...], al_ref[...], ds_ref[...], ng_ref[...],
                              ex_ref[...], lt_ref[...], s_prev)
            s_ref[i] = f["s_next"]
            yb_ref[i] = f["out"].astype(MXU)

    def seq(width, col=0):
        return pl.BlockSpec((ns, CH, width), lambda b, c: (b, c, col))

    cat, sall, conv = pl.pallas_call(
        body, grid=(nb // ns, nch), name="ssd_fwd",
        in_specs=[seq(1024), seq(CONV_CH),
                  pl.BlockSpec((ns, 8, CONV_CH), lambda b, c: (b, jnp.maximum(c * (CH // 8) - 1, 0), 0)),
                  seq(128),
                  _const((8, CONV_CH)), _const((1, CONV_CH)), _const((1, 128)), _const((1, 128)), _const((1, 1024)),
                  _const((1, 1024)), _const((128, 1024)), _const((CH, CH)), _ANY],
        out_specs=[seq(1024, 1), pl.BlockSpec((ns, 1, 128, 1024), lambda b, c: (b, c, 0, 0)), seq(CONV_CH)],
        out_shape=[_sds((nb, S, 2048), MXU), _sds((nb, nch, 128, 1024), F32), _sds((nb, S, CONV_CH), F32)],
        scratch_shapes=[pltpu.VMEM((ns, 128, 1024), F32)],
        input_output_aliases={12: 0},
        compiler_params=_cp(2),
    )(pz.reshape(nb, S, 1024), pxbc.reshape(nb, S, CONV_CH), pxbc.reshape(nb, S, CONV_CH), dtraw.reshape(nb, S, 128),
      convw, convb, dtb, alog, dskip, ng, ex, ltri, cat.reshape(nb, S, 2048))
    return cat.reshape(T, 2048), sall, conv.reshape(T, CONV_CH)


def _outproj_ff1(cat, wo, x, g, w1, tm):
    T = x.shape[0]

    def body(cat_ref, wo_ref, x_ref, g_ref, w1_ref, h1_ref, hn_ref, hid_ref):
        h1 = x_ref[...] + _dot(cat_ref[...], wo_ref[...])
        h1_ref[...] = h1
        hn = (_rms(h1)[0] * g_ref[...]).astype(MXU)
        hn_ref[...] = hn
        for n in range(4):
            hid_ref[:, n * 1024:(n + 1) * 1024] = jnp.maximum(_dot(hn, w1_ref[n]), 0.0).astype(MXU)

    return pl.pallas_call(
        body, grid=(T // tm,), name="outproj_ff1",
        in_specs=[_rows(tm, 2048), _const((2048, D)), _rows(tm, D), _const((1, D)), _const((4, D, 1024))],
        out_specs=[_rows(tm, D), _rows(tm, D), _rows(tm, DFF)],
        out_shape=[_sds((T, D), F32), _sds((T, D), MXU), _sds((T, DFF), MXU)],
        compiler_params=_cp(),
    )(cat, wo, x, g, w1)


def _sq(hid):
    h = hid.astype(F32)
    return (h * h).astype(MXU)


def _ff2_tail(hid, w2, h1, g_ple, p, tgt, wg, wp, gf, tm):
    T = h1.shape[0]

    def body(hid_ref, w2_ref, h1_ref, g_ref, p_ref, t_ref, wg_ref, wp_ref, gf_ref,
             hp_ref, dgl_ref, dpe_ref, dh2_ref, dh2b_ref, loss_ref, dgf_ref, dg_ref):
        @pl.when(pl.program_id(0) == 0)
        def _():
            loss_ref[...] = jnp.zeros_like(loss_ref)
            dgf_ref[...] = jnp.zeros_like(dgf_ref)
            dg_ref[...] = jnp.zeros_like(dg_ref)

        h2 = h1_ref[...] + _dot(_sq(hid_ref[...]), w2_ref[...])
        h2h, r2 = _rms(h2)
        g_ple = g_ref[...]
        hp = (h2h * g_ple).astype(MXU)
        hp_ref[...] = hp
        gate = _sigmoid(_dot(hp, wg_ref[...]))
        pb = p_ref[...].astype(MXU)
        pe = jnp.concatenate([_dot(pb, wp_ref[k]) for k in range(4)], axis=1)
        h3 = h2 + gate * pe
        hh, r = _rms(h3)
        gf = gf_ref[...]
        diff = hh * gf - t_ref[...]
        loss_ref[...] += 0.5 * jnp.sum(jnp.mean(diff * diff, axis=-1, keepdims=True))
        dout = diff * (1.0 / D)
        dgf_ref[...] += jnp.sum(dout * hh, axis=0, keepdims=True)
        dh3 = _rms_bwd(dout, hh, r, gf)
        dgl = (dh3 * pe * gate * (1.0 - gate)).astype(MXU)
        dgl_ref[...] = dgl
        dpe_ref[...] = (dh3 * gate).astype(MXU)
        dhp = _dot_nt(dgl, wg_ref[...])
        dg_ref[...] += jnp.sum(dhp * h2h, axis=0, keepdims=True)
        dh2 = dh3 + _rms_bwd(dhp, h2h, r2, g_ple)
        dh2_ref[...] = dh2
        dh2b_ref[...] = dh2.astype(MXU)

    return pl.pallas_call(
        body, grid=(T // tm,), name="ff2_tail",
        in_specs=[_rows(tm, DFF), _const((DFF, D)), _rows(tm, D), _const((1, D)), _rows(tm, DPLE), _rows(tm, D),
                  _const((D, D)), _const((4, DPLE, 256)), _const((1, D))],
        out_specs=[_rows(tm, D), _rows(tm, D), _rows(tm, D), _rows(tm, D), _rows(tm, D), _const((8, 128)),
                   _const((1, D)), _const((1, D))],
        out_shape=[_sds((T, D), MXU), _sds((T, D), MXU), _sds((T, D), MXU), _sds((T, D), F32), _sds((T, D), MXU),
                   _sds((8, 128), F32), _sds((1, D), F32), _sds((1, D), F32)],
        compiler_params=_cp(),
    )(hid, w2, h1, g_ple, p, tgt, wg, wp, gf)


def _ff2_bwd(dh2b, w2, hid, tm):
    T = hid.shape[0]

    def body(dh2b_ref, w2_ref, hid_ref, dpre_ref):
        d = dh2b_ref[...]
        for n in range(DFF // 1024):
            sl = slice(n * 1024, (n + 1) * 1024)
            da = _dot_nt(d, w2_ref[sl, :])
            dpre_ref[:, sl] = (2.0 * da * hid_ref[:, sl].astype(F32)).astype(MXU)

    return pl.pallas_call(
        body, grid=(T // tm,), name="ff2_bwd",
        in_specs=[_rows(tm, D), _const((DFF, D)), _rows(tm, DFF)],
        out_specs=_rows(tm, DFF),
        out_shape=_sds((T, DFF), MXU),
        compiler_params=_cp(),
    )(dh2b, w2, hid)


def _ff1_bwd(dpre, w1, dh2, h1, g, wo, tm):
    T = h1.shape[0]

    def body(dpre_ref, w1_ref, dh2_ref, h1_ref, g_ref, wo_ref, dh1_ref, dh1b_ref, dg_ref, dcat_ref):
        @pl.when(pl.program_id(0) == 0)
        def _():
            dg_ref[...] = jnp.zeros_like(dg_ref)

        dhn = _dot_nt(dpre_ref[:, 0:1024], w1_ref[0])
        for k in range(1, 4):
            dhn = dhn + _dot_nt(dpre_ref[:, k * 1024:(k + 1) * 1024], w1_ref[k])
        hh, r = _rms(h1_ref[...])
        dg_ref[...] += jnp.sum(dhn * hh, axis=0, keepdims=True)
        dh1 = dh2_ref[...] + _rms_bwd(dhn, hh, r, g_ref[...])
        dh1_ref[...] = dh1
        dh1b = dh1.astype(MXU)
        dh1b_ref[...] = dh1b
        dcat_ref[:, 0:1024] = _dot_nt(dh1b, wo_ref[0:1024, :])
        dcat_ref[:, 1024:2048] = _dot_nt(dh1b, wo_ref[1024:2048, :])

    return pl.pallas_call(
        body, grid=(T // tm,), name="ff1_bwd",
        in_specs=[_rows(tm, DFF), _const((4, D, 1024)), _rows(tm, D), _rows(tm, D), _const((1, D)),
                  _const((2048, D))],
        out_specs=[_rows(tm, D), _rows(tm, D), _const((1, D)), _rows(tm, 2048)],
        out_shape=[_sds((T, D), F32), _sds((T, D), MXU), _sds((1, D), F32), _sds((T, 2048), F32)],
        compiler_params=_cp(),
    )(dpre, w1, dh2, h1, g, wo)


def _gmlp_bwd(uv, dcat, gv, ws, bst, gout, wm):
    T = uv.shape[0]
    nck = 4 if T % (4 * CH) == 0 else 1
    tb = nck * CH

    def body(uv_ref, dya_ref, gv_ref, ws_ref, bst_ref, gout_ref, wuv_ref, duv_ref, dgv_ref, dws_ref, dbst_ref,
             dgo_ref, dxn_ref):
        @pl.when(pl.program_id(0) == 0)
        def _():
            dgv_ref[...] = jnp.zeros_like(dgv_ref)
            dws_ref[...] = jnp.zeros_like(dws_ref)
            dbst_ref[...] = jnp.zeros_like(dbst_ref)
            dgo_ref[...] = jnp.zeros_like(dgo_ref)

        for k in range(nck):
            chunk(slice(k * CH, (k + 1) * CH), uv_ref, dya_ref, gv_ref, ws_ref, bst_ref, gout_ref, duv_ref,
                  dgv_ref, dws_ref, dbst_ref, dgo_ref)
        dxn_ref[...] = _dot_nt(duv_ref[...], wuv_ref[...])

    def chunk(rows, uv_ref, dya_ref, gv_ref, ws_ref, bst_ref, gout_ref, duv_ref, dgv_ref, dws_ref, dbst_ref,
              dgo_ref):
        gv = gv_ref[...]
        f = _gmlp_fwd_vals(uv_ref[rows, 0:1024], uv_ref[rows, 1024:2048], gv, ws_ref, bst_ref[...], gout_ref[...])
        dya = dya_ref[rows, :]
        dgo_ref[...] += jnp.sum(dya * f["yhat"], axis=0, keepdims=True)
        dy = _rms_bwd(dya, f["yhat"], f["ry"], gout_ref[...])
        lane = lax.broadcasted_iota(jnp.int32, (CH, 128), 1)
        dbs = jnp.zeros((CH, 128), F32)
        dug, dvg, dgvs = [], [], []
        for h in range(GM_HEADS):
            sl = slice(h * 128, (h + 1) * 128)
            vhat, rv, vn, wt, mixed = f["heads"][h]
            dyh = dy[:, sl]
            dug.append(dyh * mixed)
            dmixed = dyh * f["ug"][:, sl]
            dmb = dmixed.astype(MXU)
            dws_ref[h] += jnp.where(f["tril"], _dot_nt(dmb, vn), 0.0)
            dbs = dbs + jnp.where(lane == h, jnp.sum(dmixed, axis=1, keepdims=True), 0.0)
            dvn = _dot_tn(wt.astype(MXU), dmb)
            dgvs.append(jnp.sum(dvn * vhat, axis=0, keepdims=True))
            dvg.append(_rms_bwd(dvn, vhat, rv, gv[:, sl]))
        dbst_ref[...] += dbs
        dgv_ref[...] += jnp.concatenate(dgvs, axis=1)
        duv_ref[rows, 0:1024] = (jnp.concatenate(dug, axis=1) * f["dug"]).astype(MXU)
        duv_ref[rows, 1024:2048] = (jnp.concatenate(dvg, axis=1) * f["dvg"]).astype(MXU)

    return pl.pallas_call(
        body, grid=(T // tb,), name="gmlp_bwd",
        in_specs=[_rows(tb, 2048), _rows(tb, 1024, 0), _const((1, 1024)),
                  _const((GM_HEADS, CH, CH)), _const((CH, 128)), _const((1, 1024)), _const((D, 2048))],
        out_specs=[_rows(tb, 2048), _const((1, 1024)), _const((GM_HEADS, CH, CH)), _const((CH, 128)),
                   _const((1, 1024)), _rows(tb, D)],
        out_shape=[_sds((T, 2048), MXU), _sds((1, 1024), F32), _sds((GM_HEADS, CH, CH), F32), _sds((CH, 128), F32),
                   _sds((1, 1024), F32), _sds((T, D), F32)],
        compiler_params=_cp(),
    )(uv, dcat, gv, ws, bst, gout, wm)


def _ssd_bwd(pz, pxbc, conv, dtraw, sall, dcat, convw, dtb, alog, dskip, ng, ex, ltri, ext, nb, after):
    T = pz.shape[0]
    S = T // nb
    nch = S // CH
    ns = _SEQS_PER_STEP if nb % _SEQS_PER_STEP == 0 else 1

    def seq(width, col=0):
        return pl.BlockSpec((ns, CH, width), lambda b, c: (b, nch - 1 - c, col))

    in_specs = [
        seq(1024), seq(CONV_CH), seq(CONV_CH), seq(128),
        _const((8, CONV_CH)), _const((1, 128)), _const((1, 128)), _const((1, 1024)),
        _const((1, 1024)), _const((128, 1024)), _const((CH, CH)),
        _const((1024, 128)),
        pl.BlockSpec((ns, 1, 128, 1024), lambda b, c: (b, nch - 1 - c, 0, 0)),
        seq(1024, 1),
        _ANY,
    ]

    def body(z_ref, xbc_ref, conv_ref, dt_ref, cw_ref, dtb_ref, al_ref, ds_ref, ng_ref, ex_ref, lt_ref,
             ext_ref, sall_ref, dyb_ref,
             dssd_ref, ddt_ref, dcw_ref, dcb_ref, ddtb_ref, dal_ref, dds_ref, dng_ref,
             dst_ref, dnext_ref, ddse_ref):
        b = pl.program_id(0)
        c = pl.program_id(1)

        @pl.when((b == 0) & (c == 0))
        def _():
            for r in (dcw_ref, dcb_ref, ddtb_ref, dal_ref, dds_ref, dng_ref, ddse_ref):
                r[...] = jnp.zeros_like(r)

        @pl.when(c == 0)
        def _():
            dst_ref[...] = jnp.zeros_like(dst_ref)
            dnext_ref[...] = jnp.zeros_like(dnext_ref)

        ex = ex_ref[...]
        ext = ext_ref[...]
        cw = cw_ref[...]
        ng = ng_ref[...]
        for i in range(ns):
            one_chunk(i, ex, ext, cw, ng, z_ref, xbc_ref, conv_ref, dt_ref, dtb_ref, al_ref, ds_ref, lt_ref, sall_ref,
                      dyb_ref, dssd_ref, ddt_ref, dcw_ref, dcb_ref, ddtb_ref, dal_ref, dng_ref, dst_ref, dnext_ref,
                      ddse_ref)

        @pl.when((b == nb // ns - 1) & (c == nch - 1))
        def _():
            dds_ref[...] = _dot_01(jnp.broadcast_to(ddse_ref[...], (8, 1024)), ext)[0:1]

    def one_chunk(i, ex, ext, cw, ng, z_ref, xbc_ref, conv_ref, dt_ref, dtb_ref, al_ref, ds_ref, lt_ref, sall_ref,
                  dyb_ref, dssd_ref, ddt_ref, dcw_ref, dcb_ref, ddtb_ref, dal_ref, dng_ref, dst_ref, dnext_ref,
                  ddse_ref):
        z = z_ref[i]
        s_prev = sall_ref[i, 0]
        conv = conv_ref[i]
        f = _ssd_fwd_vals(z, conv, dt_ref[i], dtb_ref[...], al_ref[...], ds_ref[...], ng, ex, lt_ref[...], s_prev)
        xs, xdt, cs, dec, dt = f["xs"], f["xdt"], f["cs"], f["dec"], f["dt"]
        dyb = dyb_ref[i]
        dyg, dngs = [], []
        for g in range(2):
            sl = slice(g * 512, (g + 1) * 512)
            dngs.append(jnp.sum(dyb[:, sl] * f["yhat"][g], axis=0, keepdims=True))
            dyg.append(_rms_bwd(dyb[:, sl], f["yhat"][g], f["rr"][g], ng[:, sl]))
        dng_ref[...] += jnp.concatenate(dngs, axis=1)
        dyg = jnp.concatenate(dyg, axis=1)
        sig_z = f["sig_z"]
        silu_z = z * sig_z
        dy = dyg * silu_z
        dz = dyg * f["ypre"] * (sig_z + silu_z * (1.0 - sig_z))
        ddse_ref[...] += jnp.sum(dy * xs, axis=0, keepdims=True)
        dxs = dy * f["de"]
        dye = dy * f["ecse"]
        dyeb = dye.astype(MXU)
        dst = dst_ref[i]
        dstb = dst.astype(MXU)
        bmb, cmb, sb, xdec = f["bmb"], f["cmb"], f["sb"], f["xdec"]
        u = jnp.concatenate([_dot(bmb[g], dstb[:, g * 512:(g + 1) * 512]) for g in range(2)], axis=1)
        dxdt = [u[:, q * 128:(q + 1) * 128] * f["dece"][:, q * 128:(q + 1) * 128] for q in range(8)]
        per_head = _dot_01(jnp.concatenate(
            [dy * f["yo"], u * xdt, jnp.broadcast_to(jnp.sum(dst * s_prev, axis=0, keepdims=True), (8, 1024))],
            axis=0), ext)
        dcs = per_head[0:CH]
        t = per_head[CH:2 * CH] * dec
        dcd = per_head[2 * CH:2 * CH + 1]
        row = lax.broadcasted_iota(jnp.int32, (CH, 128), 0)
        lane = lax.broadcasted_iota(jnp.int32, (CH, 128), 1)
        cd = jnp.exp(f["last"])
        dcs = dcs - t + jnp.where(row == CH - 1, jnp.sum(t, axis=0, keepdims=True) + dcd * cd, 0.0)
        dcst = jnp.zeros((128, CH), F32)
        lo = f["lo"]
        dbm, dcm, ds_prev = [], [], []
        for g in range(2):
            sl = slice(g * 512, (g + 1) * 512)
            dmg = jnp.zeros((CH, CH), F32)
            for q in range(4 * g, 4 * g + 4):
                dyq = dy[:, q * 128:(q + 1) * 128]
                xq = xdt[:, q * 128:(q + 1) * 128].astype(MXU)
                for hh in range(2):
                    h = 2 * q + hh
                    m = lo if hh == 0 else ~lo
                    dym = jnp.where(m, dyq, 0.0).astype(MXU)
                    gh = _dot_nt(dym, xq)
                    gl = gh * f["lms"][h]
                    dmg = dmg + gl
                    qh = gl * f["mg"][g]
                    dcs = dcs + jnp.where(lane == h, jnp.sum(qh, axis=1, keepdims=True), 0.0)
                    dcst = dcst - jnp.where(row == h, jnp.sum(qh, axis=0, keepdims=True), 0.0)
                    dxdt[q] = dxdt[q] + _dot_tn(f["whs"][h], dym)
            dmgb = dmg.astype(MXU)
            dcm.append(_dot(dmgb, bmb[g]) + _dot_nt(dyeb[:, sl], sb[:, sl]))
            dbm.append(_dot_tn(dmgb, cmb[g]) + _dot_nt(xdec[:, sl], dstb[:, sl]))
            ds_prev.append(_dot_tn(cmb[g], dyeb[:, sl]))
        dst_ref[i] = jnp.concatenate(ds_prev, axis=1) + dst * f["cde"]
        dcs = dcs + dcst.T
        da = _dot_hi(lt_ref[...].T, dcs)
        dxdt = jnp.concatenate(dxdt, axis=1)
        a_neg = f["a_neg"]
        ddt = da * a_neg + _dot_01(dxdt * xs, ext)
        dal_ref[...] += jnp.sum(da * dt, axis=0, keepdims=True) * a_neg
        dxs = dxs + dxdt * f["dte"]
        ddtraw = jnp.where(lane < SSD_HEADS, ddt * _sigmoid(f["dtpre"]), 0.0)
        ddtb_ref[...] += jnp.sum(ddtraw, axis=0, keepdims=True)
        ddt_ref[i] = ddtraw.astype(MXU)
        dxa = jnp.concatenate([dxs, dbm[0], dbm[1], dcm[0], dcm[1]], axis=1)
        sig_c = f["sig_c"]
        dconv = dxa * (sig_c + f["xa"] * (1.0 - sig_c))
        dcb_ref[...] += jnp.sum(dconv, axis=0, keepdims=True)
        xbc = xbc_ref[i]
        dcw_ref[3:4, :] += jnp.sum(dconv * xbc, axis=0, keepdims=True)
        dxbc = cw[3:4] * dconv
        for j, up in zip((1, 2, 3), _shifts_up(dconv, dnext_ref[i])):
            dcw_ref[3 - j:4 - j, :] += jnp.sum(up * xbc, axis=0, keepdims=True)
            dxbc = dxbc + cw[3 - j:4 - j] * up
        dnext_ref[i] = dconv[0:8]
        dssd_ref[i, :, 0:1024] = dz.astype(MXU)
        dssd_ref[i, :, 1024:2560] = dxbc.astype(MXU)

    dssd, ddt, *small = pl.pallas_call(
        _after(14, body), grid=(nb // ns, nch), name="ssd_bwd",
        in_specs=in_specs,
        out_specs=[seq(2560), seq(128),
                   _const((8, CONV_CH)), _const((1, CONV_CH)), _const((1, 128)), _const((1, 128)), _const((1, 128)),
                   _const((1, 1024))],
        out_shape=[_sds((nb, S, 2560), MXU), _sds((nb, S, 128), MXU), _sds((8, CONV_CH), F32),
                   _sds((1, CONV_CH), F32), _sds((1, 128), F32), _sds((1, 128), F32), _sds((1, 128), F32),
                   _sds((1, 1024), F32)],
        scratch_shapes=[pltpu.VMEM((ns, 128, 1024), F32), pltpu.VMEM((ns, 8, CONV_CH), F32),
                        pltpu.VMEM((1, 1024), F32)],
        compiler_params=_cp(2),
    )(pz.reshape(nb, S, 1024), pxbc.reshape(nb, S, CONV_CH), conv.reshape(nb, S, CONV_CH), dtraw.reshape(nb, S, 128),
      convw, dtb, alog, dskip, ng, ex, ltri, ext, sall, dcat.reshape(nb, S, 2048), after)
    return (dssd.reshape(T, 2560), ddt.reshape(T, 128), *small)


def _inproj_bwd(dxn_uv, dssd, ddt, wm, wdt, dh1, x, g, tm, after):
    T = x.shape[0]

    def body(dxnuv_ref, dssd_ref, ddt_ref, wm_ref, wdt_ref, dh1_ref, x_ref, g_ref, dx_ref, dg_ref):
        @pl.when(pl.program_id(0) == 0)
        def _():
            dg_ref[...] = jnp.zeros_like(dg_ref)

        dxn = (dxnuv_ref[...] + _dot_nt(dssd_ref[...], wm_ref[:, 2048:N_MAIN])
               + _dot_nt(ddt_ref[...], wdt_ref[...]))
        xh, r = _rms(x_ref[...])
        dg_ref[...] += jnp.sum(dxn * xh, axis=0, keepdims=True)
        dx_ref[...] = dh1_ref[...] + _rms_bwd(dxn, xh, r, g_ref[...])

    return pl.pallas_call(
        _after(8, body), grid=(T // tm,), name="inproj_bwd",
        in_specs=[_rows(tm, D), _rows(tm, 2560), _rows(tm, 128), _const((D, N_MAIN)), _const((D, 128)),
                  _rows(tm, D), _rows(tm, D), _const((1, D)), _ANY],
        out_specs=[_rows(tm, D), _const((1, D))],
        out_shape=[_sds((T, D), F32), _sds((1, D), F32)],
        compiler_params=_cp(),
    )(dxn_uv, dssd, ddt, wm, wdt, dh1, x, g, after)


def _matmul_tn(a, b, name, a_fn=None):
    T, M = a.shape
    N = b.shape[1]
    tm = min(M, 1024)
    tn = 1280 if N == 2560 else min(N, 1024)
    tk = min(T, 2048)

    def body(a_ref, b_ref, o_ref, acc_ref):
        k = pl.program_id(2)

        @pl.when(k == 0)
        def _():
            acc_ref[...] = jnp.zeros_like(acc_ref)

        av = a_ref[...]
        if a_fn is not None:
            av = a_fn(av)
        acc_ref[...] += _dot_tn(av, b_ref[...])

        @pl.when(k == T // tk - 1)
        def _():
            o_ref[...] = acc_ref[...].astype(o_ref.dtype)

    return pl.pallas_call(
        body, grid=(M // tm, N // tn, T // tk), name=name,
        in_specs=[pl.BlockSpec((tk, tm), lambda i, j, k: (k, i)), pl.BlockSpec((tk, tn), lambda i, j, k: (k, j))],
        out_specs=pl.BlockSpec((tm, tn), lambda i, j, k: (i, j)),
        out_shape=_sds((M, N), GRAD),
        scratch_shapes=[pltpu.VMEM((tm, tn), F32)],
        compiler_params=_cp(3),
    )(a, b)


def _adamw_vals(w, g, m, v):
    m = B1 * m + (1.0 - B1) * g
    v = B2 * v + (1.0 - B2) * (g * g)
    m_hat = m / (1.0 - B1 ** STEP)
    v_hat = v / (1.0 - B2 ** STEP)
    return -LR * (m_hat / (jnp.sqrt(v_hat) + ADAM_EPS) + WD * w), m, v


_PARTS = 4


def _adamw_halves(items, name):
    n = len(items)

    def body(*refs):
        mine = (pl.program_id(0) // _PARTS) == lax.axis_index("c")
        for k in range(n):
            w_ref, own_ref, oth_ref, m_ref, v_ref = refs[5 * k:5 * k + 5]
            g_ref, d_ref, mo_ref, vo_ref = refs[5 * n + 4 * k:5 * n + 4 * k + 4]
            g = jnp.where(mine, own_ref[...], oth_ref[...])
            g_ref[...] = g
            d_ref[...], mo_ref[...], vo_ref[...] = _adamw_vals(w_ref[...], g, m_ref[...], v_ref[...])

    in_specs, out_specs, out_shape = [], [], []
    for w, *_ in items:
        R, C = w.shape
        full = _rows(R // (2 * _PARTS), C)
        part = pl.BlockSpec((R // (2 * _PARTS), C), lambda i: (i % _PARTS, 0))
        in_specs += [full, part, part, full, full]
        out_specs += [full] * 4
        out_shape += [_sds((R, C), F32)] * 4
    res = pl.pallas_call(
        body, grid=(2 * _PARTS,), name=name, in_specs=in_specs, out_specs=out_specs, out_shape=out_shape,
        compiler_params=_cp(),
    )(*[a for item in items for a in item])
    return [tuple(res[4 * k:4 * k + 4]) for k in range(n)]


_TJ = 128


def _adamw_transposed(w, own, other, m, v, name):
    C, _, R = w.shape

    def body(w_ref, own_ref, oth_ref, m_ref, v_ref, g_ref, d_ref, mo_ref, vo_ref):
        first = lax.axis_index("c") == 0
        g = jnp.concatenate([jnp.where(first, own_ref[...], oth_ref[...]),
                             jnp.where(first, oth_ref[...], own_ref[...])], axis=0).T
        d, mo, vo = _adamw_vals(w_ref[:, 0, :], g, m_ref[:, 0, :], v_ref[:, 0, :])
        for ref, val in ((g_ref, g), (d_ref, d), (mo_ref, mo), (vo_ref, vo)):
            ref[:, 0, :] = val

    cols = pl.BlockSpec((_TJ, 1, R), lambda j: (j, 0, 0))
    half = pl.BlockSpec((R // 2, _TJ), lambda j: (0, j))
    return pl.pallas_call(
        body, grid=(pl.cdiv(C, _TJ),), name=name,
        in_specs=[cols, half, half, cols, cols], out_specs=[cols] * 4, out_shape=[_sds((C, 1, R), F32)] * 4,
        compiler_params=_cp(),
    )(w, own, other, m, v)


def _lanes(rows):
    return jnp.concatenate([rows[i:i + 1, :] for i in range(rows.shape[0])], axis=1)


def _small_update(a, packs):
    names = [n for n, _ in _SMALL]
    where = {}
    for k, (pnames, rows, _) in enumerate(packs):
        o = 0
        for n, r in zip(pnames, rows):
            where[n] = (k, o, r)
            o += r
    view = {n: (1, 1024) for n in names}
    view.update(gm_ws=(1024, 128), gm_bs=(8, 128), ssd_conv_w=(4, 384), ssd_conv_b=(1, CONV_CH),
                ssd_dt_bias=(1, 16), ssd_a_log=(1, 16), ssd_d=(1, 16))
    npk = len(packs)

    def body(*refs):
        tots = []
        for k in range(npk):
            tot = refs[k][0]
            for d in range(1, 8):
                tot = tot + refs[k][d]
            tots.append(tot)
        ins, outs = refs[npk:npk + 3 * len(names)], refs[npk + 3 * len(names):]
        chip = 2 * lax.axis_index("x") + lax.axis_index("y")
        for i, n in enumerate(names):
            k, o, r = where[n]
            blk = tots[k][o:o + r, :]
            if n == "gm_ws":
                g = blk
            elif n == "gm_bs":
                g = blk[0:8]
            elif view[n] == (1, 16):
                g = blk[0:1, 0:16]
            elif n == "ssd_conv_w":
                taps = jnp.concatenate([_lanes(blk[12 * t:12 * t + 12]) for t in range(4)], axis=0)
                g = taps[:, 0:384]
                for c in range(1, 4):
                    g = jnp.where(chip == c, taps[:, 384 * c:384 * (c + 1)], g)
            else:
                g = _lanes(blk[0:view[n][1] // 128])
            d, mo, vo = _adamw_vals(ins[3 * i][...], g, ins[3 * i + 1][...], ins[3 * i + 2][...])
            for j, val in enumerate((g, d, mo, vo)):
                outs[4 * i + j][...] = val
        k, o, _ = where["loss"]
        outs[-1][...] = tots[k][o:o + 1, 0:1]

    ins = [a[pre + n].reshape(view[n]) for n in names for pre in ("", "m_", "v_")]
    res = pl.pallas_call(
        body, name="small_update",
        out_shape=[_sds(view[n], F32) for n in names for _ in range(4)] + [_sds((1, 1), F32)],
    )(*[slots for _, _, slots in packs], *ins)
    return {n: tuple(r.reshape(a[n].shape) for r in res[4 * i:4 * i + 4]) for i, n in enumerate(names)}, res[-1]


def _sum_slots(items, kh, name):
    n = len(items)
    in_specs, out_specs, out_shape = [], [], []
    for slots, src, kind, (R, C) in items:
        tr = R // (2 * _PARTS)
        if kind == "slab":
            src_spec = pl.BlockSpec((1, tr, C), lambda i, kh: (kh[0], kh[1] * _PARTS + i, 0))
        elif kind == "rows":
            src_spec = pl.BlockSpec((tr, C), lambda i, kh: (kh[0] * (2 * _PARTS) + kh[1] * _PARTS + i, 0))
        else:
            src_spec = pl.BlockSpec((tr, C), lambda i, kh: (kh[1] * _PARTS + i, kh[0]))
        in_specs += [pl.BlockSpec((8, tr, C), lambda i, kh: (0, i, 0)), src_spec]
        out_specs.append(pl.BlockSpec((tr, C), lambda i, kh: (i, 0)))
        out_shape.append(_sds((R // 2, C), F32))

    def body(kh_ref, *refs):
        me = 2 * kh_ref[0] + kh_ref[1]
        for k, (_, _, kind, _) in enumerate(items):
            s_ref, own_ref, o_ref = refs[2 * k], refs[2 * k + 1], refs[2 * n + k]
            acc = (own_ref[0] if kind == "slab" else own_ref[...]).astype(F32)
            for j in range(1, 8):
                acc = acc + s_ref[me ^ j].astype(F32)
            o_ref[...] = acc

    return pl.pallas_call(
        body, name=name,
        grid_spec=pltpu.PrefetchScalarGridSpec(
            num_scalar_prefetch=1, grid=(_PARTS,), in_specs=in_specs, out_specs=out_specs),
        out_shape=out_shape,
        compiler_params=_cp(),
    )(kh, *[a for slots, src, _, _ in items for a in (slots, src)])


def _assemble_w_in(slabs):
    tr = 256

    def body(s_ref, wm_ref, wdt_ref):
        full = jnp.concatenate([s_ref[k] for k in range(4)], axis=1)
        wm_ref[...] = full[:, :N_MAIN]
        wdt_ref[...] = jnp.concatenate([full[:, N_MAIN:], jnp.zeros((tr, 128 - 16), full.dtype)], axis=1)

    return pl.pallas_call(
        body, grid=(D // tr,), name="assemble_w_in",
        in_specs=[pl.BlockSpec((4, tr, 1156), lambda i: (0, i, 0))],
        out_specs=[_rows(tr, N_MAIN), _rows(tr, 128)],
        out_shape=[_sds((D, N_MAIN), slabs.dtype), _sds((D, 128), slabs.dtype)],
        compiler_params=_cp(),
    )(slabs)


def _split_dw_in(d_uv, d_ssd, d_dt):
    tr = 256

    def body(uv_ref, ssd_ref, dt_ref, o_ref):
        full = jnp.concatenate([uv_ref[...], ssd_ref[...], dt_ref[:, 0:16]], axis=1)
        for k in range(4):
            o_ref[k] = full[:, 1156 * k:1156 * (k + 1)]

    return pl.pallas_call(
        body, grid=(D // tr,), name="split_dw_in",
        in_specs=[_rows(tr, 2048), _rows(tr, 2560), _rows(tr, 128)],
        out_specs=pl.BlockSpec((4, tr, 1156), lambda i: (0, i, 0)),
        out_shape=_sds((4, D, 1156), d_uv.dtype),
        compiler_params=_cp(),
    )(d_uv, d_ssd, d_dt)


def _cast_w_in(w, kh):
    C, _, R = w.shape

    def body(kh_ref, w_ref, o_ref):
        o_ref[0] = w_ref[:, 0, :].T.astype(BF16)

    return pl.pallas_call(
        body, name="cast_w_in",
        grid_spec=pltpu.PrefetchScalarGridSpec(
            num_scalar_prefetch=1, grid=(pl.cdiv(C, _TJ),),
            in_specs=[pl.BlockSpec((_TJ, 1, R), lambda j, kh: (j, 0, 0))],
            out_specs=pl.BlockSpec((1, R, _TJ), lambda j, kh: (kh[0], 0, j))),
        out_shape=_sds((4, R, C), BF16),
        compiler_params=_cp(),
    )(kh, w)


def _cast_into_slot(ws, kh, name):
    n = len(ws)

    def body(kh_ref, *refs):
        for k in range(n):
            refs[n + k][0] = refs[k][...].astype(BF16)

    return pl.pallas_call(
        body, name=name,
        grid_spec=pltpu.PrefetchScalarGridSpec(
            num_scalar_prefetch=1, grid=(_PARTS,),
            in_specs=[pl.BlockSpec((w.shape[0] // _PARTS, w.shape[1]), lambda i, kh: (i, 0)) for w in ws],
            out_specs=[pl.BlockSpec((1, w.shape[0] // _PARTS, w.shape[1]), lambda i, kh: (kh[0], i, 0))
                       for w in ws]),
        out_shape=[_sds((4,) + w.shape, BF16) for w in ws],
        compiler_params=_cp(),
    )(kh, *ws)


_ANY = pl.BlockSpec(memory_space=pl.ANY)
_CHIP_FLIPS = [(1, 0), (0, 1), (1, 1)]
_DEVICE_FLIPS = [(fx, fy, fc) for fx in (0, 1) for fy in (0, 1) for fc in (0, 1)][1:]


def _half(h, rows):
    return pl.ds(pl.multiple_of(h * rows, rows), rows)


def _remote(src, dst, ssem, rsem, to):
    return pltpu.make_async_remote_copy(src_ref=src, dst_ref=dst, send_sem=ssem, recv_sem=rsem,
                                        device_id=to, device_id_type=MESH)


def _weight_gather(bufs, conv):
    n = len(bufs)

    def body(*refs):
        conv_ref, outs, conv_out = refs[n], refs[n + 1:2 * n + 1], refs[2 * n + 1]
        send_sems, recv_sems, fsend_sems, frecv_sems, csend_sems, crecv_sems, local_sem = refs[2 * n + 2:]
        x, y, c = lax.axis_index("x"), lax.axis_index("y"), lax.axis_index("c")
        me = 2 * x + y
        halves = [_half(c, r.shape[1] // 2) for r in outs]
        others = [_half(1 - c, r.shape[1] // 2) for r in outs]
        remote = _remote
        local = [pltpu.make_async_copy(conv_ref, conv_out.at[me], local_sem)]
        for cp in local:
            cp.start()
        sends = []
        for k, (fx, fy) in enumerate(_CHIP_FLIPS):
            peer = (x ^ fx, y ^ fy, c)
            for i in range(n):
                mine = outs[i].at[me, halves[i]]
                sends.append(remote(mine, mine, send_sems.at[k * n + i], recv_sems.at[k * n + i], peer))
            sends.append(remote(conv_ref, conv_out.at[me], csend_sems.at[k], crecv_sems.at[k], peer))
        for cp in sends:
            cp.start()
        sibling = (x, y, 1 - c)
        forwards = []
        for k, (fx, fy) in enumerate(_CHIP_FLIPS):
            peer = (x ^ fx, y ^ fy, c)
            src = 2 * (x ^ fx) + (y ^ fy)
            for i in range(n):
                landed = outs[i].at[src, halves[i]]
                remote(landed, landed, send_sems.at[k * n + i], recv_sems.at[k * n + i], peer).wait_recv()
                fw = remote(landed, landed, fsend_sems.at[k * n + i], frecv_sems.at[k * n + i], sibling)
                fw.start()
                forwards.append(fw)
            remote(conv_out.at[src], conv_out.at[src], csend_sems.at[k], crecv_sems.at[k], peer).wait_recv()
        for k, (fx, fy) in enumerate(_CHIP_FLIPS):
            src = 2 * (x ^ fx) + (y ^ fy)
            for i in range(n):
                theirs = outs[i].at[src, others[i]]
                remote(theirs, theirs, fsend_sems.at[k * n + i], frecv_sems.at[k * n + i], sibling).wait_recv()
        for cp in sends + forwards:
            cp.wait_send()
        for cp in local:
            cp.wait()

    dma = pltpu.SemaphoreType.DMA
    return pl.pallas_call(
        body, name="weight_gather",
        in_specs=[_ANY] * (n + 1), out_specs=[_ANY] * (n + 1),
        out_shape=[_sds(b.shape, b.dtype) for b in bufs] + [_sds((4,) + conv.shape, conv.dtype)],
        input_output_aliases={i: i for i in range(n)},
        scratch_shapes=[dma((3 * n,)), dma((3 * n,)), dma((3 * n,)), dma((3 * n,)), dma((3,)), dma((3,)), dma],
    )(*bufs, conv)


def _piece(ref, kind, R, C, k, h):
    if kind == "slab":
        return ref.at[k, _half(h, R // 2), :]
    if kind == "rows":
        return ref.at[pl.ds(pl.multiple_of(k * R + h * (R // 2), R // 2), R // 2), :]
    return ref.at[_half(h, R // 2), pl.ds(pl.multiple_of(k * C, C), C)]


_HBM = pl.BlockSpec(memory_space=pltpu.HBM)
_SEM = pl.BlockSpec(memory_space=pltpu.SEMAPHORE)


def _split_start(name, arrays, n_copies, plan, after=None):
    n = len(arrays)
    extra = [] if after is None else [after]

    def body(*refs):
        m = n + len(extra)
        arrs, send_sems, recv_sems, token = refs[:n], refs[m], refs[m + 1], refs[-1]
        for j, (src, dst, peer) in enumerate(plan(arrs)):
            _remote(src, dst, send_sems.at[j], recv_sems.at[j], peer).start()
        token[...] = jnp.zeros_like(token)

    dma = pltpu.SemaphoreType.DMA
    res = pl.pallas_call(
        body, name=name,
        out_shape=(dma((n_copies,)), dma((n_copies,)), *[pltpu.HBM(a.shape, a.dtype) for a in arrays],
                   _sds((8, 128), F32)),
        in_specs=[_HBM] * n + [_ANY] * len(extra),
        out_specs=(_SEM, _SEM, *[_HBM] * n, pl.BlockSpec(memory_space=pltpu.VMEM)),
        input_output_aliases={i: 2 + i for i in range(n)},
        compiler_params=pltpu.CompilerParams(has_side_effects=pltpu.SideEffectType.DATAFLOW_SIDE_EFFECTING),
    )(*[pltpu.with_memory_space_constraint(a, pltpu.HBM) for a in arrays], *extra)
    return res[0], res[1], list(res[2:2 + n]), res[-1]


def _split_wait(name, arrays, send_sems, recv_sems, plan, after, first=0):
    n = len(arrays)

    def body(*refs):
        arrs, ssems, rsems = refs[:n], refs[n], refs[n + 1]
        for j, (src, dst, peer) in enumerate(plan(arrs), first):
            cp = _remote(src, dst, ssems.at[j], rsems.at[j], peer)
            cp.wait_send()
            cp.wait_recv()

    return list(pl.pallas_call(
        body, name=name,
        out_shape=tuple(pltpu.HBM(a.shape, a.dtype) for a in arrays),
        in_specs=[_HBM] * n + [_SEM, _SEM, _ANY],
        out_specs=tuple([_HBM] * n),
        input_output_aliases={i: i for i in range(n)},
        compiler_params=pltpu.CompilerParams(has_side_effects=pltpu.SideEffectType.DATAFLOW_SIDE_EFFECTING),
    )(*arrays, send_sems, recv_sems, after))


def _gather_plan(n):
    def plan(bufs):
        x, y, c = lax.axis_index("x"), lax.axis_index("y"), lax.axis_index("c")
        me = 2 * x + y
        return [(bufs[i].at[me], bufs[i].at[me], (x ^ fx, y ^ fy, c)) for fx, fy in _CHIP_FLIPS for i in range(n)]

    return plan


def _reduce_plan(specs, n_small):
    n = len(specs)

    def plan(arrs):
        x, y, c = lax.axis_index("x"), lax.axis_index("y"), lax.axis_index("c")
        slot = 4 * x + 2 * y + c
        out = []
        for fx, fy, fc in _DEVICE_FLIPS:
            peer = (x ^ fx, y ^ fy, c ^ fc)
            for i, (kind, (R, C)) in enumerate(specs):
                out.append((_piece(arrs[i], kind, R, C, 2 * peer[0] + peer[1], peer[2]), arrs[n + i].at[slot], peer))
            for s in range(n_small):
                out.append((arrs[2 * n + 2 * s], arrs[2 * n + 2 * s + 1].at[slot], peer))
        return out

    return plan


def _sibling_plan(n):
    def plan(arrs):
        x, y, c = lax.axis_index("x"), lax.axis_index("y"), lax.axis_index("c")
        return [(arrs[i], arrs[n + i], (x, y, 1 - c)) for i in range(n)]

    return plan


def _sibling_exchange(halves, name, small=None):
    n = len(halves)
    ns = 0 if small is None else 1

    def body(*refs):
        ins, outs = refs[:n], refs[n + ns:2 * n + ns]
        send_sems, recv_sems = refs[2 * (n + ns)], refs[2 * (n + ns) + 1]
        x, y, c = lax.axis_index("x"), lax.axis_index("y"), lax.axis_index("c")
        copies = [_remote(ins[i], outs[i], send_sems.at[i], recv_sems.at[i], (x, y, 1 - c)) for i in range(n)]
        waits = list(copies)
        if ns:
            s_ref, slots_ref, ssend_sems, srecv_sems, local_sem = refs[n], refs[2 * n + 1], *refs[2 * (n + ns) + 2:]
            slot = 4 * x + 2 * y + c
            own = pltpu.make_async_copy(s_ref, slots_ref.at[slot], local_sem)
            own.start()
            for k, (fx, fy, fc) in enumerate(_DEVICE_FLIPS):
                peer = (x ^ fx, y ^ fy, c ^ fc)
                copies.append(_remote(s_ref, slots_ref.at[slot], ssend_sems.at[k], srecv_sems.at[k], peer))
                theirs = slots_ref.at[slot ^ (k + 1)]
                waits.append(_remote(theirs, theirs, ssend_sems.at[k], srecv_sems.at[k], peer))
        for cp in copies:
            cp.start()
        for cp in waits:
            cp.wait()
        if ns:
            own.wait()

    dma = pltpu.SemaphoreType.DMA
    extra_in = [] if small is None else [small]
    extra_out = [] if small is None else [_sds((8,) + small.shape, F32)]
    return pl.pallas_call(
        body, name=name,
        in_specs=[_ANY] * (n + ns), out_specs=[_ANY] * (n + ns),
        out_shape=[_sds(h.shape, h.dtype) for h in halves] + extra_out,
        scratch_shapes=[dma((n,)), dma((n,))] + ([dma((7,)), dma((7,)), dma] if ns else []),
    )(*halves, *extra_in)


_BIG = [("w_in", (1024, 1156), "slab"), ("w_out", (512, 1024), "rows"), ("w_ff1", (1024, 1024), "cols"),
        ("w_ff2", (1024, 1024), "rows"), ("w_ple_gate", (256, 1024), "rows"), ("w_ple_proj", (256, 256), "cols")]
_SMALL = [("norm_mix_g", (1, 1024)), ("gm_v_norm_g", (1, 1024)), ("gm_ws", (1, 8, 128, 128)), ("gm_bs", (1, 8, 128)),
          ("gm_out_norm_g", (1, 1024)), ("ssd_conv_w", (1, 4, 1536)), ("ssd_conv_b", (1, 1536)),
          ("ssd_dt_bias", (1, 16)), ("ssd_a_log", (1, 16)), ("ssd_d", (1, 16)), ("ssd_norm_g", (1, 1024)),
          ("norm_mlp_g", (1, 1024)), ("ple_norm_g", (1, 1024)), ("final_norm_g", (1024,))]


def _rows128(a):
    flat = a.reshape(-1)
    rows = -(-flat.shape[0] // 1024) * 8
    return jnp.pad(flat, (0, rows * 128 - flat.shape[0])).reshape(rows, 128)


def _pad_lanes(v, n=128):
    v = v.reshape(1, -1)
    return jnp.pad(v, ((0, 0), (0, n - v.shape[1])))


_SMALL_SHAPES = dict(_SMALL + [("loss", ())])
_BIG_SPECS = {n: (kind, shp) for n, shp, kind in _BIG}


class _Comm:
    def __init__(self, a, kh):
        self.a, self.kh = a, kh
        rest = _BIG[1:]
        self.bufs = {"w_in": _cast_w_in(a["w_in"].transpose(2, 0, 1), kh)}
        cast = _cast_into_slot([a[n].reshape(shp) for n, shp, _ in rest], kh, "cast_rest")
        self.bufs.update({n: c for (n, _, _), c in zip(rest, cast)})
        self.sent = []
        self.small_packs = []

    def w_in(self):
        g_win, g_cw = _weight_gather([self.bufs["w_in"]], self.a["ssd_conv_w"].reshape(4, 384))
        token = g_cw
        self.gather = {}
        for tag, names in (("out", ["w_out", "w_ff1"]), ("ff", ["w_ff2", "w_ple_gate", "w_ple_proj"])):
            plan = _gather_plan(len(names))
            ssem, rsem, thru, token = _split_start("gather_start_" + tag, [self.bufs[n] for n in names],
                                                   3 * len(names), plan, after=token)
            self.gather[tag] = (plan, ssem, rsem, thru)
        wm, wdt = _assemble_w_in(g_win)
        return wm, wdt, jnp.concatenate([g_cw[k] for k in range(4)], axis=1), token

    def rest(self, tag, after):
        plan, ssem, rsem, thru = self.gather[tag]
        got = _split_wait("gather_wait_" + tag, thru, ssem, rsem, plan, after)
        if tag == "out":
            return got[0].reshape(2048, D), got[1]
        g_w2, g_wg, g_wp = got
        return g_w2.reshape(DFF, D), g_wg.reshape(D, D), g_wp

    def send(self, tag, grads):
        big = [n for n, _, _ in _BIG if n in grads]
        small = [n for n in _SMALL_SHAPES if n in grads]
        parts = [_rows128(grads[n]) for n in small]
        rows = [s.shape[0] for s in parts]
        if not big:
            self.last_small = (tag, small, rows, jnp.concatenate(parts, axis=0))
            return None
        srcs = [grads[n] for n in big]
        lands = [lax.empty((8, _BIG_SPECS[n][1][0] // 2, _BIG_SPECS[n][1][1]), GRAD) for n in big]
        extra = []
        if small:
            pack = jnp.concatenate(parts, axis=0)
            extra = [pack, jnp.broadcast_to(pack, (8,) + pack.shape)]
        red_plan = _reduce_plan([_BIG_SPECS[n] for n in big], len(extra) // 2)
        n_copies = 7 * (len(big) + len(extra) // 2)
        halves = []
        if self.sent:
            self.early = self._landed(self.sent[0], srcs[0])
            halves = list(self.early.values())
        arrays = srcs + lands + extra
        nr, nh = len(arrays), len(halves)
        sib_plan = _sibling_plan(nh)

        def plan(arrs):
            return sib_plan(arrs[nr:]) + red_plan(arrs[:nr])

        arrays += halves + [lax.empty(h.shape, F32) for h in halves]
        ssem, rsem, thru, token = _split_start("reduce_start_" + tag, arrays, n_copies + nh, plan)
        self.sent.append((tag, big, small, rows, red_plan, ssem, rsem, thru[:nr], nh))
        self.swap = (sib_plan, ssem, rsem, thru[nr:])
        return token

    def _landed(self, sent, after):
        tag, big, small, rows, plan, ssem, rsem, thru, first = sent
        arrs = _split_wait("reduce_wait_" + tag, thru, ssem, rsem, plan, after, first)
        nb_ = len(big)
        sums = _sum_slots([(arrs[nb_ + i], arrs[i]) + _BIG_SPECS[n] for i, n in enumerate(big)], self.kh, "sum_" + tag)
        if small:
            self.small_packs.append((small, rows, arrs[2 * nb_ + 1]))
        return dict(zip(big, sums))

    def finish(self, after):
        a, results = self.a, {}
        sib_plan, ssem, rsem, swapped = self.swap
        names = list(self.early)
        swapped = _split_wait("sibling_wait_early", swapped, ssem, rsem, sib_plan, after)
        mine, other = swapped[:len(names)], swapped[len(names):]
        items = [(a[n].reshape(_BIG_SPECS[n][1]), own, oth, a["m_" + n].reshape(_BIG_SPECS[n][1]),
                  a["v_" + n].reshape(_BIG_SPECS[n][1])) for n, own, oth in zip(names, mine, other)]
        results.update(zip(names, _adamw_halves(items, "adamw_early")))
        own = self._landed(self.sent[-1], results[names[-1]][1])
        stag, small, rows, pack = self.last_small
        other, slots = _sibling_exchange([own["w_in"]], "sibling_exchange_late", pack)
        self.small_packs.append((small, rows, slots))
        w, m, v = (a[k].transpose(2, 0, 1) for k in ("w_in", "m_w_in", "v_w_in"))
        raw = _adamw_transposed(w, own["w_in"], other, m, v, "adamw_late")
        results["w_in"] = tuple(r.transpose(1, 2, 0) for r in raw)
        return results, self.small_packs


def _local_step(x, p, tgt, sm, comm, nb, tm):
    T = x.shape[0]
    wm, wdt, conv_w, token = comm.w_in()
    g_mix, gv, gout = sm["norm_mix_g"].reshape(1, D), sm["gm_v_norm_g"].reshape(1, D), sm["gm_out_norm_g"].reshape(1, D)
    ws = sm["gm_ws"].reshape(GM_HEADS, CH, CH)
    bst = jnp.pad(sm["gm_bs"].reshape(GM_HEADS, CH).T, ((0, 0), (0, 128 - GM_HEADS)))
    convw = jnp.pad(conv_w, ((0, 4), (0, 0)))
    convb = sm["ssd_conv_b"].reshape(1, CONV_CH)
    dtb, alog = _pad_lanes(sm["ssd_dt_bias"]), _pad_lanes(sm["ssd_a_log"])
    dskip = jnp.repeat(sm["ssd_d"].reshape(SSD_HEADS), SSD_P).reshape(1, 1024)
    ng, g_mlp, g_ple = sm["ssd_norm_g"].reshape(1, D), sm["norm_mlp_g"].reshape(1, D), sm["ple_norm_g"].reshape(1, D)
    gf = sm["final_norm_g"].reshape(1, D)
    head_of_lane = lax.broadcasted_iota(jnp.int32, (128, 1024), 1) // SSD_P
    ex = (lax.broadcasted_iota(jnp.int32, (128, 1024), 0) == head_of_lane).astype(BF16)
    ext = ex.T
    ltri = (lax.broadcasted_iota(jnp.int32, (CH, CH), 0) >= lax.broadcasted_iota(jnp.int32, (CH, CH), 1)).astype(F32)

    pz, pxbc, dtraw, xn, cat, uv = _inproj_gmlp(x, g_mix, wm, wdt, gv, ws, bst, gout, tm, token)
    cat, sall, conv = _ssd_fwd(pz, pxbc, dtraw, cat, convw, convb, dtb, alog, dskip, ng, ex, ltri, nb)
    wo, w1 = comm.rest("out", cat)
    h1, hn, hid = _outproj_ff1(cat, wo, x, g_mlp, w1, tm)
    w2, wg, wp = comm.rest("ff", hn)
    hp, dgl, dpe, dh2, dh2b, loss, d_gf, d_gple = _ff2_tail(hid, w2, h1, g_ple, p, tgt, wg, wp, gf, tm)

    d_wp = _matmul_tn(p, dpe, "dw_ple_proj", a_fn=lambda a: a.astype(MXU))
    d_wg = _matmul_tn(hp, dgl, "dw_ple_gate")
    d_w2 = _matmul_tn(hid, dh2b, "dw_ff2", a_fn=_sq)
    dpre = _ff2_bwd(dh2b, w2, hid, min(T, 2 * tm))
    d_w1 = _matmul_tn(hn, dpre, "dw_ff1")
    dh1, dh1b, d_gmlp, dcat = _ff1_bwd(dpre, w1, dh2, h1, g_mlp, wo, tm)
    d_wo = _matmul_tn(cat, dh1b, "dw_out")
    duv, d_gv, d_ws, d_bst, d_gout, dxn_uv = _gmlp_bwd(uv, dcat, gv, ws, bst, gout, wm)
    token = comm.send("early", {
        "w_ple_proj": d_wp, "w_ple_gate": d_wg, "w_ff2": d_w2, "w_ff1": d_w1, "w_out": d_wo, "loss": loss[0:1, 0:1], "final_norm_g": d_gf, "ple_norm_g": d_gple, "norm_mlp_g": d_gmlp,
        "gm_v_norm_g": d_gv, "gm_ws": d_ws, "gm_bs": d_bst[:, :GM_HEADS].T, "gm_out_norm_g": d_gout})
    dssd, ddt, d_cw, d_cb, d_dtb, d_al, d_ds, d_ng = _ssd_bwd(
        pz, pxbc, conv, dtraw, sall, dcat, convw, dtb, alog, dskip, ng, ex, ltri, ext, nb, token)
    d_win = _split_dw_in(_matmul_tn(xn, duv, "dw_in_uv"), _matmul_tn(xn, dssd, "dw_in_ssd"),
                         _matmul_tn(xn, ddt, "dw_in_dt"))
    token = comm.send("late", {"w_in": d_win})
    dx, d_gmix = _inproj_bwd(dxn_uv, dssd, ddt, wm, wdt, dh1, x, g_mix, tm, token)
    comm.send("d", {"norm_mix_g": d_gmix, "ssd_conv_w": d_cw[0:4], "ssd_conv_b": d_cb, "ssd_dt_bias": d_dtb[:, :16],
                    "ssd_a_log": d_al[:, :16], "ssd_d": d_ds[:, :16], "ssd_norm_g": d_ng})
    return dx


def kernel(x, p, norm_mix_g, w_in, gm_v_norm_g, gm_ws, gm_bs, gm_out_norm_g, ssd_conv_w, ssd_conv_b, ssd_dt_bias, ssd_a_log, ssd_d, ssd_norm_g, w_out, norm_mlp_g, w_ff1, w_ff2, ple_norm_g, w_ple_gate, w_ple_proj, final_norm_g, loss_target, m_norm_mix_g, m_w_in, m_gm_v_norm_g, m_gm_ws, m_gm_bs, m_gm_out_norm_g, m_ssd_conv_w, m_ssd_conv_b, m_ssd_dt_bias, m_ssd_a_log, m_ssd_d, m_ssd_norm_g, m_w_out, m_norm_mlp_g, m_w_ff1, m_w_ff2, m_ple_norm_g, m_w_ple_gate, m_w_ple_proj, m_final_norm_g, v_norm_mix_g, v_w_in, v_gm_v_norm_g, v_gm_ws, v_gm_bs, v_gm_out_norm_g, v_ssd_conv_w, v_ssd_conv_b, v_ssd_dt_bias, v_ssd_a_log, v_ssd_d, v_ssd_norm_g, v_w_out, v_norm_mlp_g, v_w_ff1, v_w_ff2, v_ple_norm_g, v_w_ple_gate, v_w_ple_proj, v_final_norm_g):
    a = dict(locals())
    order = ["norm_mix_g", "w_in", "gm_v_norm_g", "gm_ws", "gm_bs", "gm_out_norm_g", "ssd_conv_w", "ssd_conv_b",
             "ssd_dt_bias", "ssd_a_log", "ssd_d", "ssd_norm_g", "w_out", "norm_mlp_g", "w_ff1", "w_ff2", "ple_norm_g",
             "w_ple_gate", "w_ple_proj", "final_norm_g"]
    chip = 2 * lax.axis_index("x") + lax.axis_index("y")
    nb, S = x.shape[0], x.shape[1]
    T = nb * S
    sm = {n: a[n] for n, _ in _SMALL if n != "ssd_conv_w"}
    comm = _Comm(a, jnp.stack([chip, lax.axis_index("c")]).astype(jnp.int32))
    dx = _local_step(x.reshape(T, D), p.reshape(T, DPLE), loss_target.reshape(T, D), sm, comm, nb, 512)
    big, small_packs = comm.finish(dx)
    small, loss = _small_update(a, small_packs)
    g_out, delta, new_m, new_v = {}, {}, {}, {}
    for n in order:
        g_out[n], delta[n], new_m[n], new_v[n] = (r.reshape(a[n].shape) for r in (big[n] if n in big else small[n]))
    return (loss.reshape(()), dx.reshape(x.shape), *[g_out[n] for n in order], *[delta[n] for n in order],
            *[new_m[n] for n in order], *[new_v[n] for n in order])
```

```python
import jax
import jax.numpy as jnp
from jax import lax
from jax.experimental import pallas as pl
from jax.experimental.pallas import tpu as pltpu

F32 = jnp.float32
BF16 = jnp.bfloat16
MXU = jnp.bfloat16
GRAD = jnp.bfloat16

D = 1024
CH = 128
GM_HEADS = 8
SSD_HEADS = 16
SSD_P = 64
CONV_CH = 1536
N_MAIN = 4608
DFF = 4096
DPLE = 256
EPS = 1e-6
NEG = -1e30

LR, B1, B2, ADAM_EPS, WD, STEP = 0.001, 0.9, 0.999, 1e-08, 0.01, 10

VMEM_LIMIT = 56 * 1024 * 1024
_SEQS_PER_STEP = 4
MESH = pl.DeviceIdType.MESH

INV_SQRT2 = 0.7071067811865476
INV_SQRT_2PI = 0.3989422804014327


def _cp(n_axes=1):
    return pltpu.CompilerParams(dimension_semantics=("arbitrary",) * n_axes, vmem_limit_bytes=VMEM_LIMIT)


def _dot(a, b):
    return jnp.dot(a, b, preferred_element_type=F32)


def _dot_nt(a, b):
    return lax.dot_general(a, b, (((1,), (1,)), ((), ())), preferred_element_type=F32)


def _dot_tn(a, b):
    return lax.dot_general(a, b, (((0,), (0,)), ((), ())), preferred_element_type=F32)


def _dot_hi(a, b):
    return jnp.dot(a, b, preferred_element_type=F32, precision=lax.Precision.HIGHEST)


def _dot_01(a, sel):
    hi = a.astype(BF16)
    lo = (a - hi.astype(F32)).astype(BF16)
    n = a.shape[0]
    r = _dot(jnp.concatenate([hi, lo], axis=0), sel)
    return r[0:n] + r[n:2 * n]


def _rows(tm, n, j=0):
    return pl.BlockSpec((tm, n), lambda i: (i, j))


def _const(shape):
    nd = len(shape)
    return pl.BlockSpec(shape, lambda *_: (0,) * nd)


def _sds(shape, dtype):
    return jax.ShapeDtypeStruct(shape, dtype)


def _rms(x):
    r = lax.rsqrt(jnp.mean(x * x, axis=-1, keepdims=True) + EPS)
    return x * r, r


def _rms_bwd(dy, xhat, r, g):
    dyg = dy * g
    return r * (dyg - xhat * jnp.mean(dyg * xhat, axis=-1, keepdims=True))


def _sigmoid(x):
    return 1.0 / (1.0 + jnp.exp(-x))


def _gelu(x):
    cdf = 0.5 * (1.0 + lax.erf(x * INV_SQRT2))
    pdf = jnp.exp(-0.5 * x * x) * INV_SQRT_2PI
    return x * cdf, cdf + x * pdf


def _softplus(x):
    e = jnp.exp(-jnp.abs(x))
    u = 1.0 + e
    log1p = jnp.where(u == 1.0, e, jnp.log(u) * e / (u - 1.0))
    return jnp.maximum(x, 0.0) + log1p


def _after(n_in, fn):
    def body(*refs):
        return fn(*refs[:n_in], *refs[n_in + 1:])

    return body


def _inproj_gmlp(x, g, wm, wdt, gv, ws, bst, gout, tm, after):
    T = x.shape[0]

    def body(x_ref, g_ref, wm_ref, wdt_ref, gv_ref, ws_ref, bst_ref, gout_ref,
             z_ref, xbc_ref, dt_ref, xn_ref, ya_ref, uv_ref):
        xh, _ = _rms(x_ref[...])
        xn = (xh * g_ref[...]).astype(MXU)
        xn_ref[...] = xn
        for n in range(4):
            uv_ref[:, n * 512:(n + 1) * 512] = _dot(xn, wm_ref[:, n * 512:(n + 1) * 512])
        for n in range(2):
            z_ref[:, n * 512:(n + 1) * 512] = _dot(xn, wm_ref[:, 2048 + n * 512:2048 + (n + 1) * 512])
        for n in range(3):
            xbc_ref[:, n * 512:(n + 1) * 512] = _dot(xn, wm_ref[:, 3072 + n * 512:3072 + (n + 1) * 512])
        dt_ref[...] = _dot(xn, wdt_ref[...])
        for k in range(tm // CH):
            rows = slice(k * CH, (k + 1) * CH)
            f = _gmlp_fwd_vals(uv_ref[rows, 0:1024], uv_ref[rows, 1024:2048], gv_ref[...], ws_ref, bst_ref[...],
                               gout_ref[...])
            ya_ref[rows, :] = f["out"].astype(MXU)

    return pl.pallas_call(
        _after(8, body), grid=(T // tm,), name="inproj_gmlp",
        in_specs=[_rows(tm, D), _const((1, D)), _const((D, N_MAIN)), _const((D, 128)), _const((1, 1024)),
                  _const((GM_HEADS, CH, CH)), _const((CH, 128)), _const((1, 1024)), _ANY],
        out_specs=[_rows(tm, 1024), _rows(tm, CONV_CH), _rows(tm, 128), _rows(tm, D), _rows(tm, 1024, 0),
                   _rows(tm, 2048)],
        out_shape=[_sds((T, 1024), F32), _sds((T, CONV_CH), F32), _sds((T, 128), F32), _sds((T, D), MXU),
                   _sds((T, 2048), MXU), _sds((T, 2048), F32)],
        compiler_params=_cp(),
    )(x, g, wm, wdt, gv, ws, bst, gout, after)


def _gmlp_fwd_vals(u, v, gv, ws_ref, bst, gout):
    ug, dug = _gelu(u)
    vg, dvg = _gelu(v)
    row = lax.broadcasted_iota(jnp.int32, (CH, CH), 0)
    col = lax.broadcasted_iota(jnp.int32, (CH, CH), 1)
    tril = row >= col
    ys, heads = [], []
    for h in range(GM_HEADS):
        sl = slice(h * 128, (h + 1) * 128)
        vhat, rv = _rms(vg[:, sl])
        vn = (vhat * gv[:, sl]).astype(MXU)
        wt = jnp.where(tril, ws_ref[h], 0.0)
        mixed = _dot(wt.astype(MXU), vn) + bst[:, h:h + 1]
        ys.append(ug[:, sl] * mixed)
        heads.append((vhat, rv, vn, wt, mixed))
    y = jnp.concatenate(ys, axis=1)
    yhat, ry = _rms(y)
    return dict(ug=ug, dug=dug, dvg=dvg, heads=heads, yhat=yhat, ry=ry, tril=tril, out=yhat * gout)


def _shifts_down(cur, halo):
    row8 = lax.broadcasted_iota(jnp.int32, (8, cur.shape[1]), 0)
    out = [cur]
    for j in (1, 2, 3):
        sh = pltpu.roll(cur, j, 0)
        top = jnp.where(row8 < j, pltpu.roll(halo, j, 0), sh[0:8])
        out.append(jnp.concatenate([top, sh[8:]], axis=0))
    return out


def _shifts_up(cur, halo):
    row8 = lax.broadcasted_iota(jnp.int32, (8, cur.shape[1]), 0)
    out = []
    for j in (1, 2, 3):
        sh = pltpu.roll(cur, CH - j, 0)
        bot = jnp.where(row8 + j >= 8, pltpu.roll(halo, 8 - j, 0), sh[CH - 8:CH])
        out.append(jnp.concatenate([sh[0:CH - 8], bot], axis=0))
    return out


def _conv(xbc, halo, convw, convb):
    sh = _shifts_down(xbc, halo)
    return convb + convw[3:4] * sh[0] + convw[2:3] * sh[1] + convw[1:2] * sh[2] + convw[0:1] * sh[3]


def _ssd_fwd_vals(z, conv, dtraw, dtb, alog, dskip, ng, ex, ltri, s_prev):
    sig_c = _sigmoid(conv)
    xa = conv * sig_c
    xs = xa[:, :1024]
    bm = [xa[:, 1024:1152], xa[:, 1152:1280]]
    cm = [xa[:, 1280:1408], xa[:, 1408:1536]]
    dtpre = dtraw + dtb
    dt = _softplus(dtpre)
    a_neg = -jnp.exp(alog)
    cs = _dot_hi(ltri, dt * a_neg)
    cst = cs.T
    last = cs[CH - 1:CH]
    ecs = jnp.exp(cs)
    dec = jnp.exp(last - cs)
    spread = _dot_01(jnp.concatenate([dt, ecs, dec], axis=0), ex)
    dte, ecse, dece = spread[0:CH], spread[CH:2 * CH], spread[2 * CH:3 * CH]
    cde = ecse[CH - 1:CH]
    de = dskip
    xdt = xs * dte
    row = lax.broadcasted_iota(jnp.int32, (CH, CH), 0)
    col = lax.broadcasted_iota(jnp.int32, (CH, CH), 1)
    tril = row >= col
    lo = col < SSD_P
    bmb = [b.astype(MXU) for b in bm]
    cmb = [c.astype(MXU) for c in cm]
    mg = [_dot_nt(cmb[g], bmb[g]) for g in range(2)]
    yd, lms, whs = [], [], []
    for q in range(8):
        g = q // 4
        xq = xdt[:, q * 128:(q + 1) * 128]
        acc = None
        for hh in range(2):
            h = 2 * q + hh
            seg = cs[:, h:h + 1] - cst[h:h + 1, :]
            lm = jnp.exp(jnp.where(tril, seg, NEG))
            wh = (mg[g] * lm).astype(MXU)
            xm = jnp.where(lo if hh == 0 else ~lo, xq, 0.0).astype(MXU)
            part = _dot(wh, xm)
            acc = part if acc is None else acc + part
            lms.append(lm)
            whs.append(wh)
        yd.append(acc)
    yd = jnp.concatenate(yd, axis=1)
    sb = s_prev.astype(MXU)
    yo = jnp.concatenate([_dot(cmb[g], sb[:, g * 512:(g + 1) * 512]) for g in range(2)], axis=1) * ecse
    xdec = (xdt * dece).astype(MXU)
    states = jnp.concatenate([_dot_tn(bmb[g], xdec[:, g * 512:(g + 1) * 512]) for g in range(2)], axis=1)
    s_next = s_prev * cde + states
    ypre = yd + yo + de * xs
    sig_z = _sigmoid(z)
    yg = ypre * z * sig_z
    outs, yhat, rr = [], [], []
    for g in range(2):
        sl = slice(g * 512, (g + 1) * 512)
        yh, r = _rms(yg[:, sl])
        yhat.append(yh)
        rr.append(r)
        outs.append(yh * ng[:, sl])
    return dict(sig_c=sig_c, xa=xa, xs=xs, bmb=bmb, cmb=cmb, dtpre=dtpre, dt=dt, a_neg=a_neg,
                cs=cs, last=last, ecs=ecs, dec=dec, dte=dte, ecse=ecse, dece=dece, cde=cde, de=de, xdt=xdt,
                mg=mg, lms=lms, whs=whs, lo=lo, yo=yo, sb=sb, xdec=xdec, s_next=s_next, ypre=ypre, sig_z=sig_z,
                yhat=yhat, rr=rr, out=jnp.concatenate(outs, axis=1))


def _ssd_fwd(pz, pxbc, dtraw, cat, convw, convb, dtb, alog, dskip, ng, ex, ltri, nb):
    T = pz.shape[0]
    S = T // nb
    nch = S // CH
    ns = _SEQS_PER_STEP if nb % _SEQS_PER_STEP == 0 else 1

    def body(z_ref, xbc_ref, halo_ref, dt_ref, cw_ref, cb_ref, dtb_ref, al_ref, ds_ref, ng_ref, ex_ref, lt_ref,
             cat_in_ref, yb_ref, sall_ref, conv_ref, s_ref):
        del cat_in_ref
        c = pl.program_id(1)

        @pl.when(c == 0)
        def _():
            s_ref[...] = jnp.zeros_like(s_ref)

        for i in range(ns):
            halo = jnp.where(c == 0, 0.0, halo_ref[i])
            s_prev = s_ref[i]
            sall_ref[i, 0] = s_prev
            conv = _conv(xbc_ref[i], halo, cw_ref[...], cb_ref[...])
            conv_ref[i] = conv
            f = _ssd_fwd_vals(z_ref[i], conv, dt_ref[i], dtb_ref[...], al_ref[...], ds_ref[...], ng_ref[...],
                              ex_ref[...], lt_ref[...], s_prev)
            s_ref[i] = f["s_next"]
            yb_ref[i] = f["out"].astype(MXU)

    def seq(width, col=0):
        return pl.BlockSpec((ns, CH, width), lambda b, c: (b, c, col))

    cat, sall, conv = pl.pallas_call(
        body, grid=(nb // ns, nch), name="ssd_fwd",
        in_specs=[seq(1024), seq(CONV_CH),
                  pl.BlockSpec((ns, 8, CONV_CH), lambda b, c: (b, jnp.maximum(c * (CH // 8) - 1, 0), 0)),
                  seq(128),
                  _const((8, CONV_CH)), _const((1, CONV_CH)), _const((1, 128)), _const((1, 128)), _const((1, 1024)),
                  _const((1, 1024)), _const((128, 1024)), _const((CH, CH)), _ANY],
        out_specs=[seq(1024, 1), pl.BlockSpec((ns, 1, 128, 1024), lambda b, c: (b, c, 0, 0)), seq(CONV_CH)],
        out_shape=[_sds((nb, S, 2048), MXU), _sds((nb, nch, 128, 1024), F32), _sds((nb, S, CONV_CH), F32)],
        scratch_shapes=[pltpu.VMEM((ns, 128, 1024), F32)],
        input_output_aliases={12: 0},
        compiler_params=_cp(2),
    )(pz.reshape(nb, S, 1024), pxbc.reshape(nb, S, CONV_CH), pxbc.reshape(nb, S, CONV_CH), dtraw.reshape(nb, S, 128),
      convw, convb, dtb, alog, dskip, ng, ex, ltri, cat.reshape(nb, S, 2048))
    return cat.reshape(T, 2048), sall, conv.reshape(T, CONV_CH)


def _outproj_ff1(cat, wo, x, g, w1, tm):
    T = x.shape[0]

    def body(cat_ref, wo_ref, x_ref, g_ref, w1_ref, h1_ref, hn_ref, hid_ref):
        h1 = x_ref[...] + _dot(cat_ref[...], wo_ref[...])
        h1_ref[...] = h1
        hn = (_rms(h1)[0] * g_ref[...]).astype(MXU)
        hn_ref[...] = hn
        for n in range(4):
            hid_ref[:, n * 1024:(n + 1) * 1024] = jnp.maximum(_dot(hn, w1_ref[n]), 0.0).astype(MXU)

    return pl.pallas_call(
        body, grid=(T // tm,), name="outproj_ff1",
        in_specs=[_rows(tm, 2048), _const((2048, D)), _rows(tm, D), _const((1, D)), _const((4, D, 1024))],
        out_specs=[_rows(tm, D), _rows(tm, D), _rows(tm, DFF)],
        out_shape=[_sds((T, D), F32), _sds((T, D), MXU), _sds((T, DFF), MXU)],
        compiler_params=_cp(),
    )(cat, wo, x, g, w1)


def _sq(hid):
    h = hid.astype(F32)
    return (h * h).astype(MXU)


def _ff2_tail(hid, w2, h1, g_ple, p, tgt, wg, wp, gf, tm):
    T = h1.shape[0]

    def body(hid_ref, w2_ref, h1_ref, g_ref, p_ref, t_ref, wg_ref, wp_ref, gf_ref,
             hp_ref, dgl_ref, dpe_ref, dh2_ref, dh2b_ref, loss_ref, dgf_ref, dg_ref):
        @pl.when(pl.program_id(0) == 0)
        def _():
            loss_ref[...] = jnp.zeros_like(loss_ref)
            dgf_ref[...] = jnp.zeros_like(dgf_ref)
            dg_ref[...] = jnp.zeros_like(dg_ref)

        h2 = h1_ref[...] + _dot(_sq(hid_ref[...]), w2_ref[...])
        h2h, r2 = _rms(h2)
        g_ple = g_ref[...]
        hp = (h2h * g_ple).astype(MXU)
        hp_ref[...] = hp
        gate = _sigmoid(_dot(hp, wg_ref[...]))
        pb = p_ref[...].astype(MXU)
        pe = jnp.concatenate([_dot(pb, wp_ref[k]) for k in range(4)], axis=1)
        h3 = h2 + gate * pe
        hh, r = _rms(h3)
        gf = gf_ref[...]
        diff = hh * gf - t_ref[...]
        loss_ref[...] += 0.5 * jnp.sum(jnp.mean(diff * diff, axis=-1, keepdims=True))
        dout = diff * (1.0 / D)
        dgf_ref[...] += jnp.sum(dout * hh, axis=0, keepdims=True)
        dh3 = _rms_bwd(dout, hh, r, gf)
        dgl = (dh3 * pe * gate * (1.0 - gate)).astype(MXU)
        dgl_ref[...] = dgl
        dpe_ref[...] = (dh3 * gate).astype(MXU)
        dhp = _dot_nt(dgl, wg_ref[...])
        dg_ref[...] += jnp.sum(dhp * h2h, axis=0, keepdims=True)
        dh2 = dh3 + _rms_bwd(dhp, h2h, r2, g_ple)
        dh2_ref[...] = dh2
        dh2b_ref[...] = dh2.astype(MXU)

    return pl.pallas_call(
        body, grid=(T // tm,), name="ff2_tail",
        in_specs=[_rows(tm, DFF), _const((DFF, D)), _rows(tm, D), _const((1, D)), _rows(tm, DPLE), _rows(tm, D),
                  _const((D, D)), _const((4, DPLE, 256)), _const((1, D))],
        out_specs=[_rows(tm, D), _rows(tm, D), _rows(tm, D), _rows(tm, D), _rows(tm, D), _const((8, 128)),
                   _const((1, D)), _const((1, D))],
        out_shape=[_sds((T, D), MXU), _sds((T, D), MXU), _sds((T, D), MXU), _sds((T, D), F32), _sds((T, D), MXU),
                   _sds((8, 128), F32), _sds((1, D), F32), _sds((1, D), F32)],
        compiler_params=_cp(),
    )(hid, w2, h1, g_ple, p, tgt, wg, wp, gf)


def _ff2_bwd(dh2b, w2, hid, tm):
    T = hid.shape[0]

    def body(dh2b_ref, w2_ref, hid_ref, dpre_ref):
        d = dh2b_ref[...]
        for n in range(DFF // 1024):
            sl = slice(n * 1024, (n + 1) * 1024)
            da = _dot_nt(d, w2_ref[sl, :])
            dpre_ref[:, sl] = (2.0 * da * hid_ref[:, sl].astype(F32)).astype(MXU)

    return pl.pallas_call(
        body, grid=(T // tm,), name="ff2_bwd",
        in_specs=[_rows(tm, D), _const((DFF, D)), _rows(tm, DFF)],
        out_specs=_rows(tm, DFF),
        out_shape=_sds((T, DFF), MXU),
        compiler_params=_cp(),
    )(dh2b, w2, hid)


def _ff1_bwd(dpre, w1, dh2, h1, g, wo, tm):
    T = h1.shape[0]

    def body(dpre_ref, w1_ref, dh2_ref, h1_ref, g_ref, wo_ref, dh1_ref, dh1b_ref, dg_ref, dcat_ref):
        @pl.when(pl.program_id(0) == 0)
        def _():
            dg_ref[...] = jnp.zeros_like(dg_ref)

        dhn = _dot_nt(dpre_ref[:, 0:1024], w1_ref[0])
        for k in range(1, 4):
            dhn = dhn + _dot_nt(dpre_ref[:, k * 1024:(k + 1) * 1024], w1_ref[k])
        hh, r = _rms(h1_ref[...])
        dg_ref[...] += jnp.sum(dhn * hh, axis=0, keepdims=True)
        dh1 = dh2_ref[...] + _rms_bwd(dhn, hh, r, g_ref[...])
        dh1_ref[...] = dh1
        dh1b = dh1.astype(MXU)
        dh1b_ref[...] = dh1b
        dcat_ref[:, 0:1024] = _dot_nt(dh1b, wo_ref[0:1024, :])
        dcat_ref[:, 1024:2048] = _dot_nt(dh1b, wo_ref[1024:2048, :])

    return pl.pallas_call(
        body, grid=(T // tm,), name="ff1_bwd",
        in_specs=[_rows(tm, DFF), _const((4, D, 1024)), _rows(tm, D), _rows(tm, D), _const((1, D)),
                  _const((2048, D))],
        out_specs=[_rows(tm, D), _rows(tm, D), _const((1, D)), _rows(tm, 2048)],
        out_shape=[_sds((T, D), F32), _sds((T, D), MXU), _sds((1, D), F32), _sds((T, 2048), F32)],
        compiler_params=_cp(),
    )(dpre, w1, dh2, h1, g, wo)


def _gmlp_bwd(uv, dcat, gv, ws, bst, gout, wm):
    T = uv.shape[0]
    nck = 4 if T % (4 * CH) == 0 else 1
    tb = nck * CH

    def body(uv_ref, dya_ref, gv_ref, ws_ref, bst_ref, gout_ref, wuv_ref, duv_ref, dgv_ref, dws_ref, dbst_ref,
             dgo_ref, dxn_ref):
        @pl.when(pl.program_id(0) == 0)
        def _():
            dgv_ref[...] = jnp.zeros_like(dgv_ref)
            dws_ref[...] = jnp.zeros_like(dws_ref)
            dbst_ref[...] = jnp.zeros_like(dbst_ref)
            dgo_ref[...] = jnp.zeros_like(dgo_ref)

        for k in range(nck):
            chunk(slice(k * CH, (k + 1) * CH), uv_ref, dya_ref, gv_ref, ws_ref, bst_ref, gout_ref, duv_ref,
                  dgv_ref, dws_ref, dbst_ref, dgo_ref)
        dxn_ref[...] = _dot_nt(duv_ref[...], wuv_ref[...])

    def chunk(rows, uv_ref, dya_ref, gv_ref, ws_ref, bst_ref, gout_ref, duv_ref, dgv_ref, dws_ref, dbst_ref,
              dgo_ref):
        gv = gv_ref[...]
        f = _gmlp_fwd_vals(uv_ref[rows, 0:1024], uv_ref[rows, 1024:2048], gv, ws_ref, bst_ref[...], gout_ref[...])
        dya = dya_ref[rows, :]
        dgo_ref[...] += jnp.sum(dya * f["yhat"], axis=0, keepdims=True)
        dy = _rms_bwd(dya, f["yhat"], f["ry"], gout_ref[...])
        lane = lax.broadcasted_iota(jnp.int32, (CH, 128), 1)
        dbs = jnp.zeros((CH, 128), F32)
        dug, dvg, dgvs = [], [], []
        for h in range(GM_HEADS):
            sl = slice(h * 128, (h + 1) * 128)
            vhat, rv, vn, wt, mixed = f["heads"][h]
            dyh = dy[:, sl]
            dug.append(dyh * mixed)
            dmixed = dyh * f["ug"][:, sl]
            dmb = dmixed.astype(MXU)
            dws_ref[h] += jnp.where(f["tril"], _dot_nt(dmb, vn), 0.0)
            dbs = dbs + jnp.where(lane == h, jnp.sum(dmixed, axis=1, keepdims=True), 0.0)
            dvn = _dot_tn(wt.astype(MXU), dmb)
            dgvs.append(jnp.sum(dvn * vhat, axis=0, keepdims=True))
            dvg.append(_rms_bwd(dvn, vhat, rv, gv[:, sl]))
        dbst_ref[...] += dbs
        dgv_ref[...] += jnp.concatenate(dgvs, axis=1)
        duv_ref[rows, 0:1024] = (jnp.concatenate(dug, axis=1) * f["dug"]).astype(MXU)
        duv_ref[rows, 1024:2048] = (jnp.concatenate(dvg, axis=1) * f["dvg"]).astype(MXU)

    return pl.pallas_call(
        body, grid=(T // tb,), name="gmlp_bwd",
        in_specs=[_rows(tb, 2048), _rows(tb, 1024, 0), _const((1, 1024)),
                  _const((GM_HEADS, CH, CH)), _const((CH, 128)), _const((1, 1024)), _const((D, 2048))],
        out_specs=[_rows(tb, 2048), _const((1, 1024)), _const((GM_HEADS, CH, CH)), _const((CH, 128)),
                   _const((1, 1024)), _rows(tb, D)],
        out_shape=[_sds((T, 2048), MXU), _sds((1, 1024), F32), _sds((GM_HEADS, CH, CH), F32), _sds((CH, 128), F32),
                   _sds((1, 1024), F32), _sds((T, D), F32)],
        compiler_params=_cp(),
    )(uv, dcat, gv, ws, bst, gout, wm)


def _ssd_bwd(pz, pxbc, conv, dtraw, sall, dcat, convw, dtb, alog, dskip, ng, ex, ltri, ext, nb, after):
    T = pz.shape[0]
    S = T // nb
    nch = S // CH
    ns = _SEQS_PER_STEP if nb % _SEQS_PER_STEP == 0 else 1

    def seq(width, col=0):
        return pl.BlockSpec((ns, CH, width), lambda b, c: (b, nch - 1 - c, col))

    in_specs = [
        seq(1024), seq(CONV_CH), seq(CONV_CH), seq(128),
        _const((8, CONV_CH)), _const((1, 128)), _const((1, 128)), _const((1, 1024)),
        _const((1, 1024)), _const((128, 1024)), _const((CH, CH)),
        _const((1024, 128)),
        pl.BlockSpec((ns, 1, 128, 1024), lambda b, c: (b, nch - 1 - c, 0, 0)),
        seq(1024, 1),
        _ANY,
    ]

    def body(z_ref, xbc_ref, conv_ref, dt_ref, cw_ref, dtb_ref, al_ref, ds_ref, ng_ref, ex_ref, lt_ref,
             ext_ref, sall_ref, dyb_ref,
             dssd_ref, ddt_ref, dcw_ref, dcb_ref, ddtb_ref, dal_ref, dds_ref, dng_ref,
             dst_ref, dnext_ref, ddse_ref):
        b = pl.program_id(0)
        c = pl.program_id(1)

        @pl.when((b == 0) & (c == 0))
        def _():
            for r in (dcw_ref, dcb_ref, ddtb_ref, dal_ref, dds_ref, dng_ref, ddse_ref):
                r[...] = jnp.zeros_like(r)

        @pl.when(c == 0)
        def _():
            dst_ref[...] = jnp.zeros_like(dst_ref)
            dnext_ref[...] = jnp.zeros_like(dnext_ref)

        ex = ex_ref[...]
        ext = ext_ref[...]
        cw = cw_ref[...]
        ng = ng_ref[...]
        for i in range(ns):
            one_chunk(i, ex, ext, cw, ng, z_ref, xbc_ref, conv_ref, dt_ref, dtb_ref, al_ref, ds_ref, lt_ref, sall_ref,
                      dyb_ref, dssd_ref, ddt_ref, dcw_ref, dcb_ref, ddtb_ref, dal_ref, dng_ref, dst_ref, dnext_ref,
                      ddse_ref)

        @pl.when((b == nb // ns - 1) & (c == nch - 1))
        def _():
            dds_ref[...] = _dot_01(jnp.broadcast_to(ddse_ref[...], (8, 1024)), ext)[0:1]

    def one_chunk(i, ex, ext, cw, ng, z_ref, xbc_ref, conv_ref, dt_ref, dtb_ref, al_ref, ds_ref, lt_ref, sall_ref,
                  dyb_ref, dssd_ref, ddt_ref, dcw_ref, dcb_ref, ddtb_ref, dal_ref, dng_ref, dst_ref, dnext_ref,
                  ddse_ref):
        z = z_ref[i]
        s_prev = sall_ref[i, 0]
        conv = conv_ref[i]
        f = _ssd_fwd_vals(z, conv, dt_ref[i], dtb_ref[...], al_ref[...], ds_ref[...], ng, ex, lt_ref[...], s_prev)
        xs, xdt, cs, dec, dt = f["xs"], f["xdt"], f["cs"], f["dec"], f["dt"]
        dyb = dyb_ref[i]
        dyg, dngs = [], []
        for g in range(2):
            sl = slice(g * 512, (g + 1) * 512)
            dngs.append(jnp.sum(dyb[:, sl] * f["yhat"][g], axis=0, keepdims=True))
            dyg.append(_rms_bwd(dyb[:, sl], f["yhat"][g], f["rr"][g], ng[:, sl]))
        dng_ref[...] += jnp.concatenate(dngs, axis=1)
        dyg = jnp.concatenate(dyg, axis=1)
        sig_z = f["sig_z"]
        silu_z = z * sig_z
        dy = dyg * silu_z
        dz = dyg * f["ypre"] * (sig_z + silu_z * (1.0 - sig_z))
        ddse_ref[...] += jnp.sum(dy * xs, axis=0, keepdims=True)
        dxs = dy * f["de"]
        dye = dy * f["ecse"]
        dyeb = dye.astype(MXU)
        dst = dst_ref[i]
        dstb = dst.astype(MXU)
        bmb, cmb, sb, xdec = f["bmb"], f["cmb"], f["sb"], f["xdec"]
        u = jnp.concatenate([_dot(bmb[g], dstb[:, g * 512:(g + 1) * 512]) for g in range(2)], axis=1)
        dxdt = [u[:, q * 128:(q + 1) * 128] * f["dece"][:, q * 128:(q + 1) * 128] for q in range(8)]
        per_head = _dot_01(jnp.concatenate(
            [dy * f["yo"], u * xdt, jnp.broadcast_to(jnp.sum(dst * s_prev, axis=0, keepdims=True), (8, 1024))],
            axis=0), ext)
        dcs = per_head[0:CH]
        t = per_head[CH:2 * CH] * dec
        dcd = per_head[2 * CH:2 * CH + 1]
        row = lax.broadcasted_iota(jnp.int32, (CH, 128), 0)
        lane = lax.broadcasted_iota(jnp.int32, (CH, 128), 1)
        cd = jnp.exp(f["last"])
        dcs = dcs - t + jnp.where(row == CH - 1, jnp.sum(t, axis=0, keepdims=True) + dcd * cd, 0.0)
        dcst = jnp.zeros((128, CH), F32)
        lo = f["lo"]
        dbm, dcm, ds_prev = [], [], []
        for g in range(2):
            sl = slice(g * 512, (g + 1) * 512)
            dmg = jnp.zeros((CH, CH), F32)
            for q in range(4 * g, 4 * g + 4):
                dyq = dy[:, q * 128:(q + 1) * 128]
                xq = xdt[:, q * 128:(q + 1) * 128].astype(MXU)
                for hh in range(2):
                    h = 2 * q + hh
                    m = lo if hh == 0 else ~lo
                    dym = jnp.where(m, dyq, 0.0).astype(MXU)
                    gh = _dot_nt(dym, xq)
                    gl = gh * f["lms"][h]
                    dmg = dmg + gl
                    qh = gl * f["mg"][g]
                    dcs = dcs + jnp.where(lane == h, jnp.sum(qh, axis=1, keepdims=True), 0.0)
                    dcst = dcst - jnp.where(row == h, jnp.sum(qh, axis=0, keepdims=True), 0.0)
                    dxdt[q] = dxdt[q] + _dot_tn(f["whs"][h], dym)
            dmgb = dmg.astype(MXU)
            dcm.append(_dot(dmgb, bmb[g]) + _dot_nt(dyeb[:, sl], sb[:, sl]))
            dbm.append(_dot_tn(dmgb, cmb[g]) + _dot_nt(xdec[:, sl], dstb[:, sl]))
            ds_prev.append(_dot_tn(cmb[g], dyeb[:, sl]))
        dst_ref[i] = jnp.concatenate(ds_prev, axis=1) + dst * f["cde"]
        dcs = dcs + dcst.T
        da = _dot_hi(lt_ref[...].T, dcs)
        dxdt = jnp.concatenate(dxdt, axis=1)
        a_neg = f["a_neg"]
        ddt = da * a_neg + _dot_01(dxdt * xs, ext)
        dal_ref[...] += jnp.sum(da * dt, axis=0, keepdims=True) * a_neg
        dxs = dxs + dxdt * f["dte"]
        ddtraw = jnp.where(lane < SSD_HEADS, ddt * _sigmoid(f["dtpre"]), 0.0)
        ddtb_ref[...] += jnp.sum(ddtraw, axis=0, keepdims=True)
        ddt_ref[i] = ddtraw.astype(MXU)
        dxa = jnp.concatenate([dxs, dbm[0], dbm[1], dcm[0], dcm[1]], axis=1)
        sig_c = f["sig_c"]
        dconv = dxa * (sig_c + f["xa"] * (1.0 - sig_c))
        dcb_ref[...] += jnp.sum(dconv, axis=0, keepdims=True)
        xbc = xbc_ref[i]
        dcw_ref[3:4, :] += jnp.sum(dconv * xbc, axis=0, keepdims=True)
        dxbc = cw[3:4] * dconv
        for j, up in zip((1, 2, 3), _shifts_up(dconv, dnext_ref[i])):
            dcw_ref[3 - j:4 - j, :] += jnp.sum(up * xbc, axis=0, keepdims=True)
            dxbc = dxbc + cw[3 - j:4 - j] * up
        dnext_ref[i] = dconv[0:8]
        dssd_ref[i, :, 0:1024] = dz.astype(MXU)
        dssd_ref[i, :, 1024:2560] = dxbc.astype(MXU)

    dssd, ddt, *small = pl.pallas_call(
        _after(14, body), grid=(nb // ns, nch), name="ssd_bwd",
        in_specs=in_specs,
        out_specs=[seq(2560), seq(128),
                   _const((8, CONV_CH)), _const((1, CONV_CH)), _const((1, 128)), _const((1, 128)), _const((1, 128)),
                   _const((1, 1024))],
        out_shape=[_sds((nb, S, 2560), MXU), _sds((nb, S, 128), MXU), _sds((8, CONV_CH), F32),
                   _sds((1, CONV_CH), F32), _sds((1, 128), F32), _sds((1, 128), F32), _sds((1, 128), F32),
                   _sds((1, 1024), F32)],
        scratch_shapes=[pltpu.VMEM((ns, 128, 1024), F32), pltpu.VMEM((ns, 8, CONV_CH), F32),
                        pltpu.VMEM((1, 1024), F32)],
        compiler_params=_cp(2),
    )(pz.reshape(nb, S, 1024), pxbc.reshape(nb, S, CONV_CH), conv.reshape(nb, S, CONV_CH), dtraw.reshape(nb, S, 128),
      convw, dtb, alog, dskip, ng, ex, ltri, ext, sall, dcat.reshape(nb, S, 2048), after)
    return (dssd.reshape(T, 2560), ddt.reshape(T, 128), *small)


def _inproj_bwd(dxn_uv, dssd, ddt, wm, wdt, dh1, x, g, tm, after):
    T = x.shape[0]

    def body(dxnuv_ref, dssd_ref, ddt_ref, wm_ref, wdt_ref, dh1_ref, x_ref, g_ref, dx_ref, dg_ref):
        @pl.when(pl.program_id(0) == 0)
        def _():
            dg_ref[...] = jnp.zeros_like(dg_ref)

        dxn = (dxnuv_ref[...] + _dot_nt(dssd_ref[...], wm_ref[:, 2048:N_MAIN])
               + _dot_nt(ddt_ref[...], wdt_ref[...]))
        xh, r = _rms(x_ref[...])
        dg_ref[...] += jnp.sum(dxn * xh, axis=0, keepdims=True)
        dx_ref[...] = dh1_ref[...] + _rms_bwd(dxn, xh, r, g_ref[...])

    return pl.pallas_call(
        _after(8, body), grid=(T // tm,), name="inproj_bwd",
        in_specs=[_rows(tm, D), _rows(tm, 2560), _rows(tm, 128), _const((D, N_MAIN)), _const((D, 128)),
                  _rows(tm, D), _rows(tm, D), _const((1, D)), _ANY],
        out_specs=[_rows(tm, D), _const((1, D))],
        out_shape=[_sds((T, D), F32), _sds((1, D), F32)],
        compiler_params=_cp(),
    )(dxn_uv, dssd, ddt, wm, wdt, dh1, x, g, after)


def _matmul_tn(a, b, name, a_fn=None):
    T, M = a.shape
    N = b.shape[1]
    tm = min(M, 1024)
    tn = 1280 if N == 2560 else min(N, 1024)
    tk = min(T, 2048)

    def body(a_ref, b_ref, o_ref, acc_ref):
        k = pl.program_id(2)

        @pl.when(k == 0)
        def _():
            acc_ref[...] = jnp.zeros_like(acc_ref)

        av = a_ref[...]
        if a_fn is not None:
            av = a_fn(av)
        acc_ref[...] += _dot_tn(av, b_ref[...])

        @pl.when(k == T // tk - 1)
        def _():
            o_ref[...] = acc_ref[...].astype(o_ref.dtype)

    return pl.pallas_call(
        body, grid=(M // tm, N // tn, T // tk), name=name,
        in_specs=[pl.BlockSpec((tk, tm), lambda i, j, k: (k, i)), pl.BlockSpec((tk, tn), lambda i, j, k: (k, j))],
        out_specs=pl.BlockSpec((tm, tn), lambda i, j, k: (i, j)),
        out_shape=_sds((M, N), GRAD),
        scratch_shapes=[pltpu.VMEM((tm, tn), F32)],
        compiler_params=_cp(3),
    )(a, b)


def _adamw_vals(w, g, m, v):
    m = B1 * m + (1.0 - B1) * g
    v = B2 * v + (1.0 - B2) * (g * g)
    m_hat = m / (1.0 - B1 ** STEP)
    v_hat = v / (1.0 - B2 ** STEP)
    return -LR * (m_hat / (jnp.sqrt(v_hat) + ADAM_EPS) + WD * w), m, v


_PARTS = 4


def _adamw_halves(items, name):
    n = len(items)

    def body(*refs):
        mine = (pl.program_id(0) // _PARTS) == lax.axis_index("c")
        for k in range(n):
            w_ref, own_ref, oth_ref, m_ref, v_ref = refs[5 * k:5 * k + 5]
            g_ref, d_ref, mo_ref, vo_ref = refs[5 * n + 4 * k:5 * n + 4 * k + 4]
            g = jnp.where(mine, own_ref[...], oth_ref[...])
            g_ref[...] = g
            d_ref[...], mo_ref[...], vo_ref[...] = _adamw_vals(w_ref[...], g, m_ref[...], v_ref[...])

    in_specs, out_specs, out_shape = [], [], []
    for w, *_ in items:
        R, C = w.shape
        full = _rows(R // (2 * _PARTS), C)
        part = pl.BlockSpec((R // (2 * _PARTS), C), lambda i: (i % _PARTS, 0))
        in_specs += [full, part, part, full, full]
        out_specs += [full] * 4
        out_shape += [_sds((R, C), F32)] * 4
    res = pl.pallas_call(
        body, grid=(2 * _PARTS,), name=name, in_specs=in_specs, out_specs=out_specs, out_shape=out_shape,
        compiler_params=_cp(),
    )(*[a for item in items for a in item])
    return [tuple(res[4 * k:4 * k + 4]) for k in range(n)]


_TJ = 128


def _adamw_transposed(w, own, other, m, v, name):
    C, _, R = w.shape

    def body(w_ref, own_ref, oth_ref, m_ref, v_ref, g_ref, d_ref, mo_ref, vo_ref):
        first = lax.axis_index("c") == 0
        g = jnp.concatenate([jnp.where(first, own_ref[...], oth_ref[...]),
                             jnp.where(first, oth_ref[...], own_ref[...])], axis=0).T
        d, mo, vo = _adamw_vals(w_ref[:, 0, :], g, m_ref[:, 0, :], v_ref[:, 0, :])
        for ref, val in ((g_ref, g), (d_ref, d), (mo_ref, mo), (vo_ref, vo)):
            ref[:, 0, :] = val

    cols = pl.BlockSpec((_TJ, 1, R), lambda j: (j, 0, 0))
    half = pl.BlockSpec((R // 2, _TJ), lambda j: (0, j))
    return pl.pallas_call(
        body, grid=(pl.cdiv(C, _TJ),), name=name,
        in_specs=[cols, half, half, cols, cols], out_specs=[cols] * 4, out_shape=[_sds((C, 1, R), F32)] * 4,
        compiler_params=_cp(),
    )(w, own, other, m, v)


def _lanes(rows):
    return jnp.concatenate([rows[i:i + 1, :] for i in range(rows.shape[0])], axis=1)


def _small_update(a, packs):
    names = [n for n, _ in _SMALL]
    where = {}
    for k, (pnames, rows, _) in enumerate(packs):
        o = 0
        for n, r in zip(pnames, rows):
            where[n] = (k, o, r)
            o += r
    view = {n: (1, 1024) for n in names}
    view.update(gm_ws=(1024, 128), gm_bs=(8, 128), ssd_conv_w=(4, 384), ssd_conv_b=(1, CONV_CH),
                ssd_dt_bias=(1, 16), ssd_a_log=(1, 16), ssd_d=(1, 16))
    npk = len(packs)

    def body(*refs):
        tots = []
        for k in range(npk):
            tot = refs[k][0]
            for d in range(1, 8):
                tot = tot + refs[k][d]
            tots.append(tot)
        ins, outs = refs[npk:npk + 3 * len(names)], refs[npk + 3 * len(names):]
        chip = 2 * lax.axis_index("x") + lax.axis_index("y")
        for i, n in enumerate(names):
            k, o, r = where[n]
            blk = tots[k][o:o + r, :]
            if n == "gm_ws":
                g = blk
            elif n == "gm_bs":
                g = blk[0:8]
            elif view[n] == (1, 16):
                g = blk[0:1, 0:16]
            elif n == "ssd_conv_w":
                taps = jnp.concatenate([_lanes(blk[12 * t:12 * t + 12]) for t in range(4)], axis=0)
                g = taps[:, 0:384]
                for c in range(1, 4):
                    g = jnp.where(chip == c, taps[:, 384 * c:384 * (c + 1)], g)
            else:
                g = _lanes(blk[0:view[n][1] // 128])
            d, mo, vo = _adamw_vals(ins[3 * i][...], g, ins[3 * i + 1][...], ins[3 * i + 2][...])
            for j, val in enumerate((g, d, mo, vo)):
                outs[4 * i + j][...] = val
        k, o, _ = where["loss"]
        outs[-1][...] = tots[k][o:o + 1, 0:1]

    ins = [a[pre + n].reshape(view[n]) for n in names for pre in ("", "m_", "v_")]
    res = pl.pallas_call(
        body, name="small_update",
        out_shape=[_sds(view[n], F32) for n in names for _ in range(4)] + [_sds((1, 1), F32)],
    )(*[slots for _, _, slots in packs], *ins)
    return {n: tuple(r.reshape(a[n].shape) for r in res[4 * i:4 * i + 4]) for i, n in enumerate(names)}, res[-1]


def _sum_slots(items, kh, name):
    n = len(items)
    in_specs, out_specs, out_shape = [], [], []
    for slots, src, kind, (R, C) in items:
        tr = R // (2 * _PARTS)
        if kind == "slab":
            src_spec = pl.BlockSpec((1, tr, C), lambda i, kh: (kh[0], kh[1] * _PARTS + i, 0))
        elif kind == "rows":
            src_spec = pl.BlockSpec((tr, C), lambda i, kh: (kh[0] * (2 * _PARTS) + kh[1] * _PARTS + i, 0))
        else:
            src_spec = pl.BlockSpec((tr, C), lambda i, kh: (kh[1] * _PARTS + i, kh[0]))
        in_specs += [pl.BlockSpec((8, tr, C), lambda i, kh: (0, i, 0)), src_spec]
        out_specs.append(pl.BlockSpec((tr, C), lambda i, kh: (i, 0)))
        out_shape.append(_sds((R // 2, C), F32))

    def body(kh_ref, *refs):
        me = 2 * kh_ref[0] + kh_ref[1]
        for k, (_, _, kind, _) in enumerate(items):
            s_ref, own_ref, o_ref = refs[2 * k], refs[2 * k + 1], refs[2 * n + k]
            acc = (own_ref[0] if kind == "slab" else own_ref[...]).astype(F32)
            for j in range(1, 8):
                acc = acc + s_ref[me ^ j].astype(F32)
            o_ref[...] = acc

    return pl.pallas_call(
        body, name=name,
        grid_spec=pltpu.PrefetchScalarGridSpec(
            num_scalar_prefetch=1, grid=(_PARTS,), in_specs=in_specs, out_specs=out_specs),
        out_shape=out_shape,
        compiler_params=_cp(),
    )(kh, *[a for slots, src, _, _ in items for a in (slots, src)])


def _assemble_w_in(slabs):
    tr = 256

    def body(s_ref, wm_ref, wdt_ref):
        full = jnp.concatenate([s_ref[k] for k in range(4)], axis=1)
        wm_ref[...] = full[:, :N_MAIN]
        wdt_ref[...] = jnp.concatenate([full[:, N_MAIN:], jnp.zeros((tr, 128 - 16), full.dtype)], axis=1)

    return pl.pallas_call(
        body, grid=(D // tr,), name="assemble_w_in",
        in_specs=[pl.BlockSpec((4, tr, 1156), lambda i: (0, i, 0))],
        out_specs=[_rows(tr, N_MAIN), _rows(tr, 128)],
        out_shape=[_sds((D, N_MAIN), slabs.dtype), _sds((D, 128), slabs.dtype)],
        compiler_params=_cp(),
    )(slabs)


def _split_dw_in(d_uv, d_ssd, d_dt):
    tr = 256

    def body(uv_ref, ssd_ref, dt_ref, o_ref):
        full = jnp.concatenate([uv_ref[...], ssd_ref[...], dt_ref[:, 0:16]], axis=1)
        for k in range(4):
            o_ref[k] = full[:, 1156 * k:1156 * (k + 1)]

    return pl.pallas_call(
        body, grid=(D // tr,), name="split_dw_in",
        in_specs=[_rows(tr, 2048), _rows(tr, 2560), _rows(tr, 128)],
        out_specs=pl.BlockSpec((4, tr, 1156), lambda i: (0, i, 0)),
        out_shape=_sds((4, D, 1156), d_uv.dtype),
        compiler_params=_cp(),
    )(d_uv, d_ssd, d_dt)


def _cast_w_in(w, kh):
    C, _, R = w.shape

    def body(kh_ref, w_ref, o_ref):
        o_ref[0] = w_ref[:, 0, :].T.astype(BF16)

    return pl.pallas_call(
        body, name="cast_w_in",
        grid_spec=pltpu.PrefetchScalarGridSpec(
            num_scalar_prefetch=1, grid=(pl.cdiv(C, _TJ),),
            in_specs=[pl.BlockSpec((_TJ, 1, R), lambda j, kh: (j, 0, 0))],
            out_specs=pl.BlockSpec((1, R, _TJ), lambda j, kh: (kh[0], 0, j))),
        out_shape=_sds((4, R, C), BF16),
        compiler_params=_cp(),
    )(kh, w)


def _cast_into_slot(ws, kh, name):
    n = len(ws)

    def body(kh_ref, *refs):
        for k in range(n):
            refs[n + k][0] = refs[k][...].astype(BF16)

    return pl.pallas_call(
        body, name=name,
        grid_spec=pltpu.PrefetchScalarGridSpec(
            num_scalar_prefetch=1, grid=(_PARTS,),
            in_specs=[pl.BlockSpec((w.shape[0] // _PARTS, w.shape[1]), lambda i, kh: (i, 0)) for w in ws],
            out_specs=[pl.BlockSpec((1, w.shape[0] // _PARTS, w.shape[1]), lambda i, kh: (kh[0], i, 0))
                       for w in ws]),
        out_shape=[_sds((4,) + w.shape, BF16) for w in ws],
        compiler_params=_cp(),
    )(kh, *ws)


_ANY = pl.BlockSpec(memory_space=pl.ANY)
_CHIP_FLIPS = [(1, 0), (0, 1), (1, 1)]
_DEVICE_FLIPS = [(fx, fy, fc) for fx in (0, 1) for fy in (0, 1) for fc in (0, 1)][1:]


def _half(h, rows):
    return pl.ds(pl.multiple_of(h * rows, rows), rows)


def _remote(src, dst, ssem, rsem, to):
    return pltpu.make_async_remote_copy(src_ref=src, dst_ref=dst, send_sem=ssem, recv_sem=rsem,
                                        device_id=to, device_id_type=MESH)


def _weight_gather(bufs, conv):
    n = len(bufs)

    def body(*refs):
        conv_ref, outs, conv_out = refs[n], refs[n + 1:2 * n + 1], refs[2 * n + 1]
        send_sems, recv_sems, fsend_sems, frecv_sems, csend_sems, crecv_sems, local_sem = refs[2 * n + 2:]
        x, y, c = lax.axis_index("x"), lax.axis_index("y"), lax.axis_index("c")
        me = 2 * x + y
        halves = [_half(c, r.shape[1] // 2) for r in outs]
        others = [_half(1 - c, r.shape[1] // 2) for r in outs]
        remote = _remote
        local = [pltpu.make_async_copy(conv_ref, conv_out.at[me], local_sem)]
        for cp in local:
            cp.start()
        sends = []
        for k, (fx, fy) in enumerate(_CHIP_FLIPS):
            peer = (x ^ fx, y ^ fy, c)
            for i in range(n):
                mine = outs[i].at[me, halves[i]]
                sends.append(remote(mine, mine, send_sems.at[k * n + i], recv_sems.at[k * n + i], peer))
            sends.append(remote(conv_ref, conv_out.at[me], csend_sems.at[k], crecv_sems.at[k], peer))
        for cp in sends:
            cp.start()
        sibling = (x, y, 1 - c)
        forwards = []
        for k, (fx, fy) in enumerate(_CHIP_FLIPS):
            peer = (x ^ fx, y ^ fy, c)
            src = 2 * (x ^ fx) + (y ^ fy)
            for i in range(n):
                landed = outs[i].at[src, halves[i]]
                remote(landed, landed, send_sems.at[k * n + i], recv_sems.at[k * n + i], peer).wait_recv()
                fw = remote(landed, landed, fsend_sems.at[k * n + i], frecv_sems.at[k * n + i], sibling)
                fw.start()
                forwards.append(fw)
            remote(conv_out.at[src], conv_out.at[src], csend_sems.at[k], crecv_sems.at[k], peer).wait_recv()
        for k, (fx, fy) in enumerate(_CHIP_FLIPS):
            src = 2 * (x ^ fx) + (y ^ fy)
            for i in range(n):
                theirs = outs[i].at[src, others[i]]
                remote(theirs, theirs, fsend_sems.at[k * n + i], frecv_sems.at[k * n + i], sibling).wait_recv()
        for cp in sends + forwards:
            cp.wait_send()
        for cp in local:
            cp.wait()

    dma = pltpu.SemaphoreType.DMA
    return pl.pallas_call(
        body, name="weight_gather",
        in_specs=[_ANY] * (n + 1), out_specs=[_ANY] * (n + 1),
        out_shape=[_sds(b.shape, b.dtype) for b in bufs] + [_sds((4,) + conv.shape, conv.dtype)],
        input_output_aliases={i: i for i in range(n)},
        scratch_shapes=[dma((3 * n,)), dma((3 * n,)), dma((3 * n,)), dma((3 * n,)), dma((3,)), dma((3,)), dma],
    )(*bufs, conv)


def _piece(ref, kind, R, C, k, h):
    if kind == "slab":
        return ref.at[k, _half(h, R // 2), :]
    if kind == "rows":
        return ref.at[pl.ds(pl.multiple_of(k * R + h * (R // 2), R // 2), R // 2), :]
    return ref.at[_half(h, R // 2), pl.ds(pl.multiple_of(k * C, C), C)]


_HBM = pl.BlockSpec(memory_space=pltpu.HBM)
_SEM = pl.BlockSpec(memory_space=pltpu.SEMAPHORE)


def _split_start(name, arrays, n_copies, plan, after=None):
    n = len(arrays)
    extra = [] if after is None else [after]

    def body(*refs):
        m = n + len(extra)
        arrs, send_sems, recv_sems, token = refs[:n], refs[m], refs[m + 1], refs[-1]
        for j, (src, dst, peer) in enumerate(plan(arrs)):
            _remote(src, dst, send_sems.at[j], recv_sems.at[j], peer).start()
        token[...] = jnp.zeros_like(token)

    dma = pltpu.SemaphoreType.DMA
    res = pl.pallas_call(
        body, name=name,
        out_shape=(dma((n_copies,)), dma((n_copies,)), *[pltpu.HBM(a.shape, a.dtype) for a in arrays],
                   _sds((8, 128), F32)),
        in_specs=[_HBM] * n + [_ANY] * len(extra),
        out_specs=(_SEM, _SEM, *[_HBM] * n, pl.BlockSpec(memory_space=pltpu.VMEM)),
        input_output_aliases={i: 2 + i for i in range(n)},
        compiler_params=pltpu.CompilerParams(has_side_effects=pltpu.SideEffectType.DATAFLOW_SIDE_EFFECTING),
    )(*[pltpu.with_memory_space_constraint(a, pltpu.HBM) for a in arrays], *extra)
    return res[0], res[1], list(res[2:2 + n]), res[-1]


def _split_wait(name, arrays, send_sems, recv_sems, plan, after, first=0):
    n = len(arrays)

    def body(*refs):
        arrs, ssems, rsems = refs[:n], refs[n], refs[n + 1]
        for j, (src, dst, peer) in enumerate(plan(arrs), first):
            cp = _remote(src, dst, ssems.at[j], rsems.at[j], peer)
            cp.wait_send()
            cp.wait_recv()

    return list(pl.pallas_call(
        body, name=name,
        out_shape=tuple(pltpu.HBM(a.shape, a.dtype) for a in arrays),
        in_specs=[_HBM] * n + [_SEM, _SEM, _ANY],
        out_specs=tuple([_HBM] * n),
        input_output_aliases={i: i for i in range(n)},
        compiler_params=pltpu.CompilerParams(has_side_effects=pltpu.SideEffectType.DATAFLOW_SIDE_EFFECTING),
    )(*arrays, send_sems, recv_sems, after))


def _gather_plan(n):
    def plan(bufs):
        x, y, c = lax.axis_index("x"), lax.axis_index("y"), lax.axis_index("c")
        me = 2 * x + y
        return [(bufs[i].at[me], bufs[i].at[me], (x ^ fx, y ^ fy, c)) for fx, fy in _CHIP_FLIPS for i in range(n)]

    return plan


def _reduce_plan(specs, n_small):
    n = len(specs)

    def plan(arrs):
        x, y, c = lax.axis_index("x"), lax.axis_index("y"), lax.axis_index("c")
        slot = 4 * x + 2 * y + c
        out = []
        for fx, fy, fc in _DEVICE_FLIPS:
            peer = (x ^ fx, y ^ fy, c ^ fc)
            for i, (kind, (R, C)) in enumerate(specs):
                out.append((_piece(arrs[i], kind, R, C, 2 * peer[0] + peer[1], peer[2]), arrs[n + i].at[slot], peer))
            for s in range(n_small):
                out.append((arrs[2 * n + 2 * s], arrs[2 * n + 2 * s + 1].at[slot], peer))
        return out

    return plan


def _sibling_plan(n):
    def plan(arrs):
        x, y, c = lax.axis_index("x"), lax.axis_index("y"), lax.axis_index("c")
        return [(arrs[i], arrs[n + i], (x, y, 1 - c)) for i in range(n)]

    return plan


def _sibling_exchange(halves, name, small=None):
    n = len(halves)
    ns = 0 if small is None else 1

    def body(*refs):
        ins, outs = refs[:n], refs[n + ns:2 * n + ns]
        send_sems, recv_sems = refs[2 * (n + ns)], refs[2 * (n + ns) + 1]
        x, y, c = lax.axis_index("x"), lax.axis_index("y"), lax.axis_index("c")
        copies = [_remote(ins[i], outs[i], send_sems.at[i], recv_sems.at[i], (x, y, 1 - c)) for i in range(n)]
        waits = list(copies)
        if ns:
            s_ref, slots_ref, ssend_sems, srecv_sems, local_sem = refs[n], refs[2 * n + 1], *refs[2 * (n + ns) + 2:]
            slot = 4 * x + 2 * y + c
            own = pltpu.make_async_copy(s_ref, slots_ref.at[slot], local_sem)
            own.start()
            for k, (fx, fy, fc) in enumerate(_DEVICE_FLIPS):
                peer = (x ^ fx, y ^ fy, c ^ fc)
                copies.append(_remote(s_ref, slots_ref.at[slot], ssend_sems.at[k], srecv_sems.at[k], peer))
                theirs = slots_ref.at[slot ^ (k + 1)]
                waits.append(_remote(theirs, theirs, ssend_sems.at[k], srecv_sems.at[k], peer))
        for cp in copies:
            cp.start()
        for cp in waits:
            cp.wait()
        if ns:
            own.wait()

    dma = pltpu.SemaphoreType.DMA
    extra_in = [] if small is None else [small]
    extra_out = [] if small is None else [_sds((8,) + small.shape, F32)]
    return pl.pallas_call(
        body, name=name,
        in_specs=[_ANY] * (n + ns), out_specs=[_ANY] * (n + ns),
        out_shape=[_sds(h.shape, h.dtype) for h in halves] + extra_out,
        scratch_shapes=[dma((n,)), dma((n,))] + ([dma((7,)), dma((7,)), dma] if ns else []),
    )(*halves, *extra_in)


_BIG = [("w_in", (1024, 1156), "slab"), ("w_out", (512, 1024), "rows"), ("w_ff1", (1024, 1024), "cols"),
        ("w_ff2", (1024, 1024), "rows"), ("w_ple_gate", (256, 1024), "rows"), ("w_ple_proj", (256, 256), "cols")]
_SMALL = [("norm_mix_g", (1, 1024)), ("gm_v_norm_g", (1, 1024)), ("gm_ws", (1, 8, 128, 128)), ("gm_bs", (1, 8, 128)),
          ("gm_out_norm_g", (1, 1024)), ("ssd_conv_w", (1, 4, 1536)), ("ssd_conv_b", (1, 1536)),
          ("ssd_dt_bias", (1, 16)), ("ssd_a_log", (1, 16)), ("ssd_d", (1, 16)), ("ssd_norm_g", (1, 1024)),
          ("norm_mlp_g", (1, 1024)), ("ple_norm_g", (1, 1024)), ("final_norm_g", (1024,))]


def _rows128(a):
    flat = a.reshape(-1)
    rows = -(-flat.shape[0] // 1024) * 8
    return jnp.pad(flat, (0, rows * 128 - flat.shape[0])).reshape(rows, 128)


def _pad_lanes(v, n=128):
    v = v.reshape(1, -1)
    return jnp.pad(v, ((0, 0), (0, n - v.shape[1])))


_SMALL_SHAPES = dict(_SMALL + [("loss", ())])
_BIG_SPECS = {n: (kind, shp) for n, shp, kind in _BIG}


class _Comm:
    def __init__(self, a, kh):
        self.a, self.kh = a, kh
        rest = _BIG[1:]
        self.bufs = {"w_in": _cast_w_in(a["w_in"].transpose(2, 0, 1), kh)}
        cast = _cast_into_slot([a[n].reshape(shp) for n, shp, _ in rest], kh, "cast_rest")
        self.bufs.update({n: c for (n, _, _), c in zip(rest, cast)})
        self.sent = []
        self.small_packs = []

    def w_in(self):
        g_win, g_cw = _weight_gather([self.bufs["w_in"]], self.a["ssd_conv_w"].reshape(4, 384))
        first, later = ["w_out", "w_ff1"], ["w_ff2", "w_ple_gate", "w_ple_proj"]
        nf = len(first)
        plans = _gather_plan(nf), _gather_plan(len(later))

        def plan(bufs):
            return plans[0](bufs[:nf]) + plans[1](bufs[nf:])

        ssem, rsem, thru, token = _split_start("gather_start", [self.bufs[n] for n in first + later],
                                               3 * len(first + later), plan, after=g_cw)
        self.gather = {"out": (plans[0], ssem, rsem, thru[:nf], 0), "ff": (plans[1], ssem, rsem, thru[nf:], 3 * nf)}
        wm, wdt = _assemble_w_in(g_win)
        return wm, wdt, jnp.concatenate([g_cw[k] for k in range(4)], axis=1), token

    def rest(self, tag, after):
        plan, ssem, rsem, thru, first = self.gather[tag]
        got = _split_wait("gather_wait_" + tag, thru, ssem, rsem, plan, after, first)
        if tag == "out":
            return got[0].reshape(2048, D), got[1]
        g_w2, g_wg, g_wp = got
        return g_w2.reshape(DFF, D), g_wg.reshape(D, D), g_wp

    def send(self, tag, grads):
        big = [n for n, _, _ in _BIG if n in grads]
        small = [n for n in _SMALL_SHAPES if n in grads]
        parts = [_rows128(grads[n]) for n in small]
        rows = [s.shape[0] for s in parts]
        if not big:
            self.last_small = (tag, small, rows, jnp.concatenate(parts, axis=0))
            return None
        srcs = [grads[n] for n in big]
        lands = [lax.empty((8, _BIG_SPECS[n][1][0] // 2, _BIG_SPECS[n][1][1]), GRAD) for n in big]
        extra = []
        if small:
            pack = jnp.concatenate(parts, axis=0)
            extra = [pack, jnp.broadcast_to(pack, (8,) + pack.shape)]
        red_plan = _reduce_plan([_BIG_SPECS[n] for n in big], len(extra) // 2)
        n_copies = 7 * (len(big) + len(extra) // 2)
        halves = []
        if self.sent:
            self.early = self._landed(self.sent[0], srcs[0])
            halves = list(self.early.values())
        arrays = srcs + lands + extra
        nr, nh = len(arrays), len(halves)
        sib_plan = _sibling_plan(nh)

        def plan(arrs):
            return sib_plan(arrs[nr:]) + red_plan(arrs[:nr])

        arrays += halves + [lax.empty(h.shape, F32) for h in halves]
        ssem, rsem, thru, token = _split_start("reduce_start_" + tag, arrays, n_copies + nh, plan)
        self.sent.append((tag, big, small, rows, red_plan, ssem, rsem, thru[:nr], nh))
        self.swap = (sib_plan, ssem, rsem, thru[nr:])
        return token

    def _landed(self, sent, after):
        tag, big, small, rows, plan, ssem, rsem, thru, first = sent
        arrs = _split_wait("reduce_wait_" + tag, thru, ssem, rsem, plan, after, first)
        nb_ = len(big)
        sums = _sum_slots([(arrs[nb_ + i], arrs[i]) + _BIG_SPECS[n] for i, n in enumerate(big)], self.kh, "sum_" + tag)
        if small:
            self.small_packs.append((small, rows, arrs[2 * nb_ + 1]))
        return dict(zip(big, sums))

    def finish(self, after):
        a, results = self.a, {}
        sib_plan, ssem, rsem, swapped = self.swap
        names = list(self.early)
        swapped = _split_wait("sibling_wait_early", swapped, ssem, rsem, sib_plan, after)
        mine, other = swapped[:len(names)], swapped[len(names):]
        items = [(a[n].reshape(_BIG_SPECS[n][1]), own, oth, a["m_" + n].reshape(_BIG_SPECS[n][1]),
                  a["v_" + n].reshape(_BIG_SPECS[n][1])) for n, own, oth in zip(names, mine, other)]
        results.update(zip(names, _adamw_halves(items, "adamw_early")))
        own = self._landed(self.sent[-1], results[names[-1]][1])
        stag, small, rows, pack = self.last_small
        other, slots = _sibling_exchange([own["w_in"]], "sibling_exchange_late", pack)
        self.small_packs.append((small, rows, slots))
        w, m, v = (a[k].transpose(2, 0, 1) for k in ("w_in", "m_w_in", "v_w_in"))
        raw = _adamw_transposed(w, own["w_in"], other, m, v, "adamw_late")
        results["w_in"] = tuple(r.transpose(1, 2, 0) for r in raw)
        return results, self.small_packs


def _local_step(x, p, tgt, sm, comm, nb, tm):
    T = x.shape[0]
    wm, wdt, conv_w, token = comm.w_in()
    g_mix, gv, gout = sm["norm_mix_g"].reshape(1, D), sm["gm_v_norm_g"].reshape(1, D), sm["gm_out_norm_g"].reshape(1, D)
    ws = sm["gm_ws"].reshape(GM_HEADS, CH, CH)
    bst = jnp.pad(sm["gm_bs"].reshape(GM_HEADS, CH).T, ((0, 0), (0, 128 - GM_HEADS)))
    convw = jnp.pad(conv_w, ((0, 4), (0, 0)))
    convb = sm["ssd_conv_b"].reshape(1, CONV_CH)
    dtb, alog = _pad_lanes(sm["ssd_dt_bias"]), _pad_lanes(sm["ssd_a_log"])
    dskip = jnp.repeat(sm["ssd_d"].reshape(SSD_HEADS), SSD_P).reshape(1, 1024)
    ng, g_mlp, g_ple = sm["ssd_norm_g"].reshape(1, D), sm["norm_mlp_g"].reshape(1, D), sm["ple_norm_g"].reshape(1, D)
    gf = sm["final_norm_g"].reshape(1, D)
    head_of_lane = lax.broadcasted_iota(jnp.int32, (128, 1024), 1) // SSD_P
    ex = (lax.broadcasted_iota(jnp.int32, (128, 1024), 0) == head_of_lane).astype(BF16)
    ext = ex.T
    ltri = (lax.broadcasted_iota(jnp.int32, (CH, CH), 0) >= lax.broadcasted_iota(jnp.int32, (CH, CH), 1)).astype(F32)

    pz, pxbc, dtraw, xn, cat, uv = _inproj_gmlp(x, g_mix, wm, wdt, gv, ws, bst, gout, tm, token)
    cat, sall, conv = _ssd_fwd(pz, pxbc, dtraw, cat, convw, convb, dtb, alog, dskip, ng, ex, ltri, nb)
    wo, w1 = comm.rest("out", cat)
    h1, hn, hid = _outproj_ff1(cat, wo, x, g_mlp, w1, tm)
    w2, wg, wp = comm.rest("ff", hn)
    hp, dgl, dpe, dh2, dh2b, loss, d_gf, d_gple = _ff2_tail(hid, w2, h1, g_ple, p, tgt, wg, wp, gf, tm)

    d_wp = _matmul_tn(p, dpe, "dw_ple_proj", a_fn=lambda a: a.astype(MXU))
    d_wg = _matmul_tn(hp, dgl, "dw_ple_gate")
    d_w2 = _matmul_tn(hid, dh2b, "dw_ff2", a_fn=_sq)
    dpre = _ff2_bwd(dh2b, w2, hid, min(T, 2 * tm))
    d_w1 = _matmul_tn(hn, dpre, "dw_ff1")
    dh1, dh1b, d_gmlp, dcat = _ff1_bwd(dpre, w1, dh2, h1, g_mlp, wo, tm)
    d_wo = _matmul_tn(cat, dh1b, "dw_out")
    duv, d_gv, d_ws, d_bst, d_gout, dxn_uv = _gmlp_bwd(uv, dcat, gv, ws, bst, gout, wm)
    token = comm.send("early", {
        "w_ple_proj": d_wp, "w_ple_gate": d_wg, "w_ff2": d_w2, "w_ff1": d_w1, "w_out": d_wo, "loss": loss[0:1, 0:1], "final_norm_g": d_gf, "ple_norm_g": d_gple, "norm_mlp_g": d_gmlp,
        "gm_v_norm_g": d_gv, "gm_ws": d_ws, "gm_bs": d_bst[:, :GM_HEADS].T, "gm_out_norm_g": d_gout})
    dssd, ddt, d_cw, d_cb, d_dtb, d_al, d_ds, d_ng = _ssd_bwd(
        pz, pxbc, conv, dtraw, sall, dcat, convw, dtb, alog, dskip, ng, ex, ltri, ext, nb, token)
    d_win = _split_dw_in(_matmul_tn(xn, duv, "dw_in_uv"), _matmul_tn(xn, dssd, "dw_in_ssd"),
                         _matmul_tn(xn, ddt, "dw_in_dt"))
    token = comm.send("late", {"w_in": d_win})
    dx, d_gmix = _inproj_bwd(dxn_uv, dssd, ddt, wm, wdt, dh1, x, g_mix, tm, token)
    comm.send("d", {"norm_mix_g": d_gmix, "ssd_conv_w": d_cw[0:4], "ssd_conv_b": d_cb, "ssd_dt_bias": d_dtb[:, :16],
                    "ssd_a_log": d_al[:, :16], "ssd_d": d_ds[:, :16], "ssd_norm_g": d_ng})
    return dx


def kernel(x, p, norm_mix_g, w_in, gm_v_norm_g, gm_ws, gm_bs, gm_out_norm_g, ssd_conv_w, ssd_conv_b, ssd_dt_bias, ssd_a_log, ssd_d, ssd_norm_g, w_out, norm_mlp_g, w_ff1, w_ff2, ple_norm_g, w_ple_gate, w_ple_proj, final_norm_g, loss_target, m_norm_mix_g, m_w_in, m_gm_v_norm_g, m_gm_ws, m_gm_bs, m_gm_out_norm_g, m_ssd_conv_w, m_ssd_conv_b, m_ssd_dt_bias, m_ssd_a_log, m_ssd_d, m_ssd_norm_g, m_w_out, m_norm_mlp_g, m_w_ff1, m_w_ff2, m_ple_norm_g, m_w_ple_gate, m_w_ple_proj, m_final_norm_g, v_norm_mix_g, v_w_in, v_gm_v_norm_g, v_gm_ws, v_gm_bs, v_gm_out_norm_g, v_ssd_conv_w, v_ssd_conv_b, v_ssd_dt_bias, v_ssd_a_log, v_ssd_d, v_ssd_norm_g, v_w_out, v_norm_mlp_g, v_w_ff1, v_w_ff2, v_ple_norm_g, v_w_ple_gate, v_w_ple_proj, v_final_norm_g):
    a = dict(locals())
    order = ["norm_mix_g", "w_in", "gm_v_norm_g", "gm_ws", "gm_bs", "gm_out_norm_g", "ssd_conv_w", "ssd_conv_b",
             "ssd_dt_bias", "ssd_a_log", "ssd_d", "ssd_norm_g", "w_out", "norm_mlp_g", "w_ff1", "w_ff2", "ple_norm_g",
             "w_ple_gate", "w_ple_proj", "final_norm_g"]
    chip = 2 * lax.axis_index("x") + lax.axis_index("y")
    nb, S = x.shape[0], x.shape[1]
    T = nb * S
    sm = {n: a[n] for n, _ in _SMALL if n != "ssd_conv_w"}
    comm = _Comm(a, jnp.stack([chip, lax.axis_index("c")]).astype(jnp.int32))
    dx = _local_step(x.reshape(T, D), p.reshape(T, DPLE), loss_target.reshape(T, D), sm, comm, nb, 512)
    big, small_packs = comm.finish(dx)
    small, loss = _small_update(a, small_packs)
    g_out, delta, new_m, new_v = {}, {}, {}, {}
    for n in order:
        g_out[n], delta[n], new_m[n], new_v[n] = (r.reshape(a[n].shape) for r in (big[n] if n in big else small[n]))
    return (loss.reshape(()), dx.reshape(x.shape), *[g_out[n] for n in order], *[delta[n] for n in order],
            *[new_m[n] for n in order], *[new_v[n] for n in order])
```

```python
import jax
import jax.numpy as jnp
from jax import lax
from jax.experimental import pallas as pl
from jax.experimental.pallas import tpu as pltpu

F32 = jnp.float32
BF16 = jnp.bfloat16
MXU = jnp.bfloat16
GRAD = jnp.bfloat16

D = 1024
CH = 128
GM_HEADS = 8
SSD_HEADS = 16
SSD_P = 64
CONV_CH = 1536
N_MAIN = 4608
DFF = 4096
DPLE = 256
EPS = 1e-6
NEG = -1e30

LR, B1, B2, ADAM_EPS, WD, STEP = 0.001, 0.9, 0.999, 1e-08, 0.01, 10

VMEM_LIMIT = 56 * 1024 * 1024
_SEQS_PER_STEP = 4
MESH = pl.DeviceIdType.MESH

INV_SQRT2 = 0.7071067811865476
INV_SQRT_2PI = 0.3989422804014327


def _cp(n_axes=1):
    return pltpu.CompilerParams(dimension_semantics=("arbitrary",) * n_axes, vmem_limit_bytes=VMEM_LIMIT)


def _dot(a, b):
    return jnp.dot(a, b, preferred_element_type=F32)


def _dot_nt(a, b):
    return lax.dot_general(a, b, (((1,), (1,)), ((), ())), preferred_element_type=F32)


def _dot_tn(a, b):
    return lax.dot_general(a, b, (((0,), (0,)), ((), ())), preferred_element_type=F32)


def _dot_hi(a, b):
    return jnp.dot(a, b, preferred_element_type=F32, precision=lax.Precision.HIGHEST)


def _dot_01(a, sel):
    hi = a.astype(BF16)
    lo = (a - hi.astype(F32)).astype(BF16)
    n = a.shape[0]
    r = _dot(jnp.concatenate([hi, lo], axis=0), sel)
    return r[0:n] + r[n:2 * n]


def _rows(tm, n, j=0):
    return pl.BlockSpec((tm, n), lambda i: (i, j))


def _const(shape):
    nd = len(shape)
    return pl.BlockSpec(shape, lambda *_: (0,) * nd)


def _sds(shape, dtype):
    return jax.ShapeDtypeStruct(shape, dtype)


def _rms(x):
    r = lax.rsqrt(jnp.mean(x * x, axis=-1, keepdims=True) + EPS)
    return x * r, r


def _rms_bwd(dy, xhat, r, g):
    dyg = dy * g
    return r * (dyg - xhat * jnp.mean(dyg * xhat, axis=-1, keepdims=True))


def _sigmoid(x):
    return 1.0 / (1.0 + jnp.exp(-x))


def _gelu(x):
    cdf = 0.5 * (1.0 + lax.erf(x * INV_SQRT2))
    pdf = jnp.exp(-0.5 * x * x) * INV_SQRT_2PI
    return x * cdf, cdf + x * pdf


def _softplus(x):
    e = jnp.exp(-jnp.abs(x))
    u = 1.0 + e
    log1p = jnp.where(u == 1.0, e, jnp.log(u) * e / (u - 1.0))
    return jnp.maximum(x, 0.0) + log1p


def _after(n_in, fn):
    def body(*refs):
        return fn(*refs[:n_in], *refs[n_in + 1:])

    return body


def _inproj_gmlp(x, g, wm, wdt, gv, ws, bst, gout, tm, after):
    T = x.shape[0]

    def body(x_ref, g_ref, wm_ref, wdt_ref, gv_ref, ws_ref, bst_ref, gout_ref,
             z_ref, xbc_ref, dt_ref, xn_ref, ya_ref, uv_ref):
        xh, _ = _rms(x_ref[...])
        xn = (xh * g_ref[...]).astype(MXU)
        xn_ref[...] = xn
        for n in range(4):
            uv_ref[:, n * 512:(n + 1) * 512] = _dot(xn, wm_ref[:, n * 512:(n + 1) * 512])
        for n in range(2):
            z_ref[:, n * 512:(n + 1) * 512] = _dot(xn, wm_ref[:, 2048 + n * 512:2048 + (n + 1) * 512])
        for n in range(3):
            xbc_ref[:, n * 512:(n + 1) * 512] = _dot(xn, wm_ref[:, 3072 + n * 512:3072 + (n + 1) * 512])
        dt_ref[...] = _dot(xn, wdt_ref[...])
        for k in range(tm // CH):
            rows = slice(k * CH, (k + 1) * CH)
            f = _gmlp_fwd_vals(uv_ref[rows, 0:1024], uv_ref[rows, 1024:2048], gv_ref[...], ws_ref, bst_ref[...],
                               gout_ref[...])
            ya_ref[rows, :] = f["out"].astype(MXU)

    return pl.pallas_call(
        _after(8, body), grid=(T // tm,), name="inproj_gmlp",
        in_specs=[_rows(tm, D), _const((1, D)), _const((D, N_MAIN)), _const((D, 128)), _const((1, 1024)),
                  _const((GM_HEADS, CH, CH)), _const((CH, 128)), _const((1, 1024)), _ANY],
        out_specs=[_rows(tm, 1024), _rows(tm, CONV_CH), _rows(tm, 128), _rows(tm, D), _rows(tm, 1024, 0),
                   _rows(tm, 2048)],
        out_shape=[_sds((T, 1024), F32), _sds((T, CONV_CH), F32), _sds((T, 128), F32), _sds((T, D), MXU),
                   _sds((T, 2048), MXU), _sds((T, 2048), F32)],
        compiler_params=_cp(),
    )(x, g, wm, wdt, gv, ws, bst, gout, after)


def _gmlp_fwd_vals(u, v, gv, ws_ref, bst, gout):
    ug, dug = _gelu(u)
    vg, dvg = _gelu(v)
    row = lax.broadcasted_iota(jnp.int32, (CH, CH), 0)
    col = lax.broadcasted_iota(jnp.int32, (CH, CH), 1)
    tril = row >= col
    ys, heads = [], []
    for h in range(GM_HEADS):
        sl = slice(h * 128, (h + 1) * 128)
        vhat, rv = _rms(vg[:, sl])
        vn = (vhat * gv[:, sl]).astype(MXU)
        wt = jnp.where(tril, ws_ref[h], 0.0)
        mixed = _dot(wt.astype(MXU), vn) + bst[:, h:h + 1]
        ys.append(ug[:, sl] * mixed)
        heads.append((vhat, rv, vn, wt, mixed))
    y = jnp.concatenate(ys, axis=1)
    yhat, ry = _rms(y)
    return dict(ug=ug, dug=dug, dvg=dvg, heads=heads, yhat=yhat, ry=ry, tril=tril, out=yhat * gout)


def _shifts_down(cur, halo):
    row8 = lax.broadcasted_iota(jnp.int32, (8, cur.shape[1]), 0)
    out = [cur]
    for j in (1, 2, 3):
        sh = pltpu.roll(cur, j, 0)
        top = jnp.where(row8 < j, pltpu.roll(halo, j, 0), sh[0:8])
        out.append(jnp.concatenate([top, sh[8:]], axis=0))
    return out


def _shifts_up(cur, halo):
    row8 = lax.broadcasted_iota(jnp.int32, (8, cur.shape[1]), 0)
    out = []
    for j in (1, 2, 3):
        sh = pltpu.roll(cur, CH - j, 0)
        bot = jnp.where(row8 + j >= 8, pltpu.roll(halo, 8 - j, 0), sh[CH - 8:CH])
        out.append(jnp.concatenate([sh[0:CH - 8], bot], axis=0))
    return out


def _conv(xbc, halo, convw, convb):
    sh = _shifts_down(xbc, halo)
    return convb + convw[3:4] * sh[0] + convw[2:3] * sh[1] + convw[1:2] * sh[2] + convw[0:1] * sh[3]


def _ssd_fwd_vals(z, conv, dtraw, dtb, alog, dskip, ng, ex, ltri, s_prev):
    sig_c = _sigmoid(conv)
    xa = conv * sig_c
    xs = xa[:, :1024]
    bm = [xa[:, 1024:1152], xa[:, 1152:1280]]
    cm = [xa[:, 1280:1408], xa[:, 1408:1536]]
    dtpre = dtraw + dtb
    dt = _softplus(dtpre)
    a_neg = -jnp.exp(alog)
    cs = _dot_hi(ltri, dt * a_neg)
    cst = cs.T
    last = cs[CH - 1:CH]
    ecs = jnp.exp(cs)
    dec = jnp.exp(last - cs)
    spread = _dot_01(jnp.concatenate([dt, ecs, dec], axis=0), ex)
    dte, ecse, dece = spread[0:CH], spread[CH:2 * CH], spread[2 * CH:3 * CH]
    cde = ecse[CH - 1:CH]
    de = dskip
    xdt = xs * dte
    row = lax.broadcasted_iota(jnp.int32, (CH, CH), 0)
    col = lax.broadcasted_iota(jnp.int32, (CH, CH), 1)
    tril = row >= col
    lo = col < SSD_P
    bmb = [b.astype(MXU) for b in bm]
    cmb = [c.astype(MXU) for c in cm]
    mg = [_dot_nt(cmb[g], bmb[g]) for g in range(2)]
    yd, lms, whs = [], [], []
    for q in range(8):
        g = q // 4
        xq = xdt[:, q * 128:(q + 1) * 128]
        acc = None
        for hh in range(2):
            h = 2 * q + hh
            seg = cs[:, h:h + 1] - cst[h:h + 1, :]
            lm = jnp.exp(jnp.where(tril, seg, NEG))
            wh = (mg[g] * lm).astype(MXU)
            xm = jnp.where(lo if hh == 0 else ~lo, xq, 0.0).astype(MXU)
            part = _dot(wh, xm)
            acc = part if acc is None else acc + part
            lms.append(lm)
            whs.append(wh)
        yd.append(acc)
    yd = jnp.concatenate(yd, axis=1)
    sb = s_prev.astype(MXU)
    yo = jnp.concatenate([_dot(cmb[g], sb[:, g * 512:(g + 1) * 512]) for g in range(2)], axis=1) * ecse
    xdec = (xdt * dece).astype(MXU)
    states = jnp.concatenate([_dot_tn(bmb[g], xdec[:, g * 512:(g + 1) * 512]) for g in range(2)], axis=1)
    s_next = s_prev * cde + states
    ypre = yd + yo + de * xs
    sig_z = _sigmoid(z)
    yg = ypre * z * sig_z
    outs, yhat, rr = [], [], []
    for g in range(2):
        sl = slice(g * 512, (g + 1) * 512)
        yh, r = _rms(yg[:, sl])
        yhat.append(yh)
        rr.append(r)
        outs.append(yh * ng[:, sl])
    return dict(sig_c=sig_c, xa=xa, xs=xs, bmb=bmb, cmb=cmb, dtpre=dtpre, dt=dt, a_neg=a_neg,
                cs=cs, last=last, ecs=ecs, dec=dec, dte=dte, ecse=ecse, dece=dece, cde=cde, de=de, xdt=xdt,
                mg=mg, lms=lms, whs=whs, lo=lo, yo=yo, sb=sb, xdec=xdec, s_next=s_next, ypre=ypre, sig_z=sig_z,
                yhat=yhat, rr=rr, out=jnp.concatenate(outs, axis=1))


def _ssd_fwd(pz, pxbc, dtraw, cat, convw, convb, dtb, alog, dskip, ng, ex, ltri, nb):
    T = pz.shape[0]
    S = T // nb
    nch = S // CH
    ns = _SEQS_PER_STEP if nb % _SEQS_PER_STEP == 0 else 1

    def body(z_ref, xbc_ref, halo_ref, dt_ref, cw_ref, cb_ref, dtb_ref, al_ref, ds_ref, ng_ref, ex_ref, lt_ref,
             cat_in_ref, yb_ref, sall_ref, conv_ref, s_ref):
        del cat_in_ref
        c = pl.program_id(1)

        @pl.when(c == 0)
        def _():
            s_ref[...] = jnp.zeros_like(s_ref)

        for i in range(ns):
            halo = jnp.where(c == 0, 0.0, halo_ref[i])
            s_prev = s_ref[i]
            sall_ref[i, 0] = s_prev
            conv = _conv(xbc_ref[i], halo, cw_ref[...], cb_ref[...])
            conv_ref[i] = conv
            f = _ssd_fwd_vals(z_ref[i], conv, dt_ref[i], dtb_ref[...], al_ref[...], ds_ref[...], ng_ref[...],
                              ex_ref[...], lt_ref[...], s_prev)
            s_ref[i] = f["s_next"]
            yb_ref[i] = f["out"].astype(MXU)

    def seq(width, col=0):
        return pl.BlockSpec((ns, CH, width), lambda b, c: (b, c, col))

    cat, sall, conv = pl.pallas_call(
        body, grid=(nb // ns, nch), name="ssd_fwd",
        in_specs=[seq(1024), seq(CONV_CH),
                  pl.BlockSpec((ns, 8, CONV_CH), lambda b, c: (b, jnp.maximum(c * (CH // 8) - 1, 0), 0)),
                  seq(128),
                  _const((8, CONV_CH)), _const((1, CONV_CH)), _const((1, 128)), _const((1, 128)), _const((1, 1024)),
                  _const((1, 1024)), _const((128, 1024)), _const((CH, CH)), _ANY],
        out_specs=[seq(1024, 1), pl.BlockSpec((ns, 1, 128, 1024), lambda b, c: (b, c, 0, 0)), seq(CONV_CH)],
        out_shape=[_sds((nb, S, 2048), MXU), _sds((nb, nch, 128, 1024), F32), _sds((nb, S, CONV_CH), F32)],
        scratch_shapes=[pltpu.VMEM((ns, 128, 1024), F32)],
        input_output_aliases={12: 0},
        compiler_params=_cp(2),
    )(pz.reshape(nb, S, 1024), pxbc.reshape(nb, S, CONV_CH), pxbc.reshape(nb, S, CONV_CH), dtraw.reshape(nb, S, 128),
      convw, convb, dtb, alog, dskip, ng, ex, ltri, cat.reshape(nb, S, 2048))
    return cat.reshape(T, 2048), sall, conv.reshape(T, CONV_CH)


def _outproj_ff1(cat, wo, x, g, w1, tm):
    T = x.shape[0]

    def body(cat_ref, wo_ref, x_ref, g_ref, w1_ref, h1_ref, hn_ref, hid_ref):
        h1 = x_ref[...] + _dot(cat_ref[...], wo_ref[...])
        h1_ref[...] = h1
        hn = (_rms(h1)[0] * g_ref[...]).astype(MXU)
        hn_ref[...] = hn
        for n in range(4):
            hid_ref[:, n * 1024:(n + 1) * 1024] = jnp.maximum(_dot(hn, w1_ref[n]), 0.0).astype(MXU)

    return pl.pallas_call(
        body, grid=(T // tm,), name="outproj_ff1",
        in_specs=[_rows(tm, 2048), _const((2048, D)), _rows(tm, D), _const((1, D)), _const((4, D, 1024))],
        out_specs=[_rows(tm, D), _rows(tm, D), _rows(tm, DFF)],
        out_shape=[_sds((T, D), F32), _sds((T, D), MXU), _sds((T, DFF), MXU)],
        compiler_params=_cp(),
    )(cat, wo, x, g, w1)


def _sq(hid):
    h = hid.astype(F32)
    return (h * h).astype(MXU)


def _ff2_tail(hid, w2, h1, g_ple, p, tgt, wg, wp, gf, tm):
    T = h1.shape[0]

    def body(hid_ref, w2_ref, h1_ref, g_ref, p_ref, t_ref, wg_ref, wp_ref, gf_ref,
             hp_ref, dgl_ref, dpe_ref, dh2_ref, dh2b_ref, loss_ref, dgf_ref, dg_ref):
        @pl.when(pl.program_id(0) == 0)
        def _():
            loss_ref[...] = jnp.zeros_like(loss_ref)
            dgf_ref[...] = jnp.zeros_like(dgf_ref)
            dg_ref[...] = jnp.zeros_like(dg_ref)

        h2 = h1_ref[...] + _dot(_sq(hid_ref[...]), w2_ref[...])
        h2h, r2 = _rms(h2)
        g_ple = g_ref[...]
        hp = (h2h * g_ple).astype(MXU)
        hp_ref[...] = hp
        gate = _sigmoid(_dot(hp, wg_ref[...]))
        pb = p_ref[...].astype(MXU)
        pe = jnp.concatenate([_dot(pb, wp_ref[k]) for k in range(4)], axis=1)
        h3 = h2 + gate * pe
        hh, r = _rms(h3)
        gf = gf_ref[...]
        diff = hh * gf - t_ref[...]
        loss_ref[...] += 0.5 * jnp.sum(jnp.mean(diff * diff, axis=-1, keepdims=True))
        dout = diff * (1.0 / D)
        dgf_ref[...] += jnp.sum(dout * hh, axis=0, keepdims=True)
        dh3 = _rms_bwd(dout, hh, r, gf)
        dgl = (dh3 * pe * gate * (1.0 - gate)).astype(MXU)
        dgl_ref[...] = dgl
        dpe_ref[...] = (dh3 * gate).astype(MXU)
        dhp = _dot_nt(dgl, wg_ref[...])
        dg_ref[...] += jnp.sum(dhp * h2h, axis=0, keepdims=True)
        dh2 = dh3 + _rms_bwd(dhp, h2h, r2, g_ple)
        dh2_ref[...] = dh2
        dh2b_ref[...] = dh2.astype(MXU)

    return pl.pallas_call(
        body, grid=(T // tm,), name="ff2_tail",
        in_specs=[_rows(tm, DFF), _const((DFF, D)), _rows(tm, D), _const((1, D)), _rows(tm, DPLE), _rows(tm, D),
                  _const((D, D)), _const((4, DPLE, 256)), _const((1, D))],
        out_specs=[_rows(tm, D), _rows(tm, D), _rows(tm, D), _rows(tm, D), _rows(tm, D), _const((8, 128)),
                   _const((1, D)), _const((1, D))],
        out_shape=[_sds((T, D), MXU), _sds((T, D), MXU), _sds((T, D), MXU), _sds((T, D), F32), _sds((T, D), MXU),
                   _sds((8, 128), F32), _sds((1, D), F32), _sds((1, D), F32)],
        compiler_params=_cp(),
    )(hid, w2, h1, g_ple, p, tgt, wg, wp, gf)


def _ff2_bwd(dh2b, w2, hid, tm):
    T = hid.shape[0]

    def body(dh2b_ref, w2_ref, hid_ref, dpre_ref):
        d = dh2b_ref[...]
        for n in range(DFF // 1024):
            sl = slice(n * 1024, (n + 1) * 1024)
            da = _dot_nt(d, w2_ref[sl, :])
            dpre_ref[:, sl] = (2.0 * da * hid_ref[:, sl].astype(F32)).astype(MXU)

    return pl.pallas_call(
        body, grid=(T // tm,), name="ff2_bwd",
        in_specs=[_rows(tm, D), _const((DFF, D)), _rows(tm, DFF)],
        out_specs=_rows(tm, DFF),
        out_shape=_sds((T, DFF), MXU),
        compiler_params=_cp(),
    )(dh2b, w2, hid)


def _ff1_bwd(dpre, w1, dh2, h1, g, wo, tm):
    T = h1.shape[0]

    def body(dpre_ref, w1_ref, dh2_ref, h1_ref, g_ref, wo_ref, dh1_ref, dh1b_ref, dg_ref, dcat_ref):
        @pl.when(pl.program_id(0) == 0)
        def _():
            dg_ref[...] = jnp.zeros_like(dg_ref)

        dhn = _dot_nt(dpre_ref[:, 0:1024], w1_ref[0])
        for k in range(1, 4):
            dhn = dhn + _dot_nt(dpre_ref[:, k * 1024:(k + 1) * 1024], w1_ref[k])
        hh, r = _rms(h1_ref[...])
        dg_ref[...] += jnp.sum(dhn * hh, axis=0, keepdims=True)
        dh1 = dh2_ref[...] + _rms_bwd(dhn, hh, r, g_ref[...])
        dh1_ref[...] = dh1
        dh1b = dh1.astype(MXU)
        dh1b_ref[...] = dh1b
        dcat_ref[:, 0:1024] = _dot_nt(dh1b, wo_ref[0:1024, :])
        dcat_ref[:, 1024:2048] = _dot_nt(dh1b, wo_ref[1024:2048, :])

    return pl.pallas_call(
        body, grid=(T // tm,), name="ff1_bwd",
        in_specs=[_rows(tm, DFF), _const((4, D, 1024)), _rows(tm, D), _rows(tm, D), _const((1, D)),
                  _const((2048, D))],
        out_specs=[_rows(tm, D), _rows(tm, D), _const((1, D)), _rows(tm, 2048)],
        out_shape=[_sds((T, D), F32), _sds((T, D), MXU), _sds((1, D), F32), _sds((T, 2048), F32)],
        compiler_params=_cp(),
    )(dpre, w1, dh2, h1, g, wo)


def _gmlp_bwd(uv, dcat, gv, ws, bst, gout, wm):
    T = uv.shape[0]
    nck = 4 if T % (4 * CH) == 0 else 1
    tb = nck * CH

    def body(uv_ref, dya_ref, gv_ref, ws_ref, bst_ref, gout_ref, wuv_ref, duv_ref, dgv_ref, dws_ref, dbst_ref,
             dgo_ref, dxn_ref):
        @pl.when(pl.program_id(0) == 0)
        def _():
            dgv_ref[...] = jnp.zeros_like(dgv_ref)
            dws_ref[...] = jnp.zeros_like(dws_ref)
            dbst_ref[...] = jnp.zeros_like(dbst_ref)
            dgo_ref[...] = jnp.zeros_like(dgo_ref)

        for k in range(nck):
            chunk(slice(k * CH, (k + 1) * CH), uv_ref, dya_ref, gv_ref, ws_ref, bst_ref, gout_ref, duv_ref,
                  dgv_ref, dws_ref, dbst_ref, dgo_ref)
        dxn_ref[...] = _dot_nt(duv_ref[...], wuv_ref[...])

    def chunk(rows, uv_ref, dya_ref, gv_ref, ws_ref, bst_ref, gout_ref, duv_ref, dgv_ref, dws_ref, dbst_ref,
              dgo_ref):
        gv = gv_ref[...]
        f = _gmlp_fwd_vals(uv_ref[rows, 0:1024], uv_ref[rows, 1024:2048], gv, ws_ref, bst_ref[...], gout_ref[...])
        dya = dya_ref[rows, :]
        dgo_ref[...] += jnp.sum(dya * f["yhat"], axis=0, keepdims=True)
        dy = _rms_bwd(dya, f["yhat"], f["ry"], gout_ref[...])
        lane = lax.broadcasted_iota(jnp.int32, (CH, 128), 1)
        dbs = jnp.zeros((CH, 128), F32)
        dug, dvg, dgvs = [], [], []
        for h in range(GM_HEADS):
            sl = slice(h * 128, (h + 1) * 128)
            vhat, rv, vn, wt, mixed = f["heads"][h]
            dyh = dy[:, sl]
            dug.append(dyh * mixed)
            dmixed = dyh * f["ug"][:, sl]
            dmb = dmixed.astype(MXU)
            dws_ref[h] += jnp.where(f["tril"], _dot_nt(dmb, vn), 0.0)
            dbs = dbs + jnp.where(lane == h, jnp.sum(dmixed, axis=1, keepdims=True), 0.0)
            dvn = _dot_tn(wt.astype(MXU), dmb)
            dgvs.append(jnp.sum(dvn * vhat, axis=0, keepdims=True))
            dvg.append(_rms_bwd(dvn, vhat, rv, gv[:, sl]))
        dbst_ref[...] += dbs
        dgv_ref[...] += jnp.concatenate(dgvs, axis=1)
        duv_ref[rows, 0:1024] = (jnp.concatenate(dug, axis=1) * f["dug"]).astype(MXU)
        duv_ref[rows, 1024:2048] = (jnp.concatenate(dvg, axis=1) * f["dvg"]).astype(MXU)

    return pl.pallas_call(
        body, grid=(T // tb,), name="gmlp_bwd",
        in_specs=[_rows(tb, 2048), _rows(tb, 1024, 0), _const((1, 1024)),
                  _const((GM_HEADS, CH, CH)), _const((CH, 128)), _const((1, 1024)), _const((D, 2048))],
        out_specs=[_rows(tb, 2048), _const((1, 1024)), _const((GM_HEADS, CH, CH)), _const((CH, 128)),
                   _const((1, 1024)), _rows(tb, D)],
        out_shape=[_sds((T, 2048), MXU), _sds((1, 1024), F32), _sds((GM_HEADS, CH, CH), F32), _sds((CH, 128), F32),
                   _sds((1, 1024), F32), _sds((T, D), F32)],
        compiler_params=_cp(),
    )(uv, dcat, gv, ws, bst, gout, wm)


def _ssd_bwd(pz, pxbc, conv, dtraw, sall, dcat, convw, dtb, alog, dskip, ng, ex, ltri, ext, nb, after):
    T = pz.shape[0]
    S = T // nb
    nch = S // CH
    ns = _SEQS_PER_STEP if nb % _SEQS_PER_STEP == 0 else 1

    def seq(width, col=0):
        return pl.BlockSpec((ns, CH, width), lambda b, c: (b, nch - 1 - c, col))

    in_specs = [
        seq(1024), seq(CONV_CH), seq(CONV_CH), seq(128),
        _const((8, CONV_CH)), _const((1, 128)), _const((1, 128)), _const((1, 1024)),
        _const((1, 1024)), _const((128, 1024)), _const((CH, CH)),
        _const((1024, 128)),
        pl.BlockSpec((ns, 1, 128, 1024), lambda b, c: (b, nch - 1 - c, 0, 0)),
        seq(1024, 1),
        _ANY,
    ]

    def body(z_ref, xbc_ref, conv_ref, dt_ref, cw_ref, dtb_ref, al_ref, ds_ref, ng_ref, ex_ref, lt_ref,
             ext_ref, sall_ref, dyb_ref,
             dssd_ref, ddt_ref, dcw_ref, dcb_ref, ddtb_ref, dal_ref, dds_ref, dng_ref,
             dst_ref, dnext_ref, ddse_ref):
        b = pl.program_id(0)
        c = pl.program_id(1)

        @pl.when((b == 0) & (c == 0))
        def _():
            for r in (dcw_ref, dcb_ref, ddtb_ref, dal_ref, dds_ref, dng_ref, ddse_ref):
                r[...] = jnp.zeros_like(r)

        @pl.when(c == 0)
        def _():
            dst_ref[...] = jnp.zeros_like(dst_ref)
            dnext_ref[...] = jnp.zeros_like(dnext_ref)

        ex = ex_ref[...]
        ext = ext_ref[...]
        cw = cw_ref[...]
        ng = ng_ref[...]
        for i in range(ns):
            one_chunk(i, ex, ext, cw, ng, z_ref, xbc_ref, conv_ref, dt_ref, dtb_ref, al_ref, ds_ref, lt_ref, sall_ref,
                      dyb_ref, dssd_ref, ddt_ref, dcw_ref, dcb_ref, ddtb_ref, dal_ref, dng_ref, dst_ref, dnext_ref,
                      ddse_ref)

        @pl.when((b == nb // ns - 1) & (c == nch - 1))
        def _():
            dds_ref[...] = _dot_01(jnp.broadcast_to(ddse_ref[...], (8, 1024)), ext)[0:1]

    def one_chunk(i, ex, ext, cw, ng, z_ref, xbc_ref, conv_ref, dt_ref, dtb_ref, al_ref, ds_ref, lt_ref, sall_ref,
                  dyb_ref, dssd_ref, ddt_ref, dcw_ref, dcb_ref, ddtb_ref, dal_ref, dng_ref, dst_ref, dnext_ref,
                  ddse_ref):
        z = z_ref[i]
        s_prev = sall_ref[i, 0]
        conv = conv_ref[i]
        f = _ssd_fwd_vals(z, conv, dt_ref[i], dtb_ref[...], al_ref[...], ds_ref[...], ng, ex, lt_ref[...], s_prev)
        xs, xdt, cs, dec, dt = f["xs"], f["xdt"], f["cs"], f["dec"], f["dt"]
        dyb = dyb_ref[i]
        dyg, dngs = [], []
        for g in range(2):
            sl = slice(g * 512, (g + 1) * 512)
            dngs.append(jnp.sum(dyb[:, sl] * f["yhat"][g], axis=0, keepdims=True))
            dyg.append(_rms_bwd(dyb[:, sl], f["yhat"][g], f["rr"][g], ng[:, sl]))
        dng_ref[...] += jnp.concatenate(dngs, axis=1)
        dyg = jnp.concatenate(dyg, axis=1)
        sig_z = f["sig_z"]
        silu_z = z * sig_z
        dy = dyg * silu_z
        dz = dyg * f["ypre"] * (sig_z + silu_z * (1.0 - sig_z))
        ddse_ref[...] += jnp.sum(dy * xs, axis=0, keepdims=True)
        dxs = dy * f["de"]
        dye = dy * f["ecse"]
        dyeb = dye.astype(MXU)
        dst = dst_ref[i]
        dstb = dst.astype(MXU)
        bmb, cmb, sb, xdec = f["bmb"], f["cmb"], f["sb"], f["xdec"]
        u = jnp.concatenate([_dot(bmb[g], dstb[:, g * 512:(g + 1) * 512]) for g in range(2)], axis=1)
        dxdt = [u[:, q * 128:(q + 1) * 128] * f["dece"][:, q * 128:(q + 1) * 128] for q in range(8)]
        per_head = _dot_01(jnp.concatenate(
            [dy * f["yo"], u * xdt, jnp.broadcast_to(jnp.sum(dst * s_prev, axis=0, keepdims=True), (8, 1024))],
            axis=0), ext)
        dcs = per_head[0:CH]
        t = per_head[CH:2 * CH] * dec
        dcd = per_head[2 * CH:2 * CH + 1]
        row = lax.broadcasted_iota(jnp.int32, (CH, 128), 0)
        lane = lax.broadcasted_iota(jnp.int32, (CH, 128), 1)
        cd = jnp.exp(f["last"])
        dcs = dcs - t + jnp.where(row == CH - 1, jnp.sum(t, axis=0, keepdims=True) + dcd * cd, 0.0)
        dcst = jnp.zeros((128, CH), F32)
        lo = f["lo"]
        dbm, dcm, ds_prev = [], [], []
        for g in range(2):
            sl = slice(g * 512, (g + 1) * 512)
            dmg = jnp.zeros((CH, CH), F32)
            for q in range(4 * g, 4 * g + 4):
                dyq = dy[:, q * 128:(q + 1) * 128]
                xq = xdt[:, q * 128:(q + 1) * 128].astype(MXU)
                for hh in range(2):
                    h = 2 * q + hh
                    m = lo if hh == 0 else ~lo
                    dym = jnp.where(m, dyq, 0.0).astype(MXU)
                    gh = _dot_nt(dym, xq)
                    gl = gh * f["lms"][h]
                    dmg = dmg + gl
                    qh = gl * f["mg"][g]
                    dcs = dcs + jnp.where(lane == h, jnp.sum(qh, axis=1, keepdims=True), 0.0)
                    dcst = dcst - jnp.where(row == h, jnp.sum(qh, axis=0, keepdims=True), 0.0)
                    dxdt[q] = dxdt[q] + _dot_tn(f["whs"][h], dym)
            dmgb = dmg.astype(MXU)
            dcm.append(_dot(dmgb, bmb[g]) + _dot_nt(dyeb[:, sl], sb[:, sl]))
            dbm.append(_dot_tn(dmgb, cmb[g]) + _dot_nt(xdec[:, sl], dstb[:, sl]))
            ds_prev.append(_dot_tn(cmb[g], dyeb[:, sl]))
        dst_ref[i] = jnp.concatenate(ds_prev, axis=1) + dst * f["cde"]
        dcs = dcs + dcst.T
        da = _dot_hi(lt_ref[...].T, dcs)
        dxdt = jnp.concatenate(dxdt, axis=1)
        a_neg = f["a_neg"]
        ddt = da * a_neg + _dot_01(dxdt * xs, ext)
        dal_ref[...] += jnp.sum(da * dt, axis=0, keepdims=True) * a_neg
        dxs = dxs + dxdt * f["dte"]
        ddtraw = jnp.where(lane < SSD_HEADS, ddt * _sigmoid(f["dtpre"]), 0.0)
        ddtb_ref[...] += jnp.sum(ddtraw, axis=0, keepdims=True)
        ddt_ref[i] = ddtraw.astype(MXU)
        dxa = jnp.concatenate([dxs, dbm[0], dbm[1], dcm[0], dcm[1]], axis=1)
        sig_c = f["sig_c"]
        dconv = dxa * (sig_c + f["xa"] * (1.0 - sig_c))
        dcb_ref[...] += jnp.sum(dconv, axis=0, keepdims=True)
        xbc = xbc_ref[i]
        dcw_ref[3:4, :] += jnp.sum(dconv * xbc, axis=0, keepdims=True)
        dxbc = cw[3:4] * dconv
        for j, up in zip((1, 2, 3), _shifts_up(dconv, dnext_ref[i])):
            dcw_ref[3 - j:4 - j, :] += jnp.sum(up * xbc, axis=0, keepdims=True)
            dxbc = dxbc + cw[3 - j:4 - j] * up
        dnext_ref[i] = dconv[0:8]
        dssd_ref[i, :, 0:1024] = dz.astype(MXU)
        dssd_ref[i, :, 1024:2560] = dxbc.astype(MXU)

    dssd, ddt, *small = pl.pallas_call(
        _after(14, body), grid=(nb // ns, nch), name="ssd_bwd",
        in_specs=in_specs,
        out_specs=[seq(2560), seq(128),
                   _const((8, CONV_CH)), _const((1, CONV_CH)), _const((1, 128)), _const((1, 128)), _const((1, 128)),
                   _const((1, 1024))],
        out_shape=[_sds((nb, S, 2560), MXU), _sds((nb, S, 128), MXU), _sds((8, CONV_CH), F32),
                   _sds((1, CONV_CH), F32), _sds((1, 128), F32), _sds((1, 128), F32), _sds((1, 128), F32),
                   _sds((1, 1024), F32)],
        scratch_shapes=[pltpu.VMEM((ns, 128, 1024), F32), pltpu.VMEM((ns, 8, CONV_CH), F32),
                        pltpu.VMEM((1, 1024), F32)],
        compiler_params=_cp(2),
    )(pz.reshape(nb, S, 1024), pxbc.reshape(nb, S, CONV_CH), conv.reshape(nb, S, CONV_CH), dtraw.reshape(nb, S, 128),
      convw, dtb, alog, dskip, ng, ex, ltri, ext, sall, dcat.reshape(nb, S, 2048), after)
    return (dssd.reshape(T, 2560), ddt.reshape(T, 128), *small)


def _inproj_bwd(dxn_uv, dssd, ddt, wm, wdt, dh1, x, g, tm, after):
    T = x.shape[0]

    def body(dxnuv_ref, dssd_ref, ddt_ref, wm_ref, wdt_ref, dh1_ref, x_ref, g_ref, dx_ref, dg_ref):
        @pl.when(pl.program_id(0) == 0)
        def _():
            dg_ref[...] = jnp.zeros_like(dg_ref)

        dxn = (dxnuv_ref[...] + _dot_nt(dssd_ref[...], wm_ref[:, 2048:N_MAIN])
               + _dot_nt(ddt_ref[...], wdt_ref[...]))
        xh, r = _rms(x_ref[...])
        dg_ref[...] += jnp.sum(dxn * xh, axis=0, keepdims=True)
        dx_ref[...] = dh1_ref[...] + _rms_bwd(dxn, xh, r, g_ref[...])

    return pl.pallas_call(
        _after(8, body), grid=(T // tm,), name="inproj_bwd",
        in_specs=[_rows(tm, D), _rows(tm, 2560), _rows(tm, 128), _const((D, N_MAIN)), _const((D, 128)),
                  _rows(tm, D), _rows(tm, D), _const((1, D)), _ANY],
        out_specs=[_rows(tm, D), _const((1, D))],
        out_shape=[_sds((T, D), F32), _sds((1, D), F32)],
        compiler_params=_cp(),
    )(dxn_uv, dssd, ddt, wm, wdt, dh1, x, g, after)


def _matmul_tn(a, b, name, a_fn=None):
    T, M = a.shape
    N = b.shape[1]
    tm = min(M, 1024)
    tn = 1280 if N == 2560 else min(N, 1024)
    tk = min(T, 2048)

    def body(a_ref, b_ref, o_ref, acc_ref):
        k = pl.program_id(2)

        @pl.when(k == 0)
        def _():
            acc_ref[...] = jnp.zeros_like(acc_ref)

        av = a_ref[...]
        if a_fn is not None:
            av = a_fn(av)
        acc_ref[...] += _dot_tn(av, b_ref[...])

        @pl.when(k == T // tk - 1)
        def _():
            o_ref[...] = acc_ref[...].astype(o_ref.dtype)

    return pl.pallas_call(
        body, grid=(M // tm, N // tn, T // tk), name=name,
        in_specs=[pl.BlockSpec((tk, tm), lambda i, j, k: (k, i)), pl.BlockSpec((tk, tn), lambda i, j, k: (k, j))],
        out_specs=pl.BlockSpec((tm, tn), lambda i, j, k: (i, j)),
        out_shape=_sds((M, N), GRAD),
        scratch_shapes=[pltpu.VMEM((tm, tn), F32)],
        compiler_params=_cp(3),
    )(a, b)


def _adamw_vals(w, g, m, v):
    m = B1 * m + (1.0 - B1) * g
    v = B2 * v + (1.0 - B2) * (g * g)
    m_hat = m / (1.0 - B1 ** STEP)
    v_hat = v / (1.0 - B2 ** STEP)
    return -LR * (m_hat / (jnp.sqrt(v_hat) + ADAM_EPS) + WD * w), m, v


_PARTS = 4


def _adamw_halves(items, kh, name):
    n = len(items)

    def body(kh_ref, *refs):
        mine = (pl.program_id(0) // _PARTS) == kh_ref[1]
        for k in range(n):
            w_ref, own_ref, oth_ref, m_ref, v_ref = refs[5 * k:5 * k + 5]
            g_ref, d_ref, mo_ref, vo_ref = refs[5 * n + 4 * k:5 * n + 4 * k + 4]
            g = jnp.where(mine, own_ref[...], oth_ref[...])
            g_ref[...] = g
            d_ref[...], mo_ref[...], vo_ref[...] = _adamw_vals(w_ref[...], g, m_ref[...], v_ref[...])

    def early(i):
        return jnp.minimum(i, _PARTS - 1)

    def late(i):
        return jnp.maximum(i - _PARTS, 0)

    in_specs, out_specs, out_shape = [], [], []
    for w, *_ in items:
        R, C = w.shape
        tr = R // (2 * _PARTS)
        full = pl.BlockSpec((tr, C), lambda i, kh: (i, 0))
        own = pl.BlockSpec((tr, C), lambda i, kh: (jnp.where(kh[1] == 0, early(i), late(i)), 0))
        oth = pl.BlockSpec((tr, C), lambda i, kh: (jnp.where(kh[1] == 0, late(i), early(i)), 0))
        in_specs += [full, own, oth, full, full]
        out_specs += [full] * 4
        out_shape += [_sds((R, C), F32)] * 4
    res = pl.pallas_call(
        body, name=name,
        grid_spec=pltpu.PrefetchScalarGridSpec(
            num_scalar_prefetch=1, grid=(2 * _PARTS,), in_specs=in_specs, out_specs=out_specs),
        out_shape=out_shape,
        compiler_params=_cp(),
    )(kh, *[a for item in items for a in item])
    return [tuple(res[4 * k:4 * k + 4]) for k in range(n)]


_TJ = 128


def _adamw_transposed(w, own, other, m, v, name):
    C, _, R = w.shape

    def body(w_ref, own_ref, oth_ref, m_ref, v_ref, g_ref, d_ref, mo_ref, vo_ref):
        first = lax.axis_index("c") == 0
        g = jnp.concatenate([jnp.where(first, own_ref[...], oth_ref[...]),
                             jnp.where(first, oth_ref[...], own_ref[...])], axis=0).T
        d, mo, vo = _adamw_vals(w_ref[:, 0, :], g, m_ref[:, 0, :], v_ref[:, 0, :])
        for ref, val in ((g_ref, g), (d_ref, d), (mo_ref, mo), (vo_ref, vo)):
            ref[:, 0, :] = val

    cols = pl.BlockSpec((_TJ, 1, R), lambda j: (j, 0, 0))
    half = pl.BlockSpec((R // 2, _TJ), lambda j: (0, j))
    return pl.pallas_call(
        body, grid=(pl.cdiv(C, _TJ),), name=name,
        in_specs=[cols, half, half, cols, cols], out_specs=[cols] * 4, out_shape=[_sds((C, 1, R), F32)] * 4,
        compiler_params=_cp(),
    )(w, own, other, m, v)


def _lanes(rows):
    return jnp.concatenate([rows[i:i + 1, :] for i in range(rows.shape[0])], axis=1)


def _small_update(a, packs):
    names = [n for n, _ in _SMALL]
    where = {}
    for k, (pnames, rows, _) in enumerate(packs):
        o = 0
        for n, r in zip(pnames, rows):
            where[n] = (k, o, r)
            o += r
    view = {n: (1, 1024) for n in names}
    view.update(gm_ws=(1024, 128), gm_bs=(8, 128), ssd_conv_w=(4, 384), ssd_conv_b=(1, CONV_CH),
                ssd_dt_bias=(1, 16), ssd_a_log=(1, 16), ssd_d=(1, 16))
    npk = len(packs)

    def body(*refs):
        tots = []
        for k in range(npk):
            tot = refs[k][0]
            for d in range(1, 8):
                tot = tot + refs[k][d]
            tots.append(tot)
        ins, outs = refs[npk:npk + 3 * len(names)], refs[npk + 3 * len(names):]
        chip = 2 * lax.axis_index("x") + lax.axis_index("y")
        for i, n in enumerate(names):
            k, o, r = where[n]
            blk = tots[k][o:o + r, :]
            if n == "gm_ws":
                g = blk
            elif n == "gm_bs":
                g = blk[0:8]
            elif view[n] == (1, 16):
                g = blk[0:1, 0:16]
            elif n == "ssd_conv_w":
                taps = jnp.concatenate([_lanes(blk[12 * t:12 * t + 12]) for t in range(4)], axis=0)
                g = taps[:, 0:384]
                for c in range(1, 4):
                    g = jnp.where(chip == c, taps[:, 384 * c:384 * (c + 1)], g)
            else:
                g = _lanes(blk[0:view[n][1] // 128])
            d, mo, vo = _adamw_vals(ins[3 * i][...], g, ins[3 * i + 1][...], ins[3 * i + 2][...])
            for j, val in enumerate((g, d, mo, vo)):
                outs[4 * i + j][...] = val
        k, o, _ = where["loss"]
        outs[-1][...] = tots[k][o:o + 1, 0:1]

    ins = [a[pre + n].reshape(view[n]) for n in names for pre in ("", "m_", "v_")]
    res = pl.pallas_call(
        body, name="small_update",
        out_shape=[_sds(view[n], F32) for n in names for _ in range(4)] + [_sds((1, 1), F32)],
    )(*[slots for _, _, slots in packs], *ins)
    return {n: tuple(r.reshape(a[n].shape) for r in res[4 * i:4 * i + 4]) for i, n in enumerate(names)}, res[-1]


def _sum_slots(items, kh, name):
    n = len(items)
    in_specs, out_specs, out_shape = [], [], []
    for slots, src, kind, (R, C) in items:
        tr = R // (2 * _PARTS)
        if kind == "slab":
            src_spec = pl.BlockSpec((1, tr, C), lambda i, kh: (kh[0], kh[1] * _PARTS + i, 0))
        elif kind == "rows":
            src_spec = pl.BlockSpec((tr, C), lambda i, kh: (kh[0] * (2 * _PARTS) + kh[1] * _PARTS + i, 0))
        else:
            src_spec = pl.BlockSpec((tr, C), lambda i, kh: (kh[1] * _PARTS + i, kh[0]))
        in_specs += [pl.BlockSpec((8, tr, C), lambda i, kh: (0, i, 0)), src_spec]
        out_specs.append(pl.BlockSpec((tr, C), lambda i, kh: (i, 0)))
        out_shape.append(_sds((R // 2, C), F32))

    def body(kh_ref, *refs):
        me = 2 * kh_ref[0] + kh_ref[1]
        for k, (_, _, kind, _) in enumerate(items):
            s_ref, own_ref, o_ref = refs[2 * k], refs[2 * k + 1], refs[2 * n + k]
            acc = (own_ref[0] if kind == "slab" else own_ref[...]).astype(F32)
            for j in range(1, 8):
                acc = acc + s_ref[me ^ j].astype(F32)
            o_ref[...] = acc

    return pl.pallas_call(
        body, name=name,
        grid_spec=pltpu.PrefetchScalarGridSpec(
            num_scalar_prefetch=1, grid=(_PARTS,), in_specs=in_specs, out_specs=out_specs),
        out_shape=out_shape,
        compiler_params=_cp(),
    )(kh, *[a for slots, src, _, _ in items for a in (slots, src)])


def _assemble_w_in(slabs):
    tr = 256

    def body(s_ref, wm_ref, wdt_ref):
        full = jnp.concatenate([s_ref[k] for k in range(4)], axis=1)
        wm_ref[...] = full[:, :N_MAIN]
        wdt_ref[...] = jnp.concatenate([full[:, N_MAIN:], jnp.zeros((tr, 128 - 16), full.dtype)], axis=1)

    return pl.pallas_call(
        body, grid=(D // tr,), name="assemble_w_in",
        in_specs=[pl.BlockSpec((4, tr, 1156), lambda i: (0, i, 0))],
        out_specs=[_rows(tr, N_MAIN), _rows(tr, 128)],
        out_shape=[_sds((D, N_MAIN), slabs.dtype), _sds((D, 128), slabs.dtype)],
        compiler_params=_cp(),
    )(slabs)


def _split_dw_in(d_uv, d_ssd, d_dt):
    tr = 256

    def body(uv_ref, ssd_ref, dt_ref, o_ref):
        full = jnp.concatenate([uv_ref[...], ssd_ref[...], dt_ref[:, 0:16]], axis=1)
        for k in range(4):
            o_ref[k] = full[:, 1156 * k:1156 * (k + 1)]

    return pl.pallas_call(
        body, grid=(D // tr,), name="split_dw_in",
        in_specs=[_rows(tr, 2048), _rows(tr, 2560), _rows(tr, 128)],
        out_specs=pl.BlockSpec((4, tr, 1156), lambda i: (0, i, 0)),
        out_shape=_sds((4, D, 1156), d_uv.dtype),
        compiler_params=_cp(),
    )(d_uv, d_ssd, d_dt)


def _cast_w_in(w, kh):
    C, _, R = w.shape

    def body(kh_ref, w_ref, o_ref):
        o_ref[0] = w_ref[:, 0, :].T.astype(BF16)

    return pl.pallas_call(
        body, name="cast_w_in",
        grid_spec=pltpu.PrefetchScalarGridSpec(
            num_scalar_prefetch=1, grid=(pl.cdiv(C, _TJ),),
            in_specs=[pl.BlockSpec((_TJ, 1, R), lambda j, kh: (j, 0, 0))],
            out_specs=pl.BlockSpec((1, R, _TJ), lambda j, kh: (kh[0], 0, j))),
        out_shape=_sds((4, R, C), BF16),
        compiler_params=_cp(),
    )(kh, w)


def _cast_into_slot(ws, kh, name):
    n = len(ws)

    def body(kh_ref, *refs):
        for k in range(n):
            refs[n + k][0] = refs[k][...].astype(BF16)

    return pl.pallas_call(
        body, name=name,
        grid_spec=pltpu.PrefetchScalarGridSpec(
            num_scalar_prefetch=1, grid=(_PARTS,),
            in_specs=[pl.BlockSpec((w.shape[0] // _PARTS, w.shape[1]), lambda i, kh: (i, 0)) for w in ws],
            out_specs=[pl.BlockSpec((1, w.shape[0] // _PARTS, w.shape[1]), lambda i, kh: (kh[0], i, 0))
                       for w in ws]),
        out_shape=[_sds((4,) + w.shape, BF16) for w in ws],
        compiler_params=_cp(),
    )(kh, *ws)


_ANY = pl.BlockSpec(memory_space=pl.ANY)
_CHIP_FLIPS = [(1, 0), (0, 1), (1, 1)]
_DEVICE_FLIPS = [(fx, fy, fc) for fx in (0, 1) for fy in (0, 1) for fc in (0, 1)][1:]


def _half(h, rows):
    return pl.ds(pl.multiple_of(h * rows, rows), rows)


def _remote(src, dst, ssem, rsem, to):
    return pltpu.make_async_remote_copy(src_ref=src, dst_ref=dst, send_sem=ssem, recv_sem=rsem,
                                        device_id=to, device_id_type=MESH)


def _weight_gather(bufs, conv):
    n = len(bufs)

    def body(*refs):
        conv_ref, outs, conv_out = refs[n], refs[n + 1:2 * n + 1], refs[2 * n + 1]
        send_sems, recv_sems, fsend_sems, frecv_sems, csend_sems, crecv_sems, local_sem = refs[2 * n + 2:]
        x, y, c = lax.axis_index("x"), lax.axis_index("y"), lax.axis_index("c")
        me = 2 * x + y
        halves = [_half(c, r.shape[1] // 2) for r in outs]
        others = [_half(1 - c, r.shape[1] // 2) for r in outs]
        remote = _remote
        local = [pltpu.make_async_copy(conv_ref, conv_out.at[me], local_sem)]
        for cp in local:
            cp.start()
        sends = []
        for k, (fx, fy) in enumerate(_CHIP_FLIPS):
            peer = (x ^ fx, y ^ fy, c)
            for i in range(n):
                mine = outs[i].at[me, halves[i]]
                sends.append(remote(mine, mine, send_sems.at[k * n + i], recv_sems.at[k * n + i], peer))
            sends.append(remote(conv_ref, conv_out.at[me], csend_sems.at[k], crecv_sems.at[k], peer))
        for cp in sends:
            cp.start()
        sibling = (x, y, 1 - c)
        forwards = []
        for k, (fx, fy) in enumerate(_CHIP_FLIPS):
            peer = (x ^ fx, y ^ fy, c)
            src = 2 * (x ^ fx) + (y ^ fy)
            for i in range(n):
                landed = outs[i].at[src, halves[i]]
                remote(landed, landed, send_sems.at[k * n + i], recv_sems.at[k * n + i], peer).wait_recv()
                fw = remote(landed, landed, fsend_sems.at[k * n + i], frecv_sems.at[k * n + i], sibling)
                fw.start()
                forwards.append(fw)
            remote(conv_out.at[src], conv_out.at[src], csend_sems.at[k], crecv_sems.at[k], peer).wait_recv()
        for k, (fx, fy) in enumerate(_CHIP_FLIPS):
            src = 2 * (x ^ fx) + (y ^ fy)
            for i in range(n):
                theirs = outs[i].at[src, others[i]]
                remote(theirs, theirs, fsend_sems.at[k * n + i], frecv_sems.at[k * n + i], sibling).wait_recv()
        for cp in sends + forwards:
            cp.wait_send()
        for cp in local:
            cp.wait()

    dma = pltpu.SemaphoreType.DMA
    return pl.pallas_call(
        body, name="weight_gather",
        in_specs=[_ANY] * (n + 1), out_specs=[_ANY] * (n + 1),
        out_shape=[_sds(b.shape, b.dtype) for b in bufs] + [_sds((4,) + conv.shape, conv.dtype)],
        input_output_aliases={i: i for i in range(n)},
        scratch_shapes=[dma((3 * n,)), dma((3 * n,)), dma((3 * n,)), dma((3 * n,)), dma((3,)), dma((3,)), dma],
    )(*bufs, conv)


def _piece(ref, kind, R, C, k, h):
    if kind == "slab":
        return ref.at[k, _half(h, R // 2), :]
    if kind == "rows":
        return ref.at[pl.ds(pl.multiple_of(k * R + h * (R // 2), R // 2), R // 2), :]
    return ref.at[_half(h, R // 2), pl.ds(pl.multiple_of(k * C, C), C)]


_HBM = pl.BlockSpec(memory_space=pltpu.HBM)
_SEM = pl.BlockSpec(memory_space=pltpu.SEMAPHORE)


def _split_start(name, arrays, n_copies, plan, after=None):
    n = len(arrays)
    extra = [] if after is None else [after]

    def body(*refs):
        m = n + len(extra)
        arrs, send_sems, recv_sems, token = refs[:n], refs[m], refs[m + 1], refs[-1]
        for j, (src, dst, peer) in enumerate(plan(arrs)):
            _remote(src, dst, send_sems.at[j], recv_sems.at[j], peer).start()
        token[...] = jnp.zeros_like(token)

    dma = pltpu.SemaphoreType.DMA
    res = pl.pallas_call(
        body, name=name,
        out_shape=(dma((n_copies,)), dma((n_copies,)), *[pltpu.HBM(a.shape, a.dtype) for a in arrays],
                   _sds((8, 128), F32)),
        in_specs=[_HBM] * n + [_ANY] * len(extra),
        out_specs=(_SEM, _SEM, *[_HBM] * n, pl.BlockSpec(memory_space=pltpu.VMEM)),
        input_output_aliases={i: 2 + i for i in range(n)},
        compiler_params=pltpu.CompilerParams(has_side_effects=pltpu.SideEffectType.DATAFLOW_SIDE_EFFECTING),
    )(*[pltpu.with_memory_space_constraint(a, pltpu.HBM) for a in arrays], *extra)
    return res[0], res[1], list(res[2:2 + n]), res[-1]


def _split_wait(name, arrays, send_sems, recv_sems, plan, after, first=0):
    n = len(arrays)

    def body(*refs):
        arrs, ssems, rsems = refs[:n], refs[n], refs[n + 1]
        for j, (src, dst, peer) in enumerate(plan(arrs), first):
            cp = _remote(src, dst, ssems.at[j], rsems.at[j], peer)
            cp.wait_send()
            cp.wait_recv()

    return list(pl.pallas_call(
        body, name=name,
        out_shape=tuple(pltpu.HBM(a.shape, a.dtype) for a in arrays),
        in_specs=[_HBM] * n + [_SEM, _SEM, _ANY],
        out_specs=tuple([_HBM] * n),
        input_output_aliases={i: i for i in range(n)},
        compiler_params=pltpu.CompilerParams(has_side_effects=pltpu.SideEffectType.DATAFLOW_SIDE_EFFECTING),
    )(*arrays, send_sems, recv_sems, after))


def _gather_plan(n):
    def plan(bufs):
        x, y, c = lax.axis_index("x"), lax.axis_index("y"), lax.axis_index("c")
        me = 2 * x + y
        return [(bufs[i].at[me], bufs[i].at[me], (x ^ fx, y ^ fy, c)) for fx, fy in _CHIP_FLIPS for i in range(n)]

    return plan


def _reduce_plan(specs, n_small):
    n = len(specs)

    def plan(arrs):
        x, y, c = lax.axis_index("x"), lax.axis_index("y"), lax.axis_index("c")
        slot = 4 * x + 2 * y + c
        out = []
        for fx, fy, fc in _DEVICE_FLIPS:
            peer = (x ^ fx, y ^ fy, c ^ fc)
            for i, (kind, (R, C)) in enumerate(specs):
                out.append((_piece(arrs[i], kind, R, C, 2 * peer[0] + peer[1], peer[2]), arrs[n + i].at[slot], peer))
            for s in range(n_small):
                out.append((arrs[2 * n + 2 * s], arrs[2 * n + 2 * s + 1].at[slot], peer))
        return out

    return plan


def _sibling_plan(n):
    def plan(arrs):
        x, y, c = lax.axis_index("x"), lax.axis_index("y"), lax.axis_index("c")
        return [(arrs[i], arrs[n + i], (x, y, 1 - c)) for i in range(n)]

    return plan


def _sibling_exchange(halves, name, small=None):
    n = len(halves)
    ns = 0 if small is None else 1

    def body(*refs):
        ins, outs = refs[:n], refs[n + ns:2 * n + ns]
        send_sems, recv_sems = refs[2 * (n + ns)], refs[2 * (n + ns) + 1]
        x, y, c = lax.axis_index("x"), lax.axis_index("y"), lax.axis_index("c")
        copies = [_remote(ins[i], outs[i], send_sems.at[i], recv_sems.at[i], (x, y, 1 - c)) for i in range(n)]
        waits = list(copies)
        if ns:
            s_ref, slots_ref, ssend_sems, srecv_sems, local_sem = refs[n], refs[2 * n + 1], *refs[2 * (n + ns) + 2:]
            slot = 4 * x + 2 * y + c
            own = pltpu.make_async_copy(s_ref, slots_ref.at[slot], local_sem)
            own.start()
            for k, (fx, fy, fc) in enumerate(_DEVICE_FLIPS):
                peer = (x ^ fx, y ^ fy, c ^ fc)
                copies.append(_remote(s_ref, slots_ref.at[slot], ssend_sems.at[k], srecv_sems.at[k], peer))
                theirs = slots_ref.at[slot ^ (k + 1)]
                waits.append(_remote(theirs, theirs, ssend_sems.at[k], srecv_sems.at[k], peer))
        for cp in copies:
            cp.start()
        for cp in waits:
            cp.wait()
        if ns:
            own.wait()

    dma = pltpu.SemaphoreType.DMA
    extra_in = [] if small is None else [small]
    extra_out = [] if small is None else [_sds((8,) + small.shape, F32)]
    return pl.pallas_call(
        body, name=name,
        in_specs=[_ANY] * (n + ns), out_specs=[_ANY] * (n + ns),
        out_shape=[_sds(h.shape, h.dtype) for h in halves] + extra_out,
        scratch_shapes=[dma((n,)), dma((n,))] + ([dma((7,)), dma((7,)), dma] if ns else []),
    )(*halves, *extra_in)


_BIG = [("w_in", (1024, 1156), "slab"), ("w_out", (512, 1024), "rows"), ("w_ff1", (1024, 1024), "cols"),
        ("w_ff2", (1024, 1024), "rows"), ("w_ple_gate", (256, 1024), "rows"), ("w_ple_proj", (256, 256), "cols")]
_SMALL = [("norm_mix_g", (1, 1024)), ("gm_v_norm_g", (1, 1024)), ("gm_ws", (1, 8, 128, 128)), ("gm_bs", (1, 8, 128)),
          ("gm_out_norm_g", (1, 1024)), ("ssd_conv_w", (1, 4, 1536)), ("ssd_conv_b", (1, 1536)),
          ("ssd_dt_bias", (1, 16)), ("ssd_a_log", (1, 16)), ("ssd_d", (1, 16)), ("ssd_norm_g", (1, 1024)),
          ("norm_mlp_g", (1, 1024)), ("ple_norm_g", (1, 1024)), ("final_norm_g", (1024,))]


def _rows128(a):
    flat = a.reshape(-1)
    rows = -(-flat.shape[0] // 1024) * 8
    return jnp.pad(flat, (0, rows * 128 - flat.shape[0])).reshape(rows, 128)


def _pad_lanes(v, n=128):
    v = v.reshape(1, -1)
    return jnp.pad(v, ((0, 0), (0, n - v.shape[1])))


_SMALL_SHAPES = dict(_SMALL + [("loss", ())])
_BIG_SPECS = {n: (kind, shp) for n, shp, kind in _BIG}


class _Comm:
    def __init__(self, a, kh):
        self.a, self.kh = a, kh
        rest = _BIG[1:]
        self.bufs = {"w_in": _cast_w_in(a["w_in"].transpose(2, 0, 1), kh)}
        cast = _cast_into_slot([a[n].reshape(shp) for n, shp, _ in rest], kh, "cast_rest")
        self.bufs.update({n: c for (n, _, _), c in zip(rest, cast)})
        self.sent = []
        self.small_packs = []

    def w_in(self):
        g_win, g_cw = _weight_gather([self.bufs["w_in"]], self.a["ssd_conv_w"].reshape(4, 384))
        first, later = ["w_out", "w_ff1"], ["w_ff2", "w_ple_gate", "w_ple_proj"]
        nf = len(first)
        plans = _gather_plan(nf), _gather_plan(len(later))

        def plan(bufs):
            return plans[0](bufs[:nf]) + plans[1](bufs[nf:])

        ssem, rsem, thru, token = _split_start("gather_start", [self.bufs[n] for n in first + later],
                                               3 * len(first + later), plan, after=g_cw)
        self.gather = {"out": (plans[0], ssem, rsem, thru[:nf], 0), "ff": (plans[1], ssem, rsem, thru[nf:], 3 * nf)}
        wm, wdt = _assemble_w_in(g_win)
        return wm, wdt, jnp.concatenate([g_cw[k] for k in range(4)], axis=1), token

    def rest(self, tag, after):
        plan, ssem, rsem, thru, first = self.gather[tag]
        got = _split_wait("gather_wait_" + tag, thru, ssem, rsem, plan, after, first)
        if tag == "out":
            return got[0].reshape(2048, D), got[1]
        g_w2, g_wg, g_wp = got
        return g_w2.reshape(DFF, D), g_wg.reshape(D, D), g_wp

    def send(self, tag, grads):
        big = [n for n, _, _ in _BIG if n in grads]
        small = [n for n in _SMALL_SHAPES if n in grads]
        parts = [_rows128(grads[n]) for n in small]
        rows = [s.shape[0] for s in parts]
        if not big:
            self.last_small = (tag, small, rows, jnp.concatenate(parts, axis=0))
            return None
        srcs = [grads[n] for n in big]
        lands = [lax.empty((8, _BIG_SPECS[n][1][0] // 2, _BIG_SPECS[n][1][1]), GRAD) for n in big]
        extra = []
        if small:
            pack = jnp.concatenate(parts, axis=0)
            extra = [pack, jnp.broadcast_to(pack, (8,) + pack.shape)]
        red_plan = _reduce_plan([_BIG_SPECS[n] for n in big], len(extra) // 2)
        n_copies = 7 * (len(big) + len(extra) // 2)
        halves = []
        if self.sent:
            self.early = self._landed(self.sent[0], srcs[0])
            halves = list(self.early.values())
        arrays = srcs + lands + extra
        nr, nh = len(arrays), len(halves)
        sib_plan = _sibling_plan(nh)

        def plan(arrs):
            return sib_plan(arrs[nr:]) + red_plan(arrs[:nr])

        arrays += halves + [lax.empty(h.shape, F32) for h in halves]
        ssem, rsem, thru, token = _split_start("reduce_start_" + tag, arrays, n_copies + nh, plan)
        self.sent.append((tag, big, small, rows, red_plan, ssem, rsem, thru[:nr], nh))
        self.swap = (sib_plan, ssem, rsem, thru[nr:])
        return token

    def _landed(self, sent, after):
        tag, big, small, rows, plan, ssem, rsem, thru, first = sent
        arrs = _split_wait("reduce_wait_" + tag, thru, ssem, rsem, plan, after, first)
        nb_ = len(big)
        sums = _sum_slots([(arrs[nb_ + i], arrs[i]) + _BIG_SPECS[n] for i, n in enumerate(big)], self.kh, "sum_" + tag)
        if small:
            self.small_packs.append((small, rows, arrs[2 * nb_ + 1]))
        return dict(zip(big, sums))

    def finish(self, after):
        a, results = self.a, {}
        sib_plan, ssem, rsem, swapped = self.swap
        names = list(self.early)
        swapped = _split_wait("sibling_wait_early", swapped, ssem, rsem, sib_plan, after)
        mine, other = swapped[:len(names)], swapped[len(names):]
        items = [(a[n].reshape(_BIG_SPECS[n][1]), own, oth, a["m_" + n].reshape(_BIG_SPECS[n][1]),
                  a["v_" + n].reshape(_BIG_SPECS[n][1])) for n, own, oth in zip(names, mine, other)]
        results.update(zip(names, _adamw_halves(items, self.kh, "adamw_early")))
        own = self._landed(self.sent[-1], results[names[-1]][1])
        stag, small, rows, pack = self.last_small
        other, slots = _sibling_exchange([own["w_in"]], "sibling_exchange_late", pack)
        self.small_packs.append((small, rows, slots))
        w, m, v = (a[k].transpose(2, 0, 1) for k in ("w_in", "m_w_in", "v_w_in"))
        raw = _adamw_transposed(w, own["w_in"], other, m, v, "adamw_late")
        results["w_in"] = tuple(r.transpose(1, 2, 0) for r in raw)
        return results, self.small_packs


def _local_step(x, p, tgt, sm, comm, nb, tm):
    T = x.shape[0]
    wm, wdt, conv_w, token = comm.w_in()
    g_mix, gv, gout = sm["norm_mix_g"].reshape(1, D), sm["gm_v_norm_g"].reshape(1, D), sm["gm_out_norm_g"].reshape(1, D)
    ws = sm["gm_ws"].reshape(GM_HEADS, CH, CH)
    bst = jnp.pad(sm["gm_bs"].reshape(GM_HEADS, CH).T, ((0, 0), (0, 128 - GM_HEADS)))
    convw = jnp.pad(conv_w, ((0, 4), (0, 0)))
    convb = sm["ssd_conv_b"].reshape(1, CONV_CH)
    dtb, alog = _pad_lanes(sm["ssd_dt_bias"]), _pad_lanes(sm["ssd_a_log"])
    dskip = jnp.repeat(sm["ssd_d"].reshape(SSD_HEADS), SSD_P).reshape(1, 1024)
    ng, g_mlp, g_ple = sm["ssd_norm_g"].reshape(1, D), sm["norm_mlp_g"].reshape(1, D), sm["ple_norm_g"].reshape(1, D)
    gf = sm["final_norm_g"].reshape(1, D)
    head_of_lane = lax.broadcasted_iota(jnp.int32, (128, 1024), 1) // SSD_P
    ex = (lax.broadcasted_iota(jnp.int32, (128, 1024), 0) == head_of_lane).astype(BF16)
    ext = ex.T
    ltri = (lax.broadcasted_iota(jnp.int32, (CH, CH), 0) >= lax.broadcasted_iota(jnp.int32, (CH, CH), 1)).astype(F32)

    pz, pxbc, dtraw, xn, cat, uv = _inproj_gmlp(x, g_mix, wm, wdt, gv, ws, bst, gout, tm, token)
    cat, sall, conv = _ssd_fwd(pz, pxbc, dtraw, cat, convw, convb, dtb, alog, dskip, ng, ex, ltri, nb)
    wo, w1 = comm.rest("out", cat)
    h1, hn, hid = _outproj_ff1(cat, wo, x, g_mlp, w1, tm)
    w2, wg, wp = comm.rest("ff", hn)
    hp, dgl, dpe, dh2, dh2b, loss, d_gf, d_gple = _ff2_tail(hid, w2, h1, g_ple, p, tgt, wg, wp, gf, tm)

    d_wp = _matmul_tn(p, dpe, "dw_ple_proj", a_fn=lambda a: a.astype(MXU))
    d_wg = _matmul_tn(hp, dgl, "dw_ple_gate")
    d_w2 = _matmul_tn(hid, dh2b, "dw_ff2", a_fn=_sq)
    dpre = _ff2_bwd(dh2b, w2, hid, min(T, 2 * tm))
    d_w1 = _matmul_tn(hn, dpre, "dw_ff1")
    dh1, dh1b, d_gmlp, dcat = _ff1_bwd(dpre, w1, dh2, h1, g_mlp, wo, tm)
    d_wo = _matmul_tn(cat, dh1b, "dw_out")
    duv, d_gv, d_ws, d_bst, d_gout, dxn_uv = _gmlp_bwd(uv, dcat, gv, ws, bst, gout, wm)
    token = comm.send("early", {
        "w_ple_proj": d_wp, "w_ple_gate": d_wg, "w_ff2": d_w2, "w_ff1": d_w1, "w_out": d_wo, "loss": loss[0:1, 0:1], "final_norm_g": d_gf, "ple_norm_g": d_gple, "norm_mlp_g": d_gmlp,
        "gm_v_norm_g": d_gv, "gm_ws": d_ws, "gm_bs": d_bst[:, :GM_HEADS].T, "gm_out_norm_g": d_gout})
    dssd, ddt, d_cw, d_cb, d_dtb, d_al, d_ds, d_ng = _ssd_bwd(
        pz, pxbc, conv, dtraw, sall, dcat, convw, dtb, alog, dskip, ng, ex, ltri, ext, nb, token)
    d_win = _split_dw_in(_matmul_tn(xn, duv, "dw_in_uv"), _matmul_tn(xn, dssd, "dw_in_ssd"),
                         _matmul_tn(xn, ddt, "dw_in_dt"))
    token = comm.send("late", {"w_in": d_win})
    dx, d_gmix = _inproj_bwd(dxn_uv, dssd, ddt, wm, wdt, dh1, x, g_mix, tm, token)
    comm.send("d", {"norm_mix_g": d_gmix, "ssd_conv_w": d_cw[0:4], "ssd_conv_b": d_cb, "ssd_dt_bias": d_dtb[:, :16],
                    "ssd_a_log": d_al[:, :16], "ssd_d": d_ds[:, :16], "ssd_norm_g": d_ng})
    return dx


def kernel(x, p, norm_mix_g, w_in, gm_v_norm_g, gm_ws, gm_bs, gm_out_norm_g, ssd_conv_w, ssd_conv_b, ssd_dt_bias, ssd_a_log, ssd_d, ssd_norm_g, w_out, norm_mlp_g, w_ff1, w_ff2, ple_norm_g, w_ple_gate, w_ple_proj, final_norm_g, loss_target, m_norm_mix_g, m_w_in, m_gm_v_norm_g, m_gm_ws, m_gm_bs, m_gm_out_norm_g, m_ssd_conv_w, m_ssd_conv_b, m_ssd_dt_bias, m_ssd_a_log, m_ssd_d, m_ssd_norm_g, m_w_out, m_norm_mlp_g, m_w_ff1, m_w_ff2, m_ple_norm_g, m_w_ple_gate, m_w_ple_proj, m_final_norm_g, v_norm_mix_g, v_w_in, v_gm_v_norm_g, v_gm_ws, v_gm_bs, v_gm_out_norm_g, v_ssd_conv_w, v_ssd_conv_b, v_ssd_dt_bias, v_ssd_a_log, v_ssd_d, v_ssd_norm_g, v_w_out, v_norm_mlp_g, v_w_ff1, v_w_ff2, v_ple_norm_g, v_w_ple_gate, v_w_ple_proj, v_final_norm_g):
    a = dict(locals())
    order = ["norm_mix_g", "w_in", "gm_v_norm_g", "gm_ws", "gm_bs", "gm_out_norm_g", "ssd_conv_w", "ssd_conv_b",
             "ssd_dt_bias", "ssd_a_log", "ssd_d", "ssd_norm_g", "w_out", "norm_mlp_g", "w_ff1", "w_ff2", "ple_norm_g",
             "w_ple_gate", "w_ple_proj", "final_norm_g"]
    chip = 2 * lax.axis_index("x") + lax.axis_index("y")
    nb, S = x.shape[0], x.shape[1]
    T = nb * S
    sm = {n: a[n] for n, _ in _SMALL if n != "ssd_conv_w"}
    comm = _Comm(a, jnp.stack([chip, lax.axis_index("c")]).astype(jnp.int32))
    dx = _local_step(x.reshape(T, D), p.reshape(T, DPLE), loss_target.reshape(T, D), sm, comm, nb, 512)
    big, small_packs = comm.finish(dx)
    small, loss = _small_update(a, small_packs)
    g_out, delta, new_m, new_v = {}, {}, {}, {}
    for n in order:
        g_out[n], delta[n], new_m[n], new_v[n] = (r.reshape(a[n].shape) for r in (big[n] if n in big else small[n]))
    return (loss.reshape(()), dx.reshape(x.shape), *[g_out[n] for n in order], *[delta[n] for n in order],
            *[new_m[n] for n in order], *[new_v[n] for n in order])
```

```python
import jax
import jax.numpy as jnp
from jax import lax
from jax.experimental import pallas as pl
from jax.experimental.pallas import tpu as pltpu

F32 = jnp.float32
BF16 = jnp.bfloat16
MXU = jnp.bfloat16
GRAD = jnp.bfloat16

D = 1024
CH = 128
GM_HEADS = 8
SSD_HEADS = 16
SSD_P = 64
CONV_CH = 1536
N_MAIN = 4608
DFF = 4096
DPLE = 256
EPS = 1e-6
NEG = -1e30

LR, B1, B2, ADAM_EPS, WD, STEP = 0.001, 0.9, 0.999, 1e-08, 0.01, 10

VMEM_LIMIT = 56 * 1024 * 1024
_SEQS_PER_STEP = 4
MESH = pl.DeviceIdType.MESH

INV_SQRT2 = 0.7071067811865476
INV_SQRT_2PI = 0.3989422804014327


def _cp(n_axes=1):
    return pltpu.CompilerParams(dimension_semantics=("arbitrary",) * n_axes, vmem_limit_bytes=VMEM_LIMIT)


def _dot(a, b):
    return jnp.dot(a, b, preferred_element_type=F32)


def _dot_nt(a, b):
    return lax.dot_general(a, b, (((1,), (1,)), ((), ())), preferred_element_type=F32)


def _dot_tn(a, b):
    return lax.dot_general(a, b, (((0,), (0,)), ((), ())), preferred_element_type=F32)


def _dot_hi(a, b):
    return jnp.dot(a, b, preferred_element_type=F32, precision=lax.Precision.HIGHEST)


def _dot_01(a, sel):
    hi = a.astype(BF16)
    lo = (a - hi.astype(F32)).astype(BF16)
    n = a.shape[0]
    r = _dot(jnp.concatenate([hi, lo], axis=0), sel)
    return r[0:n] + r[n:2 * n]


def _rows(tm, n, j=0):
    return pl.BlockSpec((tm, n), lambda i: (i, j))


def _const(shape):
    nd = len(shape)
    return pl.BlockSpec(shape, lambda *_: (0,) * nd)


def _sds(shape, dtype):
    return jax.ShapeDtypeStruct(shape, dtype)


def _rms(x):
    r = lax.rsqrt(jnp.mean(x * x, axis=-1, keepdims=True) + EPS)
    return x * r, r


def _rms_bwd(dy, xhat, r, g):
    dyg = dy * g
    return r * (dyg - xhat * jnp.mean(dyg * xhat, axis=-1, keepdims=True))


def _sigmoid(x):
    return 1.0 / (1.0 + jnp.exp(-x))


def _gelu(x):
    cdf = 0.5 * (1.0 + lax.erf(x * INV_SQRT2))
    pdf = jnp.exp(-0.5 * x * x) * INV_SQRT_2PI
    return x * cdf, cdf + x * pdf


def _softplus(x):
    e = jnp.exp(-jnp.abs(x))
    u = 1.0 + e
    log1p = jnp.where(u == 1.0, e, jnp.log(u) * e / (u - 1.0))
    return jnp.maximum(x, 0.0) + log1p


def _after(n_in, fn):
    def body(*refs):
        return fn(*refs[:n_in], *refs[n_in + 1:])

    return body


def _inproj_gmlp(x, g, wm, wdt, gv, ws, bst, gout, tm, after):
    T = x.shape[0]

    def body(x_ref, g_ref, wm_ref, wdt_ref, gv_ref, ws_ref, bst_ref, gout_ref,
             z_ref, xbc_ref, dt_ref, xn_ref, ya_ref, uv_ref):
        xh, _ = _rms(x_ref[...])
        xn = (xh * g_ref[...]).astype(MXU)
        xn_ref[...] = xn
        for n in range(4):
            uv_ref[:, n * 512:(n + 1) * 512] = _dot(xn, wm_ref[:, n * 512:(n + 1) * 512])
        for n in range(2):
            z_ref[:, n * 512:(n + 1) * 512] = _dot(xn, wm_ref[:, 2048 + n * 512:2048 + (n + 1) * 512])
        for n in range(3):
            xbc_ref[:, n * 512:(n + 1) * 512] = _dot(xn, wm_ref[:, 3072 + n * 512:3072 + (n + 1) * 512])
        dt_ref[...] = _dot(xn, wdt_ref[...])
        for k in range(tm // CH):
            rows = slice(k * CH, (k + 1) * CH)
            f = _gmlp_fwd_vals(uv_ref[rows, 0:1024], uv_ref[rows, 1024:2048], gv_ref[...], ws_ref, bst_ref[...],
                               gout_ref[...])
            ya_ref[rows, :] = f["out"].astype(MXU)

    return pl.pallas_call(
        _after(8, body), grid=(T // tm,), name="inproj_gmlp",
        in_specs=[_rows(tm, D), _const((1, D)), _const((D, N_MAIN)), _const((D, 128)), _const((1, 1024)),
                  _const((GM_HEADS, CH, CH)), _const((CH, 128)), _const((1, 1024)), _ANY],
        out_specs=[_rows(tm, 1024), _rows(tm, CONV_CH), _rows(tm, 128), _rows(tm, D), _rows(tm, 1024, 0),
                   _rows(tm, 2048)],
        out_shape=[_sds((T, 1024), F32), _sds((T, CONV_CH), F32), _sds((T, 128), F32), _sds((T, D), MXU),
                   _sds((T, 2048), MXU), _sds((T, 2048), F32)],
        compiler_params=_cp(),
    )(x, g, wm, wdt, gv, ws, bst, gout, after)


def _gmlp_fwd_vals(u, v, gv, ws_ref, bst, gout):
    ug, dug = _gelu(u)
    vg, dvg = _gelu(v)
    row = lax.broadcasted_iota(jnp.int32, (CH, CH), 0)
    col = lax.broadcasted_iota(jnp.int32, (CH, CH), 1)
    tril = row >= col
    ys, heads = [], []
    for h in range(GM_HEADS):
        sl = slice(h * 128, (h + 1) * 128)
        vhat, rv = _rms(vg[:, sl])
        vn = (vhat * gv[:, sl]).astype(MXU)
        wt = jnp.where(tril, ws_ref[h], 0.0)
        mixed = _dot(wt.astype(MXU), vn) + bst[:, h:h + 1]
        ys.append(ug[:, sl] * mixed)
        heads.append((vhat, rv, vn, wt, mixed))
    y = jnp.concatenate(ys, axis=1)
    yhat, ry = _rms(y)
    return dict(ug=ug, dug=dug, dvg=dvg, heads=heads, yhat=yhat, ry=ry, tril=tril, out=yhat * gout)


def _shifts_down(cur, halo):
    row8 = lax.broadcasted_iota(jnp.int32, (8, cur.shape[1]), 0)
    out = [cur]
    for j in (1, 2, 3):
        sh = pltpu.roll(cur, j, 0)
        top = jnp.where(row8 < j, pltpu.roll(halo, j, 0), sh[0:8])
        out.append(jnp.concatenate([top, sh[8:]], axis=0))
    return out


def _shifts_up(cur, halo):
    row8 = lax.broadcasted_iota(jnp.int32, (8, cur.shape[1]), 0)
    out = []
    for j in (1, 2, 3):
        sh = pltpu.roll(cur, CH - j, 0)
        bot = jnp.where(row8 + j >= 8, pltpu.roll(halo, 8 - j, 0), sh[CH - 8:CH])
        out.append(jnp.concatenate([sh[0:CH - 8], bot], axis=0))
    return out


def _conv(xbc, halo, convw, convb):
    sh = _shifts_down(xbc, halo)
    return convb + convw[3:4] * sh[0] + convw[2:3] * sh[1] + convw[1:2] * sh[2] + convw[0:1] * sh[3]


def _ssd_fwd_vals(z, conv, dtraw, dtb, alog, dskip, ng, ex, ltri, s_prev):
    sig_c = _sigmoid(conv)
    xa = conv * sig_c
    xs = xa[:, :1024]
    bm = [xa[:, 1024:1152], xa[:, 1152:1280]]
    cm = [xa[:, 1280:1408], xa[:, 1408:1536]]
    dtpre = dtraw + dtb
    dt = _softplus(dtpre)
    a_neg = -jnp.exp(alog)
    cs = _dot_hi(ltri, dt * a_neg)
    cst = cs.T
    last = cs[CH - 1:CH]
    ecs = jnp.exp(cs)
    dec = jnp.exp(last - cs)
    spread = _dot_01(jnp.concatenate([dt, ecs, dec], axis=0), ex)
    dte, ecse, dece = spread[0:CH], spread[CH:2 * CH], spread[2 * CH:3 * CH]
    cde = ecse[CH - 1:CH]
    de = dskip
    xdt = xs * dte
    row = lax.broadcasted_iota(jnp.int32, (CH, CH), 0)
    col = lax.broadcasted_iota(jnp.int32, (CH, CH), 1)
    tril = row >= col
    lo = col < SSD_P
    bmb = [b.astype(MXU) for b in bm]
    cmb = [c.astype(MXU) for c in cm]
    mg = [_dot_nt(cmb[g], bmb[g]) for g in range(2)]
    yd, lms, whs = [], [], []
    for q in range(8):
        g = q // 4
        xq = xdt[:, q * 128:(q + 1) * 128]
        acc = None
        for hh in range(2):
            h = 2 * q + hh
            seg = cs[:, h:h + 1] - cst[h:h + 1, :]
            lm = jnp.exp(jnp.where(tril, seg, NEG))
            wh = (mg[g] * lm).astype(MXU)
            xm = jnp.where(lo if hh == 0 else ~lo, xq, 0.0).astype(MXU)
            part = _dot(wh, xm)
            acc = part if acc is None else acc + part
            lms.append(lm)
            whs.append(wh)
        yd.append(acc)
    yd = jnp.concatenate(yd, axis=1)
    sb = s_prev.astype(MXU)
    yo = jnp.concatenate([_dot(cmb[g], sb[:, g * 512:(g + 1) * 512]) for g in range(2)], axis=1) * ecse
    xdec = (xdt * dece).astype(MXU)
    states = jnp.concatenate([_dot_tn(bmb[g], xdec[:, g * 512:(g + 1) * 512]) for g in range(2)], axis=1)
    s_next = s_prev * cde + states
    ypre = yd + yo + de * xs
    sig_z = _sigmoid(z)
    yg = ypre * z * sig_z
    outs, yhat, rr = [], [], []
    for g in range(2):
        sl = slice(g * 512, (g + 1) * 512)
        yh, r = _rms(yg[:, sl])
        yhat.append(yh)
        rr.append(r)
        outs.append(yh * ng[:, sl])
    return dict(sig_c=sig_c, xa=xa, xs=xs, bmb=bmb, cmb=cmb, dtpre=dtpre, dt=dt, a_neg=a_neg,
                cs=cs, last=last, ecs=ecs, dec=dec, dte=dte, ecse=ecse, dece=dece, cde=cde, de=de, xdt=xdt,
                mg=mg, lms=lms, whs=whs, lo=lo, yo=yo, sb=sb, xdec=xdec, s_next=s_next, ypre=ypre, sig_z=sig_z,
                yhat=yhat, rr=rr, out=jnp.concatenate(outs, axis=1))


def _ssd_fwd(pz, pxbc, dtraw, cat, convw, convb, dtb, alog, dskip, ng, ex, ltri, nb):
    T = pz.shape[0]
    S = T // nb
    nch = S // CH
    ns = _SEQS_PER_STEP if nb % _SEQS_PER_STEP == 0 else 1

    def body(z_ref, xbc_ref, halo_ref, dt_ref, cw_ref, cb_ref, dtb_ref, al_ref, ds_ref, ng_ref, ex_ref, lt_ref,
             cat_in_ref, yb_ref, sall_ref, conv_ref, s_ref):
        del cat_in_ref
        c = pl.program_id(1)

        @pl.when(c == 0)
        def _():
            s_ref[...] = jnp.zeros_like(s_ref)

        for i in range(ns):
            halo = jnp.where(c == 0, 0.0, halo_ref[i])
            s_prev = s_ref[i]
            sall_ref[i, 0] = s_prev
            conv = _conv(xbc_ref[i], halo, cw_ref[...], cb_ref[...])
            conv_ref[i] = conv
            f = _ssd_fwd_vals(z_ref[i], conv, dt_ref[i], dtb_ref[...], al_ref[...], ds_ref[...], ng_ref[...],
                              ex_ref[...], lt_ref[...], s_prev)
            s_ref[i] = f["s_next"]
            yb_ref[i] = f["out"].astype(MXU)

    def seq(width, col=0):
        return pl.BlockSpec((ns, CH, width), lambda b, c: (b, c, col))

    cat, sall, conv = pl.pallas_call(
        body, grid=(nb // ns, nch), name="ssd_fwd",
        in_specs=[seq(1024), seq(CONV_CH),
                  pl.BlockSpec((ns, 8, CONV_CH), lambda b, c: (b, jnp.maximum(c * (CH // 8) - 1, 0), 0)),
                  seq(128),
                  _const((8, CONV_CH)), _const((1, CONV_CH)), _const((1, 128)), _const((1, 128)), _const((1, 1024)),
                  _const((1, 1024)), _const((128, 1024)), _const((CH, CH)), _ANY],
        out_specs=[seq(1024, 1), pl.BlockSpec((ns, 1, 128, 1024), lambda b, c: (b, c, 0, 0)), seq(CONV_CH)],
        out_shape=[_sds((nb, S, 2048), MXU), _sds((nb, nch, 128, 1024), F32), _sds((nb, S, CONV_CH), F32)],
        scratch_shapes=[pltpu.VMEM((ns, 128, 1024), F32)],
        input_output_aliases={12: 0},
        compiler_params=_cp(2),
    )(pz.reshape(nb, S, 1024), pxbc.reshape(nb, S, CONV_CH), pxbc.reshape(nb, S, CONV_CH), dtraw.reshape(nb, S, 128),
      convw, convb, dtb, alog, dskip, ng, ex, ltri, cat.reshape(nb, S, 2048))
    return cat.reshape(T, 2048), sall, conv.reshape(T, CONV_CH)


def _outproj_ff1(cat, wo, x, g, w1, tm):
    T = x.shape[0]

    def body(cat_ref, wo_ref, x_ref, g_ref, w1_ref, h1_ref, hn_ref, hid_ref):
        h1 = x_ref[...] + _dot(cat_ref[...], wo_ref[...])
        h1_ref[...] = h1
        hn = (_rms(h1)[0] * g_ref[...]).astype(MXU)
        hn_ref[...] = hn
        for n in range(4):
            hid_ref[:, n * 1024:(n + 1) * 1024] = jnp.maximum(_dot(hn, w1_ref[n]), 0.0).astype(MXU)

    return pl.pallas_call(
        body, grid=(T // tm,), name="outproj_ff1",
        in_specs=[_rows(tm, 2048), _const((2048, D)), _rows(tm, D), _const((1, D)), _const((4, D, 1024))],
        out_specs=[_rows(tm, D), _rows(tm, D), _rows(tm, DFF)],
        out_shape=[_sds((T, D), F32), _sds((T, D), MXU), _sds((T, DFF), MXU)],
        compiler_params=_cp(),
    )(cat, wo, x, g, w1)


def _sq(hid):
    h = hid.astype(F32)
    return (h * h).astype(MXU)


def _ff2_tail(hid, w2, h1, g_ple, p, tgt, wg, wp, gf, tm):
    T = h1.shape[0]

    def body(hid_ref, w2_ref, h1_ref, g_ref, p_ref, t_ref, wg_ref, wp_ref, gf_ref,
             hp_ref, dgl_ref, dpe_ref, dh2_ref, dh2b_ref, loss_ref, dgf_ref, dg_ref):
        @pl.when(pl.program_id(0) == 0)
        def _():
            loss_ref[...] = jnp.zeros_like(loss_ref)
            dgf_ref[...] = jnp.zeros_like(dgf_ref)
            dg_ref[...] = jnp.zeros_like(dg_ref)

        h2 = h1_ref[...] + _dot(_sq(hid_ref[...]), w2_ref[...])
        h2h, r2 = _rms(h2)
        g_ple = g_ref[...]
        hp = (h2h * g_ple).astype(MXU)
        hp_ref[...] = hp
        gate = _sigmoid(_dot(hp, wg_ref[...]))
        pb = p_ref[...].astype(MXU)
        pe = jnp.concatenate([_dot(pb, wp_ref[k]) for k in range(4)], axis=1)
        h3 = h2 + gate * pe
        hh, r = _rms(h3)
        gf = gf_ref[...]
        diff = hh * gf - t_ref[...]
        loss_ref[...] += 0.5 * jnp.sum(jnp.mean(diff * diff, axis=-1, keepdims=True))
        dout = diff * (1.0 / D)
        dgf_ref[...] += jnp.sum(dout * hh, axis=0, keepdims=True)
        dh3 = _rms_bwd(dout, hh, r, gf)
        dgl = (dh3 * pe * gate * (1.0 - gate)).astype(MXU)
        dgl_ref[...] = dgl
        dpe_ref[...] = (dh3 * gate).astype(MXU)
        dhp = _dot_nt(dgl, wg_ref[...])
        dg_ref[...] += jnp.sum(dhp * h2h, axis=0, keepdims=True)
        dh2 = dh3 + _rms_bwd(dhp, h2h, r2, g_ple)
        dh2_ref[...] = dh2
        dh2b_ref[...] = dh2.astype(MXU)

    return pl.pallas_call(
        body, grid=(T // tm,), name="ff2_tail",
        in_specs=[_rows(tm, DFF), _const((DFF, D)), _rows(tm, D), _const((1, D)), _rows(tm, DPLE), _rows(tm, D),
                  _const((D, D)), _const((4, DPLE, 256)), _const((1, D))],
        out_specs=[_rows(tm, D), _rows(tm, D), _rows(tm, D), _rows(tm, D), _rows(tm, D), _const((8, 128)),
                   _const((1, D)), _const((1, D))],
        out_shape=[_sds((T, D), MXU), _sds((T, D), MXU), _sds((T, D), MXU), _sds((T, D), F32), _sds((T, D), MXU),
                   _sds((8, 128), F32), _sds((1, D), F32), _sds((1, D), F32)],
        compiler_params=_cp(),
    )(hid, w2, h1, g_ple, p, tgt, wg, wp, gf)


def _ff2_bwd(dh2b, w2, hid, tm):
    T = hid.shape[0]

    def body(dh2b_ref, w2_ref, hid_ref, dpre_ref):
        d = dh2b_ref[...]
        for n in range(DFF // 1024):
            sl = slice(n * 1024, (n + 1) * 1024)
            da = _dot_nt(d, w2_ref[sl, :])
            dpre_ref[:, sl] = (2.0 * da * hid_ref[:, sl].astype(F32)).astype(MXU)

    return pl.pallas_call(
        body, grid=(T // tm,), name="ff2_bwd",
        in_specs=[_rows(tm, D), _const((DFF, D)), _rows(tm, DFF)],
        out_specs=_rows(tm, DFF),
        out_shape=_sds((T, DFF), MXU),
        compiler_params=_cp(),
    )(dh2b, w2, hid)


def _ff1_bwd(dpre, w1, dh2, h1, g, wo, tm):
    T = h1.shape[0]

    def body(dpre_ref, w1_ref, dh2_ref, h1_ref, g_ref, wo_ref, dh1_ref, dh1b_ref, dg_ref, dcat_ref):
        @pl.when(pl.program_id(0) == 0)
        def _():
            dg_ref[...] = jnp.zeros_like(dg_ref)

        dhn = _dot_nt(dpre_ref[:, 0:1024], w1_ref[0])
        for k in range(1, 4):
            dhn = dhn + _dot_nt(dpre_ref[:, k * 1024:(k + 1) * 1024], w1_ref[k])
        hh, r = _rms(h1_ref[...])
        dg_ref[...] += jnp.sum(dhn * hh, axis=0, keepdims=True)
        dh1 = dh2_ref[...] + _rms_bwd(dhn, hh, r, g_ref[...])
        dh1_ref[...] = dh1
        dh1b = dh1.astype(MXU)
        dh1b_ref[...] = dh1b
        dcat_ref[:, 0:1024] = _dot_nt(dh1b, wo_ref[0:1024, :])
        dcat_ref[:, 1024:2048] = _dot_nt(dh1b, wo_ref[1024:2048, :])

    return pl.pallas_call(
        body, grid=(T // tm,), name="ff1_bwd",
        in_specs=[_rows(tm, DFF), _const((4, D, 1024)), _rows(tm, D), _rows(tm, D), _const((1, D)),
                  _const((2048, D))],
        out_specs=[_rows(tm, D), _rows(tm, D), _const((1, D)), _rows(tm, 2048)],
        out_shape=[_sds((T, D), F32), _sds((T, D), MXU), _sds((1, D), F32), _sds((T, 2048), F32)],
        compiler_params=_cp(),
    )(dpre, w1, dh2, h1, g, wo)


def _gmlp_bwd(uv, dcat, gv, ws, bst, gout, wm):
    T = uv.shape[0]
    nck = 4 if T % (4 * CH) == 0 else 1
    tb = nck * CH

    def body(uv_ref, dya_ref, gv_ref, ws_ref, bst_ref, gout_ref, wuv_ref, duv_ref, dgv_ref, dws_ref, dbst_ref,
             dgo_ref, dxn_ref):
        @pl.when(pl.program_id(0) == 0)
        def _():
            dgv_ref[...] = jnp.zeros_like(dgv_ref)
            dws_ref[...] = jnp.zeros_like(dws_ref)
            dbst_ref[...] = jnp.zeros_like(dbst_ref)
            dgo_ref[...] = jnp.zeros_like(dgo_ref)

        for k in range(nck):
            chunk(slice(k * CH, (k + 1) * CH), uv_ref, dya_ref, gv_ref, ws_ref, bst_ref, gout_ref, duv_ref,
                  dgv_ref, dws_ref, dbst_ref, dgo_ref)
        dxn_ref[...] = _dot_nt(duv_ref[...], wuv_ref[...])

    def chunk(rows, uv_ref, dya_ref, gv_ref, ws_ref, bst_ref, gout_ref, duv_ref, dgv_ref, dws_ref, dbst_ref,
              dgo_ref):
        gv = gv_ref[...]
        f = _gmlp_fwd_vals(uv_ref[rows, 0:1024], uv_ref[rows, 1024:2048], gv, ws_ref, bst_ref[...], gout_ref[...])
        dya = dya_ref[rows, :]
        dgo_ref[...] += jnp.sum(dya * f["yhat"], axis=0, keepdims=True)
        dy = _rms_bwd(dya, f["yhat"], f["ry"], gout_ref[...])
        lane = lax.broadcasted_iota(jnp.int32, (CH, 128), 1)
        dbs = jnp.zeros((CH, 128), F32)
        dug, dvg, dgvs = [], [], []
        for h in range(GM_HEADS):
            sl = slice(h * 128, (h + 1) * 128)
            vhat, rv, vn, wt, mixed = f["heads"][h]
            dyh = dy[:, sl]
            dug.append(dyh * mixed)
            dmixed = dyh * f["ug"][:, sl]
            dmb = dmixed.astype(MXU)
            dws_ref[h] += jnp.where(f["tril"], _dot_nt(dmb, vn), 0.0)
            dbs = dbs + jnp.where(lane == h, jnp.sum(dmixed, axis=1, keepdims=True), 0.0)
            dvn = _dot_tn(wt.astype(MXU), dmb)
            dgvs.append(jnp.sum(dvn * vhat, axis=0, keepdims=True))
            dvg.append(_rms_bwd(dvn, vhat, rv, gv[:, sl]))
        dbst_ref[...] += dbs
        dgv_ref[...] += jnp.concatenate(dgvs, axis=1)
        duv_ref[rows, 0:1024] = (jnp.concatenate(dug, axis=1) * f["dug"]).astype(MXU)
        duv_ref[rows, 1024:2048] = (jnp.concatenate(dvg, axis=1) * f["dvg"]).astype(MXU)

    return pl.pallas_call(
        body, grid=(T // tb,), name="gmlp_bwd",
        in_specs=[_rows(tb, 2048), _rows(tb, 1024, 0), _const((1, 1024)),
                  _const((GM_HEADS, CH, CH)), _const((CH, 128)), _const((1, 1024)), _const((D, 2048))],
        out_specs=[_rows(tb, 2048), _const((1, 1024)), _const((GM_HEADS, CH, CH)), _const((CH, 128)),
                   _const((1, 1024)), _rows(tb, D)],
        out_shape=[_sds((T, 2048), MXU), _sds((1, 1024), F32), _sds((GM_HEADS, CH, CH), F32), _sds((CH, 128), F32),
                   _sds((1, 1024), F32), _sds((T, D), F32)],
        compiler_params=_cp(),
    )(uv, dcat, gv, ws, bst, gout, wm)


def _ssd_bwd(pz, pxbc, conv, dtraw, sall, dcat, convw, dtb, alog, dskip, ng, ex, ltri, ext, nb, after):
    T = pz.shape[0]
    S = T // nb
    nch = S // CH
    ns = _SEQS_PER_STEP if nb % _SEQS_PER_STEP == 0 else 1

    def seq(width, col=0):
        return pl.BlockSpec((ns, CH, width), lambda b, c: (b, nch - 1 - c, col))

    in_specs = [
        seq(1024), seq(CONV_CH), seq(CONV_CH), seq(128),
        _const((8, CONV_CH)), _const((1, 128)), _const((1, 128)), _const((1, 1024)),
        _const((1, 1024)), _const((128, 1024)), _const((CH, CH)),
        _const((1024, 128)),
        pl.BlockSpec((ns, 1, 128, 1024), lambda b, c: (b, nch - 1 - c, 0, 0)),
        seq(1024, 1),
        _ANY,
    ]

    def body(z_ref, xbc_ref, conv_ref, dt_ref, cw_ref, dtb_ref, al_ref, ds_ref, ng_ref, ex_ref, lt_ref,
             ext_ref, sall_ref, dyb_ref,
             dssd_ref, ddt_ref, dcw_ref, dcb_ref, ddtb_ref, dal_ref, dds_ref, dng_ref,
             dst_ref, dnext_ref, ddse_ref):
        b = pl.program_id(0)
        c = pl.program_id(1)

        @pl.when((b == 0) & (c == 0))
        def _():
            for r in (dcw_ref, dcb_ref, ddtb_ref, dal_ref, dds_ref, dng_ref, ddse_ref):
                r[...] = jnp.zeros_like(r)

        @pl.when(c == 0)
        def _():
            dst_ref[...] = jnp.zeros_like(dst_ref)
            dnext_ref[...] = jnp.zeros_like(dnext_ref)

        ex = ex_ref[...]
        ext = ext_ref[...]
        cw = cw_ref[...]
        ng = ng_ref[...]
        for i in range(ns):
            one_chunk(i, ex, ext, cw, ng, z_ref, xbc_ref, conv_ref, dt_ref, dtb_ref, al_ref, ds_ref, lt_ref, sall_ref,
                      dyb_ref, dssd_ref, ddt_ref, dcw_ref, dcb_ref, ddtb_ref, dal_ref, dng_ref, dst_ref, dnext_ref,
                      ddse_ref)

        @pl.when((b == nb // ns - 1) & (c == nch - 1))
        def _():
            dds_ref[...] = _dot_01(jnp.broadcast_to(ddse_ref[...], (8, 1024)), ext)[0:1]

    def one_chunk(i, ex, ext, cw, ng, z_ref, xbc_ref, conv_ref, dt_ref, dtb_ref, al_ref, ds_ref, lt_ref, sall_ref,
                  dyb_ref, dssd_ref, ddt_ref, dcw_ref, dcb_ref, ddtb_ref, dal_ref, dng_ref, dst_ref, dnext_ref,
                  ddse_ref):
        z = z_ref[i]
        s_prev = sall_ref[i, 0]
        conv = conv_ref[i]
        f = _ssd_fwd_vals(z, conv, dt_ref[i], dtb_ref[...], al_ref[...], ds_ref[...], ng, ex, lt_ref[...], s_prev)
        xs, xdt, cs, dec, dt = f["xs"], f["xdt"], f["cs"], f["dec"], f["dt"]
        dyb = dyb_ref[i]
        dyg, dngs = [], []
        for g in range(2):
            sl = slice(g * 512, (g + 1) * 512)
            dngs.append(jnp.sum(dyb[:, sl] * f["yhat"][g], axis=0, keepdims=True))
            dyg.append(_rms_bwd(dyb[:, sl], f["yhat"][g], f["rr"][g], ng[:, sl]))
        dng_ref[...] += jnp.concatenate(dngs, axis=1)
        dyg = jnp.concatenate(dyg, axis=1)
        sig_z = f["sig_z"]
        silu_z = z * sig_z
        dy = dyg * silu_z
        dz = dyg * f["ypre"] * (sig_z + silu_z * (1.0 - sig_z))
        ddse_ref[...] += jnp.sum(dy * xs, axis=0, keepdims=True)
        dxs = dy * f["de"]
        dye = dy * f["ecse"]
        dyeb = dye.astype(MXU)
        dst = dst_ref[i]
        dstb = dst.astype(MXU)
        bmb, cmb, sb, xdec = f["bmb"], f["cmb"], f["sb"], f["xdec"]
        u = jnp.concatenate([_dot(bmb[g], dstb[:, g * 512:(g + 1) * 512]) for g in range(2)], axis=1)
        dxdt = [u[:, q * 128:(q + 1) * 128] * f["dece"][:, q * 128:(q + 1) * 128] for q in range(8)]
        per_head = _dot_01(jnp.concatenate(
            [dy * f["yo"], u * xdt, jnp.broadcast_to(jnp.sum(dst * s_prev, axis=0, keepdims=True), (8, 1024))],
            axis=0), ext)
        dcs = per_head[0:CH]
        t = per_head[CH:2 * CH] * dec
        dcd = per_head[2 * CH:2 * CH + 1]
        row = lax.broadcasted_iota(jnp.int32, (CH, 128), 0)
        lane = lax.broadcasted_iota(jnp.int32, (CH, 128), 1)
        cd = jnp.exp(f["last"])
        dcs = dcs - t + jnp.where(row == CH - 1, jnp.sum(t, axis=0, keepdims=True) + dcd * cd, 0.0)
        dcst = jnp.zeros((128, CH), F32)
        lo = f["lo"]
        dbm, dcm, ds_prev = [], [], []
        for g in range(2):
            sl = slice(g * 512, (g + 1) * 512)
            dmg = jnp.zeros((CH, CH), F32)
            for q in range(4 * g, 4 * g + 4):
                dyq = dy[:, q * 128:(q + 1) * 128]
                xq = xdt[:, q * 128:(q + 1) * 128].astype(MXU)
                for hh in range(2):
                    h = 2 * q + hh
                    m = lo if hh == 0 else ~lo
                    dym = jnp.where(m, dyq, 0.0).astype(MXU)
                    gh = _dot_nt(dym, xq)
                    gl = gh * f["lms"][h]
                    dmg = dmg + gl
                    qh = gl * f["mg"][g]
                    dcs = dcs + jnp.where(lane == h, jnp.sum(qh, axis=1, keepdims=True), 0.0)
                    dcst = dcst - jnp.where(row == h, jnp.sum(qh, axis=0, keepdims=True), 0.0)
                    dxdt[q] = dxdt[q] + _dot_tn(f["whs"][h], dym)
            dmgb = dmg.astype(MXU)
            dcm.append(_dot(dmgb, bmb[g]) + _dot_nt(dyeb[:, sl], sb[:, sl]))
            dbm.append(_dot_tn(dmgb, cmb[g]) + _dot_nt(xdec[:, sl], dstb[:, sl]))
            ds_prev.append(_dot_tn(cmb[g], dyeb[:, sl]))
        dst_ref[i] = jnp.concatenate(ds_prev, axis=1) + dst * f["cde"]
        dcs = dcs + dcst.T
        da = _dot_hi(lt_ref[...].T, dcs)
        dxdt = jnp.concatenate(dxdt, axis=1)
        a_neg = f["a_neg"]
        ddt = da * a_neg + _dot_01(dxdt * xs, ext)
        dal_ref[...] += jnp.sum(da * dt, axis=0, keepdims=True) * a_neg
        dxs = dxs + dxdt * f["dte"]
        ddtraw = jnp.where(lane < SSD_HEADS, ddt * _sigmoid(f["dtpre"]), 0.0)
        ddtb_ref[...] += jnp.sum(ddtraw, axis=0, keepdims=True)
        ddt_ref[i] = ddtraw.astype(MXU)
        dxa = jnp.concatenate([dxs, dbm[0], dbm[1], dcm[0], dcm[1]], axis=1)
        sig_c = f["sig_c"]
        dconv = dxa * (sig_c + f["xa"] * (1.0 - sig_c))
        dcb_ref[...] += jnp.sum(dconv, axis=0, keepdims=True)
        xbc = xbc_ref[i]
        dcw_ref[3:4, :] += jnp.sum(dconv * xbc, axis=0, keepdims=True)
        dxbc = cw[3:4] * dconv
        for j, up in zip((1, 2, 3), _shifts_up(dconv, dnext_ref[i])):
            dcw_ref[3 - j:4 - j, :] += jnp.sum(up * xbc, axis=0, keepdims=True)
            dxbc = dxbc + cw[3 - j:4 - j] * up
        dnext_ref[i] = dconv[0:8]
        dssd_ref[i, :, 0:1024] = dz.astype(MXU)
        dssd_ref[i, :, 1024:2560] = dxbc.astype(MXU)

    dssd, ddt, *small = pl.pallas_call(
        _after(14, body), grid=(nb // ns, nch), name="ssd_bwd",
        in_specs=in_specs,
        out_specs=[seq(2560), seq(128),
                   _const((8, CONV_CH)), _const((1, CONV_CH)), _const((1, 128)), _const((1, 128)), _const((1, 128)),
                   _const((1, 1024))],
        out_shape=[_sds((nb, S, 2560), MXU), _sds((nb, S, 128), MXU), _sds((8, CONV_CH), F32),
                   _sds((1, CONV_CH), F32), _sds((1, 128), F32), _sds((1, 128), F32), _sds((1, 128), F32),
                   _sds((1, 1024), F32)],
        scratch_shapes=[pltpu.VMEM((ns, 128, 1024), F32), pltpu.VMEM((ns, 8, CONV_CH), F32),
                        pltpu.VMEM((1, 1024), F32)],
        compiler_params=_cp(2),
    )(pz.reshape(nb, S, 1024), pxbc.reshape(nb, S, CONV_CH), conv.reshape(nb, S, CONV_CH), dtraw.reshape(nb, S, 128),
      convw, dtb, alog, dskip, ng, ex, ltri, ext, sall, dcat.reshape(nb, S, 2048), after)
    return (dssd.reshape(T, 2560), ddt.reshape(T, 128), *small)


def _inproj_bwd(dxn_uv, dssd, ddt, wm, wdt, dh1, x, g, tm, after):
    T = x.shape[0]

    def body(dxnuv_ref, dssd_ref, ddt_ref, wm_ref, wdt_ref, dh1_ref, x_ref, g_ref, dx_ref, dg_ref):
        @pl.when(pl.program_id(0) == 0)
        def _():
            dg_ref[...] = jnp.zeros_like(dg_ref)

        dxn = (dxnuv_ref[...] + _dot_nt(dssd_ref[...], wm_ref[:, 2048:N_MAIN])
               + _dot_nt(ddt_ref[...], wdt_ref[...]))
        xh, r = _rms(x_ref[...])
        dg_ref[...] += jnp.sum(dxn * xh, axis=0, keepdims=True)
        dx_ref[...] = dh1_ref[...] + _rms_bwd(dxn, xh, r, g_ref[...])

    return pl.pallas_call(
        _after(8, body), grid=(T // tm,), name="inproj_bwd",
        in_specs=[_rows(tm, D), _rows(tm, 2560), _rows(tm, 128), _const((D, N_MAIN)), _const((D, 128)),
                  _rows(tm, D), _rows(tm, D), _const((1, D)), _ANY],
        out_specs=[_rows(tm, D), _const((1, D))],
        out_shape=[_sds((T, D), F32), _sds((1, D), F32)],
        compiler_params=_cp(),
    )(dxn_uv, dssd, ddt, wm, wdt, dh1, x, g, after)


def _matmul_tn(a, b, name, a_fn=None):
    T, M = a.shape
    N = b.shape[1]
    tm = min(M, 1024)
    tn = 1280 if N == 2560 else min(N, 1024)
    tk = min(T, 2048)

    def body(a_ref, b_ref, o_ref, acc_ref):
        k = pl.program_id(2)

        @pl.when(k == 0)
        def _():
            acc_ref[...] = jnp.zeros_like(acc_ref)

        av = a_ref[...]
        if a_fn is not None:
            av = a_fn(av)
        acc_ref[...] += _dot_tn(av, b_ref[...])

        @pl.when(k == T // tk - 1)
        def _():
            o_ref[...] = acc_ref[...].astype(o_ref.dtype)

    return pl.pallas_call(
        body, grid=(M // tm, N // tn, T // tk), name=name,
        in_specs=[pl.BlockSpec((tk, tm), lambda i, j, k: (k, i)), pl.BlockSpec((tk, tn), lambda i, j, k: (k, j))],
        out_specs=pl.BlockSpec((tm, tn), lambda i, j, k: (i, j)),
        out_shape=_sds((M, N), GRAD),
        scratch_shapes=[pltpu.VMEM((tm, tn), F32)],
        compiler_params=_cp(3),
    )(a, b)


def _adamw_vals(w, g, m, v):
    m = B1 * m + (1.0 - B1) * g
    v = B2 * v + (1.0 - B2) * (g * g)
    m_hat = m / (1.0 - B1 ** STEP)
    v_hat = v / (1.0 - B2 ** STEP)
    return -LR * (m_hat / (jnp.sqrt(v_hat) + ADAM_EPS) + WD * w), m, v


_PARTS = 4


def _adamw_halves(items, kh, name):
    n = len(items)

    def body(kh_ref, *refs):
        mine = (pl.program_id(0) // _PARTS) == kh_ref[1]
        for k in range(n):
            w_ref, own_ref, oth_ref, m_ref, v_ref = refs[5 * k:5 * k + 5]
            g_ref, d_ref, mo_ref, vo_ref = refs[5 * n + 4 * k:5 * n + 4 * k + 4]
            g = jnp.where(mine, own_ref[...], oth_ref[...])
            g_ref[...] = g
            d_ref[...], mo_ref[...], vo_ref[...] = _adamw_vals(w_ref[...], g, m_ref[...], v_ref[...])

    def early(i):
        return jnp.minimum(i, _PARTS - 1)

    def late(i):
        return jnp.maximum(i - _PARTS, 0)

    in_specs, out_specs, out_shape = [], [], []
    for w, *_ in items:
        R, C = w.shape
        tr = R // (2 * _PARTS)
        full = pl.BlockSpec((tr, C), lambda i, kh: (i, 0))
        own = pl.BlockSpec((tr, C), lambda i, kh: (jnp.where(kh[1] == 0, early(i), late(i)), 0))
        oth = pl.BlockSpec((tr, C), lambda i, kh: (jnp.where(kh[1] == 0, late(i), early(i)), 0))
        in_specs += [full, own, oth, full, full]
        out_specs += [full] * 4
        out_shape += [_sds((R, C), F32)] * 4
    res = pl.pallas_call(
        body, name=name,
        grid_spec=pltpu.PrefetchScalarGridSpec(
            num_scalar_prefetch=1, grid=(2 * _PARTS,), in_specs=in_specs, out_specs=out_specs),
        out_shape=out_shape,
        compiler_params=_cp(),
    )(kh, *[a for item in items for a in item])
    return [tuple(res[4 * k:4 * k + 4]) for k in range(n)]


_TJ = 128


def _adamw_transposed(w, own, other, m, v, name):
    C, _, R = w.shape

    def body(w_ref, own_ref, oth_ref, m_ref, v_ref, g_ref, d_ref, mo_ref, vo_ref):
        first = lax.axis_index("c") == 0
        g = jnp.concatenate([jnp.where(first, own_ref[...], oth_ref[...]),
                             jnp.where(first, oth_ref[...], own_ref[...])], axis=0).T
        d, mo, vo = _adamw_vals(w_ref[:, 0, :], g, m_ref[:, 0, :], v_ref[:, 0, :])
        for ref, val in ((g_ref, g), (d_ref, d), (mo_ref, mo), (vo_ref, vo)):
            ref[:, 0, :] = val

    cols = pl.BlockSpec((_TJ, 1, R), lambda j: (j, 0, 0))
    half = pl.BlockSpec((R // 2, _TJ), lambda j: (0, j))
    return pl.pallas_call(
        body, grid=(pl.cdiv(C, _TJ),), name=name,
        in_specs=[cols, half, half, cols, cols], out_specs=[cols] * 4, out_shape=[_sds((C, 1, R), F32)] * 4,
        compiler_params=_cp(),
    )(w, own, other, m, v)


def _lanes(rows):
    return jnp.concatenate([rows[i:i + 1, :] for i in range(rows.shape[0])], axis=1)


def _small_update(a, packs):
    names = [n for n, _ in _SMALL]
    where = {}
    for k, (pnames, rows, _) in enumerate(packs):
        o = 0
        for n, r in zip(pnames, rows):
            where[n] = (k, o, r)
            o += r
    view = {n: (1, 1024) for n in names}
    view.update(gm_ws=(1024, 128), gm_bs=(8, 128), ssd_conv_w=(4, 384), ssd_conv_b=(1, CONV_CH),
                ssd_dt_bias=(1, 16), ssd_a_log=(1, 16), ssd_d=(1, 16))
    npk = len(packs)

    def body(*refs):
        tots = []
        for k in range(npk):
            tot = refs[k][0]
            for d in range(1, 8):
                tot = tot + refs[k][d]
            tots.append(tot)
        ins, outs = refs[npk:npk + 3 * len(names)], refs[npk + 3 * len(names):]
        chip = 2 * lax.axis_index("x") + lax.axis_index("y")
        for i, n in enumerate(names):
            k, o, r = where[n]
            blk = tots[k][o:o + r, :]
            if n == "gm_ws":
                g = blk
            elif n == "gm_bs":
                g = blk[0:8]
            elif view[n] == (1, 16):
                g = blk[0:1, 0:16]
            elif n == "ssd_conv_w":
                taps = jnp.concatenate([_lanes(blk[12 * t:12 * t + 12]) for t in range(4)], axis=0)
                g = taps[:, 0:384]
                for c in range(1, 4):
                    g = jnp.where(chip == c, taps[:, 384 * c:384 * (c + 1)], g)
            else:
                g = _lanes(blk[0:view[n][1] // 128])
            d, mo, vo = _adamw_vals(ins[3 * i][...], g, ins[3 * i + 1][...], ins[3 * i + 2][...])
            for j, val in enumerate((g, d, mo, vo)):
                outs[4 * i + j][...] = val
        k, o, _ = where["loss"]
        outs[-1][...] = tots[k][o:o + 1, 0:1]

    ins = [a[pre + n].reshape(view[n]) for n in names for pre in ("", "m_", "v_")]
    res = pl.pallas_call(
        body, name="small_update",
        out_shape=[_sds(view[n], F32) for n in names for _ in range(4)] + [_sds((1, 1), F32)],
    )(*[slots for _, _, slots in packs], *ins)
    return {n: tuple(r.reshape(a[n].shape) for r in res[4 * i:4 * i + 4]) for i, n in enumerate(names)}, res[-1]


def _sum_slots(items, kh, name):
    n = len(items)
    in_specs, out_specs, out_shape = [], [], []
    for slots, src, kind, (R, C) in items:
        tr = R // (2 * _PARTS)
        if kind == "slab":
            src_spec = pl.BlockSpec((1, tr, C), lambda i, kh: (kh[0], kh[1] * _PARTS + i, 0))
        elif kind == "rows":
            src_spec = pl.BlockSpec((tr, C), lambda i, kh: (kh[0] * (2 * _PARTS) + kh[1] * _PARTS + i, 0))
        else:
            src_spec = pl.BlockSpec((tr, C), lambda i, kh: (kh[1] * _PARTS + i, kh[0]))
        in_specs += [pl.BlockSpec((1, tr, C), lambda i, kh, j=j: ((2 * kh[0] + kh[1]) ^ j, i, 0)) for j in range(1, 8)]
        in_specs.append(src_spec)
        out_specs.append(pl.BlockSpec((tr, C), lambda i, kh: (i, 0)))
        out_shape.append(_sds((R // 2, C), F32))

    def body(kh_ref, *refs):
        for k, (_, _, kind, _) in enumerate(items):
            s_refs, own_ref, o_ref = refs[8 * k:8 * k + 7], refs[8 * k + 7], refs[8 * n + k]
            acc = (own_ref[0] if kind == "slab" else own_ref[...]).astype(F32)
            for s_ref in s_refs:
                acc = acc + s_ref[0].astype(F32)
            o_ref[...] = acc

    return pl.pallas_call(
        body, name=name,
        grid_spec=pltpu.PrefetchScalarGridSpec(
            num_scalar_prefetch=1, grid=(_PARTS,), in_specs=in_specs, out_specs=out_specs),
        out_shape=out_shape,
        compiler_params=_cp(),
    )(kh, *[a for slots, src, _, _ in items for a in [slots] * 7 + [src]])


def _assemble_w_in(slabs):
    tr = 256

    def body(s_ref, wm_ref, wdt_ref):
        full = jnp.concatenate([s_ref[k] for k in range(4)], axis=1)
        wm_ref[...] = full[:, :N_MAIN]
        wdt_ref[...] = jnp.concatenate([full[:, N_MAIN:], jnp.zeros((tr, 128 - 16), full.dtype)], axis=1)

    return pl.pallas_call(
        body, grid=(D // tr,), name="assemble_w_in",
        in_specs=[pl.BlockSpec((4, tr, 1156), lambda i: (0, i, 0))],
        out_specs=[_rows(tr, N_MAIN), _rows(tr, 128)],
        out_shape=[_sds((D, N_MAIN), slabs.dtype), _sds((D, 128), slabs.dtype)],
        compiler_params=_cp(),
    )(slabs)


def _split_dw_in(d_uv, d_ssd, d_dt):
    tr = 256

    def body(uv_ref, ssd_ref, dt_ref, o_ref):
        full = jnp.concatenate([uv_ref[...], ssd_ref[...], dt_ref[:, 0:16]], axis=1)
        for k in range(4):
            o_ref[k] = full[:, 1156 * k:1156 * (k + 1)]

    return pl.pallas_call(
        body, grid=(D // tr,), name="split_dw_in",
        in_specs=[_rows(tr, 2048), _rows(tr, 2560), _rows(tr, 128)],
        out_specs=pl.BlockSpec((4, tr, 1156), lambda i: (0, i, 0)),
        out_shape=_sds((4, D, 1156), d_uv.dtype),
        compiler_params=_cp(),
    )(d_uv, d_ssd, d_dt)


def _cast_w_in(w, kh):
    C, _, R = w.shape

    def body(kh_ref, w_ref, o_ref):
        o_ref[0] = w_ref[:, 0, :].T.astype(BF16)

    return pl.pallas_call(
        body, name="cast_w_in",
        grid_spec=pltpu.PrefetchScalarGridSpec(
            num_scalar_prefetch=1, grid=(pl.cdiv(C, _TJ),),
            in_specs=[pl.BlockSpec((_TJ, 1, R), lambda j, kh: (j, 0, 0))],
            out_specs=pl.BlockSpec((1, R, _TJ), lambda j, kh: (kh[0], 0, j))),
        out_shape=_sds((4, R, C), BF16),
        compiler_params=_cp(),
    )(kh, w)


def _cast_into_slot(ws, kh, name):
    n = len(ws)

    def body(kh_ref, *refs):
        for k in range(n):
            refs[n + k][0] = refs[k][...].astype(BF16)

    return pl.pallas_call(
        body, name=name,
        grid_spec=pltpu.PrefetchScalarGridSpec(
            num_scalar_prefetch=1, grid=(_PARTS,),
            in_specs=[pl.BlockSpec((w.shape[0] // _PARTS, w.shape[1]), lambda i, kh: (i, 0)) for w in ws],
            out_specs=[pl.BlockSpec((1, w.shape[0] // _PARTS, w.shape[1]), lambda i, kh: (kh[0], i, 0))
                       for w in ws]),
        out_shape=[_sds((4,) + w.shape, BF16) for w in ws],
        compiler_params=_cp(),
    )(kh, *ws)


_ANY = pl.BlockSpec(memory_space=pl.ANY)
_CHIP_FLIPS = [(1, 0), (0, 1), (1, 1)]
_DEVICE_FLIPS = [(fx, fy, fc) for fx in (0, 1) for fy in (0, 1) for fc in (0, 1)][1:]


def _half(h, rows):
    return pl.ds(pl.multiple_of(h * rows, rows), rows)


def _remote(src, dst, ssem, rsem, to):
    return pltpu.make_async_remote_copy(src_ref=src, dst_ref=dst, send_sem=ssem, recv_sem=rsem,
                                        device_id=to, device_id_type=MESH)


def _weight_gather(bufs, conv):
    n = len(bufs)

    def body(*refs):
        conv_ref, outs, conv_out = refs[n], refs[n + 1:2 * n + 1], refs[2 * n + 1]
        send_sems, recv_sems, fsend_sems, frecv_sems, csend_sems, crecv_sems, local_sem = refs[2 * n + 2:]
        x, y, c = lax.axis_index("x"), lax.axis_index("y"), lax.axis_index("c")
        me = 2 * x + y
        halves = [_half(c, r.shape[1] // 2) for r in outs]
        others = [_half(1 - c, r.shape[1] // 2) for r in outs]
        remote = _remote
        local = [pltpu.make_async_copy(conv_ref, conv_out.at[me], local_sem)]
        for cp in local:
            cp.start()
        sends = []
        for k, (fx, fy) in enumerate(_CHIP_FLIPS):
            peer = (x ^ fx, y ^ fy, c)
            for i in range(n):
                mine = outs[i].at[me, halves[i]]
                sends.append(remote(mine, mine, send_sems.at[k * n + i], recv_sems.at[k * n + i], peer))
            sends.append(remote(conv_ref, conv_out.at[me], csend_sems.at[k], crecv_sems.at[k], peer))
        for cp in sends:
            cp.start()
        sibling = (x, y, 1 - c)
        forwards = []
        for k, (fx, fy) in enumerate(_CHIP_FLIPS):
            peer = (x ^ fx, y ^ fy, c)
            src = 2 * (x ^ fx) + (y ^ fy)
            for i in range(n):
                landed = outs[i].at[src, halves[i]]
                remote(landed, landed, send_sems.at[k * n + i], recv_sems.at[k * n + i], peer).wait_recv()
                fw = remote(landed, landed, fsend_sems.at[k * n + i], frecv_sems.at[k * n + i], sibling)
                fw.start()
                forwards.append(fw)
            remote(conv_out.at[src], conv_out.at[src], csend_sems.at[k], crecv_sems.at[k], peer).wait_recv()
        for k, (fx, fy) in enumerate(_CHIP_FLIPS):
            src = 2 * (x ^ fx) + (y ^ fy)
            for i in range(n):
                theirs = outs[i].at[src, others[i]]
                remote(theirs, theirs, fsend_sems.at[k * n + i], frecv_sems.at[k * n + i], sibling).wait_recv()
        for cp in sends + forwards:
            cp.wait_send()
        for cp in local:
            cp.wait()

    dma = pltpu.SemaphoreType.DMA
    return pl.pallas_call(
        body, name="weight_gather",
        in_specs=[_ANY] * (n + 1), out_specs=[_ANY] * (n + 1),
        out_shape=[_sds(b.shape, b.dtype) for b in bufs] + [_sds((4,) + conv.shape, conv.dtype)],
        input_output_aliases={i: i for i in range(n)},
        scratch_shapes=[dma((3 * n,)), dma((3 * n,)), dma((3 * n,)), dma((3 * n,)), dma((3,)), dma((3,)), dma],
    )(*bufs, conv)


def _piece(ref, kind, R, C, k, h):
    if kind == "slab":
        return ref.at[k, _half(h, R // 2), :]
    if kind == "rows":
        return ref.at[pl.ds(pl.multiple_of(k * R + h * (R // 2), R // 2), R // 2), :]
    return ref.at[_half(h, R // 2), pl.ds(pl.multiple_of(k * C, C), C)]


_HBM = pl.BlockSpec(memory_space=pltpu.HBM)
_SEM = pl.BlockSpec(memory_space=pltpu.SEMAPHORE)


def _split_start(name, arrays, n_copies, plan, after=None):
    n = len(arrays)
    extra = [] if after is None else [after]

    def body(*refs):
        m = n + len(extra)
        arrs, send_sems, recv_sems, token = refs[:n], refs[m], refs[m + 1], refs[-1]
        for j, (src, dst, peer) in enumerate(plan(arrs)):
            _remote(src, dst, send_sems.at[j], recv_sems.at[j], peer).start()
        token[...] = jnp.zeros_like(token)

    dma = pltpu.SemaphoreType.DMA
    res = pl.pallas_call(
        body, name=name,
        out_shape=(dma((n_copies,)), dma((n_copies,)), *[pltpu.HBM(a.shape, a.dtype) for a in arrays],
                   _sds((8, 128), F32)),
        in_specs=[_HBM] * n + [_ANY] * len(extra),
        out_specs=(_SEM, _SEM, *[_HBM] * n, pl.BlockSpec(memory_space=pltpu.VMEM)),
        input_output_aliases={i: 2 + i for i in range(n)},
        compiler_params=pltpu.CompilerParams(has_side_effects=pltpu.SideEffectType.DATAFLOW_SIDE_EFFECTING),
    )(*[pltpu.with_memory_space_constraint(a, pltpu.HBM) for a in arrays], *extra)
    return res[0], res[1], list(res[2:2 + n]), res[-1]


def _split_wait(name, arrays, send_sems, recv_sems, plan, after, first=0):
    n = len(arrays)

    def body(*refs):
        arrs, ssems, rsems = refs[:n], refs[n], refs[n + 1]
        for j, (src, dst, peer) in enumerate(plan(arrs), first):
            cp = _remote(src, dst, ssems.at[j], rsems.at[j], peer)
            cp.wait_send()
            cp.wait_recv()

    return list(pl.pallas_call(
        body, name=name,
        out_shape=tuple(pltpu.HBM(a.shape, a.dtype) for a in arrays),
        in_specs=[_HBM] * n + [_SEM, _SEM, _ANY],
        out_specs=tuple([_HBM] * n),
        input_output_aliases={i: i for i in range(n)},
        compiler_params=pltpu.CompilerParams(has_side_effects=pltpu.SideEffectType.DATAFLOW_SIDE_EFFECTING),
    )(*arrays, send_sems, recv_sems, after))


def _gather_plan(n):
    def plan(bufs):
        x, y, c = lax.axis_index("x"), lax.axis_index("y"), lax.axis_index("c")
        me = 2 * x + y
        return [(bufs[i].at[me], bufs[i].at[me], (x ^ fx, y ^ fy, c)) for fx, fy in _CHIP_FLIPS for i in range(n)]

    return plan


def _reduce_plan(specs, n_small):
    n = len(specs)

    def plan(arrs):
        x, y, c = lax.axis_index("x"), lax.axis_index("y"), lax.axis_index("c")
        slot = 4 * x + 2 * y + c
        out = []
        for fx, fy, fc in _DEVICE_FLIPS:
            peer = (x ^ fx, y ^ fy, c ^ fc)
            for i, (kind, (R, C)) in enumerate(specs):
                out.append((_piece(arrs[i], kind, R, C, 2 * peer[0] + peer[1], peer[2]), arrs[n + i].at[slot], peer))
            for s in range(n_small):
                out.append((arrs[2 * n + 2 * s], arrs[2 * n + 2 * s + 1].at[slot], peer))
        return out

    return plan


def _sibling_plan(n):
    def plan(arrs):
        x, y, c = lax.axis_index("x"), lax.axis_index("y"), lax.axis_index("c")
        return [(arrs[i], arrs[n + i], (x, y, 1 - c)) for i in range(n)]

    return plan


def _sibling_exchange(halves, name, small=None):
    n = len(halves)
    ns = 0 if small is None else 1

    def body(*refs):
        ins, outs = refs[:n], refs[n + ns:2 * n + ns]
        send_sems, recv_sems = refs[2 * (n + ns)], refs[2 * (n + ns) + 1]
        x, y, c = lax.axis_index("x"), lax.axis_index("y"), lax.axis_index("c")
        copies = [_remote(ins[i], outs[i], send_sems.at[i], recv_sems.at[i], (x, y, 1 - c)) for i in range(n)]
        waits = list(copies)
        if ns:
            s_ref, slots_ref, ssend_sems, srecv_sems, local_sem = refs[n], refs[2 * n + 1], *refs[2 * (n + ns) + 2:]
            slot = 4 * x + 2 * y + c
            own = pltpu.make_async_copy(s_ref, slots_ref.at[slot], local_sem)
            own.start()
            for k, (fx, fy, fc) in enumerate(_DEVICE_FLIPS):
                peer = (x ^ fx, y ^ fy, c ^ fc)
                copies.append(_remote(s_ref, slots_ref.at[slot], ssend_sems.at[k], srecv_sems.at[k], peer))
                theirs = slots_ref.at[slot ^ (k + 1)]
                waits.append(_remote(theirs, theirs, ssend_sems.at[k], srecv_sems.at[k], peer))
        for cp in copies:
            cp.start()
        for cp in waits:
            cp.wait()
        if ns:
            own.wait()

    dma = pltpu.SemaphoreType.DMA
    extra_in = [] if small is None else [small]
    extra_out = [] if small is None else [_sds((8,) + small.shape, F32)]
    return pl.pallas_call(
        body, name=name,
        in_specs=[_ANY] * (n + ns), out_specs=[_ANY] * (n + ns),
        out_shape=[_sds(h.shape, h.dtype) for h in halves] + extra_out,
        scratch_shapes=[dma((n,)), dma((n,))] + ([dma((7,)), dma((7,)), dma] if ns else []),
    )(*halves, *extra_in)


_BIG = [("w_in", (1024, 1156), "slab"), ("w_out", (512, 1024), "rows"), ("w_ff1", (1024, 1024), "cols"),
        ("w_ff2", (1024, 1024), "rows"), ("w_ple_gate", (256, 1024), "rows"), ("w_ple_proj", (256, 256), "cols")]
_SMALL = [("norm_mix_g", (1, 1024)), ("gm_v_norm_g", (1, 1024)), ("gm_ws", (1, 8, 128, 128)), ("gm_bs", (1, 8, 128)),
          ("gm_out_norm_g", (1, 1024)), ("ssd_conv_w", (1, 4, 1536)), ("ssd_conv_b", (1, 1536)),
          ("ssd_dt_bias", (1, 16)), ("ssd_a_log", (1, 16)), ("ssd_d", (1, 16)), ("ssd_norm_g", (1, 1024)),
          ("norm_mlp_g", (1, 1024)), ("ple_norm_g", (1, 1024)), ("final_norm_g", (1024,))]


def _rows128(a):
    flat = a.reshape(-1)
    rows = -(-flat.shape[0] // 1024) * 8
    return jnp.pad(flat, (0, rows * 128 - flat.shape[0])).reshape(rows, 128)


def _pad_lanes(v, n=128):
    v = v.reshape(1, -1)
    return jnp.pad(v, ((0, 0), (0, n - v.shape[1])))


_SMALL_SHAPES = dict(_SMALL + [("loss", ())])
_BIG_SPECS = {n: (kind, shp) for n, shp, kind in _BIG}


class _Comm:
    def __init__(self, a, kh):
        self.a, self.kh = a, kh
        rest = _BIG[1:]
        self.bufs = {"w_in": _cast_w_in(a["w_in"].transpose(2, 0, 1), kh)}
        cast = _cast_into_slot([a[n].reshape(shp) for n, shp, _ in rest], kh, "cast_rest")
        self.bufs.update({n: c for (n, _, _), c in zip(rest, cast)})
        self.sent = []
        self.small_packs = []

    def w_in(self):
        g_win, g_cw = _weight_gather([self.bufs["w_in"]], self.a["ssd_conv_w"].reshape(4, 384))
        first, later = ["w_out", "w_ff1"], ["w_ff2", "w_ple_gate", "w_ple_proj"]
        nf = len(first)
        plans = _gather_plan(nf), _gather_plan(len(later))

        def plan(bufs):
            return plans[0](bufs[:nf]) + plans[1](bufs[nf:])

        ssem, rsem, thru, token = _split_start("gather_start", [self.bufs[n] for n in first + later],
                                               3 * len(first + later), plan, after=g_cw)
        self.gather = {"out": (plans[0], ssem, rsem, thru[:nf], 0), "ff": (plans[1], ssem, rsem, thru[nf:], 3 * nf)}
        wm, wdt = _assemble_w_in(g_win)
        return wm, wdt, jnp.concatenate([g_cw[k] for k in range(4)], axis=1), token

    def rest(self, tag, after):
        plan, ssem, rsem, thru, first = self.gather[tag]
        got = _split_wait("gather_wait_" + tag, thru, ssem, rsem, plan, after, first)
        if tag == "out":
            return got[0].reshape(2048, D), got[1]
        g_w2, g_wg, g_wp = got
        return g_w2.reshape(DFF, D), g_wg.reshape(D, D), g_wp

    def send(self, tag, grads):
        big = [n for n, _, _ in _BIG if n in grads]
        small = [n for n in _SMALL_SHAPES if n in grads]
        parts = [_rows128(grads[n]) for n in small]
        rows = [s.shape[0] for s in parts]
        if not big:
            self.last_small = (tag, small, rows, jnp.concatenate(parts, axis=0))
            return None
        srcs = [grads[n] for n in big]
        lands = [lax.empty((8, _BIG_SPECS[n][1][0] // 2, _BIG_SPECS[n][1][1]), GRAD) for n in big]
        extra = []
        if small:
            pack = jnp.concatenate(parts, axis=0)
            extra = [pack, jnp.broadcast_to(pack, (8,) + pack.shape)]
        red_plan = _reduce_plan([_BIG_SPECS[n] for n in big], len(extra) // 2)
        n_copies = 7 * (len(big) + len(extra) // 2)
        halves = []
        if self.sent:
            self.early = self._landed(self.sent[0], srcs[0])
            halves = list(self.early.values())
        arrays = srcs + lands + extra
        nr, nh = len(arrays), len(halves)
        sib_plan = _sibling_plan(nh)

        def plan(arrs):
            return sib_plan(arrs[nr:]) + red_plan(arrs[:nr])

        arrays += halves + [lax.empty(h.shape, F32) for h in halves]
        ssem, rsem, thru, token = _split_start("reduce_start_" + tag, arrays, n_copies + nh, plan)
        self.sent.append((tag, big, small, rows, red_plan, ssem, rsem, thru[:nr], nh))
        self.swap = (sib_plan, ssem, rsem, thru[nr:])
        return token

    def _landed(self, sent, after):
        tag, big, small, rows, plan, ssem, rsem, thru, first = sent
        arrs = _split_wait("reduce_wait_" + tag, thru, ssem, rsem, plan, after, first)
        nb_ = len(big)
        sums = _sum_slots([(arrs[nb_ + i], arrs[i]) + _BIG_SPECS[n] for i, n in enumerate(big)], self.kh, "sum_" + tag)
        if small:
            self.small_packs.append((small, rows, arrs[2 * nb_ + 1]))
        return dict(zip(big, sums))

    def finish(self, after):
        a, results = self.a, {}
        sib_plan, ssem, rsem, swapped = self.swap
        names = list(self.early)
        swapped = _split_wait("sibling_wait_early", swapped, ssem, rsem, sib_plan, after)
        mine, other = swapped[:len(names)], swapped[len(names):]
        items = [(a[n].reshape(_BIG_SPECS[n][1]), own, oth, a["m_" + n].reshape(_BIG_SPECS[n][1]),
                  a["v_" + n].reshape(_BIG_SPECS[n][1])) for n, own, oth in zip(names, mine, other)]
        results.update(zip(names, _adamw_halves(items, self.kh, "adamw_early")))
        own = self._landed(self.sent[-1], results[names[-1]][1])
        stag, small, rows, pack = self.last_small
        other, slots = _sibling_exchange([own["w_in"]], "sibling_exchange_late", pack)
        self.small_packs.append((small, rows, slots))
        w, m, v = (a[k].transpose(2, 0, 1) for k in ("w_in", "m_w_in", "v_w_in"))
        raw = _adamw_transposed(w, own["w_in"], other, m, v, "adamw_late")
        results["w_in"] = tuple(r.transpose(1, 2, 0) for r in raw)
        return results, self.small_packs


def _local_step(x, p, tgt, sm, comm, nb, tm):
    T = x.shape[0]
    wm, wdt, conv_w, token = comm.w_in()
    g_mix, gv, gout = sm["norm_mix_g"].reshape(1, D), sm["gm_v_norm_g"].reshape(1, D), sm["gm_out_norm_g"].reshape(1, D)
    ws = sm["gm_ws"].reshape(GM_HEADS, CH, CH)
    bst = jnp.pad(sm["gm_bs"].reshape(GM_HEADS, CH).T, ((0, 0), (0, 128 - GM_HEADS)))
    convw = jnp.pad(conv_w, ((0, 4), (0, 0)))
    convb = sm["ssd_conv_b"].reshape(1, CONV_CH)
    dtb, alog = _pad_lanes(sm["ssd_dt_bias"]), _pad_lanes(sm["ssd_a_log"])
    dskip = jnp.repeat(sm["ssd_d"].reshape(SSD_HEADS), SSD_P).reshape(1, 1024)
    ng, g_mlp, g_ple = sm["ssd_norm_g"].reshape(1, D), sm["norm_mlp_g"].reshape(1, D), sm["ple_norm_g"].reshape(1, D)
    gf = sm["final_norm_g"].reshape(1, D)
    head_of_lane = lax.broadcasted_iota(jnp.int32, (128, 1024), 1) // SSD_P
    ex = (lax.broadcasted_iota(jnp.int32, (128, 1024), 0) == head_of_lane).astype(BF16)
    ext = ex.T
    ltri = (lax.broadcasted_iota(jnp.int32, (CH, CH), 0) >= lax.broadcasted_iota(jnp.int32, (CH, CH), 1)).astype(F32)

    pz, pxbc, dtraw, xn, cat, uv = _inproj_gmlp(x, g_mix, wm, wdt, gv, ws, bst, gout, tm, token)
    cat, sall, conv = _ssd_fwd(pz, pxbc, dtraw, cat, convw, convb, dtb, alog, dskip, ng, ex, ltri, nb)
    wo, w1 = comm.rest("out", cat)
    h1, hn, hid = _outproj_ff1(cat, wo, x, g_mlp, w1, tm)
    w2, wg, wp = comm.rest("ff", hn)
    hp, dgl, dpe, dh2, dh2b, loss, d_gf, d_gple = _ff2_tail(hid, w2, h1, g_ple, p, tgt, wg, wp, gf, tm)

    d_wp = _matmul_tn(p, dpe, "dw_ple_proj", a_fn=lambda a: a.astype(MXU))
    d_wg = _matmul_tn(hp, dgl, "dw_ple_gate")
    d_w2 = _matmul_tn(hid, dh2b, "dw_ff2", a_fn=_sq)
    dpre = _ff2_bwd(dh2b, w2, hid, min(T, 2 * tm))
    d_w1 = _matmul_tn(hn, dpre, "dw_ff1")
    dh1, dh1b, d_gmlp, dcat = _ff1_bwd(dpre, w1, dh2, h1, g_mlp, wo, tm)
    d_wo = _matmul_tn(cat, dh1b, "dw_out")
    duv, d_gv, d_ws, d_bst, d_gout, dxn_uv = _gmlp_bwd(uv, dcat, gv, ws, bst, gout, wm)
    token = comm.send("early", {
        "w_ple_proj": d_wp, "w_ple_gate": d_wg, "w_ff2": d_w2, "w_ff1": d_w1, "w_out": d_wo, "loss": loss[0:1, 0:1], "final_norm_g": d_gf, "ple_norm_g": d_gple, "norm_mlp_g": d_gmlp,
        "gm_v_norm_g": d_gv, "gm_ws": d_ws, "gm_bs": d_bst[:, :GM_HEADS].T, "gm_out_norm_g": d_gout})
    dssd, ddt, d_cw, d_cb, d_dtb, d_al, d_ds, d_ng = _ssd_bwd(
        pz, pxbc, conv, dtraw, sall, dcat, convw, dtb, alog, dskip, ng, ex, ltri, ext, nb, token)
    d_win = _split_dw_in(_matmul_tn(xn, duv, "dw_in_uv"), _matmul_tn(xn, dssd, "dw_in_ssd"),
                         _matmul_tn(xn, ddt, "dw_in_dt"))
    token = comm.send("late", {"w_in": d_win})
    dx, d_gmix = _inproj_bwd(dxn_uv, dssd, ddt, wm, wdt, dh1, x, g_mix, tm, token)
    comm.send("d", {"norm_mix_g": d_gmix, "ssd_conv_w": d_cw[0:4], "ssd_conv_b": d_cb, "ssd_dt_bias": d_dtb[:, :16],
                    "ssd_a_log": d_al[:, :16], "ssd_d": d_ds[:, :16], "ssd_norm_g": d_ng})
    return dx


def kernel(x, p, norm_mix_g, w_in, gm_v_norm_g, gm_ws, gm_bs, gm_out_norm_g, ssd_conv_w, ssd_conv_b, ssd_dt_bias, ssd_a_log, ssd_d, ssd_norm_g, w_out, norm_mlp_g, w_ff1, w_ff2, ple_norm_g, w_ple_gate, w_ple_proj, final_norm_g, loss_target, m_norm_mix_g, m_w_in, m_gm_v_norm_g, m_gm_ws, m_gm_bs, m_gm_out_norm_g, m_ssd_conv_w, m_ssd_conv_b, m_ssd_dt_bias, m_ssd_a_log, m_ssd_d, m_ssd_norm_g, m_w_out, m_norm_mlp_g, m_w_ff1, m_w_ff2, m_ple_norm_g, m_w_ple_gate, m_w_ple_proj, m_final_norm_g, v_norm_mix_g, v_w_in, v_gm_v_norm_g, v_gm_ws, v_gm_bs, v_gm_out_norm_g, v_ssd_conv_w, v_ssd_conv_b, v_ssd_dt_bias, v_ssd_a_log, v_ssd_d, v_ssd_norm_g, v_w_out, v_norm_mlp_g, v_w_ff1, v_w_ff2, v_ple_norm_g, v_w_ple_gate, v_w_ple_proj, v_final_norm_g):
    a = dict(locals())
    order = ["norm_mix_g", "w_in", "gm_v_norm_g", "gm_ws", "gm_bs", "gm_out_norm_g", "ssd_conv_w", "ssd_conv_b",
             "ssd_dt_bias", "ssd_a_log", "ssd_d", "ssd_norm_g", "w_out", "norm_mlp_g", "w_ff1", "w_ff2", "ple_norm_g",
             "w_ple_gate", "w_ple_proj", "final_norm_g"]
    chip = 2 * lax.axis_index("x") + lax.axis_index("y")
    nb, S = x.shape[0], x.shape[1]
    T = nb * S
    sm = {n: a[n] for n, _ in _SMALL if n != "ssd_conv_w"}
    comm = _Comm(a, jnp.stack([chip, lax.axis_index("c")]).astype(jnp.int32))
    dx = _local_step(x.reshape(T, D), p.reshape(T, DPLE), loss_target.reshape(T, D), sm, comm, nb, 512)
    big, small_packs = comm.finish(dx)
    small, loss = _small_update(a, small_packs)
    g_out, delta, new_m, new_v = {}, {}, {}, {}
    for n in order:
        g_out[n], delta[n], new_m[n], new_v[n] = (r.reshape(a[n].shape) for r in (big[n] if n in big else small[n]))
    return (loss.reshape(()), dx.reshape(x.shape), *[g_out[n] for n in order], *[delta[n] for n in order],
            *[new_m[n] for n in order], *[new_v[n] for n in order])
```
